```python
import jax, jax.numpy as jnp
from jax import lax
import numpy as np

D_MODEL = 1024
BATCH = 8
SEQ = 8192
DEPTH = 2

EPS = 1e-6
ROPE_THETA = 10000.0
BLOCK = 128
MLA_HEADS = 4
MLA_Q_RANK = 256
MLA_KV_RANK = 128
MLA_NOPE = 128
MLA_ROPE = 64
MLA_V = 128
MLA_QK = MLA_NOPE + MLA_ROPE
MLA_WIDTH = MLA_HEADS * MLA_V
SWA_HEADS = 8
SWA_KV_HEADS = 2
SWA_HEAD_DIM = 64
WINDOW = 128
SWA_WIDTH = SWA_HEADS * SWA_HEAD_DIM
MIX_WIDTH = MLA_WIDTH + SWA_WIDTH
IN_SPLITS = (MLA_Q_RANK, MLA_KV_RANK, MLA_ROPE,
             SWA_HEADS * SWA_HEAD_DIM, SWA_KV_HEADS * SWA_HEAD_DIM, SWA_KV_HEADS * SWA_HEAD_DIM)
IN_COLS = sum(IN_SPLITS)
D_FF = 2816

kernel_name = "hybrid_mla_swa_sink_macaron"


def rmsnorm(t, g):
    tf = t.astype(jnp.float32)
    out = tf * lax.rsqrt(jnp.mean(tf * tf, axis=-1, keepdims=True) + EPS)
    return (out * g.astype(jnp.float32)).astype(t.dtype)


def swiglu(t, w_gate, w_up, w_down):
    return (jax.nn.silu(t @ w_gate) * (t @ w_up)) @ w_down


def rope_table(seq, dim):
    pos = jnp.arange(seq, dtype=jnp.float32)
    inv = 1.0 / (ROPE_THETA ** (jnp.arange(0, dim, 2, dtype=jnp.float32) / dim))
    ang = pos[:, None] * inv[None, :]
    return jnp.cos(ang), jnp.sin(ang)


def apply_rope(t, cos, sin):
    half = t.shape[-1] // 2
    t1 = t[..., :half].astype(jnp.float32)
    t2 = t[..., half:].astype(jnp.float32)
    c = cos[None, :, None, :]
    s = sin[None, :, None, :]
    return jnp.concatenate([t1 * c - t2 * s, t2 * c + t1 * s], axis=-1).astype(t.dtype)


def dense_causal_attention(q, k, v, scale):
    B, S, H, Dq = q.shape
    nb = S // BLOCK
    qb = q.reshape(B, nb, BLOCK, H, Dq).transpose(1, 0, 2, 3, 4)
    k_pos = jnp.arange(S)

    def one_block(args):
        q_blk, i = args
        s = jnp.einsum('bqhd,bkhd->bhqk', q_blk, k, preferred_element_type=jnp.float32) * scale
        q_pos = i * BLOCK + jnp.arange(BLOCK)
        mask = k_pos[None, :] <= q_pos[:, None]
        s = jnp.where(mask[None, None], s, -jnp.inf)
        p = jax.nn.softmax(s, axis=-1)
        return jnp.einsum('bhqk,bkhd->bqhd', p.astype(v.dtype), v)

    out = lax.map(one_block, (qb, jnp.arange(nb)))
    return out.transpose(1, 0, 2, 3, 4).reshape(B, S, H, v.shape[-1])


def sliding_window_sink_attention(q, k, v, sinks, scale):
    B, S, H, D = q.shape
    KV = k.shape[2]
    G = H // KV
    nb = S // BLOCK
    qb = q.reshape(B, nb, BLOCK, KV, G, D)

    def band(t):
        tb = t.reshape(B, nb, BLOCK, KV, D)
        prev = jnp.pad(tb[:, :-1], ((0, 0), (1, 0), (0, 0), (0, 0), (0, 0)))
        return jnp.concatenate([prev, tb], axis=2)

    kb, vb = band(k), band(v)
    s = jnp.einsum('bnqcgd,bnkcd->bncgqk', qb, kb, preferred_element_type=jnp.float32) * scale
    q_rel = jnp.arange(BLOCK)[:, None] + BLOCK
    k_rel = jnp.arange(2 * BLOCK)[None, :]
    dist = q_rel - k_rel
    in_window = (dist >= 0) & (dist < WINDOW)
    k_abs = jnp.arange(nb)[:, None, None] * BLOCK + k_rel[None] - BLOCK
    valid = in_window[None] & (k_abs >= 0)
    s = jnp.where(valid[None, :, None, None], s, -jnp.inf)
    sink = sinks.astype(jnp.float32).reshape(KV, G)[None, None, :, :, None, None]
    m = jnp.maximum(jnp.max(s, axis=-1, keepdims=True), sink)
    e = jnp.exp(s - m)
    p = e / (jnp.sum(e, axis=-1, keepdims=True) + jnp.exp(sink - m))
    out = jnp.einsum('bncgqk,bnkcd->bnqcgd', p.astype(v.dtype), vb)
    return out.reshape(B, S, H, D)


def _fwd_setup_inputs(seed: int = 0) -> dict:
    key = jax.random.key(seed)
    ks = iter(jax.random.split(key, 32))

    def w(shape, fan_in):
        return jax.random.normal(next(ks), shape, jnp.float32) * (fan_in ** -0.5)

    def gain(shape):
        return 1.0 + 0.02 * jax.random.normal(next(ks), shape, jnp.float32)

    L = DEPTH
    return {
        "x": jax.random.normal(next(ks), (BATCH, SEQ, D_MODEL), jnp.float32),
        "ffn1_norm": gain((L, D_MODEL)),
        "ffn1_w_gate": w((L, D_MODEL, D_FF), D_MODEL),
        "ffn1_w_up": w((L, D_MODEL, D_FF), D_MODEL),
        "ffn1_w_down": w((L, D_FF, D_MODEL), D_FF),
        "mix_norm": gain((L, D_MODEL)),
        "w_in": w((L, D_MODEL, IN_COLS), D_MODEL),
        "mla_q_a_norm": gain((L, MLA_Q_RANK)),
        "mla_w_q_b": w((L, MLA_Q_RANK, MLA_HEADS * MLA_QK), MLA_Q_RANK),
        "mla_kv_a_norm": gain((L, MLA_KV_RANK)),
        "mla_w_kv_b": w((L, MLA_KV_RANK, MLA_HEADS * (MLA_NOPE + MLA_V)), MLA_KV_RANK),
        "mla_q_norm": gain((L, MLA_QK)),
        "mla_k_norm": gain((L, MLA_QK)),
        "swa_q_norm": gain((L, SWA_HEAD_DIM)),
        "swa_k_norm": gain((L, SWA_HEAD_DIM)),
        "swa_sinks": 0.5 * jax.random.normal(next(ks), (L, SWA_HEADS), jnp.float32),
        "mla_out_norm": gain((L, MLA_WIDTH)),
        "swa_out_norm": gain((L, SWA_WIDTH)),
        "w_o": w((L, MIX_WIDTH, D_MODEL), MIX_WIDTH),
        "ffn2_norm": gain((L, D_MODEL)),
        "ffn2_w_gate": w((L, D_MODEL, D_FF), D_MODEL),
        "ffn2_w_up": w((L, D_MODEL, D_FF), D_MODEL),
        "ffn2_w_down": w((L, D_FF, D_MODEL), D_FF),
    }


def _fwd_reference(x, ffn1_norm, ffn1_w_gate, ffn1_w_up, ffn1_w_down, mix_norm, w_in,
              mla_q_a_norm, mla_w_q_b, mla_kv_a_norm, mla_w_kv_b, mla_q_norm, mla_k_norm,
              swa_q_norm, swa_k_norm, swa_sinks, mla_out_norm, swa_out_norm, w_o,
              ffn2_norm, ffn2_w_gate, ffn2_w_up, ffn2_w_down):
    B, S, _ = x.shape
    cos, sin = rope_table(S, MLA_ROPE)
    split_idx = np.cumsum(IN_SPLITS)[:-1].tolist()
    mla_scale = MLA_QK ** -0.5
    swa_scale = SWA_HEAD_DIM ** -0.5

    for l in range(DEPTH):
        x = x + 0.5 * swiglu(rmsnorm(x, ffn1_norm[l]), ffn1_w_gate[l], ffn1_w_up[l], ffn1_w_down[l])

        h = rmsnorm(x, mix_norm[l])
        c_q, c_kv, k_pe, q_s, k_s, v_s = jnp.split(h @ w_in[l], split_idx, axis=-1)

        q_a = (rmsnorm(c_q, mla_q_a_norm[l]) @ mla_w_q_b[l]).reshape(B, S, MLA_HEADS, MLA_QK)
        kv_a = (rmsnorm(c_kv, mla_kv_a_norm[l]) @ mla_w_kv_b[l]).reshape(B, S, MLA_HEADS, MLA_NOPE + MLA_V)
        k_nope, v_a = kv_a[..., :MLA_NOPE], kv_a[..., MLA_NOPE:]
        k_a = jnp.concatenate(
            [k_nope, jnp.broadcast_to(k_pe[:, :, None, :], (B, S, MLA_HEADS, MLA_ROPE))], axis=-1)
        q_a = rmsnorm(q_a, mla_q_norm[l])
        k_a = rmsnorm(k_a, mla_k_norm[l])
        q_a = jnp.concatenate([q_a[..., :MLA_NOPE], apply_rope(q_a[..., MLA_NOPE:], cos, sin)], axis=-1)
        k_a = jnp.concatenate([k_a[..., :MLA_NOPE], apply_rope(k_a[..., MLA_NOPE:], cos, sin)], axis=-1)
        out_a = dense_causal_attention(q_a, k_a, v_a, mla_scale).reshape(B, S, MLA_WIDTH)

        q_b = rmsnorm(q_s.reshape(B, S, SWA_HEADS, SWA_HEAD_DIM), swa_q_norm[l])
        k_b = rmsnorm(k_s.reshape(B, S, SWA_KV_HEADS, SWA_HEAD_DIM), swa_k_norm[l])
        v_b = v_s.reshape(B, S, SWA_KV_HEADS, SWA_HEAD_DIM)
        q_b = apply_rope(q_b, cos, sin)
        k_b = apply_rope(k_b, cos, sin)
        out_b = sliding_window_sink_attention(q_b, k_b, v_b, swa_sinks[l], swa_scale).reshape(B, S, SWA_WIDTH)

        mixed = jnp.concatenate([rmsnorm(out_a, mla_out_norm[l]), rmsnorm(out_b, swa_out_norm[l])], axis=-1)
        x = x + mixed @ w_o[l]

        x = x + 0.5 * swiglu(rmsnorm(x, ffn2_norm[l]), ffn2_w_gate[l], ffn2_w_up[l], ffn2_w_down[l])
    return x


import jax as _jax
import jax.numpy as _jnp

TWIN_FORMAT = 'train_step'
FWD_PARAMS = ['x', 'ffn1_norm', 'ffn1_w_gate', 'ffn1_w_up', 'ffn1_w_down', 'mix_norm', 'w_in', 'mla_q_a_norm', 'mla_w_q_b', 'mla_kv_a_norm', 'mla_w_kv_b', 'mla_q_norm', 'mla_k_norm', 'swa_q_norm', 'swa_k_norm', 'swa_sinks', 'mla_out_norm', 'swa_out_norm', 'w_o', 'ffn2_norm', 'ffn2_w_gate', 'ffn2_w_up', 'ffn2_w_down']
TWIN_WEIGHTS = ['ffn1_norm', 'ffn1_w_gate', 'ffn1_w_up', 'ffn1_w_down', 'mix_norm', 'w_in', 'mla_q_a_norm', 'mla_w_q_b', 'mla_kv_a_norm', 'mla_w_kv_b', 'mla_q_norm', 'mla_k_norm', 'swa_q_norm', 'swa_k_norm', 'swa_sinks', 'mla_out_norm', 'swa_out_norm', 'w_o', 'ffn2_norm', 'ffn2_w_gate', 'ffn2_w_up', 'ffn2_w_down']
TWIN_DIFF_INPUT = 'x'
TWIN_INPUTS = ['x', 'ffn1_norm', 'ffn1_w_gate', 'ffn1_w_up', 'ffn1_w_down', 'mix_norm', 'w_in', 'mla_q_a_norm', 'mla_w_q_b', 'mla_kv_a_norm', 'mla_w_kv_b', 'mla_q_norm', 'mla_k_norm', 'swa_q_norm', 'swa_k_norm', 'swa_sinks', 'mla_out_norm', 'swa_out_norm', 'w_o', 'ffn2_norm', 'ffn2_w_gate', 'ffn2_w_up', 'ffn2_w_down', 'loss_target', 'm_ffn1_norm', 'm_ffn1_w_gate', 'm_ffn1_w_up', 'm_ffn1_w_down', 'm_mix_norm', 'm_w_in', 'm_mla_q_a_norm', 'm_mla_w_q_b', 'm_mla_kv_a_norm', 'm_mla_w_kv_b', 'm_mla_q_norm', 'm_mla_k_norm', 'm_swa_q_norm', 'm_swa_k_norm', 'm_swa_sinks', 'm_mla_out_norm', 'm_swa_out_norm', 'm_w_o', 'm_ffn2_norm', 'm_ffn2_w_gate', 'm_ffn2_w_up', 'm_ffn2_w_down', 'v_ffn1_norm', 'v_ffn1_w_gate', 'v_ffn1_w_up', 'v_ffn1_w_down', 'v_mix_norm', 'v_w_in', 'v_mla_q_a_norm', 'v_mla_w_q_b', 'v_mla_kv_a_norm', 'v_mla_w_kv_b', 'v_mla_q_norm', 'v_mla_k_norm', 'v_swa_q_norm', 'v_swa_k_norm', 'v_swa_sinks', 'v_mla_out_norm', 'v_swa_out_norm', 'v_w_o', 'v_ffn2_norm', 'v_ffn2_w_gate', 'v_ffn2_w_up', 'v_ffn2_w_down']
TWIN_OUTPUTS = ['loss', 'grad_x', 'grad_ffn1_norm', 'grad_ffn1_w_gate', 'grad_ffn1_w_up', 'grad_ffn1_w_down', 'grad_mix_norm', 'grad_w_in', 'grad_mla_q_a_norm', 'grad_mla_w_q_b', 'grad_mla_kv_a_norm', 'grad_mla_w_kv_b', 'grad_mla_q_norm', 'grad_mla_k_norm', 'grad_swa_q_norm', 'grad_swa_k_norm', 'grad_swa_sinks', 'grad_mla_out_norm', 'grad_swa_out_norm', 'grad_w_o', 'grad_ffn2_norm', 'grad_ffn2_w_gate', 'grad_ffn2_w_up', 'grad_ffn2_w_down', 'delta_ffn1_norm', 'delta_ffn1_w_gate', 'delta_ffn1_w_up', 'delta_ffn1_w_down', 'delta_mix_norm', 'delta_w_in', 'delta_mla_q_a_norm', 'delta_mla_w_q_b', 'delta_mla_kv_a_norm', 'delta_mla_w_kv_b', 'delta_mla_q_norm', 'delta_mla_k_norm', 'delta_swa_q_norm', 'delta_swa_k_norm', 'delta_swa_sinks', 'delta_mla_out_norm', 'delta_swa_out_norm', 'delta_w_o', 'delta_ffn2_norm', 'delta_ffn2_w_gate', 'delta_ffn2_w_up', 'delta_ffn2_w_down', 'new_m_ffn1_norm', 'new_m_ffn1_w_gate', 'new_m_ffn1_w_up', 'new_m_ffn1_w_down', 'new_m_mix_norm', 'new_m_w_in', 'new_m_mla_q_a_norm', 'new_m_mla_w_q_b', 'new_m_mla_kv_a_norm', 'new_m_mla_w_kv_b', 'new_m_mla_q_norm', 'new_m_mla_k_norm', 'new_m_swa_q_norm', 'new_m_swa_k_norm', 'new_m_swa_sinks', 'new_m_mla_out_norm', 'new_m_swa_out_norm', 'new_m_w_o', 'new_m_ffn2_norm', 'new_m_ffn2_w_gate', 'new_m_ffn2_w_up', 'new_m_ffn2_w_down', 'new_v_ffn1_norm', 'new_v_ffn1_w_gate', 'new_v_ffn1_w_up', 'new_v_ffn1_w_down', 'new_v_mix_norm', 'new_v_w_in', 'new_v_mla_q_a_norm', 'new_v_mla_w_q_b', 'new_v_mla_kv_a_norm', 'new_v_mla_w_kv_b', 'new_v_mla_q_norm', 'new_v_mla_k_norm', 'new_v_swa_q_norm', 'new_v_swa_k_norm', 'new_v_swa_sinks', 'new_v_mla_out_norm', 'new_v_swa_out_norm', 'new_v_w_o', 'new_v_ffn2_norm', 'new_v_ffn2_w_gate', 'new_v_ffn2_w_up', 'new_v_ffn2_w_down']
TWIN_LEAF_KINDS = {'loss': 'loss', 'grad_x': 'grad_x', 'grad_ffn1_norm': 'grad_w', 'grad_ffn1_w_gate': 'grad_w', 'grad_ffn1_w_up': 'grad_w', 'grad_ffn1_w_down': 'grad_w', 'grad_mix_norm': 'grad_w', 'grad_w_in': 'grad_w', 'grad_mla_q_a_norm': 'grad_w', 'grad_mla_w_q_b': 'grad_w', 'grad_mla_kv_a_norm': 'grad_w', 'grad_mla_w_kv_b': 'grad_w', 'grad_mla_q_norm': 'grad_w', 'grad_mla_k_norm': 'grad_w', 'grad_swa_q_norm': 'grad_w', 'grad_swa_k_norm': 'grad_w', 'grad_swa_sinks': 'grad_w', 'grad_mla_out_norm': 'grad_w', 'grad_swa_out_norm': 'grad_w', 'grad_w_o': 'grad_w', 'grad_ffn2_norm': 'grad_w', 'grad_ffn2_w_gate': 'grad_w', 'grad_ffn2_w_up': 'grad_w', 'grad_ffn2_w_down': 'grad_w', 'delta_ffn1_norm': 'delta_w', 'delta_ffn1_w_gate': 'delta_w', 'delta_ffn1_w_up': 'delta_w', 'delta_ffn1_w_down': 'delta_w', 'delta_mix_norm': 'delta_w', 'delta_w_in': 'delta_w', 'delta_mla_q_a_norm': 'delta_w', 'delta_mla_w_q_b': 'delta_w', 'delta_mla_kv_a_norm': 'delta_w', 'delta_mla_w_kv_b': 'delta_w', 'delta_mla_q_norm': 'delta_w', 'delta_mla_k_norm': 'delta_w', 'delta_swa_q_norm': 'delta_w', 'delta_swa_k_norm': 'delta_w', 'delta_swa_sinks': 'delta_w', 'delta_mla_out_norm': 'delta_w', 'delta_swa_out_norm': 'delta_w', 'delta_w_o': 'delta_w', 'delta_ffn2_norm': 'delta_w', 'delta_ffn2_w_gate': 'delta_w', 'delta_ffn2_w_up': 'delta_w', 'delta_ffn2_w_down': 'delta_w', 'new_m_ffn1_norm': 'new_m', 'new_m_ffn1_w_gate': 'new_m', 'new_m_ffn1_w_up': 'new_m', 'new_m_ffn1_w_down': 'new_m', 'new_m_mix_norm': 'new_m', 'new_m_w_in': 'new_m', 'new_m_mla_q_a_norm': 'new_m', 'new_m_mla_w_q_b': 'new_m', 'new_m_mla_kv_a_norm': 'new_m', 'new_m_mla_w_kv_b': 'new_m', 'new_m_mla_q_norm': 'new_m', 'new_m_mla_k_norm': 'new_m', 'new_m_swa_q_norm': 'new_m', 'new_m_swa_k_norm': 'new_m', 'new_m_swa_sinks': 'new_m', 'new_m_mla_out_norm': 'new_m', 'new_m_swa_out_norm': 'new_m', 'new_m_w_o': 'new_m', 'new_m_ffn2_norm': 'new_m', 'new_m_ffn2_w_gate': 'new_m', 'new_m_ffn2_w_up': 'new_m', 'new_m_ffn2_w_down': 'new_m', 'new_v_ffn1_norm': 'new_v', 'new_v_ffn1_w_gate': 'new_v', 'new_v_ffn1_w_up': 'new_v', 'new_v_ffn1_w_down': 'new_v', 'new_v_mix_norm': 'new_v', 'new_v_w_in': 'new_v', 'new_v_mla_q_a_norm': 'new_v', 'new_v_mla_w_q_b': 'new_v', 'new_v_mla_kv_a_norm': 'new_v', 'new_v_mla_w_kv_b': 'new_v', 'new_v_mla_q_norm': 'new_v', 'new_v_mla_k_norm': 'new_v', 'new_v_swa_q_norm': 'new_v', 'new_v_swa_k_norm': 'new_v', 'new_v_swa_sinks': 'new_v', 'new_v_mla_out_norm': 'new_v', 'new_v_swa_out_norm': 'new_v', 'new_v_w_o': 'new_v', 'new_v_ffn2_norm': 'new_v', 'new_v_ffn2_w_gate': 'new_v', 'new_v_ffn2_w_up': 'new_v', 'new_v_ffn2_w_down': 'new_v'}


def _forward(args):
    return _fwd_reference(*[args[k] for k in FWD_PARAMS])


def _output_shape():
    def fwd():
        inp = _fwd_setup_inputs(0)
        return _fwd_reference(*[inp[k] for k in FWD_PARAMS])
    out = _jax.eval_shape(fwd)
    return out.shape, out.dtype

N_MICROBATCH = 1
ADAM_LR = 0.001
ADAM_B1 = 0.9
ADAM_B2 = 0.999
ADAM_EPS = 1e-08
ADAM_WD = 0.01
ADAM_STEP = 10
PER_EXAMPLE_BATCH_AXIS = {'x': 0, 'loss_target': 0}
SHARED_INPUTS = []
_WEIGHT_DTYPES = {'ffn1_norm': _jnp.float32, 'ffn1_w_gate': _jnp.float32, 'ffn1_w_up': _jnp.float32, 'ffn1_w_down': _jnp.float32, 'mix_norm': _jnp.float32, 'w_in': _jnp.float32, 'mla_q_a_norm': _jnp.float32, 'mla_w_q_b': _jnp.float32, 'mla_kv_a_norm': _jnp.float32, 'mla_w_kv_b': _jnp.float32, 'mla_q_norm': _jnp.float32, 'mla_k_norm': _jnp.float32, 'swa_q_norm': _jnp.float32, 'swa_k_norm': _jnp.float32, 'swa_sinks': _jnp.float32, 'mla_out_norm': _jnp.float32, 'swa_out_norm': _jnp.float32, 'w_o': _jnp.float32, 'ffn2_norm': _jnp.float32, 'ffn2_w_gate': _jnp.float32, 'ffn2_w_up': _jnp.float32, 'ffn2_w_down': _jnp.float32}
MOMENT_SCALE = {'ffn1_norm': 1.315598e+01, 'ffn1_w_gate': 1.449911e+00, 'ffn1_w_up': 1.470671e+00, 'ffn1_w_down': 2.523656e+00, 'mix_norm': 8.711136e+00, 'w_in': 8.381858e+00, 'mla_q_a_norm': 3.268420e+00, 'mla_w_q_b': 1.326059e+00, 'mla_kv_a_norm': 2.797324e+01, 'mla_w_kv_b': 9.764708e+00, 'mla_q_norm': 3.334677e+00, 'mla_k_norm': 3.515299e+00, 'swa_q_norm': 3.537688e+00, 'swa_k_norm': 3.129922e+00, 'swa_sinks': 7.240643e-01, 'mla_out_norm': 6.670389e+01, 'swa_out_norm': 6.122632e+01, 'w_o': 1.061246e+01, 'ffn2_norm': 1.243232e+01, 'ffn2_w_gate': 4.182857e-01, 'ffn2_w_up': 5.720443e-01, 'ffn2_w_down': 9.368701e-01}


def _to_microbatches(a, axis):
    t = _jnp.moveaxis(a, axis, 0)
    t = t.reshape((N_MICROBATCH, t.shape[0] // N_MICROBATCH) + t.shape[1:])
    return _jnp.moveaxis(t, 1, axis + 1)


def setup_inputs(seed: int = 0) -> dict:
    inp = _fwd_setup_inputs(seed)
    key = _jax.random.fold_in(_jax.random.key(seed), 7919)
    shape, _ = _output_shape()
    out = dict(inp)
    out["loss_target"] = _jax.random.normal(_jax.random.fold_in(key, 0), shape, _jnp.float32)
    for i, name in enumerate(TWIN_WEIGHTS):
        w = inp[name].astype(_jnp.float32)
        if MOMENT_SCALE is None:
            s = _jnp.sqrt(_jnp.mean(_jnp.square(w)) + 1e-30)
        else:
            s = MOMENT_SCALE[name]
        km, kv = _jax.random.split(_jax.random.fold_in(key, i + 1))
        out[name] = w
        out["m_" + name] = s * _jax.random.normal(km, w.shape, _jnp.float32)
        out["v_" + name] = (s * s) * _jax.random.uniform(kv, w.shape, _jnp.float32, 0.5, 1.5)
    if N_MICROBATCH > 1:
        for name, axis in PER_EXAMPLE_BATCH_AXIS.items():
            out[name] = _to_microbatches(out[name], axis)
    return {'x': out['x'], 'ffn1_norm': out['ffn1_norm'], 'ffn1_w_gate': out['ffn1_w_gate'], 'ffn1_w_up': out['ffn1_w_up'], 'ffn1_w_down': out['ffn1_w_down'], 'mix_norm': out['mix_norm'], 'w_in': out['w_in'], 'mla_q_a_norm': out['mla_q_a_norm'], 'mla_w_q_b': out['mla_w_q_b'], 'mla_kv_a_norm': out['mla_kv_a_norm'], 'mla_w_kv_b': out['mla_w_kv_b'], 'mla_q_norm': out['mla_q_norm'], 'mla_k_norm': out['mla_k_norm'], 'swa_q_norm': out['swa_q_norm'], 'swa_k_norm': out['swa_k_norm'], 'swa_sinks': out['swa_sinks'], 'mla_out_norm': out['mla_out_norm'], 'swa_out_norm': out['swa_out_norm'], 'w_o': out['w_o'], 'ffn2_norm': out['ffn2_norm'], 'ffn2_w_gate': out['ffn2_w_gate'], 'ffn2_w_up': out['ffn2_w_up'], 'ffn2_w_down': out['ffn2_w_down'], 'loss_target': out['loss_target'], 'm_ffn1_norm': out['m_ffn1_norm'], 'm_ffn1_w_gate': out['m_ffn1_w_gate'], 'm_ffn1_w_up': out['m_ffn1_w_up'], 'm_ffn1_w_down': out['m_ffn1_w_down'], 'm_mix_norm': out['m_mix_norm'], 'm_w_in': out['m_w_in'], 'm_mla_q_a_norm': out['m_mla_q_a_norm'], 'm_mla_w_q_b': out['m_mla_w_q_b'], 'm_mla_kv_a_norm': out['m_mla_kv_a_norm'], 'm_mla_w_kv_b': out['m_mla_w_kv_b'], 'm_mla_q_norm': out['m_mla_q_norm'], 'm_mla_k_norm': out['m_mla_k_norm'], 'm_swa_q_norm': out['m_swa_q_norm'], 'm_swa_k_norm': out['m_swa_k_norm'], 'm_swa_sinks': out['m_swa_sinks'], 'm_mla_out_norm': out['m_mla_out_norm'], 'm_swa_out_norm': out['m_swa_out_norm'], 'm_w_o': out['m_w_o'], 'm_ffn2_norm': out['m_ffn2_norm'], 'm_ffn2_w_gate': out['m_ffn2_w_gate'], 'm_ffn2_w_up': out['m_ffn2_w_up'], 'm_ffn2_w_down': out['m_ffn2_w_down'], 'v_ffn1_norm': out['v_ffn1_norm'], 'v_ffn1_w_gate': out['v_ffn1_w_gate'], 'v_ffn1_w_up': out['v_ffn1_w_up'], 'v_ffn1_w_down': out['v_ffn1_w_down'], 'v_mix_norm': out['v_mix_norm'], 'v_w_in': out['v_w_in'], 'v_mla_q_a_norm': out['v_mla_q_a_norm'], 'v_mla_w_q_b': out['v_mla_w_q_b'], 'v_mla_kv_a_norm': out['v_mla_kv_a_norm'], 'v_mla_w_kv_b': out['v_mla_w_kv_b'], 'v_mla_q_norm': out['v_mla_q_norm'], 'v_mla_k_norm': out['v_mla_k_norm'], 'v_swa_q_norm': out['v_swa_q_norm'], 'v_swa_k_norm': out['v_swa_k_norm'], 'v_swa_sinks': out['v_swa_sinks'], 'v_mla_out_norm': out['v_mla_out_norm'], 'v_swa_out_norm': out['v_swa_out_norm'], 'v_w_o': out['v_w_o'], 'v_ffn2_norm': out['v_ffn2_norm'], 'v_ffn2_w_gate': out['v_ffn2_w_gate'], 'v_ffn2_w_up': out['v_ffn2_w_up'], 'v_ffn2_w_down': out['v_ffn2_w_down']}


def _loss(weights, diff, rest, loss_target):
    with _jax.named_scope("forward"):
        args = {**rest, TWIN_DIFF_INPUT: diff, **{k: w.astype(_WEIGHT_DTYPES[k]) for k, w in weights.items()}}
        y = _forward(args)
    with _jax.named_scope("loss_head"):
        err = _jnp.square(y.astype(_jnp.float32) - loss_target)
        return 0.5 * _jnp.sum(_jnp.mean(err, axis=-1)) if err.ndim else 0.5 * err


def _adamw(w, g, m, v):
    m = ADAM_B1 * m + (1.0 - ADAM_B1) * g
    v = ADAM_B2 * v + (1.0 - ADAM_B2) * _jnp.square(g)
    m_hat = m / (1.0 - ADAM_B1 ** ADAM_STEP)
    v_hat = v / (1.0 - ADAM_B2 ** ADAM_STEP)
    delta = -ADAM_LR * (m_hat / (_jnp.sqrt(v_hat) + ADAM_EPS) + ADAM_WD * w)
    return delta, m, v


def reference(x, ffn1_norm, ffn1_w_gate, ffn1_w_up, ffn1_w_down, mix_norm, w_in, mla_q_a_norm, mla_w_q_b, mla_kv_a_norm, mla_w_kv_b, mla_q_norm, mla_k_norm, swa_q_norm, swa_k_norm, swa_sinks, mla_out_norm, swa_out_norm, w_o, ffn2_norm, ffn2_w_gate, ffn2_w_up, ffn2_w_down, loss_target, m_ffn1_norm, m_ffn1_w_gate, m_ffn1_w_up, m_ffn1_w_down, m_mix_norm, m_w_in, m_mla_q_a_norm, m_mla_w_q_b, m_mla_kv_a_norm, m_mla_w_kv_b, m_mla_q_norm, m_mla_k_norm, m_swa_q_norm, m_swa_k_norm, m_swa_sinks, m_mla_out_norm, m_swa_out_norm, m_w_o, m_ffn2_norm, m_ffn2_w_gate, m_ffn2_w_up, m_ffn2_w_down, v_ffn1_norm, v_ffn1_w_gate, v_ffn1_w_up, v_ffn1_w_down, v_mix_norm, v_w_in, v_mla_q_a_norm, v_mla_w_q_b, v_mla_kv_a_norm, v_mla_w_kv_b, v_mla_q_norm, v_mla_k_norm, v_swa_q_norm, v_swa_k_norm, v_swa_sinks, v_mla_out_norm, v_swa_out_norm, v_w_o, v_ffn2_norm, v_ffn2_w_gate, v_ffn2_w_up, v_ffn2_w_down):
    given = dict(x=x, ffn1_norm=ffn1_norm, ffn1_w_gate=ffn1_w_gate, ffn1_w_up=ffn1_w_up, ffn1_w_down=ffn1_w_down, mix_norm=mix_norm, w_in=w_in, mla_q_a_norm=mla_q_a_norm, mla_w_q_b=mla_w_q_b, mla_kv_a_norm=mla_kv_a_norm, mla_w_kv_b=mla_w_kv_b, mla_q_norm=mla_q_norm, mla_k_norm=mla_k_norm, swa_q_norm=swa_q_norm, swa_k_norm=swa_k_norm, swa_sinks=swa_sinks, mla_out_norm=mla_out_norm, swa_out_norm=swa_out_norm, w_o=w_o, ffn2_norm=ffn2_norm, ffn2_w_gate=ffn2_w_gate, ffn2_w_up=ffn2_w_up, ffn2_w_down=ffn2_w_down, loss_target=loss_target, m_ffn1_norm=m_ffn1_norm, m_ffn1_w_gate=m_ffn1_w_gate, m_ffn1_w_up=m_ffn1_w_up, m_ffn1_w_down=m_ffn1_w_down, m_mix_norm=m_mix_norm, m_w_in=m_w_in, m_mla_q_a_norm=m_mla_q_a_norm, m_mla_w_q_b=m_mla_w_q_b, m_mla_kv_a_norm=m_mla_kv_a_norm, m_mla_w_kv_b=m_mla_w_kv_b, m_mla_q_norm=m_mla_q_norm, m_mla_k_norm=m_mla_k_norm, m_swa_q_norm=m_swa_q_norm, m_swa_k_norm=m_swa_k_norm, m_swa_sinks=m_swa_sinks, m_mla_out_norm=m_mla_out_norm, m_swa_out_norm=m_swa_out_norm, m_w_o=m_w_o, m_ffn2_norm=m_ffn2_norm, m_ffn2_w_gate=m_ffn2_w_gate, m_ffn2_w_up=m_ffn2_w_up, m_ffn2_w_down=m_ffn2_w_down, v_ffn1_norm=v_ffn1_norm, v_ffn1_w_gate=v_ffn1_w_gate, v_ffn1_w_up=v_ffn1_w_up, v_ffn1_w_down=v_ffn1_w_down, v_mix_norm=v_mix_norm, v_w_in=v_w_in, v_mla_q_a_norm=v_mla_q_a_norm, v_mla_w_q_b=v_mla_w_q_b, v_mla_kv_a_norm=v_mla_kv_a_norm, v_mla_w_kv_b=v_mla_w_kv_b, v_mla_q_norm=v_mla_q_norm, v_mla_k_norm=v_mla_k_norm, v_swa_q_norm=v_swa_q_norm, v_swa_k_norm=v_swa_k_norm, v_swa_sinks=v_swa_sinks, v_mla_out_norm=v_mla_out_norm, v_swa_out_norm=v_swa_out_norm, v_w_o=v_w_o, v_ffn2_norm=v_ffn2_norm, v_ffn2_w_gate=v_ffn2_w_gate, v_ffn2_w_up=v_ffn2_w_up, v_ffn2_w_down=v_ffn2_w_down)
    weights = {n: given[n] for n in TWIN_WEIGHTS}
    shared = {n: given[n] for n in SHARED_INPUTS}
    per_example = {n: given[n] for n in ['x']}
    grad_fn = _jax.value_and_grad(_loss, argnums=(0, 1))

    def one_microbatch(ex, loss_target):
        ex = dict(ex)
        diff = ex.pop(TWIN_DIFF_INPUT)
        return grad_fn(weights, diff, {**shared, **ex}, loss_target)

    if N_MICROBATCH == 1:
        loss, (grad_w, grad_x) = one_microbatch(per_example, given["loss_target"])
    else:
        def body(carry, xs):
            loss_sum, grad_sum = carry
            l_k, (gw_k, gx_k) = one_microbatch(xs[0], xs[1])
            with _jax.named_scope("update"):
                return (loss_sum + l_k, _jax.tree.map(_jnp.add, grad_sum, gw_k)), gx_k

        init = (_jnp.zeros((), _jnp.float32), _jax.tree.map(_jnp.zeros_like, weights))
        (loss, grad_w), grad_x = _jax.lax.scan(body, init, (per_example, given["loss_target"]))
    with _jax.named_scope("update"):
        delta_w, new_m, new_v = {}, {}, {}
        for n in TWIN_WEIGHTS:
            delta_w[n], new_m[n], new_v[n] = _adamw(weights[n], grad_w[n], given["m_" + n], given["v_" + n])
    return (loss, grad_x, *[grad_w[n] for n in TWIN_WEIGHTS], *[delta_w[n] for n in TWIN_WEIGHTS],
            *[new_m[n] for n in TWIN_WEIGHTS], *[new_v[n] for n in TWIN_WEIGHTS])
```

```python
import functools

import numpy as np
import jax
import jax.numpy as jnp
from jax import lax
from jax.experimental import pallas as pl
from jax.experimental.pallas import tpu as pltpu

F32 = jnp.float32
BF16 = jnp.bfloat16

D_MODEL = 1024
DEPTH = 2
EPS = 1e-6
ROPE_THETA = 10000.0
BLOCK = 128
MLA_HEADS = 4
MLA_Q_RANK = 256
MLA_KV_RANK = 128
MLA_NOPE = 128
MLA_ROPE = 64
MLA_V = 128
MLA_QK = MLA_NOPE + MLA_ROPE
MLA_WIDTH = MLA_HEADS * MLA_V
SWA_HEADS = 8
SWA_KV_HEADS = 2
SWA_GROUP = SWA_HEADS // SWA_KV_HEADS
SWA_HEAD_DIM = 64
SWA_WIDTH = SWA_HEADS * SWA_HEAD_DIM
MIX_WIDTH = MLA_WIDTH + SWA_WIDTH
IN_SPLITS = (MLA_Q_RANK, MLA_KV_RANK, MLA_ROPE, SWA_WIDTH, SWA_KV_HEADS * SWA_HEAD_DIM, SWA_KV_HEADS * SWA_HEAD_DIM)
IN_COLS = sum(IN_SPLITS)
IN_OFFS = tuple(int(v) for v in np.cumsum((0,) + IN_SPLITS))
D_FF = 2816
MLA_SCALE = MLA_QK ** -0.5
SWA_SCALE = SWA_HEAD_DIM ** -0.5
NEG = -1e30

ADAM_LR = 0.001
ADAM_B1 = 0.9
ADAM_B2 = 0.999
ADAM_EPS = 1e-08
ADAM_WD = 0.01
ADAM_STEP = 10

N_SHARD = 4
N_DEV = 8
PACK_W = 1024
VMEM_LIMIT = 56 * 1024 * 1024
MESH = pl.DeviceIdType.MESH

WEIGHT_NAMES = ['ffn1_norm', 'ffn1_w_gate', 'ffn1_w_up', 'ffn1_w_down', 'mix_norm', 'w_in', 'mla_q_a_norm', 'mla_w_q_b',
                'mla_kv_a_norm', 'mla_w_kv_b', 'mla_q_norm', 'mla_k_norm', 'swa_q_norm', 'swa_k_norm', 'swa_sinks',
                'mla_out_norm', 'swa_out_norm', 'w_o', 'ffn2_norm', 'ffn2_w_gate', 'ffn2_w_up', 'ffn2_w_down']
BIG = {'ffn1_w_gate': ((D_MODEL, D_FF), 1), 'ffn1_w_up': ((D_MODEL, D_FF), 1), 'ffn1_w_down': ((D_FF, D_MODEL), 0),
       'w_in': ((D_MODEL, IN_COLS), 1), 'mla_w_q_b': ((MLA_Q_RANK, MLA_HEADS * MLA_QK), 1),
       'mla_w_kv_b': ((MLA_KV_RANK, MLA_HEADS * (MLA_NOPE + MLA_V)), 1), 'w_o': ((MIX_WIDTH, D_MODEL), 0),
       'ffn2_w_gate': ((D_MODEL, D_FF), 1), 'ffn2_w_up': ((D_MODEL, D_FF), 1), 'ffn2_w_down': ((D_FF, D_MODEL), 0)}
BIG_NAMES = [n for n in WEIGHT_NAMES if n in BIG]
SMALL_NAMES = [n for n in WEIGHT_NAMES if n not in BIG]

_pallas_call = pl.pallas_call


def _params(**kw):
    return pltpu.CompilerParams(vmem_limit_bytes=VMEM_LIMIT, **kw)


def _full(shape):
    n = len(shape)
    return pl.BlockSpec(shape, lambda *_: (0,) * n)


def _resident(shape):
    n = len(shape)
    return pl.BlockSpec(shape, lambda *_: (0,) * n, pipeline_mode=pl.Buffered(1))


@jax.custom_vjp
def _mm(a, w):
    return jnp.dot(a.astype(BF16), w, preferred_element_type=F32)


def _mm_fwd(a, w):
    return _mm(a, w), w


def _mm_bwd(w, dy):
    return lax.dot_general(dy.astype(BF16), w, (((1,), (1,)), ((), ())), preferred_element_type=F32), None


_mm.defvjp(_mm_fwd, _mm_bwd)


def _dot_nt(a, b):
    return lax.dot_general(a, b, (((1,), (1,)), ((), ())), preferred_element_type=F32)


def _dot_tn(a, b):
    return lax.dot_general(a, b, (((0,), (0,)), ((), ())), preferred_element_type=F32)


def _rms(t, g):
    return t * lax.rsqrt(jnp.mean(t * t, axis=-1, keepdims=True) + EPS) * g


def _rope(t, cos, sin):
    half = t.shape[-1] // 2
    t1, t2 = t[:, :half], t[:, half:]
    return jnp.concatenate([t1 * cos - t2 * sin, t2 * cos + t1 * sin], axis=-1)


def _sigmoid(z):
    return 1.0 / (1.0 + jnp.exp(-z))


def _row_tile(s, want):
    return min(want, s)


FF_CHUNK = 1408


def _ffn_fwd(x, g, wg, wu, wd, name):
    s = x.shape[0]
    tm = _row_tile(s, 256)

    def body(x_ref, g_ref, wg_ref, wu_ref, wd_ref, y_ref, gate_ref, up_ref):
        xv = x_ref[...]
        nb = _rms(xv, g_ref[...]).astype(BF16)
        acc = xv
        for c in range(0, D_FF, FF_CHUNK):
            gate = jnp.dot(nb, wg_ref[:, c:c + FF_CHUNK], preferred_element_type=F32)
            up = jnp.dot(nb, wu_ref[:, c:c + FF_CHUNK], preferred_element_type=F32)
            gate_ref[:, c:c + FF_CHUNK] = gate.astype(BF16)
            up_ref[:, c:c + FF_CHUNK] = up.astype(BF16)
            act = (gate * _sigmoid(gate) * up).astype(BF16)
            acc = acc + 0.5 * jnp.dot(act, wd_ref[c:c + FF_CHUNK, :], preferred_element_type=F32)
        y_ref[...] = acc

    return _pallas_call(
        body, name=name, grid=(s // tm,),
        in_specs=[pl.BlockSpec((tm, D_MODEL), lambda i: (i, 0)), _full((1, D_MODEL)),
                  _resident((D_MODEL, D_FF)), _resident((D_MODEL, D_FF)), _resident((D_FF, D_MODEL))],
        out_specs=[pl.BlockSpec((tm, D_MODEL), lambda i: (i, 0)), pl.BlockSpec((tm, D_FF), lambda i: (i, 0)),
                   pl.BlockSpec((tm, D_FF), lambda i: (i, 0))],
        out_shape=[jax.ShapeDtypeStruct((s, D_MODEL), F32), jax.ShapeDtypeStruct((s, D_FF), BF16),
                   jax.ShapeDtypeStruct((s, D_FF), BF16)],
        compiler_params=_params(dimension_semantics=("arbitrary",)),
    )(x, g, wg, wu, wd)


def _ffn_bwd(x, dy, gate, up, g, wg, wu, wd, name):
    s = x.shape[0]
    tm = _row_tile(s, 256)

    def body(x_ref, dy_ref, gate_ref, up_ref, g_ref, wg_ref, wu_ref, wd_ref,
             dx_ref, dgain_ref, n_ref, act_ref, dgate_ref, dup_ref):
        i = pl.program_id(0)
        xv = x_ref[...]
        dyv = dy_ref[...]
        gv = g_ref[...]
        r = lax.rsqrt(jnp.mean(xv * xv, axis=-1, keepdims=True) + EPS)
        xh = xv * r
        n_ref[...] = (xh * gv).astype(BF16)
        dyh = (0.5 * dyv).astype(BF16)
        dn = jnp.zeros_like(xv)
        for c in range(0, D_FF, FF_CHUNK):
            dact = _dot_nt(dyh, wd_ref[c:c + FF_CHUNK, :])
            gt = gate_ref[:, c:c + FF_CHUNK].astype(F32)
            u = up_ref[:, c:c + FF_CHUNK].astype(F32)
            sg = _sigmoid(gt)
            sl = gt * sg
            act_ref[:, c:c + FF_CHUNK] = (sl * u).astype(BF16)
            dup = (dact * sl).astype(BF16)
            dgate = (dact * u * (sg * (1.0 + gt * (1.0 - sg)))).astype(BF16)
            dup_ref[:, c:c + FF_CHUNK] = dup
            dgate_ref[:, c:c + FF_CHUNK] = dgate
            dn = dn + _dot_nt(dgate, wg_ref[:, c:c + FF_CHUNK]) + _dot_nt(dup, wu_ref[:, c:c + FF_CHUNK])
        part = jnp.sum(dn * xh, axis=0, keepdims=True)

        @pl.when(i == 0)
        def _():
            dgain_ref[...] = part

        @pl.when(i > 0)
        def _():
            dgain_ref[...] += part

        dxh = dn * gv
        dx_ref[...] = dyv + r * (dxh - xh * jnp.mean(dxh * xh, axis=-1, keepdims=True))

    row = lambda w: pl.BlockSpec((tm, w), lambda i: (i, 0))
    return _pallas_call(
        body, name=name, grid=(s // tm,),
        in_specs=[row(D_MODEL), row(D_MODEL), row(D_FF), row(D_FF), _full((1, D_MODEL)),
                  _resident((D_MODEL, D_FF)), _resident((D_MODEL, D_FF)), _resident((D_FF, D_MODEL))],
        out_specs=[row(D_MODEL), _full((1, D_MODEL)), row(D_MODEL), row(D_FF), row(D_FF), row(D_FF)],
        out_shape=[jax.ShapeDtypeStruct((s, D_MODEL), F32), jax.ShapeDtypeStruct((1, D_MODEL), F32),
                   jax.ShapeDtypeStruct((s, D_MODEL), BF16), jax.ShapeDtypeStruct((s, D_FF), BF16),
                   jax.ShapeDtypeStruct((s, D_FF), BF16), jax.ShapeDtypeStruct((s, D_FF), BF16)],
        compiler_params=_params(dimension_semantics=("arbitrary",)),
    )(x, dy, gate, up, g, wg, wu, wd)


def _matmul_tn(a, b, scale, name):
    t, m = a.shape
    n = b.shape[1]
    tk = _row_tile(t, 512)
    tn = n
    for cand in (1408, 1024, 768, 512):
        if n % cand == 0 and n > cand:
            tn = cand
            break
    nk = t // tk

    def body(a_ref, b_ref, o_ref):
        k = pl.program_id(1)
        bv = b_ref[...]
        if scale != 1.0:
            bv = bv.astype(F32) * scale
        part = _dot_tn(a_ref[...].astype(BF16), bv.astype(BF16))

        @pl.when(k == 0)
        def _():
            o_ref[...] = part

        @pl.when(k > 0)
        def _():
            o_ref[...] += part

    return _pallas_call(
        body, name=name, grid=(n // tn, nk),
        in_specs=[pl.BlockSpec((tk, m), lambda j, k: (k, 0)), pl.BlockSpec((tk, tn), lambda j, k: (k, j))],
        out_specs=pl.BlockSpec((m, tn), lambda j, k: (0, j)),
        out_shape=jax.ShapeDtypeStruct((m, n), F32),
        compiler_params=_params(dimension_semantics=("arbitrary", "arbitrary")),
    )(a, b)


def _pre_math(x, gm, gqa, gkva, gq, gk, gsq, gsk, taps, win, wqb, wkvb, cos, sin):
    h = _rms(x, gm)
    proj = _mm(h, win)
    if taps is not None:
        proj = proj + taps[0]
    o = IN_OFFS
    c_q, c_kv, k_pe = proj[:, o[0]:o[1]], proj[:, o[1]:o[2]], proj[:, o[2]:o[3]]
    q_s, k_s, v_s = proj[:, o[3]:o[4]], proj[:, o[4]:o[5]], proj[:, o[5]:o[6]]
    cqn = _rms(c_q, gqa)
    qa_all = _mm(cqn, wqb)
    ckvn = _rms(c_kv, gkva)
    kv_all = _mm(ckvn, wkvb)
    if taps is not None:
        qa_all = qa_all + taps[1]
        kv_all = kv_all + taps[2]
    q_a, k_a, v_a = [], [], []
    kvw = MLA_NOPE + MLA_V
    for hd in range(MLA_HEADS):
        qn = _rms(qa_all[:, hd * MLA_QK:(hd + 1) * MLA_QK], gq)
        k_nope = kv_all[:, hd * kvw:hd * kvw + MLA_NOPE]
        v_a.append(kv_all[:, hd * kvw + MLA_NOPE:(hd + 1) * kvw])
        kn = _rms(jnp.concatenate([k_nope, k_pe], axis=-1), gk)
        q_a.append(jnp.concatenate([qn[:, :MLA_NOPE], _rope(qn[:, MLA_NOPE:], cos, sin)], axis=-1))
        k_a.append(jnp.concatenate([kn[:, :MLA_NOPE], _rope(kn[:, MLA_NOPE:], cos, sin)], axis=-1))
    d = SWA_HEAD_DIM
    q_b = [_rope(_rms(q_s[:, hd * d:(hd + 1) * d], gsq), cos, sin) for hd in range(SWA_HEADS)]
    k_b = [_rope(_rms(k_s[:, j * d:(j + 1) * d], gsk), cos, sin) for j in range(SWA_KV_HEADS)]
    v_b = [v_s[:, j * d:(j + 1) * d] for j in range(SWA_KV_HEADS)]
    return (q_a, k_a, v_a, q_b, k_b, v_b), (h, cqn, ckvn)


_PRE_GAIN_WIDTHS = (D_MODEL, MLA_Q_RANK, MLA_KV_RANK, MLA_QK, MLA_QK, SWA_HEAD_DIM, SWA_HEAD_DIM)
_PRE_HEADS = ((MLA_HEADS, MLA_QK), (MLA_HEADS, MLA_QK), (MLA_HEADS, MLA_V),
              (SWA_HEADS, SWA_HEAD_DIM), (SWA_KV_HEADS, SWA_HEAD_DIM), (SWA_KV_HEADS, SWA_HEAD_DIM))


def _pre_fwd(x, gains, win, wqb, wkvb, cos, sin, name):
    s = x.shape[0]
    tm = _row_tile(s, 256)

    def body(x_ref, *refs):
        g_refs, (win_ref, wqb_ref, wkvb_ref, cos_ref, sin_ref), out_refs = refs[:7], refs[7:12], refs[12:]
        outs, _ = _pre_math(x_ref[...], *[g[...] for g in g_refs], None, win_ref[...], wqb_ref[...], wkvb_ref[...],
                            cos_ref[...], sin_ref[...])
        for ref, heads in zip(out_refs, outs):
            for hd, val in enumerate(heads):
                ref[hd] = val.astype(BF16)

    heads_spec = lambda nh, w: pl.BlockSpec((nh, tm, w), lambda i: (0, i, 0))
    return _pallas_call(
        body, name=name, grid=(s // tm,),
        in_specs=[pl.BlockSpec((tm, D_MODEL), lambda i: (i, 0))] + [_full((1, w)) for w in _PRE_GAIN_WIDTHS]
        + [_resident(win.shape), _resident(wqb.shape), _resident(wkvb.shape),
           pl.BlockSpec((tm, MLA_ROPE // 2), lambda i: (i, 0)), pl.BlockSpec((tm, MLA_ROPE // 2), lambda i: (i, 0))],
        out_specs=[heads_spec(nh, w) for nh, w in _PRE_HEADS],
        out_shape=[jax.ShapeDtypeStruct((nh, s, w), BF16) for nh, w in _PRE_HEADS],
        compiler_params=_params(dimension_semantics=("arbitrary",)),
    )(x, *gains, win, wqb, wkvb, cos, sin)


def _pre_bwd(x, dx_res, cts, gains, win, wqb, wkvb, cos, sin, name):
    s = x.shape[0]
    tm = _row_tile(s, 256)
    tap_widths = (IN_COLS, MLA_HEADS * MLA_QK, MLA_HEADS * (MLA_NOPE + MLA_V))

    def body(x_ref, dxr_ref, *refs):
        ct_refs, g_refs = refs[:6], refs[6:13]
        win_ref, wqb_ref, wkvb_ref, cos_ref, sin_ref = refs[13:18]
        dx_ref, dg_refs, dw_refs = refs[18], refs[19:26], refs[26:29]
        i = pl.program_id(0)
        win_v, wqb_v, wkvb_v, cos_v, sin_v = win_ref[...], wqb_ref[...], wkvb_ref[...], cos_ref[...], sin_ref[...]

        def f(xv, gm, gqa, gkva, gq, gk, gsq, gsk, t0, t1, t2):
            return _pre_math(xv, gm, gqa, gkva, gq, gk, gsq, gsk, (t0, t1, t2), win_v, wqb_v, wkvb_v, cos_v, sin_v)

        taps = [jnp.zeros((tm, w), F32) for w in tap_widths]
        _, vjp, acts = jax.vjp(f, x_ref[...], *[g[...] for g in g_refs], *taps, has_aux=True)
        ct = tuple([ref[hd] for hd in range(nh)] for ref, (nh, _) in zip(ct_refs, _PRE_HEADS))
        grads = vjp(ct)
        dx_ref[...] = grads[0] + dxr_ref[...]
        dws = [_dot_tn(a.astype(BF16), t.astype(BF16)) for a, t in zip(acts, grads[8:11])]

        @pl.when(i == 0)
        def _():
            for ref, val in zip(dg_refs, grads[1:8]):
                ref[...] = val
            for ref, val in zip(dw_refs, dws):
                ref[...] = val

        @pl.when(i > 0)
        def _():
            for ref, val in zip(dg_refs, grads[1:8]):
                ref[...] += val
            for ref, val in zip(dw_refs, dws):
                ref[...] += val

    heads_spec = lambda nh, w: pl.BlockSpec((nh, tm, w), lambda i: (0, i, 0))
    row = pl.BlockSpec((tm, D_MODEL), lambda i: (i, 0))
    half = pl.BlockSpec((tm, MLA_ROPE // 2), lambda i: (i, 0))
    return _pallas_call(
        body, name=name, grid=(s // tm,),
        in_specs=[row, row] + [heads_spec(nh, w) for nh, w in _PRE_HEADS] + [_full((1, w)) for w in _PRE_GAIN_WIDTHS]
        + [_resident(win.shape), _resident(wqb.shape), _resident(wkvb.shape), half, half],
        out_specs=[row] + [_full((1, w)) for w in _PRE_GAIN_WIDTHS] + [_full(win.shape), _full(wqb.shape), _full(wkvb.shape)],
        out_shape=[jax.ShapeDtypeStruct((s, D_MODEL), F32)] + [jax.ShapeDtypeStruct((1, w), F32) for w in _PRE_GAIN_WIDTHS]
        + [jax.ShapeDtypeStruct(w.shape, F32) for w in (win, wqb, wkvb)],
        compiler_params=_params(dimension_semantics=("arbitrary",)),
    )(x, dx_res, *cts, *gains, win, wqb, wkvb, cos, sin)


def _post_math(oa, ob, ga, gb, wo):
    mixed = jnp.concatenate([_rms(jnp.concatenate(oa, axis=-1), ga), _rms(jnp.concatenate(ob, axis=-1), gb)], axis=-1)
    return _mm(mixed, wo), mixed


def _post_fwd(x, oa, ob, ga, gb, wo, name):
    s = x.shape[0]
    tm = _row_tile(s, 256)

    def body(x_ref, oa_ref, ob_ref, ga_ref, gb_ref, wo_ref, y_ref):
        y, _ = _post_math([oa_ref[hd] for hd in range(MLA_HEADS)], [ob_ref[hd] for hd in range(SWA_HEADS)],
                          ga_ref[...], gb_ref[...], wo_ref[...])
        y_ref[...] = x_ref[...] + y

    row = pl.BlockSpec((tm, D_MODEL), lambda i: (i, 0))
    return _pallas_call(
        body, name=name, grid=(s // tm,),
        in_specs=[row, pl.BlockSpec((MLA_HEADS, tm, MLA_V), lambda i: (0, i, 0)),
                  pl.BlockSpec((SWA_HEADS, tm, SWA_HEAD_DIM), lambda i: (0, i, 0)),
                  _full((1, MLA_WIDTH)), _full((1, SWA_WIDTH)), _resident(wo.shape)],
        out_specs=row, out_shape=jax.ShapeDtypeStruct((s, D_MODEL), F32),
        compiler_params=_params(dimension_semantics=("arbitrary",)),
    )(x, oa, ob, ga, gb, wo)


def _post_bwd(dy, oa, ob, ga, gb, wo, name):
    s = dy.shape[0]
    tm = _row_tile(s, 256)

    def body(dy_ref, oa_ref, ob_ref, ga_ref, gb_ref, wo_ref, doa_ref, dob_ref, dga_ref, dgb_ref, dwo_ref):
        i = pl.program_id(0)
        wo_v = wo_ref[...]
        dyv = dy_ref[...]

        def f(oa_l, ob_l, ga_v, gb_v):
            return _post_math(oa_l, ob_l, ga_v, gb_v, wo_v)

        _, vjp, mixed = jax.vjp(f, [oa_ref[hd] for hd in range(MLA_HEADS)], [ob_ref[hd] for hd in range(SWA_HEADS)],
                                ga_ref[...], gb_ref[...], has_aux=True)
        doa, dob, dga, dgb = vjp(dyv)
        for hd in range(MLA_HEADS):
            doa_ref[hd] = doa[hd]
        for hd in range(SWA_HEADS):
            dob_ref[hd] = dob[hd]
        dwo = _dot_tn(mixed.astype(BF16), dyv.astype(BF16))

        @pl.when(i == 0)
        def _():
            dga_ref[...] = dga
            dgb_ref[...] = dgb
            dwo_ref[...] = dwo

        @pl.when(i > 0)
        def _():
            dga_ref[...] += dga
            dgb_ref[...] += dgb
            dwo_ref[...] += dwo

    row = pl.BlockSpec((tm, D_MODEL), lambda i: (i, 0))
    oa_spec = pl.BlockSpec((MLA_HEADS, tm, MLA_V), lambda i: (0, i, 0))
    ob_spec = pl.BlockSpec((SWA_HEADS, tm, SWA_HEAD_DIM), lambda i: (0, i, 0))
    return _pallas_call(
        body, name=name, grid=(s // tm,),
        in_specs=[row, oa_spec, ob_spec, _full((1, MLA_WIDTH)), _full((1, SWA_WIDTH)), _resident(wo.shape)],
        out_specs=[oa_spec, ob_spec, _full((1, MLA_WIDTH)), _full((1, SWA_WIDTH)), _full(wo.shape)],
        out_shape=[jax.ShapeDtypeStruct((MLA_HEADS, s, MLA_V), F32), jax.ShapeDtypeStruct((SWA_HEADS, s, SWA_HEAD_DIM), F32),
                   jax.ShapeDtypeStruct((1, MLA_WIDTH), F32), jax.ShapeDtypeStruct((1, SWA_WIDTH), F32),
                   jax.ShapeDtypeStruct(wo.shape, F32)],
        compiler_params=_params(dimension_semantics=("arbitrary",)),
    )(dy, oa, ob, ga, gb, wo)


def _attn_tile(s):
    return 512 if s >= 2048 else 128


def _causal_mask(t):
    return lax.broadcasted_iota(jnp.int32, (t, t), 1) <= lax.broadcasted_iota(jnp.int32, (t, t), 0)


def _mla_fwd(q, k, v, name):
    nh, s, _ = q.shape
    t = _attn_tile(s)

    def body(q_ref, k_ref, v_ref, o_ref, lse_ref):
        qi = pl.program_id(1)
        qv = q_ref[...]

        def block(j, carry, masked):
            m, l, acc = carry
            start = pl.multiple_of(j * t, t)
            sc = _dot_nt(qv, k_ref[pl.ds(start, t), :]) * MLA_SCALE
            if masked:
                sc = jnp.where(_causal_mask(t), sc, NEG)
            m_new = jnp.maximum(m, jnp.max(sc, axis=-1, keepdims=True))
            alpha = jnp.exp(m - m_new)
            p = jnp.exp(sc - m_new)
            l = alpha * l + jnp.sum(p, axis=-1, keepdims=True)
            acc = alpha * acc + jnp.dot(p.astype(BF16), v_ref[pl.ds(start, t), :], preferred_element_type=F32)
            return m_new, l, acc

        init = (jnp.full((t, 1), NEG, F32), jnp.zeros((t, 1), F32), jnp.zeros((t, MLA_V), F32))
        carry = lax.fori_loop(0, qi, lambda j, c: block(j, c, False), init)
        m, l, acc = block(qi, carry, True)
        o_ref[...] = acc / l
        lse_ref[...] = m + jnp.log(l)

    return _pallas_call(
        body, name=name, grid=(nh, s // t),
        in_specs=[pl.BlockSpec((None, t, MLA_QK), lambda h, i: (h, i, 0)), pl.BlockSpec((None, s, MLA_QK), lambda h, i: (h, 0, 0)),
                  pl.BlockSpec((None, s, MLA_V), lambda h, i: (h, 0, 0))],
        out_specs=[pl.BlockSpec((None, t, MLA_V), lambda h, i: (h, i, 0)), pl.BlockSpec((None, t, 1), lambda h, i: (h, i, 0))],
        out_shape=[jax.ShapeDtypeStruct((nh, s, MLA_V), F32), jax.ShapeDtypeStruct((nh, s, 1), F32)],
        compiler_params=_params(dimension_semantics=("arbitrary", "arbitrary")),
    )(q, k, v)


def _mla_bwd_dq(q, k, v, o, do, lse, name):
    nh, s, _ = q.shape
    t = _attn_tile(s)

    def body(q_ref, k_ref, v_ref, o_ref, do_ref, lse_ref, dq_ref, delta_ref):
        qi = pl.program_id(1)
        qv = q_ref[...]
        dov = do_ref[...]
        lse = lse_ref[...]
        delta = jnp.sum(dov * o_ref[...], axis=-1, keepdims=True)
        delta_ref[...] = delta
        dob = dov.astype(BF16)

        def block(j, dq, masked):
            start = pl.multiple_of(j * t, t)
            kb = k_ref[pl.ds(start, t), :]
            sc = _dot_nt(qv, kb) * MLA_SCALE
            p = jnp.exp(sc - lse)
            if masked:
                p = jnp.where(_causal_mask(t), p, 0.0)
            dp = _dot_nt(dob, v_ref[pl.ds(start, t), :])
            ds = p * (dp - delta) * MLA_SCALE
            return dq + jnp.dot(ds.astype(BF16), kb, preferred_element_type=F32)

        dq = lax.fori_loop(0, qi, lambda j, c: block(j, c, False), jnp.zeros((t, MLA_QK), F32))
        dq_ref[...] = block(qi, dq, True)

    tile = lambda w: pl.BlockSpec((None, t, w), lambda h, i: (h, i, 0))
    whole = lambda w: pl.BlockSpec((None, s, w), lambda h, i: (h, 0, 0))
    return _pallas_call(
        body, name=name, grid=(nh, s // t),
        in_specs=[tile(MLA_QK), whole(MLA_QK), whole(MLA_V), tile(MLA_V), tile(MLA_V), tile(1)],
        out_specs=[tile(MLA_QK), tile(1)],
        out_shape=[jax.ShapeDtypeStruct((nh, s, MLA_QK), F32), jax.ShapeDtypeStruct((nh, s, 1), F32)],
        compiler_params=_params(dimension_semantics=("arbitrary", "arbitrary")),
    )(q, k, v, o, do, lse)


def _mla_bwd_dkv(q, k, v, do, lse_row, delta_row, name):
    nh, s, _ = q.shape
    t = _attn_tile(s)
    nq = s // t

    def body(q_ref, k_ref, v_ref, do_ref, lse_ref, delta_ref, dk_ref, dv_ref):
        kj = pl.program_id(1)
        kv_, vv = k_ref[...], v_ref[...]

        def block(i, carry, masked):
            dk, dv = carry
            start = pl.multiple_of(i * t, t)
            qb = q_ref[pl.ds(start, t), :]
            dob = do_ref[pl.ds(start, t), :].astype(BF16)
            sc = _dot_nt(kv_, qb) * MLA_SCALE
            p = jnp.exp(sc - lse_ref[pl.ds(i, 1), :])
            if masked:
                p = jnp.where(lax.broadcasted_iota(jnp.int32, (t, t), 0) <= lax.broadcasted_iota(jnp.int32, (t, t), 1), p, 0.0)
            dv = dv + jnp.dot(p.astype(BF16), dob, preferred_element_type=F32)
            dp = _dot_nt(vv, dob)
            ds = p * (dp - delta_ref[pl.ds(i, 1), :]) * MLA_SCALE
            dk = dk + jnp.dot(ds.astype(BF16), qb, preferred_element_type=F32)
            return dk, dv

        carry = block(kj, (jnp.zeros((t, MLA_QK), F32), jnp.zeros((t, MLA_V), F32)), True)
        dk, dv = lax.fori_loop(kj + 1, nq, lambda i, c: block(i, c, False), carry)
        dk_ref[...] = dk
        dv_ref[...] = dv

    tile = lambda w: pl.BlockSpec((None, t, w), lambda h, j: (h, j, 0))
    whole = lambda w: pl.BlockSpec((None, s, w), lambda h, j: (h, 0, 0))
    rows = pl.BlockSpec((None, nq, t), lambda h, j: (h, 0, 0))
    return _pallas_call(
        body, name=name, grid=(nh, nq),
        in_specs=[whole(MLA_QK), tile(MLA_QK), tile(MLA_V), whole(MLA_V), rows, rows],
        out_specs=[tile(MLA_QK), tile(MLA_V)],
        out_shape=[jax.ShapeDtypeStruct((nh, s, MLA_QK), F32), jax.ShapeDtypeStruct((nh, s, MLA_V), F32)],
        compiler_params=_params(dimension_semantics=("arbitrary", "arbitrary")),
    )(q, k, v, do, lse_row, delta_row)


def _swa_math(q_l, kband, vband, sink_l, first):
    q_rel = lax.broadcasted_iota(jnp.int32, (BLOCK, 2 * BLOCK), 0) + BLOCK
    k_rel = lax.broadcasted_iota(jnp.int32, (BLOCK, 2 * BLOCK), 1)
    dist = q_rel - k_rel
    valid = (dist >= 0) & (dist < BLOCK) & ((k_rel >= BLOCK) | jnp.logical_not(first))
    kb, vb = kband.astype(BF16), vband.astype(BF16)
    outs, psinks = [], []
    for qh, sink in zip(q_l, sink_l):
        sc = jnp.where(valid, _dot_nt(qh.astype(BF16), kb) * SWA_SCALE, NEG)
        m = lax.stop_gradient(jnp.maximum(jnp.max(sc, axis=-1, keepdims=True), sink))
        e = jnp.exp(sc - m)
        es = jnp.exp(sink - m)
        den = jnp.sum(e, axis=-1, keepdims=True) + es
        outs.append(jnp.dot((e / den).astype(BF16), vb, preferred_element_type=F32))
        psinks.append(es / den)
    return outs, psinks


def _swa_specs():
    grp = lambda: pl.BlockSpec((SWA_GROUP, BLOCK, SWA_HEAD_DIM), lambda j, n: (j, n, 0))
    prev = pl.BlockSpec((None, BLOCK, SWA_HEAD_DIM), lambda j, n: (j, n, 0))
    cur = pl.BlockSpec((None, BLOCK, SWA_HEAD_DIM), lambda j, n: (j, n + 1, 0))
    sink = pl.BlockSpec((None, SWA_GROUP, 128), lambda j, n: (j, 0, 0))
    return grp, prev, cur, sink


def _swa_fwd(q, kpad, vpad, sinks, name):
    _, s, _ = q.shape
    grp, prev, cur, sink = _swa_specs()

    def body(q_ref, kp_ref, kc_ref, vp_ref, vc_ref, sink_ref, o_ref):
        first = pl.program_id(1) == 0
        kband = jnp.concatenate([kp_ref[...], kc_ref[...]], axis=0)
        vband = jnp.concatenate([vp_ref[...], vc_ref[...]], axis=0)
        sk = sink_ref[...]
        outs, _ = _swa_math([q_ref[g] for g in range(SWA_GROUP)], kband, vband,
                            [sk[g:g + 1, 0:1] for g in range(SWA_GROUP)], first)
        for g in range(SWA_GROUP):
            o_ref[g] = outs[g]

    return _pallas_call(
        body, name=name, grid=(SWA_KV_HEADS, s // BLOCK),
        in_specs=[grp(), prev, cur, prev, cur, sink], out_specs=grp(),
        out_shape=jax.ShapeDtypeStruct((SWA_HEADS, s, SWA_HEAD_DIM), F32),
        compiler_params=_params(dimension_semantics=("arbitrary", "arbitrary")),
    )(q, kpad, kpad, vpad, vpad, sinks)


def _swa_bwd(q, kpad, vpad, sinks, do, name):
    _, s, _ = q.shape
    grp, prev, cur, sink = _swa_specs()

    def body(q_ref, kp_ref, kc_ref, vp_ref, vc_ref, sink_ref, do_ref, dq_ref, dk_ref, dv_ref, dsink_ref):
        n = pl.program_id(1)
        first = n == 0
        sk = sink_ref[...]
        sink_l = [sk[g:g + 1, 0:1] for g in range(SWA_GROUP)]
        kband = jnp.concatenate([kp_ref[...], kc_ref[...]], axis=0).astype(F32)
        vband = jnp.concatenate([vp_ref[...], vc_ref[...]], axis=0).astype(F32)
        q_l = [q_ref[g].astype(F32) for g in range(SWA_GROUP)]

        def f(ql, kb, vb):
            return _swa_math(ql, kb, vb, sink_l, first)

        outs, vjp, psinks = jax.vjp(f, q_l, kband, vband, has_aux=True)
        do_l = [do_ref[g] for g in range(SWA_GROUP)]
        dq_l, dkb, dvb = vjp(do_l)
        for g in range(SWA_GROUP):
            dq_ref[g] = dq_l[g]
        dsk = jnp.concatenate(
            [jnp.broadcast_to(-jnp.sum(psinks[g] * jnp.sum(do_l[g] * outs[g], axis=-1, keepdims=True), axis=0, keepdims=True),
                              (1, 128)) for g in range(SWA_GROUP)] + [jnp.zeros((8 - SWA_GROUP, 128), F32)], axis=0)

        @pl.when(first)
        def _():
            dk_ref[...] = jnp.zeros_like(dk_ref)
            dv_ref[...] = jnp.zeros_like(dv_ref)
            dsink_ref[...] = jnp.zeros_like(dsink_ref)

        rows = pl.ds(pl.multiple_of(n * BLOCK, BLOCK), 2 * BLOCK)
        dk_ref[rows, :] += dkb
        dv_ref[rows, :] += dvb
        dsink_ref[...] += dsk

    acc = pl.BlockSpec((None, s + BLOCK, SWA_HEAD_DIM), lambda j, n: (j, 0, 0))
    return _pallas_call(
        body, name=name, grid=(SWA_KV_HEADS, s // BLOCK),
        in_specs=[grp(), prev, cur, prev, cur, sink, grp()],
        out_specs=[grp(), acc, acc, pl.BlockSpec((None, 8, 128), lambda j, n: (j, 0, 0))],
        out_shape=[jax.ShapeDtypeStruct((SWA_HEADS, s, SWA_HEAD_DIM), F32),
                   jax.ShapeDtypeStruct((SWA_KV_HEADS, s + BLOCK, SWA_HEAD_DIM), F32),
                   jax.ShapeDtypeStruct((SWA_KV_HEADS, s + BLOCK, SWA_HEAD_DIM), F32),
                   jax.ShapeDtypeStruct((SWA_KV_HEADS, 8, 128), F32)],
        compiler_params=_params(dimension_semantics=("arbitrary", "arbitrary")),
    )(q, kpad, kpad, vpad, vpad, sinks, do)


def _loss_head(y, target, name):
    s = y.shape[0]
    tm = _row_tile(s, 512)

    def body(y_ref, t_ref, dy_ref, loss_ref):
        i = pl.program_id(0)
        err = y_ref[...] - t_ref[...]
        dy_ref[...] = err * (1.0 / D_MODEL)
        part = jnp.broadcast_to(0.5 * jnp.sum(jnp.mean(err * err, axis=-1, keepdims=True), axis=0, keepdims=True), (1, 128))

        @pl.when(i == 0)
        def _():
            loss_ref[...] = part

        @pl.when(i > 0)
        def _():
            loss_ref[...] += part

    row = pl.BlockSpec((tm, D_MODEL), lambda i: (i, 0))
    return _pallas_call(
        body, name=name, grid=(s // tm,), in_specs=[row, row], out_specs=[row, _full((1, 128))],
        out_shape=[jax.ShapeDtypeStruct((s, D_MODEL), F32), jax.ShapeDtypeStruct((1, 128), F32)],
        compiler_params=_params(dimension_semantics=("arbitrary",)),
    )(y, target)


def _adamw(w, g, m, v, name):
    rows, cols = w.shape
    tr = rows
    for cand in (512, 256, 128, 64, 32, 16, 8):
        if rows % cand == 0 and rows > cand:
            tr = cand
            break

    def body(w_ref, g_ref, m_ref, v_ref, d_ref, nm_ref, nv_ref):
        gv = g_ref[...]
        nm = ADAM_B1 * m_ref[...] + (1.0 - ADAM_B1) * gv
        nv = ADAM_B2 * v_ref[...] + (1.0 - ADAM_B2) * (gv * gv)
        m_hat = nm / (1.0 - ADAM_B1 ** ADAM_STEP)
        v_hat = nv / (1.0 - ADAM_B2 ** ADAM_STEP)
        d_ref[...] = -ADAM_LR * (m_hat / (jnp.sqrt(v_hat) + ADAM_EPS) + ADAM_WD * w_ref[...])
        nm_ref[...] = nm
        nv_ref[...] = nv

    blk = pl.BlockSpec((tr, cols), lambda i: (i, 0))
    return _pallas_call(
        body, name=name, grid=(rows // tr,), in_specs=[blk] * 4, out_specs=[blk] * 3,
        out_shape=[jax.ShapeDtypeStruct((rows, cols), F32)] * 3,
        compiler_params=_params(dimension_semantics=("arbitrary",)),
    )(w, g, m, v)


def _position():
    return lax.axis_index("x"), lax.axis_index("y"), lax.axis_index("c")


def _remote(src, dst, send_sems, recv_sems, k, to):
    return pltpu.make_async_remote_copy(src_ref=src, dst_ref=dst, send_sem=send_sems.at[k], recv_sem=recv_sems.at[k],
                                        device_id=to, device_id_type=MESH)


_HBM = pl.BlockSpec(memory_space=pltpu.HBM)


def _all_gather_blocks(mine, name):
    _, m, w = mine.shape

    def body(x_ref, out_ref, send_sems, recv_sems, local_sem):
        x, y, c = _position()
        me, sibling = (x, y, c), (x, y, 1 - c)
        chips = [(1 - x, y), (x, 1 - y), (1 - x, 1 - y)]
        slot = lambda px, py, pc: out_ref.at[4 * px + 2 * py + pc]
        src = x_ref.at[c]
        own = pltpu.make_async_copy(src, slot(*me), local_sem)
        own.start()
        first = [_remote(src, slot(*me), send_sems, recv_sems, 0, sibling)]
        first += [_remote(src, slot(*me), send_sems, recv_sems, 1 + j, (*chip, c)) for j, chip in enumerate(chips)]
        for cp in first:
            cp.start()
        passed = [_remote(slot(*chip, c), slot(*chip, c), send_sems, recv_sems, 4 + j, sibling) for j, chip in enumerate(chips)]
        for j, chip in enumerate(chips):
            _remote(slot(*chip, c), slot(*chip, c), send_sems, recv_sems, 1 + j, me).wait_recv()
            passed[j].start()
        _remote(slot(*sibling), slot(*sibling), send_sems, recv_sems, 0, me).wait_recv()
        for j, chip in enumerate(chips):
            _remote(slot(*chip, 1 - c), slot(*chip, 1 - c), send_sems, recv_sems, 4 + j, me).wait_recv()
        for cp in first + passed:
            cp.wait_send()
        own.wait()

    return _pallas_call(
        body, name=name, in_specs=[_HBM], out_specs=_HBM,
        out_shape=jax.ShapeDtypeStruct((N_DEV, m, w), mine.dtype),
        scratch_shapes=[pltpu.SemaphoreType.DMA((7,)), pltpu.SemaphoreType.DMA((7,)), pltpu.SemaphoreType.DMA],
    )(mine)


def _sibling_swap_halves(parts, name):
    n, _, m, w = parts.shape

    def body(p_ref, got_ref, send_sems, recv_sems):
        x, y, c = _position()
        sibling = (x, y, 1 - c)
        copies = [_remote(p_ref.at[sh, 1 - c], got_ref.at[sh], send_sems, recv_sems, sh, sibling) for sh in range(n)]
        for cp in copies:
            cp.start()
        for cp in copies:
            cp.wait()

    return _pallas_call(
        body, name=name, in_specs=[_HBM], out_specs=_HBM,
        out_shape=jax.ShapeDtypeStruct((n, m, w), parts.dtype),
        scratch_shapes=[pltpu.SemaphoreType.DMA((n,)), pltpu.SemaphoreType.DMA((n,))],
    )(parts)


def _scatter_to_chips(parts, name):
    _, m, w = parts.shape

    def body(p_ref, got_ref, send_sems, recv_sems):
        x, y, c = _position()
        chips = [(1 - x, y), (x, 1 - y), (1 - x, 1 - y)]
        copies = [_remote(p_ref.at[2 * px + py], got_ref.at[j], send_sems, recv_sems, j, (px, py, c))
                  for j, (px, py) in enumerate(chips)]
        for cp in copies:
            cp.start()
        for cp in copies:
            cp.wait()

    return _pallas_call(
        body, name=name, in_specs=[_HBM], out_specs=_HBM,
        out_shape=jax.ShapeDtypeStruct((3, m, w), parts.dtype),
        scratch_shapes=[pltpu.SemaphoreType.DMA((3,)), pltpu.SemaphoreType.DMA((3,))],
    )(parts)


def _sibling_share(half, name):
    m, w = half.shape

    def body(h_ref, out_ref, send_sems, recv_sems, local_sem):
        x, y, c = _position()
        own = pltpu.make_async_copy(h_ref, out_ref.at[c], local_sem)
        own.start()
        cp = _remote(h_ref, out_ref.at[c], send_sems, recv_sems, 0, (x, y, 1 - c))
        cp.start()
        _remote(out_ref.at[1 - c], out_ref.at[1 - c], send_sems, recv_sems, 0, (x, y, c)).wait_recv()
        cp.wait_send()
        own.wait()

    return _pallas_call(
        body, name=name, in_specs=[_HBM], out_specs=_HBM,
        out_shape=jax.ShapeDtypeStruct((2, m, w), half.dtype),
        scratch_shapes=[pltpu.SemaphoreType.DMA((1,)), pltpu.SemaphoreType.DMA((1,)), pltpu.SemaphoreType.DMA],
    )(half)


def _all_reduce_small(vec, name):
    r, l = vec.shape

    def body(v_ref, out_ref, gath_ref, send_sems, recv_sems):
        x, y, c = _position()
        me = 4 * x + 2 * y + c
        gath_ref[me] = v_ref[...]
        copies = []
        for k in range(1, N_DEV):
            to = (x ^ (k >> 2), y ^ ((k >> 1) & 1), c ^ (k & 1))
            copies.append(_remote(gath_ref.at[me], gath_ref.at[me], send_sems, recv_sems, k - 1, to))
        for cp in copies:
            cp.start()
        for k in range(1, N_DEV):
            frm = 4 * (x ^ (k >> 2)) + 2 * (y ^ ((k >> 1) & 1)) + (c ^ (k & 1))
            _remote(gath_ref.at[frm], gath_ref.at[frm], send_sems, recv_sems, k - 1, (x, y, c)).wait_recv()
        for cp in copies:
            cp.wait_send()
        total = gath_ref[0]
        for d in range(1, N_DEV):
            total = total + gath_ref[d]
        out_ref[...] = total

    vm = pl.BlockSpec(memory_space=pltpu.VMEM)
    return _pallas_call(
        body, name=name, in_specs=[vm], out_specs=vm, out_shape=jax.ShapeDtypeStruct((r, l), F32),
        scratch_shapes=[pltpu.VMEM((N_DEV, r, l), F32), pltpu.SemaphoreType.DMA((N_DEV - 1,)),
                        pltpu.SemaphoreType.DMA((N_DEV - 1,))],
    )(vec)


def _sum_blocks(blocks, out_dtype, name):
    m, w = blocks[0].shape
    tr = _row_tile(m, 304)

    def body(*refs):
        total = refs[0][...].astype(F32)
        for ref in refs[1:-1]:
            total = total + ref[...].astype(F32)
        refs[-1][...] = total.astype(out_dtype)

    blk = pl.BlockSpec((tr, w), lambda i: (i, 0))
    return _pallas_call(
        body, name=name, grid=(m // tr,), in_specs=[blk] * len(blocks), out_specs=blk,
        out_shape=jax.ShapeDtypeStruct((m, w), out_dtype),
        compiler_params=_params(dimension_semantics=("arbitrary",)),
    )(*blocks)


def _shard_major(full, axis):
    ax = axis + 1
    shp = full.shape
    split = full.reshape(shp[:ax] + (N_SHARD, shp[ax] // N_SHARD) + shp[ax + 1:])
    return jnp.moveaxis(split, ax, 0).reshape(N_SHARD, -1)


def _from_shard_major(flat, name):
    shape, axis = BIG[name]
    ax = axis + 1
    shp = (DEPTH,) + shape
    split = flat.reshape((N_SHARD,) + shp[:ax] + (shp[ax] // N_SHARD,) + shp[ax + 1:])
    return jnp.moveaxis(split, 0, ax).reshape(shp)


def _shard_sizes():
    return [DEPTH * int(np.prod(BIG[n][0])) // N_SHARD for n in BIG_NAMES]


def _rope_tables(s):
    pos = jnp.arange(s, dtype=F32)
    inv = 1.0 / (ROPE_THETA ** (jnp.arange(0, MLA_ROPE, 2, dtype=F32) / MLA_ROPE))
    ang = pos[:, None] * inv[None, :]
    return jnp.cos(ang), jnp.sin(ang)


def _local_step(x, target, small, big):
    s = x.shape[0]
    cos, sin = _rope_tables(s)
    row = lambda name, l: small[name][l][None, :]
    saved = []
    for l in range(DEPTH):
        sv = {'x0': x}
        x, sv['g1'], sv['u1'] = _ffn_fwd(x, row('ffn1_norm', l), big['ffn1_w_gate'][l], big['ffn1_w_up'][l],
                                        big['ffn1_w_down'][l], f"ffn1_fwd_{l}")
        sv['x1'] = x
        gains = [row(n, l) for n in ('mix_norm', 'mla_q_a_norm', 'mla_kv_a_norm', 'mla_q_norm', 'mla_k_norm',
                                     'swa_q_norm', 'swa_k_norm')]
        q_a, k_a, v_a, q_b, k_b, v_b = _pre_fwd(x, gains, big['w_in'][l], big['mla_w_q_b'][l], big['mla_w_kv_b'][l],
                                                cos, sin, f"pre_fwd_{l}")
        o_a, lse = _mla_fwd(q_a, k_a, v_a, f"mla_fwd_{l}")
        kpad = jnp.pad(k_b, ((0, 0), (BLOCK, 0), (0, 0)))
        vpad = jnp.pad(v_b, ((0, 0), (BLOCK, 0), (0, 0)))
        sinks = jnp.broadcast_to(small['swa_sinks'][l].reshape(SWA_KV_HEADS, SWA_GROUP, 1), (SWA_KV_HEADS, SWA_GROUP, 128))
        o_b = _swa_fwd(q_b, kpad, vpad, sinks, f"swa_fwd_{l}")
        sv.update(gains=gains, q_a=q_a, k_a=k_a, v_a=v_a, q_b=q_b, kpad=kpad, vpad=vpad, sinks=sinks, o_a=o_a, lse=lse, o_b=o_b)
        x = _post_fwd(x, o_a, o_b, row('mla_out_norm', l), row('swa_out_norm', l), big['w_o'][l], f"post_fwd_{l}")
        sv['x2'] = x
        x, sv['g2'], sv['u2'] = _ffn_fwd(x, row('ffn2_norm', l), big['ffn2_w_gate'][l], big['ffn2_w_up'][l],
                                        big['ffn2_w_down'][l], f"ffn2_fwd_{l}")
        saved.append(sv)

    dx, loss = _loss_head(x, target, "loss_head")

    gs = {n: [None] * DEPTH for n in SMALL_NAMES}
    gb = {n: [None] * DEPTH for n in BIG_NAMES}
    t = _attn_tile(s)
    for l in reversed(range(DEPTH)):
        sv = saved[l]

        def ffn_back(tag, xin, dy, gate, up):
            dxi, dgain, nb, act, dgate, dup = _ffn_bwd(xin, dy, gate, up, row(tag + '_norm', l), big[tag + '_w_gate'][l],
                                                      big[tag + '_w_up'][l], big[tag + '_w_down'][l], f"{tag}_bwd_{l}")
            gs[tag + '_norm'][l] = dgain[0]
            gb[tag + '_w_gate'][l] = _matmul_tn(nb, dgate, 1.0, f"{tag}_dwg_{l}")
            gb[tag + '_w_up'][l] = _matmul_tn(nb, dup, 1.0, f"{tag}_dwu_{l}")
            gb[tag + '_w_down'][l] = _matmul_tn(act, dy, 0.5, f"{tag}_dwd_{l}")
            return dxi

        dx = ffn_back('ffn2', sv['x2'], dx, sv['g2'], sv['u2'])
        do_a, do_b, dga, dgb, dwo = _post_bwd(dx, sv['o_a'], sv['o_b'], row('mla_out_norm', l), row('swa_out_norm', l),
                                              big['w_o'][l], f"post_bwd_{l}")
        gs['mla_out_norm'][l], gs['swa_out_norm'][l], gb['w_o'][l] = dga[0], dgb[0], dwo
        dq_a, delta = _mla_bwd_dq(sv['q_a'], sv['k_a'], sv['v_a'], sv['o_a'], do_a, sv['lse'], f"mla_dq_{l}")
        dk_a, dv_a = _mla_bwd_dkv(sv['q_a'], sv['k_a'], sv['v_a'], do_a, sv['lse'].reshape(MLA_HEADS, s // t, t),
                                  delta.reshape(MLA_HEADS, s // t, t), f"mla_dkv_{l}")
        dq_b, dkpad, dvpad, dsink = _swa_bwd(sv['q_b'], sv['kpad'], sv['vpad'], sv['sinks'], do_b, f"swa_bwd_{l}")
        gs['swa_sinks'][l] = dsink[:, :SWA_GROUP, 0].reshape(SWA_HEADS)
        cts = [dq_a, dk_a, dv_a, dq_b, dkpad[:, BLOCK:], dvpad[:, BLOCK:]]
        outs = _pre_bwd(sv['x1'], dx, cts, sv['gains'], big['w_in'][l], big['mla_w_q_b'][l], big['mla_w_kv_b'][l],
                        cos, sin, f"pre_bwd_{l}")
        dx = outs[0]
        for n, val in zip(('mix_norm', 'mla_q_a_norm', 'mla_kv_a_norm', 'mla_q_norm', 'mla_k_norm', 'swa_q_norm', 'swa_k_norm'),
                          outs[1:8]):
            gs[n][l] = val[0]
        gb['w_in'][l], gb['mla_w_q_b'][l], gb['mla_w_kv_b'][l] = outs[8:11]
        dx = ffn_back('ffn1', sv['x0'], dx, sv['g1'], sv['u1'])
    return loss, dx, gs, gb


def kernel(x, ffn1_norm, ffn1_w_gate, ffn1_w_up, ffn1_w_down, mix_norm, w_in, mla_q_a_norm, mla_w_q_b, mla_kv_a_norm, mla_w_kv_b, mla_q_norm, mla_k_norm, swa_q_norm, swa_k_norm, swa_sinks, mla_out_norm, swa_out_norm, w_o, ffn2_norm, ffn2_w_gate, ffn2_w_up, ffn2_w_down, loss_target, m_ffn1_norm, m_ffn1_w_gate, m_ffn1_w_up, m_ffn1_w_down, m_mix_norm, m_w_in, m_mla_q_a_norm, m_mla_w_q_b, m_mla_kv_a_norm, m_mla_w_kv_b, m_mla_q_norm, m_mla_k_norm, m_swa_q_norm, m_swa_k_norm, m_swa_sinks, m_mla_out_norm, m_swa_out_norm, m_w_o, m_ffn2_norm, m_ffn2_w_gate, m_ffn2_w_up, m_ffn2_w_down, v_ffn1_norm, v_ffn1_w_gate, v_ffn1_w_up, v_ffn1_w_down, v_mix_norm, v_w_in, v_mla_q_a_norm, v_mla_w_q_b, v_mla_kv_a_norm, v_mla_w_kv_b, v_mla_q_norm, v_mla_k_norm, v_swa_q_norm, v_swa_k_norm, v_swa_sinks, v_mla_out_norm, v_swa_out_norm, v_w_o, v_ffn2_norm, v_ffn2_w_gate, v_ffn2_w_up, v_ffn2_w_down):
    args = dict(locals())
    weights = {n: args[n] for n in WEIGHT_NAMES}
    mom_m = {n: args["m_" + n] for n in WEIGHT_NAMES}
    mom_v = {n: args["v_" + n] for n in WEIGHT_NAMES}

    sizes = _shard_sizes()
    total = sum(sizes)
    half_rows = total // (2 * PACK_W)
    mine = jnp.concatenate([weights[n].astype(BF16).reshape(-1) for n in BIG_NAMES]).reshape(2, half_rows, PACK_W)
    gathered = _all_gather_blocks(mine, "gather_weights").reshape(N_SHARD, total)
    offs = np.cumsum([0] + sizes)
    big = {n: _from_shard_major(gathered[:, offs[i]:offs[i + 1]], n) for i, n in enumerate(BIG_NAMES)}
    small = {n: weights[n] for n in SMALL_NAMES}

    loss, dx, gs, gb = _local_step(x[0], loss_target[0], small, big)

    small_flat = jnp.concatenate([jnp.stack(gs[n]).reshape(-1) for n in SMALL_NAMES] + [loss[0, :1]])
    n_small = small_flat.shape[0]
    lanes = -(-n_small // (8 * 128)) * 128
    small_sum = _all_reduce_small(jnp.pad(small_flat, (0, 8 * lanes - n_small)).reshape(8, lanes), "reduce_small").reshape(-1)
    grads = {}
    off = 0
    for n in SMALL_NAMES:
        cnt = int(np.prod(weights[n].shape))
        grads[n] = small_sum[off:off + cnt].reshape(weights[n].shape)
        off += cnt
    loss_out = small_sum[off]

    packed = jnp.concatenate([_shard_major(jnp.stack(gb[n]), BIG[n][1]).astype(BF16) for n in BIG_NAMES], axis=1)
    packed = packed.reshape(N_SHARD, 2, half_rows, PACK_W)
    c = lax.axis_index("c")
    chip = 2 * lax.axis_index("x") + lax.axis_index("y")
    from_sibling = _sibling_swap_halves(packed, "swap_halves")
    kept = lax.dynamic_index_in_dim(packed, c, axis=1, keepdims=False)
    chip_sum = jnp.stack([_sum_blocks([kept[sh], from_sibling[sh]], BF16, f"sum_pair_{sh}") for sh in range(N_SHARD)])
    from_chips = _scatter_to_chips(chip_sum, "scatter_chips")
    own = lax.dynamic_index_in_dim(chip_sum, chip, axis=0, keepdims=False)
    reduced_half = _sum_blocks([own, from_chips[0], from_chips[1], from_chips[2]], F32, "sum_chips")
    reduced = _sibling_share(reduced_half, "share_halves").reshape(total)
    for i, n in enumerate(BIG_NAMES):
        grads[n] = reduced[offs[i]:offs[i + 1]].reshape(weights[n].shape)

    deltas, new_m, new_v = {}, {}, {}
    for n in WEIGHT_NAMES:
        shp = weights[n].shape
        two_d = (DEPTH, shp[-1]) if len(shp) == 2 else (shp[0] * shp[1], shp[2])
        d, nm, nv = _adamw(weights[n].reshape(two_d), grads[n].reshape(two_d), mom_m[n].reshape(two_d),
                           mom_v[n].reshape(two_d), f"adamw_{n}")
        deltas[n], new_m[n], new_v[n] = d.reshape(shp), nm.reshape(shp), nv.reshape(shp)

    return (loss_out, dx[None], *[grads[n] for n in WEIGHT_NAMES], *[deltas[n] for n in WEIGHT_NAMES],
            *[new_m[n] for n in WEIGHT_NAMES], *[new_v[n] for n in WEIGHT_NAMES])
```

```python
import functools

import numpy as np
import jax
import jax.numpy as jnp
from jax import lax
from jax.experimental import pallas as pl
from jax.experimental.pallas import tpu as pltpu

F32 = jnp.float32
BF16 = jnp.bfloat16

D_MODEL = 1024
DEPTH = 2
EPS = 1e-6
ROPE_THETA = 10000.0
BLOCK = 128
MLA_HEADS = 4
MLA_Q_RANK = 256
MLA_KV_RANK = 128
MLA_NOPE = 128
MLA_ROPE = 64
MLA_V = 128
MLA_QK = MLA_NOPE + MLA_ROPE
MLA_WIDTH = MLA_HEADS * MLA_V
SWA_HEADS = 8
SWA_KV_HEADS = 2
SWA_GROUP = SWA_HEADS // SWA_KV_HEADS
SWA_HEAD_DIM = 64
SWA_WIDTH = SWA_HEADS * SWA_HEAD_DIM
MIX_WIDTH = MLA_WIDTH + SWA_WIDTH
IN_SPLITS = (MLA_Q_RANK, MLA_KV_RANK, MLA_ROPE, SWA_WIDTH, SWA_KV_HEADS * SWA_HEAD_DIM, SWA_KV_HEADS * SWA_HEAD_DIM)
IN_COLS = sum(IN_SPLITS)
IN_OFFS = tuple(int(v) for v in np.cumsum((0,) + IN_SPLITS))
D_FF = 2816
MLA_SCALE = MLA_QK ** -0.5
SWA_SCALE = SWA_HEAD_DIM ** -0.5
NEG = -1e30

ADAM_LR = 0.001
ADAM_B1 = 0.9
ADAM_B2 = 0.999
ADAM_EPS = 1e-08
ADAM_WD = 0.01
ADAM_STEP = 10

N_SHARD = 4
N_DEV = 8
VMEM_LIMIT = 56 * 1024 * 1024
MESH = pl.DeviceIdType.MESH

WEIGHT_NAMES = ['ffn1_norm', 'ffn1_w_gate', 'ffn1_w_up', 'ffn1_w_down', 'mix_norm', 'w_in', 'mla_q_a_norm', 'mla_w_q_b',
                'mla_kv_a_norm', 'mla_w_kv_b', 'mla_q_norm', 'mla_k_norm', 'swa_q_norm', 'swa_k_norm', 'swa_sinks',
                'mla_out_norm', 'swa_out_norm', 'w_o', 'ffn2_norm', 'ffn2_w_gate', 'ffn2_w_up', 'ffn2_w_down']
BIG = {'ffn1_w_gate': ((D_MODEL, D_FF), 1), 'ffn1_w_up': ((D_MODEL, D_FF), 1), 'ffn1_w_down': ((D_FF, D_MODEL), 0),
       'w_in': ((D_MODEL, IN_COLS), 1), 'mla_w_q_b': ((MLA_Q_RANK, MLA_HEADS * MLA_QK), 1),
       'mla_w_kv_b': ((MLA_KV_RANK, MLA_HEADS * (MLA_NOPE + MLA_V)), 1), 'w_o': ((MIX_WIDTH, D_MODEL), 0),
       'ffn2_w_gate': ((D_MODEL, D_FF), 1), 'ffn2_w_up': ((D_MODEL, D_FF), 1), 'ffn2_w_down': ((D_FF, D_MODEL), 0)}
BIG_NAMES = [n for n in WEIGHT_NAMES if n in BIG]
SMALL_NAMES = [n for n in WEIGHT_NAMES if n not in BIG]

_pallas_call = pl.pallas_call


def _params(**kw):
    return pltpu.CompilerParams(vmem_limit_bytes=VMEM_LIMIT, **kw)


def _full(shape):
    n = len(shape)
    return pl.BlockSpec(shape, lambda *_: (0,) * n)


def _resident(shape):
    n = len(shape)
    return pl.BlockSpec(shape, lambda *_: (0,) * n, pipeline_mode=pl.Buffered(1))


@jax.custom_vjp
def _mm(a, w):
    return jnp.dot(a.astype(BF16), w, preferred_element_type=F32)


def _mm_fwd(a, w):
    return _mm(a, w), w


def _mm_bwd(w, dy):
    return lax.dot_general(dy.astype(BF16), w, (((1,), (1,)), ((), ())), preferred_element_type=F32), None


_mm.defvjp(_mm_fwd, _mm_bwd)


def _dot_nt(a, b):
    return lax.dot_general(a, b, (((1,), (1,)), ((), ())), preferred_element_type=F32)


def _dot_tn(a, b):
    return lax.dot_general(a, b, (((0,), (0,)), ((), ())), preferred_element_type=F32)


def _rms(t, g):
    return t * lax.rsqrt(jnp.mean(t * t, axis=-1, keepdims=True) + EPS) * g


def _rope(t, cos, sin):
    half = t.shape[-1] // 2
    t1, t2 = t[:, :half], t[:, half:]
    return jnp.concatenate([t1 * cos - t2 * sin, t2 * cos + t1 * sin], axis=-1)


def _sigmoid(z):
    return 1.0 / (1.0 + jnp.exp(-z))


def _row_tile(s, want):
    return min(want, s)


FF_CHUNK = 1408


def _ffn_fwd(x, g, wg, wu, wd, name):
    s = x.shape[0]
    tm = _row_tile(s, 256)

    def body(x_ref, g_ref, wg_ref, wu_ref, wd_ref, y_ref, gate_ref, up_ref):
        xv = x_ref[...]
        nb = _rms(xv, g_ref[...]).astype(BF16)
        acc = xv
        for c in range(0, D_FF, FF_CHUNK):
            gate = jnp.dot(nb, wg_ref[:, c:c + FF_CHUNK], preferred_element_type=F32)
            up = jnp.dot(nb, wu_ref[:, c:c + FF_CHUNK], preferred_element_type=F32)
            gate_ref[:, c:c + FF_CHUNK] = gate.astype(BF16)
            up_ref[:, c:c + FF_CHUNK] = up.astype(BF16)
            act = (gate * _sigmoid(gate) * up).astype(BF16)
            acc = acc + 0.5 * jnp.dot(act, wd_ref[c:c + FF_CHUNK, :], preferred_element_type=F32)
        y_ref[...] = acc

    return _pallas_call(
        body, name=name, grid=(s // tm,),
        in_specs=[pl.BlockSpec((tm, D_MODEL), lambda i: (i, 0)), _full((1, D_MODEL)),
                  _resident((D_MODEL, D_FF)), _resident((D_MODEL, D_FF)), _resident((D_FF, D_MODEL))],
        out_specs=[pl.BlockSpec((tm, D_MODEL), lambda i: (i, 0)), pl.BlockSpec((tm, D_FF), lambda i: (i, 0)),
                   pl.BlockSpec((tm, D_FF), lambda i: (i, 0))],
        out_shape=[jax.ShapeDtypeStruct((s, D_MODEL), F32), jax.ShapeDtypeStruct((s, D_FF), BF16),
                   jax.ShapeDtypeStruct((s, D_FF), BF16)],
        compiler_params=_params(dimension_semantics=("arbitrary",)),
    )(x, g, wg, wu, wd)


def _ffn_bwd(x, dy, gate, up, g, wg, wu, wd, name):
    s = x.shape[0]
    tm = _row_tile(s, 256)

    def body(x_ref, dy_ref, gate_ref, up_ref, g_ref, wg_ref, wu_ref, wd_ref,
             dx_ref, dgain_ref, n_ref, act_ref, dgate_ref, dup_ref):
        i = pl.program_id(0)
        xv = x_ref[...]
        dyv = dy_ref[...]
        gv = g_ref[...]
        r = lax.rsqrt(jnp.mean(xv * xv, axis=-1, keepdims=True) + EPS)
        xh = xv * r
        n_ref[...] = (xh * gv).astype(BF16)
        dyh = (0.5 * dyv).astype(BF16)
        dn = jnp.zeros_like(xv)
        for c in range(0, D_FF, FF_CHUNK):
            dact = _dot_nt(dyh, wd_ref[c:c + FF_CHUNK, :])
            gt = gate_ref[:, c:c + FF_CHUNK].astype(F32)
            u = up_ref[:, c:c + FF_CHUNK].astype(F32)
            sg = _sigmoid(gt)
            sl = gt * sg
            act_ref[:, c:c + FF_CHUNK] = (sl * u).astype(BF16)
            dup = (dact * sl).astype(BF16)
            dgate = (dact * u * (sg * (1.0 + gt * (1.0 - sg)))).astype(BF16)
            dup_ref[:, c:c + FF_CHUNK] = dup
            dgate_ref[:, c:c + FF_CHUNK] = dgate
            dn = dn + _dot_nt(dgate, wg_ref[:, c:c + FF_CHUNK]) + _dot_nt(dup, wu_ref[:, c:c + FF_CHUNK])
        part = jnp.sum(dn * xh, axis=0, keepdims=True)

        @pl.when(i == 0)
        def _():
            dgain_ref[...] = part

        @pl.when(i > 0)
        def _():
            dgain_ref[...] += part

        dxh = dn * gv
        dx_ref[...] = dyv + r * (dxh - xh * jnp.mean(dxh * xh, axis=-1, keepdims=True))

    row = lambda w: pl.BlockSpec((tm, w), lambda i: (i, 0))
    return _pallas_call(
        body, name=name, grid=(s // tm,),
        in_specs=[row(D_MODEL), row(D_MODEL), row(D_FF), row(D_FF), _full((1, D_MODEL)),
                  _resident((D_MODEL, D_FF)), _resident((D_MODEL, D_FF)), _resident((D_FF, D_MODEL))],
        out_specs=[row(D_MODEL), _full((1, D_MODEL)), row(D_MODEL), row(D_FF), row(D_FF), row(D_FF)],
        out_shape=[jax.ShapeDtypeStruct((s, D_MODEL), F32), jax.ShapeDtypeStruct((1, D_MODEL), F32),
                   jax.ShapeDtypeStruct((s, D_MODEL), BF16), jax.ShapeDtypeStruct((s, D_FF), BF16),
                   jax.ShapeDtypeStruct((s, D_FF), BF16), jax.ShapeDtypeStruct((s, D_FF), BF16)],
        compiler_params=_params(dimension_semantics=("arbitrary",)),
    )(x, dy, gate, up, g, wg, wu, wd)


def _store_col_shards(o_ref, acc, first_shard, n_here, width):
    for q in range(n_here):
        o_ref[q] = acc[:, (first_shard + q) * width:(first_shard + q + 1) * width].astype(BF16)


def _matmul_tn(a, b, scale, name, col_shards):
    t, m = a.shape
    n = b.shape[1]
    tk = _row_tile(t, 512)
    tn = n // 2
    per = n // N_SHARD
    nk = t // tk

    def body(a_ref, b_ref, o_ref, acc_ref):
        k = pl.program_id(1)
        bv = b_ref[...]
        if scale != 1.0:
            bv = bv.astype(F32) * scale
        part = _dot_tn(a_ref[...].astype(BF16), bv.astype(BF16))

        @pl.when(k == 0)
        def _():
            acc_ref[...] = part

        @pl.when(k > 0)
        def _():
            acc_ref[...] += part

        @pl.when(k == nk - 1)
        def _():
            if col_shards:
                _store_col_shards(o_ref, acc_ref[...], 0, tn // per, per)
            else:
                o_ref[...] = acc_ref[...].astype(BF16)

    if col_shards:
        out_spec = pl.BlockSpec((tn // per, m, per), lambda j, k: (j, 0, 0))
        out_shape = jax.ShapeDtypeStruct((N_SHARD, m, per), BF16)
    else:
        out_spec = pl.BlockSpec((m, tn), lambda j, k: (0, j))
        out_shape = jax.ShapeDtypeStruct((m, n), BF16)
    return _pallas_call(
        body, name=name, grid=(n // tn, nk),
        in_specs=[pl.BlockSpec((tk, m), lambda j, k: (k, 0)), pl.BlockSpec((tk, tn), lambda j, k: (k, j))],
        out_specs=out_spec, out_shape=out_shape, scratch_shapes=[pltpu.VMEM((m, tn), F32)],
        compiler_params=_params(dimension_semantics=("arbitrary", "arbitrary")),
    )(a, b)


def _pre_math(x, gm, gqa, gkva, gq, gk, gsq, gsk, taps, win, wqb, wkvb, cos, sin):
    h = _rms(x, gm)
    proj = _mm(h, win)
    if taps is not None:
        proj = proj + taps[0]
    o = IN_OFFS
    c_q, c_kv, k_pe = proj[:, o[0]:o[1]], proj[:, o[1]:o[2]], proj[:, o[2]:o[3]]
    q_s, k_s, v_s = proj[:, o[3]:o[4]], proj[:, o[4]:o[5]], proj[:, o[5]:o[6]]
    cqn = _rms(c_q, gqa)
    qa_all = _mm(cqn, wqb)
    ckvn = _rms(c_kv, gkva)
    kv_all = _mm(ckvn, wkvb)
    if taps is not None:
        qa_all = qa_all + taps[1]
        kv_all = kv_all + taps[2]
    q_a, k_a, v_a = [], [], []
    kvw = MLA_NOPE + MLA_V
    for hd in range(MLA_HEADS):
        qn = _rms(qa_all[:, hd * MLA_QK:(hd + 1) * MLA_QK], gq)
        k_nope = kv_all[:, hd * kvw:hd * kvw + MLA_NOPE]
        v_a.append(kv_all[:, hd * kvw + MLA_NOPE:(hd + 1) * kvw])
        kn = _rms(jnp.concatenate([k_nope, k_pe], axis=-1), gk)
        q_a.append(jnp.concatenate([qn[:, :MLA_NOPE], _rope(qn[:, MLA_NOPE:], cos, sin)], axis=-1))
        k_a.append(jnp.concatenate([kn[:, :MLA_NOPE], _rope(kn[:, MLA_NOPE:], cos, sin)], axis=-1))
    d = SWA_HEAD_DIM
    q_b = [_rope(_rms(q_s[:, hd * d:(hd + 1) * d], gsq), cos, sin) for hd in range(SWA_HEADS)]
    k_b = [_rope(_rms(k_s[:, j * d:(j + 1) * d], gsk), cos, sin) for j in range(SWA_KV_HEADS)]
    v_b = [v_s[:, j * d:(j + 1) * d] for j in range(SWA_KV_HEADS)]
    return (q_a, k_a, v_a, q_b, k_b, v_b), (h, cqn, ckvn)


_PRE_GAIN_WIDTHS = (D_MODEL, MLA_Q_RANK, MLA_KV_RANK, MLA_QK, MLA_QK, SWA_HEAD_DIM, SWA_HEAD_DIM)
_PRE_HEADS = ((MLA_HEADS, MLA_QK), (MLA_HEADS, MLA_QK), (MLA_HEADS, MLA_V),
              (SWA_HEADS, SWA_HEAD_DIM), (SWA_KV_HEADS, SWA_HEAD_DIM), (SWA_KV_HEADS, SWA_HEAD_DIM))


def _pre_fwd(x, gains, win, wqb, wkvb, cos, sin, name):
    s = x.shape[0]
    tm = _row_tile(s, 256)

    def body(x_ref, *refs):
        g_refs, (win_ref, wqb_ref, wkvb_ref, cos_ref, sin_ref), out_refs = refs[:7], refs[7:12], refs[12:]
        outs, _ = _pre_math(x_ref[...], *[g[...] for g in g_refs], None, win_ref[...], wqb_ref[...], wkvb_ref[...],
                            cos_ref[...], sin_ref[...])
        for ref, heads in zip(out_refs, outs):
            for hd, val in enumerate(heads):
                ref[hd] = val.astype(BF16)

    heads_spec = lambda nh, w: pl.BlockSpec((nh, tm, w), lambda i: (0, i, 0))
    return _pallas_call(
        body, name=name, grid=(s // tm,),
        in_specs=[pl.BlockSpec((tm, D_MODEL), lambda i: (i, 0))] + [_full((1, w)) for w in _PRE_GAIN_WIDTHS]
        + [_resident(win.shape), _resident(wqb.shape), _resident(wkvb.shape),
           pl.BlockSpec((tm, MLA_ROPE // 2), lambda i: (i, 0)), pl.BlockSpec((tm, MLA_ROPE // 2), lambda i: (i, 0))],
        out_specs=[heads_spec(nh, w) for nh, w in _PRE_HEADS],
        out_shape=[jax.ShapeDtypeStruct((nh, s, w), BF16) for nh, w in _PRE_HEADS],
        compiler_params=_params(dimension_semantics=("arbitrary",)),
    )(x, *gains, win, wqb, wkvb, cos, sin)


def _pre_bwd(x, dx_res, cts, gains, win, wqb, wkvb, cos, sin, name):
    s = x.shape[0]
    tm = _row_tile(s, 256)
    tap_widths = (IN_COLS, MLA_HEADS * MLA_QK, MLA_HEADS * (MLA_NOPE + MLA_V))

    def body(x_ref, dxr_ref, *refs):
        ct_refs, g_refs = refs[:6], refs[6:13]
        win_ref, wqb_ref, wkvb_ref, cos_ref, sin_ref = refs[13:18]
        dx_ref, dg_refs, dw_refs, acc_refs = refs[18], refs[19:26], refs[26:29], refs[29:32]
        i = pl.program_id(0)
        win_v, wqb_v, wkvb_v, cos_v, sin_v = win_ref[...], wqb_ref[...], wkvb_ref[...], cos_ref[...], sin_ref[...]

        def f(xv, gm, gqa, gkva, gq, gk, gsq, gsk, t0, t1, t2):
            return _pre_math(xv, gm, gqa, gkva, gq, gk, gsq, gsk, (t0, t1, t2), win_v, wqb_v, wkvb_v, cos_v, sin_v)

        taps = [jnp.zeros((tm, w), F32) for w in tap_widths]
        _, vjp, acts = jax.vjp(f, x_ref[...], *[g[...] for g in g_refs], *taps, has_aux=True)
        ct = tuple([ref[hd] for hd in range(nh)] for ref, (nh, _) in zip(ct_refs, _PRE_HEADS))
        grads = vjp(ct)
        dx_ref[...] = grads[0] + dxr_ref[...]
        dws = [_dot_tn(a.astype(BF16), t.astype(BF16)) for a, t in zip(acts, grads[8:11])]

        @pl.when(i == 0)
        def _():
            for ref, val in zip(dg_refs, grads[1:8]):
                ref[...] = val
            for ref, val in zip(acc_refs, dws):
                ref[...] = val

        @pl.when(i > 0)
        def _():
            for ref, val in zip(dg_refs, grads[1:8]):
                ref[...] += val
            for ref, val in zip(acc_refs, dws):
                ref[...] += val

        @pl.when(i == s // tm - 1)
        def _():
            for ref, acc in zip(dw_refs, acc_refs):
                _store_col_shards(ref, acc[...], 0, N_SHARD, acc.shape[1] // N_SHARD)

    heads_spec = lambda nh, w: pl.BlockSpec((nh, tm, w), lambda i: (0, i, 0))
    row = pl.BlockSpec((tm, D_MODEL), lambda i: (i, 0))
    half = pl.BlockSpec((tm, MLA_ROPE // 2), lambda i: (i, 0))
    shard_shapes = [(N_SHARD, w.shape[0], w.shape[1] // N_SHARD) for w in (win, wqb, wkvb)]
    return _pallas_call(
        body, name=name, grid=(s // tm,),
        in_specs=[row, row] + [heads_spec(nh, w) for nh, w in _PRE_HEADS] + [_full((1, w)) for w in _PRE_GAIN_WIDTHS]
        + [_resident(win.shape), _resident(wqb.shape), _resident(wkvb.shape), half, half],
        out_specs=[row] + [_full((1, w)) for w in _PRE_GAIN_WIDTHS] + [_full(shp) for shp in shard_shapes],
        out_shape=[jax.ShapeDtypeStruct((s, D_MODEL), F32)] + [jax.ShapeDtypeStruct((1, w), F32) for w in _PRE_GAIN_WIDTHS]
        + [jax.ShapeDtypeStruct(shp, BF16) for shp in shard_shapes],
        scratch_shapes=[pltpu.VMEM(w.shape, F32) for w in (win, wqb, wkvb)],
        compiler_params=_params(dimension_semantics=("arbitrary",)),
    )(x, dx_res, *cts, *gains, win, wqb, wkvb, cos, sin)


def _post_math(oa, ob, ga, gb, wo):
    mixed = jnp.concatenate([_rms(jnp.concatenate(oa, axis=-1), ga), _rms(jnp.concatenate(ob, axis=-1), gb)], axis=-1)
    return _mm(mixed, wo), mixed


def _post_fwd(x, oa, ob, ga, gb, wo, name):
    s = x.shape[0]
    tm = _row_tile(s, 256)

    def body(x_ref, oa_ref, ob_ref, ga_ref, gb_ref, wo_ref, y_ref):
        y, _ = _post_math([oa_ref[hd] for hd in range(MLA_HEADS)], [ob_ref[hd] for hd in range(SWA_HEADS)],
                          ga_ref[...], gb_ref[...], wo_ref[...])
        y_ref[...] = x_ref[...] + y

    row = pl.BlockSpec((tm, D_MODEL), lambda i: (i, 0))
    return _pallas_call(
        body, name=name, grid=(s // tm,),
        in_specs=[row, pl.BlockSpec((MLA_HEADS, tm, MLA_V), lambda i: (0, i, 0)),
                  pl.BlockSpec((SWA_HEADS, tm, SWA_HEAD_DIM), lambda i: (0, i, 0)),
                  _full((1, MLA_WIDTH)), _full((1, SWA_WIDTH)), _resident(wo.shape)],
        out_specs=row, out_shape=jax.ShapeDtypeStruct((s, D_MODEL), F32),
        compiler_params=_params(dimension_semantics=("arbitrary",)),
    )(x, oa, ob, ga, gb, wo)


def _post_bwd(dy, oa, ob, ga, gb, wo, name):
    s = dy.shape[0]
    tm = _row_tile(s, 256)

    def body(dy_ref, oa_ref, ob_ref, ga_ref, gb_ref, wo_ref, doa_ref, dob_ref, dga_ref, dgb_ref, dwo_ref, acc_ref):
        i = pl.program_id(0)
        wo_v = wo_ref[...]
        dyv = dy_ref[...]

        def f(oa_l, ob_l, ga_v, gb_v):
            return _post_math(oa_l, ob_l, ga_v, gb_v, wo_v)

        _, vjp, mixed = jax.vjp(f, [oa_ref[hd] for hd in range(MLA_HEADS)], [ob_ref[hd] for hd in range(SWA_HEADS)],
                                ga_ref[...], gb_ref[...], has_aux=True)
        doa, dob, dga, dgb = vjp(dyv)
        for hd in range(MLA_HEADS):
            doa_ref[hd] = doa[hd]
        for hd in range(SWA_HEADS):
            dob_ref[hd] = dob[hd]
        dwo = _dot_tn(mixed.astype(BF16), dyv.astype(BF16))

        @pl.when(i == 0)
        def _():
            dga_ref[...] = dga
            dgb_ref[...] = dgb
            acc_ref[...] = dwo

        @pl.when(i > 0)
        def _():
            dga_ref[...] += dga
            dgb_ref[...] += dgb
            acc_ref[...] += dwo

        @pl.when(i == s // tm - 1)
        def _():
            dwo_ref[...] = acc_ref[...].astype(BF16)

    row = pl.BlockSpec((tm, D_MODEL), lambda i: (i, 0))
    oa_spec = pl.BlockSpec((MLA_HEADS, tm, MLA_V), lambda i: (0, i, 0))
    ob_spec = pl.BlockSpec((SWA_HEADS, tm, SWA_HEAD_DIM), lambda i: (0, i, 0))
    return _pallas_call(
        body, name=name, grid=(s // tm,),
        in_specs=[row, oa_spec, ob_spec, _full((1, MLA_WIDTH)), _full((1, SWA_WIDTH)), _resident(wo.shape)],
        out_specs=[oa_spec, ob_spec, _full((1, MLA_WIDTH)), _full((1, SWA_WIDTH)), _full(wo.shape)],
        out_shape=[jax.ShapeDtypeStruct((MLA_HEADS, s, MLA_V), F32), jax.ShapeDtypeStruct((SWA_HEADS, s, SWA_HEAD_DIM), F32),
                   jax.ShapeDtypeStruct((1, MLA_WIDTH), F32), jax.ShapeDtypeStruct((1, SWA_WIDTH), F32),
                   jax.ShapeDtypeStruct(wo.shape, BF16)],
        scratch_shapes=[pltpu.VMEM(wo.shape, F32)],
        compiler_params=_params(dimension_semantics=("arbitrary",)),
    )(dy, oa, ob, ga, gb, wo)


def _attn_tile(s):
    return 512 if s >= 2048 else 128


def _causal_mask(t):
    return lax.broadcasted_iota(jnp.int32, (t, t), 1) <= lax.broadcasted_iota(jnp.int32, (t, t), 0)


def _mla_fwd(q, k, v, name):
    nh, s, _ = q.shape
    t = _attn_tile(s)

    def body(q_ref, k_ref, v_ref, o_ref, lse_ref):
        qi = pl.program_id(1)
        qv = q_ref[...]

        def block(j, carry, masked):
            m, l, acc = carry
            start = pl.multiple_of(j * t, t)
            sc = _dot_nt(qv, k_ref[pl.ds(start, t), :]) * MLA_SCALE
            if masked:
                sc = jnp.where(_causal_mask(t), sc, NEG)
            m_new = jnp.maximum(m, jnp.max(sc, axis=-1, keepdims=True))
            alpha = jnp.exp(m - m_new)
            p = jnp.exp(sc - m_new)
            l = alpha * l + jnp.sum(p, axis=-1, keepdims=True)
            acc = alpha * acc + jnp.dot(p.astype(BF16), v_ref[pl.ds(start, t), :], preferred_element_type=F32)
            return m_new, l, acc

        init = (jnp.full((t, 1), NEG, F32), jnp.zeros((t, 1), F32), jnp.zeros((t, MLA_V), F32))
        carry = lax.fori_loop(0, qi, lambda j, c: block(j, c, False), init)
        m, l, acc = block(qi, carry, True)
        o_ref[...] = acc / l
        lse_ref[...] = m + jnp.log(l)

    return _pallas_call(
        body, name=name, grid=(nh, s // t),
        in_specs=[pl.BlockSpec((None, t, MLA_QK), lambda h, i: (h, i, 0)), pl.BlockSpec((None, s, MLA_QK), lambda h, i: (h, 0, 0)),
                  pl.BlockSpec((None, s, MLA_V), lambda h, i: (h, 0, 0))],
        out_specs=[pl.BlockSpec((None, t, MLA_V), lambda h, i: (h, i, 0)), pl.BlockSpec((None, t, 1), lambda h, i: (h, i, 0))],
        out_shape=[jax.ShapeDtypeStruct((nh, s, MLA_V), F32), jax.ShapeDtypeStruct((nh, s, 1), F32)],
        compiler_params=_params(dimension_semantics=("arbitrary", "arbitrary")),
    )(q, k, v)


def _mla_bwd_dq(q, k, v, o, do, lse, name):
    nh, s, _ = q.shape
    t = _attn_tile(s)

    def body(q_ref, k_ref, v_ref, o_ref, do_ref, lse_ref, dq_ref, delta_ref):
        qi = pl.program_id(1)
        qv = q_ref[...]
        dov = do_ref[...]
        lse = lse_ref[...]
        delta = jnp.sum(dov * o_ref[...], axis=-1, keepdims=True)
        delta_ref[...] = delta
        dob = dov.astype(BF16)

        def block(j, dq, masked):
            start = pl.multiple_of(j * t, t)
            kb = k_ref[pl.ds(start, t), :]
            sc = _dot_nt(qv, kb) * MLA_SCALE
            p = jnp.exp(sc - lse)
            if masked:
                p = jnp.where(_causal_mask(t), p, 0.0)
            dp = _dot_nt(dob, v_ref[pl.ds(start, t), :])
            ds = p * (dp - delta) * MLA_SCALE
            return dq + jnp.dot(ds.astype(BF16), kb, preferred_element_type=F32)

        dq = lax.fori_loop(0, qi, lambda j, c: block(j, c, False), jnp.zeros((t, MLA_QK), F32))
        dq_ref[...] = block(qi, dq, True)

    tile = lambda w: pl.BlockSpec((None, t, w), lambda h, i: (h, i, 0))
    whole = lambda w: pl.BlockSpec((None, s, w), lambda h, i: (h, 0, 0))
    return _pallas_call(
        body, name=name, grid=(nh, s // t),
        in_specs=[tile(MLA_QK), whole(MLA_QK), whole(MLA_V), tile(MLA_V), tile(MLA_V), tile(1)],
        out_specs=[tile(MLA_QK), tile(1)],
        out_shape=[jax.ShapeDtypeStruct((nh, s, MLA_QK), F32), jax.ShapeDtypeStruct((nh, s, 1), F32)],
        compiler_params=_params(dimension_semantics=("arbitrary", "arbitrary")),
    )(q, k, v, o, do, lse)


def _mla_bwd_dkv(q, k, v, do, lse_row, delta_row, name):
    nh, s, _ = q.shape
    t = _attn_tile(s)
    nq = s // t

    def body(q_ref, k_ref, v_ref, do_ref, lse_ref, delta_ref, dk_ref, dv_ref):
        kj = pl.program_id(1)
        kv_, vv = k_ref[...], v_ref[...]

        def block(i, carry, masked):
            dk, dv = carry
            start = pl.multiple_of(i * t, t)
            qb = q_ref[pl.ds(start, t), :]
            dob = do_ref[pl.ds(start, t), :].astype(BF16)
            sc = _dot_nt(kv_, qb) * MLA_SCALE
            p = jnp.exp(sc - lse_ref[pl.ds(i, 1), :])
            if masked:
                p = jnp.where(lax.broadcasted_iota(jnp.int32, (t, t), 0) <= lax.broadcasted_iota(jnp.int32, (t, t), 1), p, 0.0)
            dv = dv + jnp.dot(p.astype(BF16), dob, preferred_element_type=F32)
            dp = _dot_nt(vv, dob)
            ds = p * (dp - delta_ref[pl.ds(i, 1), :]) * MLA_SCALE
            dk = dk + jnp.dot(ds.astype(BF16), qb, preferred_element_type=F32)
            return dk, dv

        carry = block(kj, (jnp.zeros((t, MLA_QK), F32), jnp.zeros((t, MLA_V), F32)), True)
        dk, dv = lax.fori_loop(kj + 1, nq, lambda i, c: block(i, c, False), carry)
        dk_ref[...] = dk
        dv_ref[...] = dv

    tile = lambda w: pl.BlockSpec((None, t, w), lambda h, j: (h, j, 0))
    whole = lambda w: pl.BlockSpec((None, s, w), lambda h, j: (h, 0, 0))
    rows = pl.BlockSpec((None, nq, t), lambda h, j: (h, 0, 0))
    return _pallas_call(
        body, name=name, grid=(nh, nq),
        in_specs=[whole(MLA_QK), tile(MLA_QK), tile(MLA_V), whole(MLA_V), rows, rows],
        out_specs=[tile(MLA_QK), tile(MLA_V)],
        out_shape=[jax.ShapeDtypeStruct((nh, s, MLA_QK), F32), jax.ShapeDtypeStruct((nh, s, MLA_V), F32)],
        compiler_params=_params(dimension_semantics=("arbitrary", "arbitrary")),
    )(q, k, v, do, lse_row, delta_row)


def _swa_math(q_l, kband, vband, sink_l, first):
    q_rel = lax.broadcasted_iota(jnp.int32, (BLOCK, 2 * BLOCK), 0) + BLOCK
    k_rel = lax.broadcasted_iota(jnp.int32, (BLOCK, 2 * BLOCK), 1)
    dist = q_rel - k_rel
    valid = (dist >= 0) & (dist < BLOCK) & ((k_rel >= BLOCK) | jnp.logical_not(first))
    kb, vb = kband.astype(BF16), vband.astype(BF16)
    outs, psinks = [], []
    for qh, sink in zip(q_l, sink_l):
        sc = jnp.where(valid, _dot_nt(qh.astype(BF16), kb) * SWA_SCALE, NEG)
        m = lax.stop_gradient(jnp.maximum(jnp.max(sc, axis=-1, keepdims=True), sink))
        e = jnp.exp(sc - m)
        es = jnp.exp(sink - m)
        den = jnp.sum(e, axis=-1, keepdims=True) + es
        outs.append(jnp.dot((e / den).astype(BF16), vb, preferred_element_type=F32))
        psinks.append(es / den)
    return outs, psinks


def _swa_specs():
    grp = lambda: pl.BlockSpec((SWA_GROUP, BLOCK, SWA_HEAD_DIM), lambda j, n: (j, n, 0))
    prev = pl.BlockSpec((None, BLOCK, SWA_HEAD_DIM), lambda j, n: (j, n, 0))
    cur = pl.BlockSpec((None, BLOCK, SWA_HEAD_DIM), lambda j, n: (j, n + 1, 0))
    sink = pl.BlockSpec((None, SWA_GROUP, 128), lambda j, n: (j, 0, 0))
    return grp, prev, cur, sink


def _swa_fwd(q, kpad, vpad, sinks, name):
    _, s, _ = q.shape
    grp, prev, cur, sink = _swa_specs()

    def body(q_ref, kp_ref, kc_ref, vp_ref, vc_ref, sink_ref, o_ref):
        first = pl.program_id(1) == 0
        kband = jnp.concatenate([kp_ref[...], kc_ref[...]], axis=0)
        vband = jnp.concatenate([vp_ref[...], vc_ref[...]], axis=0)
        sk = sink_ref[...]
        outs, _ = _swa_math([q_ref[g] for g in range(SWA_GROUP)], kband, vband,
                            [sk[g:g + 1, 0:1] for g in range(SWA_GROUP)], first)
        for g in range(SWA_GROUP):
            o_ref[g] = outs[g]

    return _pallas_call(
        body, name=name, grid=(SWA_KV_HEADS, s // BLOCK),
        in_specs=[grp(), prev, cur, prev, cur, sink], out_specs=grp(),
        out_shape=jax.ShapeDtypeStruct((SWA_HEADS, s, SWA_HEAD_DIM), F32),
        compiler_params=_params(dimension_semantics=("arbitrary", "arbitrary")),
    )(q, kpad, kpad, vpad, vpad, sinks)


def _swa_bwd(q, kpad, vpad, sinks, do, name):
    _, s, _ = q.shape
    grp, prev, cur, sink = _swa_specs()

    def body(q_ref, kp_ref, kc_ref, vp_ref, vc_ref, sink_ref, do_ref, dq_ref, dk_ref, dv_ref, dsink_ref):
        n = pl.program_id(1)
        first = n == 0
        sk = sink_ref[...]
        sink_l = [sk[g:g + 1, 0:1] for g in range(SWA_GROUP)]
        kband = jnp.concatenate([kp_ref[...], kc_ref[...]], axis=0).astype(F32)
        vband = jnp.concatenate([vp_ref[...], vc_ref[...]], axis=0).astype(F32)
        q_l = [q_ref[g].astype(F32) for g in range(SWA_GROUP)]

        def f(ql, kb, vb):
            return _swa_math(ql, kb, vb, sink_l, first)

        outs, vjp, psinks = jax.vjp(f, q_l, kband, vband, has_aux=True)
        do_l = [do_ref[g] for g in range(SWA_GROUP)]
        dq_l, dkb, dvb = vjp(do_l)
        for g in range(SWA_GROUP):
            dq_ref[g] = dq_l[g]
        dsk = jnp.concatenate(
            [jnp.broadcast_to(-jnp.sum(psinks[g] * jnp.sum(do_l[g] * outs[g], axis=-1, keepdims=True), axis=0, keepdims=True),
                              (1, 128)) for g in range(SWA_GROUP)] + [jnp.zeros((8 - SWA_GROUP, 128), F32)], axis=0)

        @pl.when(first)
        def _():
            dk_ref[...] = jnp.zeros_like(dk_ref)
            dv_ref[...] = jnp.zeros_like(dv_ref)
            dsink_ref[...] = jnp.zeros_like(dsink_ref)

        rows = pl.ds(pl.multiple_of(n * BLOCK, BLOCK), 2 * BLOCK)
        dk_ref[rows, :] += dkb
        dv_ref[rows, :] += dvb
        dsink_ref[...] += dsk

    acc = pl.BlockSpec((None, s + BLOCK, SWA_HEAD_DIM), lambda j, n: (j, 0, 0))
    return _pallas_call(
        body, name=name, grid=(SWA_KV_HEADS, s // BLOCK),
        in_specs=[grp(), prev, cur, prev, cur, sink, grp()],
        out_specs=[grp(), acc, acc, pl.BlockSpec((None, 8, 128), lambda j, n: (j, 0, 0))],
        out_shape=[jax.ShapeDtypeStruct((SWA_HEADS, s, SWA_HEAD_DIM), F32),
                   jax.ShapeDtypeStruct((SWA_KV_HEADS, s + BLOCK, SWA_HEAD_DIM), F32),
                   jax.ShapeDtypeStruct((SWA_KV_HEADS, s + BLOCK, SWA_HEAD_DIM), F32),
                   jax.ShapeDtypeStruct((SWA_KV_HEADS, 8, 128), F32)],
        compiler_params=_params(dimension_semantics=("arbitrary", "arbitrary")),
    )(q, kpad, kpad, vpad, vpad, sinks, do)


def _loss_head(y, target, name):
    s = y.shape[0]
    tm = _row_tile(s, 512)

    def body(y_ref, t_ref, dy_ref, loss_ref):
        i = pl.program_id(0)
        err = y_ref[...] - t_ref[...]
        dy_ref[...] = err * (1.0 / D_MODEL)
        part = jnp.broadcast_to(0.5 * jnp.sum(jnp.mean(err * err, axis=-1, keepdims=True), axis=0, keepdims=True), (1, 128))

        @pl.when(i == 0)
        def _():
            loss_ref[...] = part

        @pl.when(i > 0)
        def _():
            loss_ref[...] += part

    row = pl.BlockSpec((tm, D_MODEL), lambda i: (i, 0))
    return _pallas_call(
        body, name=name, grid=(s // tm,), in_specs=[row, row], out_specs=[row, _full((1, 128))],
        out_shape=[jax.ShapeDtypeStruct((s, D_MODEL), F32), jax.ShapeDtypeStruct((1, 128), F32)],
        compiler_params=_params(dimension_semantics=("arbitrary",)),
    )(y, target)


def _adamw(w, g, m, v, name):
    rows, cols = w.shape
    tr = rows
    for cand in (512, 256, 128, 64, 32, 16, 8):
        if rows % cand == 0 and rows > cand:
            tr = cand
            break

    def body(w_ref, g_ref, m_ref, v_ref, d_ref, nm_ref, nv_ref):
        gv = g_ref[...]
        nm = ADAM_B1 * m_ref[...] + (1.0 - ADAM_B1) * gv
        nv = ADAM_B2 * v_ref[...] + (1.0 - ADAM_B2) * (gv * gv)
        m_hat = nm / (1.0 - ADAM_B1 ** ADAM_STEP)
        v_hat = nv / (1.0 - ADAM_B2 ** ADAM_STEP)
        d_ref[...] = -ADAM_LR * (m_hat / (jnp.sqrt(v_hat) + ADAM_EPS) + ADAM_WD * w_ref[...])
        nm_ref[...] = nm
        nv_ref[...] = nv

    blk = pl.BlockSpec((tr, cols), lambda i: (i, 0))
    return _pallas_call(
        body, name=name, grid=(rows // tr,), in_specs=[blk] * 4, out_specs=[blk] * 3,
        out_shape=[jax.ShapeDtypeStruct((rows, cols), F32)] * 3,
        compiler_params=_params(dimension_semantics=("arbitrary",)),
    )(w, g, m, v)


def _position():
    return lax.axis_index("x"), lax.axis_index("y"), lax.axis_index("c")


def _remote(src, dst, send_sems, recv_sems, k, to):
    return pltpu.make_async_remote_copy(src_ref=src, dst_ref=dst, send_sem=send_sems.at[k], recv_sem=recv_sems.at[k],
                                        device_id=to, device_id_type=MESH)


_HBM = pl.BlockSpec(memory_space=pltpu.HBM)


def _all_gather_layers(mine, name):
    na = len(mine)

    def body(*refs):
        ins, outs = refs[:na], refs[na:2 * na]
        send_sems, recv_sems, local_sems = refs[2 * na:]
        x, y, c = _position()
        me, sibling = (x, y, c), (x, y, 1 - c)
        chips = [(1 - x, y), (x, 1 - y), (1 - x, 1 - y)]

        def slot(a, px, py, pc):
            return outs[a].at[4 * px + 2 * py + pc]

        def cp(a, k, block, to, src=None):
            return _remote(slot(a, *block) if src is None else src, slot(a, *block), send_sems, recv_sems, 7 * a + k, to)

        own = [pltpu.make_async_copy(ins[a].at[c], slot(a, *me), local_sems.at[a]) for a in range(na)]
        for o in own:
            o.start()
        first = []
        for a in range(na):
            first.append(cp(a, 0, me, sibling, src=ins[a].at[c]))
            first += [cp(a, 1 + j, me, (*chip, c), src=ins[a].at[c]) for j, chip in enumerate(chips)]
        for f in first:
            f.start()
        passed = []
        for j, chip in enumerate(chips):
            for a in range(na):
                cp(a, 1 + j, (*chip, c), me).wait_recv()
                fwd = cp(a, 4 + j, (*chip, c), sibling)
                fwd.start()
                passed.append(fwd)
        for a in range(na):
            cp(a, 0, sibling, me).wait_recv()
            for j, chip in enumerate(chips):
                cp(a, 4 + j, (*chip, 1 - c), me).wait_recv()
        for f in first + passed:
            f.wait_send()
        for o in own:
            o.wait()

    return _pallas_call(
        body, name=name, in_specs=[_HBM] * na, out_specs=[_HBM] * na,
        out_shape=[jax.ShapeDtypeStruct((N_DEV,) + m.shape[1:], m.dtype) for m in mine],
        scratch_shapes=[pltpu.SemaphoreType.DMA((7 * na,)), pltpu.SemaphoreType.DMA((7 * na,)), pltpu.SemaphoreType.DMA((na,))],
    )(*mine)


def _sibling_swap_layers(parts, name):
    na = len(parts)

    def body(*refs):
        ins, outs, (send_sems, recv_sems) = refs[:na], refs[na:2 * na], refs[2 * na:]
        x, y, c = _position()
        copies = [_remote(ins[a].at[1 - c], outs[a], send_sems, recv_sems, a, (x, y, 1 - c)) for a in range(na)]
        for cp in copies:
            cp.start()
        for cp in copies:
            cp.wait()

    return _pallas_call(
        body, name=name, in_specs=[_HBM] * na, out_specs=[_HBM] * na,
        out_shape=[jax.ShapeDtypeStruct(p.shape[1:], p.dtype) for p in parts],
        scratch_shapes=[pltpu.SemaphoreType.DMA((na,)), pltpu.SemaphoreType.DMA((na,))],
    )(*parts)


def _scatter_to_chips(parts, name):
    na = len(parts)

    def body(*refs):
        ins, outs, (send_sems, recv_sems) = refs[:na], refs[na:2 * na], refs[2 * na:]
        x, y, c = _position()
        chips = [(1 - x, y), (x, 1 - y), (1 - x, 1 - y)]
        copies = [_remote(ins[a].at[2 * px + py], outs[a].at[j], send_sems, recv_sems, 3 * a + j, (px, py, c))
                  for a in range(na) for j, (px, py) in enumerate(chips)]
        for cp in copies:
            cp.start()
        for cp in copies:
            cp.wait()

    return _pallas_call(
        body, name=name, in_specs=[_HBM] * na, out_specs=[_HBM] * na,
        out_shape=[jax.ShapeDtypeStruct((3,) + p.shape[1:], p.dtype) for p in parts],
        scratch_shapes=[pltpu.SemaphoreType.DMA((3 * na,)), pltpu.SemaphoreType.DMA((3 * na,))],
    )(*parts)


def _sibling_share(halves, name):
    na = len(halves)

    def body(*refs):
        ins, outs = refs[:na], refs[na:2 * na]
        send_sems, recv_sems, local_sems = refs[2 * na:]
        x, y, c = _position()
        own = [pltpu.make_async_copy(ins[a], outs[a].at[c], local_sems.at[a]) for a in range(na)]
        sends = [_remote(ins[a], outs[a].at[c], send_sems, recv_sems, a, (x, y, 1 - c)) for a in range(na)]
        for cp in own + sends:
            cp.start()
        for a in range(na):
            _remote(outs[a].at[1 - c], outs[a].at[1 - c], send_sems, recv_sems, a, (x, y, c)).wait_recv()
        for cp in sends:
            cp.wait_send()
        for cp in own:
            cp.wait()

    return _pallas_call(
        body, name=name, in_specs=[_HBM] * na, out_specs=[_HBM] * na,
        out_shape=[jax.ShapeDtypeStruct((2,) + h.shape, h.dtype) for h in halves],
        scratch_shapes=[pltpu.SemaphoreType.DMA((na,)), pltpu.SemaphoreType.DMA((na,)), pltpu.SemaphoreType.DMA((na,))],
    )(*halves)


def _assemble_cols(g4, name):
    _, nl, r, w = g4.shape
    tr = _row_tile(r, 256)

    def body(in_ref, out_ref):
        out_ref[...] = jnp.concatenate([in_ref[sh] for sh in range(N_SHARD)], axis=-1)

    return _pallas_call(
        body, name=name, grid=(nl, r // tr),
        in_specs=[pl.BlockSpec((N_SHARD, None, tr, w), lambda l, i: (0, l, i, 0))],
        out_specs=pl.BlockSpec((None, tr, N_SHARD * w), lambda l, i: (l, i, 0)),
        out_shape=jax.ShapeDtypeStruct((nl, r, N_SHARD * w), g4.dtype),
        compiler_params=_params(dimension_semantics=("arbitrary", "arbitrary")),
    )(g4)


def _all_reduce_small(vec, name):
    r, l = vec.shape

    def body(v_ref, out_ref, gath_ref, send_sems, recv_sems):
        x, y, c = _position()
        me = 4 * x + 2 * y + c
        gath_ref[me] = v_ref[...]
        copies = []
        for k in range(1, N_DEV):
            to = (x ^ (k >> 2), y ^ ((k >> 1) & 1), c ^ (k & 1))
            copies.append(_remote(gath_ref.at[me], gath_ref.at[me], send_sems, recv_sems, k - 1, to))
        for cp in copies:
            cp.start()
        for k in range(1, N_DEV):
            frm = 4 * (x ^ (k >> 2)) + 2 * (y ^ ((k >> 1) & 1)) + (c ^ (k & 1))
            _remote(gath_ref.at[frm], gath_ref.at[frm], send_sems, recv_sems, k - 1, (x, y, c)).wait_recv()
        for cp in copies:
            cp.wait_send()
        total = gath_ref[0]
        for d in range(1, N_DEV):
            total = total + gath_ref[d]
        out_ref[...] = total

    vm = pl.BlockSpec(memory_space=pltpu.VMEM)
    return _pallas_call(
        body, name=name, in_specs=[vm], out_specs=vm, out_shape=jax.ShapeDtypeStruct((r, l), F32),
        scratch_shapes=[pltpu.VMEM((N_DEV, r, l), F32), pltpu.SemaphoreType.DMA((N_DEV - 1,)),
                        pltpu.SemaphoreType.DMA((N_DEV - 1,))],
    )(vec)


def _sum_blocks(blocks, out_dtype, name):
    m, w = blocks[0].shape
    tr = next(cand for cand in (512, 256, 128) if m % cand == 0)

    def body(*refs):
        total = refs[0][...].astype(F32)
        for ref in refs[1:-1]:
            total = total + ref[...].astype(F32)
        refs[-1][...] = total.astype(out_dtype)

    blk = pl.BlockSpec((tr, w), lambda i: (i, 0))
    return _pallas_call(
        body, name=name, grid=(m // tr,), in_specs=[blk] * len(blocks), out_specs=blk,
        out_shape=jax.ShapeDtypeStruct((m, w), out_dtype),
        compiler_params=_params(dimension_semantics=("arbitrary",)),
    )(*blocks)


GROUPS = (('ffn1_w_gate', 'ffn1_w_up', 'ffn2_w_gate', 'ffn2_w_up'), ('ffn1_w_down', 'ffn2_w_down', 'w_o'),
          ('w_in',), ('mla_w_q_b',), ('mla_w_kv_b',))


def _shard_rows(name):
    shape, axis = BIG[name]
    return shape[0] // N_SHARD if axis == 0 else shape[0]


def _group_row_offsets(group):
    return [int(v) for v in np.cumsum([0] + [_shard_rows(n) for n in group])]


def _rope_tables(s):
    pos = jnp.arange(s, dtype=F32)
    inv = 1.0 / (ROPE_THETA ** (jnp.arange(0, MLA_ROPE, 2, dtype=F32) / MLA_ROPE))
    ang = pos[:, None] * inv[None, :]
    return jnp.cos(ang), jnp.sin(ang)


def _local_step(x, target, small, big):
    s = x.shape[0]
    cos, sin = _rope_tables(s)
    row = lambda name, l: small[name][l][None, :]
    saved = []
    for l in range(DEPTH):
        sv = {'x0': x}
        x, sv['g1'], sv['u1'] = _ffn_fwd(x, row('ffn1_norm', l), big['ffn1_w_gate'][l], big['ffn1_w_up'][l],
                                        big['ffn1_w_down'][l], f"ffn1_fwd_{l}")
        sv['x1'] = x
        gains = [row(n, l) for n in ('mix_norm', 'mla_q_a_norm', 'mla_kv_a_norm', 'mla_q_norm', 'mla_k_norm',
                                     'swa_q_norm', 'swa_k_norm')]
        q_a, k_a, v_a, q_b, k_b, v_b = _pre_fwd(x, gains, big['w_in'][l], big['mla_w_q_b'][l], big['mla_w_kv_b'][l],
                                                cos, sin, f"pre_fwd_{l}")
        o_a, lse = _mla_fwd(q_a, k_a, v_a, f"mla_fwd_{l}")
        kpad = jnp.pad(k_b, ((0, 0), (BLOCK, 0), (0, 0)))
        vpad = jnp.pad(v_b, ((0, 0), (BLOCK, 0), (0, 0)))
        sinks = jnp.broadcast_to(small['swa_sinks'][l].reshape(SWA_KV_HEADS, SWA_GROUP, 1), (SWA_KV_HEADS, SWA_GROUP, 128))
        o_b = _swa_fwd(q_b, kpad, vpad, sinks, f"swa_fwd_{l}")
        sv.update(gains=gains, q_a=q_a, k_a=k_a, v_a=v_a, q_b=q_b, kpad=kpad, vpad=vpad, sinks=sinks, o_a=o_a, lse=lse, o_b=o_b)
        x = _post_fwd(x, o_a, o_b, row('mla_out_norm', l), row('swa_out_norm', l), big['w_o'][l], f"post_fwd_{l}")
        sv['x2'] = x
        x, sv['g2'], sv['u2'] = _ffn_fwd(x, row('ffn2_norm', l), big['ffn2_w_gate'][l], big['ffn2_w_up'][l],
                                        big['ffn2_w_down'][l], f"ffn2_fwd_{l}")
        saved.append(sv)

    dx, loss = _loss_head(x, target, "loss_head")

    gs = {n: [None] * DEPTH for n in SMALL_NAMES}
    gb = {n: [None] * DEPTH for n in BIG_NAMES}
    t = _attn_tile(s)
    for l in reversed(range(DEPTH)):
        sv = saved[l]

        def ffn_back(tag, xin, dy, gate, up):
            dxi, dgain, nb, act, dgate, dup = _ffn_bwd(xin, dy, gate, up, row(tag + '_norm', l), big[tag + '_w_gate'][l],
                                                      big[tag + '_w_up'][l], big[tag + '_w_down'][l], f"{tag}_bwd_{l}")
            gs[tag + '_norm'][l] = dgain[0]
            gb[tag + '_w_gate'][l] = _matmul_tn(nb, dgate, 1.0, f"{tag}_dwg_{l}", True)
            gb[tag + '_w_up'][l] = _matmul_tn(nb, dup, 1.0, f"{tag}_dwu_{l}", True)
            gb[tag + '_w_down'][l] = _matmul_tn(act, dy, 0.5, f"{tag}_dwd_{l}", False).reshape(N_SHARD, D_FF // N_SHARD, D_MODEL)
            return dxi

        dx = ffn_back('ffn2', sv['x2'], dx, sv['g2'], sv['u2'])
        do_a, do_b, dga, dgb, dwo = _post_bwd(dx, sv['o_a'], sv['o_b'], row('mla_out_norm', l), row('swa_out_norm', l),
                                              big['w_o'][l], f"post_bwd_{l}")
        gs['mla_out_norm'][l], gs['swa_out_norm'][l] = dga[0], dgb[0]
        gb['w_o'][l] = dwo.reshape(N_SHARD, MIX_WIDTH // N_SHARD, D_MODEL)
        dq_a, delta = _mla_bwd_dq(sv['q_a'], sv['k_a'], sv['v_a'], sv['o_a'], do_a, sv['lse'], f"mla_dq_{l}")
        dk_a, dv_a = _mla_bwd_dkv(sv['q_a'], sv['k_a'], sv['v_a'], do_a, sv['lse'].reshape(MLA_HEADS, s // t, t),
                                  delta.reshape(MLA_HEADS, s // t, t), f"mla_dkv_{l}")
        dq_b, dkpad, dvpad, dsink = _swa_bwd(sv['q_b'], sv['kpad'], sv['vpad'], sv['sinks'], do_b, f"swa_bwd_{l}")
        gs['swa_sinks'][l] = dsink[:, :SWA_GROUP, 0].reshape(SWA_HEADS)
        cts = [dq_a, dk_a, dv_a, dq_b, dkpad[:, BLOCK:], dvpad[:, BLOCK:]]
        outs = _pre_bwd(sv['x1'], dx, cts, sv['gains'], big['w_in'][l], big['mla_w_q_b'][l], big['mla_w_kv_b'][l],
                        cos, sin, f"pre_bwd_{l}")
        dx = outs[0]
        for n, val in zip(('mix_norm', 'mla_q_a_norm', 'mla_kv_a_norm', 'mla_q_norm', 'mla_k_norm', 'swa_q_norm', 'swa_k_norm'),
                          outs[1:8]):
            gs[n][l] = val[0]
        gb['w_in'][l], gb['mla_w_q_b'][l], gb['mla_w_kv_b'][l] = outs[8:11]
        dx = ffn_back('ffn1', sv['x0'], dx, sv['g1'], sv['u1'])
    return loss, dx, gs, gb


def kernel(x, ffn1_norm, ffn1_w_gate, ffn1_w_up, ffn1_w_down, mix_norm, w_in, mla_q_a_norm, mla_w_q_b, mla_kv_a_norm, mla_w_kv_b, mla_q_norm, mla_k_norm, swa_q_norm, swa_k_norm, swa_sinks, mla_out_norm, swa_out_norm, w_o, ffn2_norm, ffn2_w_gate, ffn2_w_up, ffn2_w_down, loss_target, m_ffn1_norm, m_ffn1_w_gate, m_ffn1_w_up, m_ffn1_w_down, m_mix_norm, m_w_in, m_mla_q_a_norm, m_mla_w_q_b, m_mla_kv_a_norm, m_mla_w_kv_b, m_mla_q_norm, m_mla_k_norm, m_swa_q_norm, m_swa_k_norm, m_swa_sinks, m_mla_out_norm, m_swa_out_norm, m_w_o, m_ffn2_norm, m_ffn2_w_gate, m_ffn2_w_up, m_ffn2_w_down, v_ffn1_norm, v_ffn1_w_gate, v_ffn1_w_up, v_ffn1_w_down, v_mix_norm, v_w_in, v_mla_q_a_norm, v_mla_w_q_b, v_mla_kv_a_norm, v_mla_w_kv_b, v_mla_q_norm, v_mla_k_norm, v_swa_q_norm, v_swa_k_norm, v_swa_sinks, v_mla_out_norm, v_swa_out_norm, v_w_o, v_ffn2_norm, v_ffn2_w_gate, v_ffn2_w_up, v_ffn2_w_down):
    args = dict(locals())
    weights = {n: args[n] for n in WEIGHT_NAMES}
    mom_m = {n: args["m_" + n] for n in WEIGHT_NAMES}
    mom_v = {n: args["v_" + n] for n in WEIGHT_NAMES}

    mine = [jnp.concatenate([weights[n].astype(BF16) for n in group], axis=1) for group in GROUPS]
    gathered = [g.reshape((N_SHARD, DEPTH) + g.shape[1:]) for g in _all_gather_layers(mine, "gather_weights")]
    big = {}
    for gi, group in enumerate(GROUPS):
        offs = _group_row_offsets(group)
        col_sharded = BIG[group[0]][1] == 1
        full = _assemble_cols(gathered[gi], f"assemble_{gi}") if col_sharded else None
        for i, n in enumerate(group):
            if col_sharded:
                big[n] = [full[l, offs[i]:offs[i + 1]] for l in range(DEPTH)]
            else:
                big[n] = [gathered[gi][:, l, offs[i]:offs[i + 1]].reshape(BIG[n][0]) for l in range(DEPTH)]
    small = {n: weights[n] for n in SMALL_NAMES}

    loss, dx, gs, gb = _local_step(x[0], loss_target[0], small, big)

    small_flat = jnp.concatenate([jnp.stack(gs[n]).reshape(-1) for n in SMALL_NAMES] + [loss[0, :1]])
    n_small = small_flat.shape[0]
    lanes = -(-n_small // (8 * 128)) * 128
    small_sum = _all_reduce_small(jnp.pad(small_flat, (0, 8 * lanes - n_small)).reshape(8, lanes), "reduce_small").reshape(-1)
    grads = {}
    off = 0
    for n in SMALL_NAMES:
        cnt = int(np.prod(weights[n].shape))
        grads[n] = small_sum[off:off + cnt].reshape(weights[n].shape)
        off += cnt
    loss_out = small_sum[off]

    c = lax.axis_index("c")
    chip = 2 * lax.axis_index("x") + lax.axis_index("y")
    parts = [jnp.stack([jnp.concatenate([gb[n][l] for n in group], axis=1) for l in range(DEPTH)]) for group in GROUPS]
    from_sibling = _sibling_swap_layers(parts, "swap_layers")
    chip_sums = []
    for gi, (p, got) in enumerate(zip(parts, from_sibling)):
        kept = lax.dynamic_index_in_dim(p, c, axis=0, keepdims=False)
        rows = N_SHARD * p.shape[2]
        pair = _sum_blocks([kept.reshape(rows, -1), got.reshape(rows, -1)], BF16, f"sum_pair_{gi}")
        chip_sums.append(pair.reshape(p.shape[1:]))
    from_chips = _scatter_to_chips(chip_sums, "scatter_chips")
    halves = []
    for gi, (cs, got) in enumerate(zip(chip_sums, from_chips)):
        own = lax.dynamic_index_in_dim(cs, chip, axis=0, keepdims=False)
        halves.append(_sum_blocks([own, got[0], got[1], got[2]], F32, f"sum_chips_{gi}"))
    reduced = _sibling_share(halves, "share_layers")
    for gi, group in enumerate(GROUPS):
        offs = _group_row_offsets(group)
        for i, n in enumerate(group):
            grads[n] = reduced[gi][:, offs[i]:offs[i + 1]]

    deltas, new_m, new_v = {}, {}, {}
    for n in WEIGHT_NAMES:
        shp = weights[n].shape
        two_d = (DEPTH, shp[-1]) if len(shp) == 2 else (shp[0] * shp[1], shp[2])
        d, nm, nv = _adamw(weights[n].reshape(two_d), grads[n].reshape(two_d), mom_m[n].reshape(two_d),
                           mom_v[n].reshape(two_d), f"adamw_{n}")
        deltas[n], new_m[n], new_v[n] = d.reshape(shp), nm.reshape(shp), nv.reshape(shp)

    return (loss_out, dx[None], *[grads[n] for n in WEIGHT_NAMES], *[deltas[n] for n in WEIGHT_NAMES],
            *[new_m[n] for n in WEIGHT_NAMES], *[new_v[n] for n in WEIGHT_NAMES])
```

```python
import functools

import numpy as np
import jax
import jax.numpy as jnp
from jax import lax
from jax.experimental import pallas as pl
from jax.experimental.pallas import tpu as pltpu

F32 = jnp.float32
BF16 = jnp.bfloat16

D_MODEL = 1024
DEPTH = 2
EPS = 1e-6
ROPE_THETA = 10000.0
BLOCK = 128
MLA_HEADS = 4
MLA_Q_RANK = 256
MLA_KV_RANK = 128
MLA_NOPE = 128
MLA_ROPE = 64
MLA_V = 128
MLA_QK = MLA_NOPE + MLA_ROPE
MLA_WIDTH = MLA_HEADS * MLA_V
SWA_HEADS = 8
SWA_KV_HEADS = 2
SWA_GROUP = SWA_HEADS // SWA_KV_HEADS
SWA_HEAD_DIM = 64
SWA_WIDTH = SWA_HEADS * SWA_HEAD_DIM
MIX_WIDTH = MLA_WIDTH + SWA_WIDTH
IN_SPLITS = (MLA_Q_RANK, MLA_KV_RANK, MLA_ROPE, SWA_WIDTH, SWA_KV_HEADS * SWA_HEAD_DIM, SWA_KV_HEADS * SWA_HEAD_DIM)
IN_COLS = sum(IN_SPLITS)
IN_OFFS = tuple(int(v) for v in np.cumsum((0,) + IN_SPLITS))
D_FF = 2816
MLA_SCALE = MLA_QK ** -0.5
SWA_SCALE = SWA_HEAD_DIM ** -0.5
NEG = -1e30

ADAM_LR = 0.001
ADAM_B1 = 0.9
ADAM_B2 = 0.999
ADAM_EPS = 1e-08
ADAM_WD = 0.01
ADAM_STEP = 10

N_SHARD = 4
N_DEV = 8
VMEM_LIMIT = 56 * 1024 * 1024
MESH = pl.DeviceIdType.MESH

WEIGHT_NAMES = ['ffn1_norm', 'ffn1_w_gate', 'ffn1_w_up', 'ffn1_w_down', 'mix_norm', 'w_in', 'mla_q_a_norm', 'mla_w_q_b',
                'mla_kv_a_norm', 'mla_w_kv_b', 'mla_q_norm', 'mla_k_norm', 'swa_q_norm', 'swa_k_norm', 'swa_sinks',
                'mla_out_norm', 'swa_out_norm', 'w_o', 'ffn2_norm', 'ffn2_w_gate', 'ffn2_w_up', 'ffn2_w_down']
BIG = {'ffn1_w_gate': ((D_MODEL, D_FF), 1), 'ffn1_w_up': ((D_MODEL, D_FF), 1), 'ffn1_w_down': ((D_FF, D_MODEL), 0),
       'w_in': ((D_MODEL, IN_COLS), 1), 'mla_w_q_b': ((MLA_Q_RANK, MLA_HEADS * MLA_QK), 1),
       'mla_w_kv_b': ((MLA_KV_RANK, MLA_HEADS * (MLA_NOPE + MLA_V)), 1), 'w_o': ((MIX_WIDTH, D_MODEL), 0),
       'ffn2_w_gate': ((D_MODEL, D_FF), 1), 'ffn2_w_up': ((D_MODEL, D_FF), 1), 'ffn2_w_down': ((D_FF, D_MODEL), 0)}
BIG_NAMES = [n for n in WEIGHT_NAMES if n in BIG]
SMALL_NAMES = [n for n in WEIGHT_NAMES if n not in BIG]

_pallas_call = pl.pallas_call


def _params(**kw):
    return pltpu.CompilerParams(vmem_limit_bytes=VMEM_LIMIT, **kw)


def _full(shape):
    n = len(shape)
    return pl.BlockSpec(shape, lambda *_: (0,) * n)


def _resident(shape):
    n = len(shape)
    return pl.BlockSpec(shape, lambda *_: (0,) * n, pipeline_mode=pl.Buffered(1))


@jax.custom_vjp
def _mm(a, w):
    return jnp.dot(a.astype(BF16), w, preferred_element_type=F32)


def _mm_fwd(a, w):
    return _mm(a, w), w


def _mm_bwd(w, dy):
    return lax.dot_general(dy.astype(BF16), w, (((1,), (1,)), ((), ())), preferred_element_type=F32), None


_mm.defvjp(_mm_fwd, _mm_bwd)


def _dot_nt(a, b):
    return lax.dot_general(a, b, (((1,), (1,)), ((), ())), preferred_element_type=F32)


def _dot_tn(a, b):
    return lax.dot_general(a, b, (((0,), (0,)), ((), ())), preferred_element_type=F32)


def _rms(t, g):
    return t * lax.rsqrt(jnp.mean(t * t, axis=-1, keepdims=True) + EPS) * g


def _rope(t, cos, sin):
    half = t.shape[-1] // 2
    t1, t2 = t[:, :half], t[:, half:]
    return jnp.concatenate([t1 * cos - t2 * sin, t2 * cos + t1 * sin], axis=-1)


def _sigmoid(z):
    return 1.0 / (1.0 + jnp.exp(-z))


def _row_tile(s, want):
    return min(want, s)


FF_CHUNK = 1408


def _ffn_fwd(x, g, wg, wu, wd, name):
    s = x.shape[0]
    tm = _row_tile(s, 256)

    def body(x_ref, g_ref, wg_ref, wu_ref, wd_ref, y_ref, gate_ref, up_ref):
        xv = x_ref[...]
        nb = _rms(xv, g_ref[...]).astype(BF16)
        acc = xv
        for c in range(0, D_FF, FF_CHUNK):
            gate = jnp.dot(nb, wg_ref[:, c:c + FF_CHUNK], preferred_element_type=F32)
            up = jnp.dot(nb, wu_ref[:, c:c + FF_CHUNK], preferred_element_type=F32)
            gate_ref[:, c:c + FF_CHUNK] = gate.astype(BF16)
            up_ref[:, c:c + FF_CHUNK] = up.astype(BF16)
            act = (gate * _sigmoid(gate) * up).astype(BF16)
            acc = acc + 0.5 * jnp.dot(act, wd_ref[c:c + FF_CHUNK, :], preferred_element_type=F32)
        y_ref[...] = acc

    return _pallas_call(
        body, name=name, grid=(s // tm,),
        in_specs=[pl.BlockSpec((tm, D_MODEL), lambda i: (i, 0)), _full((1, D_MODEL)),
                  _resident((D_MODEL, D_FF)), _resident((D_MODEL, D_FF)), _resident((D_FF, D_MODEL))],
        out_specs=[pl.BlockSpec((tm, D_MODEL), lambda i: (i, 0)), pl.BlockSpec((tm, D_FF), lambda i: (i, 0)),
                   pl.BlockSpec((tm, D_FF), lambda i: (i, 0))],
        out_shape=[jax.ShapeDtypeStruct((s, D_MODEL), F32), jax.ShapeDtypeStruct((s, D_FF), BF16),
                   jax.ShapeDtypeStruct((s, D_FF), BF16)],
        compiler_params=_params(dimension_semantics=("arbitrary",)),
    )(x, g, wg, wu, wd)


def _ffn_bwd(x, dy, gate, up, g, wg, wu, wd, name):
    s = x.shape[0]
    tm = _row_tile(s, 256)

    def body(x_ref, dy_ref, gate_ref, up_ref, g_ref, wg_ref, wu_ref, wd_ref,
             dx_ref, dgain_ref, n_ref, act_ref, dgate_ref, dup_ref):
        i = pl.program_id(0)
        xv = x_ref[...]
        dyv = dy_ref[...]
        gv = g_ref[...]
        r = lax.rsqrt(jnp.mean(xv * xv, axis=-1, keepdims=True) + EPS)
        xh = xv * r
        n_ref[...] = (xh * gv).astype(BF16)
        dyh = (0.5 * dyv).astype(BF16)
        dn = jnp.zeros_like(xv)
        for c in range(0, D_FF, FF_CHUNK):
            dact = _dot_nt(dyh, wd_ref[c:c + FF_CHUNK, :])
            gt = gate_ref[:, c:c + FF_CHUNK].astype(F32)
            u = up_ref[:, c:c + FF_CHUNK].astype(F32)
            sg = _sigmoid(gt)
            sl = gt * sg
            act_ref[:, c:c + FF_CHUNK] = (sl * u).astype(BF16)
            dup = (dact * sl).astype(BF16)
            dgate = (dact * u * (sg * (1.0 + gt * (1.0 - sg)))).astype(BF16)
            dup_ref[:, c:c + FF_CHUNK] = dup
            dgate_ref[:, c:c + FF_CHUNK] = dgate
            dn = dn + _dot_nt(dgate, wg_ref[:, c:c + FF_CHUNK]) + _dot_nt(dup, wu_ref[:, c:c + FF_CHUNK])
        part = jnp.sum(dn * xh, axis=0, keepdims=True)

        @pl.when(i == 0)
        def _():
            dgain_ref[...] = part

        @pl.when(i > 0)
        def _():
            dgain_ref[...] += part

        dxh = dn * gv
        dx_ref[...] = dyv + r * (dxh - xh * jnp.mean(dxh * xh, axis=-1, keepdims=True))

    row = lambda w: pl.BlockSpec((tm, w), lambda i: (i, 0))
    return _pallas_call(
        body, name=name, grid=(s // tm,),
        in_specs=[row(D_MODEL), row(D_MODEL), row(D_FF), row(D_FF), _full((1, D_MODEL)),
                  _resident((D_MODEL, D_FF)), _resident((D_MODEL, D_FF)), _resident((D_FF, D_MODEL))],
        out_specs=[row(D_MODEL), _full((1, D_MODEL)), row(D_MODEL), row(D_FF), row(D_FF), row(D_FF)],
        out_shape=[jax.ShapeDtypeStruct((s, D_MODEL), F32), jax.ShapeDtypeStruct((1, D_MODEL), F32),
                   jax.ShapeDtypeStruct((s, D_MODEL), BF16), jax.ShapeDtypeStruct((s, D_FF), BF16),
                   jax.ShapeDtypeStruct((s, D_FF), BF16), jax.ShapeDtypeStruct((s, D_FF), BF16)],
        compiler_params=_params(dimension_semantics=("arbitrary",)),
    )(x, dy, gate, up, g, wg, wu, wd)


def _store_col_shards(o_ref, acc, first_shard, n_here, width):
    for q in range(n_here):
        o_ref[q] = acc[:, (first_shard + q) * width:(first_shard + q + 1) * width].astype(BF16)


def _matmul_tn(a, b, scale, name, col_shards):
    t, m = a.shape
    n = b.shape[1]
    tk = _row_tile(t, 512)
    tn = n // 2
    per = n // N_SHARD
    nk = t // tk

    def body(a_ref, b_ref, o_ref, acc_ref):
        k = pl.program_id(1)
        bv = b_ref[...]
        if scale != 1.0:
            bv = bv.astype(F32) * scale
        part = _dot_tn(a_ref[...].astype(BF16), bv.astype(BF16))

        @pl.when(k == 0)
        def _():
            acc_ref[...] = part

        @pl.when(k > 0)
        def _():
            acc_ref[...] += part

        @pl.when(k == nk - 1)
        def _():
            if col_shards:
                _store_col_shards(o_ref, acc_ref[...], 0, tn // per, per)
            else:
                o_ref[...] = acc_ref[...].astype(BF16)

    if col_shards:
        out_spec = pl.BlockSpec((tn // per, m, per), lambda j, k: (j, 0, 0))
        out_shape = jax.ShapeDtypeStruct((N_SHARD, m, per), BF16)
    else:
        out_spec = pl.BlockSpec((m, tn), lambda j, k: (0, j))
        out_shape = jax.ShapeDtypeStruct((m, n), BF16)
    return _pallas_call(
        body, name=name, grid=(n // tn, nk),
        in_specs=[pl.BlockSpec((tk, m), lambda j, k: (k, 0)), pl.BlockSpec((tk, tn), lambda j, k: (k, j))],
        out_specs=out_spec, out_shape=out_shape, scratch_shapes=[pltpu.VMEM((m, tn), F32)],
        compiler_params=_params(dimension_semantics=("arbitrary", "arbitrary")),
    )(a, b)


def _pre_math(x, gm, gqa, gkva, gq, gk, gsq, gsk, taps, win, wqb, wkvb, cos, sin):
    h = _rms(x, gm)
    proj = _mm(h, win)
    if taps is not None:
        proj = proj + taps[0]
    o = IN_OFFS
    c_q, c_kv, k_pe = proj[:, o[0]:o[1]], proj[:, o[1]:o[2]], proj[:, o[2]:o[3]]
    q_s, k_s, v_s = proj[:, o[3]:o[4]], proj[:, o[4]:o[5]], proj[:, o[5]:o[6]]
    cqn = _rms(c_q, gqa)
    qa_all = _mm(cqn, wqb)
    ckvn = _rms(c_kv, gkva)
    kv_all = _mm(ckvn, wkvb)
    if taps is not None:
        qa_all = qa_all + taps[1]
        kv_all = kv_all + taps[2]
    q_a, k_a, v_a = [], [], []
    kvw = MLA_NOPE + MLA_V
    for hd in range(MLA_HEADS):
        qn = _rms(qa_all[:, hd * MLA_QK:(hd + 1) * MLA_QK], gq)
        k_nope = kv_all[:, hd * kvw:hd * kvw + MLA_NOPE]
        v_a.append(kv_all[:, hd * kvw + MLA_NOPE:(hd + 1) * kvw])
        kn = _rms(jnp.concatenate([k_nope, k_pe], axis=-1), gk)
        q_a.append(jnp.concatenate([qn[:, :MLA_NOPE], _rope(qn[:, MLA_NOPE:], cos, sin)], axis=-1))
        k_a.append(jnp.concatenate([kn[:, :MLA_NOPE], _rope(kn[:, MLA_NOPE:], cos, sin)], axis=-1))
    d = SWA_HEAD_DIM
    q_b = [_rope(_rms(q_s[:, hd * d:(hd + 1) * d], gsq), cos, sin) for hd in range(SWA_HEADS)]
    k_b = [_rope(_rms(k_s[:, j * d:(j + 1) * d], gsk), cos, sin) for j in range(SWA_KV_HEADS)]
    v_b = [v_s[:, j * d:(j + 1) * d] for j in range(SWA_KV_HEADS)]
    return (q_a, k_a, v_a, q_b, k_b, v_b), (h, cqn, ckvn)


_PRE_GAIN_WIDTHS = (D_MODEL, MLA_Q_RANK, MLA_KV_RANK, MLA_QK, MLA_QK, SWA_HEAD_DIM, SWA_HEAD_DIM)
_PRE_HEADS = ((MLA_HEADS, MLA_QK), (MLA_HEADS, MLA_QK), (MLA_HEADS, MLA_V),
              (SWA_HEADS, SWA_HEAD_DIM), (SWA_KV_HEADS, SWA_HEAD_DIM), (SWA_KV_HEADS, SWA_HEAD_DIM))


def _pre_fwd(x, gains, win, wqb, wkvb, cos, sin, name):
    s = x.shape[0]
    tm = _row_tile(s, 256)

    def body(x_ref, *refs):
        g_refs, (win_ref, wqb_ref, wkvb_ref, cos_ref, sin_ref), out_refs = refs[:7], refs[7:12], refs[12:]
        outs, _ = _pre_math(x_ref[...], *[g[...] for g in g_refs], None, win_ref[...], wqb_ref[...], wkvb_ref[...],
                            cos_ref[...], sin_ref[...])
        for ref, heads in zip(out_refs, outs):
            for hd, val in enumerate(heads):
                ref[hd] = val.astype(BF16)

    heads_spec = lambda nh, w: pl.BlockSpec((nh, tm, w), lambda i: (0, i, 0))
    return _pallas_call(
        body, name=name, grid=(s // tm,),
        in_specs=[pl.BlockSpec((tm, D_MODEL), lambda i: (i, 0))] + [_full((1, w)) for w in _PRE_GAIN_WIDTHS]
        + [_resident(win.shape), _resident(wqb.shape), _resident(wkvb.shape),
           pl.BlockSpec((tm, MLA_ROPE // 2), lambda i: (i, 0)), pl.BlockSpec((tm, MLA_ROPE // 2), lambda i: (i, 0))],
        out_specs=[heads_spec(nh, w) for nh, w in _PRE_HEADS],
        out_shape=[jax.ShapeDtypeStruct((nh, s, w), BF16) for nh, w in _PRE_HEADS],
        compiler_params=_params(dimension_semantics=("arbitrary",)),
    )(x, *gains, win, wqb, wkvb, cos, sin)


def _pre_bwd(x, dx_res, cts, gains, win, wqb, wkvb, cos, sin, name):
    s = x.shape[0]
    tm = _row_tile(s, 256)
    tap_widths = (IN_COLS, MLA_HEADS * MLA_QK, MLA_HEADS * (MLA_NOPE + MLA_V))

    def body(x_ref, dxr_ref, *refs):
        ct_refs, g_refs = refs[:6], refs[6:13]
        win_ref, wqb_ref, wkvb_ref, cos_ref, sin_ref = refs[13:18]
        dx_ref, dg_refs, dw_refs, acc_refs = refs[18], refs[19:26], refs[26:29], refs[29:32]
        i = pl.program_id(0)
        win_v, wqb_v, wkvb_v, cos_v, sin_v = win_ref[...], wqb_ref[...], wkvb_ref[...], cos_ref[...], sin_ref[...]

        def f(xv, gm, gqa, gkva, gq, gk, gsq, gsk, t0, t1, t2):
            return _pre_math(xv, gm, gqa, gkva, gq, gk, gsq, gsk, (t0, t1, t2), win_v, wqb_v, wkvb_v, cos_v, sin_v)

        taps = [jnp.zeros((tm, w), F32) for w in tap_widths]
        _, vjp, acts = jax.vjp(f, x_ref[...], *[g[...] for g in g_refs], *taps, has_aux=True)
        ct = tuple([ref[hd] for hd in range(nh)] for ref, (nh, _) in zip(ct_refs, _PRE_HEADS))
        grads = vjp(ct)
        dx_ref[...] = grads[0] + dxr_ref[...]
        dws = [_dot_tn(a.astype(BF16), t.astype(BF16)) for a, t in zip(acts, grads[8:11])]

        @pl.when(i == 0)
        def _():
            for ref, val in zip(dg_refs, grads[1:8]):
                ref[...] = val
            for ref, val in zip(acc_refs, dws):
                ref[...] = val

        @pl.when(i > 0)
        def _():
            for ref, val in zip(dg_refs, grads[1:8]):
                ref[...] += val
            for ref, val in zip(acc_refs, dws):
                ref[...] += val

        @pl.when(i == s // tm - 1)
        def _():
            for ref, acc in zip(dw_refs, acc_refs):
                _store_col_shards(ref, acc[...], 0, N_SHARD, acc.shape[1] // N_SHARD)

    heads_spec = lambda nh, w: pl.BlockSpec((nh, tm, w), lambda i: (0, i, 0))
    row = pl.BlockSpec((tm, D_MODEL), lambda i: (i, 0))
    half = pl.BlockSpec((tm, MLA_ROPE // 2), lambda i: (i, 0))
    shard_shapes = [(N_SHARD, w.shape[0], w.shape[1] // N_SHARD) for w in (win, wqb, wkvb)]
    return _pallas_call(
        body, name=name, grid=(s // tm,),
        in_specs=[row, row] + [heads_spec(nh, w) for nh, w in _PRE_HEADS] + [_full((1, w)) for w in _PRE_GAIN_WIDTHS]
        + [_resident(win.shape), _resident(wqb.shape), _resident(wkvb.shape), half, half],
        out_specs=[row] + [_full((1, w)) for w in _PRE_GAIN_WIDTHS] + [_full(shp) for shp in shard_shapes],
        out_shape=[jax.ShapeDtypeStruct((s, D_MODEL), F32)] + [jax.ShapeDtypeStruct((1, w), F32) for w in _PRE_GAIN_WIDTHS]
        + [jax.ShapeDtypeStruct(shp, BF16) for shp in shard_shapes],
        scratch_shapes=[pltpu.VMEM(w.shape, F32) for w in (win, wqb, wkvb)],
        compiler_params=_params(dimension_semantics=("arbitrary",)),
    )(x, dx_res, *cts, *gains, win, wqb, wkvb, cos, sin)


def _post_math(oa, ob, ga, gb, wo):
    mixed = jnp.concatenate([_rms(jnp.concatenate(oa, axis=-1), ga), _rms(jnp.concatenate(ob, axis=-1), gb)], axis=-1)
    return _mm(mixed, wo), mixed


def _post_fwd(x, oa, ob, ga, gb, wo, name):
    s = x.shape[0]
    tm = _row_tile(s, 256)

    def body(x_ref, oa_ref, ob_ref, ga_ref, gb_ref, wo_ref, y_ref):
        y, _ = _post_math([oa_ref[hd] for hd in range(MLA_HEADS)], [ob_ref[hd] for hd in range(SWA_HEADS)],
                          ga_ref[...], gb_ref[...], wo_ref[...])
        y_ref[...] = x_ref[...] + y

    row = pl.BlockSpec((tm, D_MODEL), lambda i: (i, 0))
    return _pallas_call(
        body, name=name, grid=(s // tm,),
        in_specs=[row, pl.BlockSpec((MLA_HEADS, tm, MLA_V), lambda i: (0, i, 0)),
                  pl.BlockSpec((SWA_HEADS, tm, SWA_HEAD_DIM), lambda i: (0, i, 0)),
                  _full((1, MLA_WIDTH)), _full((1, SWA_WIDTH)), _resident(wo.shape)],
        out_specs=row, out_shape=jax.ShapeDtypeStruct((s, D_MODEL), F32),
        compiler_params=_params(dimension_semantics=("arbitrary",)),
    )(x, oa, ob, ga, gb, wo)


def _post_bwd(dy, oa, ob, ga, gb, wo, name):
    s = dy.shape[0]
    tm = _row_tile(s, 256)

    def body(dy_ref, oa_ref, ob_ref, ga_ref, gb_ref, wo_ref, doa_ref, dob_ref, dga_ref, dgb_ref, dwo_ref, acc_ref):
        i = pl.program_id(0)
        wo_v = wo_ref[...]
        dyv = dy_ref[...]

        def f(oa_l, ob_l, ga_v, gb_v):
            return _post_math(oa_l, ob_l, ga_v, gb_v, wo_v)

        _, vjp, mixed = jax.vjp(f, [oa_ref[hd] for hd in range(MLA_HEADS)], [ob_ref[hd] for hd in range(SWA_HEADS)],
                                ga_ref[...], gb_ref[...], has_aux=True)
        doa, dob, dga, dgb = vjp(dyv)
        for hd in range(MLA_HEADS):
            doa_ref[hd] = doa[hd]
        for hd in range(SWA_HEADS):
            dob_ref[hd] = dob[hd]
        dwo = _dot_tn(mixed.astype(BF16), dyv.astype(BF16))

        @pl.when(i == 0)
        def _():
            dga_ref[...] = dga
            dgb_ref[...] = dgb
            acc_ref[...] = dwo

        @pl.when(i > 0)
        def _():
            dga_ref[...] += dga
            dgb_ref[...] += dgb
            acc_ref[...] += dwo

        @pl.when(i == s // tm - 1)
        def _():
            dwo_ref[...] = acc_ref[...].astype(BF16)

    row = pl.BlockSpec((tm, D_MODEL), lambda i: (i, 0))
    oa_spec = pl.BlockSpec((MLA_HEADS, tm, MLA_V), lambda i: (0, i, 0))
    ob_spec = pl.BlockSpec((SWA_HEADS, tm, SWA_HEAD_DIM), lambda i: (0, i, 0))
    return _pallas_call(
        body, name=name, grid=(s // tm,),
        in_specs=[row, oa_spec, ob_spec, _full((1, MLA_WIDTH)), _full((1, SWA_WIDTH)), _resident(wo.shape)],
        out_specs=[oa_spec, ob_spec, _full((1, MLA_WIDTH)), _full((1, SWA_WIDTH)), _full(wo.shape)],
        out_shape=[jax.ShapeDtypeStruct((MLA_HEADS, s, MLA_V), F32), jax.ShapeDtypeStruct((SWA_HEADS, s, SWA_HEAD_DIM), F32),
                   jax.ShapeDtypeStruct((1, MLA_WIDTH), F32), jax.ShapeDtypeStruct((1, SWA_WIDTH), F32),
                   jax.ShapeDtypeStruct(wo.shape, BF16)],
        scratch_shapes=[pltpu.VMEM(wo.shape, F32)],
        compiler_params=_params(dimension_semantics=("arbitrary",)),
    )(dy, oa, ob, ga, gb, wo)


def _attn_tile(s):
    return 512 if s >= 2048 else 128


def _causal_mask(t):
    return lax.broadcasted_iota(jnp.int32, (t, t), 1) <= lax.broadcasted_iota(jnp.int32, (t, t), 0)


def _mla_fwd(q, k, v, name):
    nh, s, _ = q.shape
    t = _attn_tile(s)

    def body(q_ref, k_ref, v_ref, o_ref, lse_ref):
        qi = pl.program_id(1)
        qv = q_ref[...]

        def block(j, carry, masked):
            m, l, acc = carry
            start = pl.multiple_of(j * t, t)
            sc = _dot_nt(qv, k_ref[pl.ds(start, t), :]) * MLA_SCALE
            if masked:
                sc = jnp.where(_causal_mask(t), sc, NEG)
            m_new = jnp.maximum(m, jnp.max(sc, axis=-1, keepdims=True))
            alpha = jnp.exp(m - m_new)
            p = jnp.exp(sc - m_new)
            l = alpha * l + jnp.sum(p, axis=-1, keepdims=True)
            acc = alpha * acc + jnp.dot(p.astype(BF16), v_ref[pl.ds(start, t), :], preferred_element_type=F32)
            return m_new, l, acc

        init = (jnp.full((t, 1), NEG, F32), jnp.zeros((t, 1), F32), jnp.zeros((t, MLA_V), F32))
        carry = lax.fori_loop(0, qi, lambda j, c: block(j, c, False), init)
        m, l, acc = block(qi, carry, True)
        o_ref[...] = acc / l
        lse_ref[...] = m + jnp.log(l)

    return _pallas_call(
        body, name=name, grid=(nh, s // t),
        in_specs=[pl.BlockSpec((None, t, MLA_QK), lambda h, i: (h, i, 0)), pl.BlockSpec((None, s, MLA_QK), lambda h, i: (h, 0, 0)),
                  pl.BlockSpec((None, s, MLA_V), lambda h, i: (h, 0, 0))],
        out_specs=[pl.BlockSpec((None, t, MLA_V), lambda h, i: (h, i, 0)), pl.BlockSpec((None, t, 1), lambda h, i: (h, i, 0))],
        out_shape=[jax.ShapeDtypeStruct((nh, s, MLA_V), F32), jax.ShapeDtypeStruct((nh, s, 1), F32)],
        compiler_params=_params(dimension_semantics=("arbitrary", "arbitrary")),
    )(q, k, v)


def _mla_bwd_dq(q, k, v, o, do, lse, name):
    nh, s, _ = q.shape
    t = _attn_tile(s)

    def body(q_ref, k_ref, v_ref, o_ref, do_ref, lse_ref, dq_ref, delta_ref):
        qi = pl.program_id(1)
        qv = q_ref[...]
        dov = do_ref[...]
        lse = lse_ref[...]
        delta = jnp.sum(dov * o_ref[...], axis=-1, keepdims=True)
        delta_ref[...] = delta
        dob = dov.astype(BF16)

        def block(j, dq, masked):
            start = pl.multiple_of(j * t, t)
            kb = k_ref[pl.ds(start, t), :]
            sc = _dot_nt(qv, kb) * MLA_SCALE
            p = jnp.exp(sc - lse)
            if masked:
                p = jnp.where(_causal_mask(t), p, 0.0)
            dp = _dot_nt(dob, v_ref[pl.ds(start, t), :])
            ds = p * (dp - delta) * MLA_SCALE
            return dq + jnp.dot(ds.astype(BF16), kb, preferred_element_type=F32)

        dq = lax.fori_loop(0, qi, lambda j, c: block(j, c, False), jnp.zeros((t, MLA_QK), F32))
        dq_ref[...] = block(qi, dq, True)

    tile = lambda w: pl.BlockSpec((None, t, w), lambda h, i: (h, i, 0))
    whole = lambda w: pl.BlockSpec((None, s, w), lambda h, i: (h, 0, 0))
    return _pallas_call(
        body, name=name, grid=(nh, s // t),
        in_specs=[tile(MLA_QK), whole(MLA_QK), whole(MLA_V), tile(MLA_V), tile(MLA_V), tile(1)],
        out_specs=[tile(MLA_QK), tile(1)],
        out_shape=[jax.ShapeDtypeStruct((nh, s, MLA_QK), F32), jax.ShapeDtypeStruct((nh, s, 1), F32)],
        compiler_params=_params(dimension_semantics=("arbitrary", "arbitrary")),
    )(q, k, v, o, do, lse)


def _mla_bwd_dkv(q, k, v, do, lse_row, delta_row, name):
    nh, s, _ = q.shape
    t = _attn_tile(s)
    nq = s // t

    def body(q_ref, k_ref, v_ref, do_ref, lse_ref, delta_ref, dk_ref, dv_ref):
        kj = pl.program_id(1)
        kv_, vv = k_ref[...], v_ref[...]

        def block(i, carry, masked):
            dk, dv = carry
            start = pl.multiple_of(i * t, t)
            qb = q_ref[pl.ds(start, t), :]
            dob = do_ref[pl.ds(start, t), :].astype(BF16)
            sc = _dot_nt(kv_, qb) * MLA_SCALE
            p = jnp.exp(sc - lse_ref[pl.ds(i, 1), :])
            if masked:
                p = jnp.where(lax.broadcasted_iota(jnp.int32, (t, t), 0) <= lax.broadcasted_iota(jnp.int32, (t, t), 1), p, 0.0)
            dv = dv + jnp.dot(p.astype(BF16), dob, preferred_element_type=F32)
            dp = _dot_nt(vv, dob)
            ds = p * (dp - delta_ref[pl.ds(i, 1), :]) * MLA_SCALE
            dk = dk + jnp.dot(ds.astype(BF16), qb, preferred_element_type=F32)
            return dk, dv

        carry = block(kj, (jnp.zeros((t, MLA_QK), F32), jnp.zeros((t, MLA_V), F32)), True)
        dk, dv = lax.fori_loop(kj + 1, nq, lambda i, c: block(i, c, False), carry)
        dk_ref[...] = dk
        dv_ref[...] = dv

    tile = lambda w: pl.BlockSpec((None, t, w), lambda h, j: (h, j, 0))
    whole = lambda w: pl.BlockSpec((None, s, w), lambda h, j: (h, 0, 0))
    rows = pl.BlockSpec((None, nq, t), lambda h, j: (h, 0, 0))
    return _pallas_call(
        body, name=name, grid=(nh, nq),
        in_specs=[whole(MLA_QK), tile(MLA_QK), tile(MLA_V), whole(MLA_V), rows, rows],
        out_specs=[tile(MLA_QK), tile(MLA_V)],
        out_shape=[jax.ShapeDtypeStruct((nh, s, MLA_QK), F32), jax.ShapeDtypeStruct((nh, s, MLA_V), F32)],
        compiler_params=_params(dimension_semantics=("arbitrary", "arbitrary")),
    )(q, k, v, do, lse_row, delta_row)


def _swa_math(q_l, kband, vband, sink_l, first):
    q_rel = lax.broadcasted_iota(jnp.int32, (BLOCK, 2 * BLOCK), 0) + BLOCK
    k_rel = lax.broadcasted_iota(jnp.int32, (BLOCK, 2 * BLOCK), 1)
    dist = q_rel - k_rel
    valid = (dist >= 0) & (dist < BLOCK) & ((k_rel >= BLOCK) | jnp.logical_not(first))
    kb, vb = kband.astype(BF16), vband.astype(BF16)
    outs, psinks = [], []
    for qh, sink in zip(q_l, sink_l):
        sc = jnp.where(valid, _dot_nt(qh.astype(BF16), kb) * SWA_SCALE, NEG)
        m = lax.stop_gradient(jnp.maximum(jnp.max(sc, axis=-1, keepdims=True), sink))
        e = jnp.exp(sc - m)
        es = jnp.exp(sink - m)
        den = jnp.sum(e, axis=-1, keepdims=True) + es
        outs.append(jnp.dot((e / den).astype(BF16), vb, preferred_element_type=F32))
        psinks.append(es / den)
    return outs, psinks


def _swa_specs():
    grp = lambda: pl.BlockSpec((SWA_GROUP, BLOCK, SWA_HEAD_DIM), lambda j, n: (j, n, 0))
    prev = pl.BlockSpec((None, BLOCK, SWA_HEAD_DIM), lambda j, n: (j, n, 0))
    cur = pl.BlockSpec((None, BLOCK, SWA_HEAD_DIM), lambda j, n: (j, n + 1, 0))
    sink = pl.BlockSpec((None, SWA_GROUP, 128), lambda j, n: (j, 0, 0))
    return grp, prev, cur, sink


def _swa_fwd(q, kpad, vpad, sinks, name):
    _, s, _ = q.shape
    grp, prev, cur, sink = _swa_specs()

    def body(q_ref, kp_ref, kc_ref, vp_ref, vc_ref, sink_ref, o_ref):
        first = pl.program_id(1) == 0
        kband = jnp.concatenate([kp_ref[...], kc_ref[...]], axis=0)
        vband = jnp.concatenate([vp_ref[...], vc_ref[...]], axis=0)
        sk = sink_ref[...]
        outs, _ = _swa_math([q_ref[g] for g in range(SWA_GROUP)], kband, vband,
                            [sk[g:g + 1, 0:1] for g in range(SWA_GROUP)], first)
        for g in range(SWA_GROUP):
            o_ref[g] = outs[g]

    return _pallas_call(
        body, name=name, grid=(SWA_KV_HEADS, s // BLOCK),
        in_specs=[grp(), prev, cur, prev, cur, sink], out_specs=grp(),
        out_shape=jax.ShapeDtypeStruct((SWA_HEADS, s, SWA_HEAD_DIM), F32),
        compiler_params=_params(dimension_semantics=("arbitrary", "arbitrary")),
    )(q, kpad, kpad, vpad, vpad, sinks)


def _swa_bwd(q, kpad, vpad, sinks, do, name):
    _, s, _ = q.shape
    grp, prev, cur, sink = _swa_specs()

    def body(q_ref, kp_ref, kc_ref, vp_ref, vc_ref, sink_ref, do_ref, dq_ref, dk_ref, dv_ref, dsink_ref):
        n = pl.program_id(1)
        first = n == 0
        sk = sink_ref[...]
        sink_l = [sk[g:g + 1, 0:1] for g in range(SWA_GROUP)]
        kband = jnp.concatenate([kp_ref[...], kc_ref[...]], axis=0).astype(F32)
        vband = jnp.concatenate([vp_ref[...], vc_ref[...]], axis=0).astype(F32)
        q_l = [q_ref[g].astype(F32) for g in range(SWA_GROUP)]

        def f(ql, kb, vb):
            return _swa_math(ql, kb, vb, sink_l, first)

        outs, vjp, psinks = jax.vjp(f, q_l, kband, vband, has_aux=True)
        do_l = [do_ref[g] for g in range(SWA_GROUP)]
        dq_l, dkb, dvb = vjp(do_l)
        for g in range(SWA_GROUP):
            dq_ref[g] = dq_l[g]
        dsk = jnp.concatenate(
            [jnp.broadcast_to(-jnp.sum(psinks[g] * jnp.sum(do_l[g] * outs[g], axis=-1, keepdims=True), axis=0, keepdims=True),
                              (1, 128)) for g in range(SWA_GROUP)] + [jnp.zeros((8 - SWA_GROUP, 128), F32)], axis=0)

        @pl.when(first)
        def _():
            dk_ref[...] = jnp.zeros_like(dk_ref)
            dv_ref[...] = jnp.zeros_like(dv_ref)
            dsink_ref[...] = jnp.zeros_like(dsink_ref)

        rows = pl.ds(pl.multiple_of(n * BLOCK, BLOCK), 2 * BLOCK)
        dk_ref[rows, :] += dkb
        dv_ref[rows, :] += dvb
        dsink_ref[...] += dsk

    acc = pl.BlockSpec((None, s + BLOCK, SWA_HEAD_DIM), lambda j, n: (j, 0, 0))
    return _pallas_call(
        body, name=name, grid=(SWA_KV_HEADS, s // BLOCK),
        in_specs=[grp(), prev, cur, prev, cur, sink, grp()],
        out_specs=[grp(), acc, acc, pl.BlockSpec((None, 8, 128), lambda j, n: (j, 0, 0))],
        out_shape=[jax.ShapeDtypeStruct((SWA_HEADS, s, SWA_HEAD_DIM), F32),
                   jax.ShapeDtypeStruct((SWA_KV_HEADS, s + BLOCK, SWA_HEAD_DIM), F32),
                   jax.ShapeDtypeStruct((SWA_KV_HEADS, s + BLOCK, SWA_HEAD_DIM), F32),
                   jax.ShapeDtypeStruct((SWA_KV_HEADS, 8, 128), F32)],
        compiler_params=_params(dimension_semantics=("arbitrary", "arbitrary")),
    )(q, kpad, kpad, vpad, vpad, sinks, do)


def _loss_head(y, target, name):
    s = y.shape[0]
    tm = _row_tile(s, 512)

    def body(y_ref, t_ref, dy_ref, loss_ref):
        i = pl.program_id(0)
        err = y_ref[...] - t_ref[...]
        dy_ref[...] = err * (1.0 / D_MODEL)
        part = jnp.broadcast_to(0.5 * jnp.sum(jnp.mean(err * err, axis=-1, keepdims=True), axis=0, keepdims=True), (1, 128))

        @pl.when(i == 0)
        def _():
            loss_ref[...] = part

        @pl.when(i > 0)
        def _():
            loss_ref[...] += part

    row = pl.BlockSpec((tm, D_MODEL), lambda i: (i, 0))
    return _pallas_call(
        body, name=name, grid=(s // tm,), in_specs=[row, row], out_specs=[row, _full((1, 128))],
        out_shape=[jax.ShapeDtypeStruct((s, D_MODEL), F32), jax.ShapeDtypeStruct((1, 128), F32)],
        compiler_params=_params(dimension_semantics=("arbitrary",)),
    )(y, target)


def _adamw(w, g, m, v, name):
    rows, cols = w.shape
    tr = rows
    for cand in (512, 256, 128, 64, 32, 16, 8):
        if rows % cand == 0 and rows > cand:
            tr = cand
            break

    def body(w_ref, g_ref, m_ref, v_ref, d_ref, nm_ref, nv_ref):
        gv = g_ref[...]
        nm = ADAM_B1 * m_ref[...] + (1.0 - ADAM_B1) * gv
        nv = ADAM_B2 * v_ref[...] + (1.0 - ADAM_B2) * (gv * gv)
        m_hat = nm / (1.0 - ADAM_B1 ** ADAM_STEP)
        v_hat = nv / (1.0 - ADAM_B2 ** ADAM_STEP)
        d_ref[...] = -ADAM_LR * (m_hat / (jnp.sqrt(v_hat) + ADAM_EPS) + ADAM_WD * w_ref[...])
        nm_ref[...] = nm
        nv_ref[...] = nv

    blk = pl.BlockSpec((tr, cols), lambda i: (i, 0))
    return _pallas_call(
        body, name=name, grid=(rows // tr,), in_specs=[blk] * 4, out_specs=[blk] * 3,
        out_shape=[jax.ShapeDtypeStruct((rows, cols), F32)] * 3,
        compiler_params=_params(dimension_semantics=("arbitrary",)),
    )(w, g, m, v)


def _position():
    return lax.axis_index("x"), lax.axis_index("y"), lax.axis_index("c")


def _remote(src, dst, send_sems, recv_sems, k, to):
    return pltpu.make_async_remote_copy(src_ref=src, dst_ref=dst, send_sem=send_sems.at[k], recv_sem=recv_sems.at[k],
                                        device_id=to, device_id_type=MESH)


_HBM = pl.BlockSpec(memory_space=pltpu.HBM)


def _all_gather_layers(mine, name):
    na = len(mine)

    def body(*refs):
        ins, outs, (send_sems, recv_sems) = refs[:na], refs[na:2 * na], refs[2 * na:]
        x, y, c = _position()
        me, sibling = (x, y, c), (x, y, 1 - c)
        chips = [(1 - x, y), (x, 1 - y), (1 - x, 1 - y)]

        def slot(a, px, py, pc):
            return outs[a].at[4 * px + 2 * py + pc]

        def cp(a, k, block, to, src=None):
            return _remote(slot(a, *block) if src is None else src, slot(a, *block), send_sems, recv_sems, 6 * a + k, to)

        first = [cp(a, j, me, (*chip, c), src=ins[a].at[c]) for a in range(na) for j, chip in enumerate(chips)]
        for f in first:
            f.start()
        passed = []
        for j, chip in enumerate(chips):
            for a in range(na):
                cp(a, j, (*chip, c), me).wait_recv()
                fwd = cp(a, 3 + j, (*chip, c), sibling)
                fwd.start()
                passed.append(fwd)
        for a in range(na):
            for j, chip in enumerate(chips):
                cp(a, 3 + j, (*chip, 1 - c), me).wait_recv()
        for f in first + passed:
            f.wait_send()

    return _pallas_call(
        body, name=name, in_specs=[_HBM] * na, out_specs=[_HBM] * na,
        out_shape=[jax.ShapeDtypeStruct((N_DEV,) + m.shape[1:], m.dtype) for m in mine],
        scratch_shapes=[pltpu.SemaphoreType.DMA((6 * na,)), pltpu.SemaphoreType.DMA((6 * na,))],
    )(*mine)


def _sibling_exchange(parts, name, other_layer):
    na = len(parts)

    def body(*refs):
        ins, outs, (send_sems, recv_sems) = refs[:na], refs[na:2 * na], refs[2 * na:]
        x, y, c = _position()
        copies = [_remote(ins[a].at[1 - c] if other_layer else ins[a], outs[a], send_sems, recv_sems, a, (x, y, 1 - c))
                  for a in range(na)]
        for cp in copies:
            cp.start()
        for cp in copies:
            cp.wait()

    return _pallas_call(
        body, name=name, in_specs=[_HBM] * na, out_specs=[_HBM] * na,
        out_shape=[jax.ShapeDtypeStruct(p.shape[1:] if other_layer else p.shape, p.dtype) for p in parts],
        scratch_shapes=[pltpu.SemaphoreType.DMA((na,)), pltpu.SemaphoreType.DMA((na,))],
    )(*parts)


def _scatter_to_chips(parts, name):
    na = len(parts)

    def body(*refs):
        ins, outs, (send_sems, recv_sems) = refs[:na], refs[na:2 * na], refs[2 * na:]
        x, y, c = _position()
        chips = [(1 - x, y), (x, 1 - y), (1 - x, 1 - y)]
        copies = [_remote(ins[a].at[2 * px + py], outs[a].at[j], send_sems, recv_sems, 3 * a + j, (px, py, c))
                  for a in range(na) for j, (px, py) in enumerate(chips)]
        for cp in copies:
            cp.start()
        for cp in copies:
            cp.wait()

    return _pallas_call(
        body, name=name, in_specs=[_HBM] * na, out_specs=[_HBM] * na,
        out_shape=[jax.ShapeDtypeStruct((3,) + p.shape[1:], p.dtype) for p in parts],
        scratch_shapes=[pltpu.SemaphoreType.DMA((3 * na,)), pltpu.SemaphoreType.DMA((3 * na,))],
    )(*parts)


def _assemble(g4, mine, name, side_by_side):
    _, nl, r, w = g4.shape
    tr = next(cand for cand in (256, 128) if r % cand == 0)

    def body(in_ref, mine_ref, out_ref):
        chip = 2 * lax.axis_index("x") + lax.axis_index("y")
        blocks = [jnp.where(chip == sh, mine_ref[...], in_ref[sh]) for sh in range(N_SHARD)]
        if side_by_side:
            out_ref[...] = jnp.concatenate(blocks, axis=-1)
        else:
            for sh in range(N_SHARD):
                out_ref[sh] = blocks[sh]

    if side_by_side:
        out_spec = pl.BlockSpec((None, tr, N_SHARD * w), lambda l, i: (l, i, 0))
        out_shape = jax.ShapeDtypeStruct((nl, r, N_SHARD * w), g4.dtype)
    else:
        out_spec = pl.BlockSpec((None, N_SHARD, tr, w), lambda l, i: (l, 0, i, 0))
        out_shape = jax.ShapeDtypeStruct((nl, N_SHARD, r, w), g4.dtype)
    return _pallas_call(
        body, name=name, grid=(nl, r // tr),
        in_specs=[pl.BlockSpec((N_SHARD, None, tr, w), lambda l, i: (0, l, i, 0)),
                  pl.BlockSpec((None, tr, w), lambda l, i: (l, i, 0))],
        out_specs=out_spec, out_shape=out_shape,
        compiler_params=_params(dimension_semantics=("arbitrary", "arbitrary")),
    )(g4, mine)


def _all_reduce_small(vec, name):
    r, l = vec.shape

    def body(v_ref, out_ref, gath_ref, send_sems, recv_sems):
        x, y, c = _position()
        me = 4 * x + 2 * y + c
        gath_ref[me] = v_ref[...]
        copies = []
        for k in range(1, N_DEV):
            to = (x ^ (k >> 2), y ^ ((k >> 1) & 1), c ^ (k & 1))
            copies.append(_remote(gath_ref.at[me], gath_ref.at[me], send_sems, recv_sems, k - 1, to))
        for cp in copies:
            cp.start()
        for k in range(1, N_DEV):
            frm = 4 * (x ^ (k >> 2)) + 2 * (y ^ ((k >> 1) & 1)) + (c ^ (k & 1))
            _remote(gath_ref.at[frm], gath_ref.at[frm], send_sems, recv_sems, k - 1, (x, y, c)).wait_recv()
        for cp in copies:
            cp.wait_send()
        total = gath_ref[0]
        for d in range(1, N_DEV):
            total = total + gath_ref[d]
        out_ref[...] = total

    vm = pl.BlockSpec(memory_space=pltpu.VMEM)
    return _pallas_call(
        body, name=name, in_specs=[vm], out_specs=vm, out_shape=jax.ShapeDtypeStruct((r, l), F32),
        scratch_shapes=[pltpu.VMEM((N_DEV, r, l), F32), pltpu.SemaphoreType.DMA((N_DEV - 1,)),
                        pltpu.SemaphoreType.DMA((N_DEV - 1,))],
    )(vec)


def _sum_blocks(blocks, out_dtype, name):
    m, w = blocks[0].shape
    tr = next(cand for cand in (512, 256, 128) if m % cand == 0)

    def body(*refs):
        total = refs[0][...].astype(F32)
        for ref in refs[1:-1]:
            total = total + ref[...].astype(F32)
        refs[-1][...] = total.astype(out_dtype)

    blk = pl.BlockSpec((tr, w), lambda i: (i, 0))
    return _pallas_call(
        body, name=name, grid=(m // tr,), in_specs=[blk] * len(blocks), out_specs=blk,
        out_shape=jax.ShapeDtypeStruct((m, w), out_dtype),
        compiler_params=_params(dimension_semantics=("arbitrary",)),
    )(*blocks)


GROUPS = (('ffn1_w_gate', 'ffn1_w_up', 'ffn2_w_gate', 'ffn2_w_up'), ('ffn1_w_down', 'ffn2_w_down', 'w_o'),
          ('w_in',), ('mla_w_q_b',), ('mla_w_kv_b',))


def _shard_rows(name):
    shape, axis = BIG[name]
    return shape[0] // N_SHARD if axis == 0 else shape[0]


def _group_row_offsets(group):
    return [int(v) for v in np.cumsum([0] + [_shard_rows(n) for n in group])]


def _rope_tables(s):
    pos = jnp.arange(s, dtype=F32)
    inv = 1.0 / (ROPE_THETA ** (jnp.arange(0, MLA_ROPE, 2, dtype=F32) / MLA_ROPE))
    ang = pos[:, None] * inv[None, :]
    return jnp.cos(ang), jnp.sin(ang)


def _local_step(x, target, small, big):
    s = x.shape[0]
    cos, sin = _rope_tables(s)
    row = lambda name, l: small[name][l][None, :]
    saved = []
    for l in range(DEPTH):
        sv = {'x0': x}
        x, sv['g1'], sv['u1'] = _ffn_fwd(x, row('ffn1_norm', l), big['ffn1_w_gate'][l], big['ffn1_w_up'][l],
                                        big['ffn1_w_down'][l], f"ffn1_fwd_{l}")
        sv['x1'] = x
        gains = [row(n, l) for n in ('mix_norm', 'mla_q_a_norm', 'mla_kv_a_norm', 'mla_q_norm', 'mla_k_norm',
                                     'swa_q_norm', 'swa_k_norm')]
        q_a, k_a, v_a, q_b, k_b, v_b = _pre_fwd(x, gains, big['w_in'][l], big['mla_w_q_b'][l], big['mla_w_kv_b'][l],
                                                cos, sin, f"pre_fwd_{l}")
        o_a, lse = _mla_fwd(q_a, k_a, v_a, f"mla_fwd_{l}")
        kpad = jnp.pad(k_b, ((0, 0), (BLOCK, 0), (0, 0)))
        vpad = jnp.pad(v_b, ((0, 0), (BLOCK, 0), (0, 0)))
        sinks = jnp.broadcast_to(small['swa_sinks'][l].reshape(SWA_KV_HEADS, SWA_GROUP, 1), (SWA_KV_HEADS, SWA_GROUP, 128))
        o_b = _swa_fwd(q_b, kpad, vpad, sinks, f"swa_fwd_{l}")
        sv.update(gains=gains, q_a=q_a, k_a=k_a, v_a=v_a, q_b=q_b, kpad=kpad, vpad=vpad, sinks=sinks, o_a=o_a, lse=lse, o_b=o_b)
        x = _post_fwd(x, o_a, o_b, row('mla_out_norm', l), row('swa_out_norm', l), big['w_o'][l], f"post_fwd_{l}")
        sv['x2'] = x
        x, sv['g2'], sv['u2'] = _ffn_fwd(x, row('ffn2_norm', l), big['ffn2_w_gate'][l], big['ffn2_w_up'][l],
                                        big['ffn2_w_down'][l], f"ffn2_fwd_{l}")
        saved.append(sv)

    dx, loss = _loss_head(x, target, "loss_head")

    gs = {n: [None] * DEPTH for n in SMALL_NAMES}
    gb = {n: [None] * DEPTH for n in BIG_NAMES}
    t = _attn_tile(s)
    for l in reversed(range(DEPTH)):
        sv = saved[l]

        def ffn_back(tag, xin, dy, gate, up):
            dxi, dgain, nb, act, dgate, dup = _ffn_bwd(xin, dy, gate, up, row(tag + '_norm', l), big[tag + '_w_gate'][l],
                                                      big[tag + '_w_up'][l], big[tag + '_w_down'][l], f"{tag}_bwd_{l}")
            gs[tag + '_norm'][l] = dgain[0]
            gb[tag + '_w_gate'][l] = _matmul_tn(nb, dgate, 1.0, f"{tag}_dwg_{l}", True)
            gb[tag + '_w_up'][l] = _matmul_tn(nb, dup, 1.0, f"{tag}_dwu_{l}", True)
            gb[tag + '_w_down'][l] = _matmul_tn(act, dy, 0.5, f"{tag}_dwd_{l}", False).reshape(N_SHARD, D_FF // N_SHARD, D_MODEL)
            return dxi

        dx = ffn_back('ffn2', sv['x2'], dx, sv['g2'], sv['u2'])
        do_a, do_b, dga, dgb, dwo = _post_bwd(dx, sv['o_a'], sv['o_b'], row('mla_out_norm', l), row('swa_out_norm', l),
                                              big['w_o'][l], f"post_bwd_{l}")
        gs['mla_out_norm'][l], gs['swa_out_norm'][l] = dga[0], dgb[0]
        gb['w_o'][l] = dwo.reshape(N_SHARD, MIX_WIDTH // N_SHARD, D_MODEL)
        dq_a, delta = _mla_bwd_dq(sv['q_a'], sv['k_a'], sv['v_a'], sv['o_a'], do_a, sv['lse'], f"mla_dq_{l}")
        dk_a, dv_a = _mla_bwd_dkv(sv['q_a'], sv['k_a'], sv['v_a'], do_a, sv['lse'].reshape(MLA_HEADS, s // t, t),
                                  delta.reshape(MLA_HEADS, s // t, t), f"mla_dkv_{l}")
        dq_b, dkpad, dvpad, dsink = _swa_bwd(sv['q_b'], sv['kpad'], sv['vpad'], sv['sinks'], do_b, f"swa_bwd_{l}")
        gs['swa_sinks'][l] = dsink[:, :SWA_GROUP, 0].reshape(SWA_HEADS)
        cts = [dq_a, dk_a, dv_a, dq_b, dkpad[:, BLOCK:], dvpad[:, BLOCK:]]
        outs = _pre_bwd(sv['x1'], dx, cts, sv['gains'], big['w_in'][l], big['mla_w_q_b'][l], big['mla_w_kv_b'][l],
                        cos, sin, f"pre_bwd_{l}")
        dx = outs[0]
        for n, val in zip(('mix_norm', 'mla_q_a_norm', 'mla_kv_a_norm', 'mla_q_norm', 'mla_k_norm', 'swa_q_norm', 'swa_k_norm'),
                          outs[1:8]):
            gs[n][l] = val[0]
        gb['w_in'][l], gb['mla_w_q_b'][l], gb['mla_w_kv_b'][l] = outs[8:11]
        dx = ffn_back('ffn1', sv['x0'], dx, sv['g1'], sv['u1'])
    return loss, dx, gs, gb


def kernel(x, ffn1_norm, ffn1_w_gate, ffn1_w_up, ffn1_w_down, mix_norm, w_in, mla_q_a_norm, mla_w_q_b, mla_kv_a_norm, mla_w_kv_b, mla_q_norm, mla_k_norm, swa_q_norm, swa_k_norm, swa_sinks, mla_out_norm, swa_out_norm, w_o, ffn2_norm, ffn2_w_gate, ffn2_w_up, ffn2_w_down, loss_target, m_ffn1_norm, m_ffn1_w_gate, m_ffn1_w_up, m_ffn1_w_down, m_mix_norm, m_w_in, m_mla_q_a_norm, m_mla_w_q_b, m_mla_kv_a_norm, m_mla_w_kv_b, m_mla_q_norm, m_mla_k_norm, m_swa_q_norm, m_swa_k_norm, m_swa_sinks, m_mla_out_norm, m_swa_out_norm, m_w_o, m_ffn2_norm, m_ffn2_w_gate, m_ffn2_w_up, m_ffn2_w_down, v_ffn1_norm, v_ffn1_w_gate, v_ffn1_w_up, v_ffn1_w_down, v_mix_norm, v_w_in, v_mla_q_a_norm, v_mla_w_q_b, v_mla_kv_a_norm, v_mla_w_kv_b, v_mla_q_norm, v_mla_k_norm, v_swa_q_norm, v_swa_k_norm, v_swa_sinks, v_mla_out_norm, v_swa_out_norm, v_w_o, v_ffn2_norm, v_ffn2_w_gate, v_ffn2_w_up, v_ffn2_w_down):
    args = dict(locals())
    weights = {n: args[n] for n in WEIGHT_NAMES}
    mom_m = {n: args["m_" + n] for n in WEIGHT_NAMES}
    mom_v = {n: args["v_" + n] for n in WEIGHT_NAMES}

    mine = [jnp.concatenate([weights[n].astype(BF16) for n in group], axis=1) for group in GROUPS]
    c = lax.axis_index("c")
    chip = 2 * lax.axis_index("x") + lax.axis_index("y")
    gathered = [g.reshape((N_SHARD, DEPTH) + g.shape[1:]) for g in _all_gather_layers(mine, "gather_weights")]
    big = {}
    for gi, group in enumerate(GROUPS):
        offs = _group_row_offsets(group)
        col_sharded = BIG[group[0]][1] == 1
        full = _assemble(gathered[gi], mine[gi], f"assemble_{gi}", col_sharded)
        for i, n in enumerate(group):
            if col_sharded:
                big[n] = [full[l, offs[i]:offs[i + 1]] for l in range(DEPTH)]
            else:
                big[n] = [full[l, :, offs[i]:offs[i + 1]].reshape(BIG[n][0]) for l in range(DEPTH)]
    small = {n: weights[n] for n in SMALL_NAMES}

    loss, dx, gs, gb = _local_step(x[0], loss_target[0], small, big)

    small_flat = jnp.concatenate([jnp.stack(gs[n]).reshape(-1) for n in SMALL_NAMES] + [loss[0, :1]])
    n_small = small_flat.shape[0]
    lanes = -(-n_small // (8 * 128)) * 128
    small_sum = _all_reduce_small(jnp.pad(small_flat, (0, 8 * lanes - n_small)).reshape(8, lanes), "reduce_small").reshape(-1)
    grads = {}
    off = 0
    for n in SMALL_NAMES:
        cnt = int(np.prod(weights[n].shape))
        grads[n] = small_sum[off:off + cnt].reshape(weights[n].shape)
        off += cnt
    loss_out = small_sum[off]

    parts = [jnp.stack([jnp.concatenate([gb[n][l] for n in group], axis=1) for l in range(DEPTH)]) for group in GROUPS]
    from_sibling = _sibling_exchange(parts, "swap_layers", True)
    chip_sums = []
    for gi, (p, got) in enumerate(zip(parts, from_sibling)):
        kept = lax.dynamic_index_in_dim(p, c, axis=0, keepdims=False)
        rows = N_SHARD * p.shape[2]
        pair = _sum_blocks([kept.reshape(rows, -1), got.reshape(rows, -1)], BF16, f"sum_pair_{gi}")
        chip_sums.append(pair.reshape(p.shape[1:]))
    from_chips = _scatter_to_chips(chip_sums, "scatter_chips")
    halves = []
    for gi, (cs, got) in enumerate(zip(chip_sums, from_chips)):
        own = lax.dynamic_index_in_dim(cs, chip, axis=0, keepdims=False)
        halves.append(_sum_blocks([own, got[0], got[1], got[2]], F32, f"sum_chips_{gi}"))
    others = _sibling_exchange(halves, "share_layers", False)
    for gi, group in enumerate(GROUPS):
        offs = _group_row_offsets(group)
        both = jnp.stack([jnp.where(c == 0, halves[gi], others[gi]), jnp.where(c == 0, others[gi], halves[gi])])
        for i, n in enumerate(group):
            grads[n] = both[:, offs[i]:offs[i + 1]]

    deltas, new_m, new_v = {}, {}, {}
    for n in WEIGHT_NAMES:
        shp = weights[n].shape
        two_d = (DEPTH, shp[-1]) if len(shp) == 2 else (shp[0] * shp[1], shp[2])
        d, nm, nv = _adamw(weights[n].reshape(two_d), grads[n].reshape(two_d), mom_m[n].reshape(two_d),
                           mom_v[n].reshape(two_d), f"adamw_{n}")
        deltas[n], new_m[n], new_v[n] = d.reshape(shp), nm.reshape(shp), nv.reshape(shp)

    return (loss_out, dx[None], *[grads[n] for n in WEIGHT_NAMES], *[deltas[n] for n in WEIGHT_NAMES],
            *[new_m[n] for n in WEIGHT_NAMES], *[new_v[n] for n in WEIGHT_NAMES])
```

```python
import functools

import numpy as np
import jax
import jax.numpy as jnp
from jax import lax
from jax.experimental import pallas as pl
from jax.experimental.pallas import tpu as pltpu

F32 = jnp.float32
BF16 = jnp.bfloat16

D_MODEL = 1024
DEPTH = 2
EPS = 1e-6
ROPE_THETA = 10000.0
BLOCK = 128
MLA_HEADS = 4
MLA_Q_RANK = 256
MLA_KV_RANK = 128
MLA_NOPE = 128
MLA_ROPE = 64
MLA_V = 128
MLA_QK = MLA_NOPE + MLA_ROPE
MLA_WIDTH = MLA_HEADS * MLA_V
SWA_HEADS = 8
SWA_KV_HEADS = 2
SWA_GROUP = SWA_HEADS // SWA_KV_HEADS
SWA_HEAD_DIM = 64
SWA_WIDTH = SWA_HEADS * SWA_HEAD_DIM
MIX_WIDTH = MLA_WIDTH + SWA_WIDTH
IN_SPLITS = (MLA_Q_RANK, MLA_KV_RANK, MLA_ROPE, SWA_WIDTH, SWA_KV_HEADS * SWA_HEAD_DIM, SWA_KV_HEADS * SWA_HEAD_DIM)
IN_COLS = sum(IN_SPLITS)
IN_OFFS = tuple(int(v) for v in np.cumsum((0,) + IN_SPLITS))
D_FF = 2816
MLA_SCALE = MLA_QK ** -0.5
LOG2E = 1.4426950408889634
LN2 = 0.6931471805599453
MLA_QSCALE = MLA_SCALE * LOG2E
SWA_SCALE = SWA_HEAD_DIM ** -0.5
NEG = -1e30

ADAM_LR = 0.001
ADAM_B1 = 0.9
ADAM_B2 = 0.999
ADAM_EPS = 1e-08
ADAM_WD = 0.01
ADAM_STEP = 10

N_SHARD = 4
N_DEV = 8
VMEM_LIMIT = 56 * 1024 * 1024
MESH = pl.DeviceIdType.MESH

WEIGHT_NAMES = ['ffn1_norm', 'ffn1_w_gate', 'ffn1_w_up', 'ffn1_w_down', 'mix_norm', 'w_in', 'mla_q_a_norm', 'mla_w_q_b',
                'mla_kv_a_norm', 'mla_w_kv_b', 'mla_q_norm', 'mla_k_norm', 'swa_q_norm', 'swa_k_norm', 'swa_sinks',
                'mla_out_norm', 'swa_out_norm', 'w_o', 'ffn2_norm', 'ffn2_w_gate', 'ffn2_w_up', 'ffn2_w_down']
BIG = {'ffn1_w_gate': ((D_MODEL, D_FF), 1), 'ffn1_w_up': ((D_MODEL, D_FF), 1), 'ffn1_w_down': ((D_FF, D_MODEL), 0),
       'w_in': ((D_MODEL, IN_COLS), 1), 'mla_w_q_b': ((MLA_Q_RANK, MLA_HEADS * MLA_QK), 1),
       'mla_w_kv_b': ((MLA_KV_RANK, MLA_HEADS * (MLA_NOPE + MLA_V)), 1), 'w_o': ((MIX_WIDTH, D_MODEL), 0),
       'ffn2_w_gate': ((D_MODEL, D_FF), 1), 'ffn2_w_up': ((D_MODEL, D_FF), 1), 'ffn2_w_down': ((D_FF, D_MODEL), 0)}
BIG_NAMES = [n for n in WEIGHT_NAMES if n in BIG]
SMALL_NAMES = [n for n in WEIGHT_NAMES if n not in BIG]

_pallas_call = pl.pallas_call


def _params(**kw):
    return pltpu.CompilerParams(vmem_limit_bytes=VMEM_LIMIT, **kw)


def _full(shape):
    n = len(shape)
    return pl.BlockSpec(shape, lambda *_: (0,) * n)


def _resident(shape):
    n = len(shape)
    return pl.BlockSpec(shape, lambda *_: (0,) * n, pipeline_mode=pl.Buffered(1))


@jax.custom_vjp
def _mm(a, w):
    return jnp.dot(a.astype(BF16), w, preferred_element_type=F32)


def _mm_fwd(a, w):
    return _mm(a, w), w


def _mm_bwd(w, dy):
    return lax.dot_general(dy.astype(BF16), w, (((1,), (1,)), ((), ())), preferred_element_type=F32), None


_mm.defvjp(_mm_fwd, _mm_bwd)


def _dot_nt(a, b):
    return lax.dot_general(a, b, (((1,), (1,)), ((), ())), preferred_element_type=F32)


def _dot_tn(a, b):
    return lax.dot_general(a, b, (((0,), (0,)), ((), ())), preferred_element_type=F32)


def _rms(t, g):
    return t * lax.rsqrt(jnp.mean(t * t, axis=-1, keepdims=True) + EPS) * g


def _rope(t, cos, sin):
    half = t.shape[-1] // 2
    t1, t2 = t[:, :half], t[:, half:]
    return jnp.concatenate([t1 * cos - t2 * sin, t2 * cos + t1 * sin], axis=-1)


def _sigmoid(z):
    return 1.0 / (1.0 + jnp.exp(-z))


def _row_tile(s, want):
    return min(want, s)


FF_CHUNK = 1408


def _ffn_fwd(x, g, wg, wu, wd, name):
    s = x.shape[0]
    tm = _row_tile(s, 256)

    def body(x_ref, g_ref, wg_ref, wu_ref, wd_ref, y_ref, gate_ref, up_ref):
        xv = x_ref[...]
        nb = _rms(xv, g_ref[...]).astype(BF16)
        acc = xv
        for c in range(0, D_FF, FF_CHUNK):
            gate = jnp.dot(nb, wg_ref[:, c:c + FF_CHUNK], preferred_element_type=F32)
            up = jnp.dot(nb, wu_ref[:, c:c + FF_CHUNK], preferred_element_type=F32)
            gate_ref[:, c:c + FF_CHUNK] = gate.astype(BF16)
            up_ref[:, c:c + FF_CHUNK] = up.astype(BF16)
            act = (gate * _sigmoid(gate) * up).astype(BF16)
            acc = acc + 0.5 * jnp.dot(act, wd_ref[c:c + FF_CHUNK, :], preferred_element_type=F32)
        y_ref[...] = acc

    return _pallas_call(
        body, name=name, grid=(s // tm,),
        in_specs=[pl.BlockSpec((tm, D_MODEL), lambda i: (i, 0)), _full((1, D_MODEL)),
                  _resident((D_MODEL, D_FF)), _resident((D_MODEL, D_FF)), _resident((D_FF, D_MODEL))],
        out_specs=[pl.BlockSpec((tm, D_MODEL), lambda i: (i, 0)), pl.BlockSpec((tm, D_FF), lambda i: (i, 0)),
                   pl.BlockSpec((tm, D_FF), lambda i: (i, 0))],
        out_shape=[jax.ShapeDtypeStruct((s, D_MODEL), F32), jax.ShapeDtypeStruct((s, D_FF), BF16),
                   jax.ShapeDtypeStruct((s, D_FF), BF16)],
        compiler_params=_params(dimension_semantics=("arbitrary",)),
    )(x, g, wg, wu, wd)


def _ffn_bwd(x, dy, gate, up, g, wg, wu, wd, name):
    s = x.shape[0]
    tm = _row_tile(s, 256)

    def body(x_ref, dy_ref, gate_ref, up_ref, g_ref, wg_ref, wu_ref, wd_ref,
             dx_ref, dgain_ref, n_ref, act_ref, dgate_ref, dup_ref):
        i = pl.program_id(0)
        xv = x_ref[...]
        dyv = dy_ref[...]
        gv = g_ref[...]
        r = lax.rsqrt(jnp.mean(xv * xv, axis=-1, keepdims=True) + EPS)
        xh = xv * r
        n_ref[...] = (xh * gv).astype(BF16)
        dyh = (0.5 * dyv).astype(BF16)
        dn = jnp.zeros_like(xv)
        for c in range(0, D_FF, FF_CHUNK):
            dact = _dot_nt(dyh, wd_ref[c:c + FF_CHUNK, :])
            gt = gate_ref[:, c:c + FF_CHUNK].astype(F32)
            u = up_ref[:, c:c + FF_CHUNK].astype(F32)
            sg = _sigmoid(gt)
            sl = gt * sg
            act_ref[:, c:c + FF_CHUNK] = (sl * u).astype(BF16)
            dup = (dact * sl).astype(BF16)
            dgate = (dact * u * (sg * (1.0 + gt * (1.0 - sg)))).astype(BF16)
            dup_ref[:, c:c + FF_CHUNK] = dup
            dgate_ref[:, c:c + FF_CHUNK] = dgate
            dn = dn + _dot_nt(dgate, wg_ref[:, c:c + FF_CHUNK]) + _dot_nt(dup, wu_ref[:, c:c + FF_CHUNK])
        part = jnp.sum(dn * xh, axis=0, keepdims=True)

        @pl.when(i == 0)
        def _():
            dgain_ref[...] = part

        @pl.when(i > 0)
        def _():
            dgain_ref[...] += part

        dxh = dn * gv
        dx_ref[...] = dyv + r * (dxh - xh * jnp.mean(dxh * xh, axis=-1, keepdims=True))

    row = lambda w: pl.BlockSpec((tm, w), lambda i: (i, 0))
    return _pallas_call(
        body, name=name, grid=(s // tm,),
        in_specs=[row(D_MODEL), row(D_MODEL), row(D_FF), row(D_FF), _full((1, D_MODEL)),
                  _resident((D_MODEL, D_FF)), _resident((D_MODEL, D_FF)), _resident((D_FF, D_MODEL))],
        out_specs=[row(D_MODEL), _full((1, D_MODEL)), row(D_MODEL), row(D_FF), row(D_FF), row(D_FF)],
        out_shape=[jax.ShapeDtypeStruct((s, D_MODEL), F32), jax.ShapeDtypeStruct((1, D_MODEL), F32),
                   jax.ShapeDtypeStruct((s, D_MODEL), BF16), jax.ShapeDtypeStruct((s, D_FF), BF16),
                   jax.ShapeDtypeStruct((s, D_FF), BF16), jax.ShapeDtypeStruct((s, D_FF), BF16)],
        compiler_params=_params(dimension_semantics=("arbitrary",)),
    )(x, dy, gate, up, g, wg, wu, wd)


def _store_col_shards(o_ref, acc, first_shard, n_here, width):
    for q in range(n_here):
        o_ref[q] = acc[:, (first_shard + q) * width:(first_shard + q + 1) * width].astype(BF16)


def _matmul_tn(a, b, scale, name, col_shards):
    t, m = a.shape
    n = b.shape[1]
    tk = _row_tile(t, 512)
    tn = n // 2
    per = n // N_SHARD
    nk = t // tk

    def body(a_ref, b_ref, o_ref, acc_ref):
        k = pl.program_id(1)
        bv = b_ref[...]
        if scale != 1.0:
            bv = bv.astype(F32) * scale
        part = _dot_tn(a_ref[...].astype(BF16), bv.astype(BF16))

        @pl.when(k == 0)
        def _():
            acc_ref[...] = part

        @pl.when(k > 0)
        def _():
            acc_ref[...] += part

        @pl.when(k == nk - 1)
        def _():
            if col_shards:
                _store_col_shards(o_ref, acc_ref[...], 0, tn // per, per)
            else:
                o_ref[...] = acc_ref[...].astype(BF16)

    if col_shards:
        out_spec = pl.BlockSpec((tn // per, m, per), lambda j, k: (j, 0, 0))
        out_shape = jax.ShapeDtypeStruct((N_SHARD, m, per), BF16)
    else:
        out_spec = pl.BlockSpec((m, tn), lambda j, k: (0, j))
        out_shape = jax.ShapeDtypeStruct((m, n), BF16)
    return _pallas_call(
        body, name=name, grid=(n // tn, nk),
        in_specs=[pl.BlockSpec((tk, m), lambda j, k: (k, 0)), pl.BlockSpec((tk, tn), lambda j, k: (k, j))],
        out_specs=out_spec, out_shape=out_shape, scratch_shapes=[pltpu.VMEM((m, tn), F32)],
        compiler_params=_params(dimension_semantics=("arbitrary", "arbitrary")),
    )(a, b)


def _pre_math(x, gm, gqa, gkva, gq, gk, gsq, gsk, taps, win, wqb, wkvb, cos, sin):
    h = _rms(x, gm)
    proj = _mm(h, win)
    if taps is not None:
        proj = proj + taps[0]
    o = IN_OFFS
    c_q, c_kv, k_pe = proj[:, o[0]:o[1]], proj[:, o[1]:o[2]], proj[:, o[2]:o[3]]
    q_s, k_s, v_s = proj[:, o[3]:o[4]], proj[:, o[4]:o[5]], proj[:, o[5]:o[6]]
    cqn = _rms(c_q, gqa)
    qa_all = _mm(cqn, wqb)
    ckvn = _rms(c_kv, gkva)
    kv_all = _mm(ckvn, wkvb)
    if taps is not None:
        qa_all = qa_all + taps[1]
        kv_all = kv_all + taps[2]
    q_a, k_a, v_a = [], [], []
    kvw = MLA_NOPE + MLA_V
    for hd in range(MLA_HEADS):
        qn = _rms(qa_all[:, hd * MLA_QK:(hd + 1) * MLA_QK], gq)
        k_nope = kv_all[:, hd * kvw:hd * kvw + MLA_NOPE]
        v_a.append(kv_all[:, hd * kvw + MLA_NOPE:(hd + 1) * kvw])
        kn = _rms(jnp.concatenate([k_nope, k_pe], axis=-1), gk)
        q_a.append(jnp.concatenate([qn[:, :MLA_NOPE], _rope(qn[:, MLA_NOPE:], cos, sin)], axis=-1))
        k_a.append(jnp.concatenate([kn[:, :MLA_NOPE], _rope(kn[:, MLA_NOPE:], cos, sin)], axis=-1))
    d = SWA_HEAD_DIM
    q_b = [_rope(_rms(q_s[:, hd * d:(hd + 1) * d], gsq), cos, sin) for hd in range(SWA_HEADS)]
    k_b = [_rope(_rms(k_s[:, j * d:(j + 1) * d], gsk), cos, sin) for j in range(SWA_KV_HEADS)]
    v_b = [v_s[:, j * d:(j + 1) * d] for j in range(SWA_KV_HEADS)]
    return (q_a, k_a, v_a, q_b, k_b, v_b), (h, cqn, ckvn)


_PRE_GAIN_WIDTHS = (D_MODEL, MLA_Q_RANK, MLA_KV_RANK, MLA_QK, MLA_QK, SWA_HEAD_DIM, SWA_HEAD_DIM)
_PRE_HEADS = ((MLA_HEADS, MLA_QK), (MLA_HEADS, MLA_QK), (MLA_HEADS, MLA_V),
              (SWA_HEADS, SWA_HEAD_DIM), (SWA_KV_HEADS, SWA_HEAD_DIM), (SWA_KV_HEADS, SWA_HEAD_DIM))


def _pre_fwd(x, gains, win, wqb, wkvb, cos, sin, name):
    s = x.shape[0]
    tm = _row_tile(s, 256)

    def body(x_ref, *refs):
        g_refs, (win_ref, wqb_ref, wkvb_ref, cos_ref, sin_ref), out_refs = refs[:7], refs[7:12], refs[12:]
        outs, _ = _pre_math(x_ref[...], *[g[...] for g in g_refs], None, win_ref[...], wqb_ref[...], wkvb_ref[...],
                            cos_ref[...], sin_ref[...])
        for idx, (ref, heads) in enumerate(zip(out_refs, outs)):
            for hd, val in enumerate(heads):
                ref[hd] = (val * MLA_QSCALE if idx == 0 else val).astype(BF16)

    heads_spec = lambda nh, w: pl.BlockSpec((nh, tm, w), lambda i: (0, i, 0))
    return _pallas_call(
        body, name=name, grid=(s // tm,),
        in_specs=[pl.BlockSpec((tm, D_MODEL), lambda i: (i, 0))] + [_full((1, w)) for w in _PRE_GAIN_WIDTHS]
        + [_resident(win.shape), _resident(wqb.shape), _resident(wkvb.shape),
           pl.BlockSpec((tm, MLA_ROPE // 2), lambda i: (i, 0)), pl.BlockSpec((tm, MLA_ROPE // 2), lambda i: (i, 0))],
        out_specs=[heads_spec(nh, w) for nh, w in _PRE_HEADS],
        out_shape=[jax.ShapeDtypeStruct((nh, s, w), BF16) for nh, w in _PRE_HEADS],
        compiler_params=_params(dimension_semantics=("arbitrary",)),
    )(x, *gains, win, wqb, wkvb, cos, sin)


def _pre_bwd(x, dx_res, cts, gains, win, wqb, wkvb, cos, sin, name):
    s = x.shape[0]
    tm = _row_tile(s, 256)
    tap_widths = (IN_COLS, MLA_HEADS * MLA_QK, MLA_HEADS * (MLA_NOPE + MLA_V))

    def body(x_ref, dxr_ref, *refs):
        ct_refs, g_refs = refs[:6], refs[6:13]
        win_ref, wqb_ref, wkvb_ref, cos_ref, sin_ref = refs[13:18]
        dx_ref, dg_refs, dw_refs, acc_refs = refs[18], refs[19:26], refs[26:29], refs[29:32]
        i = pl.program_id(0)
        win_v, wqb_v, wkvb_v, cos_v, sin_v = win_ref[...], wqb_ref[...], wkvb_ref[...], cos_ref[...], sin_ref[...]

        def f(xv, gm, gqa, gkva, gq, gk, gsq, gsk, t0, t1, t2):
            return _pre_math(xv, gm, gqa, gkva, gq, gk, gsq, gsk, (t0, t1, t2), win_v, wqb_v, wkvb_v, cos_v, sin_v)

        taps = [jnp.zeros((tm, w), F32) for w in tap_widths]
        _, vjp, acts = jax.vjp(f, x_ref[...], *[g[...] for g in g_refs], *taps, has_aux=True)
        ct = tuple([ref[hd] for hd in range(nh)] for ref, (nh, _) in zip(ct_refs, _PRE_HEADS))
        grads = vjp(ct)
        dx_ref[...] = grads[0] + dxr_ref[...]
        dws = [_dot_tn(a.astype(BF16), t.astype(BF16)) for a, t in zip(acts, grads[8:11])]

        @pl.when(i == 0)
        def _():
            for ref, val in zip(dg_refs, grads[1:8]):
                ref[...] = val
            for ref, val in zip(acc_refs, dws):
                ref[...] = val

        @pl.when(i > 0)
        def _():
            for ref, val in zip(dg_refs, grads[1:8]):
                ref[...] += val
            for ref, val in zip(acc_refs, dws):
                ref[...] += val

        @pl.when(i == s // tm - 1)
        def _():
            for ref, acc in zip(dw_refs, acc_refs):
                _store_col_shards(ref, acc[...], 0, N_SHARD, acc.shape[1] // N_SHARD)

    heads_spec = lambda nh, w: pl.BlockSpec((nh, tm, w), lambda i: (0, i, 0))
    row = pl.BlockSpec((tm, D_MODEL), lambda i: (i, 0))
    half = pl.BlockSpec((tm, MLA_ROPE // 2), lambda i: (i, 0))
    shard_shapes = [(N_SHARD, w.shape[0], w.shape[1] // N_SHARD) for w in (win, wqb, wkvb)]
    return _pallas_call(
        body, name=name, grid=(s // tm,),
        in_specs=[row, row] + [heads_spec(nh, w) for nh, w in _PRE_HEADS] + [_full((1, w)) for w in _PRE_GAIN_WIDTHS]
        + [_resident(win.shape), _resident(wqb.shape), _resident(wkvb.shape), half, half],
        out_specs=[row] + [_full((1, w)) for w in _PRE_GAIN_WIDTHS] + [_full(shp) for shp in shard_shapes],
        out_shape=[jax.ShapeDtypeStruct((s, D_MODEL), F32)] + [jax.ShapeDtypeStruct((1, w), F32) for w in _PRE_GAIN_WIDTHS]
        + [jax.ShapeDtypeStruct(shp, BF16) for shp in shard_shapes],
        scratch_shapes=[pltpu.VMEM(w.shape, F32) for w in (win, wqb, wkvb)],
        compiler_params=_params(dimension_semantics=("arbitrary",)),
    )(x, dx_res, *cts, *gains, win, wqb, wkvb, cos, sin)


def _post_math(oa, ob, ga, gb, wo):
    mixed = jnp.concatenate([_rms(jnp.concatenate(oa, axis=-1), ga), _rms(jnp.concatenate(ob, axis=-1), gb)], axis=-1)
    return _mm(mixed, wo), mixed


def _post_fwd(x, oa, ob, ga, gb, wo, name):
    s = x.shape[0]
    tm = _row_tile(s, 256)

    def body(x_ref, oa_ref, ob_ref, ga_ref, gb_ref, wo_ref, y_ref):
        y, _ = _post_math([oa_ref[hd] for hd in range(MLA_HEADS)], [ob_ref[hd] for hd in range(SWA_HEADS)],
                          ga_ref[...], gb_ref[...], wo_ref[...])
        y_ref[...] = x_ref[...] + y

    row = pl.BlockSpec((tm, D_MODEL), lambda i: (i, 0))
    return _pallas_call(
        body, name=name, grid=(s // tm,),
        in_specs=[row, pl.BlockSpec((MLA_HEADS, tm, MLA_V), lambda i: (0, i, 0)),
                  pl.BlockSpec((SWA_HEADS, tm, SWA_HEAD_DIM), lambda i: (0, i, 0)),
                  _full((1, MLA_WIDTH)), _full((1, SWA_WIDTH)), _resident(wo.shape)],
        out_specs=row, out_shape=jax.ShapeDtypeStruct((s, D_MODEL), F32),
        compiler_params=_params(dimension_semantics=("arbitrary",)),
    )(x, oa, ob, ga, gb, wo)


def _post_bwd(dy, oa, ob, ga, gb, wo, name):
    s = dy.shape[0]
    tm = _row_tile(s, 256)

    def body(dy_ref, oa_ref, ob_ref, ga_ref, gb_ref, wo_ref, doa_ref, dob_ref, dga_ref, dgb_ref, dwo_ref, acc_ref):
        i = pl.program_id(0)
        wo_v = wo_ref[...]
        dyv = dy_ref[...]

        def f(oa_l, ob_l, ga_v, gb_v):
            return _post_math(oa_l, ob_l, ga_v, gb_v, wo_v)

        _, vjp, mixed = jax.vjp(f, [oa_ref[hd] for hd in range(MLA_HEADS)], [ob_ref[hd] for hd in range(SWA_HEADS)],
                                ga_ref[...], gb_ref[...], has_aux=True)
        doa, dob, dga, dgb = vjp(dyv)
        for hd in range(MLA_HEADS):
            doa_ref[hd] = doa[hd]
        for hd in range(SWA_HEADS):
            dob_ref[hd] = dob[hd]
        dwo = _dot_tn(mixed.astype(BF16), dyv.astype(BF16))

        @pl.when(i == 0)
        def _():
            dga_ref[...] = dga
            dgb_ref[...] = dgb
            acc_ref[...] = dwo

        @pl.when(i > 0)
        def _():
            dga_ref[...] += dga
            dgb_ref[...] += dgb
            acc_ref[...] += dwo

        @pl.when(i == s // tm - 1)
        def _():
            dwo_ref[...] = acc_ref[...].astype(BF16)

    row = pl.BlockSpec((tm, D_MODEL), lambda i: (i, 0))
    oa_spec = pl.BlockSpec((MLA_HEADS, tm, MLA_V), lambda i: (0, i, 0))
    ob_spec = pl.BlockSpec((SWA_HEADS, tm, SWA_HEAD_DIM), lambda i: (0, i, 0))
    return _pallas_call(
        body, name=name, grid=(s // tm,),
        in_specs=[row, oa_spec, ob_spec, _full((1, MLA_WIDTH)), _full((1, SWA_WIDTH)), _resident(wo.shape)],
        out_specs=[oa_spec, ob_spec, _full((1, MLA_WIDTH)), _full((1, SWA_WIDTH)), _full(wo.shape)],
        out_shape=[jax.ShapeDtypeStruct((MLA_HEADS, s, MLA_V), F32), jax.ShapeDtypeStruct((SWA_HEADS, s, SWA_HEAD_DIM), F32),
                   jax.ShapeDtypeStruct((1, MLA_WIDTH), F32), jax.ShapeDtypeStruct((1, SWA_WIDTH), F32),
                   jax.ShapeDtypeStruct(wo.shape, BF16)],
        scratch_shapes=[pltpu.VMEM(wo.shape, F32)],
        compiler_params=_params(dimension_semantics=("arbitrary",)),
    )(dy, oa, ob, ga, gb, wo)


def _attn_tile(s):
    return 512 if s >= 2048 else 128


def _causal_mask(t):
    return lax.broadcasted_iota(jnp.int32, (t, t), 1) <= lax.broadcasted_iota(jnp.int32, (t, t), 0)


def _mla_fwd(q, k, v, name):
    nh, s, _ = q.shape
    t = _attn_tile(s)

    def body(q_ref, k_ref, v_ref, o_ref, lse_ref):
        qi = pl.program_id(1)
        qv = q_ref[...]

        def block(j, carry, masked):
            m, l, acc = carry
            start = pl.multiple_of(j * t, t)
            sc = _dot_nt(qv, k_ref[pl.ds(start, t), :])
            if masked:
                sc = jnp.where(_causal_mask(t), sc, NEG)
            m_new = jnp.maximum(m, jnp.max(sc, axis=-1, keepdims=True))
            alpha = jnp.exp2(m - m_new)
            p = jnp.exp2(sc - m_new)
            l = alpha * l + jnp.sum(p, axis=-1, keepdims=True)
            acc = alpha * acc + jnp.dot(p.astype(BF16), v_ref[pl.ds(start, t), :], preferred_element_type=F32)
            return m_new, l, acc

        init = (jnp.full((t, 1), NEG, F32), jnp.zeros((t, 1), F32), jnp.zeros((t, MLA_V), F32))
        carry = lax.fori_loop(0, qi, lambda j, c: block(j, c, False), init)
        m, l, acc = block(qi, carry, True)
        o_ref[...] = acc / l
        lse_ref[...] = m + jnp.log2(l)

    return _pallas_call(
        body, name=name, grid=(nh, s // t),
        in_specs=[pl.BlockSpec((None, t, MLA_QK), lambda h, i: (h, i, 0)), pl.BlockSpec((None, s, MLA_QK), lambda h, i: (h, 0, 0)),
                  pl.BlockSpec((None, s, MLA_V), lambda h, i: (h, 0, 0))],
        out_specs=[pl.BlockSpec((None, t, MLA_V), lambda h, i: (h, i, 0)), pl.BlockSpec((None, t, 1), lambda h, i: (h, i, 0))],
        out_shape=[jax.ShapeDtypeStruct((nh, s, MLA_V), F32), jax.ShapeDtypeStruct((nh, s, 1), F32)],
        compiler_params=_params(dimension_semantics=("arbitrary", "arbitrary")),
    )(q, k, v)


def _mla_bwd_dq(q, k, v, o, do, lse, name):
    nh, s, _ = q.shape
    t = _attn_tile(s)

    def body(q_ref, k_ref, v_ref, o_ref, do_ref, lse_ref, dq_ref, delta_ref):
        qi = pl.program_id(1)
        qv = q_ref[...]
        dov = do_ref[...]
        lse = lse_ref[...]
        delta = jnp.sum(dov * o_ref[...], axis=-1, keepdims=True)
        delta_ref[...] = delta
        dob = dov.astype(BF16)

        def block(j, dq, masked):
            start = pl.multiple_of(j * t, t)
            kb = k_ref[pl.ds(start, t), :]
            p = jnp.exp2(_dot_nt(qv, kb) - lse)
            if masked:
                p = jnp.where(_causal_mask(t), p, 0.0)
            dp = _dot_nt(dob, v_ref[pl.ds(start, t), :])
            ds = p * (dp - delta)
            return dq + jnp.dot(ds.astype(BF16), kb, preferred_element_type=F32)

        dq = lax.fori_loop(0, qi, lambda j, c: block(j, c, False), jnp.zeros((t, MLA_QK), F32))
        dq_ref[...] = block(qi, dq, True) * MLA_SCALE

    tile = lambda w: pl.BlockSpec((None, t, w), lambda h, i: (h, i, 0))
    whole = lambda w: pl.BlockSpec((None, s, w), lambda h, i: (h, 0, 0))
    return _pallas_call(
        body, name=name, grid=(nh, s // t),
        in_specs=[tile(MLA_QK), whole(MLA_QK), whole(MLA_V), tile(MLA_V), tile(MLA_V), tile(1)],
        out_specs=[tile(MLA_QK), tile(1)],
        out_shape=[jax.ShapeDtypeStruct((nh, s, MLA_QK), F32), jax.ShapeDtypeStruct((nh, s, 1), F32)],
        compiler_params=_params(dimension_semantics=("arbitrary", "arbitrary")),
    )(q, k, v, o, do, lse)


def _mla_bwd_dkv(q, k, v, do, lse_row, delta_row, name):
    nh, s, _ = q.shape
    t = _attn_tile(s)
    nq = s // t

    def body(q_ref, k_ref, v_ref, do_ref, lse_ref, delta_ref, dk_ref, dv_ref):
        kj = pl.program_id(1)
        kv_, vv = k_ref[...], v_ref[...]

        def block(i, carry, masked):
            dk, dv = carry
            start = pl.multiple_of(i * t, t)
            qb = q_ref[pl.ds(start, t), :]
            dob = do_ref[pl.ds(start, t), :].astype(BF16)
            p = jnp.exp2(_dot_nt(kv_, qb) - lse_ref[pl.ds(i, 1), :])
            if masked:
                p = jnp.where(lax.broadcasted_iota(jnp.int32, (t, t), 0) <= lax.broadcasted_iota(jnp.int32, (t, t), 1), p, 0.0)
            dv = dv + jnp.dot(p.astype(BF16), dob, preferred_element_type=F32)
            dp = _dot_nt(vv, dob)
            ds = p * (dp - delta_ref[pl.ds(i, 1), :])
            dk = dk + jnp.dot(ds.astype(BF16), qb, preferred_element_type=F32)
            return dk, dv

        carry = block(kj, (jnp.zeros((t, MLA_QK), F32), jnp.zeros((t, MLA_V), F32)), True)
        dk, dv = lax.fori_loop(kj + 1, nq, lambda i, c: block(i, c, False), carry)
        dk_ref[...] = dk * LN2
        dv_ref[...] = dv

    tile = lambda w: pl.BlockSpec((None, t, w), lambda h, j: (h, j, 0))
    whole = lambda w: pl.BlockSpec((None, s, w), lambda h, j: (h, 0, 0))
    rows = pl.BlockSpec((None, nq, t), lambda h, j: (h, 0, 0))
    return _pallas_call(
        body, name=name, grid=(nh, nq),
        in_specs=[whole(MLA_QK), tile(MLA_QK), tile(MLA_V), whole(MLA_V), rows, rows],
        out_specs=[tile(MLA_QK), tile(MLA_V)],
        out_shape=[jax.ShapeDtypeStruct((nh, s, MLA_QK), F32), jax.ShapeDtypeStruct((nh, s, MLA_V), F32)],
        compiler_params=_params(dimension_semantics=("arbitrary", "arbitrary")),
    )(q, k, v, do, lse_row, delta_row)


def _swa_math(q_l, kband, vband, sink_l, first):
    q_rel = lax.broadcasted_iota(jnp.int32, (BLOCK, 2 * BLOCK), 0) + BLOCK
    k_rel = lax.broadcasted_iota(jnp.int32, (BLOCK, 2 * BLOCK), 1)
    dist = q_rel - k_rel
    valid = (dist >= 0) & (dist < BLOCK) & ((k_rel >= BLOCK) | jnp.logical_not(first))
    kb, vb = kband.astype(BF16), vband.astype(BF16)
    outs, psinks = [], []
    for qh, sink in zip(q_l, sink_l):
        sc = jnp.where(valid, _dot_nt(qh.astype(BF16), kb) * SWA_SCALE, NEG)
        m = lax.stop_gradient(jnp.maximum(jnp.max(sc, axis=-1, keepdims=True), sink))
        e = jnp.exp(sc - m)
        es = jnp.exp(sink - m)
        den = jnp.sum(e, axis=-1, keepdims=True) + es
        outs.append(jnp.dot((e / den).astype(BF16), vb, preferred_element_type=F32))
        psinks.append(es / den)
    return outs, psinks


def _swa_tile(s):
    return min(s, 4 * BLOCK)


def _swa_specs(tq):
    nb = tq // BLOCK
    grp = lambda: pl.BlockSpec((SWA_GROUP, tq, SWA_HEAD_DIM), lambda j, i: (j, i, 0))
    main = pl.BlockSpec((None, tq, SWA_HEAD_DIM), lambda j, i: (j, i, 0))
    tail = pl.BlockSpec((None, BLOCK, SWA_HEAD_DIM), lambda j, i: (j, nb * (i + 1), 0))
    sink = pl.BlockSpec((None, SWA_GROUP, 128), lambda j, i: (j, 0, 0))
    return grp, main, tail, sink


def _swa_fwd(q, kpad, vpad, sinks, name):
    _, s, _ = q.shape
    tq = _swa_tile(s)
    grp, main, tail, sink = _swa_specs(tq)

    def body(q_ref, km_ref, kt_ref, vm_ref, vt_ref, sink_ref, o_ref):
        i = pl.program_id(1)
        kall = jnp.concatenate([km_ref[...], kt_ref[...]], axis=0)
        vall = jnp.concatenate([vm_ref[...], vt_ref[...]], axis=0)
        sk = sink_ref[...]
        sink_l = [sk[g:g + 1, 0:1] for g in range(SWA_GROUP)]
        for b in range(tq // BLOCK):
            lo = b * BLOCK
            outs, _ = _swa_math([q_ref[g, lo:lo + BLOCK, :] for g in range(SWA_GROUP)], kall[lo:lo + 2 * BLOCK],
                                vall[lo:lo + 2 * BLOCK], sink_l, i == 0 if b == 0 else False)
            for g in range(SWA_GROUP):
                o_ref[g, lo:lo + BLOCK, :] = outs[g]

    return _pallas_call(
        body, name=name, grid=(SWA_KV_HEADS, s // tq),
        in_specs=[grp(), main, tail, main, tail, sink], out_specs=grp(),
        out_shape=jax.ShapeDtypeStruct((SWA_HEADS, s, SWA_HEAD_DIM), F32),
        compiler_params=_params(dimension_semantics=("arbitrary", "arbitrary")),
    )(q, kpad, kpad, vpad, vpad, sinks)


def _swa_bwd(q, kpad, vpad, sinks, do, name):
    _, s, _ = q.shape
    tq = _swa_tile(s)
    grp, main, tail, sink = _swa_specs(tq)

    def body(q_ref, km_ref, kt_ref, vm_ref, vt_ref, sink_ref, do_ref, dq_ref, dk_ref, dv_ref, dsink_ref):
        i = pl.program_id(1)
        sk = sink_ref[...]
        sink_l = [sk[g:g + 1, 0:1] for g in range(SWA_GROUP)]
        kall = jnp.concatenate([km_ref[...], kt_ref[...]], axis=0).astype(F32)
        vall = jnp.concatenate([vm_ref[...], vt_ref[...]], axis=0).astype(F32)

        @pl.when(i == 0)
        def _():
            dk_ref[...] = jnp.zeros_like(dk_ref)
            dv_ref[...] = jnp.zeros_like(dv_ref)
            dsink_ref[...] = jnp.zeros_like(dsink_ref)

        dsk = [jnp.zeros((1, 1), F32) for _ in range(SWA_GROUP)]
        for b in range(tq // BLOCK):
            lo = b * BLOCK
            first = i == 0 if b == 0 else False

            def f(ql, kb, vb, first=first):
                return _swa_math(ql, kb, vb, sink_l, first)

            q_l = [q_ref[g, lo:lo + BLOCK, :].astype(F32) for g in range(SWA_GROUP)]
            outs, vjp, psinks = jax.vjp(f, q_l, kall[lo:lo + 2 * BLOCK], vall[lo:lo + 2 * BLOCK], has_aux=True)
            do_l = [do_ref[g, lo:lo + BLOCK, :] for g in range(SWA_GROUP)]
            dq_l, dkb, dvb = vjp(do_l)
            for g in range(SWA_GROUP):
                dq_ref[g, lo:lo + BLOCK, :] = dq_l[g]
                dsk[g] = dsk[g] - jnp.sum(psinks[g] * jnp.sum(do_l[g] * outs[g], axis=-1, keepdims=True), axis=0, keepdims=True)
            rows = pl.ds(pl.multiple_of(i * tq, BLOCK) + lo, 2 * BLOCK)
            dk_ref[rows, :] += dkb
            dv_ref[rows, :] += dvb
        dsink_ref[...] += jnp.concatenate([jnp.broadcast_to(d, (1, 128)) for d in dsk]
                                          + [jnp.zeros((8 - SWA_GROUP, 128), F32)], axis=0)

    acc = pl.BlockSpec((None, s + BLOCK, SWA_HEAD_DIM), lambda j, i: (j, 0, 0))
    return _pallas_call(
        body, name=name, grid=(SWA_KV_HEADS, s // tq),
        in_specs=[grp(), main, tail, main, tail, sink, grp()],
        out_specs=[grp(), acc, acc, pl.BlockSpec((None, 8, 128), lambda j, i: (j, 0, 0))],
        out_shape=[jax.ShapeDtypeStruct((SWA_HEADS, s, SWA_HEAD_DIM), F32),
                   jax.ShapeDtypeStruct((SWA_KV_HEADS, s + BLOCK, SWA_HEAD_DIM), F32),
                   jax.ShapeDtypeStruct((SWA_KV_HEADS, s + BLOCK, SWA_HEAD_DIM), F32),
                   jax.ShapeDtypeStruct((SWA_KV_HEADS, 8, 128), F32)],
        compiler_params=_params(dimension_semantics=("arbitrary", "arbitrary")),
    )(q, kpad, kpad, vpad, vpad, sinks, do)


def _loss_head(y, target, name):
    s = y.shape[0]
    tm = _row_tile(s, 512)

    def body(y_ref, t_ref, dy_ref, loss_ref):
        i = pl.program_id(0)
        err = y_ref[...] - t_ref[...]
        dy_ref[...] = err * (1.0 / D_MODEL)
        part = jnp.broadcast_to(0.5 * jnp.sum(jnp.mean(err * err, axis=-1, keepdims=True), axis=0, keepdims=True), (1, 128))

        @pl.when(i == 0)
        def _():
            loss_ref[...] = part

        @pl.when(i > 0)
        def _():
            loss_ref[...] += part

    row = pl.BlockSpec((tm, D_MODEL), lambda i: (i, 0))
    return _pallas_call(
        body, name=name, grid=(s // tm,), in_specs=[row, row], out_specs=[row, _full((1, 128))],
        out_shape=[jax.ShapeDtypeStruct((s, D_MODEL), F32), jax.ShapeDtypeStruct((1, 128), F32)],
        compiler_params=_params(dimension_semantics=("arbitrary",)),
    )(y, target)


def _adamw(w, g, m, v, name):
    rows, cols = w.shape
    tr = rows
    for cand in (512, 256, 128, 64, 32, 16, 8):
        if rows % cand == 0 and rows > cand:
            tr = cand
            break

    def body(w_ref, g_ref, m_ref, v_ref, d_ref, nm_ref, nv_ref):
        gv = g_ref[...]
        nm = ADAM_B1 * m_ref[...] + (1.0 - ADAM_B1) * gv
        nv = ADAM_B2 * v_ref[...] + (1.0 - ADAM_B2) * (gv * gv)
        m_hat = nm / (1.0 - ADAM_B1 ** ADAM_STEP)
        v_hat = nv / (1.0 - ADAM_B2 ** ADAM_STEP)
        d_ref[...] = -ADAM_LR * (m_hat / (jnp.sqrt(v_hat) + ADAM_EPS) + ADAM_WD * w_ref[...])
        nm_ref[...] = nm
        nv_ref[...] = nv

    blk = pl.BlockSpec((tr, cols), lambda i: (i, 0))
    return _pallas_call(
        body, name=name, grid=(rows // tr,), in_specs=[blk] * 4, out_specs=[blk] * 3,
        out_shape=[jax.ShapeDtypeStruct((rows, cols), F32)] * 3,
        compiler_params=_params(dimension_semantics=("arbitrary",)),
    )(w, g, m, v)


def _position():
    return lax.axis_index("x"), lax.axis_index("y"), lax.axis_index("c")


def _remote(src, dst, send_sems, recv_sems, k, to):
    return pltpu.make_async_remote_copy(src_ref=src, dst_ref=dst, send_sem=send_sems.at[k], recv_sem=recv_sems.at[k],
                                        device_id=to, device_id_type=MESH)


_HBM = pl.BlockSpec(memory_space=pltpu.HBM)


def _all_gather_layers(mine, name):
    na = len(mine)

    def body(*refs):
        ins, outs, (send_sems, recv_sems) = refs[:na], refs[na:2 * na], refs[2 * na:]
        x, y, c = _position()
        me, sibling = (x, y, c), (x, y, 1 - c)
        chips = [(1 - x, y), (x, 1 - y), (1 - x, 1 - y)]

        def slot(a, px, py, pc):
            return outs[a].at[4 * px + 2 * py + pc]

        def cp(a, k, block, to, src=None):
            return _remote(slot(a, *block) if src is None else src, slot(a, *block), send_sems, recv_sems, 6 * a + k, to)

        first = [cp(a, j, me, (*chip, c), src=ins[a].at[c]) for a in range(na) for j, chip in enumerate(chips)]
        for f in first:
            f.start()
        passed = []
        for j, chip in enumerate(chips):
            for a in range(na):
                cp(a, j, (*chip, c), me).wait_recv()
                fwd = cp(a, 3 + j, (*chip, c), sibling)
                fwd.start()
                passed.append(fwd)
        for a in range(na):
            for j, chip in enumerate(chips):
                cp(a, 3 + j, (*chip, 1 - c), me).wait_recv()
        for f in first + passed:
            f.wait_send()

    return _pallas_call(
        body, name=name, in_specs=[_HBM] * na, out_specs=[_HBM] * na,
        out_shape=[jax.ShapeDtypeStruct((N_DEV,) + m.shape[1:], m.dtype) for m in mine],
        scratch_shapes=[pltpu.SemaphoreType.DMA((6 * na,)), pltpu.SemaphoreType.DMA((6 * na,))],
    )(*mine)


def _sibling_exchange(parts, name, other_layer):
    na = len(parts)

    def body(*refs):
        ins, outs, (send_sems, recv_sems) = refs[:na], refs[na:2 * na], refs[2 * na:]
        x, y, c = _position()
        copies = [_remote(ins[a].at[1 - c] if other_layer else ins[a], outs[a], send_sems, recv_sems, a, (x, y, 1 - c))
                  for a in range(na)]
        for cp in copies:
            cp.start()
        for cp in copies:
            cp.wait()

    return _pallas_call(
        body, name=name, in_specs=[_HBM] * na, out_specs=[_HBM] * na,
        out_shape=[jax.ShapeDtypeStruct(p.shape[1:] if other_layer else p.shape, p.dtype) for p in parts],
        scratch_shapes=[pltpu.SemaphoreType.DMA((na,)), pltpu.SemaphoreType.DMA((na,))],
    )(*parts)


def _scatter_to_chips(parts, name):
    na = len(parts)

    def body(*refs):
        ins, outs, (send_sems, recv_sems) = refs[:na], refs[na:2 * na], refs[2 * na:]
        x, y, c = _position()
        chips = [(1 - x, y), (x, 1 - y), (1 - x, 1 - y)]
        copies = [_remote(ins[a].at[2 * px + py], outs[a].at[j], send_sems, recv_sems, 3 * a + j, (px, py, c))
                  for a in range(na) for j, (px, py) in enumerate(chips)]
        for cp in copies:
            cp.start()
        for cp in copies:
            cp.wait()

    return _pallas_call(
        body, name=name, in_specs=[_HBM] * na, out_specs=[_HBM] * na,
        out_shape=[jax.ShapeDtypeStruct((3,) + p.shape[1:], p.dtype) for p in parts],
        scratch_shapes=[pltpu.SemaphoreType.DMA((3 * na,)), pltpu.SemaphoreType.DMA((3 * na,))],
    )(*parts)


def _assemble(g4, mine, name, side_by_side):
    _, nl, r, w = g4.shape
    tr = next(cand for cand in (256, 128) if r % cand == 0)

    def body(in_ref, mine_ref, out_ref):
        chip = 2 * lax.axis_index("x") + lax.axis_index("y")
        blocks = [jnp.where(chip == sh, mine_ref[...], in_ref[sh]) for sh in range(N_SHARD)]
        if side_by_side:
            out_ref[...] = jnp.concatenate(blocks, axis=-1)
        else:
            for sh in range(N_SHARD):
                out_ref[sh] = blocks[sh]

    if side_by_side:
        out_spec = pl.BlockSpec((None, tr, N_SHARD * w), lambda l, i: (l, i, 0))
        out_shape = jax.ShapeDtypeStruct((nl, r, N_SHARD * w), g4.dtype)
    else:
        out_spec = pl.BlockSpec((None, N_SHARD, tr, w), lambda l, i: (l, 0, i, 0))
        out_shape = jax.ShapeDtypeStruct((nl, N_SHARD, r, w), g4.dtype)
    return _pallas_call(
        body, name=name, grid=(nl, r // tr),
        in_specs=[pl.BlockSpec((N_SHARD, None, tr, w), lambda l, i: (0, l, i, 0)),
                  pl.BlockSpec((None, tr, w), lambda l, i: (l, i, 0))],
        out_specs=out_spec, out_shape=out_shape,
        compiler_params=_params(dimension_semantics=("arbitrary", "arbitrary")),
    )(g4, mine)


def _all_reduce_small(vec, name):
    r, l = vec.shape

    def body(v_ref, out_ref, gath_ref, send_sems, recv_sems):
        x, y, c = _position()
        me = 4 * x + 2 * y + c
        gath_ref[me] = v_ref[...]
        copies = []
        for k in range(1, N_DEV):
            to = (x ^ (k >> 2), y ^ ((k >> 1) & 1), c ^ (k & 1))
            copies.append(_remote(gath_ref.at[me], gath_ref.at[me], send_sems, recv_sems, k - 1, to))
        for cp in copies:
            cp.start()
        for k in range(1, N_DEV):
            frm = 4 * (x ^ (k >> 2)) + 2 * (y ^ ((k >> 1) & 1)) + (c ^ (k & 1))
            _remote(gath_ref.at[frm], gath_ref.at[frm], send_sems, recv_sems, k - 1, (x, y, c)).wait_recv()
        for cp in copies:
            cp.wait_send()
        total = gath_ref[0]
        for d in range(1, N_DEV):
            total = total + gath_ref[d]
        out_ref[...] = total

    vm = pl.BlockSpec(memory_space=pltpu.VMEM)
    return _pallas_call(
        body, name=name, in_specs=[vm], out_specs=vm, out_shape=jax.ShapeDtypeStruct((r, l), F32),
        scratch_shapes=[pltpu.VMEM((N_DEV, r, l), F32), pltpu.SemaphoreType.DMA((N_DEV - 1,)),
                        pltpu.SemaphoreType.DMA((N_DEV - 1,))],
    )(vec)


def _sum_blocks(blocks, out_dtype, name):
    m, w = blocks[0].shape
    tr = next(cand for cand in (512, 256, 128) if m % cand == 0)

    def body(*refs):
        total = refs[0][...].astype(F32)
        for ref in refs[1:-1]:
            total = total + ref[...].astype(F32)
        refs[-1][...] = total.astype(out_dtype)

    blk = pl.BlockSpec((tr, w), lambda i: (i, 0))
    return _pallas_call(
        body, name=name, grid=(m // tr,), in_specs=[blk] * len(blocks), out_specs=blk,
        out_shape=jax.ShapeDtypeStruct((m, w), out_dtype),
        compiler_params=_params(dimension_semantics=("arbitrary",)),
    )(*blocks)


GROUPS = (('ffn1_w_gate', 'ffn1_w_up', 'ffn2_w_gate', 'ffn2_w_up'), ('ffn1_w_down', 'ffn2_w_down', 'w_o'),
          ('w_in',), ('mla_w_q_b',), ('mla_w_kv_b',))


def _shard_rows(name):
    shape, axis = BIG[name]
    return shape[0] // N_SHARD if axis == 0 else shape[0]


def _group_row_offsets(group):
    return [int(v) for v in np.cumsum([0] + [_shard_rows(n) for n in group])]


def _rope_tables(s):
    pos = jnp.arange(s, dtype=F32)
    inv = 1.0 / (ROPE_THETA ** (jnp.arange(0, MLA_ROPE, 2, dtype=F32) / MLA_ROPE))
    ang = pos[:, None] * inv[None, :]
    return jnp.cos(ang), jnp.sin(ang)


def _local_step(x, target, small, big):
    s = x.shape[0]
    cos, sin = _rope_tables(s)
    row = lambda name, l: small[name][l][None, :]
    saved = []
    for l in range(DEPTH):
        sv = {'x0': x}
        x, sv['g1'], sv['u1'] = _ffn_fwd(x, row('ffn1_norm', l), big['ffn1_w_gate'][l], big['ffn1_w_up'][l],
                                        big['ffn1_w_down'][l], f"ffn1_fwd_{l}")
        sv['x1'] = x
        gains = [row(n, l) for n in ('mix_norm', 'mla_q_a_norm', 'mla_kv_a_norm', 'mla_q_norm', 'mla_k_norm',
                                     'swa_q_norm', 'swa_k_norm')]
        q_a, k_a, v_a, q_b, k_b, v_b = _pre_fwd(x, gains, big['w_in'][l], big['mla_w_q_b'][l], big['mla_w_kv_b'][l],
                                                cos, sin, f"pre_fwd_{l}")
        o_a, lse = _mla_fwd(q_a, k_a, v_a, f"mla_fwd_{l}")
        kpad = jnp.pad(k_b, ((0, 0), (BLOCK, 0), (0, 0)))
        vpad = jnp.pad(v_b, ((0, 0), (BLOCK, 0), (0, 0)))
        sinks = jnp.broadcast_to(small['swa_sinks'][l].reshape(SWA_KV_HEADS, SWA_GROUP, 1), (SWA_KV_HEADS, SWA_GROUP, 128))
        o_b = _swa_fwd(q_b, kpad, vpad, sinks, f"swa_fwd_{l}")
        sv.update(gains=gains, q_a=q_a, k_a=k_a, v_a=v_a, q_b=q_b, kpad=kpad, vpad=vpad, sinks=sinks, o_a=o_a, lse=lse, o_b=o_b)
        x = _post_fwd(x, o_a, o_b, row('mla_out_norm', l), row('swa_out_norm', l), big['w_o'][l], f"post_fwd_{l}")
        sv['x2'] = x
        x, sv['g2'], sv['u2'] = _ffn_fwd(x, row('ffn2_norm', l), big['ffn2_w_gate'][l], big['ffn2_w_up'][l],
                                        big['ffn2_w_down'][l], f"ffn2_fwd_{l}")
        saved.append(sv)

    dx, loss = _loss_head(x, target, "loss_head")

    gs = {n: [None] * DEPTH for n in SMALL_NAMES}
    gb = {n: [None] * DEPTH for n in BIG_NAMES}
    t = _attn_tile(s)
    for l in reversed(range(DEPTH)):
        sv = saved[l]

        def ffn_back(tag, xin, dy, gate, up):
            dxi, dgain, nb, act, dgate, dup = _ffn_bwd(xin, dy, gate, up, row(tag + '_norm', l), big[tag + '_w_gate'][l],
                                                      big[tag + '_w_up'][l], big[tag + '_w_down'][l], f"{tag}_bwd_{l}")
            gs[tag + '_norm'][l] = dgain[0]
            gb[tag + '_w_gate'][l] = _matmul_tn(nb, dgate, 1.0, f"{tag}_dwg_{l}", True)
            gb[tag + '_w_up'][l] = _matmul_tn(nb, dup, 1.0, f"{tag}_dwu_{l}", True)
            gb[tag + '_w_down'][l] = _matmul_tn(act, dy, 0.5, f"{tag}_dwd_{l}", False).reshape(N_SHARD, D_FF // N_SHARD, D_MODEL)
            return dxi

        dx = ffn_back('ffn2', sv['x2'], dx, sv['g2'], sv['u2'])
        do_a, do_b, dga, dgb, dwo = _post_bwd(dx, sv['o_a'], sv['o_b'], row('mla_out_norm', l), row('swa_out_norm', l),
                                              big['w_o'][l], f"post_bwd_{l}")
        gs['mla_out_norm'][l], gs['swa_out_norm'][l] = dga[0], dgb[0]
        gb['w_o'][l] = dwo.reshape(N_SHARD, MIX_WIDTH // N_SHARD, D_MODEL)
        dq_a, delta = _mla_bwd_dq(sv['q_a'], sv['k_a'], sv['v_a'], sv['o_a'], do_a, sv['lse'], f"mla_dq_{l}")
        dk_a, dv_a = _mla_bwd_dkv(sv['q_a'], sv['k_a'], sv['v_a'], do_a, sv['lse'].reshape(MLA_HEADS, s // t, t),
                                  delta.reshape(MLA_HEADS, s // t, t), f"mla_dkv_{l}")
        dq_b, dkpad, dvpad, dsink = _swa_bwd(sv['q_b'], sv['kpad'], sv['vpad'], sv['sinks'], do_b, f"swa_bwd_{l}")
        gs['swa_sinks'][l] = dsink[:, :SWA_GROUP, 0].reshape(SWA_HEADS)
        cts = [dq_a, dk_a, dv_a, dq_b, dkpad[:, BLOCK:], dvpad[:, BLOCK:]]
        outs = _pre_bwd(sv['x1'], dx, cts, sv['gains'], big['w_in'][l], big['mla_w_q_b'][l], big['mla_w_kv_b'][l],
                        cos, sin, f"pre_bwd_{l}")
        dx = outs[0]
        for n, val in zip(('mix_norm', 'mla_q_a_norm', 'mla_kv_a_norm', 'mla_q_norm', 'mla_k_norm', 'swa_q_norm', 'swa_k_norm'),
                          outs[1:8]):
            gs[n][l] = val[0]
        gb['w_in'][l], gb['mla_w_q_b'][l], gb['mla_w_kv_b'][l] = outs[8:11]
        dx = ffn_back('ffn1', sv['x0'], dx, sv['g1'], sv['u1'])
    return loss, dx, gs, gb


def kernel(x, ffn1_norm, ffn1_w_gate, ffn1_w_up, ffn1_w_down, mix_norm, w_in, mla_q_a_norm, mla_w_q_b, mla_kv_a_norm, mla_w_kv_b, mla_q_norm, mla_k_norm, swa_q_norm, swa_k_norm, swa_sinks, mla_out_norm, swa_out_norm, w_o, ffn2_norm, ffn2_w_gate, ffn2_w_up, ffn2_w_down, loss_target, m_ffn1_norm, m_ffn1_w_gate, m_ffn1_w_up, m_ffn1_w_down, m_mix_norm, m_w_in, m_mla_q_a_norm, m_mla_w_q_b, m_mla_kv_a_norm, m_mla_w_kv_b, m_mla_q_norm, m_mla_k_norm, m_swa_q_norm, m_swa_k_norm, m_swa_sinks, m_mla_out_norm, m_swa_out_norm, m_w_o, m_ffn2_norm, m_ffn2_w_gate, m_ffn2_w_up, m_ffn2_w_down, v_ffn1_norm, v_ffn1_w_gate, v_ffn1_w_up, v_ffn1_w_down, v_mix_norm, v_w_in, v_mla_q_a_norm, v_mla_w_q_b, v_mla_kv_a_norm, v_mla_w_kv_b, v_mla_q_norm, v_mla_k_norm, v_swa_q_norm, v_swa_k_norm, v_swa_sinks, v_mla_out_norm, v_swa_out_norm, v_w_o, v_ffn2_norm, v_ffn2_w_gate, v_ffn2_w_up, v_ffn2_w_down):
    args = dict(locals())
    weights = {n: args[n] for n in WEIGHT_NAMES}
    mom_m = {n: args["m_" + n] for n in WEIGHT_NAMES}
    mom_v = {n: args["v_" + n] for n in WEIGHT_NAMES}

    mine = [jnp.concatenate([weights[n].astype(BF16) for n in group], axis=1) for group in GROUPS]
    c = lax.axis_index("c")
    chip = 2 * lax.axis_index("x") + lax.axis_index("y")
    gathered = [g.reshape((N_SHARD, DEPTH) + g.shape[1:]) for g in _all_gather_layers(mine, "gather_weights")]
    big = {}
    for gi, group in enumerate(GROUPS):
        offs = _group_row_offsets(group)
        col_sharded = BIG[group[0]][1] == 1
        full = _assemble(gathered[gi], mine[gi], f"assemble_{gi}", col_sharded)
        for i, n in enumerate(group):
            if col_sharded:
                big[n] = [full[l, offs[i]:offs[i + 1]] for l in range(DEPTH)]
            else:
                big[n] = [full[l, :, offs[i]:offs[i + 1]].reshape(BIG[n][0]) for l in range(DEPTH)]
    small = {n: weights[n] for n in SMALL_NAMES}

    loss, dx, gs, gb = _local_step(x[0], loss_target[0], small, big)

    small_flat = jnp.concatenate([jnp.stack(gs[n]).reshape(-1) for n in SMALL_NAMES] + [loss[0, :1]])
    n_small = small_flat.shape[0]
    lanes = -(-n_small // (8 * 128)) * 128
    small_sum = _all_reduce_small(jnp.pad(small_flat, (0, 8 * lanes - n_small)).reshape(8, lanes), "reduce_small").reshape(-1)
    grads = {}
    off = 0
    for n in SMALL_NAMES:
        cnt = int(np.prod(weights[n].shape))
        grads[n] = small_sum[off:off + cnt].reshape(weights[n].shape)
        off += cnt
    loss_out = small_sum[off]

    parts = [jnp.stack([jnp.concatenate([gb[n][l] for n in group], axis=1) for l in range(DEPTH)]) for group in GROUPS]
    from_sibling = _sibling_exchange(parts, "swap_layers", True)
    chip_sums = []
    for gi, (p, got) in enumerate(zip(parts, from_sibling)):
        kept = lax.dynamic_index_in_dim(p, c, axis=0, keepdims=False)
        rows = N_SHARD * p.shape[2]
        pair = _sum_blocks([kept.reshape(rows, -1), got.reshape(rows, -1)], BF16, f"sum_pair_{gi}")
        chip_sums.append(pair.reshape(p.shape[1:]))
    from_chips = _scatter_to_chips(chip_sums, "scatter_chips")
    halves = []
    for gi, (cs, got) in enumerate(zip(chip_sums, from_chips)):
        own = lax.dynamic_index_in_dim(cs, chip, axis=0, keepdims=False)
        halves.append(_sum_blocks([own, got[0], got[1], got[2]], F32, f"sum_chips_{gi}"))
    others = _sibling_exchange(halves, "share_layers", False)
    for gi, group in enumerate(GROUPS):
        offs = _group_row_offsets(group)
        both = jnp.stack([jnp.where(c == 0, halves[gi], others[gi]), jnp.where(c == 0, others[gi], halves[gi])])
        for i, n in enumerate(group):
            grads[n] = both[:, offs[i]:offs[i + 1]]

    deltas, new_m, new_v = {}, {}, {}
    for n in WEIGHT_NAMES:
        shp = weights[n].shape
        two_d = (DEPTH, shp[-1]) if len(shp) == 2 else (shp[0] * shp[1], shp[2])
        d, nm, nv = _adamw(weights[n].reshape(two_d), grads[n].reshape(two_d), mom_m[n].reshape(two_d),
                           mom_v[n].reshape(two_d), f"adamw_{n}")
        deltas[n], new_m[n], new_v[n] = d.reshape(shp), nm.reshape(shp), nv.reshape(shp)

    return (loss_out, dx[None], *[grads[n] for n in WEIGHT_NAMES], *[deltas[n] for n in WEIGHT_NAMES],
            *[new_m[n] for n in WEIGHT_NAMES], *[new_v[n] for n in WEIGHT_NAMES])
```

```python
import functools

import numpy as np
import jax
import jax.numpy as jnp
from jax import lax
from jax.experimental import pallas as pl
from jax.experimental.pallas import tpu as pltpu

F32 = jnp.float32
BF16 = jnp.bfloat16

D_MODEL = 1024
DEPTH = 2
EPS = 1e-6
ROPE_THETA = 10000.0
BLOCK = 128
MLA_HEADS = 4
MLA_Q_RANK = 256
MLA_KV_RANK = 128
MLA_NOPE = 128
MLA_ROPE = 64
MLA_V = 128
MLA_QK = MLA_NOPE + MLA_ROPE
MLA_WIDTH = MLA_HEADS * MLA_V
SWA_HEADS = 8
SWA_KV_HEADS = 2
SWA_GROUP = SWA_HEADS // SWA_KV_HEADS
SWA_HEAD_DIM = 64
SWA_WIDTH = SWA_HEADS * SWA_HEAD_DIM
MIX_WIDTH = MLA_WIDTH + SWA_WIDTH
IN_SPLITS = (MLA_Q_RANK, MLA_KV_RANK, MLA_ROPE, SWA_WIDTH, SWA_KV_HEADS * SWA_HEAD_DIM, SWA_KV_HEADS * SWA_HEAD_DIM)
IN_COLS = sum(IN_SPLITS)
IN_OFFS = tuple(int(v) for v in np.cumsum((0,) + IN_SPLITS))
D_FF = 2816
MLA_SCALE = MLA_QK ** -0.5
LOG2E = 1.4426950408889634
LN2 = 0.6931471805599453
MLA_QSCALE = MLA_SCALE * LOG2E
SWA_SCALE = SWA_HEAD_DIM ** -0.5
NEG = -1e30

ADAM_LR = 0.001
ADAM_B1 = 0.9
ADAM_B2 = 0.999
ADAM_EPS = 1e-08
ADAM_WD = 0.01
ADAM_STEP = 10

N_SHARD = 4
N_DEV = 8
VMEM_LIMIT = 56 * 1024 * 1024
MESH = pl.DeviceIdType.MESH

WEIGHT_NAMES = ['ffn1_norm', 'ffn1_w_gate', 'ffn1_w_up', 'ffn1_w_down', 'mix_norm', 'w_in', 'mla_q_a_norm', 'mla_w_q_b',
                'mla_kv_a_norm', 'mla_w_kv_b', 'mla_q_norm', 'mla_k_norm', 'swa_q_norm', 'swa_k_norm', 'swa_sinks',
                'mla_out_norm', 'swa_out_norm', 'w_o', 'ffn2_norm', 'ffn2_w_gate', 'ffn2_w_up', 'ffn2_w_down']
BIG = {'ffn1_w_gate': ((D_MODEL, D_FF), 1), 'ffn1_w_up': ((D_MODEL, D_FF), 1), 'ffn1_w_down': ((D_FF, D_MODEL), 0),
       'w_in': ((D_MODEL, IN_COLS), 1), 'mla_w_q_b': ((MLA_Q_RANK, MLA_HEADS * MLA_QK), 1),
       'mla_w_kv_b': ((MLA_KV_RANK, MLA_HEADS * (MLA_NOPE + MLA_V)), 1), 'w_o': ((MIX_WIDTH, D_MODEL), 0),
       'ffn2_w_gate': ((D_MODEL, D_FF), 1), 'ffn2_w_up': ((D_MODEL, D_FF), 1), 'ffn2_w_down': ((D_FF, D_MODEL), 0)}
BIG_NAMES = [n for n in WEIGHT_NAMES if n in BIG]
SMALL_NAMES = [n for n in WEIGHT_NAMES if n not in BIG]

_pallas_call = pl.pallas_call


def _params(**kw):
    return pltpu.CompilerParams(vmem_limit_bytes=VMEM_LIMIT, **kw)


def _full(shape):
    n = len(shape)
    return pl.BlockSpec(shape, lambda *_: (0,) * n)


def _resident(shape):
    n = len(shape)
    return pl.BlockSpec(shape, lambda *_: (0,) * n, pipeline_mode=pl.Buffered(1))


@jax.custom_vjp
def _mm(a, w):
    return jnp.dot(a.astype(BF16), w, preferred_element_type=F32)


def _mm_fwd(a, w):
    return _mm(a, w), w


def _mm_bwd(w, dy):
    return lax.dot_general(dy.astype(BF16), w, (((1,), (1,)), ((), ())), preferred_element_type=F32), None


_mm.defvjp(_mm_fwd, _mm_bwd)


def _dot_nt(a, b):
    return lax.dot_general(a, b, (((1,), (1,)), ((), ())), preferred_element_type=F32)


def _dot_tn(a, b):
    return lax.dot_general(a, b, (((0,), (0,)), ((), ())), preferred_element_type=F32)


def _rms(t, g):
    return t * lax.rsqrt(jnp.mean(t * t, axis=-1, keepdims=True) + EPS) * g


def _rope(t, cos, sin):
    half = t.shape[-1] // 2
    t1, t2 = t[:, :half], t[:, half:]
    return jnp.concatenate([t1 * cos - t2 * sin, t2 * cos + t1 * sin], axis=-1)


def _sigmoid(z):
    return 1.0 / (1.0 + jnp.exp(-z))


def _row_tile(s, want):
    return min(want, s)


FF_CHUNK = 1408


def _ffn_fwd(x, g, wg, wu, wd, name):
    s = x.shape[0]
    tm = _row_tile(s, 256)

    def body(x_ref, g_ref, wg_ref, wu_ref, wd_ref, y_ref, gate_ref, up_ref):
        xv = x_ref[...]
        nb = _rms(xv, g_ref[...]).astype(BF16)
        acc = xv
        for c in range(0, D_FF, FF_CHUNK):
            gate = jnp.dot(nb, wg_ref[:, c:c + FF_CHUNK], preferred_element_type=F32)
            up = jnp.dot(nb, wu_ref[:, c:c + FF_CHUNK], preferred_element_type=F32)
            gate_ref[:, c:c + FF_CHUNK] = gate.astype(BF16)
            up_ref[:, c:c + FF_CHUNK] = up.astype(BF16)
            act = (gate * _sigmoid(gate) * up).astype(BF16)
            acc = acc + 0.5 * jnp.dot(act, wd_ref[c:c + FF_CHUNK, :], preferred_element_type=F32)
        y_ref[...] = acc

    return _pallas_call(
        body, name=name, grid=(s // tm,),
        in_specs=[pl.BlockSpec((tm, D_MODEL), lambda i: (i, 0)), _full((1, D_MODEL)),
                  _resident((D_MODEL, D_FF)), _resident((D_MODEL, D_FF)), _resident((D_FF, D_MODEL))],
        out_specs=[pl.BlockSpec((tm, D_MODEL), lambda i: (i, 0)), pl.BlockSpec((tm, D_FF), lambda i: (i, 0)),
                   pl.BlockSpec((tm, D_FF), lambda i: (i, 0))],
        out_shape=[jax.ShapeDtypeStruct((s, D_MODEL), F32), jax.ShapeDtypeStruct((s, D_FF), BF16),
                   jax.ShapeDtypeStruct((s, D_FF), BF16)],
        compiler_params=_params(dimension_semantics=("arbitrary",)),
    )(x, g, wg, wu, wd)


def _ffn_bwd(x, dy, gate, up, g, wg, wu, wd, name):
    s = x.shape[0]
    tm = _row_tile(s, 256)

    def body(x_ref, dy_ref, gate_ref, up_ref, g_ref, wg_ref, wu_ref, wd_ref,
             dx_ref, dgain_ref, n_ref, act_ref, dgate_ref, dup_ref):
        i = pl.program_id(0)
        xv = x_ref[...]
        dyv = dy_ref[...]
        gv = g_ref[...]
        r = lax.rsqrt(jnp.mean(xv * xv, axis=-1, keepdims=True) + EPS)
        xh = xv * r
        n_ref[...] = (xh * gv).astype(BF16)
        dyh = (0.5 * dyv).astype(BF16)
        dn = jnp.zeros_like(xv)
        for c in range(0, D_FF, FF_CHUNK):
            dact = _dot_nt(dyh, wd_ref[c:c + FF_CHUNK, :])
            gt = gate_ref[:, c:c + FF_CHUNK].astype(F32)
            u = up_ref[:, c:c + FF_CHUNK].astype(F32)
            sg = _sigmoid(gt)
            sl = gt * sg
            act_ref[:, c:c + FF_CHUNK] = (sl * u).astype(BF16)
            dup = (dact * sl).astype(BF16)
            dgate = (dact * u * (sg * (1.0 + gt * (1.0 - sg)))).astype(BF16)
            dup_ref[:, c:c + FF_CHUNK] = dup
            dgate_ref[:, c:c + FF_CHUNK] = dgate
            dn = dn + _dot_nt(dgate, wg_ref[:, c:c + FF_CHUNK]) + _dot_nt(dup, wu_ref[:, c:c + FF_CHUNK])
        part = jnp.sum(dn * xh, axis=0, keepdims=True)

        @pl.when(i == 0)
        def _():
            dgain_ref[...] = part

        @pl.when(i > 0)
        def _():
            dgain_ref[...] += part

        dxh = dn * gv
        dx_ref[...] = dyv + r * (dxh - xh * jnp.mean(dxh * xh, axis=-1, keepdims=True))

    row = lambda w: pl.BlockSpec((tm, w), lambda i: (i, 0))
    return _pallas_call(
        body, name=name, grid=(s // tm,),
        in_specs=[row(D_MODEL), row(D_MODEL), row(D_FF), row(D_FF), _full((1, D_MODEL)),
                  _resident((D_MODEL, D_FF)), _resident((D_MODEL, D_FF)), _resident((D_FF, D_MODEL))],
        out_specs=[row(D_MODEL), _full((1, D_MODEL)), row(D_MODEL), row(D_FF), row(D_FF), row(D_FF)],
        out_shape=[jax.ShapeDtypeStruct((s, D_MODEL), F32), jax.ShapeDtypeStruct((1, D_MODEL), F32),
                   jax.ShapeDtypeStruct((s, D_MODEL), BF16), jax.ShapeDtypeStruct((s, D_FF), BF16),
                   jax.ShapeDtypeStruct((s, D_FF), BF16), jax.ShapeDtypeStruct((s, D_FF), BF16)],
        compiler_params=_params(dimension_semantics=("arbitrary",)),
    )(x, dy, gate, up, g, wg, wu, wd)


def _store_col_shards(o_ref, acc, first_shard, n_here, width):
    for q in range(n_here):
        o_ref[q] = acc[:, (first_shard + q) * width:(first_shard + q + 1) * width].astype(BF16)


def _matmul_tn(a, b, scale, name, col_shards):
    t, m = a.shape
    n = b.shape[1]
    tk = _row_tile(t, 512)
    tn = n // 2
    per = n // N_SHARD
    nk = t // tk

    def body(a_ref, b_ref, o_ref, acc_ref):
        k = pl.program_id(1)
        bv = b_ref[...]
        if scale != 1.0:
            bv = bv.astype(F32) * scale
        part = _dot_tn(a_ref[...].astype(BF16), bv.astype(BF16))

        @pl.when(k == 0)
        def _():
            acc_ref[...] = part

        @pl.when(k > 0)
        def _():
            acc_ref[...] += part

        @pl.when(k == nk - 1)
        def _():
            if col_shards:
                _store_col_shards(o_ref, acc_ref[...], 0, tn // per, per)
            else:
                o_ref[...] = acc_ref[...].astype(BF16)

    if col_shards:
        out_spec = pl.BlockSpec((tn // per, m, per), lambda j, k: (j, 0, 0))
        out_shape = jax.ShapeDtypeStruct((N_SHARD, m, per), BF16)
    else:
        out_spec = pl.BlockSpec((m, tn), lambda j, k: (0, j))
        out_shape = jax.ShapeDtypeStruct((m, n), BF16)
    return _pallas_call(
        body, name=name, grid=(n // tn, nk),
        in_specs=[pl.BlockSpec((tk, m), lambda j, k: (k, 0)), pl.BlockSpec((tk, tn), lambda j, k: (k, j))],
        out_specs=out_spec, out_shape=out_shape, scratch_shapes=[pltpu.VMEM((m, tn), F32)],
        compiler_params=_params(dimension_semantics=("arbitrary", "arbitrary")),
    )(a, b)


def _pre_math(x, gm, gqa, gkva, gq, gk, gsq, gsk, taps, win, wqb, wkvb, cos, sin):
    h = _rms(x, gm)
    proj = _mm(h, win)
    if taps is not None:
        proj = proj + taps[0]
    o = IN_OFFS
    c_q, c_kv, k_pe = proj[:, o[0]:o[1]], proj[:, o[1]:o[2]], proj[:, o[2]:o[3]]
    q_s, k_s, v_s = proj[:, o[3]:o[4]], proj[:, o[4]:o[5]], proj[:, o[5]:o[6]]
    cqn = _rms(c_q, gqa)
    qa_all = _mm(cqn, wqb)
    ckvn = _rms(c_kv, gkva)
    kv_all = _mm(ckvn, wkvb)
    if taps is not None:
        qa_all = qa_all + taps[1]
        kv_all = kv_all + taps[2]
    q_a, k_a, v_a = [], [], []
    kvw = MLA_NOPE + MLA_V
    for hd in range(MLA_HEADS):
        qn = _rms(qa_all[:, hd * MLA_QK:(hd + 1) * MLA_QK], gq)
        k_nope = kv_all[:, hd * kvw:hd * kvw + MLA_NOPE]
        v_a.append(kv_all[:, hd * kvw + MLA_NOPE:(hd + 1) * kvw])
        kn = _rms(jnp.concatenate([k_nope, k_pe], axis=-1), gk)
        q_a.append(jnp.concatenate([qn[:, :MLA_NOPE], _rope(qn[:, MLA_NOPE:], cos, sin)], axis=-1))
        k_a.append(jnp.concatenate([kn[:, :MLA_NOPE], _rope(kn[:, MLA_NOPE:], cos, sin)], axis=-1))
    d = SWA_HEAD_DIM
    q_b = [_rope(_rms(q_s[:, hd * d:(hd + 1) * d], gsq), cos, sin) for hd in range(SWA_HEADS)]
    k_b = [_rope(_rms(k_s[:, j * d:(j + 1) * d], gsk), cos, sin) for j in range(SWA_KV_HEADS)]
    v_b = [v_s[:, j * d:(j + 1) * d] for j in range(SWA_KV_HEADS)]
    return (q_a, k_a, v_a, q_b, k_b, v_b), (h, cqn, ckvn)


_PRE_GAIN_WIDTHS = (D_MODEL, MLA_Q_RANK, MLA_KV_RANK, MLA_QK, MLA_QK, SWA_HEAD_DIM, SWA_HEAD_DIM)
_PRE_HEADS = ((MLA_HEADS, MLA_QK), (MLA_HEADS, MLA_QK), (MLA_HEADS, MLA_V),
              (SWA_HEADS, SWA_HEAD_DIM), (SWA_KV_HEADS, SWA_HEAD_DIM), (SWA_KV_HEADS, SWA_HEAD_DIM))


def _pre_fwd(x, gains, win, wqb, wkvb, cos, sin, name):
    s = x.shape[0]
    tm = _row_tile(s, 256)

    def body(x_ref, *refs):
        g_refs, (win_ref, wqb_ref, wkvb_ref, cos_ref, sin_ref), out_refs = refs[:7], refs[7:12], refs[12:]
        outs, _ = _pre_math(x_ref[...], *[g[...] for g in g_refs], None, win_ref[...], wqb_ref[...], wkvb_ref[...],
                            cos_ref[...], sin_ref[...])
        for idx, (ref, heads) in enumerate(zip(out_refs, outs)):
            for hd, val in enumerate(heads):
                ref[hd] = (val * MLA_QSCALE if idx == 0 else val).astype(BF16)

    heads_spec = lambda nh, w: pl.BlockSpec((nh, tm, w), lambda i: (0, i, 0))
    return _pallas_call(
        body, name=name, grid=(s // tm,),
        in_specs=[pl.BlockSpec((tm, D_MODEL), lambda i: (i, 0))] + [_full((1, w)) for w in _PRE_GAIN_WIDTHS]
        + [_resident(win.shape), _resident(wqb.shape), _resident(wkvb.shape),
           pl.BlockSpec((tm, MLA_ROPE // 2), lambda i: (i, 0)), pl.BlockSpec((tm, MLA_ROPE // 2), lambda i: (i, 0))],
        out_specs=[heads_spec(nh, w) for nh, w in _PRE_HEADS],
        out_shape=[jax.ShapeDtypeStruct((nh, s, w), BF16) for nh, w in _PRE_HEADS],
        compiler_params=_params(dimension_semantics=("arbitrary",)),
    )(x, *gains, win, wqb, wkvb, cos, sin)


def _pre_bwd(x, dx_res, cts, gains, win, wqb, wkvb, cos, sin, name):
    s = x.shape[0]
    tm = _row_tile(s, 256)
    tap_widths = (IN_COLS, MLA_HEADS * MLA_QK, MLA_HEADS * (MLA_NOPE + MLA_V))

    def body(x_ref, dxr_ref, *refs):
        ct_refs, g_refs = refs[:6], refs[6:13]
        win_ref, wqb_ref, wkvb_ref, cos_ref, sin_ref = refs[13:18]
        dx_ref, dg_refs, dw_refs, acc_refs = refs[18], refs[19:26], refs[26:29], refs[29:32]
        i = pl.program_id(0)
        win_v, wqb_v, wkvb_v, cos_v, sin_v = win_ref[...], wqb_ref[...], wkvb_ref[...], cos_ref[...], sin_ref[...]

        def f(xv, gm, gqa, gkva, gq, gk, gsq, gsk, t0, t1, t2):
            return _pre_math(xv, gm, gqa, gkva, gq, gk, gsq, gsk, (t0, t1, t2), win_v, wqb_v, wkvb_v, cos_v, sin_v)

        taps = [jnp.zeros((tm, w), F32) for w in tap_widths]
        _, vjp, acts = jax.vjp(f, x_ref[...], *[g[...] for g in g_refs], *taps, has_aux=True)
        ct = tuple([ref[hd] for hd in range(nh)] for ref, (nh, _) in zip(ct_refs, _PRE_HEADS))
        grads = vjp(ct)
        dx_ref[...] = grads[0] + dxr_ref[...]
        dws = [_dot_tn(a.astype(BF16), t.astype(BF16)) for a, t in zip(acts, grads[8:11])]

        @pl.when(i == 0)
        def _():
            for ref, val in zip(dg_refs, grads[1:8]):
                ref[...] = val
            for ref, val in zip(acc_refs, dws):
                ref[...] = val

        @pl.when(i > 0)
        def _():
            for ref, val in zip(dg_refs, grads[1:8]):
                ref[...] += val
            for ref, val in zip(acc_refs, dws):
                ref[...] += val

        @pl.when(i == s // tm - 1)
        def _():
            for ref, acc in zip(dw_refs, acc_refs):
                _store_col_shards(ref, acc[...], 0, N_SHARD, acc.shape[1] // N_SHARD)

    heads_spec = lambda nh, w: pl.BlockSpec((nh, tm, w), lambda i: (0, i, 0))
    row = pl.BlockSpec((tm, D_MODEL), lambda i: (i, 0))
    half = pl.BlockSpec((tm, MLA_ROPE // 2), lambda i: (i, 0))
    shard_shapes = [(N_SHARD, w.shape[0], w.shape[1] // N_SHARD) for w in (win, wqb, wkvb)]
    return _pallas_call(
        body, name=name, grid=(s // tm,),
        in_specs=[row, row] + [heads_spec(nh, w) for nh, w in _PRE_HEADS] + [_full((1, w)) for w in _PRE_GAIN_WIDTHS]
        + [_resident(win.shape), _resident(wqb.shape), _resident(wkvb.shape), half, half],
        out_specs=[row] + [_full((1, w)) for w in _PRE_GAIN_WIDTHS] + [_full(shp) for shp in shard_shapes],
        out_shape=[jax.ShapeDtypeStruct((s, D_MODEL), F32)] + [jax.ShapeDtypeStruct((1, w), F32) for w in _PRE_GAIN_WIDTHS]
        + [jax.ShapeDtypeStruct(shp, BF16) for shp in shard_shapes],
        scratch_shapes=[pltpu.VMEM(w.shape, F32) for w in (win, wqb, wkvb)],
        compiler_params=_params(dimension_semantics=("arbitrary",)),
    )(x, dx_res, *cts, *gains, win, wqb, wkvb, cos, sin)


def _post_math(oa, ob, ga, gb, wo):
    mixed = jnp.concatenate([_rms(jnp.concatenate(oa, axis=-1), ga), _rms(jnp.concatenate(ob, axis=-1), gb)], axis=-1)
    return _mm(mixed, wo), mixed


def _post_fwd(x, oa, ob, ga, gb, wo, name):
    s = x.shape[0]
    tm = _row_tile(s, 256)

    def body(x_ref, oa_ref, ob_ref, ga_ref, gb_ref, wo_ref, y_ref):
        y, _ = _post_math([oa_ref[hd] for hd in range(MLA_HEADS)], [ob_ref[hd] for hd in range(SWA_HEADS)],
                          ga_ref[...], gb_ref[...], wo_ref[...])
        y_ref[...] = x_ref[...] + y

    row = pl.BlockSpec((tm, D_MODEL), lambda i: (i, 0))
    return _pallas_call(
        body, name=name, grid=(s // tm,),
        in_specs=[row, pl.BlockSpec((MLA_HEADS, tm, MLA_V), lambda i: (0, i, 0)),
                  pl.BlockSpec((SWA_HEADS, tm, SWA_HEAD_DIM), lambda i: (0, i, 0)),
                  _full((1, MLA_WIDTH)), _full((1, SWA_WIDTH)), _resident(wo.shape)],
        out_specs=row, out_shape=jax.ShapeDtypeStruct((s, D_MODEL), F32),
        compiler_params=_params(dimension_semantics=("arbitrary",)),
    )(x, oa, ob, ga, gb, wo)


def _post_bwd(dy, oa, ob, ga, gb, wo, name):
    s = dy.shape[0]
    tm = _row_tile(s, 256)

    def body(dy_ref, oa_ref, ob_ref, ga_ref, gb_ref, wo_ref, doa_ref, dob_ref, dga_ref, dgb_ref, dwo_ref, acc_ref):
        i = pl.program_id(0)
        wo_v = wo_ref[...]
        dyv = dy_ref[...]

        def f(oa_l, ob_l, ga_v, gb_v):
            return _post_math(oa_l, ob_l, ga_v, gb_v, wo_v)

        _, vjp, mixed = jax.vjp(f, [oa_ref[hd] for hd in range(MLA_HEADS)], [ob_ref[hd] for hd in range(SWA_HEADS)],
                                ga_ref[...], gb_ref[...], has_aux=True)
        doa, dob, dga, dgb = vjp(dyv)
        for hd in range(MLA_HEADS):
            doa_ref[hd] = doa[hd]
        for hd in range(SWA_HEADS):
            dob_ref[hd] = dob[hd]
        dwo = _dot_tn(mixed.astype(BF16), dyv.astype(BF16))

        @pl.when(i == 0)
        def _():
            dga_ref[...] = dga
            dgb_ref[...] = dgb
            acc_ref[...] = dwo

        @pl.when(i > 0)
        def _():
            dga_ref[...] += dga
            dgb_ref[...] += dgb
            acc_ref[...] += dwo

        @pl.when(i == s // tm - 1)
        def _():
            dwo_ref[...] = acc_ref[...].astype(BF16)

    row = pl.BlockSpec((tm, D_MODEL), lambda i: (i, 0))
    oa_spec = pl.BlockSpec((MLA_HEADS, tm, MLA_V), lambda i: (0, i, 0))
    ob_spec = pl.BlockSpec((SWA_HEADS, tm, SWA_HEAD_DIM), lambda i: (0, i, 0))
    return _pallas_call(
        body, name=name, grid=(s // tm,),
        in_specs=[row, oa_spec, ob_spec, _full((1, MLA_WIDTH)), _full((1, SWA_WIDTH)), _resident(wo.shape)],
        out_specs=[oa_spec, ob_spec, _full((1, MLA_WIDTH)), _full((1, SWA_WIDTH)), _full(wo.shape)],
        out_shape=[jax.ShapeDtypeStruct((MLA_HEADS, s, MLA_V), F32), jax.ShapeDtypeStruct((SWA_HEADS, s, SWA_HEAD_DIM), F32),
                   jax.ShapeDtypeStruct((1, MLA_WIDTH), F32), jax.ShapeDtypeStruct((1, SWA_WIDTH), F32),
                   jax.ShapeDtypeStruct(wo.shape, BF16)],
        scratch_shapes=[pltpu.VMEM(wo.shape, F32)],
        compiler_params=_params(dimension_semantics=("arbitrary",)),
    )(dy, oa, ob, ga, gb, wo)


def _attn_tile(s):
    return 512 if s >= 2048 else 128


def _causal_mask(t):
    return lax.broadcasted_iota(jnp.int32, (t, t), 1) <= lax.broadcasted_iota(jnp.int32, (t, t), 0)


def _pipelined_blocks(first, count, last_block, issue, consume, carry, prefetch_after):
    def clamped(j, slot):
        issue(jnp.minimum(j, last_block), slot)

    def pair(jj, c):
        a = first + 2 * jj
        clamped(a + 1, 1)
        c = consume(a, 0, c)
        clamped(a + 2, 0)
        return consume(a + 1, 1, c)

    clamped(first, 0)
    npairs = count // 2
    carry = lax.fori_loop(0, npairs, pair, carry)

    def odd(c):
        c = consume(first + 2 * npairs, 0, c)
        if prefetch_after:
            clamped(first + count, 0)
        return c

    return lax.cond(count - 2 * npairs == 1, odd, lambda c: c, carry)


def _mla_fwd(q, k, v, name):
    nh, s, _ = q.shape
    t = _attn_tile(s)
    nq = s // t

    def body(q_ref, k_ref, v_ref, o_ref, lse_ref, s0_ref, s1_ref):
        qi = pl.program_id(1)
        qv = q_ref[...]
        s_refs = (s0_ref, s1_ref)

        def rows(j):
            return pl.ds(pl.multiple_of(j * t, t), t)

        def issue(j, slot):
            s_refs[slot][...] = _dot_nt(qv, k_ref[rows(j), :])

        def consume(j, slot, carry, masked=False):
            m, l, acc = carry
            sc = s_refs[slot][...]
            if masked:
                sc = jnp.where(_causal_mask(t), sc, NEG)
            m_new = jnp.maximum(m, jnp.max(sc, axis=-1, keepdims=True))
            alpha = jnp.exp2(m - m_new)
            p = jnp.exp2(sc - m_new)
            l = alpha * l + jnp.sum(p, axis=-1, keepdims=True)
            acc = alpha * acc + jnp.dot(p.astype(BF16), v_ref[rows(j), :], preferred_element_type=F32)
            return m_new, l, acc

        init = (jnp.full((t, 1), NEG, F32), jnp.zeros((t, 1), F32), jnp.zeros((t, MLA_V), F32))
        carry = _pipelined_blocks(0, qi, nq - 1, issue, consume, init, True)
        m, l, acc = consume(qi, 0, carry, masked=True)
        o_ref[...] = acc / l
        lse_ref[...] = m + jnp.log2(l)

    return _pallas_call(
        body, name=name, grid=(nh, nq),
        in_specs=[pl.BlockSpec((None, t, MLA_QK), lambda h, i: (h, i, 0)), pl.BlockSpec((None, s, MLA_QK), lambda h, i: (h, 0, 0)),
                  pl.BlockSpec((None, s, MLA_V), lambda h, i: (h, 0, 0))],
        out_specs=[pl.BlockSpec((None, t, MLA_V), lambda h, i: (h, i, 0)), pl.BlockSpec((None, t, 1), lambda h, i: (h, i, 0))],
        out_shape=[jax.ShapeDtypeStruct((nh, s, MLA_V), F32), jax.ShapeDtypeStruct((nh, s, 1), F32)],
        scratch_shapes=[pltpu.VMEM((t, t), F32)] * 2,
        compiler_params=_params(dimension_semantics=("arbitrary", "arbitrary")),
    )(q, k, v)


def _mla_bwd_dq(q, k, v, o, do, lse, name):
    nh, s, _ = q.shape
    t = _attn_tile(s)
    nq = s // t

    def body(q_ref, k_ref, v_ref, o_ref, do_ref, lse_ref, dq_ref, delta_ref, s0_ref, s1_ref, dp0_ref, dp1_ref):
        qi = pl.program_id(1)
        qv = q_ref[...]
        dov = do_ref[...]
        lse = lse_ref[...]
        delta = jnp.sum(dov * o_ref[...], axis=-1, keepdims=True)
        delta_ref[...] = delta
        dob = dov.astype(BF16)
        s_refs, dp_refs = (s0_ref, s1_ref), (dp0_ref, dp1_ref)

        def rows(j):
            return pl.ds(pl.multiple_of(j * t, t), t)

        def issue(j, slot):
            s_refs[slot][...] = _dot_nt(qv, k_ref[rows(j), :])
            dp_refs[slot][...] = _dot_nt(dob, v_ref[rows(j), :])

        def consume(j, slot, dq, masked=False):
            p = jnp.exp2(s_refs[slot][...] - lse)
            if masked:
                p = jnp.where(_causal_mask(t), p, 0.0)
            ds = p * (dp_refs[slot][...] - delta)
            return dq + jnp.dot(ds.astype(BF16), k_ref[rows(j), :], preferred_element_type=F32)

        dq = _pipelined_blocks(0, qi, nq - 1, issue, consume, jnp.zeros((t, MLA_QK), F32), True)
        dq_ref[...] = consume(qi, 0, dq, masked=True) * MLA_SCALE

    tile = lambda w: pl.BlockSpec((None, t, w), lambda h, i: (h, i, 0))
    whole = lambda w: pl.BlockSpec((None, s, w), lambda h, i: (h, 0, 0))
    return _pallas_call(
        body, name=name, grid=(nh, nq),
        in_specs=[tile(MLA_QK), whole(MLA_QK), whole(MLA_V), tile(MLA_V), tile(MLA_V), tile(1)],
        out_specs=[tile(MLA_QK), tile(1)],
        out_shape=[jax.ShapeDtypeStruct((nh, s, MLA_QK), F32), jax.ShapeDtypeStruct((nh, s, 1), F32)],
        scratch_shapes=[pltpu.VMEM((t, t), F32)] * 4,
        compiler_params=_params(dimension_semantics=("arbitrary", "arbitrary")),
    )(q, k, v, o, do, lse)


def _mla_bwd_dkv(q, k, v, do, lse_row, delta_row, name):
    nh, s, _ = q.shape
    t = _attn_tile(s)
    nq = s // t

    def body(q_ref, k_ref, v_ref, do_ref, lse_ref, delta_ref, dk_ref, dv_ref, s0_ref, s1_ref, dp0_ref, dp1_ref):
        kj = pl.program_id(1)
        kv_, vv = k_ref[...], v_ref[...]
        s_refs, dp_refs = (s0_ref, s1_ref), (dp0_ref, dp1_ref)

        def rows(i):
            return pl.ds(pl.multiple_of(i * t, t), t)

        def issue(i, slot):
            s_refs[slot][...] = _dot_nt(kv_, q_ref[rows(i), :])
            dp_refs[slot][...] = _dot_nt(vv, do_ref[rows(i), :].astype(BF16))

        def consume(i, slot, carry, masked=False):
            dk, dv = carry
            p = jnp.exp2(s_refs[slot][...] - lse_ref[pl.ds(i, 1), :])
            if masked:
                p = jnp.where(lax.broadcasted_iota(jnp.int32, (t, t), 0) <= lax.broadcasted_iota(jnp.int32, (t, t), 1), p, 0.0)
            dv = dv + jnp.dot(p.astype(BF16), do_ref[rows(i), :].astype(BF16), preferred_element_type=F32)
            ds = p * (dp_refs[slot][...] - delta_ref[pl.ds(i, 1), :])
            dk = dk + jnp.dot(ds.astype(BF16), q_ref[rows(i), :], preferred_element_type=F32)
            return dk, dv

        issue(kj, 0)
        carry = consume(kj, 0, (jnp.zeros((t, MLA_QK), F32), jnp.zeros((t, MLA_V), F32)), masked=True)
        dk, dv = _pipelined_blocks(kj + 1, nq - 1 - kj, nq - 1, issue, consume, carry, False)
        dk_ref[...] = dk * LN2
        dv_ref[...] = dv

    tile = lambda w: pl.BlockSpec((None, t, w), lambda h, j: (h, j, 0))
    whole = lambda w: pl.BlockSpec((None, s, w), lambda h, j: (h, 0, 0))
    rows_spec = pl.BlockSpec((None, nq, t), lambda h, j: (h, 0, 0))
    return _pallas_call(
        body, name=name, grid=(nh, nq),
        in_specs=[whole(MLA_QK), tile(MLA_QK), tile(MLA_V), whole(MLA_V), rows_spec, rows_spec],
        out_specs=[tile(MLA_QK), tile(MLA_V)],
        out_shape=[jax.ShapeDtypeStruct((nh, s, MLA_QK), F32), jax.ShapeDtypeStruct((nh, s, MLA_V), F32)],
        scratch_shapes=[pltpu.VMEM((t, t), F32)] * 4,
        compiler_params=_params(dimension_semantics=("arbitrary", "arbitrary")),
    )(q, k, v, do, lse_row, delta_row)


def _swa_math(q_l, kband, vband, sink_l, first):
    q_rel = lax.broadcasted_iota(jnp.int32, (BLOCK, 2 * BLOCK), 0) + BLOCK
    k_rel = lax.broadcasted_iota(jnp.int32, (BLOCK, 2 * BLOCK), 1)
    dist = q_rel - k_rel
    valid = (dist >= 0) & (dist < BLOCK) & ((k_rel >= BLOCK) | jnp.logical_not(first))
    kb, vb = kband.astype(BF16), vband.astype(BF16)
    outs, psinks = [], []
    for qh, sink in zip(q_l, sink_l):
        sc = jnp.where(valid, _dot_nt(qh.astype(BF16), kb) * SWA_SCALE, NEG)
        m = lax.stop_gradient(jnp.maximum(jnp.max(sc, axis=-1, keepdims=True), sink))
        e = jnp.exp(sc - m)
        es = jnp.exp(sink - m)
        den = jnp.sum(e, axis=-1, keepdims=True) + es
        outs.append(jnp.dot((e / den).astype(BF16), vb, preferred_element_type=F32))
        psinks.append(es / den)
    return outs, psinks


def _swa_tile(s):
    return min(s, 4 * BLOCK)


def _swa_specs(tq):
    nb = tq // BLOCK
    grp = lambda: pl.BlockSpec((SWA_GROUP, tq, SWA_HEAD_DIM), lambda j, i: (j, i, 0))
    main = pl.BlockSpec((None, tq, SWA_HEAD_DIM), lambda j, i: (j, i, 0))
    tail = pl.BlockSpec((None, BLOCK, SWA_HEAD_DIM), lambda j, i: (j, nb * (i + 1), 0))
    sink = pl.BlockSpec((None, SWA_GROUP, 128), lambda j, i: (j, 0, 0))
    return grp, main, tail, sink


def _swa_fwd(q, kpad, vpad, sinks, name):
    _, s, _ = q.shape
    tq = _swa_tile(s)
    grp, main, tail, sink = _swa_specs(tq)

    def body(q_ref, km_ref, kt_ref, vm_ref, vt_ref, sink_ref, o_ref):
        i = pl.program_id(1)
        kall = jnp.concatenate([km_ref[...], kt_ref[...]], axis=0)
        vall = jnp.concatenate([vm_ref[...], vt_ref[...]], axis=0)
        sk = sink_ref[...]
        sink_l = [sk[g:g + 1, 0:1] for g in range(SWA_GROUP)]
        for b in range(tq // BLOCK):
            lo = b * BLOCK
            outs, _ = _swa_math([q_ref[g, lo:lo + BLOCK, :] for g in range(SWA_GROUP)], kall[lo:lo + 2 * BLOCK],
                                vall[lo:lo + 2 * BLOCK], sink_l, i == 0 if b == 0 else False)
            for g in range(SWA_GROUP):
                o_ref[g, lo:lo + BLOCK, :] = outs[g]

    return _pallas_call(
        body, name=name, grid=(SWA_KV_HEADS, s // tq),
        in_specs=[grp(), main, tail, main, tail, sink], out_specs=grp(),
        out_shape=jax.ShapeDtypeStruct((SWA_HEADS, s, SWA_HEAD_DIM), F32),
        compiler_params=_params(dimension_semantics=("arbitrary", "arbitrary")),
    )(q, kpad, kpad, vpad, vpad, sinks)


def _swa_bwd(q, kpad, vpad, sinks, do, name):
    _, s, _ = q.shape
    tq = _swa_tile(s)
    grp, main, tail, sink = _swa_specs(tq)

    def body(q_ref, km_ref, kt_ref, vm_ref, vt_ref, sink_ref, do_ref, dq_ref, dk_ref, dv_ref, dsink_ref):
        i = pl.program_id(1)
        sk = sink_ref[...]
        sink_l = [sk[g:g + 1, 0:1] for g in range(SWA_GROUP)]
        kall = jnp.concatenate([km_ref[...], kt_ref[...]], axis=0).astype(F32)
        vall = jnp.concatenate([vm_ref[...], vt_ref[...]], axis=0).astype(F32)

        @pl.when(i == 0)
        def _():
            dk_ref[...] = jnp.zeros_like(dk_ref)
            dv_ref[...] = jnp.zeros_like(dv_ref)
            dsink_ref[...] = jnp.zeros_like(dsink_ref)

        dsk = [jnp.zeros((1, 1), F32) for _ in range(SWA_GROUP)]
        for b in range(tq // BLOCK):
            lo = b * BLOCK
            first = i == 0 if b == 0 else False

            def f(ql, kb, vb, first=first):
                return _swa_math(ql, kb, vb, sink_l, first)

            q_l = [q_ref[g, lo:lo + BLOCK, :].astype(F32) for g in range(SWA_GROUP)]
            outs, vjp, psinks = jax.vjp(f, q_l, kall[lo:lo + 2 * BLOCK], vall[lo:lo + 2 * BLOCK], has_aux=True)
            do_l = [do_ref[g, lo:lo + BLOCK, :] for g in range(SWA_GROUP)]
            dq_l, dkb, dvb = vjp(do_l)
            for g in range(SWA_GROUP):
                dq_ref[g, lo:lo + BLOCK, :] = dq_l[g]
                dsk[g] = dsk[g] - jnp.sum(psinks[g] * jnp.sum(do_l[g] * outs[g], axis=-1, keepdims=True), axis=0, keepdims=True)
            rows = pl.ds(pl.multiple_of(i * tq, BLOCK) + lo, 2 * BLOCK)
            dk_ref[rows, :] += dkb
            dv_ref[rows, :] += dvb
        dsink_ref[...] += jnp.concatenate([jnp.broadcast_to(d, (1, 128)) for d in dsk]
                                          + [jnp.zeros((8 - SWA_GROUP, 128), F32)], axis=0)

    acc = pl.BlockSpec((None, s + BLOCK, SWA_HEAD_DIM), lambda j, i: (j, 0, 0))
    return _pallas_call(
        body, name=name, grid=(SWA_KV_HEADS, s // tq),
        in_specs=[grp(), main, tail, main, tail, sink, grp()],
        out_specs=[grp(), acc, acc, pl.BlockSpec((None, 8, 128), lambda j, i: (j, 0, 0))],
        out_shape=[jax.ShapeDtypeStruct((SWA_HEADS, s, SWA_HEAD_DIM), F32),
                   jax.ShapeDtypeStruct((SWA_KV_HEADS, s + BLOCK, SWA_HEAD_DIM), F32),
                   jax.ShapeDtypeStruct((SWA_KV_HEADS, s + BLOCK, SWA_HEAD_DIM), F32),
                   jax.ShapeDtypeStruct((SWA_KV_HEADS, 8, 128), F32)],
        compiler_params=_params(dimension_semantics=("arbitrary", "arbitrary")),
    )(q, kpad, kpad, vpad, vpad, sinks, do)


def _loss_head(y, target, name):
    s = y.shape[0]
    tm = _row_tile(s, 512)

    def body(y_ref, t_ref, dy_ref, loss_ref):
        i = pl.program_id(0)
        err = y_ref[...] - t_ref[...]
        dy_ref[...] = err * (1.0 / D_MODEL)
        part = jnp.broadcast_to(0.5 * jnp.sum(jnp.mean(err * err, axis=-1, keepdims=True), axis=0, keepdims=True), (1, 128))

        @pl.when(i == 0)
        def _():
            loss_ref[...] = part

        @pl.when(i > 0)
        def _():
            loss_ref[...] += part

    row = pl.BlockSpec((tm, D_MODEL), lambda i: (i, 0))
    return _pallas_call(
        body, name=name, grid=(s // tm,), in_specs=[row, row], out_specs=[row, _full((1, 128))],
        out_shape=[jax.ShapeDtypeStruct((s, D_MODEL), F32), jax.ShapeDtypeStruct((1, 128), F32)],
        compiler_params=_params(dimension_semantics=("arbitrary",)),
    )(y, target)


def _adamw(w, g, m, v, name):
    rows, cols = w.shape
    tr = rows
    for cand in (512, 256, 128, 64, 32, 16, 8):
        if rows % cand == 0 and rows > cand:
            tr = cand
            break

    def body(w_ref, g_ref, m_ref, v_ref, d_ref, nm_ref, nv_ref):
        gv = g_ref[...]
        nm = ADAM_B1 * m_ref[...] + (1.0 - ADAM_B1) * gv
        nv = ADAM_B2 * v_ref[...] + (1.0 - ADAM_B2) * (gv * gv)
        m_hat = nm / (1.0 - ADAM_B1 ** ADAM_STEP)
        v_hat = nv / (1.0 - ADAM_B2 ** ADAM_STEP)
        d_ref[...] = -ADAM_LR * (m_hat / (jnp.sqrt(v_hat) + ADAM_EPS) + ADAM_WD * w_ref[...])
        nm_ref[...] = nm
        nv_ref[...] = nv

    blk = pl.BlockSpec((tr, cols), lambda i: (i, 0))
    return _pallas_call(
        body, name=name, grid=(rows // tr,), in_specs=[blk] * 4, out_specs=[blk] * 3,
        out_shape=[jax.ShapeDtypeStruct((rows, cols), F32)] * 3,
        compiler_params=_params(dimension_semantics=("arbitrary",)),
    )(w, g, m, v)


def _position():
    return lax.axis_index("x"), lax.axis_index("y"), lax.axis_index("c")


def _remote(src, dst, send_sems, recv_sems, k, to):
    return pltpu.make_async_remote_copy(src_ref=src, dst_ref=dst, send_sem=send_sems.at[k], recv_sem=recv_sems.at[k],
                                        device_id=to, device_id_type=MESH)


_HBM = pl.BlockSpec(memory_space=pltpu.HBM)


def _all_gather_layers(mine, name):
    na = len(mine)

    def body(*refs):
        ins, outs, (send_sems, recv_sems) = refs[:na], refs[na:2 * na], refs[2 * na:]
        x, y, c = _position()
        me, sibling = (x, y, c), (x, y, 1 - c)
        chips = [(1 - x, y), (x, 1 - y), (1 - x, 1 - y)]

        def slot(a, px, py, pc):
            return outs[a].at[4 * px + 2 * py + pc]

        def cp(a, k, block, to, src=None):
            return _remote(slot(a, *block) if src is None else src, slot(a, *block), send_sems, recv_sems, 6 * a + k, to)

        first = [cp(a, j, me, (*chip, c), src=ins[a].at[c]) for a in range(na) for j, chip in enumerate(chips)]
        for f in first:
            f.start()
        passed = []
        for j, chip in enumerate(chips):
            for a in range(na):
                cp(a, j, (*chip, c), me).wait_recv()
                fwd = cp(a, 3 + j, (*chip, c), sibling)
                fwd.start()
                passed.append(fwd)
        for a in range(na):
            for j, chip in enumerate(chips):
                cp(a, 3 + j, (*chip, 1 - c), me).wait_recv()
        for f in first + passed:
            f.wait_send()

    return _pallas_call(
        body, name=name, in_specs=[_HBM] * na, out_specs=[_HBM] * na,
        out_shape=[jax.ShapeDtypeStruct((N_DEV,) + m.shape[1:], m.dtype) for m in mine],
        scratch_shapes=[pltpu.SemaphoreType.DMA((6 * na,)), pltpu.SemaphoreType.DMA((6 * na,))],
    )(*mine)


def _sibling_exchange(parts, name, other_layer):
    na = len(parts)

    def body(*refs):
        ins, outs, (send_sems, recv_sems) = refs[:na], refs[na:2 * na], refs[2 * na:]
        x, y, c = _position()
        copies = [_remote(ins[a].at[1 - c] if other_layer else ins[a], outs[a], send_sems, recv_sems, a, (x, y, 1 - c))
                  for a in range(na)]
        for cp in copies:
            cp.start()
        for cp in copies:
            cp.wait()

    return _pallas_call(
        body, name=name, in_specs=[_HBM] * na, out_specs=[_HBM] * na,
        out_shape=[jax.ShapeDtypeStruct(p.shape[1:] if other_layer else p.shape, p.dtype) for p in parts],
        scratch_shapes=[pltpu.SemaphoreType.DMA((na,)), pltpu.SemaphoreType.DMA((na,))],
    )(*parts)


def _scatter_to_chips(parts, name):
    na = len(parts)

    def body(*refs):
        ins, outs, (send_sems, recv_sems) = refs[:na], refs[na:2 * na], refs[2 * na:]
        x, y, c = _position()
        chips = [(1 - x, y), (x, 1 - y), (1 - x, 1 - y)]
        copies = [_remote(ins[a].at[2 * px + py], outs[a].at[j], send_sems, recv_sems, 3 * a + j, (px, py, c))
                  for a in range(na) for j, (px, py) in enumerate(chips)]
        for cp in copies:
            cp.start()
        for cp in copies:
            cp.wait()

    return _pallas_call(
        body, name=name, in_specs=[_HBM] * na, out_specs=[_HBM] * na,
        out_shape=[jax.ShapeDtypeStruct((3,) + p.shape[1:], p.dtype) for p in parts],
        scratch_shapes=[pltpu.SemaphoreType.DMA((3 * na,)), pltpu.SemaphoreType.DMA((3 * na,))],
    )(*parts)


def _assemble(g4, mine, name, side_by_side):
    _, nl, r, w = g4.shape
    tr = next(cand for cand in (256, 128) if r % cand == 0)

    def body(in_ref, mine_ref, out_ref):
        chip = 2 * lax.axis_index("x") + lax.axis_index("y")
        blocks = [jnp.where(chip == sh, mine_ref[...], in_ref[sh]) for sh in range(N_SHARD)]
        if side_by_side:
            out_ref[...] = jnp.concatenate(blocks, axis=-1)
        else:
            for sh in range(N_SHARD):
                out_ref[sh] = blocks[sh]

    if side_by_side:
        out_spec = pl.BlockSpec((None, tr, N_SHARD * w), lambda l, i: (l, i, 0))
        out_shape = jax.ShapeDtypeStruct((nl, r, N_SHARD * w), g4.dtype)
    else:
        out_spec = pl.BlockSpec((None, N_SHARD, tr, w), lambda l, i: (l, 0, i, 0))
        out_shape = jax.ShapeDtypeStruct((nl, N_SHARD, r, w), g4.dtype)
    return _pallas_call(
        body, name=name, grid=(nl, r // tr),
        in_specs=[pl.BlockSpec((N_SHARD, None, tr, w), lambda l, i: (0, l, i, 0)),
                  pl.BlockSpec((None, tr, w), lambda l, i: (l, i, 0))],
        out_specs=out_spec, out_shape=out_shape,
        compiler_params=_params(dimension_semantics=("arbitrary", "arbitrary")),
    )(g4, mine)


def _all_reduce_small(vec, name):
    r, l = vec.shape

    def body(v_ref, out_ref, gath_ref, send_sems, recv_sems):
        x, y, c = _position()
        me = 4 * x + 2 * y + c
        gath_ref[me] = v_ref[...]
        copies = []
        for k in range(1, N_DEV):
            to = (x ^ (k >> 2), y ^ ((k >> 1) & 1), c ^ (k & 1))
            copies.append(_remote(gath_ref.at[me], gath_ref.at[me], send_sems, recv_sems, k - 1, to))
        for cp in copies:
            cp.start()
        for k in range(1, N_DEV):
            frm = 4 * (x ^ (k >> 2)) + 2 * (y ^ ((k >> 1) & 1)) + (c ^ (k & 1))
            _remote(gath_ref.at[frm], gath_ref.at[frm], send_sems, recv_sems, k - 1, (x, y, c)).wait_recv()
        for cp in copies:
            cp.wait_send()
        total = gath_ref[0]
        for d in range(1, N_DEV):
            total = total + gath_ref[d]
        out_ref[...] = total

    vm = pl.BlockSpec(memory_space=pltpu.VMEM)
    return _pallas_call(
        body, name=name, in_specs=[vm], out_specs=vm, out_shape=jax.ShapeDtypeStruct((r, l), F32),
        scratch_shapes=[pltpu.VMEM((N_DEV, r, l), F32), pltpu.SemaphoreType.DMA((N_DEV - 1,)),
                        pltpu.SemaphoreType.DMA((N_DEV - 1,))],
    )(vec)


def _sum_blocks(blocks, out_dtype, name):
    m, w = blocks[0].shape
    tr = next(cand for cand in (512, 256, 128) if m % cand == 0)

    def body(*refs):
        total = refs[0][...].astype(F32)
        for ref in refs[1:-1]:
            total = total + ref[...].astype(F32)
        refs[-1][...] = total.astype(out_dtype)

    blk = pl.BlockSpec((tr, w), lambda i: (i, 0))
    return _pallas_call(
        body, name=name, grid=(m // tr,), in_specs=[blk] * len(blocks), out_specs=blk,
        out_shape=jax.ShapeDtypeStruct((m, w), out_dtype),
        compiler_params=_params(dimension_semantics=("arbitrary",)),
    )(*blocks)


GROUPS = (('ffn1_w_gate', 'ffn1_w_up', 'ffn2_w_gate', 'ffn2_w_up'), ('ffn1_w_down', 'ffn2_w_down', 'w_o'),
          ('w_in',), ('mla_w_q_b',), ('mla_w_kv_b',))


def _shard_rows(name):
    shape, axis = BIG[name]
    return shape[0] // N_SHARD if axis == 0 else shape[0]


def _group_row_offsets(group):
    return [int(v) for v in np.cumsum([0] + [_shard_rows(n) for n in group])]


def _rope_tables(s):
    pos = jnp.arange(s, dtype=F32)
    inv = 1.0 / (ROPE_THETA ** (jnp.arange(0, MLA_ROPE, 2, dtype=F32) / MLA_ROPE))
    ang = pos[:, None] * inv[None, :]
    return jnp.cos(ang), jnp.sin(ang)


def _local_step(x, target, small, big):
    s = x.shape[0]
    cos, sin = _rope_tables(s)
    row = lambda name, l: small[name][l][None, :]
    saved = []
    for l in range(DEPTH):
        sv = {'x0': x}
        x, sv['g1'], sv['u1'] = _ffn_fwd(x, row('ffn1_norm', l), big['ffn1_w_gate'][l], big['ffn1_w_up'][l],
                                        big['ffn1_w_down'][l], f"ffn1_fwd_{l}")
        sv['x1'] = x
        gains = [row(n, l) for n in ('mix_norm', 'mla_q_a_norm', 'mla_kv_a_norm', 'mla_q_norm', 'mla_k_norm',
                                     'swa_q_norm', 'swa_k_norm')]
        q_a, k_a, v_a, q_b, k_b, v_b = _pre_fwd(x, gains, big['w_in'][l], big['mla_w_q_b'][l], big['mla_w_kv_b'][l],
                                                cos, sin, f"pre_fwd_{l}")
        o_a, lse = _mla_fwd(q_a, k_a, v_a, f"mla_fwd_{l}")
        kpad = jnp.pad(k_b, ((0, 0), (BLOCK, 0), (0, 0)))
        vpad = jnp.pad(v_b, ((0, 0), (BLOCK, 0), (0, 0)))
        sinks = jnp.broadcast_to(small['swa_sinks'][l].reshape(SWA_KV_HEADS, SWA_GROUP, 1), (SWA_KV_HEADS, SWA_GROUP, 128))
        o_b = _swa_fwd(q_b, kpad, vpad, sinks, f"swa_fwd_{l}")
        sv.update(gains=gains, q_a=q_a, k_a=k_a, v_a=v_a, q_b=q_b, kpad=kpad, vpad=vpad, sinks=sinks, o_a=o_a, lse=lse, o_b=o_b)
        x = _post_fwd(x, o_a, o_b, row('mla_out_norm', l), row('swa_out_norm', l), big['w_o'][l], f"post_fwd_{l}")
        sv['x2'] = x
        x, sv['g2'], sv['u2'] = _ffn_fwd(x, row('ffn2_norm', l), big['ffn2_w_gate'][l], big['ffn2_w_up'][l],
                                        big['ffn2_w_down'][l], f"ffn2_fwd_{l}")
        saved.append(sv)

    dx, loss = _loss_head(x, target, "loss_head")

    gs = {n: [None] * DEPTH for n in SMALL_NAMES}
    gb = {n: [None] * DEPTH for n in BIG_NAMES}
    t = _attn_tile(s)
    for l in reversed(range(DEPTH)):
        sv = saved[l]

        def ffn_back(tag, xin, dy, gate, up):
            dxi, dgain, nb, act, dgate, dup = _ffn_bwd(xin, dy, gate, up, row(tag + '_norm', l), big[tag + '_w_gate'][l],
                                                      big[tag + '_w_up'][l], big[tag + '_w_down'][l], f"{tag}_bwd_{l}")
            gs[tag + '_norm'][l] = dgain[0]
            gb[tag + '_w_gate'][l] = _matmul_tn(nb, dgate, 1.0, f"{tag}_dwg_{l}", True)
            gb[tag + '_w_up'][l] = _matmul_tn(nb, dup, 1.0, f"{tag}_dwu_{l}", True)
            gb[tag + '_w_down'][l] = _matmul_tn(act, dy, 0.5, f"{tag}_dwd_{l}", False).reshape(N_SHARD, D_FF // N_SHARD, D_MODEL)
            return dxi

        dx = ffn_back('ffn2', sv['x2'], dx, sv['g2'], sv['u2'])
        do_a, do_b, dga, dgb, dwo = _post_bwd(dx, sv['o_a'], sv['o_b'], row('mla_out_norm', l), row('swa_out_norm', l),
                                              big['w_o'][l], f"post_bwd_{l}")
        gs['mla_out_norm'][l], gs['swa_out_norm'][l] = dga[0], dgb[0]
        gb['w_o'][l] = dwo.reshape(N_SHARD, MIX_WIDTH // N_SHARD, D_MODEL)
        dq_a, delta = _mla_bwd_dq(sv['q_a'], sv['k_a'], sv['v_a'], sv['o_a'], do_a, sv['lse'], f"mla_dq_{l}")
        dk_a, dv_a = _mla_bwd_dkv(sv['q_a'], sv['k_a'], sv['v_a'], do_a, sv['lse'].reshape(MLA_HEADS, s // t, t),
                                  delta.reshape(MLA_HEADS, s // t, t), f"mla_dkv_{l}")
        dq_b, dkpad, dvpad, dsink = _swa_bwd(sv['q_b'], sv['kpad'], sv['vpad'], sv['sinks'], do_b, f"swa_bwd_{l}")
        gs['swa_sinks'][l] = dsink[:, :SWA_GROUP, 0].reshape(SWA_HEADS)
        cts = [dq_a, dk_a, dv_a, dq_b, dkpad[:, BLOCK:], dvpad[:, BLOCK:]]
        outs = _pre_bwd(sv['x1'], dx, cts, sv['gains'], big['w_in'][l], big['mla_w_q_b'][l], big['mla_w_kv_b'][l],
                        cos, sin, f"pre_bwd_{l}")
        dx = outs[0]
        for n, val in zip(('mix_norm', 'mla_q_a_norm', 'mla_kv_a_norm', 'mla_q_norm', 'mla_k_norm', 'swa_q_norm', 'swa_k_norm'),
                          outs[1:8]):
            gs[n][l] = val[0]
        gb['w_in'][l], gb['mla_w_q_b'][l], gb['mla_w_kv_b'][l] = outs[8:11]
        dx = ffn_back('ffn1', sv['x0'], dx, sv['g1'], sv['u1'])
    return loss, dx, gs, gb


def kernel(x, ffn1_norm, ffn1_w_gate, ffn1_w_up, ffn1_w_down, mix_norm, w_in, mla_q_a_norm, mla_w_q_b, mla_kv_a_norm, mla_w_kv_b, mla_q_norm, mla_k_norm, swa_q_norm, swa_k_norm, swa_sinks, mla_out_norm, swa_out_norm, w_o, ffn2_norm, ffn2_w_gate, ffn2_w_up, ffn2_w_down, loss_target, m_ffn1_norm, m_ffn1_w_gate, m_ffn1_w_up, m_ffn1_w_down, m_mix_norm, m_w_in, m_mla_q_a_norm, m_mla_w_q_b, m_mla_kv_a_norm, m_mla_w_kv_b, m_mla_q_norm, m_mla_k_norm, m_swa_q_norm, m_swa_k_norm, m_swa_sinks, m_mla_out_norm, m_swa_out_norm, m_w_o, m_ffn2_norm, m_ffn2_w_gate, m_ffn2_w_up, m_ffn2_w_down, v_ffn1_norm, v_ffn1_w_gate, v_ffn1_w_up, v_ffn1_w_down, v_mix_norm, v_w_in, v_mla_q_a_norm, v_mla_w_q_b, v_mla_kv_a_norm, v_mla_w_kv_b, v_mla_q_norm, v_mla_k_norm, v_swa_q_norm, v_swa_k_norm, v_swa_sinks, v_mla_out_norm, v_swa_out_norm, v_w_o, v_ffn2_norm, v_ffn2_w_gate, v_ffn2_w_up, v_ffn2_w_down):
    args = dict(locals())
    weights = {n: args[n] for n in WEIGHT_NAMES}
    mom_m = {n: args["m_" + n] for n in WEIGHT_NAMES}
    mom_v = {n: args["v_" + n] for n in WEIGHT_NAMES}

    mine = [jnp.concatenate([weights[n].astype(BF16) for n in group], axis=1) for group in GROUPS]
    c = lax.axis_index("c")
    chip = 2 * lax.axis_index("x") + lax.axis_index("y")
    gathered = [g.reshape((N_SHARD, DEPTH) + g.shape[1:]) for g in _all_gather_layers(mine, "gather_weights")]
    big = {}
    for gi, group in enumerate(GROUPS):
        offs = _group_row_offsets(group)
        col_sharded = BIG[group[0]][1] == 1
        full = _assemble(gathered[gi], mine[gi], f"assemble_{gi}", col_sharded)
        for i, n in enumerate(group):
            if col_sharded:
                big[n] = [full[l, offs[i]:offs[i + 1]] for l in range(DEPTH)]
            else:
                big[n] = [full[l, :, offs[i]:offs[i + 1]].reshape(BIG[n][0]) for l in range(DEPTH)]
    small = {n: weights[n] for n in SMALL_NAMES}

    loss, dx, gs, gb = _local_step(x[0], loss_target[0], small, big)

    small_flat = jnp.concatenate([jnp.stack(gs[n]).reshape(-1) for n in SMALL_NAMES] + [loss[0, :1]])
    n_small = small_flat.shape[0]
    lanes = -(-n_small // (8 * 128)) * 128
    small_sum = _all_reduce_small(jnp.pad(small_flat, (0, 8 * lanes - n_small)).reshape(8, lanes), "reduce_small").reshape(-1)
    grads = {}
    off = 0
    for n in SMALL_NAMES:
        cnt = int(np.prod(weights[n].shape))
        grads[n] = small_sum[off:off + cnt].reshape(weights[n].shape)
        off += cnt
    loss_out = small_sum[off]

    parts = [jnp.stack([jnp.concatenate([gb[n][l] for n in group], axis=1) for l in range(DEPTH)]) for group in GROUPS]
    from_sibling = _sibling_exchange(parts, "swap_layers", True)
    chip_sums = []
    for gi, (p, got) in enumerate(zip(parts, from_sibling)):
        kept = lax.dynamic_index_in_dim(p, c, axis=0, keepdims=False)
        rows = N_SHARD * p.shape[2]
        pair = _sum_blocks([kept.reshape(rows, -1), got.reshape(rows, -1)], BF16, f"sum_pair_{gi}")
        chip_sums.append(pair.reshape(p.shape[1:]))
    from_chips = _scatter_to_chips(chip_sums, "scatter_chips")
    halves = []
    for gi, (cs, got) in enumerate(zip(chip_sums, from_chips)):
        own = lax.dynamic_index_in_dim(cs, chip, axis=0, keepdims=False)
        halves.append(_sum_blocks([own, got[0], got[1], got[2]], F32, f"sum_chips_{gi}"))
    others = _sibling_exchange(halves, "share_layers", False)
    for gi, group in enumerate(GROUPS):
        offs = _group_row_offsets(group)
        both = jnp.stack([jnp.where(c == 0, halves[gi], others[gi]), jnp.where(c == 0, others[gi], halves[gi])])
        for i, n in enumerate(group):
            grads[n] = both[:, offs[i]:offs[i + 1]]

    deltas, new_m, new_v = {}, {}, {}
    for n in WEIGHT_NAMES:
        shp = weights[n].shape
        two_d = (DEPTH, shp[-1]) if len(shp) == 2 else (shp[0] * shp[1], shp[2])
        d, nm, nv = _adamw(weights[n].reshape(two_d), grads[n].reshape(two_d), mom_m[n].reshape(two_d),
                           mom_v[n].reshape(two_d), f"adamw_{n}")
        deltas[n], new_m[n], new_v[n] = d.reshape(shp), nm.reshape(shp), nv.reshape(shp)

    return (loss_out, dx[None], *[grads[n] for n in WEIGHT_NAMES], *[deltas[n] for n in WEIGHT_NAMES],
            *[new_m[n] for n in WEIGHT_NAMES], *[new_v[n] for n in WEIGHT_NAMES])
```

```python
import functools

import numpy as np
import jax
import jax.numpy as jnp
from jax import lax
from jax.experimental import pallas as pl
from jax.experimental.pallas import tpu as pltpu

F32 = jnp.float32
BF16 = jnp.bfloat16

D_MODEL = 1024
DEPTH = 2
EPS = 1e-6
ROPE_THETA = 10000.0
BLOCK = 128
MLA_HEADS = 4
MLA_Q_RANK = 256
MLA_KV_RANK = 128
MLA_NOPE = 128
MLA_ROPE = 64
MLA_V = 128
MLA_QK = MLA_NOPE + MLA_ROPE
MLA_WIDTH = MLA_HEADS * MLA_V
SWA_HEADS = 8
SWA_KV_HEADS = 2
SWA_GROUP = SWA_HEADS // SWA_KV_HEADS
SWA_HEAD_DIM = 64
SWA_WIDTH = SWA_HEADS * SWA_HEAD_DIM
MIX_WIDTH = MLA_WIDTH + SWA_WIDTH
IN_SPLITS = (MLA_Q_RANK, MLA_KV_RANK, MLA_ROPE, SWA_WIDTH, SWA_KV_HEADS * SWA_HEAD_DIM, SWA_KV_HEADS * SWA_HEAD_DIM)
IN_COLS = sum(IN_SPLITS)
IN_OFFS = tuple(int(v) for v in np.cumsum((0,) + IN_SPLITS))
D_FF = 2816
MLA_SCALE = MLA_QK ** -0.5
LOG2E = 1.4426950408889634
LN2 = 0.6931471805599453
MLA_QSCALE = MLA_SCALE * LOG2E
SWA_SCALE = SWA_HEAD_DIM ** -0.5
NEG = -1e30

ADAM_LR = 0.001
ADAM_B1 = 0.9
ADAM_B2 = 0.999
ADAM_EPS = 1e-08
ADAM_WD = 0.01
ADAM_STEP = 10

N_SHARD = 4
N_DEV = 8
VMEM_LIMIT = 56 * 1024 * 1024
MESH = pl.DeviceIdType.MESH

WEIGHT_NAMES = ['ffn1_norm', 'ffn1_w_gate', 'ffn1_w_up', 'ffn1_w_down', 'mix_norm', 'w_in', 'mla_q_a_norm', 'mla_w_q_b',
                'mla_kv_a_norm', 'mla_w_kv_b', 'mla_q_norm', 'mla_k_norm', 'swa_q_norm', 'swa_k_norm', 'swa_sinks',
                'mla_out_norm', 'swa_out_norm', 'w_o', 'ffn2_norm', 'ffn2_w_gate', 'ffn2_w_up', 'ffn2_w_down']
BIG = {'ffn1_w_gate': ((D_MODEL, D_FF), 1), 'ffn1_w_up': ((D_MODEL, D_FF), 1), 'ffn1_w_down': ((D_FF, D_MODEL), 0),
       'w_in': ((D_MODEL, IN_COLS), 1), 'mla_w_q_b': ((MLA_Q_RANK, MLA_HEADS * MLA_QK), 1),
       'mla_w_kv_b': ((MLA_KV_RANK, MLA_HEADS * (MLA_NOPE + MLA_V)), 1), 'w_o': ((MIX_WIDTH, D_MODEL), 0),
       'ffn2_w_gate': ((D_MODEL, D_FF), 1), 'ffn2_w_up': ((D_MODEL, D_FF), 1), 'ffn2_w_down': ((D_FF, D_MODEL), 0)}
BIG_NAMES = [n for n in WEIGHT_NAMES if n in BIG]
SMALL_NAMES = [n for n in WEIGHT_NAMES if n not in BIG]

_pallas_call = pl.pallas_call


def _params(**kw):
    return pltpu.CompilerParams(vmem_limit_bytes=VMEM_LIMIT, **kw)


def _full(shape):
    n = len(shape)
    return pl.BlockSpec(shape, lambda *_: (0,) * n)


def _resident(shape):
    n = len(shape)
    return pl.BlockSpec(shape, lambda *_: (0,) * n, pipeline_mode=pl.Buffered(1))


@jax.custom_vjp
def _mm(a, w):
    return jnp.dot(a.astype(BF16), w, preferred_element_type=F32)


def _mm_fwd(a, w):
    return _mm(a, w), w


def _mm_bwd(w, dy):
    return lax.dot_general(dy.astype(BF16), w, (((1,), (1,)), ((), ())), preferred_element_type=F32), None


_mm.defvjp(_mm_fwd, _mm_bwd)


def _dot_nt(a, b):
    return lax.dot_general(a, b, (((1,), (1,)), ((), ())), preferred_element_type=F32)


def _dot_tn(a, b):
    return lax.dot_general(a, b, (((0,), (0,)), ((), ())), preferred_element_type=F32)


def _rms(t, g):
    return t * lax.rsqrt(jnp.mean(t * t, axis=-1, keepdims=True) + EPS) * g


def _rope(t, cos, sin):
    half = t.shape[-1] // 2
    t1, t2 = t[:, :half], t[:, half:]
    return jnp.concatenate([t1 * cos - t2 * sin, t2 * cos + t1 * sin], axis=-1)


def _sigmoid(z):
    return 1.0 / (1.0 + jnp.exp(-z))


def _row_tile(s, want):
    return min(want, s)


FF_CHUNK = 1408


def _ffn_fwd(x, g, wg, wu, wd, name):
    s = x.shape[0]
    tm = _row_tile(s, 256)

    def body(x_ref, g_ref, wg_ref, wu_ref, wd_ref, y_ref, gate_ref, up_ref):
        xv = x_ref[...]
        nb = _rms(xv, g_ref[...]).astype(BF16)
        acc = xv
        for c in range(0, D_FF, FF_CHUNK):
            gate = jnp.dot(nb, wg_ref[:, c:c + FF_CHUNK], preferred_element_type=F32)
            up = jnp.dot(nb, wu_ref[:, c:c + FF_CHUNK], preferred_element_type=F32)
            gate_ref[:, c:c + FF_CHUNK] = gate.astype(BF16)
            up_ref[:, c:c + FF_CHUNK] = up.astype(BF16)
            act = (gate * _sigmoid(gate) * up).astype(BF16)
            acc = acc + 0.5 * jnp.dot(act, wd_ref[c:c + FF_CHUNK, :], preferred_element_type=F32)
        y_ref[...] = acc

    return _pallas_call(
        body, name=name, grid=(s // tm,),
        in_specs=[pl.BlockSpec((tm, D_MODEL), lambda i: (i, 0)), _full((1, D_MODEL)),
                  _resident((D_MODEL, D_FF)), _resident((D_MODEL, D_FF)), _resident((D_FF, D_MODEL))],
        out_specs=[pl.BlockSpec((tm, D_MODEL), lambda i: (i, 0)), pl.BlockSpec((tm, D_FF), lambda i: (i, 0)),
                   pl.BlockSpec((tm, D_FF), lambda i: (i, 0))],
        out_shape=[jax.ShapeDtypeStruct((s, D_MODEL), F32), jax.ShapeDtypeStruct((s, D_FF), BF16),
                   jax.ShapeDtypeStruct((s, D_FF), BF16)],
        compiler_params=_params(dimension_semantics=("arbitrary",)),
    )(x, g, wg, wu, wd)


def _ffn_bwd(x, dy, gate, up, g, wg, wu, wd, name):
    s = x.shape[0]
    tm = _row_tile(s, 256)

    def body(x_ref, dy_ref, gate_ref, up_ref, g_ref, wg_ref, wu_ref, wd_ref,
             dx_ref, dgain_ref, n_ref, act_ref, dgate_ref, dup_ref):
        i = pl.program_id(0)
        xv = x_ref[...]
        dyv = dy_ref[...]
        gv = g_ref[...]
        r = lax.rsqrt(jnp.mean(xv * xv, axis=-1, keepdims=True) + EPS)
        xh = xv * r
        n_ref[...] = (xh * gv).astype(BF16)
        dyh = (0.5 * dyv).astype(BF16)
        dn = jnp.zeros_like(xv)
        for c in range(0, D_FF, FF_CHUNK):
            dact = _dot_nt(dyh, wd_ref[c:c + FF_CHUNK, :])
            gt = gate_ref[:, c:c + FF_CHUNK].astype(F32)
            u = up_ref[:, c:c + FF_CHUNK].astype(F32)
            sg = _sigmoid(gt)
            sl = gt * sg
            act_ref[:, c:c + FF_CHUNK] = (sl * u).astype(BF16)
            dup = (dact * sl).astype(BF16)
            dgate = (dact * u * (sg * (1.0 + gt * (1.0 - sg)))).astype(BF16)
            dup_ref[:, c:c + FF_CHUNK] = dup
            dgate_ref[:, c:c + FF_CHUNK] = dgate
            dn = dn + _dot_nt(dgate, wg_ref[:, c:c + FF_CHUNK]) + _dot_nt(dup, wu_ref[:, c:c + FF_CHUNK])
        part = jnp.sum(dn * xh, axis=0, keepdims=True)

        @pl.when(i == 0)
        def _():
            dgain_ref[...] = part

        @pl.when(i > 0)
        def _():
            dgain_ref[...] += part

        dxh = dn * gv
        dx_ref[...] = dyv + r * (dxh - xh * jnp.mean(dxh * xh, axis=-1, keepdims=True))

    row = lambda w: pl.BlockSpec((tm, w), lambda i: (i, 0))
    return _pallas_call(
        body, name=name, grid=(s // tm,),
        in_specs=[row(D_MODEL), row(D_MODEL), row(D_FF), row(D_FF), _full((1, D_MODEL)),
                  _resident((D_MODEL, D_FF)), _resident((D_MODEL, D_FF)), _resident((D_FF, D_MODEL))],
        out_specs=[row(D_MODEL), _full((1, D_MODEL)), row(D_MODEL), row(D_FF), row(D_FF), row(D_FF)],
        out_shape=[jax.ShapeDtypeStruct((s, D_MODEL), F32), jax.ShapeDtypeStruct((1, D_MODEL), F32),
                   jax.ShapeDtypeStruct((s, D_MODEL), BF16), jax.ShapeDtypeStruct((s, D_FF), BF16),
                   jax.ShapeDtypeStruct((s, D_FF), BF16), jax.ShapeDtypeStruct((s, D_FF), BF16)],
        compiler_params=_params(dimension_semantics=("arbitrary",)),
    )(x, dy, gate, up, g, wg, wu, wd)


def _store_col_shards(o_ref, acc, first_shard, n_here, width):
    for q in range(n_here):
        o_ref[q] = acc[:, (first_shard + q) * width:(first_shard + q + 1) * width].astype(BF16)


def _matmul_tn(a, b, scale, name, col_shards):
    t, m = a.shape
    n = b.shape[1]
    tk = _row_tile(t, 512)
    tn = n // 2
    per = n // N_SHARD
    nk = t // tk

    def body(a_ref, b_ref, o_ref, acc_ref):
        k = pl.program_id(1)
        bv = b_ref[...]
        if scale != 1.0:
            bv = bv.astype(F32) * scale
        part = _dot_tn(a_ref[...].astype(BF16), bv.astype(BF16))

        @pl.when(k == 0)
        def _():
            acc_ref[...] = part

        @pl.when(k > 0)
        def _():
            acc_ref[...] += part

        @pl.when(k == nk - 1)
        def _():
            if col_shards:
                _store_col_shards(o_ref, acc_ref[...], 0, tn // per, per)
            else:
                o_ref[...] = acc_ref[...].astype(BF16)

    if col_shards:
        out_spec = pl.BlockSpec((tn // per, m, per), lambda j, k: (j, 0, 0))
        out_shape = jax.ShapeDtypeStruct((N_SHARD, m, per), BF16)
    else:
        out_spec = pl.BlockSpec((m, tn), lambda j, k: (0, j))
        out_shape = jax.ShapeDtypeStruct((m, n), BF16)
    return _pallas_call(
        body, name=name, grid=(n // tn, nk),
        in_specs=[pl.BlockSpec((tk, m), lambda j, k: (k, 0)), pl.BlockSpec((tk, tn), lambda j, k: (k, j))],
        out_specs=out_spec, out_shape=out_shape, scratch_shapes=[pltpu.VMEM((m, tn), F32)],
        compiler_params=_params(dimension_semantics=("arbitrary", "arbitrary")),
    )(a, b)


def _pre_math(x, gm, gqa, gkva, gq, gk, gsq, gsk, taps, win, wqb, wkvb, cos, sin):
    h = _rms(x, gm)
    proj = _mm(h, win)
    if taps is not None:
        proj = proj + taps[0]
    o = IN_OFFS
    c_q, c_kv, k_pe = proj[:, o[0]:o[1]], proj[:, o[1]:o[2]], proj[:, o[2]:o[3]]
    q_s, k_s, v_s = proj[:, o[3]:o[4]], proj[:, o[4]:o[5]], proj[:, o[5]:o[6]]
    cqn = _rms(c_q, gqa)
    qa_all = _mm(cqn, wqb)
    ckvn = _rms(c_kv, gkva)
    kv_all = _mm(ckvn, wkvb)
    if taps is not None:
        qa_all = qa_all + taps[1]
        kv_all = kv_all + taps[2]
    q_a, k_a, v_a = [], [], []
    kvw = MLA_NOPE + MLA_V
    for hd in range(MLA_HEADS):
        qn = _rms(qa_all[:, hd * MLA_QK:(hd + 1) * MLA_QK], gq)
        k_nope = kv_all[:, hd * kvw:hd * kvw + MLA_NOPE]
        v_a.append(kv_all[:, hd * kvw + MLA_NOPE:(hd + 1) * kvw])
        kn = _rms(jnp.concatenate([k_nope, k_pe], axis=-1), gk)
        q_a.append(jnp.concatenate([qn[:, :MLA_NOPE], _rope(qn[:, MLA_NOPE:], cos, sin)], axis=-1))
        k_a.append(jnp.concatenate([kn[:, :MLA_NOPE], _rope(kn[:, MLA_NOPE:], cos, sin)], axis=-1))
    d = SWA_HEAD_DIM
    q_b = [_rope(_rms(q_s[:, hd * d:(hd + 1) * d], gsq), cos, sin) for hd in range(SWA_HEADS)]
    k_b = [_rope(_rms(k_s[:, j * d:(j + 1) * d], gsk), cos, sin) for j in range(SWA_KV_HEADS)]
    v_b = [v_s[:, j * d:(j + 1) * d] for j in range(SWA_KV_HEADS)]
    return (q_a, k_a, v_a, q_b, k_b, v_b), (h, cqn, ckvn)


_PRE_GAIN_WIDTHS = (D_MODEL, MLA_Q_RANK, MLA_KV_RANK, MLA_QK, MLA_QK, SWA_HEAD_DIM, SWA_HEAD_DIM)
_PRE_HEADS = ((MLA_HEADS, MLA_QK), (MLA_HEADS, MLA_QK), (MLA_HEADS, MLA_V),
              (SWA_HEADS, SWA_HEAD_DIM), (SWA_KV_HEADS, SWA_HEAD_DIM), (SWA_KV_HEADS, SWA_HEAD_DIM))


def _pre_fwd(x, gains, win, wqb, wkvb, cos, sin, name):
    s = x.shape[0]
    tm = _row_tile(s, 256)

    def body(x_ref, *refs):
        g_refs, (win_ref, wqb_ref, wkvb_ref, cos_ref, sin_ref), out_refs = refs[:7], refs[7:12], refs[12:]
        outs, _ = _pre_math(x_ref[...], *[g[...] for g in g_refs], None, win_ref[...], wqb_ref[...], wkvb_ref[...],
                            cos_ref[...], sin_ref[...])
        for idx, (ref, heads) in enumerate(zip(out_refs, outs)):
            for hd, val in enumerate(heads):
                ref[hd] = (val * MLA_QSCALE if idx == 0 else val).astype(BF16)

    heads_spec = lambda nh, w: pl.BlockSpec((nh, tm, w), lambda i: (0, i, 0))
    return _pallas_call(
        body, name=name, grid=(s // tm,),
        in_specs=[pl.BlockSpec((tm, D_MODEL), lambda i: (i, 0))] + [_full((1, w)) for w in _PRE_GAIN_WIDTHS]
        + [_resident(win.shape), _resident(wqb.shape), _resident(wkvb.shape),
           pl.BlockSpec((tm, MLA_ROPE // 2), lambda i: (i, 0)), pl.BlockSpec((tm, MLA_ROPE // 2), lambda i: (i, 0))],
        out_specs=[heads_spec(nh, w) for nh, w in _PRE_HEADS],
        out_shape=[jax.ShapeDtypeStruct((nh, s, w), BF16) for nh, w in _PRE_HEADS],
        compiler_params=_params(dimension_semantics=("arbitrary",)),
    )(x, *gains, win, wqb, wkvb, cos, sin)


def _pre_bwd(x, dx_res, cts, gains, win, wqb, wkvb, cos, sin, name):
    s = x.shape[0]
    tm = _row_tile(s, 256)
    tap_widths = (IN_COLS, MLA_HEADS * MLA_QK, MLA_HEADS * (MLA_NOPE + MLA_V))

    def body(x_ref, dxr_ref, *refs):
        ct_refs, g_refs = refs[:6], refs[6:13]
        win_ref, wqb_ref, wkvb_ref, cos_ref, sin_ref = refs[13:18]
        dx_ref, dg_refs, dw_refs, acc_refs = refs[18], refs[19:26], refs[26:29], refs[29:32]
        i = pl.program_id(0)
        win_v, wqb_v, wkvb_v, cos_v, sin_v = win_ref[...], wqb_ref[...], wkvb_ref[...], cos_ref[...], sin_ref[...]

        def f(xv, gm, gqa, gkva, gq, gk, gsq, gsk, t0, t1, t2):
            return _pre_math(xv, gm, gqa, gkva, gq, gk, gsq, gsk, (t0, t1, t2), win_v, wqb_v, wkvb_v, cos_v, sin_v)

        taps = [jnp.zeros((tm, w), F32) for w in tap_widths]
        _, vjp, acts = jax.vjp(f, x_ref[...], *[g[...] for g in g_refs], *taps, has_aux=True)
        ct = tuple([ref[hd] for hd in range(nh)] for ref, (nh, _) in zip(ct_refs, _PRE_HEADS))
        grads = vjp(ct)
        dx_ref[...] = grads[0] + dxr_ref[...]
        dws = [_dot_tn(a.astype(BF16), t.astype(BF16)) for a, t in zip(acts, grads[8:11])]

        @pl.when(i == 0)
        def _():
            for ref, val in zip(dg_refs, grads[1:8]):
                ref[...] = val
            for ref, val in zip(acc_refs, dws):
                ref[...] = val

        @pl.when(i > 0)
        def _():
            for ref, val in zip(dg_refs, grads[1:8]):
                ref[...] += val
            for ref, val in zip(acc_refs, dws):
                ref[...] += val

        @pl.when(i == s // tm - 1)
        def _():
            for ref, acc in zip(dw_refs, acc_refs):
                _store_col_shards(ref, acc[...], 0, N_SHARD, acc.shape[1] // N_SHARD)

    heads_spec = lambda nh, w: pl.BlockSpec((nh, tm, w), lambda i: (0, i, 0))
    row = pl.BlockSpec((tm, D_MODEL), lambda i: (i, 0))
    half = pl.BlockSpec((tm, MLA_ROPE // 2), lambda i: (i, 0))
    shard_shapes = [(N_SHARD, w.shape[0], w.shape[1] // N_SHARD) for w in (win, wqb, wkvb)]
    return _pallas_call(
        body, name=name, grid=(s // tm,),
        in_specs=[row, row] + [heads_spec(nh, w) for nh, w in _PRE_HEADS] + [_full((1, w)) for w in _PRE_GAIN_WIDTHS]
        + [_resident(win.shape), _resident(wqb.shape), _resident(wkvb.shape), half, half],
        out_specs=[row] + [_full((1, w)) for w in _PRE_GAIN_WIDTHS] + [_full(shp) for shp in shard_shapes],
        out_shape=[jax.ShapeDtypeStruct((s, D_MODEL), F32)] + [jax.ShapeDtypeStruct((1, w), F32) for w in _PRE_GAIN_WIDTHS]
        + [jax.ShapeDtypeStruct(shp, BF16) for shp in shard_shapes],
        scratch_shapes=[pltpu.VMEM(w.shape, F32) for w in (win, wqb, wkvb)],
        compiler_params=_params(dimension_semantics=("arbitrary",)),
    )(x, dx_res, *cts, *gains, win, wqb, wkvb, cos, sin)


def _post_math(oa, ob, ga, gb, wo):
    mixed = jnp.concatenate([_rms(jnp.concatenate(oa, axis=-1), ga), _rms(jnp.concatenate(ob, axis=-1), gb)], axis=-1)
    return _mm(mixed, wo), mixed


def _post_fwd(x, oa, ob, ga, gb, wo, name):
    s = x.shape[0]
    tm = _row_tile(s, 256)

    def body(x_ref, oa_ref, ob_ref, ga_ref, gb_ref, wo_ref, y_ref):
        y, _ = _post_math([oa_ref[hd] for hd in range(MLA_HEADS)], [ob_ref[hd] for hd in range(SWA_HEADS)],
                          ga_ref[...], gb_ref[...], wo_ref[...])
        y_ref[...] = x_ref[...] + y

    row = pl.BlockSpec((tm, D_MODEL), lambda i: (i, 0))
    return _pallas_call(
        body, name=name, grid=(s // tm,),
        in_specs=[row, pl.BlockSpec((MLA_HEADS, tm, MLA_V), lambda i: (0, i, 0)),
                  pl.BlockSpec((SWA_HEADS, tm, SWA_HEAD_DIM), lambda i: (0, i, 0)),
                  _full((1, MLA_WIDTH)), _full((1, SWA_WIDTH)), _resident(wo.shape)],
        out_specs=row, out_shape=jax.ShapeDtypeStruct((s, D_MODEL), F32),
        compiler_params=_params(dimension_semantics=("arbitrary",)),
    )(x, oa, ob, ga, gb, wo)


def _post_bwd(dy, oa, ob, ga, gb, wo, name):
    s = dy.shape[0]
    tm = _row_tile(s, 256)

    def body(dy_ref, oa_ref, ob_ref, ga_ref, gb_ref, wo_ref, doa_ref, dob_ref, dga_ref, dgb_ref, dwo_ref, acc_ref):
        i = pl.program_id(0)
        wo_v = wo_ref[...]
        dyv = dy_ref[...]

        def f(oa_l, ob_l, ga_v, gb_v):
            return _post_math(oa_l, ob_l, ga_v, gb_v, wo_v)

        _, vjp, mixed = jax.vjp(f, [oa_ref[hd] for hd in range(MLA_HEADS)], [ob_ref[hd] for hd in range(SWA_HEADS)],
                                ga_ref[...], gb_ref[...], has_aux=True)
        doa, dob, dga, dgb = vjp(dyv)
        for hd in range(MLA_HEADS):
            doa_ref[hd] = doa[hd]
        for hd in range(SWA_HEADS):
            dob_ref[hd] = dob[hd]
        dwo = _dot_tn(mixed.astype(BF16), dyv.astype(BF16))

        @pl.when(i == 0)
        def _():
            dga_ref[...] = dga
            dgb_ref[...] = dgb
            acc_ref[...] = dwo

        @pl.when(i > 0)
        def _():
            dga_ref[...] += dga
            dgb_ref[...] += dgb
            acc_ref[...] += dwo

        @pl.when(i == s // tm - 1)
        def _():
            dwo_ref[...] = acc_ref[...].astype(BF16)

    row = pl.BlockSpec((tm, D_MODEL), lambda i: (i, 0))
    oa_spec = pl.BlockSpec((MLA_HEADS, tm, MLA_V), lambda i: (0, i, 0))
    ob_spec = pl.BlockSpec((SWA_HEADS, tm, SWA_HEAD_DIM), lambda i: (0, i, 0))
    return _pallas_call(
        body, name=name, grid=(s // tm,),
        in_specs=[row, oa_spec, ob_spec, _full((1, MLA_WIDTH)), _full((1, SWA_WIDTH)), _resident(wo.shape)],
        out_specs=[oa_spec, ob_spec, _full((1, MLA_WIDTH)), _full((1, SWA_WIDTH)), _full(wo.shape)],
        out_shape=[jax.ShapeDtypeStruct((MLA_HEADS, s, MLA_V), F32), jax.ShapeDtypeStruct((SWA_HEADS, s, SWA_HEAD_DIM), F32),
                   jax.ShapeDtypeStruct((1, MLA_WIDTH), F32), jax.ShapeDtypeStruct((1, SWA_WIDTH), F32),
                   jax.ShapeDtypeStruct(wo.shape, BF16)],
        scratch_shapes=[pltpu.VMEM(wo.shape, F32)],
        compiler_params=_params(dimension_semantics=("arbitrary",)),
    )(dy, oa, ob, ga, gb, wo)


def _attn_tile(s):
    return 512 if s >= 2048 else 128


def _causal_mask(t):
    return lax.broadcasted_iota(jnp.int32, (t, t), 1) <= lax.broadcasted_iota(jnp.int32, (t, t), 0)


def _pipelined_blocks(first, count, last_block, issue, consume, carry, prefetch_after):
    def clamped(j, slot):
        issue(jnp.minimum(j, last_block), slot)

    def pair(jj, c):
        a = first + 2 * jj
        clamped(a + 1, 1)
        c = consume(a, 0, c)
        clamped(a + 2, 0)
        return consume(a + 1, 1, c)

    clamped(first, 0)
    npairs = count // 2
    carry = lax.fori_loop(0, npairs, pair, carry)

    def odd(c):
        c = consume(first + 2 * npairs, 0, c)
        if prefetch_after:
            clamped(first + count, 0)
        return c

    return lax.cond(count - 2 * npairs == 1, odd, lambda c: c, carry)


def _mla_fwd(q, k, v, name):
    nh, s, _ = q.shape
    t = _attn_tile(s)
    nq = s // t

    def body(q_ref, k_ref, v_ref, o_ref, lse_ref, s0_ref, s1_ref):
        qi = pl.program_id(1)
        qv = q_ref[...]
        s_refs = (s0_ref, s1_ref)

        def rows(j):
            return pl.ds(pl.multiple_of(j * t, t), t)

        def issue(j, slot):
            s_refs[slot][...] = _dot_nt(qv, k_ref[rows(j), :])

        def consume(j, slot, carry, masked=False):
            m, l, acc = carry
            sc = s_refs[slot][...]
            if masked:
                sc = jnp.where(_causal_mask(t), sc, NEG)
            m_new = jnp.maximum(m, jnp.max(sc, axis=-1, keepdims=True))
            alpha = jnp.exp2(m - m_new)
            p = jnp.exp2(sc - m_new)
            l = alpha * l + jnp.sum(p, axis=-1, keepdims=True)
            acc = alpha * acc + jnp.dot(p.astype(BF16), v_ref[rows(j), :], preferred_element_type=F32)
            return m_new, l, acc

        init = (jnp.full((t, 1), NEG, F32), jnp.zeros((t, 1), F32), jnp.zeros((t, MLA_V), F32))
        carry = _pipelined_blocks(0, qi, nq - 1, issue, consume, init, True)
        m, l, acc = consume(qi, 0, carry, masked=True)
        o_ref[...] = acc / l
        lse_ref[...] = m + jnp.log2(l)

    return _pallas_call(
        body, name=name, grid=(nh, nq),
        in_specs=[pl.BlockSpec((None, t, MLA_QK), lambda h, i: (h, i, 0)), pl.BlockSpec((None, s, MLA_QK), lambda h, i: (h, 0, 0)),
                  pl.BlockSpec((None, s, MLA_V), lambda h, i: (h, 0, 0))],
        out_specs=[pl.BlockSpec((None, t, MLA_V), lambda h, i: (h, i, 0)), pl.BlockSpec((None, t, 1), lambda h, i: (h, i, 0))],
        out_shape=[jax.ShapeDtypeStruct((nh, s, MLA_V), F32), jax.ShapeDtypeStruct((nh, s, 1), F32)],
        scratch_shapes=[pltpu.VMEM((t, t), F32)] * 2,
        compiler_params=_params(dimension_semantics=("arbitrary", "arbitrary")),
    )(q, k, v)


def _mla_bwd_dq(q, k, v, o, do, lse, name):
    nh, s, _ = q.shape
    t = _attn_tile(s)
    nq = s // t

    def body(q_ref, k_ref, v_ref, o_ref, do_ref, lse_ref, dq_ref, delta_ref, s0_ref, s1_ref, dp0_ref, dp1_ref):
        qi = pl.program_id(1)
        qv = q_ref[...]
        dov = do_ref[...]
        lse = lse_ref[...]
        delta = jnp.sum(dov * o_ref[...], axis=-1, keepdims=True)
        delta_ref[...] = delta
        dob = dov.astype(BF16)
        s_refs, dp_refs = (s0_ref, s1_ref), (dp0_ref, dp1_ref)

        def rows(j):
            return pl.ds(pl.multiple_of(j * t, t), t)

        def issue(j, slot):
            s_refs[slot][...] = _dot_nt(qv, k_ref[rows(j), :])
            dp_refs[slot][...] = _dot_nt(dob, v_ref[rows(j), :])

        def consume(j, slot, dq, masked=False):
            p = jnp.exp2(s_refs[slot][...] - lse)
            if masked:
                p = jnp.where(_causal_mask(t), p, 0.0)
            ds = p * (dp_refs[slot][...] - delta)
            return dq + jnp.dot(ds.astype(BF16), k_ref[rows(j), :], preferred_element_type=F32)

        dq = _pipelined_blocks(0, qi, nq - 1, issue, consume, jnp.zeros((t, MLA_QK), F32), True)
        dq_ref[...] = consume(qi, 0, dq, masked=True) * MLA_SCALE

    tile = lambda w: pl.BlockSpec((None, t, w), lambda h, i: (h, i, 0))
    whole = lambda w: pl.BlockSpec((None, s, w), lambda h, i: (h, 0, 0))
    return _pallas_call(
        body, name=name, grid=(nh, nq),
        in_specs=[tile(MLA_QK), whole(MLA_QK), whole(MLA_V), tile(MLA_V), tile(MLA_V), tile(1)],
        out_specs=[tile(MLA_QK), tile(1)],
        out_shape=[jax.ShapeDtypeStruct((nh, s, MLA_QK), F32), jax.ShapeDtypeStruct((nh, s, 1), F32)],
        scratch_shapes=[pltpu.VMEM((t, t), F32)] * 4,
        compiler_params=_params(dimension_semantics=("arbitrary", "arbitrary")),
    )(q, k, v, o, do, lse)


def _mla_bwd_dkv(q, k, v, do, lse_row, delta_row, name):
    nh, s, _ = q.shape
    t = _attn_tile(s)
    nq = s // t

    def body(q_ref, k_ref, v_ref, do_ref, lse_ref, delta_ref, dk_ref, dv_ref, s0_ref, s1_ref, dp0_ref, dp1_ref):
        kj = pl.program_id(1)
        kv_, vv = k_ref[...], v_ref[...]
        s_refs, dp_refs = (s0_ref, s1_ref), (dp0_ref, dp1_ref)

        def rows(i):
            return pl.ds(pl.multiple_of(i * t, t), t)

        def issue(i, slot):
            s_refs[slot][...] = _dot_nt(kv_, q_ref[rows(i), :])
            dp_refs[slot][...] = _dot_nt(vv, do_ref[rows(i), :].astype(BF16))

        def consume(i, slot, carry, masked=False):
            dk, dv = carry
            p = jnp.exp2(s_refs[slot][...] - lse_ref[pl.ds(i, 1), :])
            if masked:
                p = jnp.where(lax.broadcasted_iota(jnp.int32, (t, t), 0) <= lax.broadcasted_iota(jnp.int32, (t, t), 1), p, 0.0)
            dv = dv + jnp.dot(p.astype(BF16), do_ref[rows(i), :].astype(BF16), preferred_element_type=F32)
            ds = p * (dp_refs[slot][...] - delta_ref[pl.ds(i, 1), :])
            dk = dk + jnp.dot(ds.astype(BF16), q_ref[rows(i), :], preferred_element_type=F32)
            return dk, dv

        issue(kj, 0)
        carry = consume(kj, 0, (jnp.zeros((t, MLA_QK), F32), jnp.zeros((t, MLA_V), F32)), masked=True)
        dk, dv = _pipelined_blocks(kj + 1, nq - 1 - kj, nq - 1, issue, consume, carry, False)
        dk_ref[...] = dk * LN2
        dv_ref[...] = dv

    tile = lambda w: pl.BlockSpec((None, t, w), lambda h, j: (h, j, 0))
    whole = lambda w: pl.BlockSpec((None, s, w), lambda h, j: (h, 0, 0))
    rows_spec = pl.BlockSpec((None, nq, t), lambda h, j: (h, 0, 0))
    return _pallas_call(
        body, name=name, grid=(nh, nq),
        in_specs=[whole(MLA_QK), tile(MLA_QK), tile(MLA_V), whole(MLA_V), rows_spec, rows_spec],
        out_specs=[tile(MLA_QK), tile(MLA_V)],
        out_shape=[jax.ShapeDtypeStruct((nh, s, MLA_QK), F32), jax.ShapeDtypeStruct((nh, s, MLA_V), F32)],
        scratch_shapes=[pltpu.VMEM((t, t), F32)] * 4,
        compiler_params=_params(dimension_semantics=("arbitrary", "arbitrary")),
    )(q, k, v, do, lse_row, delta_row)


def _swa_tile(s):
    return min(s, 4 * BLOCK)


def _swa_specs(tq):
    nb = tq // BLOCK
    grp = lambda w: pl.BlockSpec((SWA_GROUP, tq, w), lambda j, i: (j, i, 0))
    main = pl.BlockSpec((None, tq, SWA_HEAD_DIM), lambda j, i: (j, i, 0))
    tail = pl.BlockSpec((None, BLOCK, SWA_HEAD_DIM), lambda j, i: (j, nb * (i + 1), 0))
    sink = pl.BlockSpec((None, SWA_GROUP, 128), lambda j, i: (j, 0, 0))
    return grp, main, tail, sink


def _swa_band_mask(first):
    shape = (SWA_GROUP * BLOCK, 2 * BLOCK)
    q_rel = (lax.broadcasted_iota(jnp.int32, shape, 0) & (BLOCK - 1)) + BLOCK
    k_rel = lax.broadcasted_iota(jnp.int32, shape, 1)
    dist = q_rel - k_rel
    return (dist >= 0) & (dist < BLOCK) & ((k_rel >= BLOCK) | jnp.logical_not(first))


def _swa_sink_column(sink_ref):
    sk = sink_ref[...]
    return jnp.concatenate([jnp.broadcast_to(sk[g:g + 1, 0:1], (BLOCK, 1)) for g in range(SWA_GROUP)], axis=0)


def _swa_fwd(q, kpad, vpad, sinks, name):
    _, s, _ = q.shape
    tq = _swa_tile(s)
    grp, main, tail, sink = _swa_specs(tq)
    d = SWA_HEAD_DIM

    def body(q_ref, km_ref, kt_ref, vm_ref, vt_ref, sink_ref, o_ref, lse_ref):
        i = pl.program_id(1)
        kall = jnp.concatenate([km_ref[...], kt_ref[...]], axis=0)
        vall = jnp.concatenate([vm_ref[...], vt_ref[...]], axis=0)
        sink_col = _swa_sink_column(sink_ref)
        for b in range(tq // BLOCK):
            lo = b * BLOCK
            valid = _swa_band_mask(i == 0 if b == 0 else False)
            q4 = q_ref[:, lo:lo + BLOCK, :].reshape(SWA_GROUP * BLOCK, d)
            sc = jnp.where(valid, _dot_nt(q4, kall[lo:lo + 2 * BLOCK]) * SWA_SCALE, NEG)
            m = jnp.maximum(jnp.max(sc, axis=-1, keepdims=True), sink_col)
            e = jnp.exp(sc - m)
            den = jnp.sum(e, axis=-1, keepdims=True) + jnp.exp(sink_col - m)
            out = jnp.dot((e * (1.0 / den)).astype(BF16), vall[lo:lo + 2 * BLOCK], preferred_element_type=F32)
            o_ref[:, lo:lo + BLOCK, :] = out.reshape(SWA_GROUP, BLOCK, d)
            lse_ref[:, lo:lo + BLOCK, :] = (m + jnp.log(den)).reshape(SWA_GROUP, BLOCK, 1)

    return _pallas_call(
        body, name=name, grid=(SWA_KV_HEADS, s // tq),
        in_specs=[grp(d), main, tail, main, tail, sink], out_specs=[grp(d), grp(1)],
        out_shape=[jax.ShapeDtypeStruct((SWA_HEADS, s, d), F32), jax.ShapeDtypeStruct((SWA_HEADS, s, 1), F32)],
        compiler_params=_params(dimension_semantics=("arbitrary", "arbitrary")),
    )(q, kpad, kpad, vpad, vpad, sinks)


def _swa_bwd(q, kpad, vpad, sinks, o, lse, do, name):
    _, s, _ = q.shape
    tq = _swa_tile(s)
    grp, main, tail, sink = _swa_specs(tq)
    d = SWA_HEAD_DIM

    def body(q_ref, km_ref, kt_ref, vm_ref, vt_ref, sink_ref, o_ref, lse_ref, do_ref, dq_ref, dk_ref, dv_ref, dsink_ref):
        i = pl.program_id(1)
        kall = jnp.concatenate([km_ref[...], kt_ref[...]], axis=0)
        vall = jnp.concatenate([vm_ref[...], vt_ref[...]], axis=0)
        sink_col = _swa_sink_column(sink_ref)

        @pl.when(i == 0)
        def _():
            dk_ref[...] = jnp.zeros_like(dk_ref)
            dv_ref[...] = jnp.zeros_like(dv_ref)
            dsink_ref[...] = jnp.zeros_like(dsink_ref)

        dsink = jnp.zeros((SWA_GROUP * BLOCK, 1), F32)
        for b in range(tq // BLOCK):
            lo = b * BLOCK
            valid = _swa_band_mask(i == 0 if b == 0 else False)
            rows4 = SWA_GROUP * BLOCK
            q4 = q_ref[:, lo:lo + BLOCK, :].reshape(rows4, d)
            do4 = do_ref[:, lo:lo + BLOCK, :].reshape(rows4, d)
            lse4 = lse_ref[:, lo:lo + BLOCK, :].reshape(rows4, 1)
            delta = jnp.sum(do4 * o_ref[:, lo:lo + BLOCK, :].reshape(rows4, d), axis=-1, keepdims=True)
            kb, vb = kall[lo:lo + 2 * BLOCK], vall[lo:lo + 2 * BLOCK]
            do4b = do4.astype(BF16)
            p = jnp.where(valid, jnp.exp(_dot_nt(q4, kb) * SWA_SCALE - lse4), 0.0)
            ds = (p * (_dot_nt(do4b, vb) - delta) * SWA_SCALE).astype(BF16)
            dq_ref[:, lo:lo + BLOCK, :] = jnp.dot(ds, kb, preferred_element_type=F32).reshape(SWA_GROUP, BLOCK, d)
            band = pl.ds(pl.multiple_of(i * tq, BLOCK) + lo, 2 * BLOCK)
            dk_ref[band, :] += _dot_tn(ds, q4)
            dv_ref[band, :] += _dot_tn(p.astype(BF16), do4b)
            dsink = dsink - jnp.exp(sink_col - lse4) * delta
        per_head = [jnp.broadcast_to(jnp.sum(dsink[g * BLOCK:(g + 1) * BLOCK], axis=0, keepdims=True), (1, 128))
                    for g in range(SWA_GROUP)]
        dsink_ref[...] += jnp.concatenate(per_head + [jnp.zeros((8 - SWA_GROUP, 128), F32)], axis=0)

    acc = pl.BlockSpec((None, s + BLOCK, d), lambda j, i: (j, 0, 0))
    return _pallas_call(
        body, name=name, grid=(SWA_KV_HEADS, s // tq),
        in_specs=[grp(d), main, tail, main, tail, sink, grp(d), grp(1), grp(d)],
        out_specs=[grp(d), acc, acc, pl.BlockSpec((None, 8, 128), lambda j, i: (j, 0, 0))],
        out_shape=[jax.ShapeDtypeStruct((SWA_HEADS, s, d), F32),
                   jax.ShapeDtypeStruct((SWA_KV_HEADS, s + BLOCK, d), F32),
                   jax.ShapeDtypeStruct((SWA_KV_HEADS, s + BLOCK, d), F32),
                   jax.ShapeDtypeStruct((SWA_KV_HEADS, 8, 128), F32)],
        compiler_params=_params(dimension_semantics=("arbitrary", "arbitrary")),
    )(q, kpad, kpad, vpad, vpad, sinks, o, lse, do)


def _loss_head(y, target, name):
    s = y.shape[0]
    tm = _row_tile(s, 512)

    def body(y_ref, t_ref, dy_ref, loss_ref):
        i = pl.program_id(0)
        err = y_ref[...] - t_ref[...]
        dy_ref[...] = err * (1.0 / D_MODEL)
        part = jnp.broadcast_to(0.5 * jnp.sum(jnp.mean(err * err, axis=-1, keepdims=True), axis=0, keepdims=True), (1, 128))

        @pl.when(i == 0)
        def _():
            loss_ref[...] = part

        @pl.when(i > 0)
        def _():
            loss_ref[...] += part

    row = pl.BlockSpec((tm, D_MODEL), lambda i: (i, 0))
    return _pallas_call(
        body, name=name, grid=(s // tm,), in_specs=[row, row], out_specs=[row, _full((1, 128))],
        out_shape=[jax.ShapeDtypeStruct((s, D_MODEL), F32), jax.ShapeDtypeStruct((1, 128), F32)],
        compiler_params=_params(dimension_semantics=("arbitrary",)),
    )(y, target)


def _adamw(w, g, m, v, name):
    rows, cols = w.shape
    tr = rows
    for cand in (512, 256, 128, 64, 32, 16, 8):
        if rows % cand == 0 and rows > cand:
            tr = cand
            break

    def body(w_ref, g_ref, m_ref, v_ref, d_ref, nm_ref, nv_ref):
        gv = g_ref[...]
        nm = ADAM_B1 * m_ref[...] + (1.0 - ADAM_B1) * gv
        nv = ADAM_B2 * v_ref[...] + (1.0 - ADAM_B2) * (gv * gv)
        m_hat = nm / (1.0 - ADAM_B1 ** ADAM_STEP)
        v_hat = nv / (1.0 - ADAM_B2 ** ADAM_STEP)
        d_ref[...] = -ADAM_LR * (m_hat / (jnp.sqrt(v_hat) + ADAM_EPS) + ADAM_WD * w_ref[...])
        nm_ref[...] = nm
        nv_ref[...] = nv

    blk = pl.BlockSpec((tr, cols), lambda i: (i, 0))
    return _pallas_call(
        body, name=name, grid=(rows // tr,), in_specs=[blk] * 4, out_specs=[blk] * 3,
        out_shape=[jax.ShapeDtypeStruct((rows, cols), F32)] * 3,
        compiler_params=_params(dimension_semantics=("arbitrary",)),
    )(w, g, m, v)


def _position():
    return lax.axis_index("x"), lax.axis_index("y"), lax.axis_index("c")


def _remote(src, dst, send_sems, recv_sems, k, to):
    return pltpu.make_async_remote_copy(src_ref=src, dst_ref=dst, send_sem=send_sems.at[k], recv_sem=recv_sems.at[k],
                                        device_id=to, device_id_type=MESH)


_HBM = pl.BlockSpec(memory_space=pltpu.HBM)


def _all_gather_layers(mine, name):
    na = len(mine)

    def body(*refs):
        ins, outs, (send_sems, recv_sems) = refs[:na], refs[na:2 * na], refs[2 * na:]
        x, y, c = _position()
        me, sibling = (x, y, c), (x, y, 1 - c)
        xn, yn, dg = (1 - x, y), (x, 1 - y), (1 - x, 1 - y)

        def slot(a, chip, pc, half=None):
            ref = outs[a].at[4 * chip[0] + 2 * chip[1] + pc]
            if half is None:
                return ref
            rows = ref.shape[0] // 2
            return ref.at[pl.ds(half * rows, rows)]

        def cp(a, k, chip, pc, half, to, src=None):
            dst = slot(a, chip, pc, half)
            return _remote(dst if src is None else src, dst, send_sems, recv_sems, 8 * a + k, to)

        sent = []
        for a in range(na):
            sent += [cp(a, 0, (x, y), c, None, (*xn, c), src=ins[a].at[c]), cp(a, 1, (x, y), c, None, (*yn, c), src=ins[a].at[c])]
        for s_ in sent:
            s_.start()
        later = []
        for k, frm, fwd_k, fwd_half, fwd_to in ((0, xn, 2, 0, yn), (1, yn, 3, 1, xn)):
            for a in range(na):
                cp(a, k, frm, c, None, me).wait_recv()
                later += [cp(a, fwd_k, frm, c, fwd_half, (*fwd_to, c)), cp(a, 4 + k, frm, c, None, sibling)]
                later[-2].start()
                later[-1].start()
        for k, half in ((2, 0), (3, 1)):
            for a in range(na):
                cp(a, k, dg, c, half, me).wait_recv()
                later.append(cp(a, 4 + k, dg, c, half, sibling))
                later[-1].start()
        for a in range(na):
            for k, chip, half in ((4, xn, None), (5, yn, None), (6, dg, 0), (7, dg, 1)):
                cp(a, k, chip, 1 - c, half, me).wait_recv()
        for s_ in sent + later:
            s_.wait_send()

    return _pallas_call(
        body, name=name, in_specs=[_HBM] * na, out_specs=[_HBM] * na,
        out_shape=[jax.ShapeDtypeStruct((N_DEV,) + m.shape[1:], m.dtype) for m in mine],
        scratch_shapes=[pltpu.SemaphoreType.DMA((8 * na,)), pltpu.SemaphoreType.DMA((8 * na,))],
    )(*mine)


def _sibling_exchange(parts, name, other_layer):
    na = len(parts)

    def body(*refs):
        ins, outs, (send_sems, recv_sems) = refs[:na], refs[na:2 * na], refs[2 * na:]
        x, y, c = _position()
        copies = [_remote(ins[a].at[1 - c] if other_layer else ins[a], outs[a], send_sems, recv_sems, a, (x, y, 1 - c))
                  for a in range(na)]
        for cp in copies:
            cp.start()
        for cp in copies:
            cp.wait()

    return _pallas_call(
        body, name=name, in_specs=[_HBM] * na, out_specs=[_HBM] * na,
        out_shape=[jax.ShapeDtypeStruct(p.shape[1:] if other_layer else p.shape, p.dtype) for p in parts],
        scratch_shapes=[pltpu.SemaphoreType.DMA((na,)), pltpu.SemaphoreType.DMA((na,))],
    )(*parts)


def _scatter_to_chips(parts, name):
    na = len(parts)

    def body(*refs):
        ins, outs, (send_sems, recv_sems) = refs[:na], refs[na:2 * na], refs[2 * na:]
        x, y, c = _position()
        chips = [(1 - x, y), (x, 1 - y), (1 - x, 1 - y)]
        copies = [_remote(ins[a].at[2 * px + py], outs[a].at[j], send_sems, recv_sems, 3 * a + j, (px, py, c))
                  for a in range(na) for j, (px, py) in enumerate(chips)]
        for cp in copies:
            cp.start()
        for cp in copies:
            cp.wait()

    return _pallas_call(
        body, name=name, in_specs=[_HBM] * na, out_specs=[_HBM] * na,
        out_shape=[jax.ShapeDtypeStruct((3,) + p.shape[1:], p.dtype) for p in parts],
        scratch_shapes=[pltpu.SemaphoreType.DMA((3 * na,)), pltpu.SemaphoreType.DMA((3 * na,))],
    )(*parts)


def _assemble(g4, mine, name, side_by_side):
    _, nl, r, w = g4.shape
    tr = next(cand for cand in (256, 128) if r % cand == 0)

    def body(in_ref, mine_ref, out_ref):
        chip = 2 * lax.axis_index("x") + lax.axis_index("y")
        blocks = [jnp.where(chip == sh, mine_ref[...], in_ref[sh]) for sh in range(N_SHARD)]
        if side_by_side:
            out_ref[...] = jnp.concatenate(blocks, axis=-1)
        else:
            for sh in range(N_SHARD):
                out_ref[sh] = blocks[sh]

    if side_by_side:
        out_spec = pl.BlockSpec((None, tr, N_SHARD * w), lambda l, i: (l, i, 0))
        out_shape = jax.ShapeDtypeStruct((nl, r, N_SHARD * w), g4.dtype)
    else:
        out_spec = pl.BlockSpec((None, N_SHARD, tr, w), lambda l, i: (l, 0, i, 0))
        out_shape = jax.ShapeDtypeStruct((nl, N_SHARD, r, w), g4.dtype)
    return _pallas_call(
        body, name=name, grid=(nl, r // tr),
        in_specs=[pl.BlockSpec((N_SHARD, None, tr, w), lambda l, i: (0, l, i, 0)),
                  pl.BlockSpec((None, tr, w), lambda l, i: (l, i, 0))],
        out_specs=out_spec, out_shape=out_shape,
        compiler_params=_params(dimension_semantics=("arbitrary", "arbitrary")),
    )(g4, mine)


def _all_reduce_small(vec, name):
    r, l = vec.shape

    def body(v_ref, out_ref, gath_ref, send_sems, recv_sems):
        x, y, c = _position()
        me = 4 * x + 2 * y + c
        gath_ref[me] = v_ref[...]
        copies = []
        for k in range(1, N_DEV):
            to = (x ^ (k >> 2), y ^ ((k >> 1) & 1), c ^ (k & 1))
            copies.append(_remote(gath_ref.at[me], gath_ref.at[me], send_sems, recv_sems, k - 1, to))
        for cp in copies:
            cp.start()
        for k in range(1, N_DEV):
            frm = 4 * (x ^ (k >> 2)) + 2 * (y ^ ((k >> 1) & 1)) + (c ^ (k & 1))
            _remote(gath_ref.at[frm], gath_ref.at[frm], send_sems, recv_sems, k - 1, (x, y, c)).wait_recv()
        for cp in copies:
            cp.wait_send()
        total = gath_ref[0]
        for d in range(1, N_DEV):
            total = total + gath_ref[d]
        out_ref[...] = total

    vm = pl.BlockSpec(memory_space=pltpu.VMEM)
    return _pallas_call(
        body, name=name, in_specs=[vm], out_specs=vm, out_shape=jax.ShapeDtypeStruct((r, l), F32),
        scratch_shapes=[pltpu.VMEM((N_DEV, r, l), F32), pltpu.SemaphoreType.DMA((N_DEV - 1,)),
                        pltpu.SemaphoreType.DMA((N_DEV - 1,))],
    )(vec)


def _sum_blocks(blocks, out_dtype, name):
    m, w = blocks[0].shape
    tr = next(cand for cand in (512, 256, 128) if m % cand == 0)

    def body(*refs):
        total = refs[0][...].astype(F32)
        for ref in refs[1:-1]:
            total = total + ref[...].astype(F32)
        refs[-1][...] = total.astype(out_dtype)

    blk = pl.BlockSpec((tr, w), lambda i: (i, 0))
    return _pallas_call(
        body, name=name, grid=(m // tr,), in_specs=[blk] * len(blocks), out_specs=blk,
        out_shape=jax.ShapeDtypeStruct((m, w), out_dtype),
        compiler_params=_params(dimension_semantics=("arbitrary",)),
    )(*blocks)


GROUPS = (('ffn1_w_gate', 'ffn1_w_up', 'ffn2_w_gate', 'ffn2_w_up'), ('ffn1_w_down', 'ffn2_w_down', 'w_o'),
          ('w_in',), ('mla_w_q_b',), ('mla_w_kv_b',))


def _shard_rows(name):
    shape, axis = BIG[name]
    return shape[0] // N_SHARD if axis == 0 else shape[0]


def _group_row_offsets(group):
    return [int(v) for v in np.cumsum([0] + [_shard_rows(n) for n in group])]


def _rope_tables(s):
    pos = jnp.arange(s, dtype=F32)
    inv = 1.0 / (ROPE_THETA ** (jnp.arange(0, MLA_ROPE, 2, dtype=F32) / MLA_ROPE))
    ang = pos[:, None] * inv[None, :]
    return jnp.cos(ang), jnp.sin(ang)


def _local_step(x, target, small, big):
    s = x.shape[0]
    cos, sin = _rope_tables(s)
    row = lambda name, l: small[name][l][None, :]
    saved = []
    for l in range(DEPTH):
        sv = {'x0': x}
        x, sv['g1'], sv['u1'] = _ffn_fwd(x, row('ffn1_norm', l), big['ffn1_w_gate'][l], big['ffn1_w_up'][l],
                                        big['ffn1_w_down'][l], f"ffn1_fwd_{l}")
        sv['x1'] = x
        gains = [row(n, l) for n in ('mix_norm', 'mla_q_a_norm', 'mla_kv_a_norm', 'mla_q_norm', 'mla_k_norm',
                                     'swa_q_norm', 'swa_k_norm')]
        q_a, k_a, v_a, q_b, k_b, v_b = _pre_fwd(x, gains, big['w_in'][l], big['mla_w_q_b'][l], big['mla_w_kv_b'][l],
                                                cos, sin, f"pre_fwd_{l}")
        o_a, lse = _mla_fwd(q_a, k_a, v_a, f"mla_fwd_{l}")
        kpad = jnp.pad(k_b, ((0, 0), (BLOCK, 0), (0, 0)))
        vpad = jnp.pad(v_b, ((0, 0), (BLOCK, 0), (0, 0)))
        sinks = jnp.broadcast_to(small['swa_sinks'][l].reshape(SWA_KV_HEADS, SWA_GROUP, 1), (SWA_KV_HEADS, SWA_GROUP, 128))
        o_b, lse_b = _swa_fwd(q_b, kpad, vpad, sinks, f"swa_fwd_{l}")
        sv.update(gains=gains, q_a=q_a, k_a=k_a, v_a=v_a, q_b=q_b, kpad=kpad, vpad=vpad, sinks=sinks, o_a=o_a, lse=lse, o_b=o_b,
                  lse_b=lse_b)
        x = _post_fwd(x, o_a, o_b, row('mla_out_norm', l), row('swa_out_norm', l), big['w_o'][l], f"post_fwd_{l}")
        sv['x2'] = x
        x, sv['g2'], sv['u2'] = _ffn_fwd(x, row('ffn2_norm', l), big['ffn2_w_gate'][l], big['ffn2_w_up'][l],
                                        big['ffn2_w_down'][l], f"ffn2_fwd_{l}")
        saved.append(sv)

    dx, loss = _loss_head(x, target, "loss_head")

    gs = {n: [None] * DEPTH for n in SMALL_NAMES}
    gb = {n: [None] * DEPTH for n in BIG_NAMES}
    t = _attn_tile(s)
    for l in reversed(range(DEPTH)):
        sv = saved[l]

        def ffn_back(tag, xin, dy, gate, up):
            dxi, dgain, nb, act, dgate, dup = _ffn_bwd(xin, dy, gate, up, row(tag + '_norm', l), big[tag + '_w_gate'][l],
                                                      big[tag + '_w_up'][l], big[tag + '_w_down'][l], f"{tag}_bwd_{l}")
            gs[tag + '_norm'][l] = dgain[0]
            gb[tag + '_w_gate'][l] = _matmul_tn(nb, dgate, 1.0, f"{tag}_dwg_{l}", True)
            gb[tag + '_w_up'][l] = _matmul_tn(nb, dup, 1.0, f"{tag}_dwu_{l}", True)
            gb[tag + '_w_down'][l] = _matmul_tn(act, dy, 0.5, f"{tag}_dwd_{l}", False).reshape(N_SHARD, D_FF // N_SHARD, D_MODEL)
            return dxi

        dx = ffn_back('ffn2', sv['x2'], dx, sv['g2'], sv['u2'])
        do_a, do_b, dga, dgb, dwo = _post_bwd(dx, sv['o_a'], sv['o_b'], row('mla_out_norm', l), row('swa_out_norm', l),
                                              big['w_o'][l], f"post_bwd_{l}")
        gs['mla_out_norm'][l], gs['swa_out_norm'][l] = dga[0], dgb[0]
        gb['w_o'][l] = dwo.reshape(N_SHARD, MIX_WIDTH // N_SHARD, D_MODEL)
        dq_a, delta = _mla_bwd_dq(sv['q_a'], sv['k_a'], sv['v_a'], sv['o_a'], do_a, sv['lse'], f"mla_dq_{l}")
        dk_a, dv_a = _mla_bwd_dkv(sv['q_a'], sv['k_a'], sv['v_a'], do_a, sv['lse'].reshape(MLA_HEADS, s // t, t),
                                  delta.reshape(MLA_HEADS, s // t, t), f"mla_dkv_{l}")
        dq_b, dkpad, dvpad, dsink = _swa_bwd(sv['q_b'], sv['kpad'], sv['vpad'], sv['sinks'], sv['o_b'], sv['lse_b'], do_b,
                                             f"swa_bwd_{l}")
        gs['swa_sinks'][l] = dsink[:, :SWA_GROUP, 0].reshape(SWA_HEADS)
        cts = [dq_a, dk_a, dv_a, dq_b, dkpad[:, BLOCK:], dvpad[:, BLOCK:]]
        outs = _pre_bwd(sv['x1'], dx, cts, sv['gains'], big['w_in'][l], big['mla_w_q_b'][l], big['mla_w_kv_b'][l],
                        cos, sin, f"pre_bwd_{l}")
        dx = outs[0]
        for n, val in zip(('mix_norm', 'mla_q_a_norm', 'mla_kv_a_norm', 'mla_q_norm', 'mla_k_norm', 'swa_q_norm', 'swa_k_norm'),
                          outs[1:8]):
            gs[n][l] = val[0]
        gb['w_in'][l], gb['mla_w_q_b'][l], gb['mla_w_kv_b'][l] = outs[8:11]
        dx = ffn_back('ffn1', sv['x0'], dx, sv['g1'], sv['u1'])
    return loss, dx, gs, gb


def kernel(x, ffn1_norm, ffn1_w_gate, ffn1_w_up, ffn1_w_down, mix_norm, w_in, mla_q_a_norm, mla_w_q_b, mla_kv_a_norm, mla_w_kv_b, mla_q_norm, mla_k_norm, swa_q_norm, swa_k_norm, swa_sinks, mla_out_norm, swa_out_norm, w_o, ffn2_norm, ffn2_w_gate, ffn2_w_up, ffn2_w_down, loss_target, m_ffn1_norm, m_ffn1_w_gate, m_ffn1_w_up, m_ffn1_w_down, m_mix_norm, m_w_in, m_mla_q_a_norm, m_mla_w_q_b, m_mla_kv_a_norm, m_mla_w_kv_b, m_mla_q_norm, m_mla_k_norm, m_swa_q_norm, m_swa_k_norm, m_swa_sinks, m_mla_out_norm, m_swa_out_norm, m_w_o, m_ffn2_norm, m_ffn2_w_gate, m_ffn2_w_up, m_ffn2_w_down, v_ffn1_norm, v_ffn1_w_gate, v_ffn1_w_up, v_ffn1_w_down, v_mix_norm, v_w_in, v_mla_q_a_norm, v_mla_w_q_b, v_mla_kv_a_norm, v_mla_w_kv_b, v_mla_q_norm, v_mla_k_norm, v_swa_q_norm, v_swa_k_norm, v_swa_sinks, v_mla_out_norm, v_swa_out_norm, v_w_o, v_ffn2_norm, v_ffn2_w_gate, v_ffn2_w_up, v_ffn2_w_down):
    args = dict(locals())
    weights = {n: args[n] for n in WEIGHT_NAMES}
    mom_m = {n: args["m_" + n] for n in WEIGHT_NAMES}
    mom_v = {n: args["v_" + n] for n in WEIGHT_NAMES}

    mine = [jnp.concatenate([weights[n].astype(BF16) for n in group], axis=1) for group in GROUPS]
    c = lax.axis_index("c")
    chip = 2 * lax.axis_index("x") + lax.axis_index("y")
    gathered = [g.reshape((N_SHARD, DEPTH) + g.shape[1:]) for g in _all_gather_layers(mine, "gather_weights")]
    big = {}
    for gi, group in enumerate(GROUPS):
        offs = _group_row_offsets(group)
        col_sharded = BIG[group[0]][1] == 1
        full = _assemble(gathered[gi], mine[gi], f"assemble_{gi}", col_sharded)
        for i, n in enumerate(group):
            if col_sharded:
                big[n] = [full[l, offs[i]:offs[i + 1]] for l in range(DEPTH)]
            else:
                big[n] = [full[l, :, offs[i]:offs[i + 1]].reshape(BIG[n][0]) for l in range(DEPTH)]
    small = {n: weights[n] for n in SMALL_NAMES}

    loss, dx, gs, gb = _local_step(x[0], loss_target[0], small, big)

    small_flat = jnp.concatenate([jnp.stack(gs[n]).reshape(-1) for n in SMALL_NAMES] + [loss[0, :1]])
    n_small = small_flat.shape[0]
    lanes = -(-n_small // (8 * 128)) * 128
    small_sum = _all_reduce_small(jnp.pad(small_flat, (0, 8 * lanes - n_small)).reshape(8, lanes), "reduce_small").reshape(-1)
    grads = {}
    off = 0
    for n in SMALL_NAMES:
        cnt = int(np.prod(weights[n].shape))
        grads[n] = small_sum[off:off + cnt].reshape(weights[n].shape)
        off += cnt
    loss_out = small_sum[off]

    parts = [jnp.stack([jnp.concatenate([gb[n][l] for n in group], axis=1) for l in range(DEPTH)]) for group in GROUPS]
    from_sibling = _sibling_exchange(parts, "swap_layers", True)
    chip_sums = []
    for gi, (p, got) in enumerate(zip(parts, from_sibling)):
        kept = lax.dynamic_index_in_dim(p, c, axis=0, keepdims=False)
        rows = N_SHARD * p.shape[2]
        pair = _sum_blocks([kept.reshape(rows, -1), got.reshape(rows, -1)], BF16, f"sum_pair_{gi}")
        chip_sums.append(pair.reshape(p.shape[1:]))
    from_chips = _scatter_to_chips(chip_sums, "scatter_chips")
    halves = []
    for gi, (cs, got) in enumerate(zip(chip_sums, from_chips)):
        own = lax.dynamic_index_in_dim(cs, chip, axis=0, keepdims=False)
        halves.append(_sum_blocks([own, got[0], got[1], got[2]], F32, f"sum_chips_{gi}"))
    others = _sibling_exchange(halves, "share_layers", False)
    for gi, group in enumerate(GROUPS):
        offs = _group_row_offsets(group)
        both = jnp.stack([jnp.where(c == 0, halves[gi], others[gi]), jnp.where(c == 0, others[gi], halves[gi])])
        for i, n in enumerate(group):
            grads[n] = both[:, offs[i]:offs[i + 1]]

    deltas, new_m, new_v = {}, {}, {}
    for n in WEIGHT_NAMES:
        shp = weights[n].shape
        two_d = (DEPTH, shp[-1]) if len(shp) == 2 else (shp[0] * shp[1], shp[2])
        d, nm, nv = _adamw(weights[n].reshape(two_d), grads[n].reshape(two_d), mom_m[n].reshape(two_d),
                           mom_v[n].reshape(two_d), f"adamw_{n}")
        deltas[n], new_m[n], new_v[n] = d.reshape(shp), nm.reshape(shp), nv.reshape(shp)

    return (loss_out, dx[None], *[grads[n] for n in WEIGHT_NAMES], *[deltas[n] for n in WEIGHT_NAMES],
            *[new_m[n] for n in WEIGHT_NAMES], *[new_v[n] for n in WEIGHT_NAMES])
```

```python
import functools

import numpy as np
import jax
import jax.numpy as jnp
from jax import lax
from jax.experimental import pallas as pl
from jax.experimental.pallas import tpu as pltpu

F32 = jnp.float32
BF16 = jnp.bfloat16

D_MODEL = 1024
DEPTH = 2
EPS = 1e-6
ROPE_THETA = 10000.0
BLOCK = 128
MLA_HEADS = 4
MLA_Q_RANK = 256
MLA_KV_RANK = 128
MLA_NOPE = 128
MLA_ROPE = 64
MLA_V = 128
MLA_QK = MLA_NOPE + MLA_ROPE
MLA_WIDTH = MLA_HEADS * MLA_V
SWA_HEADS = 8
SWA_KV_HEADS = 2
SWA_GROUP = SWA_HEADS // SWA_KV_HEADS
SWA_HEAD_DIM = 64
SWA_WIDTH = SWA_HEADS * SWA_HEAD_DIM
MIX_WIDTH = MLA_WIDTH + SWA_WIDTH
IN_SPLITS = (MLA_Q_RANK, MLA_KV_RANK, MLA_ROPE, SWA_WIDTH, SWA_KV_HEADS * SWA_HEAD_DIM, SWA_KV_HEADS * SWA_HEAD_DIM)
IN_COLS = sum(IN_SPLITS)
IN_OFFS = tuple(int(v) for v in np.cumsum((0,) + IN_SPLITS))
D_FF = 2816
MLA_SCALE = MLA_QK ** -0.5
LOG2E = 1.4426950408889634
LN2 = 0.6931471805599453
MLA_QSCALE = MLA_SCALE * LOG2E
SWA_SCALE = SWA_HEAD_DIM ** -0.5
NEG = -1e30

ADAM_LR = 0.001
ADAM_B1 = 0.9
ADAM_B2 = 0.999
ADAM_EPS = 1e-08
ADAM_WD = 0.01
ADAM_STEP = 10

N_SHARD = 4
N_DEV = 8
VMEM_LIMIT = 56 * 1024 * 1024
MESH = pl.DeviceIdType.MESH

WEIGHT_NAMES = ['ffn1_norm', 'ffn1_w_gate', 'ffn1_w_up', 'ffn1_w_down', 'mix_norm', 'w_in', 'mla_q_a_norm', 'mla_w_q_b',
                'mla_kv_a_norm', 'mla_w_kv_b', 'mla_q_norm', 'mla_k_norm', 'swa_q_norm', 'swa_k_norm', 'swa_sinks',
                'mla_out_norm', 'swa_out_norm', 'w_o', 'ffn2_norm', 'ffn2_w_gate', 'ffn2_w_up', 'ffn2_w_down']
BIG = {'ffn1_w_gate': ((D_MODEL, D_FF), 1), 'ffn1_w_up': ((D_MODEL, D_FF), 1), 'ffn1_w_down': ((D_FF, D_MODEL), 0),
       'w_in': ((D_MODEL, IN_COLS), 1), 'mla_w_q_b': ((MLA_Q_RANK, MLA_HEADS * MLA_QK), 1),
       'mla_w_kv_b': ((MLA_KV_RANK, MLA_HEADS * (MLA_NOPE + MLA_V)), 1), 'w_o': ((MIX_WIDTH, D_MODEL), 0),
       'ffn2_w_gate': ((D_MODEL, D_FF), 1), 'ffn2_w_up': ((D_MODEL, D_FF), 1), 'ffn2_w_down': ((D_FF, D_MODEL), 0)}
BIG_NAMES = [n for n in WEIGHT_NAMES if n in BIG]
SMALL_NAMES = [n for n in WEIGHT_NAMES if n not in BIG]

_pallas_call = pl.pallas_call


def _params(**kw):
    return pltpu.CompilerParams(vmem_limit_bytes=VMEM_LIMIT, **kw)


def _full(shape):
    n = len(shape)
    return pl.BlockSpec(shape, lambda *_: (0,) * n)


def _resident(shape):
    n = len(shape)
    return pl.BlockSpec(shape, lambda *_: (0,) * n, pipeline_mode=pl.Buffered(1))


@jax.custom_vjp
def _mm(a, w):
    return jnp.dot(a.astype(BF16), w, preferred_element_type=F32)


def _mm_fwd(a, w):
    return _mm(a, w), w


def _mm_bwd(w, dy):
    return lax.dot_general(dy.astype(BF16), w, (((1,), (1,)), ((), ())), preferred_element_type=F32), None


_mm.defvjp(_mm_fwd, _mm_bwd)


def _dot_nt(a, b):
    return lax.dot_general(a, b, (((1,), (1,)), ((), ())), preferred_element_type=F32)


def _dot_tn(a, b):
    return lax.dot_general(a, b, (((0,), (0,)), ((), ())), preferred_element_type=F32)


def _rms(t, g):
    return t * lax.rsqrt(jnp.mean(t * t, axis=-1, keepdims=True) + EPS) * g


def _rope(t, cos, sin):
    half = t.shape[-1] // 2
    t1, t2 = t[:, :half], t[:, half:]
    return jnp.concatenate([t1 * cos - t2 * sin, t2 * cos + t1 * sin], axis=-1)


def _sigmoid(z):
    return 1.0 / (1.0 + jnp.exp(-z))


def _row_tile(s, want):
    return min(want, s)


FF_CHUNK = 1408


def _ffn_fwd(x, g, wg, wu, wd, name):
    s = x.shape[0]
    tm = _row_tile(s, 256)

    def body(x_ref, g_ref, wg_ref, wu_ref, wd_ref, y_ref, gate_ref, up_ref):
        xv = x_ref[...]
        nb = _rms(xv, g_ref[...]).astype(BF16)
        acc = xv
        for c in range(0, D_FF, FF_CHUNK):
            gate = jnp.dot(nb, wg_ref[:, c:c + FF_CHUNK], preferred_element_type=F32)
            up = jnp.dot(nb, wu_ref[:, c:c + FF_CHUNK], preferred_element_type=F32)
            gate_ref[:, c:c + FF_CHUNK] = gate.astype(BF16)
            up_ref[:, c:c + FF_CHUNK] = up.astype(BF16)
            act = (gate * _sigmoid(gate) * up).astype(BF16)
            acc = acc + 0.5 * jnp.dot(act, wd_ref[c:c + FF_CHUNK, :], preferred_element_type=F32)
        y_ref[...] = acc

    return _pallas_call(
        body, name=name, grid=(s // tm,),
        in_specs=[pl.BlockSpec((tm, D_MODEL), lambda i: (i, 0)), _full((1, D_MODEL)),
                  _resident((D_MODEL, D_FF)), _resident((D_MODEL, D_FF)), _resident((D_FF, D_MODEL))],
        out_specs=[pl.BlockSpec((tm, D_MODEL), lambda i: (i, 0)), pl.BlockSpec((tm, D_FF), lambda i: (i, 0)),
                   pl.BlockSpec((tm, D_FF), lambda i: (i, 0))],
        out_shape=[jax.ShapeDtypeStruct((s, D_MODEL), F32), jax.ShapeDtypeStruct((s, D_FF), BF16),
                   jax.ShapeDtypeStruct((s, D_FF), BF16)],
        compiler_params=_params(dimension_semantics=("arbitrary",)),
    )(x, g, wg, wu, wd)


def _ffn_bwd(x, dy, gate, up, g, wg, wu, wd, name):
    s = x.shape[0]
    tm = _row_tile(s, 256)

    def body(x_ref, dy_ref, gate_ref, up_ref, g_ref, wg_ref, wu_ref, wd_ref,
             dx_ref, dgain_ref, n_ref, act_ref, dgate_ref, dup_ref):
        i = pl.program_id(0)
        xv = x_ref[...]
        dyv = dy_ref[...]
        gv = g_ref[...]
        r = lax.rsqrt(jnp.mean(xv * xv, axis=-1, keepdims=True) + EPS)
        xh = xv * r
        n_ref[...] = (xh * gv).astype(BF16)
        dyh = (0.5 * dyv).astype(BF16)
        dn = jnp.zeros_like(xv)
        for c in range(0, D_FF, FF_CHUNK):
            dact = _dot_nt(dyh, wd_ref[c:c + FF_CHUNK, :])
            gt = gate_ref[:, c:c + FF_CHUNK].astype(F32)
            u = up_ref[:, c:c + FF_CHUNK].astype(F32)
            sg = _sigmoid(gt)
            sl = gt * sg
            act_ref[:, c:c + FF_CHUNK] = (sl * u).astype(BF16)
            dup = (dact * sl).astype(BF16)
            dgate = (dact * u * (sg * (1.0 + gt * (1.0 - sg)))).astype(BF16)
            dup_ref[:, c:c + FF_CHUNK] = dup
            dgate_ref[:, c:c + FF_CHUNK] = dgate
            dn = dn + _dot_nt(dgate, wg_ref[:, c:c + FF_CHUNK]) + _dot_nt(dup, wu_ref[:, c:c + FF_CHUNK])
        part = jnp.sum(dn * xh, axis=0, keepdims=True)

        @pl.when(i == 0)
        def _():
            dgain_ref[...] = part

        @pl.when(i > 0)
        def _():
            dgain_ref[...] += part

        dxh = dn * gv
        dx_ref[...] = dyv + r * (dxh - xh * jnp.mean(dxh * xh, axis=-1, keepdims=True))

    row = lambda w: pl.BlockSpec((tm, w), lambda i: (i, 0))
    return _pallas_call(
        body, name=name, grid=(s // tm,),
        in_specs=[row(D_MODEL), row(D_MODEL), row(D_FF), row(D_FF), _full((1, D_MODEL)),
                  _resident((D_MODEL, D_FF)), _resident((D_MODEL, D_FF)), _resident((D_FF, D_MODEL))],
        out_specs=[row(D_MODEL), _full((1, D_MODEL)), row(D_MODEL), row(D_FF), row(D_FF), row(D_FF)],
        out_shape=[jax.ShapeDtypeStruct((s, D_MODEL), F32), jax.ShapeDtypeStruct((1, D_MODEL), F32),
                   jax.ShapeDtypeStruct((s, D_MODEL), BF16), jax.ShapeDtypeStruct((s, D_FF), BF16),
                   jax.ShapeDtypeStruct((s, D_FF), BF16), jax.ShapeDtypeStruct((s, D_FF), BF16)],
        compiler_params=_params(dimension_semantics=("arbitrary",)),
    )(x, dy, gate, up, g, wg, wu, wd)


def _store_col_shards(o_ref, acc, first_shard, n_here, width):
    for q in range(n_here):
        o_ref[q] = acc[:, (first_shard + q) * width:(first_shard + q + 1) * width].astype(BF16)


def _matmul_tn(a, b, scale, name, col_shards):
    t, m = a.shape
    n = b.shape[1]
    tk = _row_tile(t, 2048)
    tn = n // 2
    per = n // N_SHARD
    nk = t // tk

    def body(a_ref, b_ref, o_ref, acc_ref):
        k = pl.program_id(1)
        bv = b_ref[...]
        if scale != 1.0:
            bv = bv.astype(F32) * scale
        part = _dot_tn(a_ref[...].astype(BF16), bv.astype(BF16))

        @pl.when(k == 0)
        def _():
            acc_ref[...] = part

        @pl.when(k > 0)
        def _():
            acc_ref[...] += part

        @pl.when(k == nk - 1)
        def _():
            if col_shards:
                _store_col_shards(o_ref, acc_ref[...], 0, tn // per, per)
            else:
                o_ref[...] = acc_ref[...].astype(BF16)

    if col_shards:
        out_spec = pl.BlockSpec((tn // per, m, per), lambda j, k: (j, 0, 0))
        out_shape = jax.ShapeDtypeStruct((N_SHARD, m, per), BF16)
    else:
        out_spec = pl.BlockSpec((m, tn), lambda j, k: (0, j))
        out_shape = jax.ShapeDtypeStruct((m, n), BF16)
    return _pallas_call(
        body, name=name, grid=(n // tn, nk),
        in_specs=[pl.BlockSpec((tk, m), lambda j, k: (k, 0)), pl.BlockSpec((tk, tn), lambda j, k: (k, j))],
        out_specs=out_spec, out_shape=out_shape, scratch_shapes=[pltpu.VMEM((m, tn), F32)],
        compiler_params=_params(dimension_semantics=("arbitrary", "arbitrary")),
    )(a, b)


def _pre_math(x, gm, gqa, gkva, gq, gk, gsq, gsk, taps, win, wqb, wkvb, cos, sin):
    h = _rms(x, gm)
    proj = _mm(h, win)
    if taps is not None:
        proj = proj + taps[0]
    o = IN_OFFS
    c_q, c_kv, k_pe = proj[:, o[0]:o[1]], proj[:, o[1]:o[2]], proj[:, o[2]:o[3]]
    q_s, k_s, v_s = proj[:, o[3]:o[4]], proj[:, o[4]:o[5]], proj[:, o[5]:o[6]]
    cqn = _rms(c_q, gqa)
    qa_all = _mm(cqn, wqb)
    ckvn = _rms(c_kv, gkva)
    kv_all = _mm(ckvn, wkvb)
    if taps is not None:
        qa_all = qa_all + taps[1]
        kv_all = kv_all + taps[2]
    q_a, k_a, v_a = [], [], []
    kvw = MLA_NOPE + MLA_V
    for hd in range(MLA_HEADS):
        qn = _rms(qa_all[:, hd * MLA_QK:(hd + 1) * MLA_QK], gq)
        k_nope = kv_all[:, hd * kvw:hd * kvw + MLA_NOPE]
        v_a.append(kv_all[:, hd * kvw + MLA_NOPE:(hd + 1) * kvw])
        kn = _rms(jnp.concatenate([k_nope, k_pe], axis=-1), gk)
        q_a.append(jnp.concatenate([qn[:, :MLA_NOPE], _rope(qn[:, MLA_NOPE:], cos, sin)], axis=-1))
        k_a.append(jnp.concatenate([kn[:, :MLA_NOPE], _rope(kn[:, MLA_NOPE:], cos, sin)], axis=-1))
    d = SWA_HEAD_DIM
    q_b = [_rope(_rms(q_s[:, hd * d:(hd + 1) * d], gsq), cos, sin) for hd in range(SWA_HEADS)]
    k_b = [_rope(_rms(k_s[:, j * d:(j + 1) * d], gsk), cos, sin) for j in range(SWA_KV_HEADS)]
    v_b = [v_s[:, j * d:(j + 1) * d] for j in range(SWA_KV_HEADS)]
    return (q_a, k_a, v_a, q_b, k_b, v_b), (h, cqn, ckvn)


_PRE_GAIN_WIDTHS = (D_MODEL, MLA_Q_RANK, MLA_KV_RANK, MLA_QK, MLA_QK, SWA_HEAD_DIM, SWA_HEAD_DIM)
_PRE_HEADS = ((MLA_HEADS, MLA_QK), (MLA_HEADS, MLA_QK), (MLA_HEADS, MLA_V),
              (SWA_HEADS, SWA_HEAD_DIM), (SWA_KV_HEADS, SWA_HEAD_DIM), (SWA_KV_HEADS, SWA_HEAD_DIM))


def _pre_fwd(x, gains, win, wqb, wkvb, cos, sin, name):
    s = x.shape[0]
    tm = _row_tile(s, 512)

    def body(x_ref, *refs):
        g_refs, (win_ref, wqb_ref, wkvb_ref, cos_ref, sin_ref), out_refs = refs[:7], refs[7:12], refs[12:]
        outs, _ = _pre_math(x_ref[...], *[g[...] for g in g_refs], None, win_ref[...], wqb_ref[...], wkvb_ref[...],
                            cos_ref[...], sin_ref[...])
        for idx, (ref, heads) in enumerate(zip(out_refs, outs)):
            for hd, val in enumerate(heads):
                ref[hd] = (val * MLA_QSCALE if idx == 0 else val).astype(BF16)

    heads_spec = lambda nh, w: pl.BlockSpec((nh, tm, w), lambda i: (0, i, 0))
    return _pallas_call(
        body, name=name, grid=(s // tm,),
        in_specs=[pl.BlockSpec((tm, D_MODEL), lambda i: (i, 0))] + [_full((1, w)) for w in _PRE_GAIN_WIDTHS]
        + [_resident(win.shape), _resident(wqb.shape), _resident(wkvb.shape),
           pl.BlockSpec((tm, MLA_ROPE // 2), lambda i: (i, 0)), pl.BlockSpec((tm, MLA_ROPE // 2), lambda i: (i, 0))],
        out_specs=[heads_spec(nh, w) for nh, w in _PRE_HEADS],
        out_shape=[jax.ShapeDtypeStruct((nh, s, w), BF16) for nh, w in _PRE_HEADS],
        compiler_params=_params(dimension_semantics=("arbitrary",)),
    )(x, *gains, win, wqb, wkvb, cos, sin)


def _pre_bwd(x, dx_res, cts, gains, win, wqb, wkvb, cos, sin, name):
    s = x.shape[0]
    tm = _row_tile(s, 256)
    tap_widths = (IN_COLS, MLA_HEADS * MLA_QK, MLA_HEADS * (MLA_NOPE + MLA_V))

    def body(x_ref, dxr_ref, *refs):
        ct_refs, g_refs = refs[:6], refs[6:13]
        win_ref, wqb_ref, wkvb_ref, cos_ref, sin_ref = refs[13:18]
        dx_ref, dg_refs, dw_refs, acc_refs = refs[18], refs[19:26], refs[26:29], refs[29:32]
        i = pl.program_id(0)
        win_v, wqb_v, wkvb_v, cos_v, sin_v = win_ref[...], wqb_ref[...], wkvb_ref[...], cos_ref[...], sin_ref[...]

        def f(xv, gm, gqa, gkva, gq, gk, gsq, gsk, t0, t1, t2):
            return _pre_math(xv, gm, gqa, gkva, gq, gk, gsq, gsk, (t0, t1, t2), win_v, wqb_v, wkvb_v, cos_v, sin_v)

        taps = [jnp.zeros((tm, w), F32) for w in tap_widths]
        _, vjp, acts = jax.vjp(f, x_ref[...], *[g[...] for g in g_refs], *taps, has_aux=True)
        ct = tuple([ref[hd] for hd in range(nh)] for ref, (nh, _) in zip(ct_refs, _PRE_HEADS))
        grads = vjp(ct)
        dx_ref[...] = grads[0] + dxr_ref[...]
        dws = [_dot_tn(a.astype(BF16), t.astype(BF16)) for a, t in zip(acts, grads[8:11])]

        @pl.when(i == 0)
        def _():
            for ref, val in zip(dg_refs, grads[1:8]):
                ref[...] = val
            for ref, val in zip(acc_refs, dws):
                ref[...] = val

        @pl.when(i > 0)
        def _():
            for ref, val in zip(dg_refs, grads[1:8]):
                ref[...] += val
            for ref, val in zip(acc_refs, dws):
                ref[...] += val

        @pl.when(i == s // tm - 1)
        def _():
            for ref, acc in zip(dw_refs, acc_refs):
                _store_col_shards(ref, acc[...], 0, N_SHARD, acc.shape[1] // N_SHARD)

    heads_spec = lambda nh, w: pl.BlockSpec((nh, tm, w), lambda i: (0, i, 0))
    row = pl.BlockSpec((tm, D_MODEL), lambda i: (i, 0))
    half = pl.BlockSpec((tm, MLA_ROPE // 2), lambda i: (i, 0))
    shard_shapes = [(N_SHARD, w.shape[0], w.shape[1] // N_SHARD) for w in (win, wqb, wkvb)]
    return _pallas_call(
        body, name=name, grid=(s // tm,),
        in_specs=[row, row] + [heads_spec(nh, w) for nh, w in _PRE_HEADS] + [_full((1, w)) for w in _PRE_GAIN_WIDTHS]
        + [_resident(win.shape), _resident(wqb.shape), _resident(wkvb.shape), half, half],
        out_specs=[row] + [_full((1, w)) for w in _PRE_GAIN_WIDTHS] + [_full(shp) for shp in shard_shapes],
        out_shape=[jax.ShapeDtypeStruct((s, D_MODEL), F32)] + [jax.ShapeDtypeStruct((1, w), F32) for w in _PRE_GAIN_WIDTHS]
        + [jax.ShapeDtypeStruct(shp, BF16) for shp in shard_shapes],
        scratch_shapes=[pltpu.VMEM(w.shape, F32) for w in (win, wqb, wkvb)],
        compiler_params=_params(dimension_semantics=("arbitrary",)),
    )(x, dx_res, *cts, *gains, win, wqb, wkvb, cos, sin)


def _post_math(oa, ob, ga, gb, wo):
    mixed = jnp.concatenate([_rms(jnp.concatenate(oa, axis=-1), ga), _rms(jnp.concatenate(ob, axis=-1), gb)], axis=-1)
    return _mm(mixed, wo), mixed


def _post_fwd(x, oa, ob, ga, gb, wo, name):
    s = x.shape[0]
    tm = _row_tile(s, 512)

    def body(x_ref, oa_ref, ob_ref, ga_ref, gb_ref, wo_ref, y_ref):
        y, _ = _post_math([oa_ref[hd] for hd in range(MLA_HEADS)], [ob_ref[hd] for hd in range(SWA_HEADS)],
                          ga_ref[...], gb_ref[...], wo_ref[...])
        y_ref[...] = x_ref[...] + y

    row = pl.BlockSpec((tm, D_MODEL), lambda i: (i, 0))
    return _pallas_call(
        body, name=name, grid=(s // tm,),
        in_specs=[row, pl.BlockSpec((MLA_HEADS, tm, MLA_V), lambda i: (0, i, 0)),
                  pl.BlockSpec((SWA_HEADS, tm, SWA_HEAD_DIM), lambda i: (0, i, 0)),
                  _full((1, MLA_WIDTH)), _full((1, SWA_WIDTH)), _resident(wo.shape)],
        out_specs=row, out_shape=jax.ShapeDtypeStruct((s, D_MODEL), F32),
        compiler_params=_params(dimension_semantics=("arbitrary",)),
    )(x, oa, ob, ga, gb, wo)


def _post_bwd(dy, oa, ob, ga, gb, wo, name):
    s = dy.shape[0]
    tm = _row_tile(s, 512)

    def body(dy_ref, oa_ref, ob_ref, ga_ref, gb_ref, wo_ref, doa_ref, dob_ref, dga_ref, dgb_ref, dwo_ref, acc_ref):
        i = pl.program_id(0)
        wo_v = wo_ref[...]
        dyv = dy_ref[...]

        def f(oa_l, ob_l, ga_v, gb_v):
            return _post_math(oa_l, ob_l, ga_v, gb_v, wo_v)

        _, vjp, mixed = jax.vjp(f, [oa_ref[hd] for hd in range(MLA_HEADS)], [ob_ref[hd] for hd in range(SWA_HEADS)],
                                ga_ref[...], gb_ref[...], has_aux=True)
        doa, dob, dga, dgb = vjp(dyv)
        for hd in range(MLA_HEADS):
            doa_ref[hd] = doa[hd]
        for hd in range(SWA_HEADS):
            dob_ref[hd] = dob[hd]
        dwo = _dot_tn(mixed.astype(BF16), dyv.astype(BF16))

        @pl.when(i == 0)
        def _():
            dga_ref[...] = dga
            dgb_ref[...] = dgb
            acc_ref[...] = dwo

        @pl.when(i > 0)
        def _():
            dga_ref[...] += dga
            dgb_ref[...] += dgb
            acc_ref[...] += dwo

        @pl.when(i == s // tm - 1)
        def _():
            dwo_ref[...] = acc_ref[...].astype(BF16)

    row = pl.BlockSpec((tm, D_MODEL), lambda i: (i, 0))
    oa_spec = pl.BlockSpec((MLA_HEADS, tm, MLA_V), lambda i: (0, i, 0))
    ob_spec = pl.BlockSpec((SWA_HEADS, tm, SWA_HEAD_DIM), lambda i: (0, i, 0))
    return _pallas_call(
        body, name=name, grid=(s // tm,),
        in_specs=[row, oa_spec, ob_spec, _full((1, MLA_WIDTH)), _full((1, SWA_WIDTH)), _resident(wo.shape)],
        out_specs=[oa_spec, ob_spec, _full((1, MLA_WIDTH)), _full((1, SWA_WIDTH)), _full(wo.shape)],
        out_shape=[jax.ShapeDtypeStruct((MLA_HEADS, s, MLA_V), F32), jax.ShapeDtypeStruct((SWA_HEADS, s, SWA_HEAD_DIM), F32),
                   jax.ShapeDtypeStruct((1, MLA_WIDTH), F32), jax.ShapeDtypeStruct((1, SWA_WIDTH), F32),
                   jax.ShapeDtypeStruct(wo.shape, BF16)],
        scratch_shapes=[pltpu.VMEM(wo.shape, F32)],
        compiler_params=_params(dimension_semantics=("arbitrary",)),
    )(dy, oa, ob, ga, gb, wo)


def _attn_tile(s):
    return 512 if s >= 2048 else 128


def _causal_mask(t):
    return lax.broadcasted_iota(jnp.int32, (t, t), 1) <= lax.broadcasted_iota(jnp.int32, (t, t), 0)


def _pipelined_blocks(first, count, last_block, issue, consume, carry, prefetch_after):
    def clamped(j, slot):
        issue(jnp.minimum(j, last_block), slot)

    def pair(jj, c):
        a = first + 2 * jj
        clamped(a + 1, 1)
        c = consume(a, 0, c)
        clamped(a + 2, 0)
        return consume(a + 1, 1, c)

    clamped(first, 0)
    npairs = count // 2
    carry = lax.fori_loop(0, npairs, pair, carry)

    def odd(c):
        c = consume(first + 2 * npairs, 0, c)
        if prefetch_after:
            clamped(first + count, 0)
        return c

    return lax.cond(count - 2 * npairs == 1, odd, lambda c: c, carry)


def _mla_fwd(q, k, v, name):
    nh, s, _ = q.shape
    t = _attn_tile(s)
    nq = s // t

    def body(q_ref, k_ref, v_ref, o_ref, lse_ref, s0_ref, s1_ref):
        qi = pl.program_id(1)
        qv = q_ref[...]
        s_refs = (s0_ref, s1_ref)

        def rows(j):
            return pl.ds(pl.multiple_of(j * t, t), t)

        def issue(j, slot):
            s_refs[slot][...] = _dot_nt(qv, k_ref[rows(j), :])

        def consume(j, slot, carry, masked=False):
            m, l, acc = carry
            sc = s_refs[slot][...]
            if masked:
                sc = jnp.where(_causal_mask(t), sc, NEG)
            m_new = jnp.maximum(m, jnp.max(sc, axis=-1, keepdims=True))
            alpha = jnp.exp2(m - m_new)
            p = jnp.exp2(sc - m_new)
            l = alpha * l + jnp.sum(p, axis=-1, keepdims=True)
            acc = alpha * acc + jnp.dot(p.astype(BF16), v_ref[rows(j), :], preferred_element_type=F32)
            return m_new, l, acc

        init = (jnp.full((t, 1), NEG, F32), jnp.zeros((t, 1), F32), jnp.zeros((t, MLA_V), F32))
        carry = _pipelined_blocks(0, qi, nq - 1, issue, consume, init, True)
        m, l, acc = consume(qi, 0, carry, masked=True)
        o_ref[...] = acc / l
        lse_ref[...] = m + jnp.log2(l)

    return _pallas_call(
        body, name=name, grid=(nh, nq),
        in_specs=[pl.BlockSpec((None, t, MLA_QK), lambda h, i: (h, i, 0)), pl.BlockSpec((None, s, MLA_QK), lambda h, i: (h, 0, 0)),
                  pl.BlockSpec((None, s, MLA_V), lambda h, i: (h, 0, 0))],
        out_specs=[pl.BlockSpec((None, t, MLA_V), lambda h, i: (h, i, 0)), pl.BlockSpec((None, t, 1), lambda h, i: (h, i, 0))],
        out_shape=[jax.ShapeDtypeStruct((nh, s, MLA_V), F32), jax.ShapeDtypeStruct((nh, s, 1), F32)],
        scratch_shapes=[pltpu.VMEM((t, t), F32)] * 2,
        compiler_params=_params(dimension_semantics=("arbitrary", "arbitrary")),
    )(q, k, v)


def _mla_bwd_dq(q, k, v, o, do, lse, name):
    nh, s, _ = q.shape
    t = _attn_tile(s)
    nq = s // t

    def body(q_ref, k_ref, v_ref, o_ref, do_ref, lse_ref, dq_ref, delta_ref, s0_ref, s1_ref, dp0_ref, dp1_ref):
        qi = pl.program_id(1)
        qv = q_ref[...]
        dov = do_ref[...]
        lse = lse_ref[...]
        delta = jnp.sum(dov * o_ref[...], axis=-1, keepdims=True)
        delta_ref[...] = delta
        dob = dov.astype(BF16)
        s_refs, dp_refs = (s0_ref, s1_ref), (dp0_ref, dp1_ref)

        def rows(j):
            return pl.ds(pl.multiple_of(j * t, t), t)

        def issue(j, slot):
            s_refs[slot][...] = _dot_nt(qv, k_ref[rows(j), :])
            dp_refs[slot][...] = _dot_nt(dob, v_ref[rows(j), :])

        def consume(j, slot, dq, masked=False):
            p = jnp.exp2(s_refs[slot][...] - lse)
            if masked:
                p = jnp.where(_causal_mask(t), p, 0.0)
            ds = p * (dp_refs[slot][...] - delta)
            return dq + jnp.dot(ds.astype(BF16), k_ref[rows(j), :], preferred_element_type=F32)

        dq = _pipelined_blocks(0, qi, nq - 1, issue, consume, jnp.zeros((t, MLA_QK), F32), True)
        dq_ref[...] = consume(qi, 0, dq, masked=True) * MLA_SCALE

    tile = lambda w: pl.BlockSpec((None, t, w), lambda h, i: (h, i, 0))
    whole = lambda w: pl.BlockSpec((None, s, w), lambda h, i: (h, 0, 0))
    return _pallas_call(
        body, name=name, grid=(nh, nq),
        in_specs=[tile(MLA_QK), whole(MLA_QK), whole(MLA_V), tile(MLA_V), tile(MLA_V), tile(1)],
        out_specs=[tile(MLA_QK), tile(1)],
        out_shape=[jax.ShapeDtypeStruct((nh, s, MLA_QK), F32), jax.ShapeDtypeStruct((nh, s, 1), F32)],
        scratch_shapes=[pltpu.VMEM((t, t), F32)] * 4,
        compiler_params=_params(dimension_semantics=("arbitrary", "arbitrary")),
    )(q, k, v, o, do, lse)


def _mla_bwd_dkv(q, k, v, do, lse_row, delta_row, name):
    nh, s, _ = q.shape
    t = _attn_tile(s)
    nq = s // t

    def body(q_ref, k_ref, v_ref, do_ref, lse_ref, delta_ref, dk_ref, dv_ref, s0_ref, s1_ref, dp0_ref, dp1_ref):
        kj = pl.program_id(1)
        kv_, vv = k_ref[...], v_ref[...]
        s_refs, dp_refs = (s0_ref, s1_ref), (dp0_ref, dp1_ref)

        def rows(i):
            return pl.ds(pl.multiple_of(i * t, t), t)

        def issue(i, slot):
            s_refs[slot][...] = _dot_nt(kv_, q_ref[rows(i), :])
            dp_refs[slot][...] = _dot_nt(vv, do_ref[rows(i), :].astype(BF16))

        def consume(i, slot, carry, masked=False):
            dk, dv = carry
            p = jnp.exp2(s_refs[slot][...] - lse_ref[pl.ds(i, 1), :])
            if masked:
                p = jnp.where(lax.broadcasted_iota(jnp.int32, (t, t), 0) <= lax.broadcasted_iota(jnp.int32, (t, t), 1), p, 0.0)
            dv = dv + jnp.dot(p.astype(BF16), do_ref[rows(i), :].astype(BF16), preferred_element_type=F32)
            ds = p * (dp_refs[slot][...] - delta_ref[pl.ds(i, 1), :])
            dk = dk + jnp.dot(ds.astype(BF16), q_ref[rows(i), :], preferred_element_type=F32)
            return dk, dv

        issue(kj, 0)
        carry = consume(kj, 0, (jnp.zeros((t, MLA_QK), F32), jnp.zeros((t, MLA_V), F32)), masked=True)
        dk, dv = _pipelined_blocks(kj + 1, nq - 1 - kj, nq - 1, issue, consume, carry, False)
        dk_ref[...] = dk * LN2
        dv_ref[...] = dv

    tile = lambda w: pl.BlockSpec((None, t, w), lambda h, j: (h, j, 0))
    whole = lambda w: pl.BlockSpec((None, s, w), lambda h, j: (h, 0, 0))
    rows_spec = pl.BlockSpec((None, nq, t), lambda h, j: (h, 0, 0))
    return _pallas_call(
        body, name=name, grid=(nh, nq),
        in_specs=[whole(MLA_QK), tile(MLA_QK), tile(MLA_V), whole(MLA_V), rows_spec, rows_spec],
        out_specs=[tile(MLA_QK), tile(MLA_V)],
        out_shape=[jax.ShapeDtypeStruct((nh, s, MLA_QK), F32), jax.ShapeDtypeStruct((nh, s, MLA_V), F32)],
        scratch_shapes=[pltpu.VMEM((t, t), F32)] * 4,
        compiler_params=_params(dimension_semantics=("arbitrary", "arbitrary")),
    )(q, k, v, do, lse_row, delta_row)


def _swa_tile(s):
    return min(s, 4 * BLOCK)


def _swa_specs(tq):
    nb = tq // BLOCK
    grp = lambda w: pl.BlockSpec((SWA_GROUP, tq, w), lambda j, i: (j, i, 0))
    main = pl.BlockSpec((None, tq, SWA_HEAD_DIM), lambda j, i: (j, i, 0))
    tail = pl.BlockSpec((None, BLOCK, SWA_HEAD_DIM), lambda j, i: (j, nb * (i + 1), 0))
    sink = pl.BlockSpec((None, SWA_GROUP, 128), lambda j, i: (j, 0, 0))
    return grp, main, tail, sink


def _swa_band_mask(first):
    shape = (SWA_GROUP * BLOCK, 2 * BLOCK)
    q_rel = (lax.broadcasted_iota(jnp.int32, shape, 0) & (BLOCK - 1)) + BLOCK
    k_rel = lax.broadcasted_iota(jnp.int32, shape, 1)
    dist = q_rel - k_rel
    return (dist >= 0) & (dist < BLOCK) & ((k_rel >= BLOCK) | jnp.logical_not(first))


def _swa_sink_column(sink_ref):
    sk = sink_ref[...]
    return jnp.concatenate([jnp.broadcast_to(sk[g:g + 1, 0:1], (BLOCK, 1)) for g in range(SWA_GROUP)], axis=0)


def _swa_fwd(q, kpad, vpad, sinks, name):
    _, s, _ = q.shape
    tq = _swa_tile(s)
    grp, main, tail, sink = _swa_specs(tq)
    d = SWA_HEAD_DIM

    def body(q_ref, km_ref, kt_ref, vm_ref, vt_ref, sink_ref, o_ref, lse_ref):
        i = pl.program_id(1)
        kall = jnp.concatenate([km_ref[...], kt_ref[...]], axis=0)
        vall = jnp.concatenate([vm_ref[...], vt_ref[...]], axis=0)
        sink_col = _swa_sink_column(sink_ref)
        for b in range(tq // BLOCK):
            lo = b * BLOCK
            valid = _swa_band_mask(i == 0 if b == 0 else False)
            q4 = q_ref[:, lo:lo + BLOCK, :].reshape(SWA_GROUP * BLOCK, d)
            sc = jnp.where(valid, _dot_nt(q4, kall[lo:lo + 2 * BLOCK]) * SWA_SCALE, NEG)
            m = jnp.maximum(jnp.max(sc, axis=-1, keepdims=True), sink_col)
            e = jnp.exp(sc - m)
            den = jnp.sum(e, axis=-1, keepdims=True) + jnp.exp(sink_col - m)
            out = jnp.dot((e * (1.0 / den)).astype(BF16), vall[lo:lo + 2 * BLOCK], preferred_element_type=F32)
            o_ref[:, lo:lo + BLOCK, :] = out.reshape(SWA_GROUP, BLOCK, d)
            lse_ref[:, lo:lo + BLOCK, :] = (m + jnp.log(den)).reshape(SWA_GROUP, BLOCK, 1)

    return _pallas_call(
        body, name=name, grid=(SWA_KV_HEADS, s // tq),
        in_specs=[grp(d), main, tail, main, tail, sink], out_specs=[grp(d), grp(1)],
        out_shape=[jax.ShapeDtypeStruct((SWA_HEADS, s, d), F32), jax.ShapeDtypeStruct((SWA_HEADS, s, 1), F32)],
        compiler_params=_params(dimension_semantics=("arbitrary", "arbitrary")),
    )(q, kpad, kpad, vpad, vpad, sinks)


def _swa_bwd(q, kpad, vpad, sinks, o, lse, do, name):
    _, s, _ = q.shape
    tq = _swa_tile(s)
    grp, main, tail, sink = _swa_specs(tq)
    d = SWA_HEAD_DIM

    def body(q_ref, km_ref, kt_ref, vm_ref, vt_ref, sink_ref, o_ref, lse_ref, do_ref, dq_ref, dk_ref, dv_ref, dsink_ref):
        i = pl.program_id(1)
        kall = jnp.concatenate([km_ref[...], kt_ref[...]], axis=0)
        vall = jnp.concatenate([vm_ref[...], vt_ref[...]], axis=0)
        sink_col = _swa_sink_column(sink_ref)

        @pl.when(i == 0)
        def _():
            dk_ref[...] = jnp.zeros_like(dk_ref)
            dv_ref[...] = jnp.zeros_like(dv_ref)
            dsink_ref[...] = jnp.zeros_like(dsink_ref)

        dsink = jnp.zeros((SWA_GROUP * BLOCK, 1), F32)
        for b in range(tq // BLOCK):
            lo = b * BLOCK
            valid = _swa_band_mask(i == 0 if b == 0 else False)
            rows4 = SWA_GROUP * BLOCK
            q4 = q_ref[:, lo:lo + BLOCK, :].reshape(rows4, d)
            do4 = do_ref[:, lo:lo + BLOCK, :].reshape(rows4, d)
            lse4 = lse_ref[:, lo:lo + BLOCK, :].reshape(rows4, 1)
            delta = jnp.sum(do4 * o_ref[:, lo:lo + BLOCK, :].reshape(rows4, d), axis=-1, keepdims=True)
            kb, vb = kall[lo:lo + 2 * BLOCK], vall[lo:lo + 2 * BLOCK]
            do4b = do4.astype(BF16)
            p = jnp.where(valid, jnp.exp(_dot_nt(q4, kb) * SWA_SCALE - lse4), 0.0)
            ds = (p * (_dot_nt(do4b, vb) - delta) * SWA_SCALE).astype(BF16)
            dq_ref[:, lo:lo + BLOCK, :] = jnp.dot(ds, kb, preferred_element_type=F32).reshape(SWA_GROUP, BLOCK, d)
            band = pl.ds(pl.multiple_of(i * tq, BLOCK) + lo, 2 * BLOCK)
            dk_ref[band, :] += _dot_tn(ds, q4)
            dv_ref[band, :] += _dot_tn(p.astype(BF16), do4b)
            dsink = dsink - jnp.exp(sink_col - lse4) * delta
        per_head = [jnp.broadcast_to(jnp.sum(dsink[g * BLOCK:(g + 1) * BLOCK], axis=0, keepdims=True), (1, 128))
                    for g in range(SWA_GROUP)]
        dsink_ref[...] += jnp.concatenate(per_head + [jnp.zeros((8 - SWA_GROUP, 128), F32)], axis=0)

    acc = pl.BlockSpec((None, s + BLOCK, d), lambda j, i: (j, 0, 0))
    return _pallas_call(
        body, name=name, grid=(SWA_KV_HEADS, s // tq),
        in_specs=[grp(d), main, tail, main, tail, sink, grp(d), grp(1), grp(d)],
        out_specs=[grp(d), acc, acc, pl.BlockSpec((None, 8, 128), lambda j, i: (j, 0, 0))],
        out_shape=[jax.ShapeDtypeStruct((SWA_HEADS, s, d), F32),
                   jax.ShapeDtypeStruct((SWA_KV_HEADS, s + BLOCK, d), F32),
                   jax.ShapeDtypeStruct((SWA_KV_HEADS, s + BLOCK, d), F32),
                   jax.ShapeDtypeStruct((SWA_KV_HEADS, 8, 128), F32)],
        compiler_params=_params(dimension_semantics=("arbitrary", "arbitrary")),
    )(q, kpad, kpad, vpad, vpad, sinks, o, lse, do)


def _loss_head(y, target, name):
    s = y.shape[0]
    tm = _row_tile(s, 512)

    def body(y_ref, t_ref, dy_ref, loss_ref):
        i = pl.program_id(0)
        err = y_ref[...] - t_ref[...]
        dy_ref[...] = err * (1.0 / D_MODEL)
        part = jnp.broadcast_to(0.5 * jnp.sum(jnp.mean(err * err, axis=-1, keepdims=True), axis=0, keepdims=True), (1, 128))

        @pl.when(i == 0)
        def _():
            loss_ref[...] = part

        @pl.when(i > 0)
        def _():
            loss_ref[...] += part

    row = pl.BlockSpec((tm, D_MODEL), lambda i: (i, 0))
    return _pallas_call(
        body, name=name, grid=(s // tm,), in_specs=[row, row], out_specs=[row, _full((1, 128))],
        out_shape=[jax.ShapeDtypeStruct((s, D_MODEL), F32), jax.ShapeDtypeStruct((1, 128), F32)],
        compiler_params=_params(dimension_semantics=("arbitrary",)),
    )(y, target)


def _adamw(w, g, m, v, name):
    rows, cols = w.shape
    tr = rows
    for cand in (512, 256, 128, 64, 32, 16, 8):
        if rows % cand == 0 and rows > cand:
            tr = cand
            break

    def body(w_ref, g_ref, m_ref, v_ref, d_ref, nm_ref, nv_ref):
        gv = g_ref[...]
        nm = ADAM_B1 * m_ref[...] + (1.0 - ADAM_B1) * gv
        nv = ADAM_B2 * v_ref[...] + (1.0 - ADAM_B2) * (gv * gv)
        m_hat = nm / (1.0 - ADAM_B1 ** ADAM_STEP)
        v_hat = nv / (1.0 - ADAM_B2 ** ADAM_STEP)
        d_ref[...] = -ADAM_LR * (m_hat / (jnp.sqrt(v_hat) + ADAM_EPS) + ADAM_WD * w_ref[...])
        nm_ref[...] = nm
        nv_ref[...] = nv

    blk = pl.BlockSpec((tr, cols), lambda i: (i, 0))
    return _pallas_call(
        body, name=name, grid=(rows // tr,), in_specs=[blk] * 4, out_specs=[blk] * 3,
        out_shape=[jax.ShapeDtypeStruct((rows, cols), F32)] * 3,
        compiler_params=_params(dimension_semantics=("arbitrary",)),
    )(w, g, m, v)


def _position():
    return lax.axis_index("x"), lax.axis_index("y"), lax.axis_index("c")


def _remote(src, dst, send_sems, recv_sems, k, to):
    return pltpu.make_async_remote_copy(src_ref=src, dst_ref=dst, send_sem=send_sems.at[k], recv_sem=recv_sems.at[k],
                                        device_id=to, device_id_type=MESH)


_HBM = pl.BlockSpec(memory_space=pltpu.HBM)


def _all_gather_layers(mine, name):
    na = len(mine)

    def body(*refs):
        ins, outs, (send_sems, recv_sems) = refs[:na], refs[na:2 * na], refs[2 * na:]
        x, y, c = _position()
        me, sibling = (x, y, c), (x, y, 1 - c)
        xn, yn, dg = (1 - x, y), (x, 1 - y), (1 - x, 1 - y)

        def slot(a, chip, pc, half=None):
            ref = outs[a].at[4 * chip[0] + 2 * chip[1] + pc]
            if half is None:
                return ref
            rows = ref.shape[0] // 2
            return ref.at[pl.ds(half * rows, rows)]

        def cp(a, k, chip, pc, half, to, src=None):
            dst = slot(a, chip, pc, half)
            return _remote(dst if src is None else src, dst, send_sems, recv_sems, 8 * a + k, to)

        sent = []
        for a in range(na):
            sent += [cp(a, 0, (x, y), c, None, (*xn, c), src=ins[a].at[c]), cp(a, 1, (x, y), c, None, (*yn, c), src=ins[a].at[c])]
        for s_ in sent:
            s_.start()
        later = []
        for k, frm, fwd_k, fwd_half, fwd_to in ((0, xn, 2, 0, yn), (1, yn, 3, 1, xn)):
            for a in range(na):
                cp(a, k, frm, c, None, me).wait_recv()
                later += [cp(a, fwd_k, frm, c, fwd_half, (*fwd_to, c)), cp(a, 4 + k, frm, c, None, sibling)]
                later[-2].start()
                later[-1].start()
        for k, half in ((2, 0), (3, 1)):
            for a in range(na):
                cp(a, k, dg, c, half, me).wait_recv()
                later.append(cp(a, 4 + k, dg, c, half, sibling))
                later[-1].start()
        for a in range(na):
            for k, chip, half in ((4, xn, None), (5, yn, None), (6, dg, 0), (7, dg, 1)):
                cp(a, k, chip, 1 - c, half, me).wait_recv()
        for s_ in sent + later:
            s_.wait_send()

    return _pallas_call(
        body, name=name, in_specs=[_HBM] * na, out_specs=[_HBM] * na,
        out_shape=[jax.ShapeDtypeStruct((N_DEV,) + m.shape[1:], m.dtype) for m in mine],
        scratch_shapes=[pltpu.SemaphoreType.DMA((8 * na,)), pltpu.SemaphoreType.DMA((8 * na,))],
    )(*mine)


def _sibling_exchange(parts, name, other_layer):
    na = len(parts)

    def body(*refs):
        ins, outs, (send_sems, recv_sems) = refs[:na], refs[na:2 * na], refs[2 * na:]
        x, y, c = _position()
        copies = [_remote(ins[a].at[1 - c] if other_layer else ins[a], outs[a], send_sems, recv_sems, a, (x, y, 1 - c))
                  for a in range(na)]
        for cp in copies:
            cp.start()
        for cp in copies:
            cp.wait()

    return _pallas_call(
        body, name=name, in_specs=[_HBM] * na, out_specs=[_HBM] * na,
        out_shape=[jax.ShapeDtypeStruct(p.shape[1:] if other_layer else p.shape, p.dtype) for p in parts],
        scratch_shapes=[pltpu.SemaphoreType.DMA((na,)), pltpu.SemaphoreType.DMA((na,))],
    )(*parts)


def _scatter_to_chips(parts, name):
    na = len(parts)

    def body(*refs):
        ins, outs, (send_sems, recv_sems) = refs[:na], refs[na:2 * na], refs[2 * na:]
        x, y, c = _position()
        chips = [(1 - x, y), (x, 1 - y), (1 - x, 1 - y)]
        copies = [_remote(ins[a].at[2 * px + py], outs[a].at[j], send_sems, recv_sems, 3 * a + j, (px, py, c))
                  for a in range(na) for j, (px, py) in enumerate(chips)]
        for cp in copies:
            cp.start()
        for cp in copies:
            cp.wait()

    return _pallas_call(
        body, name=name, in_specs=[_HBM] * na, out_specs=[_HBM] * na,
        out_shape=[jax.ShapeDtypeStruct((3,) + p.shape[1:], p.dtype) for p in parts],
        scratch_shapes=[pltpu.SemaphoreType.DMA((3 * na,)), pltpu.SemaphoreType.DMA((3 * na,))],
    )(*parts)


def _assemble(g4, mine, name, side_by_side):
    _, nl, r, w = g4.shape
    tr = next(cand for cand in (256, 128) if r % cand == 0)

    def body(in_ref, mine_ref, out_ref):
        chip = 2 * lax.axis_index("x") + lax.axis_index("y")
        blocks = [jnp.where(chip == sh, mine_ref[...], in_ref[sh]) for sh in range(N_SHARD)]
        if side_by_side:
            out_ref[...] = jnp.concatenate(blocks, axis=-1)
        else:
            for sh in range(N_SHARD):
                out_ref[sh] = blocks[sh]

    if side_by_side:
        out_spec = pl.BlockSpec((None, tr, N_SHARD * w), lambda l, i: (l, i, 0))
        out_shape = jax.ShapeDtypeStruct((nl, r, N_SHARD * w), g4.dtype)
    else:
        out_spec = pl.BlockSpec((None, N_SHARD, tr, w), lambda l, i: (l, 0, i, 0))
        out_shape = jax.ShapeDtypeStruct((nl, N_SHARD, r, w), g4.dtype)
    return _pallas_call(
        body, name=name, grid=(nl, r // tr),
        in_specs=[pl.BlockSpec((N_SHARD, None, tr, w), lambda l, i: (0, l, i, 0)),
                  pl.BlockSpec((None, tr, w), lambda l, i: (l, i, 0))],
        out_specs=out_spec, out_shape=out_shape,
        compiler_params=_params(dimension_semantics=("arbitrary", "arbitrary")),
    )(g4, mine)


def _all_reduce_small(vec, name):
    r, l = vec.shape

    def body(v_ref, out_ref, gath_ref, send_sems, recv_sems):
        x, y, c = _position()
        me = 4 * x + 2 * y + c
        gath_ref[me] = v_ref[...]
        copies = []
        for k in range(1, N_DEV):
            to = (x ^ (k >> 2), y ^ ((k >> 1) & 1), c ^ (k & 1))
            copies.append(_remote(gath_ref.at[me], gath_ref.at[me], send_sems, recv_sems, k - 1, to))
        for cp in copies:
            cp.start()
        for k in range(1, N_DEV):
            frm = 4 * (x ^ (k >> 2)) + 2 * (y ^ ((k >> 1) & 1)) + (c ^ (k & 1))
            _remote(gath_ref.at[frm], gath_ref.at[frm], send_sems, recv_sems, k - 1, (x, y, c)).wait_recv()
        for cp in copies:
            cp.wait_send()
        total = gath_ref[0]
        for d in range(1, N_DEV):
            total = total + gath_ref[d]
        out_ref[...] = total

    vm = pl.BlockSpec(memory_space=pltpu.VMEM)
    return _pallas_call(
        body, name=name, in_specs=[vm], out_specs=vm, out_shape=jax.ShapeDtypeStruct((r, l), F32),
        scratch_shapes=[pltpu.VMEM((N_DEV, r, l), F32), pltpu.SemaphoreType.DMA((N_DEV - 1,)),
                        pltpu.SemaphoreType.DMA((N_DEV - 1,))],
    )(vec)


def _sum_blocks(blocks, out_dtype, name):
    m, w = blocks[0].shape
    tr = next(cand for cand in (512, 256, 128) if m % cand == 0)

    def body(*refs):
        total = refs[0][...].astype(F32)
        for ref in refs[1:-1]:
            total = total + ref[...].astype(F32)
        refs[-1][...] = total.astype(out_dtype)

    blk = pl.BlockSpec((tr, w), lambda i: (i, 0))
    return _pallas_call(
        body, name=name, grid=(m // tr,), in_specs=[blk] * len(blocks), out_specs=blk,
        out_shape=jax.ShapeDtypeStruct((m, w), out_dtype),
        compiler_params=_params(dimension_semantics=("arbitrary",)),
    )(*blocks)


GROUPS = (('ffn1_w_gate', 'ffn1_w_up', 'ffn2_w_gate', 'ffn2_w_up'), ('ffn1_w_down', 'ffn2_w_down', 'w_o'),
          ('w_in',), ('mla_w_q_b',), ('mla_w_kv_b',))


def _shard_rows(name):
    shape, axis = BIG[name]
    return shape[0] // N_SHARD if axis == 0 else shape[0]


def _group_row_offsets(group):
    return [int(v) for v in np.cumsum([0] + [_shard_rows(n) for n in group])]


def _rope_tables(s):
    pos = jnp.arange(s, dtype=F32)
    inv = 1.0 / (ROPE_THETA ** (jnp.arange(0, MLA_ROPE, 2, dtype=F32) / MLA_ROPE))
    ang = pos[:, None] * inv[None, :]
    return jnp.cos(ang), jnp.sin(ang)


def _local_step(x, target, small, big):
    s = x.shape[0]
    cos, sin = _rope_tables(s)
    row = lambda name, l: small[name][l][None, :]
    saved = []
    for l in range(DEPTH):
        sv = {'x0': x}
        x, sv['g1'], sv['u1'] = _ffn_fwd(x, row('ffn1_norm', l), big['ffn1_w_gate'][l], big['ffn1_w_up'][l],
                                        big['ffn1_w_down'][l], f"ffn1_fwd_{l}")
        sv['x1'] = x
        gains = [row(n, l) for n in ('mix_norm', 'mla_q_a_norm', 'mla_kv_a_norm', 'mla_q_norm', 'mla_k_norm',
                                     'swa_q_norm', 'swa_k_norm')]
        q_a, k_a, v_a, q_b, k_b, v_b = _pre_fwd(x, gains, big['w_in'][l], big['mla_w_q_b'][l], big['mla_w_kv_b'][l],
                                                cos, sin, f"pre_fwd_{l}")
        o_a, lse = _mla_fwd(q_a, k_a, v_a, f"mla_fwd_{l}")
        kpad = jnp.pad(k_b, ((0, 0), (BLOCK, 0), (0, 0)))
        vpad = jnp.pad(v_b, ((0, 0), (BLOCK, 0), (0, 0)))
        sinks = jnp.broadcast_to(small['swa_sinks'][l].reshape(SWA_KV_HEADS, SWA_GROUP, 1), (SWA_KV_HEADS, SWA_GROUP, 128))
        o_b, lse_b = _swa_fwd(q_b, kpad, vpad, sinks, f"swa_fwd_{l}")
        sv.update(gains=gains, q_a=q_a, k_a=k_a, v_a=v_a, q_b=q_b, kpad=kpad, vpad=vpad, sinks=sinks, o_a=o_a, lse=lse, o_b=o_b,
                  lse_b=lse_b)
        x = _post_fwd(x, o_a, o_b, row('mla_out_norm', l), row('swa_out_norm', l), big['w_o'][l], f"post_fwd_{l}")
        sv['x2'] = x
        x, sv['g2'], sv['u2'] = _ffn_fwd(x, row('ffn2_norm', l), big['ffn2_w_gate'][l], big['ffn2_w_up'][l],
                                        big['ffn2_w_down'][l], f"ffn2_fwd_{l}")
        saved.append(sv)

    dx, loss = _loss_head(x, target, "loss_head")

    gs = {n: [None] * DEPTH for n in SMALL_NAMES}
    gb = {n: [None] * DEPTH for n in BIG_NAMES}
    t = _attn_tile(s)
    for l in reversed(range(DEPTH)):
        sv = saved[l]

        def ffn_back(tag, xin, dy, gate, up):
            dxi, dgain, nb, act, dgate, dup = _ffn_bwd(xin, dy, gate, up, row(tag + '_norm', l), big[tag + '_w_gate'][l],
                                                      big[tag + '_w_up'][l], big[tag + '_w_down'][l], f"{tag}_bwd_{l}")
            gs[tag + '_norm'][l] = dgain[0]
            gb[tag + '_w_gate'][l] = _matmul_tn(nb, dgate, 1.0, f"{tag}_dwg_{l}", True)
            gb[tag + '_w_up'][l] = _matmul_tn(nb, dup, 1.0, f"{tag}_dwu_{l}", True)
            gb[tag + '_w_down'][l] = _matmul_tn(act, dy, 0.5, f"{tag}_dwd_{l}", False).reshape(N_SHARD, D_FF // N_SHARD, D_MODEL)
            return dxi

        dx = ffn_back('ffn2', sv['x2'], dx, sv['g2'], sv['u2'])
        do_a, do_b, dga, dgb, dwo = _post_bwd(dx, sv['o_a'], sv['o_b'], row('mla_out_norm', l), row('swa_out_norm', l),
                                              big['w_o'][l], f"post_bwd_{l}")
        gs['mla_out_norm'][l], gs['swa_out_norm'][l] = dga[0], dgb[0]
        gb['w_o'][l] = dwo.reshape(N_SHARD, MIX_WIDTH // N_SHARD, D_MODEL)
        dq_a, delta = _mla_bwd_dq(sv['q_a'], sv['k_a'], sv['v_a'], sv['o_a'], do_a, sv['lse'], f"mla_dq_{l}")
        dk_a, dv_a = _mla_bwd_dkv(sv['q_a'], sv['k_a'], sv['v_a'], do_a, sv['lse'].reshape(MLA_HEADS, s // t, t),
                                  delta.reshape(MLA_HEADS, s // t, t), f"mla_dkv_{l}")
        dq_b, dkpad, dvpad, dsink = _swa_bwd(sv['q_b'], sv['kpad'], sv['vpad'], sv['sinks'], sv['o_b'], sv['lse_b'], do_b,
                                             f"swa_bwd_{l}")
        gs['swa_sinks'][l] = dsink[:, :SWA_GROUP, 0].reshape(SWA_HEADS)
        cts = [dq_a, dk_a, dv_a, dq_b, dkpad[:, BLOCK:], dvpad[:, BLOCK:]]
        outs = _pre_bwd(sv['x1'], dx, cts, sv['gains'], big['w_in'][l], big['mla_w_q_b'][l], big['mla_w_kv_b'][l],
                        cos, sin, f"pre_bwd_{l}")
        dx = outs[0]
        for n, val in zip(('mix_norm', 'mla_q_a_norm', 'mla_kv_a_norm', 'mla_q_norm', 'mla_k_norm', 'swa_q_norm', 'swa_k_norm'),
                          outs[1:8]):
            gs[n][l] = val[0]
        gb['w_in'][l], gb['mla_w_q_b'][l], gb['mla_w_kv_b'][l] = outs[8:11]
        dx = ffn_back('ffn1', sv['x0'], dx, sv['g1'], sv['u1'])
    return loss, dx, gs, gb


def kernel(x, ffn1_norm, ffn1_w_gate, ffn1_w_up, ffn1_w_down, mix_norm, w_in, mla_q_a_norm, mla_w_q_b, mla_kv_a_norm, mla_w_kv_b, mla_q_norm, mla_k_norm, swa_q_norm, swa_k_norm, swa_sinks, mla_out_norm, swa_out_norm, w_o, ffn2_norm, ffn2_w_gate, ffn2_w_up, ffn2_w_down, loss_target, m_ffn1_norm, m_ffn1_w_gate, m_ffn1_w_up, m_ffn1_w_down, m_mix_norm, m_w_in, m_mla_q_a_norm, m_mla_w_q_b, m_mla_kv_a_norm, m_mla_w_kv_b, m_mla_q_norm, m_mla_k_norm, m_swa_q_norm, m_swa_k_norm, m_swa_sinks, m_mla_out_norm, m_swa_out_norm, m_w_o, m_ffn2_norm, m_ffn2_w_gate, m_ffn2_w_up, m_ffn2_w_down, v_ffn1_norm, v_ffn1_w_gate, v_ffn1_w_up, v_ffn1_w_down, v_mix_norm, v_w_in, v_mla_q_a_norm, v_mla_w_q_b, v_mla_kv_a_norm, v_mla_w_kv_b, v_mla_q_norm, v_mla_k_norm, v_swa_q_norm, v_swa_k_norm, v_swa_sinks, v_mla_out_norm, v_swa_out_norm, v_w_o, v_ffn2_norm, v_ffn2_w_gate, v_ffn2_w_up, v_ffn2_w_down):
    args = dict(locals())
    weights = {n: args[n] for n in WEIGHT_NAMES}
    mom_m = {n: args["m_" + n] for n in WEIGHT_NAMES}
    mom_v = {n: args["v_" + n] for n in WEIGHT_NAMES}

    mine = [jnp.concatenate([weights[n].astype(BF16) for n in group], axis=1) for group in GROUPS]
    c = lax.axis_index("c")
    chip = 2 * lax.axis_index("x") + lax.axis_index("y")
    gathered = [g.reshape((N_SHARD, DEPTH) + g.shape[1:]) for g in _all_gather_layers(mine, "gather_weights")]
    big = {}
    for gi, group in enumerate(GROUPS):
        offs = _group_row_offsets(group)
        col_sharded = BIG[group[0]][1] == 1
        full = _assemble(gathered[gi], mine[gi], f"assemble_{gi}", col_sharded)
        for i, n in enumerate(group):
            if col_sharded:
                big[n] = [full[l, offs[i]:offs[i + 1]] for l in range(DEPTH)]
            else:
                big[n] = [full[l, :, offs[i]:offs[i + 1]].reshape(BIG[n][0]) for l in range(DEPTH)]
    small = {n: weights[n] for n in SMALL_NAMES}

    loss, dx, gs, gb = _local_step(x[0], loss_target[0], small, big)

    small_flat = jnp.concatenate([jnp.stack(gs[n]).reshape(-1) for n in SMALL_NAMES] + [loss[0, :1]])
    n_small = small_flat.shape[0]
    lanes = -(-n_small // (8 * 128)) * 128
    small_sum = _all_reduce_small(jnp.pad(small_flat, (0, 8 * lanes - n_small)).reshape(8, lanes), "reduce_small").reshape(-1)
    grads = {}
    off = 0
    for n in SMALL_NAMES:
        cnt = int(np.prod(weights[n].shape))
        grads[n] = small_sum[off:off + cnt].reshape(weights[n].shape)
        off += cnt
    loss_out = small_sum[off]

    parts = [jnp.stack([jnp.concatenate([gb[n][l] for n in group], axis=1) for l in range(DEPTH)]) for group in GROUPS]
    from_sibling = _sibling_exchange(parts, "swap_layers", True)
    chip_sums = []
    for gi, (p, got) in enumerate(zip(parts, from_sibling)):
        kept = lax.dynamic_index_in_dim(p, c, axis=0, keepdims=False)
        rows = N_SHARD * p.shape[2]
        pair = _sum_blocks([kept.reshape(rows, -1), got.reshape(rows, -1)], BF16, f"sum_pair_{gi}")
        chip_sums.append(pair.reshape(p.shape[1:]))
    from_chips = _scatter_to_chips(chip_sums, "scatter_chips")
    halves = []
    for gi, (cs, got) in enumerate(zip(chip_sums, from_chips)):
        own = lax.dynamic_index_in_dim(cs, chip, axis=0, keepdims=False)
        halves.append(_sum_blocks([own, got[0], got[1], got[2]], F32, f"sum_chips_{gi}"))
    others = _sibling_exchange(halves, "share_layers", False)
    for gi, group in enumerate(GROUPS):
        offs = _group_row_offsets(group)
        both = jnp.stack([jnp.where(c == 0, halves[gi], others[gi]), jnp.where(c == 0, others[gi], halves[gi])])
        for i, n in enumerate(group):
            grads[n] = both[:, offs[i]:offs[i + 1]]

    deltas, new_m, new_v = {}, {}, {}
    for n in WEIGHT_NAMES:
        shp = weights[n].shape
        two_d = (DEPTH, shp[-1]) if len(shp) == 2 else (shp[0] * shp[1], shp[2])
        d, nm, nv = _adamw(weights[n].reshape(two_d), grads[n].reshape(two_d), mom_m[n].reshape(two_d),
                           mom_v[n].reshape(two_d), f"adamw_{n}")
        deltas[n], new_m[n], new_v[n] = d.reshape(shp), nm.reshape(shp), nv.reshape(shp)

    return (loss_out, dx[None], *[grads[n] for n in WEIGHT_NAMES], *[deltas[n] for n in WEIGHT_NAMES],
            *[new_m[n] for n in WEIGHT_NAMES], *[new_v[n] for n in WEIGHT_NAMES])
```

```python
import functools

import numpy as np
import jax
import jax.numpy as jnp
from jax import lax
from jax.experimental import pallas as pl
from jax.experimental.pallas import tpu as pltpu

F32 = jnp.float32
BF16 = jnp.bfloat16

D_MODEL = 1024
DEPTH = 2
EPS = 1e-6
ROPE_THETA = 10000.0
BLOCK = 128
MLA_HEADS = 4
MLA_Q_RANK = 256
MLA_KV_RANK = 128
MLA_NOPE = 128
MLA_ROPE = 64
MLA_V = 128
MLA_QK = MLA_NOPE + MLA_ROPE
MLA_WIDTH = MLA_HEADS * MLA_V
SWA_HEADS = 8
SWA_KV_HEADS = 2
SWA_GROUP = SWA_HEADS // SWA_KV_HEADS
SWA_HEAD_DIM = 64
SWA_WIDTH = SWA_HEADS * SWA_HEAD_DIM
MIX_WIDTH = MLA_WIDTH + SWA_WIDTH
IN_SPLITS = (MLA_Q_RANK, MLA_KV_RANK, MLA_ROPE, SWA_WIDTH, SWA_KV_HEADS * SWA_HEAD_DIM, SWA_KV_HEADS * SWA_HEAD_DIM)
IN_COLS = sum(IN_SPLITS)
IN_OFFS = tuple(int(v) for v in np.cumsum((0,) + IN_SPLITS))
D_FF = 2816
MLA_SCALE = MLA_QK ** -0.5
LOG2E = 1.4426950408889634
LN2 = 0.6931471805599453
MLA_QSCALE = MLA_SCALE * LOG2E
SWA_SCALE = SWA_HEAD_DIM ** -0.5
NEG = -1e30

ADAM_LR = 0.001
ADAM_B1 = 0.9
ADAM_B2 = 0.999
ADAM_EPS = 1e-08
ADAM_WD = 0.01
ADAM_STEP = 10

N_SHARD = 4
N_DEV = 8
VMEM_LIMIT = 56 * 1024 * 1024
MESH = pl.DeviceIdType.MESH

WEIGHT_NAMES = ['ffn1_norm', 'ffn1_w_gate', 'ffn1_w_up', 'ffn1_w_down', 'mix_norm', 'w_in', 'mla_q_a_norm', 'mla_w_q_b',
                'mla_kv_a_norm', 'mla_w_kv_b', 'mla_q_norm', 'mla_k_norm', 'swa_q_norm', 'swa_k_norm', 'swa_sinks',
                'mla_out_norm', 'swa_out_norm', 'w_o', 'ffn2_norm', 'ffn2_w_gate', 'ffn2_w_up', 'ffn2_w_down']
BIG = {'ffn1_w_gate': ((D_MODEL, D_FF), 1), 'ffn1_w_up': ((D_MODEL, D_FF), 1), 'ffn1_w_down': ((D_FF, D_MODEL), 0),
       'w_in': ((D_MODEL, IN_COLS), 1), 'mla_w_q_b': ((MLA_Q_RANK, MLA_HEADS * MLA_QK), 1),
       'mla_w_kv_b': ((MLA_KV_RANK, MLA_HEADS * (MLA_NOPE + MLA_V)), 1), 'w_o': ((MIX_WIDTH, D_MODEL), 0),
       'ffn2_w_gate': ((D_MODEL, D_FF), 1), 'ffn2_w_up': ((D_MODEL, D_FF), 1), 'ffn2_w_down': ((D_FF, D_MODEL), 0)}
BIG_NAMES = [n for n in WEIGHT_NAMES if n in BIG]
SMALL_NAMES = [n for n in WEIGHT_NAMES if n not in BIG]

_pallas_call = pl.pallas_call


def _params(**kw):
    return pltpu.CompilerParams(vmem_limit_bytes=VMEM_LIMIT, **kw)


def _full(shape):
    n = len(shape)
    return pl.BlockSpec(shape, lambda *_: (0,) * n)


def _resident(shape):
    n = len(shape)
    return pl.BlockSpec(shape, lambda *_: (0,) * n, pipeline_mode=pl.Buffered(1))


@jax.custom_vjp
def _mm(a, w):
    return jnp.dot(a.astype(BF16), w, preferred_element_type=F32)


def _mm_fwd(a, w):
    return _mm(a, w), w


def _mm_bwd(w, dy):
    return lax.dot_general(dy.astype(BF16), w, (((1,), (1,)), ((), ())), preferred_element_type=F32), None


_mm.defvjp(_mm_fwd, _mm_bwd)


def _dot_nt(a, b):
    return lax.dot_general(a, b, (((1,), (1,)), ((), ())), preferred_element_type=F32)


def _dot_tn(a, b):
    return lax.dot_general(a, b, (((0,), (0,)), ((), ())), preferred_element_type=F32)


def _rms(t, g):
    return t * lax.rsqrt(jnp.mean(t * t, axis=-1, keepdims=True) + EPS) * g


def _sigmoid(z):
    return 1.0 / (1.0 + jnp.exp(-z))


def _row_tile(s, want):
    return min(want, s)


FF_CHUNK = 1408


def _ffn_fwd(x, g, wg, wu, wd, name):
    s = x.shape[0]
    tm = _row_tile(s, 256)

    def body(x_ref, g_ref, wg_ref, wu_ref, wd_ref, y_ref, gate_ref, up_ref):
        xv = x_ref[...]
        nb = _rms(xv, g_ref[...]).astype(BF16)
        acc = xv
        for c in range(0, D_FF, FF_CHUNK):
            gate = jnp.dot(nb, wg_ref[:, c:c + FF_CHUNK], preferred_element_type=F32)
            up = jnp.dot(nb, wu_ref[:, c:c + FF_CHUNK], preferred_element_type=F32)
            gate_ref[:, c:c + FF_CHUNK] = gate.astype(BF16)
            up_ref[:, c:c + FF_CHUNK] = up.astype(BF16)
            act = (gate * _sigmoid(gate) * up).astype(BF16)
            acc = acc + 0.5 * jnp.dot(act, wd_ref[c:c + FF_CHUNK, :], preferred_element_type=F32)
        y_ref[...] = acc

    return _pallas_call(
        body, name=name, grid=(s // tm,),
        in_specs=[pl.BlockSpec((tm, D_MODEL), lambda i: (i, 0)), _full((1, D_MODEL)),
                  _resident((D_MODEL, D_FF)), _resident((D_MODEL, D_FF)), _resident((D_FF, D_MODEL))],
        out_specs=[pl.BlockSpec((tm, D_MODEL), lambda i: (i, 0)), pl.BlockSpec((tm, D_FF), lambda i: (i, 0)),
                   pl.BlockSpec((tm, D_FF), lambda i: (i, 0))],
        out_shape=[jax.ShapeDtypeStruct((s, D_MODEL), F32), jax.ShapeDtypeStruct((s, D_FF), BF16),
                   jax.ShapeDtypeStruct((s, D_FF), BF16)],
        compiler_params=_params(dimension_semantics=("arbitrary",)),
    )(x, g, wg, wu, wd)


def _ffn_bwd(x, dy, gate, up, g, wg, wu, wd, name):
    s = x.shape[0]
    tm = _row_tile(s, 256)

    def body(x_ref, dy_ref, gate_ref, up_ref, g_ref, wg_ref, wu_ref, wd_ref,
             dx_ref, dgain_ref, n_ref, act_ref, dgate_ref, dup_ref):
        i = pl.program_id(0)
        xv = x_ref[...]
        dyv = dy_ref[...]
        gv = g_ref[...]
        r = lax.rsqrt(jnp.mean(xv * xv, axis=-1, keepdims=True) + EPS)
        xh = xv * r
        n_ref[...] = (xh * gv).astype(BF16)
        dyh = (0.5 * dyv).astype(BF16)
        dn = jnp.zeros_like(xv)
        for c in range(0, D_FF, FF_CHUNK):
            dact = _dot_nt(dyh, wd_ref[c:c + FF_CHUNK, :])
            gt = gate_ref[:, c:c + FF_CHUNK].astype(F32)
            u = up_ref[:, c:c + FF_CHUNK].astype(F32)
            sg = _sigmoid(gt)
            sl = gt * sg
            act_ref[:, c:c + FF_CHUNK] = (sl * u).astype(BF16)
            dup = (dact * sl).astype(BF16)
            dgate = (dact * u * (sg * (1.0 + gt * (1.0 - sg)))).astype(BF16)
            dup_ref[:, c:c + FF_CHUNK] = dup
            dgate_ref[:, c:c + FF_CHUNK] = dgate
            dn = dn + _dot_nt(dgate, wg_ref[:, c:c + FF_CHUNK]) + _dot_nt(dup, wu_ref[:, c:c + FF_CHUNK])
        part = jnp.sum(dn * xh, axis=0, keepdims=True)

        @pl.when(i == 0)
        def _():
            dgain_ref[...] = part

        @pl.when(i > 0)
        def _():
            dgain_ref[...] += part

        dxh = dn * gv
        dx_ref[...] = dyv + r * (dxh - xh * jnp.mean(dxh * xh, axis=-1, keepdims=True))

    row = lambda w: pl.BlockSpec((tm, w), lambda i: (i, 0))
    return _pallas_call(
        body, name=name, grid=(s // tm,),
        in_specs=[row(D_MODEL), row(D_MODEL), row(D_FF), row(D_FF), _full((1, D_MODEL)),
                  _resident((D_MODEL, D_FF)), _resident((D_MODEL, D_FF)), _resident((D_FF, D_MODEL))],
        out_specs=[row(D_MODEL), _full((1, D_MODEL)), row(D_MODEL), row(D_FF), row(D_FF), row(D_FF)],
        out_shape=[jax.ShapeDtypeStruct((s, D_MODEL), F32), jax.ShapeDtypeStruct((1, D_MODEL), F32),
                   jax.ShapeDtypeStruct((s, D_MODEL), BF16), jax.ShapeDtypeStruct((s, D_FF), BF16),
                   jax.ShapeDtypeStruct((s, D_FF), BF16), jax.ShapeDtypeStruct((s, D_FF), BF16)],
        compiler_params=_params(dimension_semantics=("arbitrary",)),
    )(x, dy, gate, up, g, wg, wu, wd)


def _store_col_shards(o_ref, acc, first_shard, n_here, width):
    for q in range(n_here):
        o_ref[q] = acc[:, (first_shard + q) * width:(first_shard + q + 1) * width].astype(BF16)


def _matmul_tn(a, b, scale, name, col_shards):
    t, m = a.shape
    n = b.shape[1]
    tk = _row_tile(t, 2048)
    tn = n // 2
    per = n // N_SHARD
    nk = t // tk

    def body(a_ref, b_ref, o_ref, acc_ref):
        k = pl.program_id(1)
        bv = b_ref[...]
        if scale != 1.0:
            bv = bv.astype(F32) * scale
        part = _dot_tn(a_ref[...].astype(BF16), bv.astype(BF16))

        @pl.when(k == 0)
        def _():
            acc_ref[...] = part

        @pl.when(k > 0)
        def _():
            acc_ref[...] += part

        @pl.when(k == nk - 1)
        def _():
            if col_shards:
                _store_col_shards(o_ref, acc_ref[...], 0, tn // per, per)
            else:
                o_ref[...] = acc_ref[...].astype(BF16)

    if col_shards:
        out_spec = pl.BlockSpec((tn // per, m, per), lambda j, k: (j, 0, 0))
        out_shape = jax.ShapeDtypeStruct((N_SHARD, m, per), BF16)
    else:
        out_spec = pl.BlockSpec((m, tn), lambda j, k: (0, j))
        out_shape = jax.ShapeDtypeStruct((m, n), BF16)
    return _pallas_call(
        body, name=name, grid=(n // tn, nk),
        in_specs=[pl.BlockSpec((tk, m), lambda j, k: (k, 0)), pl.BlockSpec((tk, tn), lambda j, k: (k, j))],
        out_specs=out_spec, out_shape=out_shape, scratch_shapes=[pltpu.VMEM((m, tn), F32)],
        compiler_params=_params(dimension_semantics=("arbitrary", "arbitrary")),
    )(a, b)


_HALF = SWA_HEAD_DIM // 2
_IN_ORDER = (list(range(0, IN_OFFS[2]))
             + [IN_OFFS[3] + SWA_HEAD_DIM * h + i for h in range(SWA_HEADS) for i in range(_HALF)]
             + [IN_OFFS[3] + SWA_HEAD_DIM * h + _HALF + i for h in range(SWA_HEADS) for i in range(_HALF)]
             + list(range(IN_OFFS[5], IN_OFFS[6]))
             + [IN_OFFS[4] + SWA_HEAD_DIM * j + i for j in range(SWA_KV_HEADS) for i in range(_HALF)]
             + [IN_OFFS[4] + SWA_HEAD_DIM * j + _HALF + i for j in range(SWA_KV_HEADS) for i in range(_HALF)]
             + list(range(IN_OFFS[2], IN_OFFS[3])))
_QB_ORDER = ([MLA_QK * h + i for h in range(MLA_HEADS) for i in range(MLA_NOPE)]
             + [MLA_QK * h + MLA_NOPE + i for h in range(MLA_HEADS) for i in range(_HALF)]
             + [MLA_QK * h + MLA_NOPE + _HALF + i for h in range(MLA_HEADS) for i in range(_HALF)])
_KVB_ORDER = ([(MLA_NOPE + MLA_V) * h + i for h in range(MLA_HEADS) for i in range(MLA_NOPE)]
              + [(MLA_NOPE + MLA_V) * h + MLA_NOPE + i for h in range(MLA_HEADS) for i in range(MLA_V)])
_P_CQ, _P_CKV, _P_QA, _P_QB, _P_VS, _P_KA, _P_KB, _P_PE = (int(v) for v in np.cumsum(
    (0, MLA_Q_RANK, MLA_KV_RANK, SWA_WIDTH // 2, SWA_WIDTH // 2, IN_SPLITS[5], IN_SPLITS[4] // 2, IN_SPLITS[4] // 2)))


def _runs(order):
    out, start = [], 0
    for i in range(1, len(order) + 1):
        if i == len(order) or order[i] != order[i - 1] + 1:
            out.append((order[start], i - start))
            start = i
    return out


def _inverse(order):
    inv = [0] * len(order)
    for new, old in enumerate(order):
        inv[old] = new
    return inv


def _take_cols(a, order):
    return jnp.concatenate([a[..., st:st + w] for st, w in _runs(order)], axis=-1)


def _segment_matrix(n, seg):
    return (lax.broadcasted_iota(jnp.int32, (n, n), 0) // seg == lax.broadcasted_iota(jnp.int32, (n, n), 1) // seg).astype(BF16)


@jax.custom_vjp
def _cmm(t, b, bt):
    hi = t.astype(BF16)
    lo = (t - hi.astype(F32)).astype(BF16)
    return jnp.dot(hi, b, preferred_element_type=F32) + jnp.dot(lo, b, preferred_element_type=F32)


def _cmm_fwd(t, b, bt):
    return _cmm(t, b, bt), (b, bt)


def _cmm_bwd(res, dy):
    b, bt = res
    return _cmm(dy, bt, b), None, None


_cmm.defvjp(_cmm_fwd, _cmm_bwd)


def _segsum(t, b):
    return _cmm(t, b, b)


def _rowsum(t):
    n = t.shape[-1]
    return _cmm(t, jnp.ones((n, 128), BF16), jnp.ones((128, n), BF16))


def _by_head(vals, width):
    lane = lax.broadcasted_iota(jnp.int32, (vals[0].shape[0], len(vals) * width), 1)
    out = vals[-1]
    for hd in range(len(vals) - 2, -1, -1):
        out = jnp.where(lane < (hd + 1) * width, vals[hd], out)
    return out


def _rope2(a, b, cos, sin):
    return a * cos - b * sin, b * cos + a * sin


def _pre_math(x, gm, gqa, gkva, gq, gk, gsq, gsk, taps, win, wqb, wkvb, cos, sin):
    h = _rms(x, gm)
    proj = _mm(h, win)
    if taps is not None:
        proj = proj + taps[0]
    cqn = _rms(proj[:, _P_CQ:_P_CKV], gqa)
    qa_all = _mm(cqn, wqb)
    ckvn = _rms(proj[:, _P_CKV:_P_QA], gkva)
    kv_all = _mm(ckvn, wkvb)
    if taps is not None:
        qa_all = qa_all + taps[1]
        kv_all = kv_all + taps[2]
    nh, hw = MLA_HEADS, MLA_HEADS * _HALF
    seg_mla = _segment_matrix(hw, _HALF)
    tile = lambda g, n: jnp.concatenate([g] * n, axis=-1)
    c4, s4 = cos[:, :hw], sin[:, :hw]

    def mla_heads(nope, r1, r2, gain):
        rr = r1 * r1 + r2 * r2
        lane_head = lax.broadcasted_iota(jnp.int32, (hw, nh * MLA_NOPE), 0) // _HALF
        spread = (lane_head == lax.broadcasted_iota(jnp.int32, (hw, nh * MLA_NOPE), 1) // MLA_NOPE).astype(BF16)
        rope_on_nope = _cmm(rr, spread, spread.T)
        ss_nope = [_rowsum(jnp.square(nope[:, hd * MLA_NOPE:(hd + 1) * MLA_NOPE])) for hd in range(nh)]
        rinv = [lax.rsqrt((ss_nope[hd] + rope_on_nope[:, hd * MLA_NOPE:(hd + 1) * MLA_NOPE]) * (1.0 / MLA_QK) + EPS)
                for hd in range(nh)]
        rl = lax.rsqrt((_segsum(rr, seg_mla) + _by_head(ss_nope, _HALF)) * (1.0 / MLA_QK) + EPS)
        o1, o2 = _rope2(r1 * rl * tile(gain[:, MLA_NOPE:MLA_NOPE + _HALF], nh), r2 * rl * tile(gain[:, MLA_NOPE + _HALF:], nh), c4, s4)
        return [jnp.concatenate([nope[:, hd * MLA_NOPE:(hd + 1) * MLA_NOPE] * rinv[hd] * gain[:, :MLA_NOPE],
                                 o1[:, hd * _HALF:(hd + 1) * _HALF], o2[:, hd * _HALF:(hd + 1) * _HALF]], axis=-1)
                for hd in range(nh)]

    q_a = mla_heads(qa_all[:, :nh * MLA_NOPE], qa_all[:, nh * MLA_NOPE:nh * MLA_NOPE + hw], qa_all[:, nh * MLA_NOPE + hw:], gq)
    pe1, pe2 = proj[:, _P_PE:_P_PE + _HALF], proj[:, _P_PE + _HALF:_P_PE + 2 * _HALF]
    k_a = mla_heads(kv_all[:, :nh * MLA_NOPE], tile(pe1, nh), tile(pe2, nh), gk)
    v_a = [kv_all[:, nh * MLA_NOPE + hd * MLA_V:nh * MLA_NOPE + (hd + 1) * MLA_V] for hd in range(nh)]

    def swa_heads(a, b, gain, n):
        w = n * _HALF
        r = lax.rsqrt(_segsum(a * a + b * b, _segment_matrix(w, _HALF)) * (1.0 / SWA_HEAD_DIM) + EPS)
        o1, o2 = _rope2(a * r * tile(gain[:, :_HALF], n), b * r * tile(gain[:, _HALF:], n), cos[:, :w], sin[:, :w])
        return [jnp.concatenate([o1[:, hd * _HALF:(hd + 1) * _HALF], o2[:, hd * _HALF:(hd + 1) * _HALF]], axis=-1) for hd in range(n)]

    q_b = swa_heads(proj[:, _P_QA:_P_QB], proj[:, _P_QB:_P_VS], gsq, SWA_HEADS)
    k_b = swa_heads(proj[:, _P_KA:_P_KB], proj[:, _P_KB:_P_PE], gsk, SWA_KV_HEADS)
    v_b = [proj[:, _P_VS + j * SWA_HEAD_DIM:_P_VS + (j + 1) * SWA_HEAD_DIM] for j in range(SWA_KV_HEADS)]
    return (q_a, k_a, v_a, q_b, k_b, v_b), (h, cqn, ckvn)


_PRE_GAIN_WIDTHS = (D_MODEL, MLA_Q_RANK, MLA_KV_RANK, MLA_QK, MLA_QK, SWA_HEAD_DIM, SWA_HEAD_DIM)
_PRE_HEADS = ((MLA_HEADS, MLA_QK), (MLA_HEADS, MLA_QK), (MLA_HEADS, MLA_V),
              (SWA_HEADS, SWA_HEAD_DIM), (SWA_KV_HEADS, SWA_HEAD_DIM), (SWA_KV_HEADS, SWA_HEAD_DIM))


def _pre_fwd(x, gains, win, wqb, wkvb, cos, sin, name):
    s = x.shape[0]
    tm = _row_tile(s, 512)

    def body(x_ref, *refs):
        g_refs, (win_ref, wqb_ref, wkvb_ref, cos_ref, sin_ref), out_refs = refs[:7], refs[7:12], refs[12:]
        outs, _ = _pre_math(x_ref[...], *[g[...] for g in g_refs], None, win_ref[...], wqb_ref[...], wkvb_ref[...],
                            cos_ref[...], sin_ref[...])
        for idx, (ref, heads) in enumerate(zip(out_refs, outs)):
            for hd, val in enumerate(heads):
                ref[hd] = (val * MLA_QSCALE if idx == 0 else val).astype(BF16)

    heads_spec = lambda nh, w: pl.BlockSpec((nh, tm, w), lambda i: (0, i, 0))
    return _pallas_call(
        body, name=name, grid=(s // tm,),
        in_specs=[pl.BlockSpec((tm, D_MODEL), lambda i: (i, 0))] + [_full((1, w)) for w in _PRE_GAIN_WIDTHS]
        + [_resident(win.shape), _resident(wqb.shape), _resident(wkvb.shape),
           pl.BlockSpec((tm, SWA_HEADS * _HALF), lambda i: (i, 0)), pl.BlockSpec((tm, SWA_HEADS * _HALF), lambda i: (i, 0))],
        out_specs=[heads_spec(nh, w) for nh, w in _PRE_HEADS],
        out_shape=[jax.ShapeDtypeStruct((nh, s, w), BF16) for nh, w in _PRE_HEADS],
        compiler_params=_params(dimension_semantics=("arbitrary",)),
    )(x, *gains, win, wqb, wkvb, cos, sin)


def _pre_bwd(x, dx_res, cts, gains, win, wqb, wkvb, cos, sin, name):
    s = x.shape[0]
    tm = _row_tile(s, 256)
    tap_widths = (IN_COLS, MLA_HEADS * MLA_QK, MLA_HEADS * (MLA_NOPE + MLA_V))

    def body(x_ref, dxr_ref, *refs):
        ct_refs, g_refs = refs[:6], refs[6:13]
        win_ref, wqb_ref, wkvb_ref, cos_ref, sin_ref = refs[13:18]
        dx_ref, dg_refs, dw_refs, acc_refs = refs[18], refs[19:26], refs[26:29], refs[29:32]
        i = pl.program_id(0)
        win_v, wqb_v, wkvb_v, cos_v, sin_v = win_ref[...], wqb_ref[...], wkvb_ref[...], cos_ref[...], sin_ref[...]

        def f(xv, gm, gqa, gkva, gq, gk, gsq, gsk, t0, t1, t2):
            return _pre_math(xv, gm, gqa, gkva, gq, gk, gsq, gsk, (t0, t1, t2), win_v, wqb_v, wkvb_v, cos_v, sin_v)

        taps = [jnp.zeros((tm, w), F32) for w in tap_widths]
        _, vjp, acts = jax.vjp(f, x_ref[...], *[g[...] for g in g_refs], *taps, has_aux=True)
        ct = tuple([ref[hd] for hd in range(nh)] for ref, (nh, _) in zip(ct_refs, _PRE_HEADS))
        grads = vjp(ct)
        dx_ref[...] = grads[0] + dxr_ref[...]
        dws = [_dot_tn(a.astype(BF16), t.astype(BF16)) for a, t in zip(acts, grads[8:11])]

        @pl.when(i == 0)
        def _():
            for ref, val in zip(dg_refs, grads[1:8]):
                ref[...] = val
            for ref, val in zip(acc_refs, dws):
                ref[...] = val

        @pl.when(i > 0)
        def _():
            for ref, val in zip(dg_refs, grads[1:8]):
                ref[...] += val
            for ref, val in zip(acc_refs, dws):
                ref[...] += val

        @pl.when(i == s // tm - 1)
        def _():
            for ref, acc, order in zip(dw_refs, acc_refs, (_IN_ORDER, _QB_ORDER, _KVB_ORDER)):
                _store_col_shards(ref, _take_cols(acc[...], _inverse(order)), 0, N_SHARD, acc.shape[1] // N_SHARD)

    heads_spec = lambda nh, w: pl.BlockSpec((nh, tm, w), lambda i: (0, i, 0))
    row = pl.BlockSpec((tm, D_MODEL), lambda i: (i, 0))
    half = pl.BlockSpec((tm, SWA_HEADS * _HALF), lambda i: (i, 0))
    shard_shapes = [(N_SHARD, w.shape[0], w.shape[1] // N_SHARD) for w in (win, wqb, wkvb)]
    return _pallas_call(
        body, name=name, grid=(s // tm,),
        in_specs=[row, row] + [heads_spec(nh, w) for nh, w in _PRE_HEADS] + [_full((1, w)) for w in _PRE_GAIN_WIDTHS]
        + [_resident(win.shape), _resident(wqb.shape), _resident(wkvb.shape), half, half],
        out_specs=[row] + [_full((1, w)) for w in _PRE_GAIN_WIDTHS] + [_full(shp) for shp in shard_shapes],
        out_shape=[jax.ShapeDtypeStruct((s, D_MODEL), F32)] + [jax.ShapeDtypeStruct((1, w), F32) for w in _PRE_GAIN_WIDTHS]
        + [jax.ShapeDtypeStruct(shp, BF16) for shp in shard_shapes],
        scratch_shapes=[pltpu.VMEM(w.shape, F32) for w in (win, wqb, wkvb)],
        compiler_params=_params(dimension_semantics=("arbitrary",)),
    )(x, dx_res, *cts, *gains, win, wqb, wkvb, cos, sin)


def _post_math(oa, ob, ga, gb, wo):
    mixed = jnp.concatenate([_rms(jnp.concatenate(oa, axis=-1), ga), _rms(jnp.concatenate(ob, axis=-1), gb)], axis=-1)
    return _mm(mixed, wo), mixed


def _post_fwd(x, oa, ob, ga, gb, wo, name):
    s = x.shape[0]
    tm = _row_tile(s, 512)

    def body(x_ref, oa_ref, ob_ref, ga_ref, gb_ref, wo_ref, y_ref):
        y, _ = _post_math([oa_ref[hd] for hd in range(MLA_HEADS)], [ob_ref[hd] for hd in range(SWA_HEADS)],
                          ga_ref[...], gb_ref[...], wo_ref[...])
        y_ref[...] = x_ref[...] + y

    row = pl.BlockSpec((tm, D_MODEL), lambda i: (i, 0))
    return _pallas_call(
        body, name=name, grid=(s // tm,),
        in_specs=[row, pl.BlockSpec((MLA_HEADS, tm, MLA_V), lambda i: (0, i, 0)),
                  pl.BlockSpec((SWA_HEADS, tm, SWA_HEAD_DIM), lambda i: (0, i, 0)),
                  _full((1, MLA_WIDTH)), _full((1, SWA_WIDTH)), _resident(wo.shape)],
        out_specs=row, out_shape=jax.ShapeDtypeStruct((s, D_MODEL), F32),
        compiler_params=_params(dimension_semantics=("arbitrary",)),
    )(x, oa, ob, ga, gb, wo)


def _post_bwd(dy, oa, ob, ga, gb, wo, name):
    s = dy.shape[0]
    tm = _row_tile(s, 512)

    def body(dy_ref, oa_ref, ob_ref, ga_ref, gb_ref, wo_ref, doa_ref, dob_ref, dga_ref, dgb_ref, dwo_ref, acc_ref):
        i = pl.program_id(0)
        wo_v = wo_ref[...]
        dyv = dy_ref[...]

        def f(oa_l, ob_l, ga_v, gb_v):
            return _post_math(oa_l, ob_l, ga_v, gb_v, wo_v)

        _, vjp, mixed = jax.vjp(f, [oa_ref[hd] for hd in range(MLA_HEADS)], [ob_ref[hd] for hd in range(SWA_HEADS)],
                                ga_ref[...], gb_ref[...], has_aux=True)
        doa, dob, dga, dgb = vjp(dyv)
        for hd in range(MLA_HEADS):
            doa_ref[hd] = doa[hd]
        for hd in range(SWA_HEADS):
            dob_ref[hd] = dob[hd]
        dwo = _dot_tn(mixed.astype(BF16), dyv.astype(BF16))

        @pl.when(i == 0)
        def _():
            dga_ref[...] = dga
            dgb_ref[...] = dgb
            acc_ref[...] = dwo

        @pl.when(i > 0)
        def _():
            dga_ref[...] += dga
            dgb_ref[...] += dgb
            acc_ref[...] += dwo

        @pl.when(i == s // tm - 1)
        def _():
            dwo_ref[...] = acc_ref[...].astype(BF16)

    row = pl.BlockSpec((tm, D_MODEL), lambda i: (i, 0))
    oa_spec = pl.BlockSpec((MLA_HEADS, tm, MLA_V), lambda i: (0, i, 0))
    ob_spec = pl.BlockSpec((SWA_HEADS, tm, SWA_HEAD_DIM), lambda i: (0, i, 0))
    return _pallas_call(
        body, name=name, grid=(s // tm,),
        in_specs=[row, oa_spec, ob_spec, _full((1, MLA_WIDTH)), _full((1, SWA_WIDTH)), _resident(wo.shape)],
        out_specs=[oa_spec, ob_spec, _full((1, MLA_WIDTH)), _full((1, SWA_WIDTH)), _full(wo.shape)],
        out_shape=[jax.ShapeDtypeStruct((MLA_HEADS, s, MLA_V), F32), jax.ShapeDtypeStruct((SWA_HEADS, s, SWA_HEAD_DIM), F32),
                   jax.ShapeDtypeStruct((1, MLA_WIDTH), F32), jax.ShapeDtypeStruct((1, SWA_WIDTH), F32),
                   jax.ShapeDtypeStruct(wo.shape, BF16)],
        scratch_shapes=[pltpu.VMEM(wo.shape, F32)],
        compiler_params=_params(dimension_semantics=("arbitrary",)),
    )(dy, oa, ob, ga, gb, wo)


def _attn_tile(s):
    return 512 if s >= 2048 else 128


def _causal_mask(t):
    return lax.broadcasted_iota(jnp.int32, (t, t), 1) <= lax.broadcasted_iota(jnp.int32, (t, t), 0)


def _pipelined_blocks(first, count, last_block, issue, consume, carry, prefetch_after):
    def clamped(j, slot):
        issue(jnp.minimum(j, last_block), slot)

    def pair(jj, c):
        a = first + 2 * jj
        clamped(a + 1, 1)
        c = consume(a, 0, c)
        clamped(a + 2, 0)
        return consume(a + 1, 1, c)

    clamped(first, 0)
    npairs = count // 2
    carry = lax.fori_loop(0, npairs, pair, carry)

    def odd(c):
        c = consume(first + 2 * npairs, 0, c)
        if prefetch_after:
            clamped(first + count, 0)
        return c

    return lax.cond(count - 2 * npairs == 1, odd, lambda c: c, carry)


def _mla_fwd(q, k, v, name):
    nh, s, _ = q.shape
    t = _attn_tile(s)
    nq = s // t

    def body(q_ref, k_ref, v_ref, o_ref, lse_ref, s0_ref, s1_ref):
        qi = pl.program_id(1)
        qv = q_ref[...]
        s_refs = (s0_ref, s1_ref)

        def rows(j):
            return pl.ds(pl.multiple_of(j * t, t), t)

        def issue(j, slot):
            s_refs[slot][...] = _dot_nt(qv, k_ref[rows(j), :])

        def consume(j, slot, carry, masked=False):
            m, l, acc = carry
            sc = s_refs[slot][...]
            if masked:
                sc = jnp.where(_causal_mask(t), sc, NEG)
            m_new = jnp.maximum(m, jnp.max(sc, axis=-1, keepdims=True))
            alpha = jnp.exp2(m - m_new)
            p = jnp.exp2(sc - m_new)
            l = alpha * l + jnp.sum(p, axis=-1, keepdims=True)
            acc = alpha * acc + jnp.dot(p.astype(BF16), v_ref[rows(j), :], preferred_element_type=F32)
            return m_new, l, acc

        init = (jnp.full((t, 1), NEG, F32), jnp.zeros((t, 1), F32), jnp.zeros((t, MLA_V), F32))
        carry = _pipelined_blocks(0, qi, nq - 1, issue, consume, init, True)
        m, l, acc = consume(qi, 0, carry, masked=True)
        o_ref[...] = acc / l
        lse_ref[...] = m + jnp.log2(l)

    return _pallas_call(
        body, name=name, grid=(nh, nq),
        in_specs=[pl.BlockSpec((None, t, MLA_QK), lambda h, i: (h, i, 0)), pl.BlockSpec((None, s, MLA_QK), lambda h, i: (h, 0, 0)),
                  pl.BlockSpec((None, s, MLA_V), lambda h, i: (h, 0, 0))],
        out_specs=[pl.BlockSpec((None, t, MLA_V), lambda h, i: (h, i, 0)), pl.BlockSpec((None, t, 1), lambda h, i: (h, i, 0))],
        out_shape=[jax.ShapeDtypeStruct((nh, s, MLA_V), F32), jax.ShapeDtypeStruct((nh, s, 1), F32)],
        scratch_shapes=[pltpu.VMEM((t, t), F32)] * 2,
        compiler_params=_params(dimension_semantics=("arbitrary", "arbitrary")),
    )(q, k, v)


def _mla_bwd_dq(q, k, v, o, do, lse, name):
    nh, s, _ = q.shape
    t = _attn_tile(s)
    nq = s // t

    def body(q_ref, k_ref, v_ref, o_ref, do_ref, lse_ref, dq_ref, delta_ref, s0_ref, s1_ref, dp0_ref, dp1_ref):
        qi = pl.program_id(1)
        qv = q_ref[...]
        dov = do_ref[...]
        lse = lse_ref[...]
        delta = jnp.sum(dov * o_ref[...], axis=-1, keepdims=True)
        delta_ref[...] = delta
        dob = dov.astype(BF16)
        s_refs, dp_refs = (s0_ref, s1_ref), (dp0_ref, dp1_ref)

        def rows(j):
            return pl.ds(pl.multiple_of(j * t, t), t)

        def issue(j, slot):
            s_refs[slot][...] = _dot_nt(qv, k_ref[rows(j), :])
            dp_refs[slot][...] = _dot_nt(dob, v_ref[rows(j), :])

        def consume(j, slot, dq, masked=False):
            p = jnp.exp2(s_refs[slot][...] - lse)
            if masked:
                p = jnp.where(_causal_mask(t), p, 0.0)
            ds = p * (dp_refs[slot][...] - delta)
            return dq + jnp.dot(ds.astype(BF16), k_ref[rows(j), :], preferred_element_type=F32)

        dq = _pipelined_blocks(0, qi, nq - 1, issue, consume, jnp.zeros((t, MLA_QK), F32), True)
        dq_ref[...] = consume(qi, 0, dq, masked=True) * MLA_SCALE

    tile = lambda w: pl.BlockSpec((None, t, w), lambda h, i: (h, i, 0))
    whole = lambda w: pl.BlockSpec((None, s, w), lambda h, i: (h, 0, 0))
    return _pallas_call(
        body, name=name, grid=(nh, nq),
        in_specs=[tile(MLA_QK), whole(MLA_QK), whole(MLA_V), tile(MLA_V), tile(MLA_V), tile(1)],
        out_specs=[tile(MLA_QK), tile(1)],
        out_shape=[jax.ShapeDtypeStruct((nh, s, MLA_QK), F32), jax.ShapeDtypeStruct((nh, s, 1), F32)],
        scratch_shapes=[pltpu.VMEM((t, t), F32)] * 4,
        compiler_params=_params(dimension_semantics=("arbitrary", "arbitrary")),
    )(q, k, v, o, do, lse)


def _mla_bwd_dkv(q, k, v, do, lse_row, delta_row, name):
    nh, s, _ = q.shape
    t = _attn_tile(s)
    nq = s // t

    def body(q_ref, k_ref, v_ref, do_ref, lse_ref, delta_ref, dk_ref, dv_ref, s0_ref, s1_ref, dp0_ref, dp1_ref):
        kj = pl.program_id(1)
        kv_, vv = k_ref[...], v_ref[...]
        s_refs, dp_refs = (s0_ref, s1_ref), (dp0_ref, dp1_ref)

        def rows(i):
            return pl.ds(pl.multiple_of(i * t, t), t)

        def issue(i, slot):
            s_refs[slot][...] = _dot_nt(kv_, q_ref[rows(i), :])
            dp_refs[slot][...] = _dot_nt(vv, do_ref[rows(i), :].astype(BF16))

        def consume(i, slot, carry, masked=False):
            dk, dv = carry
            p = jnp.exp2(s_refs[slot][...] - lse_ref[pl.ds(i, 1), :])
            if masked:
                p = jnp.where(lax.broadcasted_iota(jnp.int32, (t, t), 0) <= lax.broadcasted_iota(jnp.int32, (t, t), 1), p, 0.0)
            dv = dv + jnp.dot(p.astype(BF16), do_ref[rows(i), :].astype(BF16), preferred_element_type=F32)
            ds = p * (dp_refs[slot][...] - delta_ref[pl.ds(i, 1), :])
            dk = dk + jnp.dot(ds.astype(BF16), q_ref[rows(i), :], preferred_element_type=F32)
            return dk, dv

        issue(kj, 0)
        carry = consume(kj, 0, (jnp.zeros((t, MLA_QK), F32), jnp.zeros((t, MLA_V), F32)), masked=True)
        dk, dv = _pipelined_blocks(kj + 1, nq - 1 - kj, nq - 1, issue, consume, carry, False)
        dk_ref[...] = dk * LN2
        dv_ref[...] = dv

    tile = lambda w: pl.BlockSpec((None, t, w), lambda h, j: (h, j, 0))
    whole = lambda w: pl.BlockSpec((None, s, w), lambda h, j: (h, 0, 0))
    rows_spec = pl.BlockSpec((None, nq, t), lambda h, j: (h, 0, 0))
    return _pallas_call(
        body, name=name, grid=(nh, nq),
        in_specs=[whole(MLA_QK), tile(MLA_QK), tile(MLA_V), whole(MLA_V), rows_spec, rows_spec],
        out_specs=[tile(MLA_QK), tile(MLA_V)],
        out_shape=[jax.ShapeDtypeStruct((nh, s, MLA_QK), F32), jax.ShapeDtypeStruct((nh, s, MLA_V), F32)],
        scratch_shapes=[pltpu.VMEM((t, t), F32)] * 4,
        compiler_params=_params(dimension_semantics=("arbitrary", "arbitrary")),
    )(q, k, v, do, lse_row, delta_row)


def _swa_tile(s):
    return min(s, 4 * BLOCK)


def _swa_specs(tq):
    nb = tq // BLOCK
    grp = lambda w: pl.BlockSpec((SWA_GROUP, tq, w), lambda j, i: (j, i, 0))
    main = pl.BlockSpec((None, tq, SWA_HEAD_DIM), lambda j, i: (j, i, 0))
    tail = pl.BlockSpec((None, BLOCK, SWA_HEAD_DIM), lambda j, i: (j, nb * (i + 1), 0))
    sink = pl.BlockSpec((None, SWA_GROUP, 128), lambda j, i: (j, 0, 0))
    return grp, main, tail, sink


def _swa_band_mask(first):
    shape = (SWA_GROUP * BLOCK, 2 * BLOCK)
    q_rel = (lax.broadcasted_iota(jnp.int32, shape, 0) & (BLOCK - 1)) + BLOCK
    k_rel = lax.broadcasted_iota(jnp.int32, shape, 1)
    dist = q_rel - k_rel
    return (dist >= 0) & (dist < BLOCK) & ((k_rel >= BLOCK) | jnp.logical_not(first))


def _swa_sink_column(sink_ref):
    sk = sink_ref[...]
    return jnp.concatenate([jnp.broadcast_to(sk[g:g + 1, 0:1], (BLOCK, 1)) for g in range(SWA_GROUP)], axis=0)


def _swa_fwd(q, kpad, vpad, sinks, name):
    _, s, _ = q.shape
    tq = _swa_tile(s)
    grp, main, tail, sink = _swa_specs(tq)
    d = SWA_HEAD_DIM

    def body(q_ref, km_ref, kt_ref, vm_ref, vt_ref, sink_ref, o_ref, lse_ref):
        i = pl.program_id(1)
        kall = jnp.concatenate([km_ref[...], kt_ref[...]], axis=0)
        vall = jnp.concatenate([vm_ref[...], vt_ref[...]], axis=0)
        sink_col = _swa_sink_column(sink_ref)
        for b in range(tq // BLOCK):
            lo = b * BLOCK
            valid = _swa_band_mask(i == 0 if b == 0 else False)
            q4 = q_ref[:, lo:lo + BLOCK, :].reshape(SWA_GROUP * BLOCK, d)
            sc = jnp.where(valid, _dot_nt(q4, kall[lo:lo + 2 * BLOCK]) * SWA_SCALE, NEG)
            m = jnp.maximum(jnp.max(sc, axis=-1, keepdims=True), sink_col)
            e = jnp.exp(sc - m)
            den = jnp.sum(e, axis=-1, keepdims=True) + jnp.exp(sink_col - m)
            out = jnp.dot((e * (1.0 / den)).astype(BF16), vall[lo:lo + 2 * BLOCK], preferred_element_type=F32)
            o_ref[:, lo:lo + BLOCK, :] = out.reshape(SWA_GROUP, BLOCK, d)
            lse_ref[:, lo:lo + BLOCK, :] = (m + jnp.log(den)).reshape(SWA_GROUP, BLOCK, 1)

    return _pallas_call(
        body, name=name, grid=(SWA_KV_HEADS, s // tq),
        in_specs=[grp(d), main, tail, main, tail, sink], out_specs=[grp(d), grp(1)],
        out_shape=[jax.ShapeDtypeStruct((SWA_HEADS, s, d), F32), jax.ShapeDtypeStruct((SWA_HEADS, s, 1), F32)],
        compiler_params=_params(dimension_semantics=("arbitrary", "arbitrary")),
    )(q, kpad, kpad, vpad, vpad, sinks)


def _swa_bwd(q, kpad, vpad, sinks, o, lse, do, name):
    _, s, _ = q.shape
    tq = _swa_tile(s)
    grp, main, tail, sink = _swa_specs(tq)
    d = SWA_HEAD_DIM

    def body(q_ref, km_ref, kt_ref, vm_ref, vt_ref, sink_ref, o_ref, lse_ref, do_ref, dq_ref, dk_ref, dv_ref, dsink_ref):
        i = pl.program_id(1)
        kall = jnp.concatenate([km_ref[...], kt_ref[...]], axis=0)
        vall = jnp.concatenate([vm_ref[...], vt_ref[...]], axis=0)
        sink_col = _swa_sink_column(sink_ref)

        @pl.when(i == 0)
        def _():
            dk_ref[...] = jnp.zeros_like(dk_ref)
            dv_ref[...] = jnp.zeros_like(dv_ref)
            dsink_ref[...] = jnp.zeros_like(dsink_ref)

        dsink = jnp.zeros((SWA_GROUP * BLOCK, 1), F32)
        for b in range(tq // BLOCK):
            lo = b * BLOCK
            valid = _swa_band_mask(i == 0 if b == 0 else False)
            rows4 = SWA_GROUP * BLOCK
            q4 = q_ref[:, lo:lo + BLOCK, :].reshape(rows4, d)
            do4 = do_ref[:, lo:lo + BLOCK, :].reshape(rows4, d)
            lse4 = lse_ref[:, lo:lo + BLOCK, :].reshape(rows4, 1)
            delta = jnp.sum(do4 * o_ref[:, lo:lo + BLOCK, :].reshape(rows4, d), axis=-1, keepdims=True)
            kb, vb = kall[lo:lo + 2 * BLOCK], vall[lo:lo + 2 * BLOCK]
            do4b = do4.astype(BF16)
            p = jnp.where(valid, jnp.exp(_dot_nt(q4, kb) * SWA_SCALE - lse4), 0.0)
            ds = (p * (_dot_nt(do4b, vb) - delta) * SWA_SCALE).astype(BF16)
            dq_ref[:, lo:lo + BLOCK, :] = jnp.dot(ds, kb, preferred_element_type=F32).reshape(SWA_GROUP, BLOCK, d)
            band = pl.ds(pl.multiple_of(i * tq, BLOCK) + lo, 2 * BLOCK)
            dk_ref[band, :] += _dot_tn(ds, q4)
            dv_ref[band, :] += _dot_tn(p.astype(BF16), do4b)
            dsink = dsink - jnp.exp(sink_col - lse4) * delta
        per_head = [jnp.broadcast_to(jnp.sum(dsink[g * BLOCK:(g + 1) * BLOCK], axis=0, keepdims=True), (1, 128))
                    for g in range(SWA_GROUP)]
        dsink_ref[...] += jnp.concatenate(per_head + [jnp.zeros((8 - SWA_GROUP, 128), F32)], axis=0)

    acc = pl.BlockSpec((None, s + BLOCK, d), lambda j, i: (j, 0, 0))
    return _pallas_call(
        body, name=name, grid=(SWA_KV_HEADS, s // tq),
        in_specs=[grp(d), main, tail, main, tail, sink, grp(d), grp(1), grp(d)],
        out_specs=[grp(d), acc, acc, pl.BlockSpec((None, 8, 128), lambda j, i: (j, 0, 0))],
        out_shape=[jax.ShapeDtypeStruct((SWA_HEADS, s, d), F32),
                   jax.ShapeDtypeStruct((SWA_KV_HEADS, s + BLOCK, d), F32),
                   jax.ShapeDtypeStruct((SWA_KV_HEADS, s + BLOCK, d), F32),
                   jax.ShapeDtypeStruct((SWA_KV_HEADS, 8, 128), F32)],
        compiler_params=_params(dimension_semantics=("arbitrary", "arbitrary")),
    )(q, kpad, kpad, vpad, vpad, sinks, o, lse, do)


def _loss_head(y, target, name):
    s = y.shape[0]
    tm = _row_tile(s, 512)

    def body(y_ref, t_ref, dy_ref, loss_ref):
        i = pl.program_id(0)
        err = y_ref[...] - t_ref[...]
        dy_ref[...] = err * (1.0 / D_MODEL)
        part = jnp.broadcast_to(0.5 * jnp.sum(jnp.mean(err * err, axis=-1, keepdims=True), axis=0, keepdims=True), (1, 128))

        @pl.when(i == 0)
        def _():
            loss_ref[...] = part

        @pl.when(i > 0)
        def _():
            loss_ref[...] += part

    row = pl.BlockSpec((tm, D_MODEL), lambda i: (i, 0))
    return _pallas_call(
        body, name=name, grid=(s // tm,), in_specs=[row, row], out_specs=[row, _full((1, 128))],
        out_shape=[jax.ShapeDtypeStruct((s, D_MODEL), F32), jax.ShapeDtypeStruct((1, 128), F32)],
        compiler_params=_params(dimension_semantics=("arbitrary",)),
    )(y, target)


def _adamw(w, g, m, v, name):
    rows, cols = w.shape
    tr = rows
    for cand in (512, 256, 128, 64, 32, 16, 8):
        if rows % cand == 0 and rows > cand:
            tr = cand
            break

    def body(w_ref, g_ref, m_ref, v_ref, d_ref, nm_ref, nv_ref):
        gv = g_ref[...]
        nm = ADAM_B1 * m_ref[...] + (1.0 - ADAM_B1) * gv
        nv = ADAM_B2 * v_ref[...] + (1.0 - ADAM_B2) * (gv * gv)
        m_hat = nm / (1.0 - ADAM_B1 ** ADAM_STEP)
        v_hat = nv / (1.0 - ADAM_B2 ** ADAM_STEP)
        d_ref[...] = -ADAM_LR * (m_hat / (jnp.sqrt(v_hat) + ADAM_EPS) + ADAM_WD * w_ref[...])
        nm_ref[...] = nm
        nv_ref[...] = nv

    blk = pl.BlockSpec((tr, cols), lambda i: (i, 0))
    return _pallas_call(
        body, name=name, grid=(rows // tr,), in_specs=[blk] * 4, out_specs=[blk] * 3,
        out_shape=[jax.ShapeDtypeStruct((rows, cols), F32)] * 3,
        compiler_params=_params(dimension_semantics=("arbitrary",)),
    )(w, g, m, v)


def _position():
    return lax.axis_index("x"), lax.axis_index("y"), lax.axis_index("c")


def _remote(src, dst, send_sems, recv_sems, k, to):
    return pltpu.make_async_remote_copy(src_ref=src, dst_ref=dst, send_sem=send_sems.at[k], recv_sem=recv_sems.at[k],
                                        device_id=to, device_id_type=MESH)


_HBM = pl.BlockSpec(memory_space=pltpu.HBM)


def _all_gather_layers(mine, name):
    na = len(mine)

    def body(*refs):
        ins, outs, (send_sems, recv_sems) = refs[:na], refs[na:2 * na], refs[2 * na:]
        x, y, c = _position()
        me, sibling = (x, y, c), (x, y, 1 - c)
        xn, yn, dg = (1 - x, y), (x, 1 - y), (1 - x, 1 - y)

        def slot(a, chip, pc, half=None):
            ref = outs[a].at[4 * chip[0] + 2 * chip[1] + pc]
            if half is None:
                return ref
            rows = ref.shape[0] // 2
            return ref.at[pl.ds(half * rows, rows)]

        def cp(a, k, chip, pc, half, to, src=None):
            dst = slot(a, chip, pc, half)
            return _remote(dst if src is None else src, dst, send_sems, recv_sems, 8 * a + k, to)

        sent = []
        for a in range(na):
            sent += [cp(a, 0, (x, y), c, None, (*xn, c), src=ins[a].at[c]), cp(a, 1, (x, y), c, None, (*yn, c), src=ins[a].at[c])]
        for s_ in sent:
            s_.start()
        later = []
        for k, frm, fwd_k, fwd_half, fwd_to in ((0, xn, 2, 0, yn), (1, yn, 3, 1, xn)):
            for a in range(na):
                cp(a, k, frm, c, None, me).wait_recv()
                later += [cp(a, fwd_k, frm, c, fwd_half, (*fwd_to, c)), cp(a, 4 + k, frm, c, None, sibling)]
                later[-2].start()
                later[-1].start()
        for k, half in ((2, 0), (3, 1)):
            for a in range(na):
                cp(a, k, dg, c, half, me).wait_recv()
                later.append(cp(a, 4 + k, dg, c, half, sibling))
                later[-1].start()
        for a in range(na):
            for k, chip, half in ((4, xn, None), (5, yn, None), (6, dg, 0), (7, dg, 1)):
                cp(a, k, chip, 1 - c, half, me).wait_recv()
        for s_ in sent + later:
            s_.wait_send()

    return _pallas_call(
        body, name=name, in_specs=[_HBM] * na, out_specs=[_HBM] * na,
        out_shape=[jax.ShapeDtypeStruct((N_DEV,) + m.shape[1:], m.dtype) for m in mine],
        scratch_shapes=[pltpu.SemaphoreType.DMA((8 * na,)), pltpu.SemaphoreType.DMA((8 * na,))],
    )(*mine)


def _sibling_exchange(parts, name, other_layer):
    na = len(parts)

    def body(*refs):
        ins, outs, (send_sems, recv_sems) = refs[:na], refs[na:2 * na], refs[2 * na:]
        x, y, c = _position()
        copies = [_remote(ins[a].at[1 - c] if other_layer else ins[a], outs[a], send_sems, recv_sems, a, (x, y, 1 - c))
                  for a in range(na)]
        for cp in copies:
            cp.start()
        for cp in copies:
            cp.wait()

    return _pallas_call(
        body, name=name, in_specs=[_HBM] * na, out_specs=[_HBM] * na,
        out_shape=[jax.ShapeDtypeStruct(p.shape[1:] if other_layer else p.shape, p.dtype) for p in parts],
        scratch_shapes=[pltpu.SemaphoreType.DMA((na,)), pltpu.SemaphoreType.DMA((na,))],
    )(*parts)


def _scatter_to_chips(parts, name):
    na = len(parts)

    def body(*refs):
        ins, outs, (send_sems, recv_sems) = refs[:na], refs[na:2 * na], refs[2 * na:]
        x, y, c = _position()
        chips = [(1 - x, y), (x, 1 - y), (1 - x, 1 - y)]
        copies = [_remote(ins[a].at[2 * px + py], outs[a].at[j], send_sems, recv_sems, 3 * a + j, (px, py, c))
                  for a in range(na) for j, (px, py) in enumerate(chips)]
        for cp in copies:
            cp.start()
        for cp in copies:
            cp.wait()

    return _pallas_call(
        body, name=name, in_specs=[_HBM] * na, out_specs=[_HBM] * na,
        out_shape=[jax.ShapeDtypeStruct((3,) + p.shape[1:], p.dtype) for p in parts],
        scratch_shapes=[pltpu.SemaphoreType.DMA((3 * na,)), pltpu.SemaphoreType.DMA((3 * na,))],
    )(*parts)


def _assemble(g4, mine, name, side_by_side):
    _, nl, r, w = g4.shape
    tr = next(cand for cand in (256, 128) if r % cand == 0)

    def body(in_ref, mine_ref, out_ref):
        chip = 2 * lax.axis_index("x") + lax.axis_index("y")
        blocks = [jnp.where(chip == sh, mine_ref[...], in_ref[sh]) for sh in range(N_SHARD)]
        if side_by_side:
            out_ref[...] = jnp.concatenate(blocks, axis=-1)
        else:
            for sh in range(N_SHARD):
                out_ref[sh] = blocks[sh]

    if side_by_side:
        out_spec = pl.BlockSpec((None, tr, N_SHARD * w), lambda l, i: (l, i, 0))
        out_shape = jax.ShapeDtypeStruct((nl, r, N_SHARD * w), g4.dtype)
    else:
        out_spec = pl.BlockSpec((None, N_SHARD, tr, w), lambda l, i: (l, 0, i, 0))
        out_shape = jax.ShapeDtypeStruct((nl, N_SHARD, r, w), g4.dtype)
    return _pallas_call(
        body, name=name, grid=(nl, r // tr),
        in_specs=[pl.BlockSpec((N_SHARD, None, tr, w), lambda l, i: (0, l, i, 0)),
                  pl.BlockSpec((None, tr, w), lambda l, i: (l, i, 0))],
        out_specs=out_spec, out_shape=out_shape,
        compiler_params=_params(dimension_semantics=("arbitrary", "arbitrary")),
    )(g4, mine)


def _all_reduce_small(vec, name):
    r, l = vec.shape

    def body(v_ref, out_ref, gath_ref, send_sems, recv_sems):
        x, y, c = _position()
        me = 4 * x + 2 * y + c
        gath_ref[me] = v_ref[...]
        copies = []
        for k in range(1, N_DEV):
            to = (x ^ (k >> 2), y ^ ((k >> 1) & 1), c ^ (k & 1))
            copies.append(_remote(gath_ref.at[me], gath_ref.at[me], send_sems, recv_sems, k - 1, to))
        for cp in copies:
            cp.start()
        for k in range(1, N_DEV):
            frm = 4 * (x ^ (k >> 2)) + 2 * (y ^ ((k >> 1) & 1)) + (c ^ (k & 1))
            _remote(gath_ref.at[frm], gath_ref.at[frm], send_sems, recv_sems, k - 1, (x, y, c)).wait_recv()
        for cp in copies:
            cp.wait_send()
        total = gath_ref[0]
        for d in range(1, N_DEV):
            total = total + gath_ref[d]
        out_ref[...] = total

    vm = pl.BlockSpec(memory_space=pltpu.VMEM)
    return _pallas_call(
        body, name=name, in_specs=[vm], out_specs=vm, out_shape=jax.ShapeDtypeStruct((r, l), F32),
        scratch_shapes=[pltpu.VMEM((N_DEV, r, l), F32), pltpu.SemaphoreType.DMA((N_DEV - 1,)),
                        pltpu.SemaphoreType.DMA((N_DEV - 1,))],
    )(vec)


def _sum_blocks(blocks, out_dtype, name):
    m, w = blocks[0].shape
    tr = next(cand for cand in (512, 256, 128) if m % cand == 0)

    def body(*refs):
        total = refs[0][...].astype(F32)
        for ref in refs[1:-1]:
            total = total + ref[...].astype(F32)
        refs[-1][...] = total.astype(out_dtype)

    blk = pl.BlockSpec((tr, w), lambda i: (i, 0))
    return _pallas_call(
        body, name=name, grid=(m // tr,), in_specs=[blk] * len(blocks), out_specs=blk,
        out_shape=jax.ShapeDtypeStruct((m, w), out_dtype),
        compiler_params=_params(dimension_semantics=("arbitrary",)),
    )(*blocks)


GROUPS = (('ffn1_w_gate', 'ffn1_w_up', 'ffn2_w_gate', 'ffn2_w_up'), ('ffn1_w_down', 'ffn2_w_down', 'w_o'),
          ('w_in',), ('mla_w_q_b',), ('mla_w_kv_b',))


def _shard_rows(name):
    shape, axis = BIG[name]
    return shape[0] // N_SHARD if axis == 0 else shape[0]


def _group_row_offsets(group):
    return [int(v) for v in np.cumsum([0] + [_shard_rows(n) for n in group])]


def _rope_tables(s):
    pos = jnp.arange(s, dtype=F32)
    inv = 1.0 / (ROPE_THETA ** (jnp.arange(0, MLA_ROPE, 2, dtype=F32) / MLA_ROPE))
    ang = pos[:, None] * inv[None, :]
    return jnp.tile(jnp.cos(ang), (1, SWA_HEADS)), jnp.tile(jnp.sin(ang), (1, SWA_HEADS))


def _local_step(x, target, small, big):
    s = x.shape[0]
    cos, sin = _rope_tables(s)
    row = lambda name, l: small[name][l][None, :]
    saved = []
    for l in range(DEPTH):
        sv = {'x0': x}
        x, sv['g1'], sv['u1'] = _ffn_fwd(x, row('ffn1_norm', l), big['ffn1_w_gate'][l], big['ffn1_w_up'][l],
                                        big['ffn1_w_down'][l], f"ffn1_fwd_{l}")
        sv['x1'] = x
        gains = [row(n, l) for n in ('mix_norm', 'mla_q_a_norm', 'mla_kv_a_norm', 'mla_q_norm', 'mla_k_norm',
                                     'swa_q_norm', 'swa_k_norm')]
        mixer_w = (_take_cols(big['w_in'][l], _IN_ORDER), _take_cols(big['mla_w_q_b'][l], _QB_ORDER),
                   _take_cols(big['mla_w_kv_b'][l], _KVB_ORDER))
        q_a, k_a, v_a, q_b, k_b, v_b = _pre_fwd(x, gains, *mixer_w, cos, sin, f"pre_fwd_{l}")
        o_a, lse = _mla_fwd(q_a, k_a, v_a, f"mla_fwd_{l}")
        kpad = jnp.pad(k_b, ((0, 0), (BLOCK, 0), (0, 0)))
        vpad = jnp.pad(v_b, ((0, 0), (BLOCK, 0), (0, 0)))
        sinks = jnp.broadcast_to(small['swa_sinks'][l].reshape(SWA_KV_HEADS, SWA_GROUP, 1), (SWA_KV_HEADS, SWA_GROUP, 128))
        o_b, lse_b = _swa_fwd(q_b, kpad, vpad, sinks, f"swa_fwd_{l}")
        sv.update(gains=gains, mixer_w=mixer_w, q_a=q_a, k_a=k_a, v_a=v_a, q_b=q_b, kpad=kpad, vpad=vpad, sinks=sinks, o_a=o_a, lse=lse, o_b=o_b,
                  lse_b=lse_b)
        x = _post_fwd(x, o_a, o_b, row('mla_out_norm', l), row('swa_out_norm', l), big['w_o'][l], f"post_fwd_{l}")
        sv['x2'] = x
        x, sv['g2'], sv['u2'] = _ffn_fwd(x, row('ffn2_norm', l), big['ffn2_w_gate'][l], big['ffn2_w_up'][l],
                                        big['ffn2_w_down'][l], f"ffn2_fwd_{l}")
        saved.append(sv)

    dx, loss = _loss_head(x, target, "loss_head")

    gs = {n: [None] * DEPTH for n in SMALL_NAMES}
    gb = {n: [None] * DEPTH for n in BIG_NAMES}
    t = _attn_tile(s)
    for l in reversed(range(DEPTH)):
        sv = saved[l]

        def ffn_back(tag, xin, dy, gate, up):
            dxi, dgain, nb, act, dgate, dup = _ffn_bwd(xin, dy, gate, up, row(tag + '_norm', l), big[tag + '_w_gate'][l],
                                                      big[tag + '_w_up'][l], big[tag + '_w_down'][l], f"{tag}_bwd_{l}")
            gs[tag + '_norm'][l] = dgain[0]
            gb[tag + '_w_gate'][l] = _matmul_tn(nb, dgate, 1.0, f"{tag}_dwg_{l}", True)
            gb[tag + '_w_up'][l] = _matmul_tn(nb, dup, 1.0, f"{tag}_dwu_{l}", True)
            gb[tag + '_w_down'][l] = _matmul_tn(act, dy, 0.5, f"{tag}_dwd_{l}", False).reshape(N_SHARD, D_FF // N_SHARD, D_MODEL)
            return dxi

        dx = ffn_back('ffn2', sv['x2'], dx, sv['g2'], sv['u2'])
        do_a, do_b, dga, dgb, dwo = _post_bwd(dx, sv['o_a'], sv['o_b'], row('mla_out_norm', l), row('swa_out_norm', l),
                                              big['w_o'][l], f"post_bwd_{l}")
        gs['mla_out_norm'][l], gs['swa_out_norm'][l] = dga[0], dgb[0]
        gb['w_o'][l] = dwo.reshape(N_SHARD, MIX_WIDTH // N_SHARD, D_MODEL)
        dq_a, delta = _mla_bwd_dq(sv['q_a'], sv['k_a'], sv['v_a'], sv['o_a'], do_a, sv['lse'], f"mla_dq_{l}")
        dk_a, dv_a = _mla_bwd_dkv(sv['q_a'], sv['k_a'], sv['v_a'], do_a, sv['lse'].reshape(MLA_HEADS, s // t, t),
                                  delta.reshape(MLA_HEADS, s // t, t), f"mla_dkv_{l}")
        dq_b, dkpad, dvpad, dsink = _swa_bwd(sv['q_b'], sv['kpad'], sv['vpad'], sv['sinks'], sv['o_b'], sv['lse_b'], do_b,
                                             f"swa_bwd_{l}")
        gs['swa_sinks'][l] = dsink[:, :SWA_GROUP, 0].reshape(SWA_HEADS)
        cts = [dq_a, dk_a, dv_a, dq_b, dkpad[:, BLOCK:], dvpad[:, BLOCK:]]
        outs = _pre_bwd(sv['x1'], dx, cts, sv['gains'], *sv['mixer_w'], cos, sin, f"pre_bwd_{l}")
        dx = outs[0]
        for n, val in zip(('mix_norm', 'mla_q_a_norm', 'mla_kv_a_norm', 'mla_q_norm', 'mla_k_norm', 'swa_q_norm', 'swa_k_norm'),
                          outs[1:8]):
            gs[n][l] = val[0]
        gb['w_in'][l], gb['mla_w_q_b'][l], gb['mla_w_kv_b'][l] = outs[8:11]
        dx = ffn_back('ffn1', sv['x0'], dx, sv['g1'], sv['u1'])
    return loss, dx, gs, gb


def kernel(x, ffn1_norm, ffn1_w_gate, ffn1_w_up, ffn1_w_down, mix_norm, w_in, mla_q_a_norm, mla_w_q_b, mla_kv_a_norm, mla_w_kv_b, mla_q_norm, mla_k_norm, swa_q_norm, swa_k_norm, swa_sinks, mla_out_norm, swa_out_norm, w_o, ffn2_norm, ffn2_w_gate, ffn2_w_up, ffn2_w_down, loss_target, m_ffn1_norm, m_ffn1_w_gate, m_ffn1_w_up, m_ffn1_w_down, m_mix_norm, m_w_in, m_mla_q_a_norm, m_mla_w_q_b, m_mla_kv_a_norm, m_mla_w_kv_b, m_mla_q_norm, m_mla_k_norm, m_swa_q_norm, m_swa_k_norm, m_swa_sinks, m_mla_out_norm, m_swa_out_norm, m_w_o, m_ffn2_norm, m_ffn2_w_gate, m_ffn2_w_up, m_ffn2_w_down, v_ffn1_norm, v_ffn1_w_gate, v_ffn1_w_up, v_ffn1_w_down, v_mix_norm, v_w_in, v_mla_q_a_norm, v_mla_w_q_b, v_mla_kv_a_norm, v_mla_w_kv_b, v_mla_q_norm, v_mla_k_norm, v_swa_q_norm, v_swa_k_norm, v_swa_sinks, v_mla_out_norm, v_swa_out_norm, v_w_o, v_ffn2_norm, v_ffn2_w_gate, v_ffn2_w_up, v_ffn2_w_down):
    args = dict(locals())
    weights = {n: args[n] for n in WEIGHT_NAMES}
    mom_m = {n: args["m_" + n] for n in WEIGHT_NAMES}
    mom_v = {n: args["v_" + n] for n in WEIGHT_NAMES}

    mine = [jnp.concatenate([weights[n].astype(BF16) for n in group], axis=1) for group in GROUPS]
    c = lax.axis_index("c")
    chip = 2 * lax.axis_index("x") + lax.axis_index("y")
    gathered = [g.reshape((N_SHARD, DEPTH) + g.shape[1:]) for g in _all_gather_layers(mine, "gather_weights")]
    big = {}
    for gi, group in enumerate(GROUPS):
        offs = _group_row_offsets(group)
        col_sharded = BIG[group[0]][1] == 1
        full = _assemble(gathered[gi], mine[gi], f"assemble_{gi}", col_sharded)
        for i, n in enumerate(group):
            if col_sharded:
                big[n] = [full[l, offs[i]:offs[i + 1]] for l in range(DEPTH)]
            else:
                big[n] = [full[l, :, offs[i]:offs[i + 1]].reshape(BIG[n][0]) for l in range(DEPTH)]
    small = {n: weights[n] for n in SMALL_NAMES}

    loss, dx, gs, gb = _local_step(x[0], loss_target[0], small, big)

    small_flat = jnp.concatenate([jnp.stack(gs[n]).reshape(-1) for n in SMALL_NAMES] + [loss[0, :1]])
    n_small = small_flat.shape[0]
    lanes = -(-n_small // (8 * 128)) * 128
    small_sum = _all_reduce_small(jnp.pad(small_flat, (0, 8 * lanes - n_small)).reshape(8, lanes), "reduce_small").reshape(-1)
    grads = {}
    off = 0
    for n in SMALL_NAMES:
        cnt = int(np.prod(weights[n].shape))
        grads[n] = small_sum[off:off + cnt].reshape(weights[n].shape)
        off += cnt
    loss_out = small_sum[off]

    parts = [jnp.stack([jnp.concatenate([gb[n][l] for n in group], axis=1) for l in range(DEPTH)]) for group in GROUPS]
    from_sibling = _sibling_exchange(parts, "swap_layers", True)
    chip_sums = []
    for gi, (p, got) in enumerate(zip(parts, from_sibling)):
        kept = lax.dynamic_index_in_dim(p, c, axis=0, keepdims=False)
        rows = N_SHARD * p.shape[2]
        pair = _sum_blocks([kept.reshape(rows, -1), got.reshape(rows, -1)], BF16, f"sum_pair_{gi}")
        chip_sums.append(pair.reshape(p.shape[1:]))
    from_chips = _scatter_to_chips(chip_sums, "scatter_chips")
    halves = []
    for gi, (cs, got) in enumerate(zip(chip_sums, from_chips)):
        own = lax.dynamic_index_in_dim(cs, chip, axis=0, keepdims=False)
        halves.append(_sum_blocks([own, got[0], got[1], got[2]], F32, f"sum_chips_{gi}"))
    others = _sibling_exchange(halves, "share_layers", False)
    for gi, group in enumerate(GROUPS):
        offs = _group_row_offsets(group)
        both = jnp.stack([jnp.where(c == 0, halves[gi], others[gi]), jnp.where(c == 0, others[gi], halves[gi])])
        for i, n in enumerate(group):
            grads[n] = both[:, offs[i]:offs[i + 1]]

    deltas, new_m, new_v = {}, {}, {}
    for n in WEIGHT_NAMES:
        shp = weights[n].shape
        two_d = (DEPTH, shp[-1]) if len(shp) == 2 else (shp[0] * shp[1], shp[2])
        d, nm, nv = _adamw(weights[n].reshape(two_d), grads[n].reshape(two_d), mom_m[n].reshape(two_d),
                           mom_v[n].reshape(two_d), f"adamw_{n}")
        deltas[n], new_m[n], new_v[n] = d.reshape(shp), nm.reshape(shp), nv.reshape(shp)

    return (loss_out, dx[None], *[grads[n] for n in WEIGHT_NAMES], *[deltas[n] for n in WEIGHT_NAMES],
            *[new_m[n] for n in WEIGHT_NAMES], *[new_v[n] for n in WEIGHT_NAMES])
```

```python
import functools

import numpy as np
import jax
import jax.numpy as jnp
from jax import lax
from jax.experimental import pallas as pl
from jax.experimental.pallas import tpu as pltpu

F32 = jnp.float32
BF16 = jnp.bfloat16

D_MODEL = 1024
DEPTH = 2
EPS = 1e-6
ROPE_THETA = 10000.0
BLOCK = 128
MLA_HEADS = 4
MLA_Q_RANK = 256
MLA_KV_RANK = 128
MLA_NOPE = 128
MLA_ROPE = 64
MLA_V = 128
MLA_QK = MLA_NOPE + MLA_ROPE
MLA_WIDTH = MLA_HEADS * MLA_V
SWA_HEADS = 8
SWA_KV_HEADS = 2
SWA_GROUP = SWA_HEADS // SWA_KV_HEADS
SWA_HEAD_DIM = 64
SWA_WIDTH = SWA_HEADS * SWA_HEAD_DIM
MIX_WIDTH = MLA_WIDTH + SWA_WIDTH
IN_SPLITS = (MLA_Q_RANK, MLA_KV_RANK, MLA_ROPE, SWA_WIDTH, SWA_KV_HEADS * SWA_HEAD_DIM, SWA_KV_HEADS * SWA_HEAD_DIM)
IN_COLS = sum(IN_SPLITS)
IN_OFFS = tuple(int(v) for v in np.cumsum((0,) + IN_SPLITS))
D_FF = 2816
MLA_SCALE = MLA_QK ** -0.5
LOG2E = 1.4426950408889634
LN2 = 0.6931471805599453
MLA_QSCALE = MLA_SCALE * LOG2E
SWA_SCALE = SWA_HEAD_DIM ** -0.5
NEG = -1e30

ADAM_LR = 0.001
ADAM_B1 = 0.9
ADAM_B2 = 0.999
ADAM_EPS = 1e-08
ADAM_WD = 0.01
ADAM_STEP = 10

N_SHARD = 4
N_DEV = 8
VMEM_LIMIT = 56 * 1024 * 1024
MESH = pl.DeviceIdType.MESH

WEIGHT_NAMES = ['ffn1_norm', 'ffn1_w_gate', 'ffn1_w_up', 'ffn1_w_down', 'mix_norm', 'w_in', 'mla_q_a_norm', 'mla_w_q_b',
                'mla_kv_a_norm', 'mla_w_kv_b', 'mla_q_norm', 'mla_k_norm', 'swa_q_norm', 'swa_k_norm', 'swa_sinks',
                'mla_out_norm', 'swa_out_norm', 'w_o', 'ffn2_norm', 'ffn2_w_gate', 'ffn2_w_up', 'ffn2_w_down']
BIG = {'ffn1_w_gate': ((D_MODEL, D_FF), 1), 'ffn1_w_up': ((D_MODEL, D_FF), 1), 'ffn1_w_down': ((D_FF, D_MODEL), 0),
       'w_in': ((D_MODEL, IN_COLS), 1), 'mla_w_q_b': ((MLA_Q_RANK, MLA_HEADS * MLA_QK), 1),
       'mla_w_kv_b': ((MLA_KV_RANK, MLA_HEADS * (MLA_NOPE + MLA_V)), 1), 'w_o': ((MIX_WIDTH, D_MODEL), 0),
       'ffn2_w_gate': ((D_MODEL, D_FF), 1), 'ffn2_w_up': ((D_MODEL, D_FF), 1), 'ffn2_w_down': ((D_FF, D_MODEL), 0)}
BIG_NAMES = [n for n in WEIGHT_NAMES if n in BIG]
SMALL_NAMES = [n for n in WEIGHT_NAMES if n not in BIG]

_pallas_call = pl.pallas_call


def _params(**kw):
    return pltpu.CompilerParams(vmem_limit_bytes=VMEM_LIMIT, **kw)


def _full(shape):
    n = len(shape)
    return pl.BlockSpec(shape, lambda *_: (0,) * n)


def _resident(shape):
    n = len(shape)
    return pl.BlockSpec(shape, lambda *_: (0,) * n, pipeline_mode=pl.Buffered(1))


@jax.custom_vjp
def _mm(a, w):
    return jnp.dot(a.astype(BF16), w, preferred_element_type=F32)


def _mm_fwd(a, w):
    return _mm(a, w), w


def _mm_bwd(w, dy):
    return lax.dot_general(dy.astype(BF16), w, (((1,), (1,)), ((), ())), preferred_element_type=F32), None


_mm.defvjp(_mm_fwd, _mm_bwd)


def _dot_nt(a, b):
    return lax.dot_general(a, b, (((1,), (1,)), ((), ())), preferred_element_type=F32)


def _dot_tn(a, b):
    return lax.dot_general(a, b, (((0,), (0,)), ((), ())), preferred_element_type=F32)


def _rms(t, g):
    return t * lax.rsqrt(jnp.mean(t * t, axis=-1, keepdims=True) + EPS) * g


def _sigmoid(z):
    return 1.0 / (1.0 + jnp.exp(-z))


def _row_tile(s, want):
    return min(want, s)


FF_CHUNK = 1408


def _ffn_fwd(x, g, wg, wu, wd, name):
    s = x.shape[0]
    tm = _row_tile(s, 256)

    def body(x_ref, g_ref, wg_ref, wu_ref, wd_ref, y_ref, gate_ref, up_ref):
        xv = x_ref[...]
        nb = _rms(xv, g_ref[...]).astype(BF16)
        acc = xv
        for c in range(0, D_FF, FF_CHUNK):
            gate = jnp.dot(nb, wg_ref[:, c:c + FF_CHUNK], preferred_element_type=F32)
            up = jnp.dot(nb, wu_ref[:, c:c + FF_CHUNK], preferred_element_type=F32)
            gate_ref[:, c:c + FF_CHUNK] = gate.astype(BF16)
            up_ref[:, c:c + FF_CHUNK] = up.astype(BF16)
            act = (gate * _sigmoid(gate) * up).astype(BF16)
            acc = acc + 0.5 * jnp.dot(act, wd_ref[c:c + FF_CHUNK, :], preferred_element_type=F32)
        y_ref[...] = acc

    return _pallas_call(
        body, name=name, grid=(s // tm,),
        in_specs=[pl.BlockSpec((tm, D_MODEL), lambda i: (i, 0)), _full((1, D_MODEL)),
                  _resident((D_MODEL, D_FF)), _resident((D_MODEL, D_FF)), _resident((D_FF, D_MODEL))],
        out_specs=[pl.BlockSpec((tm, D_MODEL), lambda i: (i, 0)), pl.BlockSpec((tm, D_FF), lambda i: (i, 0)),
                   pl.BlockSpec((tm, D_FF), lambda i: (i, 0))],
        out_shape=[jax.ShapeDtypeStruct((s, D_MODEL), F32), jax.ShapeDtypeStruct((s, D_FF), BF16),
                   jax.ShapeDtypeStruct((s, D_FF), BF16)],
        compiler_params=_params(dimension_semantics=("arbitrary",)),
    )(x, g, wg, wu, wd)


def _ffn_bwd(x, dy, gate, up, g, wg, wu, wd, name):
    s = x.shape[0]
    tm = _row_tile(s, 256)

    def body(x_ref, dy_ref, gate_ref, up_ref, g_ref, wg_ref, wu_ref, wd_ref,
             dx_ref, dgain_ref, n_ref, act_ref, dgate_ref, dup_ref):
        i = pl.program_id(0)
        xv = x_ref[...]
        dyv = dy_ref[...]
        gv = g_ref[...]
        r = lax.rsqrt(jnp.mean(xv * xv, axis=-1, keepdims=True) + EPS)
        xh = xv * r
        n_ref[...] = (xh * gv).astype(BF16)
        dyh = (0.5 * dyv).astype(BF16)
        dn = jnp.zeros_like(xv)
        for c in range(0, D_FF, FF_CHUNK):
            dact = _dot_nt(dyh, wd_ref[c:c + FF_CHUNK, :])
            gt = gate_ref[:, c:c + FF_CHUNK].astype(F32)
            u = up_ref[:, c:c + FF_CHUNK].astype(F32)
            sg = _sigmoid(gt)
            sl = gt * sg
            act_ref[:, c:c + FF_CHUNK] = (sl * u).astype(BF16)
            dup = (dact * sl).astype(BF16)
            dgate = (dact * u * (sg * (1.0 + gt * (1.0 - sg)))).astype(BF16)
            dup_ref[:, c:c + FF_CHUNK] = dup
            dgate_ref[:, c:c + FF_CHUNK] = dgate
            dn = dn + _dot_nt(dgate, wg_ref[:, c:c + FF_CHUNK]) + _dot_nt(dup, wu_ref[:, c:c + FF_CHUNK])
        part = jnp.sum(dn * xh, axis=0, keepdims=True)

        @pl.when(i == 0)
        def _():
            dgain_ref[...] = part

        @pl.when(i > 0)
        def _():
            dgain_ref[...] += part

        dxh = dn * gv
        dx_ref[...] = dyv + r * (dxh - xh * jnp.mean(dxh * xh, axis=-1, keepdims=True))

    row = lambda w: pl.BlockSpec((tm, w), lambda i: (i, 0))
    return _pallas_call(
        body, name=name, grid=(s // tm,),
        in_specs=[row(D_MODEL), row(D_MODEL), row(D_FF), row(D_FF), _full((1, D_MODEL)),
                  _resident((D_MODEL, D_FF)), _resident((D_MODEL, D_FF)), _resident((D_FF, D_MODEL))],
        out_specs=[row(D_MODEL), _full((1, D_MODEL)), row(D_MODEL), row(D_FF), row(D_FF), row(D_FF)],
        out_shape=[jax.ShapeDtypeStruct((s, D_MODEL), F32), jax.ShapeDtypeStruct((1, D_MODEL), F32),
                   jax.ShapeDtypeStruct((s, D_MODEL), BF16), jax.ShapeDtypeStruct((s, D_FF), BF16),
                   jax.ShapeDtypeStruct((s, D_FF), BF16), jax.ShapeDtypeStruct((s, D_FF), BF16)],
        compiler_params=_params(dimension_semantics=("arbitrary",)),
    )(x, dy, gate, up, g, wg, wu, wd)


def _store_col_shards(o_ref, acc, first_shard, n_here, width):
    for q in range(n_here):
        o_ref[q] = acc[:, (first_shard + q) * width:(first_shard + q + 1) * width].astype(BF16)


def _matmul_tn(a, b, scale, name, col_shards):
    t, m = a.shape
    n = b.shape[1]
    tk = _row_tile(t, 2048)
    tn = n // 2
    per = n // N_SHARD
    nk = t // tk

    def body(a_ref, b_ref, o_ref, acc_ref):
        k = pl.program_id(1)
        bv = b_ref[...]
        if scale != 1.0:
            bv = bv.astype(F32) * scale
        part = _dot_tn(a_ref[...].astype(BF16), bv.astype(BF16))

        @pl.when(k == 0)
        def _():
            acc_ref[...] = part

        @pl.when(k > 0)
        def _():
            acc_ref[...] += part

        @pl.when(k == nk - 1)
        def _():
            if col_shards:
                _store_col_shards(o_ref, acc_ref[...], 0, tn // per, per)
            else:
                o_ref[...] = acc_ref[...].astype(BF16)

    if col_shards:
        out_spec = pl.BlockSpec((tn // per, m, per), lambda j, k: (j, 0, 0))
        out_shape = jax.ShapeDtypeStruct((N_SHARD, m, per), BF16)
    else:
        out_spec = pl.BlockSpec((m, tn), lambda j, k: (0, j))
        out_shape = jax.ShapeDtypeStruct((m, n), BF16)
    return _pallas_call(
        body, name=name, grid=(n // tn, nk),
        in_specs=[pl.BlockSpec((tk, m), lambda j, k: (k, 0)), pl.BlockSpec((tk, tn), lambda j, k: (k, j))],
        out_specs=out_spec, out_shape=out_shape, scratch_shapes=[pltpu.VMEM((m, tn), F32)],
        compiler_params=_params(dimension_semantics=("arbitrary", "arbitrary")),
    )(a, b)


_HALF = SWA_HEAD_DIM // 2
_IN_ORDER = (list(range(0, IN_OFFS[2]))
             + [IN_OFFS[3] + SWA_HEAD_DIM * h + i for h in range(SWA_HEADS) for i in range(_HALF)]
             + [IN_OFFS[3] + SWA_HEAD_DIM * h + _HALF + i for h in range(SWA_HEADS) for i in range(_HALF)]
             + list(range(IN_OFFS[5], IN_OFFS[6]))
             + [IN_OFFS[4] + SWA_HEAD_DIM * j + i for j in range(SWA_KV_HEADS) for i in range(_HALF)]
             + [IN_OFFS[4] + SWA_HEAD_DIM * j + _HALF + i for j in range(SWA_KV_HEADS) for i in range(_HALF)]
             + list(range(IN_OFFS[2], IN_OFFS[3])))
_QB_ORDER = ([MLA_QK * h + i for h in range(MLA_HEADS) for i in range(MLA_NOPE)]
             + [MLA_QK * h + MLA_NOPE + i for h in range(MLA_HEADS) for i in range(_HALF)]
             + [MLA_QK * h + MLA_NOPE + _HALF + i for h in range(MLA_HEADS) for i in range(_HALF)])
_KVB_ORDER = ([(MLA_NOPE + MLA_V) * h + i for h in range(MLA_HEADS) for i in range(MLA_NOPE)]
              + [(MLA_NOPE + MLA_V) * h + MLA_NOPE + i for h in range(MLA_HEADS) for i in range(MLA_V)])
_P_CQ, _P_CKV, _P_QA, _P_QB, _P_VS, _P_KA, _P_KB, _P_PE = (int(v) for v in np.cumsum(
    (0, MLA_Q_RANK, MLA_KV_RANK, SWA_WIDTH // 2, SWA_WIDTH // 2, IN_SPLITS[5], IN_SPLITS[4] // 2, IN_SPLITS[4] // 2)))


def _runs(order):
    out, start = [], 0
    for i in range(1, len(order) + 1):
        if i == len(order) or order[i] != order[i - 1] + 1:
            out.append((order[start], i - start))
            start = i
    return out


def _inverse(order):
    inv = [0] * len(order)
    for new, old in enumerate(order):
        inv[old] = new
    return inv


def _take_cols(a, order):
    return jnp.concatenate([a[..., st:st + w] for st, w in _runs(order)], axis=-1)


def _segment_matrix(n, seg):
    return (lax.broadcasted_iota(jnp.int32, (n, n), 0) // seg == lax.broadcasted_iota(jnp.int32, (n, n), 1) // seg).astype(BF16)


@jax.custom_vjp
def _cmm(t, b, bt):
    hi = t.astype(BF16)
    lo = (t - hi.astype(F32)).astype(BF16)
    return jnp.dot(hi, b, preferred_element_type=F32) + jnp.dot(lo, b, preferred_element_type=F32)


def _cmm_fwd(t, b, bt):
    return _cmm(t, b, bt), (b, bt)


def _cmm_bwd(res, dy):
    b, bt = res
    return _cmm(dy, bt, b), None, None


_cmm.defvjp(_cmm_fwd, _cmm_bwd)


def _segsum(t, b):
    return _cmm(t, b, b)


def _rowsum(t):
    n = t.shape[-1]
    return _cmm(t, jnp.ones((n, 128), BF16), jnp.ones((128, n), BF16))


def _by_head(vals, width):
    lane = lax.broadcasted_iota(jnp.int32, (vals[0].shape[0], len(vals) * width), 1)
    out = vals[-1]
    for hd in range(len(vals) - 2, -1, -1):
        out = jnp.where(lane < (hd + 1) * width, vals[hd], out)
    return out


def _rope2(a, b, cos, sin):
    return a * cos - b * sin, b * cos + a * sin


def _pre_math(x, gm, gqa, gkva, gq, gk, gsq, gsk, taps, win, wqb, wkvb, cos, sin):
    h = _rms(x, gm)
    proj = _mm(h, win)
    if taps is not None:
        proj = proj + taps[0]
    cqn = _rms(proj[:, _P_CQ:_P_CKV], gqa)
    qa_all = _mm(cqn, wqb)
    ckvn = _rms(proj[:, _P_CKV:_P_QA], gkva)
    kv_all = _mm(ckvn, wkvb)
    if taps is not None:
        qa_all = qa_all + taps[1]
        kv_all = kv_all + taps[2]
    nh, hw = MLA_HEADS, MLA_HEADS * _HALF
    seg_mla = _segment_matrix(hw, _HALF)
    tile = lambda g, n: jnp.concatenate([g] * n, axis=-1)
    c4, s4 = cos[:, :hw], sin[:, :hw]

    def mla_heads(nope, r1, r2, gain):
        rr = r1 * r1 + r2 * r2
        lane_head = lax.broadcasted_iota(jnp.int32, (hw, nh * MLA_NOPE), 0) // _HALF
        spread = (lane_head == lax.broadcasted_iota(jnp.int32, (hw, nh * MLA_NOPE), 1) // MLA_NOPE).astype(BF16)
        rope_on_nope = _cmm(rr, spread, spread.T)
        ss_nope = [_rowsum(jnp.square(nope[:, hd * MLA_NOPE:(hd + 1) * MLA_NOPE])) for hd in range(nh)]
        rinv = [lax.rsqrt((ss_nope[hd] + rope_on_nope[:, hd * MLA_NOPE:(hd + 1) * MLA_NOPE]) * (1.0 / MLA_QK) + EPS)
                for hd in range(nh)]
        rl = lax.rsqrt((_segsum(rr, seg_mla) + _by_head(ss_nope, _HALF)) * (1.0 / MLA_QK) + EPS)
        o1, o2 = _rope2(r1 * rl * tile(gain[:, MLA_NOPE:MLA_NOPE + _HALF], nh), r2 * rl * tile(gain[:, MLA_NOPE + _HALF:], nh), c4, s4)
        return [jnp.concatenate([nope[:, hd * MLA_NOPE:(hd + 1) * MLA_NOPE] * rinv[hd] * gain[:, :MLA_NOPE],
                                 o1[:, hd * _HALF:(hd + 1) * _HALF], o2[:, hd * _HALF:(hd + 1) * _HALF]], axis=-1)
                for hd in range(nh)]

    q_a = mla_heads(qa_all[:, :nh * MLA_NOPE], qa_all[:, nh * MLA_NOPE:nh * MLA_NOPE + hw], qa_all[:, nh * MLA_NOPE + hw:], gq)
    pe1, pe2 = proj[:, _P_PE:_P_PE + _HALF], proj[:, _P_PE + _HALF:_P_PE + 2 * _HALF]
    k_a = mla_heads(kv_all[:, :nh * MLA_NOPE], tile(pe1, nh), tile(pe2, nh), gk)
    v_a = [kv_all[:, nh * MLA_NOPE + hd * MLA_V:nh * MLA_NOPE + (hd + 1) * MLA_V] for hd in range(nh)]

    def swa_heads(a, b, gain, n):
        w = n * _HALF
        r = lax.rsqrt(_segsum(a * a + b * b, _segment_matrix(w, _HALF)) * (1.0 / SWA_HEAD_DIM) + EPS)
        o1, o2 = _rope2(a * r * tile(gain[:, :_HALF], n), b * r * tile(gain[:, _HALF:], n), cos[:, :w], sin[:, :w])
        return [jnp.concatenate([o1[:, hd * _HALF:(hd + 1) * _HALF], o2[:, hd * _HALF:(hd + 1) * _HALF]], axis=-1) for hd in range(n)]

    q_b = swa_heads(proj[:, _P_QA:_P_QB], proj[:, _P_QB:_P_VS], gsq, SWA_HEADS)
    k_b = swa_heads(proj[:, _P_KA:_P_KB], proj[:, _P_KB:_P_PE], gsk, SWA_KV_HEADS)
    v_b = [proj[:, _P_VS + j * SWA_HEAD_DIM:_P_VS + (j + 1) * SWA_HEAD_DIM] for j in range(SWA_KV_HEADS)]
    return (q_a, k_a, v_a, q_b, k_b, v_b), (h, cqn, ckvn)


_PRE_GAIN_WIDTHS = (D_MODEL, MLA_Q_RANK, MLA_KV_RANK, MLA_QK, MLA_QK, SWA_HEAD_DIM, SWA_HEAD_DIM)
_PRE_HEADS = ((MLA_HEADS, MLA_QK), (MLA_HEADS, MLA_QK), (MLA_HEADS, MLA_V),
              (SWA_HEADS, SWA_HEAD_DIM), (SWA_KV_HEADS, SWA_HEAD_DIM), (SWA_KV_HEADS, SWA_HEAD_DIM))


def _pre_fwd(x, gains, win, wqb, wkvb, cos, sin, name):
    s = x.shape[0]
    tm = _row_tile(s, 512)

    def body(x_ref, *refs):
        g_refs, (win_ref, wqb_ref, wkvb_ref, cos_ref, sin_ref), out_refs = refs[:7], refs[7:12], refs[12:]
        outs, _ = _pre_math(x_ref[...], *[g[...] for g in g_refs], None, win_ref[...], wqb_ref[...], wkvb_ref[...],
                            cos_ref[...], sin_ref[...])
        for idx, (ref, heads) in enumerate(zip(out_refs, outs)):
            for hd, val in enumerate(heads):
                ref[hd] = (val * MLA_QSCALE if idx == 0 else val).astype(BF16)

    heads_spec = lambda nh, w: pl.BlockSpec((nh, tm, w), lambda i: (0, i, 0))
    return _pallas_call(
        body, name=name, grid=(s // tm,),
        in_specs=[pl.BlockSpec((tm, D_MODEL), lambda i: (i, 0))] + [_full((1, w)) for w in _PRE_GAIN_WIDTHS]
        + [_resident(win.shape), _resident(wqb.shape), _resident(wkvb.shape),
           pl.BlockSpec((tm, SWA_HEADS * _HALF), lambda i: (i, 0)), pl.BlockSpec((tm, SWA_HEADS * _HALF), lambda i: (i, 0))],
        out_specs=[heads_spec(nh, w) for nh, w in _PRE_HEADS],
        out_shape=[jax.ShapeDtypeStruct((nh, s, w), BF16) for nh, w in _PRE_HEADS],
        compiler_params=_params(dimension_semantics=("arbitrary",)),
    )(x, *gains, win, wqb, wkvb, cos, sin)


def _pre_bwd(x, dx_res, cts, gains, win, wqb, wkvb, cos, sin, name):
    s = x.shape[0]
    tm = _row_tile(s, 256)
    tap_widths = (IN_COLS, MLA_HEADS * MLA_QK, MLA_HEADS * (MLA_NOPE + MLA_V))

    def body(x_ref, dxr_ref, *refs):
        ct_refs, g_refs = refs[:6], refs[6:13]
        win_ref, wqb_ref, wkvb_ref, cos_ref, sin_ref = refs[13:18]
        dx_ref, dg_refs, dw_refs, acc_refs = refs[18], refs[19:26], refs[26:29], refs[29:32]
        i = pl.program_id(0)
        win_v, wqb_v, wkvb_v, cos_v, sin_v = win_ref[...], wqb_ref[...], wkvb_ref[...], cos_ref[...], sin_ref[...]

        def f(xv, gm, gqa, gkva, gq, gk, gsq, gsk, t0, t1, t2):
            return _pre_math(xv, gm, gqa, gkva, gq, gk, gsq, gsk, (t0, t1, t2), win_v, wqb_v, wkvb_v, cos_v, sin_v)

        taps = [jnp.zeros((tm, w), F32) for w in tap_widths]
        _, vjp, acts = jax.vjp(f, x_ref[...], *[g[...] for g in g_refs], *taps, has_aux=True)
        ct = tuple([ref[hd] for hd in range(nh)] for ref, (nh, _) in zip(ct_refs, _PRE_HEADS))
        grads = vjp(ct)
        dx_ref[...] = grads[0] + dxr_ref[...]
        dws = [_dot_tn(a.astype(BF16), t.astype(BF16)) for a, t in zip(acts, grads[8:11])]

        @pl.when(i == 0)
        def _():
            for ref, val in zip(dg_refs, grads[1:8]):
                ref[...] = val
            for ref, val in zip(acc_refs, dws):
                ref[...] = val

        @pl.when(i > 0)
        def _():
            for ref, val in zip(dg_refs, grads[1:8]):
                ref[...] += val
            for ref, val in zip(acc_refs, dws):
                ref[...] += val

        @pl.when(i == s // tm - 1)
        def _():
            for ref, acc, order in zip(dw_refs, acc_refs, (_IN_ORDER, _QB_ORDER, _KVB_ORDER)):
                _store_col_shards(ref, _take_cols(acc[...], _inverse(order)), 0, N_SHARD, acc.shape[1] // N_SHARD)

    heads_spec = lambda nh, w: pl.BlockSpec((nh, tm, w), lambda i: (0, i, 0))
    row = pl.BlockSpec((tm, D_MODEL), lambda i: (i, 0))
    half = pl.BlockSpec((tm, SWA_HEADS * _HALF), lambda i: (i, 0))
    shard_shapes = [(N_SHARD, w.shape[0], w.shape[1] // N_SHARD) for w in (win, wqb, wkvb)]
    return _pallas_call(
        body, name=name, grid=(s // tm,),
        in_specs=[row, row] + [heads_spec(nh, w) for nh, w in _PRE_HEADS] + [_full((1, w)) for w in _PRE_GAIN_WIDTHS]
        + [_resident(win.shape), _resident(wqb.shape), _resident(wkvb.shape), half, half],
        out_specs=[row] + [_full((1, w)) for w in _PRE_GAIN_WIDTHS] + [_full(shp) for shp in shard_shapes],
        out_shape=[jax.ShapeDtypeStruct((s, D_MODEL), F32)] + [jax.ShapeDtypeStruct((1, w), F32) for w in _PRE_GAIN_WIDTHS]
        + [jax.ShapeDtypeStruct(shp, BF16) for shp in shard_shapes],
        scratch_shapes=[pltpu.VMEM(w.shape, F32) for w in (win, wqb, wkvb)],
        compiler_params=_params(dimension_semantics=("arbitrary",)),
    )(x, dx_res, *cts, *gains, win, wqb, wkvb, cos, sin)


def _post_math(oa, ob, ga, gb, wo):
    mixed = jnp.concatenate([_rms(jnp.concatenate(oa, axis=-1), ga), _rms(jnp.concatenate(ob, axis=-1), gb)], axis=-1)
    return _mm(mixed, wo), mixed


def _post_fwd(x, oa, ob, ga, gb, wo, name):
    s = x.shape[0]
    tm = _row_tile(s, 512)

    def body(x_ref, oa_ref, ob_ref, ga_ref, gb_ref, wo_ref, y_ref):
        y, _ = _post_math([oa_ref[hd] for hd in range(MLA_HEADS)], [ob_ref[hd] for hd in range(SWA_HEADS)],
                          ga_ref[...], gb_ref[...], wo_ref[...])
        y_ref[...] = x_ref[...] + y

    row = pl.BlockSpec((tm, D_MODEL), lambda i: (i, 0))
    return _pallas_call(
        body, name=name, grid=(s // tm,),
        in_specs=[row, pl.BlockSpec((MLA_HEADS, tm, MLA_V), lambda i: (0, i, 0)),
                  pl.BlockSpec((SWA_HEADS, tm, SWA_HEAD_DIM), lambda i: (0, i, 0)),
                  _full((1, MLA_WIDTH)), _full((1, SWA_WIDTH)), _resident(wo.shape)],
        out_specs=row, out_shape=jax.ShapeDtypeStruct((s, D_MODEL), F32),
        compiler_params=_params(dimension_semantics=("arbitrary",)),
    )(x, oa, ob, ga, gb, wo)


def _post_bwd(dy, oa, ob, ga, gb, wo, name):
    s = dy.shape[0]
    tm = _row_tile(s, 512)

    def body(dy_ref, oa_ref, ob_ref, ga_ref, gb_ref, wo_ref, doa_ref, dob_ref, dga_ref, dgb_ref, dwo_ref, acc_ref):
        i = pl.program_id(0)
        wo_v = wo_ref[...]
        dyv = dy_ref[...]

        def f(oa_l, ob_l, ga_v, gb_v):
            return _post_math(oa_l, ob_l, ga_v, gb_v, wo_v)

        _, vjp, mixed = jax.vjp(f, [oa_ref[hd] for hd in range(MLA_HEADS)], [ob_ref[hd] for hd in range(SWA_HEADS)],
                                ga_ref[...], gb_ref[...], has_aux=True)
        doa, dob, dga, dgb = vjp(dyv)
        for hd in range(MLA_HEADS):
            doa_ref[hd] = doa[hd]
        for hd in range(SWA_HEADS):
            dob_ref[hd] = dob[hd]
        dwo = _dot_tn(mixed.astype(BF16), dyv.astype(BF16))

        @pl.when(i == 0)
        def _():
            dga_ref[...] = dga
            dgb_ref[...] = dgb
            acc_ref[...] = dwo

        @pl.when(i > 0)
        def _():
            dga_ref[...] += dga
            dgb_ref[...] += dgb
            acc_ref[...] += dwo

        @pl.when(i == s // tm - 1)
        def _():
            dwo_ref[...] = acc_ref[...].astype(BF16)

    row = pl.BlockSpec((tm, D_MODEL), lambda i: (i, 0))
    oa_spec = pl.BlockSpec((MLA_HEADS, tm, MLA_V), lambda i: (0, i, 0))
    ob_spec = pl.BlockSpec((SWA_HEADS, tm, SWA_HEAD_DIM), lambda i: (0, i, 0))
    return _pallas_call(
        body, name=name, grid=(s // tm,),
        in_specs=[row, oa_spec, ob_spec, _full((1, MLA_WIDTH)), _full((1, SWA_WIDTH)), _resident(wo.shape)],
        out_specs=[oa_spec, ob_spec, _full((1, MLA_WIDTH)), _full((1, SWA_WIDTH)), _full(wo.shape)],
        out_shape=[jax.ShapeDtypeStruct((MLA_HEADS, s, MLA_V), F32), jax.ShapeDtypeStruct((SWA_HEADS, s, SWA_HEAD_DIM), F32),
                   jax.ShapeDtypeStruct((1, MLA_WIDTH), F32), jax.ShapeDtypeStruct((1, SWA_WIDTH), F32),
                   jax.ShapeDtypeStruct(wo.shape, BF16)],
        scratch_shapes=[pltpu.VMEM(wo.shape, F32)],
        compiler_params=_params(dimension_semantics=("arbitrary",)),
    )(dy, oa, ob, ga, gb, wo)


def _attn_tile(s):
    return 512 if s >= 2048 else 128


def _causal_mask(t):
    return lax.broadcasted_iota(jnp.int32, (t, t), 1) <= lax.broadcasted_iota(jnp.int32, (t, t), 0)


def _pipelined_blocks(first, count, last_block, issue, consume, carry, prefetch_after):
    def clamped(j, slot):
        issue(jnp.minimum(j, last_block), slot)

    def pair(jj, c):
        a = first + 2 * jj
        clamped(a + 1, 1)
        c = consume(a, 0, c)
        clamped(a + 2, 0)
        return consume(a + 1, 1, c)

    clamped(first, 0)
    npairs = count // 2
    carry = lax.fori_loop(0, npairs, pair, carry)

    def odd(c):
        c = consume(first + 2 * npairs, 0, c)
        if prefetch_after:
            clamped(first + count, 0)
        return c

    return lax.cond(count - 2 * npairs == 1, odd, lambda c: c, carry)


def _run_stages_at(stages, steps):
    for stage, (s0, s1) in zip(stages, steps):
        pl.when((pl.program_id(0) == s0) & (pl.program_id(1) == s1))(stage)


def _mla_fwd(q, k, v, name, gather=None):
    nh, s, _ = q.shape
    t = _attn_tile(s)
    nq = s // t
    ng = len(gather) if gather else 0

    def body(q_ref, k_ref, v_ref, *rest):
        g_ins, (o_ref, lse_ref), g_outs = rest[:ng], rest[ng:ng + 2], rest[ng + 2:2 * ng + 2]
        (s0_ref, s1_ref), sems = rest[2 * ng + 2:2 * ng + 4], rest[2 * ng + 4:]
        if ng:
            _run_stages_at(_gather_stages(g_ins, g_outs, *sems), [(0, 0), (nh // 2, 0), (nh - 1, 0), (nh - 1, nq - 1)])
        qi = pl.program_id(1)
        qv = q_ref[...]
        s_refs = (s0_ref, s1_ref)

        def rows(j):
            return pl.ds(pl.multiple_of(j * t, t), t)

        def issue(j, slot):
            s_refs[slot][...] = _dot_nt(qv, k_ref[rows(j), :])

        def consume(j, slot, carry, masked=False):
            m, l, acc = carry
            sc = s_refs[slot][...]
            if masked:
                sc = jnp.where(_causal_mask(t), sc, NEG)
            m_new = jnp.maximum(m, jnp.max(sc, axis=-1, keepdims=True))
            alpha = jnp.exp2(m - m_new)
            p = jnp.exp2(sc - m_new)
            l = alpha * l + jnp.sum(p, axis=-1, keepdims=True)
            acc = alpha * acc + jnp.dot(p.astype(BF16), v_ref[rows(j), :], preferred_element_type=F32)
            return m_new, l, acc

        init = (jnp.full((t, 1), NEG, F32), jnp.zeros((t, 1), F32), jnp.zeros((t, MLA_V), F32))
        carry = _pipelined_blocks(0, qi, nq - 1, issue, consume, init, True)
        m, l, acc = consume(qi, 0, carry, masked=True)
        o_ref[...] = acc / l
        lse_ref[...] = m + jnp.log2(l)

    outs = _pallas_call(
        body, name=name, grid=(nh, nq),
        in_specs=[pl.BlockSpec((None, t, MLA_QK), lambda h, i: (h, i, 0)), pl.BlockSpec((None, s, MLA_QK), lambda h, i: (h, 0, 0)),
                  pl.BlockSpec((None, s, MLA_V), lambda h, i: (h, 0, 0))] + [_HBM] * ng,
        out_specs=[pl.BlockSpec((None, t, MLA_V), lambda h, i: (h, i, 0)), pl.BlockSpec((None, t, 1), lambda h, i: (h, i, 0))]
        + [_HBM] * ng,
        out_shape=[jax.ShapeDtypeStruct((nh, s, MLA_V), F32), jax.ShapeDtypeStruct((nh, s, 1), F32)]
        + (_gather_out_shapes(gather) if ng else []),
        scratch_shapes=[pltpu.VMEM((t, t), F32)] * 2 + (_exchange_scratch(8, ng) if ng else []),
        compiler_params=_params(dimension_semantics=("arbitrary", "arbitrary")),
    )(q, k, v, *(gather or []))
    return outs[0], outs[1], outs[2:]


def _mla_bwd_dq(q, k, v, o, do, lse, name):
    nh, s, _ = q.shape
    t = _attn_tile(s)
    nq = s // t

    def body(q_ref, k_ref, v_ref, o_ref, do_ref, lse_ref, dq_ref, delta_ref, s0_ref, s1_ref, dp0_ref, dp1_ref):
        qi = pl.program_id(1)
        qv = q_ref[...]
        dov = do_ref[...]
        lse = lse_ref[...]
        delta = jnp.sum(dov * o_ref[...], axis=-1, keepdims=True)
        delta_ref[...] = delta
        dob = dov.astype(BF16)
        s_refs, dp_refs = (s0_ref, s1_ref), (dp0_ref, dp1_ref)

        def rows(j):
            return pl.ds(pl.multiple_of(j * t, t), t)

        def issue(j, slot):
            s_refs[slot][...] = _dot_nt(qv, k_ref[rows(j), :])
            dp_refs[slot][...] = _dot_nt(dob, v_ref[rows(j), :])

        def consume(j, slot, dq, masked=False):
            p = jnp.exp2(s_refs[slot][...] - lse)
            if masked:
                p = jnp.where(_causal_mask(t), p, 0.0)
            ds = p * (dp_refs[slot][...] - delta)
            return dq + jnp.dot(ds.astype(BF16), k_ref[rows(j), :], preferred_element_type=F32)

        dq = _pipelined_blocks(0, qi, nq - 1, issue, consume, jnp.zeros((t, MLA_QK), F32), True)
        dq_ref[...] = consume(qi, 0, dq, masked=True) * MLA_SCALE

    tile = lambda w: pl.BlockSpec((None, t, w), lambda h, i: (h, i, 0))
    whole = lambda w: pl.BlockSpec((None, s, w), lambda h, i: (h, 0, 0))
    return _pallas_call(
        body, name=name, grid=(nh, nq),
        in_specs=[tile(MLA_QK), whole(MLA_QK), whole(MLA_V), tile(MLA_V), tile(MLA_V), tile(1)],
        out_specs=[tile(MLA_QK), tile(1)],
        out_shape=[jax.ShapeDtypeStruct((nh, s, MLA_QK), F32), jax.ShapeDtypeStruct((nh, s, 1), F32)],
        scratch_shapes=[pltpu.VMEM((t, t), F32)] * 4,
        compiler_params=_params(dimension_semantics=("arbitrary", "arbitrary")),
    )(q, k, v, o, do, lse)


def _mla_bwd_dkv(q, k, v, do, lse_row, delta_row, name, scatter=None):
    nh, s, _ = q.shape
    t = _attn_tile(s)
    nq = s // t
    ng = len(scatter) if scatter else 0

    def body(q_ref, k_ref, v_ref, do_ref, lse_ref, delta_ref, *rest):
        c_ins, (dk_ref, dv_ref), c_outs = rest[:ng], rest[ng:ng + 2], rest[ng + 2:2 * ng + 2]
        (s0_ref, s1_ref, dp0_ref, dp1_ref), sems = rest[2 * ng + 2:2 * ng + 6], rest[2 * ng + 6:]
        if ng:
            _run_stages_at(_scatter_stages(c_ins, c_outs, *sems), [(0, 0), (nh - 1, nq - 1)])
        kj = pl.program_id(1)
        kv_, vv = k_ref[...], v_ref[...]
        s_refs, dp_refs = (s0_ref, s1_ref), (dp0_ref, dp1_ref)

        def rows(i):
            return pl.ds(pl.multiple_of(i * t, t), t)

        def issue(i, slot):
            s_refs[slot][...] = _dot_nt(kv_, q_ref[rows(i), :])
            dp_refs[slot][...] = _dot_nt(vv, do_ref[rows(i), :].astype(BF16))

        def consume(i, slot, carry, masked=False):
            dk, dv = carry
            p = jnp.exp2(s_refs[slot][...] - lse_ref[pl.ds(i, 1), :])
            if masked:
                p = jnp.where(lax.broadcasted_iota(jnp.int32, (t, t), 0) <= lax.broadcasted_iota(jnp.int32, (t, t), 1), p, 0.0)
            dv = dv + jnp.dot(p.astype(BF16), do_ref[rows(i), :].astype(BF16), preferred_element_type=F32)
            ds = p * (dp_refs[slot][...] - delta_ref[pl.ds(i, 1), :])
            dk = dk + jnp.dot(ds.astype(BF16), q_ref[rows(i), :], preferred_element_type=F32)
            return dk, dv

        issue(kj, 0)
        carry = consume(kj, 0, (jnp.zeros((t, MLA_QK), F32), jnp.zeros((t, MLA_V), F32)), masked=True)
        dk, dv = _pipelined_blocks(kj + 1, nq - 1 - kj, nq - 1, issue, consume, carry, False)
        dk_ref[...] = dk * LN2
        dv_ref[...] = dv

    tile = lambda w: pl.BlockSpec((None, t, w), lambda h, j: (h, j, 0))
    whole = lambda w: pl.BlockSpec((None, s, w), lambda h, j: (h, 0, 0))
    rows_spec = pl.BlockSpec((None, nq, t), lambda h, j: (h, 0, 0))
    outs = _pallas_call(
        body, name=name, grid=(nh, nq),
        in_specs=[whole(MLA_QK), tile(MLA_QK), tile(MLA_V), whole(MLA_V), rows_spec, rows_spec] + [_HBM] * ng,
        out_specs=[tile(MLA_QK), tile(MLA_V)] + [_HBM] * ng,
        out_shape=[jax.ShapeDtypeStruct((nh, s, MLA_QK), F32), jax.ShapeDtypeStruct((nh, s, MLA_V), F32)]
        + (_scatter_out_shapes(scatter) if ng else []),
        scratch_shapes=[pltpu.VMEM((t, t), F32)] * 4 + (_exchange_scratch(3, ng) if ng else []),
        compiler_params=_params(dimension_semantics=("arbitrary", "arbitrary")),
    )(q, k, v, do, lse_row, delta_row, *(scatter or []))
    return outs[0], outs[1], outs[2:]


def _swa_tile(s):
    return min(s, 4 * BLOCK)


def _swa_specs(tq):
    nb = tq // BLOCK
    grp = lambda w: pl.BlockSpec((SWA_GROUP, tq, w), lambda j, i: (j, i, 0))
    main = pl.BlockSpec((None, tq, SWA_HEAD_DIM), lambda j, i: (j, i, 0))
    tail = pl.BlockSpec((None, BLOCK, SWA_HEAD_DIM), lambda j, i: (j, nb * (i + 1), 0))
    sink = pl.BlockSpec((None, SWA_GROUP, 128), lambda j, i: (j, 0, 0))
    return grp, main, tail, sink


def _swa_band_mask(first):
    shape = (SWA_GROUP * BLOCK, 2 * BLOCK)
    q_rel = (lax.broadcasted_iota(jnp.int32, shape, 0) & (BLOCK - 1)) + BLOCK
    k_rel = lax.broadcasted_iota(jnp.int32, shape, 1)
    dist = q_rel - k_rel
    return (dist >= 0) & (dist < BLOCK) & ((k_rel >= BLOCK) | jnp.logical_not(first))


def _swa_sink_column(sink_ref):
    sk = sink_ref[...]
    return jnp.concatenate([jnp.broadcast_to(sk[g:g + 1, 0:1], (BLOCK, 1)) for g in range(SWA_GROUP)], axis=0)


def _swa_fwd(q, kpad, vpad, sinks, name):
    _, s, _ = q.shape
    tq = _swa_tile(s)
    grp, main, tail, sink = _swa_specs(tq)
    d = SWA_HEAD_DIM

    def body(q_ref, km_ref, kt_ref, vm_ref, vt_ref, sink_ref, o_ref, lse_ref):
        i = pl.program_id(1)
        kall = jnp.concatenate([km_ref[...], kt_ref[...]], axis=0)
        vall = jnp.concatenate([vm_ref[...], vt_ref[...]], axis=0)
        sink_col = _swa_sink_column(sink_ref)
        for b in range(tq // BLOCK):
            lo = b * BLOCK
            valid = _swa_band_mask(i == 0 if b == 0 else False)
            q4 = q_ref[:, lo:lo + BLOCK, :].reshape(SWA_GROUP * BLOCK, d)
            sc = jnp.where(valid, _dot_nt(q4, kall[lo:lo + 2 * BLOCK]) * SWA_SCALE, NEG)
            m = jnp.maximum(jnp.max(sc, axis=-1, keepdims=True), sink_col)
            e = jnp.exp(sc - m)
            den = jnp.sum(e, axis=-1, keepdims=True) + jnp.exp(sink_col - m)
            out = jnp.dot((e * (1.0 / den)).astype(BF16), vall[lo:lo + 2 * BLOCK], preferred_element_type=F32)
            o_ref[:, lo:lo + BLOCK, :] = out.reshape(SWA_GROUP, BLOCK, d)
            lse_ref[:, lo:lo + BLOCK, :] = (m + jnp.log(den)).reshape(SWA_GROUP, BLOCK, 1)

    return _pallas_call(
        body, name=name, grid=(SWA_KV_HEADS, s // tq),
        in_specs=[grp(d), main, tail, main, tail, sink], out_specs=[grp(d), grp(1)],
        out_shape=[jax.ShapeDtypeStruct((SWA_HEADS, s, d), F32), jax.ShapeDtypeStruct((SWA_HEADS, s, 1), F32)],
        compiler_params=_params(dimension_semantics=("arbitrary", "arbitrary")),
    )(q, kpad, kpad, vpad, vpad, sinks)


def _swa_bwd(q, kpad, vpad, sinks, o, lse, do, name):
    _, s, _ = q.shape
    tq = _swa_tile(s)
    grp, main, tail, sink = _swa_specs(tq)
    d = SWA_HEAD_DIM

    def body(q_ref, km_ref, kt_ref, vm_ref, vt_ref, sink_ref, o_ref, lse_ref, do_ref, dq_ref, dk_ref, dv_ref, dsink_ref):
        i = pl.program_id(1)
        kall = jnp.concatenate([km_ref[...], kt_ref[...]], axis=0)
        vall = jnp.concatenate([vm_ref[...], vt_ref[...]], axis=0)
        sink_col = _swa_sink_column(sink_ref)

        @pl.when(i == 0)
        def _():
            dk_ref[...] = jnp.zeros_like(dk_ref)
            dv_ref[...] = jnp.zeros_like(dv_ref)
            dsink_ref[...] = jnp.zeros_like(dsink_ref)

        dsink = jnp.zeros((SWA_GROUP * BLOCK, 1), F32)
        for b in range(tq // BLOCK):
            lo = b * BLOCK
            valid = _swa_band_mask(i == 0 if b == 0 else False)
            rows4 = SWA_GROUP * BLOCK
            q4 = q_ref[:, lo:lo + BLOCK, :].reshape(rows4, d)
            do4 = do_ref[:, lo:lo + BLOCK, :].reshape(rows4, d)
            lse4 = lse_ref[:, lo:lo + BLOCK, :].reshape(rows4, 1)
            delta = jnp.sum(do4 * o_ref[:, lo:lo + BLOCK, :].reshape(rows4, d), axis=-1, keepdims=True)
            kb, vb = kall[lo:lo + 2 * BLOCK], vall[lo:lo + 2 * BLOCK]
            do4b = do4.astype(BF16)
            p = jnp.where(valid, jnp.exp(_dot_nt(q4, kb) * SWA_SCALE - lse4), 0.0)
            ds = (p * (_dot_nt(do4b, vb) - delta) * SWA_SCALE).astype(BF16)
            dq_ref[:, lo:lo + BLOCK, :] = jnp.dot(ds, kb, preferred_element_type=F32).reshape(SWA_GROUP, BLOCK, d)
            band = pl.ds(pl.multiple_of(i * tq, BLOCK) + lo, 2 * BLOCK)
            dk_ref[band, :] += _dot_tn(ds, q4)
            dv_ref[band, :] += _dot_tn(p.astype(BF16), do4b)
            dsink = dsink - jnp.exp(sink_col - lse4) * delta
        per_head = [jnp.broadcast_to(jnp.sum(dsink[g * BLOCK:(g + 1) * BLOCK], axis=0, keepdims=True), (1, 128))
                    for g in range(SWA_GROUP)]
        dsink_ref[...] += jnp.concatenate(per_head + [jnp.zeros((8 - SWA_GROUP, 128), F32)], axis=0)

    acc = pl.BlockSpec((None, s + BLOCK, d), lambda j, i: (j, 0, 0))
    return _pallas_call(
        body, name=name, grid=(SWA_KV_HEADS, s // tq),
        in_specs=[grp(d), main, tail, main, tail, sink, grp(d), grp(1), grp(d)],
        out_specs=[grp(d), acc, acc, pl.BlockSpec((None, 8, 128), lambda j, i: (j, 0, 0))],
        out_shape=[jax.ShapeDtypeStruct((SWA_HEADS, s, d), F32),
                   jax.ShapeDtypeStruct((SWA_KV_HEADS, s + BLOCK, d), F32),
                   jax.ShapeDtypeStruct((SWA_KV_HEADS, s + BLOCK, d), F32),
                   jax.ShapeDtypeStruct((SWA_KV_HEADS, 8, 128), F32)],
        compiler_params=_params(dimension_semantics=("arbitrary", "arbitrary")),
    )(q, kpad, kpad, vpad, vpad, sinks, o, lse, do)


def _loss_head(y, target, name):
    s = y.shape[0]
    tm = _row_tile(s, 512)

    def body(y_ref, t_ref, dy_ref, loss_ref):
        i = pl.program_id(0)
        err = y_ref[...] - t_ref[...]
        dy_ref[...] = err * (1.0 / D_MODEL)
        part = jnp.broadcast_to(0.5 * jnp.sum(jnp.mean(err * err, axis=-1, keepdims=True), axis=0, keepdims=True), (1, 128))

        @pl.when(i == 0)
        def _():
            loss_ref[...] = part

        @pl.when(i > 0)
        def _():
            loss_ref[...] += part

    row = pl.BlockSpec((tm, D_MODEL), lambda i: (i, 0))
    return _pallas_call(
        body, name=name, grid=(s // tm,), in_specs=[row, row], out_specs=[row, _full((1, 128))],
        out_shape=[jax.ShapeDtypeStruct((s, D_MODEL), F32), jax.ShapeDtypeStruct((1, 128), F32)],
        compiler_params=_params(dimension_semantics=("arbitrary",)),
    )(y, target)


def _adamw(w, g, m, v, name):
    rows, cols = w.shape
    tr = rows
    for cand in (512, 256, 128, 64, 32, 16, 8):
        if rows % cand == 0 and rows > cand:
            tr = cand
            break

    def body(w_ref, g_ref, m_ref, v_ref, d_ref, nm_ref, nv_ref):
        gv = g_ref[...]
        nm = ADAM_B1 * m_ref[...] + (1.0 - ADAM_B1) * gv
        nv = ADAM_B2 * v_ref[...] + (1.0 - ADAM_B2) * (gv * gv)
        m_hat = nm / (1.0 - ADAM_B1 ** ADAM_STEP)
        v_hat = nv / (1.0 - ADAM_B2 ** ADAM_STEP)
        d_ref[...] = -ADAM_LR * (m_hat / (jnp.sqrt(v_hat) + ADAM_EPS) + ADAM_WD * w_ref[...])
        nm_ref[...] = nm
        nv_ref[...] = nv

    blk = pl.BlockSpec((tr, cols), lambda i: (i, 0))
    return _pallas_call(
        body, name=name, grid=(rows // tr,), in_specs=[blk] * 4, out_specs=[blk] * 3,
        out_shape=[jax.ShapeDtypeStruct((rows, cols), F32)] * 3,
        compiler_params=_params(dimension_semantics=("arbitrary",)),
    )(w, g, m, v)


def _position():
    return lax.axis_index("x"), lax.axis_index("y"), lax.axis_index("c")


def _remote(src, dst, send_sems, recv_sems, k, to):
    return pltpu.make_async_remote_copy(src_ref=src, dst_ref=dst, send_sem=send_sems.at[k], recv_sem=recv_sems.at[k],
                                        device_id=to, device_id_type=MESH)


_HBM = pl.BlockSpec(memory_space=pltpu.HBM)


def _gather_stages(ins, outs, send_sems, recv_sems):
    na = len(ins)
    x, y, c = _position()
    me, sibling = (x, y, c), (x, y, 1 - c)
    xn, yn, dg = (1 - x, y), (x, 1 - y), (1 - x, 1 - y)

    def slot(a, chip, pc, half=None):
        ref = outs[a].at[4 * chip[0] + 2 * chip[1] + pc]
        if half is None:
            return ref
        rows = ref.shape[0] // 2
        return ref.at[pl.ds(half * rows, rows)]

    def cp(a, k, chip, pc, half, to, src=None):
        dst = slot(a, chip, pc, half)
        return _remote(dst if src is None else src, dst, send_sems, recv_sems, 8 * a + k, to)

    first_hop = [(0, xn), (1, yn)]
    second_hop = [(0, xn, 2, 0, yn), (1, yn, 3, 1, xn)]

    def sends():
        out = []
        for a in range(na):
            out += [cp(a, k, (x, y), c, None, (*to, c), src=ins[a].at[c]) for k, to in first_hop]
            out += [cp(a, fwd_k, frm, c, half, (*to, c)) for _, frm, fwd_k, half, to in second_hop]
            out += [cp(a, 4 + k, frm, c, None, sibling) for k, frm in first_hop]
            out += [cp(a, 6 + half, dg, c, half, sibling) for half in (0, 1)]
        return out

    def stage0():
        for a in range(na):
            for k, to in first_hop:
                cp(a, k, (x, y), c, None, (*to, c), src=ins[a].at[c]).start()

    def stage1():
        for k, frm, fwd_k, half, to in second_hop:
            for a in range(na):
                cp(a, k, frm, c, None, me).wait_recv()
                cp(a, fwd_k, frm, c, half, (*to, c)).start()
                cp(a, 4 + k, frm, c, None, sibling).start()

    def stage2():
        for half in (0, 1):
            for a in range(na):
                cp(a, 2 + half, dg, c, half, me).wait_recv()
                cp(a, 6 + half, dg, c, half, sibling).start()

    def stage3():
        for a in range(na):
            for k, chip, half in ((4, xn, None), (5, yn, None), (6, dg, 0), (7, dg, 1)):
                cp(a, k, chip, 1 - c, half, me).wait_recv()
        for sent in sends():
            sent.wait_send()

    return [stage0, stage1, stage2, stage3]


def _gather_out_shapes(mine):
    return [jax.ShapeDtypeStruct((N_DEV,) + m.shape[1:], m.dtype) for m in mine]


def _exchange_scratch(per_array, na):
    return [pltpu.SemaphoreType.DMA((per_array * na,)), pltpu.SemaphoreType.DMA((per_array * na,))]


def _all_gather_halves(mine, name):
    na = len(mine)

    def body(*refs):
        for stage in _gather_stages(refs[:na], refs[na:2 * na], *refs[2 * na:]):
            stage()

    return _pallas_call(body, name=name, in_specs=[_HBM] * na, out_specs=[_HBM] * na, out_shape=_gather_out_shapes(mine),
                        scratch_shapes=_exchange_scratch(8, na))(*mine)


def _sibling_exchange(parts, name, other_half):
    na = len(parts)

    def body(*refs):
        ins, outs, (send_sems, recv_sems) = refs[:na], refs[na:2 * na], refs[2 * na:]
        x, y, c = _position()
        copies = [_remote(ins[a].at[:, 1 - c] if other_half else ins[a], outs[a], send_sems, recv_sems, a, (x, y, 1 - c))
                  for a in range(na)]
        for cp in copies:
            cp.start()
        for cp in copies:
            cp.wait()

    return _pallas_call(
        body, name=name, in_specs=[_HBM] * na, out_specs=[_HBM] * na,
        out_shape=[jax.ShapeDtypeStruct(p.shape[:1] + p.shape[2:] if other_half else p.shape, p.dtype) for p in parts],
        scratch_shapes=_exchange_scratch(1, na),
    )(*parts)


def _scatter_stages(ins, outs, send_sems, recv_sems):
    na = len(ins)
    x, y, c = _position()
    chips = [(1 - x, y), (x, 1 - y), (1 - x, 1 - y)]

    def copies():
        return [_remote(ins[a].at[2 * px + py], outs[a].at[j], send_sems, recv_sems, 3 * a + j, (px, py, c))
                for a in range(na) for j, (px, py) in enumerate(chips)]

    def start():
        for cp in copies():
            cp.start()

    def wait():
        for cp in copies():
            cp.wait()

    return [start, wait]


def _scatter_out_shapes(parts):
    return [jax.ShapeDtypeStruct((3,) + p.shape[1:], p.dtype) for p in parts]


def _scatter_to_chips(parts, name):
    na = len(parts)

    def body(*refs):
        for stage in _scatter_stages(refs[:na], refs[na:2 * na], *refs[2 * na:]):
            stage()

    return _pallas_call(body, name=name, in_specs=[_HBM] * na, out_specs=[_HBM] * na, out_shape=_scatter_out_shapes(parts),
                        scratch_shapes=_exchange_scratch(3, na))(*parts)


def _assemble(g4, mine, name, side_by_side):
    _, nl, r, w = g4.shape
    tr = next(cand for cand in (256, 128) if r % cand == 0)

    def body(in_ref, mine_ref, out_ref):
        chip = 2 * lax.axis_index("x") + lax.axis_index("y")
        blocks = [jnp.where(chip == sh, mine_ref[...], in_ref[sh]) for sh in range(N_SHARD)]
        if side_by_side:
            out_ref[...] = jnp.concatenate(blocks, axis=-1)
        else:
            for sh in range(N_SHARD):
                out_ref[sh] = blocks[sh]

    if side_by_side:
        out_spec = pl.BlockSpec((None, tr, N_SHARD * w), lambda l, i: (l, i, 0))
        out_shape = jax.ShapeDtypeStruct((nl, r, N_SHARD * w), g4.dtype)
    else:
        out_spec = pl.BlockSpec((None, N_SHARD, tr, w), lambda l, i: (l, 0, i, 0))
        out_shape = jax.ShapeDtypeStruct((nl, N_SHARD, r, w), g4.dtype)
    return _pallas_call(
        body, name=name, grid=(nl, r // tr),
        in_specs=[pl.BlockSpec((N_SHARD, None, tr, w), lambda l, i: (0, l, i, 0)),
                  pl.BlockSpec((None, tr, w), lambda l, i: (l, i, 0))],
        out_specs=out_spec, out_shape=out_shape,
        compiler_params=_params(dimension_semantics=("arbitrary", "arbitrary")),
    )(g4, mine)


def _all_reduce_small(vec, name):
    r, l = vec.shape

    def body(v_ref, out_ref, gath_ref, send_sems, recv_sems):
        x, y, c = _position()
        me = 4 * x + 2 * y + c
        gath_ref[me] = v_ref[...]
        copies = []
        for k in range(1, N_DEV):
            to = (x ^ (k >> 2), y ^ ((k >> 1) & 1), c ^ (k & 1))
            copies.append(_remote(gath_ref.at[me], gath_ref.at[me], send_sems, recv_sems, k - 1, to))
        for cp in copies:
            cp.start()
        for k in range(1, N_DEV):
            frm = 4 * (x ^ (k >> 2)) + 2 * (y ^ ((k >> 1) & 1)) + (c ^ (k & 1))
            _remote(gath_ref.at[frm], gath_ref.at[frm], send_sems, recv_sems, k - 1, (x, y, c)).wait_recv()
        for cp in copies:
            cp.wait_send()
        total = gath_ref[0]
        for d in range(1, N_DEV):
            total = total + gath_ref[d]
        out_ref[...] = total

    vm = pl.BlockSpec(memory_space=pltpu.VMEM)
    return _pallas_call(
        body, name=name, in_specs=[vm], out_specs=vm, out_shape=jax.ShapeDtypeStruct((r, l), F32),
        scratch_shapes=[pltpu.VMEM((N_DEV, r, l), F32), pltpu.SemaphoreType.DMA((N_DEV - 1,)),
                        pltpu.SemaphoreType.DMA((N_DEV - 1,))],
    )(vec)


def _sum_blocks(blocks, out_dtype, name):
    m, w = blocks[0].shape
    tr = next(cand for cand in (512, 256, 128, 64) if m % cand == 0)

    def body(*refs):
        total = refs[0][...].astype(F32)
        for ref in refs[1:-1]:
            total = total + ref[...].astype(F32)
        refs[-1][...] = total.astype(out_dtype)

    blk = pl.BlockSpec((tr, w), lambda i: (i, 0))
    return _pallas_call(
        body, name=name, grid=(m // tr,), in_specs=[blk] * len(blocks), out_specs=blk,
        out_shape=jax.ShapeDtypeStruct((m, w), out_dtype),
        compiler_params=_params(dimension_semantics=("arbitrary",)),
    )(*blocks)


GROUPS = (('ffn1_w_gate', 'ffn1_w_up', 'ffn2_w_gate', 'ffn2_w_up'), ('ffn1_w_down', 'ffn2_w_down', 'w_o'),
          ('w_in',), ('mla_w_q_b',), ('mla_w_kv_b',))


def _shard_rows(name):
    shape, axis = BIG[name]
    return shape[0] // N_SHARD if axis == 0 else shape[0]


def _group_row_offsets(group):
    return [int(v) for v in np.cumsum([0] + [_shard_rows(n) for n in group])]


def _rope_tables(s):
    pos = jnp.arange(s, dtype=F32)
    inv = 1.0 / (ROPE_THETA ** (jnp.arange(0, MLA_ROPE, 2, dtype=F32) / MLA_ROPE))
    ang = pos[:, None] * inv[None, :]
    return jnp.tile(jnp.cos(ang), (1, SWA_HEADS)), jnp.tile(jnp.sin(ang), (1, SWA_HEADS))


_MIXER_GAINS = ('mix_norm', 'mla_q_a_norm', 'mla_kv_a_norm', 'mla_q_norm', 'mla_k_norm', 'swa_q_norm', 'swa_k_norm')


def _local_step(x, target, small, weights_of, gather_next=None, reduce_begin=None):
    s = x.shape[0]
    cos, sin = _rope_tables(s)
    row = lambda name, l: small[name][l][None, :]
    saved, bigs, gathered = [], [], None
    for l in range(DEPTH):
        big = weights_of(l, gathered)
        bigs.append(big)
        sv = {'x0': x}
        x, sv['g1'], sv['u1'] = _ffn_fwd(x, row('ffn1_norm', l), big['ffn1_w_gate'], big['ffn1_w_up'], big['ffn1_w_down'],
                                        f"ffn1_fwd_{l}")
        sv['x1'] = x
        gains = [row(n, l) for n in _MIXER_GAINS]
        mixer_w = (_take_cols(big['w_in'], _IN_ORDER), _take_cols(big['mla_w_q_b'], _QB_ORDER),
                   _take_cols(big['mla_w_kv_b'], _KVB_ORDER))
        q_a, k_a, v_a, q_b, k_b, v_b = _pre_fwd(x, gains, *mixer_w, cos, sin, f"pre_fwd_{l}")
        o_a, lse, gathered = _mla_fwd(q_a, k_a, v_a, f"mla_fwd_{l}", gather_next[l] if gather_next else None)
        kpad = jnp.pad(k_b, ((0, 0), (BLOCK, 0), (0, 0)))
        vpad = jnp.pad(v_b, ((0, 0), (BLOCK, 0), (0, 0)))
        sinks = jnp.broadcast_to(small['swa_sinks'][l].reshape(SWA_KV_HEADS, SWA_GROUP, 1), (SWA_KV_HEADS, SWA_GROUP, 128))
        o_b, lse_b = _swa_fwd(q_b, kpad, vpad, sinks, f"swa_fwd_{l}")
        sv.update(gains=gains, mixer_w=mixer_w, q_a=q_a, k_a=k_a, v_a=v_a, q_b=q_b, kpad=kpad, vpad=vpad, sinks=sinks,
                  o_a=o_a, lse=lse, o_b=o_b, lse_b=lse_b)
        x = _post_fwd(x, o_a, o_b, row('mla_out_norm', l), row('swa_out_norm', l), big['w_o'], f"post_fwd_{l}")
        sv['x2'] = x
        x, sv['g2'], sv['u2'] = _ffn_fwd(x, row('ffn2_norm', l), big['ffn2_w_gate'], big['ffn2_w_up'], big['ffn2_w_down'],
                                        f"ffn2_fwd_{l}")
        saved.append(sv)

    dx, loss = _loss_head(x, target, "loss_head")

    gs = {n: [None] * DEPTH for n in SMALL_NAMES}
    gb = {n: [None] * DEPTH for n in BIG_NAMES}
    begun, received = [None] * DEPTH, [None] * DEPTH
    t = _attn_tile(s)
    for l in reversed(range(DEPTH)):
        sv, big = saved[l], bigs[l]

        def ffn_back(tag, xin, dy, gate, up):
            dxi, dgain, nb, act, dgate, dup = _ffn_bwd(xin, dy, gate, up, row(tag + '_norm', l), big[tag + '_w_gate'],
                                                      big[tag + '_w_up'], big[tag + '_w_down'], f"{tag}_bwd_{l}")
            gs[tag + '_norm'][l] = dgain[0]
            gb[tag + '_w_gate'][l] = _matmul_tn(nb, dgate, 1.0, f"{tag}_dwg_{l}", True)
            gb[tag + '_w_up'][l] = _matmul_tn(nb, dup, 1.0, f"{tag}_dwu_{l}", True)
            gb[tag + '_w_down'][l] = _matmul_tn(act, dy, 0.5, f"{tag}_dwd_{l}", False).reshape(N_SHARD, D_FF // N_SHARD, D_MODEL)
            return dxi

        dx = ffn_back('ffn2', sv['x2'], dx, sv['g2'], sv['u2'])
        do_a, do_b, dga, dgb, dwo = _post_bwd(dx, sv['o_a'], sv['o_b'], row('mla_out_norm', l), row('swa_out_norm', l),
                                              big['w_o'], f"post_bwd_{l}")
        gs['mla_out_norm'][l], gs['swa_out_norm'][l] = dga[0], dgb[0]
        gb['w_o'][l] = dwo.reshape(N_SHARD, MIX_WIDTH // N_SHARD, D_MODEL)
        dq_a, delta = _mla_bwd_dq(sv['q_a'], sv['k_a'], sv['v_a'], sv['o_a'], do_a, sv['lse'], f"mla_dq_{l}")
        riding = begun[l + 1] if l + 1 < DEPTH else None
        dk_a, dv_a, got = _mla_bwd_dkv(sv['q_a'], sv['k_a'], sv['v_a'], do_a, sv['lse'].reshape(MLA_HEADS, s // t, t),
                                       delta.reshape(MLA_HEADS, s // t, t), f"mla_dkv_{l}", riding)
        if riding is not None:
            received[l + 1] = got
        dq_b, dkpad, dvpad, dsink = _swa_bwd(sv['q_b'], sv['kpad'], sv['vpad'], sv['sinks'], sv['o_b'], sv['lse_b'], do_b,
                                             f"swa_bwd_{l}")
        gs['swa_sinks'][l] = dsink[:, :SWA_GROUP, 0].reshape(SWA_HEADS)
        cts = [dq_a, dk_a, dv_a, dq_b, dkpad[:, BLOCK:], dvpad[:, BLOCK:]]
        outs = _pre_bwd(sv['x1'], dx, cts, sv['gains'], *sv['mixer_w'], cos, sin, f"pre_bwd_{l}")
        dx = outs[0]
        for n, val in zip(_MIXER_GAINS, outs[1:8]):
            gs[n][l] = val[0]
        gb['w_in'][l], gb['mla_w_q_b'][l], gb['mla_w_kv_b'][l] = outs[8:11]
        dx = ffn_back('ffn1', sv['x0'], dx, sv['g1'], sv['u1'])
        if reduce_begin is not None:
            begun[l] = reduce_begin(l, {n: gb[n][l] for n in BIG_NAMES})
    return loss, dx, gs, gb, begun, received


def kernel(x, ffn1_norm, ffn1_w_gate, ffn1_w_up, ffn1_w_down, mix_norm, w_in, mla_q_a_norm, mla_w_q_b, mla_kv_a_norm, mla_w_kv_b, mla_q_norm, mla_k_norm, swa_q_norm, swa_k_norm, swa_sinks, mla_out_norm, swa_out_norm, w_o, ffn2_norm, ffn2_w_gate, ffn2_w_up, ffn2_w_down, loss_target, m_ffn1_norm, m_ffn1_w_gate, m_ffn1_w_up, m_ffn1_w_down, m_mix_norm, m_w_in, m_mla_q_a_norm, m_mla_w_q_b, m_mla_kv_a_norm, m_mla_w_kv_b, m_mla_q_norm, m_mla_k_norm, m_swa_q_norm, m_swa_k_norm, m_swa_sinks, m_mla_out_norm, m_swa_out_norm, m_w_o, m_ffn2_norm, m_ffn2_w_gate, m_ffn2_w_up, m_ffn2_w_down, v_ffn1_norm, v_ffn1_w_gate, v_ffn1_w_up, v_ffn1_w_down, v_mix_norm, v_w_in, v_mla_q_a_norm, v_mla_w_q_b, v_mla_kv_a_norm, v_mla_w_kv_b, v_mla_q_norm, v_mla_k_norm, v_swa_q_norm, v_swa_k_norm, v_swa_sinks, v_mla_out_norm, v_swa_out_norm, v_w_o, v_ffn2_norm, v_ffn2_w_gate, v_ffn2_w_up, v_ffn2_w_down):
    args = dict(locals())
    weights = {n: args[n] for n in WEIGHT_NAMES}
    mom_m = {n: args["m_" + n] for n in WEIGHT_NAMES}
    mom_v = {n: args["v_" + n] for n in WEIGHT_NAMES}
    c = lax.axis_index("c")
    chip = 2 * lax.axis_index("x") + lax.axis_index("y")
    halves_of = lambda a: a.reshape(a.shape[:-2] + (2, a.shape[-2] // 2, a.shape[-1]))

    mine = [[halves_of(jnp.concatenate([weights[n][l].astype(BF16) for n in group], axis=0)) for group in GROUPS]
            for l in range(DEPTH)]

    def weights_of(l, gathered):
        if not gathered:
            gathered = _all_gather_halves(mine[l], f"gather_weights_{l}")
        big = {}
        for gi, group in enumerate(GROUPS):
            offs = _group_row_offsets(group)
            _, rh, w = mine[l][gi].shape
            col_sharded = BIG[group[0]][1] == 1
            full = _assemble(gathered[gi].reshape(N_SHARD, 1, 2 * rh, w), mine[l][gi].reshape(1, 2 * rh, w),
                             f"assemble_{gi}_{l}", col_sharded)
            for i, n in enumerate(group):
                big[n] = full[0, offs[i]:offs[i + 1]] if col_sharded else full[0, :, offs[i]:offs[i + 1]].reshape(BIG[n][0])
        return big

    def reduce_begin(l, grads_l):
        parts = [halves_of(jnp.concatenate([grads_l[n] for n in group], axis=1)) for group in GROUPS]
        from_sibling = _sibling_exchange(parts, f"swap_halves_{l}", True)
        chip_sums = []
        for gi, (p, got) in enumerate(zip(parts, from_sibling)):
            kept = lax.dynamic_index_in_dim(p, c, axis=1, keepdims=False)
            rows = N_SHARD * p.shape[2]
            pair = _sum_blocks([kept.reshape(rows, -1), got.reshape(rows, -1)], BF16, f"sum_pair_{gi}_{l}")
            chip_sums.append(pair.reshape(got.shape))
        return chip_sums

    small = {n: weights[n] for n in SMALL_NAMES}
    loss, dx, gs, _, begun, received = _local_step(x[0], loss_target[0], small, weights_of,
                                                   [mine[l + 1] if l + 1 < DEPTH else None for l in range(DEPTH)], reduce_begin)
    received[0] = _scatter_to_chips(begun[0], "scatter_chips_0")

    small_flat = jnp.concatenate([jnp.stack(gs[n]).reshape(-1) for n in SMALL_NAMES] + [loss[0, :1]])
    n_small = small_flat.shape[0]
    lanes = -(-n_small // (8 * 128)) * 128
    small_sum = _all_reduce_small(jnp.pad(small_flat, (0, 8 * lanes - n_small)).reshape(8, lanes), "reduce_small").reshape(-1)
    grads = {}
    off = 0
    for n in SMALL_NAMES:
        cnt = int(np.prod(weights[n].shape))
        grads[n] = small_sum[off:off + cnt].reshape(weights[n].shape)
        off += cnt
    loss_out = small_sum[off]

    halves = []
    for l in range(DEPTH):
        for gi, (cs, got) in enumerate(zip(begun[l], received[l])):
            own = lax.dynamic_index_in_dim(cs, chip, axis=0, keepdims=False)
            halves.append(_sum_blocks([own, got[0], got[1], got[2]], F32, f"sum_chips_{gi}_{l}"))
    others = _sibling_exchange(halves, "share_halves", False)
    for gi, group in enumerate(GROUPS):
        offs = _group_row_offsets(group)
        layers = []
        for l in range(DEPTH):
            mine_h, other_h = halves[l * len(GROUPS) + gi], others[l * len(GROUPS) + gi]
            layers.append(jnp.where(c == 0, jnp.concatenate([mine_h, other_h]), jnp.concatenate([other_h, mine_h])))
        both = jnp.stack(layers)
        for i, n in enumerate(group):
            grads[n] = both[:, offs[i]:offs[i + 1]]

    deltas, new_m, new_v = {}, {}, {}
    for n in WEIGHT_NAMES:
        shp = weights[n].shape
        two_d = (DEPTH, shp[-1]) if len(shp) == 2 else (shp[0] * shp[1], shp[2])
        d, nm, nv = _adamw(weights[n].reshape(two_d), grads[n].reshape(two_d), mom_m[n].reshape(two_d),
                           mom_v[n].reshape(two_d), f"adamw_{n}")
        deltas[n], new_m[n], new_v[n] = d.reshape(shp), nm.reshape(shp), nv.reshape(shp)

    return (loss_out, dx[None], *[grads[n] for n in WEIGHT_NAMES], *[deltas[n] for n in WEIGHT_NAMES],
            *[new_m[n] for n in WEIGHT_NAMES], *[new_v[n] for n in WEIGHT_NAMES])
```

```python
import functools

import numpy as np
import jax
import jax.numpy as jnp
from jax import lax
from jax.experimental import pallas as pl
from jax.experimental.pallas import tpu as pltpu

F32 = jnp.float32
BF16 = jnp.bfloat16

D_MODEL = 1024
DEPTH = 2
EPS = 1e-6
ROPE_THETA = 10000.0
BLOCK = 128
MLA_HEADS = 4
MLA_Q_RANK = 256
MLA_KV_RANK = 128
MLA_NOPE = 128
MLA_ROPE = 64
MLA_V = 128
MLA_QK = MLA_NOPE + MLA_ROPE
MLA_WIDTH = MLA_HEADS * MLA_V
SWA_HEADS = 8
SWA_KV_HEADS = 2
SWA_GROUP = SWA_HEADS // SWA_KV_HEADS
SWA_HEAD_DIM = 64
SWA_WIDTH = SWA_HEADS * SWA_HEAD_DIM
MIX_WIDTH = MLA_WIDTH + SWA_WIDTH
IN_SPLITS = (MLA_Q_RANK, MLA_KV_RANK, MLA_ROPE, SWA_WIDTH, SWA_KV_HEADS * SWA_HEAD_DIM, SWA_KV_HEADS * SWA_HEAD_DIM)
IN_COLS = sum(IN_SPLITS)
IN_OFFS = tuple(int(v) for v in np.cumsum((0,) + IN_SPLITS))
D_FF = 2816
MLA_SCALE = MLA_QK ** -0.5
LOG2E = 1.4426950408889634
LN2 = 0.6931471805599453
MLA_QSCALE = MLA_SCALE * LOG2E
SWA_SCALE = SWA_HEAD_DIM ** -0.5
NEG = -1e30

ADAM_LR = 0.001
ADAM_B1 = 0.9
ADAM_B2 = 0.999
ADAM_EPS = 1e-08
ADAM_WD = 0.01
ADAM_STEP = 10

N_SHARD = 4
N_DEV = 8
VMEM_LIMIT = 56 * 1024 * 1024
MESH = pl.DeviceIdType.MESH

WEIGHT_NAMES = ['ffn1_norm', 'ffn1_w_gate', 'ffn1_w_up', 'ffn1_w_down', 'mix_norm', 'w_in', 'mla_q_a_norm', 'mla_w_q_b',
                'mla_kv_a_norm', 'mla_w_kv_b', 'mla_q_norm', 'mla_k_norm', 'swa_q_norm', 'swa_k_norm', 'swa_sinks',
                'mla_out_norm', 'swa_out_norm', 'w_o', 'ffn2_norm', 'ffn2_w_gate', 'ffn2_w_up', 'ffn2_w_down']
BIG = {'ffn1_w_gate': ((D_MODEL, D_FF), 1), 'ffn1_w_up': ((D_MODEL, D_FF), 1), 'ffn1_w_down': ((D_FF, D_MODEL), 0),
       'w_in': ((D_MODEL, IN_COLS), 1), 'mla_w_q_b': ((MLA_Q_RANK, MLA_HEADS * MLA_QK), 1),
       'mla_w_kv_b': ((MLA_KV_RANK, MLA_HEADS * (MLA_NOPE + MLA_V)), 1), 'w_o': ((MIX_WIDTH, D_MODEL), 0),
       'ffn2_w_gate': ((D_MODEL, D_FF), 1), 'ffn2_w_up': ((D_MODEL, D_FF), 1), 'ffn2_w_down': ((D_FF, D_MODEL), 0)}
BIG_NAMES = [n for n in WEIGHT_NAMES if n in BIG]
SMALL_NAMES = [n for n in WEIGHT_NAMES if n not in BIG]

_pallas_call = pl.pallas_call


def _params(**kw):
    return pltpu.CompilerParams(vmem_limit_bytes=VMEM_LIMIT, **kw)


def _full(shape):
    n = len(shape)
    return pl.BlockSpec(shape, lambda *_: (0,) * n)


def _resident(shape):
    n = len(shape)
    return pl.BlockSpec(shape, lambda *_: (0,) * n, pipeline_mode=pl.Buffered(1))


@jax.custom_vjp
def _mm(a, w):
    return jnp.dot(a.astype(BF16), w, preferred_element_type=F32)


def _mm_fwd(a, w):
    return _mm(a, w), w


def _mm_bwd(w, dy):
    return lax.dot_general(dy.astype(BF16), w, (((1,), (1,)), ((), ())), preferred_element_type=F32), None


_mm.defvjp(_mm_fwd, _mm_bwd)


def _dot_nt(a, b):
    return lax.dot_general(a, b, (((1,), (1,)), ((), ())), preferred_element_type=F32)


def _dot_tn(a, b):
    return lax.dot_general(a, b, (((0,), (0,)), ((), ())), preferred_element_type=F32)


def _rms(t, g):
    return t * lax.rsqrt(jnp.mean(t * t, axis=-1, keepdims=True) + EPS) * g


def _sigmoid(z):
    return 1.0 / (1.0 + jnp.exp(-z))


def _row_tile(s, want):
    return min(want, s)


FF_CHUNK = 1408


def _weight_operand(w):
    if isinstance(w, tuple):
        arr, block, index = w
        return arr, pl.BlockSpec(block, lambda *_: index, pipeline_mode=pl.Buffered(1))
    return w, _resident(w.shape)


def _weight_rows(ref, start, n):
    if len(ref.shape) == 2:
        return ref[start:start + n, :]
    per = ref.shape[1]
    return ref[start // per:(start + n) // per].reshape(n, ref.shape[2])


def _ffn_fwd(x, g, wg, wu, wd, name):
    s = x.shape[0]
    tm = _row_tile(s, 256)
    (wg, wg_spec), (wu, wu_spec), (wd, wd_spec) = _weight_operand(wg), _weight_operand(wu), _weight_operand(wd)

    def body(x_ref, g_ref, wg_ref, wu_ref, wd_ref, y_ref, gate_ref, up_ref):
        xv = x_ref[...]
        nb = _rms(xv, g_ref[...]).astype(BF16)
        acc = xv
        for c in range(0, D_FF, FF_CHUNK):
            gate = jnp.dot(nb, wg_ref[:, c:c + FF_CHUNK], preferred_element_type=F32)
            up = jnp.dot(nb, wu_ref[:, c:c + FF_CHUNK], preferred_element_type=F32)
            gate_ref[:, c:c + FF_CHUNK] = gate.astype(BF16)
            up_ref[:, c:c + FF_CHUNK] = up.astype(BF16)
            act = (gate * _sigmoid(gate) * up).astype(BF16)
            acc = acc + 0.5 * jnp.dot(act, _weight_rows(wd_ref, c, FF_CHUNK), preferred_element_type=F32)
        y_ref[...] = acc

    return _pallas_call(
        body, name=name, grid=(s // tm,),
        in_specs=[pl.BlockSpec((tm, D_MODEL), lambda i: (i, 0)), _full((1, D_MODEL)), wg_spec, wu_spec, wd_spec],
        out_specs=[pl.BlockSpec((tm, D_MODEL), lambda i: (i, 0)), pl.BlockSpec((tm, D_FF), lambda i: (i, 0)),
                   pl.BlockSpec((tm, D_FF), lambda i: (i, 0))],
        out_shape=[jax.ShapeDtypeStruct((s, D_MODEL), F32), jax.ShapeDtypeStruct((s, D_FF), BF16),
                   jax.ShapeDtypeStruct((s, D_FF), BF16)],
        compiler_params=_params(dimension_semantics=("arbitrary",)),
    )(x, g, wg, wu, wd)


def _ffn_bwd(x, dy, gate, up, g, wg, wu, wd, name):
    s = x.shape[0]
    tm = _row_tile(s, 256)
    (wg, wg_spec), (wu, wu_spec), (wd, wd_spec) = _weight_operand(wg), _weight_operand(wu), _weight_operand(wd)

    def body(x_ref, dy_ref, gate_ref, up_ref, g_ref, wg_ref, wu_ref, wd_ref,
             dx_ref, dgain_ref, n_ref, act_ref, dgate_ref, dup_ref):
        i = pl.program_id(0)
        xv = x_ref[...]
        dyv = dy_ref[...]
        gv = g_ref[...]
        r = lax.rsqrt(jnp.mean(xv * xv, axis=-1, keepdims=True) + EPS)
        xh = xv * r
        n_ref[...] = (xh * gv).astype(BF16)
        dyh = (0.5 * dyv).astype(BF16)
        dn = jnp.zeros_like(xv)
        for c in range(0, D_FF, FF_CHUNK):
            dact = _dot_nt(dyh, _weight_rows(wd_ref, c, FF_CHUNK))
            gt = gate_ref[:, c:c + FF_CHUNK].astype(F32)
            u = up_ref[:, c:c + FF_CHUNK].astype(F32)
            sg = _sigmoid(gt)
            sl = gt * sg
            act_ref[:, c:c + FF_CHUNK] = (sl * u).astype(BF16)
            dup = (dact * sl).astype(BF16)
            dgate = (dact * u * (sg * (1.0 + gt * (1.0 - sg)))).astype(BF16)
            dup_ref[:, c:c + FF_CHUNK] = dup
            dgate_ref[:, c:c + FF_CHUNK] = dgate
            dn = dn + _dot_nt(dgate, wg_ref[:, c:c + FF_CHUNK]) + _dot_nt(dup, wu_ref[:, c:c + FF_CHUNK])
        part = jnp.sum(dn * xh, axis=0, keepdims=True)

        @pl.when(i == 0)
        def _():
            dgain_ref[...] = part

        @pl.when(i > 0)
        def _():
            dgain_ref[...] += part

        dxh = dn * gv
        dx_ref[...] = dyv + r * (dxh - xh * jnp.mean(dxh * xh, axis=-1, keepdims=True))

    row = lambda w: pl.BlockSpec((tm, w), lambda i: (i, 0))
    return _pallas_call(
        body, name=name, grid=(s // tm,),
        in_specs=[row(D_MODEL), row(D_MODEL), row(D_FF), row(D_FF), _full((1, D_MODEL)), wg_spec, wu_spec, wd_spec],
        out_specs=[row(D_MODEL), _full((1, D_MODEL)), row(D_MODEL), row(D_FF), row(D_FF), row(D_FF)],
        out_shape=[jax.ShapeDtypeStruct((s, D_MODEL), F32), jax.ShapeDtypeStruct((1, D_MODEL), F32),
                   jax.ShapeDtypeStruct((s, D_MODEL), BF16), jax.ShapeDtypeStruct((s, D_FF), BF16),
                   jax.ShapeDtypeStruct((s, D_FF), BF16), jax.ShapeDtypeStruct((s, D_FF), BF16)],
        compiler_params=_params(dimension_semantics=("arbitrary",)),
    )(x, dy, gate, up, g, wg, wu, wd)


def _store_col_shards(o_ref, acc, first_shard, n_here, width):
    for q in range(n_here):
        o_ref[q] = acc[:, (first_shard + q) * width:(first_shard + q + 1) * width].astype(BF16)


def _matmul_tn(a, b, scale, name, col_shards):
    t, m = a.shape
    n = b.shape[1]
    tk = _row_tile(t, 2048)
    tn = n // 2
    per = n // N_SHARD
    nk = t // tk

    def body(a_ref, b_ref, o_ref, acc_ref):
        k = pl.program_id(1)
        bv = b_ref[...]
        if scale != 1.0:
            bv = bv.astype(F32) * scale
        part = _dot_tn(a_ref[...].astype(BF16), bv.astype(BF16))

        @pl.when(k == 0)
        def _():
            acc_ref[...] = part

        @pl.when(k > 0)
        def _():
            acc_ref[...] += part

        @pl.when(k == nk - 1)
        def _():
            if col_shards:
                _store_col_shards(o_ref, acc_ref[...], 0, tn // per, per)
            else:
                o_ref[...] = acc_ref[...].astype(BF16)

    if col_shards:
        out_spec = pl.BlockSpec((tn // per, m, per), lambda j, k: (j, 0, 0))
        out_shape = jax.ShapeDtypeStruct((N_SHARD, m, per), BF16)
    else:
        out_spec = pl.BlockSpec((m, tn), lambda j, k: (0, j))
        out_shape = jax.ShapeDtypeStruct((m, n), BF16)
    return _pallas_call(
        body, name=name, grid=(n // tn, nk),
        in_specs=[pl.BlockSpec((tk, m), lambda j, k: (k, 0)), pl.BlockSpec((tk, tn), lambda j, k: (k, j))],
        out_specs=out_spec, out_shape=out_shape, scratch_shapes=[pltpu.VMEM((m, tn), F32)],
        compiler_params=_params(dimension_semantics=("arbitrary", "arbitrary")),
    )(a, b)


_HALF = SWA_HEAD_DIM // 2
_IN_ORDER = (list(range(0, IN_OFFS[2]))
             + [IN_OFFS[3] + SWA_HEAD_DIM * h + i for h in range(SWA_HEADS) for i in range(_HALF)]
             + [IN_OFFS[3] + SWA_HEAD_DIM * h + _HALF + i for h in range(SWA_HEADS) for i in range(_HALF)]
             + list(range(IN_OFFS[5], IN_OFFS[6]))
             + [IN_OFFS[4] + SWA_HEAD_DIM * j + i for j in range(SWA_KV_HEADS) for i in range(_HALF)]
             + [IN_OFFS[4] + SWA_HEAD_DIM * j + _HALF + i for j in range(SWA_KV_HEADS) for i in range(_HALF)]
             + list(range(IN_OFFS[2], IN_OFFS[3])))
_QB_ORDER = ([MLA_QK * h + i for h in range(MLA_HEADS) for i in range(MLA_NOPE)]
             + [MLA_QK * h + MLA_NOPE + i for h in range(MLA_HEADS) for i in range(_HALF)]
             + [MLA_QK * h + MLA_NOPE + _HALF + i for h in range(MLA_HEADS) for i in range(_HALF)])
_KVB_ORDER = ([(MLA_NOPE + MLA_V) * h + i for h in range(MLA_HEADS) for i in range(MLA_NOPE)]
              + [(MLA_NOPE + MLA_V) * h + MLA_NOPE + i for h in range(MLA_HEADS) for i in range(MLA_V)])
_P_CQ, _P_CKV, _P_QA, _P_QB, _P_VS, _P_KA, _P_KB, _P_PE = (int(v) for v in np.cumsum(
    (0, MLA_Q_RANK, MLA_KV_RANK, SWA_WIDTH // 2, SWA_WIDTH // 2, IN_SPLITS[5], IN_SPLITS[4] // 2, IN_SPLITS[4] // 2)))


def _runs(order):
    out, start = [], 0
    for i in range(1, len(order) + 1):
        if i == len(order) or order[i] != order[i - 1] + 1:
            out.append((order[start], i - start))
            start = i
    return out


def _inverse(order):
    inv = [0] * len(order)
    for new, old in enumerate(order):
        inv[old] = new
    return inv


def _take_cols(a, order):
    return jnp.concatenate([a[..., st:st + w] for st, w in _runs(order)], axis=-1)


def _segment_matrix(n, seg):
    return (lax.broadcasted_iota(jnp.int32, (n, n), 0) // seg == lax.broadcasted_iota(jnp.int32, (n, n), 1) // seg).astype(BF16)


@jax.custom_vjp
def _cmm(t, b, bt):
    hi = t.astype(BF16)
    lo = (t - hi.astype(F32)).astype(BF16)
    return jnp.dot(hi, b, preferred_element_type=F32) + jnp.dot(lo, b, preferred_element_type=F32)


def _cmm_fwd(t, b, bt):
    return _cmm(t, b, bt), (b, bt)


def _cmm_bwd(res, dy):
    b, bt = res
    return _cmm(dy, bt, b), None, None


_cmm.defvjp(_cmm_fwd, _cmm_bwd)


def _segsum(t, b):
    return _cmm(t, b, b)


def _rowsum(t):
    n = t.shape[-1]
    return _cmm(t, jnp.ones((n, 128), BF16), jnp.ones((128, n), BF16))


def _by_head(vals, width):
    lane = lax.broadcasted_iota(jnp.int32, (vals[0].shape[0], len(vals) * width), 1)
    out = vals[-1]
    for hd in range(len(vals) - 2, -1, -1):
        out = jnp.where(lane < (hd + 1) * width, vals[hd], out)
    return out


def _rope2(a, b, cos, sin):
    return a * cos - b * sin, b * cos + a * sin


def _pre_math(x, gm, gqa, gkva, gq, gk, gsq, gsk, taps, win, wqb, wkvb, cos, sin):
    h = _rms(x, gm)
    proj = _mm(h, win)
    if taps is not None:
        proj = proj + taps[0]
    cqn = _rms(proj[:, _P_CQ:_P_CKV], gqa)
    qa_all = _mm(cqn, wqb)
    ckvn = _rms(proj[:, _P_CKV:_P_QA], gkva)
    kv_all = _mm(ckvn, wkvb)
    if taps is not None:
        qa_all = qa_all + taps[1]
        kv_all = kv_all + taps[2]
    nh, hw = MLA_HEADS, MLA_HEADS * _HALF
    seg_mla = _segment_matrix(hw, _HALF)
    tile = lambda g, n: jnp.concatenate([g] * n, axis=-1)
    c4, s4 = cos[:, :hw], sin[:, :hw]

    def mla_heads(nope, r1, r2, gain):
        rr = r1 * r1 + r2 * r2
        lane_head = lax.broadcasted_iota(jnp.int32, (hw, nh * MLA_NOPE), 0) // _HALF
        spread = (lane_head == lax.broadcasted_iota(jnp.int32, (hw, nh * MLA_NOPE), 1) // MLA_NOPE).astype(BF16)
        rope_on_nope = _cmm(rr, spread, spread.T)
        ss_nope = [_rowsum(jnp.square(nope[:, hd * MLA_NOPE:(hd + 1) * MLA_NOPE])) for hd in range(nh)]
        rinv = [lax.rsqrt((ss_nope[hd] + rope_on_nope[:, hd * MLA_NOPE:(hd + 1) * MLA_NOPE]) * (1.0 / MLA_QK) + EPS)
                for hd in range(nh)]
        rl = lax.rsqrt((_segsum(rr, seg_mla) + _by_head(ss_nope, _HALF)) * (1.0 / MLA_QK) + EPS)
        o1, o2 = _rope2(r1 * rl * tile(gain[:, MLA_NOPE:MLA_NOPE + _HALF], nh), r2 * rl * tile(gain[:, MLA_NOPE + _HALF:], nh), c4, s4)
        return [jnp.concatenate([nope[:, hd * MLA_NOPE:(hd + 1) * MLA_NOPE] * rinv[hd] * gain[:, :MLA_NOPE],
                                 o1[:, hd * _HALF:(hd + 1) * _HALF], o2[:, hd * _HALF:(hd + 1) * _HALF]], axis=-1)
                for hd in range(nh)]

    q_a = mla_heads(qa_all[:, :nh * MLA_NOPE], qa_all[:, nh * MLA_NOPE:nh * MLA_NOPE + hw], qa_all[:, nh * MLA_NOPE + hw:], gq)
    pe1, pe2 = proj[:, _P_PE:_P_PE + _HALF], proj[:, _P_PE + _HALF:_P_PE + 2 * _HALF]
    k_a = mla_heads(kv_all[:, :nh * MLA_NOPE], tile(pe1, nh), tile(pe2, nh), gk)
    v_a = [kv_all[:, nh * MLA_NOPE + hd * MLA_V:nh * MLA_NOPE + (hd + 1) * MLA_V] for hd in range(nh)]

    def swa_heads(a, b, gain, n):
        w = n * _HALF
        r = lax.rsqrt(_segsum(a * a + b * b, _segment_matrix(w, _HALF)) * (1.0 / SWA_HEAD_DIM) + EPS)
        o1, o2 = _rope2(a * r * tile(gain[:, :_HALF], n), b * r * tile(gain[:, _HALF:], n), cos[:, :w], sin[:, :w])
        return [jnp.concatenate([o1[:, hd * _HALF:(hd + 1) * _HALF], o2[:, hd * _HALF:(hd + 1) * _HALF]], axis=-1) for hd in range(n)]

    q_b = swa_heads(proj[:, _P_QA:_P_QB], proj[:, _P_QB:_P_VS], gsq, SWA_HEADS)
    k_b = swa_heads(proj[:, _P_KA:_P_KB], proj[:, _P_KB:_P_PE], gsk, SWA_KV_HEADS)
    v_b = [proj[:, _P_VS + j * SWA_HEAD_DIM:_P_VS + (j + 1) * SWA_HEAD_DIM] for j in range(SWA_KV_HEADS)]
    return (q_a, k_a, v_a, q_b, k_b, v_b), (h, cqn, ckvn)


_PRE_GAIN_WIDTHS = (D_MODEL, MLA_Q_RANK, MLA_KV_RANK, MLA_QK, MLA_QK, SWA_HEAD_DIM, SWA_HEAD_DIM)
_PRE_HEADS = ((MLA_HEADS, MLA_QK), (MLA_HEADS, MLA_QK), (MLA_HEADS, MLA_V),
              (SWA_HEADS, SWA_HEAD_DIM), (SWA_KV_HEADS, SWA_HEAD_DIM), (SWA_KV_HEADS, SWA_HEAD_DIM))


def _pre_fwd(x, gains, win, wqb, wkvb, cos, sin, name):
    s = x.shape[0]
    tm = _row_tile(s, 512)

    def body(x_ref, *refs):
        g_refs, (win_ref, wqb_ref, wkvb_ref, cos_ref, sin_ref), out_refs = refs[:7], refs[7:12], refs[12:]
        outs, _ = _pre_math(x_ref[...], *[g[...] for g in g_refs], None, win_ref[...], wqb_ref[...], wkvb_ref[...],
                            cos_ref[...], sin_ref[...])
        for idx, (ref, heads) in enumerate(zip(out_refs, outs)):
            for hd, val in enumerate(heads):
                ref[hd] = (val * MLA_QSCALE if idx == 0 else val).astype(BF16)

    heads_spec = lambda nh, w: pl.BlockSpec((nh, tm, w), lambda i: (0, i, 0))
    return _pallas_call(
        body, name=name, grid=(s // tm,),
        in_specs=[pl.BlockSpec((tm, D_MODEL), lambda i: (i, 0))] + [_full((1, w)) for w in _PRE_GAIN_WIDTHS]
        + [_resident(win.shape), _resident(wqb.shape), _resident(wkvb.shape),
           pl.BlockSpec((tm, SWA_HEADS * _HALF), lambda i: (i, 0)), pl.BlockSpec((tm, SWA_HEADS * _HALF), lambda i: (i, 0))],
        out_specs=[heads_spec(nh, w) for nh, w in _PRE_HEADS],
        out_shape=[jax.ShapeDtypeStruct((nh, s, w), BF16) for nh, w in _PRE_HEADS],
        compiler_params=_params(dimension_semantics=("arbitrary",)),
    )(x, *gains, win, wqb, wkvb, cos, sin)


def _pre_bwd(x, dx_res, cts, gains, win, wqb, wkvb, cos, sin, name):
    s = x.shape[0]
    tm = _row_tile(s, 256)
    tap_widths = (IN_COLS, MLA_HEADS * MLA_QK, MLA_HEADS * (MLA_NOPE + MLA_V))

    def body(x_ref, dxr_ref, *refs):
        ct_refs, g_refs = refs[:6], refs[6:13]
        win_ref, wqb_ref, wkvb_ref, cos_ref, sin_ref = refs[13:18]
        dx_ref, dg_refs, dw_refs, acc_refs = refs[18], refs[19:26], refs[26:29], refs[29:32]
        i = pl.program_id(0)
        win_v, wqb_v, wkvb_v, cos_v, sin_v = win_ref[...], wqb_ref[...], wkvb_ref[...], cos_ref[...], sin_ref[...]

        def f(xv, gm, gqa, gkva, gq, gk, gsq, gsk, t0, t1, t2):
            return _pre_math(xv, gm, gqa, gkva, gq, gk, gsq, gsk, (t0, t1, t2), win_v, wqb_v, wkvb_v, cos_v, sin_v)

        taps = [jnp.zeros((tm, w), F32) for w in tap_widths]
        _, vjp, acts = jax.vjp(f, x_ref[...], *[g[...] for g in g_refs], *taps, has_aux=True)
        ct = tuple([ref[hd] for hd in range(nh)] for ref, (nh, _) in zip(ct_refs, _PRE_HEADS))
        grads = vjp(ct)
        dx_ref[...] = grads[0] + dxr_ref[...]
        dws = [_dot_tn(a.astype(BF16), t.astype(BF16)) for a, t in zip(acts, grads[8:11])]

        @pl.when(i == 0)
        def _():
            for ref, val in zip(dg_refs, grads[1:8]):
                ref[...] = val
            for ref, val in zip(acc_refs, dws):
                ref[...] = val

        @pl.when(i > 0)
        def _():
            for ref, val in zip(dg_refs, grads[1:8]):
                ref[...] += val
            for ref, val in zip(acc_refs, dws):
                ref[...] += val

        @pl.when(i == s // tm - 1)
        def _():
            for ref, acc, order in zip(dw_refs, acc_refs, (_IN_ORDER, _QB_ORDER, _KVB_ORDER)):
                _store_col_shards(ref, _take_cols(acc[...], _inverse(order)), 0, N_SHARD, acc.shape[1] // N_SHARD)

    heads_spec = lambda nh, w: pl.BlockSpec((nh, tm, w), lambda i: (0, i, 0))
    row = pl.BlockSpec((tm, D_MODEL), lambda i: (i, 0))
    half = pl.BlockSpec((tm, SWA_HEADS * _HALF), lambda i: (i, 0))
    shard_shapes = [(N_SHARD, w.shape[0], w.shape[1] // N_SHARD) for w in (win, wqb, wkvb)]
    return _pallas_call(
        body, name=name, grid=(s // tm,),
        in_specs=[row, row] + [heads_spec(nh, w) for nh, w in _PRE_HEADS] + [_full((1, w)) for w in _PRE_GAIN_WIDTHS]
        + [_resident(win.shape), _resident(wqb.shape), _resident(wkvb.shape), half, half],
        out_specs=[row] + [_full((1, w)) for w in _PRE_GAIN_WIDTHS] + [_full(shp) for shp in shard_shapes],
        out_shape=[jax.ShapeDtypeStruct((s, D_MODEL), F32)] + [jax.ShapeDtypeStruct((1, w), F32) for w in _PRE_GAIN_WIDTHS]
        + [jax.ShapeDtypeStruct(shp, BF16) for shp in shard_shapes],
        scratch_shapes=[pltpu.VMEM(w.shape, F32) for w in (win, wqb, wkvb)],
        compiler_params=_params(dimension_semantics=("arbitrary",)),
    )(x, dx_res, *cts, *gains, win, wqb, wkvb, cos, sin)


def _post_math(oa, ob, ga, gb, wo):
    mixed = jnp.concatenate([_rms(jnp.concatenate(oa, axis=-1), ga), _rms(jnp.concatenate(ob, axis=-1), gb)], axis=-1)
    return _mm(mixed, wo), mixed


def _post_fwd(x, oa, ob, ga, gb, wo, name):
    s = x.shape[0]
    tm = _row_tile(s, 512)

    def body(x_ref, oa_ref, ob_ref, ga_ref, gb_ref, wo_ref, y_ref):
        y, _ = _post_math([oa_ref[hd] for hd in range(MLA_HEADS)], [ob_ref[hd] for hd in range(SWA_HEADS)],
                          ga_ref[...], gb_ref[...], wo_ref[...])
        y_ref[...] = x_ref[...] + y

    row = pl.BlockSpec((tm, D_MODEL), lambda i: (i, 0))
    return _pallas_call(
        body, name=name, grid=(s // tm,),
        in_specs=[row, pl.BlockSpec((MLA_HEADS, tm, MLA_V), lambda i: (0, i, 0)),
                  pl.BlockSpec((SWA_HEADS, tm, SWA_HEAD_DIM), lambda i: (0, i, 0)),
                  _full((1, MLA_WIDTH)), _full((1, SWA_WIDTH)), _resident(wo.shape)],
        out_specs=row, out_shape=jax.ShapeDtypeStruct((s, D_MODEL), F32),
        compiler_params=_params(dimension_semantics=("arbitrary",)),
    )(x, oa, ob, ga, gb, wo)


def _post_bwd(dy, oa, ob, ga, gb, wo, name):
    s = dy.shape[0]
    tm = _row_tile(s, 512)

    def body(dy_ref, oa_ref, ob_ref, ga_ref, gb_ref, wo_ref, doa_ref, dob_ref, dga_ref, dgb_ref, dwo_ref, acc_ref):
        i = pl.program_id(0)
        wo_v = wo_ref[...]
        dyv = dy_ref[...]

        def f(oa_l, ob_l, ga_v, gb_v):
            return _post_math(oa_l, ob_l, ga_v, gb_v, wo_v)

        _, vjp, mixed = jax.vjp(f, [oa_ref[hd] for hd in range(MLA_HEADS)], [ob_ref[hd] for hd in range(SWA_HEADS)],
                                ga_ref[...], gb_ref[...], has_aux=True)
        doa, dob, dga, dgb = vjp(dyv)
        for hd in range(MLA_HEADS):
            doa_ref[hd] = doa[hd]
        for hd in range(SWA_HEADS):
            dob_ref[hd] = dob[hd]
        dwo = _dot_tn(mixed.astype(BF16), dyv.astype(BF16))

        @pl.when(i == 0)
        def _():
            dga_ref[...] = dga
            dgb_ref[...] = dgb
            acc_ref[...] = dwo

        @pl.when(i > 0)
        def _():
            dga_ref[...] += dga
            dgb_ref[...] += dgb
            acc_ref[...] += dwo

        @pl.when(i == s // tm - 1)
        def _():
            dwo_ref[...] = acc_ref[...].astype(BF16)

    row = pl.BlockSpec((tm, D_MODEL), lambda i: (i, 0))
    oa_spec = pl.BlockSpec((MLA_HEADS, tm, MLA_V), lambda i: (0, i, 0))
    ob_spec = pl.BlockSpec((SWA_HEADS, tm, SWA_HEAD_DIM), lambda i: (0, i, 0))
    return _pallas_call(
        body, name=name, grid=(s // tm,),
        in_specs=[row, oa_spec, ob_spec, _full((1, MLA_WIDTH)), _full((1, SWA_WIDTH)), _resident(wo.shape)],
        out_specs=[oa_spec, ob_spec, _full((1, MLA_WIDTH)), _full((1, SWA_WIDTH)), _full(wo.shape)],
        out_shape=[jax.ShapeDtypeStruct((MLA_HEADS, s, MLA_V), F32), jax.ShapeDtypeStruct((SWA_HEADS, s, SWA_HEAD_DIM), F32),
                   jax.ShapeDtypeStruct((1, MLA_WIDTH), F32), jax.ShapeDtypeStruct((1, SWA_WIDTH), F32),
                   jax.ShapeDtypeStruct(wo.shape, BF16)],
        scratch_shapes=[pltpu.VMEM(wo.shape, F32)],
        compiler_params=_params(dimension_semantics=("arbitrary",)),
    )(dy, oa, ob, ga, gb, wo)


def _attn_tile(s):
    return 512 if s >= 2048 else 128


def _causal_mask(t):
    return lax.broadcasted_iota(jnp.int32, (t, t), 1) <= lax.broadcasted_iota(jnp.int32, (t, t), 0)


def _pipelined_blocks(first, count, last_block, issue, consume, carry, prefetch_after):
    def clamped(j, slot):
        issue(jnp.minimum(j, last_block), slot)

    def pair(jj, c):
        a = first + 2 * jj
        clamped(a + 1, 1)
        c = consume(a, 0, c)
        clamped(a + 2, 0)
        return consume(a + 1, 1, c)

    clamped(first, 0)
    npairs = count // 2
    carry = lax.fori_loop(0, npairs, pair, carry)

    def odd(c):
        c = consume(first + 2 * npairs, 0, c)
        if prefetch_after:
            clamped(first + count, 0)
        return c

    return lax.cond(count - 2 * npairs == 1, odd, lambda c: c, carry)


def _run_stages_at(stages, steps):
    for stage, (s0, s1) in zip(stages, steps):
        pl.when((pl.program_id(0) == s0) & (pl.program_id(1) == s1))(stage)


def _mla_fwd(q, k, v, name, gather=None):
    nh, s, _ = q.shape
    t = _attn_tile(s)
    nq = s // t
    ng = len(gather) if gather else 0

    def body(q_ref, k_ref, v_ref, *rest):
        g_ins, (o_ref, lse_ref), g_outs = rest[:ng], rest[ng:ng + 2], rest[ng + 2:2 * ng + 2]
        (s0_ref, s1_ref), sems = rest[2 * ng + 2:2 * ng + 4], rest[2 * ng + 4:]
        if ng:
            _run_stages_at(_gather_stages(g_ins, g_outs, *sems), [(0, 0), (nh // 2, 0), (nh - 1, 0), (nh - 1, nq - 1)])
        qi = pl.program_id(1)
        qv = q_ref[...]
        s_refs = (s0_ref, s1_ref)

        def rows(j):
            return pl.ds(pl.multiple_of(j * t, t), t)

        def issue(j, slot):
            s_refs[slot][...] = _dot_nt(qv, k_ref[rows(j), :])

        def consume(j, slot, carry, masked=False):
            m, l, acc = carry
            sc = s_refs[slot][...]
            if masked:
                sc = jnp.where(_causal_mask(t), sc, NEG)
            m_new = jnp.maximum(m, jnp.max(sc, axis=-1, keepdims=True))
            alpha = jnp.exp2(m - m_new)
            p = jnp.exp2(sc - m_new)
            l = alpha * l + jnp.sum(p, axis=-1, keepdims=True)
            acc = alpha * acc + jnp.dot(p.astype(BF16), v_ref[rows(j), :], preferred_element_type=F32)
            return m_new, l, acc

        init = (jnp.full((t, 1), NEG, F32), jnp.zeros((t, 1), F32), jnp.zeros((t, MLA_V), F32))
        carry = _pipelined_blocks(0, qi, nq - 1, issue, consume, init, True)
        m, l, acc = consume(qi, 0, carry, masked=True)
        o_ref[...] = acc / l
        lse_ref[...] = m + jnp.log2(l)

    outs = _pallas_call(
        body, name=name, grid=(nh, nq),
        in_specs=[pl.BlockSpec((None, t, MLA_QK), lambda h, i: (h, i, 0)), pl.BlockSpec((None, s, MLA_QK), lambda h, i: (h, 0, 0)),
                  pl.BlockSpec((None, s, MLA_V), lambda h, i: (h, 0, 0))] + [_HBM] * ng,
        out_specs=[pl.BlockSpec((None, t, MLA_V), lambda h, i: (h, i, 0)), pl.BlockSpec((None, t, 1), lambda h, i: (h, i, 0))]
        + [_HBM] * ng,
        out_shape=[jax.ShapeDtypeStruct((nh, s, MLA_V), F32), jax.ShapeDtypeStruct((nh, s, 1), F32)]
        + (_gather_out_shapes(gather) if ng else []),
        scratch_shapes=[pltpu.VMEM((t, t), F32)] * 2 + (_exchange_scratch(8, ng) if ng else []),
        compiler_params=_params(dimension_semantics=("arbitrary", "arbitrary")),
    )(q, k, v, *(gather or []))
    return outs[0], outs[1], outs[2:]


def _mla_bwd_dq(q, k, v, o, do, lse, name):
    nh, s, _ = q.shape
    t = _attn_tile(s)
    nq = s // t

    def body(q_ref, k_ref, v_ref, o_ref, do_ref, lse_ref, dq_ref, delta_ref, s0_ref, s1_ref, dp0_ref, dp1_ref):
        qi = pl.program_id(1)
        qv = q_ref[...]
        dov = do_ref[...]
        lse = lse_ref[...]
        delta = jnp.sum(dov * o_ref[...], axis=-1, keepdims=True)
        delta_ref[...] = delta
        dob = dov.astype(BF16)
        s_refs, dp_refs = (s0_ref, s1_ref), (dp0_ref, dp1_ref)

        def rows(j):
            return pl.ds(pl.multiple_of(j * t, t), t)

        def issue(j, slot):
            s_refs[slot][...] = _dot_nt(qv, k_ref[rows(j), :])
            dp_refs[slot][...] = _dot_nt(dob, v_ref[rows(j), :])

        def consume(j, slot, dq, masked=False):
            p = jnp.exp2(s_refs[slot][...] - lse)
            if masked:
                p = jnp.where(_causal_mask(t), p, 0.0)
            ds = p * (dp_refs[slot][...] - delta)
            return dq + jnp.dot(ds.astype(BF16), k_ref[rows(j), :], preferred_element_type=F32)

        dq = _pipelined_blocks(0, qi, nq - 1, issue, consume, jnp.zeros((t, MLA_QK), F32), True)
        dq_ref[...] = consume(qi, 0, dq, masked=True) * MLA_SCALE

    tile = lambda w: pl.BlockSpec((None, t, w), lambda h, i: (h, i, 0))
    whole = lambda w: pl.BlockSpec((None, s, w), lambda h, i: (h, 0, 0))
    return _pallas_call(
        body, name=name, grid=(nh, nq),
        in_specs=[tile(MLA_QK), whole(MLA_QK), whole(MLA_V), tile(MLA_V), tile(MLA_V), tile(1)],
        out_specs=[tile(MLA_QK), tile(1)],
        out_shape=[jax.ShapeDtypeStruct((nh, s, MLA_QK), F32), jax.ShapeDtypeStruct((nh, s, 1), F32)],
        scratch_shapes=[pltpu.VMEM((t, t), F32)] * 4,
        compiler_params=_params(dimension_semantics=("arbitrary", "arbitrary")),
    )(q, k, v, o, do, lse)


def _mla_bwd_dkv(q, k, v, do, lse_row, delta_row, name, scatter=None):
    nh, s, _ = q.shape
    t = _attn_tile(s)
    nq = s // t
    ng = len(scatter) if scatter else 0

    def body(q_ref, k_ref, v_ref, do_ref, lse_ref, delta_ref, *rest):
        c_ins, (dk_ref, dv_ref), c_outs = rest[:ng], rest[ng:ng + 2], rest[ng + 2:2 * ng + 2]
        (s0_ref, s1_ref, dp0_ref, dp1_ref), sems = rest[2 * ng + 2:2 * ng + 6], rest[2 * ng + 6:]
        if ng:
            _run_stages_at(_scatter_stages(c_ins, c_outs, *sems), [(0, 0), (nh - 1, nq - 1)])
        kj = pl.program_id(1)
        kv_, vv = k_ref[...], v_ref[...]
        s_refs, dp_refs = (s0_ref, s1_ref), (dp0_ref, dp1_ref)

        def rows(i):
            return pl.ds(pl.multiple_of(i * t, t), t)

        def issue(i, slot):
            s_refs[slot][...] = _dot_nt(kv_, q_ref[rows(i), :])
            dp_refs[slot][...] = _dot_nt(vv, do_ref[rows(i), :].astype(BF16))

        def consume(i, slot, carry, masked=False):
            dk, dv = carry
            p = jnp.exp2(s_refs[slot][...] - lse_ref[pl.ds(i, 1), :])
            if masked:
                p = jnp.where(lax.broadcasted_iota(jnp.int32, (t, t), 0) <= lax.broadcasted_iota(jnp.int32, (t, t), 1), p, 0.0)
            dv = dv + jnp.dot(p.astype(BF16), do_ref[rows(i), :].astype(BF16), preferred_element_type=F32)
            ds = p * (dp_refs[slot][...] - delta_ref[pl.ds(i, 1), :])
            dk = dk + jnp.dot(ds.astype(BF16), q_ref[rows(i), :], preferred_element_type=F32)
            return dk, dv

        issue(kj, 0)
        carry = consume(kj, 0, (jnp.zeros((t, MLA_QK), F32), jnp.zeros((t, MLA_V), F32)), masked=True)
        dk, dv = _pipelined_blocks(kj + 1, nq - 1 - kj, nq - 1, issue, consume, carry, False)
        dk_ref[...] = dk * LN2
        dv_ref[...] = dv

    tile = lambda w: pl.BlockSpec((None, t, w), lambda h, j: (h, j, 0))
    whole = lambda w: pl.BlockSpec((None, s, w), lambda h, j: (h, 0, 0))
    rows_spec = pl.BlockSpec((None, nq, t), lambda h, j: (h, 0, 0))
    outs = _pallas_call(
        body, name=name, grid=(nh, nq),
        in_specs=[whole(MLA_QK), tile(MLA_QK), tile(MLA_V), whole(MLA_V), rows_spec, rows_spec] + [_HBM] * ng,
        out_specs=[tile(MLA_QK), tile(MLA_V)] + [_HBM] * ng,
        out_shape=[jax.ShapeDtypeStruct((nh, s, MLA_QK), F32), jax.ShapeDtypeStruct((nh, s, MLA_V), F32)]
        + (_scatter_out_shapes(scatter) if ng else []),
        scratch_shapes=[pltpu.VMEM((t, t), F32)] * 4 + (_exchange_scratch(3, ng) if ng else []),
        compiler_params=_params(dimension_semantics=("arbitrary", "arbitrary")),
    )(q, k, v, do, lse_row, delta_row, *(scatter or []))
    return outs[0], outs[1], outs[2:]


def _swa_tile(s):
    return min(s, 4 * BLOCK)


def _swa_specs(tq):
    nb = tq // BLOCK
    grp = lambda w: pl.BlockSpec((SWA_GROUP, tq, w), lambda j, i: (j, i, 0))
    main = pl.BlockSpec((None, tq, SWA_HEAD_DIM), lambda j, i: (j, i, 0))
    tail = pl.BlockSpec((None, BLOCK, SWA_HEAD_DIM), lambda j, i: (j, nb * (i + 1), 0))
    sink = pl.BlockSpec((None, SWA_GROUP, 128), lambda j, i: (j, 0, 0))
    return grp, main, tail, sink


def _swa_band_mask(first):
    shape = (SWA_GROUP * BLOCK, 2 * BLOCK)
    q_rel = (lax.broadcasted_iota(jnp.int32, shape, 0) & (BLOCK - 1)) + BLOCK
    k_rel = lax.broadcasted_iota(jnp.int32, shape, 1)
    dist = q_rel - k_rel
    return (dist >= 0) & (dist < BLOCK) & ((k_rel >= BLOCK) | jnp.logical_not(first))


def _swa_sink_column(sink_ref):
    sk = sink_ref[...]
    return jnp.concatenate([jnp.broadcast_to(sk[g:g + 1, 0:1], (BLOCK, 1)) for g in range(SWA_GROUP)], axis=0)


def _swa_fwd(q, kpad, vpad, sinks, name):
    _, s, _ = q.shape
    tq = _swa_tile(s)
    grp, main, tail, sink = _swa_specs(tq)
    d = SWA_HEAD_DIM

    def body(q_ref, km_ref, kt_ref, vm_ref, vt_ref, sink_ref, o_ref, lse_ref):
        i = pl.program_id(1)
        kall = jnp.concatenate([km_ref[...], kt_ref[...]], axis=0)
        vall = jnp.concatenate([vm_ref[...], vt_ref[...]], axis=0)
        sink_col = _swa_sink_column(sink_ref)
        for b in range(tq // BLOCK):
            lo = b * BLOCK
            valid = _swa_band_mask(i == 0 if b == 0 else False)
            q4 = q_ref[:, lo:lo + BLOCK, :].reshape(SWA_GROUP * BLOCK, d)
            sc = jnp.where(valid, _dot_nt(q4, kall[lo:lo + 2 * BLOCK]) * SWA_SCALE, NEG)
            m = jnp.maximum(jnp.max(sc, axis=-1, keepdims=True), sink_col)
            e = jnp.exp(sc - m)
            den = jnp.sum(e, axis=-1, keepdims=True) + jnp.exp(sink_col - m)
            out = jnp.dot((e * (1.0 / den)).astype(BF16), vall[lo:lo + 2 * BLOCK], preferred_element_type=F32)
            o_ref[:, lo:lo + BLOCK, :] = out.reshape(SWA_GROUP, BLOCK, d)
            lse_ref[:, lo:lo + BLOCK, :] = (m + jnp.log(den)).reshape(SWA_GROUP, BLOCK, 1)

    return _pallas_call(
        body, name=name, grid=(SWA_KV_HEADS, s // tq),
        in_specs=[grp(d), main, tail, main, tail, sink], out_specs=[grp(d), grp(1)],
        out_shape=[jax.ShapeDtypeStruct((SWA_HEADS, s, d), F32), jax.ShapeDtypeStruct((SWA_HEADS, s, 1), F32)],
        compiler_params=_params(dimension_semantics=("arbitrary", "arbitrary")),
    )(q, kpad, kpad, vpad, vpad, sinks)


def _swa_bwd(q, kpad, vpad, sinks, o, lse, do, name):
    _, s, _ = q.shape
    tq = _swa_tile(s)
    grp, main, tail, sink = _swa_specs(tq)
    d = SWA_HEAD_DIM

    def body(q_ref, km_ref, kt_ref, vm_ref, vt_ref, sink_ref, o_ref, lse_ref, do_ref, dq_ref, dk_ref, dv_ref, dsink_ref):
        i = pl.program_id(1)
        kall = jnp.concatenate([km_ref[...], kt_ref[...]], axis=0)
        vall = jnp.concatenate([vm_ref[...], vt_ref[...]], axis=0)
        sink_col = _swa_sink_column(sink_ref)

        @pl.when(i == 0)
        def _():
            dk_ref[...] = jnp.zeros_like(dk_ref)
            dv_ref[...] = jnp.zeros_like(dv_ref)
            dsink_ref[...] = jnp.zeros_like(dsink_ref)

        dsink = jnp.zeros((SWA_GROUP * BLOCK, 1), F32)
        for b in range(tq // BLOCK):
            lo = b * BLOCK
            valid = _swa_band_mask(i == 0 if b == 0 else False)
            rows4 = SWA_GROUP * BLOCK
            q4 = q_ref[:, lo:lo + BLOCK, :].reshape(rows4, d)
            do4 = do_ref[:, lo:lo + BLOCK, :].reshape(rows4, d)
            lse4 = lse_ref[:, lo:lo + BLOCK, :].reshape(rows4, 1)
            delta = jnp.sum(do4 * o_ref[:, lo:lo + BLOCK, :].reshape(rows4, d), axis=-1, keepdims=True)
            kb, vb = kall[lo:lo + 2 * BLOCK], vall[lo:lo + 2 * BLOCK]
            do4b = do4.astype(BF16)
            p = jnp.where(valid, jnp.exp(_dot_nt(q4, kb) * SWA_SCALE - lse4), 0.0)
            ds = (p * (_dot_nt(do4b, vb) - delta) * SWA_SCALE).astype(BF16)
            dq_ref[:, lo:lo + BLOCK, :] = jnp.dot(ds, kb, preferred_element_type=F32).reshape(SWA_GROUP, BLOCK, d)
            band = pl.ds(pl.multiple_of(i * tq, BLOCK) + lo, 2 * BLOCK)
            dk_ref[band, :] += _dot_tn(ds, q4)
            dv_ref[band, :] += _dot_tn(p.astype(BF16), do4b)
            dsink = dsink - jnp.exp(sink_col - lse4) * delta
        per_head = [jnp.broadcast_to(jnp.sum(dsink[g * BLOCK:(g + 1) * BLOCK], axis=0, keepdims=True), (1, 128))
                    for g in range(SWA_GROUP)]
        dsink_ref[...] += jnp.concatenate(per_head + [jnp.zeros((8 - SWA_GROUP, 128), F32)], axis=0)

    acc = pl.BlockSpec((None, s + BLOCK, d), lambda j, i: (j, 0, 0))
    return _pallas_call(
        body, name=name, grid=(SWA_KV_HEADS, s // tq),
        in_specs=[grp(d), main, tail, main, tail, sink, grp(d), grp(1), grp(d)],
        out_specs=[grp(d), acc, acc, pl.BlockSpec((None, 8, 128), lambda j, i: (j, 0, 0))],
        out_shape=[jax.ShapeDtypeStruct((SWA_HEADS, s, d), F32),
                   jax.ShapeDtypeStruct((SWA_KV_HEADS, s + BLOCK, d), F32),
                   jax.ShapeDtypeStruct((SWA_KV_HEADS, s + BLOCK, d), F32),
                   jax.ShapeDtypeStruct((SWA_KV_HEADS, 8, 128), F32)],
        compiler_params=_params(dimension_semantics=("arbitrary", "arbitrary")),
    )(q, kpad, kpad, vpad, vpad, sinks, o, lse, do)


def _loss_head(y, target, name):
    s = y.shape[0]
    tm = _row_tile(s, 512)

    def body(y_ref, t_ref, dy_ref, loss_ref):
        i = pl.program_id(0)
        err = y_ref[...] - t_ref[...]
        dy_ref[...] = err * (1.0 / D_MODEL)
        part = jnp.broadcast_to(0.5 * jnp.sum(jnp.mean(err * err, axis=-1, keepdims=True), axis=0, keepdims=True), (1, 128))

        @pl.when(i == 0)
        def _():
            loss_ref[...] = part

        @pl.when(i > 0)
        def _():
            loss_ref[...] += part

    row = pl.BlockSpec((tm, D_MODEL), lambda i: (i, 0))
    return _pallas_call(
        body, name=name, grid=(s // tm,), in_specs=[row, row], out_specs=[row, _full((1, 128))],
        out_shape=[jax.ShapeDtypeStruct((s, D_MODEL), F32), jax.ShapeDtypeStruct((1, 128), F32)],
        compiler_params=_params(dimension_semantics=("arbitrary",)),
    )(y, target)


def _adamw(w, g, m, v, name):
    rows, cols = w.shape
    tr = rows
    for cand in (512, 256, 128, 64, 32, 16, 8):
        if rows % cand == 0 and rows > cand:
            tr = cand
            break

    def body(w_ref, g_ref, m_ref, v_ref, d_ref, nm_ref, nv_ref):
        gv = g_ref[...]
        nm = ADAM_B1 * m_ref[...] + (1.0 - ADAM_B1) * gv
        nv = ADAM_B2 * v_ref[...] + (1.0 - ADAM_B2) * (gv * gv)
        m_hat = nm / (1.0 - ADAM_B1 ** ADAM_STEP)
        v_hat = nv / (1.0 - ADAM_B2 ** ADAM_STEP)
        d_ref[...] = -ADAM_LR * (m_hat / (jnp.sqrt(v_hat) + ADAM_EPS) + ADAM_WD * w_ref[...])
        nm_ref[...] = nm
        nv_ref[...] = nv

    blk = pl.BlockSpec((tr, cols), lambda i: (i, 0))
    return _pallas_call(
        body, name=name, grid=(rows // tr,), in_specs=[blk] * 4, out_specs=[blk] * 3,
        out_shape=[jax.ShapeDtypeStruct((rows, cols), F32)] * 3,
        compiler_params=_params(dimension_semantics=("arbitrary",)),
    )(w, g, m, v)


def _position():
    return lax.axis_index("x"), lax.axis_index("y"), lax.axis_index("c")


def _remote(src, dst, send_sems, recv_sems, k, to):
    return pltpu.make_async_remote_copy(src_ref=src, dst_ref=dst, send_sem=send_sems.at[k], recv_sem=recv_sems.at[k],
                                        device_id=to, device_id_type=MESH)


_HBM = pl.BlockSpec(memory_space=pltpu.HBM)


def _gather_stages(ins, outs, send_sems, recv_sems):
    na = len(ins)
    x, y, c = _position()
    me, sibling = (x, y, c), (x, y, 1 - c)
    xn, yn, dg = (1 - x, y), (x, 1 - y), (1 - x, 1 - y)

    def slot(a, chip, pc, half=None):
        ref = outs[a].at[4 * chip[0] + 2 * chip[1] + pc]
        if half is None:
            return ref
        rows = ref.shape[0] // 2
        return ref.at[pl.ds(half * rows, rows)]

    def cp(a, k, chip, pc, half, to, src=None):
        dst = slot(a, chip, pc, half)
        return _remote(dst if src is None else src, dst, send_sems, recv_sems, 8 * a + k, to)

    first_hop = [(0, xn), (1, yn)]
    second_hop = [(0, xn, 2, 0, yn), (1, yn, 3, 1, xn)]

    def sends():
        out = []
        for a in range(na):
            out += [cp(a, k, (x, y), c, None, (*to, c), src=ins[a].at[c]) for k, to in first_hop]
            out += [cp(a, fwd_k, frm, c, half, (*to, c)) for _, frm, fwd_k, half, to in second_hop]
            out += [cp(a, 4 + k, frm, c, None, sibling) for k, frm in first_hop]
            out += [cp(a, 6 + half, dg, c, half, sibling) for half in (0, 1)]
        return out

    def stage0():
        for a in range(na):
            for k, to in first_hop:
                cp(a, k, (x, y), c, None, (*to, c), src=ins[a].at[c]).start()

    def stage1():
        for k, frm, fwd_k, half, to in second_hop:
            for a in range(na):
                cp(a, k, frm, c, None, me).wait_recv()
                cp(a, fwd_k, frm, c, half, (*to, c)).start()
                cp(a, 4 + k, frm, c, None, sibling).start()

    def stage2():
        for half in (0, 1):
            for a in range(na):
                cp(a, 2 + half, dg, c, half, me).wait_recv()
                cp(a, 6 + half, dg, c, half, sibling).start()

    def stage3():
        for a in range(na):
            for k, chip, half in ((4, xn, None), (5, yn, None), (6, dg, 0), (7, dg, 1)):
                cp(a, k, chip, 1 - c, half, me).wait_recv()
        for sent in sends():
            sent.wait_send()

    return [stage0, stage1, stage2, stage3]


def _gather_out_shapes(mine):
    return [jax.ShapeDtypeStruct((N_DEV,) + m.shape[1:], m.dtype) for m in mine]


def _exchange_scratch(per_array, na):
    return [pltpu.SemaphoreType.DMA((per_array * na,)), pltpu.SemaphoreType.DMA((per_array * na,))]


def _all_gather_halves(mine, name):
    na = len(mine)

    def body(*refs):
        for stage in _gather_stages(refs[:na], refs[na:2 * na], *refs[2 * na:]):
            stage()

    return _pallas_call(body, name=name, in_specs=[_HBM] * na, out_specs=[_HBM] * na, out_shape=_gather_out_shapes(mine),
                        scratch_shapes=_exchange_scratch(8, na))(*mine)


def _sibling_exchange(parts, name, other_half):
    na = len(parts)

    def body(*refs):
        ins, outs, (send_sems, recv_sems) = refs[:na], refs[na:2 * na], refs[2 * na:]
        x, y, c = _position()
        copies = [_remote(ins[a].at[:, 1 - c] if other_half else ins[a], outs[a], send_sems, recv_sems, a, (x, y, 1 - c))
                  for a in range(na)]
        for cp in copies:
            cp.start()
        for cp in copies:
            cp.wait()

    return _pallas_call(
        body, name=name, in_specs=[_HBM] * na, out_specs=[_HBM] * na,
        out_shape=[jax.ShapeDtypeStruct(p.shape[:1] + p.shape[2:] if other_half else p.shape, p.dtype) for p in parts],
        scratch_shapes=_exchange_scratch(1, na),
    )(*parts)


def _scatter_stages(ins, outs, send_sems, recv_sems):
    na = len(ins)
    x, y, c = _position()
    chips = [(1 - x, y), (x, 1 - y), (1 - x, 1 - y)]

    def copies():
        return [_remote(ins[a].at[2 * px + py], outs[a].at[j], send_sems, recv_sems, 3 * a + j, (px, py, c))
                for a in range(na) for j, (px, py) in enumerate(chips)]

    def start():
        for cp in copies():
            cp.start()

    def wait():
        for cp in copies():
            cp.wait()

    return [start, wait]


def _scatter_out_shapes(parts):
    return [jax.ShapeDtypeStruct((3,) + p.shape[1:], p.dtype) for p in parts]


def _scatter_to_chips(parts, name):
    na = len(parts)

    def body(*refs):
        for stage in _scatter_stages(refs[:na], refs[na:2 * na], *refs[2 * na:]):
            stage()

    return _pallas_call(body, name=name, in_specs=[_HBM] * na, out_specs=[_HBM] * na, out_shape=_scatter_out_shapes(parts),
                        scratch_shapes=_exchange_scratch(3, na))(*parts)


def _assemble(g4, mine, name, side_by_side):
    _, nl, r, w = g4.shape
    tr = next(cand for cand in (256, 128) if r % cand == 0)

    def body(in_ref, mine_ref, out_ref):
        chip = 2 * lax.axis_index("x") + lax.axis_index("y")
        blocks = [jnp.where(chip == sh, mine_ref[...], in_ref[sh]) for sh in range(N_SHARD)]
        if side_by_side:
            out_ref[...] = jnp.concatenate(blocks, axis=-1)
        else:
            for sh in range(N_SHARD):
                out_ref[sh] = blocks[sh]

    if side_by_side:
        out_spec = pl.BlockSpec((None, tr, N_SHARD * w), lambda l, i: (l, i, 0))
        out_shape = jax.ShapeDtypeStruct((nl, r, N_SHARD * w), g4.dtype)
    else:
        out_spec = pl.BlockSpec((None, N_SHARD, tr, w), lambda l, i: (l, 0, i, 0))
        out_shape = jax.ShapeDtypeStruct((nl, N_SHARD, r, w), g4.dtype)
    return _pallas_call(
        body, name=name, grid=(nl, r // tr),
        in_specs=[pl.BlockSpec((N_SHARD, None, tr, w), lambda l, i: (0, l, i, 0)),
                  pl.BlockSpec((None, tr, w), lambda l, i: (l, i, 0))],
        out_specs=out_spec, out_shape=out_shape,
        compiler_params=_params(dimension_semantics=("arbitrary", "arbitrary")),
    )(g4, mine)


def _all_reduce_small(vec, name):
    r, l = vec.shape

    def body(v_ref, out_ref, gath_ref, send_sems, recv_sems):
        x, y, c = _position()
        me = 4 * x + 2 * y + c
        gath_ref[me] = v_ref[...]
        copies = []
        for k in range(1, N_DEV):
            to = (x ^ (k >> 2), y ^ ((k >> 1) & 1), c ^ (k & 1))
            copies.append(_remote(gath_ref.at[me], gath_ref.at[me], send_sems, recv_sems, k - 1, to))
        for cp in copies:
            cp.start()
        for k in range(1, N_DEV):
            frm = 4 * (x ^ (k >> 2)) + 2 * (y ^ ((k >> 1) & 1)) + (c ^ (k & 1))
            _remote(gath_ref.at[frm], gath_ref.at[frm], send_sems, recv_sems, k - 1, (x, y, c)).wait_recv()
        for cp in copies:
            cp.wait_send()
        total = gath_ref[0]
        for d in range(1, N_DEV):
            total = total + gath_ref[d]
        out_ref[...] = total

    vm = pl.BlockSpec(memory_space=pltpu.VMEM)
    return _pallas_call(
        body, name=name, in_specs=[vm], out_specs=vm, out_shape=jax.ShapeDtypeStruct((r, l), F32),
        scratch_shapes=[pltpu.VMEM((N_DEV, r, l), F32), pltpu.SemaphoreType.DMA((N_DEV - 1,)),
                        pltpu.SemaphoreType.DMA((N_DEV - 1,))],
    )(vec)


def _sum_blocks(blocks, out_dtype, name):
    m, w = blocks[0].shape
    tr = next(cand for cand in (512, 256, 128, 64) if m % cand == 0)

    def body(*refs):
        total = refs[0][...].astype(F32)
        for ref in refs[1:-1]:
            total = total + ref[...].astype(F32)
        refs[-1][...] = total.astype(out_dtype)

    blk = pl.BlockSpec((tr, w), lambda i: (i, 0))
    return _pallas_call(
        body, name=name, grid=(m // tr,), in_specs=[blk] * len(blocks), out_specs=blk,
        out_shape=jax.ShapeDtypeStruct((m, w), out_dtype),
        compiler_params=_params(dimension_semantics=("arbitrary",)),
    )(*blocks)


GROUPS = (('ffn1_w_gate', 'ffn1_w_up', 'ffn2_w_gate', 'ffn2_w_up'), ('ffn1_w_down', 'ffn2_w_down', 'w_o'),
          ('w_in',), ('mla_w_q_b',), ('mla_w_kv_b',))


def _shard_rows(name):
    shape, axis = BIG[name]
    return shape[0] // N_SHARD if axis == 0 else shape[0]


def _group_row_offsets(group):
    return [int(v) for v in np.cumsum([0] + [_shard_rows(n) for n in group])]


def _rope_tables(s):
    pos = jnp.arange(s, dtype=F32)
    inv = 1.0 / (ROPE_THETA ** (jnp.arange(0, MLA_ROPE, 2, dtype=F32) / MLA_ROPE))
    ang = pos[:, None] * inv[None, :]
    return jnp.tile(jnp.cos(ang), (1, SWA_HEADS)), jnp.tile(jnp.sin(ang), (1, SWA_HEADS))


_MIXER_GAINS = ('mix_norm', 'mla_q_a_norm', 'mla_kv_a_norm', 'mla_q_norm', 'mla_k_norm', 'swa_q_norm', 'swa_k_norm')


def _local_step(x, target, small, weights_of, gather_next=None, reduce_begin=None):
    s = x.shape[0]
    cos, sin = _rope_tables(s)
    row = lambda name, l: small[name][l][None, :]
    saved, bigs, gathered = [], [], None
    for l in range(DEPTH):
        big = weights_of(l, gathered)
        bigs.append(big)
        sv = {'x0': x}
        x, sv['g1'], sv['u1'] = _ffn_fwd(x, row('ffn1_norm', l), big['ffn1_w_gate'], big['ffn1_w_up'], big['ffn1_w_down'],
                                        f"ffn1_fwd_{l}")
        sv['x1'] = x
        gains = [row(n, l) for n in _MIXER_GAINS]
        mixer_w = (_take_cols(big['w_in'], _IN_ORDER), _take_cols(big['mla_w_q_b'], _QB_ORDER),
                   _take_cols(big['mla_w_kv_b'], _KVB_ORDER))
        q_a, k_a, v_a, q_b, k_b, v_b = _pre_fwd(x, gains, *mixer_w, cos, sin, f"pre_fwd_{l}")
        o_a, lse, gathered = _mla_fwd(q_a, k_a, v_a, f"mla_fwd_{l}", gather_next[l] if gather_next else None)
        kpad = jnp.pad(k_b, ((0, 0), (BLOCK, 0), (0, 0)))
        vpad = jnp.pad(v_b, ((0, 0), (BLOCK, 0), (0, 0)))
        sinks = jnp.broadcast_to(small['swa_sinks'][l].reshape(SWA_KV_HEADS, SWA_GROUP, 1), (SWA_KV_HEADS, SWA_GROUP, 128))
        o_b, lse_b = _swa_fwd(q_b, kpad, vpad, sinks, f"swa_fwd_{l}")
        sv.update(gains=gains, mixer_w=mixer_w, q_a=q_a, k_a=k_a, v_a=v_a, q_b=q_b, kpad=kpad, vpad=vpad, sinks=sinks,
                  o_a=o_a, lse=lse, o_b=o_b, lse_b=lse_b)
        x = _post_fwd(x, o_a, o_b, row('mla_out_norm', l), row('swa_out_norm', l), big['w_o'], f"post_fwd_{l}")
        sv['x2'] = x
        x, sv['g2'], sv['u2'] = _ffn_fwd(x, row('ffn2_norm', l), big['ffn2_w_gate'], big['ffn2_w_up'], big['ffn2_w_down'],
                                        f"ffn2_fwd_{l}")
        saved.append(sv)

    dx, loss = _loss_head(x, target, "loss_head")

    gs = {n: [None] * DEPTH for n in SMALL_NAMES}
    gb = {n: [None] * DEPTH for n in BIG_NAMES}
    begun, received = [None] * DEPTH, [None] * DEPTH
    t = _attn_tile(s)
    for l in reversed(range(DEPTH)):
        sv, big = saved[l], bigs[l]

        def ffn_back(tag, xin, dy, gate, up):
            dxi, dgain, nb, act, dgate, dup = _ffn_bwd(xin, dy, gate, up, row(tag + '_norm', l), big[tag + '_w_gate'],
                                                      big[tag + '_w_up'], big[tag + '_w_down'], f"{tag}_bwd_{l}")
            gs[tag + '_norm'][l] = dgain[0]
            gb[tag + '_w_gate'][l] = _matmul_tn(nb, dgate, 1.0, f"{tag}_dwg_{l}", True)
            gb[tag + '_w_up'][l] = _matmul_tn(nb, dup, 1.0, f"{tag}_dwu_{l}", True)
            gb[tag + '_w_down'][l] = _matmul_tn(act, dy, 0.5, f"{tag}_dwd_{l}", False).reshape(N_SHARD, D_FF // N_SHARD, D_MODEL)
            return dxi

        dx = ffn_back('ffn2', sv['x2'], dx, sv['g2'], sv['u2'])
        do_a, do_b, dga, dgb, dwo = _post_bwd(dx, sv['o_a'], sv['o_b'], row('mla_out_norm', l), row('swa_out_norm', l),
                                              big['w_o'], f"post_bwd_{l}")
        gs['mla_out_norm'][l], gs['swa_out_norm'][l] = dga[0], dgb[0]
        gb['w_o'][l] = dwo.reshape(N_SHARD, MIX_WIDTH // N_SHARD, D_MODEL)
        dq_a, delta = _mla_bwd_dq(sv['q_a'], sv['k_a'], sv['v_a'], sv['o_a'], do_a, sv['lse'], f"mla_dq_{l}")
        riding = begun[l + 1] if l + 1 < DEPTH else None
        dk_a, dv_a, got = _mla_bwd_dkv(sv['q_a'], sv['k_a'], sv['v_a'], do_a, sv['lse'].reshape(MLA_HEADS, s // t, t),
                                       delta.reshape(MLA_HEADS, s // t, t), f"mla_dkv_{l}", riding)
        if riding is not None:
            received[l + 1] = got
        dq_b, dkpad, dvpad, dsink = _swa_bwd(sv['q_b'], sv['kpad'], sv['vpad'], sv['sinks'], sv['o_b'], sv['lse_b'], do_b,
                                             f"swa_bwd_{l}")
        gs['swa_sinks'][l] = dsink[:, :SWA_GROUP, 0].reshape(SWA_HEADS)
        cts = [dq_a, dk_a, dv_a, dq_b, dkpad[:, BLOCK:], dvpad[:, BLOCK:]]
        outs = _pre_bwd(sv['x1'], dx, cts, sv['gains'], *sv['mixer_w'], cos, sin, f"pre_bwd_{l}")
        dx = outs[0]
        for n, val in zip(_MIXER_GAINS, outs[1:8]):
            gs[n][l] = val[0]
        gb['w_in'][l], gb['mla_w_q_b'][l], gb['mla_w_kv_b'][l] = outs[8:11]
        dx = ffn_back('ffn1', sv['x0'], dx, sv['g1'], sv['u1'])
        if reduce_begin is not None:
            begun[l] = reduce_begin(l, {n: gb[n][l] for n in BIG_NAMES})
    return loss, dx, gs, gb, begun, received


def kernel(x, ffn1_norm, ffn1_w_gate, ffn1_w_up, ffn1_w_down, mix_norm, w_in, mla_q_a_norm, mla_w_q_b, mla_kv_a_norm, mla_w_kv_b, mla_q_norm, mla_k_norm, swa_q_norm, swa_k_norm, swa_sinks, mla_out_norm, swa_out_norm, w_o, ffn2_norm, ffn2_w_gate, ffn2_w_up, ffn2_w_down, loss_target, m_ffn1_norm, m_ffn1_w_gate, m_ffn1_w_up, m_ffn1_w_down, m_mix_norm, m_w_in, m_mla_q_a_norm, m_mla_w_q_b, m_mla_kv_a_norm, m_mla_w_kv_b, m_mla_q_norm, m_mla_k_norm, m_swa_q_norm, m_swa_k_norm, m_swa_sinks, m_mla_out_norm, m_swa_out_norm, m_w_o, m_ffn2_norm, m_ffn2_w_gate, m_ffn2_w_up, m_ffn2_w_down, v_ffn1_norm, v_ffn1_w_gate, v_ffn1_w_up, v_ffn1_w_down, v_mix_norm, v_w_in, v_mla_q_a_norm, v_mla_w_q_b, v_mla_kv_a_norm, v_mla_w_kv_b, v_mla_q_norm, v_mla_k_norm, v_swa_q_norm, v_swa_k_norm, v_swa_sinks, v_mla_out_norm, v_swa_out_norm, v_w_o, v_ffn2_norm, v_ffn2_w_gate, v_ffn2_w_up, v_ffn2_w_down):
    args = dict(locals())
    weights = {n: args[n] for n in WEIGHT_NAMES}
    mom_m = {n: args["m_" + n] for n in WEIGHT_NAMES}
    mom_v = {n: args["v_" + n] for n in WEIGHT_NAMES}
    c = lax.axis_index("c")
    chip = 2 * lax.axis_index("x") + lax.axis_index("y")
    halves_of = lambda a: a.reshape(a.shape[:-2] + (2, a.shape[-2] // 2, a.shape[-1]))

    mine = [[halves_of(jnp.concatenate([weights[n][l].astype(BF16) for n in group], axis=0)) for group in GROUPS]
            for l in range(DEPTH)]

    def weights_of(l, gathered):
        if not gathered:
            gathered = _all_gather_halves(mine[l], f"gather_weights_{l}")
        big = {}
        for gi, group in enumerate(GROUPS):
            offs = _group_row_offsets(group)
            _, rh, w = mine[l][gi].shape
            col_sharded = BIG[group[0]][1] == 1
            full = _assemble(gathered[gi].reshape(N_SHARD, 1, 2 * rh, w), mine[l][gi].reshape(1, 2 * rh, w),
                             f"assemble_{gi}_{l}", col_sharded)
            for i, n in enumerate(group):
                rows = offs[i + 1] - offs[i]
                if len(group) > 1 and offs[i] % rows == 0:
                    big[n] = ((full, (None, rows, N_SHARD * w), (0, offs[i] // rows, 0)) if col_sharded else
                              (full, (None, N_SHARD, rows, w), (0, 0, offs[i] // rows, 0)))
                elif col_sharded:
                    big[n] = full[0, offs[i]:offs[i + 1]]
                else:
                    big[n] = full[0, :, offs[i]:offs[i + 1]].reshape(BIG[n][0])
        return big

    def reduce_begin(l, grads_l):
        parts = [halves_of(jnp.concatenate([grads_l[n] for n in group], axis=1)) for group in GROUPS]
        from_sibling = _sibling_exchange(parts, f"swap_halves_{l}", True)
        chip_sums = []
        for gi, (p, got) in enumerate(zip(parts, from_sibling)):
            kept = lax.dynamic_index_in_dim(p, c, axis=1, keepdims=False)
            rows = N_SHARD * p.shape[2]
            pair = _sum_blocks([kept.reshape(rows, -1), got.reshape(rows, -1)], BF16, f"sum_pair_{gi}_{l}")
            chip_sums.append(pair.reshape(got.shape))
        return chip_sums

    small = {n: weights[n] for n in SMALL_NAMES}
    loss, dx, gs, _, begun, received = _local_step(x[0], loss_target[0], small, weights_of,
                                                   [mine[l + 1] if l + 1 < DEPTH else None for l in range(DEPTH)], reduce_begin)
    received[0] = _scatter_to_chips(begun[0], "scatter_chips_0")

    small_flat = jnp.concatenate([jnp.stack(gs[n]).reshape(-1) for n in SMALL_NAMES] + [loss[0, :1]])
    n_small = small_flat.shape[0]
    lanes = -(-n_small // (8 * 128)) * 128
    small_sum = _all_reduce_small(jnp.pad(small_flat, (0, 8 * lanes - n_small)).reshape(8, lanes), "reduce_small").reshape(-1)
    grads = {}
    off = 0
    for n in SMALL_NAMES:
        cnt = int(np.prod(weights[n].shape))
        grads[n] = small_sum[off:off + cnt].reshape(weights[n].shape)
        off += cnt
    loss_out = small_sum[off]

    halves = []
    for l in range(DEPTH):
        for gi, (cs, got) in enumerate(zip(begun[l], received[l])):
            own = lax.dynamic_index_in_dim(cs, chip, axis=0, keepdims=False)
            halves.append(_sum_blocks([own, got[0], got[1], got[2]], F32, f"sum_chips_{gi}_{l}"))
    others = _sibling_exchange(halves, "share_halves", False)
    for gi, group in enumerate(GROUPS):
        offs = _group_row_offsets(group)
        layers = []
        for l in range(DEPTH):
            mine_h, other_h = halves[l * len(GROUPS) + gi], others[l * len(GROUPS) + gi]
            layers.append(jnp.where(c == 0, jnp.concatenate([mine_h, other_h]), jnp.concatenate([other_h, mine_h])))
        both = jnp.stack(layers)
        for i, n in enumerate(group):
            grads[n] = both[:, offs[i]:offs[i + 1]]

    deltas, new_m, new_v = {}, {}, {}
    for n in WEIGHT_NAMES:
        shp = weights[n].shape
        two_d = (DEPTH, shp[-1]) if len(shp) == 2 else (shp[0] * shp[1], shp[2])
        d, nm, nv = _adamw(weights[n].reshape(two_d), grads[n].reshape(two_d), mom_m[n].reshape(two_d),
                           mom_v[n].reshape(two_d), f"adamw_{n}")
        deltas[n], new_m[n], new_v[n] = d.reshape(shp), nm.reshape(shp), nv.reshape(shp)

    return (loss_out, dx[None], *[grads[n] for n in WEIGHT_NAMES], *[deltas[n] for n in WEIGHT_NAMES],
            *[new_m[n] for n in WEIGHT_NAMES], *[new_v[n] for n in WEIGHT_NAMES])
```

```python
import functools

import numpy as np
import jax
import jax.numpy as jnp
from jax import lax
from jax.experimental import pallas as pl
from jax.experimental.pallas import tpu as pltpu

F32 = jnp.float32
BF16 = jnp.bfloat16

D_MODEL = 1024
DEPTH = 2
EPS = 1e-6
ROPE_THETA = 10000.0
BLOCK = 128
MLA_HEADS = 4
MLA_Q_RANK = 256
MLA_KV_RANK = 128
MLA_NOPE = 128
MLA_ROPE = 64
MLA_V = 128
MLA_QK = MLA_NOPE + MLA_ROPE
MLA_WIDTH = MLA_HEADS * MLA_V
SWA_HEADS = 8
SWA_KV_HEADS = 2
SWA_GROUP = SWA_HEADS // SWA_KV_HEADS
SWA_HEAD_DIM = 64
SWA_WIDTH = SWA_HEADS * SWA_HEAD_DIM
MIX_WIDTH = MLA_WIDTH + SWA_WIDTH
IN_SPLITS = (MLA_Q_RANK, MLA_KV_RANK, MLA_ROPE, SWA_WIDTH, SWA_KV_HEADS * SWA_HEAD_DIM, SWA_KV_HEADS * SWA_HEAD_DIM)
IN_COLS = sum(IN_SPLITS)
IN_OFFS = tuple(int(v) for v in np.cumsum((0,) + IN_SPLITS))
D_FF = 2816
MLA_SCALE = MLA_QK ** -0.5
LOG2E = 1.4426950408889634
LN2 = 0.6931471805599453
MLA_QSCALE = MLA_SCALE * LOG2E
SWA_SCALE = SWA_HEAD_DIM ** -0.5
NEG = -1e30

ADAM_LR = 0.001
ADAM_B1 = 0.9
ADAM_B2 = 0.999
ADAM_EPS = 1e-08
ADAM_WD = 0.01
ADAM_STEP = 10

N_SHARD = 4
N_DEV = 8
VMEM_LIMIT = 56 * 1024 * 1024
MESH = pl.DeviceIdType.MESH

WEIGHT_NAMES = ['ffn1_norm', 'ffn1_w_gate', 'ffn1_w_up', 'ffn1_w_down', 'mix_norm', 'w_in', 'mla_q_a_norm', 'mla_w_q_b',
                'mla_kv_a_norm', 'mla_w_kv_b', 'mla_q_norm', 'mla_k_norm', 'swa_q_norm', 'swa_k_norm', 'swa_sinks',
                'mla_out_norm', 'swa_out_norm', 'w_o', 'ffn2_norm', 'ffn2_w_gate', 'ffn2_w_up', 'ffn2_w_down']
BIG = {'ffn1_w_gate': ((D_MODEL, D_FF), 1), 'ffn1_w_up': ((D_MODEL, D_FF), 1), 'ffn1_w_down': ((D_FF, D_MODEL), 0),
       'w_in': ((D_MODEL, IN_COLS), 1), 'mla_w_q_b': ((MLA_Q_RANK, MLA_HEADS * MLA_QK), 1),
       'mla_w_kv_b': ((MLA_KV_RANK, MLA_HEADS * (MLA_NOPE + MLA_V)), 1), 'w_o': ((MIX_WIDTH, D_MODEL), 0),
       'ffn2_w_gate': ((D_MODEL, D_FF), 1), 'ffn2_w_up': ((D_MODEL, D_FF), 1), 'ffn2_w_down': ((D_FF, D_MODEL), 0)}
BIG_NAMES = [n for n in WEIGHT_NAMES if n in BIG]
SMALL_NAMES = [n for n in WEIGHT_NAMES if n not in BIG]

_pallas_call = pl.pallas_call


def _params(**kw):
    return pltpu.CompilerParams(vmem_limit_bytes=VMEM_LIMIT, **kw)


def _full(shape):
    n = len(shape)
    return pl.BlockSpec(shape, lambda *_: (0,) * n)


def _resident(shape):
    n = len(shape)
    return pl.BlockSpec(shape, lambda *_: (0,) * n, pipeline_mode=pl.Buffered(1))


@jax.custom_vjp
def _mm(a, w):
    return jnp.dot(a.astype(BF16), w, preferred_element_type=F32)


def _mm_fwd(a, w):
    return _mm(a, w), w


def _mm_bwd(w, dy):
    return lax.dot_general(dy.astype(BF16), w, (((1,), (1,)), ((), ())), preferred_element_type=F32), None


_mm.defvjp(_mm_fwd, _mm_bwd)


def _dot_nt(a, b):
    return lax.dot_general(a, b, (((1,), (1,)), ((), ())), preferred_element_type=F32)


def _dot_tn(a, b):
    return lax.dot_general(a, b, (((0,), (0,)), ((), ())), preferred_element_type=F32)


def _rms(t, g):
    return t * lax.rsqrt(jnp.mean(t * t, axis=-1, keepdims=True) + EPS) * g


def _sigmoid(z):
    return 1.0 / (1.0 + jnp.exp(-z))


def _row_tile(s, want):
    return min(want, s)


FF_CHUNK = 1408


def _weight_operand(w):
    if isinstance(w, tuple):
        arr, block, index = w
        return arr, pl.BlockSpec(block, lambda *_: index, pipeline_mode=pl.Buffered(1))
    return w, _resident(w.shape)


def _weight_rows(ref, start, n):
    if len(ref.shape) == 2:
        return ref[start:start + n, :]
    per = ref.shape[1]
    return ref[start // per:(start + n) // per].reshape(n, ref.shape[2])


def _ffn_fwd(x, g, wg, wu, wd, name, gather=None):
    s = x.shape[0]
    tm = _row_tile(s, 256)
    steps = s // tm
    ng = len(gather) if gather else 0
    (wg, wg_spec), (wu, wu_spec), (wd, wd_spec) = _weight_operand(wg), _weight_operand(wu), _weight_operand(wd)

    def body(x_ref, g_ref, wg_ref, wu_ref, wd_ref, *rest):
        g_ins, (y_ref, gate_ref, up_ref), g_outs, sems = rest[:ng], rest[ng:ng + 3], rest[ng + 3:2 * ng + 3], rest[2 * ng + 3:]
        if ng:
            _run_stages_at(_gather_stages(g_ins, g_outs, *sems), [(0,), (steps * 3 // 8,), (steps * 11 // 16,), (steps - 1,)])
        xv = x_ref[...]
        nb = _rms(xv, g_ref[...]).astype(BF16)
        acc = xv
        for c in range(0, D_FF, FF_CHUNK):
            gate = jnp.dot(nb, wg_ref[:, c:c + FF_CHUNK], preferred_element_type=F32)
            up = jnp.dot(nb, wu_ref[:, c:c + FF_CHUNK], preferred_element_type=F32)
            gate_ref[:, c:c + FF_CHUNK] = gate.astype(BF16)
            up_ref[:, c:c + FF_CHUNK] = up.astype(BF16)
            act = (gate * _sigmoid(gate) * up).astype(BF16)
            acc = acc + 0.5 * jnp.dot(act, _weight_rows(wd_ref, c, FF_CHUNK), preferred_element_type=F32)
        y_ref[...] = acc

    outs = _pallas_call(
        body, name=name, grid=(steps,),
        in_specs=[pl.BlockSpec((tm, D_MODEL), lambda i: (i, 0)), _full((1, D_MODEL)), wg_spec, wu_spec, wd_spec] + [_HBM] * ng,
        out_specs=[pl.BlockSpec((tm, D_MODEL), lambda i: (i, 0)), pl.BlockSpec((tm, D_FF), lambda i: (i, 0)),
                   pl.BlockSpec((tm, D_FF), lambda i: (i, 0))] + [_HBM] * ng,
        out_shape=[jax.ShapeDtypeStruct((s, D_MODEL), F32), jax.ShapeDtypeStruct((s, D_FF), BF16),
                   jax.ShapeDtypeStruct((s, D_FF), BF16)] + (_gather_out_shapes(gather) if ng else []),
        scratch_shapes=_exchange_scratch(8, ng) if ng else [],
        compiler_params=_params(dimension_semantics=("arbitrary",)),
    )(x, g, wg, wu, wd, *(gather or []))
    return outs[0], outs[1], outs[2], outs[3:]


def _ffn_bwd(x, dy, gate, up, g, wg, wu, wd, name, scatter=None):
    s = x.shape[0]
    tm = _row_tile(s, 256)
    steps = s // tm
    ng = len(scatter) if scatter else 0
    (wg, wg_spec), (wu, wu_spec), (wd, wd_spec) = _weight_operand(wg), _weight_operand(wu), _weight_operand(wd)

    def body(x_ref, dy_ref, gate_ref, up_ref, g_ref, wg_ref, wu_ref, wd_ref, *rest):
        c_ins, (dx_ref, dgain_ref, n_ref, act_ref, dgate_ref, dup_ref) = rest[:ng], rest[ng:ng + 6]
        c_outs, sems = rest[ng + 6:2 * ng + 6], rest[2 * ng + 6:]
        if ng:
            _run_stages_at(_scatter_stages(c_ins, c_outs, *sems), [(0,), (steps - 1,)])
        i = pl.program_id(0)
        xv = x_ref[...]
        dyv = dy_ref[...]
        gv = g_ref[...]
        r = lax.rsqrt(jnp.mean(xv * xv, axis=-1, keepdims=True) + EPS)
        xh = xv * r
        n_ref[...] = (xh * gv).astype(BF16)
        dyh = (0.5 * dyv).astype(BF16)
        dn = jnp.zeros_like(xv)
        for c in range(0, D_FF, FF_CHUNK):
            dact = _dot_nt(dyh, _weight_rows(wd_ref, c, FF_CHUNK))
            gt = gate_ref[:, c:c + FF_CHUNK].astype(F32)
            u = up_ref[:, c:c + FF_CHUNK].astype(F32)
            sg = _sigmoid(gt)
            sl = gt * sg
            act_ref[:, c:c + FF_CHUNK] = (sl * u).astype(BF16)
            dup = (dact * sl).astype(BF16)
            dgate = (dact * u * (sg * (1.0 + gt * (1.0 - sg)))).astype(BF16)
            dup_ref[:, c:c + FF_CHUNK] = dup
            dgate_ref[:, c:c + FF_CHUNK] = dgate
            dn = dn + _dot_nt(dgate, wg_ref[:, c:c + FF_CHUNK]) + _dot_nt(dup, wu_ref[:, c:c + FF_CHUNK])
        part = jnp.sum(dn * xh, axis=0, keepdims=True)

        @pl.when(i == 0)
        def _():
            dgain_ref[...] = part

        @pl.when(i > 0)
        def _():
            dgain_ref[...] += part

        dxh = dn * gv
        dx_ref[...] = dyv + r * (dxh - xh * jnp.mean(dxh * xh, axis=-1, keepdims=True))

    row = lambda w: pl.BlockSpec((tm, w), lambda i: (i, 0))
    outs = _pallas_call(
        body, name=name, grid=(steps,),
        in_specs=[row(D_MODEL), row(D_MODEL), row(D_FF), row(D_FF), _full((1, D_MODEL)), wg_spec, wu_spec, wd_spec]
        + [_HBM] * ng,
        out_specs=[row(D_MODEL), _full((1, D_MODEL)), row(D_MODEL), row(D_FF), row(D_FF), row(D_FF)] + [_HBM] * ng,
        out_shape=[jax.ShapeDtypeStruct((s, D_MODEL), F32), jax.ShapeDtypeStruct((1, D_MODEL), F32),
                   jax.ShapeDtypeStruct((s, D_MODEL), BF16), jax.ShapeDtypeStruct((s, D_FF), BF16),
                   jax.ShapeDtypeStruct((s, D_FF), BF16), jax.ShapeDtypeStruct((s, D_FF), BF16)]
        + (_scatter_out_shapes(scatter) if ng else []),
        scratch_shapes=_exchange_scratch(3, ng) if ng else [],
        compiler_params=_params(dimension_semantics=("arbitrary",)),
    )(x, dy, gate, up, g, wg, wu, wd, *(scatter or []))
    return outs[:6], outs[6:]


def _store_col_shards(o_ref, acc, first_shard, n_here, width):
    for q in range(n_here):
        o_ref[q] = acc[:, (first_shard + q) * width:(first_shard + q + 1) * width].astype(BF16)


def _matmul_tn(a, b, scale, name, col_shards):
    t, m = a.shape
    n = b.shape[1]
    tk = _row_tile(t, 2048)
    tn = n // 2
    per = n // N_SHARD
    nk = t // tk

    def body(a_ref, b_ref, o_ref, acc_ref):
        k = pl.program_id(1)
        bv = b_ref[...]
        if scale != 1.0:
            bv = bv.astype(F32) * scale
        part = _dot_tn(a_ref[...].astype(BF16), bv.astype(BF16))

        @pl.when(k == 0)
        def _():
            acc_ref[...] = part

        @pl.when(k > 0)
        def _():
            acc_ref[...] += part

        @pl.when(k == nk - 1)
        def _():
            if col_shards:
                _store_col_shards(o_ref, acc_ref[...], 0, tn // per, per)
            else:
                o_ref[...] = acc_ref[...].astype(BF16)

    if col_shards:
        out_spec = pl.BlockSpec((tn // per, m, per), lambda j, k: (j, 0, 0))
        out_shape = jax.ShapeDtypeStruct((N_SHARD, m, per), BF16)
    else:
        out_spec = pl.BlockSpec((m, tn), lambda j, k: (0, j))
        out_shape = jax.ShapeDtypeStruct((m, n), BF16)
    return _pallas_call(
        body, name=name, grid=(n // tn, nk),
        in_specs=[pl.BlockSpec((tk, m), lambda j, k: (k, 0)), pl.BlockSpec((tk, tn), lambda j, k: (k, j))],
        out_specs=out_spec, out_shape=out_shape, scratch_shapes=[pltpu.VMEM((m, tn), F32)],
        compiler_params=_params(dimension_semantics=("arbitrary", "arbitrary")),
    )(a, b)


_HALF = SWA_HEAD_DIM // 2
_IN_ORDER = (list(range(0, IN_OFFS[2]))
             + [IN_OFFS[3] + SWA_HEAD_DIM * h + i for h in range(SWA_HEADS) for i in range(_HALF)]
             + [IN_OFFS[3] + SWA_HEAD_DIM * h + _HALF + i for h in range(SWA_HEADS) for i in range(_HALF)]
             + list(range(IN_OFFS[5], IN_OFFS[6]))
             + [IN_OFFS[4] + SWA_HEAD_DIM * j + i for j in range(SWA_KV_HEADS) for i in range(_HALF)]
             + [IN_OFFS[4] + SWA_HEAD_DIM * j + _HALF + i for j in range(SWA_KV_HEADS) for i in range(_HALF)]
             + list(range(IN_OFFS[2], IN_OFFS[3])))
_QB_ORDER = ([MLA_QK * h + i for h in range(MLA_HEADS) for i in range(MLA_NOPE)]
             + [MLA_QK * h + MLA_NOPE + i for h in range(MLA_HEADS) for i in range(_HALF)]
             + [MLA_QK * h + MLA_NOPE + _HALF + i for h in range(MLA_HEADS) for i in range(_HALF)])
_KVB_ORDER = ([(MLA_NOPE + MLA_V) * h + i for h in range(MLA_HEADS) for i in range(MLA_NOPE)]
              + [(MLA_NOPE + MLA_V) * h + MLA_NOPE + i for h in range(MLA_HEADS) for i in range(MLA_V)])
_P_CQ, _P_CKV, _P_QA, _P_QB, _P_VS, _P_KA, _P_KB, _P_PE = (int(v) for v in np.cumsum(
    (0, MLA_Q_RANK, MLA_KV_RANK, SWA_WIDTH // 2, SWA_WIDTH // 2, IN_SPLITS[5], IN_SPLITS[4] // 2, IN_SPLITS[4] // 2)))


def _runs(order):
    out, start = [], 0
    for i in range(1, len(order) + 1):
        if i == len(order) or order[i] != order[i - 1] + 1:
            out.append((order[start], i - start))
            start = i
    return out


def _inverse(order):
    inv = [0] * len(order)
    for new, old in enumerate(order):
        inv[old] = new
    return inv


def _take_cols(a, order):
    return jnp.concatenate([a[..., st:st + w] for st, w in _runs(order)], axis=-1)


def _segment_matrix(n, seg):
    return (lax.broadcasted_iota(jnp.int32, (n, n), 0) // seg == lax.broadcasted_iota(jnp.int32, (n, n), 1) // seg).astype(BF16)


@jax.custom_vjp
def _cmm(t, b, bt):
    hi = t.astype(BF16)
    lo = (t - hi.astype(F32)).astype(BF16)
    return jnp.dot(hi, b, preferred_element_type=F32) + jnp.dot(lo, b, preferred_element_type=F32)


def _cmm_fwd(t, b, bt):
    return _cmm(t, b, bt), (b, bt)


def _cmm_bwd(res, dy):
    b, bt = res
    return _cmm(dy, bt, b), None, None


_cmm.defvjp(_cmm_fwd, _cmm_bwd)


def _segsum(t, b):
    return _cmm(t, b, b)


def _rowsum(t):
    n = t.shape[-1]
    return _cmm(t, jnp.ones((n, 128), BF16), jnp.ones((128, n), BF16))


def _by_head(vals, width):
    lane = lax.broadcasted_iota(jnp.int32, (vals[0].shape[0], len(vals) * width), 1)
    out = vals[-1]
    for hd in range(len(vals) - 2, -1, -1):
        out = jnp.where(lane < (hd + 1) * width, vals[hd], out)
    return out


def _rope2(a, b, cos, sin):
    return a * cos - b * sin, b * cos + a * sin


def _pre_math(x, gm, gqa, gkva, gq, gk, gsq, gsk, taps, win, wqb, wkvb, cos, sin):
    h = _rms(x, gm)
    proj = _mm(h, win)
    if taps is not None:
        proj = proj + taps[0]
    cqn = _rms(proj[:, _P_CQ:_P_CKV], gqa)
    qa_all = _mm(cqn, wqb)
    ckvn = _rms(proj[:, _P_CKV:_P_QA], gkva)
    kv_all = _mm(ckvn, wkvb)
    if taps is not None:
        qa_all = qa_all + taps[1]
        kv_all = kv_all + taps[2]
    nh, hw = MLA_HEADS, MLA_HEADS * _HALF
    seg_mla = _segment_matrix(hw, _HALF)
    tile = lambda g, n: jnp.concatenate([g] * n, axis=-1)
    c4, s4 = cos[:, :hw], sin[:, :hw]

    def mla_heads(nope, r1, r2, gain):
        rr = r1 * r1 + r2 * r2
        lane_head = lax.broadcasted_iota(jnp.int32, (hw, nh * MLA_NOPE), 0) // _HALF
        spread = (lane_head == lax.broadcasted_iota(jnp.int32, (hw, nh * MLA_NOPE), 1) // MLA_NOPE).astype(BF16)
        rope_on_nope = _cmm(rr, spread, spread.T)
        ss_nope = [_rowsum(jnp.square(nope[:, hd * MLA_NOPE:(hd + 1) * MLA_NOPE])) for hd in range(nh)]
        rinv = [lax.rsqrt((ss_nope[hd] + rope_on_nope[:, hd * MLA_NOPE:(hd + 1) * MLA_NOPE]) * (1.0 / MLA_QK) + EPS)
                for hd in range(nh)]
        rl = lax.rsqrt((_segsum(rr, seg_mla) + _by_head(ss_nope, _HALF)) * (1.0 / MLA_QK) + EPS)
        o1, o2 = _rope2(r1 * rl * tile(gain[:, MLA_NOPE:MLA_NOPE + _HALF], nh), r2 * rl * tile(gain[:, MLA_NOPE + _HALF:], nh), c4, s4)
        return [jnp.concatenate([nope[:, hd * MLA_NOPE:(hd + 1) * MLA_NOPE] * rinv[hd] * gain[:, :MLA_NOPE],
                                 o1[:, hd * _HALF:(hd + 1) * _HALF], o2[:, hd * _HALF:(hd + 1) * _HALF]], axis=-1)
                for hd in range(nh)]

    q_a = mla_heads(qa_all[:, :nh * MLA_NOPE], qa_all[:, nh * MLA_NOPE:nh * MLA_NOPE + hw], qa_all[:, nh * MLA_NOPE + hw:], gq)
    pe1, pe2 = proj[:, _P_PE:_P_PE + _HALF], proj[:, _P_PE + _HALF:_P_PE + 2 * _HALF]
    k_a = mla_heads(kv_all[:, :nh * MLA_NOPE], tile(pe1, nh), tile(pe2, nh), gk)
    v_a = [kv_all[:, nh * MLA_NOPE + hd * MLA_V:nh * MLA_NOPE + (hd + 1) * MLA_V] for hd in range(nh)]

    def swa_heads(a, b, gain, n):
        w = n * _HALF
        r = lax.rsqrt(_segsum(a * a + b * b, _segment_matrix(w, _HALF)) * (1.0 / SWA_HEAD_DIM) + EPS)
        o1, o2 = _rope2(a * r * tile(gain[:, :_HALF], n), b * r * tile(gain[:, _HALF:], n), cos[:, :w], sin[:, :w])
        return [jnp.concatenate([o1[:, hd * _HALF:(hd + 1) * _HALF], o2[:, hd * _HALF:(hd + 1) * _HALF]], axis=-1) for hd in range(n)]

    q_b = swa_heads(proj[:, _P_QA:_P_QB], proj[:, _P_QB:_P_VS], gsq, SWA_HEADS)
    k_b = swa_heads(proj[:, _P_KA:_P_KB], proj[:, _P_KB:_P_PE], gsk, SWA_KV_HEADS)
    v_b = [proj[:, _P_VS + j * SWA_HEAD_DIM:_P_VS + (j + 1) * SWA_HEAD_DIM] for j in range(SWA_KV_HEADS)]
    return (q_a, k_a, v_a, q_b, k_b, v_b), (h, cqn, ckvn)


_PRE_GAIN_WIDTHS = (D_MODEL, MLA_Q_RANK, MLA_KV_RANK, MLA_QK, MLA_QK, SWA_HEAD_DIM, SWA_HEAD_DIM)
_PRE_HEADS = ((MLA_HEADS, MLA_QK), (MLA_HEADS, MLA_QK), (MLA_HEADS, MLA_V),
              (SWA_HEADS, SWA_HEAD_DIM), (SWA_KV_HEADS, SWA_HEAD_DIM), (SWA_KV_HEADS, SWA_HEAD_DIM))


def _pre_fwd(x, gains, win, wqb, wkvb, cos, sin, name):
    s = x.shape[0]
    tm = _row_tile(s, 512)

    def body(x_ref, *refs):
        g_refs, (win_ref, wqb_ref, wkvb_ref, cos_ref, sin_ref), out_refs = refs[:7], refs[7:12], refs[12:]
        outs, _ = _pre_math(x_ref[...], *[g[...] for g in g_refs], None, win_ref[...], wqb_ref[...], wkvb_ref[...],
                            cos_ref[...], sin_ref[...])
        for idx, (ref, heads) in enumerate(zip(out_refs, outs)):
            for hd, val in enumerate(heads):
                ref[hd] = (val * MLA_QSCALE if idx == 0 else val).astype(BF16)

    heads_spec = lambda nh, w: pl.BlockSpec((nh, tm, w), lambda i: (0, i, 0))
    return _pallas_call(
        body, name=name, grid=(s // tm,),
        in_specs=[pl.BlockSpec((tm, D_MODEL), lambda i: (i, 0))] + [_full((1, w)) for w in _PRE_GAIN_WIDTHS]
        + [_resident(win.shape), _resident(wqb.shape), _resident(wkvb.shape),
           pl.BlockSpec((tm, SWA_HEADS * _HALF), lambda i: (i, 0)), pl.BlockSpec((tm, SWA_HEADS * _HALF), lambda i: (i, 0))],
        out_specs=[heads_spec(nh, w) for nh, w in _PRE_HEADS],
        out_shape=[jax.ShapeDtypeStruct((nh, s, w), BF16) for nh, w in _PRE_HEADS],
        compiler_params=_params(dimension_semantics=("arbitrary",)),
    )(x, *gains, win, wqb, wkvb, cos, sin)


def _pre_bwd(x, dx_res, cts, gains, win, wqb, wkvb, cos, sin, name):
    s = x.shape[0]
    tm = _row_tile(s, 256)
    tap_widths = (IN_COLS, MLA_HEADS * MLA_QK, MLA_HEADS * (MLA_NOPE + MLA_V))

    def body(x_ref, dxr_ref, *refs):
        ct_refs, g_refs = refs[:6], refs[6:13]
        win_ref, wqb_ref, wkvb_ref, cos_ref, sin_ref = refs[13:18]
        dx_ref, dg_refs, dw_refs, acc_refs = refs[18], refs[19:26], refs[26:29], refs[29:32]
        i = pl.program_id(0)
        win_v, wqb_v, wkvb_v, cos_v, sin_v = win_ref[...], wqb_ref[...], wkvb_ref[...], cos_ref[...], sin_ref[...]

        def f(xv, gm, gqa, gkva, gq, gk, gsq, gsk, t0, t1, t2):
            return _pre_math(xv, gm, gqa, gkva, gq, gk, gsq, gsk, (t0, t1, t2), win_v, wqb_v, wkvb_v, cos_v, sin_v)

        taps = [jnp.zeros((tm, w), F32) for w in tap_widths]
        _, vjp, acts = jax.vjp(f, x_ref[...], *[g[...] for g in g_refs], *taps, has_aux=True)
        ct = tuple([ref[hd] for hd in range(nh)] for ref, (nh, _) in zip(ct_refs, _PRE_HEADS))
        grads = vjp(ct)
        dx_ref[...] = grads[0] + dxr_ref[...]
        dws = [_dot_tn(a.astype(BF16), t.astype(BF16)) for a, t in zip(acts, grads[8:11])]

        @pl.when(i == 0)
        def _():
            for ref, val in zip(dg_refs, grads[1:8]):
                ref[...] = val
            for ref, val in zip(acc_refs, dws):
                ref[...] = val

        @pl.when(i > 0)
        def _():
            for ref, val in zip(dg_refs, grads[1:8]):
                ref[...] += val
            for ref, val in zip(acc_refs, dws):
                ref[...] += val

        @pl.when(i == s // tm - 1)
        def _():
            for ref, acc, order in zip(dw_refs, acc_refs, (_IN_ORDER, _QB_ORDER, _KVB_ORDER)):
                _store_col_shards(ref, _take_cols(acc[...], _inverse(order)), 0, N_SHARD, acc.shape[1] // N_SHARD)

    heads_spec = lambda nh, w: pl.BlockSpec((nh, tm, w), lambda i: (0, i, 0))
    row = pl.BlockSpec((tm, D_MODEL), lambda i: (i, 0))
    half = pl.BlockSpec((tm, SWA_HEADS * _HALF), lambda i: (i, 0))
    shard_shapes = [(N_SHARD, w.shape[0], w.shape[1] // N_SHARD) for w in (win, wqb, wkvb)]
    return _pallas_call(
        body, name=name, grid=(s // tm,),
        in_specs=[row, row] + [heads_spec(nh, w) for nh, w in _PRE_HEADS] + [_full((1, w)) for w in _PRE_GAIN_WIDTHS]
        + [_resident(win.shape), _resident(wqb.shape), _resident(wkvb.shape), half, half],
        out_specs=[row] + [_full((1, w)) for w in _PRE_GAIN_WIDTHS] + [_full(shp) for shp in shard_shapes],
        out_shape=[jax.ShapeDtypeStruct((s, D_MODEL), F32)] + [jax.ShapeDtypeStruct((1, w), F32) for w in _PRE_GAIN_WIDTHS]
        + [jax.ShapeDtypeStruct(shp, BF16) for shp in shard_shapes],
        scratch_shapes=[pltpu.VMEM(w.shape, F32) for w in (win, wqb, wkvb)],
        compiler_params=_params(dimension_semantics=("arbitrary",)),
    )(x, dx_res, *cts, *gains, win, wqb, wkvb, cos, sin)


def _post_math(oa, ob, ga, gb, wo):
    mixed = jnp.concatenate([_rms(jnp.concatenate(oa, axis=-1), ga), _rms(jnp.concatenate(ob, axis=-1), gb)], axis=-1)
    return _mm(mixed, wo), mixed


def _post_fwd(x, oa, ob, ga, gb, wo, name):
    s = x.shape[0]
    tm = _row_tile(s, 512)

    def body(x_ref, oa_ref, ob_ref, ga_ref, gb_ref, wo_ref, y_ref):
        y, _ = _post_math([oa_ref[hd] for hd in range(MLA_HEADS)], [ob_ref[hd] for hd in range(SWA_HEADS)],
                          ga_ref[...], gb_ref[...], wo_ref[...])
        y_ref[...] = x_ref[...] + y

    row = pl.BlockSpec((tm, D_MODEL), lambda i: (i, 0))
    return _pallas_call(
        body, name=name, grid=(s // tm,),
        in_specs=[row, pl.BlockSpec((MLA_HEADS, tm, MLA_V), lambda i: (0, i, 0)),
                  pl.BlockSpec((SWA_HEADS, tm, SWA_HEAD_DIM), lambda i: (0, i, 0)),
                  _full((1, MLA_WIDTH)), _full((1, SWA_WIDTH)), _resident(wo.shape)],
        out_specs=row, out_shape=jax.ShapeDtypeStruct((s, D_MODEL), F32),
        compiler_params=_params(dimension_semantics=("arbitrary",)),
    )(x, oa, ob, ga, gb, wo)


def _post_bwd(dy, oa, ob, ga, gb, wo, name):
    s = dy.shape[0]
    tm = _row_tile(s, 512)

    def body(dy_ref, oa_ref, ob_ref, ga_ref, gb_ref, wo_ref, doa_ref, dob_ref, dga_ref, dgb_ref, dwo_ref, acc_ref):
        i = pl.program_id(0)
        wo_v = wo_ref[...]
        dyv = dy_ref[...]

        def f(oa_l, ob_l, ga_v, gb_v):
            return _post_math(oa_l, ob_l, ga_v, gb_v, wo_v)

        _, vjp, mixed = jax.vjp(f, [oa_ref[hd] for hd in range(MLA_HEADS)], [ob_ref[hd] for hd in range(SWA_HEADS)],
                                ga_ref[...], gb_ref[...], has_aux=True)
        doa, dob, dga, dgb = vjp(dyv)
        for hd in range(MLA_HEADS):
            doa_ref[hd] = doa[hd]
        for hd in range(SWA_HEADS):
            dob_ref[hd] = dob[hd]
        dwo = _dot_tn(mixed.astype(BF16), dyv.astype(BF16))

        @pl.when(i == 0)
        def _():
            dga_ref[...] = dga
            dgb_ref[...] = dgb
            acc_ref[...] = dwo

        @pl.when(i > 0)
        def _():
            dga_ref[...] += dga
            dgb_ref[...] += dgb
            acc_ref[...] += dwo

        @pl.when(i == s // tm - 1)
        def _():
            dwo_ref[...] = acc_ref[...].astype(BF16)

    row = pl.BlockSpec((tm, D_MODEL), lambda i: (i, 0))
    oa_spec = pl.BlockSpec((MLA_HEADS, tm, MLA_V), lambda i: (0, i, 0))
    ob_spec = pl.BlockSpec((SWA_HEADS, tm, SWA_HEAD_DIM), lambda i: (0, i, 0))
    return _pallas_call(
        body, name=name, grid=(s // tm,),
        in_specs=[row, oa_spec, ob_spec, _full((1, MLA_WIDTH)), _full((1, SWA_WIDTH)), _resident(wo.shape)],
        out_specs=[oa_spec, ob_spec, _full((1, MLA_WIDTH)), _full((1, SWA_WIDTH)), _full(wo.shape)],
        out_shape=[jax.ShapeDtypeStruct((MLA_HEADS, s, MLA_V), F32), jax.ShapeDtypeStruct((SWA_HEADS, s, SWA_HEAD_DIM), F32),
                   jax.ShapeDtypeStruct((1, MLA_WIDTH), F32), jax.ShapeDtypeStruct((1, SWA_WIDTH), F32),
                   jax.ShapeDtypeStruct(wo.shape, BF16)],
        scratch_shapes=[pltpu.VMEM(wo.shape, F32)],
        compiler_params=_params(dimension_semantics=("arbitrary",)),
    )(dy, oa, ob, ga, gb, wo)


def _attn_tile(s):
    return 512 if s >= 2048 else 128


def _causal_mask(t):
    return lax.broadcasted_iota(jnp.int32, (t, t), 1) <= lax.broadcasted_iota(jnp.int32, (t, t), 0)


def _pipelined_blocks(first, count, last_block, issue, consume, carry, prefetch_after):
    def clamped(j, slot):
        issue(jnp.minimum(j, last_block), slot)

    def pair(jj, c):
        a = first + 2 * jj
        clamped(a + 1, 1)
        c = consume(a, 0, c)
        clamped(a + 2, 0)
        return consume(a + 1, 1, c)

    clamped(first, 0)
    npairs = count // 2
    carry = lax.fori_loop(0, npairs, pair, carry)

    def odd(c):
        c = consume(first + 2 * npairs, 0, c)
        if prefetch_after:
            clamped(first + count, 0)
        return c

    return lax.cond(count - 2 * npairs == 1, odd, lambda c: c, carry)


def _run_stages_at(stages, steps):
    for stage, step in zip(stages, steps):
        here = pl.program_id(0) == step[0]
        for axis in range(1, len(step)):
            here = here & (pl.program_id(axis) == step[axis])
        pl.when(here)(stage)


def _mla_fwd(q, k, v, name, gather=None):
    nh, s, _ = q.shape
    t = _attn_tile(s)
    nq = s // t
    ng = len(gather) if gather else 0

    def body(q_ref, k_ref, v_ref, *rest):
        g_ins, (o_ref, lse_ref), g_outs = rest[:ng], rest[ng:ng + 2], rest[ng + 2:2 * ng + 2]
        (s0_ref, s1_ref), sems = rest[2 * ng + 2:2 * ng + 4], rest[2 * ng + 4:]
        if ng:
            _run_stages_at(_gather_stages(g_ins, g_outs, *sems), [(0, 0), (nh // 2, 0), (nh - 1, 0), (nh - 1, nq - 1)])
        qi = pl.program_id(1)
        qv = q_ref[...]
        s_refs = (s0_ref, s1_ref)

        def rows(j):
            return pl.ds(pl.multiple_of(j * t, t), t)

        def issue(j, slot):
            s_refs[slot][...] = _dot_nt(qv, k_ref[rows(j), :])

        def consume(j, slot, carry, masked=False):
            m, l, acc = carry
            sc = s_refs[slot][...]
            if masked:
                sc = jnp.where(_causal_mask(t), sc, NEG)
            m_new = jnp.maximum(m, jnp.max(sc, axis=-1, keepdims=True))
            alpha = jnp.exp2(m - m_new)
            p = jnp.exp2(sc - m_new)
            l = alpha * l + jnp.sum(p, axis=-1, keepdims=True)
            acc = alpha * acc + jnp.dot(p.astype(BF16), v_ref[rows(j), :], preferred_element_type=F32)
            return m_new, l, acc

        init = (jnp.full((t, 1), NEG, F32), jnp.zeros((t, 1), F32), jnp.zeros((t, MLA_V), F32))
        carry = _pipelined_blocks(0, qi, nq - 1, issue, consume, init, True)
        m, l, acc = consume(qi, 0, carry, masked=True)
        o_ref[...] = acc / l
        lse_ref[...] = m + jnp.log2(l)

    outs = _pallas_call(
        body, name=name, grid=(nh, nq),
        in_specs=[pl.BlockSpec((None, t, MLA_QK), lambda h, i: (h, i, 0)), pl.BlockSpec((None, s, MLA_QK), lambda h, i: (h, 0, 0)),
                  pl.BlockSpec((None, s, MLA_V), lambda h, i: (h, 0, 0))] + [_HBM] * ng,
        out_specs=[pl.BlockSpec((None, t, MLA_V), lambda h, i: (h, i, 0)), pl.BlockSpec((None, t, 1), lambda h, i: (h, i, 0))]
        + [_HBM] * ng,
        out_shape=[jax.ShapeDtypeStruct((nh, s, MLA_V), F32), jax.ShapeDtypeStruct((nh, s, 1), F32)]
        + (_gather_out_shapes(gather) if ng else []),
        scratch_shapes=[pltpu.VMEM((t, t), F32)] * 2 + (_exchange_scratch(8, ng) if ng else []),
        compiler_params=_params(dimension_semantics=("arbitrary", "arbitrary")),
    )(q, k, v, *(gather or []))
    return outs[0], outs[1], outs[2:]


def _mla_bwd_dq(q, k, v, o, do, lse, name):
    nh, s, _ = q.shape
    t = _attn_tile(s)
    nq = s // t

    def body(q_ref, k_ref, v_ref, o_ref, do_ref, lse_ref, dq_ref, delta_ref, s0_ref, s1_ref, dp0_ref, dp1_ref):
        qi = pl.program_id(1)
        qv = q_ref[...]
        dov = do_ref[...]
        lse = lse_ref[...]
        delta = jnp.sum(dov * o_ref[...], axis=-1, keepdims=True)
        delta_ref[...] = delta
        dob = dov.astype(BF16)
        s_refs, dp_refs = (s0_ref, s1_ref), (dp0_ref, dp1_ref)

        def rows(j):
            return pl.ds(pl.multiple_of(j * t, t), t)

        def issue(j, slot):
            s_refs[slot][...] = _dot_nt(qv, k_ref[rows(j), :])
            dp_refs[slot][...] = _dot_nt(dob, v_ref[rows(j), :])

        def consume(j, slot, dq, masked=False):
            p = jnp.exp2(s_refs[slot][...] - lse)
            if masked:
                p = jnp.where(_causal_mask(t), p, 0.0)
            ds = p * (dp_refs[slot][...] - delta)
            return dq + jnp.dot(ds.astype(BF16), k_ref[rows(j), :], preferred_element_type=F32)

        dq = _pipelined_blocks(0, qi, nq - 1, issue, consume, jnp.zeros((t, MLA_QK), F32), True)
        dq_ref[...] = consume(qi, 0, dq, masked=True) * MLA_SCALE

    tile = lambda w: pl.BlockSpec((None, t, w), lambda h, i: (h, i, 0))
    whole = lambda w: pl.BlockSpec((None, s, w), lambda h, i: (h, 0, 0))
    return _pallas_call(
        body, name=name, grid=(nh, nq),
        in_specs=[tile(MLA_QK), whole(MLA_QK), whole(MLA_V), tile(MLA_V), tile(MLA_V), tile(1)],
        out_specs=[tile(MLA_QK), tile(1)],
        out_shape=[jax.ShapeDtypeStruct((nh, s, MLA_QK), F32), jax.ShapeDtypeStruct((nh, s, 1), F32)],
        scratch_shapes=[pltpu.VMEM((t, t), F32)] * 4,
        compiler_params=_params(dimension_semantics=("arbitrary", "arbitrary")),
    )(q, k, v, o, do, lse)


def _mla_bwd_dkv(q, k, v, do, lse_row, delta_row, name, scatter=None):
    nh, s, _ = q.shape
    t = _attn_tile(s)
    nq = s // t
    ng = len(scatter) if scatter else 0

    def body(q_ref, k_ref, v_ref, do_ref, lse_ref, delta_ref, *rest):
        c_ins, (dk_ref, dv_ref), c_outs = rest[:ng], rest[ng:ng + 2], rest[ng + 2:2 * ng + 2]
        (s0_ref, s1_ref, dp0_ref, dp1_ref), sems = rest[2 * ng + 2:2 * ng + 6], rest[2 * ng + 6:]
        if ng:
            _run_stages_at(_scatter_stages(c_ins, c_outs, *sems), [(0, 0), (nh - 1, nq - 1)])
        kj = pl.program_id(1)
        kv_, vv = k_ref[...], v_ref[...]
        s_refs, dp_refs = (s0_ref, s1_ref), (dp0_ref, dp1_ref)

        def rows(i):
            return pl.ds(pl.multiple_of(i * t, t), t)

        def issue(i, slot):
            s_refs[slot][...] = _dot_nt(kv_, q_ref[rows(i), :])
            dp_refs[slot][...] = _dot_nt(vv, do_ref[rows(i), :].astype(BF16))

        def consume(i, slot, carry, masked=False):
            dk, dv = carry
            p = jnp.exp2(s_refs[slot][...] - lse_ref[pl.ds(i, 1), :])
            if masked:
                p = jnp.where(lax.broadcasted_iota(jnp.int32, (t, t), 0) <= lax.broadcasted_iota(jnp.int32, (t, t), 1), p, 0.0)
            dv = dv + jnp.dot(p.astype(BF16), do_ref[rows(i), :].astype(BF16), preferred_element_type=F32)
            ds = p * (dp_refs[slot][...] - delta_ref[pl.ds(i, 1), :])
            dk = dk + jnp.dot(ds.astype(BF16), q_ref[rows(i), :], preferred_element_type=F32)
            return dk, dv

        issue(kj, 0)
        carry = consume(kj, 0, (jnp.zeros((t, MLA_QK), F32), jnp.zeros((t, MLA_V), F32)), masked=True)
        dk, dv = _pipelined_blocks(kj + 1, nq - 1 - kj, nq - 1, issue, consume, carry, False)
        dk_ref[...] = dk * LN2
        dv_ref[...] = dv

    tile = lambda w: pl.BlockSpec((None, t, w), lambda h, j: (h, j, 0))
    whole = lambda w: pl.BlockSpec((None, s, w), lambda h, j: (h, 0, 0))
    rows_spec = pl.BlockSpec((None, nq, t), lambda h, j: (h, 0, 0))
    outs = _pallas_call(
        body, name=name, grid=(nh, nq),
        in_specs=[whole(MLA_QK), tile(MLA_QK), tile(MLA_V), whole(MLA_V), rows_spec, rows_spec] + [_HBM] * ng,
        out_specs=[tile(MLA_QK), tile(MLA_V)] + [_HBM] * ng,
        out_shape=[jax.ShapeDtypeStruct((nh, s, MLA_QK), F32), jax.ShapeDtypeStruct((nh, s, MLA_V), F32)]
        + (_scatter_out_shapes(scatter) if ng else []),
        scratch_shapes=[pltpu.VMEM((t, t), F32)] * 4 + (_exchange_scratch(3, ng) if ng else []),
        compiler_params=_params(dimension_semantics=("arbitrary", "arbitrary")),
    )(q, k, v, do, lse_row, delta_row, *(scatter or []))
    return outs[0], outs[1], outs[2:]


def _swa_tile(s):
    return min(s, 4 * BLOCK)


def _swa_specs(tq):
    nb = tq // BLOCK
    grp = lambda w: pl.BlockSpec((SWA_GROUP, tq, w), lambda j, i: (j, i, 0))
    main = pl.BlockSpec((None, tq, SWA_HEAD_DIM), lambda j, i: (j, i, 0))
    tail = pl.BlockSpec((None, BLOCK, SWA_HEAD_DIM), lambda j, i: (j, nb * (i + 1), 0))
    sink = pl.BlockSpec((None, SWA_GROUP, 128), lambda j, i: (j, 0, 0))
    return grp, main, tail, sink


def _swa_band_mask(first):
    shape = (SWA_GROUP * BLOCK, 2 * BLOCK)
    q_rel = (lax.broadcasted_iota(jnp.int32, shape, 0) & (BLOCK - 1)) + BLOCK
    k_rel = lax.broadcasted_iota(jnp.int32, shape, 1)
    dist = q_rel - k_rel
    return (dist >= 0) & (dist < BLOCK) & ((k_rel >= BLOCK) | jnp.logical_not(first))


def _swa_sink_column(sink_ref):
    sk = sink_ref[...]
    return jnp.concatenate([jnp.broadcast_to(sk[g:g + 1, 0:1], (BLOCK, 1)) for g in range(SWA_GROUP)], axis=0)


def _swa_fwd(q, kpad, vpad, sinks, name):
    _, s, _ = q.shape
    tq = _swa_tile(s)
    grp, main, tail, sink = _swa_specs(tq)
    d = SWA_HEAD_DIM

    def body(q_ref, km_ref, kt_ref, vm_ref, vt_ref, sink_ref, o_ref, lse_ref):
        i = pl.program_id(1)
        kall = jnp.concatenate([km_ref[...], kt_ref[...]], axis=0)
        vall = jnp.concatenate([vm_ref[...], vt_ref[...]], axis=0)
        sink_col = _swa_sink_column(sink_ref)
        for b in range(tq // BLOCK):
            lo = b * BLOCK
            valid = _swa_band_mask(i == 0 if b == 0 else False)
            q4 = q_ref[:, lo:lo + BLOCK, :].reshape(SWA_GROUP * BLOCK, d)
            sc = jnp.where(valid, _dot_nt(q4, kall[lo:lo + 2 * BLOCK]) * SWA_SCALE, NEG)
            m = jnp.maximum(jnp.max(sc, axis=-1, keepdims=True), sink_col)
            e = jnp.exp(sc - m)
            den = jnp.sum(e, axis=-1, keepdims=True) + jnp.exp(sink_col - m)
            out = jnp.dot((e * (1.0 / den)).astype(BF16), vall[lo:lo + 2 * BLOCK], preferred_element_type=F32)
            o_ref[:, lo:lo + BLOCK, :] = out.reshape(SWA_GROUP, BLOCK, d)
            lse_ref[:, lo:lo + BLOCK, :] = (m + jnp.log(den)).reshape(SWA_GROUP, BLOCK, 1)

    return _pallas_call(
        body, name=name, grid=(SWA_KV_HEADS, s // tq),
        in_specs=[grp(d), main, tail, main, tail, sink], out_specs=[grp(d), grp(1)],
        out_shape=[jax.ShapeDtypeStruct((SWA_HEADS, s, d), F32), jax.ShapeDtypeStruct((SWA_HEADS, s, 1), F32)],
        compiler_params=_params(dimension_semantics=("arbitrary", "arbitrary")),
    )(q, kpad, kpad, vpad, vpad, sinks)


def _swa_bwd(q, kpad, vpad, sinks, o, lse, do, name):
    _, s, _ = q.shape
    tq = _swa_tile(s)
    grp, main, tail, sink = _swa_specs(tq)
    d = SWA_HEAD_DIM

    def body(q_ref, km_ref, kt_ref, vm_ref, vt_ref, sink_ref, o_ref, lse_ref, do_ref, dq_ref, dk_ref, dv_ref, dsink_ref):
        i = pl.program_id(1)
        kall = jnp.concatenate([km_ref[...], kt_ref[...]], axis=0)
        vall = jnp.concatenate([vm_ref[...], vt_ref[...]], axis=0)
        sink_col = _swa_sink_column(sink_ref)

        @pl.when(i == 0)
        def _():
            dk_ref[...] = jnp.zeros_like(dk_ref)
            dv_ref[...] = jnp.zeros_like(dv_ref)
            dsink_ref[...] = jnp.zeros_like(dsink_ref)

        dsink = jnp.zeros((SWA_GROUP * BLOCK, 1), F32)
        for b in range(tq // BLOCK):
            lo = b * BLOCK
            valid = _swa_band_mask(i == 0 if b == 0 else False)
            rows4 = SWA_GROUP * BLOCK
            q4 = q_ref[:, lo:lo + BLOCK, :].reshape(rows4, d)
            do4 = do_ref[:, lo:lo + BLOCK, :].reshape(rows4, d)
            lse4 = lse_ref[:, lo:lo + BLOCK, :].reshape(rows4, 1)
            delta = jnp.sum(do4 * o_ref[:, lo:lo + BLOCK, :].reshape(rows4, d), axis=-1, keepdims=True)
            kb, vb = kall[lo:lo + 2 * BLOCK], vall[lo:lo + 2 * BLOCK]
            do4b = do4.astype(BF16)
            p = jnp.where(valid, jnp.exp(_dot_nt(q4, kb) * SWA_SCALE - lse4), 0.0)
            ds = (p * (_dot_nt(do4b, vb) - delta) * SWA_SCALE).astype(BF16)
            dq_ref[:, lo:lo + BLOCK, :] = jnp.dot(ds, kb, preferred_element_type=F32).reshape(SWA_GROUP, BLOCK, d)
            band = pl.ds(pl.multiple_of(i * tq, BLOCK) + lo, 2 * BLOCK)
            dk_ref[band, :] += _dot_tn(ds, q4)
            dv_ref[band, :] += _dot_tn(p.astype(BF16), do4b)
            dsink = dsink - jnp.exp(sink_col - lse4) * delta
        per_head = [jnp.broadcast_to(jnp.sum(dsink[g * BLOCK:(g + 1) * BLOCK], axis=0, keepdims=True), (1, 128))
                    for g in range(SWA_GROUP)]
        dsink_ref[...] += jnp.concatenate(per_head + [jnp.zeros((8 - SWA_GROUP, 128), F32)], axis=0)

    acc = pl.BlockSpec((None, s + BLOCK, d), lambda j, i: (j, 0, 0))
    return _pallas_call(
        body, name=name, grid=(SWA_KV_HEADS, s // tq),
        in_specs=[grp(d), main, tail, main, tail, sink, grp(d), grp(1), grp(d)],
        out_specs=[grp(d), acc, acc, pl.BlockSpec((None, 8, 128), lambda j, i: (j, 0, 0))],
        out_shape=[jax.ShapeDtypeStruct((SWA_HEADS, s, d), F32),
                   jax.ShapeDtypeStruct((SWA_KV_HEADS, s + BLOCK, d), F32),
                   jax.ShapeDtypeStruct((SWA_KV_HEADS, s + BLOCK, d), F32),
                   jax.ShapeDtypeStruct((SWA_KV_HEADS, 8, 128), F32)],
        compiler_params=_params(dimension_semantics=("arbitrary", "arbitrary")),
    )(q, kpad, kpad, vpad, vpad, sinks, o, lse, do)


def _loss_head(y, target, name):
    s = y.shape[0]
    tm = _row_tile(s, 512)

    def body(y_ref, t_ref, dy_ref, loss_ref):
        i = pl.program_id(0)
        err = y_ref[...] - t_ref[...]
        dy_ref[...] = err * (1.0 / D_MODEL)
        part = jnp.broadcast_to(0.5 * jnp.sum(jnp.mean(err * err, axis=-1, keepdims=True), axis=0, keepdims=True), (1, 128))

        @pl.when(i == 0)
        def _():
            loss_ref[...] = part

        @pl.when(i > 0)
        def _():
            loss_ref[...] += part

    row = pl.BlockSpec((tm, D_MODEL), lambda i: (i, 0))
    return _pallas_call(
        body, name=name, grid=(s // tm,), in_specs=[row, row], out_specs=[row, _full((1, 128))],
        out_shape=[jax.ShapeDtypeStruct((s, D_MODEL), F32), jax.ShapeDtypeStruct((1, 128), F32)],
        compiler_params=_params(dimension_semantics=("arbitrary",)),
    )(y, target)


def _adamw(w, g, m, v, name):
    rows, cols = w.shape
    tr = rows
    for cand in (512, 256, 128, 64, 32, 16, 8):
        if rows % cand == 0 and rows > cand:
            tr = cand
            break

    def body(w_ref, g_ref, m_ref, v_ref, d_ref, nm_ref, nv_ref):
        gv = g_ref[...]
        nm = ADAM_B1 * m_ref[...] + (1.0 - ADAM_B1) * gv
        nv = ADAM_B2 * v_ref[...] + (1.0 - ADAM_B2) * (gv * gv)
        m_hat = nm / (1.0 - ADAM_B1 ** ADAM_STEP)
        v_hat = nv / (1.0 - ADAM_B2 ** ADAM_STEP)
        d_ref[...] = -ADAM_LR * (m_hat / (jnp.sqrt(v_hat) + ADAM_EPS) + ADAM_WD * w_ref[...])
        nm_ref[...] = nm
        nv_ref[...] = nv

    blk = pl.BlockSpec((tr, cols), lambda i: (i, 0))
    return _pallas_call(
        body, name=name, grid=(rows // tr,), in_specs=[blk] * 4, out_specs=[blk] * 3,
        out_shape=[jax.ShapeDtypeStruct((rows, cols), F32)] * 3,
        compiler_params=_params(dimension_semantics=("arbitrary",)),
    )(w, g, m, v)


def _position():
    return lax.axis_index("x"), lax.axis_index("y"), lax.axis_index("c")


def _remote(src, dst, send_sems, recv_sems, k, to):
    return pltpu.make_async_remote_copy(src_ref=src, dst_ref=dst, send_sem=send_sems.at[k], recv_sem=recv_sems.at[k],
                                        device_id=to, device_id_type=MESH)


_HBM = pl.BlockSpec(memory_space=pltpu.HBM)


def _gather_stages(ins, outs, send_sems, recv_sems):
    na = len(ins)
    x, y, c = _position()
    me, sibling = (x, y, c), (x, y, 1 - c)
    xn, yn, dg = (1 - x, y), (x, 1 - y), (1 - x, 1 - y)

    def slot(a, chip, pc, half=None):
        ref = outs[a].at[4 * chip[0] + 2 * chip[1] + pc]
        if half is None:
            return ref
        rows = ref.shape[0] // 2
        return ref.at[pl.ds(half * rows, rows)]

    def cp(a, k, chip, pc, half, to, src=None):
        dst = slot(a, chip, pc, half)
        return _remote(dst if src is None else src, dst, send_sems, recv_sems, 8 * a + k, to)

    first_hop = [(0, xn), (1, yn)]
    second_hop = [(0, xn, 2, 0, yn), (1, yn, 3, 1, xn)]

    def sends():
        out = []
        for a in range(na):
            out += [cp(a, k, (x, y), c, None, (*to, c), src=ins[a].at[c]) for k, to in first_hop]
            out += [cp(a, fwd_k, frm, c, half, (*to, c)) for _, frm, fwd_k, half, to in second_hop]
            out += [cp(a, 4 + k, frm, c, None, sibling) for k, frm in first_hop]
            out += [cp(a, 6 + half, dg, c, half, sibling) for half in (0, 1)]
        return out

    def stage0():
        for a in range(na):
            for k, to in first_hop:
                cp(a, k, (x, y), c, None, (*to, c), src=ins[a].at[c]).start()

    def stage1():
        for k, frm, fwd_k, half, to in second_hop:
            for a in range(na):
                cp(a, k, frm, c, None, me).wait_recv()
                cp(a, fwd_k, frm, c, half, (*to, c)).start()
                cp(a, 4 + k, frm, c, None, sibling).start()

    def stage2():
        for half in (0, 1):
            for a in range(na):
                cp(a, 2 + half, dg, c, half, me).wait_recv()
                cp(a, 6 + half, dg, c, half, sibling).start()

    def stage3():
        for a in range(na):
            for k, chip, half in ((4, xn, None), (5, yn, None), (6, dg, 0), (7, dg, 1)):
                cp(a, k, chip, 1 - c, half, me).wait_recv()
        for sent in sends():
            sent.wait_send()

    return [stage0, stage1, stage2, stage3]


def _gather_out_shapes(mine):
    return [jax.ShapeDtypeStruct((N_DEV,) + m.shape[1:], m.dtype) for m in mine]


def _exchange_scratch(per_array, na):
    return [pltpu.SemaphoreType.DMA((per_array * na,)), pltpu.SemaphoreType.DMA((per_array * na,))]


def _all_gather_halves(mine, name):
    na = len(mine)

    def body(*refs):
        for stage in _gather_stages(refs[:na], refs[na:2 * na], *refs[2 * na:]):
            stage()

    return _pallas_call(body, name=name, in_specs=[_HBM] * na, out_specs=[_HBM] * na, out_shape=_gather_out_shapes(mine),
                        scratch_shapes=_exchange_scratch(8, na))(*mine)


def _sibling_exchange(parts, name, other_half):
    na = len(parts)

    def body(*refs):
        ins, outs, (send_sems, recv_sems) = refs[:na], refs[na:2 * na], refs[2 * na:]
        x, y, c = _position()
        copies = [_remote(ins[a].at[:, 1 - c] if other_half else ins[a], outs[a], send_sems, recv_sems, a, (x, y, 1 - c))
                  for a in range(na)]
        for cp in copies:
            cp.start()
        for cp in copies:
            cp.wait()

    return _pallas_call(
        body, name=name, in_specs=[_HBM] * na, out_specs=[_HBM] * na,
        out_shape=[jax.ShapeDtypeStruct(p.shape[:1] + p.shape[2:] if other_half else p.shape, p.dtype) for p in parts],
        scratch_shapes=_exchange_scratch(1, na),
    )(*parts)


def _scatter_stages(ins, outs, send_sems, recv_sems):
    na = len(ins)
    x, y, c = _position()
    chips = [(1 - x, y), (x, 1 - y), (1 - x, 1 - y)]

    def copies():
        return [_remote(ins[a].at[2 * px + py], outs[a].at[j], send_sems, recv_sems, 3 * a + j, (px, py, c))
                for a in range(na) for j, (px, py) in enumerate(chips)]

    def start():
        for cp in copies():
            cp.start()

    def wait():
        for cp in copies():
            cp.wait()

    return [start, wait]


def _scatter_out_shapes(parts):
    return [jax.ShapeDtypeStruct((3,) + p.shape[1:], p.dtype) for p in parts]


def _scatter_to_chips(parts, name):
    na = len(parts)

    def body(*refs):
        for stage in _scatter_stages(refs[:na], refs[na:2 * na], *refs[2 * na:]):
            stage()

    return _pallas_call(body, name=name, in_specs=[_HBM] * na, out_specs=[_HBM] * na, out_shape=_scatter_out_shapes(parts),
                        scratch_shapes=_exchange_scratch(3, na))(*parts)


def _assemble(g4, mine, name, side_by_side):
    _, nl, r, w = g4.shape
    tr = next(cand for cand in (256, 128, 64) if r % cand == 0)

    def body(in_ref, mine_ref, out_ref):
        chip = 2 * lax.axis_index("x") + lax.axis_index("y")
        blocks = [jnp.where(chip == sh, mine_ref[...], in_ref[sh]) for sh in range(N_SHARD)]
        if side_by_side:
            out_ref[...] = jnp.concatenate(blocks, axis=-1)
        else:
            for sh in range(N_SHARD):
                out_ref[sh] = blocks[sh]

    if side_by_side:
        out_spec = pl.BlockSpec((None, tr, N_SHARD * w), lambda l, i: (l, i, 0))
        out_shape = jax.ShapeDtypeStruct((nl, r, N_SHARD * w), g4.dtype)
    else:
        out_spec = pl.BlockSpec((None, N_SHARD, tr, w), lambda l, i: (l, 0, i, 0))
        out_shape = jax.ShapeDtypeStruct((nl, N_SHARD, r, w), g4.dtype)
    return _pallas_call(
        body, name=name, grid=(nl, r // tr),
        in_specs=[pl.BlockSpec((N_SHARD, None, tr, w), lambda l, i: (0, l, i, 0)),
                  pl.BlockSpec((None, tr, w), lambda l, i: (l, i, 0))],
        out_specs=out_spec, out_shape=out_shape,
        compiler_params=_params(dimension_semantics=("arbitrary", "arbitrary")),
    )(g4, mine)


def _all_reduce_small(vec, name):
    r, l = vec.shape

    def body(v_ref, out_ref, gath_ref, send_sems, recv_sems):
        x, y, c = _position()
        me = 4 * x + 2 * y + c
        gath_ref[me] = v_ref[...]
        copies = []
        for k in range(1, N_DEV):
            to = (x ^ (k >> 2), y ^ ((k >> 1) & 1), c ^ (k & 1))
            copies.append(_remote(gath_ref.at[me], gath_ref.at[me], send_sems, recv_sems, k - 1, to))
        for cp in copies:
            cp.start()
        for k in range(1, N_DEV):
            frm = 4 * (x ^ (k >> 2)) + 2 * (y ^ ((k >> 1) & 1)) + (c ^ (k & 1))
            _remote(gath_ref.at[frm], gath_ref.at[frm], send_sems, recv_sems, k - 1, (x, y, c)).wait_recv()
        for cp in copies:
            cp.wait_send()
        total = gath_ref[0]
        for d in range(1, N_DEV):
            total = total + gath_ref[d]
        out_ref[...] = total

    vm = pl.BlockSpec(memory_space=pltpu.VMEM)
    return _pallas_call(
        body, name=name, in_specs=[vm], out_specs=vm, out_shape=jax.ShapeDtypeStruct((r, l), F32),
        scratch_shapes=[pltpu.VMEM((N_DEV, r, l), F32), pltpu.SemaphoreType.DMA((N_DEV - 1,)),
                        pltpu.SemaphoreType.DMA((N_DEV - 1,))],
    )(vec)


def _sum_blocks(blocks, out_dtype, name):
    m, w = blocks[0].shape
    tr = next(cand for cand in (512, 256, 128, 64, 32) if m % cand == 0)

    def body(*refs):
        total = refs[0][...].astype(F32)
        for ref in refs[1:-1]:
            total = total + ref[...].astype(F32)
        refs[-1][...] = total.astype(out_dtype)

    blk = pl.BlockSpec((tr, w), lambda i: (i, 0))
    return _pallas_call(
        body, name=name, grid=(m // tr,), in_specs=[blk] * len(blocks), out_specs=blk,
        out_shape=jax.ShapeDtypeStruct((m, w), out_dtype),
        compiler_params=_params(dimension_semantics=("arbitrary",)),
    )(*blocks)


FIRST_GROUPS = (('ffn1_w_gate', 'ffn1_w_up'), ('ffn1_w_down',))
REST_GROUPS = (('ffn2_w_gate', 'ffn2_w_up'), ('ffn2_w_down', 'w_o'), ('w_in',), ('mla_w_q_b',), ('mla_w_kv_b',))
N_FIRST = len(FIRST_GROUPS)


def _shard_rows(name):
    shape, axis = BIG[name]
    return shape[0] // N_SHARD if axis == 0 else shape[0]


def _group_row_offsets(group):
    return [int(v) for v in np.cumsum([0] + [_shard_rows(n) for n in group])]


def _rope_tables(s):
    pos = jnp.arange(s, dtype=F32)
    inv = 1.0 / (ROPE_THETA ** (jnp.arange(0, MLA_ROPE, 2, dtype=F32) / MLA_ROPE))
    ang = pos[:, None] * inv[None, :]
    return jnp.tile(jnp.cos(ang), (1, SWA_HEADS)), jnp.tile(jnp.sin(ang), (1, SWA_HEADS))


_MIXER_GAINS = ('mix_norm', 'mla_q_a_norm', 'mla_kv_a_norm', 'mla_q_norm', 'mla_k_norm', 'swa_q_norm', 'swa_k_norm')


def _local_step(x, target, small, ex):
    s = x.shape[0]
    cos, sin = _rope_tables(s)
    row = lambda name, l: small[name][l][None, :]
    saved, bigs = [], []
    for l in range(DEPTH):
        big = ex.first_weights(l)
        sv = {'x0': x}
        x, sv['g1'], sv['u1'], got = _ffn_fwd(x, row('ffn1_norm', l), big['ffn1_w_gate'], big['ffn1_w_up'], big['ffn1_w_down'],
                                             f"ffn1_fwd_{l}", ex.gather_behind_ffn1(l))
        big.update(ex.rest_weights(l, got))
        bigs.append(big)
        sv['x1'] = x
        gains = [row(n, l) for n in _MIXER_GAINS]
        mixer_w = (_take_cols(big['w_in'], _IN_ORDER), _take_cols(big['mla_w_q_b'], _QB_ORDER),
                   _take_cols(big['mla_w_kv_b'], _KVB_ORDER))
        q_a, k_a, v_a, q_b, k_b, v_b = _pre_fwd(x, gains, *mixer_w, cos, sin, f"pre_fwd_{l}")
        o_a, lse, got = _mla_fwd(q_a, k_a, v_a, f"mla_fwd_{l}", ex.gather_behind_mla(l))
        ex.gathered_behind_mla(l, got)
        kpad = jnp.pad(k_b, ((0, 0), (BLOCK, 0), (0, 0)))
        vpad = jnp.pad(v_b, ((0, 0), (BLOCK, 0), (0, 0)))
        sinks = jnp.broadcast_to(small['swa_sinks'][l].reshape(SWA_KV_HEADS, SWA_GROUP, 1), (SWA_KV_HEADS, SWA_GROUP, 128))
        o_b, lse_b = _swa_fwd(q_b, kpad, vpad, sinks, f"swa_fwd_{l}")
        sv.update(gains=gains, mixer_w=mixer_w, q_a=q_a, k_a=k_a, v_a=v_a, q_b=q_b, kpad=kpad, vpad=vpad, sinks=sinks,
                  o_a=o_a, lse=lse, o_b=o_b, lse_b=lse_b)
        x = _post_fwd(x, o_a, o_b, row('mla_out_norm', l), row('swa_out_norm', l), big['w_o'], f"post_fwd_{l}")
        sv['x2'] = x
        x, sv['g2'], sv['u2'], _ = _ffn_fwd(x, row('ffn2_norm', l), big['ffn2_w_gate'], big['ffn2_w_up'], big['ffn2_w_down'],
                                           f"ffn2_fwd_{l}")
        saved.append(sv)

    dx, loss = _loss_head(x, target, "loss_head")

    gs = {n: [None] * DEPTH for n in SMALL_NAMES}
    t = _attn_tile(s)
    for l in reversed(range(DEPTH)):
        sv, big = saved[l], bigs[l]

        def ffn_back(tag, xin, dy, gate, up, scatter=None):
            (dxi, dgain, nb, act, dgate, dup), got = _ffn_bwd(xin, dy, gate, up, row(tag + '_norm', l), big[tag + '_w_gate'],
                                                             big[tag + '_w_up'], big[tag + '_w_down'], f"{tag}_bwd_{l}", scatter)
            gs[tag + '_norm'][l] = dgain[0]
            grads = {tag + '_w_gate': _matmul_tn(nb, dgate, 1.0, f"{tag}_dwg_{l}", True),
                     tag + '_w_up': _matmul_tn(nb, dup, 1.0, f"{tag}_dwu_{l}", True),
                     tag + '_w_down': _matmul_tn(act, dy, 0.5, f"{tag}_dwd_{l}", False).reshape(N_SHARD, D_FF // N_SHARD, D_MODEL)}
            return dxi, grads, got

        dx, rest_grads, _ = ffn_back('ffn2', sv['x2'], dx, sv['g2'], sv['u2'])
        do_a, do_b, dga, dgb, dwo = _post_bwd(dx, sv['o_a'], sv['o_b'], row('mla_out_norm', l), row('swa_out_norm', l),
                                              big['w_o'], f"post_bwd_{l}")
        gs['mla_out_norm'][l], gs['swa_out_norm'][l] = dga[0], dgb[0]
        rest_grads['w_o'] = dwo.reshape(N_SHARD, MIX_WIDTH // N_SHARD, D_MODEL)
        dq_a, delta = _mla_bwd_dq(sv['q_a'], sv['k_a'], sv['v_a'], sv['o_a'], do_a, sv['lse'], f"mla_dq_{l}")
        dk_a, dv_a, got = _mla_bwd_dkv(sv['q_a'], sv['k_a'], sv['v_a'], do_a, sv['lse'].reshape(MLA_HEADS, s // t, t),
                                       delta.reshape(MLA_HEADS, s // t, t), f"mla_dkv_{l}", ex.scatter_behind_mla(l))
        ex.scattered_behind_mla(l, got)
        dq_b, dkpad, dvpad, dsink = _swa_bwd(sv['q_b'], sv['kpad'], sv['vpad'], sv['sinks'], sv['o_b'], sv['lse_b'], do_b,
                                             f"swa_bwd_{l}")
        gs['swa_sinks'][l] = dsink[:, :SWA_GROUP, 0].reshape(SWA_HEADS)
        cts = [dq_a, dk_a, dv_a, dq_b, dkpad[:, BLOCK:], dvpad[:, BLOCK:]]
        outs = _pre_bwd(sv['x1'], dx, cts, sv['gains'], *sv['mixer_w'], cos, sin, f"pre_bwd_{l}")
        dx = outs[0]
        for n, val in zip(_MIXER_GAINS, outs[1:8]):
            gs[n][l] = val[0]
        rest_grads['w_in'], rest_grads['mla_w_q_b'], rest_grads['mla_w_kv_b'] = outs[8:11]
        ex.grads_ready(l, 'rest', rest_grads)
        dx, first_grads, got = ffn_back('ffn1', sv['x0'], dx, sv['g1'], sv['u1'], ex.scatter_behind_ffn1(l))
        ex.scattered_behind_ffn1(l, got)
        ex.grads_ready(l, 'first', first_grads)
    return loss, dx, gs


class _Exchange:
    def __init__(self, weights, c, chip):
        halves_of = lambda a: a.reshape(a.shape[:-2] + (2, a.shape[-2] // 2, a.shape[-1]))
        self.halves_of, self.c, self.chip = halves_of, c, chip
        self.mine = [[halves_of(jnp.concatenate([weights[n][l].astype(BF16) for n in group], axis=0))
                      for group in FIRST_GROUPS + REST_GROUPS] for l in range(DEPTH)]
        self.ahead, self.begun, self.received = {}, {}, {}

    def _assembled(self, l, which, gathered):
        groups, base = (FIRST_GROUPS, 0) if which == 'first' else (REST_GROUPS, N_FIRST)
        big = {}
        for gi, group in enumerate(groups):
            offs = _group_row_offsets(group)
            _, rh, w = self.mine[l][base + gi].shape
            col_sharded = BIG[group[0]][1] == 1
            full = _assemble(gathered[gi].reshape(N_SHARD, 1, 2 * rh, w), self.mine[l][base + gi].reshape(1, 2 * rh, w),
                             f"assemble_{which}{gi}_{l}", col_sharded)
            for i, n in enumerate(group):
                rows = offs[i + 1] - offs[i]
                if len(group) > 1 and offs[i] % rows == 0:
                    big[n] = ((full, (None, rows, N_SHARD * w), (0, offs[i] // rows, 0)) if col_sharded else
                              (full, (None, N_SHARD, rows, w), (0, 0, offs[i] // rows, 0)))
                elif col_sharded:
                    big[n] = full[0, offs[i]:offs[i + 1]]
                else:
                    big[n] = full[0, :, offs[i]:offs[i + 1]].reshape(BIG[n][0])
        return big

    def first_weights(self, l):
        got = self.ahead[l][:N_FIRST] if l in self.ahead else _all_gather_halves(self.mine[l][:N_FIRST], f"gather_first_{l}")
        return self._assembled(l, 'first', got)

    def gather_behind_ffn1(self, l):
        return None if l in self.ahead else self.mine[l][N_FIRST:]

    def rest_weights(self, l, got):
        return self._assembled(l, 'rest', self.ahead[l][N_FIRST:] if l in self.ahead else got)

    def gather_behind_mla(self, l):
        return self.mine[l + 1] if l + 1 < DEPTH else None

    def gathered_behind_mla(self, l, got):
        if got:
            self.ahead[l + 1] = got

    def grads_ready(self, l, which, grads):
        groups = FIRST_GROUPS if which == 'first' else REST_GROUPS
        parts = [self.halves_of(jnp.concatenate([grads[n] for n in group], axis=1)) for group in groups]
        from_sibling = _sibling_exchange(parts, f"swap_{which}_{l}", True)
        chip_sums = []
        for gi, (p, got) in enumerate(zip(parts, from_sibling)):
            kept = lax.dynamic_index_in_dim(p, self.c, axis=1, keepdims=False)
            rows = N_SHARD * p.shape[2]
            pair = _sum_blocks([kept.reshape(rows, -1), got.reshape(rows, -1)], BF16, f"sum_pair_{which}{gi}_{l}")
            chip_sums.append(pair.reshape(got.shape))
        self.begun[(l, which)] = chip_sums

    def scatter_behind_mla(self, l):
        return self.begun[(l + 1, 'first')] + self.begun[(l + 1, 'rest')] if l + 1 < DEPTH else None

    def scattered_behind_mla(self, l, got):
        if got:
            self.received[(l + 1, 'first')], self.received[(l + 1, 'rest')] = got[:N_FIRST], got[N_FIRST:]

    def scatter_behind_ffn1(self, l):
        return self.begun[(l, 'rest')] if l == 0 else None

    def scattered_behind_ffn1(self, l, got):
        if got:
            self.received[(l, 'rest')] = got

    def reduced(self):
        keys = sorted(self.begun)
        for key in keys:
            if key not in self.received:
                self.received[key] = _scatter_to_chips(self.begun[key], f"scatter_{key[1]}_{key[0]}")
        halves = []
        for l, which in keys:
            for gi, (cs, got) in enumerate(zip(self.begun[(l, which)], self.received[(l, which)])):
                own = lax.dynamic_index_in_dim(cs, self.chip, axis=0, keepdims=False)
                halves.append(_sum_blocks([own, got[0], got[1], got[2]], F32, f"sum_chips_{which}{gi}_{l}"))
        others = _sibling_exchange(halves, "share_halves", False)
        per_layer, at = {}, 0
        for l, which in keys:
            for group in (FIRST_GROUPS if which == 'first' else REST_GROUPS):
                mine_h, other_h = halves[at], others[at]
                at += 1
                full = jnp.where(self.c == 0, jnp.concatenate([mine_h, other_h]), jnp.concatenate([other_h, mine_h]))
                offs = _group_row_offsets(group)
                for i, n in enumerate(group):
                    per_layer[(n, l)] = full[offs[i]:offs[i + 1]]
        return {n: jnp.stack([per_layer[(n, l)] for l in range(DEPTH)]) for n in BIG_NAMES}


def kernel(x, ffn1_norm, ffn1_w_gate, ffn1_w_up, ffn1_w_down, mix_norm, w_in, mla_q_a_norm, mla_w_q_b, mla_kv_a_norm, mla_w_kv_b, mla_q_norm, mla_k_norm, swa_q_norm, swa_k_norm, swa_sinks, mla_out_norm, swa_out_norm, w_o, ffn2_norm, ffn2_w_gate, ffn2_w_up, ffn2_w_down, loss_target, m_ffn1_norm, m_ffn1_w_gate, m_ffn1_w_up, m_ffn1_w_down, m_mix_norm, m_w_in, m_mla_q_a_norm, m_mla_w_q_b, m_mla_kv_a_norm, m_mla_w_kv_b, m_mla_q_norm, m_mla_k_norm, m_swa_q_norm, m_swa_k_norm, m_swa_sinks, m_mla_out_norm, m_swa_out_norm, m_w_o, m_ffn2_norm, m_ffn2_w_gate, m_ffn2_w_up, m_ffn2_w_down, v_ffn1_norm, v_ffn1_w_gate, v_ffn1_w_up, v_ffn1_w_down, v_mix_norm, v_w_in, v_mla_q_a_norm, v_mla_w_q_b, v_mla_kv_a_norm, v_mla_w_kv_b, v_mla_q_norm, v_mla_k_norm, v_swa_q_norm, v_swa_k_norm, v_swa_sinks, v_mla_out_norm, v_swa_out_norm, v_w_o, v_ffn2_norm, v_ffn2_w_gate, v_ffn2_w_up, v_ffn2_w_down):
    args = dict(locals())
    weights = {n: args[n] for n in WEIGHT_NAMES}
    mom_m = {n: args["m_" + n] for n in WEIGHT_NAMES}
    mom_v = {n: args["v_" + n] for n in WEIGHT_NAMES}
    ex = _Exchange(weights, lax.axis_index("c"), 2 * lax.axis_index("x") + lax.axis_index("y"))
    loss, dx, gs = _local_step(x[0], loss_target[0], {n: weights[n] for n in SMALL_NAMES}, ex)

    small_flat = jnp.concatenate([jnp.stack(gs[n]).reshape(-1) for n in SMALL_NAMES] + [loss[0, :1]])
    n_small = small_flat.shape[0]
    lanes = -(-n_small // (8 * 128)) * 128
    small_sum = _all_reduce_small(jnp.pad(small_flat, (0, 8 * lanes - n_small)).reshape(8, lanes), "reduce_small").reshape(-1)
    grads = ex.reduced()
    off = 0
    for n in SMALL_NAMES:
        cnt = int(np.prod(weights[n].shape))
        grads[n] = small_sum[off:off + cnt].reshape(weights[n].shape)
        off += cnt
    loss_out = small_sum[off]

    deltas, new_m, new_v = {}, {}, {}
    for n in WEIGHT_NAMES:
        shp = weights[n].shape
        two_d = (DEPTH, shp[-1]) if len(shp) == 2 else (shp[0] * shp[1], shp[2])
        d, nm, nv = _adamw(weights[n].reshape(two_d), grads[n].reshape(two_d), mom_m[n].reshape(two_d),
                           mom_v[n].reshape(two_d), f"adamw_{n}")
        deltas[n], new_m[n], new_v[n] = d.reshape(shp), nm.reshape(shp), nv.reshape(shp)

    return (loss_out, dx[None], *[grads[n] for n in WEIGHT_NAMES], *[deltas[n] for n in WEIGHT_NAMES],
            *[new_m[n] for n in WEIGHT_NAMES], *[new_v[n] for n in WEIGHT_NAMES])
```

```python
import functools

import numpy as np
import jax
import jax.numpy as jnp
from jax import lax
from jax.experimental import pallas as pl
from jax.experimental.pallas import tpu as pltpu

F32 = jnp.float32
BF16 = jnp.bfloat16

D_MODEL = 1024
DEPTH = 2
EPS = 1e-6
ROPE_THETA = 10000.0
BLOCK = 128
MLA_HEADS = 4
MLA_Q_RANK = 256
MLA_KV_RANK = 128
MLA_NOPE = 128
MLA_ROPE = 64
MLA_V = 128
MLA_QK = MLA_NOPE + MLA_ROPE
MLA_WIDTH = MLA_HEADS * MLA_V
SWA_HEADS = 8
SWA_KV_HEADS = 2
SWA_GROUP = SWA_HEADS // SWA_KV_HEADS
SWA_HEAD_DIM = 64
SWA_WIDTH = SWA_HEADS * SWA_HEAD_DIM
MIX_WIDTH = MLA_WIDTH + SWA_WIDTH
IN_SPLITS = (MLA_Q_RANK, MLA_KV_RANK, MLA_ROPE, SWA_WIDTH, SWA_KV_HEADS * SWA_HEAD_DIM, SWA_KV_HEADS * SWA_HEAD_DIM)
IN_COLS = sum(IN_SPLITS)
IN_OFFS = tuple(int(v) for v in np.cumsum((0,) + IN_SPLITS))
D_FF = 2816
MLA_SCALE = MLA_QK ** -0.5
LOG2E = 1.4426950408889634
LN2 = 0.6931471805599453
MLA_QSCALE = MLA_SCALE * LOG2E
SWA_SCALE = SWA_HEAD_DIM ** -0.5
NEG = -1e30

ADAM_LR = 0.001
ADAM_B1 = 0.9
ADAM_B2 = 0.999
ADAM_EPS = 1e-08
ADAM_WD = 0.01
ADAM_STEP = 10

N_SHARD = 4
N_DEV = 8
VMEM_LIMIT = 56 * 1024 * 1024
MESH = pl.DeviceIdType.MESH

WEIGHT_NAMES = ['ffn1_norm', 'ffn1_w_gate', 'ffn1_w_up', 'ffn1_w_down', 'mix_norm', 'w_in', 'mla_q_a_norm', 'mla_w_q_b',
                'mla_kv_a_norm', 'mla_w_kv_b', 'mla_q_norm', 'mla_k_norm', 'swa_q_norm', 'swa_k_norm', 'swa_sinks',
                'mla_out_norm', 'swa_out_norm', 'w_o', 'ffn2_norm', 'ffn2_w_gate', 'ffn2_w_up', 'ffn2_w_down']
BIG = {'ffn1_w_gate': ((D_MODEL, D_FF), 1), 'ffn1_w_up': ((D_MODEL, D_FF), 1), 'ffn1_w_down': ((D_FF, D_MODEL), 0),
       'w_in': ((D_MODEL, IN_COLS), 1), 'mla_w_q_b': ((MLA_Q_RANK, MLA_HEADS * MLA_QK), 1),
       'mla_w_kv_b': ((MLA_KV_RANK, MLA_HEADS * (MLA_NOPE + MLA_V)), 1), 'w_o': ((MIX_WIDTH, D_MODEL), 0),
       'ffn2_w_gate': ((D_MODEL, D_FF), 1), 'ffn2_w_up': ((D_MODEL, D_FF), 1), 'ffn2_w_down': ((D_FF, D_MODEL), 0)}
BIG_NAMES = [n for n in WEIGHT_NAMES if n in BIG]
SMALL_NAMES = [n for n in WEIGHT_NAMES if n not in BIG]

_pallas_call = pl.pallas_call


def _params(**kw):
    return pltpu.CompilerParams(vmem_limit_bytes=VMEM_LIMIT, **kw)


def _full(shape):
    n = len(shape)
    return pl.BlockSpec(shape, lambda *_: (0,) * n)


def _resident(shape):
    n = len(shape)
    return pl.BlockSpec(shape, lambda *_: (0,) * n, pipeline_mode=pl.Buffered(1))


@jax.custom_vjp
def _mm(a, w):
    return jnp.dot(a.astype(BF16), w, preferred_element_type=F32)


def _mm_fwd(a, w):
    return _mm(a, w), w


def _mm_bwd(w, dy):
    return lax.dot_general(dy.astype(BF16), w, (((1,), (1,)), ((), ())), preferred_element_type=F32), None


_mm.defvjp(_mm_fwd, _mm_bwd)


def _dot_nt(a, b):
    return lax.dot_general(a, b, (((1,), (1,)), ((), ())), preferred_element_type=F32)


def _dot_tn(a, b):
    return lax.dot_general(a, b, (((0,), (0,)), ((), ())), preferred_element_type=F32)


def _rms(t, g):
    return t * lax.rsqrt(jnp.mean(t * t, axis=-1, keepdims=True) + EPS) * g


def _sigmoid(z):
    return 1.0 / (1.0 + jnp.exp(-z))


def _row_tile(s, want):
    return min(want, s)


FF_CHUNK = 1408


def _weight_operand(w):
    if isinstance(w, tuple):
        arr, block, index = w
        return arr, pl.BlockSpec(block, lambda *_: index, pipeline_mode=pl.Buffered(1))
    return w, _resident(w.shape)


def _weight_rows(ref, start, n):
    if len(ref.shape) == 2:
        return ref[start:start + n, :]
    per = ref.shape[1]
    return ref[start // per:(start + n) // per].reshape(n, ref.shape[2])


def _ffn_fwd(x, g, wg, wu, wd, name, gather=None):
    s = x.shape[0]
    tm = _row_tile(s, 256)
    steps = s // tm
    ng = len(gather) if gather else 0
    (wg, wg_spec), (wu, wu_spec), (wd, wd_spec) = _weight_operand(wg), _weight_operand(wu), _weight_operand(wd)

    def body(x_ref, g_ref, wg_ref, wu_ref, wd_ref, *rest):
        g_ins, (y_ref, gate_ref, up_ref), g_outs, sems = rest[:ng], rest[ng:ng + 3], rest[ng + 3:2 * ng + 3], rest[2 * ng + 3:]
        if ng:
            _run_stages_at(_gather_stages(g_ins, g_outs, *sems), [(0,), (steps * 3 // 8,), (steps * 11 // 16,), (steps - 1,)])
        xv = x_ref[...]
        nb = _rms(xv, g_ref[...]).astype(BF16)
        acc = xv
        for c in range(0, D_FF, FF_CHUNK):
            gate = jnp.dot(nb, wg_ref[:, c:c + FF_CHUNK], preferred_element_type=F32)
            up = jnp.dot(nb, wu_ref[:, c:c + FF_CHUNK], preferred_element_type=F32)
            gate_ref[:, c:c + FF_CHUNK] = gate.astype(BF16)
            up_ref[:, c:c + FF_CHUNK] = up.astype(BF16)
            act = (gate * _sigmoid(gate) * up).astype(BF16)
            acc = acc + 0.5 * jnp.dot(act, _weight_rows(wd_ref, c, FF_CHUNK), preferred_element_type=F32)
        y_ref[...] = acc

    outs = _pallas_call(
        body, name=name, grid=(steps,),
        in_specs=[pl.BlockSpec((tm, D_MODEL), lambda i: (i, 0)), _full((1, D_MODEL)), wg_spec, wu_spec, wd_spec] + [_HBM] * ng,
        out_specs=[pl.BlockSpec((tm, D_MODEL), lambda i: (i, 0)), pl.BlockSpec((tm, D_FF), lambda i: (i, 0)),
                   pl.BlockSpec((tm, D_FF), lambda i: (i, 0))] + [_HBM] * ng,
        out_shape=[jax.ShapeDtypeStruct((s, D_MODEL), F32), jax.ShapeDtypeStruct((s, D_FF), BF16),
                   jax.ShapeDtypeStruct((s, D_FF), BF16)] + (_gather_out_shapes(gather) if ng else []),
        scratch_shapes=_exchange_scratch(8, ng) if ng else [],
        compiler_params=_params(dimension_semantics=("arbitrary",)),
    )(x, g, wg, wu, wd, *(gather or []))
    return outs[0], outs[1], outs[2], outs[3:]


def _ffn_bwd(x, dy, gate, up, g, wg, wu, wd, name, scatter=None):
    s = x.shape[0]
    tm = _row_tile(s, 256)
    steps = s // tm
    ng = len(scatter) if scatter else 0
    (wg, wg_spec), (wu, wu_spec), (wd, wd_spec) = _weight_operand(wg), _weight_operand(wu), _weight_operand(wd)

    def body(x_ref, dy_ref, gate_ref, up_ref, g_ref, wg_ref, wu_ref, wd_ref, *rest):
        c_ins, (dx_ref, dgain_ref, n_ref, act_ref, dgate_ref, dup_ref) = rest[:ng], rest[ng:ng + 6]
        c_outs, sems = rest[ng + 6:2 * ng + 6], rest[2 * ng + 6:]
        if ng:
            _run_stages_at(_scatter_stages(c_ins, c_outs, *sems), [(0,), (steps - 1,)])
        i = pl.program_id(0)
        xv = x_ref[...]
        dyv = dy_ref[...]
        gv = g_ref[...]
        r = lax.rsqrt(jnp.mean(xv * xv, axis=-1, keepdims=True) + EPS)
        xh = xv * r
        n_ref[...] = (xh * gv).astype(BF16)
        dyh = (0.5 * dyv).astype(BF16)
        dn = jnp.zeros_like(xv)
        for c in range(0, D_FF, FF_CHUNK):
            dact = _dot_nt(dyh, _weight_rows(wd_ref, c, FF_CHUNK))
            gt = gate_ref[:, c:c + FF_CHUNK].astype(F32)
            u = up_ref[:, c:c + FF_CHUNK].astype(F32)
            sg = _sigmoid(gt)
            sl = gt * sg
            act_ref[:, c:c + FF_CHUNK] = (sl * u).astype(BF16)
            dup = (dact * sl).astype(BF16)
            dgate = (dact * u * (sg * (1.0 + gt * (1.0 - sg)))).astype(BF16)
            dup_ref[:, c:c + FF_CHUNK] = dup
            dgate_ref[:, c:c + FF_CHUNK] = dgate
            dn = dn + _dot_nt(dgate, wg_ref[:, c:c + FF_CHUNK]) + _dot_nt(dup, wu_ref[:, c:c + FF_CHUNK])
        part = jnp.sum(dn * xh, axis=0, keepdims=True)

        @pl.when(i == 0)
        def _():
            dgain_ref[...] = part

        @pl.when(i > 0)
        def _():
            dgain_ref[...] += part

        dxh = dn * gv
        dx_ref[...] = dyv + r * (dxh - xh * jnp.mean(dxh * xh, axis=-1, keepdims=True))

    row = lambda w: pl.BlockSpec((tm, w), lambda i: (i, 0))
    outs = _pallas_call(
        body, name=name, grid=(steps,),
        in_specs=[row(D_MODEL), row(D_MODEL), row(D_FF), row(D_FF), _full((1, D_MODEL)), wg_spec, wu_spec, wd_spec]
        + [_HBM] * ng,
        out_specs=[row(D_MODEL), _full((1, D_MODEL)), row(D_MODEL), row(D_FF), row(D_FF), row(D_FF)] + [_HBM] * ng,
        out_shape=[jax.ShapeDtypeStruct((s, D_MODEL), F32), jax.ShapeDtypeStruct((1, D_MODEL), F32),
                   jax.ShapeDtypeStruct((s, D_MODEL), BF16), jax.ShapeDtypeStruct((s, D_FF), BF16),
                   jax.ShapeDtypeStruct((s, D_FF), BF16), jax.ShapeDtypeStruct((s, D_FF), BF16)]
        + (_scatter_out_shapes(scatter) if ng else []),
        scratch_shapes=_exchange_scratch(3, ng) if ng else [],
        compiler_params=_params(dimension_semantics=("arbitrary",)),
    )(x, dy, gate, up, g, wg, wu, wd, *(scatter or []))
    return outs[:6], outs[6:]


def _store_col_shards(o_ref, acc, first_shard, n_here, width):
    for q in range(n_here):
        o_ref[q] = acc[:, (first_shard + q) * width:(first_shard + q + 1) * width].astype(BF16)


def _matmul_tn(a, b, scale, name, col_shards):
    t, m = a.shape
    n = b.shape[1]
    tk = _row_tile(t, 2048)
    tn = n // 2
    per = n // N_SHARD
    nk = t // tk

    def body(a_ref, b_ref, o_ref, acc_ref):
        k = pl.program_id(1)
        bv = b_ref[...]
        if scale != 1.0:
            bv = bv.astype(F32) * scale
        part = _dot_tn(a_ref[...].astype(BF16), bv.astype(BF16))

        @pl.when(k == 0)
        def _():
            acc_ref[...] = part

        @pl.when(k > 0)
        def _():
            acc_ref[...] += part

        @pl.when(k == nk - 1)
        def _():
            if col_shards:
                _store_col_shards(o_ref, acc_ref[...], 0, tn // per, per)
            else:
                o_ref[...] = acc_ref[...].astype(BF16)

    if col_shards:
        out_spec = pl.BlockSpec((tn // per, m, per), lambda j, k: (j, 0, 0))
        out_shape = jax.ShapeDtypeStruct((N_SHARD, m, per), BF16)
    else:
        out_spec = pl.BlockSpec((m, tn), lambda j, k: (0, j))
        out_shape = jax.ShapeDtypeStruct((m, n), BF16)
    return _pallas_call(
        body, name=name, grid=(n // tn, nk),
        in_specs=[pl.BlockSpec((tk, m), lambda j, k: (k, 0)), pl.BlockSpec((tk, tn), lambda j, k: (k, j))],
        out_specs=out_spec, out_shape=out_shape, scratch_shapes=[pltpu.VMEM((m, tn), F32)],
        compiler_params=_params(dimension_semantics=("arbitrary", "arbitrary")),
    )(a, b)


_HALF = SWA_HEAD_DIM // 2
_IN_ORDER = (list(range(0, IN_OFFS[2]))
             + [IN_OFFS[3] + SWA_HEAD_DIM * h + i for h in range(SWA_HEADS) for i in range(_HALF)]
             + [IN_OFFS[3] + SWA_HEAD_DIM * h + _HALF + i for h in range(SWA_HEADS) for i in range(_HALF)]
             + list(range(IN_OFFS[5], IN_OFFS[6]))
             + [IN_OFFS[4] + SWA_HEAD_DIM * j + i for j in range(SWA_KV_HEADS) for i in range(_HALF)]
             + [IN_OFFS[4] + SWA_HEAD_DIM * j + _HALF + i for j in range(SWA_KV_HEADS) for i in range(_HALF)]
             + list(range(IN_OFFS[2], IN_OFFS[3])))
_QB_ORDER = ([MLA_QK * h + i for h in range(MLA_HEADS) for i in range(MLA_NOPE)]
             + [MLA_QK * h + MLA_NOPE + i for h in range(MLA_HEADS) for i in range(_HALF)]
             + [MLA_QK * h + MLA_NOPE + _HALF + i for h in range(MLA_HEADS) for i in range(_HALF)])
_KVB_ORDER = ([(MLA_NOPE + MLA_V) * h + i for h in range(MLA_HEADS) for i in range(MLA_NOPE)]
              + [(MLA_NOPE + MLA_V) * h + MLA_NOPE + i for h in range(MLA_HEADS) for i in range(MLA_V)])
_P_CQ, _P_CKV, _P_QA, _P_QB, _P_VS, _P_KA, _P_KB, _P_PE = (int(v) for v in np.cumsum(
    (0, MLA_Q_RANK, MLA_KV_RANK, SWA_WIDTH // 2, SWA_WIDTH // 2, IN_SPLITS[5], IN_SPLITS[4] // 2, IN_SPLITS[4] // 2)))


def _runs(order):
    out, start = [], 0
    for i in range(1, len(order) + 1):
        if i == len(order) or order[i] != order[i - 1] + 1:
            out.append((order[start], i - start))
            start = i
    return out


def _inverse(order):
    inv = [0] * len(order)
    for new, old in enumerate(order):
        inv[old] = new
    return inv


def _take_cols(a, order):
    return jnp.concatenate([a[..., st:st + w] for st, w in _runs(order)], axis=-1)


def _segment_matrix(n, seg):
    return (lax.broadcasted_iota(jnp.int32, (n, n), 0) // seg == lax.broadcasted_iota(jnp.int32, (n, n), 1) // seg).astype(BF16)


@jax.custom_vjp
def _cmm(t, b, bt):
    hi = t.astype(BF16)
    lo = (t - hi.astype(F32)).astype(BF16)
    return jnp.dot(hi, b, preferred_element_type=F32) + jnp.dot(lo, b, preferred_element_type=F32)


def _cmm_fwd(t, b, bt):
    return _cmm(t, b, bt), (b, bt)


def _cmm_bwd(res, dy):
    b, bt = res
    return _cmm(dy, bt, b), None, None


_cmm.defvjp(_cmm_fwd, _cmm_bwd)


def _segsum(t, b):
    return _cmm(t, b, b)


def _rowsum(t):
    n = t.shape[-1]
    return _cmm(t, jnp.ones((n, 128), BF16), jnp.ones((128, n), BF16))


def _by_head(vals, width):
    lane = lax.broadcasted_iota(jnp.int32, (vals[0].shape[0], len(vals) * width), 1)
    out = vals[-1]
    for hd in range(len(vals) - 2, -1, -1):
        out = jnp.where(lane < (hd + 1) * width, vals[hd], out)
    return out


def _rope2(a, b, cos, sin):
    return a * cos - b * sin, b * cos + a * sin


def _pre_math(x, gm, gqa, gkva, gq, gk, gsq, gsk, taps, win, wqb, wkvb, cos, sin):
    h = _rms(x, gm)
    proj = _mm(h, win)
    if taps is not None:
        proj = proj + taps[0]
    cqn = _rms(proj[:, _P_CQ:_P_CKV], gqa)
    qa_all = _mm(cqn, wqb)
    ckvn = _rms(proj[:, _P_CKV:_P_QA], gkva)
    kv_all = _mm(ckvn, wkvb)
    if taps is not None:
        qa_all = qa_all + taps[1]
        kv_all = kv_all + taps[2]
    nh, hw = MLA_HEADS, MLA_HEADS * _HALF
    seg_mla = _segment_matrix(hw, _HALF)
    tile = lambda g, n: jnp.concatenate([g] * n, axis=-1)
    c4, s4 = cos[:, :hw], sin[:, :hw]

    def mla_heads(nope, r1, r2, gain):
        rr = r1 * r1 + r2 * r2
        lane_head = lax.broadcasted_iota(jnp.int32, (hw, nh * MLA_NOPE), 0) // _HALF
        spread = (lane_head == lax.broadcasted_iota(jnp.int32, (hw, nh * MLA_NOPE), 1) // MLA_NOPE).astype(BF16)
        rope_on_nope = _cmm(rr, spread, spread.T)
        ss_nope = [_rowsum(jnp.square(nope[:, hd * MLA_NOPE:(hd + 1) * MLA_NOPE])) for hd in range(nh)]
        rinv = [lax.rsqrt((ss_nope[hd] + rope_on_nope[:, hd * MLA_NOPE:(hd + 1) * MLA_NOPE]) * (1.0 / MLA_QK) + EPS)
                for hd in range(nh)]
        rl = lax.rsqrt((_segsum(rr, seg_mla) + _by_head(ss_nope, _HALF)) * (1.0 / MLA_QK) + EPS)
        o1, o2 = _rope2(r1 * rl * tile(gain[:, MLA_NOPE:MLA_NOPE + _HALF], nh), r2 * rl * tile(gain[:, MLA_NOPE + _HALF:], nh), c4, s4)
        return [jnp.concatenate([nope[:, hd * MLA_NOPE:(hd + 1) * MLA_NOPE] * rinv[hd] * gain[:, :MLA_NOPE],
                                 o1[:, hd * _HALF:(hd + 1) * _HALF], o2[:, hd * _HALF:(hd + 1) * _HALF]], axis=-1)
                for hd in range(nh)]

    q_a = mla_heads(qa_all[:, :nh * MLA_NOPE], qa_all[:, nh * MLA_NOPE:nh * MLA_NOPE + hw], qa_all[:, nh * MLA_NOPE + hw:], gq)
    pe1, pe2 = proj[:, _P_PE:_P_PE + _HALF], proj[:, _P_PE + _HALF:_P_PE + 2 * _HALF]
    k_a = mla_heads(kv_all[:, :nh * MLA_NOPE], tile(pe1, nh), tile(pe2, nh), gk)
    v_a = [kv_all[:, nh * MLA_NOPE + hd * MLA_V:nh * MLA_NOPE + (hd + 1) * MLA_V] for hd in range(nh)]

    def swa_heads(a, b, gain, n):
        w = n * _HALF
        r = lax.rsqrt(_segsum(a * a + b * b, _segment_matrix(w, _HALF)) * (1.0 / SWA_HEAD_DIM) + EPS)
        o1, o2 = _rope2(a * r * tile(gain[:, :_HALF], n), b * r * tile(gain[:, _HALF:], n), cos[:, :w], sin[:, :w])
        return [jnp.concatenate([o1[:, hd * _HALF:(hd + 1) * _HALF], o2[:, hd * _HALF:(hd + 1) * _HALF]], axis=-1) for hd in range(n)]

    q_b = swa_heads(proj[:, _P_QA:_P_QB], proj[:, _P_QB:_P_VS], gsq, SWA_HEADS)
    k_b = swa_heads(proj[:, _P_KA:_P_KB], proj[:, _P_KB:_P_PE], gsk, SWA_KV_HEADS)
    v_b = [proj[:, _P_VS + j * SWA_HEAD_DIM:_P_VS + (j + 1) * SWA_HEAD_DIM] for j in range(SWA_KV_HEADS)]
    return (q_a, k_a, v_a, q_b, k_b, v_b), (h, cqn, ckvn)


_PRE_GAIN_WIDTHS = (D_MODEL, MLA_Q_RANK, MLA_KV_RANK, MLA_QK, MLA_QK, SWA_HEAD_DIM, SWA_HEAD_DIM)
_PRE_HEADS = ((MLA_HEADS, MLA_QK), (MLA_HEADS, MLA_QK), (MLA_HEADS, MLA_V),
              (SWA_HEADS, SWA_HEAD_DIM), (SWA_KV_HEADS, SWA_HEAD_DIM), (SWA_KV_HEADS, SWA_HEAD_DIM))


def _pre_fwd(x, gains, win, wqb, wkvb, cos, sin, name):
    s = x.shape[0]
    tm = _row_tile(s, 512)

    def body(x_ref, *refs):
        g_refs, (win_ref, wqb_ref, wkvb_ref, cos_ref, sin_ref), out_refs = refs[:7], refs[7:12], refs[12:]
        outs, _ = _pre_math(x_ref[...], *[g[...] for g in g_refs], None, win_ref[...], wqb_ref[...], wkvb_ref[...],
                            cos_ref[...], sin_ref[...])
        for idx, (ref, heads) in enumerate(zip(out_refs, outs)):
            for hd, val in enumerate(heads):
                ref[hd] = (val * MLA_QSCALE if idx == 0 else val).astype(BF16)

    heads_spec = lambda nh, w: pl.BlockSpec((nh, tm, w), lambda i: (0, i, 0))
    return _pallas_call(
        body, name=name, grid=(s // tm,),
        in_specs=[pl.BlockSpec((tm, D_MODEL), lambda i: (i, 0))] + [_full((1, w)) for w in _PRE_GAIN_WIDTHS]
        + [_resident(win.shape), _resident(wqb.shape), _resident(wkvb.shape),
           pl.BlockSpec((tm, SWA_HEADS * _HALF), lambda i: (i, 0)), pl.BlockSpec((tm, SWA_HEADS * _HALF), lambda i: (i, 0))],
        out_specs=[heads_spec(nh, w) for nh, w in _PRE_HEADS],
        out_shape=[jax.ShapeDtypeStruct((nh, s, w), BF16) for nh, w in _PRE_HEADS],
        compiler_params=_params(dimension_semantics=("arbitrary",)),
    )(x, *gains, win, wqb, wkvb, cos, sin)


def _pre_bwd(x, dx_res, cts, gains, win, wqb, wkvb, cos, sin, name):
    s = x.shape[0]
    tm = _row_tile(s, 256)
    tap_widths = (IN_COLS, MLA_HEADS * MLA_QK, MLA_HEADS * (MLA_NOPE + MLA_V))

    def body(x_ref, dxr_ref, *refs):
        ct_refs, g_refs = refs[:6], refs[6:13]
        win_ref, wqb_ref, wkvb_ref, cos_ref, sin_ref = refs[13:18]
        dx_ref, dg_refs, dw_refs, acc_refs = refs[18], refs[19:26], refs[26:29], refs[29:32]
        i = pl.program_id(0)
        win_v, wqb_v, wkvb_v, cos_v, sin_v = win_ref[...], wqb_ref[...], wkvb_ref[...], cos_ref[...], sin_ref[...]

        def f(xv, gm, gqa, gkva, gq, gk, gsq, gsk, t0, t1, t2):
            return _pre_math(xv, gm, gqa, gkva, gq, gk, gsq, gsk, (t0, t1, t2), win_v, wqb_v, wkvb_v, cos_v, sin_v)

        taps = [jnp.zeros((tm, w), F32) for w in tap_widths]
        _, vjp, acts = jax.vjp(f, x_ref[...], *[g[...] for g in g_refs], *taps, has_aux=True)
        ct = tuple([ref[hd] for hd in range(nh)] for ref, (nh, _) in zip(ct_refs, _PRE_HEADS))
        grads = vjp(ct)
        dx_ref[...] = grads[0] + dxr_ref[...]
        dws = [_dot_tn(a.astype(BF16), t.astype(BF16)) for a, t in zip(acts, grads[8:11])]

        @pl.when(i == 0)
        def _():
            for ref, val in zip(dg_refs, grads[1:8]):
                ref[...] = val
            for ref, val in zip(acc_refs, dws):
                ref[...] = val

        @pl.when(i > 0)
        def _():
            for ref, val in zip(dg_refs, grads[1:8]):
                ref[...] += val
            for ref, val in zip(acc_refs, dws):
                ref[...] += val

        @pl.when(i == s // tm - 1)
        def _():
            for ref, acc, order in zip(dw_refs, acc_refs, (_IN_ORDER, _QB_ORDER, _KVB_ORDER)):
                _store_col_shards(ref, _take_cols(acc[...], _inverse(order)), 0, N_SHARD, acc.shape[1] // N_SHARD)

    heads_spec = lambda nh, w: pl.BlockSpec((nh, tm, w), lambda i: (0, i, 0))
    row = pl.BlockSpec((tm, D_MODEL), lambda i: (i, 0))
    half = pl.BlockSpec((tm, SWA_HEADS * _HALF), lambda i: (i, 0))
    shard_shapes = [(N_SHARD, w.shape[0], w.shape[1] // N_SHARD) for w in (win, wqb, wkvb)]
    return _pallas_call(
        body, name=name, grid=(s // tm,),
        in_specs=[row, row] + [heads_spec(nh, w) for nh, w in _PRE_HEADS] + [_full((1, w)) for w in _PRE_GAIN_WIDTHS]
        + [_resident(win.shape), _resident(wqb.shape), _resident(wkvb.shape), half, half],
        out_specs=[row] + [_full((1, w)) for w in _PRE_GAIN_WIDTHS] + [_full(shp) for shp in shard_shapes],
        out_shape=[jax.ShapeDtypeStruct((s, D_MODEL), F32)] + [jax.ShapeDtypeStruct((1, w), F32) for w in _PRE_GAIN_WIDTHS]
        + [jax.ShapeDtypeStruct(shp, BF16) for shp in shard_shapes],
        scratch_shapes=[pltpu.VMEM(w.shape, F32) for w in (win, wqb, wkvb)],
        compiler_params=_params(dimension_semantics=("arbitrary",)),
    )(x, dx_res, *cts, *gains, win, wqb, wkvb, cos, sin)


def _post_math(oa, ob, ga, gb, wo):
    mixed = jnp.concatenate([_rms(jnp.concatenate(oa, axis=-1), ga), _rms(jnp.concatenate(ob, axis=-1), gb)], axis=-1)
    return _mm(mixed, wo), mixed


def _post_fwd(x, oa, ob, ga, gb, wo, name):
    s = x.shape[0]
    tm = _row_tile(s, 512)

    def body(x_ref, oa_ref, ob_ref, ga_ref, gb_ref, wo_ref, y_ref):
        y, _ = _post_math([oa_ref[hd] for hd in range(MLA_HEADS)], [ob_ref[hd] for hd in range(SWA_HEADS)],
                          ga_ref[...], gb_ref[...], wo_ref[...])
        y_ref[...] = x_ref[...] + y

    row = pl.BlockSpec((tm, D_MODEL), lambda i: (i, 0))
    return _pallas_call(
        body, name=name, grid=(s // tm,),
        in_specs=[row, pl.BlockSpec((MLA_HEADS, tm, MLA_V), lambda i: (0, i, 0)),
                  pl.BlockSpec((SWA_HEADS, tm, SWA_HEAD_DIM), lambda i: (0, i, 0)),
                  _full((1, MLA_WIDTH)), _full((1, SWA_WIDTH)), _resident(wo.shape)],
        out_specs=row, out_shape=jax.ShapeDtypeStruct((s, D_MODEL), F32),
        compiler_params=_params(dimension_semantics=("arbitrary",)),
    )(x, oa, ob, ga, gb, wo)


def _post_bwd(dy, oa, ob, ga, gb, wo, name):
    s = dy.shape[0]
    tm = _row_tile(s, 512)

    def body(dy_ref, oa_ref, ob_ref, ga_ref, gb_ref, wo_ref, doa_ref, dob_ref, dga_ref, dgb_ref, dwo_ref, acc_ref):
        i = pl.program_id(0)
        wo_v = wo_ref[...]
        dyv = dy_ref[...]

        def f(oa_l, ob_l, ga_v, gb_v):
            return _post_math(oa_l, ob_l, ga_v, gb_v, wo_v)

        _, vjp, mixed = jax.vjp(f, [oa_ref[hd] for hd in range(MLA_HEADS)], [ob_ref[hd] for hd in range(SWA_HEADS)],
                                ga_ref[...], gb_ref[...], has_aux=True)
        doa, dob, dga, dgb = vjp(dyv)
        for hd in range(MLA_HEADS):
            doa_ref[hd] = doa[hd]
        for hd in range(SWA_HEADS):
            dob_ref[hd] = dob[hd]
        dwo = _dot_tn(mixed.astype(BF16), dyv.astype(BF16))

        @pl.when(i == 0)
        def _():
            dga_ref[...] = dga
            dgb_ref[...] = dgb
            acc_ref[...] = dwo

        @pl.when(i > 0)
        def _():
            dga_ref[...] += dga
            dgb_ref[...] += dgb
            acc_ref[...] += dwo

        @pl.when(i == s // tm - 1)
        def _():
            dwo_ref[...] = acc_ref[...].astype(BF16)

    row = pl.BlockSpec((tm, D_MODEL), lambda i: (i, 0))
    oa_spec = pl.BlockSpec((MLA_HEADS, tm, MLA_V), lambda i: (0, i, 0))
    ob_spec = pl.BlockSpec((SWA_HEADS, tm, SWA_HEAD_DIM), lambda i: (0, i, 0))
    return _pallas_call(
        body, name=name, grid=(s // tm,),
        in_specs=[row, oa_spec, ob_spec, _full((1, MLA_WIDTH)), _full((1, SWA_WIDTH)), _resident(wo.shape)],
        out_specs=[oa_spec, ob_spec, _full((1, MLA_WIDTH)), _full((1, SWA_WIDTH)), _full(wo.shape)],
        out_shape=[jax.ShapeDtypeStruct((MLA_HEADS, s, MLA_V), F32), jax.ShapeDtypeStruct((SWA_HEADS, s, SWA_HEAD_DIM), F32),
                   jax.ShapeDtypeStruct((1, MLA_WIDTH), F32), jax.ShapeDtypeStruct((1, SWA_WIDTH), F32),
                   jax.ShapeDtypeStruct(wo.shape, BF16)],
        scratch_shapes=[pltpu.VMEM(wo.shape, F32)],
        compiler_params=_params(dimension_semantics=("arbitrary",)),
    )(dy, oa, ob, ga, gb, wo)


def _attn_tile(s):
    return 512 if s >= 2048 else 128


def _causal_mask(t):
    return lax.broadcasted_iota(jnp.int32, (t, t), 1) <= lax.broadcasted_iota(jnp.int32, (t, t), 0)


def _pipelined_blocks(first, count, last_block, issue, consume, carry, prefetch_after):
    def clamped(j, slot):
        issue(jnp.minimum(j, last_block), slot)

    def pair(jj, c):
        a = first + 2 * jj
        clamped(a + 1, 1)
        c = consume(a, 0, c)
        clamped(a + 2, 0)
        return consume(a + 1, 1, c)

    clamped(first, 0)
    npairs = count // 2
    carry = lax.fori_loop(0, npairs, pair, carry)

    def odd(c):
        c = consume(first + 2 * npairs, 0, c)
        if prefetch_after:
            clamped(first + count, 0)
        return c

    return lax.cond(count - 2 * npairs == 1, odd, lambda c: c, carry)


def _run_stages_at(stages, steps):
    for stage, step in zip(stages, steps):
        here = pl.program_id(0) == step[0]
        for axis in range(1, len(step)):
            here = here & (pl.program_id(axis) == step[axis])
        pl.when(here)(stage)


def _mla_fwd(q, k, v, name, gather=None):
    nh, s, _ = q.shape
    t = _attn_tile(s)
    nq = s // t
    ng = len(gather) if gather else 0

    def body(q_ref, k_ref, v_ref, *rest):
        g_ins, (o_ref, lse_ref), g_outs = rest[:ng], rest[ng:ng + 2], rest[ng + 2:2 * ng + 2]
        (s0_ref, s1_ref), sems = rest[2 * ng + 2:2 * ng + 4], rest[2 * ng + 4:]
        if ng:
            _run_stages_at(_gather_stages(g_ins, g_outs, *sems), [(0, 0), (nh // 2, 0), (nh - 1, 0), (nh - 1, nq - 1)])
        qi = pl.program_id(1)
        qv = q_ref[...]
        s_refs = (s0_ref, s1_ref)

        def rows(j):
            return pl.ds(pl.multiple_of(j * t, t), t)

        def issue(j, slot):
            s_refs[slot][...] = _dot_nt(qv, k_ref[rows(j), :])

        def consume(j, slot, carry, masked=False):
            m, l, acc = carry
            sc = s_refs[slot][...]
            if masked:
                sc = jnp.where(_causal_mask(t), sc, NEG)
            m_new = jnp.maximum(m, jnp.max(sc, axis=-1, keepdims=True))
            alpha = jnp.exp2(m - m_new)
            p = jnp.exp2(sc - m_new)
            l = alpha * l + jnp.sum(p, axis=-1, keepdims=True)
            acc = alpha * acc + jnp.dot(p.astype(BF16), v_ref[rows(j), :], preferred_element_type=F32)
            return m_new, l, acc

        init = (jnp.full((t, 1), NEG, F32), jnp.zeros((t, 1), F32), jnp.zeros((t, MLA_V), F32))
        carry = _pipelined_blocks(0, qi, nq - 1, issue, consume, init, True)
        m, l, acc = consume(qi, 0, carry, masked=True)
        o_ref[...] = acc / l
        lse_ref[...] = m + jnp.log2(l)

    outs = _pallas_call(
        body, name=name, grid=(nh, nq),
        in_specs=[pl.BlockSpec((None, t, MLA_QK), lambda h, i: (h, i, 0)), pl.BlockSpec((None, s, MLA_QK), lambda h, i: (h, 0, 0)),
                  pl.BlockSpec((None, s, MLA_V), lambda h, i: (h, 0, 0))] + [_HBM] * ng,
        out_specs=[pl.BlockSpec((None, t, MLA_V), lambda h, i: (h, i, 0)), pl.BlockSpec((None, t, 1), lambda h, i: (h, i, 0))]
        + [_HBM] * ng,
        out_shape=[jax.ShapeDtypeStruct((nh, s, MLA_V), F32), jax.ShapeDtypeStruct((nh, s, 1), F32)]
        + (_gather_out_shapes(gather) if ng else []),
        scratch_shapes=[pltpu.VMEM((t, t), F32)] * 2 + (_exchange_scratch(8, ng) if ng else []),
        compiler_params=_params(dimension_semantics=("arbitrary", "arbitrary")),
    )(q, k, v, *(gather or []))
    return outs[0], outs[1], outs[2:]


def _mla_delta(o, do, name):
    nh, s, _ = o.shape
    t = _attn_tile(s)

    def body(o_ref, do_ref, delta_ref):
        delta_ref[...] = jnp.sum(do_ref[...] * o_ref[...], axis=-1, keepdims=True)

    tile = lambda w: pl.BlockSpec((None, t, w), lambda h, i: (h, i, 0))
    return _pallas_call(body, name=name, grid=(nh, s // t), in_specs=[tile(MLA_V), tile(MLA_V)], out_specs=tile(1),
                        out_shape=jax.ShapeDtypeStruct((nh, s, 1), F32),
                        compiler_params=_params(dimension_semantics=("arbitrary", "arbitrary")))(o, do)


def _mla_bwd(q, k, v, do, lse_row, delta_row, name, scatter=None):
    nh, s, _ = q.shape
    t = _attn_tile(s)
    nq = s // t
    ng = len(scatter) if scatter else 0

    def body(q_ref, k_ref, v_ref, do_ref, lse_ref, delta_ref, *rest):
        c_ins, (dq_ref, dk_ref, dv_ref), c_outs = rest[:ng], rest[ng:ng + 3], rest[ng + 3:2 * ng + 3]
        (s0_ref, s1_ref, dp0_ref, dp1_ref), sems = rest[2 * ng + 3:2 * ng + 7], rest[2 * ng + 7:]
        if ng:
            _run_stages_at(_scatter_stages(c_ins, c_outs, *sems), [(0, 0), (nh - 1, nq - 1)])
        kj = pl.program_id(1)
        kv_, vv = k_ref[...], v_ref[...]
        s_refs, dp_refs = (s0_ref, s1_ref), (dp0_ref, dp1_ref)

        @pl.when(kj == 0)
        def _():
            dq_ref[...] = jnp.zeros_like(dq_ref)

        def rows(i):
            return pl.ds(pl.multiple_of(i * t, t), t)

        def issue(i, slot):
            s_refs[slot][...] = _dot_nt(kv_, q_ref[rows(i), :])
            dp_refs[slot][...] = _dot_nt(vv, do_ref[rows(i), :].astype(BF16))

        def consume(i, slot, carry, masked=False):
            dk, dv = carry
            p = jnp.exp2(s_refs[slot][...] - lse_ref[pl.ds(i, 1), :])
            if masked:
                p = jnp.where(lax.broadcasted_iota(jnp.int32, (t, t), 0) <= lax.broadcasted_iota(jnp.int32, (t, t), 1), p, 0.0)
            dv = dv + jnp.dot(p.astype(BF16), do_ref[rows(i), :].astype(BF16), preferred_element_type=F32)
            ds = (p * (dp_refs[slot][...] - delta_ref[pl.ds(i, 1), :])).astype(BF16)
            dk = dk + jnp.dot(ds, q_ref[rows(i), :], preferred_element_type=F32)
            dq_ref[rows(i), :] += _dot_tn(ds, kv_) * MLA_SCALE
            return dk, dv

        issue(kj, 0)
        carry = consume(kj, 0, (jnp.zeros((t, MLA_QK), F32), jnp.zeros((t, MLA_V), F32)), masked=True)
        dk, dv = _pipelined_blocks(kj + 1, nq - 1 - kj, nq - 1, issue, consume, carry, False)
        dk_ref[...] = dk * LN2
        dv_ref[...] = dv

    tile = lambda w: pl.BlockSpec((None, t, w), lambda h, j: (h, j, 0))
    whole = lambda w: pl.BlockSpec((None, s, w), lambda h, j: (h, 0, 0))
    rows_spec = pl.BlockSpec((None, nq, t), lambda h, j: (h, 0, 0))
    outs = _pallas_call(
        body, name=name, grid=(nh, nq),
        in_specs=[whole(MLA_QK), tile(MLA_QK), tile(MLA_V), whole(MLA_V), rows_spec, rows_spec] + [_HBM] * ng,
        out_specs=[whole(MLA_QK), tile(MLA_QK), tile(MLA_V)] + [_HBM] * ng,
        out_shape=[jax.ShapeDtypeStruct((nh, s, MLA_QK), F32), jax.ShapeDtypeStruct((nh, s, MLA_QK), F32),
                   jax.ShapeDtypeStruct((nh, s, MLA_V), F32)] + (_scatter_out_shapes(scatter) if ng else []),
        scratch_shapes=[pltpu.VMEM((t, t), F32)] * 4 + (_exchange_scratch(3, ng) if ng else []),
        compiler_params=_params(dimension_semantics=("arbitrary", "arbitrary")),
    )(q, k, v, do, lse_row, delta_row, *(scatter or []))
    return outs[0], outs[1], outs[2], outs[3:]


def _swa_tile(s):
    return min(s, 4 * BLOCK)


def _swa_specs(tq):
    nb = tq // BLOCK
    grp = lambda w: pl.BlockSpec((SWA_GROUP, tq, w), lambda j, i: (j, i, 0))
    main = pl.BlockSpec((None, tq, SWA_HEAD_DIM), lambda j, i: (j, i, 0))
    tail = pl.BlockSpec((None, BLOCK, SWA_HEAD_DIM), lambda j, i: (j, nb * (i + 1), 0))
    sink = pl.BlockSpec((None, SWA_GROUP, 128), lambda j, i: (j, 0, 0))
    return grp, main, tail, sink


def _swa_band_mask(first):
    shape = (SWA_GROUP * BLOCK, 2 * BLOCK)
    q_rel = (lax.broadcasted_iota(jnp.int32, shape, 0) & (BLOCK - 1)) + BLOCK
    k_rel = lax.broadcasted_iota(jnp.int32, shape, 1)
    dist = q_rel - k_rel
    return (dist >= 0) & (dist < BLOCK) & ((k_rel >= BLOCK) | jnp.logical_not(first))


def _swa_sink_column(sink_ref):
    sk = sink_ref[...]
    return jnp.concatenate([jnp.broadcast_to(sk[g:g + 1, 0:1], (BLOCK, 1)) for g in range(SWA_GROUP)], axis=0)


def _swa_fwd(q, kpad, vpad, sinks, name):
    _, s, _ = q.shape
    tq = _swa_tile(s)
    grp, main, tail, sink = _swa_specs(tq)
    d = SWA_HEAD_DIM

    def body(q_ref, km_ref, kt_ref, vm_ref, vt_ref, sink_ref, o_ref, lse_ref):
        i = pl.program_id(1)
        kall = jnp.concatenate([km_ref[...], kt_ref[...]], axis=0)
        vall = jnp.concatenate([vm_ref[...], vt_ref[...]], axis=0)
        sink_col = _swa_sink_column(sink_ref)
        for b in range(tq // BLOCK):
            lo = b * BLOCK
            valid = _swa_band_mask(i == 0 if b == 0 else False)
            q4 = q_ref[:, lo:lo + BLOCK, :].reshape(SWA_GROUP * BLOCK, d)
            sc = jnp.where(valid, _dot_nt(q4, kall[lo:lo + 2 * BLOCK]) * SWA_SCALE, NEG)
            m = jnp.maximum(jnp.max(sc, axis=-1, keepdims=True), sink_col)
            e = jnp.exp(sc - m)
            den = jnp.sum(e, axis=-1, keepdims=True) + jnp.exp(sink_col - m)
            out = jnp.dot((e * (1.0 / den)).astype(BF16), vall[lo:lo + 2 * BLOCK], preferred_element_type=F32)
            o_ref[:, lo:lo + BLOCK, :] = out.reshape(SWA_GROUP, BLOCK, d)
            lse_ref[:, lo:lo + BLOCK, :] = (m + jnp.log(den)).reshape(SWA_GROUP, BLOCK, 1)

    return _pallas_call(
        body, name=name, grid=(SWA_KV_HEADS, s // tq),
        in_specs=[grp(d), main, tail, main, tail, sink], out_specs=[grp(d), grp(1)],
        out_shape=[jax.ShapeDtypeStruct((SWA_HEADS, s, d), F32), jax.ShapeDtypeStruct((SWA_HEADS, s, 1), F32)],
        compiler_params=_params(dimension_semantics=("arbitrary", "arbitrary")),
    )(q, kpad, kpad, vpad, vpad, sinks)


def _swa_bwd(q, kpad, vpad, sinks, o, lse, do, name):
    _, s, _ = q.shape
    tq = _swa_tile(s)
    grp, main, tail, sink = _swa_specs(tq)
    d = SWA_HEAD_DIM

    def body(q_ref, km_ref, kt_ref, vm_ref, vt_ref, sink_ref, o_ref, lse_ref, do_ref, dq_ref, dk_ref, dv_ref, dsink_ref):
        i = pl.program_id(1)
        kall = jnp.concatenate([km_ref[...], kt_ref[...]], axis=0)
        vall = jnp.concatenate([vm_ref[...], vt_ref[...]], axis=0)
        sink_col = _swa_sink_column(sink_ref)

        @pl.when(i == 0)
        def _():
            dk_ref[...] = jnp.zeros_like(dk_ref)
            dv_ref[...] = jnp.zeros_like(dv_ref)
            dsink_ref[...] = jnp.zeros_like(dsink_ref)

        dsink = jnp.zeros((SWA_GROUP * BLOCK, 1), F32)
        for b in range(tq // BLOCK):
            lo = b * BLOCK
            valid = _swa_band_mask(i == 0 if b == 0 else False)
            rows4 = SWA_GROUP * BLOCK
            q4 = q_ref[:, lo:lo + BLOCK, :].reshape(rows4, d)
            do4 = do_ref[:, lo:lo + BLOCK, :].reshape(rows4, d)
            lse4 = lse_ref[:, lo:lo + BLOCK, :].reshape(rows4, 1)
            delta = jnp.sum(do4 * o_ref[:, lo:lo + BLOCK, :].reshape(rows4, d), axis=-1, keepdims=True)
            kb, vb = kall[lo:lo + 2 * BLOCK], vall[lo:lo + 2 * BLOCK]
            do4b = do4.astype(BF16)
            p = jnp.where(valid, jnp.exp(_dot_nt(q4, kb) * SWA_SCALE - lse4), 0.0)
            ds = (p * (_dot_nt(do4b, vb) - delta) * SWA_SCALE).astype(BF16)
            dq_ref[:, lo:lo + BLOCK, :] = jnp.dot(ds, kb, preferred_element_type=F32).reshape(SWA_GROUP, BLOCK, d)
            band = pl.ds(pl.multiple_of(i * tq, BLOCK) + lo, 2 * BLOCK)
            dk_ref[band, :] += _dot_tn(ds, q4)
            dv_ref[band, :] += _dot_tn(p.astype(BF16), do4b)
            dsink = dsink - jnp.exp(sink_col - lse4) * delta
        per_head = [jnp.broadcast_to(jnp.sum(dsink[g * BLOCK:(g + 1) * BLOCK], axis=0, keepdims=True), (1, 128))
                    for g in range(SWA_GROUP)]
        dsink_ref[...] += jnp.concatenate(per_head + [jnp.zeros((8 - SWA_GROUP, 128), F32)], axis=0)

    acc = pl.BlockSpec((None, s + BLOCK, d), lambda j, i: (j, 0, 0))
    return _pallas_call(
        body, name=name, grid=(SWA_KV_HEADS, s // tq),
        in_specs=[grp(d), main, tail, main, tail, sink, grp(d), grp(1), grp(d)],
        out_specs=[grp(d), acc, acc, pl.BlockSpec((None, 8, 128), lambda j, i: (j, 0, 0))],
        out_shape=[jax.ShapeDtypeStruct((SWA_HEADS, s, d), F32),
                   jax.ShapeDtypeStruct((SWA_KV_HEADS, s + BLOCK, d), F32),
                   jax.ShapeDtypeStruct((SWA_KV_HEADS, s + BLOCK, d), F32),
                   jax.ShapeDtypeStruct((SWA_KV_HEADS, 8, 128), F32)],
        compiler_params=_params(dimension_semantics=("arbitrary", "arbitrary")),
    )(q, kpad, kpad, vpad, vpad, sinks, o, lse, do)


def _loss_head(y, target, name):
    s = y.shape[0]
    tm = _row_tile(s, 512)

    def body(y_ref, t_ref, dy_ref, loss_ref):
        i = pl.program_id(0)
        err = y_ref[...] - t_ref[...]
        dy_ref[...] = err * (1.0 / D_MODEL)
        part = jnp.broadcast_to(0.5 * jnp.sum(jnp.mean(err * err, axis=-1, keepdims=True), axis=0, keepdims=True), (1, 128))

        @pl.when(i == 0)
        def _():
            loss_ref[...] = part

        @pl.when(i > 0)
        def _():
            loss_ref[...] += part

    row = pl.BlockSpec((tm, D_MODEL), lambda i: (i, 0))
    return _pallas_call(
        body, name=name, grid=(s // tm,), in_specs=[row, row], out_specs=[row, _full((1, 128))],
        out_shape=[jax.ShapeDtypeStruct((s, D_MODEL), F32), jax.ShapeDtypeStruct((1, 128), F32)],
        compiler_params=_params(dimension_semantics=("arbitrary",)),
    )(y, target)


def _adamw(w, g, m, v, name):
    rows, cols = w.shape
    tr = rows
    for cand in (512, 256, 128, 64, 32, 16, 8):
        if rows % cand == 0 and rows > cand:
            tr = cand
            break

    def body(w_ref, g_ref, m_ref, v_ref, d_ref, nm_ref, nv_ref):
        gv = g_ref[...]
        nm = ADAM_B1 * m_ref[...] + (1.0 - ADAM_B1) * gv
        nv = ADAM_B2 * v_ref[...] + (1.0 - ADAM_B2) * (gv * gv)
        m_hat = nm / (1.0 - ADAM_B1 ** ADAM_STEP)
        v_hat = nv / (1.0 - ADAM_B2 ** ADAM_STEP)
        d_ref[...] = -ADAM_LR * (m_hat / (jnp.sqrt(v_hat) + ADAM_EPS) + ADAM_WD * w_ref[...])
        nm_ref[...] = nm
        nv_ref[...] = nv

    blk = pl.BlockSpec((tr, cols), lambda i: (i, 0))
    return _pallas_call(
        body, name=name, grid=(rows // tr,), in_specs=[blk] * 4, out_specs=[blk] * 3,
        out_shape=[jax.ShapeDtypeStruct((rows, cols), F32)] * 3,
        compiler_params=_params(dimension_semantics=("arbitrary",)),
    )(w, g, m, v)


def _position():
    return lax.axis_index("x"), lax.axis_index("y"), lax.axis_index("c")


def _remote(src, dst, send_sems, recv_sems, k, to):
    return pltpu.make_async_remote_copy(src_ref=src, dst_ref=dst, send_sem=send_sems.at[k], recv_sem=recv_sems.at[k],
                                        device_id=to, device_id_type=MESH)


_HBM = pl.BlockSpec(memory_space=pltpu.HBM)


def _gather_stages(ins, outs, send_sems, recv_sems):
    na = len(ins)
    x, y, c = _position()
    me, sibling = (x, y, c), (x, y, 1 - c)
    xn, yn, dg = (1 - x, y), (x, 1 - y), (1 - x, 1 - y)

    def slot(a, chip, pc, half=None):
        ref = outs[a].at[4 * chip[0] + 2 * chip[1] + pc]
        if half is None:
            return ref
        rows = ref.shape[0] // 2
        return ref.at[pl.ds(half * rows, rows)]

    def cp(a, k, chip, pc, half, to, src=None):
        dst = slot(a, chip, pc, half)
        return _remote(dst if src is None else src, dst, send_sems, recv_sems, 8 * a + k, to)

    first_hop = [(0, xn), (1, yn)]
    second_hop = [(0, xn, 2, 0, yn), (1, yn, 3, 1, xn)]

    def sends():
        out = []
        for a in range(na):
            out += [cp(a, k, (x, y), c, None, (*to, c), src=ins[a].at[c]) for k, to in first_hop]
            out += [cp(a, fwd_k, frm, c, half, (*to, c)) for _, frm, fwd_k, half, to in second_hop]
            out += [cp(a, 4 + k, frm, c, None, sibling) for k, frm in first_hop]
            out += [cp(a, 6 + half, dg, c, half, sibling) for half in (0, 1)]
        return out

    def stage0():
        for a in range(na):
            for k, to in first_hop:
                cp(a, k, (x, y), c, None, (*to, c), src=ins[a].at[c]).start()

    def stage1():
        for k, frm, fwd_k, half, to in second_hop:
            for a in range(na):
                cp(a, k, frm, c, None, me).wait_recv()
                cp(a, fwd_k, frm, c, half, (*to, c)).start()
                cp(a, 4 + k, frm, c, None, sibling).start()

    def stage2():
        for half in (0, 1):
            for a in range(na):
                cp(a, 2 + half, dg, c, half, me).wait_recv()
                cp(a, 6 + half, dg, c, half, sibling).start()

    def stage3():
        for a in range(na):
            for k, chip, half in ((4, xn, None), (5, yn, None), (6, dg, 0), (7, dg, 1)):
                cp(a, k, chip, 1 - c, half, me).wait_recv()
        for sent in sends():
            sent.wait_send()

    return [stage0, stage1, stage2, stage3]


def _gather_out_shapes(mine):
    return [jax.ShapeDtypeStruct((N_DEV,) + m.shape[1:], m.dtype) for m in mine]


def _exchange_scratch(per_array, na):
    return [pltpu.SemaphoreType.DMA((per_array * na,)), pltpu.SemaphoreType.DMA((per_array * na,))]


def _all_gather_halves(mine, name):
    na = len(mine)

    def body(*refs):
        for stage in _gather_stages(refs[:na], refs[na:2 * na], *refs[2 * na:]):
            stage()

    return _pallas_call(body, name=name, in_specs=[_HBM] * na, out_specs=[_HBM] * na, out_shape=_gather_out_shapes(mine),
                        scratch_shapes=_exchange_scratch(8, na))(*mine)


def _sibling_exchange(parts, name, other_half):
    na = len(parts)

    def body(*refs):
        ins, outs, (send_sems, recv_sems) = refs[:na], refs[na:2 * na], refs[2 * na:]
        x, y, c = _position()
        copies = [_remote(ins[a].at[:, 1 - c] if other_half else ins[a], outs[a], send_sems, recv_sems, a, (x, y, 1 - c))
                  for a in range(na)]
        for cp in copies:
            cp.start()
        for cp in copies:
            cp.wait()

    return _pallas_call(
        body, name=name, in_specs=[_HBM] * na, out_specs=[_HBM] * na,
        out_shape=[jax.ShapeDtypeStruct(p.shape[:1] + p.shape[2:] if other_half else p.shape, p.dtype) for p in parts],
        scratch_shapes=_exchange_scratch(1, na),
    )(*parts)


def _scatter_stages(ins, outs, send_sems, recv_sems):
    na = len(ins)
    x, y, c = _position()
    chips = [(1 - x, y), (x, 1 - y), (1 - x, 1 - y)]

    def copies():
        return [_remote(ins[a].at[2 * px + py], outs[a].at[j], send_sems, recv_sems, 3 * a + j, (px, py, c))
                for a in range(na) for j, (px, py) in enumerate(chips)]

    def start():
        for cp in copies():
            cp.start()

    def wait():
        for cp in copies():
            cp.wait()

    return [start, wait]


def _scatter_out_shapes(parts):
    return [jax.ShapeDtypeStruct((3,) + p.shape[1:], p.dtype) for p in parts]


def _scatter_to_chips(parts, name):
    na = len(parts)

    def body(*refs):
        for stage in _scatter_stages(refs[:na], refs[na:2 * na], *refs[2 * na:]):
            stage()

    return _pallas_call(body, name=name, in_specs=[_HBM] * na, out_specs=[_HBM] * na, out_shape=_scatter_out_shapes(parts),
                        scratch_shapes=_exchange_scratch(3, na))(*parts)


def _assemble(g4, mine, name, side_by_side):
    _, nl, r, w = g4.shape
    tr = next(cand for cand in (256, 128, 64) if r % cand == 0)

    def body(in_ref, mine_ref, out_ref):
        chip = 2 * lax.axis_index("x") + lax.axis_index("y")
        blocks = [jnp.where(chip == sh, mine_ref[...], in_ref[sh]) for sh in range(N_SHARD)]
        if side_by_side:
            out_ref[...] = jnp.concatenate(blocks, axis=-1)
        else:
            for sh in range(N_SHARD):
                out_ref[sh] = blocks[sh]

    if side_by_side:
        out_spec = pl.BlockSpec((None, tr, N_SHARD * w), lambda l, i: (l, i, 0))
        out_shape = jax.ShapeDtypeStruct((nl, r, N_SHARD * w), g4.dtype)
    else:
        out_spec = pl.BlockSpec((None, N_SHARD, tr, w), lambda l, i: (l, 0, i, 0))
        out_shape = jax.ShapeDtypeStruct((nl, N_SHARD, r, w), g4.dtype)
    return _pallas_call(
        body, name=name, grid=(nl, r // tr),
        in_specs=[pl.BlockSpec((N_SHARD, None, tr, w), lambda l, i: (0, l, i, 0)),
                  pl.BlockSpec((None, tr, w), lambda l, i: (l, i, 0))],
        out_specs=out_spec, out_shape=out_shape,
        compiler_params=_params(dimension_semantics=("arbitrary", "arbitrary")),
    )(g4, mine)


def _all_reduce_small(vec, name):
    r, l = vec.shape

    def body(v_ref, out_ref, gath_ref, send_sems, recv_sems):
        x, y, c = _position()
        me = 4 * x + 2 * y + c
        gath_ref[me] = v_ref[...]
        copies = []
        for k in range(1, N_DEV):
            to = (x ^ (k >> 2), y ^ ((k >> 1) & 1), c ^ (k & 1))
            copies.append(_remote(gath_ref.at[me], gath_ref.at[me], send_sems, recv_sems, k - 1, to))
        for cp in copies:
            cp.start()
        for k in range(1, N_DEV):
            frm = 4 * (x ^ (k >> 2)) + 2 * (y ^ ((k >> 1) & 1)) + (c ^ (k & 1))
            _remote(gath_ref.at[frm], gath_ref.at[frm], send_sems, recv_sems, k - 1, (x, y, c)).wait_recv()
        for cp in copies:
            cp.wait_send()
        total = gath_ref[0]
        for d in range(1, N_DEV):
            total = total + gath_ref[d]
        out_ref[...] = total

    vm = pl.BlockSpec(memory_space=pltpu.VMEM)
    return _pallas_call(
        body, name=name, in_specs=[vm], out_specs=vm, out_shape=jax.ShapeDtypeStruct((r, l), F32),
        scratch_shapes=[pltpu.VMEM((N_DEV, r, l), F32), pltpu.SemaphoreType.DMA((N_DEV - 1,)),
                        pltpu.SemaphoreType.DMA((N_DEV - 1,))],
    )(vec)


def _sum_blocks(blocks, out_dtype, name):
    m, w = blocks[0].shape
    tr = next(cand for cand in (512, 256, 128, 64, 32) if m % cand == 0)

    def body(*refs):
        total = refs[0][...].astype(F32)
        for ref in refs[1:-1]:
            total = total + ref[...].astype(F32)
        refs[-1][...] = total.astype(out_dtype)

    blk = pl.BlockSpec((tr, w), lambda i: (i, 0))
    return _pallas_call(
        body, name=name, grid=(m // tr,), in_specs=[blk] * len(blocks), out_specs=blk,
        out_shape=jax.ShapeDtypeStruct((m, w), out_dtype),
        compiler_params=_params(dimension_semantics=("arbitrary",)),
    )(*blocks)


FIRST_GROUPS = (('ffn1_w_gate', 'ffn1_w_up'), ('ffn1_w_down',))
REST_GROUPS = (('ffn2_w_gate', 'ffn2_w_up'), ('ffn2_w_down', 'w_o'), ('w_in',), ('mla_w_q_b',), ('mla_w_kv_b',))
N_FIRST = len(FIRST_GROUPS)


def _shard_rows(name):
    shape, axis = BIG[name]
    return shape[0] // N_SHARD if axis == 0 else shape[0]


def _group_row_offsets(group):
    return [int(v) for v in np.cumsum([0] + [_shard_rows(n) for n in group])]


def _rope_tables(s):
    pos = jnp.arange(s, dtype=F32)
    inv = 1.0 / (ROPE_THETA ** (jnp.arange(0, MLA_ROPE, 2, dtype=F32) / MLA_ROPE))
    ang = pos[:, None] * inv[None, :]
    return jnp.tile(jnp.cos(ang), (1, SWA_HEADS)), jnp.tile(jnp.sin(ang), (1, SWA_HEADS))


_MIXER_GAINS = ('mix_norm', 'mla_q_a_norm', 'mla_kv_a_norm', 'mla_q_norm', 'mla_k_norm', 'swa_q_norm', 'swa_k_norm')


def _local_step(x, target, small, ex):
    s = x.shape[0]
    cos, sin = _rope_tables(s)
    row = lambda name, l: small[name][l][None, :]
    saved, bigs = [], []
    for l in range(DEPTH):
        big = ex.first_weights(l)
        sv = {'x0': x}
        x, sv['g1'], sv['u1'], got = _ffn_fwd(x, row('ffn1_norm', l), big['ffn1_w_gate'], big['ffn1_w_up'], big['ffn1_w_down'],
                                             f"ffn1_fwd_{l}", ex.gather_behind_ffn1(l))
        big.update(ex.rest_weights(l, got))
        bigs.append(big)
        sv['x1'] = x
        gains = [row(n, l) for n in _MIXER_GAINS]
        mixer_w = (_take_cols(big['w_in'], _IN_ORDER), _take_cols(big['mla_w_q_b'], _QB_ORDER),
                   _take_cols(big['mla_w_kv_b'], _KVB_ORDER))
        q_a, k_a, v_a, q_b, k_b, v_b = _pre_fwd(x, gains, *mixer_w, cos, sin, f"pre_fwd_{l}")
        o_a, lse, got = _mla_fwd(q_a, k_a, v_a, f"mla_fwd_{l}", ex.gather_behind_mla(l))
        ex.gathered_behind_mla(l, got)
        kpad = jnp.pad(k_b, ((0, 0), (BLOCK, 0), (0, 0)))
        vpad = jnp.pad(v_b, ((0, 0), (BLOCK, 0), (0, 0)))
        sinks = jnp.broadcast_to(small['swa_sinks'][l].reshape(SWA_KV_HEADS, SWA_GROUP, 1), (SWA_KV_HEADS, SWA_GROUP, 128))
        o_b, lse_b = _swa_fwd(q_b, kpad, vpad, sinks, f"swa_fwd_{l}")
        sv.update(gains=gains, mixer_w=mixer_w, q_a=q_a, k_a=k_a, v_a=v_a, q_b=q_b, kpad=kpad, vpad=vpad, sinks=sinks,
                  o_a=o_a, lse=lse, o_b=o_b, lse_b=lse_b)
        x = _post_fwd(x, o_a, o_b, row('mla_out_norm', l), row('swa_out_norm', l), big['w_o'], f"post_fwd_{l}")
        sv['x2'] = x
        x, sv['g2'], sv['u2'], _ = _ffn_fwd(x, row('ffn2_norm', l), big['ffn2_w_gate'], big['ffn2_w_up'], big['ffn2_w_down'],
                                           f"ffn2_fwd_{l}")
        saved.append(sv)

    dx, loss = _loss_head(x, target, "loss_head")

    gs = {n: [None] * DEPTH for n in SMALL_NAMES}
    t = _attn_tile(s)
    for l in reversed(range(DEPTH)):
        sv, big = saved[l], bigs[l]

        def ffn_back(tag, xin, dy, gate, up, scatter=None):
            (dxi, dgain, nb, act, dgate, dup), got = _ffn_bwd(xin, dy, gate, up, row(tag + '_norm', l), big[tag + '_w_gate'],
                                                             big[tag + '_w_up'], big[tag + '_w_down'], f"{tag}_bwd_{l}", scatter)
            gs[tag + '_norm'][l] = dgain[0]
            grads = {tag + '_w_gate': _matmul_tn(nb, dgate, 1.0, f"{tag}_dwg_{l}", True),
                     tag + '_w_up': _matmul_tn(nb, dup, 1.0, f"{tag}_dwu_{l}", True),
                     tag + '_w_down': _matmul_tn(act, dy, 0.5, f"{tag}_dwd_{l}", False).reshape(N_SHARD, D_FF // N_SHARD, D_MODEL)}
            return dxi, grads, got

        dx, rest_grads, _ = ffn_back('ffn2', sv['x2'], dx, sv['g2'], sv['u2'])
        do_a, do_b, dga, dgb, dwo = _post_bwd(dx, sv['o_a'], sv['o_b'], row('mla_out_norm', l), row('swa_out_norm', l),
                                              big['w_o'], f"post_bwd_{l}")
        gs['mla_out_norm'][l], gs['swa_out_norm'][l] = dga[0], dgb[0]
        rest_grads['w_o'] = dwo.reshape(N_SHARD, MIX_WIDTH // N_SHARD, D_MODEL)
        delta = _mla_delta(sv['o_a'], do_a, f"mla_delta_{l}")
        dq_a, dk_a, dv_a, got = _mla_bwd(sv['q_a'], sv['k_a'], sv['v_a'], do_a, sv['lse'].reshape(MLA_HEADS, s // t, t),
                                         delta.reshape(MLA_HEADS, s // t, t), f"mla_bwd_{l}", ex.scatter_behind_mla(l))
        ex.scattered_behind_mla(l, got)
        dq_b, dkpad, dvpad, dsink = _swa_bwd(sv['q_b'], sv['kpad'], sv['vpad'], sv['sinks'], sv['o_b'], sv['lse_b'], do_b,
                                             f"swa_bwd_{l}")
        gs['swa_sinks'][l] = dsink[:, :SWA_GROUP, 0].reshape(SWA_HEADS)
        cts = [dq_a, dk_a, dv_a, dq_b, dkpad[:, BLOCK:], dvpad[:, BLOCK:]]
        outs = _pre_bwd(sv['x1'], dx, cts, sv['gains'], *sv['mixer_w'], cos, sin, f"pre_bwd_{l}")
        dx = outs[0]
        for n, val in zip(_MIXER_GAINS, outs[1:8]):
            gs[n][l] = val[0]
        rest_grads['w_in'], rest_grads['mla_w_q_b'], rest_grads['mla_w_kv_b'] = outs[8:11]
        ex.grads_ready(l, 'rest', rest_grads)
        dx, first_grads, got = ffn_back('ffn1', sv['x0'], dx, sv['g1'], sv['u1'], ex.scatter_behind_ffn1(l))
        ex.scattered_behind_ffn1(l, got)
        ex.grads_ready(l, 'first', first_grads)
    return loss, dx, gs


class _Exchange:
    def __init__(self, weights, c, chip):
        halves_of = lambda a: a.reshape(a.shape[:-2] + (2, a.shape[-2] // 2, a.shape[-1]))
        self.halves_of, self.c, self.chip = halves_of, c, chip
        self.mine = [[halves_of(jnp.concatenate([weights[n][l].astype(BF16) for n in group], axis=0))
                      for group in FIRST_GROUPS + REST_GROUPS] for l in range(DEPTH)]
        self.ahead, self.begun, self.received = {}, {}, {}

    def _assembled(self, l, which, gathered):
        groups, base = (FIRST_GROUPS, 0) if which == 'first' else (REST_GROUPS, N_FIRST)
        big = {}
        for gi, group in enumerate(groups):
            offs = _group_row_offsets(group)
            _, rh, w = self.mine[l][base + gi].shape
            col_sharded = BIG[group[0]][1] == 1
            full = _assemble(gathered[gi].reshape(N_SHARD, 1, 2 * rh, w), self.mine[l][base + gi].reshape(1, 2 * rh, w),
                             f"assemble_{which}{gi}_{l}", col_sharded)
            for i, n in enumerate(group):
                rows = offs[i + 1] - offs[i]
                if len(group) > 1 and offs[i] % rows == 0:
                    big[n] = ((full, (None, rows, N_SHARD * w), (0, offs[i] // rows, 0)) if col_sharded else
                              (full, (None, N_SHARD, rows, w), (0, 0, offs[i] // rows, 0)))
                elif col_sharded:
                    big[n] = full[0, offs[i]:offs[i + 1]]
                else:
                    big[n] = full[0, :, offs[i]:offs[i + 1]].reshape(BIG[n][0])
        return big

    def first_weights(self, l):
        got = self.ahead[l][:N_FIRST] if l in self.ahead else _all_gather_halves(self.mine[l][:N_FIRST], f"gather_first_{l}")
        return self._assembled(l, 'first', got)

    def gather_behind_ffn1(self, l):
        return None if l in self.ahead else self.mine[l][N_FIRST:]

    def rest_weights(self, l, got):
        return self._assembled(l, 'rest', self.ahead[l][N_FIRST:] if l in self.ahead else got)

    def gather_behind_mla(self, l):
        return self.mine[l + 1] if l + 1 < DEPTH else None

    def gathered_behind_mla(self, l, got):
        if got:
            self.ahead[l + 1] = got

    def grads_ready(self, l, which, grads):
        groups = FIRST_GROUPS if which == 'first' else REST_GROUPS
        parts = [self.halves_of(jnp.concatenate([grads[n] for n in group], axis=1)) for group in groups]
        from_sibling = _sibling_exchange(parts, f"swap_{which}_{l}", True)
        chip_sums = []
        for gi, (p, got) in enumerate(zip(parts, from_sibling)):
            kept = lax.dynamic_index_in_dim(p, self.c, axis=1, keepdims=False)
            rows = N_SHARD * p.shape[2]
            pair = _sum_blocks([kept.reshape(rows, -1), got.reshape(rows, -1)], BF16, f"sum_pair_{which}{gi}_{l}")
            chip_sums.append(pair.reshape(got.shape))
        self.begun[(l, which)] = chip_sums

    def scatter_behind_mla(self, l):
        return self.begun[(l + 1, 'first')] + self.begun[(l + 1, 'rest')] if l + 1 < DEPTH else None

    def scattered_behind_mla(self, l, got):
        if got:
            self.received[(l + 1, 'first')], self.received[(l + 1, 'rest')] = got[:N_FIRST], got[N_FIRST:]

    def scatter_behind_ffn1(self, l):
        return self.begun[(l, 'rest')] if l == 0 else None

    def scattered_behind_ffn1(self, l, got):
        if got:
            self.received[(l, 'rest')] = got

    def reduced(self):
        keys = sorted(self.begun)
        for key in keys:
            if key not in self.received:
                self.received[key] = _scatter_to_chips(self.begun[key], f"scatter_{key[1]}_{key[0]}")
        halves = []
        for l, which in keys:
            for gi, (cs, got) in enumerate(zip(self.begun[(l, which)], self.received[(l, which)])):
                own = lax.dynamic_index_in_dim(cs, self.chip, axis=0, keepdims=False)
                halves.append(_sum_blocks([own, got[0], got[1], got[2]], F32, f"sum_chips_{which}{gi}_{l}"))
        others = _sibling_exchange(halves, "share_halves", False)
        per_layer, at = {}, 0
        for l, which in keys:
            for group in (FIRST_GROUPS if which == 'first' else REST_GROUPS):
                mine_h, other_h = halves[at], others[at]
                at += 1
                full = jnp.where(self.c == 0, jnp.concatenate([mine_h, other_h]), jnp.concatenate([other_h, mine_h]))
                offs = _group_row_offsets(group)
                for i, n in enumerate(group):
                    per_layer[(n, l)] = full[offs[i]:offs[i + 1]]
        return {n: jnp.stack([per_layer[(n, l)] for l in range(DEPTH)]) for n in BIG_NAMES}


def kernel(x, ffn1_norm, ffn1_w_gate, ffn1_w_up, ffn1_w_down, mix_norm, w_in, mla_q_a_norm, mla_w_q_b, mla_kv_a_norm, mla_w_kv_b, mla_q_norm, mla_k_norm, swa_q_norm, swa_k_norm, swa_sinks, mla_out_norm, swa_out_norm, w_o, ffn2_norm, ffn2_w_gate, ffn2_w_up, ffn2_w_down, loss_target, m_ffn1_norm, m_ffn1_w_gate, m_ffn1_w_up, m_ffn1_w_down, m_mix_norm, m_w_in, m_mla_q_a_norm, m_mla_w_q_b, m_mla_kv_a_norm, m_mla_w_kv_b, m_mla_q_norm, m_mla_k_norm, m_swa_q_norm, m_swa_k_norm, m_swa_sinks, m_mla_out_norm, m_swa_out_norm, m_w_o, m_ffn2_norm, m_ffn2_w_gate, m_ffn2_w_up, m_ffn2_w_down, v_ffn1_norm, v_ffn1_w_gate, v_ffn1_w_up, v_ffn1_w_down, v_mix_norm, v_w_in, v_mla_q_a_norm, v_mla_w_q_b, v_mla_kv_a_norm, v_mla_w_kv_b, v_mla_q_norm, v_mla_k_norm, v_swa_q_norm, v_swa_k_norm, v_swa_sinks, v_mla_out_norm, v_swa_out_norm, v_w_o, v_ffn2_norm, v_ffn2_w_gate, v_ffn2_w_up, v_ffn2_w_down):
    args = dict(locals())
    weights = {n: args[n] for n in WEIGHT_NAMES}
    mom_m = {n: args["m_" + n] for n in WEIGHT_NAMES}
    mom_v = {n: args["v_" + n] for n in WEIGHT_NAMES}
    ex = _Exchange(weights, lax.axis_index("c"), 2 * lax.axis_index("x") + lax.axis_index("y"))
    loss, dx, gs = _local_step(x[0], loss_target[0], {n: weights[n] for n in SMALL_NAMES}, ex)

    small_flat = jnp.concatenate([jnp.stack(gs[n]).reshape(-1) for n in SMALL_NAMES] + [loss[0, :1]])
    n_small = small_flat.shape[0]
    lanes = -(-n_small // (8 * 128)) * 128
    small_sum = _all_reduce_small(jnp.pad(small_flat, (0, 8 * lanes - n_small)).reshape(8, lanes), "reduce_small").reshape(-1)
    grads = ex.reduced()
    off = 0
    for n in SMALL_NAMES:
        cnt = int(np.prod(weights[n].shape))
        grads[n] = small_sum[off:off + cnt].reshape(weights[n].shape)
        off += cnt
    loss_out = small_sum[off]

    deltas, new_m, new_v = {}, {}, {}
    for n in WEIGHT_NAMES:
        shp = weights[n].shape
        two_d = (DEPTH, shp[-1]) if len(shp) == 2 else (shp[0] * shp[1], shp[2])
        d, nm, nv = _adamw(weights[n].reshape(two_d), grads[n].reshape(two_d), mom_m[n].reshape(two_d),
                           mom_v[n].reshape(two_d), f"adamw_{n}")
        deltas[n], new_m[n], new_v[n] = d.reshape(shp), nm.reshape(shp), nv.reshape(shp)

    return (loss_out, dx[None], *[grads[n] for n in WEIGHT_NAMES], *[deltas[n] for n in WEIGHT_NAMES],
            *[new_m[n] for n in WEIGHT_NAMES], *[new_v[n] for n in WEIGHT_NAMES])
```

```python
import functools

import numpy as np
import jax
import jax.numpy as jnp
from jax import lax
from jax.experimental import pallas as pl
from jax.experimental.pallas import tpu as pltpu

F32 = jnp.float32
BF16 = jnp.bfloat16

D_MODEL = 1024
DEPTH = 2
EPS = 1e-6
ROPE_THETA = 10000.0
BLOCK = 128
MLA_HEADS = 4
MLA_Q_RANK = 256
MLA_KV_RANK = 128
MLA_NOPE = 128
MLA_ROPE = 64
MLA_V = 128
MLA_QK = MLA_NOPE + MLA_ROPE
MLA_WIDTH = MLA_HEADS * MLA_V
SWA_HEADS = 8
SWA_KV_HEADS = 2
SWA_GROUP = SWA_HEADS // SWA_KV_HEADS
SWA_HEAD_DIM = 64
SWA_WIDTH = SWA_HEADS * SWA_HEAD_DIM
MIX_WIDTH = MLA_WIDTH + SWA_WIDTH
IN_SPLITS = (MLA_Q_RANK, MLA_KV_RANK, MLA_ROPE, SWA_WIDTH, SWA_KV_HEADS * SWA_HEAD_DIM, SWA_KV_HEADS * SWA_HEAD_DIM)
IN_COLS = sum(IN_SPLITS)
IN_OFFS = tuple(int(v) for v in np.cumsum((0,) + IN_SPLITS))
D_FF = 2816
MLA_SCALE = MLA_QK ** -0.5
LOG2E = 1.4426950408889634
LN2 = 0.6931471805599453
MLA_QSCALE = MLA_SCALE * LOG2E
SWA_SCALE = SWA_HEAD_DIM ** -0.5
NEG = -1e30

ADAM_LR = 0.001
ADAM_B1 = 0.9
ADAM_B2 = 0.999
ADAM_EPS = 1e-08
ADAM_WD = 0.01
ADAM_STEP = 10

N_SHARD = 4
N_DEV = 8
VMEM_LIMIT = 56 * 1024 * 1024
MESH = pl.DeviceIdType.MESH

WEIGHT_NAMES = ['ffn1_norm', 'ffn1_w_gate', 'ffn1_w_up', 'ffn1_w_down', 'mix_norm', 'w_in', 'mla_q_a_norm', 'mla_w_q_b',
                'mla_kv_a_norm', 'mla_w_kv_b', 'mla_q_norm', 'mla_k_norm', 'swa_q_norm', 'swa_k_norm', 'swa_sinks',
                'mla_out_norm', 'swa_out_norm', 'w_o', 'ffn2_norm', 'ffn2_w_gate', 'ffn2_w_up', 'ffn2_w_down']
BIG = {'ffn1_w_gate': ((D_MODEL, D_FF), 1), 'ffn1_w_up': ((D_MODEL, D_FF), 1), 'ffn1_w_down': ((D_FF, D_MODEL), 0),
       'w_in': ((D_MODEL, IN_COLS), 1), 'mla_w_q_b': ((MLA_Q_RANK, MLA_HEADS * MLA_QK), 1),
       'mla_w_kv_b': ((MLA_KV_RANK, MLA_HEADS * (MLA_NOPE + MLA_V)), 1), 'w_o': ((MIX_WIDTH, D_MODEL), 0),
       'ffn2_w_gate': ((D_MODEL, D_FF), 1), 'ffn2_w_up': ((D_MODEL, D_FF), 1), 'ffn2_w_down': ((D_FF, D_MODEL), 0)}
BIG_NAMES = [n for n in WEIGHT_NAMES if n in BIG]
SMALL_NAMES = [n for n in WEIGHT_NAMES if n not in BIG]

_pallas_call = pl.pallas_call


def _params(**kw):
    return pltpu.CompilerParams(vmem_limit_bytes=VMEM_LIMIT, **kw)


def _full(shape):
    n = len(shape)
    return pl.BlockSpec(shape, lambda *_: (0,) * n)


def _resident(shape):
    n = len(shape)
    return pl.BlockSpec(shape, lambda *_: (0,) * n, pipeline_mode=pl.Buffered(1))


@jax.custom_vjp
def _mm(a, w):
    return jnp.dot(a.astype(BF16), w, preferred_element_type=F32)


def _mm_fwd(a, w):
    return _mm(a, w), w


def _mm_bwd(w, dy):
    return lax.dot_general(dy.astype(BF16), w, (((1,), (1,)), ((), ())), preferred_element_type=F32), None


_mm.defvjp(_mm_fwd, _mm_bwd)


def _dot_nt(a, b):
    return lax.dot_general(a, b, (((1,), (1,)), ((), ())), preferred_element_type=F32)


def _dot_tn(a, b):
    return lax.dot_general(a, b, (((0,), (0,)), ((), ())), preferred_element_type=F32)


def _rms(t, g):
    return t * lax.rsqrt(jnp.mean(t * t, axis=-1, keepdims=True) + EPS) * g


def _sigmoid(z):
    return 1.0 / (1.0 + jnp.exp(-z))


def _row_tile(s, want):
    return min(want, s)


FF_CHUNK = 1408


def _weight_operand(w):
    if isinstance(w, tuple):
        arr, block, index = w
        return arr, pl.BlockSpec(block, lambda *_: index, pipeline_mode=pl.Buffered(1))
    return w, _resident(w.shape)


def _weight_rows(ref, start, n):
    if len(ref.shape) == 2:
        return ref[start:start + n, :]
    per = ref.shape[1]
    return ref[start // per:(start + n) // per].reshape(n, ref.shape[2])


def _ffn_fwd(x, g, wg, wu, wd, name, gather=None):
    s = x.shape[0]
    tm = _row_tile(s, 256)
    steps = s // tm
    ng = len(gather) if gather else 0
    (wg, wg_spec), (wu, wu_spec), (wd, wd_spec) = _weight_operand(wg), _weight_operand(wu), _weight_operand(wd)

    def body(x_ref, g_ref, wg_ref, wu_ref, wd_ref, *rest):
        g_ins, (y_ref, gate_ref, up_ref), g_outs, sems = rest[:ng], rest[ng:ng + 3], rest[ng + 3:2 * ng + 3], rest[2 * ng + 3:]
        if ng:
            _run_stages_at(_gather_stages(g_ins, g_outs, *sems), [(0,), (steps * 3 // 8,), (steps * 11 // 16,), (steps - 1,)])
        xv = x_ref[...]
        nb = _rms(xv, g_ref[...]).astype(BF16)
        acc = xv
        for c in range(0, D_FF, FF_CHUNK):
            gate = jnp.dot(nb, wg_ref[:, c:c + FF_CHUNK], preferred_element_type=F32)
            up = jnp.dot(nb, wu_ref[:, c:c + FF_CHUNK], preferred_element_type=F32)
            gate_ref[:, c:c + FF_CHUNK] = gate.astype(BF16)
            up_ref[:, c:c + FF_CHUNK] = up.astype(BF16)
            act = (gate * _sigmoid(gate) * up).astype(BF16)
            acc = acc + 0.5 * jnp.dot(act, _weight_rows(wd_ref, c, FF_CHUNK), preferred_element_type=F32)
        y_ref[...] = acc

    outs = _pallas_call(
        body, name=name, grid=(steps,),
        in_specs=[pl.BlockSpec((tm, D_MODEL), lambda i: (i, 0)), _full((1, D_MODEL)), wg_spec, wu_spec, wd_spec] + [_HBM] * ng,
        out_specs=[pl.BlockSpec((tm, D_MODEL), lambda i: (i, 0)), pl.BlockSpec((tm, D_FF), lambda i: (i, 0)),
                   pl.BlockSpec((tm, D_FF), lambda i: (i, 0))] + [_HBM] * ng,
        out_shape=[jax.ShapeDtypeStruct((s, D_MODEL), F32), jax.ShapeDtypeStruct((s, D_FF), BF16),
                   jax.ShapeDtypeStruct((s, D_FF), BF16)] + (_gather_out_shapes(gather) if ng else []),
        scratch_shapes=_exchange_scratch(8, ng) if ng else [],
        compiler_params=_params(dimension_semantics=("arbitrary",)),
    )(x, g, wg, wu, wd, *(gather or []))
    return outs[0], outs[1], outs[2], outs[3:]


def _ffn_bwd(x, dy, gate, up, g, wg, wu, wd, name, scatter=None):
    s = x.shape[0]
    tm = _row_tile(s, 256)
    steps = s // tm
    ng = len(scatter) if scatter else 0
    (wg, wg_spec), (wu, wu_spec), (wd, wd_spec) = _weight_operand(wg), _weight_operand(wu), _weight_operand(wd)

    def body(x_ref, dy_ref, gate_ref, up_ref, g_ref, wg_ref, wu_ref, wd_ref, *rest):
        c_ins, (dx_ref, dgain_ref, n_ref, act_ref, dgate_ref, dup_ref) = rest[:ng], rest[ng:ng + 6]
        c_outs, sems = rest[ng + 6:2 * ng + 6], rest[2 * ng + 6:]
        if ng:
            _run_stages_at(_scatter_stages(c_ins, c_outs, *sems), [(0,), (steps - 1,)])
        i = pl.program_id(0)
        xv = x_ref[...]
        dyv = dy_ref[...]
        gv = g_ref[...]
        r = lax.rsqrt(jnp.mean(xv * xv, axis=-1, keepdims=True) + EPS)
        xh = xv * r
        n_ref[...] = (xh * gv).astype(BF16)
        dyh = (0.5 * dyv).astype(BF16)
        dn = jnp.zeros_like(xv)
        for c in range(0, D_FF, FF_CHUNK):
            dact = _dot_nt(dyh, _weight_rows(wd_ref, c, FF_CHUNK))
            gt = gate_ref[:, c:c + FF_CHUNK].astype(F32)
            u = up_ref[:, c:c + FF_CHUNK].astype(F32)
            sg = _sigmoid(gt)
            sl = gt * sg
            act_ref[:, c:c + FF_CHUNK] = (sl * u).astype(BF16)
            dup = (dact * sl).astype(BF16)
            dgate = (dact * u * (sg * (1.0 + gt * (1.0 - sg)))).astype(BF16)
            dup_ref[:, c:c + FF_CHUNK] = dup
            dgate_ref[:, c:c + FF_CHUNK] = dgate
            dn = dn + _dot_nt(dgate, wg_ref[:, c:c + FF_CHUNK]) + _dot_nt(dup, wu_ref[:, c:c + FF_CHUNK])
        part = jnp.sum(dn * xh, axis=0, keepdims=True)

        @pl.when(i == 0)
        def _():
            dgain_ref[...] = part

        @pl.when(i > 0)
        def _():
            dgain_ref[...] += part

        dxh = dn * gv
        dx_ref[...] = dyv + r * (dxh - xh * jnp.mean(dxh * xh, axis=-1, keepdims=True))

    row = lambda w: pl.BlockSpec((tm, w), lambda i: (i, 0))
    outs = _pallas_call(
        body, name=name, grid=(steps,),
        in_specs=[row(D_MODEL), row(D_MODEL), row(D_FF), row(D_FF), _full((1, D_MODEL)), wg_spec, wu_spec, wd_spec]
        + [_HBM] * ng,
        out_specs=[row(D_MODEL), _full((1, D_MODEL)), row(D_MODEL), row(D_FF), row(D_FF), row(D_FF)] + [_HBM] * ng,
        out_shape=[jax.ShapeDtypeStruct((s, D_MODEL), F32), jax.ShapeDtypeStruct((1, D_MODEL), F32),
                   jax.ShapeDtypeStruct((s, D_MODEL), BF16), jax.ShapeDtypeStruct((s, D_FF), BF16),
                   jax.ShapeDtypeStruct((s, D_FF), BF16), jax.ShapeDtypeStruct((s, D_FF), BF16)]
        + (_scatter_out_shapes(scatter) if ng else []),
        scratch_shapes=_exchange_scratch(3, ng) if ng else [],
        compiler_params=_params(dimension_semantics=("arbitrary",)),
    )(x, dy, gate, up, g, wg, wu, wd, *(scatter or []))
    return outs[:6], outs[6:]


def _store_col_shards(o_ref, acc, first_shard, n_here, width):
    for q in range(n_here):
        o_ref[q] = acc[:, (first_shard + q) * width:(first_shard + q + 1) * width].astype(BF16)


def _matmul_tn(a, b, scale, name, col_shards):
    t, m = a.shape
    n = b.shape[1]
    tk = _row_tile(t, 2048)
    tn = n // 2
    per = n // N_SHARD
    nk = t // tk

    def body(a_ref, b_ref, o_ref, acc_ref):
        k = pl.program_id(1)
        bv = b_ref[...]
        if scale != 1.0:
            bv = bv.astype(F32) * scale
        part = _dot_tn(a_ref[...].astype(BF16), bv.astype(BF16))

        @pl.when(k == 0)
        def _():
            acc_ref[...] = part

        @pl.when(k > 0)
        def _():
            acc_ref[...] += part

        @pl.when(k == nk - 1)
        def _():
            if col_shards:
                _store_col_shards(o_ref, acc_ref[...], 0, tn // per, per)
            else:
                o_ref[...] = acc_ref[...].astype(BF16)

    if col_shards:
        out_spec = pl.BlockSpec((tn // per, m, per), lambda j, k: (j, 0, 0))
        out_shape = jax.ShapeDtypeStruct((N_SHARD, m, per), BF16)
    else:
        out_spec = pl.BlockSpec((m, tn), lambda j, k: (0, j))
        out_shape = jax.ShapeDtypeStruct((m, n), BF16)
    return _pallas_call(
        body, name=name, grid=(n // tn, nk),
        in_specs=[pl.BlockSpec((tk, m), lambda j, k: (k, 0)), pl.BlockSpec((tk, tn), lambda j, k: (k, j))],
        out_specs=out_spec, out_shape=out_shape, scratch_shapes=[pltpu.VMEM((m, tn), F32)],
        compiler_params=_params(dimension_semantics=("arbitrary", "arbitrary")),
    )(a, b)


_HALF = SWA_HEAD_DIM // 2
_IN_ORDER = (list(range(0, IN_OFFS[2]))
             + [IN_OFFS[3] + SWA_HEAD_DIM * h + i for h in range(SWA_HEADS) for i in range(_HALF)]
             + [IN_OFFS[3] + SWA_HEAD_DIM * h + _HALF + i for h in range(SWA_HEADS) for i in range(_HALF)]
             + list(range(IN_OFFS[5], IN_OFFS[6]))
             + [IN_OFFS[4] + SWA_HEAD_DIM * j + i for j in range(SWA_KV_HEADS) for i in range(_HALF)]
             + [IN_OFFS[4] + SWA_HEAD_DIM * j + _HALF + i for j in range(SWA_KV_HEADS) for i in range(_HALF)]
             + list(range(IN_OFFS[2], IN_OFFS[3])))
_QB_ORDER = ([MLA_QK * h + i for h in range(MLA_HEADS) for i in range(MLA_NOPE)]
             + [MLA_QK * h + MLA_NOPE + i for h in range(MLA_HEADS) for i in range(_HALF)]
             + [MLA_QK * h + MLA_NOPE + _HALF + i for h in range(MLA_HEADS) for i in range(_HALF)])
_KVB_ORDER = ([(MLA_NOPE + MLA_V) * h + i for h in range(MLA_HEADS) for i in range(MLA_NOPE)]
              + [(MLA_NOPE + MLA_V) * h + MLA_NOPE + i for h in range(MLA_HEADS) for i in range(MLA_V)])
_P_CQ, _P_CKV, _P_QA, _P_QB, _P_VS, _P_KA, _P_KB, _P_PE = (int(v) for v in np.cumsum(
    (0, MLA_Q_RANK, MLA_KV_RANK, SWA_WIDTH // 2, SWA_WIDTH // 2, IN_SPLITS[5], IN_SPLITS[4] // 2, IN_SPLITS[4] // 2)))


def _runs(order):
    out, start = [], 0
    for i in range(1, len(order) + 1):
        if i == len(order) or order[i] != order[i - 1] + 1:
            out.append((order[start], i - start))
            start = i
    return out


def _inverse(order):
    inv = [0] * len(order)
    for new, old in enumerate(order):
        inv[old] = new
    return inv


def _take_cols(a, order):
    return jnp.concatenate([a[..., st:st + w] for st, w in _runs(order)], axis=-1)


def _segment_matrix(n, seg):
    return (lax.broadcasted_iota(jnp.int32, (n, n), 0) // seg == lax.broadcasted_iota(jnp.int32, (n, n), 1) // seg).astype(BF16)


@jax.custom_vjp
def _cmm(t, b, bt):
    hi = t.astype(BF16)
    lo = (t - hi.astype(F32)).astype(BF16)
    return jnp.dot(hi, b, preferred_element_type=F32) + jnp.dot(lo, b, preferred_element_type=F32)


def _cmm_fwd(t, b, bt):
    return _cmm(t, b, bt), (b, bt)


def _cmm_bwd(res, dy):
    b, bt = res
    return _cmm(dy, bt, b), None, None


_cmm.defvjp(_cmm_fwd, _cmm_bwd)


def _segsum(t, b):
    return _cmm(t, b, b)


def _rowsum(t):
    n = t.shape[-1]
    return _cmm(t, jnp.ones((n, 128), BF16), jnp.ones((128, n), BF16))


def _by_head(vals, width):
    lane = lax.broadcasted_iota(jnp.int32, (vals[0].shape[0], len(vals) * width), 1)
    out = vals[-1]
    for hd in range(len(vals) - 2, -1, -1):
        out = jnp.where(lane < (hd + 1) * width, vals[hd], out)
    return out


def _rope2(a, b, cos, sin):
    return a * cos - b * sin, b * cos + a * sin


def _pre_math(x, gm, gqa, gkva, gq, gk, gsq, gsk, taps, win, wqb, wkvb, cos, sin):
    h = _rms(x, gm)
    proj = _mm(h, win)
    if taps is not None:
        proj = proj + taps[0]
    cqn = _rms(proj[:, _P_CQ:_P_CKV], gqa)
    qa_all = _mm(cqn, wqb)
    ckvn = _rms(proj[:, _P_CKV:_P_QA], gkva)
    kv_all = _mm(ckvn, wkvb)
    if taps is not None:
        qa_all = qa_all + taps[1]
        kv_all = kv_all + taps[2]
    nh, hw = MLA_HEADS, MLA_HEADS * _HALF
    seg_mla = _segment_matrix(hw, _HALF)
    tile = lambda g, n: jnp.concatenate([g] * n, axis=-1)
    c4, s4 = cos[:, :hw], sin[:, :hw]

    def mla_heads(nope, r1, r2, gain):
        rr = r1 * r1 + r2 * r2
        lane_head = lax.broadcasted_iota(jnp.int32, (hw, nh * MLA_NOPE), 0) // _HALF
        spread = (lane_head == lax.broadcasted_iota(jnp.int32, (hw, nh * MLA_NOPE), 1) // MLA_NOPE).astype(BF16)
        rope_on_nope = _cmm(rr, spread, spread.T)
        ss_nope = [_rowsum(jnp.square(nope[:, hd * MLA_NOPE:(hd + 1) * MLA_NOPE])) for hd in range(nh)]
        rinv = [lax.rsqrt((ss_nope[hd] + rope_on_nope[:, hd * MLA_NOPE:(hd + 1) * MLA_NOPE]) * (1.0 / MLA_QK) + EPS)
                for hd in range(nh)]
        rl = lax.rsqrt((_segsum(rr, seg_mla) + _by_head(ss_nope, _HALF)) * (1.0 / MLA_QK) + EPS)
        o1, o2 = _rope2(r1 * rl * tile(gain[:, MLA_NOPE:MLA_NOPE + _HALF], nh), r2 * rl * tile(gain[:, MLA_NOPE + _HALF:], nh), c4, s4)
        return [jnp.concatenate([nope[:, hd * MLA_NOPE:(hd + 1) * MLA_NOPE] * rinv[hd] * gain[:, :MLA_NOPE],
                                 o1[:, hd * _HALF:(hd + 1) * _HALF], o2[:, hd * _HALF:(hd + 1) * _HALF]], axis=-1)
                for hd in range(nh)]

    q_a = mla_heads(qa_all[:, :nh * MLA_NOPE], qa_all[:, nh * MLA_NOPE:nh * MLA_NOPE + hw], qa_all[:, nh * MLA_NOPE + hw:], gq)
    pe1, pe2 = proj[:, _P_PE:_P_PE + _HALF], proj[:, _P_PE + _HALF:_P_PE + 2 * _HALF]
    k_a = mla_heads(kv_all[:, :nh * MLA_NOPE], tile(pe1, nh), tile(pe2, nh), gk)
    v_a = [kv_all[:, nh * MLA_NOPE + hd * MLA_V:nh * MLA_NOPE + (hd + 1) * MLA_V] for hd in range(nh)]

    def swa_heads(a, b, gain, n):
        w = n * _HALF
        r = lax.rsqrt(_segsum(a * a + b * b, _segment_matrix(w, _HALF)) * (1.0 / SWA_HEAD_DIM) + EPS)
        o1, o2 = _rope2(a * r * tile(gain[:, :_HALF], n), b * r * tile(gain[:, _HALF:], n), cos[:, :w], sin[:, :w])
        return [jnp.concatenate([o1[:, hd * _HALF:(hd + 1) * _HALF], o2[:, hd * _HALF:(hd + 1) * _HALF]], axis=-1) for hd in range(n)]

    q_b = swa_heads(proj[:, _P_QA:_P_QB], proj[:, _P_QB:_P_VS], gsq, SWA_HEADS)
    k_b = swa_heads(proj[:, _P_KA:_P_KB], proj[:, _P_KB:_P_PE], gsk, SWA_KV_HEADS)
    v_b = [proj[:, _P_VS + j * SWA_HEAD_DIM:_P_VS + (j + 1) * SWA_HEAD_DIM] for j in range(SWA_KV_HEADS)]
    return (q_a, k_a, v_a, q_b, k_b, v_b), (h, cqn, ckvn)


_PRE_GAIN_WIDTHS = (D_MODEL, MLA_Q_RANK, MLA_KV_RANK, MLA_QK, MLA_QK, SWA_HEAD_DIM, SWA_HEAD_DIM)
_PRE_HEADS = ((MLA_HEADS, MLA_QK), (MLA_HEADS, MLA_QK), (MLA_HEADS, MLA_V),
              (SWA_HEADS, SWA_HEAD_DIM), (SWA_KV_HEADS, SWA_HEAD_DIM), (SWA_KV_HEADS, SWA_HEAD_DIM))


def _pre_fwd(x, gains, win, wqb, wkvb, cos, sin, name):
    s = x.shape[0]
    tm = _row_tile(s, 512)

    def body(x_ref, *refs):
        g_refs, (win_ref, wqb_ref, wkvb_ref, cos_ref, sin_ref), out_refs = refs[:7], refs[7:12], refs[12:]
        outs, _ = _pre_math(x_ref[...], *[g[...] for g in g_refs], None, win_ref[...], wqb_ref[...], wkvb_ref[...],
                            cos_ref[...], sin_ref[...])
        for idx, (ref, heads) in enumerate(zip(out_refs, outs)):
            for hd, val in enumerate(heads):
                ref[hd] = (val * MLA_QSCALE if idx == 0 else val).astype(BF16)

    heads_spec = lambda nh, w: pl.BlockSpec((nh, tm, w), lambda i: (0, i, 0))
    return _pallas_call(
        body, name=name, grid=(s // tm,),
        in_specs=[pl.BlockSpec((tm, D_MODEL), lambda i: (i, 0))] + [_full((1, w)) for w in _PRE_GAIN_WIDTHS]
        + [_resident(win.shape), _resident(wqb.shape), _resident(wkvb.shape),
           pl.BlockSpec((tm, SWA_HEADS * _HALF), lambda i: (i, 0)), pl.BlockSpec((tm, SWA_HEADS * _HALF), lambda i: (i, 0))],
        out_specs=[heads_spec(nh, w) for nh, w in _PRE_HEADS],
        out_shape=[jax.ShapeDtypeStruct((nh, s, w), BF16) for nh, w in _PRE_HEADS],
        compiler_params=_params(dimension_semantics=("arbitrary",)),
    )(x, *gains, win, wqb, wkvb, cos, sin)


def _pre_bwd(x, dx_res, cts, gains, win, wqb, wkvb, cos, sin, name):
    s = x.shape[0]
    tm = _row_tile(s, 256)
    tap_widths = (IN_COLS, MLA_HEADS * MLA_QK, MLA_HEADS * (MLA_NOPE + MLA_V))

    def body(x_ref, dxr_ref, *refs):
        ct_refs, g_refs = refs[:6], refs[6:13]
        win_ref, wqb_ref, wkvb_ref, cos_ref, sin_ref = refs[13:18]
        dx_ref, dg_refs, dw_refs, acc_refs = refs[18], refs[19:26], refs[26:29], refs[29:32]
        i = pl.program_id(0)
        win_v, wqb_v, wkvb_v, cos_v, sin_v = win_ref[...], wqb_ref[...], wkvb_ref[...], cos_ref[...], sin_ref[...]

        def f(xv, gm, gqa, gkva, gq, gk, gsq, gsk, t0, t1, t2):
            return _pre_math(xv, gm, gqa, gkva, gq, gk, gsq, gsk, (t0, t1, t2), win_v, wqb_v, wkvb_v, cos_v, sin_v)

        taps = [jnp.zeros((tm, w), F32) for w in tap_widths]
        _, vjp, acts = jax.vjp(f, x_ref[...], *[g[...] for g in g_refs], *taps, has_aux=True)
        ct = tuple([ref[hd] for hd in range(nh)] for ref, (nh, _) in zip(ct_refs, _PRE_HEADS))
        grads = vjp(ct)
        dx_ref[...] = grads[0] + dxr_ref[...]
        dws = [_dot_tn(a.astype(BF16), t.astype(BF16)) for a, t in zip(acts, grads[8:11])]

        @pl.when(i == 0)
        def _():
            for ref, val in zip(dg_refs, grads[1:8]):
                ref[...] = val
            for ref, val in zip(acc_refs, dws):
                ref[...] = val

        @pl.when(i > 0)
        def _():
            for ref, val in zip(dg_refs, grads[1:8]):
                ref[...] += val
            for ref, val in zip(acc_refs, dws):
                ref[...] += val

        @pl.when(i == s // tm - 1)
        def _():
            for ref, acc, order in zip(dw_refs, acc_refs, (_IN_ORDER, _QB_ORDER, _KVB_ORDER)):
                _store_col_shards(ref, _take_cols(acc[...], _inverse(order)), 0, N_SHARD, acc.shape[1] // N_SHARD)

    heads_spec = lambda nh, w: pl.BlockSpec((nh, tm, w), lambda i: (0, i, 0))
    row = pl.BlockSpec((tm, D_MODEL), lambda i: (i, 0))
    half = pl.BlockSpec((tm, SWA_HEADS * _HALF), lambda i: (i, 0))
    shard_shapes = [(N_SHARD, w.shape[0], w.shape[1] // N_SHARD) for w in (win, wqb, wkvb)]
    return _pallas_call(
        body, name=name, grid=(s // tm,),
        in_specs=[row, row] + [heads_spec(nh, w) for nh, w in _PRE_HEADS] + [_full((1, w)) for w in _PRE_GAIN_WIDTHS]
        + [_resident(win.shape), _resident(wqb.shape), _resident(wkvb.shape), half, half],
        out_specs=[row] + [_full((1, w)) for w in _PRE_GAIN_WIDTHS] + [_full(shp) for shp in shard_shapes],
        out_shape=[jax.ShapeDtypeStruct((s, D_MODEL), F32)] + [jax.ShapeDtypeStruct((1, w), F32) for w in _PRE_GAIN_WIDTHS]
        + [jax.ShapeDtypeStruct(shp, BF16) for shp in shard_shapes],
        scratch_shapes=[pltpu.VMEM(w.shape, F32) for w in (win, wqb, wkvb)],
        compiler_params=_params(dimension_semantics=("arbitrary",)),
    )(x, dx_res, *cts, *gains, win, wqb, wkvb, cos, sin)


def _post_math(oa, ob, ga, gb, wo):
    mixed = jnp.concatenate([_rms(jnp.concatenate(oa, axis=-1), ga), _rms(jnp.concatenate(ob, axis=-1), gb)], axis=-1)
    return _mm(mixed, wo), mixed


def _post_fwd(x, oa, ob, ga, gb, wo, name):
    s = x.shape[0]
    tm = _row_tile(s, 512)

    def body(x_ref, oa_ref, ob_ref, ga_ref, gb_ref, wo_ref, y_ref):
        y, _ = _post_math([oa_ref[hd] for hd in range(MLA_HEADS)], [ob_ref[hd] for hd in range(SWA_HEADS)],
                          ga_ref[...], gb_ref[...], wo_ref[...])
        y_ref[...] = x_ref[...] + y

    row = pl.BlockSpec((tm, D_MODEL), lambda i: (i, 0))
    return _pallas_call(
        body, name=name, grid=(s // tm,),
        in_specs=[row, pl.BlockSpec((MLA_HEADS, tm, MLA_V), lambda i: (0, i, 0)),
                  pl.BlockSpec((SWA_HEADS, tm, SWA_HEAD_DIM), lambda i: (0, i, 0)),
                  _full((1, MLA_WIDTH)), _full((1, SWA_WIDTH)), _resident(wo.shape)],
        out_specs=row, out_shape=jax.ShapeDtypeStruct((s, D_MODEL), F32),
        compiler_params=_params(dimension_semantics=("arbitrary",)),
    )(x, oa, ob, ga, gb, wo)


def _post_bwd(dy, oa, ob, ga, gb, wo, name):
    s = dy.shape[0]
    tm = _row_tile(s, 512)

    def body(dy_ref, oa_ref, ob_ref, ga_ref, gb_ref, wo_ref, doa_ref, dob_ref, dga_ref, dgb_ref, dwo_ref, acc_ref):
        i = pl.program_id(0)
        wo_v = wo_ref[...]
        dyv = dy_ref[...]

        def f(oa_l, ob_l, ga_v, gb_v):
            return _post_math(oa_l, ob_l, ga_v, gb_v, wo_v)

        _, vjp, mixed = jax.vjp(f, [oa_ref[hd] for hd in range(MLA_HEADS)], [ob_ref[hd] for hd in range(SWA_HEADS)],
                                ga_ref[...], gb_ref[...], has_aux=True)
        doa, dob, dga, dgb = vjp(dyv)
        for hd in range(MLA_HEADS):
            doa_ref[hd] = doa[hd]
        for hd in range(SWA_HEADS):
            dob_ref[hd] = dob[hd]
        dwo = _dot_tn(mixed.astype(BF16), dyv.astype(BF16))

        @pl.when(i == 0)
        def _():
            dga_ref[...] = dga
            dgb_ref[...] = dgb
            acc_ref[...] = dwo

        @pl.when(i > 0)
        def _():
            dga_ref[...] += dga
            dgb_ref[...] += dgb
            acc_ref[...] += dwo

        @pl.when(i == s // tm - 1)
        def _():
            dwo_ref[...] = acc_ref[...].astype(BF16)

    row = pl.BlockSpec((tm, D_MODEL), lambda i: (i, 0))
    oa_spec = pl.BlockSpec((MLA_HEADS, tm, MLA_V), lambda i: (0, i, 0))
    ob_spec = pl.BlockSpec((SWA_HEADS, tm, SWA_HEAD_DIM), lambda i: (0, i, 0))
    return _pallas_call(
        body, name=name, grid=(s // tm,),
        in_specs=[row, oa_spec, ob_spec, _full((1, MLA_WIDTH)), _full((1, SWA_WIDTH)), _resident(wo.shape)],
        out_specs=[oa_spec, ob_spec, _full((1, MLA_WIDTH)), _full((1, SWA_WIDTH)), _full(wo.shape)],
        out_shape=[jax.ShapeDtypeStruct((MLA_HEADS, s, MLA_V), F32), jax.ShapeDtypeStruct((SWA_HEADS, s, SWA_HEAD_DIM), F32),
                   jax.ShapeDtypeStruct((1, MLA_WIDTH), F32), jax.ShapeDtypeStruct((1, SWA_WIDTH), F32),
                   jax.ShapeDtypeStruct(wo.shape, BF16)],
        scratch_shapes=[pltpu.VMEM(wo.shape, F32)],
        compiler_params=_params(dimension_semantics=("arbitrary",)),
    )(dy, oa, ob, ga, gb, wo)


def _attn_tile(s):
    return 512 if s >= 2048 else 128


def _causal_mask(t):
    return lax.broadcasted_iota(jnp.int32, (t, t), 1) <= lax.broadcasted_iota(jnp.int32, (t, t), 0)


def _pipelined_blocks(first, count, last_block, issue, consume, carry, prefetch_after):
    def clamped(j, slot):
        issue(jnp.minimum(j, last_block), slot)

    def pair(jj, c):
        a = first + 2 * jj
        clamped(a + 1, 1)
        c = consume(a, 0, c)
        clamped(a + 2, 0)
        return consume(a + 1, 1, c)

    clamped(first, 0)
    npairs = count // 2
    carry = lax.fori_loop(0, npairs, pair, carry)

    def odd(c):
        c = consume(first + 2 * npairs, 0, c)
        if prefetch_after:
            clamped(first + count, 0)
        return c

    return lax.cond(count - 2 * npairs == 1, odd, lambda c: c, carry)


def _run_stages_at(stages, steps):
    for stage, step in zip(stages, steps):
        here = pl.program_id(0) == step[0]
        for axis in range(1, len(step)):
            here = here & (pl.program_id(axis) == step[axis])
        pl.when(here)(stage)


def _mla_fwd(q, k, v, name, gather=None):
    nh, s, _ = q.shape
    t = _attn_tile(s)
    nq = s // t
    ng = len(gather) if gather else 0

    def body(q_ref, k_ref, v_ref, *rest):
        g_ins, (o_ref, lse_ref), g_outs = rest[:ng], rest[ng:ng + 2], rest[ng + 2:2 * ng + 2]
        (s0_ref, s1_ref), sems = rest[2 * ng + 2:2 * ng + 4], rest[2 * ng + 4:]
        if ng:
            _run_stages_at(_gather_stages(g_ins, g_outs, *sems), [(0, 0), (nh // 2, 0), (nh - 1, 0), (nh - 1, nq - 1)])
        qi = pl.program_id(1)
        qv = q_ref[...]
        s_refs = (s0_ref, s1_ref)

        def rows(j):
            return pl.ds(pl.multiple_of(j * t, t), t)

        def issue(j, slot):
            s_refs[slot][...] = _dot_nt(qv, k_ref[rows(j), :])

        def consume(j, slot, carry, masked=False):
            m, l, acc = carry
            sc = s_refs[slot][...]
            if masked:
                sc = jnp.where(_causal_mask(t), sc, NEG)
            m_new = jnp.maximum(m, jnp.max(sc, axis=-1, keepdims=True))
            alpha = jnp.exp2(m - m_new)
            p = jnp.exp2(sc - m_new)
            l = alpha * l + jnp.sum(p, axis=-1, keepdims=True)
            acc = alpha * acc + jnp.dot(p.astype(BF16), v_ref[rows(j), :], preferred_element_type=F32)
            return m_new, l, acc

        init = (jnp.full((t, 1), NEG, F32), jnp.zeros((t, 1), F32), jnp.zeros((t, MLA_V), F32))
        carry = _pipelined_blocks(0, qi, nq - 1, issue, consume, init, True)
        m, l, acc = consume(qi, 0, carry, masked=True)
        o_ref[...] = acc / l
        lse_ref[...] = m + jnp.log2(l)

    outs = _pallas_call(
        body, name=name, grid=(nh, nq),
        in_specs=[pl.BlockSpec((None, t, MLA_QK), lambda h, i: (h, i, 0)), pl.BlockSpec((None, s, MLA_QK), lambda h, i: (h, 0, 0)),
                  pl.BlockSpec((None, s, MLA_V), lambda h, i: (h, 0, 0))] + [_HBM] * ng,
        out_specs=[pl.BlockSpec((None, t, MLA_V), lambda h, i: (h, i, 0)), pl.BlockSpec((None, t, 1), lambda h, i: (h, i, 0))]
        + [_HBM] * ng,
        out_shape=[jax.ShapeDtypeStruct((nh, s, MLA_V), F32), jax.ShapeDtypeStruct((nh, s, 1), F32)]
        + (_gather_out_shapes(gather) if ng else []),
        scratch_shapes=[pltpu.VMEM((t, t), F32)] * 2 + (_exchange_scratch(8, ng) if ng else []),
        compiler_params=_params(dimension_semantics=("arbitrary", "arbitrary")),
    )(q, k, v, *(gather or []))
    return outs[0], outs[1], outs[2:]


def _mla_delta(o, do, name):
    nh, s, _ = o.shape
    t = _attn_tile(s)

    def body(o_ref, do_ref, delta_ref):
        prod = do_ref[...] * o_ref[...]
        hi = prod.astype(BF16)
        lo = (prod - hi.astype(F32)).astype(BF16)
        ones = jnp.ones((8, MLA_V), BF16)
        delta_ref[...] = _dot_nt(ones, hi) + _dot_nt(ones, lo)

    tile = pl.BlockSpec((None, t, MLA_V), lambda h, i: (h, i, 0))
    return _pallas_call(body, name=name, grid=(nh, s // t), in_specs=[tile, tile],
                        out_specs=pl.BlockSpec((None, None, 8, t), lambda h, i: (h, i, 0, 0)),
                        out_shape=jax.ShapeDtypeStruct((nh, s // t, 8, t), F32),
                        compiler_params=_params(dimension_semantics=("arbitrary", "arbitrary")))(o, do)


def _mla_bwd(q, k, v, do, lse_row, delta_row, name, scatter=None):
    nh, s, _ = q.shape
    t = _attn_tile(s)
    nq = s // t
    ng = len(scatter) if scatter else 0

    def body(q_ref, k_ref, v_ref, do_ref, lse_ref, delta_ref, *rest):
        c_ins, (dq_ref, dk_ref, dv_ref), c_outs = rest[:ng], rest[ng:ng + 3], rest[ng + 3:2 * ng + 3]
        (s0_ref, s1_ref, dp0_ref, dp1_ref), sems = rest[2 * ng + 3:2 * ng + 7], rest[2 * ng + 7:]
        if ng:
            _run_stages_at(_scatter_stages(c_ins, c_outs, *sems), [(0, 0), (nh - 1, nq - 1)])
        kj = pl.program_id(1)
        kv_, vv = k_ref[...], v_ref[...]
        s_refs, dp_refs = (s0_ref, s1_ref), (dp0_ref, dp1_ref)

        @pl.when(kj == 0)
        def _():
            dq_ref[...] = jnp.zeros_like(dq_ref)

        def rows(i):
            return pl.ds(pl.multiple_of(i * t, t), t)

        def issue(i, slot):
            s_refs[slot][...] = _dot_nt(kv_, q_ref[rows(i), :])
            dp_refs[slot][...] = _dot_nt(vv, do_ref[rows(i), :].astype(BF16))

        def consume(i, slot, carry, masked=False):
            dk, dv = carry
            p = jnp.exp2(s_refs[slot][...] - lse_ref[pl.ds(i, 1), :])
            if masked:
                p = jnp.where(lax.broadcasted_iota(jnp.int32, (t, t), 0) <= lax.broadcasted_iota(jnp.int32, (t, t), 1), p, 0.0)
            dv = dv + jnp.dot(p.astype(BF16), do_ref[rows(i), :].astype(BF16), preferred_element_type=F32)
            ds = (p * (dp_refs[slot][...] - delta_ref[i][0:1, :])).astype(BF16)
            dk = dk + jnp.dot(ds, q_ref[rows(i), :], preferred_element_type=F32)
            dq_ref[rows(i), :] += _dot_tn(ds, kv_) * MLA_SCALE
            return dk, dv

        issue(kj, 0)
        carry = consume(kj, 0, (jnp.zeros((t, MLA_QK), F32), jnp.zeros((t, MLA_V), F32)), masked=True)
        dk, dv = _pipelined_blocks(kj + 1, nq - 1 - kj, nq - 1, issue, consume, carry, False)
        dk_ref[...] = dk * LN2
        dv_ref[...] = dv

    tile = lambda w: pl.BlockSpec((None, t, w), lambda h, j: (h, j, 0))
    whole = lambda w: pl.BlockSpec((None, s, w), lambda h, j: (h, 0, 0))
    rows_spec = pl.BlockSpec((None, nq, t), lambda h, j: (h, 0, 0))
    outs = _pallas_call(
        body, name=name, grid=(nh, nq),
        in_specs=[whole(MLA_QK), tile(MLA_QK), tile(MLA_V), whole(MLA_V), rows_spec,
                  pl.BlockSpec((None, nq, 8, t), lambda h, j: (h, 0, 0, 0))] + [_HBM] * ng,
        out_specs=[whole(MLA_QK), tile(MLA_QK), tile(MLA_V)] + [_HBM] * ng,
        out_shape=[jax.ShapeDtypeStruct((nh, s, MLA_QK), F32), jax.ShapeDtypeStruct((nh, s, MLA_QK), F32),
                   jax.ShapeDtypeStruct((nh, s, MLA_V), F32)] + (_scatter_out_shapes(scatter) if ng else []),
        scratch_shapes=[pltpu.VMEM((t, t), F32)] * 4 + (_exchange_scratch(3, ng) if ng else []),
        compiler_params=_params(dimension_semantics=("arbitrary", "arbitrary")),
    )(q, k, v, do, lse_row, delta_row, *(scatter or []))
    return outs[0], outs[1], outs[2], outs[3:]


def _swa_tile(s):
    return min(s, 8 * BLOCK)


def _swa_specs(tq):
    nb = tq // BLOCK
    grp = lambda w: pl.BlockSpec((SWA_GROUP, tq, w), lambda j, i: (j, i, 0))
    main = pl.BlockSpec((None, tq, SWA_HEAD_DIM), lambda j, i: (j, i, 0))
    tail = pl.BlockSpec((None, BLOCK, SWA_HEAD_DIM), lambda j, i: (j, nb * (i + 1), 0))
    sink = pl.BlockSpec((None, SWA_GROUP, 128), lambda j, i: (j, 0, 0))
    return grp, main, tail, sink


def _swa_band_mask(first):
    shape = (SWA_GROUP * BLOCK, 2 * BLOCK)
    q_rel = (lax.broadcasted_iota(jnp.int32, shape, 0) & (BLOCK - 1)) + BLOCK
    k_rel = lax.broadcasted_iota(jnp.int32, shape, 1)
    dist = q_rel - k_rel
    return (dist >= 0) & (dist < BLOCK) & ((k_rel >= BLOCK) | jnp.logical_not(first))


def _swa_sink_column(sink_ref):
    sk = sink_ref[...]
    return jnp.concatenate([jnp.broadcast_to(sk[g:g + 1, 0:1], (BLOCK, 1)) for g in range(SWA_GROUP)], axis=0)


def _swa_fwd(q, kpad, vpad, sinks, name):
    _, s, _ = q.shape
    tq = _swa_tile(s)
    grp, main, tail, sink = _swa_specs(tq)
    d = SWA_HEAD_DIM

    def body(q_ref, km_ref, kt_ref, vm_ref, vt_ref, sink_ref, o_ref, lse_ref):
        i = pl.program_id(1)
        kall = jnp.concatenate([km_ref[...], kt_ref[...]], axis=0)
        vall = jnp.concatenate([vm_ref[...], vt_ref[...]], axis=0)
        sink_col = _swa_sink_column(sink_ref)
        for b in range(tq // BLOCK):
            lo = b * BLOCK
            valid = _swa_band_mask(i == 0 if b == 0 else False)
            q4 = q_ref[:, lo:lo + BLOCK, :].reshape(SWA_GROUP * BLOCK, d)
            sc = jnp.where(valid, _dot_nt(q4, kall[lo:lo + 2 * BLOCK]) * SWA_SCALE, NEG)
            m = jnp.maximum(jnp.max(sc, axis=-1, keepdims=True), sink_col)
            e = jnp.exp(sc - m)
            den = jnp.sum(e, axis=-1, keepdims=True) + jnp.exp(sink_col - m)
            out = jnp.dot((e * (1.0 / den)).astype(BF16), vall[lo:lo + 2 * BLOCK], preferred_element_type=F32)
            o_ref[:, lo:lo + BLOCK, :] = out.reshape(SWA_GROUP, BLOCK, d)
            lse_ref[:, lo:lo + BLOCK, :] = (m + jnp.log(den)).reshape(SWA_GROUP, BLOCK, 1)

    return _pallas_call(
        body, name=name, grid=(SWA_KV_HEADS, s // tq),
        in_specs=[grp(d), main, tail, main, tail, sink], out_specs=[grp(d), grp(1)],
        out_shape=[jax.ShapeDtypeStruct((SWA_HEADS, s, d), F32), jax.ShapeDtypeStruct((SWA_HEADS, s, 1), F32)],
        compiler_params=_params(dimension_semantics=("arbitrary", "arbitrary")),
    )(q, kpad, kpad, vpad, vpad, sinks)


def _swa_bwd(q, kpad, vpad, sinks, o, lse, do, name):
    _, s, _ = q.shape
    tq = _swa_tile(s)
    grp, main, tail, sink = _swa_specs(tq)
    d = SWA_HEAD_DIM

    def body(q_ref, km_ref, kt_ref, vm_ref, vt_ref, sink_ref, o_ref, lse_ref, do_ref, dq_ref, dk_ref, dv_ref, dsink_ref):
        i = pl.program_id(1)
        kall = jnp.concatenate([km_ref[...], kt_ref[...]], axis=0)
        vall = jnp.concatenate([vm_ref[...], vt_ref[...]], axis=0)
        sink_col = _swa_sink_column(sink_ref)

        @pl.when(i == 0)
        def _():
            dk_ref[...] = jnp.zeros_like(dk_ref)
            dv_ref[...] = jnp.zeros_like(dv_ref)
            dsink_ref[...] = jnp.zeros_like(dsink_ref)

        dsink = jnp.zeros((SWA_GROUP * BLOCK, 1), F32)
        for b in range(tq // BLOCK):
            lo = b * BLOCK
            valid = _swa_band_mask(i == 0 if b == 0 else False)
            rows4 = SWA_GROUP * BLOCK
            q4 = q_ref[:, lo:lo + BLOCK, :].reshape(rows4, d)
            do4 = do_ref[:, lo:lo + BLOCK, :].reshape(rows4, d)
            lse4 = lse_ref[:, lo:lo + BLOCK, :].reshape(rows4, 1)
            delta = jnp.sum(do4 * o_ref[:, lo:lo + BLOCK, :].reshape(rows4, d), axis=-1, keepdims=True)
            kb, vb = kall[lo:lo + 2 * BLOCK], vall[lo:lo + 2 * BLOCK]
            do4b = do4.astype(BF16)
            p = jnp.where(valid, jnp.exp(_dot_nt(q4, kb) * SWA_SCALE - lse4), 0.0)
            ds = (p * (_dot_nt(do4b, vb) - delta) * SWA_SCALE).astype(BF16)
            dq_ref[:, lo:lo + BLOCK, :] = jnp.dot(ds, kb, preferred_element_type=F32).reshape(SWA_GROUP, BLOCK, d)
            band = pl.ds(pl.multiple_of(i * tq, BLOCK) + lo, 2 * BLOCK)
            dk_ref[band, :] += _dot_tn(ds, q4)
            dv_ref[band, :] += _dot_tn(p.astype(BF16), do4b)
            dsink = dsink - jnp.exp(sink_col - lse4) * delta
        per_head = [jnp.broadcast_to(jnp.sum(dsink[g * BLOCK:(g + 1) * BLOCK], axis=0, keepdims=True), (1, 128))
                    for g in range(SWA_GROUP)]
        dsink_ref[...] += jnp.concatenate(per_head + [jnp.zeros((8 - SWA_GROUP, 128), F32)], axis=0)

    acc = pl.BlockSpec((None, s + BLOCK, d), lambda j, i: (j, 0, 0))
    return _pallas_call(
        body, name=name, grid=(SWA_KV_HEADS, s // tq),
        in_specs=[grp(d), main, tail, main, tail, sink, grp(d), grp(1), grp(d)],
        out_specs=[grp(d), acc, acc, pl.BlockSpec((None, 8, 128), lambda j, i: (j, 0, 0))],
        out_shape=[jax.ShapeDtypeStruct((SWA_HEADS, s, d), F32),
                   jax.ShapeDtypeStruct((SWA_KV_HEADS, s + BLOCK, d), F32),
                   jax.ShapeDtypeStruct((SWA_KV_HEADS, s + BLOCK, d), F32),
                   jax.ShapeDtypeStruct((SWA_KV_HEADS, 8, 128), F32)],
        compiler_params=_params(dimension_semantics=("arbitrary", "arbitrary")),
    )(q, kpad, kpad, vpad, vpad, sinks, o, lse, do)


def _loss_head(y, target, name):
    s = y.shape[0]
    tm = _row_tile(s, 512)

    def body(y_ref, t_ref, dy_ref, loss_ref):
        i = pl.program_id(0)
        err = y_ref[...] - t_ref[...]
        dy_ref[...] = err * (1.0 / D_MODEL)
        part = jnp.broadcast_to(0.5 * jnp.sum(jnp.mean(err * err, axis=-1, keepdims=True), axis=0, keepdims=True), (1, 128))

        @pl.when(i == 0)
        def _():
            loss_ref[...] = part

        @pl.when(i > 0)
        def _():
            loss_ref[...] += part

    row = pl.BlockSpec((tm, D_MODEL), lambda i: (i, 0))
    return _pallas_call(
        body, name=name, grid=(s // tm,), in_specs=[row, row], out_specs=[row, _full((1, 128))],
        out_shape=[jax.ShapeDtypeStruct((s, D_MODEL), F32), jax.ShapeDtypeStruct((1, 128), F32)],
        compiler_params=_params(dimension_semantics=("arbitrary",)),
    )(y, target)


def _adamw(w, g, m, v, name):
    rows, cols = w.shape
    tr = rows
    for cand in (512, 256, 128, 64, 32, 16, 8):
        if rows % cand == 0 and rows > cand:
            tr = cand
            break

    def body(w_ref, g_ref, m_ref, v_ref, d_ref, nm_ref, nv_ref):
        gv = g_ref[...]
        nm = ADAM_B1 * m_ref[...] + (1.0 - ADAM_B1) * gv
        nv = ADAM_B2 * v_ref[...] + (1.0 - ADAM_B2) * (gv * gv)
        m_hat = nm / (1.0 - ADAM_B1 ** ADAM_STEP)
        v_hat = nv / (1.0 - ADAM_B2 ** ADAM_STEP)
        d_ref[...] = -ADAM_LR * (m_hat / (jnp.sqrt(v_hat) + ADAM_EPS) + ADAM_WD * w_ref[...])
        nm_ref[...] = nm
        nv_ref[...] = nv

    blk = pl.BlockSpec((tr, cols), lambda i: (i, 0))
    return _pallas_call(
        body, name=name, grid=(rows // tr,), in_specs=[blk] * 4, out_specs=[blk] * 3,
        out_shape=[jax.ShapeDtypeStruct((rows, cols), F32)] * 3,
        compiler_params=_params(dimension_semantics=("arbitrary",)),
    )(w, g, m, v)


def _position():
    return lax.axis_index("x"), lax.axis_index("y"), lax.axis_index("c")


def _remote(src, dst, send_sems, recv_sems, k, to):
    return pltpu.make_async_remote_copy(src_ref=src, dst_ref=dst, send_sem=send_sems.at[k], recv_sem=recv_sems.at[k],
                                        device_id=to, device_id_type=MESH)


_HBM = pl.BlockSpec(memory_space=pltpu.HBM)


def _gather_stages(ins, outs, send_sems, recv_sems):
    na = len(ins)
    x, y, c = _position()
    me, sibling = (x, y, c), (x, y, 1 - c)
    xn, yn, dg = (1 - x, y), (x, 1 - y), (1 - x, 1 - y)

    def slot(a, chip, pc, half=None):
        ref = outs[a].at[4 * chip[0] + 2 * chip[1] + pc]
        if half is None:
            return ref
        rows = ref.shape[0] // 2
        return ref.at[pl.ds(half * rows, rows)]

    def cp(a, k, chip, pc, half, to, src=None):
        dst = slot(a, chip, pc, half)
        return _remote(dst if src is None else src, dst, send_sems, recv_sems, 8 * a + k, to)

    first_hop = [(0, xn), (1, yn)]
    second_hop = [(0, xn, 2, 0, yn), (1, yn, 3, 1, xn)]

    def sends():
        out = []
        for a in range(na):
            out += [cp(a, k, (x, y), c, None, (*to, c), src=ins[a].at[c]) for k, to in first_hop]
            out += [cp(a, fwd_k, frm, c, half, (*to, c)) for _, frm, fwd_k, half, to in second_hop]
            out += [cp(a, 4 + k, frm, c, None, sibling) for k, frm in first_hop]
            out += [cp(a, 6 + half, dg, c, half, sibling) for half in (0, 1)]
        return out

    def stage0():
        for a in range(na):
            for k, to in first_hop:
                cp(a, k, (x, y), c, None, (*to, c), src=ins[a].at[c]).start()

    def stage1():
        for k, frm, fwd_k, half, to in second_hop:
            for a in range(na):
                cp(a, k, frm, c, None, me).wait_recv()
                cp(a, fwd_k, frm, c, half, (*to, c)).start()
                cp(a, 4 + k, frm, c, None, sibling).start()

    def stage2():
        for half in (0, 1):
            for a in range(na):
                cp(a, 2 + half, dg, c, half, me).wait_recv()
                cp(a, 6 + half, dg, c, half, sibling).start()

    def stage3():
        for a in range(na):
            for k, chip, half in ((4, xn, None), (5, yn, None), (6, dg, 0), (7, dg, 1)):
                cp(a, k, chip, 1 - c, half, me).wait_recv()
        for sent in sends():
            sent.wait_send()

    return [stage0, stage1, stage2, stage3]


def _gather_out_shapes(mine):
    return [jax.ShapeDtypeStruct((N_DEV,) + m.shape[1:], m.dtype) for m in mine]


def _exchange_scratch(per_array, na):
    return [pltpu.SemaphoreType.DMA((per_array * na,)), pltpu.SemaphoreType.DMA((per_array * na,))]


def _all_gather_halves(mine, name):
    na = len(mine)

    def body(*refs):
        for stage in _gather_stages(refs[:na], refs[na:2 * na], *refs[2 * na:]):
            stage()

    return _pallas_call(body, name=name, in_specs=[_HBM] * na, out_specs=[_HBM] * na, out_shape=_gather_out_shapes(mine),
                        scratch_shapes=_exchange_scratch(8, na))(*mine)


def _sibling_exchange(parts, name, other_half):
    na = len(parts)

    def body(*refs):
        ins, outs, (send_sems, recv_sems) = refs[:na], refs[na:2 * na], refs[2 * na:]
        x, y, c = _position()
        copies = [_remote(ins[a].at[:, 1 - c] if other_half else ins[a], outs[a], send_sems, recv_sems, a, (x, y, 1 - c))
                  for a in range(na)]
        for cp in copies:
            cp.start()
        for cp in copies:
            cp.wait()

    return _pallas_call(
        body, name=name, in_specs=[_HBM] * na, out_specs=[_HBM] * na,
        out_shape=[jax.ShapeDtypeStruct(p.shape[:1] + p.shape[2:] if other_half else p.shape, p.dtype) for p in parts],
        scratch_shapes=_exchange_scratch(1, na),
    )(*parts)


def _scatter_stages(ins, outs, send_sems, recv_sems):
    na = len(ins)
    x, y, c = _position()
    chips = [(1 - x, y), (x, 1 - y), (1 - x, 1 - y)]

    def copies():
        return [_remote(ins[a].at[2 * px + py], outs[a].at[j], send_sems, recv_sems, 3 * a + j, (px, py, c))
                for a in range(na) for j, (px, py) in enumerate(chips)]

    def start():
        for cp in copies():
            cp.start()

    def wait():
        for cp in copies():
            cp.wait()

    return [start, wait]


def _scatter_out_shapes(parts):
    return [jax.ShapeDtypeStruct((3,) + p.shape[1:], p.dtype) for p in parts]


def _scatter_to_chips(parts, name):
    na = len(parts)

    def body(*refs):
        for stage in _scatter_stages(refs[:na], refs[na:2 * na], *refs[2 * na:]):
            stage()

    return _pallas_call(body, name=name, in_specs=[_HBM] * na, out_specs=[_HBM] * na, out_shape=_scatter_out_shapes(parts),
                        scratch_shapes=_exchange_scratch(3, na))(*parts)


def _assemble(g4, mine, name, side_by_side):
    _, nl, r, w = g4.shape
    tr = next(cand for cand in (256, 128, 64) if r % cand == 0)

    def body(in_ref, mine_ref, out_ref):
        chip = 2 * lax.axis_index("x") + lax.axis_index("y")
        blocks = [jnp.where(chip == sh, mine_ref[...], in_ref[sh]) for sh in range(N_SHARD)]
        if side_by_side:
            out_ref[...] = jnp.concatenate(blocks, axis=-1)
        else:
            for sh in range(N_SHARD):
                out_ref[sh] = blocks[sh]

    if side_by_side:
        out_spec = pl.BlockSpec((None, tr, N_SHARD * w), lambda l, i: (l, i, 0))
        out_shape = jax.ShapeDtypeStruct((nl, r, N_SHARD * w), g4.dtype)
    else:
        out_spec = pl.BlockSpec((None, N_SHARD, tr, w), lambda l, i: (l, 0, i, 0))
        out_shape = jax.ShapeDtypeStruct((nl, N_SHARD, r, w), g4.dtype)
    return _pallas_call(
        body, name=name, grid=(nl, r // tr),
        in_specs=[pl.BlockSpec((N_SHARD, None, tr, w), lambda l, i: (0, l, i, 0)),
                  pl.BlockSpec((None, tr, w), lambda l, i: (l, i, 0))],
        out_specs=out_spec, out_shape=out_shape,
        compiler_params=_params(dimension_semantics=("arbitrary", "arbitrary")),
    )(g4, mine)


def _all_reduce_small(vec, name):
    r, l = vec.shape

    def body(v_ref, out_ref, gath_ref, send_sems, recv_sems):
        x, y, c = _position()
        me = 4 * x + 2 * y + c
        gath_ref[me] = v_ref[...]
        copies = []
        for k in range(1, N_DEV):
            to = (x ^ (k >> 2), y ^ ((k >> 1) & 1), c ^ (k & 1))
            copies.append(_remote(gath_ref.at[me], gath_ref.at[me], send_sems, recv_sems, k - 1, to))
        for cp in copies:
            cp.start()
        for k in range(1, N_DEV):
            frm = 4 * (x ^ (k >> 2)) + 2 * (y ^ ((k >> 1) & 1)) + (c ^ (k & 1))
            _remote(gath_ref.at[frm], gath_ref.at[frm], send_sems, recv_sems, k - 1, (x, y, c)).wait_recv()
        for cp in copies:
            cp.wait_send()
        total = gath_ref[0]
        for d in range(1, N_DEV):
            total = total + gath_ref[d]
        out_ref[...] = total

    vm = pl.BlockSpec(memory_space=pltpu.VMEM)
    return _pallas_call(
        body, name=name, in_specs=[vm], out_specs=vm, out_shape=jax.ShapeDtypeStruct((r, l), F32),
        scratch_shapes=[pltpu.VMEM((N_DEV, r, l), F32), pltpu.SemaphoreType.DMA((N_DEV - 1,)),
                        pltpu.SemaphoreType.DMA((N_DEV - 1,))],
    )(vec)


def _sum_blocks(blocks, out_dtype, name):
    m, w = blocks[0].shape
    tr = next(cand for cand in (512, 256, 128, 64, 32) if m % cand == 0)

    def body(*refs):
        total = refs[0][...].astype(F32)
        for ref in refs[1:-1]:
            total = total + ref[...].astype(F32)
        refs[-1][...] = total.astype(out_dtype)

    blk = pl.BlockSpec((tr, w), lambda i: (i, 0))
    return _pallas_call(
        body, name=name, grid=(m // tr,), in_specs=[blk] * len(blocks), out_specs=blk,
        out_shape=jax.ShapeDtypeStruct((m, w), out_dtype),
        compiler_params=_params(dimension_semantics=("arbitrary",)),
    )(*blocks)


FIRST_GROUPS = (('ffn1_w_gate', 'ffn1_w_up'), ('ffn1_w_down',))
REST_GROUPS = (('ffn2_w_gate', 'ffn2_w_up'), ('ffn2_w_down', 'w_o'), ('w_in',), ('mla_w_q_b',), ('mla_w_kv_b',))
N_FIRST = len(FIRST_GROUPS)


def _shard_rows(name):
    shape, axis = BIG[name]
    return shape[0] // N_SHARD if axis == 0 else shape[0]


def _group_row_offsets(group):
    return [int(v) for v in np.cumsum([0] + [_shard_rows(n) for n in group])]


def _rope_tables(s):
    pos = jnp.arange(s, dtype=F32)
    inv = 1.0 / (ROPE_THETA ** (jnp.arange(0, MLA_ROPE, 2, dtype=F32) / MLA_ROPE))
    ang = pos[:, None] * inv[None, :]
    return jnp.tile(jnp.cos(ang), (1, SWA_HEADS)), jnp.tile(jnp.sin(ang), (1, SWA_HEADS))


_MIXER_GAINS = ('mix_norm', 'mla_q_a_norm', 'mla_kv_a_norm', 'mla_q_norm', 'mla_k_norm', 'swa_q_norm', 'swa_k_norm')


def _local_step(x, target, small, ex):
    s = x.shape[0]
    cos, sin = _rope_tables(s)
    row = lambda name, l: small[name][l][None, :]
    saved, bigs = [], []
    for l in range(DEPTH):
        big = ex.first_weights(l)
        sv = {'x0': x}
        x, sv['g1'], sv['u1'], got = _ffn_fwd(x, row('ffn1_norm', l), big['ffn1_w_gate'], big['ffn1_w_up'], big['ffn1_w_down'],
                                             f"ffn1_fwd_{l}", ex.gather_behind_ffn1(l))
        big.update(ex.rest_weights(l, got))
        bigs.append(big)
        sv['x1'] = x
        gains = [row(n, l) for n in _MIXER_GAINS]
        mixer_w = (_take_cols(big['w_in'], _IN_ORDER), _take_cols(big['mla_w_q_b'], _QB_ORDER),
                   _take_cols(big['mla_w_kv_b'], _KVB_ORDER))
        q_a, k_a, v_a, q_b, k_b, v_b = _pre_fwd(x, gains, *mixer_w, cos, sin, f"pre_fwd_{l}")
        o_a, lse, got = _mla_fwd(q_a, k_a, v_a, f"mla_fwd_{l}", ex.gather_behind_mla(l))
        ex.gathered_behind_mla(l, got)
        kpad = jnp.pad(k_b, ((0, 0), (BLOCK, 0), (0, 0)))
        vpad = jnp.pad(v_b, ((0, 0), (BLOCK, 0), (0, 0)))
        sinks = jnp.broadcast_to(small['swa_sinks'][l].reshape(SWA_KV_HEADS, SWA_GROUP, 1), (SWA_KV_HEADS, SWA_GROUP, 128))
        o_b, lse_b = _swa_fwd(q_b, kpad, vpad, sinks, f"swa_fwd_{l}")
        sv.update(gains=gains, mixer_w=mixer_w, q_a=q_a, k_a=k_a, v_a=v_a, q_b=q_b, kpad=kpad, vpad=vpad, sinks=sinks,
                  o_a=o_a, lse=lse, o_b=o_b, lse_b=lse_b)
        x = _post_fwd(x, o_a, o_b, row('mla_out_norm', l), row('swa_out_norm', l), big['w_o'], f"post_fwd_{l}")
        sv['x2'] = x
        x, sv['g2'], sv['u2'], _ = _ffn_fwd(x, row('ffn2_norm', l), big['ffn2_w_gate'], big['ffn2_w_up'], big['ffn2_w_down'],
                                           f"ffn2_fwd_{l}")
        saved.append(sv)

    dx, loss = _loss_head(x, target, "loss_head")

    gs = {n: [None] * DEPTH for n in SMALL_NAMES}
    t = _attn_tile(s)
    for l in reversed(range(DEPTH)):
        sv, big = saved[l], bigs[l]

        def ffn_back(tag, xin, dy, gate, up, scatter=None):
            (dxi, dgain, nb, act, dgate, dup), got = _ffn_bwd(xin, dy, gate, up, row(tag + '_norm', l), big[tag + '_w_gate'],
                                                             big[tag + '_w_up'], big[tag + '_w_down'], f"{tag}_bwd_{l}", scatter)
            gs[tag + '_norm'][l] = dgain[0]
            grads = {tag + '_w_gate': _matmul_tn(nb, dgate, 1.0, f"{tag}_dwg_{l}", True),
                     tag + '_w_up': _matmul_tn(nb, dup, 1.0, f"{tag}_dwu_{l}", True),
                     tag + '_w_down': _matmul_tn(act, dy, 0.5, f"{tag}_dwd_{l}", False).reshape(N_SHARD, D_FF // N_SHARD, D_MODEL)}
            return dxi, grads, got

        dx, rest_grads, _ = ffn_back('ffn2', sv['x2'], dx, sv['g2'], sv['u2'])
        do_a, do_b, dga, dgb, dwo = _post_bwd(dx, sv['o_a'], sv['o_b'], row('mla_out_norm', l), row('swa_out_norm', l),
                                              big['w_o'], f"post_bwd_{l}")
        gs['mla_out_norm'][l], gs['swa_out_norm'][l] = dga[0], dgb[0]
        rest_grads['w_o'] = dwo.reshape(N_SHARD, MIX_WIDTH // N_SHARD, D_MODEL)
        delta = _mla_delta(sv['o_a'], do_a, f"mla_delta_{l}")
        dq_a, dk_a, dv_a, got = _mla_bwd(sv['q_a'], sv['k_a'], sv['v_a'], do_a, sv['lse'].reshape(MLA_HEADS, s // t, t),
                                         delta, f"mla_bwd_{l}", ex.scatter_behind_mla(l))
        ex.scattered_behind_mla(l, got)
        dq_b, dkpad, dvpad, dsink = _swa_bwd(sv['q_b'], sv['kpad'], sv['vpad'], sv['sinks'], sv['o_b'], sv['lse_b'], do_b,
                                             f"swa_bwd_{l}")
        gs['swa_sinks'][l] = dsink[:, :SWA_GROUP, 0].reshape(SWA_HEADS)
        cts = [dq_a, dk_a, dv_a, dq_b, dkpad[:, BLOCK:], dvpad[:, BLOCK:]]
        outs = _pre_bwd(sv['x1'], dx, cts, sv['gains'], *sv['mixer_w'], cos, sin, f"pre_bwd_{l}")
        dx = outs[0]
        for n, val in zip(_MIXER_GAINS, outs[1:8]):
            gs[n][l] = val[0]
        rest_grads['w_in'], rest_grads['mla_w_q_b'], rest_grads['mla_w_kv_b'] = outs[8:11]
        ex.grads_ready(l, 'rest', rest_grads)
        dx, first_grads, got = ffn_back('ffn1', sv['x0'], dx, sv['g1'], sv['u1'], ex.scatter_behind_ffn1(l))
        ex.scattered_behind_ffn1(l, got)
        ex.grads_ready(l, 'first', first_grads)
    return loss, dx, gs


class _Exchange:
    def __init__(self, weights, c, chip):
        halves_of = lambda a: a.reshape(a.shape[:-2] + (2, a.shape[-2] // 2, a.shape[-1]))
        self.halves_of, self.c, self.chip = halves_of, c, chip
        self.mine = [[halves_of(jnp.concatenate([weights[n][l].astype(BF16) for n in group], axis=0))
                      for group in FIRST_GROUPS + REST_GROUPS] for l in range(DEPTH)]
        self.ahead, self.begun, self.received = {}, {}, {}

    def _assembled(self, l, which, gathered):
        groups, base = (FIRST_GROUPS, 0) if which == 'first' else (REST_GROUPS, N_FIRST)
        big = {}
        for gi, group in enumerate(groups):
            offs = _group_row_offsets(group)
            _, rh, w = self.mine[l][base + gi].shape
            col_sharded = BIG[group[0]][1] == 1
            full = _assemble(gathered[gi].reshape(N_SHARD, 1, 2 * rh, w), self.mine[l][base + gi].reshape(1, 2 * rh, w),
                             f"assemble_{which}{gi}_{l}", col_sharded)
            for i, n in enumerate(group):
                rows = offs[i + 1] - offs[i]
                if len(group) > 1 and offs[i] % rows == 0:
                    big[n] = ((full, (None, rows, N_SHARD * w), (0, offs[i] // rows, 0)) if col_sharded else
                              (full, (None, N_SHARD, rows, w), (0, 0, offs[i] // rows, 0)))
                elif col_sharded:
                    big[n] = full[0, offs[i]:offs[i + 1]]
                else:
                    big[n] = full[0, :, offs[i]:offs[i + 1]].reshape(BIG[n][0])
        return big

    def first_weights(self, l):
        got = self.ahead[l][:N_FIRST] if l in self.ahead else _all_gather_halves(self.mine[l][:N_FIRST], f"gather_first_{l}")
        return self._assembled(l, 'first', got)

    def gather_behind_ffn1(self, l):
        return None if l in self.ahead else self.mine[l][N_FIRST:]

    def rest_weights(self, l, got):
        return self._assembled(l, 'rest', self.ahead[l][N_FIRST:] if l in self.ahead else got)

    def gather_behind_mla(self, l):
        return self.mine[l + 1] if l + 1 < DEPTH else None

    def gathered_behind_mla(self, l, got):
        if got:
            self.ahead[l + 1] = got

    def grads_ready(self, l, which, grads):
        groups = FIRST_GROUPS if which == 'first' else REST_GROUPS
        parts = [self.halves_of(jnp.concatenate([grads[n] for n in group], axis=1)) for group in groups]
        from_sibling = _sibling_exchange(parts, f"swap_{which}_{l}", True)
        chip_sums = []
        for gi, (p, got) in enumerate(zip(parts, from_sibling)):
            kept = lax.dynamic_index_in_dim(p, self.c, axis=1, keepdims=False)
            rows = N_SHARD * p.shape[2]
            pair = _sum_blocks([kept.reshape(rows, -1), got.reshape(rows, -1)], BF16, f"sum_pair_{which}{gi}_{l}")
            chip_sums.append(pair.reshape(got.shape))
        self.begun[(l, which)] = chip_sums

    def scatter_behind_mla(self, l):
        return self.begun[(l + 1, 'first')] + self.begun[(l + 1, 'rest')] if l + 1 < DEPTH else None

    def scattered_behind_mla(self, l, got):
        if got:
            self.received[(l + 1, 'first')], self.received[(l + 1, 'rest')] = got[:N_FIRST], got[N_FIRST:]

    def scatter_behind_ffn1(self, l):
        return self.begun[(l, 'rest')] if l == 0 else None

    def scattered_behind_ffn1(self, l, got):
        if got:
            self.received[(l, 'rest')] = got

    def reduced(self):
        keys = sorted(self.begun)
        for key in keys:
            if key not in self.received:
                self.received[key] = _scatter_to_chips(self.begun[key], f"scatter_{key[1]}_{key[0]}")
        halves = []
        for l, which in keys:
            for gi, (cs, got) in enumerate(zip(self.begun[(l, which)], self.received[(l, which)])):
                own = lax.dynamic_index_in_dim(cs, self.chip, axis=0, keepdims=False)
                halves.append(_sum_blocks([own, got[0], got[1], got[2]], F32, f"sum_chips_{which}{gi}_{l}"))
        others = _sibling_exchange(halves, "share_halves", False)
        per_layer, at = {}, 0
        for l, which in keys:
            for group in (FIRST_GROUPS if which == 'first' else REST_GROUPS):
                mine_h, other_h = halves[at], others[at]
                at += 1
                full = jnp.where(self.c == 0, jnp.concatenate([mine_h, other_h]), jnp.concatenate([other_h, mine_h]))
                offs = _group_row_offsets(group)
                for i, n in enumerate(group):
                    per_layer[(n, l)] = full[offs[i]:offs[i + 1]]
        return {n: jnp.stack([per_layer[(n, l)] for l in range(DEPTH)]) for n in BIG_NAMES}


def kernel(x, ffn1_norm, ffn1_w_gate, ffn1_w_up, ffn1_w_down, mix_norm, w_in, mla_q_a_norm, mla_w_q_b, mla_kv_a_norm, mla_w_kv_b, mla_q_norm, mla_k_norm, swa_q_norm, swa_k_norm, swa_sinks, mla_out_norm, swa_out_norm, w_o, ffn2_norm, ffn2_w_gate, ffn2_w_up, ffn2_w_down, loss_target, m_ffn1_norm, m_ffn1_w_gate, m_ffn1_w_up, m_ffn1_w_down, m_mix_norm, m_w_in, m_mla_q_a_norm, m_mla_w_q_b, m_mla_kv_a_norm, m_mla_w_kv_b, m_mla_q_norm, m_mla_k_norm, m_swa_q_norm, m_swa_k_norm, m_swa_sinks, m_mla_out_norm, m_swa_out_norm, m_w_o, m_ffn2_norm, m_ffn2_w_gate, m_ffn2_w_up, m_ffn2_w_down, v_ffn1_norm, v_ffn1_w_gate, v_ffn1_w_up, v_ffn1_w_down, v_mix_norm, v_w_in, v_mla_q_a_norm, v_mla_w_q_b, v_mla_kv_a_norm, v_mla_w_kv_b, v_mla_q_norm, v_mla_k_norm, v_swa_q_norm, v_swa_k_norm, v_swa_sinks, v_mla_out_norm, v_swa_out_norm, v_w_o, v_ffn2_norm, v_ffn2_w_gate, v_ffn2_w_up, v_ffn2_w_down):
    args = dict(locals())
    weights = {n: args[n] for n in WEIGHT_NAMES}
    mom_m = {n: args["m_" + n] for n in WEIGHT_NAMES}
    mom_v = {n: args["v_" + n] for n in WEIGHT_NAMES}
    ex = _Exchange(weights, lax.axis_index("c"), 2 * lax.axis_index("x") + lax.axis_index("y"))
    loss, dx, gs = _local_step(x[0], loss_target[0], {n: weights[n] for n in SMALL_NAMES}, ex)

    small_flat = jnp.concatenate([jnp.stack(gs[n]).reshape(-1) for n in SMALL_NAMES] + [loss[0, :1]])
    n_small = small_flat.shape[0]
    lanes = -(-n_small // (8 * 128)) * 128
    small_sum = _all_reduce_small(jnp.pad(small_flat, (0, 8 * lanes - n_small)).reshape(8, lanes), "reduce_small").reshape(-1)
    grads = ex.reduced()
    off = 0
    for n in SMALL_NAMES:
        cnt = int(np.prod(weights[n].shape))
        grads[n] = small_sum[off:off + cnt].reshape(weights[n].shape)
        off += cnt
    loss_out = small_sum[off]

    deltas, new_m, new_v = {}, {}, {}
    for n in WEIGHT_NAMES:
        shp = weights[n].shape
        two_d = (DEPTH, shp[-1]) if len(shp) == 2 else (shp[0] * shp[1], shp[2])
        d, nm, nv = _adamw(weights[n].reshape(two_d), grads[n].reshape(two_d), mom_m[n].reshape(two_d),
                           mom_v[n].reshape(two_d), f"adamw_{n}")
        deltas[n], new_m[n], new_v[n] = d.reshape(shp), nm.reshape(shp), nv.reshape(shp)

    return (loss_out, dx[None], *[grads[n] for n in WEIGHT_NAMES], *[deltas[n] for n in WEIGHT_NAMES],
            *[new_m[n] for n in WEIGHT_NAMES], *[new_v[n] for n in WEIGHT_NAMES])
```

```python
import functools

import numpy as np
import jax
import jax.numpy as jnp
from jax import lax
from jax.experimental import pallas as pl
from jax.experimental.pallas import tpu as pltpu

F32 = jnp.float32
BF16 = jnp.bfloat16

D_MODEL = 1024
DEPTH = 2
EPS = 1e-6
ROPE_THETA = 10000.0
BLOCK = 128
MLA_HEADS = 4
MLA_Q_RANK = 256
MLA_KV_RANK = 128
MLA_NOPE = 128
MLA_ROPE = 64
MLA_V = 128
MLA_QK = MLA_NOPE + MLA_ROPE
MLA_WIDTH = MLA_HEADS * MLA_V
SWA_HEADS = 8
SWA_KV_HEADS = 2
SWA_GROUP = SWA_HEADS // SWA_KV_HEADS
SWA_HEAD_DIM = 64
SWA_WIDTH = SWA_HEADS * SWA_HEAD_DIM
MIX_WIDTH = MLA_WIDTH + SWA_WIDTH
IN_SPLITS = (MLA_Q_RANK, MLA_KV_RANK, MLA_ROPE, SWA_WIDTH, SWA_KV_HEADS * SWA_HEAD_DIM, SWA_KV_HEADS * SWA_HEAD_DIM)
IN_COLS = sum(IN_SPLITS)
IN_OFFS = tuple(int(v) for v in np.cumsum((0,) + IN_SPLITS))
D_FF = 2816
MLA_SCALE = MLA_QK ** -0.5
LOG2E = 1.4426950408889634
LN2 = 0.6931471805599453
MLA_QSCALE = MLA_SCALE * LOG2E
SWA_SCALE = SWA_HEAD_DIM ** -0.5
NEG = -1e30

ADAM_LR = 0.001
ADAM_B1 = 0.9
ADAM_B2 = 0.999
ADAM_EPS = 1e-08
ADAM_WD = 0.01
ADAM_STEP = 10

N_SHARD = 4
N_DEV = 8
VMEM_LIMIT = 56 * 1024 * 1024
MESH = pl.DeviceIdType.MESH

WEIGHT_NAMES = ['ffn1_norm', 'ffn1_w_gate', 'ffn1_w_up', 'ffn1_w_down', 'mix_norm', 'w_in', 'mla_q_a_norm', 'mla_w_q_b',
                'mla_kv_a_norm', 'mla_w_kv_b', 'mla_q_norm', 'mla_k_norm', 'swa_q_norm', 'swa_k_norm', 'swa_sinks',
                'mla_out_norm', 'swa_out_norm', 'w_o', 'ffn2_norm', 'ffn2_w_gate', 'ffn2_w_up', 'ffn2_w_down']
BIG = {'ffn1_w_gate': ((D_MODEL, D_FF), 1), 'ffn1_w_up': ((D_MODEL, D_FF), 1), 'ffn1_w_down': ((D_FF, D_MODEL), 0),
       'w_in': ((D_MODEL, IN_COLS), 1), 'mla_w_q_b': ((MLA_Q_RANK, MLA_HEADS * MLA_QK), 1),
       'mla_w_kv_b': ((MLA_KV_RANK, MLA_HEADS * (MLA_NOPE + MLA_V)), 1), 'w_o': ((MIX_WIDTH, D_MODEL), 0),
       'ffn2_w_gate': ((D_MODEL, D_FF), 1), 'ffn2_w_up': ((D_MODEL, D_FF), 1), 'ffn2_w_down': ((D_FF, D_MODEL), 0)}
BIG_NAMES = [n for n in WEIGHT_NAMES if n in BIG]
SMALL_NAMES = [n for n in WEIGHT_NAMES if n not in BIG]

_pallas_call = pl.pallas_call


def _params(**kw):
    return pltpu.CompilerParams(vmem_limit_bytes=VMEM_LIMIT, **kw)


def _full(shape):
    n = len(shape)
    return pl.BlockSpec(shape, lambda *_: (0,) * n)


def _resident(shape):
    n = len(shape)
    return pl.BlockSpec(shape, lambda *_: (0,) * n, pipeline_mode=pl.Buffered(1))


@jax.custom_vjp
def _mm(a, w):
    return jnp.dot(a.astype(BF16), w, preferred_element_type=F32)


def _mm_fwd(a, w):
    return _mm(a, w), w


def _mm_bwd(w, dy):
    return lax.dot_general(dy.astype(BF16), w, (((1,), (1,)), ((), ())), preferred_element_type=F32), None


_mm.defvjp(_mm_fwd, _mm_bwd)


def _dot_nt(a, b):
    return lax.dot_general(a, b, (((1,), (1,)), ((), ())), preferred_element_type=F32)


def _dot_tn(a, b):
    return lax.dot_general(a, b, (((0,), (0,)), ((), ())), preferred_element_type=F32)


def _rms(t, g):
    return t * lax.rsqrt(jnp.mean(t * t, axis=-1, keepdims=True) + EPS) * g


def _sigmoid(z):
    return 1.0 / (1.0 + jnp.exp(-z))


def _row_tile(s, want):
    return min(want, s)


FF_CHUNK = 1408


def _weight_operand(w):
    if isinstance(w, tuple):
        arr, block, index = w
        return arr, pl.BlockSpec(block, lambda *_: index, pipeline_mode=pl.Buffered(1))
    return w, _resident(w.shape)


def _weight_rows(ref, start, n):
    if len(ref.shape) == 2:
        return ref[start:start + n, :]
    per = ref.shape[1]
    return ref[start // per:(start + n) // per].reshape(n, ref.shape[2])


def _ffn_fwd(x, g, wg, wu, wd, name, gather=None):
    s = x.shape[0]
    tm = _row_tile(s, 256)
    steps = s // tm
    ng = len(gather) if gather else 0
    (wg, wg_spec), (wu, wu_spec), (wd, wd_spec) = _weight_operand(wg), _weight_operand(wu), _weight_operand(wd)

    def body(x_ref, g_ref, wg_ref, wu_ref, wd_ref, *rest):
        g_ins, (y_ref, gate_ref, up_ref), g_outs, sems = rest[:ng], rest[ng:ng + 3], rest[ng + 3:2 * ng + 3], rest[2 * ng + 3:]
        if ng:
            _run_stages_at(_gather_stages(g_ins, g_outs, *sems), [(0,), (steps * 3 // 8,), (steps * 11 // 16,), (steps - 1,)])
        xv = x_ref[...]
        nb = _rms(xv, g_ref[...]).astype(BF16)
        acc = xv
        for c in range(0, D_FF, FF_CHUNK):
            gate = jnp.dot(nb, wg_ref[:, c:c + FF_CHUNK], preferred_element_type=F32)
            up = jnp.dot(nb, wu_ref[:, c:c + FF_CHUNK], preferred_element_type=F32)
            gate_ref[:, c:c + FF_CHUNK] = gate.astype(BF16)
            up_ref[:, c:c + FF_CHUNK] = up.astype(BF16)
            act = (gate * _sigmoid(gate) * up).astype(BF16)
            acc = acc + 0.5 * jnp.dot(act, _weight_rows(wd_ref, c, FF_CHUNK), preferred_element_type=F32)
        y_ref[...] = acc

    outs = _pallas_call(
        body, name=name, grid=(steps,),
        in_specs=[pl.BlockSpec((tm, D_MODEL), lambda i: (i, 0)), _full((1, D_MODEL)), wg_spec, wu_spec, wd_spec] + [_HBM] * ng,
        out_specs=[pl.BlockSpec((tm, D_MODEL), lambda i: (i, 0)), pl.BlockSpec((tm, D_FF), lambda i: (i, 0)),
                   pl.BlockSpec((tm, D_FF), lambda i: (i, 0))] + [_HBM] * ng,
        out_shape=[jax.ShapeDtypeStruct((s, D_MODEL), F32), jax.ShapeDtypeStruct((s, D_FF), BF16),
                   jax.ShapeDtypeStruct((s, D_FF), BF16)] + (_gather_out_shapes(gather) if ng else []),
        scratch_shapes=_exchange_scratch(8, ng) if ng else [],
        compiler_params=_params(dimension_semantics=("arbitrary",)),
    )(x, g, wg, wu, wd, *(gather or []))
    return outs[0], outs[1], outs[2], outs[3:]


def _ffn_bwd(x, dy, gate, up, g, wg, wu, wd, name, scatter=None):
    s = x.shape[0]
    tm = _row_tile(s, 256)
    steps = s // tm
    ng = len(scatter) if scatter else 0
    (wg, wg_spec), (wu, wu_spec), (wd, wd_spec) = _weight_operand(wg), _weight_operand(wu), _weight_operand(wd)

    def body(x_ref, dy_ref, gate_ref, up_ref, g_ref, wg_ref, wu_ref, wd_ref, *rest):
        c_ins, (dx_ref, dgain_ref, n_ref, act_ref, dgate_ref, dup_ref) = rest[:ng], rest[ng:ng + 6]
        c_outs, sems = rest[ng + 6:2 * ng + 6], rest[2 * ng + 6:]
        if ng:
            _run_stages_at(_scatter_stages(c_ins, c_outs, *sems), [(0,), (steps - 1,)])
        i = pl.program_id(0)
        xv = x_ref[...]
        dyv = dy_ref[...]
        gv = g_ref[...]
        r = lax.rsqrt(jnp.mean(xv * xv, axis=-1, keepdims=True) + EPS)
        xh = xv * r
        n_ref[...] = (xh * gv).astype(BF16)
        dyh = (0.5 * dyv).astype(BF16)
        dn = jnp.zeros_like(xv)
        for c in range(0, D_FF, FF_CHUNK):
            dact = _dot_nt(dyh, _weight_rows(wd_ref, c, FF_CHUNK))
            gt = gate_ref[:, c:c + FF_CHUNK].astype(F32)
            u = up_ref[:, c:c + FF_CHUNK].astype(F32)
            sg = _sigmoid(gt)
            sl = gt * sg
            act_ref[:, c:c + FF_CHUNK] = (sl * u).astype(BF16)
            dup = (dact * sl).astype(BF16)
            dgate = (dact * u * (sg * (1.0 + gt * (1.0 - sg)))).astype(BF16)
            dup_ref[:, c:c + FF_CHUNK] = dup
            dgate_ref[:, c:c + FF_CHUNK] = dgate
            dn = dn + _dot_nt(dgate, wg_ref[:, c:c + FF_CHUNK]) + _dot_nt(dup, wu_ref[:, c:c + FF_CHUNK])
        part = jnp.sum(dn * xh, axis=0, keepdims=True)

        @pl.when(i == 0)
        def _():
            dgain_ref[...] = part

        @pl.when(i > 0)
        def _():
            dgain_ref[...] += part

        dxh = dn * gv
        dx_ref[...] = dyv + r * (dxh - xh * jnp.mean(dxh * xh, axis=-1, keepdims=True))

    row = lambda w: pl.BlockSpec((tm, w), lambda i: (i, 0))
    outs = _pallas_call(
        body, name=name, grid=(steps,),
        in_specs=[row(D_MODEL), row(D_MODEL), row(D_FF), row(D_FF), _full((1, D_MODEL)), wg_spec, wu_spec, wd_spec]
        + [_HBM] * ng,
        out_specs=[row(D_MODEL), _full((1, D_MODEL)), row(D_MODEL), row(D_FF), row(D_FF), row(D_FF)] + [_HBM] * ng,
        out_shape=[jax.ShapeDtypeStruct((s, D_MODEL), F32), jax.ShapeDtypeStruct((1, D_MODEL), F32),
                   jax.ShapeDtypeStruct((s, D_MODEL), BF16), jax.ShapeDtypeStruct((s, D_FF), BF16),
                   jax.ShapeDtypeStruct((s, D_FF), BF16), jax.ShapeDtypeStruct((s, D_FF), BF16)]
        + (_scatter_out_shapes(scatter) if ng else []),
        scratch_shapes=_exchange_scratch(3, ng) if ng else [],
        compiler_params=_params(dimension_semantics=("arbitrary",)),
    )(x, dy, gate, up, g, wg, wu, wd, *(scatter or []))
    return outs[:6], outs[6:]


def _store_col_shards(o_ref, acc, first_shard, n_here, width):
    for q in range(n_here):
        o_ref[q] = acc[:, (first_shard + q) * width:(first_shard + q + 1) * width].astype(BF16)


def _matmul_tn(a, b, scale, name, col_shards):
    t, m = a.shape
    n = b.shape[1]
    tk = _row_tile(t, 2048)
    tn = n // 2
    per = n // N_SHARD
    nk = t // tk

    def body(a_ref, b_ref, o_ref, acc_ref):
        k = pl.program_id(1)
        bv = b_ref[...]
        if scale != 1.0:
            bv = bv.astype(F32) * scale
        part = _dot_tn(a_ref[...].astype(BF16), bv.astype(BF16))

        @pl.when(k == 0)
        def _():
            acc_ref[...] = part

        @pl.when(k > 0)
        def _():
            acc_ref[...] += part

        @pl.when(k == nk - 1)
        def _():
            if col_shards:
                _store_col_shards(o_ref, acc_ref[...], 0, tn // per, per)
            else:
                o_ref[...] = acc_ref[...].astype(BF16)

    if col_shards:
        out_spec = pl.BlockSpec((tn // per, m, per), lambda j, k: (j, 0, 0))
        out_shape = jax.ShapeDtypeStruct((N_SHARD, m, per), BF16)
    else:
        out_spec = pl.BlockSpec((m, tn), lambda j, k: (0, j))
        out_shape = jax.ShapeDtypeStruct((m, n), BF16)
    return _pallas_call(
        body, name=name, grid=(n // tn, nk),
        in_specs=[pl.BlockSpec((tk, m), lambda j, k: (k, 0)), pl.BlockSpec((tk, tn), lambda j, k: (k, j))],
        out_specs=out_spec, out_shape=out_shape, scratch_shapes=[pltpu.VMEM((m, tn), F32)],
        compiler_params=_params(dimension_semantics=("arbitrary", "arbitrary")),
    )(a, b)


_HALF = SWA_HEAD_DIM // 2
_IN_ORDER = (list(range(0, IN_OFFS[2]))
             + [IN_OFFS[3] + SWA_HEAD_DIM * h + i for h in range(SWA_HEADS) for i in range(_HALF)]
             + [IN_OFFS[3] + SWA_HEAD_DIM * h + _HALF + i for h in range(SWA_HEADS) for i in range(_HALF)]
             + list(range(IN_OFFS[5], IN_OFFS[6]))
             + [IN_OFFS[4] + SWA_HEAD_DIM * j + i for j in range(SWA_KV_HEADS) for i in range(_HALF)]
             + [IN_OFFS[4] + SWA_HEAD_DIM * j + _HALF + i for j in range(SWA_KV_HEADS) for i in range(_HALF)]
             + list(range(IN_OFFS[2], IN_OFFS[3])))
_QB_ORDER = ([MLA_QK * h + i for h in range(MLA_HEADS) for i in range(MLA_NOPE)]
             + [MLA_QK * h + MLA_NOPE + i for h in range(MLA_HEADS) for i in range(_HALF)]
             + [MLA_QK * h + MLA_NOPE + _HALF + i for h in range(MLA_HEADS) for i in range(_HALF)])
_KVB_ORDER = ([(MLA_NOPE + MLA_V) * h + i for h in range(MLA_HEADS) for i in range(MLA_NOPE)]
              + [(MLA_NOPE + MLA_V) * h + MLA_NOPE + i for h in range(MLA_HEADS) for i in range(MLA_V)])
MIXER_ORDERS = {'w_in': _IN_ORDER, 'mla_w_q_b': _QB_ORDER, 'mla_w_kv_b': _KVB_ORDER}
_P_CQ, _P_CKV, _P_QA, _P_QB, _P_VS, _P_KA, _P_KB, _P_PE = (int(v) for v in np.cumsum(
    (0, MLA_Q_RANK, MLA_KV_RANK, SWA_WIDTH // 2, SWA_WIDTH // 2, IN_SPLITS[5], IN_SPLITS[4] // 2, IN_SPLITS[4] // 2)))


def _runs(order):
    out, start = [], 0
    for i in range(1, len(order) + 1):
        if i == len(order) or order[i] != order[i - 1] + 1:
            out.append((order[start], i - start))
            start = i
    return out


def _inverse(order):
    inv = [0] * len(order)
    for new, old in enumerate(order):
        inv[old] = new
    return inv


def _take_cols(a, order):
    return jnp.concatenate([a[..., st:st + w] for st, w in _runs(order)], axis=-1)


def _segment_matrix(n, seg):
    return (lax.broadcasted_iota(jnp.int32, (n, n), 0) // seg == lax.broadcasted_iota(jnp.int32, (n, n), 1) // seg).astype(BF16)


@jax.custom_vjp
def _cmm(t, b, bt):
    hi = t.astype(BF16)
    lo = (t - hi.astype(F32)).astype(BF16)
    return jnp.dot(hi, b, preferred_element_type=F32) + jnp.dot(lo, b, preferred_element_type=F32)


def _cmm_fwd(t, b, bt):
    return _cmm(t, b, bt), (b, bt)


def _cmm_bwd(res, dy):
    b, bt = res
    return _cmm(dy, bt, b), None, None


_cmm.defvjp(_cmm_fwd, _cmm_bwd)


def _segsum(t, b):
    return _cmm(t, b, b)


def _rowsum(t):
    n = t.shape[-1]
    return _cmm(t, jnp.ones((n, 128), BF16), jnp.ones((128, n), BF16))


def _by_head(vals, width):
    lane = lax.broadcasted_iota(jnp.int32, (vals[0].shape[0], len(vals) * width), 1)
    out = vals[-1]
    for hd in range(len(vals) - 2, -1, -1):
        out = jnp.where(lane < (hd + 1) * width, vals[hd], out)
    return out


def _rope2(a, b, cos, sin):
    return a * cos - b * sin, b * cos + a * sin


def _pre_math(x, gm, gqa, gkva, gq, gk, gsq, gsk, taps, win, wqb, wkvb, cos, sin):
    h = _rms(x, gm)
    proj = _mm(h, win)
    if taps is not None:
        proj = proj + taps[0]
    cqn = _rms(proj[:, _P_CQ:_P_CKV], gqa)
    qa_all = _mm(cqn, wqb)
    ckvn = _rms(proj[:, _P_CKV:_P_QA], gkva)
    kv_all = _mm(ckvn, wkvb)
    if taps is not None:
        qa_all = qa_all + taps[1]
        kv_all = kv_all + taps[2]
    nh, hw = MLA_HEADS, MLA_HEADS * _HALF
    seg_mla = _segment_matrix(hw, _HALF)
    tile = lambda g, n: jnp.concatenate([g] * n, axis=-1)
    c4, s4 = cos[:, :hw], sin[:, :hw]

    def mla_heads(nope, r1, r2, gain):
        rr = r1 * r1 + r2 * r2
        lane_head = lax.broadcasted_iota(jnp.int32, (hw, nh * MLA_NOPE), 0) // _HALF
        spread = (lane_head == lax.broadcasted_iota(jnp.int32, (hw, nh * MLA_NOPE), 1) // MLA_NOPE).astype(BF16)
        rope_on_nope = _cmm(rr, spread, spread.T)
        ss_nope = [_rowsum(jnp.square(nope[:, hd * MLA_NOPE:(hd + 1) * MLA_NOPE])) for hd in range(nh)]
        rinv = [lax.rsqrt((ss_nope[hd] + rope_on_nope[:, hd * MLA_NOPE:(hd + 1) * MLA_NOPE]) * (1.0 / MLA_QK) + EPS)
                for hd in range(nh)]
        rl = lax.rsqrt((_segsum(rr, seg_mla) + _by_head(ss_nope, _HALF)) * (1.0 / MLA_QK) + EPS)
        o1, o2 = _rope2(r1 * rl * tile(gain[:, MLA_NOPE:MLA_NOPE + _HALF], nh), r2 * rl * tile(gain[:, MLA_NOPE + _HALF:], nh), c4, s4)
        return [jnp.concatenate([nope[:, hd * MLA_NOPE:(hd + 1) * MLA_NOPE] * rinv[hd] * gain[:, :MLA_NOPE],
                                 o1[:, hd * _HALF:(hd + 1) * _HALF], o2[:, hd * _HALF:(hd + 1) * _HALF]], axis=-1)
                for hd in range(nh)]

    q_a = mla_heads(qa_all[:, :nh * MLA_NOPE], qa_all[:, nh * MLA_NOPE:nh * MLA_NOPE + hw], qa_all[:, nh * MLA_NOPE + hw:], gq)
    pe1, pe2 = proj[:, _P_PE:_P_PE + _HALF], proj[:, _P_PE + _HALF:_P_PE + 2 * _HALF]
    k_a = mla_heads(kv_all[:, :nh * MLA_NOPE], tile(pe1, nh), tile(pe2, nh), gk)
    v_a = [kv_all[:, nh * MLA_NOPE + hd * MLA_V:nh * MLA_NOPE + (hd + 1) * MLA_V] for hd in range(nh)]

    def swa_heads(a, b, gain, n):
        w = n * _HALF
        r = lax.rsqrt(_segsum(a * a + b * b, _segment_matrix(w, _HALF)) * (1.0 / SWA_HEAD_DIM) + EPS)
        o1, o2 = _rope2(a * r * tile(gain[:, :_HALF], n), b * r * tile(gain[:, _HALF:], n), cos[:, :w], sin[:, :w])
        return [jnp.concatenate([o1[:, hd * _HALF:(hd + 1) * _HALF], o2[:, hd * _HALF:(hd + 1) * _HALF]], axis=-1) for hd in range(n)]

    q_b = swa_heads(proj[:, _P_QA:_P_QB], proj[:, _P_QB:_P_VS], gsq, SWA_HEADS)
    k_b = swa_heads(proj[:, _P_KA:_P_KB], proj[:, _P_KB:_P_PE], gsk, SWA_KV_HEADS)
    v_b = [proj[:, _P_VS + j * SWA_HEAD_DIM:_P_VS + (j + 1) * SWA_HEAD_DIM] for j in range(SWA_KV_HEADS)]
    return (q_a, k_a, v_a, q_b, k_b, v_b), (h, cqn, ckvn)


_PRE_GAIN_WIDTHS = (D_MODEL, MLA_Q_RANK, MLA_KV_RANK, MLA_QK, MLA_QK, SWA_HEAD_DIM, SWA_HEAD_DIM)
_PRE_HEADS = ((MLA_HEADS, MLA_QK), (MLA_HEADS, MLA_QK), (MLA_HEADS, MLA_V),
              (SWA_HEADS, SWA_HEAD_DIM), (SWA_KV_HEADS, SWA_HEAD_DIM), (SWA_KV_HEADS, SWA_HEAD_DIM))


def _pre_fwd(x, gains, win, wqb, wkvb, cos, sin, name):
    s = x.shape[0]
    tm = _row_tile(s, 512)

    def body(x_ref, *refs):
        g_refs, (win_ref, wqb_ref, wkvb_ref, cos_ref, sin_ref), out_refs = refs[:7], refs[7:12], refs[12:]
        outs, _ = _pre_math(x_ref[...], *[g[...] for g in g_refs], None, win_ref[...], wqb_ref[...], wkvb_ref[...],
                            cos_ref[...], sin_ref[...])
        for idx, (ref, heads) in enumerate(zip(out_refs, outs)):
            for hd, val in enumerate(heads):
                ref[hd] = (val * MLA_QSCALE if idx == 0 else val).astype(BF16)

    heads_spec = lambda nh, w: pl.BlockSpec((nh, tm, w), lambda i: (0, i, 0))
    return _pallas_call(
        body, name=name, grid=(s // tm,),
        in_specs=[pl.BlockSpec((tm, D_MODEL), lambda i: (i, 0))] + [_full((1, w)) for w in _PRE_GAIN_WIDTHS]
        + [_resident(win.shape), _resident(wqb.shape), _resident(wkvb.shape),
           pl.BlockSpec((tm, SWA_HEADS * _HALF), lambda i: (i, 0)), pl.BlockSpec((tm, SWA_HEADS * _HALF), lambda i: (i, 0))],
        out_specs=[heads_spec(nh, w) for nh, w in _PRE_HEADS],
        out_shape=[jax.ShapeDtypeStruct((nh, s, w), BF16) for nh, w in _PRE_HEADS],
        compiler_params=_params(dimension_semantics=("arbitrary",)),
    )(x, *gains, win, wqb, wkvb, cos, sin)


def _pre_bwd(x, dx_res, cts, gains, win, wqb, wkvb, cos, sin, name):
    s = x.shape[0]
    tm = _row_tile(s, 256)
    tap_widths = (IN_COLS, MLA_HEADS * MLA_QK, MLA_HEADS * (MLA_NOPE + MLA_V))

    def body(x_ref, dxr_ref, *refs):
        ct_refs, g_refs = refs[:6], refs[6:13]
        win_ref, wqb_ref, wkvb_ref, cos_ref, sin_ref = refs[13:18]
        dx_ref, dg_refs, dw_refs, acc_refs = refs[18], refs[19:26], refs[26:29], refs[29:32]
        i = pl.program_id(0)
        win_v, wqb_v, wkvb_v, cos_v, sin_v = win_ref[...], wqb_ref[...], wkvb_ref[...], cos_ref[...], sin_ref[...]

        def f(xv, gm, gqa, gkva, gq, gk, gsq, gsk, t0, t1, t2):
            return _pre_math(xv, gm, gqa, gkva, gq, gk, gsq, gsk, (t0, t1, t2), win_v, wqb_v, wkvb_v, cos_v, sin_v)

        taps = [jnp.zeros((tm, w), F32) for w in tap_widths]
        _, vjp, acts = jax.vjp(f, x_ref[...], *[g[...] for g in g_refs], *taps, has_aux=True)
        ct = tuple([ref[hd] for hd in range(nh)] for ref, (nh, _) in zip(ct_refs, _PRE_HEADS))
        grads = vjp(ct)
        dx_ref[...] = grads[0] + dxr_ref[...]
        dws = [_dot_tn(a.astype(BF16), t.astype(BF16)) for a, t in zip(acts, grads[8:11])]

        @pl.when(i == 0)
        def _():
            for ref, val in zip(dg_refs, grads[1:8]):
                ref[...] = val
            for ref, val in zip(acc_refs, dws):
                ref[...] = val

        @pl.when(i > 0)
        def _():
            for ref, val in zip(dg_refs, grads[1:8]):
                ref[...] += val
            for ref, val in zip(acc_refs, dws):
                ref[...] += val

        @pl.when(i == s // tm - 1)
        def _():
            for ref, acc, order in zip(dw_refs, acc_refs, (_IN_ORDER, _QB_ORDER, _KVB_ORDER)):
                _store_col_shards(ref, _take_cols(acc[...], _inverse(order)), 0, N_SHARD, acc.shape[1] // N_SHARD)

    heads_spec = lambda nh, w: pl.BlockSpec((nh, tm, w), lambda i: (0, i, 0))
    row = pl.BlockSpec((tm, D_MODEL), lambda i: (i, 0))
    half = pl.BlockSpec((tm, SWA_HEADS * _HALF), lambda i: (i, 0))
    shard_shapes = [(N_SHARD, w.shape[0], w.shape[1] // N_SHARD) for w in (win, wqb, wkvb)]
    return _pallas_call(
        body, name=name, grid=(s // tm,),
        in_specs=[row, row] + [heads_spec(nh, w) for nh, w in _PRE_HEADS] + [_full((1, w)) for w in _PRE_GAIN_WIDTHS]
        + [_resident(win.shape), _resident(wqb.shape), _resident(wkvb.shape), half, half],
        out_specs=[row] + [_full((1, w)) for w in _PRE_GAIN_WIDTHS] + [_full(shp) for shp in shard_shapes],
        out_shape=[jax.ShapeDtypeStruct((s, D_MODEL), F32)] + [jax.ShapeDtypeStruct((1, w), F32) for w in _PRE_GAIN_WIDTHS]
        + [jax.ShapeDtypeStruct(shp, BF16) for shp in shard_shapes],
        scratch_shapes=[pltpu.VMEM(w.shape, F32) for w in (win, wqb, wkvb)],
        compiler_params=_params(dimension_semantics=("arbitrary",)),
    )(x, dx_res, *cts, *gains, win, wqb, wkvb, cos, sin)


def _post_math(oa, ob, ga, gb, wo):
    mixed = jnp.concatenate([_rms(jnp.concatenate(oa, axis=-1), ga), _rms(jnp.concatenate(ob, axis=-1), gb)], axis=-1)
    return _mm(mixed, wo), mixed


def _post_fwd(x, oa, ob, ga, gb, wo, name):
    s = x.shape[0]
    tm = _row_tile(s, 512)

    def body(x_ref, oa_ref, ob_ref, ga_ref, gb_ref, wo_ref, y_ref):
        y, _ = _post_math([oa_ref[hd] for hd in range(MLA_HEADS)], [ob_ref[hd] for hd in range(SWA_HEADS)],
                          ga_ref[...], gb_ref[...], wo_ref[...])
        y_ref[...] = x_ref[...] + y

    row = pl.BlockSpec((tm, D_MODEL), lambda i: (i, 0))
    return _pallas_call(
        body, name=name, grid=(s // tm,),
        in_specs=[row, pl.BlockSpec((MLA_HEADS, tm, MLA_V), lambda i: (0, i, 0)),
                  pl.BlockSpec((SWA_HEADS, tm, SWA_HEAD_DIM), lambda i: (0, i, 0)),
                  _full((1, MLA_WIDTH)), _full((1, SWA_WIDTH)), _resident(wo.shape)],
        out_specs=row, out_shape=jax.ShapeDtypeStruct((s, D_MODEL), F32),
        compiler_params=_params(dimension_semantics=("arbitrary",)),
    )(x, oa, ob, ga, gb, wo)


def _post_bwd(dy, oa, ob, ga, gb, wo, name):
    s = dy.shape[0]
    tm = _row_tile(s, 512)

    def body(dy_ref, oa_ref, ob_ref, ga_ref, gb_ref, wo_ref, doa_ref, dob_ref, dga_ref, dgb_ref, dwo_ref, acc_ref):
        i = pl.program_id(0)
        wo_v = wo_ref[...]
        dyv = dy_ref[...]

        def f(oa_l, ob_l, ga_v, gb_v):
            return _post_math(oa_l, ob_l, ga_v, gb_v, wo_v)

        _, vjp, mixed = jax.vjp(f, [oa_ref[hd] for hd in range(MLA_HEADS)], [ob_ref[hd] for hd in range(SWA_HEADS)],
                                ga_ref[...], gb_ref[...], has_aux=True)
        doa, dob, dga, dgb = vjp(dyv)
        for hd in range(MLA_HEADS):
            doa_ref[hd] = doa[hd]
        for hd in range(SWA_HEADS):
            dob_ref[hd] = dob[hd]
        dwo = _dot_tn(mixed.astype(BF16), dyv.astype(BF16))

        @pl.when(i == 0)
        def _():
            dga_ref[...] = dga
            dgb_ref[...] = dgb
            acc_ref[...] = dwo

        @pl.when(i > 0)
        def _():
            dga_ref[...] += dga
            dgb_ref[...] += dgb
            acc_ref[...] += dwo

        @pl.when(i == s // tm - 1)
        def _():
            dwo_ref[...] = acc_ref[...].astype(BF16)

    row = pl.BlockSpec((tm, D_MODEL), lambda i: (i, 0))
    oa_spec = pl.BlockSpec((MLA_HEADS, tm, MLA_V), lambda i: (0, i, 0))
    ob_spec = pl.BlockSpec((SWA_HEADS, tm, SWA_HEAD_DIM), lambda i: (0, i, 0))
    return _pallas_call(
        body, name=name, grid=(s // tm,),
        in_specs=[row, oa_spec, ob_spec, _full((1, MLA_WIDTH)), _full((1, SWA_WIDTH)), _resident(wo.shape)],
        out_specs=[oa_spec, ob_spec, _full((1, MLA_WIDTH)), _full((1, SWA_WIDTH)), _full(wo.shape)],
        out_shape=[jax.ShapeDtypeStruct((MLA_HEADS, s, MLA_V), F32), jax.ShapeDtypeStruct((SWA_HEADS, s, SWA_HEAD_DIM), F32),
                   jax.ShapeDtypeStruct((1, MLA_WIDTH), F32), jax.ShapeDtypeStruct((1, SWA_WIDTH), F32),
                   jax.ShapeDtypeStruct(wo.shape, BF16)],
        scratch_shapes=[pltpu.VMEM(wo.shape, F32)],
        compiler_params=_params(dimension_semantics=("arbitrary",)),
    )(dy, oa, ob, ga, gb, wo)


def _attn_tile(s):
    return 512 if s >= 2048 else 128


def _causal_mask(t):
    return lax.broadcasted_iota(jnp.int32, (t, t), 1) <= lax.broadcasted_iota(jnp.int32, (t, t), 0)


def _pipelined_blocks(first, count, last_block, issue, consume, carry, prefetch_after):
    def clamped(j, slot):
        issue(jnp.minimum(j, last_block), slot)

    def pair(jj, c):
        a = first + 2 * jj
        clamped(a + 1, 1)
        c = consume(a, 0, c)
        clamped(a + 2, 0)
        return consume(a + 1, 1, c)

    clamped(first, 0)
    npairs = count // 2
    carry = lax.fori_loop(0, npairs, pair, carry)

    def odd(c):
        c = consume(first + 2 * npairs, 0, c)
        if prefetch_after:
            clamped(first + count, 0)
        return c

    return lax.cond(count - 2 * npairs == 1, odd, lambda c: c, carry)


def _run_stages_at(stages, steps):
    for stage, step in zip(stages, steps):
        here = pl.program_id(0) == step[0]
        for axis in range(1, len(step)):
            here = here & (pl.program_id(axis) == step[axis])
        pl.when(here)(stage)


def _mla_fwd(q, k, v, name, gather=None):
    nh, s, _ = q.shape
    t = _attn_tile(s)
    nq = s // t
    ng = len(gather) if gather else 0

    def body(q_ref, k_ref, v_ref, *rest):
        g_ins, (o_ref, lse_ref), g_outs = rest[:ng], rest[ng:ng + 2], rest[ng + 2:2 * ng + 2]
        (s0_ref, s1_ref), sems = rest[2 * ng + 2:2 * ng + 4], rest[2 * ng + 4:]
        if ng:
            _run_stages_at(_gather_stages(g_ins, g_outs, *sems), [(0, 0), (nh // 2, 0), (nh - 1, 0), (nh - 1, nq - 1)])
        qi = pl.program_id(1)
        qv = q_ref[...]
        s_refs = (s0_ref, s1_ref)

        def rows(j):
            return pl.ds(pl.multiple_of(j * t, t), t)

        def issue(j, slot):
            s_refs[slot][...] = _dot_nt(qv, k_ref[rows(j), :])

        def consume(j, slot, carry, masked=False):
            m, l, acc = carry
            sc = s_refs[slot][...]
            if masked:
                sc = jnp.where(_causal_mask(t), sc, NEG)
            m_new = jnp.maximum(m, jnp.max(sc, axis=-1, keepdims=True))
            alpha = jnp.exp2(m - m_new)
            p = jnp.exp2(sc - m_new)
            l = alpha * l + jnp.sum(p, axis=-1, keepdims=True)
            acc = alpha * acc + jnp.dot(p.astype(BF16), v_ref[rows(j), :], preferred_element_type=F32)
            return m_new, l, acc

        init = (jnp.full((t, 1), NEG, F32), jnp.zeros((t, 1), F32), jnp.zeros((t, MLA_V), F32))
        carry = _pipelined_blocks(0, qi, nq - 1, issue, consume, init, True)
        m, l, acc = consume(qi, 0, carry, masked=True)
        o_ref[...] = acc / l
        lse_ref[...] = m + jnp.log2(l)

    outs = _pallas_call(
        body, name=name, grid=(nh, nq),
        in_specs=[pl.BlockSpec((None, t, MLA_QK), lambda h, i: (h, i, 0)), pl.BlockSpec((None, s, MLA_QK), lambda h, i: (h, 0, 0)),
                  pl.BlockSpec((None, s, MLA_V), lambda h, i: (h, 0, 0))] + [_HBM] * ng,
        out_specs=[pl.BlockSpec((None, t, MLA_V), lambda h, i: (h, i, 0)), pl.BlockSpec((None, t, 1), lambda h, i: (h, i, 0))]
        + [_HBM] * ng,
        out_shape=[jax.ShapeDtypeStruct((nh, s, MLA_V), F32), jax.ShapeDtypeStruct((nh, s, 1), F32)]
        + (_gather_out_shapes(gather) if ng else []),
        scratch_shapes=[pltpu.VMEM((t, t), F32)] * 2 + (_exchange_scratch(8, ng) if ng else []),
        compiler_params=_params(dimension_semantics=("arbitrary", "arbitrary")),
    )(q, k, v, *(gather or []))
    return outs[0], outs[1], outs[2:]


def _mla_delta(o, do, name):
    nh, s, _ = o.shape
    t = _attn_tile(s)

    def body(o_ref, do_ref, delta_ref):
        prod = do_ref[...] * o_ref[...]
        hi = prod.astype(BF16)
        lo = (prod - hi.astype(F32)).astype(BF16)
        ones = jnp.ones((8, MLA_V), BF16)
        delta_ref[...] = _dot_nt(ones, hi) + _dot_nt(ones, lo)

    tile = pl.BlockSpec((None, t, MLA_V), lambda h, i: (h, i, 0))
    return _pallas_call(body, name=name, grid=(nh, s // t), in_specs=[tile, tile],
                        out_specs=pl.BlockSpec((None, None, 8, t), lambda h, i: (h, i, 0, 0)),
                        out_shape=jax.ShapeDtypeStruct((nh, s // t, 8, t), F32),
                        compiler_params=_params(dimension_semantics=("arbitrary", "arbitrary")))(o, do)


def _mla_bwd(q, k, v, do, lse_row, delta_row, name, scatter=None):
    nh, s, _ = q.shape
    t = _attn_tile(s)
    nq = s // t
    ng = len(scatter) if scatter else 0

    def body(q_ref, k_ref, v_ref, do_ref, lse_ref, delta_ref, *rest):
        c_ins, (dq_ref, dk_ref, dv_ref), c_outs = rest[:ng], rest[ng:ng + 3], rest[ng + 3:2 * ng + 3]
        (s0_ref, s1_ref, dp0_ref, dp1_ref), sems = rest[2 * ng + 3:2 * ng + 7], rest[2 * ng + 7:]
        if ng:
            _run_stages_at(_scatter_stages(c_ins, c_outs, *sems), [(0, 0), (nh - 1, nq - 1)])
        kj = pl.program_id(1)
        kv_, vv = k_ref[...], v_ref[...]
        s_refs, dp_refs = (s0_ref, s1_ref), (dp0_ref, dp1_ref)

        @pl.when(kj == 0)
        def _():
            dq_ref[...] = jnp.zeros_like(dq_ref)

        def rows(i):
            return pl.ds(pl.multiple_of(i * t, t), t)

        def issue(i, slot):
            s_refs[slot][...] = _dot_nt(kv_, q_ref[rows(i), :])
            dp_refs[slot][...] = _dot_nt(vv, do_ref[rows(i), :].astype(BF16))

        def consume(i, slot, carry, masked=False):
            dk, dv = carry
            p = jnp.exp2(s_refs[slot][...] - lse_ref[pl.ds(i, 1), :])
            if masked:
                p = jnp.where(lax.broadcasted_iota(jnp.int32, (t, t), 0) <= lax.broadcasted_iota(jnp.int32, (t, t), 1), p, 0.0)
            dv = dv + jnp.dot(p.astype(BF16), do_ref[rows(i), :].astype(BF16), preferred_element_type=F32)
            ds = (p * (dp_refs[slot][...] - delta_ref[i][0:1, :])).astype(BF16)
            dk = dk + jnp.dot(ds, q_ref[rows(i), :], preferred_element_type=F32)
            dq_ref[rows(i), :] += _dot_tn(ds, kv_) * MLA_SCALE
            return dk, dv

        issue(kj, 0)
        carry = consume(kj, 0, (jnp.zeros((t, MLA_QK), F32), jnp.zeros((t, MLA_V), F32)), masked=True)
        dk, dv = _pipelined_blocks(kj + 1, nq - 1 - kj, nq - 1, issue, consume, carry, False)
        dk_ref[...] = dk * LN2
        dv_ref[...] = dv

    tile = lambda w: pl.BlockSpec((None, t, w), lambda h, j: (h, j, 0))
    whole = lambda w: pl.BlockSpec((None, s, w), lambda h, j: (h, 0, 0))
    rows_spec = pl.BlockSpec((None, nq, t), lambda h, j: (h, 0, 0))
    outs = _pallas_call(
        body, name=name, grid=(nh, nq),
        in_specs=[whole(MLA_QK), tile(MLA_QK), tile(MLA_V), whole(MLA_V), rows_spec,
                  pl.BlockSpec((None, nq, 8, t), lambda h, j: (h, 0, 0, 0))] + [_HBM] * ng,
        out_specs=[whole(MLA_QK), tile(MLA_QK), tile(MLA_V)] + [_HBM] * ng,
        out_shape=[jax.ShapeDtypeStruct((nh, s, MLA_QK), F32), jax.ShapeDtypeStruct((nh, s, MLA_QK), F32),
                   jax.ShapeDtypeStruct((nh, s, MLA_V), F32)] + (_scatter_out_shapes(scatter) if ng else []),
        scratch_shapes=[pltpu.VMEM((t, t), F32)] * 4 + (_exchange_scratch(3, ng) if ng else []),
        compiler_params=_params(dimension_semantics=("arbitrary", "arbitrary")),
    )(q, k, v, do, lse_row, delta_row, *(scatter or []))
    return outs[0], outs[1], outs[2], outs[3:]


def _swa_tile(s):
    return min(s, 8 * BLOCK)


def _swa_specs(tq):
    nb = tq // BLOCK
    grp = lambda w: pl.BlockSpec((SWA_GROUP, tq, w), lambda j, i: (j, i, 0))
    main = pl.BlockSpec((None, tq, SWA_HEAD_DIM), lambda j, i: (j, i, 0))
    tail = pl.BlockSpec((None, BLOCK, SWA_HEAD_DIM), lambda j, i: (j, nb * (i + 1), 0))
    sink = pl.BlockSpec((None, SWA_GROUP, 128), lambda j, i: (j, 0, 0))
    return grp, main, tail, sink


def _swa_band_mask(first):
    shape = (SWA_GROUP * BLOCK, 2 * BLOCK)
    q_rel = (lax.broadcasted_iota(jnp.int32, shape, 0) & (BLOCK - 1)) + BLOCK
    k_rel = lax.broadcasted_iota(jnp.int32, shape, 1)
    dist = q_rel - k_rel
    return (dist >= 0) & (dist < BLOCK) & ((k_rel >= BLOCK) | jnp.logical_not(first))


def _swa_sink_column(sink_ref):
    sk = sink_ref[...]
    return jnp.concatenate([jnp.broadcast_to(sk[g:g + 1, 0:1], (BLOCK, 1)) for g in range(SWA_GROUP)], axis=0)


def _swa_fwd(q, kpad, vpad, sinks, name):
    _, s, _ = q.shape
    tq = _swa_tile(s)
    grp, main, tail, sink = _swa_specs(tq)
    d = SWA_HEAD_DIM

    def body(q_ref, km_ref, kt_ref, vm_ref, vt_ref, sink_ref, o_ref, lse_ref):
        i = pl.program_id(1)
        kall = jnp.concatenate([km_ref[...], kt_ref[...]], axis=0)
        vall = jnp.concatenate([vm_ref[...], vt_ref[...]], axis=0)
        sink_col = _swa_sink_column(sink_ref)
        for b in range(tq // BLOCK):
            lo = b * BLOCK
            valid = _swa_band_mask(i == 0 if b == 0 else False)
            q4 = q_ref[:, lo:lo + BLOCK, :].reshape(SWA_GROUP * BLOCK, d)
            sc = jnp.where(valid, _dot_nt(q4, kall[lo:lo + 2 * BLOCK]) * SWA_SCALE, NEG)
            m = jnp.maximum(jnp.max(sc, axis=-1, keepdims=True), sink_col)
            e = jnp.exp(sc - m)
            den = jnp.sum(e, axis=-1, keepdims=True) + jnp.exp(sink_col - m)
            out = jnp.dot((e * (1.0 / den)).astype(BF16), vall[lo:lo + 2 * BLOCK], preferred_element_type=F32)
            o_ref[:, lo:lo + BLOCK, :] = out.reshape(SWA_GROUP, BLOCK, d)
            lse_ref[:, lo:lo + BLOCK, :] = (m + jnp.log(den)).reshape(SWA_GROUP, BLOCK, 1)

    return _pallas_call(
        body, name=name, grid=(SWA_KV_HEADS, s // tq),
        in_specs=[grp(d), main, tail, main, tail, sink], out_specs=[grp(d), grp(1)],
        out_shape=[jax.ShapeDtypeStruct((SWA_HEADS, s, d), F32), jax.ShapeDtypeStruct((SWA_HEADS, s, 1), F32)],
        compiler_params=_params(dimension_semantics=("arbitrary", "arbitrary")),
    )(q, kpad, kpad, vpad, vpad, sinks)


def _swa_bwd(q, kpad, vpad, sinks, o, lse, do, name):
    _, s, _ = q.shape
    tq = _swa_tile(s)
    grp, main, tail, sink = _swa_specs(tq)
    d = SWA_HEAD_DIM

    def body(q_ref, km_ref, kt_ref, vm_ref, vt_ref, sink_ref, o_ref, lse_ref, do_ref, dq_ref, dk_ref, dv_ref, dsink_ref):
        i = pl.program_id(1)
        kall = jnp.concatenate([km_ref[...], kt_ref[...]], axis=0)
        vall = jnp.concatenate([vm_ref[...], vt_ref[...]], axis=0)
        sink_col = _swa_sink_column(sink_ref)

        @pl.when(i == 0)
        def _():
            dk_ref[...] = jnp.zeros_like(dk_ref)
            dv_ref[...] = jnp.zeros_like(dv_ref)
            dsink_ref[...] = jnp.zeros_like(dsink_ref)

        dsink = jnp.zeros((SWA_GROUP * BLOCK, 1), F32)
        for b in range(tq // BLOCK):
            lo = b * BLOCK
            valid = _swa_band_mask(i == 0 if b == 0 else False)
            rows4 = SWA_GROUP * BLOCK
            q4 = q_ref[:, lo:lo + BLOCK, :].reshape(rows4, d)
            do4 = do_ref[:, lo:lo + BLOCK, :].reshape(rows4, d)
            lse4 = lse_ref[:, lo:lo + BLOCK, :].reshape(rows4, 1)
            delta = jnp.sum(do4 * o_ref[:, lo:lo + BLOCK, :].reshape(rows4, d), axis=-1, keepdims=True)
            kb, vb = kall[lo:lo + 2 * BLOCK], vall[lo:lo + 2 * BLOCK]
            do4b = do4.astype(BF16)
            p = jnp.where(valid, jnp.exp(_dot_nt(q4, kb) * SWA_SCALE - lse4), 0.0)
            ds = (p * (_dot_nt(do4b, vb) - delta) * SWA_SCALE).astype(BF16)
            dq_ref[:, lo:lo + BLOCK, :] = jnp.dot(ds, kb, preferred_element_type=F32).reshape(SWA_GROUP, BLOCK, d)
            band = pl.ds(pl.multiple_of(i * tq, BLOCK) + lo, 2 * BLOCK)
            dk_ref[band, :] += _dot_tn(ds, q4)
            dv_ref[band, :] += _dot_tn(p.astype(BF16), do4b)
            dsink = dsink - jnp.exp(sink_col - lse4) * delta
        per_head = [jnp.broadcast_to(jnp.sum(dsink[g * BLOCK:(g + 1) * BLOCK], axis=0, keepdims=True), (1, 128))
                    for g in range(SWA_GROUP)]
        dsink_ref[...] += jnp.concatenate(per_head + [jnp.zeros((8 - SWA_GROUP, 128), F32)], axis=0)

    acc = pl.BlockSpec((None, s + BLOCK, d), lambda j, i: (j, 0, 0))
    return _pallas_call(
        body, name=name, grid=(SWA_KV_HEADS, s // tq),
        in_specs=[grp(d), main, tail, main, tail, sink, grp(d), grp(1), grp(d)],
        out_specs=[grp(d), acc, acc, pl.BlockSpec((None, 8, 128), lambda j, i: (j, 0, 0))],
        out_shape=[jax.ShapeDtypeStruct((SWA_HEADS, s, d), F32),
                   jax.ShapeDtypeStruct((SWA_KV_HEADS, s + BLOCK, d), F32),
                   jax.ShapeDtypeStruct((SWA_KV_HEADS, s + BLOCK, d), F32),
                   jax.ShapeDtypeStruct((SWA_KV_HEADS, 8, 128), F32)],
        compiler_params=_params(dimension_semantics=("arbitrary", "arbitrary")),
    )(q, kpad, kpad, vpad, vpad, sinks, o, lse, do)


def _loss_head(y, target, name):
    s = y.shape[0]
    tm = _row_tile(s, 512)

    def body(y_ref, t_ref, dy_ref, loss_ref):
        i = pl.program_id(0)
        err = y_ref[...] - t_ref[...]
        dy_ref[...] = err * (1.0 / D_MODEL)
        part = jnp.broadcast_to(0.5 * jnp.sum(jnp.mean(err * err, axis=-1, keepdims=True), axis=0, keepdims=True), (1, 128))

        @pl.when(i == 0)
        def _():
            loss_ref[...] = part

        @pl.when(i > 0)
        def _():
            loss_ref[...] += part

    row = pl.BlockSpec((tm, D_MODEL), lambda i: (i, 0))
    return _pallas_call(
        body, name=name, grid=(s // tm,), in_specs=[row, row], out_specs=[row, _full((1, 128))],
        out_shape=[jax.ShapeDtypeStruct((s, D_MODEL), F32), jax.ShapeDtypeStruct((1, 128), F32)],
        compiler_params=_params(dimension_semantics=("arbitrary",)),
    )(y, target)


def _adamw(w, g, m, v, name):
    rows, cols = w.shape
    tr = rows
    for cand in (512, 256, 128, 64, 32, 16, 8):
        if rows % cand == 0 and rows > cand:
            tr = cand
            break

    def body(w_ref, g_ref, m_ref, v_ref, d_ref, nm_ref, nv_ref):
        gv = g_ref[...]
        nm = ADAM_B1 * m_ref[...] + (1.0 - ADAM_B1) * gv
        nv = ADAM_B2 * v_ref[...] + (1.0 - ADAM_B2) * (gv * gv)
        m_hat = nm / (1.0 - ADAM_B1 ** ADAM_STEP)
        v_hat = nv / (1.0 - ADAM_B2 ** ADAM_STEP)
        d_ref[...] = -ADAM_LR * (m_hat / (jnp.sqrt(v_hat) + ADAM_EPS) + ADAM_WD * w_ref[...])
        nm_ref[...] = nm
        nv_ref[...] = nv

    blk = pl.BlockSpec((tr, cols), lambda i: (i, 0))
    return _pallas_call(
        body, name=name, grid=(rows // tr,), in_specs=[blk] * 4, out_specs=[blk] * 3,
        out_shape=[jax.ShapeDtypeStruct((rows, cols), F32)] * 3,
        compiler_params=_params(dimension_semantics=("arbitrary",)),
    )(w, g, m, v)


def _position():
    return lax.axis_index("x"), lax.axis_index("y"), lax.axis_index("c")


def _remote(src, dst, send_sems, recv_sems, k, to):
    return pltpu.make_async_remote_copy(src_ref=src, dst_ref=dst, send_sem=send_sems.at[k], recv_sem=recv_sems.at[k],
                                        device_id=to, device_id_type=MESH)


_HBM = pl.BlockSpec(memory_space=pltpu.HBM)


def _gather_stages(ins, outs, send_sems, recv_sems):
    na = len(ins)
    x, y, c = _position()
    me, sibling = (x, y, c), (x, y, 1 - c)
    xn, yn, dg = (1 - x, y), (x, 1 - y), (1 - x, 1 - y)

    def slot(a, chip, pc, half=None):
        ref = outs[a].at[4 * chip[0] + 2 * chip[1] + pc]
        if half is None:
            return ref
        rows = ref.shape[0] // 2
        return ref.at[pl.ds(half * rows, rows)]

    def cp(a, k, chip, pc, half, to, src=None):
        dst = slot(a, chip, pc, half)
        return _remote(dst if src is None else src, dst, send_sems, recv_sems, 8 * a + k, to)

    first_hop = [(0, xn), (1, yn)]
    second_hop = [(0, xn, 2, 0, yn), (1, yn, 3, 1, xn)]

    def sends():
        out = []
        for a in range(na):
            out += [cp(a, k, (x, y), c, None, (*to, c), src=ins[a].at[c]) for k, to in first_hop]
            out += [cp(a, fwd_k, frm, c, half, (*to, c)) for _, frm, fwd_k, half, to in second_hop]
            out += [cp(a, 4 + k, frm, c, None, sibling) for k, frm in first_hop]
            out += [cp(a, 6 + half, dg, c, half, sibling) for half in (0, 1)]
        return out

    def stage0():
        for a in range(na):
            for k, to in first_hop:
                cp(a, k, (x, y), c, None, (*to, c), src=ins[a].at[c]).start()

    def stage1():
        for k, frm, fwd_k, half, to in second_hop:
            for a in range(na):
                cp(a, k, frm, c, None, me).wait_recv()
                cp(a, fwd_k, frm, c, half, (*to, c)).start()
                cp(a, 4 + k, frm, c, None, sibling).start()

    def stage2():
        for half in (0, 1):
            for a in range(na):
                cp(a, 2 + half, dg, c, half, me).wait_recv()
                cp(a, 6 + half, dg, c, half, sibling).start()

    def stage3():
        for a in range(na):
            for k, chip, half in ((4, xn, None), (5, yn, None), (6, dg, 0), (7, dg, 1)):
                cp(a, k, chip, 1 - c, half, me).wait_recv()
        for sent in sends():
            sent.wait_send()

    return [stage0, stage1, stage2, stage3]


def _gather_out_shapes(mine):
    return [jax.ShapeDtypeStruct((N_DEV,) + m.shape[1:], m.dtype) for m in mine]


def _exchange_scratch(per_array, na):
    return [pltpu.SemaphoreType.DMA((per_array * na,)), pltpu.SemaphoreType.DMA((per_array * na,))]


def _all_gather_halves(mine, name):
    na = len(mine)

    def body(*refs):
        for stage in _gather_stages(refs[:na], refs[na:2 * na], *refs[2 * na:]):
            stage()

    return _pallas_call(body, name=name, in_specs=[_HBM] * na, out_specs=[_HBM] * na, out_shape=_gather_out_shapes(mine),
                        scratch_shapes=_exchange_scratch(8, na))(*mine)


def _sibling_exchange(parts, name, other_half):
    na = len(parts)

    def body(*refs):
        ins, outs, (send_sems, recv_sems) = refs[:na], refs[na:2 * na], refs[2 * na:]
        x, y, c = _position()
        copies = [_remote(ins[a].at[:, 1 - c] if other_half else ins[a], outs[a], send_sems, recv_sems, a, (x, y, 1 - c))
                  for a in range(na)]
        for cp in copies:
            cp.start()
        for cp in copies:
            cp.wait()

    return _pallas_call(
        body, name=name, in_specs=[_HBM] * na, out_specs=[_HBM] * na,
        out_shape=[jax.ShapeDtypeStruct(p.shape[:1] + p.shape[2:] if other_half else p.shape, p.dtype) for p in parts],
        scratch_shapes=_exchange_scratch(1, na),
    )(*parts)


def _scatter_stages(ins, outs, send_sems, recv_sems):
    na = len(ins)
    x, y, c = _position()
    chips = [(1 - x, y), (x, 1 - y), (1 - x, 1 - y)]

    def copies():
        return [_remote(ins[a].at[2 * px + py], outs[a].at[j], send_sems, recv_sems, 3 * a + j, (px, py, c))
                for a in range(na) for j, (px, py) in enumerate(chips)]

    def start():
        for cp in copies():
            cp.start()

    def wait():
        for cp in copies():
            cp.wait()

    return [start, wait]


def _scatter_out_shapes(parts):
    return [jax.ShapeDtypeStruct((3,) + p.shape[1:], p.dtype) for p in parts]


def _scatter_to_chips(parts, name):
    na = len(parts)

    def body(*refs):
        for stage in _scatter_stages(refs[:na], refs[na:2 * na], *refs[2 * na:]):
            stage()

    return _pallas_call(body, name=name, in_specs=[_HBM] * na, out_specs=[_HBM] * na, out_shape=_scatter_out_shapes(parts),
                        scratch_shapes=_exchange_scratch(3, na))(*parts)


def _assemble(g4, mine, name, side_by_side, order=None):
    _, nl, r, w = g4.shape
    tr = next(cand for cand in (256, 128, 64) if r % cand == 0)

    def body(in_ref, mine_ref, out_ref):
        chip = 2 * lax.axis_index("x") + lax.axis_index("y")
        blocks = [jnp.where(chip == sh, mine_ref[...], in_ref[sh]) for sh in range(N_SHARD)]
        if side_by_side:
            full = jnp.concatenate(blocks, axis=-1)
            out_ref[...] = full if order is None else _take_cols(full, order)
        else:
            for sh in range(N_SHARD):
                out_ref[sh] = blocks[sh]

    if side_by_side:
        out_spec = pl.BlockSpec((None, tr, N_SHARD * w), lambda l, i: (l, i, 0))
        out_shape = jax.ShapeDtypeStruct((nl, r, N_SHARD * w), g4.dtype)
    else:
        out_spec = pl.BlockSpec((None, N_SHARD, tr, w), lambda l, i: (l, 0, i, 0))
        out_shape = jax.ShapeDtypeStruct((nl, N_SHARD, r, w), g4.dtype)
    return _pallas_call(
        body, name=name, grid=(nl, r // tr),
        in_specs=[pl.BlockSpec((N_SHARD, None, tr, w), lambda l, i: (0, l, i, 0)),
                  pl.BlockSpec((None, tr, w), lambda l, i: (l, i, 0))],
        out_specs=out_spec, out_shape=out_shape,
        compiler_params=_params(dimension_semantics=("arbitrary", "arbitrary")),
    )(g4, mine)


def _all_reduce_small(vec, name):
    r, l = vec.shape

    def body(v_ref, out_ref, gath_ref, send_sems, recv_sems):
        x, y, c = _position()
        me = 4 * x + 2 * y + c
        gath_ref[me] = v_ref[...]
        copies = []
        for k in range(1, N_DEV):
            to = (x ^ (k >> 2), y ^ ((k >> 1) & 1), c ^ (k & 1))
            copies.append(_remote(gath_ref.at[me], gath_ref.at[me], send_sems, recv_sems, k - 1, to))
        for cp in copies:
            cp.start()
        for k in range(1, N_DEV):
            frm = 4 * (x ^ (k >> 2)) + 2 * (y ^ ((k >> 1) & 1)) + (c ^ (k & 1))
            _remote(gath_ref.at[frm], gath_ref.at[frm], send_sems, recv_sems, k - 1, (x, y, c)).wait_recv()
        for cp in copies:
            cp.wait_send()
        total = gath_ref[0]
        for d in range(1, N_DEV):
            total = total + gath_ref[d]
        out_ref[...] = total

    vm = pl.BlockSpec(memory_space=pltpu.VMEM)
    return _pallas_call(
        body, name=name, in_specs=[vm], out_specs=vm, out_shape=jax.ShapeDtypeStruct((r, l), F32),
        scratch_shapes=[pltpu.VMEM((N_DEV, r, l), F32), pltpu.SemaphoreType.DMA((N_DEV - 1,)),
                        pltpu.SemaphoreType.DMA((N_DEV - 1,))],
    )(vec)


def _sum_blocks(blocks, out_dtype, name):
    m, w = blocks[0].shape
    tr = next(cand for cand in (512, 256, 128, 64, 32) if m % cand == 0)

    def body(*refs):
        total = refs[0][...].astype(F32)
        for ref in refs[1:-1]:
            total = total + ref[...].astype(F32)
        refs[-1][...] = total.astype(out_dtype)

    blk = pl.BlockSpec((tr, w), lambda i: (i, 0))
    return _pallas_call(
        body, name=name, grid=(m // tr,), in_specs=[blk] * len(blocks), out_specs=blk,
        out_shape=jax.ShapeDtypeStruct((m, w), out_dtype),
        compiler_params=_params(dimension_semantics=("arbitrary",)),
    )(*blocks)


FIRST_GROUPS = (('ffn1_w_gate', 'ffn1_w_up'), ('ffn1_w_down',))
REST_GROUPS = (('ffn2_w_gate', 'ffn2_w_up'), ('ffn2_w_down', 'w_o'), ('w_in',), ('mla_w_q_b',), ('mla_w_kv_b',))
N_FIRST = len(FIRST_GROUPS)


def _shard_rows(name):
    shape, axis = BIG[name]
    return shape[0] // N_SHARD if axis == 0 else shape[0]


def _group_row_offsets(group):
    return [int(v) for v in np.cumsum([0] + [_shard_rows(n) for n in group])]


def _rope_tables(s):
    pos = jnp.arange(s, dtype=F32)
    inv = 1.0 / (ROPE_THETA ** (jnp.arange(0, MLA_ROPE, 2, dtype=F32) / MLA_ROPE))
    ang = pos[:, None] * inv[None, :]
    return jnp.tile(jnp.cos(ang), (1, SWA_HEADS)), jnp.tile(jnp.sin(ang), (1, SWA_HEADS))


_MIXER_GAINS = ('mix_norm', 'mla_q_a_norm', 'mla_kv_a_norm', 'mla_q_norm', 'mla_k_norm', 'swa_q_norm', 'swa_k_norm')


def _local_step(x, target, small, ex):
    s = x.shape[0]
    cos, sin = _rope_tables(s)
    row = lambda name, l: small[name][l][None, :]
    saved, bigs = [], []
    for l in range(DEPTH):
        big = ex.first_weights(l)
        sv = {'x0': x}
        x, sv['g1'], sv['u1'], got = _ffn_fwd(x, row('ffn1_norm', l), big['ffn1_w_gate'], big['ffn1_w_up'], big['ffn1_w_down'],
                                             f"ffn1_fwd_{l}", ex.gather_behind_ffn1(l))
        big.update(ex.rest_weights(l, got))
        bigs.append(big)
        sv['x1'] = x
        gains = [row(n, l) for n in _MIXER_GAINS]
        mixer_w = (big['w_in'], big['mla_w_q_b'], big['mla_w_kv_b'])
        q_a, k_a, v_a, q_b, k_b, v_b = _pre_fwd(x, gains, *mixer_w, cos, sin, f"pre_fwd_{l}")
        o_a, lse, got = _mla_fwd(q_a, k_a, v_a, f"mla_fwd_{l}", ex.gather_behind_mla(l))
        ex.gathered_behind_mla(l, got)
        kpad = jnp.pad(k_b, ((0, 0), (BLOCK, 0), (0, 0)))
        vpad = jnp.pad(v_b, ((0, 0), (BLOCK, 0), (0, 0)))
        sinks = jnp.broadcast_to(small['swa_sinks'][l].reshape(SWA_KV_HEADS, SWA_GROUP, 1), (SWA_KV_HEADS, SWA_GROUP, 128))
        o_b, lse_b = _swa_fwd(q_b, kpad, vpad, sinks, f"swa_fwd_{l}")
        sv.update(gains=gains, mixer_w=mixer_w, q_a=q_a, k_a=k_a, v_a=v_a, q_b=q_b, kpad=kpad, vpad=vpad, sinks=sinks,
                  o_a=o_a, lse=lse, o_b=o_b, lse_b=lse_b)
        x = _post_fwd(x, o_a, o_b, row('mla_out_norm', l), row('swa_out_norm', l), big['w_o'], f"post_fwd_{l}")
        sv['x2'] = x
        x, sv['g2'], sv['u2'], _ = _ffn_fwd(x, row('ffn2_norm', l), big['ffn2_w_gate'], big['ffn2_w_up'], big['ffn2_w_down'],
                                           f"ffn2_fwd_{l}")
        saved.append(sv)

    dx, loss = _loss_head(x, target, "loss_head")

    gs = {n: [None] * DEPTH for n in SMALL_NAMES}
    t = _attn_tile(s)
    for l in reversed(range(DEPTH)):
        sv, big = saved[l], bigs[l]

        def ffn_back(tag, xin, dy, gate, up, scatter=None):
            (dxi, dgain, nb, act, dgate, dup), got = _ffn_bwd(xin, dy, gate, up, row(tag + '_norm', l), big[tag + '_w_gate'],
                                                             big[tag + '_w_up'], big[tag + '_w_down'], f"{tag}_bwd_{l}", scatter)
            gs[tag + '_norm'][l] = dgain[0]
            grads = {tag + '_w_gate': _matmul_tn(nb, dgate, 1.0, f"{tag}_dwg_{l}", True),
                     tag + '_w_up': _matmul_tn(nb, dup, 1.0, f"{tag}_dwu_{l}", True),
                     tag + '_w_down': _matmul_tn(act, dy, 0.5, f"{tag}_dwd_{l}", False).reshape(N_SHARD, D_FF // N_SHARD, D_MODEL)}
            return dxi, grads, got

        dx, rest_grads, _ = ffn_back('ffn2', sv['x2'], dx, sv['g2'], sv['u2'])
        do_a, do_b, dga, dgb, dwo = _post_bwd(dx, sv['o_a'], sv['o_b'], row('mla_out_norm', l), row('swa_out_norm', l),
                                              big['w_o'], f"post_bwd_{l}")
        gs['mla_out_norm'][l], gs['swa_out_norm'][l] = dga[0], dgb[0]
        rest_grads['w_o'] = dwo.reshape(N_SHARD, MIX_WIDTH // N_SHARD, D_MODEL)
        delta = _mla_delta(sv['o_a'], do_a, f"mla_delta_{l}")
        dq_a, dk_a, dv_a, got = _mla_bwd(sv['q_a'], sv['k_a'], sv['v_a'], do_a, sv['lse'].reshape(MLA_HEADS, s // t, t),
                                         delta, f"mla_bwd_{l}", ex.scatter_behind_mla(l))
        ex.scattered_behind_mla(l, got)
        dq_b, dkpad, dvpad, dsink = _swa_bwd(sv['q_b'], sv['kpad'], sv['vpad'], sv['sinks'], sv['o_b'], sv['lse_b'], do_b,
                                             f"swa_bwd_{l}")
        gs['swa_sinks'][l] = dsink[:, :SWA_GROUP, 0].reshape(SWA_HEADS)
        cts = [dq_a, dk_a, dv_a, dq_b, dkpad[:, BLOCK:], dvpad[:, BLOCK:]]
        outs = _pre_bwd(sv['x1'], dx, cts, sv['gains'], *sv['mixer_w'], cos, sin, f"pre_bwd_{l}")
        dx = outs[0]
        for n, val in zip(_MIXER_GAINS, outs[1:8]):
            gs[n][l] = val[0]
        rest_grads['w_in'], rest_grads['mla_w_q_b'], rest_grads['mla_w_kv_b'] = outs[8:11]
        ex.grads_ready(l, 'rest', rest_grads)
        dx, first_grads, got = ffn_back('ffn1', sv['x0'], dx, sv['g1'], sv['u1'], ex.scatter_behind_ffn1(l))
        ex.scattered_behind_ffn1(l, got)
        ex.grads_ready(l, 'first', first_grads)
    return loss, dx, gs


class _Exchange:
    def __init__(self, weights, c, chip):
        halves_of = lambda a: a.reshape(a.shape[:-2] + (2, a.shape[-2] // 2, a.shape[-1]))
        self.halves_of, self.c, self.chip = halves_of, c, chip
        self.mine = [[halves_of(jnp.concatenate([weights[n][l].astype(BF16) for n in group], axis=0))
                      for group in FIRST_GROUPS + REST_GROUPS] for l in range(DEPTH)]
        self.ahead, self.begun, self.received = {}, {}, {}

    def _assembled(self, l, which, gathered):
        groups, base = (FIRST_GROUPS, 0) if which == 'first' else (REST_GROUPS, N_FIRST)
        big = {}
        for gi, group in enumerate(groups):
            offs = _group_row_offsets(group)
            _, rh, w = self.mine[l][base + gi].shape
            col_sharded = BIG[group[0]][1] == 1
            full = _assemble(gathered[gi].reshape(N_SHARD, 1, 2 * rh, w), self.mine[l][base + gi].reshape(1, 2 * rh, w),
                             f"assemble_{which}{gi}_{l}", col_sharded, MIXER_ORDERS.get(group[0]))
            for i, n in enumerate(group):
                rows = offs[i + 1] - offs[i]
                if len(group) > 1 and offs[i] % rows == 0:
                    big[n] = ((full, (None, rows, N_SHARD * w), (0, offs[i] // rows, 0)) if col_sharded else
                              (full, (None, N_SHARD, rows, w), (0, 0, offs[i] // rows, 0)))
                elif col_sharded:
                    big[n] = full[0, offs[i]:offs[i + 1]]
                else:
                    big[n] = full[0, :, offs[i]:offs[i + 1]].reshape(BIG[n][0])
        return big

    def first_weights(self, l):
        got = self.ahead[l][:N_FIRST] if l in self.ahead else _all_gather_halves(self.mine[l][:N_FIRST], f"gather_first_{l}")
        return self._assembled(l, 'first', got)

    def gather_behind_ffn1(self, l):
        return None if l in self.ahead else self.mine[l][N_FIRST:]

    def rest_weights(self, l, got):
        return self._assembled(l, 'rest', self.ahead[l][N_FIRST:] if l in self.ahead else got)

    def gather_behind_mla(self, l):
        return self.mine[l + 1] if l + 1 < DEPTH else None

    def gathered_behind_mla(self, l, got):
        if got:
            self.ahead[l + 1] = got

    def grads_ready(self, l, which, grads):
        groups = FIRST_GROUPS if which == 'first' else REST_GROUPS
        parts = [self.halves_of(jnp.concatenate([grads[n] for n in group], axis=1)) for group in groups]
        from_sibling = _sibling_exchange(parts, f"swap_{which}_{l}", True)
        chip_sums = []
        for gi, (p, got) in enumerate(zip(parts, from_sibling)):
            kept = lax.dynamic_index_in_dim(p, self.c, axis=1, keepdims=False)
            rows = N_SHARD * p.shape[2]
            pair = _sum_blocks([kept.reshape(rows, -1), got.reshape(rows, -1)], BF16, f"sum_pair_{which}{gi}_{l}")
            chip_sums.append(pair.reshape(got.shape))
        self.begun[(l, which)] = chip_sums

    def scatter_behind_mla(self, l):
        return self.begun[(l + 1, 'first')] + self.begun[(l + 1, 'rest')] if l + 1 < DEPTH else None

    def scattered_behind_mla(self, l, got):
        if got:
            self.received[(l + 1, 'first')], self.received[(l + 1, 'rest')] = got[:N_FIRST], got[N_FIRST:]

    def scatter_behind_ffn1(self, l):
        return self.begun[(l, 'rest')] if l == 0 else None

    def scattered_behind_ffn1(self, l, got):
        if got:
            self.received[(l, 'rest')] = got

    def reduced(self):
        keys = sorted(self.begun)
        for key in keys:
            if key not in self.received:
                self.received[key] = _scatter_to_chips(self.begun[key], f"scatter_{key[1]}_{key[0]}")
        halves = []
        for l, which in keys:
            for gi, (cs, got) in enumerate(zip(self.begun[(l, which)], self.received[(l, which)])):
                own = lax.dynamic_index_in_dim(cs, self.chip, axis=0, keepdims=False)
                halves.append(_sum_blocks([own, got[0], got[1], got[2]], F32, f"sum_chips_{which}{gi}_{l}"))
        others = _sibling_exchange(halves, "share_halves", False)
        per_layer, at = {}, 0
        for l, which in keys:
            for group in (FIRST_GROUPS if which == 'first' else REST_GROUPS):
                mine_h, other_h = halves[at], others[at]
                at += 1
                full = jnp.where(self.c == 0, jnp.concatenate([mine_h, other_h]), jnp.concatenate([other_h, mine_h]))
                offs = _group_row_offsets(group)
                for i, n in enumerate(group):
                    per_layer[(n, l)] = full[offs[i]:offs[i + 1]]
        return {n: jnp.stack([per_layer[(n, l)] for l in range(DEPTH)]) for n in BIG_NAMES}


def kernel(x, ffn1_norm, ffn1_w_gate, ffn1_w_up, ffn1_w_down, mix_norm, w_in, mla_q_a_norm, mla_w_q_b, mla_kv_a_norm, mla_w_kv_b, mla_q_norm, mla_k_norm, swa_q_norm, swa_k_norm, swa_sinks, mla_out_norm, swa_out_norm, w_o, ffn2_norm, ffn2_w_gate, ffn2_w_up, ffn2_w_down, loss_target, m_ffn1_norm, m_ffn1_w_gate, m_ffn1_w_up, m_ffn1_w_down, m_mix_norm, m_w_in, m_mla_q_a_norm, m_mla_w_q_b, m_mla_kv_a_norm, m_mla_w_kv_b, m_mla_q_norm, m_mla_k_norm, m_swa_q_norm, m_swa_k_norm, m_swa_sinks, m_mla_out_norm, m_swa_out_norm, m_w_o, m_ffn2_norm, m_ffn2_w_gate, m_ffn2_w_up, m_ffn2_w_down, v_ffn1_norm, v_ffn1_w_gate, v_ffn1_w_up, v_ffn1_w_down, v_mix_norm, v_w_in, v_mla_q_a_norm, v_mla_w_q_b, v_mla_kv_a_norm, v_mla_w_kv_b, v_mla_q_norm, v_mla_k_norm, v_swa_q_norm, v_swa_k_norm, v_swa_sinks, v_mla_out_norm, v_swa_out_norm, v_w_o, v_ffn2_norm, v_ffn2_w_gate, v_ffn2_w_up, v_ffn2_w_down):
    args = dict(locals())
    weights = {n: args[n] for n in WEIGHT_NAMES}
    mom_m = {n: args["m_" + n] for n in WEIGHT_NAMES}
    mom_v = {n: args["v_" + n] for n in WEIGHT_NAMES}
    ex = _Exchange(weights, lax.axis_index("c"), 2 * lax.axis_index("x") + lax.axis_index("y"))
    loss, dx, gs = _local_step(x[0], loss_target[0], {n: weights[n] for n in SMALL_NAMES}, ex)

    small_flat = jnp.concatenate([jnp.stack(gs[n]).reshape(-1) for n in SMALL_NAMES] + [loss[0, :1]])
    n_small = small_flat.shape[0]
    lanes = -(-n_small // (8 * 128)) * 128
    small_sum = _all_reduce_small(jnp.pad(small_flat, (0, 8 * lanes - n_small)).reshape(8, lanes), "reduce_small").reshape(-1)
    grads = ex.reduced()
    off = 0
    for n in SMALL_NAMES:
        cnt = int(np.prod(weights[n].shape))
        grads[n] = small_sum[off:off + cnt].reshape(weights[n].shape)
        off += cnt
    loss_out = small_sum[off]

    deltas, new_m, new_v = {}, {}, {}
    for n in WEIGHT_NAMES:
        shp = weights[n].shape
        two_d = (DEPTH, shp[-1]) if len(shp) == 2 else (shp[0] * shp[1], shp[2])
        d, nm, nv = _adamw(weights[n].reshape(two_d), grads[n].reshape(two_d), mom_m[n].reshape(two_d),
                           mom_v[n].reshape(two_d), f"adamw_{n}")
        deltas[n], new_m[n], new_v[n] = d.reshape(shp), nm.reshape(shp), nv.reshape(shp)

    return (loss_out, dx[None], *[grads[n] for n in WEIGHT_NAMES], *[deltas[n] for n in WEIGHT_NAMES],
            *[new_m[n] for n in WEIGHT_NAMES], *[new_v[n] for n in WEIGHT_NAMES])
```

```python
import functools

import numpy as np
import jax
import jax.numpy as jnp
from jax import lax
from jax.experimental import pallas as pl
from jax.experimental.pallas import tpu as pltpu

F32 = jnp.float32
BF16 = jnp.bfloat16

D_MODEL = 1024
DEPTH = 2
EPS = 1e-6
ROPE_THETA = 10000.0
BLOCK = 128
MLA_HEADS = 4
MLA_Q_RANK = 256
MLA_KV_RANK = 128
MLA_NOPE = 128
MLA_ROPE = 64
MLA_V = 128
MLA_QK = MLA_NOPE + MLA_ROPE
MLA_WIDTH = MLA_HEADS * MLA_V
SWA_HEADS = 8
SWA_KV_HEADS = 2
SWA_GROUP = SWA_HEADS // SWA_KV_HEADS
SWA_HEAD_DIM = 64
SWA_WIDTH = SWA_HEADS * SWA_HEAD_DIM
MIX_WIDTH = MLA_WIDTH + SWA_WIDTH
IN_SPLITS = (MLA_Q_RANK, MLA_KV_RANK, MLA_ROPE, SWA_WIDTH, SWA_KV_HEADS * SWA_HEAD_DIM, SWA_KV_HEADS * SWA_HEAD_DIM)
IN_COLS = sum(IN_SPLITS)
IN_OFFS = tuple(int(v) for v in np.cumsum((0,) + IN_SPLITS))
D_FF = 2816
MLA_SCALE = MLA_QK ** -0.5
LOG2E = 1.4426950408889634
LN2 = 0.6931471805599453
MLA_QSCALE = MLA_SCALE * LOG2E
SWA_SCALE = SWA_HEAD_DIM ** -0.5
NEG = -1e30

ADAM_LR = 0.001
ADAM_B1 = 0.9
ADAM_B2 = 0.999
ADAM_EPS = 1e-08
ADAM_WD = 0.01
ADAM_STEP = 10

N_SHARD = 4
N_DEV = 8
VMEM_LIMIT = 56 * 1024 * 1024
MESH = pl.DeviceIdType.MESH

WEIGHT_NAMES = ['ffn1_norm', 'ffn1_w_gate', 'ffn1_w_up', 'ffn1_w_down', 'mix_norm', 'w_in', 'mla_q_a_norm', 'mla_w_q_b',
                'mla_kv_a_norm', 'mla_w_kv_b', 'mla_q_norm', 'mla_k_norm', 'swa_q_norm', 'swa_k_norm', 'swa_sinks',
                'mla_out_norm', 'swa_out_norm', 'w_o', 'ffn2_norm', 'ffn2_w_gate', 'ffn2_w_up', 'ffn2_w_down']
TRANSPOSED = ('ffn1_w_gate', 'ffn1_w_up', 'ffn2_w_gate', 'ffn2_w_up')
BIG = {'ffn1_w_gate': ((D_FF, D_MODEL), 0), 'ffn1_w_up': ((D_FF, D_MODEL), 0), 'ffn1_w_down': ((D_FF, D_MODEL), 0),
       'w_in': ((D_MODEL, IN_COLS), 1), 'mla_w_q_b': ((MLA_Q_RANK, MLA_HEADS * MLA_QK), 1),
       'mla_w_kv_b': ((MLA_KV_RANK, MLA_HEADS * (MLA_NOPE + MLA_V)), 1), 'w_o': ((MIX_WIDTH, D_MODEL), 0),
       'ffn2_w_gate': ((D_FF, D_MODEL), 0), 'ffn2_w_up': ((D_FF, D_MODEL), 0), 'ffn2_w_down': ((D_FF, D_MODEL), 0)}
BIG_NAMES = [n for n in WEIGHT_NAMES if n in BIG]
SMALL_NAMES = [n for n in WEIGHT_NAMES if n not in BIG]

_pallas_call = pl.pallas_call


def _params(**kw):
    return pltpu.CompilerParams(vmem_limit_bytes=VMEM_LIMIT, **kw)


def _full(shape):
    n = len(shape)
    return pl.BlockSpec(shape, lambda *_: (0,) * n)


def _resident(shape):
    n = len(shape)
    return pl.BlockSpec(shape, lambda *_: (0,) * n, pipeline_mode=pl.Buffered(1))


@jax.custom_vjp
def _mm(a, w):
    return jnp.dot(a.astype(BF16), w, preferred_element_type=F32)


def _mm_fwd(a, w):
    return _mm(a, w), w


def _mm_bwd(w, dy):
    return lax.dot_general(dy.astype(BF16), w, (((1,), (1,)), ((), ())), preferred_element_type=F32), None


_mm.defvjp(_mm_fwd, _mm_bwd)


def _dot_nt(a, b):
    return lax.dot_general(a, b, (((1,), (1,)), ((), ())), preferred_element_type=F32)


def _dot_tn(a, b):
    return lax.dot_general(a, b, (((0,), (0,)), ((), ())), preferred_element_type=F32)


def _rms(t, g):
    return t * lax.rsqrt(jnp.mean(t * t, axis=-1, keepdims=True) + EPS) * g


def _sigmoid(z):
    return 1.0 / (1.0 + jnp.exp(-z))


def _row_tile(s, want):
    return min(want, s)


FF_CHUNK = 1408


def _weight_operand(w):
    if isinstance(w, tuple):
        arr, block, index = w
        return arr, pl.BlockSpec(block, lambda *_: index, pipeline_mode=pl.Buffered(1))
    return w, _resident(w.shape)


def _weight_rows(ref, start, n):
    if len(ref.shape) == 2:
        return ref[start:start + n, :]
    per = ref.shape[1]
    return ref[start // per:(start + n) // per].reshape(n, ref.shape[2])


def _ffn_fwd(x, g, wg, wu, wd, name, gather=None):
    s = x.shape[0]
    tm = _row_tile(s, 256)
    steps = s // tm
    ng = len(gather) if gather else 0
    (wg, wg_spec), (wu, wu_spec), (wd, wd_spec) = _weight_operand(wg), _weight_operand(wu), _weight_operand(wd)

    def body(x_ref, g_ref, wg_ref, wu_ref, wd_ref, *rest):
        g_ins, (y_ref, gate_ref, up_ref), g_outs, sems = rest[:ng], rest[ng:ng + 3], rest[ng + 3:2 * ng + 3], rest[2 * ng + 3:]
        if ng:
            _run_stages_at(_gather_stages(g_ins, g_outs, *sems), [(0,), (steps * 3 // 8,), (steps * 11 // 16,), (steps - 1,)])
        xv = x_ref[...]
        nb = _rms(xv, g_ref[...]).astype(BF16)
        acc = xv
        for c in range(0, D_FF, FF_CHUNK):
            gate = _dot_nt(nb, _weight_rows(wg_ref, c, FF_CHUNK))
            up = _dot_nt(nb, _weight_rows(wu_ref, c, FF_CHUNK))
            gate_ref[:, c:c + FF_CHUNK] = gate.astype(BF16)
            up_ref[:, c:c + FF_CHUNK] = up.astype(BF16)
            act = (gate * _sigmoid(gate) * up).astype(BF16)
            acc = acc + 0.5 * jnp.dot(act, _weight_rows(wd_ref, c, FF_CHUNK), preferred_element_type=F32)
        y_ref[...] = acc

    outs = _pallas_call(
        body, name=name, grid=(steps,),
        in_specs=[pl.BlockSpec((tm, D_MODEL), lambda i: (i, 0)), _full((1, D_MODEL)), wg_spec, wu_spec, wd_spec] + [_HBM] * ng,
        out_specs=[pl.BlockSpec((tm, D_MODEL), lambda i: (i, 0)), pl.BlockSpec((tm, D_FF), lambda i: (i, 0)),
                   pl.BlockSpec((tm, D_FF), lambda i: (i, 0))] + [_HBM] * ng,
        out_shape=[jax.ShapeDtypeStruct((s, D_MODEL), F32), jax.ShapeDtypeStruct((s, D_FF), BF16),
                   jax.ShapeDtypeStruct((s, D_FF), BF16)] + (_gather_out_shapes(gather) if ng else []),
        scratch_shapes=_exchange_scratch(8, ng) if ng else [],
        compiler_params=_params(dimension_semantics=("arbitrary",)),
    )(x, g, wg, wu, wd, *(gather or []))
    return outs[0], outs[1], outs[2], outs[3:]


def _ffn_bwd(x, dy, gate, up, g, wg, wu, wd, name, scatter=None):
    s = x.shape[0]
    tm = _row_tile(s, 256)
    steps = s // tm
    ng = len(scatter) if scatter else 0
    (wg, wg_spec), (wu, wu_spec), (wd, wd_spec) = _weight_operand(wg), _weight_operand(wu), _weight_operand(wd)

    def body(x_ref, dy_ref, gate_ref, up_ref, g_ref, wg_ref, wu_ref, wd_ref, *rest):
        c_ins, (dx_ref, dgain_ref, n_ref, act_ref, dgate_ref, dup_ref) = rest[:ng], rest[ng:ng + 6]
        c_outs, sems = rest[ng + 6:2 * ng + 6], rest[2 * ng + 6:]
        if ng:
            _run_stages_at(_scatter_stages(c_ins, c_outs, *sems), [(0,), (steps - 1,)])
        i = pl.program_id(0)
        xv = x_ref[...]
        dyv = dy_ref[...]
        gv = g_ref[...]
        r = lax.rsqrt(jnp.mean(xv * xv, axis=-1, keepdims=True) + EPS)
        xh = xv * r
        n_ref[...] = (xh * gv).astype(BF16)
        dyh = (0.5 * dyv).astype(BF16)
        dn = jnp.zeros_like(xv)
        for c in range(0, D_FF, FF_CHUNK):
            dact = _dot_nt(dyh, _weight_rows(wd_ref, c, FF_CHUNK))
            gt = gate_ref[:, c:c + FF_CHUNK].astype(F32)
            u = up_ref[:, c:c + FF_CHUNK].astype(F32)
            sg = _sigmoid(gt)
            sl = gt * sg
            act_ref[:, c:c + FF_CHUNK] = (sl * u).astype(BF16)
            dup = (dact * sl).astype(BF16)
            dgate = (dact * u * (sg * (1.0 + gt * (1.0 - sg)))).astype(BF16)
            dup_ref[:, c:c + FF_CHUNK] = dup
            dgate_ref[:, c:c + FF_CHUNK] = dgate
            dn = (dn + jnp.dot(dgate, _weight_rows(wg_ref, c, FF_CHUNK), preferred_element_type=F32)
                  + jnp.dot(dup, _weight_rows(wu_ref, c, FF_CHUNK), preferred_element_type=F32))
        part = jnp.sum(dn * xh, axis=0, keepdims=True)

        @pl.when(i == 0)
        def _():
            dgain_ref[...] = part

        @pl.when(i > 0)
        def _():
            dgain_ref[...] += part

        dxh = dn * gv
        dx_ref[...] = dyv + r * (dxh - xh * jnp.mean(dxh * xh, axis=-1, keepdims=True))

    row = lambda w: pl.BlockSpec((tm, w), lambda i: (i, 0))
    outs = _pallas_call(
        body, name=name, grid=(steps,),
        in_specs=[row(D_MODEL), row(D_MODEL), row(D_FF), row(D_FF), _full((1, D_MODEL)), wg_spec, wu_spec, wd_spec]
        + [_HBM] * ng,
        out_specs=[row(D_MODEL), _full((1, D_MODEL)), row(D_MODEL), row(D_FF), row(D_FF), row(D_FF)] + [_HBM] * ng,
        out_shape=[jax.ShapeDtypeStruct((s, D_MODEL), F32), jax.ShapeDtypeStruct((1, D_MODEL), F32),
                   jax.ShapeDtypeStruct((s, D_MODEL), BF16), jax.ShapeDtypeStruct((s, D_FF), BF16),
                   jax.ShapeDtypeStruct((s, D_FF), BF16), jax.ShapeDtypeStruct((s, D_FF), BF16)]
        + (_scatter_out_shapes(scatter) if ng else []),
        scratch_shapes=_exchange_scratch(3, ng) if ng else [],
        compiler_params=_params(dimension_semantics=("arbitrary",)),
    )(x, dy, gate, up, g, wg, wu, wd, *(scatter or []))
    return outs[:6], outs[6:]


def _store_col_shards(o_ref, acc, first_shard, n_here, width):
    for q in range(n_here):
        o_ref[q] = acc[:, (first_shard + q) * width:(first_shard + q + 1) * width].astype(BF16)


def _matmul_tn(a, b, scale, name):
    t, m = a.shape
    n = b.shape[1]
    tk = _row_tile(t, 2048)
    tn = n // 2
    nk = t // tk

    def body(a_ref, b_ref, o_ref, acc_ref):
        k = pl.program_id(1)
        bv = b_ref[...]
        if scale != 1.0:
            bv = bv.astype(F32) * scale
        part = _dot_tn(a_ref[...].astype(BF16), bv.astype(BF16))

        @pl.when(k == 0)
        def _():
            acc_ref[...] = part

        @pl.when(k > 0)
        def _():
            acc_ref[...] += part

        @pl.when(k == nk - 1)
        def _():
            o_ref[...] = acc_ref[...].astype(BF16)

    return _pallas_call(
        body, name=name, grid=(n // tn, nk),
        in_specs=[pl.BlockSpec((tk, m), lambda j, k: (k, 0)), pl.BlockSpec((tk, tn), lambda j, k: (k, j))],
        out_specs=pl.BlockSpec((m, tn), lambda j, k: (0, j)), out_shape=jax.ShapeDtypeStruct((m, n), BF16),
        scratch_shapes=[pltpu.VMEM((m, tn), F32)],
        compiler_params=_params(dimension_semantics=("arbitrary", "arbitrary")),
    )(a, b)


_HALF = SWA_HEAD_DIM // 2
_IN_ORDER = (list(range(0, IN_OFFS[2]))
             + [IN_OFFS[3] + SWA_HEAD_DIM * h + i for h in range(SWA_HEADS) for i in range(_HALF)]
             + [IN_OFFS[3] + SWA_HEAD_DIM * h + _HALF + i for h in range(SWA_HEADS) for i in range(_HALF)]
             + list(range(IN_OFFS[5], IN_OFFS[6]))
             + [IN_OFFS[4] + SWA_HEAD_DIM * j + i for j in range(SWA_KV_HEADS) for i in range(_HALF)]
             + [IN_OFFS[4] + SWA_HEAD_DIM * j + _HALF + i for j in range(SWA_KV_HEADS) for i in range(_HALF)]
             + list(range(IN_OFFS[2], IN_OFFS[3])))
_QB_ORDER = ([MLA_QK * h + i for h in range(MLA_HEADS) for i in range(MLA_NOPE)]
             + [MLA_QK * h + MLA_NOPE + i for h in range(MLA_HEADS) for i in range(_HALF)]
             + [MLA_QK * h + MLA_NOPE + _HALF + i for h in range(MLA_HEADS) for i in range(_HALF)])
_KVB_ORDER = ([(MLA_NOPE + MLA_V) * h + i for h in range(MLA_HEADS) for i in range(MLA_NOPE)]
              + [(MLA_NOPE + MLA_V) * h + MLA_NOPE + i for h in range(MLA_HEADS) for i in range(MLA_V)])
MIXER_ORDERS = {'w_in': _IN_ORDER, 'mla_w_q_b': _QB_ORDER, 'mla_w_kv_b': _KVB_ORDER}
_P_CQ, _P_CKV, _P_QA, _P_QB, _P_VS, _P_KA, _P_KB, _P_PE = (int(v) for v in np.cumsum(
    (0, MLA_Q_RANK, MLA_KV_RANK, SWA_WIDTH // 2, SWA_WIDTH // 2, IN_SPLITS[5], IN_SPLITS[4] // 2, IN_SPLITS[4] // 2)))


def _runs(order):
    out, start = [], 0
    for i in range(1, len(order) + 1):
        if i == len(order) or order[i] != order[i - 1] + 1:
            out.append((order[start], i - start))
            start = i
    return out


def _inverse(order):
    inv = [0] * len(order)
    for new, old in enumerate(order):
        inv[old] = new
    return inv


def _take_cols(a, order):
    return jnp.concatenate([a[..., st:st + w] for st, w in _runs(order)], axis=-1)


def _segment_matrix(n, seg):
    return (lax.broadcasted_iota(jnp.int32, (n, n), 0) // seg == lax.broadcasted_iota(jnp.int32, (n, n), 1) // seg).astype(BF16)


@jax.custom_vjp
def _cmm(t, b, bt):
    hi = t.astype(BF16)
    lo = (t - hi.astype(F32)).astype(BF16)
    return jnp.dot(hi, b, preferred_element_type=F32) + jnp.dot(lo, b, preferred_element_type=F32)


def _cmm_fwd(t, b, bt):
    return _cmm(t, b, bt), (b, bt)


def _cmm_bwd(res, dy):
    b, bt = res
    return _cmm(dy, bt, b), None, None


_cmm.defvjp(_cmm_fwd, _cmm_bwd)


def _segsum(t, b):
    return _cmm(t, b, b)


def _rowsum(t):
    n = t.shape[-1]
    return _cmm(t, jnp.ones((n, 128), BF16), jnp.ones((128, n), BF16))


def _by_head(vals, width):
    lane = lax.broadcasted_iota(jnp.int32, (vals[0].shape[0], len(vals) * width), 1)
    out = vals[-1]
    for hd in range(len(vals) - 2, -1, -1):
        out = jnp.where(lane < (hd + 1) * width, vals[hd], out)
    return out


def _rope2(a, b, cos, sin):
    return a * cos - b * sin, b * cos + a * sin


def _pre_math(x, gm, gqa, gkva, gq, gk, gsq, gsk, taps, win, wqb, wkvb, cos, sin):
    h = _rms(x, gm)
    proj = _mm(h, win)
    if taps is not None:
        proj = proj + taps[0]
    cqn = _rms(proj[:, _P_CQ:_P_CKV], gqa)
    qa_all = _mm(cqn, wqb)
    ckvn = _rms(proj[:, _P_CKV:_P_QA], gkva)
    kv_all = _mm(ckvn, wkvb)
    if taps is not None:
        qa_all = qa_all + taps[1]
        kv_all = kv_all + taps[2]
    nh, hw = MLA_HEADS, MLA_HEADS * _HALF
    seg_mla = _segment_matrix(hw, _HALF)
    tile = lambda g, n: jnp.concatenate([g] * n, axis=-1)
    c4, s4 = cos[:, :hw], sin[:, :hw]

    def mla_heads(nope, r1, r2, gain):
        rr = r1 * r1 + r2 * r2
        lane_head = lax.broadcasted_iota(jnp.int32, (hw, nh * MLA_NOPE), 0) // _HALF
        spread = (lane_head == lax.broadcasted_iota(jnp.int32, (hw, nh * MLA_NOPE), 1) // MLA_NOPE).astype(BF16)
        rope_on_nope = _cmm(rr, spread, spread.T)
        ss_nope = [_rowsum(jnp.square(nope[:, hd * MLA_NOPE:(hd + 1) * MLA_NOPE])) for hd in range(nh)]
        rinv = [lax.rsqrt((ss_nope[hd] + rope_on_nope[:, hd * MLA_NOPE:(hd + 1) * MLA_NOPE]) * (1.0 / MLA_QK) + EPS)
                for hd in range(nh)]
        rl = lax.rsqrt((_segsum(rr, seg_mla) + _by_head(ss_nope, _HALF)) * (1.0 / MLA_QK) + EPS)
        o1, o2 = _rope2(r1 * rl * tile(gain[:, MLA_NOPE:MLA_NOPE + _HALF], nh), r2 * rl * tile(gain[:, MLA_NOPE + _HALF:], nh), c4, s4)
        return [jnp.concatenate([nope[:, hd * MLA_NOPE:(hd + 1) * MLA_NOPE] * rinv[hd] * gain[:, :MLA_NOPE],
                                 o1[:, hd * _HALF:(hd + 1) * _HALF], o2[:, hd * _HALF:(hd + 1) * _HALF]], axis=-1)
                for hd in range(nh)]

    q_a = mla_heads(qa_all[:, :nh * MLA_NOPE], qa_all[:, nh * MLA_NOPE:nh * MLA_NOPE + hw], qa_all[:, nh * MLA_NOPE + hw:], gq)
    pe1, pe2 = proj[:, _P_PE:_P_PE + _HALF], proj[:, _P_PE + _HALF:_P_PE + 2 * _HALF]
    k_a = mla_heads(kv_all[:, :nh * MLA_NOPE], tile(pe1, nh), tile(pe2, nh), gk)
    v_a = [kv_all[:, nh * MLA_NOPE + hd * MLA_V:nh * MLA_NOPE + (hd + 1) * MLA_V] for hd in range(nh)]

    def swa_heads(a, b, gain, n):
        w = n * _HALF
        r = lax.rsqrt(_segsum(a * a + b * b, _segment_matrix(w, _HALF)) * (1.0 / SWA_HEAD_DIM) + EPS)
        o1, o2 = _rope2(a * r * tile(gain[:, :_HALF], n), b * r * tile(gain[:, _HALF:], n), cos[:, :w], sin[:, :w])
        return [jnp.concatenate([o1[:, hd * _HALF:(hd + 1) * _HALF], o2[:, hd * _HALF:(hd + 1) * _HALF]], axis=-1) for hd in range(n)]

    q_b = swa_heads(proj[:, _P_QA:_P_QB], proj[:, _P_QB:_P_VS], gsq, SWA_HEADS)
    k_b = swa_heads(proj[:, _P_KA:_P_KB], proj[:, _P_KB:_P_PE], gsk, SWA_KV_HEADS)
    v_b = [proj[:, _P_VS + j * SWA_HEAD_DIM:_P_VS + (j + 1) * SWA_HEAD_DIM] for j in range(SWA_KV_HEADS)]
    return (q_a, k_a, v_a, q_b, k_b, v_b), (h, cqn, ckvn)


_PRE_GAIN_WIDTHS = (D_MODEL, MLA_Q_RANK, MLA_KV_RANK, MLA_QK, MLA_QK, SWA_HEAD_DIM, SWA_HEAD_DIM)
_PRE_HEADS = ((MLA_HEADS, MLA_QK), (MLA_HEADS, MLA_QK), (MLA_HEADS, MLA_V),
              (SWA_HEADS, SWA_HEAD_DIM), (SWA_KV_HEADS, SWA_HEAD_DIM), (SWA_KV_HEADS, SWA_HEAD_DIM))


def _pre_fwd(x, gains, win, wqb, wkvb, cos, sin, name):
    s = x.shape[0]
    tm = _row_tile(s, 512)

    def body(x_ref, *refs):
        g_refs, (win_ref, wqb_ref, wkvb_ref, cos_ref, sin_ref), out_refs = refs[:7], refs[7:12], refs[12:]
        outs, _ = _pre_math(x_ref[...], *[g[...] for g in g_refs], None, win_ref[...], wqb_ref[...], wkvb_ref[...],
                            cos_ref[...], sin_ref[...])
        for idx, (ref, heads) in enumerate(zip(out_refs, outs)):
            for hd, val in enumerate(heads):
                ref[hd] = (val * MLA_QSCALE if idx == 0 else val).astype(BF16)

    heads_spec = lambda nh, w: pl.BlockSpec((nh, tm, w), lambda i: (0, i, 0))
    return _pallas_call(
        body, name=name, grid=(s // tm,),
        in_specs=[pl.BlockSpec((tm, D_MODEL), lambda i: (i, 0))] + [_full((1, w)) for w in _PRE_GAIN_WIDTHS]
        + [_resident(win.shape), _resident(wqb.shape), _resident(wkvb.shape),
           pl.BlockSpec((tm, SWA_HEADS * _HALF), lambda i: (i, 0)), pl.BlockSpec((tm, SWA_HEADS * _HALF), lambda i: (i, 0))],
        out_specs=[heads_spec(nh, w) for nh, w in _PRE_HEADS],
        out_shape=[jax.ShapeDtypeStruct((nh, s, w), BF16) for nh, w in _PRE_HEADS],
        compiler_params=_params(dimension_semantics=("arbitrary",)),
    )(x, *gains, win, wqb, wkvb, cos, sin)


def _pre_bwd(x, dx_res, cts, gains, win, wqb, wkvb, cos, sin, name):
    s = x.shape[0]
    tm = _row_tile(s, 256)
    tap_widths = (IN_COLS, MLA_HEADS * MLA_QK, MLA_HEADS * (MLA_NOPE + MLA_V))

    def body(x_ref, dxr_ref, *refs):
        ct_refs, g_refs = refs[:6], refs[6:13]
        win_ref, wqb_ref, wkvb_ref, cos_ref, sin_ref = refs[13:18]
        dx_ref, dg_refs, dw_refs, acc_refs = refs[18], refs[19:26], refs[26:29], refs[29:32]
        i = pl.program_id(0)
        win_v, wqb_v, wkvb_v, cos_v, sin_v = win_ref[...], wqb_ref[...], wkvb_ref[...], cos_ref[...], sin_ref[...]

        def f(xv, gm, gqa, gkva, gq, gk, gsq, gsk, t0, t1, t2):
            return _pre_math(xv, gm, gqa, gkva, gq, gk, gsq, gsk, (t0, t1, t2), win_v, wqb_v, wkvb_v, cos_v, sin_v)

        taps = [jnp.zeros((tm, w), F32) for w in tap_widths]
        _, vjp, acts = jax.vjp(f, x_ref[...], *[g[...] for g in g_refs], *taps, has_aux=True)
        ct = tuple([ref[hd] for hd in range(nh)] for ref, (nh, _) in zip(ct_refs, _PRE_HEADS))
        grads = vjp(ct)
        dx_ref[...] = grads[0] + dxr_ref[...]
        dws = [_dot_tn(a.astype(BF16), t.astype(BF16)) for a, t in zip(acts, grads[8:11])]

        @pl.when(i == 0)
        def _():
            for ref, val in zip(dg_refs, grads[1:8]):
                ref[...] = val
            for ref, val in zip(acc_refs, dws):
                ref[...] = val

        @pl.when(i > 0)
        def _():
            for ref, val in zip(dg_refs, grads[1:8]):
                ref[...] += val
            for ref, val in zip(acc_refs, dws):
                ref[...] += val

        @pl.when(i == s // tm - 1)
        def _():
            for ref, acc, order in zip(dw_refs, acc_refs, (_IN_ORDER, _QB_ORDER, _KVB_ORDER)):
                _store_col_shards(ref, _take_cols(acc[...], _inverse(order)), 0, N_SHARD, acc.shape[1] // N_SHARD)

    heads_spec = lambda nh, w: pl.BlockSpec((nh, tm, w), lambda i: (0, i, 0))
    row = pl.BlockSpec((tm, D_MODEL), lambda i: (i, 0))
    half = pl.BlockSpec((tm, SWA_HEADS * _HALF), lambda i: (i, 0))
    shard_shapes = [(N_SHARD, w.shape[0], w.shape[1] // N_SHARD) for w in (win, wqb, wkvb)]
    return _pallas_call(
        body, name=name, grid=(s // tm,),
        in_specs=[row, row] + [heads_spec(nh, w) for nh, w in _PRE_HEADS] + [_full((1, w)) for w in _PRE_GAIN_WIDTHS]
        + [_resident(win.shape), _resident(wqb.shape), _resident(wkvb.shape), half, half],
        out_specs=[row] + [_full((1, w)) for w in _PRE_GAIN_WIDTHS] + [_full(shp) for shp in shard_shapes],
        out_shape=[jax.ShapeDtypeStruct((s, D_MODEL), F32)] + [jax.ShapeDtypeStruct((1, w), F32) for w in _PRE_GAIN_WIDTHS]
        + [jax.ShapeDtypeStruct(shp, BF16) for shp in shard_shapes],
        scratch_shapes=[pltpu.VMEM(w.shape, F32) for w in (win, wqb, wkvb)],
        compiler_params=_params(dimension_semantics=("arbitrary",)),
    )(x, dx_res, *cts, *gains, win, wqb, wkvb, cos, sin)


def _post_math(oa, ob, ga, gb, wo):
    mixed = jnp.concatenate([_rms(jnp.concatenate(oa, axis=-1), ga), _rms(jnp.concatenate(ob, axis=-1), gb)], axis=-1)
    return _mm(mixed, wo), mixed


def _post_fwd(x, oa, ob, ga, gb, wo, name):
    s = x.shape[0]
    tm = _row_tile(s, 512)

    def body(x_ref, oa_ref, ob_ref, ga_ref, gb_ref, wo_ref, y_ref):
        y, _ = _post_math([oa_ref[hd] for hd in range(MLA_HEADS)], [ob_ref[hd] for hd in range(SWA_HEADS)],
                          ga_ref[...], gb_ref[...], wo_ref[...])
        y_ref[...] = x_ref[...] + y

    row = pl.BlockSpec((tm, D_MODEL), lambda i: (i, 0))
    return _pallas_call(
        body, name=name, grid=(s // tm,),
        in_specs=[row, pl.BlockSpec((MLA_HEADS, tm, MLA_V), lambda i: (0, i, 0)),
                  pl.BlockSpec((SWA_HEADS, tm, SWA_HEAD_DIM), lambda i: (0, i, 0)),
                  _full((1, MLA_WIDTH)), _full((1, SWA_WIDTH)), _resident(wo.shape)],
        out_specs=row, out_shape=jax.ShapeDtypeStruct((s, D_MODEL), F32),
        compiler_params=_params(dimension_semantics=("arbitrary",)),
    )(x, oa, ob, ga, gb, wo)


def _post_bwd(dy, oa, ob, ga, gb, wo, name):
    s = dy.shape[0]
    tm = _row_tile(s, 512)

    def body(dy_ref, oa_ref, ob_ref, ga_ref, gb_ref, wo_ref, doa_ref, dob_ref, dga_ref, dgb_ref, dwo_ref, acc_ref):
        i = pl.program_id(0)
        wo_v = wo_ref[...]
        dyv = dy_ref[...]

        def f(oa_l, ob_l, ga_v, gb_v):
            return _post_math(oa_l, ob_l, ga_v, gb_v, wo_v)

        _, vjp, mixed = jax.vjp(f, [oa_ref[hd] for hd in range(MLA_HEADS)], [ob_ref[hd] for hd in range(SWA_HEADS)],
                                ga_ref[...], gb_ref[...], has_aux=True)
        doa, dob, dga, dgb = vjp(dyv)
        for hd in range(MLA_HEADS):
            doa_ref[hd] = doa[hd]
        for hd in range(SWA_HEADS):
            dob_ref[hd] = dob[hd]
        dwo = _dot_tn(mixed.astype(BF16), dyv.astype(BF16))

        @pl.when(i == 0)
        def _():
            dga_ref[...] = dga
            dgb_ref[...] = dgb
            acc_ref[...] = dwo

        @pl.when(i > 0)
        def _():
            dga_ref[...] += dga
            dgb_ref[...] += dgb
            acc_ref[...] += dwo

        @pl.when(i == s // tm - 1)
        def _():
            dwo_ref[...] = acc_ref[...].astype(BF16)

    row = pl.BlockSpec((tm, D_MODEL), lambda i: (i, 0))
    oa_spec = pl.BlockSpec((MLA_HEADS, tm, MLA_V), lambda i: (0, i, 0))
    ob_spec = pl.BlockSpec((SWA_HEADS, tm, SWA_HEAD_DIM), lambda i: (0, i, 0))
    return _pallas_call(
        body, name=name, grid=(s // tm,),
        in_specs=[row, oa_spec, ob_spec, _full((1, MLA_WIDTH)), _full((1, SWA_WIDTH)), _resident(wo.shape)],
        out_specs=[oa_spec, ob_spec, _full((1, MLA_WIDTH)), _full((1, SWA_WIDTH)), _full(wo.shape)],
        out_shape=[jax.ShapeDtypeStruct((MLA_HEADS, s, MLA_V), F32), jax.ShapeDtypeStruct((SWA_HEADS, s, SWA_HEAD_DIM), F32),
                   jax.ShapeDtypeStruct((1, MLA_WIDTH), F32), jax.ShapeDtypeStruct((1, SWA_WIDTH), F32),
                   jax.ShapeDtypeStruct(wo.shape, BF16)],
        scratch_shapes=[pltpu.VMEM(wo.shape, F32)],
        compiler_params=_params(dimension_semantics=("arbitrary",)),
    )(dy, oa, ob, ga, gb, wo)


def _attn_tile(s):
    return 512 if s >= 2048 else 128


def _causal_mask(t):
    return lax.broadcasted_iota(jnp.int32, (t, t), 1) <= lax.broadcasted_iota(jnp.int32, (t, t), 0)


def _pipelined_blocks(first, count, last_block, issue, consume, carry, prefetch_after):
    def clamped(j, slot):
        issue(jnp.minimum(j, last_block), slot)

    def pair(jj, c):
        a = first + 2 * jj
        clamped(a + 1, 1)
        c = consume(a, 0, c)
        clamped(a + 2, 0)
        return consume(a + 1, 1, c)

    clamped(first, 0)
    npairs = count // 2
    carry = lax.fori_loop(0, npairs, pair, carry)

    def odd(c):
        c = consume(first + 2 * npairs, 0, c)
        if prefetch_after:
            clamped(first + count, 0)
        return c

    return lax.cond(count - 2 * npairs == 1, odd, lambda c: c, carry)


def _run_stages_at(stages, steps):
    for stage, step in zip(stages, steps):
        here = pl.program_id(0) == step[0]
        for axis in range(1, len(step)):
            here = here & (pl.program_id(axis) == step[axis])
        pl.when(here)(stage)


def _mla_fwd(q, k, v, name, gather=None):
    nh, s, _ = q.shape
    t = _attn_tile(s)
    nq = s // t
    ng = len(gather) if gather else 0

    def body(q_ref, k_ref, v_ref, *rest):
        g_ins, (o_ref, lse_ref), g_outs = rest[:ng], rest[ng:ng + 2], rest[ng + 2:2 * ng + 2]
        (s0_ref, s1_ref), sems = rest[2 * ng + 2:2 * ng + 4], rest[2 * ng + 4:]
        if ng:
            _run_stages_at(_gather_stages(g_ins, g_outs, *sems), [(0, 0), (nh // 2, 0), (nh - 1, 0), (nh - 1, nq - 1)])
        qi = pl.program_id(1)
        qv = q_ref[...]
        s_refs = (s0_ref, s1_ref)

        def rows(j):
            return pl.ds(pl.multiple_of(j * t, t), t)

        def issue(j, slot):
            s_refs[slot][...] = _dot_nt(qv, k_ref[rows(j), :])

        def consume(j, slot, carry, masked=False):
            m, l, acc = carry
            sc = s_refs[slot][...]
            if masked:
                sc = jnp.where(_causal_mask(t), sc, NEG)
            m_new = jnp.maximum(m, jnp.max(sc, axis=-1, keepdims=True))
            alpha = jnp.exp2(m - m_new)
            p = jnp.exp2(sc - m_new)
            l = alpha * l + jnp.sum(p, axis=-1, keepdims=True)
            acc = alpha * acc + jnp.dot(p.astype(BF16), v_ref[rows(j), :], preferred_element_type=F32)
            return m_new, l, acc

        init = (jnp.full((t, 1), NEG, F32), jnp.zeros((t, 1), F32), jnp.zeros((t, MLA_V), F32))
        carry = _pipelined_blocks(0, qi, nq - 1, issue, consume, init, True)
        m, l, acc = consume(qi, 0, carry, masked=True)
        o_ref[...] = acc / l
        lse_ref[...] = m + jnp.log2(l)

    outs = _pallas_call(
        body, name=name, grid=(nh, nq),
        in_specs=[pl.BlockSpec((None, t, MLA_QK), lambda h, i: (h, i, 0)), pl.BlockSpec((None, s, MLA_QK), lambda h, i: (h, 0, 0)),
                  pl.BlockSpec((None, s, MLA_V), lambda h, i: (h, 0, 0))] + [_HBM] * ng,
        out_specs=[pl.BlockSpec((None, t, MLA_V), lambda h, i: (h, i, 0)), pl.BlockSpec((None, t, 1), lambda h, i: (h, i, 0))]
        + [_HBM] * ng,
        out_shape=[jax.ShapeDtypeStruct((nh, s, MLA_V), F32), jax.ShapeDtypeStruct((nh, s, 1), F32)]
        + (_gather_out_shapes(gather) if ng else []),
        scratch_shapes=[pltpu.VMEM((t, t), F32)] * 2 + (_exchange_scratch(8, ng) if ng else []),
        compiler_params=_params(dimension_semantics=("arbitrary", "arbitrary")),
    )(q, k, v, *(gather or []))
    return outs[0], outs[1], outs[2:]


def _mla_delta(o, do, name):
    nh, s, _ = o.shape
    t = _attn_tile(s)

    def body(o_ref, do_ref, delta_ref):
        prod = do_ref[...] * o_ref[...]
        hi = prod.astype(BF16)
        lo = (prod - hi.astype(F32)).astype(BF16)
        ones = jnp.ones((8, MLA_V), BF16)
        delta_ref[...] = _dot_nt(ones, hi) + _dot_nt(ones, lo)

    tile = pl.BlockSpec((None, t, MLA_V), lambda h, i: (h, i, 0))
    return _pallas_call(body, name=name, grid=(nh, s // t), in_specs=[tile, tile],
                        out_specs=pl.BlockSpec((None, None, 8, t), lambda h, i: (h, i, 0, 0)),
                        out_shape=jax.ShapeDtypeStruct((nh, s // t, 8, t), F32),
                        compiler_params=_params(dimension_semantics=("arbitrary", "arbitrary")))(o, do)


def _mla_bwd(q, k, v, do, lse_row, delta_row, name, scatter=None):
    nh, s, _ = q.shape
    t = _attn_tile(s)
    nq = s // t
    ng = len(scatter) if scatter else 0

    def body(q_ref, k_ref, v_ref, do_ref, lse_ref, delta_ref, *rest):
        c_ins, (dq_ref, dk_ref, dv_ref), c_outs = rest[:ng], rest[ng:ng + 3], rest[ng + 3:2 * ng + 3]
        (s0_ref, s1_ref, dp0_ref, dp1_ref), sems = rest[2 * ng + 3:2 * ng + 7], rest[2 * ng + 7:]
        if ng:
            _run_stages_at(_scatter_stages(c_ins, c_outs, *sems), [(0, 0), (nh - 1, nq - 1)])
        kj = pl.program_id(1)
        kv_, vv = k_ref[...], v_ref[...]
        s_refs, dp_refs = (s0_ref, s1_ref), (dp0_ref, dp1_ref)

        @pl.when(kj == 0)
        def _():
            dq_ref[...] = jnp.zeros_like(dq_ref)

        def rows(i):
            return pl.ds(pl.multiple_of(i * t, t), t)

        def issue(i, slot):
            s_refs[slot][...] = _dot_nt(kv_, q_ref[rows(i), :])
            dp_refs[slot][...] = _dot_nt(vv, do_ref[rows(i), :].astype(BF16))

        def consume(i, slot, carry, masked=False):
            dk, dv = carry
            p = jnp.exp2(s_refs[slot][...] - lse_ref[pl.ds(i, 1), :])
            if masked:
                p = jnp.where(lax.broadcasted_iota(jnp.int32, (t, t), 0) <= lax.broadcasted_iota(jnp.int32, (t, t), 1), p, 0.0)
            dv = dv + jnp.dot(p.astype(BF16), do_ref[rows(i), :].astype(BF16), preferred_element_type=F32)
            ds = (p * (dp_refs[slot][...] - delta_ref[i][0:1, :])).astype(BF16)
            dk = dk + jnp.dot(ds, q_ref[rows(i), :], preferred_element_type=F32)
            dq_ref[rows(i), :] += _dot_tn(ds, kv_) * MLA_SCALE
            return dk, dv

        issue(kj, 0)
        carry = consume(kj, 0, (jnp.zeros((t, MLA_QK), F32), jnp.zeros((t, MLA_V), F32)), masked=True)
        dk, dv = _pipelined_blocks(kj + 1, nq - 1 - kj, nq - 1, issue, consume, carry, False)
        dk_ref[...] = dk * LN2
        dv_ref[...] = dv

    tile = lambda w: pl.BlockSpec((None, t, w), lambda h, j: (h, j, 0))
    whole = lambda w: pl.BlockSpec((None, s, w), lambda h, j: (h, 0, 0))
    rows_spec = pl.BlockSpec((None, nq, t), lambda h, j: (h, 0, 0))
    outs = _pallas_call(
        body, name=name, grid=(nh, nq),
        in_specs=[whole(MLA_QK), tile(MLA_QK), tile(MLA_V), whole(MLA_V), rows_spec,
                  pl.BlockSpec((None, nq, 8, t), lambda h, j: (h, 0, 0, 0))] + [_HBM] * ng,
        out_specs=[whole(MLA_QK), tile(MLA_QK), tile(MLA_V)] + [_HBM] * ng,
        out_shape=[jax.ShapeDtypeStruct((nh, s, MLA_QK), F32), jax.ShapeDtypeStruct((nh, s, MLA_QK), F32),
                   jax.ShapeDtypeStruct((nh, s, MLA_V), F32)] + (_scatter_out_shapes(scatter) if ng else []),
        scratch_shapes=[pltpu.VMEM((t, t), F32)] * 4 + (_exchange_scratch(3, ng) if ng else []),
        compiler_params=_params(dimension_semantics=("arbitrary", "arbitrary")),
    )(q, k, v, do, lse_row, delta_row, *(scatter or []))
    return outs[0], outs[1], outs[2], outs[3:]


def _swa_tile(s):
    return min(s, 8 * BLOCK)


def _swa_specs(tq):
    nb = tq // BLOCK
    grp = lambda w: pl.BlockSpec((SWA_GROUP, tq, w), lambda j, i: (j, i, 0))
    main = pl.BlockSpec((None, tq, SWA_HEAD_DIM), lambda j, i: (j, i, 0))
    tail = pl.BlockSpec((None, BLOCK, SWA_HEAD_DIM), lambda j, i: (j, nb * (i + 1), 0))
    sink = pl.BlockSpec((None, SWA_GROUP, 128), lambda j, i: (j, 0, 0))
    return grp, main, tail, sink


def _swa_band_mask(first):
    shape = (SWA_GROUP * BLOCK, 2 * BLOCK)
    q_rel = (lax.broadcasted_iota(jnp.int32, shape, 0) & (BLOCK - 1)) + BLOCK
    k_rel = lax.broadcasted_iota(jnp.int32, shape, 1)
    dist = q_rel - k_rel
    return (dist >= 0) & (dist < BLOCK) & ((k_rel >= BLOCK) | jnp.logical_not(first))


def _swa_sink_column(sink_ref):
    sk = sink_ref[...]
    return jnp.concatenate([jnp.broadcast_to(sk[g:g + 1, 0:1], (BLOCK, 1)) for g in range(SWA_GROUP)], axis=0)


def _swa_fwd(q, kpad, vpad, sinks, name):
    _, s, _ = q.shape
    tq = _swa_tile(s)
    grp, main, tail, sink = _swa_specs(tq)
    d = SWA_HEAD_DIM

    def body(q_ref, km_ref, kt_ref, vm_ref, vt_ref, sink_ref, o_ref, lse_ref):
        i = pl.program_id(1)
        kall = jnp.concatenate([km_ref[...], kt_ref[...]], axis=0)
        vall = jnp.concatenate([vm_ref[...], vt_ref[...]], axis=0)
        sink_col = _swa_sink_column(sink_ref)
        for b in range(tq // BLOCK):
            lo = b * BLOCK
            valid = _swa_band_mask(i == 0 if b == 0 else False)
            q4 = q_ref[:, lo:lo + BLOCK, :].reshape(SWA_GROUP * BLOCK, d)
            sc = jnp.where(valid, _dot_nt(q4, kall[lo:lo + 2 * BLOCK]) * SWA_SCALE, NEG)
            m = jnp.maximum(jnp.max(sc, axis=-1, keepdims=True), sink_col)
            e = jnp.exp(sc - m)
            den = jnp.sum(e, axis=-1, keepdims=True) + jnp.exp(sink_col - m)
            out = jnp.dot((e * (1.0 / den)).astype(BF16), vall[lo:lo + 2 * BLOCK], preferred_element_type=F32)
            o_ref[:, lo:lo + BLOCK, :] = out.reshape(SWA_GROUP, BLOCK, d)
            lse_ref[:, lo:lo + BLOCK, :] = (m + jnp.log(den)).reshape(SWA_GROUP, BLOCK, 1)

    return _pallas_call(
        body, name=name, grid=(SWA_KV_HEADS, s // tq),
        in_specs=[grp(d), main, tail, main, tail, sink], out_specs=[grp(d), grp(1)],
        out_shape=[jax.ShapeDtypeStruct((SWA_HEADS, s, d), F32), jax.ShapeDtypeStruct((SWA_HEADS, s, 1), F32)],
        compiler_params=_params(dimension_semantics=("arbitrary", "arbitrary")),
    )(q, kpad, kpad, vpad, vpad, sinks)


def _swa_bwd(q, kpad, vpad, sinks, o, lse, do, name):
    _, s, _ = q.shape
    tq = _swa_tile(s)
    grp, main, tail, sink = _swa_specs(tq)
    d = SWA_HEAD_DIM

    def body(q_ref, km_ref, kt_ref, vm_ref, vt_ref, sink_ref, o_ref, lse_ref, do_ref, dq_ref, dk_ref, dv_ref, dsink_ref):
        i = pl.program_id(1)
        kall = jnp.concatenate([km_ref[...], kt_ref[...]], axis=0)
        vall = jnp.concatenate([vm_ref[...], vt_ref[...]], axis=0)
        sink_col = _swa_sink_column(sink_ref)

        @pl.when(i == 0)
        def _():
            dk_ref[...] = jnp.zeros_like(dk_ref)
            dv_ref[...] = jnp.zeros_like(dv_ref)
            dsink_ref[...] = jnp.zeros_like(dsink_ref)

        dsink = jnp.zeros((SWA_GROUP * BLOCK, 1), F32)
        for b in range(tq // BLOCK):
            lo = b * BLOCK
            valid = _swa_band_mask(i == 0 if b == 0 else False)
            rows4 = SWA_GROUP * BLOCK
            q4 = q_ref[:, lo:lo + BLOCK, :].reshape(rows4, d)
            do4 = do_ref[:, lo:lo + BLOCK, :].reshape(rows4, d)
            lse4 = lse_ref[:, lo:lo + BLOCK, :].reshape(rows4, 1)
            delta = jnp.sum(do4 * o_ref[:, lo:lo + BLOCK, :].reshape(rows4, d), axis=-1, keepdims=True)
            kb, vb = kall[lo:lo + 2 * BLOCK], vall[lo:lo + 2 * BLOCK]
            do4b = do4.astype(BF16)
            p = jnp.where(valid, jnp.exp(_dot_nt(q4, kb) * SWA_SCALE - lse4), 0.0)
            ds = (p * (_dot_nt(do4b, vb) - delta) * SWA_SCALE).astype(BF16)
            dq_ref[:, lo:lo + BLOCK, :] = jnp.dot(ds, kb, preferred_element_type=F32).reshape(SWA_GROUP, BLOCK, d)
            band = pl.ds(pl.multiple_of(i * tq, BLOCK) + lo, 2 * BLOCK)
            dk_ref[band, :] += _dot_tn(ds, q4)
            dv_ref[band, :] += _dot_tn(p.astype(BF16), do4b)
            dsink = dsink - jnp.exp(sink_col - lse4) * delta
        per_head = [jnp.broadcast_to(jnp.sum(dsink[g * BLOCK:(g + 1) * BLOCK], axis=0, keepdims=True), (1, 128))
                    for g in range(SWA_GROUP)]
        dsink_ref[...] += jnp.concatenate(per_head + [jnp.zeros((8 - SWA_GROUP, 128), F32)], axis=0)

    acc = pl.BlockSpec((None, s + BLOCK, d), lambda j, i: (j, 0, 0))
    return _pallas_call(
        body, name=name, grid=(SWA_KV_HEADS, s // tq),
        in_specs=[grp(d), main, tail, main, tail, sink, grp(d), grp(1), grp(d)],
        out_specs=[grp(d), acc, acc, pl.BlockSpec((None, 8, 128), lambda j, i: (j, 0, 0))],
        out_shape=[jax.ShapeDtypeStruct((SWA_HEADS, s, d), F32),
                   jax.ShapeDtypeStruct((SWA_KV_HEADS, s + BLOCK, d), F32),
                   jax.ShapeDtypeStruct((SWA_KV_HEADS, s + BLOCK, d), F32),
                   jax.ShapeDtypeStruct((SWA_KV_HEADS, 8, 128), F32)],
        compiler_params=_params(dimension_semantics=("arbitrary", "arbitrary")),
    )(q, kpad, kpad, vpad, vpad, sinks, o, lse, do)


def _loss_head(y, target, name):
    s = y.shape[0]
    tm = _row_tile(s, 512)

    def body(y_ref, t_ref, dy_ref, loss_ref):
        i = pl.program_id(0)
        err = y_ref[...] - t_ref[...]
        dy_ref[...] = err * (1.0 / D_MODEL)
        part = jnp.broadcast_to(0.5 * jnp.sum(jnp.mean(err * err, axis=-1, keepdims=True), axis=0, keepdims=True), (1, 128))

        @pl.when(i == 0)
        def _():
            loss_ref[...] = part

        @pl.when(i > 0)
        def _():
            loss_ref[...] += part

    row = pl.BlockSpec((tm, D_MODEL), lambda i: (i, 0))
    return _pallas_call(
        body, name=name, grid=(s // tm,), in_specs=[row, row], out_specs=[row, _full((1, 128))],
        out_shape=[jax.ShapeDtypeStruct((s, D_MODEL), F32), jax.ShapeDtypeStruct((1, 128), F32)],
        compiler_params=_params(dimension_semantics=("arbitrary",)),
    )(y, target)


def _adamw(w, g, m, v, name):
    rows, cols = w.shape
    tr = rows
    for cand in (512, 256, 128, 64, 32, 16, 8):
        if rows % cand == 0 and rows > cand:
            tr = cand
            break

    def body(w_ref, g_ref, m_ref, v_ref, d_ref, nm_ref, nv_ref):
        gv = g_ref[...]
        nm = ADAM_B1 * m_ref[...] + (1.0 - ADAM_B1) * gv
        nv = ADAM_B2 * v_ref[...] + (1.0 - ADAM_B2) * (gv * gv)
        m_hat = nm / (1.0 - ADAM_B1 ** ADAM_STEP)
        v_hat = nv / (1.0 - ADAM_B2 ** ADAM_STEP)
        d_ref[...] = -ADAM_LR * (m_hat / (jnp.sqrt(v_hat) + ADAM_EPS) + ADAM_WD * w_ref[...])
        nm_ref[...] = nm
        nv_ref[...] = nv

    blk = pl.BlockSpec((tr, cols), lambda i: (i, 0))
    return _pallas_call(
        body, name=name, grid=(rows // tr,), in_specs=[blk] * 4, out_specs=[blk] * 3,
        out_shape=[jax.ShapeDtypeStruct((rows, cols), F32)] * 3,
        compiler_params=_params(dimension_semantics=("arbitrary",)),
    )(w, g, m, v)


def _position():
    return lax.axis_index("x"), lax.axis_index("y"), lax.axis_index("c")


def _remote(src, dst, send_sems, recv_sems, k, to):
    return pltpu.make_async_remote_copy(src_ref=src, dst_ref=dst, send_sem=send_sems.at[k], recv_sem=recv_sems.at[k],
                                        device_id=to, device_id_type=MESH)


_HBM = pl.BlockSpec(memory_space=pltpu.HBM)


def _gather_stages(ins, outs, send_sems, recv_sems):
    na = len(ins)
    x, y, c = _position()
    me, sibling = (x, y, c), (x, y, 1 - c)
    xn, yn, dg = (1 - x, y), (x, 1 - y), (1 - x, 1 - y)

    def slot(a, chip, pc, half=None):
        ref = outs[a].at[4 * chip[0] + 2 * chip[1] + pc]
        if half is None:
            return ref
        rows = ref.shape[0] // 2
        return ref.at[pl.ds(half * rows, rows)]

    def cp(a, k, chip, pc, half, to, src=None):
        dst = slot(a, chip, pc, half)
        return _remote(dst if src is None else src, dst, send_sems, recv_sems, 8 * a + k, to)

    first_hop = [(0, xn), (1, yn)]
    second_hop = [(0, xn, 2, 0, yn), (1, yn, 3, 1, xn)]

    def sends():
        out = []
        for a in range(na):
            out += [cp(a, k, (x, y), c, None, (*to, c), src=ins[a].at[c]) for k, to in first_hop]
            out += [cp(a, fwd_k, frm, c, half, (*to, c)) for _, frm, fwd_k, half, to in second_hop]
            out += [cp(a, 4 + k, frm, c, None, sibling) for k, frm in first_hop]
            out += [cp(a, 6 + half, dg, c, half, sibling) for half in (0, 1)]
        return out

    def stage0():
        for a in range(na):
            for k, to in first_hop:
                cp(a, k, (x, y), c, None, (*to, c), src=ins[a].at[c]).start()

    def stage1():
        for k, frm, fwd_k, half, to in second_hop:
            for a in range(na):
                cp(a, k, frm, c, None, me).wait_recv()
                cp(a, fwd_k, frm, c, half, (*to, c)).start()
                cp(a, 4 + k, frm, c, None, sibling).start()

    def stage2():
        for half in (0, 1):
            for a in range(na):
                cp(a, 2 + half, dg, c, half, me).wait_recv()
                cp(a, 6 + half, dg, c, half, sibling).start()

    def stage3():
        for a in range(na):
            for k, chip, half in ((4, xn, None), (5, yn, None), (6, dg, 0), (7, dg, 1)):
                cp(a, k, chip, 1 - c, half, me).wait_recv()
        for sent in sends():
            sent.wait_send()

    return [stage0, stage1, stage2, stage3]


def _gather_out_shapes(mine):
    return [jax.ShapeDtypeStruct((N_DEV,) + m.shape[1:], m.dtype) for m in mine]


def _exchange_scratch(per_array, na):
    return [pltpu.SemaphoreType.DMA((per_array * na,)), pltpu.SemaphoreType.DMA((per_array * na,))]


def _all_gather_halves(mine, name):
    na = len(mine)

    def body(*refs):
        for stage in _gather_stages(refs[:na], refs[na:2 * na], *refs[2 * na:]):
            stage()

    return _pallas_call(body, name=name, in_specs=[_HBM] * na, out_specs=[_HBM] * na, out_shape=_gather_out_shapes(mine),
                        scratch_shapes=_exchange_scratch(8, na))(*mine)


def _sibling_exchange(parts, name, other_half):
    na = len(parts)

    def body(*refs):
        ins, outs, (send_sems, recv_sems) = refs[:na], refs[na:2 * na], refs[2 * na:]
        x, y, c = _position()
        copies = [_remote(ins[a].at[:, 1 - c] if other_half else ins[a], outs[a], send_sems, recv_sems, a, (x, y, 1 - c))
                  for a in range(na)]
        for cp in copies:
            cp.start()
        for cp in copies:
            cp.wait()

    return _pallas_call(
        body, name=name, in_specs=[_HBM] * na, out_specs=[_HBM] * na,
        out_shape=[jax.ShapeDtypeStruct(p.shape[:1] + p.shape[2:] if other_half else p.shape, p.dtype) for p in parts],
        scratch_shapes=_exchange_scratch(1, na),
    )(*parts)


def _scatter_stages(ins, outs, send_sems, recv_sems):
    na = len(ins)
    x, y, c = _position()
    chips = [(1 - x, y), (x, 1 - y), (1 - x, 1 - y)]

    def copies():
        return [_remote(ins[a].at[2 * px + py], outs[a].at[j], send_sems, recv_sems, 3 * a + j, (px, py, c))
                for a in range(na) for j, (px, py) in enumerate(chips)]

    def start():
        for cp in copies():
            cp.start()

    def wait():
        for cp in copies():
            cp.wait()

    return [start, wait]


def _scatter_out_shapes(parts):
    return [jax.ShapeDtypeStruct((3,) + p.shape[1:], p.dtype) for p in parts]


def _scatter_to_chips(parts, name):
    na = len(parts)

    def body(*refs):
        for stage in _scatter_stages(refs[:na], refs[na:2 * na], *refs[2 * na:]):
            stage()

    return _pallas_call(body, name=name, in_specs=[_HBM] * na, out_specs=[_HBM] * na, out_shape=_scatter_out_shapes(parts),
                        scratch_shapes=_exchange_scratch(3, na))(*parts)


def _assemble(g4, mine, name, side_by_side, order=None):
    _, nl, r, w = g4.shape
    tr = next(cand for cand in (256, 128, 64) if r % cand == 0)

    def body(in_ref, mine_ref, out_ref):
        chip = 2 * lax.axis_index("x") + lax.axis_index("y")
        blocks = [jnp.where(chip == sh, mine_ref[...], in_ref[sh]) for sh in range(N_SHARD)]
        if side_by_side:
            full = jnp.concatenate(blocks, axis=-1)
            out_ref[...] = full if order is None else _take_cols(full, order)
        else:
            for sh in range(N_SHARD):
                out_ref[sh] = blocks[sh]

    if side_by_side:
        out_spec = pl.BlockSpec((None, tr, N_SHARD * w), lambda l, i: (l, i, 0))
        out_shape = jax.ShapeDtypeStruct((nl, r, N_SHARD * w), g4.dtype)
    else:
        out_spec = pl.BlockSpec((None, N_SHARD, tr, w), lambda l, i: (l, 0, i, 0))
        out_shape = jax.ShapeDtypeStruct((nl, N_SHARD, r, w), g4.dtype)
    return _pallas_call(
        body, name=name, grid=(nl, r // tr),
        in_specs=[pl.BlockSpec((N_SHARD, None, tr, w), lambda l, i: (0, l, i, 0)),
                  pl.BlockSpec((None, tr, w), lambda l, i: (l, i, 0))],
        out_specs=out_spec, out_shape=out_shape,
        compiler_params=_params(dimension_semantics=("arbitrary", "arbitrary")),
    )(g4, mine)


def _all_reduce_small(vec, name):
    r, l = vec.shape

    def body(v_ref, out_ref, gath_ref, send_sems, recv_sems):
        x, y, c = _position()
        me = 4 * x + 2 * y + c
        gath_ref[me] = v_ref[...]
        copies = []
        for k in range(1, N_DEV):
            to = (x ^ (k >> 2), y ^ ((k >> 1) & 1), c ^ (k & 1))
            copies.append(_remote(gath_ref.at[me], gath_ref.at[me], send_sems, recv_sems, k - 1, to))
        for cp in copies:
            cp.start()
        for k in range(1, N_DEV):
            frm = 4 * (x ^ (k >> 2)) + 2 * (y ^ ((k >> 1) & 1)) + (c ^ (k & 1))
            _remote(gath_ref.at[frm], gath_ref.at[frm], send_sems, recv_sems, k - 1, (x, y, c)).wait_recv()
        for cp in copies:
            cp.wait_send()
        total = gath_ref[0]
        for d in range(1, N_DEV):
            total = total + gath_ref[d]
        out_ref[...] = total

    vm = pl.BlockSpec(memory_space=pltpu.VMEM)
    return _pallas_call(
        body, name=name, in_specs=[vm], out_specs=vm, out_shape=jax.ShapeDtypeStruct((r, l), F32),
        scratch_shapes=[pltpu.VMEM((N_DEV, r, l), F32), pltpu.SemaphoreType.DMA((N_DEV - 1,)),
                        pltpu.SemaphoreType.DMA((N_DEV - 1,))],
    )(vec)


def _sum_blocks(blocks, out_dtype, name):
    m, w = blocks[0].shape
    tr = next(cand for cand in (512, 256, 128, 64, 32) if m % cand == 0)

    def body(*refs):
        total = refs[0][...].astype(F32)
        for ref in refs[1:-1]:
            total = total + ref[...].astype(F32)
        refs[-1][...] = total.astype(out_dtype)

    blk = pl.BlockSpec((tr, w), lambda i: (i, 0))
    return _pallas_call(
        body, name=name, grid=(m // tr,), in_specs=[blk] * len(blocks), out_specs=blk,
        out_shape=jax.ShapeDtypeStruct((m, w), out_dtype),
        compiler_params=_params(dimension_semantics=("arbitrary",)),
    )(*blocks)


FIRST_GROUPS = (('ffn1_w_gate', 'ffn1_w_up', 'ffn1_w_down'),)
REST_GROUPS = (('ffn2_w_gate', 'ffn2_w_up', 'ffn2_w_down', 'w_o'), ('w_in',), ('mla_w_q_b',), ('mla_w_kv_b',))
N_FIRST = len(FIRST_GROUPS)


def _shard_rows(name):
    shape, axis = BIG[name]
    return shape[0] // N_SHARD if axis == 0 else shape[0]


def _group_row_offsets(group):
    return [int(v) for v in np.cumsum([0] + [_shard_rows(n) for n in group])]


def _rope_tables(s):
    pos = jnp.arange(s, dtype=F32)
    inv = 1.0 / (ROPE_THETA ** (jnp.arange(0, MLA_ROPE, 2, dtype=F32) / MLA_ROPE))
    ang = pos[:, None] * inv[None, :]
    return jnp.tile(jnp.cos(ang), (1, SWA_HEADS)), jnp.tile(jnp.sin(ang), (1, SWA_HEADS))


_MIXER_GAINS = ('mix_norm', 'mla_q_a_norm', 'mla_kv_a_norm', 'mla_q_norm', 'mla_k_norm', 'swa_q_norm', 'swa_k_norm')


def _local_step(x, target, small, ex):
    s = x.shape[0]
    cos, sin = _rope_tables(s)
    row = lambda name, l: small[name][l][None, :]
    saved, bigs = [], []
    for l in range(DEPTH):
        big = ex.first_weights(l)
        sv = {'x0': x}
        x, sv['g1'], sv['u1'], got = _ffn_fwd(x, row('ffn1_norm', l), big['ffn1_w_gate'], big['ffn1_w_up'], big['ffn1_w_down'],
                                             f"ffn1_fwd_{l}", ex.gather_behind_ffn1(l))
        big.update(ex.rest_weights(l, got))
        bigs.append(big)
        sv['x1'] = x
        gains = [row(n, l) for n in _MIXER_GAINS]
        mixer_w = (big['w_in'], big['mla_w_q_b'], big['mla_w_kv_b'])
        q_a, k_a, v_a, q_b, k_b, v_b = _pre_fwd(x, gains, *mixer_w, cos, sin, f"pre_fwd_{l}")
        o_a, lse, got = _mla_fwd(q_a, k_a, v_a, f"mla_fwd_{l}", ex.gather_behind_mla(l))
        ex.gathered_behind_mla(l, got)
        kpad = jnp.pad(k_b, ((0, 0), (BLOCK, 0), (0, 0)))
        vpad = jnp.pad(v_b, ((0, 0), (BLOCK, 0), (0, 0)))
        sinks = jnp.broadcast_to(small['swa_sinks'][l].reshape(SWA_KV_HEADS, SWA_GROUP, 1), (SWA_KV_HEADS, SWA_GROUP, 128))
        o_b, lse_b = _swa_fwd(q_b, kpad, vpad, sinks, f"swa_fwd_{l}")
        sv.update(gains=gains, mixer_w=mixer_w, q_a=q_a, k_a=k_a, v_a=v_a, q_b=q_b, kpad=kpad, vpad=vpad, sinks=sinks,
                  o_a=o_a, lse=lse, o_b=o_b, lse_b=lse_b)
        x = _post_fwd(x, o_a, o_b, row('mla_out_norm', l), row('swa_out_norm', l), big['w_o'], f"post_fwd_{l}")
        sv['x2'] = x
        x, sv['g2'], sv['u2'], _ = _ffn_fwd(x, row('ffn2_norm', l), big['ffn2_w_gate'], big['ffn2_w_up'], big['ffn2_w_down'],
                                           f"ffn2_fwd_{l}")
        saved.append(sv)

    dx, loss = _loss_head(x, target, "loss_head")

    gs = {n: [None] * DEPTH for n in SMALL_NAMES}
    t = _attn_tile(s)
    for l in reversed(range(DEPTH)):
        sv, big = saved[l], bigs[l]

        def ffn_back(tag, xin, dy, gate, up, scatter=None):
            (dxi, dgain, nb, act, dgate, dup), got = _ffn_bwd(xin, dy, gate, up, row(tag + '_norm', l), big[tag + '_w_gate'],
                                                             big[tag + '_w_up'], big[tag + '_w_down'], f"{tag}_bwd_{l}", scatter)
            gs[tag + '_norm'][l] = dgain[0]
            shards = lambda full: full.reshape(N_SHARD, D_FF // N_SHARD, D_MODEL)
            grads = {tag + '_w_gate': shards(_matmul_tn(dgate, nb, 1.0, f"{tag}_dwg_{l}")),
                     tag + '_w_up': shards(_matmul_tn(dup, nb, 1.0, f"{tag}_dwu_{l}")),
                     tag + '_w_down': shards(_matmul_tn(act, dy, 0.5, f"{tag}_dwd_{l}"))}
            return dxi, grads, got

        dx, rest_grads, _ = ffn_back('ffn2', sv['x2'], dx, sv['g2'], sv['u2'])
        do_a, do_b, dga, dgb, dwo = _post_bwd(dx, sv['o_a'], sv['o_b'], row('mla_out_norm', l), row('swa_out_norm', l),
                                              big['w_o'], f"post_bwd_{l}")
        gs['mla_out_norm'][l], gs['swa_out_norm'][l] = dga[0], dgb[0]
        rest_grads['w_o'] = dwo.reshape(N_SHARD, MIX_WIDTH // N_SHARD, D_MODEL)
        delta = _mla_delta(sv['o_a'], do_a, f"mla_delta_{l}")
        dq_a, dk_a, dv_a, got = _mla_bwd(sv['q_a'], sv['k_a'], sv['v_a'], do_a, sv['lse'].reshape(MLA_HEADS, s // t, t),
                                         delta, f"mla_bwd_{l}", ex.scatter_behind_mla(l))
        ex.scattered_behind_mla(l, got)
        dq_b, dkpad, dvpad, dsink = _swa_bwd(sv['q_b'], sv['kpad'], sv['vpad'], sv['sinks'], sv['o_b'], sv['lse_b'], do_b,
                                             f"swa_bwd_{l}")
        gs['swa_sinks'][l] = dsink[:, :SWA_GROUP, 0].reshape(SWA_HEADS)
        cts = [dq_a, dk_a, dv_a, dq_b, dkpad[:, BLOCK:], dvpad[:, BLOCK:]]
        outs = _pre_bwd(sv['x1'], dx, cts, sv['gains'], *sv['mixer_w'], cos, sin, f"pre_bwd_{l}")
        dx = outs[0]
        for n, val in zip(_MIXER_GAINS, outs[1:8]):
            gs[n][l] = val[0]
        rest_grads['w_in'], rest_grads['mla_w_q_b'], rest_grads['mla_w_kv_b'] = outs[8:11]
        ex.grads_ready(l, 'rest', rest_grads)
        dx, first_grads, got = ffn_back('ffn1', sv['x0'], dx, sv['g1'], sv['u1'], ex.scatter_behind_ffn1(l))
        ex.scattered_behind_ffn1(l, got)
        ex.grads_ready(l, 'first', first_grads)
    return loss, dx, gs


class _Exchange:
    def __init__(self, weights, c, chip):
        halves_of = lambda a: a.reshape(a.shape[:-2] + (2, a.shape[-2] // 2, a.shape[-1]))
        self.halves_of, self.c, self.chip = halves_of, c, chip
        self.mine = [[halves_of(jnp.concatenate([weights[n][l].astype(BF16) for n in group], axis=0))
                      for group in FIRST_GROUPS + REST_GROUPS] for l in range(DEPTH)]
        self.ahead, self.begun, self.received = {}, {}, {}

    def _assembled(self, l, which, gathered):
        groups, base = (FIRST_GROUPS, 0) if which == 'first' else (REST_GROUPS, N_FIRST)
        big = {}
        for gi, group in enumerate(groups):
            offs = _group_row_offsets(group)
            _, rh, w = self.mine[l][base + gi].shape
            col_sharded = BIG[group[0]][1] == 1
            full = _assemble(gathered[gi].reshape(N_SHARD, 1, 2 * rh, w), self.mine[l][base + gi].reshape(1, 2 * rh, w),
                             f"assemble_{which}{gi}_{l}", col_sharded, MIXER_ORDERS.get(group[0]))
            for i, n in enumerate(group):
                rows = offs[i + 1] - offs[i]
                if len(group) > 1 and offs[i] % rows == 0:
                    big[n] = ((full, (None, rows, N_SHARD * w), (0, offs[i] // rows, 0)) if col_sharded else
                              (full, (None, N_SHARD, rows, w), (0, 0, offs[i] // rows, 0)))
                elif col_sharded:
                    big[n] = full[0, offs[i]:offs[i + 1]]
                else:
                    big[n] = full[0, :, offs[i]:offs[i + 1]].reshape(BIG[n][0])
        return big

    def first_weights(self, l):
        got = self.ahead[l][:N_FIRST] if l in self.ahead else _all_gather_halves(self.mine[l][:N_FIRST], f"gather_first_{l}")
        return self._assembled(l, 'first', got)

    def gather_behind_ffn1(self, l):
        return None if l in self.ahead else self.mine[l][N_FIRST:]

    def rest_weights(self, l, got):
        return self._assembled(l, 'rest', self.ahead[l][N_FIRST:] if l in self.ahead else got)

    def gather_behind_mla(self, l):
        return self.mine[l + 1] if l + 1 < DEPTH else None

    def gathered_behind_mla(self, l, got):
        if got:
            self.ahead[l + 1] = got

    def grads_ready(self, l, which, grads):
        groups = FIRST_GROUPS if which == 'first' else REST_GROUPS
        parts = [self.halves_of(jnp.concatenate([grads[n] for n in group], axis=1)) for group in groups]
        from_sibling = _sibling_exchange(parts, f"swap_{which}_{l}", True)
        chip_sums = []
        for gi, (p, got) in enumerate(zip(parts, from_sibling)):
            kept = lax.dynamic_index_in_dim(p, self.c, axis=1, keepdims=False)
            rows = N_SHARD * p.shape[2]
            pair = _sum_blocks([kept.reshape(rows, -1), got.reshape(rows, -1)], BF16, f"sum_pair_{which}{gi}_{l}")
            chip_sums.append(pair.reshape(got.shape))
        self.begun[(l, which)] = chip_sums

    def scatter_behind_mla(self, l):
        return self.begun[(l + 1, 'first')] + self.begun[(l + 1, 'rest')] if l + 1 < DEPTH else None

    def scattered_behind_mla(self, l, got):
        if got:
            self.received[(l + 1, 'first')], self.received[(l + 1, 'rest')] = got[:N_FIRST], got[N_FIRST:]

    def scatter_behind_ffn1(self, l):
        return self.begun[(l, 'rest')] if l == 0 else None

    def scattered_behind_ffn1(self, l, got):
        if got:
            self.received[(l, 'rest')] = got

    def reduced(self):
        keys = sorted(self.begun)
        for key in keys:
            if key not in self.received:
                self.received[key] = _scatter_to_chips(self.begun[key], f"scatter_{key[1]}_{key[0]}")
        halves = []
        for l, which in keys:
            for gi, (cs, got) in enumerate(zip(self.begun[(l, which)], self.received[(l, which)])):
                own = lax.dynamic_index_in_dim(cs, self.chip, axis=0, keepdims=False)
                halves.append(_sum_blocks([own, got[0], got[1], got[2]], F32, f"sum_chips_{which}{gi}_{l}"))
        others = _sibling_exchange(halves, "share_halves", False)
        per_layer, at = {}, 0
        for l, which in keys:
            for group in (FIRST_GROUPS if which == 'first' else REST_GROUPS):
                mine_h, other_h = halves[at], others[at]
                at += 1
                full = jnp.where(self.c == 0, jnp.concatenate([mine_h, other_h]), jnp.concatenate([other_h, mine_h]))
                offs = _group_row_offsets(group)
                for i, n in enumerate(group):
                    per_layer[(n, l)] = full[offs[i]:offs[i + 1]]
        return {n: jnp.stack([per_layer[(n, l)] for l in range(DEPTH)]) for n in BIG_NAMES}


def kernel(x, ffn1_norm, ffn1_w_gate, ffn1_w_up, ffn1_w_down, mix_norm, w_in, mla_q_a_norm, mla_w_q_b, mla_kv_a_norm, mla_w_kv_b, mla_q_norm, mla_k_norm, swa_q_norm, swa_k_norm, swa_sinks, mla_out_norm, swa_out_norm, w_o, ffn2_norm, ffn2_w_gate, ffn2_w_up, ffn2_w_down, loss_target, m_ffn1_norm, m_ffn1_w_gate, m_ffn1_w_up, m_ffn1_w_down, m_mix_norm, m_w_in, m_mla_q_a_norm, m_mla_w_q_b, m_mla_kv_a_norm, m_mla_w_kv_b, m_mla_q_norm, m_mla_k_norm, m_swa_q_norm, m_swa_k_norm, m_swa_sinks, m_mla_out_norm, m_swa_out_norm, m_w_o, m_ffn2_norm, m_ffn2_w_gate, m_ffn2_w_up, m_ffn2_w_down, v_ffn1_norm, v_ffn1_w_gate, v_ffn1_w_up, v_ffn1_w_down, v_mix_norm, v_w_in, v_mla_q_a_norm, v_mla_w_q_b, v_mla_kv_a_norm, v_mla_w_kv_b, v_mla_q_norm, v_mla_k_norm, v_swa_q_norm, v_swa_k_norm, v_swa_sinks, v_mla_out_norm, v_swa_out_norm, v_w_o, v_ffn2_norm, v_ffn2_w_gate, v_ffn2_w_up, v_ffn2_w_down):
    args = dict(locals())
    transposed = lambda a: jnp.swapaxes(a, 1, 2)
    as_kernels_see = lambda n, a: transposed(a) if n in TRANSPOSED else a
    weights = {n: as_kernels_see(n, args[n]) for n in WEIGHT_NAMES}
    mom_m = {n: as_kernels_see(n, args["m_" + n]) for n in WEIGHT_NAMES}
    mom_v = {n: as_kernels_see(n, args["v_" + n]) for n in WEIGHT_NAMES}
    ex = _Exchange(weights, lax.axis_index("c"), 2 * lax.axis_index("x") + lax.axis_index("y"))
    loss, dx, gs = _local_step(x[0], loss_target[0], {n: weights[n] for n in SMALL_NAMES}, ex)

    small_flat = jnp.concatenate([jnp.stack(gs[n]).reshape(-1) for n in SMALL_NAMES] + [loss[0, :1]])
    n_small = small_flat.shape[0]
    lanes = -(-n_small // (8 * 128)) * 128
    small_sum = _all_reduce_small(jnp.pad(small_flat, (0, 8 * lanes - n_small)).reshape(8, lanes), "reduce_small").reshape(-1)
    grads = ex.reduced()
    off = 0
    for n in SMALL_NAMES:
        cnt = int(np.prod(weights[n].shape))
        grads[n] = small_sum[off:off + cnt].reshape(weights[n].shape)
        off += cnt
    loss_out = small_sum[off]

    deltas, new_m, new_v = {}, {}, {}
    for n in WEIGHT_NAMES:
        shp = weights[n].shape
        two_d = (DEPTH, shp[-1]) if len(shp) == 2 else (shp[0] * shp[1], shp[2])
        d, nm, nv = _adamw(weights[n].reshape(two_d), grads[n].reshape(two_d), mom_m[n].reshape(two_d),
                           mom_v[n].reshape(two_d), f"adamw_{n}")
        deltas[n], new_m[n], new_v[n] = d.reshape(shp), nm.reshape(shp), nv.reshape(shp)
    for n in TRANSPOSED:
        grads[n], deltas[n], new_m[n], new_v[n] = (transposed(a) for a in (grads[n], deltas[n], new_m[n], new_v[n]))

    return (loss_out, dx[None], *[grads[n] for n in WEIGHT_NAMES], *[deltas[n] for n in WEIGHT_NAMES],
            *[new_m[n] for n in WEIGHT_NAMES], *[new_v[n] for n in WEIGHT_NAMES])
```

```python
import functools

import numpy as np
import jax
import jax.numpy as jnp
from jax import lax
from jax.experimental import pallas as pl
from jax.experimental.pallas import tpu as pltpu

F32 = jnp.float32
BF16 = jnp.bfloat16

D_MODEL = 1024
DEPTH = 2
EPS = 1e-6
ROPE_THETA = 10000.0
BLOCK = 128
MLA_HEADS = 4
MLA_Q_RANK = 256
MLA_KV_RANK = 128
MLA_NOPE = 128
MLA_ROPE = 64
MLA_V = 128
MLA_QK = MLA_NOPE + MLA_ROPE
MLA_WIDTH = MLA_HEADS * MLA_V
SWA_HEADS = 8
SWA_KV_HEADS = 2
SWA_GROUP = SWA_HEADS // SWA_KV_HEADS
SWA_HEAD_DIM = 64
SWA_WIDTH = SWA_HEADS * SWA_HEAD_DIM
MIX_WIDTH = MLA_WIDTH + SWA_WIDTH
IN_SPLITS = (MLA_Q_RANK, MLA_KV_RANK, MLA_ROPE, SWA_WIDTH, SWA_KV_HEADS * SWA_HEAD_DIM, SWA_KV_HEADS * SWA_HEAD_DIM)
IN_COLS = sum(IN_SPLITS)
IN_OFFS = tuple(int(v) for v in np.cumsum((0,) + IN_SPLITS))
D_FF = 2816
MLA_SCALE = MLA_QK ** -0.5
LOG2E = 1.4426950408889634
LN2 = 0.6931471805599453
MLA_QSCALE = MLA_SCALE * LOG2E
SWA_SCALE = SWA_HEAD_DIM ** -0.5
NEG = -1e30

ADAM_LR = 0.001
ADAM_B1 = 0.9
ADAM_B2 = 0.999
ADAM_EPS = 1e-08
ADAM_WD = 0.01
ADAM_STEP = 10

N_SHARD = 4
N_DEV = 8
VMEM_LIMIT = 56 * 1024 * 1024
MESH = pl.DeviceIdType.MESH

WEIGHT_NAMES = ['ffn1_norm', 'ffn1_w_gate', 'ffn1_w_up', 'ffn1_w_down', 'mix_norm', 'w_in', 'mla_q_a_norm', 'mla_w_q_b',
                'mla_kv_a_norm', 'mla_w_kv_b', 'mla_q_norm', 'mla_k_norm', 'swa_q_norm', 'swa_k_norm', 'swa_sinks',
                'mla_out_norm', 'swa_out_norm', 'w_o', 'ffn2_norm', 'ffn2_w_gate', 'ffn2_w_up', 'ffn2_w_down']
TRANSPOSED = ('ffn1_w_gate', 'ffn1_w_up', 'ffn2_w_gate', 'ffn2_w_up')
BIG = {'ffn1_w_gate': ((D_FF, D_MODEL), 0), 'ffn1_w_up': ((D_FF, D_MODEL), 0), 'ffn1_w_down': ((D_FF, D_MODEL), 0),
       'w_in': ((D_MODEL, IN_COLS), 1), 'mla_w_q_b': ((MLA_Q_RANK, MLA_HEADS * MLA_QK), 1),
       'mla_w_kv_b': ((MLA_KV_RANK, MLA_HEADS * (MLA_NOPE + MLA_V)), 1), 'w_o': ((MIX_WIDTH, D_MODEL), 0),
       'ffn2_w_gate': ((D_FF, D_MODEL), 0), 'ffn2_w_up': ((D_FF, D_MODEL), 0), 'ffn2_w_down': ((D_FF, D_MODEL), 0)}
BIG_NAMES = [n for n in WEIGHT_NAMES if n in BIG]
SMALL_NAMES = [n for n in WEIGHT_NAMES if n not in BIG]

_pallas_call = pl.pallas_call


def _params(**kw):
    return pltpu.CompilerParams(vmem_limit_bytes=VMEM_LIMIT, **kw)


def _full(shape):
    n = len(shape)
    return pl.BlockSpec(shape, lambda *_: (0,) * n)


def _resident(shape):
    n = len(shape)
    return pl.BlockSpec(shape, lambda *_: (0,) * n, pipeline_mode=pl.Buffered(1))


@jax.custom_vjp
def _mm(a, w):
    return jnp.dot(a.astype(BF16), w, preferred_element_type=F32)


def _mm_fwd(a, w):
    return _mm(a, w), w


def _mm_bwd(w, dy):
    return lax.dot_general(dy.astype(BF16), w, (((1,), (1,)), ((), ())), preferred_element_type=F32), None


_mm.defvjp(_mm_fwd, _mm_bwd)


def _dot_nt(a, b):
    return lax.dot_general(a, b, (((1,), (1,)), ((), ())), preferred_element_type=F32)


def _dot_tn(a, b):
    return lax.dot_general(a, b, (((0,), (0,)), ((), ())), preferred_element_type=F32)


def _rms(t, g):
    return t * lax.rsqrt(jnp.mean(t * t, axis=-1, keepdims=True) + EPS) * g


def _sigmoid(z):
    return 1.0 / (1.0 + jnp.exp(-z))


def _row_tile(s, want):
    return min(want, s)


def _divisor_tile(rows, cap):
    return max(d for d in range(16, min(rows, cap) + 1, 16) if rows % d == 0)


FF_CHUNK = 1408


def _weight_operand(w):
    if isinstance(w, tuple):
        arr, block, index = w
        return arr, pl.BlockSpec(block, lambda *_: index, pipeline_mode=pl.Buffered(1))
    return w, _resident(w.shape)


def _weight_rows(ref, start, n):
    if len(ref.shape) == 2:
        return ref[start:start + n, :]
    per = ref.shape[1]
    return ref[start // per:(start + n) // per].reshape(n, ref.shape[2])


def _ffn_fwd(x, g, wg, wu, wd, name, gather=None):
    s = x.shape[0]
    tm = _row_tile(s, 256)
    steps = s // tm
    ng = len(gather) if gather else 0
    (wg, wg_spec), (wu, wu_spec), (wd, wd_spec) = _weight_operand(wg), _weight_operand(wu), _weight_operand(wd)

    def body(x_ref, g_ref, wg_ref, wu_ref, wd_ref, *rest):
        g_ins, (y_ref, gate_ref, up_ref), g_outs, sems = rest[:ng], rest[ng:ng + 3], rest[ng + 3:2 * ng + 3], rest[2 * ng + 3:]
        if ng:
            _run_stages_at(_gather_stages(g_ins, g_outs, *sems), [(0,), (steps * 3 // 8,), (steps * 11 // 16,), (steps - 1,)])
        xv = x_ref[...]
        nb = _rms(xv, g_ref[...]).astype(BF16)
        acc = xv
        for c in range(0, D_FF, FF_CHUNK):
            gate = _dot_nt(nb, _weight_rows(wg_ref, c, FF_CHUNK))
            up = _dot_nt(nb, _weight_rows(wu_ref, c, FF_CHUNK))
            gate_ref[:, c:c + FF_CHUNK] = gate.astype(BF16)
            up_ref[:, c:c + FF_CHUNK] = up.astype(BF16)
            act = (gate * _sigmoid(gate) * up).astype(BF16)
            acc = acc + 0.5 * jnp.dot(act, _weight_rows(wd_ref, c, FF_CHUNK), preferred_element_type=F32)
        y_ref[...] = acc

    outs = _pallas_call(
        body, name=name, grid=(steps,),
        in_specs=[pl.BlockSpec((tm, D_MODEL), lambda i: (i, 0)), _full((1, D_MODEL)), wg_spec, wu_spec, wd_spec] + [_HBM] * ng,
        out_specs=[pl.BlockSpec((tm, D_MODEL), lambda i: (i, 0)), pl.BlockSpec((tm, D_FF), lambda i: (i, 0)),
                   pl.BlockSpec((tm, D_FF), lambda i: (i, 0))] + [_HBM] * ng,
        out_shape=[jax.ShapeDtypeStruct((s, D_MODEL), F32), jax.ShapeDtypeStruct((s, D_FF), BF16),
                   jax.ShapeDtypeStruct((s, D_FF), BF16)] + (_gather_out_shapes(gather) if ng else []),
        scratch_shapes=_exchange_scratch(8, ng) if ng else [],
        compiler_params=_params(dimension_semantics=("arbitrary",)),
    )(x, g, wg, wu, wd, *(gather or []))
    return outs[0], outs[1], outs[2], outs[3:]


def _ffn_bwd(x, dy, gate, up, g, wg, wu, wd, name, scatter=None):
    s = x.shape[0]
    tm = _row_tile(s, 256)
    steps = s // tm
    ng = len(scatter) if scatter else 0
    (wg, wg_spec), (wu, wu_spec), (wd, wd_spec) = _weight_operand(wg), _weight_operand(wu), _weight_operand(wd)

    def body(x_ref, dy_ref, gate_ref, up_ref, g_ref, wg_ref, wu_ref, wd_ref, *rest):
        c_ins, (dx_ref, dgain_ref, n_ref, act_ref, dgate_ref, dup_ref) = rest[:ng], rest[ng:ng + 6]
        c_outs, sems = rest[ng + 6:2 * ng + 6], rest[2 * ng + 6:]
        if ng:
            _run_stages_at(_scatter_stages(c_ins, c_outs, *sems), [(0,), (steps - 1,)])
        i = pl.program_id(0)
        xv = x_ref[...]
        dyv = dy_ref[...]
        gv = g_ref[...]
        r = lax.rsqrt(jnp.mean(xv * xv, axis=-1, keepdims=True) + EPS)
        xh = xv * r
        n_ref[...] = (xh * gv).astype(BF16)
        dyh = (0.5 * dyv).astype(BF16)
        dn = jnp.zeros_like(xv)
        for c in range(0, D_FF, FF_CHUNK):
            dact = _dot_nt(dyh, _weight_rows(wd_ref, c, FF_CHUNK))
            gt = gate_ref[:, c:c + FF_CHUNK].astype(F32)
            u = up_ref[:, c:c + FF_CHUNK].astype(F32)
            sg = _sigmoid(gt)
            sl = gt * sg
            act_ref[:, c:c + FF_CHUNK] = (sl * u).astype(BF16)
            dup = (dact * sl).astype(BF16)
            dgate = (dact * u * (sg * (1.0 + gt * (1.0 - sg)))).astype(BF16)
            dup_ref[:, c:c + FF_CHUNK] = dup
            dgate_ref[:, c:c + FF_CHUNK] = dgate
            dn = (dn + jnp.dot(dgate, _weight_rows(wg_ref, c, FF_CHUNK), preferred_element_type=F32)
                  + jnp.dot(dup, _weight_rows(wu_ref, c, FF_CHUNK), preferred_element_type=F32))
        part = jnp.sum(dn * xh, axis=0, keepdims=True)

        @pl.when(i == 0)
        def _():
            dgain_ref[...] = part

        @pl.when(i > 0)
        def _():
            dgain_ref[...] += part

        dxh = dn * gv
        dx_ref[...] = dyv + r * (dxh - xh * jnp.mean(dxh * xh, axis=-1, keepdims=True))

    row = lambda w: pl.BlockSpec((tm, w), lambda i: (i, 0))
    outs = _pallas_call(
        body, name=name, grid=(steps,),
        in_specs=[row(D_MODEL), row(D_MODEL), row(D_FF), row(D_FF), _full((1, D_MODEL)), wg_spec, wu_spec, wd_spec]
        + [_HBM] * ng,
        out_specs=[row(D_MODEL), _full((1, D_MODEL)), row(D_MODEL), row(D_FF), row(D_FF), row(D_FF)] + [_HBM] * ng,
        out_shape=[jax.ShapeDtypeStruct((s, D_MODEL), F32), jax.ShapeDtypeStruct((1, D_MODEL), F32),
                   jax.ShapeDtypeStruct((s, D_MODEL), BF16), jax.ShapeDtypeStruct((s, D_FF), BF16),
                   jax.ShapeDtypeStruct((s, D_FF), BF16), jax.ShapeDtypeStruct((s, D_FF), BF16)]
        + (_scatter_out_shapes(scatter) if ng else []),
        scratch_shapes=_exchange_scratch(3, ng) if ng else [],
        compiler_params=_params(dimension_semantics=("arbitrary",)),
    )(x, dy, gate, up, g, wg, wu, wd, *(scatter or []))
    return outs[:6], outs[6:]


def _store_col_shards(o_ref, acc, first_shard, n_here, width):
    for q in range(n_here):
        o_ref[q] = acc[:, (first_shard + q) * width:(first_shard + q + 1) * width].astype(BF16)


def _matmul_tn(a, b, scale, name):
    t, m = a.shape
    n = b.shape[1]
    tk = _row_tile(t, 2048)
    tn = n // 2
    nk = t // tk

    def body(a_ref, b_ref, o_ref, acc_ref):
        k = pl.program_id(1)
        bv = b_ref[...]
        if scale != 1.0:
            bv = bv.astype(F32) * scale
        part = _dot_tn(a_ref[...].astype(BF16), bv.astype(BF16))

        @pl.when(k == 0)
        def _():
            acc_ref[...] = part

        @pl.when(k > 0)
        def _():
            acc_ref[...] += part

        @pl.when(k == nk - 1)
        def _():
            o_ref[...] = acc_ref[...].astype(BF16)

    return _pallas_call(
        body, name=name, grid=(n // tn, nk),
        in_specs=[pl.BlockSpec((tk, m), lambda j, k: (k, 0)), pl.BlockSpec((tk, tn), lambda j, k: (k, j))],
        out_specs=pl.BlockSpec((m, tn), lambda j, k: (0, j)), out_shape=jax.ShapeDtypeStruct((m, n), BF16),
        scratch_shapes=[pltpu.VMEM((m, tn), F32)],
        compiler_params=_params(dimension_semantics=("arbitrary", "arbitrary")),
    )(a, b)


_HALF = SWA_HEAD_DIM // 2
_IN_ORDER = (list(range(0, IN_OFFS[2]))
             + [IN_OFFS[3] + SWA_HEAD_DIM * h + i for h in range(SWA_HEADS) for i in range(_HALF)]
             + [IN_OFFS[3] + SWA_HEAD_DIM * h + _HALF + i for h in range(SWA_HEADS) for i in range(_HALF)]
             + list(range(IN_OFFS[5], IN_OFFS[6]))
             + [IN_OFFS[4] + SWA_HEAD_DIM * j + i for j in range(SWA_KV_HEADS) for i in range(_HALF)]
             + [IN_OFFS[4] + SWA_HEAD_DIM * j + _HALF + i for j in range(SWA_KV_HEADS) for i in range(_HALF)]
             + list(range(IN_OFFS[2], IN_OFFS[3])))
_QB_ORDER = ([MLA_QK * h + i for h in range(MLA_HEADS) for i in range(MLA_NOPE)]
             + [MLA_QK * h + MLA_NOPE + i for h in range(MLA_HEADS) for i in range(_HALF)]
             + [MLA_QK * h + MLA_NOPE + _HALF + i for h in range(MLA_HEADS) for i in range(_HALF)])
_KVB_ORDER = ([(MLA_NOPE + MLA_V) * h + i for h in range(MLA_HEADS) for i in range(MLA_NOPE)]
              + [(MLA_NOPE + MLA_V) * h + MLA_NOPE + i for h in range(MLA_HEADS) for i in range(MLA_V)])
MIXER_ORDERS = {'w_in': _IN_ORDER, 'mla_w_q_b': _QB_ORDER, 'mla_w_kv_b': _KVB_ORDER}
_P_CQ, _P_CKV, _P_QA, _P_QB, _P_VS, _P_KA, _P_KB, _P_PE = (int(v) for v in np.cumsum(
    (0, MLA_Q_RANK, MLA_KV_RANK, SWA_WIDTH // 2, SWA_WIDTH // 2, IN_SPLITS[5], IN_SPLITS[4] // 2, IN_SPLITS[4] // 2)))


def _runs(order):
    out, start = [], 0
    for i in range(1, len(order) + 1):
        if i == len(order) or order[i] != order[i - 1] + 1:
            out.append((order[start], i - start))
            start = i
    return out


def _inverse(order):
    inv = [0] * len(order)
    for new, old in enumerate(order):
        inv[old] = new
    return inv


def _take_cols(a, order):
    return jnp.concatenate([a[..., st:st + w] for st, w in _runs(order)], axis=-1)


def _segment_matrix(n, seg):
    return (lax.broadcasted_iota(jnp.int32, (n, n), 0) // seg == lax.broadcasted_iota(jnp.int32, (n, n), 1) // seg).astype(BF16)


@jax.custom_vjp
def _cmm(t, b, bt):
    hi = t.astype(BF16)
    lo = (t - hi.astype(F32)).astype(BF16)
    return jnp.dot(hi, b, preferred_element_type=F32) + jnp.dot(lo, b, preferred_element_type=F32)


def _cmm_fwd(t, b, bt):
    return _cmm(t, b, bt), (b, bt)


def _cmm_bwd(res, dy):
    b, bt = res
    return _cmm(dy, bt, b), None, None


_cmm.defvjp(_cmm_fwd, _cmm_bwd)


def _segsum(t, b):
    return _cmm(t, b, b)


def _rowsum(t):
    n = t.shape[-1]
    return _cmm(t, jnp.ones((n, 128), BF16), jnp.ones((128, n), BF16))


def _by_head(vals, width):
    lane = lax.broadcasted_iota(jnp.int32, (vals[0].shape[0], len(vals) * width), 1)
    out = vals[-1]
    for hd in range(len(vals) - 2, -1, -1):
        out = jnp.where(lane < (hd + 1) * width, vals[hd], out)
    return out


def _rope2(a, b, cos, sin):
    return a * cos - b * sin, b * cos + a * sin


def _pre_math(x, gm, gqa, gkva, gq, gk, gsq, gsk, taps, win, wqb, wkvb, cos, sin):
    h = _rms(x, gm)
    proj = _mm(h, win)
    if taps is not None:
        proj = proj + taps[0]
    cqn = _rms(proj[:, _P_CQ:_P_CKV], gqa)
    qa_all = _mm(cqn, wqb)
    ckvn = _rms(proj[:, _P_CKV:_P_QA], gkva)
    kv_all = _mm(ckvn, wkvb)
    if taps is not None:
        qa_all = qa_all + taps[1]
        kv_all = kv_all + taps[2]
    nh, hw = MLA_HEADS, MLA_HEADS * _HALF
    seg_mla = _segment_matrix(hw, _HALF)
    tile = lambda g, n: jnp.concatenate([g] * n, axis=-1)
    c4, s4 = cos[:, :hw], sin[:, :hw]

    def mla_heads(nope, r1, r2, gain):
        rr = r1 * r1 + r2 * r2
        lane_head = lax.broadcasted_iota(jnp.int32, (hw, nh * MLA_NOPE), 0) // _HALF
        spread = (lane_head == lax.broadcasted_iota(jnp.int32, (hw, nh * MLA_NOPE), 1) // MLA_NOPE).astype(BF16)
        rope_on_nope = _cmm(rr, spread, spread.T)
        ss_nope = [_rowsum(jnp.square(nope[:, hd * MLA_NOPE:(hd + 1) * MLA_NOPE])) for hd in range(nh)]
        rinv = [lax.rsqrt((ss_nope[hd] + rope_on_nope[:, hd * MLA_NOPE:(hd + 1) * MLA_NOPE]) * (1.0 / MLA_QK) + EPS)
                for hd in range(nh)]
        rl = lax.rsqrt((_segsum(rr, seg_mla) + _by_head(ss_nope, _HALF)) * (1.0 / MLA_QK) + EPS)
        o1, o2 = _rope2(r1 * rl * tile(gain[:, MLA_NOPE:MLA_NOPE + _HALF], nh), r2 * rl * tile(gain[:, MLA_NOPE + _HALF:], nh), c4, s4)
        return [jnp.concatenate([nope[:, hd * MLA_NOPE:(hd + 1) * MLA_NOPE] * rinv[hd] * gain[:, :MLA_NOPE],
                                 o1[:, hd * _HALF:(hd + 1) * _HALF], o2[:, hd * _HALF:(hd + 1) * _HALF]], axis=-1)
                for hd in range(nh)]

    q_a = mla_heads(qa_all[:, :nh * MLA_NOPE], qa_all[:, nh * MLA_NOPE:nh * MLA_NOPE + hw], qa_all[:, nh * MLA_NOPE + hw:], gq)
    pe1, pe2 = proj[:, _P_PE:_P_PE + _HALF], proj[:, _P_PE + _HALF:_P_PE + 2 * _HALF]
    k_a = mla_heads(kv_all[:, :nh * MLA_NOPE], tile(pe1, nh), tile(pe2, nh), gk)
    v_a = [kv_all[:, nh * MLA_NOPE + hd * MLA_V:nh * MLA_NOPE + (hd + 1) * MLA_V] for hd in range(nh)]

    def swa_heads(a, b, gain, n):
        w = n * _HALF
        r = lax.rsqrt(_segsum(a * a + b * b, _segment_matrix(w, _HALF)) * (1.0 / SWA_HEAD_DIM) + EPS)
        o1, o2 = _rope2(a * r * tile(gain[:, :_HALF], n), b * r * tile(gain[:, _HALF:], n), cos[:, :w], sin[:, :w])
        return [jnp.concatenate([o1[:, hd * _HALF:(hd + 1) * _HALF], o2[:, hd * _HALF:(hd + 1) * _HALF]], axis=-1) for hd in range(n)]

    q_b = swa_heads(proj[:, _P_QA:_P_QB], proj[:, _P_QB:_P_VS], gsq, SWA_HEADS)
    k_b = swa_heads(proj[:, _P_KA:_P_KB], proj[:, _P_KB:_P_PE], gsk, SWA_KV_HEADS)
    v_b = [proj[:, _P_VS + j * SWA_HEAD_DIM:_P_VS + (j + 1) * SWA_HEAD_DIM] for j in range(SWA_KV_HEADS)]
    return (q_a, k_a, v_a, q_b, k_b, v_b), (h, cqn, ckvn)


_PRE_GAIN_WIDTHS = (D_MODEL, MLA_Q_RANK, MLA_KV_RANK, MLA_QK, MLA_QK, SWA_HEAD_DIM, SWA_HEAD_DIM)
_PRE_HEADS = ((MLA_HEADS, MLA_QK), (MLA_HEADS, MLA_QK), (MLA_HEADS, MLA_V),
              (SWA_HEADS, SWA_HEAD_DIM), (SWA_KV_HEADS, SWA_HEAD_DIM), (SWA_KV_HEADS, SWA_HEAD_DIM))


def _pre_fwd(x, gains, win, wqb, wkvb, cos, sin, name):
    s = x.shape[0]
    tm = _row_tile(s, 512)

    def body(x_ref, *refs):
        g_refs, (win_ref, wqb_ref, wkvb_ref, cos_ref, sin_ref), out_refs = refs[:7], refs[7:12], refs[12:]
        outs, _ = _pre_math(x_ref[...], *[g[...] for g in g_refs], None, win_ref[...], wqb_ref[...], wkvb_ref[...],
                            cos_ref[...], sin_ref[...])
        for idx, (ref, heads) in enumerate(zip(out_refs, outs)):
            for hd, val in enumerate(heads):
                ref[hd] = (val * MLA_QSCALE if idx == 0 else val).astype(BF16)

    heads_spec = lambda nh, w: pl.BlockSpec((nh, tm, w), lambda i: (0, i, 0))
    return _pallas_call(
        body, name=name, grid=(s // tm,),
        in_specs=[pl.BlockSpec((tm, D_MODEL), lambda i: (i, 0))] + [_full((1, w)) for w in _PRE_GAIN_WIDTHS]
        + [_resident(win.shape), _resident(wqb.shape), _resident(wkvb.shape),
           pl.BlockSpec((tm, SWA_HEADS * _HALF), lambda i: (i, 0)), pl.BlockSpec((tm, SWA_HEADS * _HALF), lambda i: (i, 0))],
        out_specs=[heads_spec(nh, w) for nh, w in _PRE_HEADS],
        out_shape=[jax.ShapeDtypeStruct((nh, s, w), BF16) for nh, w in _PRE_HEADS],
        compiler_params=_params(dimension_semantics=("arbitrary",)),
    )(x, *gains, win, wqb, wkvb, cos, sin)


def _pre_bwd(x, dx_res, cts, gains, win, wqb, wkvb, cos, sin, name):
    s = x.shape[0]
    tm = _row_tile(s, 256)
    tap_widths = (IN_COLS, MLA_HEADS * MLA_QK, MLA_HEADS * (MLA_NOPE + MLA_V))

    def body(x_ref, dxr_ref, *refs):
        ct_refs, g_refs = refs[:6], refs[6:13]
        win_ref, wqb_ref, wkvb_ref, cos_ref, sin_ref = refs[13:18]
        dx_ref, dg_refs, dw_refs, acc_refs = refs[18], refs[19:26], refs[26:29], refs[29:32]
        i = pl.program_id(0)
        win_v, wqb_v, wkvb_v, cos_v, sin_v = win_ref[...], wqb_ref[...], wkvb_ref[...], cos_ref[...], sin_ref[...]

        def f(xv, gm, gqa, gkva, gq, gk, gsq, gsk, t0, t1, t2):
            return _pre_math(xv, gm, gqa, gkva, gq, gk, gsq, gsk, (t0, t1, t2), win_v, wqb_v, wkvb_v, cos_v, sin_v)

        taps = [jnp.zeros((tm, w), F32) for w in tap_widths]
        _, vjp, acts = jax.vjp(f, x_ref[...], *[g[...] for g in g_refs], *taps, has_aux=True)
        ct = tuple([ref[hd] for hd in range(nh)] for ref, (nh, _) in zip(ct_refs, _PRE_HEADS))
        grads = vjp(ct)
        dx_ref[...] = grads[0] + dxr_ref[...]
        dws = [_dot_tn(a.astype(BF16), t.astype(BF16)) for a, t in zip(acts, grads[8:11])]

        @pl.when(i == 0)
        def _():
            for ref, val in zip(dg_refs, grads[1:8]):
                ref[...] = val
            for ref, val in zip(acc_refs, dws):
                ref[...] = val

        @pl.when(i > 0)
        def _():
            for ref, val in zip(dg_refs, grads[1:8]):
                ref[...] += val
            for ref, val in zip(acc_refs, dws):
                ref[...] += val

        @pl.when(i == s // tm - 1)
        def _():
            for ref, acc, order in zip(dw_refs, acc_refs, (_IN_ORDER, _QB_ORDER, _KVB_ORDER)):
                _store_col_shards(ref, _take_cols(acc[...], _inverse(order)), 0, N_SHARD, acc.shape[1] // N_SHARD)

    heads_spec = lambda nh, w: pl.BlockSpec((nh, tm, w), lambda i: (0, i, 0))
    row = pl.BlockSpec((tm, D_MODEL), lambda i: (i, 0))
    half = pl.BlockSpec((tm, SWA_HEADS * _HALF), lambda i: (i, 0))
    shard_shapes = [(N_SHARD, w.shape[0], w.shape[1] // N_SHARD) for w in (win, wqb, wkvb)]
    return _pallas_call(
        body, name=name, grid=(s // tm,),
        in_specs=[row, row] + [heads_spec(nh, w) for nh, w in _PRE_HEADS] + [_full((1, w)) for w in _PRE_GAIN_WIDTHS]
        + [_resident(win.shape), _resident(wqb.shape), _resident(wkvb.shape), half, half],
        out_specs=[row] + [_full((1, w)) for w in _PRE_GAIN_WIDTHS] + [_full(shp) for shp in shard_shapes],
        out_shape=[jax.ShapeDtypeStruct((s, D_MODEL), F32)] + [jax.ShapeDtypeStruct((1, w), F32) for w in _PRE_GAIN_WIDTHS]
        + [jax.ShapeDtypeStruct(shp, BF16) for shp in shard_shapes],
        scratch_shapes=[pltpu.VMEM(w.shape, F32) for w in (win, wqb, wkvb)],
        compiler_params=_params(dimension_semantics=("arbitrary",)),
    )(x, dx_res, *cts, *gains, win, wqb, wkvb, cos, sin)


def _post_math(oa, ob, ga, gb, wo):
    mixed = jnp.concatenate([_rms(jnp.concatenate(oa, axis=-1), ga), _rms(jnp.concatenate(ob, axis=-1), gb)], axis=-1)
    return _mm(mixed, wo), mixed


def _post_fwd(x, oa, ob, ga, gb, wo, name):
    s = x.shape[0]
    tm = _row_tile(s, 512)

    def body(x_ref, oa_ref, ob_ref, ga_ref, gb_ref, wo_ref, y_ref):
        y, _ = _post_math([oa_ref[hd] for hd in range(MLA_HEADS)], [ob_ref[hd] for hd in range(SWA_HEADS)],
                          ga_ref[...], gb_ref[...], wo_ref[...])
        y_ref[...] = x_ref[...] + y

    row = pl.BlockSpec((tm, D_MODEL), lambda i: (i, 0))
    return _pallas_call(
        body, name=name, grid=(s // tm,),
        in_specs=[row, pl.BlockSpec((MLA_HEADS, tm, MLA_V), lambda i: (0, i, 0)),
                  pl.BlockSpec((SWA_HEADS, tm, SWA_HEAD_DIM), lambda i: (0, i, 0)),
                  _full((1, MLA_WIDTH)), _full((1, SWA_WIDTH)), _resident(wo.shape)],
        out_specs=row, out_shape=jax.ShapeDtypeStruct((s, D_MODEL), F32),
        compiler_params=_params(dimension_semantics=("arbitrary",)),
    )(x, oa, ob, ga, gb, wo)


def _post_bwd(dy, oa, ob, ga, gb, wo, name):
    s = dy.shape[0]
    tm = _row_tile(s, 512)

    def body(dy_ref, oa_ref, ob_ref, ga_ref, gb_ref, wo_ref, doa_ref, dob_ref, dga_ref, dgb_ref, dwo_ref, acc_ref):
        i = pl.program_id(0)
        wo_v = wo_ref[...]
        dyv = dy_ref[...]

        def f(oa_l, ob_l, ga_v, gb_v):
            return _post_math(oa_l, ob_l, ga_v, gb_v, wo_v)

        _, vjp, mixed = jax.vjp(f, [oa_ref[hd] for hd in range(MLA_HEADS)], [ob_ref[hd] for hd in range(SWA_HEADS)],
                                ga_ref[...], gb_ref[...], has_aux=True)
        doa, dob, dga, dgb = vjp(dyv)
        for hd in range(MLA_HEADS):
            doa_ref[hd] = doa[hd]
        for hd in range(SWA_HEADS):
            dob_ref[hd] = dob[hd]
        dwo = _dot_tn(mixed.astype(BF16), dyv.astype(BF16))

        @pl.when(i == 0)
        def _():
            dga_ref[...] = dga
            dgb_ref[...] = dgb
            acc_ref[...] = dwo

        @pl.when(i > 0)
        def _():
            dga_ref[...] += dga
            dgb_ref[...] += dgb
            acc_ref[...] += dwo

        @pl.when(i == s // tm - 1)
        def _():
            dwo_ref[...] = acc_ref[...].astype(BF16)

    row = pl.BlockSpec((tm, D_MODEL), lambda i: (i, 0))
    oa_spec = pl.BlockSpec((MLA_HEADS, tm, MLA_V), lambda i: (0, i, 0))
    ob_spec = pl.BlockSpec((SWA_HEADS, tm, SWA_HEAD_DIM), lambda i: (0, i, 0))
    return _pallas_call(
        body, name=name, grid=(s // tm,),
        in_specs=[row, oa_spec, ob_spec, _full((1, MLA_WIDTH)), _full((1, SWA_WIDTH)), _resident(wo.shape)],
        out_specs=[oa_spec, ob_spec, _full((1, MLA_WIDTH)), _full((1, SWA_WIDTH)), _full(wo.shape)],
        out_shape=[jax.ShapeDtypeStruct((MLA_HEADS, s, MLA_V), F32), jax.ShapeDtypeStruct((SWA_HEADS, s, SWA_HEAD_DIM), F32),
                   jax.ShapeDtypeStruct((1, MLA_WIDTH), F32), jax.ShapeDtypeStruct((1, SWA_WIDTH), F32),
                   jax.ShapeDtypeStruct(wo.shape, BF16)],
        scratch_shapes=[pltpu.VMEM(wo.shape, F32)],
        compiler_params=_params(dimension_semantics=("arbitrary",)),
    )(dy, oa, ob, ga, gb, wo)


def _attn_tile(s):
    return 512 if s >= 2048 else 128


def _causal_mask(t):
    return lax.broadcasted_iota(jnp.int32, (t, t), 1) <= lax.broadcasted_iota(jnp.int32, (t, t), 0)


def _pipelined_blocks(first, count, last_block, issue, consume, carry, prefetch_after):
    def clamped(j, slot):
        issue(jnp.minimum(j, last_block), slot)

    def pair(jj, c):
        a = first + 2 * jj
        clamped(a + 1, 1)
        c = consume(a, 0, c)
        clamped(a + 2, 0)
        return consume(a + 1, 1, c)

    clamped(first, 0)
    npairs = count // 2
    carry = lax.fori_loop(0, npairs, pair, carry)

    def odd(c):
        c = consume(first + 2 * npairs, 0, c)
        if prefetch_after:
            clamped(first + count, 0)
        return c

    return lax.cond(count - 2 * npairs == 1, odd, lambda c: c, carry)


def _run_stages_at(stages, steps):
    for stage, step in zip(stages, steps):
        here = pl.program_id(0) == step[0]
        for axis in range(1, len(step)):
            here = here & (pl.program_id(axis) == step[axis])
        pl.when(here)(stage)


def _mla_fwd(q, k, v, name, gather=None):
    nh, s, _ = q.shape
    t = _attn_tile(s)
    nq = s // t
    ng = len(gather) if gather else 0

    def body(q_ref, k_ref, v_ref, *rest):
        g_ins, (o_ref, lse_ref), g_outs = rest[:ng], rest[ng:ng + 2], rest[ng + 2:2 * ng + 2]
        (s0_ref, s1_ref), sems = rest[2 * ng + 2:2 * ng + 4], rest[2 * ng + 4:]
        if ng:
            _run_stages_at(_gather_stages(g_ins, g_outs, *sems), [(0, 0), (nh // 2, 0), (nh - 1, 0), (nh - 1, nq - 1)])
        qi = pl.program_id(1)
        qv = q_ref[...]
        s_refs = (s0_ref, s1_ref)

        def rows(j):
            return pl.ds(pl.multiple_of(j * t, t), t)

        def issue(j, slot):
            s_refs[slot][...] = _dot_nt(qv, k_ref[rows(j), :])

        def consume(j, slot, carry, masked=False):
            m, l, acc = carry
            sc = s_refs[slot][...]
            if masked:
                sc = jnp.where(_causal_mask(t), sc, NEG)
            m_new = jnp.maximum(m, jnp.max(sc, axis=-1, keepdims=True))
            alpha = jnp.exp2(m - m_new)
            p = jnp.exp2(sc - m_new)
            l = alpha * l + jnp.sum(p, axis=-1, keepdims=True)
            acc = alpha * acc + jnp.dot(p.astype(BF16), v_ref[rows(j), :], preferred_element_type=F32)
            return m_new, l, acc

        init = (jnp.full((t, 1), NEG, F32), jnp.zeros((t, 1), F32), jnp.zeros((t, MLA_V), F32))
        carry = _pipelined_blocks(0, qi, nq - 1, issue, consume, init, True)
        m, l, acc = consume(qi, 0, carry, masked=True)
        o_ref[...] = acc / l
        lse_ref[...] = m + jnp.log2(l)

    outs = _pallas_call(
        body, name=name, grid=(nh, nq),
        in_specs=[pl.BlockSpec((None, t, MLA_QK), lambda h, i: (h, i, 0)), pl.BlockSpec((None, s, MLA_QK), lambda h, i: (h, 0, 0)),
                  pl.BlockSpec((None, s, MLA_V), lambda h, i: (h, 0, 0))] + [_HBM] * ng,
        out_specs=[pl.BlockSpec((None, t, MLA_V), lambda h, i: (h, i, 0)), pl.BlockSpec((None, t, 1), lambda h, i: (h, i, 0))]
        + [_HBM] * ng,
        out_shape=[jax.ShapeDtypeStruct((nh, s, MLA_V), F32), jax.ShapeDtypeStruct((nh, s, 1), F32)]
        + (_gather_out_shapes(gather) if ng else []),
        scratch_shapes=[pltpu.VMEM((t, t), F32)] * 2 + (_exchange_scratch(8, ng) if ng else []),
        compiler_params=_params(dimension_semantics=("arbitrary", "arbitrary")),
    )(q, k, v, *(gather or []))
    return outs[0], outs[1], outs[2:]


def _mla_delta(o, do, name):
    nh, s, _ = o.shape
    t = _attn_tile(s)

    def body(o_ref, do_ref, delta_ref):
        prod = do_ref[...] * o_ref[...]
        hi = prod.astype(BF16)
        lo = (prod - hi.astype(F32)).astype(BF16)
        ones = jnp.ones((8, MLA_V), BF16)
        delta_ref[...] = _dot_nt(ones, hi) + _dot_nt(ones, lo)

    tile = pl.BlockSpec((None, t, MLA_V), lambda h, i: (h, i, 0))
    return _pallas_call(body, name=name, grid=(nh, s // t), in_specs=[tile, tile],
                        out_specs=pl.BlockSpec((None, None, 8, t), lambda h, i: (h, i, 0, 0)),
                        out_shape=jax.ShapeDtypeStruct((nh, s // t, 8, t), F32),
                        compiler_params=_params(dimension_semantics=("arbitrary", "arbitrary")))(o, do)


def _mla_bwd(q, k, v, do, lse_row, delta_row, name, scatter=None):
    nh, s, _ = q.shape
    t = _attn_tile(s)
    nq = s // t
    ng = len(scatter) if scatter else 0

    def body(q_ref, k_ref, v_ref, do_ref, lse_ref, delta_ref, *rest):
        c_ins, (dq_ref, dk_ref, dv_ref), c_outs = rest[:ng], rest[ng:ng + 3], rest[ng + 3:2 * ng + 3]
        (s0_ref, s1_ref, dp0_ref, dp1_ref), sems = rest[2 * ng + 3:2 * ng + 7], rest[2 * ng + 7:]
        if ng:
            _run_stages_at(_scatter_stages(c_ins, c_outs, *sems), [(0, 0), (nh - 1, nq - 1)])
        kj = pl.program_id(1)
        kv_, vv = k_ref[...], v_ref[...]
        s_refs, dp_refs = (s0_ref, s1_ref), (dp0_ref, dp1_ref)

        @pl.when(kj == 0)
        def _():
            dq_ref[...] = jnp.zeros_like(dq_ref)

        def rows(i):
            return pl.ds(pl.multiple_of(i * t, t), t)

        def issue(i, slot):
            s_refs[slot][...] = _dot_nt(kv_, q_ref[rows(i), :])
            dp_refs[slot][...] = _dot_nt(vv, do_ref[rows(i), :].astype(BF16))

        def consume(i, slot, carry, masked=False):
            dk, dv = carry
            p = jnp.exp2(s_refs[slot][...] - lse_ref[pl.ds(i, 1), :])
            if masked:
                p = jnp.where(lax.broadcasted_iota(jnp.int32, (t, t), 0) <= lax.broadcasted_iota(jnp.int32, (t, t), 1), p, 0.0)
            dv = dv + jnp.dot(p.astype(BF16), do_ref[rows(i), :].astype(BF16), preferred_element_type=F32)
            ds = (p * (dp_refs[slot][...] - delta_ref[i][0:1, :])).astype(BF16)
            dk = dk + jnp.dot(ds, q_ref[rows(i), :], preferred_element_type=F32)
            dq_ref[rows(i), :] += _dot_tn(ds, kv_) * MLA_SCALE
            return dk, dv

        issue(kj, 0)
        carry = consume(kj, 0, (jnp.zeros((t, MLA_QK), F32), jnp.zeros((t, MLA_V), F32)), masked=True)
        dk, dv = _pipelined_blocks(kj + 1, nq - 1 - kj, nq - 1, issue, consume, carry, False)
        dk_ref[...] = dk * LN2
        dv_ref[...] = dv

    tile = lambda w: pl.BlockSpec((None, t, w), lambda h, j: (h, j, 0))
    whole = lambda w: pl.BlockSpec((None, s, w), lambda h, j: (h, 0, 0))
    rows_spec = pl.BlockSpec((None, nq, t), lambda h, j: (h, 0, 0))
    outs = _pallas_call(
        body, name=name, grid=(nh, nq),
        in_specs=[whole(MLA_QK), tile(MLA_QK), tile(MLA_V), whole(MLA_V), rows_spec,
                  pl.BlockSpec((None, nq, 8, t), lambda h, j: (h, 0, 0, 0))] + [_HBM] * ng,
        out_specs=[whole(MLA_QK), tile(MLA_QK), tile(MLA_V)] + [_HBM] * ng,
        out_shape=[jax.ShapeDtypeStruct((nh, s, MLA_QK), F32), jax.ShapeDtypeStruct((nh, s, MLA_QK), F32),
                   jax.ShapeDtypeStruct((nh, s, MLA_V), F32)] + (_scatter_out_shapes(scatter) if ng else []),
        scratch_shapes=[pltpu.VMEM((t, t), F32)] * 4 + (_exchange_scratch(3, ng) if ng else []),
        compiler_params=_params(dimension_semantics=("arbitrary", "arbitrary")),
    )(q, k, v, do, lse_row, delta_row, *(scatter or []))
    return outs[0], outs[1], outs[2], outs[3:]


def _swa_tile(s):
    return min(s, 8 * BLOCK)


def _swa_specs(tq):
    nb = tq // BLOCK
    grp = lambda w: pl.BlockSpec((SWA_GROUP, tq, w), lambda j, i: (j, i, 0))
    main = pl.BlockSpec((None, tq, SWA_HEAD_DIM), lambda j, i: (j, i, 0))
    tail = pl.BlockSpec((None, BLOCK, SWA_HEAD_DIM), lambda j, i: (j, nb * (i + 1), 0))
    sink = pl.BlockSpec((None, SWA_GROUP, 128), lambda j, i: (j, 0, 0))
    return grp, main, tail, sink


def _swa_band_mask(first):
    shape = (SWA_GROUP * BLOCK, 2 * BLOCK)
    q_rel = (lax.broadcasted_iota(jnp.int32, shape, 0) & (BLOCK - 1)) + BLOCK
    k_rel = lax.broadcasted_iota(jnp.int32, shape, 1)
    dist = q_rel - k_rel
    return (dist >= 0) & (dist < BLOCK) & ((k_rel >= BLOCK) | jnp.logical_not(first))


def _swa_sink_column(sink_ref):
    sk = sink_ref[...]
    return jnp.concatenate([jnp.broadcast_to(sk[g:g + 1, 0:1], (BLOCK, 1)) for g in range(SWA_GROUP)], axis=0)


def _swa_fwd(q, kpad, vpad, sinks, name):
    _, s, _ = q.shape
    tq = _swa_tile(s)
    grp, main, tail, sink = _swa_specs(tq)
    d = SWA_HEAD_DIM

    def body(q_ref, km_ref, kt_ref, vm_ref, vt_ref, sink_ref, o_ref, lse_ref):
        i = pl.program_id(1)
        kall = jnp.concatenate([km_ref[...], kt_ref[...]], axis=0)
        vall = jnp.concatenate([vm_ref[...], vt_ref[...]], axis=0)
        sink_col = _swa_sink_column(sink_ref)
        for b in range(tq // BLOCK):
            lo = b * BLOCK
            valid = _swa_band_mask(i == 0 if b == 0 else False)
            q4 = q_ref[:, lo:lo + BLOCK, :].reshape(SWA_GROUP * BLOCK, d)
            sc = jnp.where(valid, _dot_nt(q4, kall[lo:lo + 2 * BLOCK]) * SWA_SCALE, NEG)
            m = jnp.maximum(jnp.max(sc, axis=-1, keepdims=True), sink_col)
            e = jnp.exp(sc - m)
            den = jnp.sum(e, axis=-1, keepdims=True) + jnp.exp(sink_col - m)
            out = jnp.dot((e * (1.0 / den)).astype(BF16), vall[lo:lo + 2 * BLOCK], preferred_element_type=F32)
            o_ref[:, lo:lo + BLOCK, :] = out.reshape(SWA_GROUP, BLOCK, d)
            lse_ref[:, lo:lo + BLOCK, :] = (m + jnp.log(den)).reshape(SWA_GROUP, BLOCK, 1)

    return _pallas_call(
        body, name=name, grid=(SWA_KV_HEADS, s // tq),
        in_specs=[grp(d), main, tail, main, tail, sink], out_specs=[grp(d), grp(1)],
        out_shape=[jax.ShapeDtypeStruct((SWA_HEADS, s, d), F32), jax.ShapeDtypeStruct((SWA_HEADS, s, 1), F32)],
        compiler_params=_params(dimension_semantics=("arbitrary", "arbitrary")),
    )(q, kpad, kpad, vpad, vpad, sinks)


def _swa_bwd(q, kpad, vpad, sinks, o, lse, do, name):
    _, s, _ = q.shape
    tq = _swa_tile(s)
    grp, main, tail, sink = _swa_specs(tq)
    d = SWA_HEAD_DIM

    def body(q_ref, km_ref, kt_ref, vm_ref, vt_ref, sink_ref, o_ref, lse_ref, do_ref, dq_ref, dk_ref, dv_ref, dsink_ref):
        i = pl.program_id(1)
        kall = jnp.concatenate([km_ref[...], kt_ref[...]], axis=0)
        vall = jnp.concatenate([vm_ref[...], vt_ref[...]], axis=0)
        sink_col = _swa_sink_column(sink_ref)

        @pl.when(i == 0)
        def _():
            dk_ref[...] = jnp.zeros_like(dk_ref)
            dv_ref[...] = jnp.zeros_like(dv_ref)
            dsink_ref[...] = jnp.zeros_like(dsink_ref)

        dsink = jnp.zeros((SWA_GROUP * BLOCK, 1), F32)
        for b in range(tq // BLOCK):
            lo = b * BLOCK
            valid = _swa_band_mask(i == 0 if b == 0 else False)
            rows4 = SWA_GROUP * BLOCK
            q4 = q_ref[:, lo:lo + BLOCK, :].reshape(rows4, d)
            do4 = do_ref[:, lo:lo + BLOCK, :].reshape(rows4, d)
            lse4 = lse_ref[:, lo:lo + BLOCK, :].reshape(rows4, 1)
            delta = jnp.sum(do4 * o_ref[:, lo:lo + BLOCK, :].reshape(rows4, d), axis=-1, keepdims=True)
            kb, vb = kall[lo:lo + 2 * BLOCK], vall[lo:lo + 2 * BLOCK]
            do4b = do4.astype(BF16)
            p = jnp.where(valid, jnp.exp(_dot_nt(q4, kb) * SWA_SCALE - lse4), 0.0)
            ds = (p * (_dot_nt(do4b, vb) - delta) * SWA_SCALE).astype(BF16)
            dq_ref[:, lo:lo + BLOCK, :] = jnp.dot(ds, kb, preferred_element_type=F32).reshape(SWA_GROUP, BLOCK, d)
            band = pl.ds(pl.multiple_of(i * tq, BLOCK) + lo, 2 * BLOCK)
            dk_ref[band, :] += _dot_tn(ds, q4)
            dv_ref[band, :] += _dot_tn(p.astype(BF16), do4b)
            dsink = dsink - jnp.exp(sink_col - lse4) * delta
        per_head = [jnp.broadcast_to(jnp.sum(dsink[g * BLOCK:(g + 1) * BLOCK], axis=0, keepdims=True), (1, 128))
                    for g in range(SWA_GROUP)]
        dsink_ref[...] += jnp.concatenate(per_head + [jnp.zeros((8 - SWA_GROUP, 128), F32)], axis=0)

    acc = pl.BlockSpec((None, s + BLOCK, d), lambda j, i: (j, 0, 0))
    return _pallas_call(
        body, name=name, grid=(SWA_KV_HEADS, s // tq),
        in_specs=[grp(d), main, tail, main, tail, sink, grp(d), grp(1), grp(d)],
        out_specs=[grp(d), acc, acc, pl.BlockSpec((None, 8, 128), lambda j, i: (j, 0, 0))],
        out_shape=[jax.ShapeDtypeStruct((SWA_HEADS, s, d), F32),
                   jax.ShapeDtypeStruct((SWA_KV_HEADS, s + BLOCK, d), F32),
                   jax.ShapeDtypeStruct((SWA_KV_HEADS, s + BLOCK, d), F32),
                   jax.ShapeDtypeStruct((SWA_KV_HEADS, 8, 128), F32)],
        compiler_params=_params(dimension_semantics=("arbitrary", "arbitrary")),
    )(q, kpad, kpad, vpad, vpad, sinks, o, lse, do)


def _loss_head(y, target, name):
    s = y.shape[0]
    tm = _row_tile(s, 512)

    def body(y_ref, t_ref, dy_ref, loss_ref):
        i = pl.program_id(0)
        err = y_ref[...] - t_ref[...]
        dy_ref[...] = err * (1.0 / D_MODEL)
        part = jnp.broadcast_to(0.5 * jnp.sum(jnp.mean(err * err, axis=-1, keepdims=True), axis=0, keepdims=True), (1, 128))

        @pl.when(i == 0)
        def _():
            loss_ref[...] = part

        @pl.when(i > 0)
        def _():
            loss_ref[...] += part

    row = pl.BlockSpec((tm, D_MODEL), lambda i: (i, 0))
    return _pallas_call(
        body, name=name, grid=(s // tm,), in_specs=[row, row], out_specs=[row, _full((1, 128))],
        out_shape=[jax.ShapeDtypeStruct((s, D_MODEL), F32), jax.ShapeDtypeStruct((1, 128), F32)],
        compiler_params=_params(dimension_semantics=("arbitrary",)),
    )(y, target)


def _adamw(w, g, m, v, name):
    rows, cols = w.shape
    tr = rows
    for cand in (512, 256, 128, 64, 32, 16, 8):
        if rows % cand == 0 and rows > cand:
            tr = cand
            break

    def body(w_ref, g_ref, m_ref, v_ref, d_ref, nm_ref, nv_ref):
        gv = g_ref[...]
        nm = ADAM_B1 * m_ref[...] + (1.0 - ADAM_B1) * gv
        nv = ADAM_B2 * v_ref[...] + (1.0 - ADAM_B2) * (gv * gv)
        m_hat = nm / (1.0 - ADAM_B1 ** ADAM_STEP)
        v_hat = nv / (1.0 - ADAM_B2 ** ADAM_STEP)
        d_ref[...] = -ADAM_LR * (m_hat / (jnp.sqrt(v_hat) + ADAM_EPS) + ADAM_WD * w_ref[...])
        nm_ref[...] = nm
        nv_ref[...] = nv

    blk = pl.BlockSpec((tr, cols), lambda i: (i, 0))
    return _pallas_call(
        body, name=name, grid=(rows // tr,), in_specs=[blk] * 4, out_specs=[blk] * 3,
        out_shape=[jax.ShapeDtypeStruct((rows, cols), F32)] * 3,
        compiler_params=_params(dimension_semantics=("arbitrary",)),
    )(w, g, m, v)


def _position():
    return lax.axis_index("x"), lax.axis_index("y"), lax.axis_index("c")


def _remote(src, dst, send_sems, recv_sems, k, to):
    return pltpu.make_async_remote_copy(src_ref=src, dst_ref=dst, send_sem=send_sems.at[k], recv_sem=recv_sems.at[k],
                                        device_id=to, device_id_type=MESH)


_HBM = pl.BlockSpec(memory_space=pltpu.HBM)


def _gather_stages(ins, outs, send_sems, recv_sems):
    na = len(ins)
    x, y, c = _position()
    me, sibling = (x, y, c), (x, y, 1 - c)
    xn, yn, dg = (1 - x, y), (x, 1 - y), (1 - x, 1 - y)

    def slot(a, chip, pc, half=None):
        ref = outs[a].at[4 * chip[0] + 2 * chip[1] + pc]
        if half is None:
            return ref
        rows = ref.shape[0] // 2
        return ref.at[pl.ds(half * rows, rows)]

    def cp(a, k, chip, pc, half, to, src=None):
        dst = slot(a, chip, pc, half)
        return _remote(dst if src is None else src, dst, send_sems, recv_sems, 8 * a + k, to)

    first_hop = [(0, xn), (1, yn)]
    second_hop = [(0, xn, 2, 0, yn), (1, yn, 3, 1, xn)]

    def sends():
        out = []
        for a in range(na):
            out += [cp(a, k, (x, y), c, None, (*to, c), src=ins[a].at[c]) for k, to in first_hop]
            out += [cp(a, fwd_k, frm, c, half, (*to, c)) for _, frm, fwd_k, half, to in second_hop]
            out += [cp(a, 4 + k, frm, c, None, sibling) for k, frm in first_hop]
            out += [cp(a, 6 + half, dg, c, half, sibling) for half in (0, 1)]
        return out

    def stage0():
        for a in range(na):
            for k, to in first_hop:
                cp(a, k, (x, y), c, None, (*to, c), src=ins[a].at[c]).start()

    def stage1():
        for k, frm, fwd_k, half, to in second_hop:
            for a in range(na):
                cp(a, k, frm, c, None, me).wait_recv()
                cp(a, fwd_k, frm, c, half, (*to, c)).start()
                cp(a, 4 + k, frm, c, None, sibling).start()

    def stage2():
        for half in (0, 1):
            for a in range(na):
                cp(a, 2 + half, dg, c, half, me).wait_recv()
                cp(a, 6 + half, dg, c, half, sibling).start()

    def stage3():
        for a in range(na):
            for k, chip, half in ((4, xn, None), (5, yn, None), (6, dg, 0), (7, dg, 1)):
                cp(a, k, chip, 1 - c, half, me).wait_recv()
        for sent in sends():
            sent.wait_send()

    return [stage0, stage1, stage2, stage3]


def _gather_out_shapes(mine):
    return [jax.ShapeDtypeStruct((N_DEV,) + m.shape[1:], m.dtype) for m in mine]


def _exchange_scratch(per_array, na):
    return [pltpu.SemaphoreType.DMA((per_array * na,)), pltpu.SemaphoreType.DMA((per_array * na,))]


def _all_gather_halves(mine, name):
    na = len(mine)

    def body(*refs):
        for stage in _gather_stages(refs[:na], refs[na:2 * na], *refs[2 * na:]):
            stage()

    return _pallas_call(body, name=name, in_specs=[_HBM] * na, out_specs=[_HBM] * na, out_shape=_gather_out_shapes(mine),
                        scratch_shapes=_exchange_scratch(8, na))(*mine)


def _sibling_exchange(parts, name, other_half):
    na = len(parts)

    def body(*refs):
        ins, outs, (send_sems, recv_sems) = refs[:na], refs[na:2 * na], refs[2 * na:]
        x, y, c = _position()
        copies = [_remote(ins[a].at[:, 1 - c] if other_half else ins[a], outs[a], send_sems, recv_sems, a, (x, y, 1 - c))
                  for a in range(na)]
        for cp in copies:
            cp.start()
        for cp in copies:
            cp.wait()

    return _pallas_call(
        body, name=name, in_specs=[_HBM] * na, out_specs=[_HBM] * na,
        out_shape=[jax.ShapeDtypeStruct(p.shape[:1] + p.shape[2:] if other_half else p.shape, p.dtype) for p in parts],
        scratch_shapes=_exchange_scratch(1, na),
    )(*parts)


def _scatter_stages(ins, outs, send_sems, recv_sems):
    na = len(ins)
    x, y, c = _position()
    chips = [(1 - x, y), (x, 1 - y), (1 - x, 1 - y)]

    def copies():
        return [_remote(ins[a].at[2 * px + py], outs[a].at[j], send_sems, recv_sems, 3 * a + j, (px, py, c))
                for a in range(na) for j, (px, py) in enumerate(chips)]

    def start():
        for cp in copies():
            cp.start()

    def wait():
        for cp in copies():
            cp.wait()

    return [start, wait]


def _scatter_out_shapes(parts):
    return [jax.ShapeDtypeStruct((3,) + p.shape[1:], p.dtype) for p in parts]


def _scatter_to_chips(parts, name):
    na = len(parts)

    def body(*refs):
        for stage in _scatter_stages(refs[:na], refs[na:2 * na], *refs[2 * na:]):
            stage()

    return _pallas_call(body, name=name, in_specs=[_HBM] * na, out_specs=[_HBM] * na, out_shape=_scatter_out_shapes(parts),
                        scratch_shapes=_exchange_scratch(3, na))(*parts)


def _assemble(g4, mine, name, side_by_side, order=None):
    _, nl, r, w = g4.shape
    tr = _divisor_tile(r, 640)

    def body(in_ref, mine_ref, out_ref):
        chip = 2 * lax.axis_index("x") + lax.axis_index("y")
        blocks = [jnp.where(chip == sh, mine_ref[...], in_ref[sh]) for sh in range(N_SHARD)]
        if side_by_side:
            full = jnp.concatenate(blocks, axis=-1)
            out_ref[...] = full if order is None else _take_cols(full, order)
        else:
            for sh in range(N_SHARD):
                out_ref[sh] = blocks[sh]

    if side_by_side:
        out_spec = pl.BlockSpec((None, tr, N_SHARD * w), lambda l, i: (l, i, 0))
        out_shape = jax.ShapeDtypeStruct((nl, r, N_SHARD * w), g4.dtype)
    else:
        out_spec = pl.BlockSpec((None, N_SHARD, tr, w), lambda l, i: (l, 0, i, 0))
        out_shape = jax.ShapeDtypeStruct((nl, N_SHARD, r, w), g4.dtype)
    return _pallas_call(
        body, name=name, grid=(nl, r // tr),
        in_specs=[pl.BlockSpec((N_SHARD, None, tr, w), lambda l, i: (0, l, i, 0)),
                  pl.BlockSpec((None, tr, w), lambda l, i: (l, i, 0))],
        out_specs=out_spec, out_shape=out_shape,
        compiler_params=_params(dimension_semantics=("arbitrary", "arbitrary")),
    )(g4, mine)


def _all_reduce_small(vec, name):
    r, l = vec.shape

    def body(v_ref, out_ref, gath_ref, send_sems, recv_sems):
        x, y, c = _position()
        me = 4 * x + 2 * y + c
        gath_ref[me] = v_ref[...]
        copies = []
        for k in range(1, N_DEV):
            to = (x ^ (k >> 2), y ^ ((k >> 1) & 1), c ^ (k & 1))
            copies.append(_remote(gath_ref.at[me], gath_ref.at[me], send_sems, recv_sems, k - 1, to))
        for cp in copies:
            cp.start()
        for k in range(1, N_DEV):
            frm = 4 * (x ^ (k >> 2)) + 2 * (y ^ ((k >> 1) & 1)) + (c ^ (k & 1))
            _remote(gath_ref.at[frm], gath_ref.at[frm], send_sems, recv_sems, k - 1, (x, y, c)).wait_recv()
        for cp in copies:
            cp.wait_send()
        total = gath_ref[0]
        for d in range(1, N_DEV):
            total = total + gath_ref[d]
        out_ref[...] = total

    vm = pl.BlockSpec(memory_space=pltpu.VMEM)
    return _pallas_call(
        body, name=name, in_specs=[vm], out_specs=vm, out_shape=jax.ShapeDtypeStruct((r, l), F32),
        scratch_shapes=[pltpu.VMEM((N_DEV, r, l), F32), pltpu.SemaphoreType.DMA((N_DEV - 1,)),
                        pltpu.SemaphoreType.DMA((N_DEV - 1,))],
    )(vec)


def _sum_blocks(blocks, out_dtype, name):
    m, w = blocks[0].shape
    tr = _divisor_tile(m, 1024)

    def body(*refs):
        total = refs[0][...].astype(F32)
        for ref in refs[1:-1]:
            total = total + ref[...].astype(F32)
        refs[-1][...] = total.astype(out_dtype)

    blk = pl.BlockSpec((tr, w), lambda i: (i, 0))
    return _pallas_call(
        body, name=name, grid=(m // tr,), in_specs=[blk] * len(blocks), out_specs=blk,
        out_shape=jax.ShapeDtypeStruct((m, w), out_dtype),
        compiler_params=_params(dimension_semantics=("arbitrary",)),
    )(*blocks)


FIRST_GROUPS = (('ffn1_w_gate', 'ffn1_w_up', 'ffn1_w_down'),)
REST_GROUPS = (('ffn2_w_gate', 'ffn2_w_up', 'ffn2_w_down', 'w_o'), ('w_in',), ('mla_w_q_b',), ('mla_w_kv_b',))
N_FIRST = len(FIRST_GROUPS)


def _shard_rows(name):
    shape, axis = BIG[name]
    return shape[0] // N_SHARD if axis == 0 else shape[0]


def _group_row_offsets(group):
    return [int(v) for v in np.cumsum([0] + [_shard_rows(n) for n in group])]


def _rope_tables(s):
    pos = jnp.arange(s, dtype=F32)
    inv = 1.0 / (ROPE_THETA ** (jnp.arange(0, MLA_ROPE, 2, dtype=F32) / MLA_ROPE))
    ang = pos[:, None] * inv[None, :]
    return jnp.tile(jnp.cos(ang), (1, SWA_HEADS)), jnp.tile(jnp.sin(ang), (1, SWA_HEADS))


_MIXER_GAINS = ('mix_norm', 'mla_q_a_norm', 'mla_kv_a_norm', 'mla_q_norm', 'mla_k_norm', 'swa_q_norm', 'swa_k_norm')


def _local_step(x, target, small, ex):
    s = x.shape[0]
    cos, sin = _rope_tables(s)
    row = lambda name, l: small[name][l][None, :]
    saved, bigs = [], []
    for l in range(DEPTH):
        big = ex.first_weights(l)
        sv = {'x0': x}
        x, sv['g1'], sv['u1'], got = _ffn_fwd(x, row('ffn1_norm', l), big['ffn1_w_gate'], big['ffn1_w_up'], big['ffn1_w_down'],
                                             f"ffn1_fwd_{l}", ex.gather_behind_ffn1(l))
        big.update(ex.rest_weights(l, got))
        bigs.append(big)
        sv['x1'] = x
        gains = [row(n, l) for n in _MIXER_GAINS]
        mixer_w = (big['w_in'], big['mla_w_q_b'], big['mla_w_kv_b'])
        q_a, k_a, v_a, q_b, k_b, v_b = _pre_fwd(x, gains, *mixer_w, cos, sin, f"pre_fwd_{l}")
        o_a, lse, got = _mla_fwd(q_a, k_a, v_a, f"mla_fwd_{l}", ex.gather_behind_mla(l))
        ex.gathered_behind_mla(l, got)
        kpad = jnp.pad(k_b, ((0, 0), (BLOCK, 0), (0, 0)))
        vpad = jnp.pad(v_b, ((0, 0), (BLOCK, 0), (0, 0)))
        sinks = jnp.broadcast_to(small['swa_sinks'][l].reshape(SWA_KV_HEADS, SWA_GROUP, 1), (SWA_KV_HEADS, SWA_GROUP, 128))
        o_b, lse_b = _swa_fwd(q_b, kpad, vpad, sinks, f"swa_fwd_{l}")
        sv.update(gains=gains, mixer_w=mixer_w, q_a=q_a, k_a=k_a, v_a=v_a, q_b=q_b, kpad=kpad, vpad=vpad, sinks=sinks,
                  o_a=o_a, lse=lse, o_b=o_b, lse_b=lse_b)
        x = _post_fwd(x, o_a, o_b, row('mla_out_norm', l), row('swa_out_norm', l), big['w_o'], f"post_fwd_{l}")
        sv['x2'] = x
        x, sv['g2'], sv['u2'], _ = _ffn_fwd(x, row('ffn2_norm', l), big['ffn2_w_gate'], big['ffn2_w_up'], big['ffn2_w_down'],
                                           f"ffn2_fwd_{l}")
        saved.append(sv)

    dx, loss = _loss_head(x, target, "loss_head")

    gs = {n: [None] * DEPTH for n in SMALL_NAMES}
    t = _attn_tile(s)
    for l in reversed(range(DEPTH)):
        sv, big = saved[l], bigs[l]

        def ffn_back(tag, xin, dy, gate, up, scatter=None):
            (dxi, dgain, nb, act, dgate, dup), got = _ffn_bwd(xin, dy, gate, up, row(tag + '_norm', l), big[tag + '_w_gate'],
                                                             big[tag + '_w_up'], big[tag + '_w_down'], f"{tag}_bwd_{l}", scatter)
            gs[tag + '_norm'][l] = dgain[0]
            shards = lambda full: full.reshape(N_SHARD, D_FF // N_SHARD, D_MODEL)
            grads = {tag + '_w_gate': shards(_matmul_tn(dgate, nb, 1.0, f"{tag}_dwg_{l}")),
                     tag + '_w_up': shards(_matmul_tn(dup, nb, 1.0, f"{tag}_dwu_{l}")),
                     tag + '_w_down': shards(_matmul_tn(act, dy, 0.5, f"{tag}_dwd_{l}"))}
            return dxi, grads, got

        dx, rest_grads, _ = ffn_back('ffn2', sv['x2'], dx, sv['g2'], sv['u2'])
        do_a, do_b, dga, dgb, dwo = _post_bwd(dx, sv['o_a'], sv['o_b'], row('mla_out_norm', l), row('swa_out_norm', l),
                                              big['w_o'], f"post_bwd_{l}")
        gs['mla_out_norm'][l], gs['swa_out_norm'][l] = dga[0], dgb[0]
        rest_grads['w_o'] = dwo.reshape(N_SHARD, MIX_WIDTH // N_SHARD, D_MODEL)
        delta = _mla_delta(sv['o_a'], do_a, f"mla_delta_{l}")
        dq_a, dk_a, dv_a, got = _mla_bwd(sv['q_a'], sv['k_a'], sv['v_a'], do_a, sv['lse'].reshape(MLA_HEADS, s // t, t),
                                         delta, f"mla_bwd_{l}", ex.scatter_behind_mla(l))
        ex.scattered_behind_mla(l, got)
        dq_b, dkpad, dvpad, dsink = _swa_bwd(sv['q_b'], sv['kpad'], sv['vpad'], sv['sinks'], sv['o_b'], sv['lse_b'], do_b,
                                             f"swa_bwd_{l}")
        gs['swa_sinks'][l] = dsink[:, :SWA_GROUP, 0].reshape(SWA_HEADS)
        cts = [dq_a, dk_a, dv_a, dq_b, dkpad[:, BLOCK:], dvpad[:, BLOCK:]]
        outs = _pre_bwd(sv['x1'], dx, cts, sv['gains'], *sv['mixer_w'], cos, sin, f"pre_bwd_{l}")
        dx = outs[0]
        for n, val in zip(_MIXER_GAINS, outs[1:8]):
            gs[n][l] = val[0]
        rest_grads['w_in'], rest_grads['mla_w_q_b'], rest_grads['mla_w_kv_b'] = outs[8:11]
        ex.grads_ready(l, 'rest', rest_grads)
        dx, first_grads, got = ffn_back('ffn1', sv['x0'], dx, sv['g1'], sv['u1'], ex.scatter_behind_ffn1(l))
        ex.scattered_behind_ffn1(l, got)
        ex.grads_ready(l, 'first', first_grads)
    return loss, dx, gs


class _Exchange:
    def __init__(self, weights, c, chip):
        halves_of = lambda a: a.reshape(a.shape[:-2] + (2, a.shape[-2] // 2, a.shape[-1]))
        self.halves_of, self.c, self.chip = halves_of, c, chip
        self.mine = [[halves_of(jnp.concatenate([weights[n][l].astype(BF16) for n in group], axis=0))
                      for group in FIRST_GROUPS + REST_GROUPS] for l in range(DEPTH)]
        self.ahead, self.begun, self.received = {}, {}, {}

    def _assembled(self, l, which, gathered):
        groups, base = (FIRST_GROUPS, 0) if which == 'first' else (REST_GROUPS, N_FIRST)
        big = {}
        for gi, group in enumerate(groups):
            offs = _group_row_offsets(group)
            _, rh, w = self.mine[l][base + gi].shape
            col_sharded = BIG[group[0]][1] == 1
            full = _assemble(gathered[gi].reshape(N_SHARD, 1, 2 * rh, w), self.mine[l][base + gi].reshape(1, 2 * rh, w),
                             f"assemble_{which}{gi}_{l}", col_sharded, MIXER_ORDERS.get(group[0]))
            for i, n in enumerate(group):
                rows = offs[i + 1] - offs[i]
                if len(group) > 1 and offs[i] % rows == 0:
                    big[n] = ((full, (None, rows, N_SHARD * w), (0, offs[i] // rows, 0)) if col_sharded else
                              (full, (None, N_SHARD, rows, w), (0, 0, offs[i] // rows, 0)))
                elif col_sharded:
                    big[n] = full[0, offs[i]:offs[i + 1]]
                else:
                    big[n] = full[0, :, offs[i]:offs[i + 1]].reshape(BIG[n][0])
        return big

    def first_weights(self, l):
        got = self.ahead[l][:N_FIRST] if l in self.ahead else _all_gather_halves(self.mine[l][:N_FIRST], f"gather_first_{l}")
        return self._assembled(l, 'first', got)

    def gather_behind_ffn1(self, l):
        return None if l in self.ahead else self.mine[l][N_FIRST:]

    def rest_weights(self, l, got):
        return self._assembled(l, 'rest', self.ahead[l][N_FIRST:] if l in self.ahead else got)

    def gather_behind_mla(self, l):
        return self.mine[l + 1] if l + 1 < DEPTH else None

    def gathered_behind_mla(self, l, got):
        if got:
            self.ahead[l + 1] = got

    def grads_ready(self, l, which, grads):
        groups = FIRST_GROUPS if which == 'first' else REST_GROUPS
        parts = [self.halves_of(jnp.concatenate([grads[n] for n in group], axis=1)) for group in groups]
        from_sibling = _sibling_exchange(parts, f"swap_{which}_{l}", True)
        chip_sums = []
        for gi, (p, got) in enumerate(zip(parts, from_sibling)):
            kept = lax.dynamic_index_in_dim(p, self.c, axis=1, keepdims=False)
            rows = N_SHARD * p.shape[2]
            pair = _sum_blocks([kept.reshape(rows, -1), got.reshape(rows, -1)], BF16, f"sum_pair_{which}{gi}_{l}")
            chip_sums.append(pair.reshape(got.shape))
        self.begun[(l, which)] = chip_sums

    def scatter_behind_mla(self, l):
        return self.begun[(l + 1, 'first')] + self.begun[(l + 1, 'rest')] if l + 1 < DEPTH else None

    def scattered_behind_mla(self, l, got):
        if got:
            self.received[(l + 1, 'first')], self.received[(l + 1, 'rest')] = got[:N_FIRST], got[N_FIRST:]

    def scatter_behind_ffn1(self, l):
        return self.begun[(l, 'rest')] if l == 0 else None

    def scattered_behind_ffn1(self, l, got):
        if got:
            self.received[(l, 'rest')] = got

    def reduced(self):
        keys = sorted(self.begun)
        for key in keys:
            if key not in self.received:
                self.received[key] = _scatter_to_chips(self.begun[key], f"scatter_{key[1]}_{key[0]}")
        halves = []
        for l, which in keys:
            for gi, (cs, got) in enumerate(zip(self.begun[(l, which)], self.received[(l, which)])):
                own = lax.dynamic_index_in_dim(cs, self.chip, axis=0, keepdims=False)
                halves.append(_sum_blocks([own, got[0], got[1], got[2]], F32, f"sum_chips_{which}{gi}_{l}"))
        others = _sibling_exchange(halves, "share_halves", False)
        per_layer, at = {}, 0
        for l, which in keys:
            for group in (FIRST_GROUPS if which == 'first' else REST_GROUPS):
                mine_h, other_h = halves[at], others[at]
                at += 1
                full = jnp.where(self.c == 0, jnp.concatenate([mine_h, other_h]), jnp.concatenate([other_h, mine_h]))
                offs = _group_row_offsets(group)
                for i, n in enumerate(group):
                    per_layer[(n, l)] = full[offs[i]:offs[i + 1]]
        return {n: jnp.stack([per_layer[(n, l)] for l in range(DEPTH)]) for n in BIG_NAMES}


def kernel(x, ffn1_norm, ffn1_w_gate, ffn1_w_up, ffn1_w_down, mix_norm, w_in, mla_q_a_norm, mla_w_q_b, mla_kv_a_norm, mla_w_kv_b, mla_q_norm, mla_k_norm, swa_q_norm, swa_k_norm, swa_sinks, mla_out_norm, swa_out_norm, w_o, ffn2_norm, ffn2_w_gate, ffn2_w_up, ffn2_w_down, loss_target, m_ffn1_norm, m_ffn1_w_gate, m_ffn1_w_up, m_ffn1_w_down, m_mix_norm, m_w_in, m_mla_q_a_norm, m_mla_w_q_b, m_mla_kv_a_norm, m_mla_w_kv_b, m_mla_q_norm, m_mla_k_norm, m_swa_q_norm, m_swa_k_norm, m_swa_sinks, m_mla_out_norm, m_swa_out_norm, m_w_o, m_ffn2_norm, m_ffn2_w_gate, m_ffn2_w_up, m_ffn2_w_down, v_ffn1_norm, v_ffn1_w_gate, v_ffn1_w_up, v_ffn1_w_down, v_mix_norm, v_w_in, v_mla_q_a_norm, v_mla_w_q_b, v_mla_kv_a_norm, v_mla_w_kv_b, v_mla_q_norm, v_mla_k_norm, v_swa_q_norm, v_swa_k_norm, v_swa_sinks, v_mla_out_norm, v_swa_out_norm, v_w_o, v_ffn2_norm, v_ffn2_w_gate, v_ffn2_w_up, v_ffn2_w_down):
    args = dict(locals())
    transposed = lambda a: jnp.swapaxes(a, 1, 2)
    as_kernels_see = lambda n, a: transposed(a) if n in TRANSPOSED else a
    weights = {n: as_kernels_see(n, args[n]) for n in WEIGHT_NAMES}
    mom_m = {n: as_kernels_see(n, args["m_" + n]) for n in WEIGHT_NAMES}
    mom_v = {n: as_kernels_see(n, args["v_" + n]) for n in WEIGHT_NAMES}
    ex = _Exchange(weights, lax.axis_index("c"), 2 * lax.axis_index("x") + lax.axis_index("y"))
    loss, dx, gs = _local_step(x[0], loss_target[0], {n: weights[n] for n in SMALL_NAMES}, ex)

    small_flat = jnp.concatenate([jnp.stack(gs[n]).reshape(-1) for n in SMALL_NAMES] + [loss[0, :1]])
    n_small = small_flat.shape[0]
    lanes = -(-n_small // (8 * 128)) * 128
    small_sum = _all_reduce_small(jnp.pad(small_flat, (0, 8 * lanes - n_small)).reshape(8, lanes), "reduce_small").reshape(-1)
    grads = ex.reduced()
    off = 0
    for n in SMALL_NAMES:
        cnt = int(np.prod(weights[n].shape))
        grads[n] = small_sum[off:off + cnt].reshape(weights[n].shape)
        off += cnt
    loss_out = small_sum[off]

    deltas, new_m, new_v = {}, {}, {}
    for n in WEIGHT_NAMES:
        shp = weights[n].shape
        two_d = (DEPTH, shp[-1]) if len(shp) == 2 else (shp[0] * shp[1], shp[2])
        d, nm, nv = _adamw(weights[n].reshape(two_d), grads[n].reshape(two_d), mom_m[n].reshape(two_d),
                           mom_v[n].reshape(two_d), f"adamw_{n}")
        deltas[n], new_m[n], new_v[n] = d.reshape(shp), nm.reshape(shp), nv.reshape(shp)
    for n in TRANSPOSED:
        grads[n], deltas[n], new_m[n], new_v[n] = (transposed(a) for a in (grads[n], deltas[n], new_m[n], new_v[n]))

    return (loss_out, dx[None], *[grads[n] for n in WEIGHT_NAMES], *[deltas[n] for n in WEIGHT_NAMES],
            *[new_m[n] for n in WEIGHT_NAMES], *[new_v[n] for n in WEIGHT_NAMES])
```

```python
import functools

import numpy as np
import jax
import jax.numpy as jnp
from jax import lax
from jax.experimental import pallas as pl
from jax.experimental.pallas import tpu as pltpu

F32 = jnp.float32
BF16 = jnp.bfloat16

D_MODEL = 1024
DEPTH = 2
EPS = 1e-6
ROPE_THETA = 10000.0
BLOCK = 128
MLA_HEADS = 4
MLA_Q_RANK = 256
MLA_KV_RANK = 128
MLA_NOPE = 128
MLA_ROPE = 64
MLA_V = 128
MLA_QK = MLA_NOPE + MLA_ROPE
MLA_WIDTH = MLA_HEADS * MLA_V
SWA_HEADS = 8
SWA_KV_HEADS = 2
SWA_GROUP = SWA_HEADS // SWA_KV_HEADS
SWA_HEAD_DIM = 64
SWA_WIDTH = SWA_HEADS * SWA_HEAD_DIM
MIX_WIDTH = MLA_WIDTH + SWA_WIDTH
IN_SPLITS = (MLA_Q_RANK, MLA_KV_RANK, MLA_ROPE, SWA_WIDTH, SWA_KV_HEADS * SWA_HEAD_DIM, SWA_KV_HEADS * SWA_HEAD_DIM)
IN_COLS = sum(IN_SPLITS)
IN_OFFS = tuple(int(v) for v in np.cumsum((0,) + IN_SPLITS))
D_FF = 2816
MLA_SCALE = MLA_QK ** -0.5
LOG2E = 1.4426950408889634
LN2 = 0.6931471805599453
MLA_QSCALE = MLA_SCALE * LOG2E
SWA_SCALE = SWA_HEAD_DIM ** -0.5
NEG = -1e30

ADAM_LR = 0.001
ADAM_B1 = 0.9
ADAM_B2 = 0.999
ADAM_EPS = 1e-08
ADAM_WD = 0.01
ADAM_STEP = 10

N_SHARD = 4
N_DEV = 8
VMEM_LIMIT = 56 * 1024 * 1024
MESH = pl.DeviceIdType.MESH

WEIGHT_NAMES = ['ffn1_norm', 'ffn1_w_gate', 'ffn1_w_up', 'ffn1_w_down', 'mix_norm', 'w_in', 'mla_q_a_norm', 'mla_w_q_b',
                'mla_kv_a_norm', 'mla_w_kv_b', 'mla_q_norm', 'mla_k_norm', 'swa_q_norm', 'swa_k_norm', 'swa_sinks',
                'mla_out_norm', 'swa_out_norm', 'w_o', 'ffn2_norm', 'ffn2_w_gate', 'ffn2_w_up', 'ffn2_w_down']
TRANSPOSED = ('ffn1_w_gate', 'ffn1_w_up', 'ffn2_w_gate', 'ffn2_w_up')
BIG = {'ffn1_w_gate': ((D_FF, D_MODEL), 0), 'ffn1_w_up': ((D_FF, D_MODEL), 0), 'ffn1_w_down': ((D_FF, D_MODEL), 0),
       'w_in': ((D_MODEL, IN_COLS), 1), 'mla_w_q_b': ((MLA_Q_RANK, MLA_HEADS * MLA_QK), 1),
       'mla_w_kv_b': ((MLA_KV_RANK, MLA_HEADS * (MLA_NOPE + MLA_V)), 1), 'w_o': ((MIX_WIDTH, D_MODEL), 0),
       'ffn2_w_gate': ((D_FF, D_MODEL), 0), 'ffn2_w_up': ((D_FF, D_MODEL), 0), 'ffn2_w_down': ((D_FF, D_MODEL), 0)}
BIG_NAMES = [n for n in WEIGHT_NAMES if n in BIG]
SMALL_NAMES = [n for n in WEIGHT_NAMES if n not in BIG]

_pallas_call = pl.pallas_call


def _params(**kw):
    return pltpu.CompilerParams(vmem_limit_bytes=VMEM_LIMIT, **kw)


def _full(shape):
    n = len(shape)
    return pl.BlockSpec(shape, lambda *_: (0,) * n)


def _resident(shape):
    n = len(shape)
    return pl.BlockSpec(shape, lambda *_: (0,) * n, pipeline_mode=pl.Buffered(1))


@jax.custom_vjp
def _mm(a, w):
    return jnp.dot(a.astype(BF16), w, preferred_element_type=F32)


def _mm_fwd(a, w):
    return _mm(a, w), w


def _mm_bwd(w, dy):
    return lax.dot_general(dy.astype(BF16), w, (((1,), (1,)), ((), ())), preferred_element_type=F32), None


_mm.defvjp(_mm_fwd, _mm_bwd)


def _dot_nt(a, b):
    return lax.dot_general(a, b, (((1,), (1,)), ((), ())), preferred_element_type=F32)


def _dot_tn(a, b):
    return lax.dot_general(a, b, (((0,), (0,)), ((), ())), preferred_element_type=F32)


def _rms(t, g):
    return t * lax.rsqrt(jnp.mean(t * t, axis=-1, keepdims=True) + EPS) * g


def _sigmoid(z):
    return 1.0 / (1.0 + jnp.exp(-z))


def _row_tile(s, want):
    return min(want, s)


def _divisor_tile(rows, cap):
    return max(d for d in range(16, min(rows, cap) + 1, 16) if rows % d == 0)


FF_CHUNK = 1408


def _weight_operand(w):
    if isinstance(w, tuple):
        arr, block, index = w
        return arr, pl.BlockSpec(block, lambda *_: index, pipeline_mode=pl.Buffered(1))
    return w, _resident(w.shape)


def _weight_rows(ref, start, n):
    if len(ref.shape) == 2:
        return ref[start:start + n, :]
    per = ref.shape[1]
    return ref[start // per:(start + n) // per].reshape(n, ref.shape[2])


def _ffn_fwd(x, g, wg, wu, wd, name, gather=None):
    s = x.shape[0]
    tm = _row_tile(s, 256)
    steps = s // tm
    ng = len(gather) if gather else 0
    (wg, wg_spec), (wu, wu_spec), (wd, wd_spec) = _weight_operand(wg), _weight_operand(wu), _weight_operand(wd)

    def body(x_ref, g_ref, wg_ref, wu_ref, wd_ref, *rest):
        g_ins, (y_ref, gate_ref, up_ref), g_outs, sems = rest[:ng], rest[ng:ng + 3], rest[ng + 3:2 * ng + 3], rest[2 * ng + 3:]
        if ng:
            _run_stages_at(_gather_stages(g_ins, g_outs, *sems), [(0,), (steps * 3 // 8,), (steps * 11 // 16,), (steps - 1,)])
        xv = x_ref[...]
        nb = _rms(xv, g_ref[...]).astype(BF16)
        acc = xv
        for c in range(0, D_FF, FF_CHUNK):
            gate = _dot_nt(nb, _weight_rows(wg_ref, c, FF_CHUNK))
            up = _dot_nt(nb, _weight_rows(wu_ref, c, FF_CHUNK))
            gate_ref[:, c:c + FF_CHUNK] = gate.astype(BF16)
            up_ref[:, c:c + FF_CHUNK] = up.astype(BF16)
            act = (gate * _sigmoid(gate) * up).astype(BF16)
            acc = acc + 0.5 * jnp.dot(act, _weight_rows(wd_ref, c, FF_CHUNK), preferred_element_type=F32)
        y_ref[...] = acc

    outs = _pallas_call(
        body, name=name, grid=(steps,),
        in_specs=[pl.BlockSpec((tm, D_MODEL), lambda i: (i, 0)), _full((1, D_MODEL)), wg_spec, wu_spec, wd_spec] + [_HBM] * ng,
        out_specs=[pl.BlockSpec((tm, D_MODEL), lambda i: (i, 0)), pl.BlockSpec((tm, D_FF), lambda i: (i, 0)),
                   pl.BlockSpec((tm, D_FF), lambda i: (i, 0))] + [_HBM] * ng,
        out_shape=[jax.ShapeDtypeStruct((s, D_MODEL), F32), jax.ShapeDtypeStruct((s, D_FF), BF16),
                   jax.ShapeDtypeStruct((s, D_FF), BF16)] + (_gather_out_shapes(gather) if ng else []),
        scratch_shapes=_exchange_scratch(8, ng) if ng else [],
        compiler_params=_params(dimension_semantics=("arbitrary",)),
    )(x, g, wg, wu, wd, *(gather or []))
    return outs[0], outs[1], outs[2], outs[3:]


def _ffn_bwd(x, dy, gate, up, g, wg, wu, wd, name, scatter=None):
    s = x.shape[0]
    tm = _row_tile(s, 256)
    steps = s // tm
    ng = len(scatter) if scatter else 0
    (wg, wg_spec), (wu, wu_spec), (wd, wd_spec) = _weight_operand(wg), _weight_operand(wu), _weight_operand(wd)

    def body(x_ref, dy_ref, gate_ref, up_ref, g_ref, wg_ref, wu_ref, wd_ref, *rest):
        c_ins, (dx_ref, dgain_ref, n_ref, act_ref, dgate_ref, dup_ref) = rest[:ng], rest[ng:ng + 6]
        c_outs, sems = rest[ng + 6:2 * ng + 6], rest[2 * ng + 6:]
        if ng:
            _run_stages_at(_scatter_stages(c_ins, c_outs, *sems), [(0,), (steps - 1,)])
        i = pl.program_id(0)
        xv = x_ref[...]
        dyv = dy_ref[...]
        gv = g_ref[...]
        r = lax.rsqrt(jnp.mean(xv * xv, axis=-1, keepdims=True) + EPS)
        xh = xv * r
        n_ref[...] = (xh * gv).astype(BF16)
        dyh = (0.5 * dyv).astype(BF16)
        dn = jnp.zeros_like(xv)
        for c in range(0, D_FF, FF_CHUNK):
            dact = _dot_nt(dyh, _weight_rows(wd_ref, c, FF_CHUNK))
            gt = gate_ref[:, c:c + FF_CHUNK].astype(F32)
            u = up_ref[:, c:c + FF_CHUNK].astype(F32)
            sg = _sigmoid(gt)
            sl = gt * sg
            act_ref[:, c:c + FF_CHUNK] = (sl * u).astype(BF16)
            dup = (dact * sl).astype(BF16)
            dgate = (dact * u * (sg * (1.0 + gt * (1.0 - sg)))).astype(BF16)
            dup_ref[:, c:c + FF_CHUNK] = dup
            dgate_ref[:, c:c + FF_CHUNK] = dgate
            dn = (dn + jnp.dot(dgate, _weight_rows(wg_ref, c, FF_CHUNK), preferred_element_type=F32)
                  + jnp.dot(dup, _weight_rows(wu_ref, c, FF_CHUNK), preferred_element_type=F32))
        part = jnp.sum(dn * xh, axis=0, keepdims=True)

        @pl.when(i == 0)
        def _():
            dgain_ref[...] = part

        @pl.when(i > 0)
        def _():
            dgain_ref[...] += part

        dxh = dn * gv
        dx_ref[...] = dyv + r * (dxh - xh * jnp.mean(dxh * xh, axis=-1, keepdims=True))

    row = lambda w: pl.BlockSpec((tm, w), lambda i: (i, 0))
    outs = _pallas_call(
        body, name=name, grid=(steps,),
        in_specs=[row(D_MODEL), row(D_MODEL), row(D_FF), row(D_FF), _full((1, D_MODEL)), wg_spec, wu_spec, wd_spec]
        + [_HBM] * ng,
        out_specs=[row(D_MODEL), _full((1, D_MODEL)), row(D_MODEL), row(D_FF), row(D_FF), row(D_FF)] + [_HBM] * ng,
        out_shape=[jax.ShapeDtypeStruct((s, D_MODEL), F32), jax.ShapeDtypeStruct((1, D_MODEL), F32),
                   jax.ShapeDtypeStruct((s, D_MODEL), BF16), jax.ShapeDtypeStruct((s, D_FF), BF16),
                   jax.ShapeDtypeStruct((s, D_FF), BF16), jax.ShapeDtypeStruct((s, D_FF), BF16)]
        + (_scatter_out_shapes(scatter) if ng else []),
        scratch_shapes=_exchange_scratch(3, ng) if ng else [],
        compiler_params=_params(dimension_semantics=("arbitrary",)),
    )(x, dy, gate, up, g, wg, wu, wd, *(scatter or []))
    return outs[:6], outs[6:]


def _store_col_shards(o_ref, acc, first_shard, n_here, width):
    for q in range(n_here):
        o_ref[q] = acc[:, (first_shard + q) * width:(first_shard + q + 1) * width].astype(BF16)


def _matmul_tn(a, b, scale, name):
    t, m = a.shape
    n = b.shape[1]
    tk = _row_tile(t, 2048)
    tn = n // 2
    nk = t // tk

    def body(a_ref, b_ref, o_ref, acc_ref):
        k = pl.program_id(1)
        bv = b_ref[...]
        if scale != 1.0:
            bv = bv.astype(F32) * scale
        part = _dot_tn(a_ref[...].astype(BF16), bv.astype(BF16))

        @pl.when(k == 0)
        def _():
            acc_ref[...] = part

        @pl.when(k > 0)
        def _():
            acc_ref[...] += part

        @pl.when(k == nk - 1)
        def _():
            o_ref[...] = acc_ref[...].astype(BF16)

    return _pallas_call(
        body, name=name, grid=(n // tn, nk),
        in_specs=[pl.BlockSpec((tk, m), lambda j, k: (k, 0)), pl.BlockSpec((tk, tn), lambda j, k: (k, j))],
        out_specs=pl.BlockSpec((m, tn), lambda j, k: (0, j)), out_shape=jax.ShapeDtypeStruct((m, n), BF16),
        scratch_shapes=[pltpu.VMEM((m, tn), F32)],
        compiler_params=_params(dimension_semantics=("arbitrary", "arbitrary")),
    )(a, b)


_HALF = SWA_HEAD_DIM // 2
_IN_ORDER = (list(range(0, IN_OFFS[2]))
             + [IN_OFFS[3] + SWA_HEAD_DIM * h + i for h in range(SWA_HEADS) for i in range(_HALF)]
             + [IN_OFFS[3] + SWA_HEAD_DIM * h + _HALF + i for h in range(SWA_HEADS) for i in range(_HALF)]
             + list(range(IN_OFFS[5], IN_OFFS[6]))
             + [IN_OFFS[4] + SWA_HEAD_DIM * j + i for j in range(SWA_KV_HEADS) for i in range(_HALF)]
             + [IN_OFFS[4] + SWA_HEAD_DIM * j + _HALF + i for j in range(SWA_KV_HEADS) for i in range(_HALF)]
             + list(range(IN_OFFS[2], IN_OFFS[3])))
_QB_ORDER = ([MLA_QK * h + i for h in range(MLA_HEADS) for i in range(MLA_NOPE)]
             + [MLA_QK * h + MLA_NOPE + i for h in range(MLA_HEADS) for i in range(_HALF)]
             + [MLA_QK * h + MLA_NOPE + _HALF + i for h in range(MLA_HEADS) for i in range(_HALF)])
_KVB_ORDER = ([(MLA_NOPE + MLA_V) * h + i for h in range(MLA_HEADS) for i in range(MLA_NOPE)]
              + [(MLA_NOPE + MLA_V) * h + MLA_NOPE + i for h in range(MLA_HEADS) for i in range(MLA_V)])
MIXER_ORDERS = {'w_in': _IN_ORDER, 'mla_w_q_b': _QB_ORDER, 'mla_w_kv_b': _KVB_ORDER}
_P_CQ, _P_CKV, _P_QA, _P_QB, _P_VS, _P_KA, _P_KB, _P_PE = (int(v) for v in np.cumsum(
    (0, MLA_Q_RANK, MLA_KV_RANK, SWA_WIDTH // 2, SWA_WIDTH // 2, IN_SPLITS[5], IN_SPLITS[4] // 2, IN_SPLITS[4] // 2)))


def _runs(order):
    out, start = [], 0
    for i in range(1, len(order) + 1):
        if i == len(order) or order[i] != order[i - 1] + 1:
            out.append((order[start], i - start))
            start = i
    return out


def _inverse(order):
    inv = [0] * len(order)
    for new, old in enumerate(order):
        inv[old] = new
    return inv


def _take_cols(a, order):
    return jnp.concatenate([a[..., st:st + w] for st, w in _runs(order)], axis=-1)


def _segment_matrix(n, seg):
    return (lax.broadcasted_iota(jnp.int32, (n, n), 0) // seg == lax.broadcasted_iota(jnp.int32, (n, n), 1) // seg).astype(BF16)


@jax.custom_vjp
def _cmm(t, b, bt):
    hi = t.astype(BF16)
    lo = (t - hi.astype(F32)).astype(BF16)
    return jnp.dot(hi, b, preferred_element_type=F32) + jnp.dot(lo, b, preferred_element_type=F32)


def _cmm_fwd(t, b, bt):
    return _cmm(t, b, bt), (b, bt)


def _cmm_bwd(res, dy):
    b, bt = res
    return _cmm(dy, bt, b), None, None


_cmm.defvjp(_cmm_fwd, _cmm_bwd)


def _segsum(t, b):
    return _cmm(t, b, b)


def _rowsum(t):
    n = t.shape[-1]
    return _cmm(t, jnp.ones((n, 128), BF16), jnp.ones((128, n), BF16))


def _by_head(vals, width):
    lane = lax.broadcasted_iota(jnp.int32, (vals[0].shape[0], len(vals) * width), 1)
    out = vals[-1]
    for hd in range(len(vals) - 2, -1, -1):
        out = jnp.where(lane < (hd + 1) * width, vals[hd], out)
    return out


def _rope2(a, b, cos, sin):
    return a * cos - b * sin, b * cos + a * sin


def _pre_math(x, gm, gqa, gkva, gq, gk, gsq, gsk, taps, win, wqb, wkvb, cos, sin):
    h = _rms(x, gm)
    proj = _mm(h, win)
    if taps is not None:
        proj = proj + taps[0]
    cqn = _rms(proj[:, _P_CQ:_P_CKV], gqa)
    qa_all = _mm(cqn, wqb)
    ckvn = _rms(proj[:, _P_CKV:_P_QA], gkva)
    kv_all = _mm(ckvn, wkvb)
    if taps is not None:
        qa_all = qa_all + taps[1]
        kv_all = kv_all + taps[2]
    nh, hw = MLA_HEADS, MLA_HEADS * _HALF
    seg_mla = _segment_matrix(hw, _HALF)
    tile = lambda g, n: jnp.concatenate([g] * n, axis=-1)
    c4, s4 = cos[:, :hw], sin[:, :hw]

    def mla_heads(nope, r1, r2, gain):
        rr = r1 * r1 + r2 * r2
        lane_head = lax.broadcasted_iota(jnp.int32, (hw, nh * MLA_NOPE), 0) // _HALF
        spread = (lane_head == lax.broadcasted_iota(jnp.int32, (hw, nh * MLA_NOPE), 1) // MLA_NOPE).astype(BF16)
        rope_on_nope = _cmm(rr, spread, spread.T)
        ss_nope = [_rowsum(jnp.square(nope[:, hd * MLA_NOPE:(hd + 1) * MLA_NOPE])) for hd in range(nh)]
        rinv = [lax.rsqrt((ss_nope[hd] + rope_on_nope[:, hd * MLA_NOPE:(hd + 1) * MLA_NOPE]) * (1.0 / MLA_QK) + EPS)
                for hd in range(nh)]
        rl = lax.rsqrt((_segsum(rr, seg_mla) + _by_head(ss_nope, _HALF)) * (1.0 / MLA_QK) + EPS)
        o1, o2 = _rope2(r1 * rl * tile(gain[:, MLA_NOPE:MLA_NOPE + _HALF], nh), r2 * rl * tile(gain[:, MLA_NOPE + _HALF:], nh), c4, s4)
        return [jnp.concatenate([nope[:, hd * MLA_NOPE:(hd + 1) * MLA_NOPE] * rinv[hd] * gain[:, :MLA_NOPE],
                                 o1[:, hd * _HALF:(hd + 1) * _HALF], o2[:, hd * _HALF:(hd + 1) * _HALF]], axis=-1)
                for hd in range(nh)]

    q_a = mla_heads(qa_all[:, :nh * MLA_NOPE], qa_all[:, nh * MLA_NOPE:nh * MLA_NOPE + hw], qa_all[:, nh * MLA_NOPE + hw:], gq)
    pe1, pe2 = proj[:, _P_PE:_P_PE + _HALF], proj[:, _P_PE + _HALF:_P_PE + 2 * _HALF]
    k_a = mla_heads(kv_all[:, :nh * MLA_NOPE], tile(pe1, nh), tile(pe2, nh), gk)
    v_a = [kv_all[:, nh * MLA_NOPE + hd * MLA_V:nh * MLA_NOPE + (hd + 1) * MLA_V] for hd in range(nh)]

    def swa_heads(a, b, gain, n):
        w = n * _HALF
        r = lax.rsqrt(_segsum(a * a + b * b, _segment_matrix(w, _HALF)) * (1.0 / SWA_HEAD_DIM) + EPS)
        o1, o2 = _rope2(a * r * tile(gain[:, :_HALF], n), b * r * tile(gain[:, _HALF:], n), cos[:, :w], sin[:, :w])
        return [jnp.concatenate([o1[:, hd * _HALF:(hd + 1) * _HALF], o2[:, hd * _HALF:(hd + 1) * _HALF]], axis=-1) for hd in range(n)]

    q_b = swa_heads(proj[:, _P_QA:_P_QB], proj[:, _P_QB:_P_VS], gsq, SWA_HEADS)
    k_b = swa_heads(proj[:, _P_KA:_P_KB], proj[:, _P_KB:_P_PE], gsk, SWA_KV_HEADS)
    v_b = [proj[:, _P_VS + j * SWA_HEAD_DIM:_P_VS + (j + 1) * SWA_HEAD_DIM] for j in range(SWA_KV_HEADS)]
    return (q_a, k_a, v_a, q_b, k_b, v_b), (h, cqn, ckvn)


_PRE_GAIN_WIDTHS = (D_MODEL, MLA_Q_RANK, MLA_KV_RANK, MLA_QK, MLA_QK, SWA_HEAD_DIM, SWA_HEAD_DIM)
_PRE_HEADS = ((MLA_HEADS, MLA_QK), (MLA_HEADS, MLA_QK), (MLA_HEADS, MLA_V),
              (SWA_HEADS, SWA_HEAD_DIM), (SWA_KV_HEADS, SWA_HEAD_DIM), (SWA_KV_HEADS, SWA_HEAD_DIM))


def _pre_fwd(x, gains, win, wqb, wkvb, cos, sin, name):
    s = x.shape[0]
    tm = _row_tile(s, 512)

    def body(x_ref, *refs):
        g_refs, (win_ref, wqb_ref, wkvb_ref, cos_ref, sin_ref), out_refs = refs[:7], refs[7:12], refs[12:]
        outs, _ = _pre_math(x_ref[...], *[g[...] for g in g_refs], None, win_ref[...], wqb_ref[...], wkvb_ref[...],
                            cos_ref[...], sin_ref[...])
        for idx, (ref, heads) in enumerate(zip(out_refs, outs)):
            for hd, val in enumerate(heads):
                ref[hd] = (val * MLA_QSCALE if idx == 0 else val).astype(BF16)

    heads_spec = lambda nh, w: pl.BlockSpec((nh, tm, w), lambda i: (0, i, 0))
    return _pallas_call(
        body, name=name, grid=(s // tm,),
        in_specs=[pl.BlockSpec((tm, D_MODEL), lambda i: (i, 0))] + [_full((1, w)) for w in _PRE_GAIN_WIDTHS]
        + [_resident(win.shape), _resident(wqb.shape), _resident(wkvb.shape),
           pl.BlockSpec((tm, SWA_HEADS * _HALF), lambda i: (i, 0)), pl.BlockSpec((tm, SWA_HEADS * _HALF), lambda i: (i, 0))],
        out_specs=[heads_spec(nh, w) for nh, w in _PRE_HEADS],
        out_shape=[jax.ShapeDtypeStruct((nh, s, w), BF16) for nh, w in _PRE_HEADS],
        compiler_params=_params(dimension_semantics=("arbitrary",)),
    )(x, *gains, win, wqb, wkvb, cos, sin)


def _pre_bwd(x, dx_res, cts, gains, win, wqb, wkvb, cos, sin, name):
    s = x.shape[0]
    tm = _row_tile(s, 256)
    tap_widths = (IN_COLS, MLA_HEADS * MLA_QK, MLA_HEADS * (MLA_NOPE + MLA_V))

    def body(x_ref, dxr_ref, *refs):
        ct_refs, g_refs = refs[:6], refs[6:13]
        win_ref, wqb_ref, wkvb_ref, cos_ref, sin_ref = refs[13:18]
        dx_ref, dg_refs, dw_refs, acc_refs = refs[18], refs[19:26], refs[26:29], refs[29:32]
        i = pl.program_id(0)
        win_v, wqb_v, wkvb_v, cos_v, sin_v = win_ref[...], wqb_ref[...], wkvb_ref[...], cos_ref[...], sin_ref[...]

        def f(xv, gm, gqa, gkva, gq, gk, gsq, gsk, t0, t1, t2):
            return _pre_math(xv, gm, gqa, gkva, gq, gk, gsq, gsk, (t0, t1, t2), win_v, wqb_v, wkvb_v, cos_v, sin_v)

        taps = [jnp.zeros((tm, w), F32) for w in tap_widths]
        _, vjp, acts = jax.vjp(f, x_ref[...], *[g[...] for g in g_refs], *taps, has_aux=True)
        ct = tuple([ref[hd] for hd in range(nh)] for ref, (nh, _) in zip(ct_refs, _PRE_HEADS))
        grads = vjp(ct)
        dx_ref[...] = grads[0] + dxr_ref[...]
        dws = [_dot_tn(a.astype(BF16), t.astype(BF16)) for a, t in zip(acts, grads[8:11])]

        @pl.when(i == 0)
        def _():
            for ref, val in zip(dg_refs, grads[1:8]):
                ref[...] = val
            for ref, val in zip(acc_refs, dws):
                ref[...] = val

        @pl.when(i > 0)
        def _():
            for ref, val in zip(dg_refs, grads[1:8]):
                ref[...] += val
            for ref, val in zip(acc_refs, dws):
                ref[...] += val

        @pl.when(i == s // tm - 1)
        def _():
            for ref, acc, order in zip(dw_refs, acc_refs, (_IN_ORDER, _QB_ORDER, _KVB_ORDER)):
                _store_col_shards(ref, _take_cols(acc[...], _inverse(order)), 0, N_SHARD, acc.shape[1] // N_SHARD)

    heads_spec = lambda nh, w: pl.BlockSpec((nh, tm, w), lambda i: (0, i, 0))
    row = pl.BlockSpec((tm, D_MODEL), lambda i: (i, 0))
    half = pl.BlockSpec((tm, SWA_HEADS * _HALF), lambda i: (i, 0))
    shard_shapes = [(N_SHARD, w.shape[0], w.shape[1] // N_SHARD) for w in (win, wqb, wkvb)]
    return _pallas_call(
        body, name=name, grid=(s // tm,),
        in_specs=[row, row] + [heads_spec(nh, w) for nh, w in _PRE_HEADS] + [_full((1, w)) for w in _PRE_GAIN_WIDTHS]
        + [_resident(win.shape), _resident(wqb.shape), _resident(wkvb.shape), half, half],
        out_specs=[row] + [_full((1, w)) for w in _PRE_GAIN_WIDTHS] + [_full(shp) for shp in shard_shapes],
        out_shape=[jax.ShapeDtypeStruct((s, D_MODEL), F32)] + [jax.ShapeDtypeStruct((1, w), F32) for w in _PRE_GAIN_WIDTHS]
        + [jax.ShapeDtypeStruct(shp, BF16) for shp in shard_shapes],
        scratch_shapes=[pltpu.VMEM(w.shape, F32) for w in (win, wqb, wkvb)],
        compiler_params=_params(dimension_semantics=("arbitrary",)),
    )(x, dx_res, *cts, *gains, win, wqb, wkvb, cos, sin)


def _post_math(oa, ob, ga, gb, wo):
    mixed = jnp.concatenate([_rms(jnp.concatenate(oa, axis=-1), ga), _rms(jnp.concatenate(ob, axis=-1), gb)], axis=-1)
    return _mm(mixed, wo), mixed


def _post_fwd(x, oa, ob, ga, gb, wo, name):
    s = x.shape[0]
    tm = _row_tile(s, 512)

    def body(x_ref, oa_ref, ob_ref, ga_ref, gb_ref, wo_ref, y_ref):
        y, _ = _post_math([oa_ref[hd] for hd in range(MLA_HEADS)], [ob_ref[hd] for hd in range(SWA_HEADS)],
                          ga_ref[...], gb_ref[...], wo_ref[...])
        y_ref[...] = x_ref[...] + y

    row = pl.BlockSpec((tm, D_MODEL), lambda i: (i, 0))
    return _pallas_call(
        body, name=name, grid=(s // tm,),
        in_specs=[row, pl.BlockSpec((MLA_HEADS, tm, MLA_V), lambda i: (0, i, 0)),
                  pl.BlockSpec((SWA_HEADS, tm, SWA_HEAD_DIM), lambda i: (0, i, 0)),
                  _full((1, MLA_WIDTH)), _full((1, SWA_WIDTH)), _resident(wo.shape)],
        out_specs=row, out_shape=jax.ShapeDtypeStruct((s, D_MODEL), F32),
        compiler_params=_params(dimension_semantics=("arbitrary",)),
    )(x, oa, ob, ga, gb, wo)


def _post_bwd(dy, oa, ob, ga, gb, wo, name):
    s = dy.shape[0]
    tm = _row_tile(s, 512)

    def body(dy_ref, oa_ref, ob_ref, ga_ref, gb_ref, wo_ref, doa_ref, dob_ref, dga_ref, dgb_ref, dwo_ref, acc_ref):
        i = pl.program_id(0)
        wo_v = wo_ref[...]
        dyv = dy_ref[...]

        def f(oa_l, ob_l, ga_v, gb_v):
            return _post_math(oa_l, ob_l, ga_v, gb_v, wo_v)

        _, vjp, mixed = jax.vjp(f, [oa_ref[hd] for hd in range(MLA_HEADS)], [ob_ref[hd] for hd in range(SWA_HEADS)],
                                ga_ref[...], gb_ref[...], has_aux=True)
        doa, dob, dga, dgb = vjp(dyv)
        for hd in range(MLA_HEADS):
            doa_ref[hd] = doa[hd]
        for hd in range(SWA_HEADS):
            dob_ref[hd] = dob[hd]
        dwo = _dot_tn(mixed.astype(BF16), dyv.astype(BF16))

        @pl.when(i == 0)
        def _():
            dga_ref[...] = dga
            dgb_ref[...] = dgb
            acc_ref[...] = dwo

        @pl.when(i > 0)
        def _():
            dga_ref[...] += dga
            dgb_ref[...] += dgb
            acc_ref[...] += dwo

        @pl.when(i == s // tm - 1)
        def _():
            dwo_ref[...] = acc_ref[...].astype(BF16)

    row = pl.BlockSpec((tm, D_MODEL), lambda i: (i, 0))
    oa_spec = pl.BlockSpec((MLA_HEADS, tm, MLA_V), lambda i: (0, i, 0))
    ob_spec = pl.BlockSpec((SWA_HEADS, tm, SWA_HEAD_DIM), lambda i: (0, i, 0))
    return _pallas_call(
        body, name=name, grid=(s // tm,),
        in_specs=[row, oa_spec, ob_spec, _full((1, MLA_WIDTH)), _full((1, SWA_WIDTH)), _resident(wo.shape)],
        out_specs=[oa_spec, ob_spec, _full((1, MLA_WIDTH)), _full((1, SWA_WIDTH)), _full(wo.shape)],
        out_shape=[jax.ShapeDtypeStruct((MLA_HEADS, s, MLA_V), F32), jax.ShapeDtypeStruct((SWA_HEADS, s, SWA_HEAD_DIM), F32),
                   jax.ShapeDtypeStruct((1, MLA_WIDTH), F32), jax.ShapeDtypeStruct((1, SWA_WIDTH), F32),
                   jax.ShapeDtypeStruct(wo.shape, BF16)],
        scratch_shapes=[pltpu.VMEM(wo.shape, F32)],
        compiler_params=_params(dimension_semantics=("arbitrary",)),
    )(dy, oa, ob, ga, gb, wo)


def _attn_tile(s):
    return 512 if s >= 2048 else 128


def _as_rows(cols):
    return cols.T[0:8, :]


def _causal_mask(t):
    return lax.broadcasted_iota(jnp.int32, (t, t), 1) <= lax.broadcasted_iota(jnp.int32, (t, t), 0)


def _pipelined_blocks(first, count, last_block, issue, consume, carry, prefetch_after):
    def clamped(j, slot):
        issue(jnp.minimum(j, last_block), slot)

    def pair(jj, c):
        a = first + 2 * jj
        clamped(a + 1, 1)
        c = consume(a, 0, c)
        clamped(a + 2, 0)
        return consume(a + 1, 1, c)

    clamped(first, 0)
    npairs = count // 2
    carry = lax.fori_loop(0, npairs, pair, carry)

    def odd(c):
        c = consume(first + 2 * npairs, 0, c)
        if prefetch_after:
            clamped(first + count, 0)
        return c

    return lax.cond(count - 2 * npairs == 1, odd, lambda c: c, carry)


def _run_stages_at(stages, steps):
    for stage, step in zip(stages, steps):
        here = pl.program_id(0) == step[0]
        for axis in range(1, len(step)):
            here = here & (pl.program_id(axis) == step[axis])
        pl.when(here)(stage)


def _mla_fwd(q, k, v, name, gather=None):
    nh, s, _ = q.shape
    t = _attn_tile(s)
    nq = s // t
    ng = len(gather) if gather else 0

    def body(q_ref, k_ref, v_ref, *rest):
        g_ins, (o_ref, lse_ref), g_outs = rest[:ng], rest[ng:ng + 2], rest[ng + 2:2 * ng + 2]
        (s0_ref, s1_ref), sems = rest[2 * ng + 2:2 * ng + 4], rest[2 * ng + 4:]
        if ng:
            _run_stages_at(_gather_stages(g_ins, g_outs, *sems), [(0, 0), (nh // 2, 0), (nh - 1, 0), (nh - 1, nq - 1)])
        qi = pl.program_id(1)
        qv = q_ref[...]
        s_refs = (s0_ref, s1_ref)

        def rows(j):
            return pl.ds(pl.multiple_of(j * t, t), t)

        def issue(j, slot):
            s_refs[slot][...] = _dot_nt(qv, k_ref[rows(j), :])

        def consume(j, slot, carry, masked=False):
            m, l, acc = carry
            sc = s_refs[slot][...]
            if masked:
                sc = jnp.where(_causal_mask(t), sc, NEG)
            m_new = jnp.maximum(m, jnp.max(sc, axis=-1, keepdims=True))
            alpha = jnp.exp2(m - m_new)
            p = jnp.exp2(sc - m_new)
            l = alpha * l + jnp.sum(p, axis=-1, keepdims=True)
            acc = alpha * acc + jnp.dot(p.astype(BF16), v_ref[rows(j), :], preferred_element_type=F32)
            return m_new, l, acc

        init = (jnp.full((t, 1), NEG, F32), jnp.zeros((t, 1), F32), jnp.zeros((t, MLA_V), F32))
        carry = _pipelined_blocks(0, qi, nq - 1, issue, consume, init, True)
        m, l, acc = consume(qi, 0, carry, masked=True)
        o_ref[...] = acc / l
        lse_ref[...] = _as_rows(jnp.broadcast_to(m + jnp.log2(l), (t, 128)))

    outs = _pallas_call(
        body, name=name, grid=(nh, nq),
        in_specs=[pl.BlockSpec((None, t, MLA_QK), lambda h, i: (h, i, 0)), pl.BlockSpec((None, s, MLA_QK), lambda h, i: (h, 0, 0)),
                  pl.BlockSpec((None, s, MLA_V), lambda h, i: (h, 0, 0))] + [_HBM] * ng,
        out_specs=[pl.BlockSpec((None, t, MLA_V), lambda h, i: (h, i, 0)), pl.BlockSpec((None, None, 8, t), lambda h, i: (h, i, 0, 0))]
        + [_HBM] * ng,
        out_shape=[jax.ShapeDtypeStruct((nh, s, MLA_V), F32), jax.ShapeDtypeStruct((nh, nq, 8, t), F32)]
        + (_gather_out_shapes(gather) if ng else []),
        scratch_shapes=[pltpu.VMEM((t, t), F32)] * 2 + (_exchange_scratch(8, ng) if ng else []),
        compiler_params=_params(dimension_semantics=("arbitrary", "arbitrary")),
    )(q, k, v, *(gather or []))
    return outs[0], outs[1], outs[2:]


def _mla_delta(o, do, name):
    nh, s, _ = o.shape
    t = _attn_tile(s)

    def body(o_ref, do_ref, delta_ref):
        delta_ref[...] = _as_rows(_rowsum(do_ref[...] * o_ref[...]))

    tile = pl.BlockSpec((None, t, MLA_V), lambda h, i: (h, i, 0))
    return _pallas_call(body, name=name, grid=(nh, s // t), in_specs=[tile, tile],
                        out_specs=pl.BlockSpec((None, None, 8, t), lambda h, i: (h, i, 0, 0)),
                        out_shape=jax.ShapeDtypeStruct((nh, s // t, 8, t), F32),
                        compiler_params=_params(dimension_semantics=("arbitrary", "arbitrary")))(o, do)


def _mla_bwd(q, k, v, do, lse_row, delta_row, name, scatter=None):
    nh, s, _ = q.shape
    t = _attn_tile(s)
    nq = s // t
    ng = len(scatter) if scatter else 0

    def body(q_ref, k_ref, v_ref, do_ref, lse_ref, delta_ref, *rest):
        c_ins, (dq_ref, dk_ref, dv_ref), c_outs = rest[:ng], rest[ng:ng + 3], rest[ng + 3:2 * ng + 3]
        (s0_ref, s1_ref, dp0_ref, dp1_ref), sems = rest[2 * ng + 3:2 * ng + 7], rest[2 * ng + 7:]
        if ng:
            _run_stages_at(_scatter_stages(c_ins, c_outs, *sems), [(0, 0), (nh - 1, nq - 1)])
        kj = pl.program_id(1)
        kv_, vv = k_ref[...], v_ref[...]
        s_refs, dp_refs = (s0_ref, s1_ref), (dp0_ref, dp1_ref)

        @pl.when(kj == 0)
        def _():
            dq_ref[...] = jnp.zeros_like(dq_ref)

        def rows(i):
            return pl.ds(pl.multiple_of(i * t, t), t)

        def issue(i, slot):
            s_refs[slot][...] = _dot_nt(kv_, q_ref[rows(i), :])
            dp_refs[slot][...] = _dot_nt(vv, do_ref[rows(i), :].astype(BF16))

        def consume(i, slot, carry, masked=False):
            dk, dv = carry
            p = jnp.exp2(s_refs[slot][...] - lse_ref[i][0:1, :])
            if masked:
                p = jnp.where(lax.broadcasted_iota(jnp.int32, (t, t), 0) <= lax.broadcasted_iota(jnp.int32, (t, t), 1), p, 0.0)
            dv = dv + jnp.dot(p.astype(BF16), do_ref[rows(i), :].astype(BF16), preferred_element_type=F32)
            ds = (p * (dp_refs[slot][...] - delta_ref[i][0:1, :])).astype(BF16)
            dk = dk + jnp.dot(ds, q_ref[rows(i), :], preferred_element_type=F32)
            dq_ref[rows(i), :] += _dot_tn(ds, kv_) * MLA_SCALE
            return dk, dv

        issue(kj, 0)
        carry = consume(kj, 0, (jnp.zeros((t, MLA_QK), F32), jnp.zeros((t, MLA_V), F32)), masked=True)
        dk, dv = _pipelined_blocks(kj + 1, nq - 1 - kj, nq - 1, issue, consume, carry, False)
        dk_ref[...] = dk * LN2
        dv_ref[...] = dv

    tile = lambda w: pl.BlockSpec((None, t, w), lambda h, j: (h, j, 0))
    whole = lambda w: pl.BlockSpec((None, s, w), lambda h, j: (h, 0, 0))
    rows_spec = pl.BlockSpec((None, nq, 8, t), lambda h, j: (h, 0, 0, 0))
    outs = _pallas_call(
        body, name=name, grid=(nh, nq),
        in_specs=[whole(MLA_QK), tile(MLA_QK), tile(MLA_V), whole(MLA_V), rows_spec, rows_spec] + [_HBM] * ng,
        out_specs=[whole(MLA_QK), tile(MLA_QK), tile(MLA_V)] + [_HBM] * ng,
        out_shape=[jax.ShapeDtypeStruct((nh, s, MLA_QK), F32), jax.ShapeDtypeStruct((nh, s, MLA_QK), F32),
                   jax.ShapeDtypeStruct((nh, s, MLA_V), F32)] + (_scatter_out_shapes(scatter) if ng else []),
        scratch_shapes=[pltpu.VMEM((t, t), F32)] * 4 + (_exchange_scratch(3, ng) if ng else []),
        compiler_params=_params(dimension_semantics=("arbitrary", "arbitrary")),
    )(q, k, v, do, lse_row, delta_row, *(scatter or []))
    return outs[0], outs[1], outs[2], outs[3:]


def _swa_tile(s):
    return min(s, 8 * BLOCK)


def _swa_specs(tq):
    nb = tq // BLOCK
    grp = lambda w: pl.BlockSpec((SWA_GROUP, tq, w), lambda j, i: (j, i, 0))
    main = pl.BlockSpec((None, tq, SWA_HEAD_DIM), lambda j, i: (j, i, 0))
    tail = pl.BlockSpec((None, BLOCK, SWA_HEAD_DIM), lambda j, i: (j, nb * (i + 1), 0))
    sink = pl.BlockSpec((None, SWA_GROUP, 128), lambda j, i: (j, 0, 0))
    return grp, main, tail, sink


def _swa_band_mask(first):
    shape = (SWA_GROUP * BLOCK, 2 * BLOCK)
    q_rel = (lax.broadcasted_iota(jnp.int32, shape, 0) & (BLOCK - 1)) + BLOCK
    k_rel = lax.broadcasted_iota(jnp.int32, shape, 1)
    dist = q_rel - k_rel
    return (dist >= 0) & (dist < BLOCK) & ((k_rel >= BLOCK) | jnp.logical_not(first))


def _swa_sink_column(sink_ref):
    sk = sink_ref[...]
    return jnp.concatenate([jnp.broadcast_to(sk[g:g + 1, 0:1], (BLOCK, 1)) for g in range(SWA_GROUP)], axis=0)


def _swa_fwd(q, kpad, vpad, sinks, name):
    _, s, _ = q.shape
    tq = _swa_tile(s)
    grp, main, tail, sink = _swa_specs(tq)
    d = SWA_HEAD_DIM

    def body(q_ref, km_ref, kt_ref, vm_ref, vt_ref, sink_ref, o_ref, lse_ref):
        i = pl.program_id(1)
        kall = jnp.concatenate([km_ref[...], kt_ref[...]], axis=0)
        vall = jnp.concatenate([vm_ref[...], vt_ref[...]], axis=0)
        sink_col = _swa_sink_column(sink_ref)
        for b in range(tq // BLOCK):
            lo = b * BLOCK
            valid = _swa_band_mask(i == 0 if b == 0 else False)
            q4 = q_ref[:, lo:lo + BLOCK, :].reshape(SWA_GROUP * BLOCK, d)
            sc = jnp.where(valid, _dot_nt(q4, kall[lo:lo + 2 * BLOCK]) * SWA_SCALE, NEG)
            m = jnp.maximum(jnp.max(sc, axis=-1, keepdims=True), sink_col)
            e = jnp.exp(sc - m)
            den = jnp.sum(e, axis=-1, keepdims=True) + jnp.exp(sink_col - m)
            out = jnp.dot((e * (1.0 / den)).astype(BF16), vall[lo:lo + 2 * BLOCK], preferred_element_type=F32)
            o_ref[:, lo:lo + BLOCK, :] = out.reshape(SWA_GROUP, BLOCK, d)
            lse_ref[:, lo:lo + BLOCK, :] = (m + jnp.log(den)).reshape(SWA_GROUP, BLOCK, 1)

    return _pallas_call(
        body, name=name, grid=(SWA_KV_HEADS, s // tq),
        in_specs=[grp(d), main, tail, main, tail, sink], out_specs=[grp(d), grp(1)],
        out_shape=[jax.ShapeDtypeStruct((SWA_HEADS, s, d), F32), jax.ShapeDtypeStruct((SWA_HEADS, s, 1), F32)],
        compiler_params=_params(dimension_semantics=("arbitrary", "arbitrary")),
    )(q, kpad, kpad, vpad, vpad, sinks)


def _swa_bwd(q, kpad, vpad, sinks, o, lse, do, name):
    _, s, _ = q.shape
    tq = _swa_tile(s)
    grp, main, tail, sink = _swa_specs(tq)
    d = SWA_HEAD_DIM

    def body(q_ref, km_ref, kt_ref, vm_ref, vt_ref, sink_ref, o_ref, lse_ref, do_ref, dq_ref, dk_ref, dv_ref, dsink_ref):
        i = pl.program_id(1)
        kall = jnp.concatenate([km_ref[...], kt_ref[...]], axis=0)
        vall = jnp.concatenate([vm_ref[...], vt_ref[...]], axis=0)
        sink_col = _swa_sink_column(sink_ref)

        @pl.when(i == 0)
        def _():
            dk_ref[...] = jnp.zeros_like(dk_ref)
            dv_ref[...] = jnp.zeros_like(dv_ref)
            dsink_ref[...] = jnp.zeros_like(dsink_ref)

        dsink = jnp.zeros((SWA_GROUP * BLOCK, 1), F32)
        for b in range(tq // BLOCK):
            lo = b * BLOCK
            valid = _swa_band_mask(i == 0 if b == 0 else False)
            rows4 = SWA_GROUP * BLOCK
            q4 = q_ref[:, lo:lo + BLOCK, :].reshape(rows4, d)
            do4 = do_ref[:, lo:lo + BLOCK, :].reshape(rows4, d)
            lse4 = lse_ref[:, lo:lo + BLOCK, :].reshape(rows4, 1)
            delta = jnp.sum(do4 * o_ref[:, lo:lo + BLOCK, :].reshape(rows4, d), axis=-1, keepdims=True)
            kb, vb = kall[lo:lo + 2 * BLOCK], vall[lo:lo + 2 * BLOCK]
            do4b = do4.astype(BF16)
            p = jnp.where(valid, jnp.exp(_dot_nt(q4, kb) * SWA_SCALE - lse4), 0.0)
            ds = (p * (_dot_nt(do4b, vb) - delta) * SWA_SCALE).astype(BF16)
            dq_ref[:, lo:lo + BLOCK, :] = jnp.dot(ds, kb, preferred_element_type=F32).reshape(SWA_GROUP, BLOCK, d)
            band = pl.ds(pl.multiple_of(i * tq, BLOCK) + lo, 2 * BLOCK)
            dk_ref[band, :] += _dot_tn(ds, q4)
            dv_ref[band, :] += _dot_tn(p.astype(BF16), do4b)
            dsink = dsink - jnp.exp(sink_col - lse4) * delta
        per_head = [jnp.broadcast_to(jnp.sum(dsink[g * BLOCK:(g + 1) * BLOCK], axis=0, keepdims=True), (1, 128))
                    for g in range(SWA_GROUP)]
        dsink_ref[...] += jnp.concatenate(per_head + [jnp.zeros((8 - SWA_GROUP, 128), F32)], axis=0)

    acc = pl.BlockSpec((None, s + BLOCK, d), lambda j, i: (j, 0, 0))
    return _pallas_call(
        body, name=name, grid=(SWA_KV_HEADS, s // tq),
        in_specs=[grp(d), main, tail, main, tail, sink, grp(d), grp(1), grp(d)],
        out_specs=[grp(d), acc, acc, pl.BlockSpec((None, 8, 128), lambda j, i: (j, 0, 0))],
        out_shape=[jax.ShapeDtypeStruct((SWA_HEADS, s, d), F32),
                   jax.ShapeDtypeStruct((SWA_KV_HEADS, s + BLOCK, d), F32),
                   jax.ShapeDtypeStruct((SWA_KV_HEADS, s + BLOCK, d), F32),
                   jax.ShapeDtypeStruct((SWA_KV_HEADS, 8, 128), F32)],
        compiler_params=_params(dimension_semantics=("arbitrary", "arbitrary")),
    )(q, kpad, kpad, vpad, vpad, sinks, o, lse, do)


def _loss_head(y, target, name):
    s = y.shape[0]
    tm = _row_tile(s, 512)

    def body(y_ref, t_ref, dy_ref, loss_ref):
        i = pl.program_id(0)
        err = y_ref[...] - t_ref[...]
        dy_ref[...] = err * (1.0 / D_MODEL)
        part = jnp.broadcast_to(0.5 * jnp.sum(jnp.mean(err * err, axis=-1, keepdims=True), axis=0, keepdims=True), (1, 128))

        @pl.when(i == 0)
        def _():
            loss_ref[...] = part

        @pl.when(i > 0)
        def _():
            loss_ref[...] += part

    row = pl.BlockSpec((tm, D_MODEL), lambda i: (i, 0))
    return _pallas_call(
        body, name=name, grid=(s // tm,), in_specs=[row, row], out_specs=[row, _full((1, 128))],
        out_shape=[jax.ShapeDtypeStruct((s, D_MODEL), F32), jax.ShapeDtypeStruct((1, 128), F32)],
        compiler_params=_params(dimension_semantics=("arbitrary",)),
    )(y, target)


def _adamw(w, g, m, v, name):
    rows, cols = w.shape
    tr = rows
    for cand in (512, 256, 128, 64, 32, 16, 8):
        if rows % cand == 0 and rows > cand:
            tr = cand
            break

    def body(w_ref, g_ref, m_ref, v_ref, d_ref, nm_ref, nv_ref):
        gv = g_ref[...]
        nm = ADAM_B1 * m_ref[...] + (1.0 - ADAM_B1) * gv
        nv = ADAM_B2 * v_ref[...] + (1.0 - ADAM_B2) * (gv * gv)
        m_hat = nm / (1.0 - ADAM_B1 ** ADAM_STEP)
        v_hat = nv / (1.0 - ADAM_B2 ** ADAM_STEP)
        d_ref[...] = -ADAM_LR * (m_hat / (jnp.sqrt(v_hat) + ADAM_EPS) + ADAM_WD * w_ref[...])
        nm_ref[...] = nm
        nv_ref[...] = nv

    blk = pl.BlockSpec((tr, cols), lambda i: (i, 0))
    return _pallas_call(
        body, name=name, grid=(rows // tr,), in_specs=[blk] * 4, out_specs=[blk] * 3,
        out_shape=[jax.ShapeDtypeStruct((rows, cols), F32)] * 3,
        compiler_params=_params(dimension_semantics=("arbitrary",)),
    )(w, g, m, v)


def _position():
    return lax.axis_index("x"), lax.axis_index("y"), lax.axis_index("c")


def _remote(src, dst, send_sems, recv_sems, k, to):
    return pltpu.make_async_remote_copy(src_ref=src, dst_ref=dst, send_sem=send_sems.at[k], recv_sem=recv_sems.at[k],
                                        device_id=to, device_id_type=MESH)


_HBM = pl.BlockSpec(memory_space=pltpu.HBM)


def _gather_stages(ins, outs, send_sems, recv_sems):
    na = len(ins)
    x, y, c = _position()
    me, sibling = (x, y, c), (x, y, 1 - c)
    xn, yn, dg = (1 - x, y), (x, 1 - y), (1 - x, 1 - y)

    def slot(a, chip, pc, half=None):
        ref = outs[a].at[4 * chip[0] + 2 * chip[1] + pc]
        if half is None:
            return ref
        rows = ref.shape[0] // 2
        return ref.at[pl.ds(half * rows, rows)]

    def cp(a, k, chip, pc, half, to, src=None):
        dst = slot(a, chip, pc, half)
        return _remote(dst if src is None else src, dst, send_sems, recv_sems, 8 * a + k, to)

    first_hop = [(0, xn), (1, yn)]
    second_hop = [(0, xn, 2, 0, yn), (1, yn, 3, 1, xn)]

    def sends():
        out = []
        for a in range(na):
            out += [cp(a, k, (x, y), c, None, (*to, c), src=ins[a].at[c]) for k, to in first_hop]
            out += [cp(a, fwd_k, frm, c, half, (*to, c)) for _, frm, fwd_k, half, to in second_hop]
            out += [cp(a, 4 + k, frm, c, None, sibling) for k, frm in first_hop]
            out += [cp(a, 6 + half, dg, c, half, sibling) for half in (0, 1)]
        return out

    def stage0():
        for a in range(na):
            for k, to in first_hop:
                cp(a, k, (x, y), c, None, (*to, c), src=ins[a].at[c]).start()

    def stage1():
        for k, frm, fwd_k, half, to in second_hop:
            for a in range(na):
                cp(a, k, frm, c, None, me).wait_recv()
                cp(a, fwd_k, frm, c, half, (*to, c)).start()
                cp(a, 4 + k, frm, c, None, sibling).start()

    def stage2():
        for half in (0, 1):
            for a in range(na):
                cp(a, 2 + half, dg, c, half, me).wait_recv()
                cp(a, 6 + half, dg, c, half, sibling).start()

    def stage3():
        for a in range(na):
            for k, chip, half in ((4, xn, None), (5, yn, None), (6, dg, 0), (7, dg, 1)):
                cp(a, k, chip, 1 - c, half, me).wait_recv()
        for sent in sends():
            sent.wait_send()

    return [stage0, stage1, stage2, stage3]


def _gather_out_shapes(mine):
    return [jax.ShapeDtypeStruct((N_DEV,) + m.shape[1:], m.dtype) for m in mine]


def _exchange_scratch(per_array, na):
    return [pltpu.SemaphoreType.DMA((per_array * na,)), pltpu.SemaphoreType.DMA((per_array * na,))]


def _all_gather_halves(mine, name):
    na = len(mine)

    def body(*refs):
        for stage in _gather_stages(refs[:na], refs[na:2 * na], *refs[2 * na:]):
            stage()

    return _pallas_call(body, name=name, in_specs=[_HBM] * na, out_specs=[_HBM] * na, out_shape=_gather_out_shapes(mine),
                        scratch_shapes=_exchange_scratch(8, na))(*mine)


def _sibling_exchange(parts, name, other_half):
    na = len(parts)

    def body(*refs):
        ins, outs, (send_sems, recv_sems) = refs[:na], refs[na:2 * na], refs[2 * na:]
        x, y, c = _position()
        copies = [_remote(ins[a].at[:, 1 - c] if other_half else ins[a], outs[a], send_sems, recv_sems, a, (x, y, 1 - c))
                  for a in range(na)]
        for cp in copies:
            cp.start()
        for cp in copies:
            cp.wait()

    return _pallas_call(
        body, name=name, in_specs=[_HBM] * na, out_specs=[_HBM] * na,
        out_shape=[jax.ShapeDtypeStruct(p.shape[:1] + p.shape[2:] if other_half else p.shape, p.dtype) for p in parts],
        scratch_shapes=_exchange_scratch(1, na),
    )(*parts)


def _scatter_stages(ins, outs, send_sems, recv_sems):
    na = len(ins)
    x, y, c = _position()
    chips = [(1 - x, y), (x, 1 - y), (1 - x, 1 - y)]

    def copies():
        return [_remote(ins[a].at[2 * px + py], outs[a].at[j], send_sems, recv_sems, 3 * a + j, (px, py, c))
                for a in range(na) for j, (px, py) in enumerate(chips)]

    def start():
        for cp in copies():
            cp.start()

    def wait():
        for cp in copies():
            cp.wait()

    return [start, wait]


def _scatter_out_shapes(parts):
    return [jax.ShapeDtypeStruct((3,) + p.shape[1:], p.dtype) for p in parts]


def _scatter_to_chips(parts, name):
    na = len(parts)

    def body(*refs):
        for stage in _scatter_stages(refs[:na], refs[na:2 * na], *refs[2 * na:]):
            stage()

    return _pallas_call(body, name=name, in_specs=[_HBM] * na, out_specs=[_HBM] * na, out_shape=_scatter_out_shapes(parts),
                        scratch_shapes=_exchange_scratch(3, na))(*parts)


def _assemble(g4, mine, name, side_by_side, order=None):
    _, nl, r, w = g4.shape
    tr = _divisor_tile(r, 640)

    def body(in_ref, mine_ref, out_ref):
        chip = 2 * lax.axis_index("x") + lax.axis_index("y")
        blocks = [jnp.where(chip == sh, mine_ref[...], in_ref[sh]) for sh in range(N_SHARD)]
        if side_by_side:
            full = jnp.concatenate(blocks, axis=-1)
            out_ref[...] = full if order is None else _take_cols(full, order)
        else:
            for sh in range(N_SHARD):
                out_ref[sh] = blocks[sh]

    if side_by_side:
        out_spec = pl.BlockSpec((None, tr, N_SHARD * w), lambda l, i: (l, i, 0))
        out_shape = jax.ShapeDtypeStruct((nl, r, N_SHARD * w), g4.dtype)
    else:
        out_spec = pl.BlockSpec((None, N_SHARD, tr, w), lambda l, i: (l, 0, i, 0))
        out_shape = jax.ShapeDtypeStruct((nl, N_SHARD, r, w), g4.dtype)
    return _pallas_call(
        body, name=name, grid=(nl, r // tr),
        in_specs=[pl.BlockSpec((N_SHARD, None, tr, w), lambda l, i: (0, l, i, 0)),
                  pl.BlockSpec((None, tr, w), lambda l, i: (l, i, 0))],
        out_specs=out_spec, out_shape=out_shape,
        compiler_params=_params(dimension_semantics=("arbitrary", "arbitrary")),
    )(g4, mine)


def _all_reduce_small(vec, name):
    r, l = vec.shape

    def body(v_ref, out_ref, gath_ref, send_sems, recv_sems):
        x, y, c = _position()
        me = 4 * x + 2 * y + c
        gath_ref[me] = v_ref[...]
        copies = []
        for k in range(1, N_DEV):
            to = (x ^ (k >> 2), y ^ ((k >> 1) & 1), c ^ (k & 1))
            copies.append(_remote(gath_ref.at[me], gath_ref.at[me], send_sems, recv_sems, k - 1, to))
        for cp in copies:
            cp.start()
        for k in range(1, N_DEV):
            frm = 4 * (x ^ (k >> 2)) + 2 * (y ^ ((k >> 1) & 1)) + (c ^ (k & 1))
            _remote(gath_ref.at[frm], gath_ref.at[frm], send_sems, recv_sems, k - 1, (x, y, c)).wait_recv()
        for cp in copies:
            cp.wait_send()
        total = gath_ref[0]
        for d in range(1, N_DEV):
            total = total + gath_ref[d]
        out_ref[...] = total

    vm = pl.BlockSpec(memory_space=pltpu.VMEM)
    return _pallas_call(
        body, name=name, in_specs=[vm], out_specs=vm, out_shape=jax.ShapeDtypeStruct((r, l), F32),
        scratch_shapes=[pltpu.VMEM((N_DEV, r, l), F32), pltpu.SemaphoreType.DMA((N_DEV - 1,)),
                        pltpu.SemaphoreType.DMA((N_DEV - 1,))],
    )(vec)


def _sum_blocks(blocks, out_dtype, name):
    m, w = blocks[0].shape
    tr = _divisor_tile(m, 1024)

    def body(*refs):
        total = refs[0][...].astype(F32)
        for ref in refs[1:-1]:
            total = total + ref[...].astype(F32)
        refs[-1][...] = total.astype(out_dtype)

    blk = pl.BlockSpec((tr, w), lambda i: (i, 0))
    return _pallas_call(
        body, name=name, grid=(m // tr,), in_specs=[blk] * len(blocks), out_specs=blk,
        out_shape=jax.ShapeDtypeStruct((m, w), out_dtype),
        compiler_params=_params(dimension_semantics=("arbitrary",)),
    )(*blocks)


FIRST_GROUPS = (('ffn1_w_gate', 'ffn1_w_up', 'ffn1_w_down'),)
REST_GROUPS = (('ffn2_w_gate', 'ffn2_w_up', 'ffn2_w_down', 'w_o'), ('w_in',), ('mla_w_q_b',), ('mla_w_kv_b',))
N_FIRST = len(FIRST_GROUPS)


def _shard_rows(name):
    shape, axis = BIG[name]
    return shape[0] // N_SHARD if axis == 0 else shape[0]


def _group_row_offsets(group):
    return [int(v) for v in np.cumsum([0] + [_shard_rows(n) for n in group])]


def _rope_tables(s):
    pos = jnp.arange(s, dtype=F32)
    inv = 1.0 / (ROPE_THETA ** (jnp.arange(0, MLA_ROPE, 2, dtype=F32) / MLA_ROPE))
    ang = pos[:, None] * inv[None, :]
    return jnp.tile(jnp.cos(ang), (1, SWA_HEADS)), jnp.tile(jnp.sin(ang), (1, SWA_HEADS))


_MIXER_GAINS = ('mix_norm', 'mla_q_a_norm', 'mla_kv_a_norm', 'mla_q_norm', 'mla_k_norm', 'swa_q_norm', 'swa_k_norm')


def _local_step(x, target, small, ex):
    s = x.shape[0]
    cos, sin = _rope_tables(s)
    row = lambda name, l: small[name][l][None, :]
    saved, bigs = [], []
    for l in range(DEPTH):
        big = ex.first_weights(l)
        sv = {'x0': x}
        x, sv['g1'], sv['u1'], got = _ffn_fwd(x, row('ffn1_norm', l), big['ffn1_w_gate'], big['ffn1_w_up'], big['ffn1_w_down'],
                                             f"ffn1_fwd_{l}", ex.gather_behind_ffn1(l))
        big.update(ex.rest_weights(l, got))
        bigs.append(big)
        sv['x1'] = x
        gains = [row(n, l) for n in _MIXER_GAINS]
        mixer_w = (big['w_in'], big['mla_w_q_b'], big['mla_w_kv_b'])
        q_a, k_a, v_a, q_b, k_b, v_b = _pre_fwd(x, gains, *mixer_w, cos, sin, f"pre_fwd_{l}")
        o_a, lse, got = _mla_fwd(q_a, k_a, v_a, f"mla_fwd_{l}", ex.gather_behind_mla(l))
        ex.gathered_behind_mla(l, got)
        kpad = jnp.pad(k_b, ((0, 0), (BLOCK, 0), (0, 0)))
        vpad = jnp.pad(v_b, ((0, 0), (BLOCK, 0), (0, 0)))
        sinks = jnp.broadcast_to(small['swa_sinks'][l].reshape(SWA_KV_HEADS, SWA_GROUP, 1), (SWA_KV_HEADS, SWA_GROUP, 128))
        o_b, lse_b = _swa_fwd(q_b, kpad, vpad, sinks, f"swa_fwd_{l}")
        sv.update(gains=gains, mixer_w=mixer_w, q_a=q_a, k_a=k_a, v_a=v_a, q_b=q_b, kpad=kpad, vpad=vpad, sinks=sinks,
                  o_a=o_a, lse=lse, o_b=o_b, lse_b=lse_b)
        x = _post_fwd(x, o_a, o_b, row('mla_out_norm', l), row('swa_out_norm', l), big['w_o'], f"post_fwd_{l}")
        sv['x2'] = x
        x, sv['g2'], sv['u2'], _ = _ffn_fwd(x, row('ffn2_norm', l), big['ffn2_w_gate'], big['ffn2_w_up'], big['ffn2_w_down'],
                                           f"ffn2_fwd_{l}")
        saved.append(sv)

    dx, loss = _loss_head(x, target, "loss_head")

    gs = {n: [None] * DEPTH for n in SMALL_NAMES}
    t = _attn_tile(s)
    for l in reversed(range(DEPTH)):
        sv, big = saved[l], bigs[l]

        def ffn_back(tag, xin, dy, gate, up, scatter=None):
            (dxi, dgain, nb, act, dgate, dup), got = _ffn_bwd(xin, dy, gate, up, row(tag + '_norm', l), big[tag + '_w_gate'],
                                                             big[tag + '_w_up'], big[tag + '_w_down'], f"{tag}_bwd_{l}", scatter)
            gs[tag + '_norm'][l] = dgain[0]
            shards = lambda full: full.reshape(N_SHARD, D_FF // N_SHARD, D_MODEL)
            grads = {tag + '_w_gate': shards(_matmul_tn(dgate, nb, 1.0, f"{tag}_dwg_{l}")),
                     tag + '_w_up': shards(_matmul_tn(dup, nb, 1.0, f"{tag}_dwu_{l}")),
                     tag + '_w_down': shards(_matmul_tn(act, dy, 0.5, f"{tag}_dwd_{l}"))}
            return dxi, grads, got

        dx, rest_grads, _ = ffn_back('ffn2', sv['x2'], dx, sv['g2'], sv['u2'])
        do_a, do_b, dga, dgb, dwo = _post_bwd(dx, sv['o_a'], sv['o_b'], row('mla_out_norm', l), row('swa_out_norm', l),
                                              big['w_o'], f"post_bwd_{l}")
        gs['mla_out_norm'][l], gs['swa_out_norm'][l] = dga[0], dgb[0]
        rest_grads['w_o'] = dwo.reshape(N_SHARD, MIX_WIDTH // N_SHARD, D_MODEL)
        delta = _mla_delta(sv['o_a'], do_a, f"mla_delta_{l}")
        dq_a, dk_a, dv_a, got = _mla_bwd(sv['q_a'], sv['k_a'], sv['v_a'], do_a, sv['lse'], delta, f"mla_bwd_{l}",
                                         ex.scatter_behind_mla(l))
        ex.scattered_behind_mla(l, got)
        dq_b, dkpad, dvpad, dsink = _swa_bwd(sv['q_b'], sv['kpad'], sv['vpad'], sv['sinks'], sv['o_b'], sv['lse_b'], do_b,
                                             f"swa_bwd_{l}")
        gs['swa_sinks'][l] = dsink[:, :SWA_GROUP, 0].reshape(SWA_HEADS)
        cts = [dq_a, dk_a, dv_a, dq_b, dkpad[:, BLOCK:], dvpad[:, BLOCK:]]
        outs = _pre_bwd(sv['x1'], dx, cts, sv['gains'], *sv['mixer_w'], cos, sin, f"pre_bwd_{l}")
        dx = outs[0]
        for n, val in zip(_MIXER_GAINS, outs[1:8]):
            gs[n][l] = val[0]
        rest_grads['w_in'], rest_grads['mla_w_q_b'], rest_grads['mla_w_kv_b'] = outs[8:11]
        ex.grads_ready(l, 'rest', rest_grads)
        dx, first_grads, got = ffn_back('ffn1', sv['x0'], dx, sv['g1'], sv['u1'], ex.scatter_behind_ffn1(l))
        ex.scattered_behind_ffn1(l, got)
        ex.grads_ready(l, 'first', first_grads)
    return loss, dx, gs


class _Exchange:
    def __init__(self, weights, c, chip):
        halves_of = lambda a: a.reshape(a.shape[:-2] + (2, a.shape[-2] // 2, a.shape[-1]))
        self.halves_of, self.c, self.chip = halves_of, c, chip
        self.mine = [[halves_of(jnp.concatenate([weights[n][l].astype(BF16) for n in group], axis=0))
                      for group in FIRST_GROUPS + REST_GROUPS] for l in range(DEPTH)]
        self.ahead, self.begun, self.received = {}, {}, {}

    def _assembled(self, l, which, gathered):
        groups, base = (FIRST_GROUPS, 0) if which == 'first' else (REST_GROUPS, N_FIRST)
        big = {}
        for gi, group in enumerate(groups):
            offs = _group_row_offsets(group)
            _, rh, w = self.mine[l][base + gi].shape
            col_sharded = BIG[group[0]][1] == 1
            full = _assemble(gathered[gi].reshape(N_SHARD, 1, 2 * rh, w), self.mine[l][base + gi].reshape(1, 2 * rh, w),
                             f"assemble_{which}{gi}_{l}", col_sharded, MIXER_ORDERS.get(group[0]))
            for i, n in enumerate(group):
                rows = offs[i + 1] - offs[i]
                if len(group) > 1 and offs[i] % rows == 0:
                    big[n] = ((full, (None, rows, N_SHARD * w), (0, offs[i] // rows, 0)) if col_sharded else
                              (full, (None, N_SHARD, rows, w), (0, 0, offs[i] // rows, 0)))
                elif col_sharded:
                    big[n] = full[0, offs[i]:offs[i + 1]]
                else:
                    big[n] = full[0, :, offs[i]:offs[i + 1]].reshape(BIG[n][0])
        return big

    def first_weights(self, l):
        got = self.ahead[l][:N_FIRST] if l in self.ahead else _all_gather_halves(self.mine[l][:N_FIRST], f"gather_first_{l}")
        return self._assembled(l, 'first', got)

    def gather_behind_ffn1(self, l):
        return None if l in self.ahead else self.mine[l][N_FIRST:]

    def rest_weights(self, l, got):
        return self._assembled(l, 'rest', self.ahead[l][N_FIRST:] if l in self.ahead else got)

    def gather_behind_mla(self, l):
        return self.mine[l + 1] if l + 1 < DEPTH else None

    def gathered_behind_mla(self, l, got):
        if got:
            self.ahead[l + 1] = got

    def grads_ready(self, l, which, grads):
        groups = FIRST_GROUPS if which == 'first' else REST_GROUPS
        parts = [self.halves_of(jnp.concatenate([grads[n] for n in group], axis=1)) for group in groups]
        from_sibling = _sibling_exchange(parts, f"swap_{which}_{l}", True)
        chip_sums = []
        for gi, (p, got) in enumerate(zip(parts, from_sibling)):
            kept = lax.dynamic_index_in_dim(p, self.c, axis=1, keepdims=False)
            rows = N_SHARD * p.shape[2]
            pair = _sum_blocks([kept.reshape(rows, -1), got.reshape(rows, -1)], BF16, f"sum_pair_{which}{gi}_{l}")
            chip_sums.append(pair.reshape(got.shape))
        self.begun[(l, which)] = chip_sums

    def scatter_behind_mla(self, l):
        return self.begun[(l + 1, 'first')] + self.begun[(l + 1, 'rest')] if l + 1 < DEPTH else None

    def scattered_behind_mla(self, l, got):
        if got:
            self.received[(l + 1, 'first')], self.received[(l + 1, 'rest')] = got[:N_FIRST], got[N_FIRST:]

    def scatter_behind_ffn1(self, l):
        return self.begun[(l, 'rest')] if l == 0 else None

    def scattered_behind_ffn1(self, l, got):
        if got:
            self.received[(l, 'rest')] = got

    def reduced(self):
        keys = sorted(self.begun)
        for key in keys:
            if key not in self.received:
                self.received[key] = _scatter_to_chips(self.begun[key], f"scatter_{key[1]}_{key[0]}")
        halves = []
        for l, which in keys:
            for gi, (cs, got) in enumerate(zip(self.begun[(l, which)], self.received[(l, which)])):
                own = lax.dynamic_index_in_dim(cs, self.chip, axis=0, keepdims=False)
                halves.append(_sum_blocks([own, got[0], got[1], got[2]], F32, f"sum_chips_{which}{gi}_{l}"))
        others = _sibling_exchange(halves, "share_halves", False)
        per_layer, at = {}, 0
        for l, which in keys:
            for group in (FIRST_GROUPS if which == 'first' else REST_GROUPS):
                mine_h, other_h = halves[at], others[at]
                at += 1
                full = jnp.where(self.c == 0, jnp.concatenate([mine_h, other_h]), jnp.concatenate([other_h, mine_h]))
                offs = _group_row_offsets(group)
                for i, n in enumerate(group):
                    per_layer[(n, l)] = full[offs[i]:offs[i + 1]]
        return {n: jnp.stack([per_layer[(n, l)] for l in range(DEPTH)]) for n in BIG_NAMES}


def kernel(x, ffn1_norm, ffn1_w_gate, ffn1_w_up, ffn1_w_down, mix_norm, w_in, mla_q_a_norm, mla_w_q_b, mla_kv_a_norm, mla_w_kv_b, mla_q_norm, mla_k_norm, swa_q_norm, swa_k_norm, swa_sinks, mla_out_norm, swa_out_norm, w_o, ffn2_norm, ffn2_w_gate, ffn2_w_up, ffn2_w_down, loss_target, m_ffn1_norm, m_ffn1_w_gate, m_ffn1_w_up, m_ffn1_w_down, m_mix_norm, m_w_in, m_mla_q_a_norm, m_mla_w_q_b, m_mla_kv_a_norm, m_mla_w_kv_b, m_mla_q_norm, m_mla_k_norm, m_swa_q_norm, m_swa_k_norm, m_swa_sinks, m_mla_out_norm, m_swa_out_norm, m_w_o, m_ffn2_norm, m_ffn2_w_gate, m_ffn2_w_up, m_ffn2_w_down, v_ffn1_norm, v_ffn1_w_gate, v_ffn1_w_up, v_ffn1_w_down, v_mix_norm, v_w_in, v_mla_q_a_norm, v_mla_w_q_b, v_mla_kv_a_norm, v_mla_w_kv_b, v_mla_q_norm, v_mla_k_norm, v_swa_q_norm, v_swa_k_norm, v_swa_sinks, v_mla_out_norm, v_swa_out_norm, v_w_o, v_ffn2_norm, v_ffn2_w_gate, v_ffn2_w_up, v_ffn2_w_down):
    args = dict(locals())
    transposed = lambda a: jnp.swapaxes(a, 1, 2)
    as_kernels_see = lambda n, a: transposed(a) if n in TRANSPOSED else a
    weights = {n: as_kernels_see(n, args[n]) for n in WEIGHT_NAMES}
    mom_m = {n: as_kernels_see(n, args["m_" + n]) for n in WEIGHT_NAMES}
    mom_v = {n: as_kernels_see(n, args["v_" + n]) for n in WEIGHT_NAMES}
    ex = _Exchange(weights, lax.axis_index("c"), 2 * lax.axis_index("x") + lax.axis_index("y"))
    loss, dx, gs = _local_step(x[0], loss_target[0], {n: weights[n] for n in SMALL_NAMES}, ex)

    small_flat = jnp.concatenate([jnp.stack(gs[n]).reshape(-1) for n in SMALL_NAMES] + [loss[0, :1]])
    n_small = small_flat.shape[0]
    lanes = -(-n_small // (8 * 128)) * 128
    small_sum = _all_reduce_small(jnp.pad(small_flat, (0, 8 * lanes - n_small)).reshape(8, lanes), "reduce_small").reshape(-1)
    grads = ex.reduced()
    off = 0
    for n in SMALL_NAMES:
        cnt = int(np.prod(weights[n].shape))
        grads[n] = small_sum[off:off + cnt].reshape(weights[n].shape)
        off += cnt
    loss_out = small_sum[off]

    deltas, new_m, new_v = {}, {}, {}
    for n in WEIGHT_NAMES:
        shp = weights[n].shape
        two_d = (DEPTH, shp[-1]) if len(shp) == 2 else (shp[0] * shp[1], shp[2])
        d, nm, nv = _adamw(weights[n].reshape(two_d), grads[n].reshape(two_d), mom_m[n].reshape(two_d),
                           mom_v[n].reshape(two_d), f"adamw_{n}")
        deltas[n], new_m[n], new_v[n] = d.reshape(shp), nm.reshape(shp), nv.reshape(shp)
    for n in TRANSPOSED:
        grads[n], deltas[n], new_m[n], new_v[n] = (transposed(a) for a in (grads[n], deltas[n], new_m[n], new_v[n]))

    return (loss_out, dx[None], *[grads[n] for n in WEIGHT_NAMES], *[deltas[n] for n in WEIGHT_NAMES],
            *[new_m[n] for n in WEIGHT_NAMES], *[new_v[n] for n in WEIGHT_NAMES])
```

```python
import functools

import numpy as np
import jax
import jax.numpy as jnp
from jax import lax
from jax.experimental import pallas as pl
from jax.experimental.pallas import tpu as pltpu

F32 = jnp.float32
BF16 = jnp.bfloat16

D_MODEL = 1024
DEPTH = 2
EPS = 1e-6
ROPE_THETA = 10000.0
BLOCK = 128
MLA_HEADS = 4
MLA_Q_RANK = 256
MLA_KV_RANK = 128
MLA_NOPE = 128
MLA_ROPE = 64
MLA_V = 128
MLA_QK = MLA_NOPE + MLA_ROPE
MLA_WIDTH = MLA_HEADS * MLA_V
SWA_HEADS = 8
SWA_KV_HEADS = 2
SWA_GROUP = SWA_HEADS // SWA_KV_HEADS
SWA_HEAD_DIM = 64
SWA_WIDTH = SWA_HEADS * SWA_HEAD_DIM
MIX_WIDTH = MLA_WIDTH + SWA_WIDTH
IN_SPLITS = (MLA_Q_RANK, MLA_KV_RANK, MLA_ROPE, SWA_WIDTH, SWA_KV_HEADS * SWA_HEAD_DIM, SWA_KV_HEADS * SWA_HEAD_DIM)
IN_COLS = sum(IN_SPLITS)
IN_OFFS = tuple(int(v) for v in np.cumsum((0,) + IN_SPLITS))
D_FF = 2816
MLA_SCALE = MLA_QK ** -0.5
LOG2E = 1.4426950408889634
LN2 = 0.6931471805599453
MLA_QSCALE = MLA_SCALE * LOG2E
SWA_SCALE = SWA_HEAD_DIM ** -0.5
NEG = -1e30

ADAM_LR = 0.001
ADAM_B1 = 0.9
ADAM_B2 = 0.999
ADAM_EPS = 1e-08
ADAM_WD = 0.01
ADAM_STEP = 10

N_SHARD = 4
N_DEV = 8
VMEM_LIMIT = 56 * 1024 * 1024
MESH = pl.DeviceIdType.MESH

WEIGHT_NAMES = ['ffn1_norm', 'ffn1_w_gate', 'ffn1_w_up', 'ffn1_w_down', 'mix_norm', 'w_in', 'mla_q_a_norm', 'mla_w_q_b',
                'mla_kv_a_norm', 'mla_w_kv_b', 'mla_q_norm', 'mla_k_norm', 'swa_q_norm', 'swa_k_norm', 'swa_sinks',
                'mla_out_norm', 'swa_out_norm', 'w_o', 'ffn2_norm', 'ffn2_w_gate', 'ffn2_w_up', 'ffn2_w_down']
TRANSPOSED = ('ffn1_w_gate', 'ffn1_w_up', 'ffn2_w_gate', 'ffn2_w_up')
BIG = {'ffn1_w_gate': ((D_FF, D_MODEL), 0), 'ffn1_w_up': ((D_FF, D_MODEL), 0), 'ffn1_w_down': ((D_FF, D_MODEL), 0),
       'w_in': ((D_MODEL, IN_COLS), 1), 'mla_w_q_b': ((MLA_Q_RANK, MLA_HEADS * MLA_QK), 1),
       'mla_w_kv_b': ((MLA_KV_RANK, MLA_HEADS * (MLA_NOPE + MLA_V)), 1), 'w_o': ((MIX_WIDTH, D_MODEL), 0),
       'ffn2_w_gate': ((D_FF, D_MODEL), 0), 'ffn2_w_up': ((D_FF, D_MODEL), 0), 'ffn2_w_down': ((D_FF, D_MODEL), 0)}
BIG_NAMES = [n for n in WEIGHT_NAMES if n in BIG]
SMALL_NAMES = [n for n in WEIGHT_NAMES if n not in BIG]

_pallas_call = pl.pallas_call


def _params(**kw):
    return pltpu.CompilerParams(vmem_limit_bytes=VMEM_LIMIT, **kw)


def _full(shape):
    n = len(shape)
    return pl.BlockSpec(shape, lambda *_: (0,) * n)


def _resident(shape):
    n = len(shape)
    return pl.BlockSpec(shape, lambda *_: (0,) * n, pipeline_mode=pl.Buffered(1))


@jax.custom_vjp
def _mm(a, w):
    return jnp.dot(a.astype(BF16), w, preferred_element_type=F32)


def _mm_fwd(a, w):
    return _mm(a, w), w


def _mm_bwd(w, dy):
    return lax.dot_general(dy.astype(BF16), w, (((1,), (1,)), ((), ())), preferred_element_type=F32), None


_mm.defvjp(_mm_fwd, _mm_bwd)


def _dot_nt(a, b):
    return lax.dot_general(a, b, (((1,), (1,)), ((), ())), preferred_element_type=F32)


def _dot_tn(a, b):
    return lax.dot_general(a, b, (((0,), (0,)), ((), ())), preferred_element_type=F32)


def _rms(t, g):
    return t * lax.rsqrt(jnp.mean(t * t, axis=-1, keepdims=True) + EPS) * g


def _sigmoid(z):
    return 1.0 / (1.0 + jnp.exp(-z))


def _row_tile(s, want):
    return min(want, s)


def _divisor_tile(rows, cap):
    return max(d for d in range(16, min(rows, cap) + 1, 16) if rows % d == 0)


FF_CHUNK = 1408


def _weight_operand(w):
    if isinstance(w, tuple):
        arr, block, index = w
        return arr, pl.BlockSpec(block, lambda *_: index, pipeline_mode=pl.Buffered(1))
    return w, _resident(w.shape)


def _weight_rows(ref, start, n):
    if len(ref.shape) == 2:
        return ref[start:start + n, :]
    per = ref.shape[1]
    return ref[start // per:(start + n) // per].reshape(n, ref.shape[2])


def _ffn_fwd(x, g, wg, wu, wd, name, gather=None):
    s = x.shape[0]
    tm = _row_tile(s, 256)
    steps = s // tm
    ng = len(gather) if gather else 0
    (wg, wg_spec), (wu, wu_spec), (wd, wd_spec) = _weight_operand(wg), _weight_operand(wu), _weight_operand(wd)

    def body(x_ref, g_ref, wg_ref, wu_ref, wd_ref, *rest):
        g_ins, (y_ref, gate_ref, up_ref), g_outs, sems = rest[:ng], rest[ng:ng + 3], rest[ng + 3:2 * ng + 3], rest[2 * ng + 3:]
        if ng:
            _run_stages_at(_gather_stages(g_ins, g_outs, *sems), [(0,), (steps * 3 // 8,), (steps * 11 // 16,), (steps - 1,)])
        xv = x_ref[...]
        nb = _rms(xv, g_ref[...]).astype(BF16)
        acc = xv
        for c in range(0, D_FF, FF_CHUNK):
            gate = _dot_nt(nb, _weight_rows(wg_ref, c, FF_CHUNK))
            up = _dot_nt(nb, _weight_rows(wu_ref, c, FF_CHUNK))
            gate_ref[:, c:c + FF_CHUNK] = gate.astype(BF16)
            up_ref[:, c:c + FF_CHUNK] = up.astype(BF16)
            act = (gate * _sigmoid(gate) * up).astype(BF16)
            acc = acc + 0.5 * jnp.dot(act, _weight_rows(wd_ref, c, FF_CHUNK), preferred_element_type=F32)
        y_ref[...] = acc

    outs = _pallas_call(
        body, name=name, grid=(steps,),
        in_specs=[pl.BlockSpec((tm, D_MODEL), lambda i: (i, 0)), _full((1, D_MODEL)), wg_spec, wu_spec, wd_spec] + [_HBM] * ng,
        out_specs=[pl.BlockSpec((tm, D_MODEL), lambda i: (i, 0)), pl.BlockSpec((tm, D_FF), lambda i: (i, 0)),
                   pl.BlockSpec((tm, D_FF), lambda i: (i, 0))] + [_HBM] * ng,
        out_shape=[jax.ShapeDtypeStruct((s, D_MODEL), F32), jax.ShapeDtypeStruct((s, D_FF), BF16),
                   jax.ShapeDtypeStruct((s, D_FF), BF16)] + (_gather_out_shapes(gather) if ng else []),
        scratch_shapes=_exchange_scratch(8, ng) if ng else [],
        compiler_params=_params(dimension_semantics=("arbitrary",)),
    )(x, g, wg, wu, wd, *(gather or []))
    return outs[0], outs[1], outs[2], outs[3:]


def _ffn_bwd(x, dy, gate, up, g, wg, wu, wd, name, scatter=None):
    s = x.shape[0]
    tm = _row_tile(s, 256)
    steps = s // tm
    ng = len(scatter) if scatter else 0
    (wg, wg_spec), (wu, wu_spec), (wd, wd_spec) = _weight_operand(wg), _weight_operand(wu), _weight_operand(wd)

    def body(x_ref, dy_ref, gate_ref, up_ref, g_ref, wg_ref, wu_ref, wd_ref, *rest):
        c_ins, (dx_ref, dgain_ref, n_ref, act_ref, dgate_ref, dup_ref) = rest[:ng], rest[ng:ng + 6]
        c_outs, sems = rest[ng + 6:2 * ng + 6], rest[2 * ng + 6:]
        if ng:
            _run_stages_at(_scatter_stages(c_ins, c_outs, *sems), [(0,), (steps - 1,)])
        i = pl.program_id(0)
        xv = x_ref[...]
        dyv = dy_ref[...]
        gv = g_ref[...]
        r = lax.rsqrt(jnp.mean(xv * xv, axis=-1, keepdims=True) + EPS)
        xh = xv * r
        n_ref[...] = (xh * gv).astype(BF16)
        dyh = (0.5 * dyv).astype(BF16)
        dn = jnp.zeros_like(xv)
        for c in range(0, D_FF, FF_CHUNK):
            dact = _dot_nt(dyh, _weight_rows(wd_ref, c, FF_CHUNK))
            gt = gate_ref[:, c:c + FF_CHUNK].astype(F32)
            u = up_ref[:, c:c + FF_CHUNK].astype(F32)
            sg = _sigmoid(gt)
            sl = gt * sg
            act_ref[:, c:c + FF_CHUNK] = (sl * u).astype(BF16)
            dup = (dact * sl).astype(BF16)
            dgate = (dact * u * (sg * (1.0 + gt * (1.0 - sg)))).astype(BF16)
            dup_ref[:, c:c + FF_CHUNK] = dup
            dgate_ref[:, c:c + FF_CHUNK] = dgate
            dn = (dn + jnp.dot(dgate, _weight_rows(wg_ref, c, FF_CHUNK), preferred_element_type=F32)
                  + jnp.dot(dup, _weight_rows(wu_ref, c, FF_CHUNK), preferred_element_type=F32))
        part = jnp.sum(dn * xh, axis=0, keepdims=True)

        @pl.when(i == 0)
        def _():
            dgain_ref[...] = part

        @pl.when(i > 0)
        def _():
            dgain_ref[...] += part

        dxh = dn * gv
        dx_ref[...] = dyv + r * (dxh - xh * jnp.mean(dxh * xh, axis=-1, keepdims=True))

    row = lambda w: pl.BlockSpec((tm, w), lambda i: (i, 0))
    outs = _pallas_call(
        body, name=name, grid=(steps,),
        in_specs=[row(D_MODEL), row(D_MODEL), row(D_FF), row(D_FF), _full((1, D_MODEL)), wg_spec, wu_spec, wd_spec]
        + [_HBM] * ng,
        out_specs=[row(D_MODEL), _full((1, D_MODEL)), row(D_MODEL), row(D_FF), row(D_FF), row(D_FF)] + [_HBM] * ng,
        out_shape=[jax.ShapeDtypeStruct((s, D_MODEL), F32), jax.ShapeDtypeStruct((1, D_MODEL), F32),
                   jax.ShapeDtypeStruct((s, D_MODEL), BF16), jax.ShapeDtypeStruct((s, D_FF), BF16),
                   jax.ShapeDtypeStruct((s, D_FF), BF16), jax.ShapeDtypeStruct((s, D_FF), BF16)]
        + (_scatter_out_shapes(scatter) if ng else []),
        scratch_shapes=_exchange_scratch(3, ng) if ng else [],
        compiler_params=_params(dimension_semantics=("arbitrary",)),
    )(x, dy, gate, up, g, wg, wu, wd, *(scatter or []))
    return outs[:6], outs[6:]


def _store_col_shards(o_ref, acc, first_shard, n_here, width):
    for q in range(n_here):
        o_ref[q] = acc[:, (first_shard + q) * width:(first_shard + q + 1) * width].astype(BF16)


def _matmul_tn(a, b, scale, name):
    t, m = a.shape
    n = b.shape[1]
    tk = _row_tile(t, 2048)
    tn = n // 2
    nk = t // tk

    def body(a_ref, b_ref, o_ref, acc_ref):
        k = pl.program_id(1)
        bv = b_ref[...]
        if scale != 1.0:
            bv = bv.astype(F32) * scale
        part = _dot_tn(a_ref[...].astype(BF16), bv.astype(BF16))

        @pl.when(k == 0)
        def _():
            acc_ref[...] = part

        @pl.when(k > 0)
        def _():
            acc_ref[...] += part

        @pl.when(k == nk - 1)
        def _():
            o_ref[...] = acc_ref[...].astype(BF16)

    return _pallas_call(
        body, name=name, grid=(n // tn, nk),
        in_specs=[pl.BlockSpec((tk, m), lambda j, k: (k, 0)), pl.BlockSpec((tk, tn), lambda j, k: (k, j))],
        out_specs=pl.BlockSpec((m, tn), lambda j, k: (0, j)), out_shape=jax.ShapeDtypeStruct((m, n), BF16),
        scratch_shapes=[pltpu.VMEM((m, tn), F32)],
        compiler_params=_params(dimension_semantics=("arbitrary", "arbitrary")),
    )(a, b)


_HALF = SWA_HEAD_DIM // 2
_IN_ORDER = (list(range(0, IN_OFFS[2]))
             + [IN_OFFS[3] + SWA_HEAD_DIM * h + i for h in range(SWA_HEADS) for i in range(_HALF)]
             + [IN_OFFS[3] + SWA_HEAD_DIM * h + _HALF + i for h in range(SWA_HEADS) for i in range(_HALF)]
             + list(range(IN_OFFS[5], IN_OFFS[6]))
             + [IN_OFFS[4] + SWA_HEAD_DIM * j + i for j in range(SWA_KV_HEADS) for i in range(_HALF)]
             + [IN_OFFS[4] + SWA_HEAD_DIM * j + _HALF + i for j in range(SWA_KV_HEADS) for i in range(_HALF)]
             + list(range(IN_OFFS[2], IN_OFFS[3])))
_QB_ORDER = ([MLA_QK * h + i for h in range(MLA_HEADS) for i in range(MLA_NOPE)]
             + [MLA_QK * h + MLA_NOPE + i for h in range(MLA_HEADS) for i in range(_HALF)]
             + [MLA_QK * h + MLA_NOPE + _HALF + i for h in range(MLA_HEADS) for i in range(_HALF)])
_KVB_ORDER = ([(MLA_NOPE + MLA_V) * h + i for h in range(MLA_HEADS) for i in range(MLA_NOPE)]
              + [(MLA_NOPE + MLA_V) * h + MLA_NOPE + i for h in range(MLA_HEADS) for i in range(MLA_V)])
MIXER_ORDERS = {'w_in': _IN_ORDER, 'mla_w_q_b': _QB_ORDER, 'mla_w_kv_b': _KVB_ORDER}
_P_CQ, _P_CKV, _P_QA, _P_QB, _P_VS, _P_KA, _P_KB, _P_PE = (int(v) for v in np.cumsum(
    (0, MLA_Q_RANK, MLA_KV_RANK, SWA_WIDTH // 2, SWA_WIDTH // 2, IN_SPLITS[5], IN_SPLITS[4] // 2, IN_SPLITS[4] // 2)))


def _runs(order):
    out, start = [], 0
    for i in range(1, len(order) + 1):
        if i == len(order) or order[i] != order[i - 1] + 1:
            out.append((order[start], i - start))
            start = i
    return out


def _inverse(order):
    inv = [0] * len(order)
    for new, old in enumerate(order):
        inv[old] = new
    return inv


def _take_cols(a, order):
    return jnp.concatenate([a[..., st:st + w] for st, w in _runs(order)], axis=-1)


def _segment_matrix(n, seg):
    return (lax.broadcasted_iota(jnp.int32, (n, n), 0) // seg == lax.broadcasted_iota(jnp.int32, (n, n), 1) // seg).astype(BF16)


@jax.custom_vjp
def _cmm(t, b, bt):
    hi = t.astype(BF16)
    lo = (t - hi.astype(F32)).astype(BF16)
    return jnp.dot(hi, b, preferred_element_type=F32) + jnp.dot(lo, b, preferred_element_type=F32)


def _cmm_fwd(t, b, bt):
    return _cmm(t, b, bt), (b, bt)


def _cmm_bwd(res, dy):
    b, bt = res
    return _cmm(dy, bt, b), None, None


_cmm.defvjp(_cmm_fwd, _cmm_bwd)


def _segsum(t, b):
    return _cmm(t, b, b)


def _rowsum(t):
    n = t.shape[-1]
    return _cmm(t, jnp.ones((n, 128), BF16), jnp.ones((128, n), BF16))


def _by_head(vals, width):
    lane = lax.broadcasted_iota(jnp.int32, (vals[0].shape[0], len(vals) * width), 1)
    out = vals[-1]
    for hd in range(len(vals) - 2, -1, -1):
        out = jnp.where(lane < (hd + 1) * width, vals[hd], out)
    return out


def _rope2(a, b, cos, sin):
    return a * cos - b * sin, b * cos + a * sin


def _pre_math(x, gm, gqa, gkva, gq, gk, gsq, gsk, taps, win, wqb, wkvb, cos, sin):
    h = _rms(x, gm)
    proj = _mm(h, win)
    if taps is not None:
        proj = proj + taps[0]
    cqn = _rms(proj[:, _P_CQ:_P_CKV], gqa)
    qa_all = _mm(cqn, wqb)
    ckvn = _rms(proj[:, _P_CKV:_P_QA], gkva)
    kv_all = _mm(ckvn, wkvb)
    if taps is not None:
        qa_all = qa_all + taps[1]
        kv_all = kv_all + taps[2]
    nh, hw = MLA_HEADS, MLA_HEADS * _HALF
    seg_mla = _segment_matrix(hw, _HALF)
    tile = lambda g, n: jnp.concatenate([g] * n, axis=-1)
    c4, s4 = cos[:, :hw], sin[:, :hw]

    def mla_heads(nope, r1, r2, gain):
        rr = r1 * r1 + r2 * r2
        lane_head = lax.broadcasted_iota(jnp.int32, (hw, nh * MLA_NOPE), 0) // _HALF
        spread = (lane_head == lax.broadcasted_iota(jnp.int32, (hw, nh * MLA_NOPE), 1) // MLA_NOPE).astype(BF16)
        rope_on_nope = _cmm(rr, spread, spread.T)
        ss_nope = [_rowsum(jnp.square(nope[:, hd * MLA_NOPE:(hd + 1) * MLA_NOPE])) for hd in range(nh)]
        rinv = [lax.rsqrt((ss_nope[hd] + rope_on_nope[:, hd * MLA_NOPE:(hd + 1) * MLA_NOPE]) * (1.0 / MLA_QK) + EPS)
                for hd in range(nh)]
        rl = lax.rsqrt((_segsum(rr, seg_mla) + _by_head(ss_nope, _HALF)) * (1.0 / MLA_QK) + EPS)
        o1, o2 = _rope2(r1 * rl * tile(gain[:, MLA_NOPE:MLA_NOPE + _HALF], nh), r2 * rl * tile(gain[:, MLA_NOPE + _HALF:], nh), c4, s4)
        return [jnp.concatenate([nope[:, hd * MLA_NOPE:(hd + 1) * MLA_NOPE] * rinv[hd] * gain[:, :MLA_NOPE],
                                 o1[:, hd * _HALF:(hd + 1) * _HALF], o2[:, hd * _HALF:(hd + 1) * _HALF]], axis=-1)
                for hd in range(nh)]

    q_a = mla_heads(qa_all[:, :nh * MLA_NOPE], qa_all[:, nh * MLA_NOPE:nh * MLA_NOPE + hw], qa_all[:, nh * MLA_NOPE + hw:], gq)
    pe1, pe2 = proj[:, _P_PE:_P_PE + _HALF], proj[:, _P_PE + _HALF:_P_PE + 2 * _HALF]
    k_a = mla_heads(kv_all[:, :nh * MLA_NOPE], tile(pe1, nh), tile(pe2, nh), gk)
    v_a = [kv_all[:, nh * MLA_NOPE + hd * MLA_V:nh * MLA_NOPE + (hd + 1) * MLA_V] for hd in range(nh)]

    def swa_heads(a, b, gain, n):
        w = n * _HALF
        r = lax.rsqrt(_segsum(a * a + b * b, _segment_matrix(w, _HALF)) * (1.0 / SWA_HEAD_DIM) + EPS)
        o1, o2 = _rope2(a * r * tile(gain[:, :_HALF], n), b * r * tile(gain[:, _HALF:], n), cos[:, :w], sin[:, :w])
        return [jnp.concatenate([o1[:, hd * _HALF:(hd + 1) * _HALF], o2[:, hd * _HALF:(hd + 1) * _HALF]], axis=-1) for hd in range(n)]

    q_b = swa_heads(proj[:, _P_QA:_P_QB], proj[:, _P_QB:_P_VS], gsq, SWA_HEADS)
    k_b = swa_heads(proj[:, _P_KA:_P_KB], proj[:, _P_KB:_P_PE], gsk, SWA_KV_HEADS)
    v_b = [proj[:, _P_VS + j * SWA_HEAD_DIM:_P_VS + (j + 1) * SWA_HEAD_DIM] for j in range(SWA_KV_HEADS)]
    return (q_a, k_a, v_a, q_b, k_b, v_b), (h, cqn, ckvn)


_PRE_GAIN_WIDTHS = (D_MODEL, MLA_Q_RANK, MLA_KV_RANK, MLA_QK, MLA_QK, SWA_HEAD_DIM, SWA_HEAD_DIM)
_PRE_HEADS = ((MLA_HEADS, MLA_QK), (MLA_HEADS, MLA_QK), (MLA_HEADS, MLA_V),
              (SWA_HEADS, SWA_HEAD_DIM), (SWA_KV_HEADS, SWA_HEAD_DIM), (SWA_KV_HEADS, SWA_HEAD_DIM))


def _pre_fwd(x, gains, win, wqb, wkvb, cos, sin, name):
    s = x.shape[0]
    tm = _row_tile(s, 512)

    def body(x_ref, *refs):
        g_refs, (win_ref, wqb_ref, wkvb_ref, cos_ref, sin_ref), out_refs = refs[:7], refs[7:12], refs[12:]
        outs, _ = _pre_math(x_ref[...], *[g[...] for g in g_refs], None, win_ref[...], wqb_ref[...], wkvb_ref[...],
                            cos_ref[...], sin_ref[...])
        for idx, (ref, heads) in enumerate(zip(out_refs, outs)):
            for hd, val in enumerate(heads):
                ref[hd] = (val * MLA_QSCALE if idx == 0 else val).astype(BF16)

    heads_spec = lambda nh, w: pl.BlockSpec((nh, tm, w), lambda i: (0, i, 0))
    return _pallas_call(
        body, name=name, grid=(s // tm,),
        in_specs=[pl.BlockSpec((tm, D_MODEL), lambda i: (i, 0))] + [_full((1, w)) for w in _PRE_GAIN_WIDTHS]
        + [_resident(win.shape), _resident(wqb.shape), _resident(wkvb.shape),
           pl.BlockSpec((tm, SWA_HEADS * _HALF), lambda i: (i, 0)), pl.BlockSpec((tm, SWA_HEADS * _HALF), lambda i: (i, 0))],
        out_specs=[heads_spec(nh, w) for nh, w in _PRE_HEADS],
        out_shape=[jax.ShapeDtypeStruct((nh, s, w), BF16) for nh, w in _PRE_HEADS],
        compiler_params=_params(dimension_semantics=("arbitrary",)),
    )(x, *gains, win, wqb, wkvb, cos, sin)


def _pre_bwd(x, dx_res, cts, gains, win, wqb, wkvb, cos, sin, name):
    s = x.shape[0]
    tm = _row_tile(s, 256)
    tap_widths = (IN_COLS, MLA_HEADS * MLA_QK, MLA_HEADS * (MLA_NOPE + MLA_V))

    def body(x_ref, dxr_ref, *refs):
        ct_refs, g_refs = refs[:6], refs[6:13]
        win_ref, wqb_ref, wkvb_ref, cos_ref, sin_ref = refs[13:18]
        dx_ref, dg_refs, dw_refs, acc_refs = refs[18], refs[19:26], refs[26:29], refs[29:32]
        i = pl.program_id(0)
        win_v, wqb_v, wkvb_v, cos_v, sin_v = win_ref[...], wqb_ref[...], wkvb_ref[...], cos_ref[...], sin_ref[...]

        def f(xv, gm, gqa, gkva, gq, gk, gsq, gsk, t0, t1, t2):
            return _pre_math(xv, gm, gqa, gkva, gq, gk, gsq, gsk, (t0, t1, t2), win_v, wqb_v, wkvb_v, cos_v, sin_v)

        taps = [jnp.zeros((tm, w), F32) for w in tap_widths]
        _, vjp, acts = jax.vjp(f, x_ref[...], *[g[...] for g in g_refs], *taps, has_aux=True)
        ct = tuple([ref[hd] for hd in range(nh)] for ref, (nh, _) in zip(ct_refs, _PRE_HEADS))
        grads = vjp(ct)
        dx_ref[...] = grads[0] + dxr_ref[...]
        dws = [_dot_tn(a.astype(BF16), t.astype(BF16)) for a, t in zip(acts, grads[8:11])]

        @pl.when(i == 0)
        def _():
            for ref, val in zip(dg_refs, grads[1:8]):
                ref[...] = val
            for ref, val in zip(acc_refs, dws):
                ref[...] = val

        @pl.when(i > 0)
        def _():
            for ref, val in zip(dg_refs, grads[1:8]):
                ref[...] += val
            for ref, val in zip(acc_refs, dws):
                ref[...] += val

        @pl.when(i == s // tm - 1)
        def _():
            for ref, acc, order in zip(dw_refs, acc_refs, (_IN_ORDER, _QB_ORDER, _KVB_ORDER)):
                _store_col_shards(ref, _take_cols(acc[...], _inverse(order)), 0, N_SHARD, acc.shape[1] // N_SHARD)

    heads_spec = lambda nh, w: pl.BlockSpec((nh, tm, w), lambda i: (0, i, 0))
    row = pl.BlockSpec((tm, D_MODEL), lambda i: (i, 0))
    half = pl.BlockSpec((tm, SWA_HEADS * _HALF), lambda i: (i, 0))
    shard_shapes = [(N_SHARD, w.shape[0], w.shape[1] // N_SHARD) for w in (win, wqb, wkvb)]
    return _pallas_call(
        body, name=name, grid=(s // tm,),
        in_specs=[row, row] + [heads_spec(nh, w) for nh, w in _PRE_HEADS] + [_full((1, w)) for w in _PRE_GAIN_WIDTHS]
        + [_resident(win.shape), _resident(wqb.shape), _resident(wkvb.shape), half, half],
        out_specs=[row] + [_full((1, w)) for w in _PRE_GAIN_WIDTHS] + [_full(shp) for shp in shard_shapes],
        out_shape=[jax.ShapeDtypeStruct((s, D_MODEL), F32)] + [jax.ShapeDtypeStruct((1, w), F32) for w in _PRE_GAIN_WIDTHS]
        + [jax.ShapeDtypeStruct(shp, BF16) for shp in shard_shapes],
        scratch_shapes=[pltpu.VMEM(w.shape, F32) for w in (win, wqb, wkvb)],
        compiler_params=_params(dimension_semantics=("arbitrary",)),
    )(x, dx_res, *cts, *gains, win, wqb, wkvb, cos, sin)


def _post_math(oa, ob, ga, gb, wo):
    mixed = jnp.concatenate([_rms(jnp.concatenate(oa, axis=-1), ga), _rms(jnp.concatenate(ob, axis=-1), gb)], axis=-1)
    return _mm(mixed, wo), mixed


def _post_fwd(x, oa, ob, ga, gb, wo, name):
    s = x.shape[0]
    tm = _row_tile(s, 512)

    def body(x_ref, oa_ref, ob_ref, ga_ref, gb_ref, wo_ref, y_ref):
        y, _ = _post_math([oa_ref[hd] for hd in range(MLA_HEADS)], [ob_ref[hd] for hd in range(SWA_HEADS)],
                          ga_ref[...], gb_ref[...], wo_ref[...])
        y_ref[...] = x_ref[...] + y

    row = pl.BlockSpec((tm, D_MODEL), lambda i: (i, 0))
    return _pallas_call(
        body, name=name, grid=(s // tm,),
        in_specs=[row, pl.BlockSpec((MLA_HEADS, tm, MLA_V), lambda i: (0, i, 0)),
                  pl.BlockSpec((SWA_HEADS, tm, SWA_HEAD_DIM), lambda i: (0, i, 0)),
                  _full((1, MLA_WIDTH)), _full((1, SWA_WIDTH)), _resident(wo.shape)],
        out_specs=row, out_shape=jax.ShapeDtypeStruct((s, D_MODEL), F32),
        compiler_params=_params(dimension_semantics=("arbitrary",)),
    )(x, oa, ob, ga, gb, wo)


def _post_bwd(dy, oa, ob, ga, gb, wo, name):
    s = dy.shape[0]
    tm = _row_tile(s, 512)

    def body(dy_ref, oa_ref, ob_ref, ga_ref, gb_ref, wo_ref, doa_ref, dob_ref, dga_ref, dgb_ref, dwo_ref, acc_ref):
        i = pl.program_id(0)
        wo_v = wo_ref[...]
        dyv = dy_ref[...]

        def f(oa_l, ob_l, ga_v, gb_v):
            return _post_math(oa_l, ob_l, ga_v, gb_v, wo_v)

        _, vjp, mixed = jax.vjp(f, [oa_ref[hd] for hd in range(MLA_HEADS)], [ob_ref[hd] for hd in range(SWA_HEADS)],
                                ga_ref[...], gb_ref[...], has_aux=True)
        doa, dob, dga, dgb = vjp(dyv)
        for hd in range(MLA_HEADS):
            doa_ref[hd] = doa[hd]
        for hd in range(SWA_HEADS):
            dob_ref[hd] = dob[hd]
        dwo = _dot_tn(mixed.astype(BF16), dyv.astype(BF16))

        @pl.when(i == 0)
        def _():
            dga_ref[...] = dga
            dgb_ref[...] = dgb
            acc_ref[...] = dwo

        @pl.when(i > 0)
        def _():
            dga_ref[...] += dga
            dgb_ref[...] += dgb
            acc_ref[...] += dwo

        @pl.when(i == s // tm - 1)
        def _():
            dwo_ref[...] = acc_ref[...].astype(BF16)

    row = pl.BlockSpec((tm, D_MODEL), lambda i: (i, 0))
    oa_spec = pl.BlockSpec((MLA_HEADS, tm, MLA_V), lambda i: (0, i, 0))
    ob_spec = pl.BlockSpec((SWA_HEADS, tm, SWA_HEAD_DIM), lambda i: (0, i, 0))
    return _pallas_call(
        body, name=name, grid=(s // tm,),
        in_specs=[row, oa_spec, ob_spec, _full((1, MLA_WIDTH)), _full((1, SWA_WIDTH)), _resident(wo.shape)],
        out_specs=[oa_spec, ob_spec, _full((1, MLA_WIDTH)), _full((1, SWA_WIDTH)), _full(wo.shape)],
        out_shape=[jax.ShapeDtypeStruct((MLA_HEADS, s, MLA_V), F32), jax.ShapeDtypeStruct((SWA_HEADS, s, SWA_HEAD_DIM), F32),
                   jax.ShapeDtypeStruct((1, MLA_WIDTH), F32), jax.ShapeDtypeStruct((1, SWA_WIDTH), F32),
                   jax.ShapeDtypeStruct(wo.shape, BF16)],
        scratch_shapes=[pltpu.VMEM(wo.shape, F32)],
        compiler_params=_params(dimension_semantics=("arbitrary",)),
    )(dy, oa, ob, ga, gb, wo)


def _attn_tile(s):
    return 512 if s >= 2048 else 128


def _as_rows(cols):
    return cols.T[0:8, :]


def _causal_mask(t):
    return lax.broadcasted_iota(jnp.int32, (t, t), 1) <= lax.broadcasted_iota(jnp.int32, (t, t), 0)


def _pipelined_blocks(first, count, last_block, issue, consume, carry, prefetch_after):
    def clamped(j, slot):
        issue(jnp.minimum(j, last_block), slot)

    def pair(jj, c):
        a = first + 2 * jj
        clamped(a + 1, 1)
        c = consume(a, 0, c)
        clamped(a + 2, 0)
        return consume(a + 1, 1, c)

    clamped(first, 0)
    npairs = count // 2
    carry = lax.fori_loop(0, npairs, pair, carry)

    def odd(c):
        c = consume(first + 2 * npairs, 0, c)
        if prefetch_after:
            clamped(first + count, 0)
        return c

    return lax.cond(count - 2 * npairs == 1, odd, lambda c: c, carry)


def _run_stages_at(stages, steps):
    for stage, step in zip(stages, steps):
        here = pl.program_id(0) == step[0]
        for axis in range(1, len(step)):
            here = here & (pl.program_id(axis) == step[axis])
        pl.when(here)(stage)


def _mla_fwd(q, k, v, name, gather=None):
    nh, s, _ = q.shape
    t = _attn_tile(s)
    nq = s // t
    ng = len(gather) if gather else 0

    def body(q_ref, k_ref, v_ref, *rest):
        g_ins, (o_ref, lse_ref), g_outs = rest[:ng], rest[ng:ng + 2], rest[ng + 2:2 * ng + 2]
        (s0_ref, s1_ref), sems = rest[2 * ng + 2:2 * ng + 4], rest[2 * ng + 4:]
        if ng:
            _run_stages_at(_gather_stages(g_ins, g_outs, *sems), [(0, 0), (nh // 2, 0), (nh - 1, 0), (nh - 1, nq - 1)])
        qi = pl.program_id(1)
        qv = q_ref[...]
        s_refs = (s0_ref, s1_ref)

        def rows(j):
            return pl.ds(pl.multiple_of(j * t, t), t)

        def issue(j, slot):
            s_refs[slot][...] = _dot_nt(qv, k_ref[rows(j), :])

        def consume(j, slot, carry, masked=False):
            m, l, acc = carry
            sc = s_refs[slot][...]
            if masked:
                sc = jnp.where(_causal_mask(t), sc, NEG)
            m_new = jnp.maximum(m, jnp.max(sc, axis=-1, keepdims=True))
            alpha = jnp.exp2(m - m_new)
            p = jnp.exp2(sc - m_new)
            l = alpha * l + jnp.sum(p, axis=-1, keepdims=True)
            acc = alpha * acc + jnp.dot(p.astype(BF16), v_ref[rows(j), :], preferred_element_type=F32)
            return m_new, l, acc

        init = (jnp.full((t, 1), NEG, F32), jnp.zeros((t, 1), F32), jnp.zeros((t, MLA_V), F32))
        carry = _pipelined_blocks(0, qi, nq - 1, issue, consume, init, True)
        m, l, acc = consume(qi, 0, carry, masked=True)
        o_ref[...] = acc / l
        lse_ref[...] = _as_rows(jnp.broadcast_to(m + jnp.log2(l), (t, 128)))

    outs = _pallas_call(
        body, name=name, grid=(nh, nq),
        in_specs=[pl.BlockSpec((None, t, MLA_QK), lambda h, i: (h, i, 0)), pl.BlockSpec((None, s, MLA_QK), lambda h, i: (h, 0, 0)),
                  pl.BlockSpec((None, s, MLA_V), lambda h, i: (h, 0, 0))] + [_HBM] * ng,
        out_specs=[pl.BlockSpec((None, t, MLA_V), lambda h, i: (h, i, 0)), pl.BlockSpec((None, None, 8, t), lambda h, i: (h, i, 0, 0))]
        + [_HBM] * ng,
        out_shape=[jax.ShapeDtypeStruct((nh, s, MLA_V), F32), jax.ShapeDtypeStruct((nh, nq, 8, t), F32)]
        + (_gather_out_shapes(gather) if ng else []),
        scratch_shapes=[pltpu.VMEM((t, t), F32)] * 2 + (_exchange_scratch(8, ng) if ng else []),
        compiler_params=_params(dimension_semantics=("arbitrary", "arbitrary")),
    )(q, k, v, *(gather or []))
    return outs[0], outs[1], outs[2:]


def _mla_delta(o, do, name):
    nh, s, _ = o.shape
    t = _attn_tile(s)
    per_step = min(4, s // t)

    def body(o_ref, do_ref, delta_ref):
        rows = _as_rows(_rowsum(do_ref[...] * o_ref[...]))
        for j in range(per_step):
            delta_ref[j] = rows[:, j * t:(j + 1) * t]

    tile = pl.BlockSpec((None, per_step * t, MLA_V), lambda h, i: (h, i, 0))
    return _pallas_call(body, name=name, grid=(nh, s // (per_step * t)), in_specs=[tile, tile],
                        out_specs=pl.BlockSpec((None, per_step, 8, t), lambda h, i: (h, i, 0, 0)),
                        out_shape=jax.ShapeDtypeStruct((nh, s // t, 8, t), F32),
                        compiler_params=_params(dimension_semantics=("arbitrary", "arbitrary")))(o, do)


def _mla_bwd(q, k, v, do, lse_row, delta_row, name, scatter=None):
    nh, s, _ = q.shape
    t = _attn_tile(s)
    nq = s // t
    ng = len(scatter) if scatter else 0

    def body(q_ref, k_ref, v_ref, do_ref, lse_ref, delta_ref, *rest):
        c_ins, (dq_ref, dk_ref, dv_ref), c_outs = rest[:ng], rest[ng:ng + 3], rest[ng + 3:2 * ng + 3]
        (s0_ref, s1_ref, dp0_ref, dp1_ref), sems = rest[2 * ng + 3:2 * ng + 7], rest[2 * ng + 7:]
        if ng:
            _run_stages_at(_scatter_stages(c_ins, c_outs, *sems), [(0, 0), (nh - 1, nq - 1)])
        kj = pl.program_id(1)
        kv_, vv = k_ref[...], v_ref[...]
        s_refs, dp_refs = (s0_ref, s1_ref), (dp0_ref, dp1_ref)

        @pl.when(kj == 0)
        def _():
            dq_ref[...] = jnp.zeros_like(dq_ref)

        def rows(i):
            return pl.ds(pl.multiple_of(i * t, t), t)

        def issue(i, slot):
            s_refs[slot][...] = _dot_nt(kv_, q_ref[rows(i), :])
            dp_refs[slot][...] = _dot_nt(vv, do_ref[rows(i), :].astype(BF16))

        def consume(i, slot, carry, masked=False):
            dk, dv = carry
            p = jnp.exp2(s_refs[slot][...] - lse_ref[i][0:1, :])
            if masked:
                p = jnp.where(lax.broadcasted_iota(jnp.int32, (t, t), 0) <= lax.broadcasted_iota(jnp.int32, (t, t), 1), p, 0.0)
            dv = dv + jnp.dot(p.astype(BF16), do_ref[rows(i), :].astype(BF16), preferred_element_type=F32)
            ds = (p * (dp_refs[slot][...] - delta_ref[i][0:1, :])).astype(BF16)
            dk = dk + jnp.dot(ds, q_ref[rows(i), :], preferred_element_type=F32)
            dq_ref[rows(i), :] += _dot_tn(ds, kv_) * MLA_SCALE
            return dk, dv

        issue(kj, 0)
        carry = consume(kj, 0, (jnp.zeros((t, MLA_QK), F32), jnp.zeros((t, MLA_V), F32)), masked=True)
        dk, dv = _pipelined_blocks(kj + 1, nq - 1 - kj, nq - 1, issue, consume, carry, False)
        dk_ref[...] = dk * LN2
        dv_ref[...] = dv

    tile = lambda w: pl.BlockSpec((None, t, w), lambda h, j: (h, j, 0))
    whole = lambda w: pl.BlockSpec((None, s, w), lambda h, j: (h, 0, 0))
    rows_spec = pl.BlockSpec((None, nq, 8, t), lambda h, j: (h, 0, 0, 0))
    outs = _pallas_call(
        body, name=name, grid=(nh, nq),
        in_specs=[whole(MLA_QK), tile(MLA_QK), tile(MLA_V), whole(MLA_V), rows_spec, rows_spec] + [_HBM] * ng,
        out_specs=[whole(MLA_QK), tile(MLA_QK), tile(MLA_V)] + [_HBM] * ng,
        out_shape=[jax.ShapeDtypeStruct((nh, s, MLA_QK), F32), jax.ShapeDtypeStruct((nh, s, MLA_QK), F32),
                   jax.ShapeDtypeStruct((nh, s, MLA_V), F32)] + (_scatter_out_shapes(scatter) if ng else []),
        scratch_shapes=[pltpu.VMEM((t, t), F32)] * 4 + (_exchange_scratch(3, ng) if ng else []),
        compiler_params=_params(dimension_semantics=("arbitrary", "arbitrary")),
    )(q, k, v, do, lse_row, delta_row, *(scatter or []))
    return outs[0], outs[1], outs[2], outs[3:]


def _swa_tile(s):
    return min(s, 8 * BLOCK)


def _swa_specs(tq):
    nb = tq // BLOCK
    grp = lambda w: pl.BlockSpec((SWA_GROUP, tq, w), lambda j, i: (j, i, 0))
    main = pl.BlockSpec((None, tq, SWA_HEAD_DIM), lambda j, i: (j, i, 0))
    tail = pl.BlockSpec((None, BLOCK, SWA_HEAD_DIM), lambda j, i: (j, nb * (i + 1), 0))
    sink = pl.BlockSpec((None, SWA_GROUP, 128), lambda j, i: (j, 0, 0))
    return grp, main, tail, sink


def _swa_band_mask(first):
    shape = (SWA_GROUP * BLOCK, 2 * BLOCK)
    q_rel = (lax.broadcasted_iota(jnp.int32, shape, 0) & (BLOCK - 1)) + BLOCK
    k_rel = lax.broadcasted_iota(jnp.int32, shape, 1)
    dist = q_rel - k_rel
    return (dist >= 0) & (dist < BLOCK) & ((k_rel >= BLOCK) | jnp.logical_not(first))


def _swa_sink_column(sink_ref):
    sk = sink_ref[...]
    return jnp.concatenate([jnp.broadcast_to(sk[g:g + 1, 0:1], (BLOCK, 1)) for g in range(SWA_GROUP)], axis=0)


def _swa_fwd(q, kpad, vpad, sinks, name):
    _, s, _ = q.shape
    tq = _swa_tile(s)
    grp, main, tail, sink = _swa_specs(tq)
    d = SWA_HEAD_DIM

    def body(q_ref, km_ref, kt_ref, vm_ref, vt_ref, sink_ref, o_ref, lse_ref):
        i = pl.program_id(1)
        kall = jnp.concatenate([km_ref[...], kt_ref[...]], axis=0)
        vall = jnp.concatenate([vm_ref[...], vt_ref[...]], axis=0)
        sink_col = _swa_sink_column(sink_ref)
        for b in range(tq // BLOCK):
            lo = b * BLOCK
            valid = _swa_band_mask(i == 0 if b == 0 else False)
            q4 = q_ref[:, lo:lo + BLOCK, :].reshape(SWA_GROUP * BLOCK, d)
            sc = jnp.where(valid, _dot_nt(q4, kall[lo:lo + 2 * BLOCK]) * SWA_SCALE, NEG)
            m = jnp.maximum(jnp.max(sc, axis=-1, keepdims=True), sink_col)
            e = jnp.exp(sc - m)
            den = jnp.sum(e, axis=-1, keepdims=True) + jnp.exp(sink_col - m)
            out = jnp.dot((e * (1.0 / den)).astype(BF16), vall[lo:lo + 2 * BLOCK], preferred_element_type=F32)
            o_ref[:, lo:lo + BLOCK, :] = out.reshape(SWA_GROUP, BLOCK, d)
            lse_ref[:, lo:lo + BLOCK, :] = (m + jnp.log(den)).reshape(SWA_GROUP, BLOCK, 1)

    return _pallas_call(
        body, name=name, grid=(SWA_KV_HEADS, s // tq),
        in_specs=[grp(d), main, tail, main, tail, sink], out_specs=[grp(d), grp(1)],
        out_shape=[jax.ShapeDtypeStruct((SWA_HEADS, s, d), F32), jax.ShapeDtypeStruct((SWA_HEADS, s, 1), F32)],
        compiler_params=_params(dimension_semantics=("arbitrary", "arbitrary")),
    )(q, kpad, kpad, vpad, vpad, sinks)


def _swa_bwd(q, kpad, vpad, sinks, o, lse, do, name):
    _, s, _ = q.shape
    tq = _swa_tile(s)
    grp, main, tail, sink = _swa_specs(tq)
    d = SWA_HEAD_DIM

    def body(q_ref, km_ref, kt_ref, vm_ref, vt_ref, sink_ref, o_ref, lse_ref, do_ref, dq_ref, dk_ref, dv_ref, dsink_ref):
        i = pl.program_id(1)
        kall = jnp.concatenate([km_ref[...], kt_ref[...]], axis=0)
        vall = jnp.concatenate([vm_ref[...], vt_ref[...]], axis=0)
        sink_col = _swa_sink_column(sink_ref)

        @pl.when(i == 0)
        def _():
            dk_ref[...] = jnp.zeros_like(dk_ref)
            dv_ref[...] = jnp.zeros_like(dv_ref)
            dsink_ref[...] = jnp.zeros_like(dsink_ref)

        dsink = jnp.zeros((SWA_GROUP * BLOCK, 1), F32)
        for b in range(tq // BLOCK):
            lo = b * BLOCK
            valid = _swa_band_mask(i == 0 if b == 0 else False)
            rows4 = SWA_GROUP * BLOCK
            q4 = q_ref[:, lo:lo + BLOCK, :].reshape(rows4, d)
            do4 = do_ref[:, lo:lo + BLOCK, :].reshape(rows4, d)
            lse4 = lse_ref[:, lo:lo + BLOCK, :].reshape(rows4, 1)
            delta = jnp.sum(do4 * o_ref[:, lo:lo + BLOCK, :].reshape(rows4, d), axis=-1, keepdims=True)
            kb, vb = kall[lo:lo + 2 * BLOCK], vall[lo:lo + 2 * BLOCK]
            do4b = do4.astype(BF16)
            p = jnp.where(valid, jnp.exp(_dot_nt(q4, kb) * SWA_SCALE - lse4), 0.0)
            ds = (p * (_dot_nt(do4b, vb) - delta) * SWA_SCALE).astype(BF16)
            dq_ref[:, lo:lo + BLOCK, :] = jnp.dot(ds, kb, preferred_element_type=F32).reshape(SWA_GROUP, BLOCK, d)
            band = pl.ds(pl.multiple_of(i * tq, BLOCK) + lo, 2 * BLOCK)
            dk_ref[band, :] += _dot_tn(ds, q4)
            dv_ref[band, :] += _dot_tn(p.astype(BF16), do4b)
            dsink = dsink - jnp.exp(sink_col - lse4) * delta
        per_head = [jnp.broadcast_to(jnp.sum(dsink[g * BLOCK:(g + 1) * BLOCK], axis=0, keepdims=True), (1, 128))
                    for g in range(SWA_GROUP)]
        dsink_ref[...] += jnp.concatenate(per_head + [jnp.zeros((8 - SWA_GROUP, 128), F32)], axis=0)

    acc = pl.BlockSpec((None, s + BLOCK, d), lambda j, i: (j, 0, 0))
    return _pallas_call(
        body, name=name, grid=(SWA_KV_HEADS, s // tq),
        in_specs=[grp(d), main, tail, main, tail, sink, grp(d), grp(1), grp(d)],
        out_specs=[grp(d), acc, acc, pl.BlockSpec((None, 8, 128), lambda j, i: (j, 0, 0))],
        out_shape=[jax.ShapeDtypeStruct((SWA_HEADS, s, d), F32),
                   jax.ShapeDtypeStruct((SWA_KV_HEADS, s + BLOCK, d), F32),
                   jax.ShapeDtypeStruct((SWA_KV_HEADS, s + BLOCK, d), F32),
                   jax.ShapeDtypeStruct((SWA_KV_HEADS, 8, 128), F32)],
        compiler_params=_params(dimension_semantics=("arbitrary", "arbitrary")),
    )(q, kpad, kpad, vpad, vpad, sinks, o, lse, do)


def _loss_head(y, target, name):
    s = y.shape[0]
    tm = _row_tile(s, 512)

    def body(y_ref, t_ref, dy_ref, loss_ref):
        i = pl.program_id(0)
        err = y_ref[...] - t_ref[...]
        dy_ref[...] = err * (1.0 / D_MODEL)
        part = jnp.broadcast_to(0.5 * jnp.sum(jnp.mean(err * err, axis=-1, keepdims=True), axis=0, keepdims=True), (1, 128))

        @pl.when(i == 0)
        def _():
            loss_ref[...] = part

        @pl.when(i > 0)
        def _():
            loss_ref[...] += part

    row = pl.BlockSpec((tm, D_MODEL), lambda i: (i, 0))
    return _pallas_call(
        body, name=name, grid=(s // tm,), in_specs=[row, row], out_specs=[row, _full((1, 128))],
        out_shape=[jax.ShapeDtypeStruct((s, D_MODEL), F32), jax.ShapeDtypeStruct((1, 128), F32)],
        compiler_params=_params(dimension_semantics=("arbitrary",)),
    )(y, target)


def _adamw(w, g, m, v, name):
    rows, cols = w.shape
    tr = rows
    for cand in (512, 256, 128, 64, 32, 16, 8):
        if rows % cand == 0 and rows > cand:
            tr = cand
            break

    def body(w_ref, g_ref, m_ref, v_ref, d_ref, nm_ref, nv_ref):
        gv = g_ref[...]
        nm = ADAM_B1 * m_ref[...] + (1.0 - ADAM_B1) * gv
        nv = ADAM_B2 * v_ref[...] + (1.0 - ADAM_B2) * (gv * gv)
        m_hat = nm / (1.0 - ADAM_B1 ** ADAM_STEP)
        v_hat = nv / (1.0 - ADAM_B2 ** ADAM_STEP)
        d_ref[...] = -ADAM_LR * (m_hat / (jnp.sqrt(v_hat) + ADAM_EPS) + ADAM_WD * w_ref[...])
        nm_ref[...] = nm
        nv_ref[...] = nv

    blk = pl.BlockSpec((tr, cols), lambda i: (i, 0))
    return _pallas_call(
        body, name=name, grid=(rows // tr,), in_specs=[blk] * 4, out_specs=[blk] * 3,
        out_shape=[jax.ShapeDtypeStruct((rows, cols), F32)] * 3,
        compiler_params=_params(dimension_semantics=("arbitrary",)),
    )(w, g, m, v)


def _position():
    return lax.axis_index("x"), lax.axis_index("y"), lax.axis_index("c")


def _remote(src, dst, send_sems, recv_sems, k, to):
    return pltpu.make_async_remote_copy(src_ref=src, dst_ref=dst, send_sem=send_sems.at[k], recv_sem=recv_sems.at[k],
                                        device_id=to, device_id_type=MESH)


_HBM = pl.BlockSpec(memory_space=pltpu.HBM)


def _gather_stages(ins, outs, send_sems, recv_sems):
    na = len(ins)
    x, y, c = _position()
    me, sibling = (x, y, c), (x, y, 1 - c)
    xn, yn, dg = (1 - x, y), (x, 1 - y), (1 - x, 1 - y)

    def slot(a, chip, pc, half=None):
        ref = outs[a].at[4 * chip[0] + 2 * chip[1] + pc]
        if half is None:
            return ref
        rows = ref.shape[0] // 2
        return ref.at[pl.ds(half * rows, rows)]

    def cp(a, k, chip, pc, half, to, src=None):
        dst = slot(a, chip, pc, half)
        return _remote(dst if src is None else src, dst, send_sems, recv_sems, 8 * a + k, to)

    first_hop = [(0, xn), (1, yn)]
    second_hop = [(0, xn, 2, 0, yn), (1, yn, 3, 1, xn)]

    def sends():
        out = []
        for a in range(na):
            out += [cp(a, k, (x, y), c, None, (*to, c), src=ins[a].at[c]) for k, to in first_hop]
            out += [cp(a, fwd_k, frm, c, half, (*to, c)) for _, frm, fwd_k, half, to in second_hop]
            out += [cp(a, 4 + k, frm, c, None, sibling) for k, frm in first_hop]
            out += [cp(a, 6 + half, dg, c, half, sibling) for half in (0, 1)]
        return out

    def stage0():
        for a in range(na):
            for k, to in first_hop:
                cp(a, k, (x, y), c, None, (*to, c), src=ins[a].at[c]).start()

    def stage1():
        for k, frm, fwd_k, half, to in second_hop:
            for a in range(na):
                cp(a, k, frm, c, None, me).wait_recv()
                cp(a, fwd_k, frm, c, half, (*to, c)).start()
                cp(a, 4 + k, frm, c, None, sibling).start()

    def stage2():
        for half in (0, 1):
            for a in range(na):
                cp(a, 2 + half, dg, c, half, me).wait_recv()
                cp(a, 6 + half, dg, c, half, sibling).start()

    def stage3():
        for a in range(na):
            for k, chip, half in ((4, xn, None), (5, yn, None), (6, dg, 0), (7, dg, 1)):
                cp(a, k, chip, 1 - c, half, me).wait_recv()
        for sent in sends():
            sent.wait_send()

    return [stage0, stage1, stage2, stage3]


def _gather_out_shapes(mine):
    return [jax.ShapeDtypeStruct((N_DEV,) + m.shape[1:], m.dtype) for m in mine]


def _exchange_scratch(per_array, na):
    return [pltpu.SemaphoreType.DMA((per_array * na,)), pltpu.SemaphoreType.DMA((per_array * na,))]


def _all_gather_halves(mine, name):
    na = len(mine)

    def body(*refs):
        for stage in _gather_stages(refs[:na], refs[na:2 * na], *refs[2 * na:]):
            stage()

    return _pallas_call(body, name=name, in_specs=[_HBM] * na, out_specs=[_HBM] * na, out_shape=_gather_out_shapes(mine),
                        scratch_shapes=_exchange_scratch(8, na))(*mine)


def _sibling_exchange(parts, name, other_half):
    na = len(parts)

    def body(*refs):
        ins, outs, (send_sems, recv_sems) = refs[:na], refs[na:2 * na], refs[2 * na:]
        x, y, c = _position()
        copies = [_remote(ins[a].at[:, 1 - c] if other_half else ins[a], outs[a], send_sems, recv_sems, a, (x, y, 1 - c))
                  for a in range(na)]
        for cp in copies:
            cp.start()
        for cp in copies:
            cp.wait()

    return _pallas_call(
        body, name=name, in_specs=[_HBM] * na, out_specs=[_HBM] * na,
        out_shape=[jax.ShapeDtypeStruct(p.shape[:1] + p.shape[2:] if other_half else p.shape, p.dtype) for p in parts],
        scratch_shapes=_exchange_scratch(1, na),
    )(*parts)


def _scatter_stages(ins, outs, send_sems, recv_sems):
    na = len(ins)
    x, y, c = _position()
    chips = [(1 - x, y), (x, 1 - y), (1 - x, 1 - y)]

    def copies():
        return [_remote(ins[a].at[2 * px + py], outs[a].at[j], send_sems, recv_sems, 3 * a + j, (px, py, c))
                for a in range(na) for j, (px, py) in enumerate(chips)]

    def start():
        for cp in copies():
            cp.start()

    def wait():
        for cp in copies():
            cp.wait()

    return [start, wait]


def _scatter_out_shapes(parts):
    return [jax.ShapeDtypeStruct((3,) + p.shape[1:], p.dtype) for p in parts]


def _scatter_to_chips(parts, name):
    na = len(parts)

    def body(*refs):
        for stage in _scatter_stages(refs[:na], refs[na:2 * na], *refs[2 * na:]):
            stage()

    return _pallas_call(body, name=name, in_specs=[_HBM] * na, out_specs=[_HBM] * na, out_shape=_scatter_out_shapes(parts),
                        scratch_shapes=_exchange_scratch(3, na))(*parts)


def _assemble(g4, mine, name, side_by_side, order=None):
    _, nl, r, w = g4.shape
    tr = _divisor_tile(r, 640)

    def body(in_ref, mine_ref, out_ref):
        chip = 2 * lax.axis_index("x") + lax.axis_index("y")
        blocks = [jnp.where(chip == sh, mine_ref[...], in_ref[sh]) for sh in range(N_SHARD)]
        if side_by_side:
            full = jnp.concatenate(blocks, axis=-1)
            out_ref[...] = full if order is None else _take_cols(full, order)
        else:
            for sh in range(N_SHARD):
                out_ref[sh] = blocks[sh]

    if side_by_side:
        out_spec = pl.BlockSpec((None, tr, N_SHARD * w), lambda l, i: (l, i, 0))
        out_shape = jax.ShapeDtypeStruct((nl, r, N_SHARD * w), g4.dtype)
    else:
        out_spec = pl.BlockSpec((None, N_SHARD, tr, w), lambda l, i: (l, 0, i, 0))
        out_shape = jax.ShapeDtypeStruct((nl, N_SHARD, r, w), g4.dtype)
    return _pallas_call(
        body, name=name, grid=(nl, r // tr),
        in_specs=[pl.BlockSpec((N_SHARD, None, tr, w), lambda l, i: (0, l, i, 0)),
                  pl.BlockSpec((None, tr, w), lambda l, i: (l, i, 0))],
        out_specs=out_spec, out_shape=out_shape,
        compiler_params=_params(dimension_semantics=("arbitrary", "arbitrary")),
    )(g4, mine)


def _all_reduce_small(vec, name):
    r, l = vec.shape

    def body(v_ref, out_ref, gath_ref, send_sems, recv_sems):
        x, y, c = _position()
        me = 4 * x + 2 * y + c
        gath_ref[me] = v_ref[...]
        copies = []
        for k in range(1, N_DEV):
            to = (x ^ (k >> 2), y ^ ((k >> 1) & 1), c ^ (k & 1))
            copies.append(_remote(gath_ref.at[me], gath_ref.at[me], send_sems, recv_sems, k - 1, to))
        for cp in copies:
            cp.start()
        for k in range(1, N_DEV):
            frm = 4 * (x ^ (k >> 2)) + 2 * (y ^ ((k >> 1) & 1)) + (c ^ (k & 1))
            _remote(gath_ref.at[frm], gath_ref.at[frm], send_sems, recv_sems, k - 1, (x, y, c)).wait_recv()
        for cp in copies:
            cp.wait_send()
        total = gath_ref[0]
        for d in range(1, N_DEV):
            total = total + gath_ref[d]
        out_ref[...] = total

    vm = pl.BlockSpec(memory_space=pltpu.VMEM)
    return _pallas_call(
        body, name=name, in_specs=[vm], out_specs=vm, out_shape=jax.ShapeDtypeStruct((r, l), F32),
        scratch_shapes=[pltpu.VMEM((N_DEV, r, l), F32), pltpu.SemaphoreType.DMA((N_DEV - 1,)),
                        pltpu.SemaphoreType.DMA((N_DEV - 1,))],
    )(vec)


def _sum_blocks(blocks, out_dtype, name):
    m, w = blocks[0].shape
    tr = _divisor_tile(m, 1024)

    def body(*refs):
        total = refs[0][...].astype(F32)
        for ref in refs[1:-1]:
            total = total + ref[...].astype(F32)
        refs[-1][...] = total.astype(out_dtype)

    blk = pl.BlockSpec((tr, w), lambda i: (i, 0))
    return _pallas_call(
        body, name=name, grid=(m // tr,), in_specs=[blk] * len(blocks), out_specs=blk,
        out_shape=jax.ShapeDtypeStruct((m, w), out_dtype),
        compiler_params=_params(dimension_semantics=("arbitrary",)),
    )(*blocks)


FIRST_GROUPS = (('ffn1_w_gate', 'ffn1_w_up', 'ffn1_w_down'),)
REST_GROUPS = (('ffn2_w_gate', 'ffn2_w_up', 'ffn2_w_down', 'w_o'), ('w_in',), ('mla_w_q_b',), ('mla_w_kv_b',))
N_FIRST = len(FIRST_GROUPS)


def _shard_rows(name):
    shape, axis = BIG[name]
    return shape[0] // N_SHARD if axis == 0 else shape[0]


def _group_row_offsets(group):
    return [int(v) for v in np.cumsum([0] + [_shard_rows(n) for n in group])]


def _rope_tables(s):
    pos = jnp.arange(s, dtype=F32)
    inv = 1.0 / (ROPE_THETA ** (jnp.arange(0, MLA_ROPE, 2, dtype=F32) / MLA_ROPE))
    ang = pos[:, None] * inv[None, :]
    return jnp.tile(jnp.cos(ang), (1, SWA_HEADS)), jnp.tile(jnp.sin(ang), (1, SWA_HEADS))


_MIXER_GAINS = ('mix_norm', 'mla_q_a_norm', 'mla_kv_a_norm', 'mla_q_norm', 'mla_k_norm', 'swa_q_norm', 'swa_k_norm')


def _local_step(x, target, small, ex):
    s = x.shape[0]
    cos, sin = _rope_tables(s)
    row = lambda name, l: small[name][l][None, :]
    saved, bigs = [], []
    for l in range(DEPTH):
        big = ex.first_weights(l)
        sv = {'x0': x}
        x, sv['g1'], sv['u1'], got = _ffn_fwd(x, row('ffn1_norm', l), big['ffn1_w_gate'], big['ffn1_w_up'], big['ffn1_w_down'],
                                             f"ffn1_fwd_{l}", ex.gather_behind_ffn1(l))
        big.update(ex.rest_weights(l, got))
        bigs.append(big)
        sv['x1'] = x
        gains = [row(n, l) for n in _MIXER_GAINS]
        mixer_w = (big['w_in'], big['mla_w_q_b'], big['mla_w_kv_b'])
        q_a, k_a, v_a, q_b, k_b, v_b = _pre_fwd(x, gains, *mixer_w, cos, sin, f"pre_fwd_{l}")
        o_a, lse, got = _mla_fwd(q_a, k_a, v_a, f"mla_fwd_{l}", ex.gather_behind_mla(l))
        ex.gathered_behind_mla(l, got)
        kpad = jnp.pad(k_b, ((0, 0), (BLOCK, 0), (0, 0)))
        vpad = jnp.pad(v_b, ((0, 0), (BLOCK, 0), (0, 0)))
        sinks = jnp.broadcast_to(small['swa_sinks'][l].reshape(SWA_KV_HEADS, SWA_GROUP, 1), (SWA_KV_HEADS, SWA_GROUP, 128))
        o_b, lse_b = _swa_fwd(q_b, kpad, vpad, sinks, f"swa_fwd_{l}")
        sv.update(gains=gains, mixer_w=mixer_w, q_a=q_a, k_a=k_a, v_a=v_a, q_b=q_b, kpad=kpad, vpad=vpad, sinks=sinks,
                  o_a=o_a, lse=lse, o_b=o_b, lse_b=lse_b)
        x = _post_fwd(x, o_a, o_b, row('mla_out_norm', l), row('swa_out_norm', l), big['w_o'], f"post_fwd_{l}")
        sv['x2'] = x
        x, sv['g2'], sv['u2'], _ = _ffn_fwd(x, row('ffn2_norm', l), big['ffn2_w_gate'], big['ffn2_w_up'], big['ffn2_w_down'],
                                           f"ffn2_fwd_{l}")
        saved.append(sv)

    dx, loss = _loss_head(x, target, "loss_head")

    gs = {n: [None] * DEPTH for n in SMALL_NAMES}
    t = _attn_tile(s)
    for l in reversed(range(DEPTH)):
        sv, big = saved[l], bigs[l]

        def ffn_back(tag, xin, dy, gate, up, scatter=None):
            (dxi, dgain, nb, act, dgate, dup), got = _ffn_bwd(xin, dy, gate, up, row(tag + '_norm', l), big[tag + '_w_gate'],
                                                             big[tag + '_w_up'], big[tag + '_w_down'], f"{tag}_bwd_{l}", scatter)
            gs[tag + '_norm'][l] = dgain[0]
            shards = lambda full: full.reshape(N_SHARD, D_FF // N_SHARD, D_MODEL)
            grads = {tag + '_w_gate': shards(_matmul_tn(dgate, nb, 1.0, f"{tag}_dwg_{l}")),
                     tag + '_w_up': shards(_matmul_tn(dup, nb, 1.0, f"{tag}_dwu_{l}")),
                     tag + '_w_down': shards(_matmul_tn(act, dy, 0.5, f"{tag}_dwd_{l}"))}
            return dxi, grads, got

        dx, rest_grads, _ = ffn_back('ffn2', sv['x2'], dx, sv['g2'], sv['u2'])
        do_a, do_b, dga, dgb, dwo = _post_bwd(dx, sv['o_a'], sv['o_b'], row('mla_out_norm', l), row('swa_out_norm', l),
                                              big['w_o'], f"post_bwd_{l}")
        gs['mla_out_norm'][l], gs['swa_out_norm'][l] = dga[0], dgb[0]
        rest_grads['w_o'] = dwo.reshape(N_SHARD, MIX_WIDTH // N_SHARD, D_MODEL)
        delta = _mla_delta(sv['o_a'], do_a, f"mla_delta_{l}")
        dq_a, dk_a, dv_a, got = _mla_bwd(sv['q_a'], sv['k_a'], sv['v_a'], do_a, sv['lse'], delta, f"mla_bwd_{l}",
                                         ex.scatter_behind_mla(l))
        ex.scattered_behind_mla(l, got)
        dq_b, dkpad, dvpad, dsink = _swa_bwd(sv['q_b'], sv['kpad'], sv['vpad'], sv['sinks'], sv['o_b'], sv['lse_b'], do_b,
                                             f"swa_bwd_{l}")
        gs['swa_sinks'][l] = dsink[:, :SWA_GROUP, 0].reshape(SWA_HEADS)
        cts = [dq_a, dk_a, dv_a, dq_b, dkpad[:, BLOCK:], dvpad[:, BLOCK:]]
        outs = _pre_bwd(sv['x1'], dx, cts, sv['gains'], *sv['mixer_w'], cos, sin, f"pre_bwd_{l}")
        dx = outs[0]
        for n, val in zip(_MIXER_GAINS, outs[1:8]):
            gs[n][l] = val[0]
        rest_grads['w_in'], rest_grads['mla_w_q_b'], rest_grads['mla_w_kv_b'] = outs[8:11]
        ex.grads_ready(l, 'rest', rest_grads)
        dx, first_grads, got = ffn_back('ffn1', sv['x0'], dx, sv['g1'], sv['u1'], ex.scatter_behind_ffn1(l))
        ex.scattered_behind_ffn1(l, got)
        ex.grads_ready(l, 'first', first_grads)
    return loss, dx, gs


class _Exchange:
    def __init__(self, weights, c, chip):
        halves_of = lambda a: a.reshape(a.shape[:-2] + (2, a.shape[-2] // 2, a.shape[-1]))
        self.halves_of, self.c, self.chip = halves_of, c, chip
        self.mine = [[halves_of(jnp.concatenate([weights[n][l].astype(BF16) for n in group], axis=0))
                      for group in FIRST_GROUPS + REST_GROUPS] for l in range(DEPTH)]
        self.ahead, self.begun, self.received = {}, {}, {}

    def _assembled(self, l, which, gathered):
        groups, base = (FIRST_GROUPS, 0) if which == 'first' else (REST_GROUPS, N_FIRST)
        big = {}
        for gi, group in enumerate(groups):
            offs = _group_row_offsets(group)
            _, rh, w = self.mine[l][base + gi].shape
            col_sharded = BIG[group[0]][1] == 1
            full = _assemble(gathered[gi].reshape(N_SHARD, 1, 2 * rh, w), self.mine[l][base + gi].reshape(1, 2 * rh, w),
                             f"assemble_{which}{gi}_{l}", col_sharded, MIXER_ORDERS.get(group[0]))
            for i, n in enumerate(group):
                rows = offs[i + 1] - offs[i]
                if len(group) > 1 and offs[i] % rows == 0:
                    big[n] = ((full, (None, rows, N_SHARD * w), (0, offs[i] // rows, 0)) if col_sharded else
                              (full, (None, N_SHARD, rows, w), (0, 0, offs[i] // rows, 0)))
                elif col_sharded:
                    big[n] = full[0, offs[i]:offs[i + 1]]
                else:
                    big[n] = full[0, :, offs[i]:offs[i + 1]].reshape(BIG[n][0])
        return big

    def first_weights(self, l):
        got = self.ahead[l][:N_FIRST] if l in self.ahead else _all_gather_halves(self.mine[l][:N_FIRST], f"gather_first_{l}")
        return self._assembled(l, 'first', got)

    def gather_behind_ffn1(self, l):
        return None if l in self.ahead else self.mine[l][N_FIRST:]

    def rest_weights(self, l, got):
        return self._assembled(l, 'rest', self.ahead[l][N_FIRST:] if l in self.ahead else got)

    def gather_behind_mla(self, l):
        return self.mine[l + 1] if l + 1 < DEPTH else None

    def gathered_behind_mla(self, l, got):
        if got:
            self.ahead[l + 1] = got

    def grads_ready(self, l, which, grads):
        groups = FIRST_GROUPS if which == 'first' else REST_GROUPS
        parts = [self.halves_of(jnp.concatenate([grads[n] for n in group], axis=1)) for group in groups]
        from_sibling = _sibling_exchange(parts, f"swap_{which}_{l}", True)
        chip_sums = []
        for gi, (p, got) in enumerate(zip(parts, from_sibling)):
            kept = lax.dynamic_index_in_dim(p, self.c, axis=1, keepdims=False)
            rows = N_SHARD * p.shape[2]
            pair = _sum_blocks([kept.reshape(rows, -1), got.reshape(rows, -1)], BF16, f"sum_pair_{which}{gi}_{l}")
            chip_sums.append(pair.reshape(got.shape))
        self.begun[(l, which)] = chip_sums

    def scatter_behind_mla(self, l):
        return self.begun[(l + 1, 'first')] + self.begun[(l + 1, 'rest')] if l + 1 < DEPTH else None

    def scattered_behind_mla(self, l, got):
        if got:
            self.received[(l + 1, 'first')], self.received[(l + 1, 'rest')] = got[:N_FIRST], got[N_FIRST:]

    def scatter_behind_ffn1(self, l):
        return self.begun[(l, 'rest')] if l == 0 else None

    def scattered_behind_ffn1(self, l, got):
        if got:
            self.received[(l, 'rest')] = got

    def reduced(self):
        keys = sorted(self.begun)
        for key in keys:
            if key not in self.received:
                self.received[key] = _scatter_to_chips(self.begun[key], f"scatter_{key[1]}_{key[0]}")
        halves = []
        for l, which in keys:
            for gi, (cs, got) in enumerate(zip(self.begun[(l, which)], self.received[(l, which)])):
                own = lax.dynamic_index_in_dim(cs, self.chip, axis=0, keepdims=False)
                halves.append(_sum_blocks([own, got[0], got[1], got[2]], F32, f"sum_chips_{which}{gi}_{l}"))
        others = _sibling_exchange(halves, "share_halves", False)
        per_layer, at = {}, 0
        for l, which in keys:
            for group in (FIRST_GROUPS if which == 'first' else REST_GROUPS):
                mine_h, other_h = halves[at], others[at]
                at += 1
                full = jnp.where(self.c == 0, jnp.concatenate([mine_h, other_h]), jnp.concatenate([other_h, mine_h]))
                offs = _group_row_offsets(group)
                for i, n in enumerate(group):
                    per_layer[(n, l)] = full[offs[i]:offs[i + 1]]
        return {n: jnp.stack([per_layer[(n, l)] for l in range(DEPTH)]) for n in BIG_NAMES}


def kernel(x, ffn1_norm, ffn1_w_gate, ffn1_w_up, ffn1_w_down, mix_norm, w_in, mla_q_a_norm, mla_w_q_b, mla_kv_a_norm, mla_w_kv_b, mla_q_norm, mla_k_norm, swa_q_norm, swa_k_norm, swa_sinks, mla_out_norm, swa_out_norm, w_o, ffn2_norm, ffn2_w_gate, ffn2_w_up, ffn2_w_down, loss_target, m_ffn1_norm, m_ffn1_w_gate, m_ffn1_w_up, m_ffn1_w_down, m_mix_norm, m_w_in, m_mla_q_a_norm, m_mla_w_q_b, m_mla_kv_a_norm, m_mla_w_kv_b, m_mla_q_norm, m_mla_k_norm, m_swa_q_norm, m_swa_k_norm, m_swa_sinks, m_mla_out_norm, m_swa_out_norm, m_w_o, m_ffn2_norm, m_ffn2_w_gate, m_ffn2_w_up, m_ffn2_w_down, v_ffn1_norm, v_ffn1_w_gate, v_ffn1_w_up, v_ffn1_w_down, v_mix_norm, v_w_in, v_mla_q_a_norm, v_mla_w_q_b, v_mla_kv_a_norm, v_mla_w_kv_b, v_mla_q_norm, v_mla_k_norm, v_swa_q_norm, v_swa_k_norm, v_swa_sinks, v_mla_out_norm, v_swa_out_norm, v_w_o, v_ffn2_norm, v_ffn2_w_gate, v_ffn2_w_up, v_ffn2_w_down):
    args = dict(locals())
    transposed = lambda a: jnp.swapaxes(a, 1, 2)
    as_kernels_see = lambda n, a: transposed(a) if n in TRANSPOSED else a
    weights = {n: as_kernels_see(n, args[n]) for n in WEIGHT_NAMES}
    mom_m = {n: as_kernels_see(n, args["m_" + n]) for n in WEIGHT_NAMES}
    mom_v = {n: as_kernels_see(n, args["v_" + n]) for n in WEIGHT_NAMES}
    ex = _Exchange(weights, lax.axis_index("c"), 2 * lax.axis_index("x") + lax.axis_index("y"))
    loss, dx, gs = _local_step(x[0], loss_target[0], {n: weights[n] for n in SMALL_NAMES}, ex)

    small_flat = jnp.concatenate([jnp.stack(gs[n]).reshape(-1) for n in SMALL_NAMES] + [loss[0, :1]])
    n_small = small_flat.shape[0]
    lanes = -(-n_small // (8 * 128)) * 128
    small_sum = _all_reduce_small(jnp.pad(small_flat, (0, 8 * lanes - n_small)).reshape(8, lanes), "reduce_small").reshape(-1)
    loss_out = small_sum[n_small - 1]
    grads = ex.reduced()

    packed = lambda d: jnp.pad(jnp.concatenate([d[n].reshape(-1) for n in SMALL_NAMES]), (0, 8 * lanes - n_small + 1)).reshape(8, lanes)
    small_out = _adamw(packed(weights), small_sum.reshape(8, lanes), packed(mom_m), packed(mom_v), "adamw_small")
    deltas, new_m, new_v = {}, {}, {}
    off = 0
    for n in SMALL_NAMES:
        cnt = int(np.prod(weights[n].shape))
        grads[n], deltas[n], new_m[n], new_v[n] = (a.reshape(-1)[off:off + cnt].reshape(weights[n].shape)
                                                    for a in (small_sum, *small_out))
        off += cnt
    for n in BIG_NAMES:
        shp = weights[n].shape
        two_d = (shp[0] * shp[1], shp[2])
        d, nm, nv = _adamw(weights[n].reshape(two_d), grads[n].reshape(two_d), mom_m[n].reshape(two_d),
                           mom_v[n].reshape(two_d), f"adamw_{n}")
        deltas[n], new_m[n], new_v[n] = d.reshape(shp), nm.reshape(shp), nv.reshape(shp)
    for n in TRANSPOSED:
        grads[n], deltas[n], new_m[n], new_v[n] = (transposed(a) for a in (grads[n], deltas[n], new_m[n], new_v[n]))

    return (loss_out, dx[None], *[grads[n] for n in WEIGHT_NAMES], *[deltas[n] for n in WEIGHT_NAMES],
            *[new_m[n] for n in WEIGHT_NAMES], *[new_v[n] for n in WEIGHT_NAMES])
```

```python
import functools

import numpy as np
import jax
import jax.numpy as jnp
from jax import lax
from jax.experimental import pallas as pl
from jax.experimental.pallas import tpu as pltpu

F32 = jnp.float32
BF16 = jnp.bfloat16

D_MODEL = 1024
DEPTH = 2
EPS = 1e-6
ROPE_THETA = 10000.0
BLOCK = 128
MLA_HEADS = 4
MLA_Q_RANK = 256
MLA_KV_RANK = 128
MLA_NOPE = 128
MLA_ROPE = 64
MLA_V = 128
MLA_QK = MLA_NOPE + MLA_ROPE
MLA_WIDTH = MLA_HEADS * MLA_V
SWA_HEADS = 8
SWA_KV_HEADS = 2
SWA_GROUP = SWA_HEADS // SWA_KV_HEADS
SWA_HEAD_DIM = 64
SWA_WIDTH = SWA_HEADS * SWA_HEAD_DIM
MIX_WIDTH = MLA_WIDTH + SWA_WIDTH
IN_SPLITS = (MLA_Q_RANK, MLA_KV_RANK, MLA_ROPE, SWA_WIDTH, SWA_KV_HEADS * SWA_HEAD_DIM, SWA_KV_HEADS * SWA_HEAD_DIM)
IN_COLS = sum(IN_SPLITS)
IN_OFFS = tuple(int(v) for v in np.cumsum((0,) + IN_SPLITS))
D_FF = 2816
MLA_SCALE = MLA_QK ** -0.5
LOG2E = 1.4426950408889634
LN2 = 0.6931471805599453
MLA_QSCALE = MLA_SCALE * LOG2E
SWA_SCALE = SWA_HEAD_DIM ** -0.5
NEG = -1e30

ADAM_LR = 0.001
ADAM_B1 = 0.9
ADAM_B2 = 0.999
ADAM_EPS = 1e-08
ADAM_WD = 0.01
ADAM_STEP = 10

N_SHARD = 4
N_DEV = 8
VMEM_LIMIT = 56 * 1024 * 1024
MESH = pl.DeviceIdType.MESH

WEIGHT_NAMES = ['ffn1_norm', 'ffn1_w_gate', 'ffn1_w_up', 'ffn1_w_down', 'mix_norm', 'w_in', 'mla_q_a_norm', 'mla_w_q_b',
                'mla_kv_a_norm', 'mla_w_kv_b', 'mla_q_norm', 'mla_k_norm', 'swa_q_norm', 'swa_k_norm', 'swa_sinks',
                'mla_out_norm', 'swa_out_norm', 'w_o', 'ffn2_norm', 'ffn2_w_gate', 'ffn2_w_up', 'ffn2_w_down']
TRANSPOSED = ('ffn1_w_gate', 'ffn1_w_up', 'ffn2_w_gate', 'ffn2_w_up')
BIG = {'ffn1_w_gate': ((D_FF, D_MODEL), 0), 'ffn1_w_up': ((D_FF, D_MODEL), 0), 'ffn1_w_down': ((D_FF, D_MODEL), 0),
       'w_in': ((D_MODEL, IN_COLS), 1), 'mla_w_q_b': ((MLA_Q_RANK, MLA_HEADS * MLA_QK), 1),
       'mla_w_kv_b': ((MLA_KV_RANK, MLA_HEADS * (MLA_NOPE + MLA_V)), 1), 'w_o': ((MIX_WIDTH, D_MODEL), 0),
       'ffn2_w_gate': ((D_FF, D_MODEL), 0), 'ffn2_w_up': ((D_FF, D_MODEL), 0), 'ffn2_w_down': ((D_FF, D_MODEL), 0)}
BIG_NAMES = [n for n in WEIGHT_NAMES if n in BIG]
SMALL_NAMES = [n for n in WEIGHT_NAMES if n not in BIG]

_pallas_call = pl.pallas_call


def _params(**kw):
    return pltpu.CompilerParams(vmem_limit_bytes=VMEM_LIMIT, **kw)


def _full(shape):
    n = len(shape)
    return pl.BlockSpec(shape, lambda *_: (0,) * n)


def _resident(shape):
    n = len(shape)
    return pl.BlockSpec(shape, lambda *_: (0,) * n, pipeline_mode=pl.Buffered(1))


@jax.custom_vjp
def _mm(a, w):
    return jnp.dot(a.astype(BF16), w, preferred_element_type=F32)


def _mm_fwd(a, w):
    return _mm(a, w), w


def _mm_bwd(w, dy):
    return lax.dot_general(dy.astype(BF16), w, (((1,), (1,)), ((), ())), preferred_element_type=F32), None


_mm.defvjp(_mm_fwd, _mm_bwd)


def _dot_nt(a, b):
    return lax.dot_general(a, b, (((1,), (1,)), ((), ())), preferred_element_type=F32)


def _dot_tn(a, b):
    return lax.dot_general(a, b, (((0,), (0,)), ((), ())), preferred_element_type=F32)


def _rms(t, g):
    return t * lax.rsqrt(jnp.mean(t * t, axis=-1, keepdims=True) + EPS) * g


def _sigmoid(z):
    return 1.0 / (1.0 + jnp.exp(-z))


def _row_tile(s, want):
    return min(want, s)


def _divisor_tile(rows, cap):
    return max(d for d in range(16, min(rows, cap) + 1, 16) if rows % d == 0)


FF_CHUNK = 1408


def _weight_operand(w):
    if isinstance(w, tuple):
        arr, block, index = w
        return arr, pl.BlockSpec(block, lambda *_: index, pipeline_mode=pl.Buffered(1))
    return w, _resident(w.shape)


def _weight_rows(ref, start, n):
    if len(ref.shape) == 2:
        return ref[start:start + n, :]
    per = ref.shape[1]
    return ref[start // per:(start + n) // per].reshape(n, ref.shape[2])


def _ffn_fwd(x, g, wg, wu, wd, name, gather=None):
    s = x.shape[0]
    tm = _row_tile(s, 256)
    steps = s // tm
    ng = len(gather) if gather else 0
    (wg, wg_spec), (wu, wu_spec), (wd, wd_spec) = _weight_operand(wg), _weight_operand(wu), _weight_operand(wd)

    def body(x_ref, g_ref, wg_ref, wu_ref, wd_ref, *rest):
        g_ins, (y_ref, gate_ref, up_ref), g_outs, sems = rest[:ng], rest[ng:ng + 3], rest[ng + 3:2 * ng + 3], rest[2 * ng + 3:]
        if ng:
            _run_stages_at(_gather_stages(g_ins, g_outs, *sems), [(0,), (steps * 3 // 8,), (steps * 11 // 16,), (steps - 1,)])
        xv = x_ref[...]
        nb = _rms(xv, g_ref[...]).astype(BF16)
        acc = xv
        for c in range(0, D_FF, FF_CHUNK):
            gate = _dot_nt(nb, _weight_rows(wg_ref, c, FF_CHUNK))
            up = _dot_nt(nb, _weight_rows(wu_ref, c, FF_CHUNK))
            gate_ref[:, c:c + FF_CHUNK] = gate.astype(BF16)
            up_ref[:, c:c + FF_CHUNK] = up.astype(BF16)
            act = (gate * _sigmoid(gate) * up).astype(BF16)
            acc = acc + 0.5 * jnp.dot(act, _weight_rows(wd_ref, c, FF_CHUNK), preferred_element_type=F32)
        y_ref[...] = acc

    outs = _pallas_call(
        body, name=name, grid=(steps,),
        in_specs=[pl.BlockSpec((tm, D_MODEL), lambda i: (i, 0)), _full((1, D_MODEL)), wg_spec, wu_spec, wd_spec] + [_HBM] * ng,
        out_specs=[pl.BlockSpec((tm, D_MODEL), lambda i: (i, 0)), pl.BlockSpec((tm, D_FF), lambda i: (i, 0)),
                   pl.BlockSpec((tm, D_FF), lambda i: (i, 0))] + [_HBM] * ng,
        out_shape=[jax.ShapeDtypeStruct((s, D_MODEL), F32), jax.ShapeDtypeStruct((s, D_FF), BF16),
                   jax.ShapeDtypeStruct((s, D_FF), BF16)] + (_gather_out_shapes(gather) if ng else []),
        scratch_shapes=_exchange_scratch(8, ng) if ng else [],
        compiler_params=_params(dimension_semantics=("arbitrary",)),
    )(x, g, wg, wu, wd, *(gather or []))
    return outs[0], outs[1], outs[2], outs[3:]


def _ffn_bwd(x, dy, gate, up, g, wg, wu, wd, name, scatter=None):
    s = x.shape[0]
    tm = _row_tile(s, 256)
    steps = s // tm
    ng = len(scatter) if scatter else 0
    (wg, wg_spec), (wu, wu_spec), (wd, wd_spec) = _weight_operand(wg), _weight_operand(wu), _weight_operand(wd)

    def body(x_ref, dy_ref, gate_ref, up_ref, g_ref, wg_ref, wu_ref, wd_ref, *rest):
        c_ins, (dx_ref, dgain_ref, n_ref, act_ref, dgate_ref, dup_ref) = rest[:ng], rest[ng:ng + 6]
        c_outs, sems = rest[ng + 6:2 * ng + 6], rest[2 * ng + 6:]
        if ng:
            _run_stages_at(_scatter_stages(c_ins, c_outs, *sems), [(0,), (steps - 1,)])
        i = pl.program_id(0)
        xv = x_ref[...]
        dyv = dy_ref[...]
        gv = g_ref[...]
        r = lax.rsqrt(jnp.mean(xv * xv, axis=-1, keepdims=True) + EPS)
        xh = xv * r
        n_ref[...] = (xh * gv).astype(BF16)
        dyh = (0.5 * dyv).astype(BF16)
        dn = jnp.zeros_like(xv)
        for c in range(0, D_FF, FF_CHUNK):
            dact = _dot_nt(dyh, _weight_rows(wd_ref, c, FF_CHUNK))
            gt = gate_ref[:, c:c + FF_CHUNK].astype(F32)
            u = up_ref[:, c:c + FF_CHUNK].astype(F32)
            sg = _sigmoid(gt)
            sl = gt * sg
            act_ref[:, c:c + FF_CHUNK] = (sl * u).astype(BF16)
            dup = (dact * sl).astype(BF16)
            dgate = (dact * u * (sg * (1.0 + gt * (1.0 - sg)))).astype(BF16)
            dup_ref[:, c:c + FF_CHUNK] = dup
            dgate_ref[:, c:c + FF_CHUNK] = dgate
            dn = (dn + jnp.dot(dgate, _weight_rows(wg_ref, c, FF_CHUNK), preferred_element_type=F32)
                  + jnp.dot(dup, _weight_rows(wu_ref, c, FF_CHUNK), preferred_element_type=F32))
        part = jnp.sum(dn * xh, axis=0, keepdims=True)

        @pl.when(i == 0)
        def _():
            dgain_ref[...] = part

        @pl.when(i > 0)
        def _():
            dgain_ref[...] += part

        dxh = dn * gv
        dx_ref[...] = dyv + r * (dxh - xh * jnp.mean(dxh * xh, axis=-1, keepdims=True))

    row = lambda w: pl.BlockSpec((tm, w), lambda i: (i, 0))
    outs = _pallas_call(
        body, name=name, grid=(steps,),
        in_specs=[row(D_MODEL), row(D_MODEL), row(D_FF), row(D_FF), _full((1, D_MODEL)), wg_spec, wu_spec, wd_spec]
        + [_HBM] * ng,
        out_specs=[row(D_MODEL), _full((1, D_MODEL)), row(D_MODEL), row(D_FF), row(D_FF), row(D_FF)] + [_HBM] * ng,
        out_shape=[jax.ShapeDtypeStruct((s, D_MODEL), F32), jax.ShapeDtypeStruct((1, D_MODEL), F32),
                   jax.ShapeDtypeStruct((s, D_MODEL), BF16), jax.ShapeDtypeStruct((s, D_FF), BF16),
                   jax.ShapeDtypeStruct((s, D_FF), BF16), jax.ShapeDtypeStruct((s, D_FF), BF16)]
        + (_scatter_out_shapes(scatter) if ng else []),
        scratch_shapes=_exchange_scratch(3, ng) if ng else [],
        compiler_params=_params(dimension_semantics=("arbitrary",)),
    )(x, dy, gate, up, g, wg, wu, wd, *(scatter or []))
    return outs[:6], outs[6:]


def _store_col_shards(o_ref, acc, first_shard, n_here, width):
    for q in range(n_here):
        o_ref[q] = acc[:, (first_shard + q) * width:(first_shard + q + 1) * width].astype(BF16)


def _matmul_tn(a, b, scale, name, group=None, slot=0, slots=1):
    t, m = a.shape
    n = b.shape[1]
    tk = _row_tile(t, 2048)
    tn = n // 2
    nk = t // tk
    per = m // N_SHARD

    def body(a_ref, b_ref, *rest):
        o_ref, acc_ref = rest[-2:]
        k = pl.program_id(1)
        bv = b_ref[...]
        if scale != 1.0:
            bv = bv.astype(F32) * scale
        part = _dot_tn(a_ref[...].astype(BF16), bv.astype(BF16))

        @pl.when(k == 0)
        def _():
            acc_ref[...] = part

        @pl.when(k > 0)
        def _():
            acc_ref[...] += part

        @pl.when(k == nk - 1)
        def _():
            for sh in range(N_SHARD):
                o_ref[sh] = acc_ref[sh * per:(sh + 1) * per, :].astype(BF16)

    earlier = [] if group is None else [group]
    return _pallas_call(
        body, name=name, grid=(n // tn, nk),
        in_specs=[pl.BlockSpec((tk, m), lambda j, k: (k, 0)), pl.BlockSpec((tk, tn), lambda j, k: (k, j))]
        + [pl.BlockSpec(memory_space=pl.ANY)] * len(earlier),
        out_specs=pl.BlockSpec((N_SHARD, per, tn), lambda j, k: (0, slot, j)),
        out_shape=jax.ShapeDtypeStruct((N_SHARD, slots * per, n), BF16),
        scratch_shapes=[pltpu.VMEM((m, tn), F32)], input_output_aliases={2: 0} if earlier else {},
        compiler_params=_params(dimension_semantics=("arbitrary", "arbitrary")),
    )(a, b, *earlier)


_HALF = SWA_HEAD_DIM // 2
_IN_ORDER = (list(range(0, IN_OFFS[2]))
             + [IN_OFFS[3] + SWA_HEAD_DIM * h + i for h in range(SWA_HEADS) for i in range(_HALF)]
             + [IN_OFFS[3] + SWA_HEAD_DIM * h + _HALF + i for h in range(SWA_HEADS) for i in range(_HALF)]
             + list(range(IN_OFFS[5], IN_OFFS[6]))
             + [IN_OFFS[4] + SWA_HEAD_DIM * j + i for j in range(SWA_KV_HEADS) for i in range(_HALF)]
             + [IN_OFFS[4] + SWA_HEAD_DIM * j + _HALF + i for j in range(SWA_KV_HEADS) for i in range(_HALF)]
             + list(range(IN_OFFS[2], IN_OFFS[3])))
_QB_ORDER = ([MLA_QK * h + i for h in range(MLA_HEADS) for i in range(MLA_NOPE)]
             + [MLA_QK * h + MLA_NOPE + i for h in range(MLA_HEADS) for i in range(_HALF)]
             + [MLA_QK * h + MLA_NOPE + _HALF + i for h in range(MLA_HEADS) for i in range(_HALF)])
_KVB_ORDER = ([(MLA_NOPE + MLA_V) * h + i for h in range(MLA_HEADS) for i in range(MLA_NOPE)]
              + [(MLA_NOPE + MLA_V) * h + MLA_NOPE + i for h in range(MLA_HEADS) for i in range(MLA_V)])
MIXER_ORDERS = {'w_in': _IN_ORDER, 'mla_w_q_b': _QB_ORDER, 'mla_w_kv_b': _KVB_ORDER}
_P_CQ, _P_CKV, _P_QA, _P_QB, _P_VS, _P_KA, _P_KB, _P_PE = (int(v) for v in np.cumsum(
    (0, MLA_Q_RANK, MLA_KV_RANK, SWA_WIDTH // 2, SWA_WIDTH // 2, IN_SPLITS[5], IN_SPLITS[4] // 2, IN_SPLITS[4] // 2)))


def _runs(order):
    out, start = [], 0
    for i in range(1, len(order) + 1):
        if i == len(order) or order[i] != order[i - 1] + 1:
            out.append((order[start], i - start))
            start = i
    return out


def _inverse(order):
    inv = [0] * len(order)
    for new, old in enumerate(order):
        inv[old] = new
    return inv


def _take_cols(a, order):
    return jnp.concatenate([a[..., st:st + w] for st, w in _runs(order)], axis=-1)


def _segment_matrix(n, seg):
    return (lax.broadcasted_iota(jnp.int32, (n, n), 0) // seg == lax.broadcasted_iota(jnp.int32, (n, n), 1) // seg).astype(BF16)


@jax.custom_vjp
def _cmm(t, b, bt):
    hi = t.astype(BF16)
    lo = (t - hi.astype(F32)).astype(BF16)
    return jnp.dot(hi, b, preferred_element_type=F32) + jnp.dot(lo, b, preferred_element_type=F32)


def _cmm_fwd(t, b, bt):
    return _cmm(t, b, bt), (b, bt)


def _cmm_bwd(res, dy):
    b, bt = res
    return _cmm(dy, bt, b), None, None


_cmm.defvjp(_cmm_fwd, _cmm_bwd)


def _segsum(t, b):
    return _cmm(t, b, b)


def _rowsum(t):
    n = t.shape[-1]
    return _cmm(t, jnp.ones((n, 128), BF16), jnp.ones((128, n), BF16))


def _by_head(vals, width):
    lane = lax.broadcasted_iota(jnp.int32, (vals[0].shape[0], len(vals) * width), 1)
    out = vals[-1]
    for hd in range(len(vals) - 2, -1, -1):
        out = jnp.where(lane < (hd + 1) * width, vals[hd], out)
    return out


def _rope2(a, b, cos, sin):
    return a * cos - b * sin, b * cos + a * sin


def _pre_math(x, gm, gqa, gkva, gq, gk, gsq, gsk, taps, win, wqb, wkvb, cos, sin):
    h = _rms(x, gm)
    proj = _mm(h, win)
    if taps is not None:
        proj = proj + taps[0]
    cqn = _rms(proj[:, _P_CQ:_P_CKV], gqa)
    qa_all = _mm(cqn, wqb)
    ckvn = _rms(proj[:, _P_CKV:_P_QA], gkva)
    kv_all = _mm(ckvn, wkvb)
    if taps is not None:
        qa_all = qa_all + taps[1]
        kv_all = kv_all + taps[2]
    nh, hw = MLA_HEADS, MLA_HEADS * _HALF
    seg_mla = _segment_matrix(hw, _HALF)
    tile = lambda g, n: jnp.concatenate([g] * n, axis=-1)
    c4, s4 = cos[:, :hw], sin[:, :hw]

    def mla_heads(nope, r1, r2, gain):
        rr = r1 * r1 + r2 * r2
        lane_head = lax.broadcasted_iota(jnp.int32, (hw, nh * MLA_NOPE), 0) // _HALF
        spread = (lane_head == lax.broadcasted_iota(jnp.int32, (hw, nh * MLA_NOPE), 1) // MLA_NOPE).astype(BF16)
        rope_on_nope = _cmm(rr, spread, spread.T)
        ss_nope = [_rowsum(jnp.square(nope[:, hd * MLA_NOPE:(hd + 1) * MLA_NOPE])) for hd in range(nh)]
        rinv = [lax.rsqrt((ss_nope[hd] + rope_on_nope[:, hd * MLA_NOPE:(hd + 1) * MLA_NOPE]) * (1.0 / MLA_QK) + EPS)
                for hd in range(nh)]
        rl = lax.rsqrt((_segsum(rr, seg_mla) + _by_head(ss_nope, _HALF)) * (1.0 / MLA_QK) + EPS)
        o1, o2 = _rope2(r1 * rl * tile(gain[:, MLA_NOPE:MLA_NOPE + _HALF], nh), r2 * rl * tile(gain[:, MLA_NOPE + _HALF:], nh), c4, s4)
        return [jnp.concatenate([nope[:, hd * MLA_NOPE:(hd + 1) * MLA_NOPE] * rinv[hd] * gain[:, :MLA_NOPE],
                                 o1[:, hd * _HALF:(hd + 1) * _HALF], o2[:, hd * _HALF:(hd + 1) * _HALF]], axis=-1)
                for hd in range(nh)]

    q_a = mla_heads(qa_all[:, :nh * MLA_NOPE], qa_all[:, nh * MLA_NOPE:nh * MLA_NOPE + hw], qa_all[:, nh * MLA_NOPE + hw:], gq)
    pe1, pe2 = proj[:, _P_PE:_P_PE + _HALF], proj[:, _P_PE + _HALF:_P_PE + 2 * _HALF]
    k_a = mla_heads(kv_all[:, :nh * MLA_NOPE], tile(pe1, nh), tile(pe2, nh), gk)
    v_a = [kv_all[:, nh * MLA_NOPE + hd * MLA_V:nh * MLA_NOPE + (hd + 1) * MLA_V] for hd in range(nh)]

    def swa_heads(a, b, gain, n):
        w = n * _HALF
        r = lax.rsqrt(_segsum(a * a + b * b, _segment_matrix(w, _HALF)) * (1.0 / SWA_HEAD_DIM) + EPS)
        o1, o2 = _rope2(a * r * tile(gain[:, :_HALF], n), b * r * tile(gain[:, _HALF:], n), cos[:, :w], sin[:, :w])
        return [jnp.concatenate([o1[:, hd * _HALF:(hd + 1) * _HALF], o2[:, hd * _HALF:(hd + 1) * _HALF]], axis=-1) for hd in range(n)]

    q_b = swa_heads(proj[:, _P_QA:_P_QB], proj[:, _P_QB:_P_VS], gsq, SWA_HEADS)
    k_b = swa_heads(proj[:, _P_KA:_P_KB], proj[:, _P_KB:_P_PE], gsk, SWA_KV_HEADS)
    v_b = [proj[:, _P_VS + j * SWA_HEAD_DIM:_P_VS + (j + 1) * SWA_HEAD_DIM] for j in range(SWA_KV_HEADS)]
    return (q_a, k_a, v_a, q_b, k_b, v_b), (h, cqn, ckvn)


_PRE_GAIN_WIDTHS = (D_MODEL, MLA_Q_RANK, MLA_KV_RANK, MLA_QK, MLA_QK, SWA_HEAD_DIM, SWA_HEAD_DIM)
_PRE_HEADS = ((MLA_HEADS, MLA_QK), (MLA_HEADS, MLA_QK), (MLA_HEADS, MLA_V),
              (SWA_HEADS, SWA_HEAD_DIM), (SWA_KV_HEADS, SWA_HEAD_DIM), (SWA_KV_HEADS, SWA_HEAD_DIM))


def _pre_fwd(x, gains, win, wqb, wkvb, cos, sin, name):
    s = x.shape[0]
    tm = _row_tile(s, 512)

    def body(x_ref, *refs):
        g_refs, (win_ref, wqb_ref, wkvb_ref, cos_ref, sin_ref), out_refs = refs[:7], refs[7:12], refs[12:]
        outs, _ = _pre_math(x_ref[...], *[g[...] for g in g_refs], None, win_ref[...], wqb_ref[...], wkvb_ref[...],
                            cos_ref[...], sin_ref[...])
        for idx, (ref, heads) in enumerate(zip(out_refs, outs)):
            for hd, val in enumerate(heads):
                ref[hd] = (val * MLA_QSCALE if idx == 0 else val).astype(BF16)

    heads_spec = lambda nh, w: pl.BlockSpec((nh, tm, w), lambda i: (0, i, 0))
    return _pallas_call(
        body, name=name, grid=(s // tm,),
        in_specs=[pl.BlockSpec((tm, D_MODEL), lambda i: (i, 0))] + [_full((1, w)) for w in _PRE_GAIN_WIDTHS]
        + [_resident(win.shape), _resident(wqb.shape), _resident(wkvb.shape),
           pl.BlockSpec((tm, SWA_HEADS * _HALF), lambda i: (i, 0)), pl.BlockSpec((tm, SWA_HEADS * _HALF), lambda i: (i, 0))],
        out_specs=[heads_spec(nh, w) for nh, w in _PRE_HEADS],
        out_shape=[jax.ShapeDtypeStruct((nh, s, w), BF16) for nh, w in _PRE_HEADS],
        compiler_params=_params(dimension_semantics=("arbitrary",)),
    )(x, *gains, win, wqb, wkvb, cos, sin)


def _pre_bwd(x, dx_res, cts, gains, win, wqb, wkvb, cos, sin, name):
    s = x.shape[0]
    tm = _row_tile(s, 256)
    tap_widths = (IN_COLS, MLA_HEADS * MLA_QK, MLA_HEADS * (MLA_NOPE + MLA_V))

    def body(x_ref, dxr_ref, *refs):
        ct_refs, g_refs = refs[:6], refs[6:13]
        win_ref, wqb_ref, wkvb_ref, cos_ref, sin_ref = refs[13:18]
        dx_ref, dg_refs, dw_refs, acc_refs = refs[18], refs[19:26], refs[26:29], refs[29:32]
        i = pl.program_id(0)
        win_v, wqb_v, wkvb_v, cos_v, sin_v = win_ref[...], wqb_ref[...], wkvb_ref[...], cos_ref[...], sin_ref[...]

        def f(xv, gm, gqa, gkva, gq, gk, gsq, gsk, t0, t1, t2):
            return _pre_math(xv, gm, gqa, gkva, gq, gk, gsq, gsk, (t0, t1, t2), win_v, wqb_v, wkvb_v, cos_v, sin_v)

        taps = [jnp.zeros((tm, w), F32) for w in tap_widths]
        _, vjp, acts = jax.vjp(f, x_ref[...], *[g[...] for g in g_refs], *taps, has_aux=True)
        ct = tuple([ref[hd] for hd in range(nh)] for ref, (nh, _) in zip(ct_refs, _PRE_HEADS))
        grads = vjp(ct)
        dx_ref[...] = grads[0] + dxr_ref[...]
        dws = [_dot_tn(a.astype(BF16), t.astype(BF16)) for a, t in zip(acts, grads[8:11])]

        @pl.when(i == 0)
        def _():
            for ref, val in zip(dg_refs, grads[1:8]):
                ref[...] = val
            for ref, val in zip(acc_refs, dws):
                ref[...] = val

        @pl.when(i > 0)
        def _():
            for ref, val in zip(dg_refs, grads[1:8]):
                ref[...] += val
            for ref, val in zip(acc_refs, dws):
                ref[...] += val

        @pl.when(i == s // tm - 1)
        def _():
            for ref, acc, order in zip(dw_refs, acc_refs, (_IN_ORDER, _QB_ORDER, _KVB_ORDER)):
                _store_col_shards(ref, _take_cols(acc[...], _inverse(order)), 0, N_SHARD, acc.shape[1] // N_SHARD)

    heads_spec = lambda nh, w: pl.BlockSpec((nh, tm, w), lambda i: (0, i, 0))
    row = pl.BlockSpec((tm, D_MODEL), lambda i: (i, 0))
    half = pl.BlockSpec((tm, SWA_HEADS * _HALF), lambda i: (i, 0))
    shard_shapes = [(N_SHARD, w.shape[0], w.shape[1] // N_SHARD) for w in (win, wqb, wkvb)]
    return _pallas_call(
        body, name=name, grid=(s // tm,),
        in_specs=[row, row] + [heads_spec(nh, w) for nh, w in _PRE_HEADS] + [_full((1, w)) for w in _PRE_GAIN_WIDTHS]
        + [_resident(win.shape), _resident(wqb.shape), _resident(wkvb.shape), half, half],
        out_specs=[row] + [_full((1, w)) for w in _PRE_GAIN_WIDTHS] + [_full(shp) for shp in shard_shapes],
        out_shape=[jax.ShapeDtypeStruct((s, D_MODEL), F32)] + [jax.ShapeDtypeStruct((1, w), F32) for w in _PRE_GAIN_WIDTHS]
        + [jax.ShapeDtypeStruct(shp, BF16) for shp in shard_shapes],
        scratch_shapes=[pltpu.VMEM(w.shape, F32) for w in (win, wqb, wkvb)],
        compiler_params=_params(dimension_semantics=("arbitrary",)),
    )(x, dx_res, *cts, *gains, win, wqb, wkvb, cos, sin)


def _post_math(oa, ob, ga, gb, wo):
    mixed = jnp.concatenate([_rms(jnp.concatenate(oa, axis=-1), ga), _rms(jnp.concatenate(ob, axis=-1), gb)], axis=-1)
    return _mm(mixed, wo), mixed


def _post_fwd(x, oa, ob, ga, gb, wo, name):
    s = x.shape[0]
    tm = _row_tile(s, 512)

    def body(x_ref, oa_ref, ob_ref, ga_ref, gb_ref, wo_ref, y_ref):
        y, _ = _post_math([oa_ref[hd] for hd in range(MLA_HEADS)], [ob_ref[hd] for hd in range(SWA_HEADS)],
                          ga_ref[...], gb_ref[...], wo_ref[...])
        y_ref[...] = x_ref[...] + y

    row = pl.BlockSpec((tm, D_MODEL), lambda i: (i, 0))
    return _pallas_call(
        body, name=name, grid=(s // tm,),
        in_specs=[row, pl.BlockSpec((MLA_HEADS, tm, MLA_V), lambda i: (0, i, 0)),
                  pl.BlockSpec((SWA_HEADS, tm, SWA_HEAD_DIM), lambda i: (0, i, 0)),
                  _full((1, MLA_WIDTH)), _full((1, SWA_WIDTH)), _resident(wo.shape)],
        out_specs=row, out_shape=jax.ShapeDtypeStruct((s, D_MODEL), F32),
        compiler_params=_params(dimension_semantics=("arbitrary",)),
    )(x, oa, ob, ga, gb, wo)


def _post_bwd(dy, oa, ob, ga, gb, wo, name):
    s = dy.shape[0]
    tm = _row_tile(s, 512)

    def body(dy_ref, oa_ref, ob_ref, ga_ref, gb_ref, wo_ref, doa_ref, dob_ref, dga_ref, dgb_ref, dwo_ref, acc_ref):
        i = pl.program_id(0)
        wo_v = wo_ref[...]
        dyv = dy_ref[...]

        def f(oa_l, ob_l, ga_v, gb_v):
            return _post_math(oa_l, ob_l, ga_v, gb_v, wo_v)

        _, vjp, mixed = jax.vjp(f, [oa_ref[hd] for hd in range(MLA_HEADS)], [ob_ref[hd] for hd in range(SWA_HEADS)],
                                ga_ref[...], gb_ref[...], has_aux=True)
        doa, dob, dga, dgb = vjp(dyv)
        for hd in range(MLA_HEADS):
            doa_ref[hd] = doa[hd]
        for hd in range(SWA_HEADS):
            dob_ref[hd] = dob[hd]
        dwo = _dot_tn(mixed.astype(BF16), dyv.astype(BF16))

        @pl.when(i == 0)
        def _():
            dga_ref[...] = dga
            dgb_ref[...] = dgb
            acc_ref[...] = dwo

        @pl.when(i > 0)
        def _():
            dga_ref[...] += dga
            dgb_ref[...] += dgb
            acc_ref[...] += dwo

        @pl.when(i == s // tm - 1)
        def _():
            dwo_ref[...] = acc_ref[...].astype(BF16)

    row = pl.BlockSpec((tm, D_MODEL), lambda i: (i, 0))
    oa_spec = pl.BlockSpec((MLA_HEADS, tm, MLA_V), lambda i: (0, i, 0))
    ob_spec = pl.BlockSpec((SWA_HEADS, tm, SWA_HEAD_DIM), lambda i: (0, i, 0))
    return _pallas_call(
        body, name=name, grid=(s // tm,),
        in_specs=[row, oa_spec, ob_spec, _full((1, MLA_WIDTH)), _full((1, SWA_WIDTH)), _resident(wo.shape)],
        out_specs=[oa_spec, ob_spec, _full((1, MLA_WIDTH)), _full((1, SWA_WIDTH)), _full(wo.shape)],
        out_shape=[jax.ShapeDtypeStruct((MLA_HEADS, s, MLA_V), F32), jax.ShapeDtypeStruct((SWA_HEADS, s, SWA_HEAD_DIM), F32),
                   jax.ShapeDtypeStruct((1, MLA_WIDTH), F32), jax.ShapeDtypeStruct((1, SWA_WIDTH), F32),
                   jax.ShapeDtypeStruct(wo.shape, BF16)],
        scratch_shapes=[pltpu.VMEM(wo.shape, F32)],
        compiler_params=_params(dimension_semantics=("arbitrary",)),
    )(dy, oa, ob, ga, gb, wo)


def _attn_tile(s):
    return 512 if s >= 2048 else 128


def _as_rows(cols):
    return cols.T[0:8, :]


def _causal_mask(t):
    return lax.broadcasted_iota(jnp.int32, (t, t), 1) <= lax.broadcasted_iota(jnp.int32, (t, t), 0)


def _pipelined_blocks(first, count, last_block, issue, consume, carry, prefetch_after):
    def clamped(j, slot):
        issue(jnp.minimum(j, last_block), slot)

    def pair(jj, c):
        a = first + 2 * jj
        clamped(a + 1, 1)
        c = consume(a, 0, c)
        clamped(a + 2, 0)
        return consume(a + 1, 1, c)

    clamped(first, 0)
    npairs = count // 2
    carry = lax.fori_loop(0, npairs, pair, carry)

    def odd(c):
        c = consume(first + 2 * npairs, 0, c)
        if prefetch_after:
            clamped(first + count, 0)
        return c

    return lax.cond(count - 2 * npairs == 1, odd, lambda c: c, carry)


def _run_stages_at(stages, steps):
    for stage, step in zip(stages, steps):
        here = pl.program_id(0) == step[0]
        for axis in range(1, len(step)):
            here = here & (pl.program_id(axis) == step[axis])
        pl.when(here)(stage)


def _mla_fwd(q, k, v, name, gather=None):
    nh, s, _ = q.shape
    t = _attn_tile(s)
    nq = s // t
    ng = len(gather) if gather else 0

    def body(q_ref, k_ref, v_ref, *rest):
        g_ins, (o_ref, lse_ref), g_outs = rest[:ng], rest[ng:ng + 2], rest[ng + 2:2 * ng + 2]
        (s0_ref, s1_ref), sems = rest[2 * ng + 2:2 * ng + 4], rest[2 * ng + 4:]
        if ng:
            _run_stages_at(_gather_stages(g_ins, g_outs, *sems), [(0, 0), (nh // 2, 0), (nh - 1, 0), (nh - 1, nq - 1)])
        qi = pl.program_id(1)
        qv = q_ref[...]
        s_refs = (s0_ref, s1_ref)

        def rows(j):
            return pl.ds(pl.multiple_of(j * t, t), t)

        def issue(j, slot):
            s_refs[slot][...] = _dot_nt(qv, k_ref[rows(j), :])

        def consume(j, slot, carry, masked=False):
            m, l, acc = carry
            sc = s_refs[slot][...]
            if masked:
                sc = jnp.where(_causal_mask(t), sc, NEG)
            m_new = jnp.maximum(m, jnp.max(sc, axis=-1, keepdims=True))
            alpha = jnp.exp2(m - m_new)
            p = jnp.exp2(sc - m_new)
            l = alpha * l + jnp.sum(p, axis=-1, keepdims=True)
            acc = alpha * acc + jnp.dot(p.astype(BF16), v_ref[rows(j), :], preferred_element_type=F32)
            return m_new, l, acc

        init = (jnp.full((t, 1), NEG, F32), jnp.zeros((t, 1), F32), jnp.zeros((t, MLA_V), F32))
        carry = _pipelined_blocks(0, qi, nq - 1, issue, consume, init, True)
        m, l, acc = consume(qi, 0, carry, masked=True)
        o_ref[...] = acc / l
        lse_ref[...] = _as_rows(jnp.broadcast_to(m + jnp.log2(l), (t, 128)))

    outs = _pallas_call(
        body, name=name, grid=(nh, nq),
        in_specs=[pl.BlockSpec((None, t, MLA_QK), lambda h, i: (h, i, 0)), pl.BlockSpec((None, s, MLA_QK), lambda h, i: (h, 0, 0)),
                  pl.BlockSpec((None, s, MLA_V), lambda h, i: (h, 0, 0))] + [_HBM] * ng,
        out_specs=[pl.BlockSpec((None, t, MLA_V), lambda h, i: (h, i, 0)), pl.BlockSpec((None, None, 8, t), lambda h, i: (h, i, 0, 0))]
        + [_HBM] * ng,
        out_shape=[jax.ShapeDtypeStruct((nh, s, MLA_V), F32), jax.ShapeDtypeStruct((nh, nq, 8, t), F32)]
        + (_gather_out_shapes(gather) if ng else []),
        scratch_shapes=[pltpu.VMEM((t, t), F32)] * 2 + (_exchange_scratch(8, ng) if ng else []),
        compiler_params=_params(dimension_semantics=("arbitrary", "arbitrary")),
    )(q, k, v, *(gather or []))
    return outs[0], outs[1], outs[2:]


def _mla_delta(o, do, name):
    nh, s, _ = o.shape
    t = _attn_tile(s)
    per_step = min(4, s // t)

    def body(o_ref, do_ref, delta_ref):
        rows = _as_rows(_rowsum(do_ref[...] * o_ref[...]))
        for j in range(per_step):
            delta_ref[j] = rows[:, j * t:(j + 1) * t]

    tile = pl.BlockSpec((None, per_step * t, MLA_V), lambda h, i: (h, i, 0))
    return _pallas_call(body, name=name, grid=(nh, s // (per_step * t)), in_specs=[tile, tile],
                        out_specs=pl.BlockSpec((None, per_step, 8, t), lambda h, i: (h, i, 0, 0)),
                        out_shape=jax.ShapeDtypeStruct((nh, s // t, 8, t), F32),
                        compiler_params=_params(dimension_semantics=("arbitrary", "arbitrary")))(o, do)


def _mla_bwd(q, k, v, do, lse_row, delta_row, name, scatter=None):
    nh, s, _ = q.shape
    t = _attn_tile(s)
    nq = s // t
    ng = len(scatter) if scatter else 0

    def body(q_ref, k_ref, v_ref, do_ref, lse_ref, delta_ref, *rest):
        c_ins, (dq_ref, dk_ref, dv_ref), c_outs = rest[:ng], rest[ng:ng + 3], rest[ng + 3:2 * ng + 3]
        (s0_ref, s1_ref, dp0_ref, dp1_ref), sems = rest[2 * ng + 3:2 * ng + 7], rest[2 * ng + 7:]
        if ng:
            _run_stages_at(_scatter_stages(c_ins, c_outs, *sems), [(0, 0), (nh - 1, nq - 1)])
        kj = pl.program_id(1)
        kv_, vv = k_ref[...], v_ref[...]
        s_refs, dp_refs = (s0_ref, s1_ref), (dp0_ref, dp1_ref)

        @pl.when(kj == 0)
        def _():
            dq_ref[...] = jnp.zeros_like(dq_ref)

        def rows(i):
            return pl.ds(pl.multiple_of(i * t, t), t)

        def issue(i, slot):
            s_refs[slot][...] = _dot_nt(kv_, q_ref[rows(i), :])
            dp_refs[slot][...] = _dot_nt(vv, do_ref[rows(i), :].astype(BF16))

        def consume(i, slot, carry, masked=False):
            dk, dv = carry
            p = jnp.exp2(s_refs[slot][...] - lse_ref[i][0:1, :])
            if masked:
                p = jnp.where(lax.broadcasted_iota(jnp.int32, (t, t), 0) <= lax.broadcasted_iota(jnp.int32, (t, t), 1), p, 0.0)
            dv = dv + jnp.dot(p.astype(BF16), do_ref[rows(i), :].astype(BF16), preferred_element_type=F32)
            ds = (p * (dp_refs[slot][...] - delta_ref[i][0:1, :])).astype(BF16)
            dk = dk + jnp.dot(ds, q_ref[rows(i), :], preferred_element_type=F32)
            dq_ref[rows(i), :] += _dot_tn(ds, kv_) * MLA_SCALE
            return dk, dv

        issue(kj, 0)
        carry = consume(kj, 0, (jnp.zeros((t, MLA_QK), F32), jnp.zeros((t, MLA_V), F32)), masked=True)
        dk, dv = _pipelined_blocks(kj + 1, nq - 1 - kj, nq - 1, issue, consume, carry, False)
        dk_ref[...] = dk * LN2
        dv_ref[...] = dv

    tile = lambda w: pl.BlockSpec((None, t, w), lambda h, j: (h, j, 0))
    whole = lambda w: pl.BlockSpec((None, s, w), lambda h, j: (h, 0, 0))
    rows_spec = pl.BlockSpec((None, nq, 8, t), lambda h, j: (h, 0, 0, 0))
    outs = _pallas_call(
        body, name=name, grid=(nh, nq),
        in_specs=[whole(MLA_QK), tile(MLA_QK), tile(MLA_V), whole(MLA_V), rows_spec, rows_spec] + [_HBM] * ng,
        out_specs=[whole(MLA_QK), tile(MLA_QK), tile(MLA_V)] + [_HBM] * ng,
        out_shape=[jax.ShapeDtypeStruct((nh, s, MLA_QK), F32), jax.ShapeDtypeStruct((nh, s, MLA_QK), F32),
                   jax.ShapeDtypeStruct((nh, s, MLA_V), F32)] + (_scatter_out_shapes(scatter) if ng else []),
        scratch_shapes=[pltpu.VMEM((t, t), F32)] * 4 + (_exchange_scratch(3, ng) if ng else []),
        compiler_params=_params(dimension_semantics=("arbitrary", "arbitrary")),
    )(q, k, v, do, lse_row, delta_row, *(scatter or []))
    return outs[0], outs[1], outs[2], outs[3:]


def _swa_tile(s):
    return min(s, 8 * BLOCK)


def _swa_specs(tq):
    nb = tq // BLOCK
    grp = lambda w: pl.BlockSpec((SWA_GROUP, tq, w), lambda j, i: (j, i, 0))
    main = pl.BlockSpec((None, tq, SWA_HEAD_DIM), lambda j, i: (j, i, 0))
    tail = pl.BlockSpec((None, BLOCK, SWA_HEAD_DIM), lambda j, i: (j, nb * (i + 1), 0))
    sink = pl.BlockSpec((None, SWA_GROUP, 128), lambda j, i: (j, 0, 0))
    return grp, main, tail, sink


def _swa_band_mask(first):
    shape = (SWA_GROUP * BLOCK, 2 * BLOCK)
    q_rel = (lax.broadcasted_iota(jnp.int32, shape, 0) & (BLOCK - 1)) + BLOCK
    k_rel = lax.broadcasted_iota(jnp.int32, shape, 1)
    dist = q_rel - k_rel
    return (dist >= 0) & (dist < BLOCK) & ((k_rel >= BLOCK) | jnp.logical_not(first))


def _swa_sink_column(sink_ref):
    sk = sink_ref[...]
    return jnp.concatenate([jnp.broadcast_to(sk[g:g + 1, 0:1], (BLOCK, 1)) for g in range(SWA_GROUP)], axis=0)


def _swa_fwd(q, kpad, vpad, sinks, name):
    _, s, _ = q.shape
    tq = _swa_tile(s)
    grp, main, tail, sink = _swa_specs(tq)
    d = SWA_HEAD_DIM

    def body(q_ref, km_ref, kt_ref, vm_ref, vt_ref, sink_ref, o_ref, lse_ref):
        i = pl.program_id(1)
        kall = jnp.concatenate([km_ref[...], kt_ref[...]], axis=0)
        vall = jnp.concatenate([vm_ref[...], vt_ref[...]], axis=0)
        sink_col = _swa_sink_column(sink_ref)
        for b in range(tq // BLOCK):
            lo = b * BLOCK
            valid = _swa_band_mask(i == 0 if b == 0 else False)
            q4 = q_ref[:, lo:lo + BLOCK, :].reshape(SWA_GROUP * BLOCK, d)
            sc = jnp.where(valid, _dot_nt(q4, kall[lo:lo + 2 * BLOCK]) * SWA_SCALE, NEG)
            m = jnp.maximum(jnp.max(sc, axis=-1, keepdims=True), sink_col)
            e = jnp.exp(sc - m)
            den = jnp.sum(e, axis=-1, keepdims=True) + jnp.exp(sink_col - m)
            out = jnp.dot((e * (1.0 / den)).astype(BF16), vall[lo:lo + 2 * BLOCK], preferred_element_type=F32)
            o_ref[:, lo:lo + BLOCK, :] = out.reshape(SWA_GROUP, BLOCK, d)
            lse_ref[:, lo:lo + BLOCK, :] = (m + jnp.log(den)).reshape(SWA_GROUP, BLOCK, 1)

    return _pallas_call(
        body, name=name, grid=(SWA_KV_HEADS, s // tq),
        in_specs=[grp(d), main, tail, main, tail, sink], out_specs=[grp(d), grp(1)],
        out_shape=[jax.ShapeDtypeStruct((SWA_HEADS, s, d), F32), jax.ShapeDtypeStruct((SWA_HEADS, s, 1), F32)],
        compiler_params=_params(dimension_semantics=("arbitrary", "arbitrary")),
    )(q, kpad, kpad, vpad, vpad, sinks)


def _swa_bwd(q, kpad, vpad, sinks, o, lse, do, name):
    _, s, _ = q.shape
    tq = _swa_tile(s)
    grp, main, tail, sink = _swa_specs(tq)
    d = SWA_HEAD_DIM

    def body(q_ref, km_ref, kt_ref, vm_ref, vt_ref, sink_ref, o_ref, lse_ref, do_ref, dq_ref, dk_ref, dv_ref, dsink_ref):
        i = pl.program_id(1)
        kall = jnp.concatenate([km_ref[...], kt_ref[...]], axis=0)
        vall = jnp.concatenate([vm_ref[...], vt_ref[...]], axis=0)
        sink_col = _swa_sink_column(sink_ref)

        @pl.when(i == 0)
        def _():
            dk_ref[...] = jnp.zeros_like(dk_ref)
            dv_ref[...] = jnp.zeros_like(dv_ref)
            dsink_ref[...] = jnp.zeros_like(dsink_ref)

        dsink = jnp.zeros((SWA_GROUP * BLOCK, 1), F32)
        for b in range(tq // BLOCK):
            lo = b * BLOCK
            valid = _swa_band_mask(i == 0 if b == 0 else False)
            rows4 = SWA_GROUP * BLOCK
            q4 = q_ref[:, lo:lo + BLOCK, :].reshape(rows4, d)
            do4 = do_ref[:, lo:lo + BLOCK, :].reshape(rows4, d)
            lse4 = lse_ref[:, lo:lo + BLOCK, :].reshape(rows4, 1)
            delta = jnp.sum(do4 * o_ref[:, lo:lo + BLOCK, :].reshape(rows4, d), axis=-1, keepdims=True)
            kb, vb = kall[lo:lo + 2 * BLOCK], vall[lo:lo + 2 * BLOCK]
            do4b = do4.astype(BF16)
            p = jnp.where(valid, jnp.exp(_dot_nt(q4, kb) * SWA_SCALE - lse4), 0.0)
            ds = (p * (_dot_nt(do4b, vb) - delta) * SWA_SCALE).astype(BF16)
            dq_ref[:, lo:lo + BLOCK, :] = jnp.dot(ds, kb, preferred_element_type=F32).reshape(SWA_GROUP, BLOCK, d)
            band = pl.ds(pl.multiple_of(i * tq, BLOCK) + lo, 2 * BLOCK)
            dk_ref[band, :] += _dot_tn(ds, q4)
            dv_ref[band, :] += _dot_tn(p.astype(BF16), do4b)
            dsink = dsink - jnp.exp(sink_col - lse4) * delta
        per_head = [jnp.broadcast_to(jnp.sum(dsink[g * BLOCK:(g + 1) * BLOCK], axis=0, keepdims=True), (1, 128))
                    for g in range(SWA_GROUP)]
        dsink_ref[...] += jnp.concatenate(per_head + [jnp.zeros((8 - SWA_GROUP, 128), F32)], axis=0)

    acc = pl.BlockSpec((None, s + BLOCK, d), lambda j, i: (j, 0, 0))
    return _pallas_call(
        body, name=name, grid=(SWA_KV_HEADS, s // tq),
        in_specs=[grp(d), main, tail, main, tail, sink, grp(d), grp(1), grp(d)],
        out_specs=[grp(d), acc, acc, pl.BlockSpec((None, 8, 128), lambda j, i: (j, 0, 0))],
        out_shape=[jax.ShapeDtypeStruct((SWA_HEADS, s, d), F32),
                   jax.ShapeDtypeStruct((SWA_KV_HEADS, s + BLOCK, d), F32),
                   jax.ShapeDtypeStruct((SWA_KV_HEADS, s + BLOCK, d), F32),
                   jax.ShapeDtypeStruct((SWA_KV_HEADS, 8, 128), F32)],
        compiler_params=_params(dimension_semantics=("arbitrary", "arbitrary")),
    )(q, kpad, kpad, vpad, vpad, sinks, o, lse, do)


def _loss_head(y, target, name):
    s = y.shape[0]
    tm = _row_tile(s, 512)

    def body(y_ref, t_ref, dy_ref, loss_ref):
        i = pl.program_id(0)
        err = y_ref[...] - t_ref[...]
        dy_ref[...] = err * (1.0 / D_MODEL)
        part = jnp.broadcast_to(0.5 * jnp.sum(jnp.mean(err * err, axis=-1, keepdims=True), axis=0, keepdims=True), (1, 128))

        @pl.when(i == 0)
        def _():
            loss_ref[...] = part

        @pl.when(i > 0)
        def _():
            loss_ref[...] += part

    row = pl.BlockSpec((tm, D_MODEL), lambda i: (i, 0))
    return _pallas_call(
        body, name=name, grid=(s // tm,), in_specs=[row, row], out_specs=[row, _full((1, 128))],
        out_shape=[jax.ShapeDtypeStruct((s, D_MODEL), F32), jax.ShapeDtypeStruct((1, 128), F32)],
        compiler_params=_params(dimension_semantics=("arbitrary",)),
    )(y, target)


def _adamw(w, g, m, v, name):
    rows, cols = w.shape
    tr = rows
    for cand in (512, 256, 128, 64, 32, 16, 8):
        if rows % cand == 0 and rows > cand:
            tr = cand
            break

    def body(w_ref, g_ref, m_ref, v_ref, d_ref, nm_ref, nv_ref):
        gv = g_ref[...]
        nm = ADAM_B1 * m_ref[...] + (1.0 - ADAM_B1) * gv
        nv = ADAM_B2 * v_ref[...] + (1.0 - ADAM_B2) * (gv * gv)
        m_hat = nm / (1.0 - ADAM_B1 ** ADAM_STEP)
        v_hat = nv / (1.0 - ADAM_B2 ** ADAM_STEP)
        d_ref[...] = -ADAM_LR * (m_hat / (jnp.sqrt(v_hat) + ADAM_EPS) + ADAM_WD * w_ref[...])
        nm_ref[...] = nm
        nv_ref[...] = nv

    blk = pl.BlockSpec((tr, cols), lambda i: (i, 0))
    return _pallas_call(
        body, name=name, grid=(rows // tr,), in_specs=[blk] * 4, out_specs=[blk] * 3,
        out_shape=[jax.ShapeDtypeStruct((rows, cols), F32)] * 3,
        compiler_params=_params(dimension_semantics=("arbitrary",)),
    )(w, g, m, v)


def _position():
    return lax.axis_index("x"), lax.axis_index("y"), lax.axis_index("c")


def _remote(src, dst, send_sems, recv_sems, k, to):
    return pltpu.make_async_remote_copy(src_ref=src, dst_ref=dst, send_sem=send_sems.at[k], recv_sem=recv_sems.at[k],
                                        device_id=to, device_id_type=MESH)


_HBM = pl.BlockSpec(memory_space=pltpu.HBM)


def _gather_stages(ins, outs, send_sems, recv_sems):
    na = len(ins)
    x, y, c = _position()
    me, sibling = (x, y, c), (x, y, 1 - c)
    xn, yn, dg = (1 - x, y), (x, 1 - y), (1 - x, 1 - y)

    def slot(a, chip, pc, half=None):
        ref = outs[a].at[4 * chip[0] + 2 * chip[1] + pc]
        if half is None:
            return ref
        rows = ref.shape[0] // 2
        return ref.at[pl.ds(half * rows, rows)]

    def cp(a, k, chip, pc, half, to, src=None):
        dst = slot(a, chip, pc, half)
        return _remote(dst if src is None else src, dst, send_sems, recv_sems, 8 * a + k, to)

    first_hop = [(0, xn), (1, yn)]
    second_hop = [(0, xn, 2, 0, yn), (1, yn, 3, 1, xn)]

    def sends():
        out = []
        for a in range(na):
            out += [cp(a, k, (x, y), c, None, (*to, c), src=ins[a].at[c]) for k, to in first_hop]
            out += [cp(a, fwd_k, frm, c, half, (*to, c)) for _, frm, fwd_k, half, to in second_hop]
            out += [cp(a, 4 + k, frm, c, None, sibling) for k, frm in first_hop]
            out += [cp(a, 6 + half, dg, c, half, sibling) for half in (0, 1)]
        return out

    def stage0():
        for a in range(na):
            for k, to in first_hop:
                cp(a, k, (x, y), c, None, (*to, c), src=ins[a].at[c]).start()

    def stage1():
        for k, frm, fwd_k, half, to in second_hop:
            for a in range(na):
                cp(a, k, frm, c, None, me).wait_recv()
                cp(a, fwd_k, frm, c, half, (*to, c)).start()
                cp(a, 4 + k, frm, c, None, sibling).start()

    def stage2():
        for half in (0, 1):
            for a in range(na):
                cp(a, 2 + half, dg, c, half, me).wait_recv()
                cp(a, 6 + half, dg, c, half, sibling).start()

    def stage3():
        for a in range(na):
            for k, chip, half in ((4, xn, None), (5, yn, None), (6, dg, 0), (7, dg, 1)):
                cp(a, k, chip, 1 - c, half, me).wait_recv()
        for sent in sends():
            sent.wait_send()

    return [stage0, stage1, stage2, stage3]


def _gather_out_shapes(mine):
    return [jax.ShapeDtypeStruct((N_DEV,) + m.shape[1:], m.dtype) for m in mine]


def _exchange_scratch(per_array, na):
    return [pltpu.SemaphoreType.DMA((per_array * na,)), pltpu.SemaphoreType.DMA((per_array * na,))]


def _all_gather_halves(mine, name):
    na = len(mine)

    def body(*refs):
        for stage in _gather_stages(refs[:na], refs[na:2 * na], *refs[2 * na:]):
            stage()

    return _pallas_call(body, name=name, in_specs=[_HBM] * na, out_specs=[_HBM] * na, out_shape=_gather_out_shapes(mine),
                        scratch_shapes=_exchange_scratch(8, na))(*mine)


def _sibling_exchange(parts, name, other_half):
    na = len(parts)

    def body(*refs):
        ins, outs, (send_sems, recv_sems) = refs[:na], refs[na:2 * na], refs[2 * na:]
        x, y, c = _position()
        copies = [_remote(ins[a].at[:, 1 - c] if other_half else ins[a], outs[a], send_sems, recv_sems, a, (x, y, 1 - c))
                  for a in range(na)]
        for cp in copies:
            cp.start()
        for cp in copies:
            cp.wait()

    return _pallas_call(
        body, name=name, in_specs=[_HBM] * na, out_specs=[_HBM] * na,
        out_shape=[jax.ShapeDtypeStruct(p.shape[:1] + p.shape[2:] if other_half else p.shape, p.dtype) for p in parts],
        scratch_shapes=_exchange_scratch(1, na),
    )(*parts)


def _scatter_stages(ins, outs, send_sems, recv_sems):
    na = len(ins)
    x, y, c = _position()
    chips = [(1 - x, y), (x, 1 - y), (1 - x, 1 - y)]

    def copies():
        return [_remote(ins[a].at[2 * px + py], outs[a].at[j], send_sems, recv_sems, 3 * a + j, (px, py, c))
                for a in range(na) for j, (px, py) in enumerate(chips)]

    def start():
        for cp in copies():
            cp.start()

    def wait():
        for cp in copies():
            cp.wait()

    return [start, wait]


def _scatter_out_shapes(parts):
    return [jax.ShapeDtypeStruct((3,) + p.shape[1:], p.dtype) for p in parts]


def _scatter_to_chips(parts, name):
    na = len(parts)

    def body(*refs):
        for stage in _scatter_stages(refs[:na], refs[na:2 * na], *refs[2 * na:]):
            stage()

    return _pallas_call(body, name=name, in_specs=[_HBM] * na, out_specs=[_HBM] * na, out_shape=_scatter_out_shapes(parts),
                        scratch_shapes=_exchange_scratch(3, na))(*parts)


def _assemble(g4, mine, name, side_by_side, order=None):
    _, nl, r, w = g4.shape
    tr = _divisor_tile(r, 640)

    def body(in_ref, mine_ref, out_ref):
        chip = 2 * lax.axis_index("x") + lax.axis_index("y")
        blocks = [jnp.where(chip == sh, mine_ref[...], in_ref[sh]) for sh in range(N_SHARD)]
        if side_by_side:
            full = jnp.concatenate(blocks, axis=-1)
            out_ref[...] = full if order is None else _take_cols(full, order)
        else:
            for sh in range(N_SHARD):
                out_ref[sh] = blocks[sh]

    if side_by_side:
        out_spec = pl.BlockSpec((None, tr, N_SHARD * w), lambda l, i: (l, i, 0))
        out_shape = jax.ShapeDtypeStruct((nl, r, N_SHARD * w), g4.dtype)
    else:
        out_spec = pl.BlockSpec((None, N_SHARD, tr, w), lambda l, i: (l, 0, i, 0))
        out_shape = jax.ShapeDtypeStruct((nl, N_SHARD, r, w), g4.dtype)
    return _pallas_call(
        body, name=name, grid=(nl, r // tr),
        in_specs=[pl.BlockSpec((N_SHARD, None, tr, w), lambda l, i: (0, l, i, 0)),
                  pl.BlockSpec((None, tr, w), lambda l, i: (l, i, 0))],
        out_specs=out_spec, out_shape=out_shape,
        compiler_params=_params(dimension_semantics=("arbitrary", "arbitrary")),
    )(g4, mine)


def _all_reduce_small(vec, name):
    r, l = vec.shape

    def body(v_ref, out_ref, gath_ref, send_sems, recv_sems):
        x, y, c = _position()
        me = 4 * x + 2 * y + c
        gath_ref[me] = v_ref[...]
        copies = []
        for k in range(1, N_DEV):
            to = (x ^ (k >> 2), y ^ ((k >> 1) & 1), c ^ (k & 1))
            copies.append(_remote(gath_ref.at[me], gath_ref.at[me], send_sems, recv_sems, k - 1, to))
        for cp in copies:
            cp.start()
        for k in range(1, N_DEV):
            frm = 4 * (x ^ (k >> 2)) + 2 * (y ^ ((k >> 1) & 1)) + (c ^ (k & 1))
            _remote(gath_ref.at[frm], gath_ref.at[frm], send_sems, recv_sems, k - 1, (x, y, c)).wait_recv()
        for cp in copies:
            cp.wait_send()
        total = gath_ref[0]
        for d in range(1, N_DEV):
            total = total + gath_ref[d]
        out_ref[...] = total

    vm = pl.BlockSpec(memory_space=pltpu.VMEM)
    return _pallas_call(
        body, name=name, in_specs=[vm], out_specs=vm, out_shape=jax.ShapeDtypeStruct((r, l), F32),
        scratch_shapes=[pltpu.VMEM((N_DEV, r, l), F32), pltpu.SemaphoreType.DMA((N_DEV - 1,)),
                        pltpu.SemaphoreType.DMA((N_DEV - 1,))],
    )(vec)


def _sum_blocks(blocks, out_dtype, name):
    m, w = blocks[0].shape
    tr = _divisor_tile(m, 1024)

    def body(*refs):
        total = refs[0][...].astype(F32)
        for ref in refs[1:-1]:
            total = total + ref[...].astype(F32)
        refs[-1][...] = total.astype(out_dtype)

    blk = pl.BlockSpec((tr, w), lambda i: (i, 0))
    return _pallas_call(
        body, name=name, grid=(m // tr,), in_specs=[blk] * len(blocks), out_specs=blk,
        out_shape=jax.ShapeDtypeStruct((m, w), out_dtype),
        compiler_params=_params(dimension_semantics=("arbitrary",)),
    )(*blocks)


FIRST_GROUPS = (('ffn1_w_gate', 'ffn1_w_up', 'ffn1_w_down'),)
REST_GROUPS = (('ffn2_w_gate', 'ffn2_w_up', 'ffn2_w_down'), ('w_o',), ('w_in',), ('mla_w_q_b',), ('mla_w_kv_b',))
N_FIRST = len(FIRST_GROUPS)


def _shard_rows(name):
    shape, axis = BIG[name]
    return shape[0] // N_SHARD if axis == 0 else shape[0]


def _group_row_offsets(group):
    return [int(v) for v in np.cumsum([0] + [_shard_rows(n) for n in group])]


def _rope_tables(s):
    pos = jnp.arange(s, dtype=F32)
    inv = 1.0 / (ROPE_THETA ** (jnp.arange(0, MLA_ROPE, 2, dtype=F32) / MLA_ROPE))
    ang = pos[:, None] * inv[None, :]
    return jnp.tile(jnp.cos(ang), (1, SWA_HEADS)), jnp.tile(jnp.sin(ang), (1, SWA_HEADS))


_MIXER_GAINS = ('mix_norm', 'mla_q_a_norm', 'mla_kv_a_norm', 'mla_q_norm', 'mla_k_norm', 'swa_q_norm', 'swa_k_norm')


def _local_step(x, target, small, ex):
    s = x.shape[0]
    cos, sin = _rope_tables(s)
    row = lambda name, l: small[name][l][None, :]
    saved, bigs = [], []
    for l in range(DEPTH):
        big = ex.first_weights(l)
        sv = {'x0': x}
        x, sv['g1'], sv['u1'], got = _ffn_fwd(x, row('ffn1_norm', l), big['ffn1_w_gate'], big['ffn1_w_up'], big['ffn1_w_down'],
                                             f"ffn1_fwd_{l}", ex.gather_behind_ffn1(l))
        big.update(ex.rest_weights(l, got))
        bigs.append(big)
        sv['x1'] = x
        gains = [row(n, l) for n in _MIXER_GAINS]
        mixer_w = (big['w_in'], big['mla_w_q_b'], big['mla_w_kv_b'])
        q_a, k_a, v_a, q_b, k_b, v_b = _pre_fwd(x, gains, *mixer_w, cos, sin, f"pre_fwd_{l}")
        o_a, lse, got = _mla_fwd(q_a, k_a, v_a, f"mla_fwd_{l}", ex.gather_behind_mla(l))
        ex.gathered_behind_mla(l, got)
        kpad = jnp.pad(k_b, ((0, 0), (BLOCK, 0), (0, 0)))
        vpad = jnp.pad(v_b, ((0, 0), (BLOCK, 0), (0, 0)))
        sinks = jnp.broadcast_to(small['swa_sinks'][l].reshape(SWA_KV_HEADS, SWA_GROUP, 1), (SWA_KV_HEADS, SWA_GROUP, 128))
        o_b, lse_b = _swa_fwd(q_b, kpad, vpad, sinks, f"swa_fwd_{l}")
        sv.update(gains=gains, mixer_w=mixer_w, q_a=q_a, k_a=k_a, v_a=v_a, q_b=q_b, kpad=kpad, vpad=vpad, sinks=sinks,
                  o_a=o_a, lse=lse, o_b=o_b, lse_b=lse_b)
        x = _post_fwd(x, o_a, o_b, row('mla_out_norm', l), row('swa_out_norm', l), big['w_o'], f"post_fwd_{l}")
        sv['x2'] = x
        x, sv['g2'], sv['u2'], _ = _ffn_fwd(x, row('ffn2_norm', l), big['ffn2_w_gate'], big['ffn2_w_up'], big['ffn2_w_down'],
                                           f"ffn2_fwd_{l}")
        saved.append(sv)

    dx, loss = _loss_head(x, target, "loss_head")

    gs = {n: [None] * DEPTH for n in SMALL_NAMES}
    t = _attn_tile(s)
    for l in reversed(range(DEPTH)):
        sv, big = saved[l], bigs[l]

        def ffn_back(tag, xin, dy, gate, up, scatter=None):
            (dxi, dgain, nb, act, dgate, dup), got = _ffn_bwd(xin, dy, gate, up, row(tag + '_norm', l), big[tag + '_w_gate'],
                                                             big[tag + '_w_up'], big[tag + '_w_down'], f"{tag}_bwd_{l}", scatter)
            gs[tag + '_norm'][l] = dgain[0]
            group = _matmul_tn(dgate, nb, 1.0, f"{tag}_dwg_{l}", None, 0, 3)
            group = _matmul_tn(dup, nb, 1.0, f"{tag}_dwu_{l}", group, 1, 3)
            group = _matmul_tn(act, dy, 0.5, f"{tag}_dwd_{l}", group, 2, 3)
            return dxi, {(tag + '_w_gate', tag + '_w_up', tag + '_w_down'): group}, got

        dx, rest_grads, _ = ffn_back('ffn2', sv['x2'], dx, sv['g2'], sv['u2'])
        do_a, do_b, dga, dgb, dwo = _post_bwd(dx, sv['o_a'], sv['o_b'], row('mla_out_norm', l), row('swa_out_norm', l),
                                              big['w_o'], f"post_bwd_{l}")
        gs['mla_out_norm'][l], gs['swa_out_norm'][l] = dga[0], dgb[0]
        rest_grads['w_o'] = dwo.reshape(N_SHARD, MIX_WIDTH // N_SHARD, D_MODEL)
        delta = _mla_delta(sv['o_a'], do_a, f"mla_delta_{l}")
        dq_a, dk_a, dv_a, got = _mla_bwd(sv['q_a'], sv['k_a'], sv['v_a'], do_a, sv['lse'], delta, f"mla_bwd_{l}",
                                         ex.scatter_behind_mla(l))
        ex.scattered_behind_mla(l, got)
        dq_b, dkpad, dvpad, dsink = _swa_bwd(sv['q_b'], sv['kpad'], sv['vpad'], sv['sinks'], sv['o_b'], sv['lse_b'], do_b,
                                             f"swa_bwd_{l}")
        gs['swa_sinks'][l] = dsink[:, :SWA_GROUP, 0].reshape(SWA_HEADS)
        cts = [dq_a, dk_a, dv_a, dq_b, dkpad[:, BLOCK:], dvpad[:, BLOCK:]]
        outs = _pre_bwd(sv['x1'], dx, cts, sv['gains'], *sv['mixer_w'], cos, sin, f"pre_bwd_{l}")
        dx = outs[0]
        for n, val in zip(_MIXER_GAINS, outs[1:8]):
            gs[n][l] = val[0]
        rest_grads['w_in'], rest_grads['mla_w_q_b'], rest_grads['mla_w_kv_b'] = outs[8:11]
        ex.grads_ready(l, 'rest', rest_grads)
        dx, first_grads, got = ffn_back('ffn1', sv['x0'], dx, sv['g1'], sv['u1'], ex.scatter_behind_ffn1(l))
        ex.scattered_behind_ffn1(l, got)
        ex.grads_ready(l, 'first', first_grads)
    return loss, dx, gs


class _Exchange:
    def __init__(self, weights, c, chip):
        halves_of = lambda a: a.reshape(a.shape[:-2] + (2, a.shape[-2] // 2, a.shape[-1]))
        self.halves_of, self.c, self.chip = halves_of, c, chip
        self.mine = [[halves_of(jnp.concatenate([weights[n][l].astype(BF16) for n in group], axis=0))
                      for group in FIRST_GROUPS + REST_GROUPS] for l in range(DEPTH)]
        self.ahead, self.begun, self.received = {}, {}, {}

    def _assembled(self, l, which, gathered):
        groups, base = (FIRST_GROUPS, 0) if which == 'first' else (REST_GROUPS, N_FIRST)
        big = {}
        for gi, group in enumerate(groups):
            offs = _group_row_offsets(group)
            _, rh, w = self.mine[l][base + gi].shape
            col_sharded = BIG[group[0]][1] == 1
            full = _assemble(gathered[gi].reshape(N_SHARD, 1, 2 * rh, w), self.mine[l][base + gi].reshape(1, 2 * rh, w),
                             f"assemble_{which}{gi}_{l}", col_sharded, MIXER_ORDERS.get(group[0]))
            for i, n in enumerate(group):
                rows = offs[i + 1] - offs[i]
                if len(group) > 1 and offs[i] % rows == 0:
                    big[n] = ((full, (None, rows, N_SHARD * w), (0, offs[i] // rows, 0)) if col_sharded else
                              (full, (None, N_SHARD, rows, w), (0, 0, offs[i] // rows, 0)))
                elif col_sharded:
                    big[n] = full[0, offs[i]:offs[i + 1]]
                else:
                    big[n] = full[0, :, offs[i]:offs[i + 1]].reshape(BIG[n][0])
        return big

    def first_weights(self, l):
        got = self.ahead[l][:N_FIRST] if l in self.ahead else _all_gather_halves(self.mine[l][:N_FIRST], f"gather_first_{l}")
        return self._assembled(l, 'first', got)

    def gather_behind_ffn1(self, l):
        return None if l in self.ahead else self.mine[l][N_FIRST:]

    def rest_weights(self, l, got):
        return self._assembled(l, 'rest', self.ahead[l][N_FIRST:] if l in self.ahead else got)

    def gather_behind_mla(self, l):
        return self.mine[l + 1] if l + 1 < DEPTH else None

    def gathered_behind_mla(self, l, got):
        if got:
            self.ahead[l + 1] = got

    def grads_ready(self, l, which, grads):
        groups = FIRST_GROUPS if which == 'first' else REST_GROUPS
        parts = [self.halves_of(grads[group] if group in grads else jnp.concatenate([grads[n] for n in group], axis=1))
                 for group in groups]
        from_sibling = _sibling_exchange(parts, f"swap_{which}_{l}", True)
        chip_sums = []
        for gi, (p, got) in enumerate(zip(parts, from_sibling)):
            kept = lax.dynamic_index_in_dim(p, self.c, axis=1, keepdims=False)
            rows = N_SHARD * p.shape[2]
            pair = _sum_blocks([kept.reshape(rows, -1), got.reshape(rows, -1)], BF16, f"sum_pair_{which}{gi}_{l}")
            chip_sums.append(pair.reshape(got.shape))
        self.begun[(l, which)] = chip_sums

    def scatter_behind_mla(self, l):
        return self.begun[(l + 1, 'first')] + self.begun[(l + 1, 'rest')] if l + 1 < DEPTH else None

    def scattered_behind_mla(self, l, got):
        if got:
            self.received[(l + 1, 'first')], self.received[(l + 1, 'rest')] = got[:N_FIRST], got[N_FIRST:]

    def scatter_behind_ffn1(self, l):
        return self.begun[(l, 'rest')] if l == 0 else None

    def scattered_behind_ffn1(self, l, got):
        if got:
            self.received[(l, 'rest')] = got

    def reduced(self):
        keys = sorted(self.begun)
        for key in keys:
            if key not in self.received:
                self.received[key] = _scatter_to_chips(self.begun[key], f"scatter_{key[1]}_{key[0]}")
        halves = []
        for l, which in keys:
            for gi, (cs, got) in enumerate(zip(self.begun[(l, which)], self.received[(l, which)])):
                own = lax.dynamic_index_in_dim(cs, self.chip, axis=0, keepdims=False)
                halves.append(_sum_blocks([own, got[0], got[1], got[2]], F32, f"sum_chips_{which}{gi}_{l}"))
        others = _sibling_exchange(halves, "share_halves", False)
        per_layer, at = {}, 0
        for l, which in keys:
            for group in (FIRST_GROUPS if which == 'first' else REST_GROUPS):
                mine_h, other_h = halves[at], others[at]
                at += 1
                full = jnp.where(self.c == 0, jnp.concatenate([mine_h, other_h]), jnp.concatenate([other_h, mine_h]))
                offs = _group_row_offsets(group)
                for i, n in enumerate(group):
                    per_layer[(n, l)] = full[offs[i]:offs[i + 1]]
        return {n: jnp.stack([per_layer[(n, l)] for l in range(DEPTH)]) for n in BIG_NAMES}


def kernel(x, ffn1_norm, ffn1_w_gate, ffn1_w_up, ffn1_w_down, mix_norm, w_in, mla_q_a_norm, mla_w_q_b, mla_kv_a_norm, mla_w_kv_b, mla_q_norm, mla_k_norm, swa_q_norm, swa_k_norm, swa_sinks, mla_out_norm, swa_out_norm, w_o, ffn2_norm, ffn2_w_gate, ffn2_w_up, ffn2_w_down, loss_target, m_ffn1_norm, m_ffn1_w_gate, m_ffn1_w_up, m_ffn1_w_down, m_mix_norm, m_w_in, m_mla_q_a_norm, m_mla_w_q_b, m_mla_kv_a_norm, m_mla_w_kv_b, m_mla_q_norm, m_mla_k_norm, m_swa_q_norm, m_swa_k_norm, m_swa_sinks, m_mla_out_norm, m_swa_out_norm, m_w_o, m_ffn2_norm, m_ffn2_w_gate, m_ffn2_w_up, m_ffn2_w_down, v_ffn1_norm, v_ffn1_w_gate, v_ffn1_w_up, v_ffn1_w_down, v_mix_norm, v_w_in, v_mla_q_a_norm, v_mla_w_q_b, v_mla_kv_a_norm, v_mla_w_kv_b, v_mla_q_norm, v_mla_k_norm, v_swa_q_norm, v_swa_k_norm, v_swa_sinks, v_mla_out_norm, v_swa_out_norm, v_w_o, v_ffn2_norm, v_ffn2_w_gate, v_ffn2_w_up, v_ffn2_w_down):
    args = dict(locals())
    transposed = lambda a: jnp.swapaxes(a, 1, 2)
    as_kernels_see = lambda n, a: transposed(a) if n in TRANSPOSED else a
    weights = {n: as_kernels_see(n, args[n]) for n in WEIGHT_NAMES}
    mom_m = {n: as_kernels_see(n, args["m_" + n]) for n in WEIGHT_NAMES}
    mom_v = {n: as_kernels_see(n, args["v_" + n]) for n in WEIGHT_NAMES}
    ex = _Exchange(weights, lax.axis_index("c"), 2 * lax.axis_index("x") + lax.axis_index("y"))
    loss, dx, gs = _local_step(x[0], loss_target[0], {n: weights[n] for n in SMALL_NAMES}, ex)

    small_flat = jnp.concatenate([jnp.stack(gs[n]).reshape(-1) for n in SMALL_NAMES] + [loss[0, :1]])
    n_small = small_flat.shape[0]
    lanes = -(-n_small // (8 * 128)) * 128
    small_sum = _all_reduce_small(jnp.pad(small_flat, (0, 8 * lanes - n_small)).reshape(8, lanes), "reduce_small").reshape(-1)
    loss_out = small_sum[n_small - 1]
    grads = ex.reduced()

    packed = lambda d: jnp.pad(jnp.concatenate([d[n].reshape(-1) for n in SMALL_NAMES]), (0, 8 * lanes - n_small + 1)).reshape(8, lanes)
    small_out = _adamw(packed(weights), small_sum.reshape(8, lanes), packed(mom_m), packed(mom_v), "adamw_small")
    deltas, new_m, new_v = {}, {}, {}
    off = 0
    for n in SMALL_NAMES:
        cnt = int(np.prod(weights[n].shape))
        grads[n], deltas[n], new_m[n], new_v[n] = (a.reshape(-1)[off:off + cnt].reshape(weights[n].shape)
                                                    for a in (small_sum, *small_out))
        off += cnt
    for n in BIG_NAMES:
        shp = weights[n].shape
        two_d = (shp[0] * shp[1], shp[2])
        d, nm, nv = _adamw(weights[n].reshape(two_d), grads[n].reshape(two_d), mom_m[n].reshape(two_d),
                           mom_v[n].reshape(two_d), f"adamw_{n}")
        deltas[n], new_m[n], new_v[n] = d.reshape(shp), nm.reshape(shp), nv.reshape(shp)
    for n in TRANSPOSED:
        grads[n], deltas[n], new_m[n], new_v[n] = (transposed(a) for a in (grads[n], deltas[n], new_m[n], new_v[n]))

    return (loss_out, dx[None], *[grads[n] for n in WEIGHT_NAMES], *[deltas[n] for n in WEIGHT_NAMES],
            *[new_m[n] for n in WEIGHT_NAMES], *[new_v[n] for n in WEIGHT_NAMES])
```

```python
import functools

import numpy as np
import jax
import jax.numpy as jnp
from jax import lax
from jax.experimental import pallas as pl
from jax.experimental.pallas import tpu as pltpu

F32 = jnp.float32
BF16 = jnp.bfloat16

D_MODEL = 1024
DEPTH = 2
EPS = 1e-6
ROPE_THETA = 10000.0
BLOCK = 128
MLA_HEADS = 4
MLA_Q_RANK = 256
MLA_KV_RANK = 128
MLA_NOPE = 128
MLA_ROPE = 64
MLA_V = 128
MLA_QK = MLA_NOPE + MLA_ROPE
MLA_WIDTH = MLA_HEADS * MLA_V
SWA_HEADS = 8
SWA_KV_HEADS = 2
SWA_GROUP = SWA_HEADS // SWA_KV_HEADS
SWA_HEAD_DIM = 64
SWA_WIDTH = SWA_HEADS * SWA_HEAD_DIM
MIX_WIDTH = MLA_WIDTH + SWA_WIDTH
IN_SPLITS = (MLA_Q_RANK, MLA_KV_RANK, MLA_ROPE, SWA_WIDTH, SWA_KV_HEADS * SWA_HEAD_DIM, SWA_KV_HEADS * SWA_HEAD_DIM)
IN_COLS = sum(IN_SPLITS)
IN_OFFS = tuple(int(v) for v in np.cumsum((0,) + IN_SPLITS))
D_FF = 2816
MLA_SCALE = MLA_QK ** -0.5
LOG2E = 1.4426950408889634
LN2 = 0.6931471805599453
MLA_QSCALE = MLA_SCALE * LOG2E
SWA_SCALE = SWA_HEAD_DIM ** -0.5
NEG = -1e30

ADAM_LR = 0.001
ADAM_B1 = 0.9
ADAM_B2 = 0.999
ADAM_EPS = 1e-08
ADAM_WD = 0.01
ADAM_STEP = 10

N_SHARD = 4
N_DEV = 8
VMEM_LIMIT = 56 * 1024 * 1024
MESH = pl.DeviceIdType.MESH

WEIGHT_NAMES = ['ffn1_norm', 'ffn1_w_gate', 'ffn1_w_up', 'ffn1_w_down', 'mix_norm', 'w_in', 'mla_q_a_norm', 'mla_w_q_b',
                'mla_kv_a_norm', 'mla_w_kv_b', 'mla_q_norm', 'mla_k_norm', 'swa_q_norm', 'swa_k_norm', 'swa_sinks',
                'mla_out_norm', 'swa_out_norm', 'w_o', 'ffn2_norm', 'ffn2_w_gate', 'ffn2_w_up', 'ffn2_w_down']
TRANSPOSED = ('ffn1_w_gate', 'ffn1_w_up', 'ffn2_w_gate', 'ffn2_w_up')
BIG = {'ffn1_w_gate': ((D_FF, D_MODEL), 0), 'ffn1_w_up': ((D_FF, D_MODEL), 0), 'ffn1_w_down': ((D_FF, D_MODEL), 0),
       'w_in': ((D_MODEL, IN_COLS), 1), 'mla_w_q_b': ((MLA_Q_RANK, MLA_HEADS * MLA_QK), 1),
       'mla_w_kv_b': ((MLA_KV_RANK, MLA_HEADS * (MLA_NOPE + MLA_V)), 1), 'w_o': ((MIX_WIDTH, D_MODEL), 0),
       'ffn2_w_gate': ((D_FF, D_MODEL), 0), 'ffn2_w_up': ((D_FF, D_MODEL), 0), 'ffn2_w_down': ((D_FF, D_MODEL), 0)}
BIG_NAMES = [n for n in WEIGHT_NAMES if n in BIG]
SMALL_NAMES = [n for n in WEIGHT_NAMES if n not in BIG]

_pallas_call = pl.pallas_call


def _params(**kw):
    return pltpu.CompilerParams(vmem_limit_bytes=VMEM_LIMIT, **kw)


def _full(shape):
    n = len(shape)
    return pl.BlockSpec(shape, lambda *_: (0,) * n)


def _resident(shape):
    n = len(shape)
    return pl.BlockSpec(shape, lambda *_: (0,) * n, pipeline_mode=pl.Buffered(1))


@jax.custom_vjp
def _mm(a, w):
    return jnp.dot(a.astype(BF16), w, preferred_element_type=F32)


def _mm_fwd(a, w):
    return _mm(a, w), w


def _mm_bwd(w, dy):
    return lax.dot_general(dy.astype(BF16), w, (((1,), (1,)), ((), ())), preferred_element_type=F32), None


_mm.defvjp(_mm_fwd, _mm_bwd)


def _dot_nt(a, b):
    return lax.dot_general(a, b, (((1,), (1,)), ((), ())), preferred_element_type=F32)


def _dot_tn(a, b):
    return lax.dot_general(a, b, (((0,), (0,)), ((), ())), preferred_element_type=F32)


def _rms(t, g):
    return t * lax.rsqrt(jnp.mean(t * t, axis=-1, keepdims=True) + EPS) * g


def _sigmoid(z):
    return 1.0 / (1.0 + jnp.exp(-z))


def _row_tile(s, want):
    return min(want, s)


def _divisor_tile(rows, cap):
    return max(d for d in range(16, min(rows, cap) + 1, 16) if rows % d == 0)


FF_CHUNK = 1408


def _weight_operand(w):
    if isinstance(w, tuple):
        arr, block, index = w
        return arr, pl.BlockSpec(block, lambda *_: index, pipeline_mode=pl.Buffered(1))
    return w, _resident(w.shape)


def _weight_rows(ref, start, n):
    if len(ref.shape) == 2:
        return ref[start:start + n, :]
    per = ref.shape[1]
    return ref[start // per:(start + n) // per].reshape(n, ref.shape[2])


def _ffn_fwd(x, g, wg, wu, wd, name, gather=None):
    s = x.shape[0]
    tm = _row_tile(s, 256)
    steps = s // tm
    ng = len(gather) if gather else 0
    (wg, wg_spec), (wu, wu_spec), (wd, wd_spec) = _weight_operand(wg), _weight_operand(wu), _weight_operand(wd)

    def body(x_ref, g_ref, wg_ref, wu_ref, wd_ref, *rest):
        g_ins, (y_ref, gate_ref, up_ref), g_outs, sems = rest[:ng], rest[ng:ng + 3], rest[ng + 3:2 * ng + 3], rest[2 * ng + 3:]
        if ng:
            _run_stages_at(_gather_stages(g_ins, g_outs, *sems), [(0,), (steps * 3 // 8,), (steps * 11 // 16,), (steps - 1,)])
        xv = x_ref[...]
        nb = _rms(xv, g_ref[...]).astype(BF16)
        acc = xv
        for c in range(0, D_FF, FF_CHUNK):
            gate = _dot_nt(nb, _weight_rows(wg_ref, c, FF_CHUNK))
            up = _dot_nt(nb, _weight_rows(wu_ref, c, FF_CHUNK))
            gate_ref[:, c:c + FF_CHUNK] = gate.astype(BF16)
            up_ref[:, c:c + FF_CHUNK] = up.astype(BF16)
            act = (gate * _sigmoid(gate) * up).astype(BF16)
            acc = acc + 0.5 * jnp.dot(act, _weight_rows(wd_ref, c, FF_CHUNK), preferred_element_type=F32)
        y_ref[...] = acc

    outs = _pallas_call(
        body, name=name, grid=(steps,),
        in_specs=[pl.BlockSpec((tm, D_MODEL), lambda i: (i, 0)), _full((1, D_MODEL)), wg_spec, wu_spec, wd_spec] + [_HBM] * ng,
        out_specs=[pl.BlockSpec((tm, D_MODEL), lambda i: (i, 0)), pl.BlockSpec((tm, D_FF), lambda i: (i, 0)),
                   pl.BlockSpec((tm, D_FF), lambda i: (i, 0))] + [_HBM] * ng,
        out_shape=[jax.ShapeDtypeStruct((s, D_MODEL), F32), jax.ShapeDtypeStruct((s, D_FF), BF16),
                   jax.ShapeDtypeStruct((s, D_FF), BF16)] + (_gather_out_shapes(gather) if ng else []),
        scratch_shapes=_exchange_scratch(8, ng) if ng else [],
        compiler_params=_params(dimension_semantics=("arbitrary",)),
    )(x, g, wg, wu, wd, *(gather or []))
    return outs[0], outs[1], outs[2], outs[3:]


def _ffn_bwd(x, dy, gate, up, g, wg, wu, wd, name, scatter=None):
    s = x.shape[0]
    tm = _row_tile(s, 256)
    steps = s // tm
    ng = len(scatter) if scatter else 0
    (wg, wg_spec), (wu, wu_spec), (wd, wd_spec) = _weight_operand(wg), _weight_operand(wu), _weight_operand(wd)

    def body(x_ref, dy_ref, gate_ref, up_ref, g_ref, wg_ref, wu_ref, wd_ref, *rest):
        c_ins, (dx_ref, dgain_ref, n_ref, act_ref, dgate_ref, dup_ref) = rest[:ng], rest[ng:ng + 6]
        c_outs, sems = rest[ng + 6:2 * ng + 6], rest[2 * ng + 6:]
        if ng:
            _run_stages_at(_scatter_stages(c_ins, c_outs, *sems), [(0,), (steps - 1,)])
        i = pl.program_id(0)
        xv = x_ref[...]
        dyv = dy_ref[...]
        gv = g_ref[...]
        r = lax.rsqrt(jnp.mean(xv * xv, axis=-1, keepdims=True) + EPS)
        xh = xv * r
        n_ref[...] = (xh * gv).astype(BF16)
        dyh = (0.5 * dyv).astype(BF16)
        dn = jnp.zeros_like(xv)
        for c in range(0, D_FF, FF_CHUNK):
            dact = _dot_nt(dyh, _weight_rows(wd_ref, c, FF_CHUNK))
            gt = gate_ref[:, c:c + FF_CHUNK].astype(F32)
            u = up_ref[:, c:c + FF_CHUNK].astype(F32)
            sg = _sigmoid(gt)
            sl = gt * sg
            act_ref[:, c:c + FF_CHUNK] = (sl * u).astype(BF16)
            dup = (dact * sl).astype(BF16)
            dgate = (dact * u * (sg * (1.0 + gt * (1.0 - sg)))).astype(BF16)
            dup_ref[:, c:c + FF_CHUNK] = dup
            dgate_ref[:, c:c + FF_CHUNK] = dgate
            dn = (dn + jnp.dot(dgate, _weight_rows(wg_ref, c, FF_CHUNK), preferred_element_type=F32)
                  + jnp.dot(dup, _weight_rows(wu_ref, c, FF_CHUNK), preferred_element_type=F32))
        part = jnp.sum(dn * xh, axis=0, keepdims=True)

        @pl.when(i == 0)
        def _():
            dgain_ref[...] = part

        @pl.when(i > 0)
        def _():
            dgain_ref[...] += part

        dxh = dn * gv
        dx_ref[...] = dyv + r * (dxh - xh * jnp.mean(dxh * xh, axis=-1, keepdims=True))

    row = lambda w: pl.BlockSpec((tm, w), lambda i: (i, 0))
    outs = _pallas_call(
        body, name=name, grid=(steps,),
        in_specs=[row(D_MODEL), row(D_MODEL), row(D_FF), row(D_FF), _full((1, D_MODEL)), wg_spec, wu_spec, wd_spec]
        + [_HBM] * ng,
        out_specs=[row(D_MODEL), _full((1, D_MODEL)), row(D_MODEL), row(D_FF), row(D_FF), row(D_FF)] + [_HBM] * ng,
        out_shape=[jax.ShapeDtypeStruct((s, D_MODEL), F32), jax.ShapeDtypeStruct((1, D_MODEL), F32),
                   jax.ShapeDtypeStruct((s, D_MODEL), BF16), jax.ShapeDtypeStruct((s, D_FF), BF16),
                   jax.ShapeDtypeStruct((s, D_FF), BF16), jax.ShapeDtypeStruct((s, D_FF), BF16)]
        + (_scatter_out_shapes(scatter) if ng else []),
        scratch_shapes=_exchange_scratch(3, ng) if ng else [],
        compiler_params=_params(dimension_semantics=("arbitrary",)),
    )(x, dy, gate, up, g, wg, wu, wd, *(scatter or []))
    return outs[:6], outs[6:]


def _store_col_shards(o_ref, acc, first_shard, n_here, width):
    for q in range(n_here):
        o_ref[q] = acc[:, (first_shard + q) * width:(first_shard + q + 1) * width].astype(BF16)


def _matmul_tn(a, b, scale, name, group=None, slot=0, slots=1):
    t, m = a.shape
    n = b.shape[1]
    tk = _row_tile(t, 2048)
    tn = n // 2
    nk = t // tk
    per = m // N_SHARD

    def body(a_ref, b_ref, *rest):
        o_ref, acc_ref = rest[-2:]
        k = pl.program_id(1)
        bv = b_ref[...]
        if scale != 1.0:
            bv = bv.astype(F32) * scale
        part = _dot_tn(a_ref[...].astype(BF16), bv.astype(BF16))

        @pl.when(k == 0)
        def _():
            acc_ref[...] = part

        @pl.when(k > 0)
        def _():
            acc_ref[...] += part

        @pl.when(k == nk - 1)
        def _():
            for sh in range(N_SHARD):
                o_ref[sh] = acc_ref[sh * per:(sh + 1) * per, :].astype(BF16)

    earlier = [] if group is None else [group]
    return _pallas_call(
        body, name=name, grid=(n // tn, nk),
        in_specs=[pl.BlockSpec((tk, m), lambda j, k: (k, 0)), pl.BlockSpec((tk, tn), lambda j, k: (k, j))]
        + [pl.BlockSpec(memory_space=pl.ANY)] * len(earlier),
        out_specs=pl.BlockSpec((N_SHARD, per, tn), lambda j, k: (0, slot, j)),
        out_shape=jax.ShapeDtypeStruct((N_SHARD, slots * per, n), BF16),
        scratch_shapes=[pltpu.VMEM((m, tn), F32)], input_output_aliases={2: 0} if earlier else {},
        compiler_params=_params(dimension_semantics=("arbitrary", "arbitrary")),
    )(a, b, *earlier)


_HALF = SWA_HEAD_DIM // 2
_IN_ORDER = (list(range(0, IN_OFFS[2]))
             + [IN_OFFS[3] + SWA_HEAD_DIM * h + i for h in range(SWA_HEADS) for i in range(_HALF)]
             + [IN_OFFS[3] + SWA_HEAD_DIM * h + _HALF + i for h in range(SWA_HEADS) for i in range(_HALF)]
             + list(range(IN_OFFS[5], IN_OFFS[6]))
             + [IN_OFFS[4] + SWA_HEAD_DIM * j + i for j in range(SWA_KV_HEADS) for i in range(_HALF)]
             + [IN_OFFS[4] + SWA_HEAD_DIM * j + _HALF + i for j in range(SWA_KV_HEADS) for i in range(_HALF)]
             + list(range(IN_OFFS[2], IN_OFFS[3])))
_QB_ORDER = ([MLA_QK * h + i for h in range(MLA_HEADS) for i in range(MLA_NOPE)]
             + [MLA_QK * h + MLA_NOPE + i for h in range(MLA_HEADS) for i in range(_HALF)]
             + [MLA_QK * h + MLA_NOPE + _HALF + i for h in range(MLA_HEADS) for i in range(_HALF)])
_KVB_ORDER = ([(MLA_NOPE + MLA_V) * h + i for h in range(MLA_HEADS) for i in range(MLA_NOPE)]
              + [(MLA_NOPE + MLA_V) * h + MLA_NOPE + i for h in range(MLA_HEADS) for i in range(MLA_V)])
MIXER_ORDERS = {'w_in': _IN_ORDER, 'mla_w_q_b': _QB_ORDER, 'mla_w_kv_b': _KVB_ORDER}
_P_CQ, _P_CKV, _P_QA, _P_QB, _P_VS, _P_KA, _P_KB, _P_PE = (int(v) for v in np.cumsum(
    (0, MLA_Q_RANK, MLA_KV_RANK, SWA_WIDTH // 2, SWA_WIDTH // 2, IN_SPLITS[5], IN_SPLITS[4] // 2, IN_SPLITS[4] // 2)))


def _runs(order):
    out, start = [], 0
    for i in range(1, len(order) + 1):
        if i == len(order) or order[i] != order[i - 1] + 1:
            out.append((order[start], i - start))
            start = i
    return out


def _inverse(order):
    inv = [0] * len(order)
    for new, old in enumerate(order):
        inv[old] = new
    return inv


def _take_cols(a, order):
    return jnp.concatenate([a[..., st:st + w] for st, w in _runs(order)], axis=-1)


def _segment_matrix(n, seg):
    return (lax.broadcasted_iota(jnp.int32, (n, n), 0) // seg == lax.broadcasted_iota(jnp.int32, (n, n), 1) // seg).astype(BF16)


@jax.custom_vjp
def _cmm(t, b, bt):
    hi = t.astype(BF16)
    lo = (t - hi.astype(F32)).astype(BF16)
    return jnp.dot(hi, b, preferred_element_type=F32) + jnp.dot(lo, b, preferred_element_type=F32)


def _cmm_fwd(t, b, bt):
    return _cmm(t, b, bt), (b, bt)


def _cmm_bwd(res, dy):
    b, bt = res
    return _cmm(dy, bt, b), None, None


_cmm.defvjp(_cmm_fwd, _cmm_bwd)


def _segsum(t, b):
    return _cmm(t, b, b)


def _rowsum(t):
    n = t.shape[-1]
    return _cmm(t, jnp.ones((n, 128), BF16), jnp.ones((128, n), BF16))


def _by_head(vals, width):
    lane = lax.broadcasted_iota(jnp.int32, (vals[0].shape[0], len(vals) * width), 1)
    out = vals[-1]
    for hd in range(len(vals) - 2, -1, -1):
        out = jnp.where(lane < (hd + 1) * width, vals[hd], out)
    return out


def _rope2(a, b, cos, sin):
    return a * cos - b * sin, b * cos + a * sin


def _pre_math(x, gm, gqa, gkva, gq, gk, gsq, gsk, taps, win, wqb, wkvb, cos, sin):
    h = _rms(x, gm)
    proj = _mm(h, win)
    if taps is not None:
        proj = proj + taps[0]
    cqn = _rms(proj[:, _P_CQ:_P_CKV], gqa)
    qa_all = _mm(cqn, wqb)
    ckvn = _rms(proj[:, _P_CKV:_P_QA], gkva)
    kv_all = _mm(ckvn, wkvb)
    if taps is not None:
        qa_all = qa_all + taps[1]
        kv_all = kv_all + taps[2]
    nh, hw = MLA_HEADS, MLA_HEADS * _HALF
    seg_mla = _segment_matrix(hw, _HALF)
    tile = lambda g, n: jnp.concatenate([g] * n, axis=-1)
    c4, s4 = cos[:, :hw], sin[:, :hw]

    def mla_heads(nope, r1, r2, gain):
        rr = r1 * r1 + r2 * r2
        lane_head = lax.broadcasted_iota(jnp.int32, (hw, nh * MLA_NOPE), 0) // _HALF
        spread = (lane_head == lax.broadcasted_iota(jnp.int32, (hw, nh * MLA_NOPE), 1) // MLA_NOPE).astype(BF16)
        rope_on_nope = _cmm(rr, spread, spread.T)
        ss_nope = [_rowsum(jnp.square(nope[:, hd * MLA_NOPE:(hd + 1) * MLA_NOPE])) for hd in range(nh)]
        rinv = [lax.rsqrt((ss_nope[hd] + rope_on_nope[:, hd * MLA_NOPE:(hd + 1) * MLA_NOPE]) * (1.0 / MLA_QK) + EPS)
                for hd in range(nh)]
        rl = lax.rsqrt((_segsum(rr, seg_mla) + _by_head(ss_nope, _HALF)) * (1.0 / MLA_QK) + EPS)
        o1, o2 = _rope2(r1 * rl * tile(gain[:, MLA_NOPE:MLA_NOPE + _HALF], nh), r2 * rl * tile(gain[:, MLA_NOPE + _HALF:], nh), c4, s4)
        return [jnp.concatenate([nope[:, hd * MLA_NOPE:(hd + 1) * MLA_NOPE] * rinv[hd] * gain[:, :MLA_NOPE],
                                 o1[:, hd * _HALF:(hd + 1) * _HALF], o2[:, hd * _HALF:(hd + 1) * _HALF]], axis=-1)
                for hd in range(nh)]

    q_a = mla_heads(qa_all[:, :nh * MLA_NOPE], qa_all[:, nh * MLA_NOPE:nh * MLA_NOPE + hw], qa_all[:, nh * MLA_NOPE + hw:], gq)
    pe1, pe2 = proj[:, _P_PE:_P_PE + _HALF], proj[:, _P_PE + _HALF:_P_PE + 2 * _HALF]
    k_a = mla_heads(kv_all[:, :nh * MLA_NOPE], tile(pe1, nh), tile(pe2, nh), gk)
    v_a = [kv_all[:, nh * MLA_NOPE + hd * MLA_V:nh * MLA_NOPE + (hd + 1) * MLA_V] for hd in range(nh)]

    def swa_heads(a, b, gain, n):
        w = n * _HALF
        r = lax.rsqrt(_segsum(a * a + b * b, _segment_matrix(w, _HALF)) * (1.0 / SWA_HEAD_DIM) + EPS)
        o1, o2 = _rope2(a * r * tile(gain[:, :_HALF], n), b * r * tile(gain[:, _HALF:], n), cos[:, :w], sin[:, :w])
        return [jnp.concatenate([o1[:, hd * _HALF:(hd + 1) * _HALF], o2[:, hd * _HALF:(hd + 1) * _HALF]], axis=-1) for hd in range(n)]

    q_b = swa_heads(proj[:, _P_QA:_P_QB], proj[:, _P_QB:_P_VS], gsq, SWA_HEADS)
    k_b = swa_heads(proj[:, _P_KA:_P_KB], proj[:, _P_KB:_P_PE], gsk, SWA_KV_HEADS)
    v_b = [proj[:, _P_VS + j * SWA_HEAD_DIM:_P_VS + (j + 1) * SWA_HEAD_DIM] for j in range(SWA_KV_HEADS)]
    return (q_a, k_a, v_a, q_b, k_b, v_b), (h, cqn, ckvn)


_PRE_GAIN_WIDTHS = (D_MODEL, MLA_Q_RANK, MLA_KV_RANK, MLA_QK, MLA_QK, SWA_HEAD_DIM, SWA_HEAD_DIM)
_PRE_HEADS = ((MLA_HEADS, MLA_QK), (MLA_HEADS, MLA_QK), (MLA_HEADS, MLA_V),
              (SWA_HEADS, SWA_HEAD_DIM), (SWA_KV_HEADS, SWA_HEAD_DIM), (SWA_KV_HEADS, SWA_HEAD_DIM))


def _pre_fwd(x, gains, win, wqb, wkvb, cos, sin, name):
    s = x.shape[0]
    tm = _row_tile(s, 512)

    def body(x_ref, *refs):
        g_refs, (win_ref, wqb_ref, wkvb_ref, cos_ref, sin_ref), out_refs = refs[:7], refs[7:12], refs[12:]
        outs, _ = _pre_math(x_ref[...], *[g[...] for g in g_refs], None, win_ref[...], wqb_ref[...], wkvb_ref[...],
                            cos_ref[...], sin_ref[...])
        for idx, (ref, heads) in enumerate(zip(out_refs, outs)):
            for hd, val in enumerate(heads):
                ref[hd] = (val * MLA_QSCALE if idx == 0 else val).astype(BF16)

    heads_spec = lambda nh, w: pl.BlockSpec((nh, tm, w), lambda i: (0, i, 0))
    return _pallas_call(
        body, name=name, grid=(s // tm,),
        in_specs=[pl.BlockSpec((tm, D_MODEL), lambda i: (i, 0))] + [_full((1, w)) for w in _PRE_GAIN_WIDTHS]
        + [_resident(win.shape), _resident(wqb.shape), _resident(wkvb.shape),
           pl.BlockSpec((tm, SWA_HEADS * _HALF), lambda i: (i, 0)), pl.BlockSpec((tm, SWA_HEADS * _HALF), lambda i: (i, 0))],
        out_specs=[heads_spec(nh, w) for nh, w in _PRE_HEADS],
        out_shape=[jax.ShapeDtypeStruct((nh, s, w), BF16) for nh, w in _PRE_HEADS],
        compiler_params=_params(dimension_semantics=("arbitrary",)),
    )(x, *gains, win, wqb, wkvb, cos, sin)


def _pre_bwd(x, dx_res, cts, gains, win, wqb, wkvb, cos, sin, name):
    s = x.shape[0]
    tm = _row_tile(s, 256)
    tap_widths = (IN_COLS, MLA_HEADS * MLA_QK, MLA_HEADS * (MLA_NOPE + MLA_V))

    def body(x_ref, dxr_ref, *refs):
        ct_refs, g_refs = refs[:6], refs[6:13]
        win_ref, wqb_ref, wkvb_ref, cos_ref, sin_ref = refs[13:18]
        dx_ref, dg_refs, dw_refs, acc_refs = refs[18], refs[19:26], refs[26:29], refs[29:32]
        i = pl.program_id(0)
        win_v, wqb_v, wkvb_v, cos_v, sin_v = win_ref[...], wqb_ref[...], wkvb_ref[...], cos_ref[...], sin_ref[...]

        def f(xv, gm, gqa, gkva, gq, gk, gsq, gsk, t0, t1, t2):
            return _pre_math(xv, gm, gqa, gkva, gq, gk, gsq, gsk, (t0, t1, t2), win_v, wqb_v, wkvb_v, cos_v, sin_v)

        taps = [jnp.zeros((tm, w), F32) for w in tap_widths]
        _, vjp, acts = jax.vjp(f, x_ref[...], *[g[...] for g in g_refs], *taps, has_aux=True)
        ct = tuple([ref[hd] for hd in range(nh)] for ref, (nh, _) in zip(ct_refs, _PRE_HEADS))
        grads = vjp(ct)
        dx_ref[...] = grads[0] + dxr_ref[...]
        dws = [_dot_tn(a.astype(BF16), t.astype(BF16)) for a, t in zip(acts, grads[8:11])]

        @pl.when(i == 0)
        def _():
            for ref, val in zip(dg_refs, grads[1:8]):
                ref[...] = val
            for ref, val in zip(acc_refs, dws):
                ref[...] = val

        @pl.when(i > 0)
        def _():
            for ref, val in zip(dg_refs, grads[1:8]):
                ref[...] += val
            for ref, val in zip(acc_refs, dws):
                ref[...] += val

        @pl.when(i == s // tm - 1)
        def _():
            for ref, acc, order in zip(dw_refs, acc_refs, (_IN_ORDER, _QB_ORDER, _KVB_ORDER)):
                _store_col_shards(ref, _take_cols(acc[...], _inverse(order)), 0, N_SHARD, acc.shape[1] // N_SHARD)

    heads_spec = lambda nh, w: pl.BlockSpec((nh, tm, w), lambda i: (0, i, 0))
    row = pl.BlockSpec((tm, D_MODEL), lambda i: (i, 0))
    half = pl.BlockSpec((tm, SWA_HEADS * _HALF), lambda i: (i, 0))
    shard_shapes = [(N_SHARD, w.shape[0], w.shape[1] // N_SHARD) for w in (win, wqb, wkvb)]
    return _pallas_call(
        body, name=name, grid=(s // tm,),
        in_specs=[row, row] + [heads_spec(nh, w) for nh, w in _PRE_HEADS] + [_full((1, w)) for w in _PRE_GAIN_WIDTHS]
        + [_resident(win.shape), _resident(wqb.shape), _resident(wkvb.shape), half, half],
        out_specs=[row] + [_full((1, w)) for w in _PRE_GAIN_WIDTHS] + [_full(shp) for shp in shard_shapes],
        out_shape=[jax.ShapeDtypeStruct((s, D_MODEL), F32)] + [jax.ShapeDtypeStruct((1, w), F32) for w in _PRE_GAIN_WIDTHS]
        + [jax.ShapeDtypeStruct(shp, BF16) for shp in shard_shapes],
        scratch_shapes=[pltpu.VMEM(w.shape, F32) for w in (win, wqb, wkvb)],
        compiler_params=_params(dimension_semantics=("arbitrary",)),
    )(x, dx_res, *cts, *gains, win, wqb, wkvb, cos, sin)


def _post_math(oa, ob, ga, gb, wo):
    mixed = jnp.concatenate([_rms(jnp.concatenate(oa, axis=-1), ga), _rms(jnp.concatenate(ob, axis=-1), gb)], axis=-1)
    return _mm(mixed, wo), mixed


def _post_fwd(x, oa, ob, ga, gb, wo, name):
    s = x.shape[0]
    tm = _row_tile(s, 512)

    def body(x_ref, oa_ref, ob_ref, ga_ref, gb_ref, wo_ref, y_ref):
        y, _ = _post_math([oa_ref[hd] for hd in range(MLA_HEADS)], [ob_ref[hd] for hd in range(SWA_HEADS)],
                          ga_ref[...], gb_ref[...], wo_ref[...])
        y_ref[...] = x_ref[...] + y

    row = pl.BlockSpec((tm, D_MODEL), lambda i: (i, 0))
    return _pallas_call(
        body, name=name, grid=(s // tm,),
        in_specs=[row, pl.BlockSpec((MLA_HEADS, tm, MLA_V), lambda i: (0, i, 0)),
                  pl.BlockSpec((SWA_HEADS, tm, SWA_HEAD_DIM), lambda i: (0, i, 0)),
                  _full((1, MLA_WIDTH)), _full((1, SWA_WIDTH)), _resident(wo.shape)],
        out_specs=row, out_shape=jax.ShapeDtypeStruct((s, D_MODEL), F32),
        compiler_params=_params(dimension_semantics=("arbitrary",)),
    )(x, oa, ob, ga, gb, wo)


def _post_bwd(dy, oa, ob, ga, gb, wo, name):
    s = dy.shape[0]
    tm = _row_tile(s, 512)
    t = _attn_tile(s)

    def body(dy_ref, oa_ref, ob_ref, ga_ref, gb_ref, wo_ref, doa_ref, dob_ref, dga_ref, dgb_ref, dwo_ref, delta_ref, acc_ref):
        i = pl.program_id(0)
        wo_v = wo_ref[...]
        dyv = dy_ref[...]

        def f(oa_l, ob_l, ga_v, gb_v):
            return _post_math(oa_l, ob_l, ga_v, gb_v, wo_v)

        _, vjp, mixed = jax.vjp(f, [oa_ref[hd] for hd in range(MLA_HEADS)], [ob_ref[hd] for hd in range(SWA_HEADS)],
                                ga_ref[...], gb_ref[...], has_aux=True)
        doa, dob, dga, dgb = vjp(dyv)
        for hd in range(MLA_HEADS):
            doa_ref[hd] = doa[hd]
            rows = _as_rows(_rowsum(doa[hd] * oa_ref[hd]))
            for j in range(tm // t):
                delta_ref[hd, j] = rows[:, j * t:(j + 1) * t]
        for hd in range(SWA_HEADS):
            dob_ref[hd] = dob[hd]
        dwo = _dot_tn(mixed.astype(BF16), dyv.astype(BF16))

        @pl.when(i == 0)
        def _():
            dga_ref[...] = dga
            dgb_ref[...] = dgb
            acc_ref[...] = dwo

        @pl.when(i > 0)
        def _():
            dga_ref[...] += dga
            dgb_ref[...] += dgb
            acc_ref[...] += dwo

        @pl.when(i == s // tm - 1)
        def _():
            dwo_ref[...] = acc_ref[...].astype(BF16)

    row = pl.BlockSpec((tm, D_MODEL), lambda i: (i, 0))
    oa_spec = pl.BlockSpec((MLA_HEADS, tm, MLA_V), lambda i: (0, i, 0))
    ob_spec = pl.BlockSpec((SWA_HEADS, tm, SWA_HEAD_DIM), lambda i: (0, i, 0))
    return _pallas_call(
        body, name=name, grid=(s // tm,),
        in_specs=[row, oa_spec, ob_spec, _full((1, MLA_WIDTH)), _full((1, SWA_WIDTH)), _resident(wo.shape)],
        out_specs=[oa_spec, ob_spec, _full((1, MLA_WIDTH)), _full((1, SWA_WIDTH)), _full(wo.shape),
                   pl.BlockSpec((MLA_HEADS, tm // t, 8, t), lambda i: (0, i, 0, 0))],
        out_shape=[jax.ShapeDtypeStruct((MLA_HEADS, s, MLA_V), F32), jax.ShapeDtypeStruct((SWA_HEADS, s, SWA_HEAD_DIM), F32),
                   jax.ShapeDtypeStruct((1, MLA_WIDTH), F32), jax.ShapeDtypeStruct((1, SWA_WIDTH), F32),
                   jax.ShapeDtypeStruct(wo.shape, BF16), jax.ShapeDtypeStruct((MLA_HEADS, s // t, 8, t), F32)],
        scratch_shapes=[pltpu.VMEM(wo.shape, F32)],
        compiler_params=_params(dimension_semantics=("arbitrary",)),
    )(dy, oa, ob, ga, gb, wo)


def _attn_tile(s):
    return 512 if s >= 2048 else 128


def _as_rows(cols):
    return cols.T[0:8, :]


def _causal_mask(t):
    return lax.broadcasted_iota(jnp.int32, (t, t), 1) <= lax.broadcasted_iota(jnp.int32, (t, t), 0)


def _pipelined_blocks(first, count, last_block, issue, consume, carry, prefetch_after):
    def clamped(j, slot):
        issue(jnp.minimum(j, last_block), slot)

    def pair(jj, c):
        a = first + 2 * jj
        clamped(a + 1, 1)
        c = consume(a, 0, c)
        clamped(a + 2, 0)
        return consume(a + 1, 1, c)

    clamped(first, 0)
    npairs = count // 2
    carry = lax.fori_loop(0, npairs, pair, carry)

    def odd(c):
        c = consume(first + 2 * npairs, 0, c)
        if prefetch_after:
            clamped(first + count, 0)
        return c

    return lax.cond(count - 2 * npairs == 1, odd, lambda c: c, carry)


def _run_stages_at(stages, steps):
    for stage, step in zip(stages, steps):
        here = pl.program_id(0) == step[0]
        for axis in range(1, len(step)):
            here = here & (pl.program_id(axis) == step[axis])
        pl.when(here)(stage)


def _mla_fwd(q, k, v, name, gather=None):
    nh, s, _ = q.shape
    t = _attn_tile(s)
    nq = s // t
    ng = len(gather) if gather else 0

    def body(q_ref, k_ref, v_ref, *rest):
        g_ins, (o_ref, lse_ref), g_outs = rest[:ng], rest[ng:ng + 2], rest[ng + 2:2 * ng + 2]
        (s0_ref, s1_ref), sems = rest[2 * ng + 2:2 * ng + 4], rest[2 * ng + 4:]
        if ng:
            _run_stages_at(_gather_stages(g_ins, g_outs, *sems), [(0, 0), (nh // 2, 0), (nh - 1, 0), (nh - 1, nq - 1)])
        qi = pl.program_id(1)
        qv = q_ref[...]
        s_refs = (s0_ref, s1_ref)

        def rows(j):
            return pl.ds(pl.multiple_of(j * t, t), t)

        def issue(j, slot):
            s_refs[slot][...] = _dot_nt(qv, k_ref[rows(j), :])

        def consume(j, slot, carry, masked=False):
            m, l, acc = carry
            sc = s_refs[slot][...]
            if masked:
                sc = jnp.where(_causal_mask(t), sc, NEG)
            m_new = jnp.maximum(m, jnp.max(sc, axis=-1, keepdims=True))
            alpha = jnp.exp2(m - m_new)
            p = jnp.exp2(sc - m_new)
            l = alpha * l + jnp.sum(p, axis=-1, keepdims=True)
            acc = alpha * acc + jnp.dot(p.astype(BF16), v_ref[rows(j), :], preferred_element_type=F32)
            return m_new, l, acc

        init = (jnp.full((t, 1), NEG, F32), jnp.zeros((t, 1), F32), jnp.zeros((t, MLA_V), F32))
        carry = _pipelined_blocks(0, qi, nq - 1, issue, consume, init, True)
        m, l, acc = consume(qi, 0, carry, masked=True)
        o_ref[...] = acc / l
        lse_ref[...] = _as_rows(jnp.broadcast_to(m + jnp.log2(l), (t, 128)))

    outs = _pallas_call(
        body, name=name, grid=(nh, nq),
        in_specs=[pl.BlockSpec((None, t, MLA_QK), lambda h, i: (h, i, 0)), pl.BlockSpec((None, s, MLA_QK), lambda h, i: (h, 0, 0)),
                  pl.BlockSpec((None, s, MLA_V), lambda h, i: (h, 0, 0))] + [_HBM] * ng,
        out_specs=[pl.BlockSpec((None, t, MLA_V), lambda h, i: (h, i, 0)), pl.BlockSpec((None, None, 8, t), lambda h, i: (h, i, 0, 0))]
        + [_HBM] * ng,
        out_shape=[jax.ShapeDtypeStruct((nh, s, MLA_V), F32), jax.ShapeDtypeStruct((nh, nq, 8, t), F32)]
        + (_gather_out_shapes(gather) if ng else []),
        scratch_shapes=[pltpu.VMEM((t, t), F32)] * 2 + (_exchange_scratch(8, ng) if ng else []),
        compiler_params=_params(dimension_semantics=("arbitrary", "arbitrary")),
    )(q, k, v, *(gather or []))
    return outs[0], outs[1], outs[2:]


def _mla_bwd(q, k, v, do, lse_row, delta_row, name, scatter=None):
    nh, s, _ = q.shape
    t = _attn_tile(s)
    nq = s // t
    ng = len(scatter) if scatter else 0

    def body(q_ref, k_ref, v_ref, do_ref, lse_ref, delta_ref, *rest):
        c_ins, (dq_ref, dk_ref, dv_ref), c_outs = rest[:ng], rest[ng:ng + 3], rest[ng + 3:2 * ng + 3]
        (s0_ref, s1_ref, dp0_ref, dp1_ref), sems = rest[2 * ng + 3:2 * ng + 7], rest[2 * ng + 7:]
        if ng:
            _run_stages_at(_scatter_stages(c_ins, c_outs, *sems), [(0, 0), (nh - 1, nq - 1)])
        kj = pl.program_id(1)
        kv_, vv = k_ref[...], v_ref[...]
        s_refs, dp_refs = (s0_ref, s1_ref), (dp0_ref, dp1_ref)

        @pl.when(kj == 0)
        def _():
            dq_ref[...] = jnp.zeros_like(dq_ref)

        def rows(i):
            return pl.ds(pl.multiple_of(i * t, t), t)

        def issue(i, slot):
            s_refs[slot][...] = _dot_nt(kv_, q_ref[rows(i), :])
            dp_refs[slot][...] = _dot_nt(vv, do_ref[rows(i), :].astype(BF16))

        def consume(i, slot, carry, masked=False):
            dk, dv = carry
            p = jnp.exp2(s_refs[slot][...] - lse_ref[i][0:1, :])
            if masked:
                p = jnp.where(lax.broadcasted_iota(jnp.int32, (t, t), 0) <= lax.broadcasted_iota(jnp.int32, (t, t), 1), p, 0.0)
            dv = dv + jnp.dot(p.astype(BF16), do_ref[rows(i), :].astype(BF16), preferred_element_type=F32)
            ds = (p * (dp_refs[slot][...] - delta_ref[i][0:1, :])).astype(BF16)
            dk = dk + jnp.dot(ds, q_ref[rows(i), :], preferred_element_type=F32)
            dq_ref[rows(i), :] += _dot_tn(ds, kv_) * MLA_SCALE
            return dk, dv

        issue(kj, 0)
        carry = consume(kj, 0, (jnp.zeros((t, MLA_QK), F32), jnp.zeros((t, MLA_V), F32)), masked=True)
        dk, dv = _pipelined_blocks(kj + 1, nq - 1 - kj, nq - 1, issue, consume, carry, False)
        dk_ref[...] = dk * LN2
        dv_ref[...] = dv

    tile = lambda w: pl.BlockSpec((None, t, w), lambda h, j: (h, j, 0))
    whole = lambda w: pl.BlockSpec((None, s, w), lambda h, j: (h, 0, 0))
    rows_spec = pl.BlockSpec((None, nq, 8, t), lambda h, j: (h, 0, 0, 0))
    outs = _pallas_call(
        body, name=name, grid=(nh, nq),
        in_specs=[whole(MLA_QK), tile(MLA_QK), tile(MLA_V), whole(MLA_V), rows_spec, rows_spec] + [_HBM] * ng,
        out_specs=[whole(MLA_QK), tile(MLA_QK), tile(MLA_V)] + [_HBM] * ng,
        out_shape=[jax.ShapeDtypeStruct((nh, s, MLA_QK), F32), jax.ShapeDtypeStruct((nh, s, MLA_QK), F32),
                   jax.ShapeDtypeStruct((nh, s, MLA_V), F32)] + (_scatter_out_shapes(scatter) if ng else []),
        scratch_shapes=[pltpu.VMEM((t, t), F32)] * 4 + (_exchange_scratch(3, ng) if ng else []),
        compiler_params=_params(dimension_semantics=("arbitrary", "arbitrary")),
    )(q, k, v, do, lse_row, delta_row, *(scatter or []))
    return outs[0], outs[1], outs[2], outs[3:]


def _swa_tile(s):
    return min(s, 8 * BLOCK)


def _swa_specs(tq):
    nb = tq // BLOCK
    grp = lambda w: pl.BlockSpec((SWA_GROUP, tq, w), lambda j, i: (j, i, 0))
    main = pl.BlockSpec((None, tq, SWA_HEAD_DIM), lambda j, i: (j, i, 0))
    tail = pl.BlockSpec((None, BLOCK, SWA_HEAD_DIM), lambda j, i: (j, nb * (i + 1), 0))
    sink = pl.BlockSpec((None, SWA_GROUP, 128), lambda j, i: (j, 0, 0))
    return grp, main, tail, sink


def _swa_band_mask(first):
    shape = (SWA_GROUP * BLOCK, 2 * BLOCK)
    q_rel = (lax.broadcasted_iota(jnp.int32, shape, 0) & (BLOCK - 1)) + BLOCK
    k_rel = lax.broadcasted_iota(jnp.int32, shape, 1)
    dist = q_rel - k_rel
    return (dist >= 0) & (dist < BLOCK) & ((k_rel >= BLOCK) | jnp.logical_not(first))


def _swa_sink_column(sink_ref):
    sk = sink_ref[...]
    return jnp.concatenate([jnp.broadcast_to(sk[g:g + 1, 0:1], (BLOCK, 1)) for g in range(SWA_GROUP)], axis=0)


def _swa_fwd(q, kpad, vpad, sinks, name):
    _, s, _ = q.shape
    tq = _swa_tile(s)
    grp, main, tail, sink = _swa_specs(tq)
    d = SWA_HEAD_DIM

    def body(q_ref, km_ref, kt_ref, vm_ref, vt_ref, sink_ref, o_ref, lse_ref):
        i = pl.program_id(1)
        kall = jnp.concatenate([km_ref[...], kt_ref[...]], axis=0)
        vall = jnp.concatenate([vm_ref[...], vt_ref[...]], axis=0)
        sink_col = _swa_sink_column(sink_ref)
        for b in range(tq // BLOCK):
            lo = b * BLOCK
            valid = _swa_band_mask(i == 0 if b == 0 else False)
            q4 = q_ref[:, lo:lo + BLOCK, :].reshape(SWA_GROUP * BLOCK, d)
            sc = jnp.where(valid, _dot_nt(q4, kall[lo:lo + 2 * BLOCK]) * SWA_SCALE, NEG)
            m = jnp.maximum(jnp.max(sc, axis=-1, keepdims=True), sink_col)
            e = jnp.exp(sc - m)
            den = jnp.sum(e, axis=-1, keepdims=True) + jnp.exp(sink_col - m)
            out = jnp.dot((e * (1.0 / den)).astype(BF16), vall[lo:lo + 2 * BLOCK], preferred_element_type=F32)
            o_ref[:, lo:lo + BLOCK, :] = out.reshape(SWA_GROUP, BLOCK, d)
            lse_ref[:, lo:lo + BLOCK, :] = (m + jnp.log(den)).reshape(SWA_GROUP, BLOCK, 1)

    return _pallas_call(
        body, name=name, grid=(SWA_KV_HEADS, s // tq),
        in_specs=[grp(d), main, tail, main, tail, sink], out_specs=[grp(d), grp(1)],
        out_shape=[jax.ShapeDtypeStruct((SWA_HEADS, s, d), F32), jax.ShapeDtypeStruct((SWA_HEADS, s, 1), F32)],
        compiler_params=_params(dimension_semantics=("arbitrary", "arbitrary")),
    )(q, kpad, kpad, vpad, vpad, sinks)


def _swa_bwd(q, kpad, vpad, sinks, o, lse, do, name):
    _, s, _ = q.shape
    tq = _swa_tile(s)
    grp, main, tail, sink = _swa_specs(tq)
    d = SWA_HEAD_DIM

    def body(q_ref, km_ref, kt_ref, vm_ref, vt_ref, sink_ref, o_ref, lse_ref, do_ref, dq_ref, dk_ref, dv_ref, dsink_ref):
        i = pl.program_id(1)
        kall = jnp.concatenate([km_ref[...], kt_ref[...]], axis=0)
        vall = jnp.concatenate([vm_ref[...], vt_ref[...]], axis=0)
        sink_col = _swa_sink_column(sink_ref)

        @pl.when(i == 0)
        def _():
            dk_ref[...] = jnp.zeros_like(dk_ref)
            dv_ref[...] = jnp.zeros_like(dv_ref)
            dsink_ref[...] = jnp.zeros_like(dsink_ref)

        dsink = jnp.zeros((SWA_GROUP * BLOCK, 1), F32)
        for b in range(tq // BLOCK):
            lo = b * BLOCK
            valid = _swa_band_mask(i == 0 if b == 0 else False)
            rows4 = SWA_GROUP * BLOCK
            q4 = q_ref[:, lo:lo + BLOCK, :].reshape(rows4, d)
            do4 = do_ref[:, lo:lo + BLOCK, :].reshape(rows4, d)
            lse4 = lse_ref[:, lo:lo + BLOCK, :].reshape(rows4, 1)
            delta = jnp.sum(do4 * o_ref[:, lo:lo + BLOCK, :].reshape(rows4, d), axis=-1, keepdims=True)
            kb, vb = kall[lo:lo + 2 * BLOCK], vall[lo:lo + 2 * BLOCK]
            do4b = do4.astype(BF16)
            p = jnp.where(valid, jnp.exp(_dot_nt(q4, kb) * SWA_SCALE - lse4), 0.0)
            ds = (p * (_dot_nt(do4b, vb) - delta) * SWA_SCALE).astype(BF16)
            dq_ref[:, lo:lo + BLOCK, :] = jnp.dot(ds, kb, preferred_element_type=F32).reshape(SWA_GROUP, BLOCK, d)
            band = pl.ds(pl.multiple_of(i * tq, BLOCK) + lo, 2 * BLOCK)
            dk_ref[band, :] += _dot_tn(ds, q4)
            dv_ref[band, :] += _dot_tn(p.astype(BF16), do4b)
            dsink = dsink - jnp.exp(sink_col - lse4) * delta
        per_head = [jnp.broadcast_to(jnp.sum(dsink[g * BLOCK:(g + 1) * BLOCK], axis=0, keepdims=True), (1, 128))
                    for g in range(SWA_GROUP)]
        dsink_ref[...] += jnp.concatenate(per_head + [jnp.zeros((8 - SWA_GROUP, 128), F32)], axis=0)

    acc = pl.BlockSpec((None, s + BLOCK, d), lambda j, i: (j, 0, 0))
    return _pallas_call(
        body, name=name, grid=(SWA_KV_HEADS, s // tq),
        in_specs=[grp(d), main, tail, main, tail, sink, grp(d), grp(1), grp(d)],
        out_specs=[grp(d), acc, acc, pl.BlockSpec((None, 8, 128), lambda j, i: (j, 0, 0))],
        out_shape=[jax.ShapeDtypeStruct((SWA_HEADS, s, d), F32),
                   jax.ShapeDtypeStruct((SWA_KV_HEADS, s + BLOCK, d), F32),
                   jax.ShapeDtypeStruct((SWA_KV_HEADS, s + BLOCK, d), F32),
                   jax.ShapeDtypeStruct((SWA_KV_HEADS, 8, 128), F32)],
        compiler_params=_params(dimension_semantics=("arbitrary", "arbitrary")),
    )(q, kpad, kpad, vpad, vpad, sinks, o, lse, do)


def _loss_head(y, target, name):
    s = y.shape[0]
    tm = _row_tile(s, 512)

    def body(y_ref, t_ref, dy_ref, loss_ref):
        i = pl.program_id(0)
        err = y_ref[...] - t_ref[...]
        dy_ref[...] = err * (1.0 / D_MODEL)
        part = jnp.broadcast_to(0.5 * jnp.sum(jnp.mean(err * err, axis=-1, keepdims=True), axis=0, keepdims=True), (1, 128))

        @pl.when(i == 0)
        def _():
            loss_ref[...] = part

        @pl.when(i > 0)
        def _():
            loss_ref[...] += part

    row = pl.BlockSpec((tm, D_MODEL), lambda i: (i, 0))
    return _pallas_call(
        body, name=name, grid=(s // tm,), in_specs=[row, row], out_specs=[row, _full((1, 128))],
        out_shape=[jax.ShapeDtypeStruct((s, D_MODEL), F32), jax.ShapeDtypeStruct((1, 128), F32)],
        compiler_params=_params(dimension_semantics=("arbitrary",)),
    )(y, target)


def _adamw(w, g, m, v, name):
    rows, cols = w.shape
    tr = rows
    for cand in (512, 256, 128, 64, 32, 16, 8):
        if rows % cand == 0 and rows > cand:
            tr = cand
            break

    def body(w_ref, g_ref, m_ref, v_ref, d_ref, nm_ref, nv_ref):
        gv = g_ref[...]
        nm = ADAM_B1 * m_ref[...] + (1.0 - ADAM_B1) * gv
        nv = ADAM_B2 * v_ref[...] + (1.0 - ADAM_B2) * (gv * gv)
        m_hat = nm / (1.0 - ADAM_B1 ** ADAM_STEP)
        v_hat = nv / (1.0 - ADAM_B2 ** ADAM_STEP)
        d_ref[...] = -ADAM_LR * (m_hat / (jnp.sqrt(v_hat) + ADAM_EPS) + ADAM_WD * w_ref[...])
        nm_ref[...] = nm
        nv_ref[...] = nv

    blk = pl.BlockSpec((tr, cols), lambda i: (i, 0))
    return _pallas_call(
        body, name=name, grid=(rows // tr,), in_specs=[blk] * 4, out_specs=[blk] * 3,
        out_shape=[jax.ShapeDtypeStruct((rows, cols), F32)] * 3,
        compiler_params=_params(dimension_semantics=("arbitrary",)),
    )(w, g, m, v)


def _position():
    return lax.axis_index("x"), lax.axis_index("y"), lax.axis_index("c")


def _remote(src, dst, send_sems, recv_sems, k, to):
    return pltpu.make_async_remote_copy(src_ref=src, dst_ref=dst, send_sem=send_sems.at[k], recv_sem=recv_sems.at[k],
                                        device_id=to, device_id_type=MESH)


_HBM = pl.BlockSpec(memory_space=pltpu.HBM)


def _gather_stages(ins, outs, send_sems, recv_sems):
    na = len(ins)
    x, y, c = _position()
    me, sibling = (x, y, c), (x, y, 1 - c)
    xn, yn, dg = (1 - x, y), (x, 1 - y), (1 - x, 1 - y)

    def slot(a, chip, pc, half=None):
        ref = outs[a].at[4 * chip[0] + 2 * chip[1] + pc]
        if half is None:
            return ref
        rows = ref.shape[0] // 2
        return ref.at[pl.ds(half * rows, rows)]

    def cp(a, k, chip, pc, half, to, src=None):
        dst = slot(a, chip, pc, half)
        return _remote(dst if src is None else src, dst, send_sems, recv_sems, 8 * a + k, to)

    first_hop = [(0, xn), (1, yn)]
    second_hop = [(0, xn, 2, 0, yn), (1, yn, 3, 1, xn)]

    def sends():
        out = []
        for a in range(na):
            out += [cp(a, k, (x, y), c, None, (*to, c), src=ins[a].at[c]) for k, to in first_hop]
            out += [cp(a, fwd_k, frm, c, half, (*to, c)) for _, frm, fwd_k, half, to in second_hop]
            out += [cp(a, 4 + k, frm, c, None, sibling) for k, frm in first_hop]
            out += [cp(a, 6 + half, dg, c, half, sibling) for half in (0, 1)]
        return out

    def stage0():
        for a in range(na):
            for k, to in first_hop:
                cp(a, k, (x, y), c, None, (*to, c), src=ins[a].at[c]).start()

    def stage1():
        for k, frm, fwd_k, half, to in second_hop:
            for a in range(na):
                cp(a, k, frm, c, None, me).wait_recv()
                cp(a, fwd_k, frm, c, half, (*to, c)).start()
                cp(a, 4 + k, frm, c, None, sibling).start()

    def stage2():
        for half in (0, 1):
            for a in range(na):
                cp(a, 2 + half, dg, c, half, me).wait_recv()
                cp(a, 6 + half, dg, c, half, sibling).start()

    def stage3():
        for a in range(na):
            for k, chip, half in ((4, xn, None), (5, yn, None), (6, dg, 0), (7, dg, 1)):
                cp(a, k, chip, 1 - c, half, me).wait_recv()
        for sent in sends():
            sent.wait_send()

    return [stage0, stage1, stage2, stage3]


def _gather_out_shapes(mine):
    return [jax.ShapeDtypeStruct((N_DEV,) + m.shape[1:], m.dtype) for m in mine]


def _exchange_scratch(per_array, na):
    return [pltpu.SemaphoreType.DMA((per_array * na,)), pltpu.SemaphoreType.DMA((per_array * na,))]


def _all_gather_halves(mine, name):
    na = len(mine)

    def body(*refs):
        for stage in _gather_stages(refs[:na], refs[na:2 * na], *refs[2 * na:]):
            stage()

    return _pallas_call(body, name=name, in_specs=[_HBM] * na, out_specs=[_HBM] * na, out_shape=_gather_out_shapes(mine),
                        scratch_shapes=_exchange_scratch(8, na))(*mine)


def _sibling_exchange(parts, name, other_half):
    na = len(parts)

    def body(*refs):
        ins, outs, (send_sems, recv_sems) = refs[:na], refs[na:2 * na], refs[2 * na:]
        x, y, c = _position()
        copies = [_remote(ins[a].at[:, 1 - c] if other_half else ins[a], outs[a], send_sems, recv_sems, a, (x, y, 1 - c))
                  for a in range(na)]
        for cp in copies:
            cp.start()
        for cp in copies:
            cp.wait()

    return _pallas_call(
        body, name=name, in_specs=[_HBM] * na, out_specs=[_HBM] * na,
        out_shape=[jax.ShapeDtypeStruct(p.shape[:1] + p.shape[2:] if other_half else p.shape, p.dtype) for p in parts],
        scratch_shapes=_exchange_scratch(1, na),
    )(*parts)


def _scatter_stages(ins, outs, send_sems, recv_sems):
    na = len(ins)
    x, y, c = _position()
    chips = [(1 - x, y), (x, 1 - y), (1 - x, 1 - y)]

    def copies():
        return [_remote(ins[a].at[2 * px + py], outs[a].at[j], send_sems, recv_sems, 3 * a + j, (px, py, c))
                for a in range(na) for j, (px, py) in enumerate(chips)]

    def start():
        for cp in copies():
            cp.start()

    def wait():
        for cp in copies():
            cp.wait()

    return [start, wait]


def _scatter_out_shapes(parts):
    return [jax.ShapeDtypeStruct((3,) + p.shape[1:], p.dtype) for p in parts]


def _scatter_to_chips(parts, name):
    na = len(parts)

    def body(*refs):
        for stage in _scatter_stages(refs[:na], refs[na:2 * na], *refs[2 * na:]):
            stage()

    return _pallas_call(body, name=name, in_specs=[_HBM] * na, out_specs=[_HBM] * na, out_shape=_scatter_out_shapes(parts),
                        scratch_shapes=_exchange_scratch(3, na))(*parts)


def _assemble(g4, mine, name, side_by_side, order=None):
    _, nl, r, w = g4.shape
    tr = _divisor_tile(r, 640)

    def body(in_ref, mine_ref, out_ref):
        chip = 2 * lax.axis_index("x") + lax.axis_index("y")
        blocks = [jnp.where(chip == sh, mine_ref[...], in_ref[sh]) for sh in range(N_SHARD)]
        if side_by_side:
            full = jnp.concatenate(blocks, axis=-1)
            out_ref[...] = full if order is None else _take_cols(full, order)
        else:
            for sh in range(N_SHARD):
                out_ref[sh] = blocks[sh]

    if side_by_side:
        out_spec = pl.BlockSpec((None, tr, N_SHARD * w), lambda l, i: (l, i, 0))
        out_shape = jax.ShapeDtypeStruct((nl, r, N_SHARD * w), g4.dtype)
    else:
        out_spec = pl.BlockSpec((None, N_SHARD, tr, w), lambda l, i: (l, 0, i, 0))
        out_shape = jax.ShapeDtypeStruct((nl, N_SHARD, r, w), g4.dtype)
    return _pallas_call(
        body, name=name, grid=(nl, r // tr),
        in_specs=[pl.BlockSpec((N_SHARD, None, tr, w), lambda l, i: (0, l, i, 0)),
                  pl.BlockSpec((None, tr, w), lambda l, i: (l, i, 0))],
        out_specs=out_spec, out_shape=out_shape,
        compiler_params=_params(dimension_semantics=("arbitrary", "arbitrary")),
    )(g4, mine)


def _all_reduce_small(vec, name):
    r, l = vec.shape

    def body(v_ref, out_ref, gath_ref, send_sems, recv_sems):
        x, y, c = _position()
        me = 4 * x + 2 * y + c
        gath_ref[me] = v_ref[...]
        copies = []
        for k in range(1, N_DEV):
            to = (x ^ (k >> 2), y ^ ((k >> 1) & 1), c ^ (k & 1))
            copies.append(_remote(gath_ref.at[me], gath_ref.at[me], send_sems, recv_sems, k - 1, to))
        for cp in copies:
            cp.start()
        for k in range(1, N_DEV):
            frm = 4 * (x ^ (k >> 2)) + 2 * (y ^ ((k >> 1) & 1)) + (c ^ (k & 1))
            _remote(gath_ref.at[frm], gath_ref.at[frm], send_sems, recv_sems, k - 1, (x, y, c)).wait_recv()
        for cp in copies:
            cp.wait_send()
        total = gath_ref[0]
        for d in range(1, N_DEV):
            total = total + gath_ref[d]
        out_ref[...] = total

    vm = pl.BlockSpec(memory_space=pltpu.VMEM)
    return _pallas_call(
        body, name=name, in_specs=[vm], out_specs=vm, out_shape=jax.ShapeDtypeStruct((r, l), F32),
        scratch_shapes=[pltpu.VMEM((N_DEV, r, l), F32), pltpu.SemaphoreType.DMA((N_DEV - 1,)),
                        pltpu.SemaphoreType.DMA((N_DEV - 1,))],
    )(vec)


def _sum_blocks(blocks, out_dtype, name):
    m, w = blocks[0].shape
    tr = _divisor_tile(m, 1024)

    def body(*refs):
        total = refs[0][...].astype(F32)
        for ref in refs[1:-1]:
            total = total + ref[...].astype(F32)
        refs[-1][...] = total.astype(out_dtype)

    blk = pl.BlockSpec((tr, w), lambda i: (i, 0))
    return _pallas_call(
        body, name=name, grid=(m // tr,), in_specs=[blk] * len(blocks), out_specs=blk,
        out_shape=jax.ShapeDtypeStruct((m, w), out_dtype),
        compiler_params=_params(dimension_semantics=("arbitrary",)),
    )(*blocks)


FIRST_GROUPS = (('ffn1_w_gate', 'ffn1_w_up', 'ffn1_w_down'),)
REST_GROUPS = (('ffn2_w_gate', 'ffn2_w_up', 'ffn2_w_down'), ('w_o',), ('w_in',), ('mla_w_q_b',), ('mla_w_kv_b',))
N_FIRST = len(FIRST_GROUPS)


def _shard_rows(name):
    shape, axis = BIG[name]
    return shape[0] // N_SHARD if axis == 0 else shape[0]


def _group_row_offsets(group):
    return [int(v) for v in np.cumsum([0] + [_shard_rows(n) for n in group])]


def _rope_tables(s):
    pos = jnp.arange(s, dtype=F32)
    inv = 1.0 / (ROPE_THETA ** (jnp.arange(0, MLA_ROPE, 2, dtype=F32) / MLA_ROPE))
    ang = pos[:, None] * inv[None, :]
    return jnp.tile(jnp.cos(ang), (1, SWA_HEADS)), jnp.tile(jnp.sin(ang), (1, SWA_HEADS))


_MIXER_GAINS = ('mix_norm', 'mla_q_a_norm', 'mla_kv_a_norm', 'mla_q_norm', 'mla_k_norm', 'swa_q_norm', 'swa_k_norm')


def _local_step(x, target, small, ex):
    s = x.shape[0]
    cos, sin = _rope_tables(s)
    row = lambda name, l: small[name][l][None, :]
    saved, bigs = [], []
    for l in range(DEPTH):
        big = ex.first_weights(l)
        sv = {'x0': x}
        x, sv['g1'], sv['u1'], got = _ffn_fwd(x, row('ffn1_norm', l), big['ffn1_w_gate'], big['ffn1_w_up'], big['ffn1_w_down'],
                                             f"ffn1_fwd_{l}", ex.gather_behind_ffn1(l))
        big.update(ex.rest_weights(l, got))
        bigs.append(big)
        sv['x1'] = x
        gains = [row(n, l) for n in _MIXER_GAINS]
        mixer_w = (big['w_in'], big['mla_w_q_b'], big['mla_w_kv_b'])
        q_a, k_a, v_a, q_b, k_b, v_b = _pre_fwd(x, gains, *mixer_w, cos, sin, f"pre_fwd_{l}")
        o_a, lse, got = _mla_fwd(q_a, k_a, v_a, f"mla_fwd_{l}", ex.gather_behind_mla(l))
        ex.gathered_behind_mla(l, got)
        kpad = jnp.pad(k_b, ((0, 0), (BLOCK, 0), (0, 0)))
        vpad = jnp.pad(v_b, ((0, 0), (BLOCK, 0), (0, 0)))
        sinks = jnp.broadcast_to(small['swa_sinks'][l].reshape(SWA_KV_HEADS, SWA_GROUP, 1), (SWA_KV_HEADS, SWA_GROUP, 128))
        o_b, lse_b = _swa_fwd(q_b, kpad, vpad, sinks, f"swa_fwd_{l}")
        sv.update(gains=gains, mixer_w=mixer_w, q_a=q_a, k_a=k_a, v_a=v_a, q_b=q_b, kpad=kpad, vpad=vpad, sinks=sinks,
                  o_a=o_a, lse=lse, o_b=o_b, lse_b=lse_b)
        x = _post_fwd(x, o_a, o_b, row('mla_out_norm', l), row('swa_out_norm', l), big['w_o'], f"post_fwd_{l}")
        sv['x2'] = x
        x, sv['g2'], sv['u2'], _ = _ffn_fwd(x, row('ffn2_norm', l), big['ffn2_w_gate'], big['ffn2_w_up'], big['ffn2_w_down'],
                                           f"ffn2_fwd_{l}")
        saved.append(sv)

    dx, loss = _loss_head(x, target, "loss_head")

    gs = {n: [None] * DEPTH for n in SMALL_NAMES}
    t = _attn_tile(s)
    for l in reversed(range(DEPTH)):
        sv, big = saved[l], bigs[l]

        def ffn_back(tag, xin, dy, gate, up, scatter=None):
            (dxi, dgain, nb, act, dgate, dup), got = _ffn_bwd(xin, dy, gate, up, row(tag + '_norm', l), big[tag + '_w_gate'],
                                                             big[tag + '_w_up'], big[tag + '_w_down'], f"{tag}_bwd_{l}", scatter)
            gs[tag + '_norm'][l] = dgain[0]
            group = _matmul_tn(dgate, nb, 1.0, f"{tag}_dwg_{l}", None, 0, 3)
            group = _matmul_tn(dup, nb, 1.0, f"{tag}_dwu_{l}", group, 1, 3)
            group = _matmul_tn(act, dy, 0.5, f"{tag}_dwd_{l}", group, 2, 3)
            return dxi, {(tag + '_w_gate', tag + '_w_up', tag + '_w_down'): group}, got

        dx, rest_grads, _ = ffn_back('ffn2', sv['x2'], dx, sv['g2'], sv['u2'])
        do_a, do_b, dga, dgb, dwo, delta = _post_bwd(dx, sv['o_a'], sv['o_b'], row('mla_out_norm', l), row('swa_out_norm', l),
                                              big['w_o'], f"post_bwd_{l}")
        gs['mla_out_norm'][l], gs['swa_out_norm'][l] = dga[0], dgb[0]
        rest_grads['w_o'] = dwo.reshape(N_SHARD, MIX_WIDTH // N_SHARD, D_MODEL)
        dq_a, dk_a, dv_a, got = _mla_bwd(sv['q_a'], sv['k_a'], sv['v_a'], do_a, sv['lse'], delta, f"mla_bwd_{l}",
                                         ex.scatter_behind_mla(l))
        ex.scattered_behind_mla(l, got)
        dq_b, dkpad, dvpad, dsink = _swa_bwd(sv['q_b'], sv['kpad'], sv['vpad'], sv['sinks'], sv['o_b'], sv['lse_b'], do_b,
                                             f"swa_bwd_{l}")
        gs['swa_sinks'][l] = dsink[:, :SWA_GROUP, 0].reshape(SWA_HEADS)
        cts = [dq_a, dk_a, dv_a, dq_b, dkpad[:, BLOCK:], dvpad[:, BLOCK:]]
        outs = _pre_bwd(sv['x1'], dx, cts, sv['gains'], *sv['mixer_w'], cos, sin, f"pre_bwd_{l}")
        dx = outs[0]
        for n, val in zip(_MIXER_GAINS, outs[1:8]):
            gs[n][l] = val[0]
        rest_grads['w_in'], rest_grads['mla_w_q_b'], rest_grads['mla_w_kv_b'] = outs[8:11]
        ex.grads_ready(l, 'rest', rest_grads)
        dx, first_grads, got = ffn_back('ffn1', sv['x0'], dx, sv['g1'], sv['u1'], ex.scatter_behind_ffn1(l))
        ex.scattered_behind_ffn1(l, got)
        ex.grads_ready(l, 'first', first_grads)
    return loss, dx, gs


class _Exchange:
    def __init__(self, weights, c, chip):
        halves_of = lambda a: a.reshape(a.shape[:-2] + (2, a.shape[-2] // 2, a.shape[-1]))
        self.halves_of, self.c, self.chip = halves_of, c, chip
        self.mine = [[halves_of(jnp.concatenate([weights[n][l].astype(BF16) for n in group], axis=0))
                      for group in FIRST_GROUPS + REST_GROUPS] for l in range(DEPTH)]
        self.ahead, self.begun, self.received = {}, {}, {}

    def _assembled(self, l, which, gathered):
        groups, base = (FIRST_GROUPS, 0) if which == 'first' else (REST_GROUPS, N_FIRST)
        big = {}
        for gi, group in enumerate(groups):
            offs = _group_row_offsets(group)
            _, rh, w = self.mine[l][base + gi].shape
            col_sharded = BIG[group[0]][1] == 1
            full = _assemble(gathered[gi].reshape(N_SHARD, 1, 2 * rh, w), self.mine[l][base + gi].reshape(1, 2 * rh, w),
                             f"assemble_{which}{gi}_{l}", col_sharded, MIXER_ORDERS.get(group[0]))
            for i, n in enumerate(group):
                rows = offs[i + 1] - offs[i]
                if len(group) > 1 and offs[i] % rows == 0:
                    big[n] = ((full, (None, rows, N_SHARD * w), (0, offs[i] // rows, 0)) if col_sharded else
                              (full, (None, N_SHARD, rows, w), (0, 0, offs[i] // rows, 0)))
                elif col_sharded:
                    big[n] = full[0, offs[i]:offs[i + 1]]
                else:
                    big[n] = full[0, :, offs[i]:offs[i + 1]].reshape(BIG[n][0])
        return big

    def first_weights(self, l):
        got = self.ahead[l][:N_FIRST] if l in self.ahead else _all_gather_halves(self.mine[l][:N_FIRST], f"gather_first_{l}")
        return self._assembled(l, 'first', got)

    def gather_behind_ffn1(self, l):
        return None if l in self.ahead else self.mine[l][N_FIRST:]

    def rest_weights(self, l, got):
        return self._assembled(l, 'rest', self.ahead[l][N_FIRST:] if l in self.ahead else got)

    def gather_behind_mla(self, l):
        return self.mine[l + 1] if l + 1 < DEPTH else None

    def gathered_behind_mla(self, l, got):
        if got:
            self.ahead[l + 1] = got

    def grads_ready(self, l, which, grads):
        groups = FIRST_GROUPS if which == 'first' else REST_GROUPS
        parts = [self.halves_of(grads[group] if group in grads else jnp.concatenate([grads[n] for n in group], axis=1))
                 for group in groups]
        from_sibling = _sibling_exchange(parts, f"swap_{which}_{l}", True)
        chip_sums = []
        for gi, (p, got) in enumerate(zip(parts, from_sibling)):
            kept = lax.dynamic_index_in_dim(p, self.c, axis=1, keepdims=False)
            rows = N_SHARD * p.shape[2]
            pair = _sum_blocks([kept.reshape(rows, -1), got.reshape(rows, -1)], BF16, f"sum_pair_{which}{gi}_{l}")
            chip_sums.append(pair.reshape(got.shape))
        self.begun[(l, which)] = chip_sums

    def scatter_behind_mla(self, l):
        return self.begun[(l + 1, 'first')] + self.begun[(l + 1, 'rest')] if l + 1 < DEPTH else None

    def scattered_behind_mla(self, l, got):
        if got:
            self.received[(l + 1, 'first')], self.received[(l + 1, 'rest')] = got[:N_FIRST], got[N_FIRST:]

    def scatter_behind_ffn1(self, l):
        return self.begun[(l, 'rest')] if l == 0 else None

    def scattered_behind_ffn1(self, l, got):
        if got:
            self.received[(l, 'rest')] = got

    def reduced(self):
        keys = sorted(self.begun)
        for key in keys:
            if key not in self.received:
                self.received[key] = _scatter_to_chips(self.begun[key], f"scatter_{key[1]}_{key[0]}")
        halves = []
        for l, which in keys:
            for gi, (cs, got) in enumerate(zip(self.begun[(l, which)], self.received[(l, which)])):
                own = lax.dynamic_index_in_dim(cs, self.chip, axis=0, keepdims=False)
                halves.append(_sum_blocks([own, got[0], got[1], got[2]], F32, f"sum_chips_{which}{gi}_{l}"))
        others = _sibling_exchange(halves, "share_halves", False)
        per_layer, at = {}, 0
        for l, which in keys:
            for group in (FIRST_GROUPS if which == 'first' else REST_GROUPS):
                mine_h, other_h = halves[at], others[at]
                at += 1
                full = jnp.where(self.c == 0, jnp.concatenate([mine_h, other_h]), jnp.concatenate([other_h, mine_h]))
                offs = _group_row_offsets(group)
                for i, n in enumerate(group):
                    per_layer[(n, l)] = full[offs[i]:offs[i + 1]]
        return {n: jnp.stack([per_layer[(n, l)] for l in range(DEPTH)]) for n in BIG_NAMES}


def kernel(x, ffn1_norm, ffn1_w_gate, ffn1_w_up, ffn1_w_down, mix_norm, w_in, mla_q_a_norm, mla_w_q_b, mla_kv_a_norm, mla_w_kv_b, mla_q_norm, mla_k_norm, swa_q_norm, swa_k_norm, swa_sinks, mla_out_norm, swa_out_norm, w_o, ffn2_norm, ffn2_w_gate, ffn2_w_up, ffn2_w_down, loss_target, m_ffn1_norm, m_ffn1_w_gate, m_ffn1_w_up, m_ffn1_w_down, m_mix_norm, m_w_in, m_mla_q_a_norm, m_mla_w_q_b, m_mla_kv_a_norm, m_mla_w_kv_b, m_mla_q_norm, m_mla_k_norm, m_swa_q_norm, m_swa_k_norm, m_swa_sinks, m_mla_out_norm, m_swa_out_norm, m_w_o, m_ffn2_norm, m_ffn2_w_gate, m_ffn2_w_up, m_ffn2_w_down, v_ffn1_norm, v_ffn1_w_gate, v_ffn1_w_up, v_ffn1_w_down, v_mix_norm, v_w_in, v_mla_q_a_norm, v_mla_w_q_b, v_mla_kv_a_norm, v_mla_w_kv_b, v_mla_q_norm, v_mla_k_norm, v_swa_q_norm, v_swa_k_norm, v_swa_sinks, v_mla_out_norm, v_swa_out_norm, v_w_o, v_ffn2_norm, v_ffn2_w_gate, v_ffn2_w_up, v_ffn2_w_down):
    args = dict(locals())
    transposed = lambda a: jnp.swapaxes(a, 1, 2)
    as_kernels_see = lambda n, a: transposed(a) if n in TRANSPOSED else a
    weights = {n: as_kernels_see(n, args[n]) for n in WEIGHT_NAMES}
    mom_m = {n: as_kernels_see(n, args["m_" + n]) for n in WEIGHT_NAMES}
    mom_v = {n: as_kernels_see(n, args["v_" + n]) for n in WEIGHT_NAMES}
    ex = _Exchange(weights, lax.axis_index("c"), 2 * lax.axis_index("x") + lax.axis_index("y"))
    loss, dx, gs = _local_step(x[0], loss_target[0], {n: weights[n] for n in SMALL_NAMES}, ex)

    small_flat = jnp.concatenate([jnp.stack(gs[n]).reshape(-1) for n in SMALL_NAMES] + [loss[0, :1]])
    n_small = small_flat.shape[0]
    lanes = -(-n_small // (8 * 128)) * 128
    small_sum = _all_reduce_small(jnp.pad(small_flat, (0, 8 * lanes - n_small)).reshape(8, lanes), "reduce_small").reshape(-1)
    loss_out = small_sum[n_small - 1]
    grads = ex.reduced()

    packed = lambda d: jnp.pad(jnp.concatenate([d[n].reshape(-1) for n in SMALL_NAMES]), (0, 8 * lanes - n_small + 1)).reshape(8, lanes)
    small_out = _adamw(packed(weights), small_sum.reshape(8, lanes), packed(mom_m), packed(mom_v), "adamw_small")
    deltas, new_m, new_v = {}, {}, {}
    off = 0
    for n in SMALL_NAMES:
        cnt = int(np.prod(weights[n].shape))
        grads[n], deltas[n], new_m[n], new_v[n] = (a.reshape(-1)[off:off + cnt].reshape(weights[n].shape)
                                                    for a in (small_sum, *small_out))
        off += cnt
    for n in BIG_NAMES:
        shp = weights[n].shape
        two_d = (shp[0] * shp[1], shp[2])
        d, nm, nv = _adamw(weights[n].reshape(two_d), grads[n].reshape(two_d), mom_m[n].reshape(two_d),
                           mom_v[n].reshape(two_d), f"adamw_{n}")
        deltas[n], new_m[n], new_v[n] = d.reshape(shp), nm.reshape(shp), nv.reshape(shp)
    for n in TRANSPOSED:
        grads[n], deltas[n], new_m[n], new_v[n] = (transposed(a) for a in (grads[n], deltas[n], new_m[n], new_v[n]))

    return (loss_out, dx[None], *[grads[n] for n in WEIGHT_NAMES], *[deltas[n] for n in WEIGHT_NAMES],
            *[new_m[n] for n in WEIGHT_NAMES], *[new_v[n] for n in WEIGHT_NAMES])
```

```python
import functools

import numpy as np
import jax
import jax.numpy as jnp
from jax import lax
from jax.experimental import pallas as pl
from jax.experimental.pallas import tpu as pltpu

F32 = jnp.float32
BF16 = jnp.bfloat16

D_MODEL = 1024
DEPTH = 2
EPS = 1e-6
ROPE_THETA = 10000.0
BLOCK = 128
MLA_HEADS = 4
MLA_Q_RANK = 256
MLA_KV_RANK = 128
MLA_NOPE = 128
MLA_ROPE = 64
MLA_V = 128
MLA_QK = MLA_NOPE + MLA_ROPE
MLA_WIDTH = MLA_HEADS * MLA_V
SWA_HEADS = 8
SWA_KV_HEADS = 2
SWA_GROUP = SWA_HEADS // SWA_KV_HEADS
SWA_HEAD_DIM = 64
SWA_WIDTH = SWA_HEADS * SWA_HEAD_DIM
MIX_WIDTH = MLA_WIDTH + SWA_WIDTH
IN_SPLITS = (MLA_Q_RANK, MLA_KV_RANK, MLA_ROPE, SWA_WIDTH, SWA_KV_HEADS * SWA_HEAD_DIM, SWA_KV_HEADS * SWA_HEAD_DIM)
IN_COLS = sum(IN_SPLITS)
IN_OFFS = tuple(int(v) for v in np.cumsum((0,) + IN_SPLITS))
D_FF = 2816
MLA_SCALE = MLA_QK ** -0.5
LOG2E = 1.4426950408889634
LN2 = 0.6931471805599453
MLA_QSCALE = MLA_SCALE * LOG2E
SWA_SCALE = SWA_HEAD_DIM ** -0.5
NEG = -1e30

ADAM_LR = 0.001
ADAM_B1 = 0.9
ADAM_B2 = 0.999
ADAM_EPS = 1e-08
ADAM_WD = 0.01
ADAM_STEP = 10

N_SHARD = 4
N_DEV = 8
VMEM_LIMIT = 56 * 1024 * 1024
MESH = pl.DeviceIdType.MESH

WEIGHT_NAMES = ['ffn1_norm', 'ffn1_w_gate', 'ffn1_w_up', 'ffn1_w_down', 'mix_norm', 'w_in', 'mla_q_a_norm', 'mla_w_q_b',
                'mla_kv_a_norm', 'mla_w_kv_b', 'mla_q_norm', 'mla_k_norm', 'swa_q_norm', 'swa_k_norm', 'swa_sinks',
                'mla_out_norm', 'swa_out_norm', 'w_o', 'ffn2_norm', 'ffn2_w_gate', 'ffn2_w_up', 'ffn2_w_down']
TRANSPOSED = ('ffn1_w_gate', 'ffn1_w_up', 'ffn2_w_gate', 'ffn2_w_up')
BIG = {'ffn1_w_gate': ((D_FF, D_MODEL), 0), 'ffn1_w_up': ((D_FF, D_MODEL), 0), 'ffn1_w_down': ((D_FF, D_MODEL), 0),
       'w_in': ((D_MODEL, IN_COLS), 1), 'mla_w_q_b': ((MLA_Q_RANK, MLA_HEADS * MLA_QK), 1),
       'mla_w_kv_b': ((MLA_KV_RANK, MLA_HEADS * (MLA_NOPE + MLA_V)), 1), 'w_o': ((MIX_WIDTH, D_MODEL), 0),
       'ffn2_w_gate': ((D_FF, D_MODEL), 0), 'ffn2_w_up': ((D_FF, D_MODEL), 0), 'ffn2_w_down': ((D_FF, D_MODEL), 0)}
BIG_NAMES = [n for n in WEIGHT_NAMES if n in BIG]
SMALL_NAMES = [n for n in WEIGHT_NAMES if n not in BIG]

_pallas_call = pl.pallas_call


def _params(**kw):
    return pltpu.CompilerParams(vmem_limit_bytes=VMEM_LIMIT, **kw)


def _full(shape):
    n = len(shape)
    return pl.BlockSpec(shape, lambda *_: (0,) * n)


def _resident(shape):
    n = len(shape)
    return pl.BlockSpec(shape, lambda *_: (0,) * n, pipeline_mode=pl.Buffered(1))


@jax.custom_vjp
def _mm(a, w):
    return jnp.dot(a.astype(BF16), w, preferred_element_type=F32)


def _mm_fwd(a, w):
    return _mm(a, w), w


def _mm_bwd(w, dy):
    return lax.dot_general(dy.astype(BF16), w, (((1,), (1,)), ((), ())), preferred_element_type=F32), None


_mm.defvjp(_mm_fwd, _mm_bwd)


def _dot_nt(a, b):
    return lax.dot_general(a, b, (((1,), (1,)), ((), ())), preferred_element_type=F32)


def _dot_tn(a, b):
    return lax.dot_general(a, b, (((0,), (0,)), ((), ())), preferred_element_type=F32)


def _rms(t, g):
    return t * lax.rsqrt(jnp.mean(t * t, axis=-1, keepdims=True) + EPS) * g


def _sigmoid(z):
    return 1.0 / (1.0 + jnp.exp(-z))


def _row_tile(s, want):
    return min(want, s)


def _divisor_tile(rows, cap):
    return max(d for d in range(16, min(rows, cap) + 1, 16) if rows % d == 0)


FF_CHUNK = D_FF


def _weight_operand(w):
    if isinstance(w, tuple):
        arr, block, index = w
        return arr, pl.BlockSpec(block, lambda *_: index, pipeline_mode=pl.Buffered(1))
    return w, _resident(w.shape)


def _weight_rows(ref, start, n):
    if len(ref.shape) == 2:
        return ref[start:start + n, :]
    per = ref.shape[1]
    return ref[start // per:(start + n) // per].reshape(n, ref.shape[2])


def _ffn_fwd(x, g, wg, wu, wd, name, gather=None):
    s = x.shape[0]
    tm = _row_tile(s, 256)
    steps = s // tm
    ng = len(gather) if gather else 0
    (wg, wg_spec), (wu, wu_spec), (wd, wd_spec) = _weight_operand(wg), _weight_operand(wu), _weight_operand(wd)

    def body(x_ref, g_ref, wg_ref, wu_ref, wd_ref, *rest):
        g_ins, (y_ref, gate_ref, up_ref), g_outs, sems = rest[:ng], rest[ng:ng + 3], rest[ng + 3:2 * ng + 3], rest[2 * ng + 3:]
        if ng:
            _run_stages_at(_gather_stages(g_ins, g_outs, *sems), [(0,), (steps * 3 // 8,), (steps * 11 // 16,), (steps - 1,)])
        xv = x_ref[...]
        nb = _rms(xv, g_ref[...]).astype(BF16)
        acc = xv
        for c in range(0, D_FF, FF_CHUNK):
            gate = _dot_nt(nb, _weight_rows(wg_ref, c, FF_CHUNK))
            up = _dot_nt(nb, _weight_rows(wu_ref, c, FF_CHUNK))
            gate_ref[:, c:c + FF_CHUNK] = gate.astype(BF16)
            up_ref[:, c:c + FF_CHUNK] = up.astype(BF16)
            act = (gate * _sigmoid(gate) * up).astype(BF16)
            acc = acc + 0.5 * jnp.dot(act, _weight_rows(wd_ref, c, FF_CHUNK), preferred_element_type=F32)
        y_ref[...] = acc

    outs = _pallas_call(
        body, name=name, grid=(steps,),
        in_specs=[pl.BlockSpec((tm, D_MODEL), lambda i: (i, 0)), _full((1, D_MODEL)), wg_spec, wu_spec, wd_spec] + [_HBM] * ng,
        out_specs=[pl.BlockSpec((tm, D_MODEL), lambda i: (i, 0)), pl.BlockSpec((tm, D_FF), lambda i: (i, 0)),
                   pl.BlockSpec((tm, D_FF), lambda i: (i, 0))] + [_HBM] * ng,
        out_shape=[jax.ShapeDtypeStruct((s, D_MODEL), F32), jax.ShapeDtypeStruct((s, D_FF), BF16),
                   jax.ShapeDtypeStruct((s, D_FF), BF16)] + (_gather_out_shapes(gather) if ng else []),
        scratch_shapes=_exchange_scratch(8, ng) if ng else [],
        compiler_params=_params(dimension_semantics=("arbitrary",)),
    )(x, g, wg, wu, wd, *(gather or []))
    return outs[0], outs[1], outs[2], outs[3:]


def _ffn_bwd(x, dy, gate, up, g, wg, wu, wd, name, scatter=None):
    s = x.shape[0]
    tm = _row_tile(s, 256)
    steps = s // tm
    ng = len(scatter) if scatter else 0
    (wg, wg_spec), (wu, wu_spec), (wd, wd_spec) = _weight_operand(wg), _weight_operand(wu), _weight_operand(wd)

    def body(x_ref, dy_ref, gate_ref, up_ref, g_ref, wg_ref, wu_ref, wd_ref, *rest):
        c_ins, (dx_ref, dgain_ref, n_ref, act_ref, dgate_ref, dup_ref) = rest[:ng], rest[ng:ng + 6]
        c_outs, sems = rest[ng + 6:2 * ng + 6], rest[2 * ng + 6:]
        if ng:
            _run_stages_at(_scatter_stages(c_ins, c_outs, *sems), [(0,), (steps - 1,)])
        i = pl.program_id(0)
        xv = x_ref[...]
        dyv = dy_ref[...]
        gv = g_ref[...]
        r = lax.rsqrt(jnp.mean(xv * xv, axis=-1, keepdims=True) + EPS)
        xh = xv * r
        n_ref[...] = (xh * gv).astype(BF16)
        dyh = (0.5 * dyv).astype(BF16)
        dn = jnp.zeros_like(xv)
        for c in range(0, D_FF, FF_CHUNK):
            dact = _dot_nt(dyh, _weight_rows(wd_ref, c, FF_CHUNK))
            gt = gate_ref[:, c:c + FF_CHUNK].astype(F32)
            u = up_ref[:, c:c + FF_CHUNK].astype(F32)
            sg = _sigmoid(gt)
            sl = gt * sg
            act_ref[:, c:c + FF_CHUNK] = (sl * u).astype(BF16)
            dup = (dact * sl).astype(BF16)
            dgate = (dact * u * (sg * (1.0 + gt * (1.0 - sg)))).astype(BF16)
            dup_ref[:, c:c + FF_CHUNK] = dup
            dgate_ref[:, c:c + FF_CHUNK] = dgate
            dn = (dn + jnp.dot(dgate, _weight_rows(wg_ref, c, FF_CHUNK), preferred_element_type=F32)
                  + jnp.dot(dup, _weight_rows(wu_ref, c, FF_CHUNK), preferred_element_type=F32))
        part = jnp.sum(dn * xh, axis=0, keepdims=True)

        @pl.when(i == 0)
        def _():
            dgain_ref[...] = part

        @pl.when(i > 0)
        def _():
            dgain_ref[...] += part

        dxh = dn * gv
        dx_ref[...] = dyv + r * (dxh - xh * jnp.mean(dxh * xh, axis=-1, keepdims=True))

    row = lambda w: pl.BlockSpec((tm, w), lambda i: (i, 0))
    outs = _pallas_call(
        body, name=name, grid=(steps,),
        in_specs=[row(D_MODEL), row(D_MODEL), row(D_FF), row(D_FF), _full((1, D_MODEL)), wg_spec, wu_spec, wd_spec]
        + [_HBM] * ng,
        out_specs=[row(D_MODEL), _full((1, D_MODEL)), row(D_MODEL), row(D_FF), row(D_FF), row(D_FF)] + [_HBM] * ng,
        out_shape=[jax.ShapeDtypeStruct((s, D_MODEL), F32), jax.ShapeDtypeStruct((1, D_MODEL), F32),
                   jax.ShapeDtypeStruct((s, D_MODEL), BF16), jax.ShapeDtypeStruct((s, D_FF), BF16),
                   jax.ShapeDtypeStruct((s, D_FF), BF16), jax.ShapeDtypeStruct((s, D_FF), BF16)]
        + (_scatter_out_shapes(scatter) if ng else []),
        scratch_shapes=_exchange_scratch(3, ng) if ng else [],
        compiler_params=_params(dimension_semantics=("arbitrary",)),
    )(x, dy, gate, up, g, wg, wu, wd, *(scatter or []))
    return outs[:6], outs[6:]


def _store_col_shards(o_ref, acc, first_shard, n_here, width):
    for q in range(n_here):
        o_ref[q] = acc[:, (first_shard + q) * width:(first_shard + q + 1) * width].astype(BF16)


def _matmul_tn(a, b, scale, name, group=None, slot=0, slots=1):
    t, m = a.shape
    n = b.shape[1]
    tk = _row_tile(t, 2048)
    tn = n // 2
    nk = t // tk
    per = m // N_SHARD

    def body(a_ref, b_ref, *rest):
        o_ref, acc_ref = rest[-2:]
        k = pl.program_id(1)
        bv = b_ref[...]
        if scale != 1.0:
            bv = bv.astype(F32) * scale
        part = _dot_tn(a_ref[...].astype(BF16), bv.astype(BF16))

        @pl.when(k == 0)
        def _():
            acc_ref[...] = part

        @pl.when(k > 0)
        def _():
            acc_ref[...] += part

        @pl.when(k == nk - 1)
        def _():
            for sh in range(N_SHARD):
                o_ref[sh] = acc_ref[sh * per:(sh + 1) * per, :].astype(BF16)

    earlier = [] if group is None else [group]
    return _pallas_call(
        body, name=name, grid=(n // tn, nk),
        in_specs=[pl.BlockSpec((tk, m), lambda j, k: (k, 0)), pl.BlockSpec((tk, tn), lambda j, k: (k, j))]
        + [pl.BlockSpec(memory_space=pl.ANY)] * len(earlier),
        out_specs=pl.BlockSpec((N_SHARD, per, tn), lambda j, k: (0, slot, j)),
        out_shape=jax.ShapeDtypeStruct((N_SHARD, slots * per, n), BF16),
        scratch_shapes=[pltpu.VMEM((m, tn), F32)], input_output_aliases={2: 0} if earlier else {},
        compiler_params=_params(dimension_semantics=("arbitrary", "arbitrary")),
    )(a, b, *earlier)


_HALF = SWA_HEAD_DIM // 2
_IN_ORDER = (list(range(0, IN_OFFS[2]))
             + [IN_OFFS[3] + SWA_HEAD_DIM * h + i for h in range(SWA_HEADS) for i in range(_HALF)]
             + [IN_OFFS[3] + SWA_HEAD_DIM * h + _HALF + i for h in range(SWA_HEADS) for i in range(_HALF)]
             + list(range(IN_OFFS[5], IN_OFFS[6]))
             + [IN_OFFS[4] + SWA_HEAD_DIM * j + i for j in range(SWA_KV_HEADS) for i in range(_HALF)]
             + [IN_OFFS[4] + SWA_HEAD_DIM * j + _HALF + i for j in range(SWA_KV_HEADS) for i in range(_HALF)]
             + list(range(IN_OFFS[2], IN_OFFS[3])))
_QB_ORDER = ([MLA_QK * h + i for h in range(MLA_HEADS) for i in range(MLA_NOPE)]
             + [MLA_QK * h + MLA_NOPE + i for h in range(MLA_HEADS) for i in range(_HALF)]
             + [MLA_QK * h + MLA_NOPE + _HALF + i for h in range(MLA_HEADS) for i in range(_HALF)])
_KVB_ORDER = ([(MLA_NOPE + MLA_V) * h + i for h in range(MLA_HEADS) for i in range(MLA_NOPE)]
              + [(MLA_NOPE + MLA_V) * h + MLA_NOPE + i for h in range(MLA_HEADS) for i in range(MLA_V)])
MIXER_ORDERS = {'w_in': _IN_ORDER, 'mla_w_q_b': _QB_ORDER, 'mla_w_kv_b': _KVB_ORDER}
_P_CQ, _P_CKV, _P_QA, _P_QB, _P_VS, _P_KA, _P_KB, _P_PE = (int(v) for v in np.cumsum(
    (0, MLA_Q_RANK, MLA_KV_RANK, SWA_WIDTH // 2, SWA_WIDTH // 2, IN_SPLITS[5], IN_SPLITS[4] // 2, IN_SPLITS[4] // 2)))


def _runs(order):
    out, start = [], 0
    for i in range(1, len(order) + 1):
        if i == len(order) or order[i] != order[i - 1] + 1:
            out.append((order[start], i - start))
            start = i
    return out


def _inverse(order):
    inv = [0] * len(order)
    for new, old in enumerate(order):
        inv[old] = new
    return inv


def _take_cols(a, order):
    return jnp.concatenate([a[..., st:st + w] for st, w in _runs(order)], axis=-1)


def _segment_matrix(n, seg):
    return (lax.broadcasted_iota(jnp.int32, (n, n), 0) // seg == lax.broadcasted_iota(jnp.int32, (n, n), 1) // seg).astype(BF16)


@jax.custom_vjp
def _cmm(t, b, bt):
    hi = t.astype(BF16)
    lo = (t - hi.astype(F32)).astype(BF16)
    return jnp.dot(hi, b, preferred_element_type=F32) + jnp.dot(lo, b, preferred_element_type=F32)


def _cmm_fwd(t, b, bt):
    return _cmm(t, b, bt), (b, bt)


def _cmm_bwd(res, dy):
    b, bt = res
    return _cmm(dy, bt, b), None, None


_cmm.defvjp(_cmm_fwd, _cmm_bwd)


def _segsum(t, b):
    return _cmm(t, b, b)


def _rowsum(t):
    n = t.shape[-1]
    return _cmm(t, jnp.ones((n, 128), BF16), jnp.ones((128, n), BF16))


def _by_head(vals, width):
    lane = lax.broadcasted_iota(jnp.int32, (vals[0].shape[0], len(vals) * width), 1)
    out = vals[-1]
    for hd in range(len(vals) - 2, -1, -1):
        out = jnp.where(lane < (hd + 1) * width, vals[hd], out)
    return out


def _rope2(a, b, cos, sin):
    return a * cos - b * sin, b * cos + a * sin


def _pre_math(x, gm, gqa, gkva, gq, gk, gsq, gsk, taps, win, wqb, wkvb, cos, sin):
    h = _rms(x, gm)
    proj = _mm(h, win)
    if taps is not None:
        proj = proj + taps[0]
    cqn = _rms(proj[:, _P_CQ:_P_CKV], gqa)
    qa_all = _mm(cqn, wqb)
    ckvn = _rms(proj[:, _P_CKV:_P_QA], gkva)
    kv_all = _mm(ckvn, wkvb)
    if taps is not None:
        qa_all = qa_all + taps[1]
        kv_all = kv_all + taps[2]
    nh, hw = MLA_HEADS, MLA_HEADS * _HALF
    seg_mla = _segment_matrix(hw, _HALF)
    tile = lambda g, n: jnp.concatenate([g] * n, axis=-1)
    c4, s4 = cos[:, :hw], sin[:, :hw]

    def mla_heads(nope, r1, r2, gain):
        rr = r1 * r1 + r2 * r2
        lane_head = lax.broadcasted_iota(jnp.int32, (hw, nh * MLA_NOPE), 0) // _HALF
        spread = (lane_head == lax.broadcasted_iota(jnp.int32, (hw, nh * MLA_NOPE), 1) // MLA_NOPE).astype(BF16)
        rope_on_nope = _cmm(rr, spread, spread.T)
        ss_nope = [_rowsum(jnp.square(nope[:, hd * MLA_NOPE:(hd + 1) * MLA_NOPE])) for hd in range(nh)]
        rinv = [lax.rsqrt((ss_nope[hd] + rope_on_nope[:, hd * MLA_NOPE:(hd + 1) * MLA_NOPE]) * (1.0 / MLA_QK) + EPS)
                for hd in range(nh)]
        rl = lax.rsqrt((_segsum(rr, seg_mla) + _by_head(ss_nope, _HALF)) * (1.0 / MLA_QK) + EPS)
        o1, o2 = _rope2(r1 * rl * tile(gain[:, MLA_NOPE:MLA_NOPE + _HALF], nh), r2 * rl * tile(gain[:, MLA_NOPE + _HALF:], nh), c4, s4)
        return [jnp.concatenate([nope[:, hd * MLA_NOPE:(hd + 1) * MLA_NOPE] * rinv[hd] * gain[:, :MLA_NOPE],
                                 o1[:, hd * _HALF:(hd + 1) * _HALF], o2[:, hd * _HALF:(hd + 1) * _HALF]], axis=-1)
                for hd in range(nh)]

    q_a = mla_heads(qa_all[:, :nh * MLA_NOPE], qa_all[:, nh * MLA_NOPE:nh * MLA_NOPE + hw], qa_all[:, nh * MLA_NOPE + hw:], gq)
    pe1, pe2 = proj[:, _P_PE:_P_PE + _HALF], proj[:, _P_PE + _HALF:_P_PE + 2 * _HALF]
    k_a = mla_heads(kv_all[:, :nh * MLA_NOPE], tile(pe1, nh), tile(pe2, nh), gk)
    v_a = [kv_all[:, nh * MLA_NOPE + hd * MLA_V:nh * MLA_NOPE + (hd + 1) * MLA_V] for hd in range(nh)]

    def swa_heads(a, b, gain, n):
        w = n * _HALF
        r = lax.rsqrt(_segsum(a * a + b * b, _segment_matrix(w, _HALF)) * (1.0 / SWA_HEAD_DIM) + EPS)
        o1, o2 = _rope2(a * r * tile(gain[:, :_HALF], n), b * r * tile(gain[:, _HALF:], n), cos[:, :w], sin[:, :w])
        return [jnp.concatenate([o1[:, hd * _HALF:(hd + 1) * _HALF], o2[:, hd * _HALF:(hd + 1) * _HALF]], axis=-1) for hd in range(n)]

    q_b = swa_heads(proj[:, _P_QA:_P_QB], proj[:, _P_QB:_P_VS], gsq, SWA_HEADS)
    k_b = swa_heads(proj[:, _P_KA:_P_KB], proj[:, _P_KB:_P_PE], gsk, SWA_KV_HEADS)
    v_b = [proj[:, _P_VS + j * SWA_HEAD_DIM:_P_VS + (j + 1) * SWA_HEAD_DIM] for j in range(SWA_KV_HEADS)]
    return (q_a, k_a, v_a, q_b, k_b, v_b), (h, cqn, ckvn)


_PRE_GAIN_WIDTHS = (D_MODEL, MLA_Q_RANK, MLA_KV_RANK, MLA_QK, MLA_QK, SWA_HEAD_DIM, SWA_HEAD_DIM)
_PRE_HEADS = ((MLA_HEADS, MLA_QK), (MLA_HEADS, MLA_QK), (MLA_HEADS, MLA_V),
              (SWA_HEADS, SWA_HEAD_DIM), (SWA_KV_HEADS, SWA_HEAD_DIM), (SWA_KV_HEADS, SWA_HEAD_DIM))


def _pre_fwd(x, gains, win, wqb, wkvb, cos, sin, name):
    s = x.shape[0]
    tm = _row_tile(s, 512)

    def body(x_ref, *refs):
        g_refs, (win_ref, wqb_ref, wkvb_ref, cos_ref, sin_ref), out_refs = refs[:7], refs[7:12], refs[12:]
        outs, _ = _pre_math(x_ref[...], *[g[...] for g in g_refs], None, win_ref[...], wqb_ref[...], wkvb_ref[...],
                            cos_ref[...], sin_ref[...])
        for idx, (ref, heads) in enumerate(zip(out_refs, outs)):
            for hd, val in enumerate(heads):
                ref[hd] = (val * MLA_QSCALE if idx == 0 else val).astype(BF16)

    heads_spec = lambda nh, w: pl.BlockSpec((nh, tm, w), lambda i: (0, i, 0))
    return _pallas_call(
        body, name=name, grid=(s // tm,),
        in_specs=[pl.BlockSpec((tm, D_MODEL), lambda i: (i, 0))] + [_full((1, w)) for w in _PRE_GAIN_WIDTHS]
        + [_resident(win.shape), _resident(wqb.shape), _resident(wkvb.shape),
           pl.BlockSpec((tm, SWA_HEADS * _HALF), lambda i: (i, 0)), pl.BlockSpec((tm, SWA_HEADS * _HALF), lambda i: (i, 0))],
        out_specs=[heads_spec(nh, w) for nh, w in _PRE_HEADS],
        out_shape=[jax.ShapeDtypeStruct((nh, s, w), BF16) for nh, w in _PRE_HEADS],
        compiler_params=_params(dimension_semantics=("arbitrary",)),
    )(x, *gains, win, wqb, wkvb, cos, sin)


def _pre_bwd(x, dx_res, cts, gains, win, wqb, wkvb, cos, sin, name):
    s = x.shape[0]
    tm = _row_tile(s, 256)
    tap_widths = (IN_COLS, MLA_HEADS * MLA_QK, MLA_HEADS * (MLA_NOPE + MLA_V))

    def body(x_ref, dxr_ref, *refs):
        ct_refs, g_refs = refs[:6], refs[6:13]
        win_ref, wqb_ref, wkvb_ref, cos_ref, sin_ref = refs[13:18]
        dx_ref, dg_refs, dw_refs, acc_refs = refs[18], refs[19:26], refs[26:29], refs[29:32]
        i = pl.program_id(0)
        win_v, wqb_v, wkvb_v, cos_v, sin_v = win_ref[...], wqb_ref[...], wkvb_ref[...], cos_ref[...], sin_ref[...]

        def f(xv, gm, gqa, gkva, gq, gk, gsq, gsk, t0, t1, t2):
            return _pre_math(xv, gm, gqa, gkva, gq, gk, gsq, gsk, (t0, t1, t2), win_v, wqb_v, wkvb_v, cos_v, sin_v)

        taps = [jnp.zeros((tm, w), F32) for w in tap_widths]
        _, vjp, acts = jax.vjp(f, x_ref[...], *[g[...] for g in g_refs], *taps, has_aux=True)
        ct = tuple([ref[hd] for hd in range(nh)] for ref, (nh, _) in zip(ct_refs, _PRE_HEADS))
        grads = vjp(ct)
        dx_ref[...] = grads[0] + dxr_ref[...]
        dws = [_dot_tn(a.astype(BF16), t.astype(BF16)) for a, t in zip(acts, grads[8:11])]

        @pl.when(i == 0)
        def _():
            for ref, val in zip(dg_refs, grads[1:8]):
                ref[...] = val
            for ref, val in zip(acc_refs, dws):
                ref[...] = val

        @pl.when(i > 0)
        def _():
            for ref, val in zip(dg_refs, grads[1:8]):
                ref[...] += val
            for ref, val in zip(acc_refs, dws):
                ref[...] += val

        @pl.when(i == s // tm - 1)
        def _():
            for ref, acc, order in zip(dw_refs, acc_refs, (_IN_ORDER, _QB_ORDER, _KVB_ORDER)):
                _store_col_shards(ref, _take_cols(acc[...], _inverse(order)), 0, N_SHARD, acc.shape[1] // N_SHARD)

    heads_spec = lambda nh, w: pl.BlockSpec((nh, tm, w), lambda i: (0, i, 0))
    row = pl.BlockSpec((tm, D_MODEL), lambda i: (i, 0))
    half = pl.BlockSpec((tm, SWA_HEADS * _HALF), lambda i: (i, 0))
    shard_shapes = [(N_SHARD, w.shape[0], w.shape[1] // N_SHARD) for w in (win, wqb, wkvb)]
    return _pallas_call(
        body, name=name, grid=(s // tm,),
        in_specs=[row, row] + [heads_spec(nh, w) for nh, w in _PRE_HEADS] + [_full((1, w)) for w in _PRE_GAIN_WIDTHS]
        + [_resident(win.shape), _resident(wqb.shape), _resident(wkvb.shape), half, half],
        out_specs=[row] + [_full((1, w)) for w in _PRE_GAIN_WIDTHS] + [_full(shp) for shp in shard_shapes],
        out_shape=[jax.ShapeDtypeStruct((s, D_MODEL), F32)] + [jax.ShapeDtypeStruct((1, w), F32) for w in _PRE_GAIN_WIDTHS]
        + [jax.ShapeDtypeStruct(shp, BF16) for shp in shard_shapes],
        scratch_shapes=[pltpu.VMEM(w.shape, F32) for w in (win, wqb, wkvb)],
        compiler_params=_params(dimension_semantics=("arbitrary",)),
    )(x, dx_res, *cts, *gains, win, wqb, wkvb, cos, sin)


def _post_math(oa, ob, ga, gb, wo):
    mixed = jnp.concatenate([_rms(jnp.concatenate(oa, axis=-1), ga), _rms(jnp.concatenate(ob, axis=-1), gb)], axis=-1)
    return _mm(mixed, wo), mixed


def _post_fwd(x, oa, ob, ga, gb, wo, name):
    s = x.shape[0]
    tm = _row_tile(s, 512)

    def body(x_ref, oa_ref, ob_ref, ga_ref, gb_ref, wo_ref, y_ref):
        y, _ = _post_math([oa_ref[hd] for hd in range(MLA_HEADS)], [ob_ref[hd] for hd in range(SWA_HEADS)],
                          ga_ref[...], gb_ref[...], wo_ref[...])
        y_ref[...] = x_ref[...] + y

    row = pl.BlockSpec((tm, D_MODEL), lambda i: (i, 0))
    return _pallas_call(
        body, name=name, grid=(s // tm,),
        in_specs=[row, pl.BlockSpec((MLA_HEADS, tm, MLA_V), lambda i: (0, i, 0)),
                  pl.BlockSpec((SWA_HEADS, tm, SWA_HEAD_DIM), lambda i: (0, i, 0)),
                  _full((1, MLA_WIDTH)), _full((1, SWA_WIDTH)), _resident(wo.shape)],
        out_specs=row, out_shape=jax.ShapeDtypeStruct((s, D_MODEL), F32),
        compiler_params=_params(dimension_semantics=("arbitrary",)),
    )(x, oa, ob, ga, gb, wo)


def _post_bwd(dy, oa, ob, ga, gb, wo, name):
    s = dy.shape[0]
    tm = _row_tile(s, 512)
    t = _attn_tile(s)

    def body(dy_ref, oa_ref, ob_ref, ga_ref, gb_ref, wo_ref, doa_ref, dob_ref, dga_ref, dgb_ref, dwo_ref, delta_ref, acc_ref):
        i = pl.program_id(0)
        wo_v = wo_ref[...]
        dyv = dy_ref[...]

        def f(oa_l, ob_l, ga_v, gb_v):
            return _post_math(oa_l, ob_l, ga_v, gb_v, wo_v)

        _, vjp, mixed = jax.vjp(f, [oa_ref[hd] for hd in range(MLA_HEADS)], [ob_ref[hd] for hd in range(SWA_HEADS)],
                                ga_ref[...], gb_ref[...], has_aux=True)
        doa, dob, dga, dgb = vjp(dyv)
        for hd in range(MLA_HEADS):
            doa_ref[hd] = doa[hd]
            rows = _as_rows(_rowsum(doa[hd] * oa_ref[hd]))
            for j in range(tm // t):
                delta_ref[hd, j] = rows[:, j * t:(j + 1) * t]
        for hd in range(SWA_HEADS):
            dob_ref[hd] = dob[hd]
        dwo = _dot_tn(mixed.astype(BF16), dyv.astype(BF16))

        @pl.when(i == 0)
        def _():
            dga_ref[...] = dga
            dgb_ref[...] = dgb
            acc_ref[...] = dwo

        @pl.when(i > 0)
        def _():
            dga_ref[...] += dga
            dgb_ref[...] += dgb
            acc_ref[...] += dwo

        @pl.when(i == s // tm - 1)
        def _():
            dwo_ref[...] = acc_ref[...].astype(BF16)

    row = pl.BlockSpec((tm, D_MODEL), lambda i: (i, 0))
    oa_spec = pl.BlockSpec((MLA_HEADS, tm, MLA_V), lambda i: (0, i, 0))
    ob_spec = pl.BlockSpec((SWA_HEADS, tm, SWA_HEAD_DIM), lambda i: (0, i, 0))
    return _pallas_call(
        body, name=name, grid=(s // tm,),
        in_specs=[row, oa_spec, ob_spec, _full((1, MLA_WIDTH)), _full((1, SWA_WIDTH)), _resident(wo.shape)],
        out_specs=[oa_spec, ob_spec, _full((1, MLA_WIDTH)), _full((1, SWA_WIDTH)), _full(wo.shape),
                   pl.BlockSpec((MLA_HEADS, tm // t, 8, t), lambda i: (0, i, 0, 0))],
        out_shape=[jax.ShapeDtypeStruct((MLA_HEADS, s, MLA_V), F32), jax.ShapeDtypeStruct((SWA_HEADS, s, SWA_HEAD_DIM), F32),
                   jax.ShapeDtypeStruct((1, MLA_WIDTH), F32), jax.ShapeDtypeStruct((1, SWA_WIDTH), F32),
                   jax.ShapeDtypeStruct(wo.shape, BF16), jax.ShapeDtypeStruct((MLA_HEADS, s // t, 8, t), F32)],
        scratch_shapes=[pltpu.VMEM(wo.shape, F32)],
        compiler_params=_params(dimension_semantics=("arbitrary",)),
    )(dy, oa, ob, ga, gb, wo)


def _attn_tile(s):
    return 512 if s >= 2048 else 128


def _as_rows(cols):
    return cols.T[0:8, :]


def _causal_mask(t):
    return lax.broadcasted_iota(jnp.int32, (t, t), 1) <= lax.broadcasted_iota(jnp.int32, (t, t), 0)


def _pipelined_blocks(first, count, last_block, issue, consume, carry, prefetch_after):
    def clamped(j, slot):
        issue(jnp.minimum(j, last_block), slot)

    def pair(jj, c):
        a = first + 2 * jj
        clamped(a + 1, 1)
        c = consume(a, 0, c)
        clamped(a + 2, 0)
        return consume(a + 1, 1, c)

    clamped(first, 0)
    npairs = count // 2
    carry = lax.fori_loop(0, npairs, pair, carry)

    def odd(c):
        c = consume(first + 2 * npairs, 0, c)
        if prefetch_after:
            clamped(first + count, 0)
        return c

    return lax.cond(count - 2 * npairs == 1, odd, lambda c: c, carry)


def _run_stages_at(stages, steps):
    for stage, step in zip(stages, steps):
        here = pl.program_id(0) == step[0]
        for axis in range(1, len(step)):
            here = here & (pl.program_id(axis) == step[axis])
        pl.when(here)(stage)


def _mla_fwd(q, k, v, name, gather=None):
    nh, s, _ = q.shape
    t = _attn_tile(s)
    nq = s // t
    ng = len(gather) if gather else 0

    def body(q_ref, k_ref, v_ref, *rest):
        g_ins, (o_ref, lse_ref), g_outs = rest[:ng], rest[ng:ng + 2], rest[ng + 2:2 * ng + 2]
        (s0_ref, s1_ref), sems = rest[2 * ng + 2:2 * ng + 4], rest[2 * ng + 4:]
        if ng:
            _run_stages_at(_gather_stages(g_ins, g_outs, *sems), [(0, 0), (nh // 2, 0), (nh - 1, 0), (nh - 1, nq - 1)])
        qi = pl.program_id(1)
        qv = q_ref[...]
        s_refs = (s0_ref, s1_ref)

        def rows(j):
            return pl.ds(pl.multiple_of(j * t, t), t)

        def issue(j, slot):
            s_refs[slot][...] = _dot_nt(qv, k_ref[rows(j), :])

        def consume(j, slot, carry, masked=False):
            m, l, acc = carry
            sc = s_refs[slot][...]
            if masked:
                sc = jnp.where(_causal_mask(t), sc, NEG)
            m_new = jnp.maximum(m, jnp.max(sc, axis=-1, keepdims=True))
            alpha = jnp.exp2(m - m_new)
            p = jnp.exp2(sc - m_new)
            l = alpha * l + jnp.sum(p, axis=-1, keepdims=True)
            acc = alpha * acc + jnp.dot(p.astype(BF16), v_ref[rows(j), :], preferred_element_type=F32)
            return m_new, l, acc

        init = (jnp.full((t, 1), NEG, F32), jnp.zeros((t, 1), F32), jnp.zeros((t, MLA_V), F32))
        carry = _pipelined_blocks(0, qi, nq - 1, issue, consume, init, True)
        m, l, acc = consume(qi, 0, carry, masked=True)
        o_ref[...] = acc / l
        lse_ref[...] = _as_rows(jnp.broadcast_to(m + jnp.log2(l), (t, 128)))

    outs = _pallas_call(
        body, name=name, grid=(nh, nq),
        in_specs=[pl.BlockSpec((None, t, MLA_QK), lambda h, i: (h, i, 0)), pl.BlockSpec((None, s, MLA_QK), lambda h, i: (h, 0, 0)),
                  pl.BlockSpec((None, s, MLA_V), lambda h, i: (h, 0, 0))] + [_HBM] * ng,
        out_specs=[pl.BlockSpec((None, t, MLA_V), lambda h, i: (h, i, 0)), pl.BlockSpec((None, None, 8, t), lambda h, i: (h, i, 0, 0))]
        + [_HBM] * ng,
        out_shape=[jax.ShapeDtypeStruct((nh, s, MLA_V), F32), jax.ShapeDtypeStruct((nh, nq, 8, t), F32)]
        + (_gather_out_shapes(gather) if ng else []),
        scratch_shapes=[pltpu.VMEM((t, t), F32)] * 2 + (_exchange_scratch(8, ng) if ng else []),
        compiler_params=_params(dimension_semantics=("arbitrary", "arbitrary")),
    )(q, k, v, *(gather or []))
    return outs[0], outs[1], outs[2:]


def _mla_bwd(q, k, v, do, lse_row, delta_row, name, scatter=None):
    nh, s, _ = q.shape
    t = _attn_tile(s)
    nq = s // t
    ng = len(scatter) if scatter else 0

    def body(q_ref, k_ref, v_ref, do_ref, lse_ref, delta_ref, *rest):
        c_ins, (dq_ref, dk_ref, dv_ref), c_outs = rest[:ng], rest[ng:ng + 3], rest[ng + 3:2 * ng + 3]
        (s0_ref, s1_ref, dp0_ref, dp1_ref), sems = rest[2 * ng + 3:2 * ng + 7], rest[2 * ng + 7:]
        if ng:
            _run_stages_at(_scatter_stages(c_ins, c_outs, *sems), [(0, 0), (nh - 1, nq - 1)])
        kj = pl.program_id(1)
        kv_, vv = k_ref[...], v_ref[...]
        s_refs, dp_refs = (s0_ref, s1_ref), (dp0_ref, dp1_ref)

        @pl.when(kj == 0)
        def _():
            dq_ref[...] = jnp.zeros_like(dq_ref)

        def rows(i):
            return pl.ds(pl.multiple_of(i * t, t), t)

        def issue(i, slot):
            s_refs[slot][...] = _dot_nt(kv_, q_ref[rows(i), :])
            dp_refs[slot][...] = _dot_nt(vv, do_ref[rows(i), :].astype(BF16))

        def consume(i, slot, carry, masked=False):
            dk, dv = carry
            p = jnp.exp2(s_refs[slot][...] - lse_ref[i][0:1, :])
            if masked:
                p = jnp.where(lax.broadcasted_iota(jnp.int32, (t, t), 0) <= lax.broadcasted_iota(jnp.int32, (t, t), 1), p, 0.0)
            dv = dv + jnp.dot(p.astype(BF16), do_ref[rows(i), :].astype(BF16), preferred_element_type=F32)
            ds = (p * (dp_refs[slot][...] - delta_ref[i][0:1, :])).astype(BF16)
            dk = dk + jnp.dot(ds, q_ref[rows(i), :], preferred_element_type=F32)
            dq_ref[rows(i), :] += _dot_tn(ds, kv_) * MLA_SCALE
            return dk, dv

        issue(kj, 0)
        carry = consume(kj, 0, (jnp.zeros((t, MLA_QK), F32), jnp.zeros((t, MLA_V), F32)), masked=True)
        dk, dv = _pipelined_blocks(kj + 1, nq - 1 - kj, nq - 1, issue, consume, carry, False)
        dk_ref[...] = dk * LN2
        dv_ref[...] = dv

    tile = lambda w: pl.BlockSpec((None, t, w), lambda h, j: (h, j, 0))
    whole = lambda w: pl.BlockSpec((None, s, w), lambda h, j: (h, 0, 0))
    rows_spec = pl.BlockSpec((None, nq, 8, t), lambda h, j: (h, 0, 0, 0))
    outs = _pallas_call(
        body, name=name, grid=(nh, nq),
        in_specs=[whole(MLA_QK), tile(MLA_QK), tile(MLA_V), whole(MLA_V), rows_spec, rows_spec] + [_HBM] * ng,
        out_specs=[whole(MLA_QK), tile(MLA_QK), tile(MLA_V)] + [_HBM] * ng,
        out_shape=[jax.ShapeDtypeStruct((nh, s, MLA_QK), F32), jax.ShapeDtypeStruct((nh, s, MLA_QK), F32),
                   jax.ShapeDtypeStruct((nh, s, MLA_V), F32)] + (_scatter_out_shapes(scatter) if ng else []),
        scratch_shapes=[pltpu.VMEM((t, t), F32)] * 4 + (_exchange_scratch(3, ng) if ng else []),
        compiler_params=_params(dimension_semantics=("arbitrary", "arbitrary")),
    )(q, k, v, do, lse_row, delta_row, *(scatter or []))
    return outs[0], outs[1], outs[2], outs[3:]


def _swa_tile(s):
    return min(s, 8 * BLOCK)


def _swa_specs(tq):
    nb = tq // BLOCK
    grp = lambda w: pl.BlockSpec((SWA_GROUP, tq, w), lambda j, i: (j, i, 0))
    main = pl.BlockSpec((None, tq, SWA_HEAD_DIM), lambda j, i: (j, i, 0))
    tail = pl.BlockSpec((None, BLOCK, SWA_HEAD_DIM), lambda j, i: (j, nb * (i + 1), 0))
    sink = pl.BlockSpec((None, SWA_GROUP, 128), lambda j, i: (j, 0, 0))
    return grp, main, tail, sink


def _swa_band_mask(first):
    shape = (SWA_GROUP * BLOCK, 2 * BLOCK)
    q_rel = (lax.broadcasted_iota(jnp.int32, shape, 0) & (BLOCK - 1)) + BLOCK
    k_rel = lax.broadcasted_iota(jnp.int32, shape, 1)
    dist = q_rel - k_rel
    return (dist >= 0) & (dist < BLOCK) & ((k_rel >= BLOCK) | jnp.logical_not(first))


def _swa_sink_column(sink_ref):
    sk = sink_ref[...]
    return jnp.concatenate([jnp.broadcast_to(sk[g:g + 1, 0:1], (BLOCK, 1)) for g in range(SWA_GROUP)], axis=0)


def _swa_fwd(q, kpad, vpad, sinks, name):
    _, s, _ = q.shape
    tq = _swa_tile(s)
    grp, main, tail, sink = _swa_specs(tq)
    d = SWA_HEAD_DIM

    def body(q_ref, km_ref, kt_ref, vm_ref, vt_ref, sink_ref, o_ref, lse_ref):
        i = pl.program_id(1)
        kall = jnp.concatenate([km_ref[...], kt_ref[...]], axis=0)
        vall = jnp.concatenate([vm_ref[...], vt_ref[...]], axis=0)
        sink_col = _swa_sink_column(sink_ref)
        for b in range(tq // BLOCK):
            lo = b * BLOCK
            valid = _swa_band_mask(i == 0 if b == 0 else False)
            q4 = q_ref[:, lo:lo + BLOCK, :].reshape(SWA_GROUP * BLOCK, d)
            sc = jnp.where(valid, _dot_nt(q4, kall[lo:lo + 2 * BLOCK]) * SWA_SCALE, NEG)
            m = jnp.maximum(jnp.max(sc, axis=-1, keepdims=True), sink_col)
            e = jnp.exp(sc - m)
            den = jnp.sum(e, axis=-1, keepdims=True) + jnp.exp(sink_col - m)
            out = jnp.dot((e * (1.0 / den)).astype(BF16), vall[lo:lo + 2 * BLOCK], preferred_element_type=F32)
            o_ref[:, lo:lo + BLOCK, :] = out.reshape(SWA_GROUP, BLOCK, d)
            lse_ref[:, lo:lo + BLOCK, :] = (m + jnp.log(den)).reshape(SWA_GROUP, BLOCK, 1)

    return _pallas_call(
        body, name=name, grid=(SWA_KV_HEADS, s // tq),
        in_specs=[grp(d), main, tail, main, tail, sink], out_specs=[grp(d), grp(1)],
        out_shape=[jax.ShapeDtypeStruct((SWA_HEADS, s, d), F32), jax.ShapeDtypeStruct((SWA_HEADS, s, 1), F32)],
        compiler_params=_params(dimension_semantics=("arbitrary", "arbitrary")),
    )(q, kpad, kpad, vpad, vpad, sinks)


def _swa_bwd(q, kpad, vpad, sinks, o, lse, do, name):
    _, s, _ = q.shape
    tq = _swa_tile(s)
    grp, main, tail, sink = _swa_specs(tq)
    d = SWA_HEAD_DIM

    def body(q_ref, km_ref, kt_ref, vm_ref, vt_ref, sink_ref, o_ref, lse_ref, do_ref, dq_ref, dk_ref, dv_ref, dsink_ref):
        i = pl.program_id(1)
        kall = jnp.concatenate([km_ref[...], kt_ref[...]], axis=0)
        vall = jnp.concatenate([vm_ref[...], vt_ref[...]], axis=0)
        sink_col = _swa_sink_column(sink_ref)

        @pl.when(i == 0)
        def _():
            dk_ref[...] = jnp.zeros_like(dk_ref)
            dv_ref[...] = jnp.zeros_like(dv_ref)
            dsink_ref[...] = jnp.zeros_like(dsink_ref)

        dsink = jnp.zeros((SWA_GROUP * BLOCK, 1), F32)
        for b in range(tq // BLOCK):
            lo = b * BLOCK
            valid = _swa_band_mask(i == 0 if b == 0 else False)
            rows4 = SWA_GROUP * BLOCK
            q4 = q_ref[:, lo:lo + BLOCK, :].reshape(rows4, d)
            do4 = do_ref[:, lo:lo + BLOCK, :].reshape(rows4, d)
            lse4 = lse_ref[:, lo:lo + BLOCK, :].reshape(rows4, 1)
            delta = jnp.sum(do4 * o_ref[:, lo:lo + BLOCK, :].reshape(rows4, d), axis=-1, keepdims=True)
            kb, vb = kall[lo:lo + 2 * BLOCK], vall[lo:lo + 2 * BLOCK]
            do4b = do4.astype(BF16)
            p = jnp.where(valid, jnp.exp(_dot_nt(q4, kb) * SWA_SCALE - lse4), 0.0)
            ds = (p * (_dot_nt(do4b, vb) - delta) * SWA_SCALE).astype(BF16)
            dq_ref[:, lo:lo + BLOCK, :] = jnp.dot(ds, kb, preferred_element_type=F32).reshape(SWA_GROUP, BLOCK, d)
            band = pl.ds(pl.multiple_of(i * tq, BLOCK) + lo, 2 * BLOCK)
            dk_ref[band, :] += _dot_tn(ds, q4)
            dv_ref[band, :] += _dot_tn(p.astype(BF16), do4b)
            dsink = dsink - jnp.exp(sink_col - lse4) * delta
        per_head = [jnp.broadcast_to(jnp.sum(dsink[g * BLOCK:(g + 1) * BLOCK], axis=0, keepdims=True), (1, 128))
                    for g in range(SWA_GROUP)]
        dsink_ref[...] += jnp.concatenate(per_head + [jnp.zeros((8 - SWA_GROUP, 128), F32)], axis=0)

    acc = pl.BlockSpec((None, s + BLOCK, d), lambda j, i: (j, 0, 0))
    return _pallas_call(
        body, name=name, grid=(SWA_KV_HEADS, s // tq),
        in_specs=[grp(d), main, tail, main, tail, sink, grp(d), grp(1), grp(d)],
        out_specs=[grp(d), acc, acc, pl.BlockSpec((None, 8, 128), lambda j, i: (j, 0, 0))],
        out_shape=[jax.ShapeDtypeStruct((SWA_HEADS, s, d), F32),
                   jax.ShapeDtypeStruct((SWA_KV_HEADS, s + BLOCK, d), F32),
                   jax.ShapeDtypeStruct((SWA_KV_HEADS, s + BLOCK, d), F32),
                   jax.ShapeDtypeStruct((SWA_KV_HEADS, 8, 128), F32)],
        compiler_params=_params(dimension_semantics=("arbitrary", "arbitrary")),
    )(q, kpad, kpad, vpad, vpad, sinks, o, lse, do)


def _loss_head(y, target, name):
    s = y.shape[0]
    tm = _row_tile(s, 512)

    def body(y_ref, t_ref, dy_ref, loss_ref):
        i = pl.program_id(0)
        err = y_ref[...] - t_ref[...]
        dy_ref[...] = err * (1.0 / D_MODEL)
        part = jnp.broadcast_to(0.5 * jnp.sum(jnp.mean(err * err, axis=-1, keepdims=True), axis=0, keepdims=True), (1, 128))

        @pl.when(i == 0)
        def _():
            loss_ref[...] = part

        @pl.when(i > 0)
        def _():
            loss_ref[...] += part

    row = pl.BlockSpec((tm, D_MODEL), lambda i: (i, 0))
    return _pallas_call(
        body, name=name, grid=(s // tm,), in_specs=[row, row], out_specs=[row, _full((1, 128))],
        out_shape=[jax.ShapeDtypeStruct((s, D_MODEL), F32), jax.ShapeDtypeStruct((1, 128), F32)],
        compiler_params=_params(dimension_semantics=("arbitrary",)),
    )(y, target)


def _adamw(w, g, m, v, name):
    rows, cols = w.shape
    tr = rows
    for cand in (512, 256, 128, 64, 32, 16, 8):
        if rows % cand == 0 and rows > cand:
            tr = cand
            break

    def body(w_ref, g_ref, m_ref, v_ref, d_ref, nm_ref, nv_ref):
        gv = g_ref[...]
        nm = ADAM_B1 * m_ref[...] + (1.0 - ADAM_B1) * gv
        nv = ADAM_B2 * v_ref[...] + (1.0 - ADAM_B2) * (gv * gv)
        m_hat = nm / (1.0 - ADAM_B1 ** ADAM_STEP)
        v_hat = nv / (1.0 - ADAM_B2 ** ADAM_STEP)
        d_ref[...] = -ADAM_LR * (m_hat / (jnp.sqrt(v_hat) + ADAM_EPS) + ADAM_WD * w_ref[...])
        nm_ref[...] = nm
        nv_ref[...] = nv

    blk = pl.BlockSpec((tr, cols), lambda i: (i, 0))
    return _pallas_call(
        body, name=name, grid=(rows // tr,), in_specs=[blk] * 4, out_specs=[blk] * 3,
        out_shape=[jax.ShapeDtypeStruct((rows, cols), F32)] * 3,
        compiler_params=_params(dimension_semantics=("arbitrary",)),
    )(w, g, m, v)


def _position():
    return lax.axis_index("x"), lax.axis_index("y"), lax.axis_index("c")


def _remote(src, dst, send_sems, recv_sems, k, to):
    return pltpu.make_async_remote_copy(src_ref=src, dst_ref=dst, send_sem=send_sems.at[k], recv_sem=recv_sems.at[k],
                                        device_id=to, device_id_type=MESH)


_HBM = pl.BlockSpec(memory_space=pltpu.HBM)


def _gather_stages(ins, outs, send_sems, recv_sems):
    na = len(ins)
    x, y, c = _position()
    me, sibling = (x, y, c), (x, y, 1 - c)
    xn, yn, dg = (1 - x, y), (x, 1 - y), (1 - x, 1 - y)

    def slot(a, chip, pc, half=None):
        ref = outs[a].at[4 * chip[0] + 2 * chip[1] + pc]
        if half is None:
            return ref
        rows = ref.shape[0] // 2
        return ref.at[pl.ds(half * rows, rows)]

    def cp(a, k, chip, pc, half, to, src=None):
        dst = slot(a, chip, pc, half)
        return _remote(dst if src is None else src, dst, send_sems, recv_sems, 8 * a + k, to)

    first_hop = [(0, xn), (1, yn)]
    second_hop = [(0, xn, 2, 0, yn), (1, yn, 3, 1, xn)]

    def sends():
        out = []
        for a in range(na):
            out += [cp(a, k, (x, y), c, None, (*to, c), src=ins[a].at[c]) for k, to in first_hop]
            out += [cp(a, fwd_k, frm, c, half, (*to, c)) for _, frm, fwd_k, half, to in second_hop]
            out += [cp(a, 4 + k, frm, c, None, sibling) for k, frm in first_hop]
            out += [cp(a, 6 + half, dg, c, half, sibling) for half in (0, 1)]
        return out

    def stage0():
        for a in range(na):
            for k, to in first_hop:
                cp(a, k, (x, y), c, None, (*to, c), src=ins[a].at[c]).start()

    def stage1():
        for k, frm, fwd_k, half, to in second_hop:
            for a in range(na):
                cp(a, k, frm, c, None, me).wait_recv()
                cp(a, fwd_k, frm, c, half, (*to, c)).start()
                cp(a, 4 + k, frm, c, None, sibling).start()

    def stage2():
        for half in (0, 1):
            for a in range(na):
                cp(a, 2 + half, dg, c, half, me).wait_recv()
                cp(a, 6 + half, dg, c, half, sibling).start()

    def stage3():
        for a in range(na):
            for k, chip, half in ((4, xn, None), (5, yn, None), (6, dg, 0), (7, dg, 1)):
                cp(a, k, chip, 1 - c, half, me).wait_recv()
        for sent in sends():
            sent.wait_send()

    return [stage0, stage1, stage2, stage3]


def _gather_out_shapes(mine):
    return [jax.ShapeDtypeStruct((N_DEV,) + m.shape[1:], m.dtype) for m in mine]


def _exchange_scratch(per_array, na):
    return [pltpu.SemaphoreType.DMA((per_array * na,)), pltpu.SemaphoreType.DMA((per_array * na,))]


def _all_gather_halves(mine, name):
    na = len(mine)

    def body(*refs):
        for stage in _gather_stages(refs[:na], refs[na:2 * na], *refs[2 * na:]):
            stage()

    return _pallas_call(body, name=name, in_specs=[_HBM] * na, out_specs=[_HBM] * na, out_shape=_gather_out_shapes(mine),
                        scratch_shapes=_exchange_scratch(8, na))(*mine)


def _sibling_exchange(parts, name, other_half):
    na = len(parts)

    def body(*refs):
        ins, outs, (send_sems, recv_sems) = refs[:na], refs[na:2 * na], refs[2 * na:]
        x, y, c = _position()
        copies = [_remote(ins[a].at[:, 1 - c] if other_half else ins[a], outs[a], send_sems, recv_sems, a, (x, y, 1 - c))
                  for a in range(na)]
        for cp in copies:
            cp.start()
        for cp in copies:
            cp.wait()

    return _pallas_call(
        body, name=name, in_specs=[_HBM] * na, out_specs=[_HBM] * na,
        out_shape=[jax.ShapeDtypeStruct(p.shape[:1] + p.shape[2:] if other_half else p.shape, p.dtype) for p in parts],
        scratch_shapes=_exchange_scratch(1, na),
    )(*parts)


def _scatter_stages(ins, outs, send_sems, recv_sems):
    na = len(ins)
    x, y, c = _position()
    chips = [(1 - x, y), (x, 1 - y), (1 - x, 1 - y)]

    def copies():
        return [_remote(ins[a].at[2 * px + py], outs[a].at[j], send_sems, recv_sems, 3 * a + j, (px, py, c))
                for a in range(na) for j, (px, py) in enumerate(chips)]

    def start():
        for cp in copies():
            cp.start()

    def wait():
        for cp in copies():
            cp.wait()

    return [start, wait]


def _scatter_out_shapes(parts):
    return [jax.ShapeDtypeStruct((3,) + p.shape[1:], p.dtype) for p in parts]


def _scatter_to_chips(parts, name):
    na = len(parts)

    def body(*refs):
        for stage in _scatter_stages(refs[:na], refs[na:2 * na], *refs[2 * na:]):
            stage()

    return _pallas_call(body, name=name, in_specs=[_HBM] * na, out_specs=[_HBM] * na, out_shape=_scatter_out_shapes(parts),
                        scratch_shapes=_exchange_scratch(3, na))(*parts)


def _assemble(g4, mine, name, side_by_side, order=None):
    _, nl, r, w = g4.shape
    tr = _divisor_tile(r, 640)

    def body(in_ref, mine_ref, out_ref):
        chip = 2 * lax.axis_index("x") + lax.axis_index("y")
        blocks = [jnp.where(chip == sh, mine_ref[...], in_ref[sh]) for sh in range(N_SHARD)]
        if side_by_side:
            full = jnp.concatenate(blocks, axis=-1)
            out_ref[...] = full if order is None else _take_cols(full, order)
        else:
            for sh in range(N_SHARD):
                out_ref[sh] = blocks[sh]

    if side_by_side:
        out_spec = pl.BlockSpec((None, tr, N_SHARD * w), lambda l, i: (l, i, 0))
        out_shape = jax.ShapeDtypeStruct((nl, r, N_SHARD * w), g4.dtype)
    else:
        out_spec = pl.BlockSpec((None, N_SHARD, tr, w), lambda l, i: (l, 0, i, 0))
        out_shape = jax.ShapeDtypeStruct((nl, N_SHARD, r, w), g4.dtype)
    return _pallas_call(
        body, name=name, grid=(nl, r // tr),
        in_specs=[pl.BlockSpec((N_SHARD, None, tr, w), lambda l, i: (0, l, i, 0)),
                  pl.BlockSpec((None, tr, w), lambda l, i: (l, i, 0))],
        out_specs=out_spec, out_shape=out_shape,
        compiler_params=_params(dimension_semantics=("arbitrary", "arbitrary")),
    )(g4, mine)


def _all_reduce_small(vec, name):
    r, l = vec.shape

    def body(v_ref, out_ref, gath_ref, send_sems, recv_sems):
        x, y, c = _position()
        me = 4 * x + 2 * y + c
        gath_ref[me] = v_ref[...]
        copies = []
        for k in range(1, N_DEV):
            to = (x ^ (k >> 2), y ^ ((k >> 1) & 1), c ^ (k & 1))
            copies.append(_remote(gath_ref.at[me], gath_ref.at[me], send_sems, recv_sems, k - 1, to))
        for cp in copies:
            cp.start()
        for k in range(1, N_DEV):
            frm = 4 * (x ^ (k >> 2)) + 2 * (y ^ ((k >> 1) & 1)) + (c ^ (k & 1))
            _remote(gath_ref.at[frm], gath_ref.at[frm], send_sems, recv_sems, k - 1, (x, y, c)).wait_recv()
        for cp in copies:
            cp.wait_send()
        total = gath_ref[0]
        for d in range(1, N_DEV):
            total = total + gath_ref[d]
        out_ref[...] = total

    vm = pl.BlockSpec(memory_space=pltpu.VMEM)
    return _pallas_call(
        body, name=name, in_specs=[vm], out_specs=vm, out_shape=jax.ShapeDtypeStruct((r, l), F32),
        scratch_shapes=[pltpu.VMEM((N_DEV, r, l), F32), pltpu.SemaphoreType.DMA((N_DEV - 1,)),
                        pltpu.SemaphoreType.DMA((N_DEV - 1,))],
    )(vec)


def _sum_blocks(blocks, out_dtype, name):
    m, w = blocks[0].shape
    tr = _divisor_tile(m, 1024)

    def body(*refs):
        total = refs[0][...].astype(F32)
        for ref in refs[1:-1]:
            total = total + ref[...].astype(F32)
        refs[-1][...] = total.astype(out_dtype)

    blk = pl.BlockSpec((tr, w), lambda i: (i, 0))
    return _pallas_call(
        body, name=name, grid=(m // tr,), in_specs=[blk] * len(blocks), out_specs=blk,
        out_shape=jax.ShapeDtypeStruct((m, w), out_dtype),
        compiler_params=_params(dimension_semantics=("arbitrary",)),
    )(*blocks)


FIRST_GROUPS = (('ffn1_w_gate', 'ffn1_w_up', 'ffn1_w_down'),)
REST_GROUPS = (('ffn2_w_gate', 'ffn2_w_up', 'ffn2_w_down'), ('w_o',), ('w_in',), ('mla_w_q_b',), ('mla_w_kv_b',))
N_FIRST = len(FIRST_GROUPS)


def _shard_rows(name):
    shape, axis = BIG[name]
    return shape[0] // N_SHARD if axis == 0 else shape[0]


def _group_row_offsets(group):
    return [int(v) for v in np.cumsum([0] + [_shard_rows(n) for n in group])]


def _rope_tables(s):
    pos = jnp.arange(s, dtype=F32)
    inv = 1.0 / (ROPE_THETA ** (jnp.arange(0, MLA_ROPE, 2, dtype=F32) / MLA_ROPE))
    ang = pos[:, None] * inv[None, :]
    return jnp.tile(jnp.cos(ang), (1, SWA_HEADS)), jnp.tile(jnp.sin(ang), (1, SWA_HEADS))


_MIXER_GAINS = ('mix_norm', 'mla_q_a_norm', 'mla_kv_a_norm', 'mla_q_norm', 'mla_k_norm', 'swa_q_norm', 'swa_k_norm')


def _local_step(x, target, small, ex):
    s = x.shape[0]
    cos, sin = _rope_tables(s)
    row = lambda name, l: small[name][l][None, :]
    saved, bigs = [], []
    for l in range(DEPTH):
        big = ex.first_weights(l)
        sv = {'x0': x}
        x, sv['g1'], sv['u1'], got = _ffn_fwd(x, row('ffn1_norm', l), big['ffn1_w_gate'], big['ffn1_w_up'], big['ffn1_w_down'],
                                             f"ffn1_fwd_{l}", ex.gather_behind_ffn1(l))
        big.update(ex.rest_weights(l, got))
        bigs.append(big)
        sv['x1'] = x
        gains = [row(n, l) for n in _MIXER_GAINS]
        mixer_w = (big['w_in'], big['mla_w_q_b'], big['mla_w_kv_b'])
        q_a, k_a, v_a, q_b, k_b, v_b = _pre_fwd(x, gains, *mixer_w, cos, sin, f"pre_fwd_{l}")
        o_a, lse, got = _mla_fwd(q_a, k_a, v_a, f"mla_fwd_{l}", ex.gather_behind_mla(l))
        ex.gathered_behind_mla(l, got)
        kpad = jnp.pad(k_b, ((0, 0), (BLOCK, 0), (0, 0)))
        vpad = jnp.pad(v_b, ((0, 0), (BLOCK, 0), (0, 0)))
        sinks = jnp.broadcast_to(small['swa_sinks'][l].reshape(SWA_KV_HEADS, SWA_GROUP, 1), (SWA_KV_HEADS, SWA_GROUP, 128))
        o_b, lse_b = _swa_fwd(q_b, kpad, vpad, sinks, f"swa_fwd_{l}")
        sv.update(gains=gains, mixer_w=mixer_w, q_a=q_a, k_a=k_a, v_a=v_a, q_b=q_b, kpad=kpad, vpad=vpad, sinks=sinks,
                  o_a=o_a, lse=lse, o_b=o_b, lse_b=lse_b)
        x = _post_fwd(x, o_a, o_b, row('mla_out_norm', l), row('swa_out_norm', l), big['w_o'], f"post_fwd_{l}")
        sv['x2'] = x
        x, sv['g2'], sv['u2'], _ = _ffn_fwd(x, row('ffn2_norm', l), big['ffn2_w_gate'], big['ffn2_w_up'], big['ffn2_w_down'],
                                           f"ffn2_fwd_{l}")
        saved.append(sv)

    dx, loss = _loss_head(x, target, "loss_head")

    gs = {n: [None] * DEPTH for n in SMALL_NAMES}
    t = _attn_tile(s)
    for l in reversed(range(DEPTH)):
        sv, big = saved[l], bigs[l]

        def ffn_back(tag, xin, dy, gate, up, scatter=None):
            (dxi, dgain, nb, act, dgate, dup), got = _ffn_bwd(xin, dy, gate, up, row(tag + '_norm', l), big[tag + '_w_gate'],
                                                             big[tag + '_w_up'], big[tag + '_w_down'], f"{tag}_bwd_{l}", scatter)
            gs[tag + '_norm'][l] = dgain[0]
            group = _matmul_tn(dgate, nb, 1.0, f"{tag}_dwg_{l}", None, 0, 3)
            group = _matmul_tn(dup, nb, 1.0, f"{tag}_dwu_{l}", group, 1, 3)
            group = _matmul_tn(act, dy, 0.5, f"{tag}_dwd_{l}", group, 2, 3)
            return dxi, {(tag + '_w_gate', tag + '_w_up', tag + '_w_down'): group}, got

        dx, rest_grads, _ = ffn_back('ffn2', sv['x2'], dx, sv['g2'], sv['u2'])
        do_a, do_b, dga, dgb, dwo, delta = _post_bwd(dx, sv['o_a'], sv['o_b'], row('mla_out_norm', l), row('swa_out_norm', l),
                                              big['w_o'], f"post_bwd_{l}")
        gs['mla_out_norm'][l], gs['swa_out_norm'][l] = dga[0], dgb[0]
        rest_grads['w_o'] = dwo.reshape(N_SHARD, MIX_WIDTH // N_SHARD, D_MODEL)
        dq_a, dk_a, dv_a, got = _mla_bwd(sv['q_a'], sv['k_a'], sv['v_a'], do_a, sv['lse'], delta, f"mla_bwd_{l}",
                                         ex.scatter_behind_mla(l))
        ex.scattered_behind_mla(l, got)
        dq_b, dkpad, dvpad, dsink = _swa_bwd(sv['q_b'], sv['kpad'], sv['vpad'], sv['sinks'], sv['o_b'], sv['lse_b'], do_b,
                                             f"swa_bwd_{l}")
        gs['swa_sinks'][l] = dsink[:, :SWA_GROUP, 0].reshape(SWA_HEADS)
        cts = [dq_a, dk_a, dv_a, dq_b, dkpad[:, BLOCK:], dvpad[:, BLOCK:]]
        outs = _pre_bwd(sv['x1'], dx, cts, sv['gains'], *sv['mixer_w'], cos, sin, f"pre_bwd_{l}")
        dx = outs[0]
        for n, val in zip(_MIXER_GAINS, outs[1:8]):
            gs[n][l] = val[0]
        rest_grads['w_in'], rest_grads['mla_w_q_b'], rest_grads['mla_w_kv_b'] = outs[8:11]
        ex.grads_ready(l, 'rest', rest_grads)
        dx, first_grads, got = ffn_back('ffn1', sv['x0'], dx, sv['g1'], sv['u1'], ex.scatter_behind_ffn1(l))
        ex.scattered_behind_ffn1(l, got)
        ex.grads_ready(l, 'first', first_grads)
    return loss, dx, gs


class _Exchange:
    def __init__(self, weights, c, chip):
        halves_of = lambda a: a.reshape(a.shape[:-2] + (2, a.shape[-2] // 2, a.shape[-1]))
        self.halves_of, self.c, self.chip = halves_of, c, chip
        self.mine = [[halves_of(jnp.concatenate([weights[n][l].astype(BF16) for n in group], axis=0))
                      for group in FIRST_GROUPS + REST_GROUPS] for l in range(DEPTH)]
        self.ahead, self.begun, self.received = {}, {}, {}

    def _assembled(self, l, which, gathered):
        groups, base = (FIRST_GROUPS, 0) if which == 'first' else (REST_GROUPS, N_FIRST)
        big = {}
        for gi, group in enumerate(groups):
            offs = _group_row_offsets(group)
            _, rh, w = self.mine[l][base + gi].shape
            col_sharded = BIG[group[0]][1] == 1
            full = _assemble(gathered[gi].reshape(N_SHARD, 1, 2 * rh, w), self.mine[l][base + gi].reshape(1, 2 * rh, w),
                             f"assemble_{which}{gi}_{l}", col_sharded, MIXER_ORDERS.get(group[0]))
            for i, n in enumerate(group):
                rows = offs[i + 1] - offs[i]
                if len(group) > 1 and offs[i] % rows == 0:
                    big[n] = ((full, (None, rows, N_SHARD * w), (0, offs[i] // rows, 0)) if col_sharded else
                              (full, (None, N_SHARD, rows, w), (0, 0, offs[i] // rows, 0)))
                elif col_sharded:
                    big[n] = full[0, offs[i]:offs[i + 1]]
                else:
                    big[n] = full[0, :, offs[i]:offs[i + 1]].reshape(BIG[n][0])
        return big

    def first_weights(self, l):
        got = self.ahead[l][:N_FIRST] if l in self.ahead else _all_gather_halves(self.mine[l][:N_FIRST], f"gather_first_{l}")
        return self._assembled(l, 'first', got)

    def gather_behind_ffn1(self, l):
        return None if l in self.ahead else self.mine[l][N_FIRST:]

    def rest_weights(self, l, got):
        return self._assembled(l, 'rest', self.ahead[l][N_FIRST:] if l in self.ahead else got)

    def gather_behind_mla(self, l):
        return self.mine[l + 1] if l + 1 < DEPTH else None

    def gathered_behind_mla(self, l, got):
        if got:
            self.ahead[l + 1] = got

    def grads_ready(self, l, which, grads):
        groups = FIRST_GROUPS if which == 'first' else REST_GROUPS
        parts = [self.halves_of(grads[group] if group in grads else jnp.concatenate([grads[n] for n in group], axis=1))
                 for group in groups]
        from_sibling = _sibling_exchange(parts, f"swap_{which}_{l}", True)
        chip_sums = []
        for gi, (p, got) in enumerate(zip(parts, from_sibling)):
            kept = lax.dynamic_index_in_dim(p, self.c, axis=1, keepdims=False)
            rows = N_SHARD * p.shape[2]
            pair = _sum_blocks([kept.reshape(rows, -1), got.reshape(rows, -1)], BF16, f"sum_pair_{which}{gi}_{l}")
            chip_sums.append(pair.reshape(got.shape))
        self.begun[(l, which)] = chip_sums

    def scatter_behind_mla(self, l):
        return self.begun[(l + 1, 'first')] + self.begun[(l + 1, 'rest')] if l + 1 < DEPTH else None

    def scattered_behind_mla(self, l, got):
        if got:
            self.received[(l + 1, 'first')], self.received[(l + 1, 'rest')] = got[:N_FIRST], got[N_FIRST:]

    def scatter_behind_ffn1(self, l):
        return self.begun[(l, 'rest')] if l == 0 else None

    def scattered_behind_ffn1(self, l, got):
        if got:
            self.received[(l, 'rest')] = got

    def reduced(self):
        keys = sorted(self.begun)
        for key in keys:
            if key not in self.received:
                self.received[key] = _scatter_to_chips(self.begun[key], f"scatter_{key[1]}_{key[0]}")
        halves = []
        for l, which in keys:
            for gi, (cs, got) in enumerate(zip(self.begun[(l, which)], self.received[(l, which)])):
                own = lax.dynamic_index_in_dim(cs, self.chip, axis=0, keepdims=False)
                halves.append(_sum_blocks([own, got[0], got[1], got[2]], F32, f"sum_chips_{which}{gi}_{l}"))
        others = _sibling_exchange(halves, "share_halves", False)
        per_layer, at = {}, 0
        for l, which in keys:
            for group in (FIRST_GROUPS if which == 'first' else REST_GROUPS):
                mine_h, other_h = halves[at], others[at]
                at += 1
                full = jnp.where(self.c == 0, jnp.concatenate([mine_h, other_h]), jnp.concatenate([other_h, mine_h]))
                offs = _group_row_offsets(group)
                for i, n in enumerate(group):
                    per_layer[(n, l)] = full[offs[i]:offs[i + 1]]
        return {n: jnp.stack([per_layer[(n, l)] for l in range(DEPTH)]) for n in BIG_NAMES}


def kernel(x, ffn1_norm, ffn1_w_gate, ffn1_w_up, ffn1_w_down, mix_norm, w_in, mla_q_a_norm, mla_w_q_b, mla_kv_a_norm, mla_w_kv_b, mla_q_norm, mla_k_norm, swa_q_norm, swa_k_norm, swa_sinks, mla_out_norm, swa_out_norm, w_o, ffn2_norm, ffn2_w_gate, ffn2_w_up, ffn2_w_down, loss_target, m_ffn1_norm, m_ffn1_w_gate, m_ffn1_w_up, m_ffn1_w_down, m_mix_norm, m_w_in, m_mla_q_a_norm, m_mla_w_q_b, m_mla_kv_a_norm, m_mla_w_kv_b, m_mla_q_norm, m_mla_k_norm, m_swa_q_norm, m_swa_k_norm, m_swa_sinks, m_mla_out_norm, m_swa_out_norm, m_w_o, m_ffn2_norm, m_ffn2_w_gate, m_ffn2_w_up, m_ffn2_w_down, v_ffn1_norm, v_ffn1_w_gate, v_ffn1_w_up, v_ffn1_w_down, v_mix_norm, v_w_in, v_mla_q_a_norm, v_mla_w_q_b, v_mla_kv_a_norm, v_mla_w_kv_b, v_mla_q_norm, v_mla_k_norm, v_swa_q_norm, v_swa_k_norm, v_swa_sinks, v_mla_out_norm, v_swa_out_norm, v_w_o, v_ffn2_norm, v_ffn2_w_gate, v_ffn2_w_up, v_ffn2_w_down):
    args = dict(locals())
    transposed = lambda a: jnp.swapaxes(a, 1, 2)
    as_kernels_see = lambda n, a: transposed(a) if n in TRANSPOSED else a
    weights = {n: as_kernels_see(n, args[n]) for n in WEIGHT_NAMES}
    mom_m = {n: as_kernels_see(n, args["m_" + n]) for n in WEIGHT_NAMES}
    mom_v = {n: as_kernels_see(n, args["v_" + n]) for n in WEIGHT_NAMES}
    ex = _Exchange(weights, lax.axis_index("c"), 2 * lax.axis_index("x") + lax.axis_index("y"))
    loss, dx, gs = _local_step(x[0], loss_target[0], {n: weights[n] for n in SMALL_NAMES}, ex)

    small_flat = jnp.concatenate([jnp.stack(gs[n]).reshape(-1) for n in SMALL_NAMES] + [loss[0, :1]])
    n_small = small_flat.shape[0]
    lanes = -(-n_small // (8 * 128)) * 128
    small_sum = _all_reduce_small(jnp.pad(small_flat, (0, 8 * lanes - n_small)).reshape(8, lanes), "reduce_small").reshape(-1)
    loss_out = small_sum[n_small - 1]
    grads = ex.reduced()

    packed = lambda d: jnp.pad(jnp.concatenate([d[n].reshape(-1) for n in SMALL_NAMES]), (0, 8 * lanes - n_small + 1)).reshape(8, lanes)
    small_out = _adamw(packed(weights), small_sum.reshape(8, lanes), packed(mom_m), packed(mom_v), "adamw_small")
    deltas, new_m, new_v = {}, {}, {}
    off = 0
    for n in SMALL_NAMES:
        cnt = int(np.prod(weights[n].shape))
        grads[n], deltas[n], new_m[n], new_v[n] = (a.reshape(-1)[off:off + cnt].reshape(weights[n].shape)
                                                    for a in (small_sum, *small_out))
        off += cnt
    for n in BIG_NAMES:
        shp = weights[n].shape
        two_d = (shp[0] * shp[1], shp[2])
        d, nm, nv = _adamw(weights[n].reshape(two_d), grads[n].reshape(two_d), mom_m[n].reshape(two_d),
                           mom_v[n].reshape(two_d), f"adamw_{n}")
        deltas[n], new_m[n], new_v[n] = d.reshape(shp), nm.reshape(shp), nv.reshape(shp)
    for n in TRANSPOSED:
        grads[n], deltas[n], new_m[n], new_v[n] = (transposed(a) for a in (grads[n], deltas[n], new_m[n], new_v[n]))

    return (loss_out, dx[None], *[grads[n] for n in WEIGHT_NAMES], *[deltas[n] for n in WEIGHT_NAMES],
            *[new_m[n] for n in WEIGHT_NAMES], *[new_v[n] for n in WEIGHT_NAMES])
```

```python
import functools

import numpy as np
import jax
import jax.numpy as jnp
from jax import lax
from jax.experimental import pallas as pl
from jax.experimental.pallas import tpu as pltpu

F32 = jnp.float32
BF16 = jnp.bfloat16

D_MODEL = 1024
DEPTH = 2
EPS = 1e-6
ROPE_THETA = 10000.0
BLOCK = 128
MLA_HEADS = 4
MLA_Q_RANK = 256
MLA_KV_RANK = 128
MLA_NOPE = 128
MLA_ROPE = 64
MLA_V = 128
MLA_QK = MLA_NOPE + MLA_ROPE
MLA_WIDTH = MLA_HEADS * MLA_V
SWA_HEADS = 8
SWA_KV_HEADS = 2
SWA_GROUP = SWA_HEADS // SWA_KV_HEADS
SWA_HEAD_DIM = 64
SWA_WIDTH = SWA_HEADS * SWA_HEAD_DIM
MIX_WIDTH = MLA_WIDTH + SWA_WIDTH
IN_SPLITS = (MLA_Q_RANK, MLA_KV_RANK, MLA_ROPE, SWA_WIDTH, SWA_KV_HEADS * SWA_HEAD_DIM, SWA_KV_HEADS * SWA_HEAD_DIM)
IN_COLS = sum(IN_SPLITS)
IN_OFFS = tuple(int(v) for v in np.cumsum((0,) + IN_SPLITS))
D_FF = 2816
MLA_SCALE = MLA_QK ** -0.5
LOG2E = 1.4426950408889634
LN2 = 0.6931471805599453
MLA_QSCALE = MLA_SCALE * LOG2E
SWA_SCALE = SWA_HEAD_DIM ** -0.5
NEG = -1e30

ADAM_LR = 0.001
ADAM_B1 = 0.9
ADAM_B2 = 0.999
ADAM_EPS = 1e-08
ADAM_WD = 0.01
ADAM_STEP = 10

N_SHARD = 4
N_DEV = 8
VMEM_LIMIT = 56 * 1024 * 1024
MESH = pl.DeviceIdType.MESH

WEIGHT_NAMES = ['ffn1_norm', 'ffn1_w_gate', 'ffn1_w_up', 'ffn1_w_down', 'mix_norm', 'w_in', 'mla_q_a_norm', 'mla_w_q_b',
                'mla_kv_a_norm', 'mla_w_kv_b', 'mla_q_norm', 'mla_k_norm', 'swa_q_norm', 'swa_k_norm', 'swa_sinks',
                'mla_out_norm', 'swa_out_norm', 'w_o', 'ffn2_norm', 'ffn2_w_gate', 'ffn2_w_up', 'ffn2_w_down']
TRANSPOSED = ('ffn1_w_gate', 'ffn1_w_up', 'ffn2_w_gate', 'ffn2_w_up')
BIG = {'ffn1_w_gate': ((D_FF, D_MODEL), 0), 'ffn1_w_up': ((D_FF, D_MODEL), 0), 'ffn1_w_down': ((D_FF, D_MODEL), 0),
       'w_in': ((D_MODEL, IN_COLS), 1), 'mla_w_q_b': ((MLA_Q_RANK, MLA_HEADS * MLA_QK), 1),
       'mla_w_kv_b': ((MLA_KV_RANK, MLA_HEADS * (MLA_NOPE + MLA_V)), 1), 'w_o': ((MIX_WIDTH, D_MODEL), 0),
       'ffn2_w_gate': ((D_FF, D_MODEL), 0), 'ffn2_w_up': ((D_FF, D_MODEL), 0), 'ffn2_w_down': ((D_FF, D_MODEL), 0)}
BIG_NAMES = [n for n in WEIGHT_NAMES if n in BIG]
SMALL_NAMES = [n for n in WEIGHT_NAMES if n not in BIG]

_pallas_call = pl.pallas_call


def _params(**kw):
    return pltpu.CompilerParams(vmem_limit_bytes=VMEM_LIMIT, **kw)


def _full(shape):
    n = len(shape)
    return pl.BlockSpec(shape, lambda *_: (0,) * n)


def _resident(shape):
    n = len(shape)
    return pl.BlockSpec(shape, lambda *_: (0,) * n, pipeline_mode=pl.Buffered(1))


@jax.custom_vjp
def _mm(a, w):
    return jnp.dot(a.astype(BF16), w, preferred_element_type=F32)


def _mm_fwd(a, w):
    return _mm(a, w), w


def _mm_bwd(w, dy):
    return lax.dot_general(dy.astype(BF16), w, (((1,), (1,)), ((), ())), preferred_element_type=F32), None


_mm.defvjp(_mm_fwd, _mm_bwd)


def _dot_nt(a, b):
    return lax.dot_general(a, b, (((1,), (1,)), ((), ())), preferred_element_type=F32)


def _dot_tn(a, b):
    return lax.dot_general(a, b, (((0,), (0,)), ((), ())), preferred_element_type=F32)


def _rms(t, g):
    return t * lax.rsqrt(jnp.mean(t * t, axis=-1, keepdims=True) + EPS) * g


def _sigmoid(z):
    return 1.0 / (1.0 + jnp.exp(-z))


def _row_tile(s, want):
    return min(want, s)


def _divisor_tile(rows, cap):
    return max(d for d in range(16, min(rows, cap) + 1, 16) if rows % d == 0)


FF_CHUNK = D_FF


def _weight_operand(w):
    if isinstance(w, tuple):
        arr, block, index = w
        return arr, pl.BlockSpec(block, lambda *_: index, pipeline_mode=pl.Buffered(1))
    return w, _resident(w.shape)


def _weight_rows(ref, start, n):
    if len(ref.shape) == 2:
        return ref[start:start + n, :]
    per = ref.shape[1]
    return ref[start // per:(start + n) // per].reshape(n, ref.shape[2])


def _ffn_fwd(x, g, wg, wu, wd, name, gather=None):
    s = x.shape[0]
    tm = _row_tile(s, 256)
    steps = s // tm
    ng = len(gather) if gather else 0
    (wg, wg_spec), (wu, wu_spec), (wd, wd_spec) = _weight_operand(wg), _weight_operand(wu), _weight_operand(wd)

    def body(x_ref, g_ref, wg_ref, wu_ref, wd_ref, *rest):
        g_ins, (y_ref, gate_ref, up_ref), g_outs, sems = rest[:ng], rest[ng:ng + 3], rest[ng + 3:2 * ng + 3], rest[2 * ng + 3:]
        if ng:
            _run_stages_at(_gather_stages(g_ins, g_outs, *sems), [(0,), (steps * 3 // 8,), (steps * 11 // 16,), (steps - 1,)])
        xv = x_ref[...]
        nb = _rms(xv, g_ref[...]).astype(BF16)
        acc = xv
        for c in range(0, D_FF, FF_CHUNK):
            gate = _dot_nt(nb, _weight_rows(wg_ref, c, FF_CHUNK))
            up = _dot_nt(nb, _weight_rows(wu_ref, c, FF_CHUNK))
            gate_ref[:, c:c + FF_CHUNK] = gate.astype(BF16)
            up_ref[:, c:c + FF_CHUNK] = up.astype(BF16)
            act = (gate * _sigmoid(gate) * up).astype(BF16)
            acc = acc + 0.5 * jnp.dot(act, _weight_rows(wd_ref, c, FF_CHUNK), preferred_element_type=F32)
        y_ref[...] = acc

    outs = _pallas_call(
        body, name=name, grid=(steps,),
        in_specs=[pl.BlockSpec((tm, D_MODEL), lambda i: (i, 0)), _full((1, D_MODEL)), wg_spec, wu_spec, wd_spec] + [_HBM] * ng,
        out_specs=[pl.BlockSpec((tm, D_MODEL), lambda i: (i, 0)), pl.BlockSpec((tm, D_FF), lambda i: (i, 0)),
                   pl.BlockSpec((tm, D_FF), lambda i: (i, 0))] + [_HBM] * ng,
        out_shape=[jax.ShapeDtypeStruct((s, D_MODEL), F32), jax.ShapeDtypeStruct((s, D_FF), BF16),
                   jax.ShapeDtypeStruct((s, D_FF), BF16)] + (_gather_out_shapes(gather) if ng else []),
        scratch_shapes=_exchange_scratch(8, ng) if ng else [],
        compiler_params=_params(dimension_semantics=("arbitrary",)),
    )(x, g, wg, wu, wd, *(gather or []))
    return outs[0], outs[1], outs[2], outs[3:]


def _ffn_bwd(x, dy, gate, up, g, wg, wu, wd, name, scatter=None):
    s = x.shape[0]
    tm = _row_tile(s, 256)
    steps = s // tm
    ng = len(scatter) if scatter else 0
    (wg, wg_spec), (wu, wu_spec), (wd, wd_spec) = _weight_operand(wg), _weight_operand(wu), _weight_operand(wd)

    def body(x_ref, dy_ref, gate_ref, up_ref, g_ref, wg_ref, wu_ref, wd_ref, *rest):
        c_ins, (dx_ref, dgain_ref, n_ref, act_ref, dgate_ref, dup_ref) = rest[:ng], rest[ng:ng + 6]
        c_outs, sems = rest[ng + 6:2 * ng + 6], rest[2 * ng + 6:]
        if ng:
            _run_stages_at(_scatter_stages(c_ins, c_outs, *sems), [(0,), (steps - 1,)])
        i = pl.program_id(0)
        xv = x_ref[...]
        dyv = dy_ref[...]
        gv = g_ref[...]
        r = lax.rsqrt(jnp.mean(xv * xv, axis=-1, keepdims=True) + EPS)
        xh = xv * r
        n_ref[...] = (xh * gv).astype(BF16)
        dyh = (0.5 * dyv).astype(BF16)
        dn = jnp.zeros_like(xv)
        for c in range(0, D_FF, FF_CHUNK):
            dact = _dot_nt(dyh, _weight_rows(wd_ref, c, FF_CHUNK))
            gt = gate_ref[:, c:c + FF_CHUNK].astype(F32)
            u = up_ref[:, c:c + FF_CHUNK].astype(F32)
            sg = _sigmoid(gt)
            sl = gt * sg
            act_ref[:, c:c + FF_CHUNK] = (sl * u).astype(BF16)
            dup = (dact * sl).astype(BF16)
            dgate = (dact * u * (sg * (1.0 + gt * (1.0 - sg)))).astype(BF16)
            dup_ref[:, c:c + FF_CHUNK] = dup
            dgate_ref[:, c:c + FF_CHUNK] = dgate
            dn = (dn + jnp.dot(dgate, _weight_rows(wg_ref, c, FF_CHUNK), preferred_element_type=F32)
                  + jnp.dot(dup, _weight_rows(wu_ref, c, FF_CHUNK), preferred_element_type=F32))
        part = jnp.sum(dn * xh, axis=0, keepdims=True)

        @pl.when(i == 0)
        def _():
            dgain_ref[...] = part

        @pl.when(i > 0)
        def _():
            dgain_ref[...] += part

        dxh = dn * gv
        dx_ref[...] = dyv + r * (dxh - xh * jnp.mean(dxh * xh, axis=-1, keepdims=True))

    row = lambda w: pl.BlockSpec((tm, w), lambda i: (i, 0))
    outs = _pallas_call(
        body, name=name, grid=(steps,),
        in_specs=[row(D_MODEL), row(D_MODEL), row(D_FF), row(D_FF), _full((1, D_MODEL)), wg_spec, wu_spec, wd_spec]
        + [_HBM] * ng,
        out_specs=[row(D_MODEL), _full((1, D_MODEL)), row(D_MODEL), row(D_FF), row(D_FF), row(D_FF)] + [_HBM] * ng,
        out_shape=[jax.ShapeDtypeStruct((s, D_MODEL), F32), jax.ShapeDtypeStruct((1, D_MODEL), F32),
                   jax.ShapeDtypeStruct((s, D_MODEL), BF16), jax.ShapeDtypeStruct((s, D_FF), BF16),
                   jax.ShapeDtypeStruct((s, D_FF), BF16), jax.ShapeDtypeStruct((s, D_FF), BF16)]
        + (_scatter_out_shapes(scatter) if ng else []),
        scratch_shapes=_exchange_scratch(3, ng) if ng else [],
        compiler_params=_params(dimension_semantics=("arbitrary",)),
    )(x, dy, gate, up, g, wg, wu, wd, *(scatter or []))
    return outs[:6], outs[6:]


def _store_col_shards(o_ref, acc, first_shard, n_here, width):
    for q in range(n_here):
        o_ref[q] = acc[:, (first_shard + q) * width:(first_shard + q + 1) * width].astype(BF16)


def _matmul_tn(a, b, scale, name, group=None, slot=0, slots=1):
    t, m = a.shape
    n = b.shape[1]
    tk = _row_tile(t, 2048)
    tn = n // 2
    nk = t // tk
    per = m // N_SHARD

    def body(a_ref, b_ref, *rest):
        o_ref, acc_ref = rest[-2:]
        k = pl.program_id(1)
        bv = b_ref[...]
        if scale != 1.0:
            bv = bv.astype(F32) * scale
        part = _dot_tn(a_ref[...].astype(BF16), bv.astype(BF16))

        @pl.when(k == 0)
        def _():
            acc_ref[...] = part

        @pl.when(k > 0)
        def _():
            acc_ref[...] += part

        @pl.when(k == nk - 1)
        def _():
            for sh in range(N_SHARD):
                o_ref[sh] = acc_ref[sh * per:(sh + 1) * per, :].astype(BF16)

    earlier = [] if group is None else [group]
    return _pallas_call(
        body, name=name, grid=(n // tn, nk),
        in_specs=[pl.BlockSpec((tk, m), lambda j, k: (k, 0)), pl.BlockSpec((tk, tn), lambda j, k: (k, j))]
        + [pl.BlockSpec(memory_space=pl.ANY)] * len(earlier),
        out_specs=pl.BlockSpec((N_SHARD, per, tn), lambda j, k: (0, slot, j)),
        out_shape=jax.ShapeDtypeStruct((N_SHARD, slots * per, n), BF16),
        scratch_shapes=[pltpu.VMEM((m, tn), F32)], input_output_aliases={2: 0} if earlier else {},
        compiler_params=_params(dimension_semantics=("arbitrary", "arbitrary")),
    )(a, b, *earlier)


_HALF = SWA_HEAD_DIM // 2
_IN_ORDER = (list(range(0, IN_OFFS[2]))
             + [IN_OFFS[3] + SWA_HEAD_DIM * h + i for h in range(SWA_HEADS) for i in range(_HALF)]
             + [IN_OFFS[3] + SWA_HEAD_DIM * h + _HALF + i for h in range(SWA_HEADS) for i in range(_HALF)]
             + list(range(IN_OFFS[5], IN_OFFS[6]))
             + [IN_OFFS[4] + SWA_HEAD_DIM * j + i for j in range(SWA_KV_HEADS) for i in range(_HALF)]
             + [IN_OFFS[4] + SWA_HEAD_DIM * j + _HALF + i for j in range(SWA_KV_HEADS) for i in range(_HALF)]
             + list(range(IN_OFFS[2], IN_OFFS[3])))
_QB_ORDER = ([MLA_QK * h + i for h in range(MLA_HEADS) for i in range(MLA_NOPE)]
             + [MLA_QK * h + MLA_NOPE + i for h in range(MLA_HEADS) for i in range(_HALF)]
             + [MLA_QK * h + MLA_NOPE + _HALF + i for h in range(MLA_HEADS) for i in range(_HALF)])
_KVB_ORDER = ([(MLA_NOPE + MLA_V) * h + i for h in range(MLA_HEADS) for i in range(MLA_NOPE)]
              + [(MLA_NOPE + MLA_V) * h + MLA_NOPE + i for h in range(MLA_HEADS) for i in range(MLA_V)])
MIXER_ORDERS = {'w_in': _IN_ORDER, 'mla_w_q_b': _QB_ORDER, 'mla_w_kv_b': _KVB_ORDER}
_P_CQ, _P_CKV, _P_QA, _P_QB, _P_VS, _P_KA, _P_KB, _P_PE = (int(v) for v in np.cumsum(
    (0, MLA_Q_RANK, MLA_KV_RANK, SWA_WIDTH // 2, SWA_WIDTH // 2, IN_SPLITS[5], IN_SPLITS[4] // 2, IN_SPLITS[4] // 2)))


def _runs(order):
    out, start = [], 0
    for i in range(1, len(order) + 1):
        if i == len(order) or order[i] != order[i - 1] + 1:
            out.append((order[start], i - start))
            start = i
    return out


def _inverse(order):
    inv = [0] * len(order)
    for new, old in enumerate(order):
        inv[old] = new
    return inv


def _take_cols(a, order):
    return jnp.concatenate([a[..., st:st + w] for st, w in _runs(order)], axis=-1)


def _segment_matrix(n, seg):
    return (lax.broadcasted_iota(jnp.int32, (n, n), 0) // seg == lax.broadcasted_iota(jnp.int32, (n, n), 1) // seg).astype(BF16)


def _cmm2(t, b):
    hi = t.astype(BF16)
    lo = (t - hi.astype(F32)).astype(BF16)
    return jnp.dot(hi, b, preferred_element_type=F32) + jnp.dot(lo, b, preferred_element_type=F32)


@jax.custom_vjp
def _cmm(t, b, bt):
    return jnp.dot(t.astype(BF16), b, preferred_element_type=F32)


def _cmm_fwd(t, b, bt):
    return _cmm(t, b, bt), (b, bt)


def _cmm_bwd(res, dy):
    b, bt = res
    return _cmm(dy, bt, b), None, None


_cmm.defvjp(_cmm_fwd, _cmm_bwd)


def _segsum(t, b):
    return _cmm(t, b, b)


def _rowsum(t, exact=False):
    n = t.shape[-1]
    if exact:
        return _cmm2(t, jnp.ones((n, 128), BF16))
    return _cmm(t, jnp.ones((n, 128), BF16), jnp.ones((128, n), BF16))


def _by_head(vals, width):
    lane = lax.broadcasted_iota(jnp.int32, (vals[0].shape[0], len(vals) * width), 1)
    out = vals[-1]
    for hd in range(len(vals) - 2, -1, -1):
        out = jnp.where(lane < (hd + 1) * width, vals[hd], out)
    return out


def _rope2(a, b, cos, sin):
    return a * cos - b * sin, b * cos + a * sin


def _pre_math(x, gm, gqa, gkva, gq, gk, gsq, gsk, taps, win, wqb, wkvb, cos, sin):
    h = _rms(x, gm)
    proj = _mm(h, win)
    if taps is not None:
        proj = proj + taps[0]
    cqn = _rms(proj[:, _P_CQ:_P_CKV], gqa)
    qa_all = _mm(cqn, wqb)
    ckvn = _rms(proj[:, _P_CKV:_P_QA], gkva)
    kv_all = _mm(ckvn, wkvb)
    if taps is not None:
        qa_all = qa_all + taps[1]
        kv_all = kv_all + taps[2]
    nh, hw = MLA_HEADS, MLA_HEADS * _HALF
    seg_mla = _segment_matrix(hw, _HALF)
    tile = lambda g, n: jnp.concatenate([g] * n, axis=-1)
    c4, s4 = cos[:, :hw], sin[:, :hw]

    def mla_heads(nope, r1, r2, gain):
        rr = r1 * r1 + r2 * r2
        lane_head = lax.broadcasted_iota(jnp.int32, (hw, nh * MLA_NOPE), 0) // _HALF
        spread = (lane_head == lax.broadcasted_iota(jnp.int32, (hw, nh * MLA_NOPE), 1) // MLA_NOPE).astype(BF16)
        rope_on_nope = _cmm(rr, spread, spread.T)
        ss_nope = [_rowsum(jnp.square(nope[:, hd * MLA_NOPE:(hd + 1) * MLA_NOPE])) for hd in range(nh)]
        rinv = [lax.rsqrt((ss_nope[hd] + rope_on_nope[:, hd * MLA_NOPE:(hd + 1) * MLA_NOPE]) * (1.0 / MLA_QK) + EPS)
                for hd in range(nh)]
        rl = lax.rsqrt((_segsum(rr, seg_mla) + _by_head(ss_nope, _HALF)) * (1.0 / MLA_QK) + EPS)
        o1, o2 = _rope2(r1 * rl * tile(gain[:, MLA_NOPE:MLA_NOPE + _HALF], nh), r2 * rl * tile(gain[:, MLA_NOPE + _HALF:], nh), c4, s4)
        return [jnp.concatenate([nope[:, hd * MLA_NOPE:(hd + 1) * MLA_NOPE] * rinv[hd] * gain[:, :MLA_NOPE],
                                 o1[:, hd * _HALF:(hd + 1) * _HALF], o2[:, hd * _HALF:(hd + 1) * _HALF]], axis=-1)
                for hd in range(nh)]

    q_a = mla_heads(qa_all[:, :nh * MLA_NOPE], qa_all[:, nh * MLA_NOPE:nh * MLA_NOPE + hw], qa_all[:, nh * MLA_NOPE + hw:], gq)
    pe1, pe2 = proj[:, _P_PE:_P_PE + _HALF], proj[:, _P_PE + _HALF:_P_PE + 2 * _HALF]
    k_a = mla_heads(kv_all[:, :nh * MLA_NOPE], tile(pe1, nh), tile(pe2, nh), gk)
    v_a = [kv_all[:, nh * MLA_NOPE + hd * MLA_V:nh * MLA_NOPE + (hd + 1) * MLA_V] for hd in range(nh)]

    def swa_heads(a, b, gain, n):
        w = n * _HALF
        r = lax.rsqrt(_segsum(a * a + b * b, _segment_matrix(w, _HALF)) * (1.0 / SWA_HEAD_DIM) + EPS)
        o1, o2 = _rope2(a * r * tile(gain[:, :_HALF], n), b * r * tile(gain[:, _HALF:], n), cos[:, :w], sin[:, :w])
        return [jnp.concatenate([o1[:, hd * _HALF:(hd + 1) * _HALF], o2[:, hd * _HALF:(hd + 1) * _HALF]], axis=-1) for hd in range(n)]

    q_b = swa_heads(proj[:, _P_QA:_P_QB], proj[:, _P_QB:_P_VS], gsq, SWA_HEADS)
    k_b = swa_heads(proj[:, _P_KA:_P_KB], proj[:, _P_KB:_P_PE], gsk, SWA_KV_HEADS)
    v_b = [proj[:, _P_VS + j * SWA_HEAD_DIM:_P_VS + (j + 1) * SWA_HEAD_DIM] for j in range(SWA_KV_HEADS)]
    return (q_a, k_a, v_a, q_b, k_b, v_b), (h, cqn, ckvn)


_PRE_GAIN_WIDTHS = (D_MODEL, MLA_Q_RANK, MLA_KV_RANK, MLA_QK, MLA_QK, SWA_HEAD_DIM, SWA_HEAD_DIM)
_PRE_HEADS = ((MLA_HEADS, MLA_QK), (MLA_HEADS, MLA_QK), (MLA_HEADS, MLA_V),
              (SWA_HEADS, SWA_HEAD_DIM), (SWA_KV_HEADS, SWA_HEAD_DIM), (SWA_KV_HEADS, SWA_HEAD_DIM))


def _pre_fwd(x, gains, win, wqb, wkvb, cos, sin, name):
    s = x.shape[0]
    tm = _row_tile(s, 512)

    def body(x_ref, *refs):
        g_refs, (win_ref, wqb_ref, wkvb_ref, cos_ref, sin_ref), out_refs = refs[:7], refs[7:12], refs[12:]
        outs, _ = _pre_math(x_ref[...], *[g[...] for g in g_refs], None, win_ref[...], wqb_ref[...], wkvb_ref[...],
                            cos_ref[...], sin_ref[...])
        for idx, (ref, heads) in enumerate(zip(out_refs, outs)):
            for hd, val in enumerate(heads):
                ref[hd] = (val * MLA_QSCALE if idx == 0 else val).astype(BF16)

    heads_spec = lambda nh, w: pl.BlockSpec((nh, tm, w), lambda i: (0, i, 0))
    return _pallas_call(
        body, name=name, grid=(s // tm,),
        in_specs=[pl.BlockSpec((tm, D_MODEL), lambda i: (i, 0))] + [_full((1, w)) for w in _PRE_GAIN_WIDTHS]
        + [_resident(win.shape), _resident(wqb.shape), _resident(wkvb.shape),
           pl.BlockSpec((tm, SWA_HEADS * _HALF), lambda i: (i, 0)), pl.BlockSpec((tm, SWA_HEADS * _HALF), lambda i: (i, 0))],
        out_specs=[heads_spec(nh, w) for nh, w in _PRE_HEADS],
        out_shape=[jax.ShapeDtypeStruct((nh, s, w), BF16) for nh, w in _PRE_HEADS],
        compiler_params=_params(dimension_semantics=("arbitrary",)),
    )(x, *gains, win, wqb, wkvb, cos, sin)


def _pre_bwd(x, dx_res, cts, gains, win, wqb, wkvb, cos, sin, name):
    s = x.shape[0]
    tm = _row_tile(s, 256)
    tap_widths = (IN_COLS, MLA_HEADS * MLA_QK, MLA_HEADS * (MLA_NOPE + MLA_V))

    def body(x_ref, dxr_ref, *refs):
        ct_refs, g_refs = refs[:6], refs[6:13]
        win_ref, wqb_ref, wkvb_ref, cos_ref, sin_ref = refs[13:18]
        dx_ref, dg_refs, dw_refs, acc_refs = refs[18], refs[19:26], refs[26:29], refs[29:32]
        i = pl.program_id(0)
        win_v, wqb_v, wkvb_v, cos_v, sin_v = win_ref[...], wqb_ref[...], wkvb_ref[...], cos_ref[...], sin_ref[...]

        def f(xv, gm, gqa, gkva, gq, gk, gsq, gsk, t0, t1, t2):
            return _pre_math(xv, gm, gqa, gkva, gq, gk, gsq, gsk, (t0, t1, t2), win_v, wqb_v, wkvb_v, cos_v, sin_v)

        taps = [jnp.zeros((tm, w), F32) for w in tap_widths]
        _, vjp, acts = jax.vjp(f, x_ref[...], *[g[...] for g in g_refs], *taps, has_aux=True)
        ct = tuple([ref[hd] for hd in range(nh)] for ref, (nh, _) in zip(ct_refs, _PRE_HEADS))
        grads = vjp(ct)
        dx_ref[...] = grads[0] + dxr_ref[...]
        dws = [_dot_tn(a.astype(BF16), t.astype(BF16)) for a, t in zip(acts, grads[8:11])]

        @pl.when(i == 0)
        def _():
            for ref, val in zip(dg_refs, grads[1:8]):
                ref[...] = val
            for ref, val in zip(acc_refs, dws):
                ref[...] = val

        @pl.when(i > 0)
        def _():
            for ref, val in zip(dg_refs, grads[1:8]):
                ref[...] += val
            for ref, val in zip(acc_refs, dws):
                ref[...] += val

        @pl.when(i == s // tm - 1)
        def _():
            for ref, acc, order in zip(dw_refs, acc_refs, (_IN_ORDER, _QB_ORDER, _KVB_ORDER)):
                _store_col_shards(ref, _take_cols(acc[...], _inverse(order)), 0, N_SHARD, acc.shape[1] // N_SHARD)

    heads_spec = lambda nh, w: pl.BlockSpec((nh, tm, w), lambda i: (0, i, 0))
    row = pl.BlockSpec((tm, D_MODEL), lambda i: (i, 0))
    half = pl.BlockSpec((tm, SWA_HEADS * _HALF), lambda i: (i, 0))
    shard_shapes = [(N_SHARD, w.shape[0], w.shape[1] // N_SHARD) for w in (win, wqb, wkvb)]
    return _pallas_call(
        body, name=name, grid=(s // tm,),
        in_specs=[row, row] + [heads_spec(nh, w) for nh, w in _PRE_HEADS] + [_full((1, w)) for w in _PRE_GAIN_WIDTHS]
        + [_resident(win.shape), _resident(wqb.shape), _resident(wkvb.shape), half, half],
        out_specs=[row] + [_full((1, w)) for w in _PRE_GAIN_WIDTHS] + [_full(shp) for shp in shard_shapes],
        out_shape=[jax.ShapeDtypeStruct((s, D_MODEL), F32)] + [jax.ShapeDtypeStruct((1, w), F32) for w in _PRE_GAIN_WIDTHS]
        + [jax.ShapeDtypeStruct(shp, BF16) for shp in shard_shapes],
        scratch_shapes=[pltpu.VMEM(w.shape, F32) for w in (win, wqb, wkvb)],
        compiler_params=_params(dimension_semantics=("arbitrary",)),
    )(x, dx_res, *cts, *gains, win, wqb, wkvb, cos, sin)


def _post_math(oa, ob, ga, gb, wo):
    mixed = jnp.concatenate([_rms(jnp.concatenate(oa, axis=-1), ga), _rms(jnp.concatenate(ob, axis=-1), gb)], axis=-1)
    return _mm(mixed, wo), mixed


def _post_fwd(x, oa, ob, ga, gb, wo, name):
    s = x.shape[0]
    tm = _row_tile(s, 512)

    def body(x_ref, oa_ref, ob_ref, ga_ref, gb_ref, wo_ref, y_ref):
        y, _ = _post_math([oa_ref[hd] for hd in range(MLA_HEADS)], [ob_ref[hd] for hd in range(SWA_HEADS)],
                          ga_ref[...], gb_ref[...], wo_ref[...])
        y_ref[...] = x_ref[...] + y

    row = pl.BlockSpec((tm, D_MODEL), lambda i: (i, 0))
    return _pallas_call(
        body, name=name, grid=(s // tm,),
        in_specs=[row, pl.BlockSpec((MLA_HEADS, tm, MLA_V), lambda i: (0, i, 0)),
                  pl.BlockSpec((SWA_HEADS, tm, SWA_HEAD_DIM), lambda i: (0, i, 0)),
                  _full((1, MLA_WIDTH)), _full((1, SWA_WIDTH)), _resident(wo.shape)],
        out_specs=row, out_shape=jax.ShapeDtypeStruct((s, D_MODEL), F32),
        compiler_params=_params(dimension_semantics=("arbitrary",)),
    )(x, oa, ob, ga, gb, wo)


def _post_bwd(dy, oa, ob, ga, gb, wo, name):
    s = dy.shape[0]
    tm = _row_tile(s, 512)
    t = _attn_tile(s)

    def body(dy_ref, oa_ref, ob_ref, ga_ref, gb_ref, wo_ref, doa_ref, dob_ref, dga_ref, dgb_ref, dwo_ref, delta_ref, acc_ref):
        i = pl.program_id(0)
        wo_v = wo_ref[...]
        dyv = dy_ref[...]

        def f(oa_l, ob_l, ga_v, gb_v):
            return _post_math(oa_l, ob_l, ga_v, gb_v, wo_v)

        _, vjp, mixed = jax.vjp(f, [oa_ref[hd] for hd in range(MLA_HEADS)], [ob_ref[hd] for hd in range(SWA_HEADS)],
                                ga_ref[...], gb_ref[...], has_aux=True)
        doa, dob, dga, dgb = vjp(dyv)
        for hd in range(MLA_HEADS):
            doa_ref[hd] = doa[hd]
            rows = _as_rows(_rowsum(doa[hd] * oa_ref[hd], exact=True))
            for j in range(tm // t):
                delta_ref[hd, j] = rows[:, j * t:(j + 1) * t]
        for hd in range(SWA_HEADS):
            dob_ref[hd] = dob[hd]
        dwo = _dot_tn(mixed.astype(BF16), dyv.astype(BF16))

        @pl.when(i == 0)
        def _():
            dga_ref[...] = dga
            dgb_ref[...] = dgb
            acc_ref[...] = dwo

        @pl.when(i > 0)
        def _():
            dga_ref[...] += dga
            dgb_ref[...] += dgb
            acc_ref[...] += dwo

        @pl.when(i == s // tm - 1)
        def _():
            dwo_ref[...] = acc_ref[...].astype(BF16)

    row = pl.BlockSpec((tm, D_MODEL), lambda i: (i, 0))
    oa_spec = pl.BlockSpec((MLA_HEADS, tm, MLA_V), lambda i: (0, i, 0))
    ob_spec = pl.BlockSpec((SWA_HEADS, tm, SWA_HEAD_DIM), lambda i: (0, i, 0))
    return _pallas_call(
        body, name=name, grid=(s // tm,),
        in_specs=[row, oa_spec, ob_spec, _full((1, MLA_WIDTH)), _full((1, SWA_WIDTH)), _resident(wo.shape)],
        out_specs=[oa_spec, ob_spec, _full((1, MLA_WIDTH)), _full((1, SWA_WIDTH)), _full(wo.shape),
                   pl.BlockSpec((MLA_HEADS, tm // t, 8, t), lambda i: (0, i, 0, 0))],
        out_shape=[jax.ShapeDtypeStruct((MLA_HEADS, s, MLA_V), F32), jax.ShapeDtypeStruct((SWA_HEADS, s, SWA_HEAD_DIM), F32),
                   jax.ShapeDtypeStruct((1, MLA_WIDTH), F32), jax.ShapeDtypeStruct((1, SWA_WIDTH), F32),
                   jax.ShapeDtypeStruct(wo.shape, BF16), jax.ShapeDtypeStruct((MLA_HEADS, s // t, 8, t), F32)],
        scratch_shapes=[pltpu.VMEM(wo.shape, F32)],
        compiler_params=_params(dimension_semantics=("arbitrary",)),
    )(dy, oa, ob, ga, gb, wo)


def _attn_tile(s):
    return 512 if s >= 2048 else 128


def _as_rows(cols):
    return cols.T[0:8, :]


def _causal_mask(t):
    return lax.broadcasted_iota(jnp.int32, (t, t), 1) <= lax.broadcasted_iota(jnp.int32, (t, t), 0)


def _pipelined_blocks(first, count, last_block, issue, consume, carry, prefetch_after):
    def clamped(j, slot):
        issue(jnp.minimum(j, last_block), slot)

    def pair(jj, c):
        a = first + 2 * jj
        clamped(a + 1, 1)
        c = consume(a, 0, c)
        clamped(a + 2, 0)
        return consume(a + 1, 1, c)

    clamped(first, 0)
    npairs = count // 2
    carry = lax.fori_loop(0, npairs, pair, carry)

    def odd(c):
        c = consume(first + 2 * npairs, 0, c)
        if prefetch_after:
            clamped(first + count, 0)
        return c

    return lax.cond(count - 2 * npairs == 1, odd, lambda c: c, carry)


def _run_stages_at(stages, steps):
    for stage, step in zip(stages, steps):
        here = pl.program_id(0) == step[0]
        for axis in range(1, len(step)):
            here = here & (pl.program_id(axis) == step[axis])
        pl.when(here)(stage)


def _mla_fwd(q, k, v, name, gather=None):
    nh, s, _ = q.shape
    t = _attn_tile(s)
    nq = s // t
    ng = len(gather) if gather else 0

    def body(q_ref, k_ref, v_ref, *rest):
        g_ins, (o_ref, lse_ref), g_outs = rest[:ng], rest[ng:ng + 2], rest[ng + 2:2 * ng + 2]
        (s0_ref, s1_ref), sems = rest[2 * ng + 2:2 * ng + 4], rest[2 * ng + 4:]
        if ng:
            _run_stages_at(_gather_stages(g_ins, g_outs, *sems), [(0, 0), (nh // 2, 0), (nh - 1, 0), (nh - 1, nq - 1)])
        qi = pl.program_id(1)
        qv = q_ref[...]
        s_refs = (s0_ref, s1_ref)

        def rows(j):
            return pl.ds(pl.multiple_of(j * t, t), t)

        def issue(j, slot):
            s_refs[slot][...] = _dot_nt(qv, k_ref[rows(j), :])

        def consume(j, slot, carry, masked=False):
            m, l, acc = carry
            sc = s_refs[slot][...]
            if masked:
                sc = jnp.where(_causal_mask(t), sc, NEG)
            m_new = jnp.maximum(m, jnp.max(sc, axis=-1, keepdims=True))
            alpha = jnp.exp2(m - m_new)
            p = jnp.exp2(sc - m_new)
            l = alpha * l + jnp.sum(p, axis=-1, keepdims=True)
            acc = alpha * acc + jnp.dot(p.astype(BF16), v_ref[rows(j), :], preferred_element_type=F32)
            return m_new, l, acc

        init = (jnp.full((t, 1), NEG, F32), jnp.zeros((t, 1), F32), jnp.zeros((t, MLA_V), F32))
        carry = _pipelined_blocks(0, qi, nq - 1, issue, consume, init, True)
        m, l, acc = consume(qi, 0, carry, masked=True)
        o_ref[...] = acc / l
        lse_ref[...] = _as_rows(jnp.broadcast_to(m + jnp.log2(l), (t, 128)))

    outs = _pallas_call(
        body, name=name, grid=(nh, nq),
        in_specs=[pl.BlockSpec((None, t, MLA_QK), lambda h, i: (h, i, 0)), pl.BlockSpec((None, s, MLA_QK), lambda h, i: (h, 0, 0)),
                  pl.BlockSpec((None, s, MLA_V), lambda h, i: (h, 0, 0))] + [_HBM] * ng,
        out_specs=[pl.BlockSpec((None, t, MLA_V), lambda h, i: (h, i, 0)), pl.BlockSpec((None, None, 8, t), lambda h, i: (h, i, 0, 0))]
        + [_HBM] * ng,
        out_shape=[jax.ShapeDtypeStruct((nh, s, MLA_V), F32), jax.ShapeDtypeStruct((nh, nq, 8, t), F32)]
        + (_gather_out_shapes(gather) if ng else []),
        scratch_shapes=[pltpu.VMEM((t, t), F32)] * 2 + (_exchange_scratch(8, ng) if ng else []),
        compiler_params=_params(dimension_semantics=("arbitrary", "arbitrary")),
    )(q, k, v, *(gather or []))
    return outs[0], outs[1], outs[2:]


def _mla_bwd(q, k, v, do, lse_row, delta_row, name, scatter=None):
    nh, s, _ = q.shape
    t = _attn_tile(s)
    nq = s // t
    ng = len(scatter) if scatter else 0

    def body(q_ref, k_ref, v_ref, do_ref, lse_ref, delta_ref, *rest):
        c_ins, (dq_ref, dk_ref, dv_ref), c_outs = rest[:ng], rest[ng:ng + 3], rest[ng + 3:2 * ng + 3]
        (s0_ref, s1_ref, dp0_ref, dp1_ref), sems = rest[2 * ng + 3:2 * ng + 7], rest[2 * ng + 7:]
        if ng:
            _run_stages_at(_scatter_stages(c_ins, c_outs, *sems), [(0, 0), (nh - 1, nq - 1)])
        kj = pl.program_id(1)
        kv_, vv = k_ref[...], v_ref[...]
        s_refs, dp_refs = (s0_ref, s1_ref), (dp0_ref, dp1_ref)

        @pl.when(kj == 0)
        def _():
            dq_ref[...] = jnp.zeros_like(dq_ref)

        def rows(i):
            return pl.ds(pl.multiple_of(i * t, t), t)

        def issue(i, slot):
            s_refs[slot][...] = _dot_nt(kv_, q_ref[rows(i), :])
            dp_refs[slot][...] = _dot_nt(vv, do_ref[rows(i), :].astype(BF16))

        def consume(i, slot, carry, masked=False):
            dk, dv = carry
            p = jnp.exp2(s_refs[slot][...] - lse_ref[i][0:1, :])
            if masked:
                p = jnp.where(lax.broadcasted_iota(jnp.int32, (t, t), 0) <= lax.broadcasted_iota(jnp.int32, (t, t), 1), p, 0.0)
            dv = dv + jnp.dot(p.astype(BF16), do_ref[rows(i), :].astype(BF16), preferred_element_type=F32)
            ds = (p * (dp_refs[slot][...] - delta_ref[i][0:1, :])).astype(BF16)
            dk = dk + jnp.dot(ds, q_ref[rows(i), :], preferred_element_type=F32)
            dq_ref[rows(i), :] += _dot_tn(ds, kv_) * MLA_SCALE
            return dk, dv

        issue(kj, 0)
        carry = consume(kj, 0, (jnp.zeros((t, MLA_QK), F32), jnp.zeros((t, MLA_V), F32)), masked=True)
        dk, dv = _pipelined_blocks(kj + 1, nq - 1 - kj, nq - 1, issue, consume, carry, False)
        dk_ref[...] = dk * LN2
        dv_ref[...] = dv

    tile = lambda w: pl.BlockSpec((None, t, w), lambda h, j: (h, j, 0))
    whole = lambda w: pl.BlockSpec((None, s, w), lambda h, j: (h, 0, 0))
    rows_spec = pl.BlockSpec((None, nq, 8, t), lambda h, j: (h, 0, 0, 0))
    outs = _pallas_call(
        body, name=name, grid=(nh, nq),
        in_specs=[whole(MLA_QK), tile(MLA_QK), tile(MLA_V), whole(MLA_V), rows_spec, rows_spec] + [_HBM] * ng,
        out_specs=[whole(MLA_QK), tile(MLA_QK), tile(MLA_V)] + [_HBM] * ng,
        out_shape=[jax.ShapeDtypeStruct((nh, s, MLA_QK), F32), jax.ShapeDtypeStruct((nh, s, MLA_QK), F32),
                   jax.ShapeDtypeStruct((nh, s, MLA_V), F32)] + (_scatter_out_shapes(scatter) if ng else []),
        scratch_shapes=[pltpu.VMEM((t, t), F32)] * 4 + (_exchange_scratch(3, ng) if ng else []),
        compiler_params=_params(dimension_semantics=("arbitrary", "arbitrary")),
    )(q, k, v, do, lse_row, delta_row, *(scatter or []))
    return outs[0], outs[1], outs[2], outs[3:]


def _swa_tile(s):
    return min(s, 8 * BLOCK)


def _swa_specs(tq):
    nb = tq // BLOCK
    grp = lambda w: pl.BlockSpec((SWA_GROUP, tq, w), lambda j, i: (j, i, 0))
    main = pl.BlockSpec((None, tq, SWA_HEAD_DIM), lambda j, i: (j, i, 0))
    tail = pl.BlockSpec((None, BLOCK, SWA_HEAD_DIM), lambda j, i: (j, nb * (i + 1), 0))
    sink = pl.BlockSpec((None, SWA_GROUP, 128), lambda j, i: (j, 0, 0))
    return grp, main, tail, sink


def _swa_band_mask(first):
    shape = (SWA_GROUP * BLOCK, 2 * BLOCK)
    q_rel = (lax.broadcasted_iota(jnp.int32, shape, 0) & (BLOCK - 1)) + BLOCK
    k_rel = lax.broadcasted_iota(jnp.int32, shape, 1)
    dist = q_rel - k_rel
    return (dist >= 0) & (dist < BLOCK) & ((k_rel >= BLOCK) | jnp.logical_not(first))


def _swa_sink_column(sink_ref):
    sk = sink_ref[...]
    return jnp.concatenate([jnp.broadcast_to(sk[g:g + 1, 0:1], (BLOCK, 1)) for g in range(SWA_GROUP)], axis=0)


def _swa_fwd(q, kpad, vpad, sinks, name):
    _, s, _ = q.shape
    tq = _swa_tile(s)
    grp, main, tail, sink = _swa_specs(tq)
    d = SWA_HEAD_DIM

    def body(q_ref, km_ref, kt_ref, vm_ref, vt_ref, sink_ref, o_ref, lse_ref):
        i = pl.program_id(1)
        kall = jnp.concatenate([km_ref[...], kt_ref[...]], axis=0)
        vall = jnp.concatenate([vm_ref[...], vt_ref[...]], axis=0)
        sink_col = _swa_sink_column(sink_ref)
        for b in range(tq // BLOCK):
            lo = b * BLOCK
            valid = _swa_band_mask(i == 0 if b == 0 else False)
            q4 = q_ref[:, lo:lo + BLOCK, :].reshape(SWA_GROUP * BLOCK, d)
            sc = jnp.where(valid, _dot_nt(q4, kall[lo:lo + 2 * BLOCK]) * SWA_SCALE, NEG)
            m = jnp.maximum(jnp.max(sc, axis=-1, keepdims=True), sink_col)
            e = jnp.exp(sc - m)
            den = jnp.sum(e, axis=-1, keepdims=True) + jnp.exp(sink_col - m)
            out = jnp.dot((e * (1.0 / den)).astype(BF16), vall[lo:lo + 2 * BLOCK], preferred_element_type=F32)
            o_ref[:, lo:lo + BLOCK, :] = out.reshape(SWA_GROUP, BLOCK, d)
            lse_ref[:, lo:lo + BLOCK, :] = (m + jnp.log(den)).reshape(SWA_GROUP, BLOCK, 1)

    return _pallas_call(
        body, name=name, grid=(SWA_KV_HEADS, s // tq),
        in_specs=[grp(d), main, tail, main, tail, sink], out_specs=[grp(d), grp(1)],
        out_shape=[jax.ShapeDtypeStruct((SWA_HEADS, s, d), F32), jax.ShapeDtypeStruct((SWA_HEADS, s, 1), F32)],
        compiler_params=_params(dimension_semantics=("arbitrary", "arbitrary")),
    )(q, kpad, kpad, vpad, vpad, sinks)


def _swa_bwd(q, kpad, vpad, sinks, o, lse, do, name):
    _, s, _ = q.shape
    tq = _swa_tile(s)
    grp, main, tail, sink = _swa_specs(tq)
    d = SWA_HEAD_DIM

    def body(q_ref, km_ref, kt_ref, vm_ref, vt_ref, sink_ref, o_ref, lse_ref, do_ref, dq_ref, dk_ref, dv_ref, dsink_ref):
        i = pl.program_id(1)
        kall = jnp.concatenate([km_ref[...], kt_ref[...]], axis=0)
        vall = jnp.concatenate([vm_ref[...], vt_ref[...]], axis=0)
        sink_col = _swa_sink_column(sink_ref)

        @pl.when(i == 0)
        def _():
            dk_ref[...] = jnp.zeros_like(dk_ref)
            dv_ref[...] = jnp.zeros_like(dv_ref)
            dsink_ref[...] = jnp.zeros_like(dsink_ref)

        dsink = jnp.zeros((SWA_GROUP * BLOCK, 1), F32)
        for b in range(tq // BLOCK):
            lo = b * BLOCK
            valid = _swa_band_mask(i == 0 if b == 0 else False)
            rows4 = SWA_GROUP * BLOCK
            q4 = q_ref[:, lo:lo + BLOCK, :].reshape(rows4, d)
            do4 = do_ref[:, lo:lo + BLOCK, :].reshape(rows4, d)
            lse4 = lse_ref[:, lo:lo + BLOCK, :].reshape(rows4, 1)
            delta = jnp.sum(do4 * o_ref[:, lo:lo + BLOCK, :].reshape(rows4, d), axis=-1, keepdims=True)
            kb, vb = kall[lo:lo + 2 * BLOCK], vall[lo:lo + 2 * BLOCK]
            do4b = do4.astype(BF16)
            p = jnp.where(valid, jnp.exp(_dot_nt(q4, kb) * SWA_SCALE - lse4), 0.0)
            ds = (p * (_dot_nt(do4b, vb) - delta) * SWA_SCALE).astype(BF16)
            dq_ref[:, lo:lo + BLOCK, :] = jnp.dot(ds, kb, preferred_element_type=F32).reshape(SWA_GROUP, BLOCK, d)
            band = pl.ds(pl.multiple_of(i * tq, BLOCK) + lo, 2 * BLOCK)
            dk_ref[band, :] += _dot_tn(ds, q4)
            dv_ref[band, :] += _dot_tn(p.astype(BF16), do4b)
            dsink = dsink - jnp.exp(sink_col - lse4) * delta
        per_head = [jnp.broadcast_to(jnp.sum(dsink[g * BLOCK:(g + 1) * BLOCK], axis=0, keepdims=True), (1, 128))
                    for g in range(SWA_GROUP)]
        dsink_ref[...] += jnp.concatenate(per_head + [jnp.zeros((8 - SWA_GROUP, 128), F32)], axis=0)

    acc = pl.BlockSpec((None, s + BLOCK, d), lambda j, i: (j, 0, 0))
    return _pallas_call(
        body, name=name, grid=(SWA_KV_HEADS, s // tq),
        in_specs=[grp(d), main, tail, main, tail, sink, grp(d), grp(1), grp(d)],
        out_specs=[grp(d), acc, acc, pl.BlockSpec((None, 8, 128), lambda j, i: (j, 0, 0))],
        out_shape=[jax.ShapeDtypeStruct((SWA_HEADS, s, d), F32),
                   jax.ShapeDtypeStruct((SWA_KV_HEADS, s + BLOCK, d), F32),
                   jax.ShapeDtypeStruct((SWA_KV_HEADS, s + BLOCK, d), F32),
                   jax.ShapeDtypeStruct((SWA_KV_HEADS, 8, 128), F32)],
        compiler_params=_params(dimension_semantics=("arbitrary", "arbitrary")),
    )(q, kpad, kpad, vpad, vpad, sinks, o, lse, do)


def _loss_head(y, target, name):
    s = y.shape[0]
    tm = _row_tile(s, 512)

    def body(y_ref, t_ref, dy_ref, loss_ref):
        i = pl.program_id(0)
        err = y_ref[...] - t_ref[...]
        dy_ref[...] = err * (1.0 / D_MODEL)
        part = jnp.broadcast_to(0.5 * jnp.sum(jnp.mean(err * err, axis=-1, keepdims=True), axis=0, keepdims=True), (1, 128))

        @pl.when(i == 0)
        def _():
            loss_ref[...] = part

        @pl.when(i > 0)
        def _():
            loss_ref[...] += part

    row = pl.BlockSpec((tm, D_MODEL), lambda i: (i, 0))
    return _pallas_call(
        body, name=name, grid=(s // tm,), in_specs=[row, row], out_specs=[row, _full((1, 128))],
        out_shape=[jax.ShapeDtypeStruct((s, D_MODEL), F32), jax.ShapeDtypeStruct((1, 128), F32)],
        compiler_params=_params(dimension_semantics=("arbitrary",)),
    )(y, target)


def _adamw(w, g, m, v, name):
    rows, cols = w.shape
    tr = rows
    for cand in (512, 256, 128, 64, 32, 16, 8):
        if rows % cand == 0 and rows > cand:
            tr = cand
            break

    def body(w_ref, g_ref, m_ref, v_ref, d_ref, nm_ref, nv_ref):
        gv = g_ref[...]
        nm = ADAM_B1 * m_ref[...] + (1.0 - ADAM_B1) * gv
        nv = ADAM_B2 * v_ref[...] + (1.0 - ADAM_B2) * (gv * gv)
        m_hat = nm / (1.0 - ADAM_B1 ** ADAM_STEP)
        v_hat = nv / (1.0 - ADAM_B2 ** ADAM_STEP)
        d_ref[...] = -ADAM_LR * (m_hat / (jnp.sqrt(v_hat) + ADAM_EPS) + ADAM_WD * w_ref[...])
        nm_ref[...] = nm
        nv_ref[...] = nv

    blk = pl.BlockSpec((tr, cols), lambda i: (i, 0))
    return _pallas_call(
        body, name=name, grid=(rows // tr,), in_specs=[blk] * 4, out_specs=[blk] * 3,
        out_shape=[jax.ShapeDtypeStruct((rows, cols), F32)] * 3,
        compiler_params=_params(dimension_semantics=("arbitrary",)),
    )(w, g, m, v)


def _position():
    return lax.axis_index("x"), lax.axis_index("y"), lax.axis_index("c")


def _remote(src, dst, send_sems, recv_sems, k, to):
    return pltpu.make_async_remote_copy(src_ref=src, dst_ref=dst, send_sem=send_sems.at[k], recv_sem=recv_sems.at[k],
                                        device_id=to, device_id_type=MESH)


_HBM = pl.BlockSpec(memory_space=pltpu.HBM)


def _gather_stages(ins, outs, send_sems, recv_sems):
    na = len(ins)
    x, y, c = _position()
    me, sibling = (x, y, c), (x, y, 1 - c)
    xn, yn, dg = (1 - x, y), (x, 1 - y), (1 - x, 1 - y)

    def slot(a, chip, pc, half=None):
        ref = outs[a].at[4 * chip[0] + 2 * chip[1] + pc]
        if half is None:
            return ref
        rows = ref.shape[0] // 2
        return ref.at[pl.ds(half * rows, rows)]

    def cp(a, k, chip, pc, half, to, src=None):
        dst = slot(a, chip, pc, half)
        return _remote(dst if src is None else src, dst, send_sems, recv_sems, 8 * a + k, to)

    first_hop = [(0, xn), (1, yn)]
    second_hop = [(0, xn, 2, 0, yn), (1, yn, 3, 1, xn)]

    def sends():
        out = []
        for a in range(na):
            out += [cp(a, k, (x, y), c, None, (*to, c), src=ins[a].at[c]) for k, to in first_hop]
            out += [cp(a, fwd_k, frm, c, half, (*to, c)) for _, frm, fwd_k, half, to in second_hop]
            out += [cp(a, 4 + k, frm, c, None, sibling) for k, frm in first_hop]
            out += [cp(a, 6 + half, dg, c, half, sibling) for half in (0, 1)]
        return out

    def stage0():
        for a in range(na):
            for k, to in first_hop:
                cp(a, k, (x, y), c, None, (*to, c), src=ins[a].at[c]).start()

    def stage1():
        for k, frm, fwd_k, half, to in second_hop:
            for a in range(na):
                cp(a, k, frm, c, None, me).wait_recv()
                cp(a, fwd_k, frm, c, half, (*to, c)).start()
                cp(a, 4 + k, frm, c, None, sibling).start()

    def stage2():
        for half in (0, 1):
            for a in range(na):
                cp(a, 2 + half, dg, c, half, me).wait_recv()
                cp(a, 6 + half, dg, c, half, sibling).start()

    def stage3():
        for a in range(na):
            for k, chip, half in ((4, xn, None), (5, yn, None), (6, dg, 0), (7, dg, 1)):
                cp(a, k, chip, 1 - c, half, me).wait_recv()
        for sent in sends():
            sent.wait_send()

    return [stage0, stage1, stage2, stage3]


def _gather_out_shapes(mine):
    return [jax.ShapeDtypeStruct((N_DEV,) + m.shape[1:], m.dtype) for m in mine]


def _exchange_scratch(per_array, na):
    return [pltpu.SemaphoreType.DMA((per_array * na,)), pltpu.SemaphoreType.DMA((per_array * na,))]


def _all_gather_halves(mine, name):
    na = len(mine)

    def body(*refs):
        for stage in _gather_stages(refs[:na], refs[na:2 * na], *refs[2 * na:]):
            stage()

    return _pallas_call(body, name=name, in_specs=[_HBM] * na, out_specs=[_HBM] * na, out_shape=_gather_out_shapes(mine),
                        scratch_shapes=_exchange_scratch(8, na))(*mine)


def _sibling_exchange(parts, name, other_half):
    na = len(parts)

    def body(*refs):
        ins, outs, (send_sems, recv_sems) = refs[:na], refs[na:2 * na], refs[2 * na:]
        x, y, c = _position()
        copies = [_remote(ins[a].at[:, 1 - c] if other_half else ins[a], outs[a], send_sems, recv_sems, a, (x, y, 1 - c))
                  for a in range(na)]
        for cp in copies:
            cp.start()
        for cp in copies:
            cp.wait()

    return _pallas_call(
        body, name=name, in_specs=[_HBM] * na, out_specs=[_HBM] * na,
        out_shape=[jax.ShapeDtypeStruct(p.shape[:1] + p.shape[2:] if other_half else p.shape, p.dtype) for p in parts],
        scratch_shapes=_exchange_scratch(1, na),
    )(*parts)


def _scatter_stages(ins, outs, send_sems, recv_sems):
    na = len(ins)
    x, y, c = _position()
    chips = [(1 - x, y), (x, 1 - y), (1 - x, 1 - y)]

    def copies():
        return [_remote(ins[a].at[2 * px + py], outs[a].at[j], send_sems, recv_sems, 3 * a + j, (px, py, c))
                for a in range(na) for j, (px, py) in enumerate(chips)]

    def start():
        for cp in copies():
            cp.start()

    def wait():
        for cp in copies():
            cp.wait()

    return [start, wait]


def _scatter_out_shapes(parts):
    return [jax.ShapeDtypeStruct((3,) + p.shape[1:], p.dtype) for p in parts]


def _scatter_to_chips(parts, name):
    na = len(parts)

    def body(*refs):
        for stage in _scatter_stages(refs[:na], refs[na:2 * na], *refs[2 * na:]):
            stage()

    return _pallas_call(body, name=name, in_specs=[_HBM] * na, out_specs=[_HBM] * na, out_shape=_scatter_out_shapes(parts),
                        scratch_shapes=_exchange_scratch(3, na))(*parts)


def _assemble(g4, mine, name, side_by_side, order=None):
    _, nl, r, w = g4.shape
    tr = _divisor_tile(r, 640)

    def body(in_ref, mine_ref, out_ref):
        chip = 2 * lax.axis_index("x") + lax.axis_index("y")
        blocks = [jnp.where(chip == sh, mine_ref[...], in_ref[sh]) for sh in range(N_SHARD)]
        if side_by_side:
            full = jnp.concatenate(blocks, axis=-1)
            out_ref[...] = full if order is None else _take_cols(full, order)
        else:
            for sh in range(N_SHARD):
                out_ref[sh] = blocks[sh]

    if side_by_side:
        out_spec = pl.BlockSpec((None, tr, N_SHARD * w), lambda l, i: (l, i, 0))
        out_shape = jax.ShapeDtypeStruct((nl, r, N_SHARD * w), g4.dtype)
    else:
        out_spec = pl.BlockSpec((None, N_SHARD, tr, w), lambda l, i: (l, 0, i, 0))
        out_shape = jax.ShapeDtypeStruct((nl, N_SHARD, r, w), g4.dtype)
    return _pallas_call(
        body, name=name, grid=(nl, r // tr),
        in_specs=[pl.BlockSpec((N_SHARD, None, tr, w), lambda l, i: (0, l, i, 0)),
                  pl.BlockSpec((None, tr, w), lambda l, i: (l, i, 0))],
        out_specs=out_spec, out_shape=out_shape,
        compiler_params=_params(dimension_semantics=("arbitrary", "arbitrary")),
    )(g4, mine)


def _all_reduce_small(vec, name):
    r, l = vec.shape

    def body(v_ref, out_ref, gath_ref, send_sems, recv_sems):
        x, y, c = _position()
        me = 4 * x + 2 * y + c
        gath_ref[me] = v_ref[...]
        copies = []
        for k in range(1, N_DEV):
            to = (x ^ (k >> 2), y ^ ((k >> 1) & 1), c ^ (k & 1))
            copies.append(_remote(gath_ref.at[me], gath_ref.at[me], send_sems, recv_sems, k - 1, to))
        for cp in copies:
            cp.start()
        for k in range(1, N_DEV):
            frm = 4 * (x ^ (k >> 2)) + 2 * (y ^ ((k >> 1) & 1)) + (c ^ (k & 1))
            _remote(gath_ref.at[frm], gath_ref.at[frm], send_sems, recv_sems, k - 1, (x, y, c)).wait_recv()
        for cp in copies:
            cp.wait_send()
        total = gath_ref[0]
        for d in range(1, N_DEV):
            total = total + gath_ref[d]
        out_ref[...] = total

    vm = pl.BlockSpec(memory_space=pltpu.VMEM)
    return _pallas_call(
        body, name=name, in_specs=[vm], out_specs=vm, out_shape=jax.ShapeDtypeStruct((r, l), F32),
        scratch_shapes=[pltpu.VMEM((N_DEV, r, l), F32), pltpu.SemaphoreType.DMA((N_DEV - 1,)),
                        pltpu.SemaphoreType.DMA((N_DEV - 1,))],
    )(vec)


def _sum_blocks(blocks, out_dtype, name):
    m, w = blocks[0].shape
    tr = _divisor_tile(m, 1024)

    def body(*refs):
        total = refs[0][...].astype(F32)
        for ref in refs[1:-1]:
            total = total + ref[...].astype(F32)
        refs[-1][...] = total.astype(out_dtype)

    blk = pl.BlockSpec((tr, w), lambda i: (i, 0))
    return _pallas_call(
        body, name=name, grid=(m // tr,), in_specs=[blk] * len(blocks), out_specs=blk,
        out_shape=jax.ShapeDtypeStruct((m, w), out_dtype),
        compiler_params=_params(dimension_semantics=("arbitrary",)),
    )(*blocks)


FIRST_GROUPS = (('ffn1_w_gate', 'ffn1_w_up', 'ffn1_w_down'),)
REST_GROUPS = (('ffn2_w_gate', 'ffn2_w_up', 'ffn2_w_down'), ('w_o',), ('w_in',), ('mla_w_q_b',), ('mla_w_kv_b',))
N_FIRST = len(FIRST_GROUPS)


def _shard_rows(name):
    shape, axis = BIG[name]
    return shape[0] // N_SHARD if axis == 0 else shape[0]


def _group_row_offsets(group):
    return [int(v) for v in np.cumsum([0] + [_shard_rows(n) for n in group])]


def _rope_tables(s):
    pos = jnp.arange(s, dtype=F32)
    inv = 1.0 / (ROPE_THETA ** (jnp.arange(0, MLA_ROPE, 2, dtype=F32) / MLA_ROPE))
    ang = pos[:, None] * inv[None, :]
    return jnp.tile(jnp.cos(ang), (1, SWA_HEADS)), jnp.tile(jnp.sin(ang), (1, SWA_HEADS))


_MIXER_GAINS = ('mix_norm', 'mla_q_a_norm', 'mla_kv_a_norm', 'mla_q_norm', 'mla_k_norm', 'swa_q_norm', 'swa_k_norm')


def _local_step(x, target, small, ex):
    s = x.shape[0]
    cos, sin = _rope_tables(s)
    row = lambda name, l: small[name][l][None, :]
    saved, bigs = [], []
    for l in range(DEPTH):
        big = ex.first_weights(l)
        sv = {'x0': x}
        x, sv['g1'], sv['u1'], got = _ffn_fwd(x, row('ffn1_norm', l), big['ffn1_w_gate'], big['ffn1_w_up'], big['ffn1_w_down'],
                                             f"ffn1_fwd_{l}", ex.gather_behind_ffn1(l))
        big.update(ex.rest_weights(l, got))
        bigs.append(big)
        sv['x1'] = x
        gains = [row(n, l) for n in _MIXER_GAINS]
        mixer_w = (big['w_in'], big['mla_w_q_b'], big['mla_w_kv_b'])
        q_a, k_a, v_a, q_b, k_b, v_b = _pre_fwd(x, gains, *mixer_w, cos, sin, f"pre_fwd_{l}")
        o_a, lse, got = _mla_fwd(q_a, k_a, v_a, f"mla_fwd_{l}", ex.gather_behind_mla(l))
        ex.gathered_behind_mla(l, got)
        kpad = jnp.pad(k_b, ((0, 0), (BLOCK, 0), (0, 0)))
        vpad = jnp.pad(v_b, ((0, 0), (BLOCK, 0), (0, 0)))
        sinks = jnp.broadcast_to(small['swa_sinks'][l].reshape(SWA_KV_HEADS, SWA_GROUP, 1), (SWA_KV_HEADS, SWA_GROUP, 128))
        o_b, lse_b = _swa_fwd(q_b, kpad, vpad, sinks, f"swa_fwd_{l}")
        sv.update(gains=gains, mixer_w=mixer_w, q_a=q_a, k_a=k_a, v_a=v_a, q_b=q_b, kpad=kpad, vpad=vpad, sinks=sinks,
                  o_a=o_a, lse=lse, o_b=o_b, lse_b=lse_b)
        x = _post_fwd(x, o_a, o_b, row('mla_out_norm', l), row('swa_out_norm', l), big['w_o'], f"post_fwd_{l}")
        sv['x2'] = x
        x, sv['g2'], sv['u2'], _ = _ffn_fwd(x, row('ffn2_norm', l), big['ffn2_w_gate'], big['ffn2_w_up'], big['ffn2_w_down'],
                                           f"ffn2_fwd_{l}")
        saved.append(sv)

    dx, loss = _loss_head(x, target, "loss_head")

    gs = {n: [None] * DEPTH for n in SMALL_NAMES}
    t = _attn_tile(s)
    for l in reversed(range(DEPTH)):
        sv, big = saved[l], bigs[l]

        def ffn_back(tag, xin, dy, gate, up, scatter=None):
            (dxi, dgain, nb, act, dgate, dup), got = _ffn_bwd(xin, dy, gate, up, row(tag + '_norm', l), big[tag + '_w_gate'],
                                                             big[tag + '_w_up'], big[tag + '_w_down'], f"{tag}_bwd_{l}", scatter)
            gs[tag + '_norm'][l] = dgain[0]
            group = _matmul_tn(dgate, nb, 1.0, f"{tag}_dwg_{l}", None, 0, 3)
            group = _matmul_tn(dup, nb, 1.0, f"{tag}_dwu_{l}", group, 1, 3)
            group = _matmul_tn(act, dy, 0.5, f"{tag}_dwd_{l}", group, 2, 3)
            return dxi, {(tag + '_w_gate', tag + '_w_up', tag + '_w_down'): group}, got

        dx, rest_grads, _ = ffn_back('ffn2', sv['x2'], dx, sv['g2'], sv['u2'])
        do_a, do_b, dga, dgb, dwo, delta = _post_bwd(dx, sv['o_a'], sv['o_b'], row('mla_out_norm', l), row('swa_out_norm', l),
                                              big['w_o'], f"post_bwd_{l}")
        gs['mla_out_norm'][l], gs['swa_out_norm'][l] = dga[0], dgb[0]
        rest_grads['w_o'] = dwo.reshape(N_SHARD, MIX_WIDTH // N_SHARD, D_MODEL)
        dq_a, dk_a, dv_a, got = _mla_bwd(sv['q_a'], sv['k_a'], sv['v_a'], do_a, sv['lse'], delta, f"mla_bwd_{l}",
                                         ex.scatter_behind_mla(l))
        ex.scattered_behind_mla(l, got)
        dq_b, dkpad, dvpad, dsink = _swa_bwd(sv['q_b'], sv['kpad'], sv['vpad'], sv['sinks'], sv['o_b'], sv['lse_b'], do_b,
                                             f"swa_bwd_{l}")
        gs['swa_sinks'][l] = dsink[:, :SWA_GROUP, 0].reshape(SWA_HEADS)
        cts = [dq_a, dk_a, dv_a, dq_b, dkpad[:, BLOCK:], dvpad[:, BLOCK:]]
        outs = _pre_bwd(sv['x1'], dx, cts, sv['gains'], *sv['mixer_w'], cos, sin, f"pre_bwd_{l}")
        dx = outs[0]
        for n, val in zip(_MIXER_GAINS, outs[1:8]):
            gs[n][l] = val[0]
        rest_grads['w_in'], rest_grads['mla_w_q_b'], rest_grads['mla_w_kv_b'] = outs[8:11]
        ex.grads_ready(l, 'rest', rest_grads)
        dx, first_grads, got = ffn_back('ffn1', sv['x0'], dx, sv['g1'], sv['u1'], ex.scatter_behind_ffn1(l))
        ex.scattered_behind_ffn1(l, got)
        ex.grads_ready(l, 'first', first_grads)
    return loss, dx, gs


class _Exchange:
    def __init__(self, weights, c, chip):
        halves_of = lambda a: a.reshape(a.shape[:-2] + (2, a.shape[-2] // 2, a.shape[-1]))
        self.halves_of, self.c, self.chip = halves_of, c, chip
        self.mine = [[halves_of(jnp.concatenate([weights[n][l].astype(BF16) for n in group], axis=0))
                      for group in FIRST_GROUPS + REST_GROUPS] for l in range(DEPTH)]
        self.ahead, self.begun, self.received = {}, {}, {}

    def _assembled(self, l, which, gathered):
        groups, base = (FIRST_GROUPS, 0) if which == 'first' else (REST_GROUPS, N_FIRST)
        big = {}
        for gi, group in enumerate(groups):
            offs = _group_row_offsets(group)
            _, rh, w = self.mine[l][base + gi].shape
            col_sharded = BIG[group[0]][1] == 1
            full = _assemble(gathered[gi].reshape(N_SHARD, 1, 2 * rh, w), self.mine[l][base + gi].reshape(1, 2 * rh, w),
                             f"assemble_{which}{gi}_{l}", col_sharded, MIXER_ORDERS.get(group[0]))
            for i, n in enumerate(group):
                rows = offs[i + 1] - offs[i]
                if len(group) > 1 and offs[i] % rows == 0:
                    big[n] = ((full, (None, rows, N_SHARD * w), (0, offs[i] // rows, 0)) if col_sharded else
                              (full, (None, N_SHARD, rows, w), (0, 0, offs[i] // rows, 0)))
                elif col_sharded:
                    big[n] = full[0, offs[i]:offs[i + 1]]
                else:
                    big[n] = full[0, :, offs[i]:offs[i + 1]].reshape(BIG[n][0])
        return big

    def first_weights(self, l):
        got = self.ahead[l][:N_FIRST] if l in self.ahead else _all_gather_halves(self.mine[l][:N_FIRST], f"gather_first_{l}")
        return self._assembled(l, 'first', got)

    def gather_behind_ffn1(self, l):
        return None if l in self.ahead else self.mine[l][N_FIRST:]

    def rest_weights(self, l, got):
        return self._assembled(l, 'rest', self.ahead[l][N_FIRST:] if l in self.ahead else got)

    def gather_behind_mla(self, l):
        return self.mine[l + 1] if l + 1 < DEPTH else None

    def gathered_behind_mla(self, l, got):
        if got:
            self.ahead[l + 1] = got

    def grads_ready(self, l, which, grads):
        groups = FIRST_GROUPS if which == 'first' else REST_GROUPS
        parts = [self.halves_of(grads[group] if group in grads else jnp.concatenate([grads[n] for n in group], axis=1))
                 for group in groups]
        from_sibling = _sibling_exchange(parts, f"swap_{which}_{l}", True)
        chip_sums = []
        for gi, (p, got) in enumerate(zip(parts, from_sibling)):
            kept = lax.dynamic_index_in_dim(p, self.c, axis=1, keepdims=False)
            rows = N_SHARD * p.shape[2]
            pair = _sum_blocks([kept.reshape(rows, -1), got.reshape(rows, -1)], BF16, f"sum_pair_{which}{gi}_{l}")
            chip_sums.append(pair.reshape(got.shape))
        self.begun[(l, which)] = chip_sums

    def scatter_behind_mla(self, l):
        return self.begun[(l + 1, 'first')] + self.begun[(l + 1, 'rest')] if l + 1 < DEPTH else None

    def scattered_behind_mla(self, l, got):
        if got:
            self.received[(l + 1, 'first')], self.received[(l + 1, 'rest')] = got[:N_FIRST], got[N_FIRST:]

    def scatter_behind_ffn1(self, l):
        return self.begun[(l, 'rest')] if l == 0 else None

    def scattered_behind_ffn1(self, l, got):
        if got:
            self.received[(l, 'rest')] = got

    def reduced(self):
        keys = sorted(self.begun)
        for key in keys:
            if key not in self.received:
                self.received[key] = _scatter_to_chips(self.begun[key], f"scatter_{key[1]}_{key[0]}")
        halves = []
        for l, which in keys:
            for gi, (cs, got) in enumerate(zip(self.begun[(l, which)], self.received[(l, which)])):
                own = lax.dynamic_index_in_dim(cs, self.chip, axis=0, keepdims=False)
                halves.append(_sum_blocks([own, got[0], got[1], got[2]], F32, f"sum_chips_{which}{gi}_{l}"))
        others = _sibling_exchange(halves, "share_halves", False)
        per_layer, at = {}, 0
        for l, which in keys:
            for group in (FIRST_GROUPS if which == 'first' else REST_GROUPS):
                mine_h, other_h = halves[at], others[at]
                at += 1
                full = jnp.where(self.c == 0, jnp.concatenate([mine_h, other_h]), jnp.concatenate([other_h, mine_h]))
                offs = _group_row_offsets(group)
                for i, n in enumerate(group):
                    per_layer[(n, l)] = full[offs[i]:offs[i + 1]]
        return {n: jnp.stack([per_layer[(n, l)] for l in range(DEPTH)]) for n in BIG_NAMES}


def kernel(x, ffn1_norm, ffn1_w_gate, ffn1_w_up, ffn1_w_down, mix_norm, w_in, mla_q_a_norm, mla_w_q_b, mla_kv_a_norm, mla_w_kv_b, mla_q_norm, mla_k_norm, swa_q_norm, swa_k_norm, swa_sinks, mla_out_norm, swa_out_norm, w_o, ffn2_norm, ffn2_w_gate, ffn2_w_up, ffn2_w_down, loss_target, m_ffn1_norm, m_ffn1_w_gate, m_ffn1_w_up, m_ffn1_w_down, m_mix_norm, m_w_in, m_mla_q_a_norm, m_mla_w_q_b, m_mla_kv_a_norm, m_mla_w_kv_b, m_mla_q_norm, m_mla_k_norm, m_swa_q_norm, m_swa_k_norm, m_swa_sinks, m_mla_out_norm, m_swa_out_norm, m_w_o, m_ffn2_norm, m_ffn2_w_gate, m_ffn2_w_up, m_ffn2_w_down, v_ffn1_norm, v_ffn1_w_gate, v_ffn1_w_up, v_ffn1_w_down, v_mix_norm, v_w_in, v_mla_q_a_norm, v_mla_w_q_b, v_mla_kv_a_norm, v_mla_w_kv_b, v_mla_q_norm, v_mla_k_norm, v_swa_q_norm, v_swa_k_norm, v_swa_sinks, v_mla_out_norm, v_swa_out_norm, v_w_o, v_ffn2_norm, v_ffn2_w_gate, v_ffn2_w_up, v_ffn2_w_down):
    args = dict(locals())
    transposed = lambda a: jnp.swapaxes(a, 1, 2)
    as_kernels_see = lambda n, a: transposed(a) if n in TRANSPOSED else a
    weights = {n: as_kernels_see(n, args[n]) for n in WEIGHT_NAMES}
    mom_m = {n: as_kernels_see(n, args["m_" + n]) for n in WEIGHT_NAMES}
    mom_v = {n: as_kernels_see(n, args["v_" + n]) for n in WEIGHT_NAMES}
    ex = _Exchange(weights, lax.axis_index("c"), 2 * lax.axis_index("x") + lax.axis_index("y"))
    loss, dx, gs = _local_step(x[0], loss_target[0], {n: weights[n] for n in SMALL_NAMES}, ex)

    small_flat = jnp.concatenate([jnp.stack(gs[n]).reshape(-1) for n in SMALL_NAMES] + [loss[0, :1]])
    n_small = small_flat.shape[0]
    lanes = -(-n_small // (8 * 128)) * 128
    small_sum = _all_reduce_small(jnp.pad(small_flat, (0, 8 * lanes - n_small)).reshape(8, lanes), "reduce_small").reshape(-1)
    loss_out = small_sum[n_small - 1]
    grads = ex.reduced()

    packed = lambda d: jnp.pad(jnp.concatenate([d[n].reshape(-1) for n in SMALL_NAMES]), (0, 8 * lanes - n_small + 1)).reshape(8, lanes)
    small_out = _adamw(packed(weights), small_sum.reshape(8, lanes), packed(mom_m), packed(mom_v), "adamw_small")
    deltas, new_m, new_v = {}, {}, {}
    off = 0
    for n in SMALL_NAMES:
        cnt = int(np.prod(weights[n].shape))
        grads[n], deltas[n], new_m[n], new_v[n] = (a.reshape(-1)[off:off + cnt].reshape(weights[n].shape)
                                                    for a in (small_sum, *small_out))
        off += cnt
    for n in BIG_NAMES:
        shp = weights[n].shape
        two_d = (shp[0] * shp[1], shp[2])
        d, nm, nv = _adamw(weights[n].reshape(two_d), grads[n].reshape(two_d), mom_m[n].reshape(two_d),
                           mom_v[n].reshape(two_d), f"adamw_{n}")
        deltas[n], new_m[n], new_v[n] = d.reshape(shp), nm.reshape(shp), nv.reshape(shp)
    for n in TRANSPOSED:
        grads[n], deltas[n], new_m[n], new_v[n] = (transposed(a) for a in (grads[n], deltas[n], new_m[n], new_v[n]))

    return (loss_out, dx[None], *[grads[n] for n in WEIGHT_NAMES], *[deltas[n] for n in WEIGHT_NAMES],
            *[new_m[n] for n in WEIGHT_NAMES], *[new_v[n] for n in WEIGHT_NAMES])
```

```python
import functools

import numpy as np
import jax
import jax.numpy as jnp
from jax import lax
from jax.experimental import pallas as pl
from jax.experimental.pallas import tpu as pltpu

F32 = jnp.float32
BF16 = jnp.bfloat16

D_MODEL = 1024
DEPTH = 2
EPS = 1e-6
ROPE_THETA = 10000.0
BLOCK = 128
MLA_HEADS = 4
MLA_Q_RANK = 256
MLA_KV_RANK = 128
MLA_NOPE = 128
MLA_ROPE = 64
MLA_V = 128
MLA_QK = MLA_NOPE + MLA_ROPE
MLA_WIDTH = MLA_HEADS * MLA_V
SWA_HEADS = 8
SWA_KV_HEADS = 2
SWA_GROUP = SWA_HEADS // SWA_KV_HEADS
SWA_HEAD_DIM = 64
SWA_WIDTH = SWA_HEADS * SWA_HEAD_DIM
MIX_WIDTH = MLA_WIDTH + SWA_WIDTH
IN_SPLITS = (MLA_Q_RANK, MLA_KV_RANK, MLA_ROPE, SWA_WIDTH, SWA_KV_HEADS * SWA_HEAD_DIM, SWA_KV_HEADS * SWA_HEAD_DIM)
IN_COLS = sum(IN_SPLITS)
IN_OFFS = tuple(int(v) for v in np.cumsum((0,) + IN_SPLITS))
D_FF = 2816
MLA_SCALE = MLA_QK ** -0.5
LOG2E = 1.4426950408889634
LN2 = 0.6931471805599453
MLA_QSCALE = MLA_SCALE * LOG2E
SWA_SCALE = SWA_HEAD_DIM ** -0.5
NEG = -1e30

ADAM_LR = 0.001
ADAM_B1 = 0.9
ADAM_B2 = 0.999
ADAM_EPS = 1e-08
ADAM_WD = 0.01
ADAM_STEP = 10

N_SHARD = 4
N_DEV = 8
VMEM_LIMIT = 56 * 1024 * 1024
MESH = pl.DeviceIdType.MESH

WEIGHT_NAMES = ['ffn1_norm', 'ffn1_w_gate', 'ffn1_w_up', 'ffn1_w_down', 'mix_norm', 'w_in', 'mla_q_a_norm', 'mla_w_q_b',
                'mla_kv_a_norm', 'mla_w_kv_b', 'mla_q_norm', 'mla_k_norm', 'swa_q_norm', 'swa_k_norm', 'swa_sinks',
                'mla_out_norm', 'swa_out_norm', 'w_o', 'ffn2_norm', 'ffn2_w_gate', 'ffn2_w_up', 'ffn2_w_down']
TRANSPOSED = ('ffn1_w_gate', 'ffn1_w_up', 'ffn2_w_gate', 'ffn2_w_up')
BIG = {'ffn1_w_gate': ((D_FF, D_MODEL), 0), 'ffn1_w_up': ((D_FF, D_MODEL), 0), 'ffn1_w_down': ((D_FF, D_MODEL), 0),
       'w_in': ((D_MODEL, IN_COLS), 1), 'mla_w_q_b': ((MLA_Q_RANK, MLA_HEADS * MLA_QK), 1),
       'mla_w_kv_b': ((MLA_KV_RANK, MLA_HEADS * (MLA_NOPE + MLA_V)), 1), 'w_o': ((MIX_WIDTH, D_MODEL), 0),
       'ffn2_w_gate': ((D_FF, D_MODEL), 0), 'ffn2_w_up': ((D_FF, D_MODEL), 0), 'ffn2_w_down': ((D_FF, D_MODEL), 0)}
BIG_NAMES = [n for n in WEIGHT_NAMES if n in BIG]
SMALL_NAMES = [n for n in WEIGHT_NAMES if n not in BIG]

_pallas_call = pl.pallas_call


def _params(**kw):
    return pltpu.CompilerParams(vmem_limit_bytes=VMEM_LIMIT, **kw)


def _full(shape):
    n = len(shape)
    return pl.BlockSpec(shape, lambda *_: (0,) * n)


def _resident(shape):
    n = len(shape)
    return pl.BlockSpec(shape, lambda *_: (0,) * n, pipeline_mode=pl.Buffered(1))


@jax.custom_vjp
def _mm(a, w):
    return jnp.dot(a.astype(BF16), w, preferred_element_type=F32)


def _mm_fwd(a, w):
    return _mm(a, w), w


def _mm_bwd(w, dy):
    return lax.dot_general(dy.astype(BF16), w, (((1,), (1,)), ((), ())), preferred_element_type=F32), None


_mm.defvjp(_mm_fwd, _mm_bwd)


def _dot_nt(a, b):
    return lax.dot_general(a, b, (((1,), (1,)), ((), ())), preferred_element_type=F32)


def _dot_tn(a, b):
    return lax.dot_general(a, b, (((0,), (0,)), ((), ())), preferred_element_type=F32)


def _rms(t, g):
    return t * lax.rsqrt(jnp.mean(t * t, axis=-1, keepdims=True) + EPS) * g


def _sigmoid(z):
    return 1.0 / (1.0 + jnp.exp(-z))


def _row_tile(s, want):
    return min(want, s)


def _divisor_tile(rows, cap):
    return max(d for d in range(16, min(rows, cap) + 1, 16) if rows % d == 0)


FF_CHUNK = D_FF


def _weight_operand(w):
    if isinstance(w, tuple):
        arr, block, index = w
        return arr, pl.BlockSpec(block, lambda *_: index, pipeline_mode=pl.Buffered(1))
    return w, _resident(w.shape)


def _weight_rows(ref, start, n):
    if len(ref.shape) == 2:
        return ref[start:start + n, :]
    per = ref.shape[1]
    return ref[start // per:(start + n) // per].reshape(n, ref.shape[2])


def _ffn_fwd(x, g, wg, wu, wd, name, gather=None):
    s = x.shape[0]
    tm = _row_tile(s, 256)
    steps = s // tm
    ng = len(gather) if gather else 0
    (wg, wg_spec), (wu, wu_spec), (wd, wd_spec) = _weight_operand(wg), _weight_operand(wu), _weight_operand(wd)

    def body(x_ref, g_ref, wg_ref, wu_ref, wd_ref, *rest):
        g_ins, (y_ref, gate_ref, up_ref), g_outs, sems = rest[:ng], rest[ng:ng + 3], rest[ng + 3:2 * ng + 3], rest[2 * ng + 3:]
        if ng:
            _run_stages_at(_gather_stages(g_ins, g_outs, *sems), [(0,), (steps * 3 // 8,), (steps * 11 // 16,), (steps - 1,)])
        xv = x_ref[...]
        nb = _rms(xv, g_ref[...]).astype(BF16)
        acc = xv
        for c in range(0, D_FF, FF_CHUNK):
            gate = _dot_nt(nb, _weight_rows(wg_ref, c, FF_CHUNK))
            up = _dot_nt(nb, _weight_rows(wu_ref, c, FF_CHUNK))
            gate_ref[:, c:c + FF_CHUNK] = gate.astype(BF16)
            up_ref[:, c:c + FF_CHUNK] = up.astype(BF16)
            act = (gate * _sigmoid(gate) * up).astype(BF16)
            acc = acc + 0.5 * jnp.dot(act, _weight_rows(wd_ref, c, FF_CHUNK), preferred_element_type=F32)
        y_ref[...] = acc

    outs = _pallas_call(
        body, name=name, grid=(steps,),
        in_specs=[pl.BlockSpec((tm, D_MODEL), lambda i: (i, 0)), _full((1, D_MODEL)), wg_spec, wu_spec, wd_spec] + [_HBM] * ng,
        out_specs=[pl.BlockSpec((tm, D_MODEL), lambda i: (i, 0)), pl.BlockSpec((tm, D_FF), lambda i: (i, 0)),
                   pl.BlockSpec((tm, D_FF), lambda i: (i, 0))] + [_HBM] * ng,
        out_shape=[jax.ShapeDtypeStruct((s, D_MODEL), F32), jax.ShapeDtypeStruct((s, D_FF), BF16),
                   jax.ShapeDtypeStruct((s, D_FF), BF16)] + (_gather_out_shapes(gather) if ng else []),
        scratch_shapes=_exchange_scratch(8, ng) if ng else [],
        compiler_params=_params(dimension_semantics=("arbitrary",)),
    )(x, g, wg, wu, wd, *(gather or []))
    return outs[0], outs[1], outs[2], outs[3:]


def _ffn_bwd(x, dy, gate, up, g, wg, wu, wd, name, scatter=None):
    s = x.shape[0]
    tm = _row_tile(s, 256)
    steps = s // tm
    ng = len(scatter) if scatter else 0
    (wg, wg_spec), (wu, wu_spec), (wd, wd_spec) = _weight_operand(wg), _weight_operand(wu), _weight_operand(wd)

    def body(x_ref, dy_ref, gate_ref, up_ref, g_ref, wg_ref, wu_ref, wd_ref, *rest):
        c_ins, (dx_ref, dgain_ref, n_ref, act_ref, dgate_ref, dup_ref) = rest[:ng], rest[ng:ng + 6]
        c_outs, sems = rest[ng + 6:2 * ng + 6], rest[2 * ng + 6:]
        if ng:
            _run_stages_at(_scatter_stages(c_ins, c_outs, *sems), [(0,), (steps - 1,)])
        i = pl.program_id(0)
        xv = x_ref[...]
        dyv = dy_ref[...]
        gv = g_ref[...]
        r = lax.rsqrt(jnp.mean(xv * xv, axis=-1, keepdims=True) + EPS)
        xh = xv * r
        n_ref[...] = (xh * gv).astype(BF16)
        dyh = (0.5 * dyv).astype(BF16)
        dn = jnp.zeros_like(xv)
        for c in range(0, D_FF, FF_CHUNK):
            dact = _dot_nt(dyh, _weight_rows(wd_ref, c, FF_CHUNK))
            gt = gate_ref[:, c:c + FF_CHUNK].astype(F32)
            u = up_ref[:, c:c + FF_CHUNK].astype(F32)
            sg = _sigmoid(gt)
            sl = gt * sg
            act_ref[:, c:c + FF_CHUNK] = (sl * u).astype(BF16)
            dup = (dact * sl).astype(BF16)
            dgate = (dact * u * (sg * (1.0 + gt * (1.0 - sg)))).astype(BF16)
            dup_ref[:, c:c + FF_CHUNK] = dup
            dgate_ref[:, c:c + FF_CHUNK] = dgate
            dn = (dn + jnp.dot(dgate, _weight_rows(wg_ref, c, FF_CHUNK), preferred_element_type=F32)
                  + jnp.dot(dup, _weight_rows(wu_ref, c, FF_CHUNK), preferred_element_type=F32))
        part = jnp.sum(dn * xh, axis=0, keepdims=True)

        @pl.when(i == 0)
        def _():
            dgain_ref[...] = part

        @pl.when(i > 0)
        def _():
            dgain_ref[...] += part

        dxh = dn * gv
        dx_ref[...] = dyv + r * (dxh - xh * jnp.mean(dxh * xh, axis=-1, keepdims=True))

    row = lambda w: pl.BlockSpec((tm, w), lambda i: (i, 0))
    outs = _pallas_call(
        body, name=name, grid=(steps,),
        in_specs=[row(D_MODEL), row(D_MODEL), row(D_FF), row(D_FF), _full((1, D_MODEL)), wg_spec, wu_spec, wd_spec]
        + [_HBM] * ng,
        out_specs=[row(D_MODEL), _full((1, D_MODEL)), row(D_MODEL), row(D_FF), row(D_FF), row(D_FF)] + [_HBM] * ng,
        out_shape=[jax.ShapeDtypeStruct((s, D_MODEL), F32), jax.ShapeDtypeStruct((1, D_MODEL), F32),
                   jax.ShapeDtypeStruct((s, D_MODEL), BF16), jax.ShapeDtypeStruct((s, D_FF), BF16),
                   jax.ShapeDtypeStruct((s, D_FF), BF16), jax.ShapeDtypeStruct((s, D_FF), BF16)]
        + (_scatter_out_shapes(scatter) if ng else []),
        scratch_shapes=_exchange_scratch(3, ng) if ng else [],
        compiler_params=_params(dimension_semantics=("arbitrary",)),
    )(x, dy, gate, up, g, wg, wu, wd, *(scatter or []))
    return outs[:6], outs[6:]


def _store_col_shards(o_ref, acc, first_shard, n_here, width):
    for q in range(n_here):
        o_ref[q] = acc[:, (first_shard + q) * width:(first_shard + q + 1) * width].astype(BF16)


def _matmul_tn(a, b, scale, name, group=None, slot=0, slots=1, scatter=None):
    t, m = a.shape
    n = b.shape[1]
    tk = _row_tile(t, 2048)
    tn = n // 2
    nk = t // tk
    per = m // N_SHARD
    earlier = [] if group is None else [group]
    ne, ng = len(earlier), len(scatter) if scatter else 0

    def body(a_ref, b_ref, *rest):
        c_ins, o_ref, c_outs = rest[ne:ne + ng], rest[ne + ng], rest[ne + ng + 1:ne + 2 * ng + 1]
        acc_ref, sems = rest[ne + 2 * ng + 1], rest[ne + 2 * ng + 2:]
        if ng:
            _run_stages_at(_scatter_stages(c_ins, c_outs, *sems), [(0, 0), (n // tn - 1, nk - 1)])
        k = pl.program_id(1)
        bv = b_ref[...]
        if scale != 1.0:
            bv = bv.astype(F32) * scale
        part = _dot_tn(a_ref[...].astype(BF16), bv.astype(BF16))

        @pl.when(k == 0)
        def _():
            acc_ref[...] = part

        @pl.when(k > 0)
        def _():
            acc_ref[...] += part

        @pl.when(k == nk - 1)
        def _():
            for sh in range(N_SHARD):
                o_ref[sh] = acc_ref[sh * per:(sh + 1) * per, :].astype(BF16)

    outs = _pallas_call(
        body, name=name, grid=(n // tn, nk),
        in_specs=[pl.BlockSpec((tk, m), lambda j, k: (k, 0)), pl.BlockSpec((tk, tn), lambda j, k: (k, j))]
        + [pl.BlockSpec(memory_space=pl.ANY)] * ne + [_HBM] * ng,
        out_specs=[pl.BlockSpec((N_SHARD, per, tn), lambda j, k: (0, slot, j))] + [_HBM] * ng,
        out_shape=[jax.ShapeDtypeStruct((N_SHARD, slots * per, n), BF16)] + (_scatter_out_shapes(scatter) if ng else []),
        scratch_shapes=[pltpu.VMEM((m, tn), F32)] + (_exchange_scratch(3, ng) if ng else []),
        input_output_aliases={2: 0} if earlier else {},
        compiler_params=_params(dimension_semantics=("arbitrary", "arbitrary")),
    )(a, b, *earlier, *(scatter or []))
    return outs[0], outs[1:]


_HALF = SWA_HEAD_DIM // 2
_IN_ORDER = (list(range(0, IN_OFFS[2]))
             + [IN_OFFS[3] + SWA_HEAD_DIM * h + i for h in range(SWA_HEADS) for i in range(_HALF)]
             + [IN_OFFS[3] + SWA_HEAD_DIM * h + _HALF + i for h in range(SWA_HEADS) for i in range(_HALF)]
             + list(range(IN_OFFS[5], IN_OFFS[6]))
             + [IN_OFFS[4] + SWA_HEAD_DIM * j + i for j in range(SWA_KV_HEADS) for i in range(_HALF)]
             + [IN_OFFS[4] + SWA_HEAD_DIM * j + _HALF + i for j in range(SWA_KV_HEADS) for i in range(_HALF)]
             + list(range(IN_OFFS[2], IN_OFFS[3])))
_QB_ORDER = ([MLA_QK * h + i for h in range(MLA_HEADS) for i in range(MLA_NOPE)]
             + [MLA_QK * h + MLA_NOPE + i for h in range(MLA_HEADS) for i in range(_HALF)]
             + [MLA_QK * h + MLA_NOPE + _HALF + i for h in range(MLA_HEADS) for i in range(_HALF)])
_KVB_ORDER = ([(MLA_NOPE + MLA_V) * h + i for h in range(MLA_HEADS) for i in range(MLA_NOPE)]
              + [(MLA_NOPE + MLA_V) * h + MLA_NOPE + i for h in range(MLA_HEADS) for i in range(MLA_V)])
MIXER_ORDERS = {'w_in': _IN_ORDER, 'mla_w_q_b': _QB_ORDER, 'mla_w_kv_b': _KVB_ORDER}
_P_CQ, _P_CKV, _P_QA, _P_QB, _P_VS, _P_KA, _P_KB, _P_PE = (int(v) for v in np.cumsum(
    (0, MLA_Q_RANK, MLA_KV_RANK, SWA_WIDTH // 2, SWA_WIDTH // 2, IN_SPLITS[5], IN_SPLITS[4] // 2, IN_SPLITS[4] // 2)))


def _runs(order):
    out, start = [], 0
    for i in range(1, len(order) + 1):
        if i == len(order) or order[i] != order[i - 1] + 1:
            out.append((order[start], i - start))
            start = i
    return out


def _inverse(order):
    inv = [0] * len(order)
    for new, old in enumerate(order):
        inv[old] = new
    return inv


def _take_cols(a, order):
    return jnp.concatenate([a[..., st:st + w] for st, w in _runs(order)], axis=-1)


def _segment_matrix(n, seg):
    return (lax.broadcasted_iota(jnp.int32, (n, n), 0) // seg == lax.broadcasted_iota(jnp.int32, (n, n), 1) // seg).astype(BF16)


def _cmm2(t, b):
    hi = t.astype(BF16)
    lo = (t - hi.astype(F32)).astype(BF16)
    return jnp.dot(hi, b, preferred_element_type=F32) + jnp.dot(lo, b, preferred_element_type=F32)


@jax.custom_vjp
def _cmm(t, b, bt):
    return jnp.dot(t.astype(BF16), b, preferred_element_type=F32)


def _cmm_fwd(t, b, bt):
    return _cmm(t, b, bt), (b, bt)


def _cmm_bwd(res, dy):
    b, bt = res
    return _cmm(dy, bt, b), None, None


_cmm.defvjp(_cmm_fwd, _cmm_bwd)


def _segsum(t, b):
    return _cmm(t, b, b)


def _rowsum(t, exact=False):
    n = t.shape[-1]
    if exact:
        return _cmm2(t, jnp.ones((n, 128), BF16))
    return _cmm(t, jnp.ones((n, 128), BF16), jnp.ones((128, n), BF16))


def _by_head(vals, width):
    lane = lax.broadcasted_iota(jnp.int32, (vals[0].shape[0], len(vals) * width), 1)
    out = vals[-1]
    for hd in range(len(vals) - 2, -1, -1):
        out = jnp.where(lane < (hd + 1) * width, vals[hd], out)
    return out


def _rope2(a, b, cos, sin):
    return a * cos - b * sin, b * cos + a * sin


def _pre_math(x, gm, gqa, gkva, gq, gk, gsq, gsk, taps, win, wqb, wkvb, cos, sin):
    h = _rms(x, gm)
    proj = _mm(h, win)
    if taps is not None:
        proj = proj + taps[0]
    cqn = _rms(proj[:, _P_CQ:_P_CKV], gqa)
    qa_all = _mm(cqn, wqb)
    ckvn = _rms(proj[:, _P_CKV:_P_QA], gkva)
    kv_all = _mm(ckvn, wkvb)
    if taps is not None:
        qa_all = qa_all + taps[1]
        kv_all = kv_all + taps[2]
    nh, hw = MLA_HEADS, MLA_HEADS * _HALF
    seg_mla = _segment_matrix(hw, _HALF)
    tile = lambda g, n: jnp.concatenate([g] * n, axis=-1)
    c4, s4 = cos[:, :hw], sin[:, :hw]

    def mla_heads(nope, r1, r2, gain):
        rr = r1 * r1 + r2 * r2
        lane_head = lax.broadcasted_iota(jnp.int32, (hw, nh * MLA_NOPE), 0) // _HALF
        spread = (lane_head == lax.broadcasted_iota(jnp.int32, (hw, nh * MLA_NOPE), 1) // MLA_NOPE).astype(BF16)
        rope_on_nope = _cmm(rr, spread, spread.T)
        ss_nope = [_rowsum(jnp.square(nope[:, hd * MLA_NOPE:(hd + 1) * MLA_NOPE])) for hd in range(nh)]
        rinv = [lax.rsqrt((ss_nope[hd] + rope_on_nope[:, hd * MLA_NOPE:(hd + 1) * MLA_NOPE]) * (1.0 / MLA_QK) + EPS)
                for hd in range(nh)]
        rl = lax.rsqrt((_segsum(rr, seg_mla) + _by_head(ss_nope, _HALF)) * (1.0 / MLA_QK) + EPS)
        o1, o2 = _rope2(r1 * rl * tile(gain[:, MLA_NOPE:MLA_NOPE + _HALF], nh), r2 * rl * tile(gain[:, MLA_NOPE + _HALF:], nh), c4, s4)
        return [jnp.concatenate([nope[:, hd * MLA_NOPE:(hd + 1) * MLA_NOPE] * rinv[hd] * gain[:, :MLA_NOPE],
                                 o1[:, hd * _HALF:(hd + 1) * _HALF], o2[:, hd * _HALF:(hd + 1) * _HALF]], axis=-1)
                for hd in range(nh)]

    q_a = mla_heads(qa_all[:, :nh * MLA_NOPE], qa_all[:, nh * MLA_NOPE:nh * MLA_NOPE + hw], qa_all[:, nh * MLA_NOPE + hw:], gq)
    pe1, pe2 = proj[:, _P_PE:_P_PE + _HALF], proj[:, _P_PE + _HALF:_P_PE + 2 * _HALF]
    k_a = mla_heads(kv_all[:, :nh * MLA_NOPE], tile(pe1, nh), tile(pe2, nh), gk)
    v_a = [kv_all[:, nh * MLA_NOPE + hd * MLA_V:nh * MLA_NOPE + (hd + 1) * MLA_V] for hd in range(nh)]

    def swa_heads(a, b, gain, n):
        w = n * _HALF
        r = lax.rsqrt(_segsum(a * a + b * b, _segment_matrix(w, _HALF)) * (1.0 / SWA_HEAD_DIM) + EPS)
        o1, o2 = _rope2(a * r * tile(gain[:, :_HALF], n), b * r * tile(gain[:, _HALF:], n), cos[:, :w], sin[:, :w])
        return [jnp.concatenate([o1[:, hd * _HALF:(hd + 1) * _HALF], o2[:, hd * _HALF:(hd + 1) * _HALF]], axis=-1) for hd in range(n)]

    q_b = swa_heads(proj[:, _P_QA:_P_QB], proj[:, _P_QB:_P_VS], gsq, SWA_HEADS)
    k_b = swa_heads(proj[:, _P_KA:_P_KB], proj[:, _P_KB:_P_PE], gsk, SWA_KV_HEADS)
    v_b = [proj[:, _P_VS + j * SWA_HEAD_DIM:_P_VS + (j + 1) * SWA_HEAD_DIM] for j in range(SWA_KV_HEADS)]
    return (q_a, k_a, v_a, q_b, k_b, v_b), (h, cqn, ckvn)


_PRE_GAIN_WIDTHS = (D_MODEL, MLA_Q_RANK, MLA_KV_RANK, MLA_QK, MLA_QK, SWA_HEAD_DIM, SWA_HEAD_DIM)
_PRE_HEADS = ((MLA_HEADS, MLA_QK), (MLA_HEADS, MLA_QK), (MLA_HEADS, MLA_V),
              (SWA_HEADS, SWA_HEAD_DIM), (SWA_KV_HEADS, SWA_HEAD_DIM), (SWA_KV_HEADS, SWA_HEAD_DIM))


def _pre_fwd(x, gains, win, wqb, wkvb, cos, sin, name):
    s = x.shape[0]
    tm = _row_tile(s, 512)

    def body(x_ref, *refs):
        g_refs, (win_ref, wqb_ref, wkvb_ref, cos_ref, sin_ref), out_refs = refs[:7], refs[7:12], refs[12:]
        outs, _ = _pre_math(x_ref[...], *[g[...] for g in g_refs], None, win_ref[...], wqb_ref[...], wkvb_ref[...],
                            cos_ref[...], sin_ref[...])
        for idx, (ref, heads) in enumerate(zip(out_refs, outs)):
            for hd, val in enumerate(heads):
                ref[hd] = (val * MLA_QSCALE if idx == 0 else val).astype(BF16)

    heads_spec = lambda nh, w: pl.BlockSpec((nh, tm, w), lambda i: (0, i, 0))
    return _pallas_call(
        body, name=name, grid=(s // tm,),
        in_specs=[pl.BlockSpec((tm, D_MODEL), lambda i: (i, 0))] + [_full((1, w)) for w in _PRE_GAIN_WIDTHS]
        + [_resident(win.shape), _resident(wqb.shape), _resident(wkvb.shape),
           pl.BlockSpec((tm, SWA_HEADS * _HALF), lambda i: (i, 0)), pl.BlockSpec((tm, SWA_HEADS * _HALF), lambda i: (i, 0))],
        out_specs=[heads_spec(nh, w) for nh, w in _PRE_HEADS],
        out_shape=[jax.ShapeDtypeStruct((nh, s, w), BF16) for nh, w in _PRE_HEADS],
        compiler_params=_params(dimension_semantics=("arbitrary",)),
    )(x, *gains, win, wqb, wkvb, cos, sin)


def _pre_bwd(x, dx_res, cts, gains, win, wqb, wkvb, cos, sin, name):
    s = x.shape[0]
    tm = _row_tile(s, 256)
    tap_widths = (IN_COLS, MLA_HEADS * MLA_QK, MLA_HEADS * (MLA_NOPE + MLA_V))

    def body(x_ref, dxr_ref, *refs):
        ct_refs, g_refs = refs[:6], refs[6:13]
        win_ref, wqb_ref, wkvb_ref, cos_ref, sin_ref = refs[13:18]
        dx_ref, dg_refs, dw_refs, acc_refs = refs[18], refs[19:26], refs[26:29], refs[29:32]
        i = pl.program_id(0)
        win_v, wqb_v, wkvb_v, cos_v, sin_v = win_ref[...], wqb_ref[...], wkvb_ref[...], cos_ref[...], sin_ref[...]

        def f(xv, gm, gqa, gkva, gq, gk, gsq, gsk, t0, t1, t2):
            return _pre_math(xv, gm, gqa, gkva, gq, gk, gsq, gsk, (t0, t1, t2), win_v, wqb_v, wkvb_v, cos_v, sin_v)

        taps = [jnp.zeros((tm, w), F32) for w in tap_widths]
        _, vjp, acts = jax.vjp(f, x_ref[...], *[g[...] for g in g_refs], *taps, has_aux=True)
        ct = tuple([ref[hd] for hd in range(nh)] for ref, (nh, _) in zip(ct_refs, _PRE_HEADS))
        grads = vjp(ct)
        dx_ref[...] = grads[0] + dxr_ref[...]
        dws = [_dot_tn(a.astype(BF16), t.astype(BF16)) for a, t in zip(acts, grads[8:11])]

        @pl.when(i == 0)
        def _():
            for ref, val in zip(dg_refs, grads[1:8]):
                ref[...] = val
            for ref, val in zip(acc_refs, dws):
                ref[...] = val

        @pl.when(i > 0)
        def _():
            for ref, val in zip(dg_refs, grads[1:8]):
                ref[...] += val
            for ref, val in zip(acc_refs, dws):
                ref[...] += val

        @pl.when(i == s // tm - 1)
        def _():
            for ref, acc, order in zip(dw_refs, acc_refs, (_IN_ORDER, _QB_ORDER, _KVB_ORDER)):
                _store_col_shards(ref, _take_cols(acc[...], _inverse(order)), 0, N_SHARD, acc.shape[1] // N_SHARD)

    heads_spec = lambda nh, w: pl.BlockSpec((nh, tm, w), lambda i: (0, i, 0))
    row = pl.BlockSpec((tm, D_MODEL), lambda i: (i, 0))
    half = pl.BlockSpec((tm, SWA_HEADS * _HALF), lambda i: (i, 0))
    shard_shapes = [(N_SHARD, w.shape[0], w.shape[1] // N_SHARD) for w in (win, wqb, wkvb)]
    return _pallas_call(
        body, name=name, grid=(s // tm,),
        in_specs=[row, row] + [heads_spec(nh, w) for nh, w in _PRE_HEADS] + [_full((1, w)) for w in _PRE_GAIN_WIDTHS]
        + [_resident(win.shape), _resident(wqb.shape), _resident(wkvb.shape), half, half],
        out_specs=[row] + [_full((1, w)) for w in _PRE_GAIN_WIDTHS] + [_full(shp) for shp in shard_shapes],
        out_shape=[jax.ShapeDtypeStruct((s, D_MODEL), F32)] + [jax.ShapeDtypeStruct((1, w), F32) for w in _PRE_GAIN_WIDTHS]
        + [jax.ShapeDtypeStruct(shp, BF16) for shp in shard_shapes],
        scratch_shapes=[pltpu.VMEM(w.shape, F32) for w in (win, wqb, wkvb)],
        compiler_params=_params(dimension_semantics=("arbitrary",)),
    )(x, dx_res, *cts, *gains, win, wqb, wkvb, cos, sin)


def _post_math(oa, ob, ga, gb, wo):
    mixed = jnp.concatenate([_rms(jnp.concatenate(oa, axis=-1), ga), _rms(jnp.concatenate(ob, axis=-1), gb)], axis=-1)
    return _mm(mixed, wo), mixed


def _post_fwd(x, oa, ob, ga, gb, wo, name):
    s = x.shape[0]
    tm = _row_tile(s, 512)

    def body(x_ref, oa_ref, ob_ref, ga_ref, gb_ref, wo_ref, y_ref):
        y, _ = _post_math([oa_ref[hd] for hd in range(MLA_HEADS)], [ob_ref[hd] for hd in range(SWA_HEADS)],
                          ga_ref[...], gb_ref[...], wo_ref[...])
        y_ref[...] = x_ref[...] + y

    row = pl.BlockSpec((tm, D_MODEL), lambda i: (i, 0))
    return _pallas_call(
        body, name=name, grid=(s // tm,),
        in_specs=[row, pl.BlockSpec((MLA_HEADS, tm, MLA_V), lambda i: (0, i, 0)),
                  pl.BlockSpec((SWA_HEADS, tm, SWA_HEAD_DIM), lambda i: (0, i, 0)),
                  _full((1, MLA_WIDTH)), _full((1, SWA_WIDTH)), _resident(wo.shape)],
        out_specs=row, out_shape=jax.ShapeDtypeStruct((s, D_MODEL), F32),
        compiler_params=_params(dimension_semantics=("arbitrary",)),
    )(x, oa, ob, ga, gb, wo)


def _post_bwd(dy, oa, ob, ga, gb, wo, name):
    s = dy.shape[0]
    tm = _row_tile(s, 512)
    t = _attn_tile(s)

    def body(dy_ref, oa_ref, ob_ref, ga_ref, gb_ref, wo_ref, doa_ref, dob_ref, dga_ref, dgb_ref, dwo_ref, delta_ref, acc_ref):
        i = pl.program_id(0)
        wo_v = wo_ref[...]
        dyv = dy_ref[...]

        def f(oa_l, ob_l, ga_v, gb_v):
            return _post_math(oa_l, ob_l, ga_v, gb_v, wo_v)

        _, vjp, mixed = jax.vjp(f, [oa_ref[hd] for hd in range(MLA_HEADS)], [ob_ref[hd] for hd in range(SWA_HEADS)],
                                ga_ref[...], gb_ref[...], has_aux=True)
        doa, dob, dga, dgb = vjp(dyv)
        for hd in range(MLA_HEADS):
            doa_ref[hd] = doa[hd]
            rows = _as_rows(_rowsum(doa[hd] * oa_ref[hd], exact=True))
            for j in range(tm // t):
                delta_ref[hd, j] = rows[:, j * t:(j + 1) * t]
        for hd in range(SWA_HEADS):
            dob_ref[hd] = dob[hd]
        dwo = _dot_tn(mixed.astype(BF16), dyv.astype(BF16))

        @pl.when(i == 0)
        def _():
            dga_ref[...] = dga
            dgb_ref[...] = dgb
            acc_ref[...] = dwo

        @pl.when(i > 0)
        def _():
            dga_ref[...] += dga
            dgb_ref[...] += dgb
            acc_ref[...] += dwo

        @pl.when(i == s // tm - 1)
        def _():
            dwo_ref[...] = acc_ref[...].astype(BF16)

    row = pl.BlockSpec((tm, D_MODEL), lambda i: (i, 0))
    oa_spec = pl.BlockSpec((MLA_HEADS, tm, MLA_V), lambda i: (0, i, 0))
    ob_spec = pl.BlockSpec((SWA_HEADS, tm, SWA_HEAD_DIM), lambda i: (0, i, 0))
    return _pallas_call(
        body, name=name, grid=(s // tm,),
        in_specs=[row, oa_spec, ob_spec, _full((1, MLA_WIDTH)), _full((1, SWA_WIDTH)), _resident(wo.shape)],
        out_specs=[oa_spec, ob_spec, _full((1, MLA_WIDTH)), _full((1, SWA_WIDTH)), _full(wo.shape),
                   pl.BlockSpec((MLA_HEADS, tm // t, 8, t), lambda i: (0, i, 0, 0))],
        out_shape=[jax.ShapeDtypeStruct((MLA_HEADS, s, MLA_V), F32), jax.ShapeDtypeStruct((SWA_HEADS, s, SWA_HEAD_DIM), F32),
                   jax.ShapeDtypeStruct((1, MLA_WIDTH), F32), jax.ShapeDtypeStruct((1, SWA_WIDTH), F32),
                   jax.ShapeDtypeStruct(wo.shape, BF16), jax.ShapeDtypeStruct((MLA_HEADS, s // t, 8, t), F32)],
        scratch_shapes=[pltpu.VMEM(wo.shape, F32)],
        compiler_params=_params(dimension_semantics=("arbitrary",)),
    )(dy, oa, ob, ga, gb, wo)


def _attn_tile(s):
    return 512 if s >= 2048 else 128


def _as_rows(cols):
    return cols.T[0:8, :]


def _causal_mask(t):
    return lax.broadcasted_iota(jnp.int32, (t, t), 1) <= lax.broadcasted_iota(jnp.int32, (t, t), 0)


def _pipelined_blocks(first, count, last_block, issue, consume, carry, prefetch_after):
    def clamped(j, slot):
        issue(jnp.minimum(j, last_block), slot)

    def pair(jj, c):
        a = first + 2 * jj
        clamped(a + 1, 1)
        c = consume(a, 0, c)
        clamped(a + 2, 0)
        return consume(a + 1, 1, c)

    clamped(first, 0)
    npairs = count // 2
    carry = lax.fori_loop(0, npairs, pair, carry)

    def odd(c):
        c = consume(first + 2 * npairs, 0, c)
        if prefetch_after:
            clamped(first + count, 0)
        return c

    return lax.cond(count - 2 * npairs == 1, odd, lambda c: c, carry)


def _run_stages_at(stages, steps):
    for stage, step in zip(stages, steps):
        here = pl.program_id(0) == step[0]
        for axis in range(1, len(step)):
            here = here & (pl.program_id(axis) == step[axis])
        pl.when(here)(stage)


def _mla_fwd(q, k, v, name, gather=None):
    nh, s, _ = q.shape
    t = _attn_tile(s)
    nq = s // t
    ng = len(gather) if gather else 0

    def body(q_ref, k_ref, v_ref, *rest):
        g_ins, (o_ref, lse_ref), g_outs = rest[:ng], rest[ng:ng + 2], rest[ng + 2:2 * ng + 2]
        (s0_ref, s1_ref), sems = rest[2 * ng + 2:2 * ng + 4], rest[2 * ng + 4:]
        if ng:
            _run_stages_at(_gather_stages(g_ins, g_outs, *sems), [(0, 0), (nh // 2, 0), (nh - 1, 0), (nh - 1, nq - 1)])
        qi = pl.program_id(1)
        qv = q_ref[...]
        s_refs = (s0_ref, s1_ref)

        def rows(j):
            return pl.ds(pl.multiple_of(j * t, t), t)

        def issue(j, slot):
            s_refs[slot][...] = _dot_nt(qv, k_ref[rows(j), :])

        def consume(j, slot, carry, masked=False):
            m, l, acc = carry
            sc = s_refs[slot][...]
            if masked:
                sc = jnp.where(_causal_mask(t), sc, NEG)
            m_new = jnp.maximum(m, jnp.max(sc, axis=-1, keepdims=True))
            alpha = jnp.exp2(m - m_new)
            p = jnp.exp2(sc - m_new)
            l = alpha * l + jnp.sum(p, axis=-1, keepdims=True)
            acc = alpha * acc + jnp.dot(p.astype(BF16), v_ref[rows(j), :], preferred_element_type=F32)
            return m_new, l, acc

        init = (jnp.full((t, 1), NEG, F32), jnp.zeros((t, 1), F32), jnp.zeros((t, MLA_V), F32))
        carry = _pipelined_blocks(0, qi, nq - 1, issue, consume, init, True)
        m, l, acc = consume(qi, 0, carry, masked=True)
        o_ref[...] = acc / l
        lse_ref[...] = _as_rows(jnp.broadcast_to(m + jnp.log2(l), (t, 128)))

    outs = _pallas_call(
        body, name=name, grid=(nh, nq),
        in_specs=[pl.BlockSpec((None, t, MLA_QK), lambda h, i: (h, i, 0)), pl.BlockSpec((None, s, MLA_QK), lambda h, i: (h, 0, 0)),
                  pl.BlockSpec((None, s, MLA_V), lambda h, i: (h, 0, 0))] + [_HBM] * ng,
        out_specs=[pl.BlockSpec((None, t, MLA_V), lambda h, i: (h, i, 0)), pl.BlockSpec((None, None, 8, t), lambda h, i: (h, i, 0, 0))]
        + [_HBM] * ng,
        out_shape=[jax.ShapeDtypeStruct((nh, s, MLA_V), F32), jax.ShapeDtypeStruct((nh, nq, 8, t), F32)]
        + (_gather_out_shapes(gather) if ng else []),
        scratch_shapes=[pltpu.VMEM((t, t), F32)] * 2 + (_exchange_scratch(8, ng) if ng else []),
        compiler_params=_params(dimension_semantics=("arbitrary", "arbitrary")),
    )(q, k, v, *(gather or []))
    return outs[0], outs[1], outs[2:]


def _mla_bwd(q, k, v, do, lse_row, delta_row, name, scatter=None):
    nh, s, _ = q.shape
    t = _attn_tile(s)
    nq = s // t
    ng = len(scatter) if scatter else 0

    def body(q_ref, k_ref, v_ref, do_ref, lse_ref, delta_ref, *rest):
        c_ins, (dq_ref, dk_ref, dv_ref), c_outs = rest[:ng], rest[ng:ng + 3], rest[ng + 3:2 * ng + 3]
        (s0_ref, s1_ref, dp0_ref, dp1_ref), sems = rest[2 * ng + 3:2 * ng + 7], rest[2 * ng + 7:]
        if ng:
            _run_stages_at(_scatter_stages(c_ins, c_outs, *sems), [(0, 0), (nh - 1, nq - 1)])
        kj = pl.program_id(1)
        kv_, vv = k_ref[...], v_ref[...]
        s_refs, dp_refs = (s0_ref, s1_ref), (dp0_ref, dp1_ref)

        @pl.when(kj == 0)
        def _():
            dq_ref[...] = jnp.zeros_like(dq_ref)

        def rows(i):
            return pl.ds(pl.multiple_of(i * t, t), t)

        def issue(i, slot):
            s_refs[slot][...] = _dot_nt(kv_, q_ref[rows(i), :])
            dp_refs[slot][...] = _dot_nt(vv, do_ref[rows(i), :].astype(BF16))

        def consume(i, slot, carry, masked=False):
            dk, dv = carry
            p = jnp.exp2(s_refs[slot][...] - lse_ref[i][0:1, :])
            if masked:
                p = jnp.where(lax.broadcasted_iota(jnp.int32, (t, t), 0) <= lax.broadcasted_iota(jnp.int32, (t, t), 1), p, 0.0)
            dv = dv + jnp.dot(p.astype(BF16), do_ref[rows(i), :].astype(BF16), preferred_element_type=F32)
            ds = (p * (dp_refs[slot][...] - delta_ref[i][0:1, :])).astype(BF16)
            dk = dk + jnp.dot(ds, q_ref[rows(i), :], preferred_element_type=F32)
            dq_ref[rows(i), :] += _dot_tn(ds, kv_) * MLA_SCALE
            return dk, dv

        issue(kj, 0)
        carry = consume(kj, 0, (jnp.zeros((t, MLA_QK), F32), jnp.zeros((t, MLA_V), F32)), masked=True)
        dk, dv = _pipelined_blocks(kj + 1, nq - 1 - kj, nq - 1, issue, consume, carry, False)
        dk_ref[...] = dk * LN2
        dv_ref[...] = dv

    tile = lambda w: pl.BlockSpec((None, t, w), lambda h, j: (h, j, 0))
    whole = lambda w: pl.BlockSpec((None, s, w), lambda h, j: (h, 0, 0))
    rows_spec = pl.BlockSpec((None, nq, 8, t), lambda h, j: (h, 0, 0, 0))
    outs = _pallas_call(
        body, name=name, grid=(nh, nq),
        in_specs=[whole(MLA_QK), tile(MLA_QK), tile(MLA_V), whole(MLA_V), rows_spec, rows_spec] + [_HBM] * ng,
        out_specs=[whole(MLA_QK), tile(MLA_QK), tile(MLA_V)] + [_HBM] * ng,
        out_shape=[jax.ShapeDtypeStruct((nh, s, MLA_QK), F32), jax.ShapeDtypeStruct((nh, s, MLA_QK), F32),
                   jax.ShapeDtypeStruct((nh, s, MLA_V), F32)] + (_scatter_out_shapes(scatter) if ng else []),
        scratch_shapes=[pltpu.VMEM((t, t), F32)] * 4 + (_exchange_scratch(3, ng) if ng else []),
        compiler_params=_params(dimension_semantics=("arbitrary", "arbitrary")),
    )(q, k, v, do, lse_row, delta_row, *(scatter or []))
    return outs[0], outs[1], outs[2], outs[3:]


def _swa_tile(s):
    return min(s, 8 * BLOCK)


def _swa_specs(tq):
    nb = tq // BLOCK
    grp = lambda w: pl.BlockSpec((SWA_GROUP, tq, w), lambda j, i: (j, i, 0))
    main = pl.BlockSpec((None, tq, SWA_HEAD_DIM), lambda j, i: (j, i, 0))
    tail = pl.BlockSpec((None, BLOCK, SWA_HEAD_DIM), lambda j, i: (j, nb * (i + 1), 0))
    sink = pl.BlockSpec((None, SWA_GROUP, 128), lambda j, i: (j, 0, 0))
    return grp, main, tail, sink


def _swa_band_mask(first):
    shape = (SWA_GROUP * BLOCK, 2 * BLOCK)
    q_rel = (lax.broadcasted_iota(jnp.int32, shape, 0) & (BLOCK - 1)) + BLOCK
    k_rel = lax.broadcasted_iota(jnp.int32, shape, 1)
    dist = q_rel - k_rel
    return (dist >= 0) & (dist < BLOCK) & ((k_rel >= BLOCK) | jnp.logical_not(first))


def _swa_sink_column(sink_ref):
    sk = sink_ref[...]
    return jnp.concatenate([jnp.broadcast_to(sk[g:g + 1, 0:1], (BLOCK, 1)) for g in range(SWA_GROUP)], axis=0)


def _swa_fwd(q, kpad, vpad, sinks, name):
    _, s, _ = q.shape
    tq = _swa_tile(s)
    grp, main, tail, sink = _swa_specs(tq)
    d = SWA_HEAD_DIM

    def body(q_ref, km_ref, kt_ref, vm_ref, vt_ref, sink_ref, o_ref, lse_ref):
        i = pl.program_id(1)
        kall = jnp.concatenate([km_ref[...], kt_ref[...]], axis=0)
        vall = jnp.concatenate([vm_ref[...], vt_ref[...]], axis=0)
        sink_col = _swa_sink_column(sink_ref)
        for b in range(tq // BLOCK):
            lo = b * BLOCK
            valid = _swa_band_mask(i == 0 if b == 0 else False)
            q4 = q_ref[:, lo:lo + BLOCK, :].reshape(SWA_GROUP * BLOCK, d)
            sc = jnp.where(valid, _dot_nt(q4, kall[lo:lo + 2 * BLOCK]) * SWA_SCALE, NEG)
            m = jnp.maximum(jnp.max(sc, axis=-1, keepdims=True), sink_col)
            e = jnp.exp(sc - m)
            den = jnp.sum(e, axis=-1, keepdims=True) + jnp.exp(sink_col - m)
            out = jnp.dot((e * (1.0 / den)).astype(BF16), vall[lo:lo + 2 * BLOCK], preferred_element_type=F32)
            o_ref[:, lo:lo + BLOCK, :] = out.reshape(SWA_GROUP, BLOCK, d)
            lse_ref[:, lo:lo + BLOCK, :] = (m + jnp.log(den)).reshape(SWA_GROUP, BLOCK, 1)

    return _pallas_call(
        body, name=name, grid=(SWA_KV_HEADS, s // tq),
        in_specs=[grp(d), main, tail, main, tail, sink], out_specs=[grp(d), grp(1)],
        out_shape=[jax.ShapeDtypeStruct((SWA_HEADS, s, d), F32), jax.ShapeDtypeStruct((SWA_HEADS, s, 1), F32)],
        compiler_params=_params(dimension_semantics=("arbitrary", "arbitrary")),
    )(q, kpad, kpad, vpad, vpad, sinks)


def _swa_bwd(q, kpad, vpad, sinks, o, lse, do, name):
    _, s, _ = q.shape
    tq = _swa_tile(s)
    grp, main, tail, sink = _swa_specs(tq)
    d = SWA_HEAD_DIM

    def body(q_ref, km_ref, kt_ref, vm_ref, vt_ref, sink_ref, o_ref, lse_ref, do_ref, dq_ref, dk_ref, dv_ref, dsink_ref):
        i = pl.program_id(1)
        kall = jnp.concatenate([km_ref[...], kt_ref[...]], axis=0)
        vall = jnp.concatenate([vm_ref[...], vt_ref[...]], axis=0)
        sink_col = _swa_sink_column(sink_ref)

        @pl.when(i == 0)
        def _():
            dk_ref[...] = jnp.zeros_like(dk_ref)
            dv_ref[...] = jnp.zeros_like(dv_ref)
            dsink_ref[...] = jnp.zeros_like(dsink_ref)

        dsink = jnp.zeros((SWA_GROUP * BLOCK, 1), F32)
        for b in range(tq // BLOCK):
            lo = b * BLOCK
            valid = _swa_band_mask(i == 0 if b == 0 else False)
            rows4 = SWA_GROUP * BLOCK
            q4 = q_ref[:, lo:lo + BLOCK, :].reshape(rows4, d)
            do4 = do_ref[:, lo:lo + BLOCK, :].reshape(rows4, d)
            lse4 = lse_ref[:, lo:lo + BLOCK, :].reshape(rows4, 1)
            delta = jnp.sum(do4 * o_ref[:, lo:lo + BLOCK, :].reshape(rows4, d), axis=-1, keepdims=True)
            kb, vb = kall[lo:lo + 2 * BLOCK], vall[lo:lo + 2 * BLOCK]
            do4b = do4.astype(BF16)
            p = jnp.where(valid, jnp.exp(_dot_nt(q4, kb) * SWA_SCALE - lse4), 0.0)
            ds = (p * (_dot_nt(do4b, vb) - delta) * SWA_SCALE).astype(BF16)
            dq_ref[:, lo:lo + BLOCK, :] = jnp.dot(ds, kb, preferred_element_type=F32).reshape(SWA_GROUP, BLOCK, d)
            band = pl.ds(pl.multiple_of(i * tq, BLOCK) + lo, 2 * BLOCK)
            dk_ref[band, :] += _dot_tn(ds, q4)
            dv_ref[band, :] += _dot_tn(p.astype(BF16), do4b)
            dsink = dsink - jnp.exp(sink_col - lse4) * delta
        per_head = [jnp.broadcast_to(jnp.sum(dsink[g * BLOCK:(g + 1) * BLOCK], axis=0, keepdims=True), (1, 128))
                    for g in range(SWA_GROUP)]
        dsink_ref[...] += jnp.concatenate(per_head + [jnp.zeros((8 - SWA_GROUP, 128), F32)], axis=0)

    acc = pl.BlockSpec((None, s + BLOCK, d), lambda j, i: (j, 0, 0))
    return _pallas_call(
        body, name=name, grid=(SWA_KV_HEADS, s // tq),
        in_specs=[grp(d), main, tail, main, tail, sink, grp(d), grp(1), grp(d)],
        out_specs=[grp(d), acc, acc, pl.BlockSpec((None, 8, 128), lambda j, i: (j, 0, 0))],
        out_shape=[jax.ShapeDtypeStruct((SWA_HEADS, s, d), F32),
                   jax.ShapeDtypeStruct((SWA_KV_HEADS, s + BLOCK, d), F32),
                   jax.ShapeDtypeStruct((SWA_KV_HEADS, s + BLOCK, d), F32),
                   jax.ShapeDtypeStruct((SWA_KV_HEADS, 8, 128), F32)],
        compiler_params=_params(dimension_semantics=("arbitrary", "arbitrary")),
    )(q, kpad, kpad, vpad, vpad, sinks, o, lse, do)


def _loss_head(y, target, name):
    s = y.shape[0]
    tm = _row_tile(s, 512)

    def body(y_ref, t_ref, dy_ref, loss_ref):
        i = pl.program_id(0)
        err = y_ref[...] - t_ref[...]
        dy_ref[...] = err * (1.0 / D_MODEL)
        part = jnp.broadcast_to(0.5 * jnp.sum(jnp.mean(err * err, axis=-1, keepdims=True), axis=0, keepdims=True), (1, 128))

        @pl.when(i == 0)
        def _():
            loss_ref[...] = part

        @pl.when(i > 0)
        def _():
            loss_ref[...] += part

    row = pl.BlockSpec((tm, D_MODEL), lambda i: (i, 0))
    return _pallas_call(
        body, name=name, grid=(s // tm,), in_specs=[row, row], out_specs=[row, _full((1, 128))],
        out_shape=[jax.ShapeDtypeStruct((s, D_MODEL), F32), jax.ShapeDtypeStruct((1, 128), F32)],
        compiler_params=_params(dimension_semantics=("arbitrary",)),
    )(y, target)


def _adamw(w, g, m, v, name):
    rows, cols = w.shape
    tr = rows
    for cand in (512, 256, 128, 64, 32, 16, 8):
        if rows % cand == 0 and rows > cand:
            tr = cand
            break

    def body(w_ref, g_ref, m_ref, v_ref, d_ref, nm_ref, nv_ref):
        gv = g_ref[...]
        nm = ADAM_B1 * m_ref[...] + (1.0 - ADAM_B1) * gv
        nv = ADAM_B2 * v_ref[...] + (1.0 - ADAM_B2) * (gv * gv)
        m_hat = nm / (1.0 - ADAM_B1 ** ADAM_STEP)
        v_hat = nv / (1.0 - ADAM_B2 ** ADAM_STEP)
        d_ref[...] = -ADAM_LR * (m_hat / (jnp.sqrt(v_hat) + ADAM_EPS) + ADAM_WD * w_ref[...])
        nm_ref[...] = nm
        nv_ref[...] = nv

    blk = pl.BlockSpec((tr, cols), lambda i: (i, 0))
    return _pallas_call(
        body, name=name, grid=(rows // tr,), in_specs=[blk] * 4, out_specs=[blk] * 3,
        out_shape=[jax.ShapeDtypeStruct((rows, cols), F32)] * 3,
        compiler_params=_params(dimension_semantics=("arbitrary",)),
    )(w, g, m, v)


def _position():
    return lax.axis_index("x"), lax.axis_index("y"), lax.axis_index("c")


def _remote(src, dst, send_sems, recv_sems, k, to):
    return pltpu.make_async_remote_copy(src_ref=src, dst_ref=dst, send_sem=send_sems.at[k], recv_sem=recv_sems.at[k],
                                        device_id=to, device_id_type=MESH)


_HBM = pl.BlockSpec(memory_space=pltpu.HBM)


def _gather_stages(ins, outs, send_sems, recv_sems):
    na = len(ins)
    x, y, c = _position()
    me, sibling = (x, y, c), (x, y, 1 - c)
    xn, yn, dg = (1 - x, y), (x, 1 - y), (1 - x, 1 - y)

    def slot(a, chip, pc, half=None):
        ref = outs[a].at[4 * chip[0] + 2 * chip[1] + pc]
        if half is None:
            return ref
        rows = ref.shape[0] // 2
        return ref.at[pl.ds(half * rows, rows)]

    def cp(a, k, chip, pc, half, to, src=None):
        dst = slot(a, chip, pc, half)
        return _remote(dst if src is None else src, dst, send_sems, recv_sems, 8 * a + k, to)

    first_hop = [(0, xn), (1, yn)]
    second_hop = [(0, xn, 2, 0, yn), (1, yn, 3, 1, xn)]

    def sends():
        out = []
        for a in range(na):
            out += [cp(a, k, (x, y), c, None, (*to, c), src=ins[a].at[c]) for k, to in first_hop]
            out += [cp(a, fwd_k, frm, c, half, (*to, c)) for _, frm, fwd_k, half, to in second_hop]
            out += [cp(a, 4 + k, frm, c, None, sibling) for k, frm in first_hop]
            out += [cp(a, 6 + half, dg, c, half, sibling) for half in (0, 1)]
        return out

    def stage0():
        for a in range(na):
            for k, to in first_hop:
                cp(a, k, (x, y), c, None, (*to, c), src=ins[a].at[c]).start()

    def stage1():
        for k, frm, fwd_k, half, to in second_hop:
            for a in range(na):
                cp(a, k, frm, c, None, me).wait_recv()
                cp(a, fwd_k, frm, c, half, (*to, c)).start()
                cp(a, 4 + k, frm, c, None, sibling).start()

    def stage2():
        for half in (0, 1):
            for a in range(na):
                cp(a, 2 + half, dg, c, half, me).wait_recv()
                cp(a, 6 + half, dg, c, half, sibling).start()

    def stage3():
        for a in range(na):
            for k, chip, half in ((4, xn, None), (5, yn, None), (6, dg, 0), (7, dg, 1)):
                cp(a, k, chip, 1 - c, half, me).wait_recv()
        for sent in sends():
            sent.wait_send()

    return [stage0, stage1, stage2, stage3]


def _gather_out_shapes(mine):
    return [jax.ShapeDtypeStruct((N_DEV,) + m.shape[1:], m.dtype) for m in mine]


def _exchange_scratch(per_array, na):
    return [pltpu.SemaphoreType.DMA((per_array * na,)), pltpu.SemaphoreType.DMA((per_array * na,))]


def _all_gather_halves(mine, name):
    na = len(mine)

    def body(*refs):
        for stage in _gather_stages(refs[:na], refs[na:2 * na], *refs[2 * na:]):
            stage()

    return _pallas_call(body, name=name, in_specs=[_HBM] * na, out_specs=[_HBM] * na, out_shape=_gather_out_shapes(mine),
                        scratch_shapes=_exchange_scratch(8, na))(*mine)


def _sibling_exchange(parts, name, other_half):
    na = len(parts)

    def body(*refs):
        ins, outs, (send_sems, recv_sems) = refs[:na], refs[na:2 * na], refs[2 * na:]
        x, y, c = _position()
        copies = [_remote(ins[a].at[:, 1 - c] if other_half else ins[a], outs[a], send_sems, recv_sems, a, (x, y, 1 - c))
                  for a in range(na)]
        for cp in copies:
            cp.start()
        for cp in copies:
            cp.wait()

    return _pallas_call(
        body, name=name, in_specs=[_HBM] * na, out_specs=[_HBM] * na,
        out_shape=[jax.ShapeDtypeStruct(p.shape[:1] + p.shape[2:] if other_half else p.shape, p.dtype) for p in parts],
        scratch_shapes=_exchange_scratch(1, na),
    )(*parts)


def _scatter_stages(ins, outs, send_sems, recv_sems):
    na = len(ins)
    x, y, c = _position()
    chips = [(1 - x, y), (x, 1 - y), (1 - x, 1 - y)]

    def copies():
        return [_remote(ins[a].at[2 * px + py], outs[a].at[j], send_sems, recv_sems, 3 * a + j, (px, py, c))
                for a in range(na) for j, (px, py) in enumerate(chips)]

    def start():
        for cp in copies():
            cp.start()

    def wait():
        for cp in copies():
            cp.wait()

    return [start, wait]


def _scatter_out_shapes(parts):
    return [jax.ShapeDtypeStruct((3,) + p.shape[1:], p.dtype) for p in parts]


def _scatter_to_chips(parts, name):
    na = len(parts)

    def body(*refs):
        for stage in _scatter_stages(refs[:na], refs[na:2 * na], *refs[2 * na:]):
            stage()

    return _pallas_call(body, name=name, in_specs=[_HBM] * na, out_specs=[_HBM] * na, out_shape=_scatter_out_shapes(parts),
                        scratch_shapes=_exchange_scratch(3, na))(*parts)


def _assemble(g4, mine, name, side_by_side, order=None):
    _, nl, r, w = g4.shape
    tr = _divisor_tile(r, 640)

    def body(in_ref, mine_ref, out_ref):
        chip = 2 * lax.axis_index("x") + lax.axis_index("y")
        blocks = [jnp.where(chip == sh, mine_ref[...], in_ref[sh]) for sh in range(N_SHARD)]
        if side_by_side:
            full = jnp.concatenate(blocks, axis=-1)
            out_ref[...] = full if order is None else _take_cols(full, order)
        else:
            for sh in range(N_SHARD):
                out_ref[sh] = blocks[sh]

    if side_by_side:
        out_spec = pl.BlockSpec((None, tr, N_SHARD * w), lambda l, i: (l, i, 0))
        out_shape = jax.ShapeDtypeStruct((nl, r, N_SHARD * w), g4.dtype)
    else:
        out_spec = pl.BlockSpec((None, N_SHARD, tr, w), lambda l, i: (l, 0, i, 0))
        out_shape = jax.ShapeDtypeStruct((nl, N_SHARD, r, w), g4.dtype)
    return _pallas_call(
        body, name=name, grid=(nl, r // tr),
        in_specs=[pl.BlockSpec((N_SHARD, None, tr, w), lambda l, i: (0, l, i, 0)),
                  pl.BlockSpec((None, tr, w), lambda l, i: (l, i, 0))],
        out_specs=out_spec, out_shape=out_shape,
        compiler_params=_params(dimension_semantics=("arbitrary", "arbitrary")),
    )(g4, mine)


def _all_reduce_small(vec, name):
    r, l = vec.shape

    def body(v_ref, out_ref, gath_ref, send_sems, recv_sems):
        x, y, c = _position()
        me = 4 * x + 2 * y + c
        gath_ref[me] = v_ref[...]
        copies = []
        for k in range(1, N_DEV):
            to = (x ^ (k >> 2), y ^ ((k >> 1) & 1), c ^ (k & 1))
            copies.append(_remote(gath_ref.at[me], gath_ref.at[me], send_sems, recv_sems, k - 1, to))
        for cp in copies:
            cp.start()
        for k in range(1, N_DEV):
            frm = 4 * (x ^ (k >> 2)) + 2 * (y ^ ((k >> 1) & 1)) + (c ^ (k & 1))
            _remote(gath_ref.at[frm], gath_ref.at[frm], send_sems, recv_sems, k - 1, (x, y, c)).wait_recv()
        for cp in copies:
            cp.wait_send()
        total = gath_ref[0]
        for d in range(1, N_DEV):
            total = total + gath_ref[d]
        out_ref[...] = total

    vm = pl.BlockSpec(memory_space=pltpu.VMEM)
    return _pallas_call(
        body, name=name, in_specs=[vm], out_specs=vm, out_shape=jax.ShapeDtypeStruct((r, l), F32),
        scratch_shapes=[pltpu.VMEM((N_DEV, r, l), F32), pltpu.SemaphoreType.DMA((N_DEV - 1,)),
                        pltpu.SemaphoreType.DMA((N_DEV - 1,))],
    )(vec)


def _sum_blocks(blocks, out_dtype, name):
    m, w = blocks[0].shape
    tr = _divisor_tile(m, 1024)

    def body(*refs):
        total = refs[0][...].astype(F32)
        for ref in refs[1:-1]:
            total = total + ref[...].astype(F32)
        refs[-1][...] = total.astype(out_dtype)

    blk = pl.BlockSpec((tr, w), lambda i: (i, 0))
    return _pallas_call(
        body, name=name, grid=(m // tr,), in_specs=[blk] * len(blocks), out_specs=blk,
        out_shape=jax.ShapeDtypeStruct((m, w), out_dtype),
        compiler_params=_params(dimension_semantics=("arbitrary",)),
    )(*blocks)


FIRST_GROUPS = (('ffn1_w_gate', 'ffn1_w_up', 'ffn1_w_down'),)
REST_GROUPS = (('ffn2_w_gate', 'ffn2_w_up', 'ffn2_w_down'), ('w_o',), ('w_in',), ('mla_w_q_b',), ('mla_w_kv_b',))
N_FIRST = len(FIRST_GROUPS)
GRAD_SETS = {'first_gu': (('ffn1_w_gate', 'ffn1_w_up'),), 'first_d': (('ffn1_w_down',),), 'rest': REST_GROUPS}
GRAD_SET_ORDER = ('first_gu', 'first_d', 'rest')


def _shard_rows(name):
    shape, axis = BIG[name]
    return shape[0] // N_SHARD if axis == 0 else shape[0]


def _group_row_offsets(group):
    return [int(v) for v in np.cumsum([0] + [_shard_rows(n) for n in group])]


def _rope_tables(s):
    pos = jnp.arange(s, dtype=F32)
    inv = 1.0 / (ROPE_THETA ** (jnp.arange(0, MLA_ROPE, 2, dtype=F32) / MLA_ROPE))
    ang = pos[:, None] * inv[None, :]
    return jnp.tile(jnp.cos(ang), (1, SWA_HEADS)), jnp.tile(jnp.sin(ang), (1, SWA_HEADS))


_MIXER_GAINS = ('mix_norm', 'mla_q_a_norm', 'mla_kv_a_norm', 'mla_q_norm', 'mla_k_norm', 'swa_q_norm', 'swa_k_norm')


def _local_step(x, target, small, ex):
    s = x.shape[0]
    cos, sin = _rope_tables(s)
    row = lambda name, l: small[name][l][None, :]
    saved, bigs = [], []
    for l in range(DEPTH):
        big = ex.first_weights(l)
        sv = {'x0': x}
        x, sv['g1'], sv['u1'], got = _ffn_fwd(x, row('ffn1_norm', l), big['ffn1_w_gate'], big['ffn1_w_up'], big['ffn1_w_down'],
                                             f"ffn1_fwd_{l}", ex.gather_behind_ffn1(l))
        big.update(ex.rest_weights(l, got))
        bigs.append(big)
        sv['x1'] = x
        gains = [row(n, l) for n in _MIXER_GAINS]
        mixer_w = (big['w_in'], big['mla_w_q_b'], big['mla_w_kv_b'])
        q_a, k_a, v_a, q_b, k_b, v_b = _pre_fwd(x, gains, *mixer_w, cos, sin, f"pre_fwd_{l}")
        o_a, lse, got = _mla_fwd(q_a, k_a, v_a, f"mla_fwd_{l}", ex.gather_behind_mla(l))
        ex.gathered_behind_mla(l, got)
        kpad = jnp.pad(k_b, ((0, 0), (BLOCK, 0), (0, 0)))
        vpad = jnp.pad(v_b, ((0, 0), (BLOCK, 0), (0, 0)))
        sinks = jnp.broadcast_to(small['swa_sinks'][l].reshape(SWA_KV_HEADS, SWA_GROUP, 1), (SWA_KV_HEADS, SWA_GROUP, 128))
        o_b, lse_b = _swa_fwd(q_b, kpad, vpad, sinks, f"swa_fwd_{l}")
        sv.update(gains=gains, mixer_w=mixer_w, q_a=q_a, k_a=k_a, v_a=v_a, q_b=q_b, kpad=kpad, vpad=vpad, sinks=sinks,
                  o_a=o_a, lse=lse, o_b=o_b, lse_b=lse_b)
        x = _post_fwd(x, o_a, o_b, row('mla_out_norm', l), row('swa_out_norm', l), big['w_o'], f"post_fwd_{l}")
        sv['x2'] = x
        x, sv['g2'], sv['u2'], _ = _ffn_fwd(x, row('ffn2_norm', l), big['ffn2_w_gate'], big['ffn2_w_up'], big['ffn2_w_down'],
                                           f"ffn2_fwd_{l}")
        saved.append(sv)

    dx, loss = _loss_head(x, target, "loss_head")

    gs = {n: [None] * DEPTH for n in SMALL_NAMES}
    t = _attn_tile(s)
    for l in reversed(range(DEPTH)):
        sv, big = saved[l], bigs[l]

        def ffn_back(tag, xin, dy, gate, up, scatter=None):
            (dxi, dgain, nb, act, dgate, dup), got = _ffn_bwd(xin, dy, gate, up, row(tag + '_norm', l), big[tag + '_w_gate'],
                                                             big[tag + '_w_up'], big[tag + '_w_down'], f"{tag}_bwd_{l}", scatter)
            gs[tag + '_norm'][l] = dgain[0]
            names = (tag + '_w_gate', tag + '_w_up', tag + '_w_down')
            if tag == 'ffn2':
                group, _ = _matmul_tn(dgate, nb, 1.0, f"{tag}_dwg_{l}", None, 0, 3)
                group, _ = _matmul_tn(dup, nb, 1.0, f"{tag}_dwu_{l}", group, 1, 3)
                group, _ = _matmul_tn(act, dy, 0.5, f"{tag}_dwd_{l}", group, 2, 3)
                return dxi, {names: group}, got
            group, _ = _matmul_tn(dgate, nb, 1.0, f"{tag}_dwg_{l}", None, 0, 2)
            group, _ = _matmul_tn(dup, nb, 1.0, f"{tag}_dwu_{l}", group, 1, 2)
            ex.grads_ready(l, 'first_gu', {names[:2]: group})
            down, arrived = _matmul_tn(act, dy, 0.5, f"{tag}_dwd_{l}", None, 0, 1, ex.scatter_behind_dwd(l))
            ex.scattered_behind_dwd(l, arrived)
            ex.grads_ready(l, 'first_d', {names[2:]: down})
            return dxi, None, got

        dx, rest_grads, _ = ffn_back('ffn2', sv['x2'], dx, sv['g2'], sv['u2'])
        do_a, do_b, dga, dgb, dwo, delta = _post_bwd(dx, sv['o_a'], sv['o_b'], row('mla_out_norm', l), row('swa_out_norm', l),
                                              big['w_o'], f"post_bwd_{l}")
        gs['mla_out_norm'][l], gs['swa_out_norm'][l] = dga[0], dgb[0]
        rest_grads['w_o'] = dwo.reshape(N_SHARD, MIX_WIDTH // N_SHARD, D_MODEL)
        dq_a, dk_a, dv_a, got = _mla_bwd(sv['q_a'], sv['k_a'], sv['v_a'], do_a, sv['lse'], delta, f"mla_bwd_{l}",
                                         ex.scatter_behind_mla(l))
        ex.scattered_behind_mla(l, got)
        dq_b, dkpad, dvpad, dsink = _swa_bwd(sv['q_b'], sv['kpad'], sv['vpad'], sv['sinks'], sv['o_b'], sv['lse_b'], do_b,
                                             f"swa_bwd_{l}")
        gs['swa_sinks'][l] = dsink[:, :SWA_GROUP, 0].reshape(SWA_HEADS)
        cts = [dq_a, dk_a, dv_a, dq_b, dkpad[:, BLOCK:], dvpad[:, BLOCK:]]
        outs = _pre_bwd(sv['x1'], dx, cts, sv['gains'], *sv['mixer_w'], cos, sin, f"pre_bwd_{l}")
        dx = outs[0]
        for n, val in zip(_MIXER_GAINS, outs[1:8]):
            gs[n][l] = val[0]
        rest_grads['w_in'], rest_grads['mla_w_q_b'], rest_grads['mla_w_kv_b'] = outs[8:11]
        ex.grads_ready(l, 'rest', rest_grads)
        scatter = ex.scatter_behind_ffn1(l)
        dx, _, got = ffn_back('ffn1', sv['x0'], dx, sv['g1'], sv['u1'], scatter)
        ex.scattered_behind_ffn1(l, got)
    return loss, dx, gs


class _Exchange:
    def __init__(self, weights, c, chip):
        halves_of = lambda a: a.reshape(a.shape[:-2] + (2, a.shape[-2] // 2, a.shape[-1]))
        self.halves_of, self.c, self.chip = halves_of, c, chip
        self.mine = [[halves_of(jnp.concatenate([weights[n][l].astype(BF16) for n in group], axis=0))
                      for group in FIRST_GROUPS + REST_GROUPS] for l in range(DEPTH)]
        self.ahead, self.begun, self.received = {}, {}, {}

    def _assembled(self, l, which, gathered):
        groups, base = (FIRST_GROUPS, 0) if which == 'first' else (REST_GROUPS, N_FIRST)
        big = {}
        for gi, group in enumerate(groups):
            offs = _group_row_offsets(group)
            _, rh, w = self.mine[l][base + gi].shape
            col_sharded = BIG[group[0]][1] == 1
            full = _assemble(gathered[gi].reshape(N_SHARD, 1, 2 * rh, w), self.mine[l][base + gi].reshape(1, 2 * rh, w),
                             f"assemble_{which}{gi}_{l}", col_sharded, MIXER_ORDERS.get(group[0]))
            for i, n in enumerate(group):
                rows = offs[i + 1] - offs[i]
                if len(group) > 1 and offs[i] % rows == 0:
                    big[n] = ((full, (None, rows, N_SHARD * w), (0, offs[i] // rows, 0)) if col_sharded else
                              (full, (None, N_SHARD, rows, w), (0, 0, offs[i] // rows, 0)))
                elif col_sharded:
                    big[n] = full[0, offs[i]:offs[i + 1]]
                else:
                    big[n] = full[0, :, offs[i]:offs[i + 1]].reshape(BIG[n][0])
        return big

    def first_weights(self, l):
        got = self.ahead[l][:N_FIRST] if l in self.ahead else _all_gather_halves(self.mine[l][:N_FIRST], f"gather_first_{l}")
        return self._assembled(l, 'first', got)

    def gather_behind_ffn1(self, l):
        return None if l in self.ahead else self.mine[l][N_FIRST:]

    def rest_weights(self, l, got):
        return self._assembled(l, 'rest', self.ahead[l][N_FIRST:] if l in self.ahead else got)

    def gather_behind_mla(self, l):
        return self.mine[l + 1] if l + 1 < DEPTH else None

    def gathered_behind_mla(self, l, got):
        if got:
            self.ahead[l + 1] = got

    def grads_ready(self, l, which, grads):
        groups = GRAD_SETS[which]
        parts = [self.halves_of(grads[group] if group in grads else jnp.concatenate([grads[n] for n in group], axis=1))
                 for group in groups]
        from_sibling = _sibling_exchange(parts, f"swap_{which}_{l}", True)
        chip_sums = []
        for gi, (p, got) in enumerate(zip(parts, from_sibling)):
            kept = lax.dynamic_index_in_dim(p, self.c, axis=1, keepdims=False)
            rows = N_SHARD * p.shape[2]
            pair = _sum_blocks([kept.reshape(rows, -1), got.reshape(rows, -1)], BF16, f"sum_pair_{which}{gi}_{l}")
            chip_sums.append(pair.reshape(got.shape))
        self.begun[(l, which)] = chip_sums

    def scatter_behind_mla(self, l):
        return sum((self.begun[(l + 1, which)] for which in GRAD_SET_ORDER), []) if l + 1 < DEPTH else None

    def scattered_behind_mla(self, l, got):
        at = 0
        for which in GRAD_SET_ORDER if got else ():
            self.received[(l + 1, which)] = got[at:at + len(GRAD_SETS[which])]
            at += len(GRAD_SETS[which])

    def scatter_behind_dwd(self, l):
        return self.begun[(l, 'first_gu')] if l == 0 else None

    def scattered_behind_dwd(self, l, got):
        if got:
            self.received[(l, 'first_gu')] = got

    def scatter_behind_ffn1(self, l):
        return self.begun[(l, 'rest')] if l == 0 else None

    def scattered_behind_ffn1(self, l, got):
        if got:
            self.received[(l, 'rest')] = got

    def reduced(self):
        keys = sorted(self.begun)
        for key in keys:
            if key not in self.received:
                self.received[key] = _scatter_to_chips(self.begun[key], f"scatter_{key[1]}_{key[0]}")
        halves = []
        for l, which in keys:
            for gi, (cs, got) in enumerate(zip(self.begun[(l, which)], self.received[(l, which)])):
                own = lax.dynamic_index_in_dim(cs, self.chip, axis=0, keepdims=False)
                halves.append(_sum_blocks([own, got[0], got[1], got[2]], F32, f"sum_chips_{which}{gi}_{l}"))
        others = _sibling_exchange(halves, "share_halves", False)
        per_layer, at = {}, 0
        for l, which in keys:
            for group in GRAD_SETS[which]:
                mine_h, other_h = halves[at], others[at]
                at += 1
                full = jnp.where(self.c == 0, jnp.concatenate([mine_h, other_h]), jnp.concatenate([other_h, mine_h]))
                offs = _group_row_offsets(group)
                for i, n in enumerate(group):
                    per_layer[(n, l)] = full[offs[i]:offs[i + 1]]
        return {n: jnp.stack([per_layer[(n, l)] for l in range(DEPTH)]) for n in BIG_NAMES}


def kernel(x, ffn1_norm, ffn1_w_gate, ffn1_w_up, ffn1_w_down, mix_norm, w_in, mla_q_a_norm, mla_w_q_b, mla_kv_a_norm, mla_w_kv_b, mla_q_norm, mla_k_norm, swa_q_norm, swa_k_norm, swa_sinks, mla_out_norm, swa_out_norm, w_o, ffn2_norm, ffn2_w_gate, ffn2_w_up, ffn2_w_down, loss_target, m_ffn1_norm, m_ffn1_w_gate, m_ffn1_w_up, m_ffn1_w_down, m_mix_norm, m_w_in, m_mla_q_a_norm, m_mla_w_q_b, m_mla_kv_a_norm, m_mla_w_kv_b, m_mla_q_norm, m_mla_k_norm, m_swa_q_norm, m_swa_k_norm, m_swa_sinks, m_mla_out_norm, m_swa_out_norm, m_w_o, m_ffn2_norm, m_ffn2_w_gate, m_ffn2_w_up, m_ffn2_w_down, v_ffn1_norm, v_ffn1_w_gate, v_ffn1_w_up, v_ffn1_w_down, v_mix_norm, v_w_in, v_mla_q_a_norm, v_mla_w_q_b, v_mla_kv_a_norm, v_mla_w_kv_b, v_mla_q_norm, v_mla_k_norm, v_swa_q_norm, v_swa_k_norm, v_swa_sinks, v_mla_out_norm, v_swa_out_norm, v_w_o, v_ffn2_norm, v_ffn2_w_gate, v_ffn2_w_up, v_ffn2_w_down):
    args = dict(locals())
    transposed = lambda a: jnp.swapaxes(a, 1, 2)
    as_kernels_see = lambda n, a: transposed(a) if n in TRANSPOSED else a
    weights = {n: as_kernels_see(n, args[n]) for n in WEIGHT_NAMES}
    mom_m = {n: as_kernels_see(n, args["m_" + n]) for n in WEIGHT_NAMES}
    mom_v = {n: as_kernels_see(n, args["v_" + n]) for n in WEIGHT_NAMES}
    ex = _Exchange(weights, lax.axis_index("c"), 2 * lax.axis_index("x") + lax.axis_index("y"))
    loss, dx, gs = _local_step(x[0], loss_target[0], {n: weights[n] for n in SMALL_NAMES}, ex)

    small_flat = jnp.concatenate([jnp.stack(gs[n]).reshape(-1) for n in SMALL_NAMES] + [loss[0, :1]])
    n_small = small_flat.shape[0]
    lanes = -(-n_small // (8 * 128)) * 128
    small_sum = _all_reduce_small(jnp.pad(small_flat, (0, 8 * lanes - n_small)).reshape(8, lanes), "reduce_small").reshape(-1)
    loss_out = small_sum[n_small - 1]
    grads = ex.reduced()

    packed = lambda d: jnp.pad(jnp.concatenate([d[n].reshape(-1) for n in SMALL_NAMES]), (0, 8 * lanes - n_small + 1)).reshape(8, lanes)
    small_out = _adamw(packed(weights), small_sum.reshape(8, lanes), packed(mom_m), packed(mom_v), "adamw_small")
    deltas, new_m, new_v = {}, {}, {}
    off = 0
    for n in SMALL_NAMES:
        cnt = int(np.prod(weights[n].shape))
        grads[n], deltas[n], new_m[n], new_v[n] = (a.reshape(-1)[off:off + cnt].reshape(weights[n].shape)
                                                    for a in (small_sum, *small_out))
        off += cnt
    for n in BIG_NAMES:
        shp = weights[n].shape
        two_d = (shp[0] * shp[1], shp[2])
        d, nm, nv = _adamw(weights[n].reshape(two_d), grads[n].reshape(two_d), mom_m[n].reshape(two_d),
                           mom_v[n].reshape(two_d), f"adamw_{n}")
        deltas[n], new_m[n], new_v[n] = d.reshape(shp), nm.reshape(shp), nv.reshape(shp)
    for n in TRANSPOSED:
        grads[n], deltas[n], new_m[n], new_v[n] = (transposed(a) for a in (grads[n], deltas[n], new_m[n], new_v[n]))

    return (loss_out, dx[None], *[grads[n] for n in WEIGHT_NAMES], *[deltas[n] for n in WEIGHT_NAMES],
            *[new_m[n] for n in WEIGHT_NAMES], *[new_v[n] for n in WEIGHT_NAMES])
```

```python
import functools

import numpy as np
import jax
import jax.numpy as jnp
from jax import lax
from jax.experimental import pallas as pl
from jax.experimental.pallas import tpu as pltpu

F32 = jnp.float32
BF16 = jnp.bfloat16

D_MODEL = 1024
DEPTH = 2
EPS = 1e-6
ROPE_THETA = 10000.0
BLOCK = 128
MLA_HEADS = 4
MLA_Q_RANK = 256
MLA_KV_RANK = 128
MLA_NOPE = 128
MLA_ROPE = 64
MLA_V = 128
MLA_QK = MLA_NOPE + MLA_ROPE
MLA_WIDTH = MLA_HEADS * MLA_V
SWA_HEADS = 8
SWA_KV_HEADS = 2
SWA_GROUP = SWA_HEADS // SWA_KV_HEADS
SWA_HEAD_DIM = 64
SWA_WIDTH = SWA_HEADS * SWA_HEAD_DIM
MIX_WIDTH = MLA_WIDTH + SWA_WIDTH
IN_SPLITS = (MLA_Q_RANK, MLA_KV_RANK, MLA_ROPE, SWA_WIDTH, SWA_KV_HEADS * SWA_HEAD_DIM, SWA_KV_HEADS * SWA_HEAD_DIM)
IN_COLS = sum(IN_SPLITS)
IN_OFFS = tuple(int(v) for v in np.cumsum((0,) + IN_SPLITS))
D_FF = 2816
MLA_SCALE = MLA_QK ** -0.5
LOG2E = 1.4426950408889634
LN2 = 0.6931471805599453
MLA_QSCALE = MLA_SCALE * LOG2E
SWA_SCALE = SWA_HEAD_DIM ** -0.5
NEG = -1e30

ADAM_LR = 0.001
ADAM_B1 = 0.9
ADAM_B2 = 0.999
ADAM_EPS = 1e-08
ADAM_WD = 0.01
ADAM_STEP = 10

N_SHARD = 4
N_DEV = 8
VMEM_LIMIT = 56 * 1024 * 1024
MESH = pl.DeviceIdType.MESH

WEIGHT_NAMES = ['ffn1_norm', 'ffn1_w_gate', 'ffn1_w_up', 'ffn1_w_down', 'mix_norm', 'w_in', 'mla_q_a_norm', 'mla_w_q_b',
                'mla_kv_a_norm', 'mla_w_kv_b', 'mla_q_norm', 'mla_k_norm', 'swa_q_norm', 'swa_k_norm', 'swa_sinks',
                'mla_out_norm', 'swa_out_norm', 'w_o', 'ffn2_norm', 'ffn2_w_gate', 'ffn2_w_up', 'ffn2_w_down']
TRANSPOSED = ('ffn1_w_gate', 'ffn1_w_up', 'ffn2_w_gate', 'ffn2_w_up')
BIG = {'ffn1_w_gate': ((D_FF, D_MODEL), 0), 'ffn1_w_up': ((D_FF, D_MODEL), 0), 'ffn1_w_down': ((D_FF, D_MODEL), 0),
       'w_in': ((D_MODEL, IN_COLS), 1), 'mla_w_q_b': ((MLA_Q_RANK, MLA_HEADS * MLA_QK), 1),
       'mla_w_kv_b': ((MLA_KV_RANK, MLA_HEADS * (MLA_NOPE + MLA_V)), 1), 'w_o': ((MIX_WIDTH, D_MODEL), 0),
       'ffn2_w_gate': ((D_FF, D_MODEL), 0), 'ffn2_w_up': ((D_FF, D_MODEL), 0), 'ffn2_w_down': ((D_FF, D_MODEL), 0)}
BIG_NAMES = [n for n in WEIGHT_NAMES if n in BIG]
SMALL_NAMES = [n for n in WEIGHT_NAMES if n not in BIG]

_pallas_call = pl.pallas_call


def _params(**kw):
    return pltpu.CompilerParams(vmem_limit_bytes=VMEM_LIMIT, **kw)


def _full(shape):
    n = len(shape)
    return pl.BlockSpec(shape, lambda *_: (0,) * n)


def _resident(shape):
    n = len(shape)
    return pl.BlockSpec(shape, lambda *_: (0,) * n, pipeline_mode=pl.Buffered(1))


@jax.custom_vjp
def _mm(a, w):
    return jnp.dot(a.astype(BF16), w, preferred_element_type=F32)


def _mm_fwd(a, w):
    return _mm(a, w), w


def _mm_bwd(w, dy):
    return lax.dot_general(dy.astype(BF16), w, (((1,), (1,)), ((), ())), preferred_element_type=F32), None


_mm.defvjp(_mm_fwd, _mm_bwd)


def _dot_nt(a, b):
    return lax.dot_general(a, b, (((1,), (1,)), ((), ())), preferred_element_type=F32)


def _dot_tn(a, b):
    return lax.dot_general(a, b, (((0,), (0,)), ((), ())), preferred_element_type=F32)


def _rms(t, g):
    return t * lax.rsqrt(jnp.mean(t * t, axis=-1, keepdims=True) + EPS) * g


def _sigmoid(z):
    return 1.0 / (1.0 + jnp.exp(-z))


def _row_tile(s, want):
    return min(want, s)


def _divisor_tile(rows, cap):
    return max(d for d in range(16, min(rows, cap) + 1, 16) if rows % d == 0)


FF_CHUNK = D_FF


def _weight_operand(w):
    if isinstance(w, tuple):
        arr, block, index = w
        return arr, pl.BlockSpec(block, lambda *_: index, pipeline_mode=pl.Buffered(1))
    return w, _resident(w.shape)


def _weight_rows(ref, start, n):
    if len(ref.shape) == 2:
        return ref[start:start + n, :]
    per = ref.shape[1]
    return ref[start // per:(start + n) // per].reshape(n, ref.shape[2])


def _ffn_fwd(x, g, wg, wu, wd, name, gather=None):
    s = x.shape[0]
    tm = _row_tile(s, 256)
    steps = s // tm
    ng = len(gather) if gather else 0
    (wg, wg_spec), (wu, wu_spec), (wd, wd_spec) = _weight_operand(wg), _weight_operand(wu), _weight_operand(wd)

    def body(x_ref, g_ref, wg_ref, wu_ref, wd_ref, *rest):
        g_ins, (y_ref, gate_ref, up_ref), g_outs, sems = rest[:ng], rest[ng:ng + 3], rest[ng + 3:2 * ng + 3], rest[2 * ng + 3:]
        if ng:
            _run_stages_at(_gather_stages(g_ins, g_outs, *sems), [(0,), (steps * 3 // 8,), (steps * 11 // 16,), (steps - 1,)])
        xv = x_ref[...]
        nb = _rms(xv, g_ref[...]).astype(BF16)
        acc = xv
        for c in range(0, D_FF, FF_CHUNK):
            gate = _dot_nt(nb, _weight_rows(wg_ref, c, FF_CHUNK))
            up = _dot_nt(nb, _weight_rows(wu_ref, c, FF_CHUNK))
            gate_ref[:, c:c + FF_CHUNK] = gate.astype(BF16)
            up_ref[:, c:c + FF_CHUNK] = up.astype(BF16)
            act = (gate * _sigmoid(gate) * up).astype(BF16)
            acc = acc + 0.5 * jnp.dot(act, _weight_rows(wd_ref, c, FF_CHUNK), preferred_element_type=F32)
        y_ref[...] = acc

    outs = _pallas_call(
        body, name=name, grid=(steps,),
        in_specs=[pl.BlockSpec((tm, D_MODEL), lambda i: (i, 0)), _full((1, D_MODEL)), wg_spec, wu_spec, wd_spec] + [_HBM] * ng,
        out_specs=[pl.BlockSpec((tm, D_MODEL), lambda i: (i, 0)), pl.BlockSpec((tm, D_FF), lambda i: (i, 0)),
                   pl.BlockSpec((tm, D_FF), lambda i: (i, 0))] + [_HBM] * ng,
        out_shape=[jax.ShapeDtypeStruct((s, D_MODEL), F32), jax.ShapeDtypeStruct((s, D_FF), BF16),
                   jax.ShapeDtypeStruct((s, D_FF), BF16)] + (_gather_out_shapes(gather) if ng else []),
        scratch_shapes=_exchange_scratch(8, ng) if ng else [],
        compiler_params=_params(dimension_semantics=("arbitrary",)),
    )(x, g, wg, wu, wd, *(gather or []))
    return outs[0], outs[1], outs[2], outs[3:]


def _ffn_bwd(x, dy, gate, up, g, wg, wu, wd, name, scatter=None):
    s = x.shape[0]
    tm = _row_tile(s, 256)
    steps = s // tm
    ng = len(scatter) if scatter else 0
    (wg, wg_spec), (wu, wu_spec), (wd, wd_spec) = _weight_operand(wg), _weight_operand(wu), _weight_operand(wd)

    def body(x_ref, dy_ref, gate_ref, up_ref, g_ref, wg_ref, wu_ref, wd_ref, *rest):
        c_ins, (dx_ref, dgain_ref, n_ref, act_ref, dgate_ref, dup_ref) = rest[:ng], rest[ng:ng + 6]
        c_outs, sems = rest[ng + 6:2 * ng + 6], rest[2 * ng + 6:]
        if ng:
            _run_stages_at(_scatter_stages(c_ins, c_outs, *sems), [(0,), (steps - 1,)])
        i = pl.program_id(0)
        xv = x_ref[...]
        dyv = dy_ref[...]
        gv = g_ref[...]
        r = lax.rsqrt(jnp.mean(xv * xv, axis=-1, keepdims=True) + EPS)
        xh = xv * r
        n_ref[...] = (xh * gv).astype(BF16)
        dyh = (0.5 * dyv).astype(BF16)
        dn = jnp.zeros_like(xv)
        for c in range(0, D_FF, FF_CHUNK):
            dact = _dot_nt(dyh, _weight_rows(wd_ref, c, FF_CHUNK))
            gt = gate_ref[:, c:c + FF_CHUNK].astype(F32)
            u = up_ref[:, c:c + FF_CHUNK].astype(F32)
            sg = _sigmoid(gt)
            sl = gt * sg
            act_ref[:, c:c + FF_CHUNK] = (sl * u).astype(BF16)
            dup = (dact * sl).astype(BF16)
            dgate = (dact * u * (sg * (1.0 + gt * (1.0 - sg)))).astype(BF16)
            dup_ref[:, c:c + FF_CHUNK] = dup
            dgate_ref[:, c:c + FF_CHUNK] = dgate
            dn = (dn + jnp.dot(dgate, _weight_rows(wg_ref, c, FF_CHUNK), preferred_element_type=F32)
                  + jnp.dot(dup, _weight_rows(wu_ref, c, FF_CHUNK), preferred_element_type=F32))
        part = jnp.sum(dn * xh, axis=0, keepdims=True)

        @pl.when(i == 0)
        def _():
            dgain_ref[...] = part

        @pl.when(i > 0)
        def _():
            dgain_ref[...] += part

        dxh = dn * gv
        dx_ref[...] = dyv + r * (dxh - xh * jnp.mean(dxh * xh, axis=-1, keepdims=True))

    row = lambda w: pl.BlockSpec((tm, w), lambda i: (i, 0))
    outs = _pallas_call(
        body, name=name, grid=(steps,),
        in_specs=[row(D_MODEL), row(D_MODEL), row(D_FF), row(D_FF), _full((1, D_MODEL)), wg_spec, wu_spec, wd_spec]
        + [_HBM] * ng,
        out_specs=[row(D_MODEL), _full((1, D_MODEL)), row(D_MODEL), row(D_FF), row(D_FF), row(D_FF)] + [_HBM] * ng,
        out_shape=[jax.ShapeDtypeStruct((s, D_MODEL), F32), jax.ShapeDtypeStruct((1, D_MODEL), F32),
                   jax.ShapeDtypeStruct((s, D_MODEL), BF16), jax.ShapeDtypeStruct((s, D_FF), BF16),
                   jax.ShapeDtypeStruct((s, D_FF), BF16), jax.ShapeDtypeStruct((s, D_FF), BF16)]
        + (_scatter_out_shapes(scatter) if ng else []),
        scratch_shapes=_exchange_scratch(3, ng) if ng else [],
        compiler_params=_params(dimension_semantics=("arbitrary",)),
    )(x, dy, gate, up, g, wg, wu, wd, *(scatter or []))
    return outs[:6], outs[6:]


def _store_col_shards(o_ref, acc, first_shard, n_here, width):
    for q in range(n_here):
        o_ref[q] = acc[:, (first_shard + q) * width:(first_shard + q + 1) * width].astype(BF16)


def _matmul_tn(a, b, scale, name, group=None, slot=0, slots=1, scatter=None):
    t, m = a.shape
    n = b.shape[1]
    tk = _row_tile(t, 2048)
    tn = n // 2
    nk = t // tk
    per = m // N_SHARD
    earlier = [] if group is None else [group]
    ne, ng = len(earlier), len(scatter) if scatter else 0

    def body(a_ref, b_ref, *rest):
        c_ins, o_ref, c_outs = rest[ne:ne + ng], rest[ne + ng], rest[ne + ng + 1:ne + 2 * ng + 1]
        acc_ref, sems = rest[ne + 2 * ng + 1], rest[ne + 2 * ng + 2:]
        if ng:
            _run_stages_at(_scatter_stages(c_ins, c_outs, *sems), [(0, 0), (n // tn - 1, nk - 1)])
        k = pl.program_id(1)
        bv = b_ref[...]
        if scale != 1.0:
            bv = bv.astype(F32) * scale
        part = _dot_tn(a_ref[...].astype(BF16), bv.astype(BF16))

        @pl.when(k == 0)
        def _():
            acc_ref[...] = part

        @pl.when(k > 0)
        def _():
            acc_ref[...] += part

        @pl.when(k == nk - 1)
        def _():
            for sh in range(N_SHARD):
                o_ref[sh] = acc_ref[sh * per:(sh + 1) * per, :].astype(BF16)

    outs = _pallas_call(
        body, name=name, grid=(n // tn, nk),
        in_specs=[pl.BlockSpec((tk, m), lambda j, k: (k, 0)), pl.BlockSpec((tk, tn), lambda j, k: (k, j))]
        + [pl.BlockSpec(memory_space=pl.ANY)] * ne + [_HBM] * ng,
        out_specs=[pl.BlockSpec((N_SHARD, per, tn), lambda j, k: (0, slot, j))] + [_HBM] * ng,
        out_shape=[jax.ShapeDtypeStruct((N_SHARD, slots * per, n), BF16)] + (_scatter_out_shapes(scatter) if ng else []),
        scratch_shapes=[pltpu.VMEM((m, tn), F32)] + (_exchange_scratch(3, ng) if ng else []),
        input_output_aliases={2: 0} if earlier else {},
        compiler_params=_params(dimension_semantics=("arbitrary", "arbitrary")),
    )(a, b, *earlier, *(scatter or []))
    return outs[0], outs[1:]


_HALF = SWA_HEAD_DIM // 2
_IN_ORDER = (list(range(0, IN_OFFS[2]))
             + [IN_OFFS[3] + SWA_HEAD_DIM * h + i for h in range(SWA_HEADS) for i in range(_HALF)]
             + [IN_OFFS[3] + SWA_HEAD_DIM * h + _HALF + i for h in range(SWA_HEADS) for i in range(_HALF)]
             + list(range(IN_OFFS[5], IN_OFFS[6]))
             + [IN_OFFS[4] + SWA_HEAD_DIM * j + i for j in range(SWA_KV_HEADS) for i in range(_HALF)]
             + [IN_OFFS[4] + SWA_HEAD_DIM * j + _HALF + i for j in range(SWA_KV_HEADS) for i in range(_HALF)]
             + list(range(IN_OFFS[2], IN_OFFS[3])))
_QB_ORDER = ([MLA_QK * h + i for h in range(MLA_HEADS) for i in range(MLA_NOPE)]
             + [MLA_QK * h + MLA_NOPE + i for h in range(MLA_HEADS) for i in range(_HALF)]
             + [MLA_QK * h + MLA_NOPE + _HALF + i for h in range(MLA_HEADS) for i in range(_HALF)])
_KVB_ORDER = ([(MLA_NOPE + MLA_V) * h + i for h in range(MLA_HEADS) for i in range(MLA_NOPE)]
              + [(MLA_NOPE + MLA_V) * h + MLA_NOPE + i for h in range(MLA_HEADS) for i in range(MLA_V)])
MIXER_ORDERS = {'w_in': _IN_ORDER, 'mla_w_q_b': _QB_ORDER, 'mla_w_kv_b': _KVB_ORDER}
_P_CQ, _P_CKV, _P_QA, _P_QB, _P_VS, _P_KA, _P_KB, _P_PE = (int(v) for v in np.cumsum(
    (0, MLA_Q_RANK, MLA_KV_RANK, SWA_WIDTH // 2, SWA_WIDTH // 2, IN_SPLITS[5], IN_SPLITS[4] // 2, IN_SPLITS[4] // 2)))


def _runs(order):
    out, start = [], 0
    for i in range(1, len(order) + 1):
        if i == len(order) or order[i] != order[i - 1] + 1:
            out.append((order[start], i - start))
            start = i
    return out


def _inverse(order):
    inv = [0] * len(order)
    for new, old in enumerate(order):
        inv[old] = new
    return inv


def _take_cols(a, order):
    return jnp.concatenate([a[..., st:st + w] for st, w in _runs(order)], axis=-1)


def _segment_matrix(n, seg):
    return (lax.broadcasted_iota(jnp.int32, (n, n), 0) // seg == lax.broadcasted_iota(jnp.int32, (n, n), 1) // seg).astype(BF16)


def _cmm2(t, b):
    hi = t.astype(BF16)
    lo = (t - hi.astype(F32)).astype(BF16)
    return jnp.dot(hi, b, preferred_element_type=F32) + jnp.dot(lo, b, preferred_element_type=F32)


@jax.custom_vjp
def _cmm(t, b, bt):
    return jnp.dot(t.astype(BF16), b, preferred_element_type=F32)


def _cmm_fwd(t, b, bt):
    return _cmm(t, b, bt), (b, bt)


def _cmm_bwd(res, dy):
    b, bt = res
    return _cmm(dy, bt, b), None, None


_cmm.defvjp(_cmm_fwd, _cmm_bwd)


def _segsum(t, b):
    return _cmm(t, b, b)


def _rowsum(t, exact=False):
    n = t.shape[-1]
    if exact:
        return _cmm2(t, jnp.ones((n, 128), BF16))
    return _cmm(t, jnp.ones((n, 128), BF16), jnp.ones((128, n), BF16))


def _by_head(vals, width):
    lane = lax.broadcasted_iota(jnp.int32, (vals[0].shape[0], len(vals) * width), 1)
    out = vals[-1]
    for hd in range(len(vals) - 2, -1, -1):
        out = jnp.where(lane < (hd + 1) * width, vals[hd], out)
    return out


def _rope2(a, b, cos, sin):
    return a * cos - b * sin, b * cos + a * sin


def _pre_math(x, gm, gqa, gkva, gq, gk, gsq, gsk, taps, win, wqb, wkvb, cos, sin):
    h = _rms(x, gm)
    proj = _mm(h, win)
    if taps is not None:
        proj = proj + taps[0]
    cqn = _rms(proj[:, _P_CQ:_P_CKV], gqa)
    qa_all = _mm(cqn, wqb)
    ckvn = _rms(proj[:, _P_CKV:_P_QA], gkva)
    kv_all = _mm(ckvn, wkvb)
    if taps is not None:
        qa_all = qa_all + taps[1]
        kv_all = kv_all + taps[2]
    nh, hw = MLA_HEADS, MLA_HEADS * _HALF
    seg_mla = _segment_matrix(hw, _HALF)
    tile = lambda g, n: jnp.concatenate([g] * n, axis=-1)
    c4, s4 = cos[:, :hw], sin[:, :hw]

    def mla_heads(nope, r1, r2, gain):
        rr = r1 * r1 + r2 * r2
        lane_head = lax.broadcasted_iota(jnp.int32, (hw, nh * MLA_NOPE), 0) // _HALF
        spread = (lane_head == lax.broadcasted_iota(jnp.int32, (hw, nh * MLA_NOPE), 1) // MLA_NOPE).astype(BF16)
        rope_on_nope = _cmm(rr, spread, spread.T)
        ss_nope = [_rowsum(jnp.square(nope[:, hd * MLA_NOPE:(hd + 1) * MLA_NOPE])) for hd in range(nh)]
        rinv = [lax.rsqrt((ss_nope[hd] + rope_on_nope[:, hd * MLA_NOPE:(hd + 1) * MLA_NOPE]) * (1.0 / MLA_QK) + EPS)
                for hd in range(nh)]
        rl = lax.rsqrt((_segsum(rr, seg_mla) + _by_head(ss_nope, _HALF)) * (1.0 / MLA_QK) + EPS)
        o1, o2 = _rope2(r1 * rl * tile(gain[:, MLA_NOPE:MLA_NOPE + _HALF], nh), r2 * rl * tile(gain[:, MLA_NOPE + _HALF:], nh), c4, s4)
        return [jnp.concatenate([nope[:, hd * MLA_NOPE:(hd + 1) * MLA_NOPE] * rinv[hd] * gain[:, :MLA_NOPE],
                                 o1[:, hd * _HALF:(hd + 1) * _HALF], o2[:, hd * _HALF:(hd + 1) * _HALF]], axis=-1)
                for hd in range(nh)]

    q_a = mla_heads(qa_all[:, :nh * MLA_NOPE], qa_all[:, nh * MLA_NOPE:nh * MLA_NOPE + hw], qa_all[:, nh * MLA_NOPE + hw:], gq)
    pe1, pe2 = proj[:, _P_PE:_P_PE + _HALF], proj[:, _P_PE + _HALF:_P_PE + 2 * _HALF]
    k_a = mla_heads(kv_all[:, :nh * MLA_NOPE], tile(pe1, nh), tile(pe2, nh), gk)
    v_a = [kv_all[:, nh * MLA_NOPE + hd * MLA_V:nh * MLA_NOPE + (hd + 1) * MLA_V] for hd in range(nh)]

    def swa_heads(a, b, gain, n):
        w = n * _HALF
        r = lax.rsqrt(_segsum(a * a + b * b, _segment_matrix(w, _HALF)) * (1.0 / SWA_HEAD_DIM) + EPS)
        o1, o2 = _rope2(a * r * tile(gain[:, :_HALF], n), b * r * tile(gain[:, _HALF:], n), cos[:, :w], sin[:, :w])
        return [jnp.concatenate([o1[:, hd * _HALF:(hd + 1) * _HALF], o2[:, hd * _HALF:(hd + 1) * _HALF]], axis=-1) for hd in range(n)]

    q_b = swa_heads(proj[:, _P_QA:_P_QB], proj[:, _P_QB:_P_VS], gsq, SWA_HEADS)
    k_b = swa_heads(proj[:, _P_KA:_P_KB], proj[:, _P_KB:_P_PE], gsk, SWA_KV_HEADS)
    v_b = [proj[:, _P_VS + j * SWA_HEAD_DIM:_P_VS + (j + 1) * SWA_HEAD_DIM] for j in range(SWA_KV_HEADS)]
    return (q_a, k_a, v_a, q_b, k_b, v_b), (h, cqn, ckvn)


_PRE_GAIN_WIDTHS = (D_MODEL, MLA_Q_RANK, MLA_KV_RANK, MLA_QK, MLA_QK, SWA_HEAD_DIM, SWA_HEAD_DIM)
_PRE_HEADS = ((MLA_HEADS, MLA_QK), (MLA_HEADS, MLA_QK), (MLA_HEADS, MLA_V),
              (SWA_HEADS, SWA_HEAD_DIM), (SWA_KV_HEADS, SWA_HEAD_DIM), (SWA_KV_HEADS, SWA_HEAD_DIM))


def _pre_fwd(x, gains, win, wqb, wkvb, cos, sin, name):
    s = x.shape[0]
    tm = _row_tile(s, 512)

    def body(x_ref, *refs):
        g_refs, (win_ref, wqb_ref, wkvb_ref, cos_ref, sin_ref), out_refs = refs[:7], refs[7:12], refs[12:]
        outs, _ = _pre_math(x_ref[...], *[g[...] for g in g_refs], None, win_ref[...], wqb_ref[...], wkvb_ref[...],
                            cos_ref[...], sin_ref[...])
        for idx, (ref, heads) in enumerate(zip(out_refs, outs)):
            for hd, val in enumerate(heads):
                ref[hd] = (val * MLA_QSCALE if idx == 0 else val).astype(BF16)

    heads_spec = lambda nh, w: pl.BlockSpec((nh, tm, w), lambda i: (0, i, 0))
    return _pallas_call(
        body, name=name, grid=(s // tm,),
        in_specs=[pl.BlockSpec((tm, D_MODEL), lambda i: (i, 0))] + [_full((1, w)) for w in _PRE_GAIN_WIDTHS]
        + [_resident(win.shape), _resident(wqb.shape), _resident(wkvb.shape),
           pl.BlockSpec((tm, SWA_HEADS * _HALF), lambda i: (i, 0)), pl.BlockSpec((tm, SWA_HEADS * _HALF), lambda i: (i, 0))],
        out_specs=[heads_spec(nh, w) for nh, w in _PRE_HEADS],
        out_shape=[jax.ShapeDtypeStruct((nh, s, w), BF16) for nh, w in _PRE_HEADS],
        compiler_params=_params(dimension_semantics=("arbitrary",)),
    )(x, *gains, win, wqb, wkvb, cos, sin)


def _pre_bwd(x, dx_res, cts, gains, win, wqb, wkvb, cos, sin, name):
    s = x.shape[0]
    tm = _row_tile(s, 256)
    tap_widths = (IN_COLS, MLA_HEADS * MLA_QK, MLA_HEADS * (MLA_NOPE + MLA_V))

    def body(x_ref, dxr_ref, *refs):
        ct_refs, g_refs = refs[:6], refs[6:13]
        win_ref, wqb_ref, wkvb_ref, cos_ref, sin_ref = refs[13:18]
        dx_ref, dg_refs, dw_refs, acc_refs = refs[18], refs[19:26], refs[26:29], refs[29:32]
        i = pl.program_id(0)
        win_v, wqb_v, wkvb_v, cos_v, sin_v = win_ref[...], wqb_ref[...], wkvb_ref[...], cos_ref[...], sin_ref[...]

        def f(xv, gm, gqa, gkva, gq, gk, gsq, gsk, t0, t1, t2):
            return _pre_math(xv, gm, gqa, gkva, gq, gk, gsq, gsk, (t0, t1, t2), win_v, wqb_v, wkvb_v, cos_v, sin_v)

        taps = [jnp.zeros((tm, w), F32) for w in tap_widths]
        _, vjp, acts = jax.vjp(f, x_ref[...], *[g[...] for g in g_refs], *taps, has_aux=True)
        ct = tuple([ref[hd] for hd in range(nh)] for ref, (nh, _) in zip(ct_refs, _PRE_HEADS))
        grads = vjp(ct)
        dx_ref[...] = grads[0] + dxr_ref[...]
        dws = [_dot_tn(a.astype(BF16), t.astype(BF16)) for a, t in zip(acts, grads[8:11])]

        @pl.when(i == 0)
        def _():
            for ref, val in zip(dg_refs, grads[1:8]):
                ref[...] = val
            for ref, val in zip(acc_refs, dws):
                ref[...] = val

        @pl.when(i > 0)
        def _():
            for ref, val in zip(dg_refs, grads[1:8]):
                ref[...] += val
            for ref, val in zip(acc_refs, dws):
                ref[...] += val

        @pl.when(i == s // tm - 1)
        def _():
            for ref, acc, order in zip(dw_refs, acc_refs, (_IN_ORDER, _QB_ORDER, _KVB_ORDER)):
                _store_col_shards(ref, _take_cols(acc[...], _inverse(order)), 0, N_SHARD, acc.shape[1] // N_SHARD)

    heads_spec = lambda nh, w: pl.BlockSpec((nh, tm, w), lambda i: (0, i, 0))
    row = pl.BlockSpec((tm, D_MODEL), lambda i: (i, 0))
    half = pl.BlockSpec((tm, SWA_HEADS * _HALF), lambda i: (i, 0))
    shard_shapes = [(N_SHARD, w.shape[0], w.shape[1] // N_SHARD) for w in (win, wqb, wkvb)]
    return _pallas_call(
        body, name=name, grid=(s // tm,),
        in_specs=[row, row] + [heads_spec(nh, w) for nh, w in _PRE_HEADS] + [_full((1, w)) for w in _PRE_GAIN_WIDTHS]
        + [_resident(win.shape), _resident(wqb.shape), _resident(wkvb.shape), half, half],
        out_specs=[row] + [_full((1, w)) for w in _PRE_GAIN_WIDTHS] + [_full(shp) for shp in shard_shapes],
        out_shape=[jax.ShapeDtypeStruct((s, D_MODEL), F32)] + [jax.ShapeDtypeStruct((1, w), F32) for w in _PRE_GAIN_WIDTHS]
        + [jax.ShapeDtypeStruct(shp, BF16) for shp in shard_shapes],
        scratch_shapes=[pltpu.VMEM(w.shape, F32) for w in (win, wqb, wkvb)],
        compiler_params=_params(dimension_semantics=("arbitrary",)),
    )(x, dx_res, *cts, *gains, win, wqb, wkvb, cos, sin)


def _post_math(oa, ob, ga, gb, wo):
    mixed = jnp.concatenate([_rms(jnp.concatenate(oa, axis=-1), ga), _rms(jnp.concatenate(ob, axis=-1), gb)], axis=-1)
    return _mm(mixed, wo), mixed


def _post_fwd(x, oa, ob, ga, gb, wo, name):
    s = x.shape[0]
    tm = _row_tile(s, 512)

    def body(x_ref, oa_ref, ob_ref, ga_ref, gb_ref, wo_ref, y_ref):
        y, _ = _post_math([oa_ref[hd] for hd in range(MLA_HEADS)], [ob_ref[hd] for hd in range(SWA_HEADS)],
                          ga_ref[...], gb_ref[...], wo_ref[...])
        y_ref[...] = x_ref[...] + y

    row = pl.BlockSpec((tm, D_MODEL), lambda i: (i, 0))
    return _pallas_call(
        body, name=name, grid=(s // tm,),
        in_specs=[row, pl.BlockSpec((MLA_HEADS, tm, MLA_V), lambda i: (0, i, 0)),
                  pl.BlockSpec((SWA_HEADS, tm, SWA_HEAD_DIM), lambda i: (0, i, 0)),
                  _full((1, MLA_WIDTH)), _full((1, SWA_WIDTH)), _resident(wo.shape)],
        out_specs=row, out_shape=jax.ShapeDtypeStruct((s, D_MODEL), F32),
        compiler_params=_params(dimension_semantics=("arbitrary",)),
    )(x, oa, ob, ga, gb, wo)


def _post_bwd(dy, oa, ob, ga, gb, wo, name):
    s = dy.shape[0]
    tm = _row_tile(s, 512)
    t = _attn_tile(s)

    def body(dy_ref, oa_ref, ob_ref, ga_ref, gb_ref, wo_ref, doa_ref, dob_ref, dga_ref, dgb_ref, dwo_ref, delta_ref, acc_ref):
        i = pl.program_id(0)
        wo_v = wo_ref[...]
        dyv = dy_ref[...]

        def f(oa_l, ob_l, ga_v, gb_v):
            return _post_math(oa_l, ob_l, ga_v, gb_v, wo_v)

        _, vjp, mixed = jax.vjp(f, [oa_ref[hd] for hd in range(MLA_HEADS)], [ob_ref[hd] for hd in range(SWA_HEADS)],
                                ga_ref[...], gb_ref[...], has_aux=True)
        doa, dob, dga, dgb = vjp(dyv)
        for hd in range(MLA_HEADS):
            doa_ref[hd] = doa[hd]
            rows = _as_rows(_rowsum(doa[hd] * oa_ref[hd], exact=True))
            for j in range(tm // t):
                delta_ref[hd, j] = rows[:, j * t:(j + 1) * t]
        for hd in range(SWA_HEADS):
            dob_ref[hd] = dob[hd]
        dwo = _dot_tn(mixed.astype(BF16), dyv.astype(BF16))

        @pl.when(i == 0)
        def _():
            dga_ref[...] = dga
            dgb_ref[...] = dgb
            acc_ref[...] = dwo

        @pl.when(i > 0)
        def _():
            dga_ref[...] += dga
            dgb_ref[...] += dgb
            acc_ref[...] += dwo

        @pl.when(i == s // tm - 1)
        def _():
            dwo_ref[...] = acc_ref[...].astype(BF16)

    row = pl.BlockSpec((tm, D_MODEL), lambda i: (i, 0))
    oa_spec = pl.BlockSpec((MLA_HEADS, tm, MLA_V), lambda i: (0, i, 0))
    ob_spec = pl.BlockSpec((SWA_HEADS, tm, SWA_HEAD_DIM), lambda i: (0, i, 0))
    return _pallas_call(
        body, name=name, grid=(s // tm,),
        in_specs=[row, oa_spec, ob_spec, _full((1, MLA_WIDTH)), _full((1, SWA_WIDTH)), _resident(wo.shape)],
        out_specs=[oa_spec, ob_spec, _full((1, MLA_WIDTH)), _full((1, SWA_WIDTH)), _full(wo.shape),
                   pl.BlockSpec((MLA_HEADS, tm // t, 8, t), lambda i: (0, i, 0, 0))],
        out_shape=[jax.ShapeDtypeStruct((MLA_HEADS, s, MLA_V), F32), jax.ShapeDtypeStruct((SWA_HEADS, s, SWA_HEAD_DIM), F32),
                   jax.ShapeDtypeStruct((1, MLA_WIDTH), F32), jax.ShapeDtypeStruct((1, SWA_WIDTH), F32),
                   jax.ShapeDtypeStruct(wo.shape, BF16), jax.ShapeDtypeStruct((MLA_HEADS, s // t, 8, t), F32)],
        scratch_shapes=[pltpu.VMEM(wo.shape, F32)],
        compiler_params=_params(dimension_semantics=("arbitrary",)),
    )(dy, oa, ob, ga, gb, wo)


def _attn_tile(s):
    return 512 if s >= 2048 else 128


def _as_rows(cols):
    return cols.T[0:8, :]


def _causal_mask(t):
    return lax.broadcasted_iota(jnp.int32, (t, t), 1) <= lax.broadcasted_iota(jnp.int32, (t, t), 0)


def _pipelined_blocks(first, count, last_block, issue, consume, carry, prefetch_after):
    def clamped(j, slot):
        issue(jnp.minimum(j, last_block), slot)

    def pair(jj, c):
        a = first + 2 * jj
        clamped(a + 1, 1)
        c = consume(a, 0, c)
        clamped(a + 2, 0)
        return consume(a + 1, 1, c)

    clamped(first, 0)
    npairs = count // 2
    carry = lax.fori_loop(0, npairs, pair, carry)

    def odd(c):
        c = consume(first + 2 * npairs, 0, c)
        if prefetch_after:
            clamped(first + count, 0)
        return c

    return lax.cond(count - 2 * npairs == 1, odd, lambda c: c, carry)


def _run_stages_at(stages, steps):
    for stage, step in zip(stages, steps):
        here = pl.program_id(0) == step[0]
        for axis in range(1, len(step)):
            here = here & (pl.program_id(axis) == step[axis])
        pl.when(here)(stage)


def _mla_fwd(q, k, v, name, gather=None):
    nh, s, _ = q.shape
    t = _attn_tile(s)
    nq = s // t
    ng = len(gather) if gather else 0

    def body(q_ref, k_ref, v_ref, *rest):
        g_ins, (o_ref, lse_ref), g_outs = rest[:ng], rest[ng:ng + 2], rest[ng + 2:2 * ng + 2]
        (s0_ref, s1_ref), sems = rest[2 * ng + 2:2 * ng + 4], rest[2 * ng + 4:]
        if ng:
            _run_stages_at(_gather_stages(g_ins, g_outs, *sems), [(0, 0), (nh // 2, 0), (nh - 1, 0), (nh - 1, nq - 1)])
        qi = pl.program_id(1)
        qv = q_ref[...]
        s_refs = (s0_ref, s1_ref)

        def rows(j):
            return pl.ds(pl.multiple_of(j * t, t), t)

        def issue(j, slot):
            s_refs[slot][...] = _dot_nt(qv, k_ref[rows(j), :])

        def consume(j, slot, carry, masked=False):
            m, l, acc = carry
            sc = s_refs[slot][...]
            if masked:
                sc = jnp.where(_causal_mask(t), sc, NEG)
            m_new = jnp.maximum(m, jnp.max(sc, axis=-1, keepdims=True))
            alpha = jnp.exp2(m - m_new)
            p = jnp.exp2(sc - m_new)
            l = alpha * l + jnp.sum(p, axis=-1, keepdims=True)
            acc = alpha * acc + jnp.dot(p.astype(BF16), v_ref[rows(j), :], preferred_element_type=F32)
            return m_new, l, acc

        init = (jnp.full((t, 1), NEG, F32), jnp.zeros((t, 1), F32), jnp.zeros((t, MLA_V), F32))
        carry = _pipelined_blocks(0, qi, nq - 1, issue, consume, init, True)
        m, l, acc = consume(qi, 0, carry, masked=True)
        o_ref[...] = acc / l
        lse_ref[...] = _as_rows(jnp.broadcast_to(m + jnp.log2(l), (t, 128)))

    outs = _pallas_call(
        body, name=name, grid=(nh, nq),
        in_specs=[pl.BlockSpec((None, t, MLA_QK), lambda h, i: (h, i, 0)), pl.BlockSpec((None, s, MLA_QK), lambda h, i: (h, 0, 0)),
                  pl.BlockSpec((None, s, MLA_V), lambda h, i: (h, 0, 0))] + [_HBM] * ng,
        out_specs=[pl.BlockSpec((None, t, MLA_V), lambda h, i: (h, i, 0)), pl.BlockSpec((None, None, 8, t), lambda h, i: (h, i, 0, 0))]
        + [_HBM] * ng,
        out_shape=[jax.ShapeDtypeStruct((nh, s, MLA_V), F32), jax.ShapeDtypeStruct((nh, nq, 8, t), F32)]
        + (_gather_out_shapes(gather) if ng else []),
        scratch_shapes=[pltpu.VMEM((t, t), F32)] * 2 + (_exchange_scratch(8, ng) if ng else []),
        compiler_params=_params(dimension_semantics=("arbitrary", "arbitrary")),
    )(q, k, v, *(gather or []))
    return outs[0], outs[1], outs[2:]


def _mla_bwd(q, k, v, do, lse_row, delta_row, name, scatter=None):
    nh, s, _ = q.shape
    t = _attn_tile(s)
    nq = s // t
    ng = len(scatter) if scatter else 0

    def body(q_ref, k_ref, v_ref, do_ref, lse_ref, delta_ref, *rest):
        c_ins, (dq_ref, dk_ref, dv_ref), c_outs = rest[:ng], rest[ng:ng + 3], rest[ng + 3:2 * ng + 3]
        (s0_ref, s1_ref, dp0_ref, dp1_ref), sems = rest[2 * ng + 3:2 * ng + 7], rest[2 * ng + 7:]
        if ng:
            _run_stages_at(_scatter_stages(c_ins, c_outs, *sems), [(0, 0), (nh - 1, nq - 1)])
        kj = pl.program_id(1)
        kv_, vv = k_ref[...], v_ref[...]
        s_refs, dp_refs = (s0_ref, s1_ref), (dp0_ref, dp1_ref)

        @pl.when(kj == 0)
        def _():
            dq_ref[...] = jnp.zeros_like(dq_ref)

        def rows(i):
            return pl.ds(pl.multiple_of(i * t, t), t)

        def issue(i, slot):
            s_refs[slot][...] = _dot_nt(kv_, q_ref[rows(i), :])
            dp_refs[slot][...] = _dot_nt(vv, do_ref[rows(i), :].astype(BF16))

        def consume(i, slot, carry, masked=False):
            dk, dv = carry
            p = jnp.exp2(s_refs[slot][...] - lse_ref[i][0:1, :])
            if masked:
                p = jnp.where(lax.broadcasted_iota(jnp.int32, (t, t), 0) <= lax.broadcasted_iota(jnp.int32, (t, t), 1), p, 0.0)
            dv = dv + jnp.dot(p.astype(BF16), do_ref[rows(i), :].astype(BF16), preferred_element_type=F32)
            ds = (p * (dp_refs[slot][...] - delta_ref[i][0:1, :])).astype(BF16)
            dk = dk + jnp.dot(ds, q_ref[rows(i), :], preferred_element_type=F32)
            dq_ref[rows(i), :] += _dot_tn(ds, kv_) * MLA_SCALE
            return dk, dv

        issue(kj, 0)
        carry = consume(kj, 0, (jnp.zeros((t, MLA_QK), F32), jnp.zeros((t, MLA_V), F32)), masked=True)
        dk, dv = _pipelined_blocks(kj + 1, nq - 1 - kj, nq - 1, issue, consume, carry, False)
        dk_ref[...] = dk * LN2
        dv_ref[...] = dv

    tile = lambda w: pl.BlockSpec((None, t, w), lambda h, j: (h, j, 0))
    whole = lambda w: pl.BlockSpec((None, s, w), lambda h, j: (h, 0, 0))
    rows_spec = pl.BlockSpec((None, nq, 8, t), lambda h, j: (h, 0, 0, 0))
    outs = _pallas_call(
        body, name=name, grid=(nh, nq),
        in_specs=[whole(MLA_QK), tile(MLA_QK), tile(MLA_V), whole(MLA_V), rows_spec, rows_spec] + [_HBM] * ng,
        out_specs=[whole(MLA_QK), tile(MLA_QK), tile(MLA_V)] + [_HBM] * ng,
        out_shape=[jax.ShapeDtypeStruct((nh, s, MLA_QK), F32), jax.ShapeDtypeStruct((nh, s, MLA_QK), F32),
                   jax.ShapeDtypeStruct((nh, s, MLA_V), F32)] + (_scatter_out_shapes(scatter) if ng else []),
        scratch_shapes=[pltpu.VMEM((t, t), F32)] * 4 + (_exchange_scratch(3, ng) if ng else []),
        compiler_params=_params(dimension_semantics=("arbitrary", "arbitrary")),
    )(q, k, v, do, lse_row, delta_row, *(scatter or []))
    return outs[0], outs[1], outs[2], outs[3:]


def _swa_tile(s):
    return min(s, 8 * BLOCK)


def _swa_specs(tq):
    nb = tq // BLOCK
    grp = lambda w: pl.BlockSpec((SWA_GROUP, tq, w), lambda j, i: (j, i, 0))
    main = pl.BlockSpec((None, tq, SWA_HEAD_DIM), lambda j, i: (j, i, 0))
    tail = pl.BlockSpec((None, BLOCK, SWA_HEAD_DIM), lambda j, i: (j, nb * (i + 1), 0))
    sink = pl.BlockSpec((None, SWA_GROUP, 128), lambda j, i: (j, 0, 0))
    return grp, main, tail, sink


def _swa_band_mask(first):
    shape = (SWA_GROUP * BLOCK, 2 * BLOCK)
    q_rel = (lax.broadcasted_iota(jnp.int32, shape, 0) & (BLOCK - 1)) + BLOCK
    k_rel = lax.broadcasted_iota(jnp.int32, shape, 1)
    dist = q_rel - k_rel
    return (dist >= 0) & (dist < BLOCK) & ((k_rel >= BLOCK) | jnp.logical_not(first))


def _swa_sink_column(sink_ref):
    sk = sink_ref[...]
    return jnp.concatenate([jnp.broadcast_to(sk[g:g + 1, 0:1], (BLOCK, 1)) for g in range(SWA_GROUP)], axis=0)


def _swa_fwd(q, kpad, vpad, sinks, name):
    _, s, _ = q.shape
    tq = _swa_tile(s)
    grp, main, tail, sink = _swa_specs(tq)
    d = SWA_HEAD_DIM

    def body(q_ref, km_ref, kt_ref, vm_ref, vt_ref, sink_ref, o_ref, lse_ref):
        i = pl.program_id(1)
        kall = jnp.concatenate([km_ref[...], kt_ref[...]], axis=0)
        vall = jnp.concatenate([vm_ref[...], vt_ref[...]], axis=0)
        sink_col = _swa_sink_column(sink_ref)
        for b in range(tq // BLOCK):
            lo = b * BLOCK
            valid = _swa_band_mask(i == 0 if b == 0 else False)
            q4 = q_ref[:, lo:lo + BLOCK, :].reshape(SWA_GROUP * BLOCK, d)
            sc = jnp.where(valid, _dot_nt(q4, kall[lo:lo + 2 * BLOCK]) * SWA_SCALE, NEG)
            m = jnp.maximum(jnp.max(sc, axis=-1, keepdims=True), sink_col)
            e = jnp.exp(sc - m)
            den = jnp.sum(e, axis=-1, keepdims=True) + jnp.exp(sink_col - m)
            out = jnp.dot((e * (1.0 / den)).astype(BF16), vall[lo:lo + 2 * BLOCK], preferred_element_type=F32)
            o_ref[:, lo:lo + BLOCK, :] = out.reshape(SWA_GROUP, BLOCK, d)
            lse_ref[:, lo:lo + BLOCK, :] = (m + jnp.log(den)).reshape(SWA_GROUP, BLOCK, 1)

    return _pallas_call(
        body, name=name, grid=(SWA_KV_HEADS, s // tq),
        in_specs=[grp(d), main, tail, main, tail, sink], out_specs=[grp(d), grp(1)],
        out_shape=[jax.ShapeDtypeStruct((SWA_HEADS, s, d), F32), jax.ShapeDtypeStruct((SWA_HEADS, s, 1), F32)],
        compiler_params=_params(dimension_semantics=("arbitrary", "arbitrary")),
    )(q, kpad, kpad, vpad, vpad, sinks)


def _swa_bwd(q, kpad, vpad, sinks, o, lse, do, name):
    _, s, _ = q.shape
    tq = _swa_tile(s)
    grp, main, tail, sink = _swa_specs(tq)
    d = SWA_HEAD_DIM

    def body(q_ref, km_ref, kt_ref, vm_ref, vt_ref, sink_ref, o_ref, lse_ref, do_ref, dq_ref, dk_ref, dv_ref, dsink_ref):
        i = pl.program_id(1)
        kall = jnp.concatenate([km_ref[...], kt_ref[...]], axis=0)
        vall = jnp.concatenate([vm_ref[...], vt_ref[...]], axis=0)
        sink_col = _swa_sink_column(sink_ref)

        @pl.when(i == 0)
        def _():
            dk_ref[...] = jnp.zeros_like(dk_ref)
            dv_ref[...] = jnp.zeros_like(dv_ref)
            dsink_ref[...] = jnp.zeros_like(dsink_ref)

        dsink = jnp.zeros((SWA_GROUP * BLOCK, 1), F32)
        for b in range(tq // BLOCK):
            lo = b * BLOCK
            valid = _swa_band_mask(i == 0 if b == 0 else False)
            rows4 = SWA_GROUP * BLOCK
            q4 = q_ref[:, lo:lo + BLOCK, :].reshape(rows4, d)
            do4 = do_ref[:, lo:lo + BLOCK, :].reshape(rows4, d)
            lse4 = lse_ref[:, lo:lo + BLOCK, :].reshape(rows4, 1)
            delta = jnp.sum(do4 * o_ref[:, lo:lo + BLOCK, :].reshape(rows4, d), axis=-1, keepdims=True)
            kb, vb = kall[lo:lo + 2 * BLOCK], vall[lo:lo + 2 * BLOCK]
            do4b = do4.astype(BF16)
            p = jnp.where(valid, jnp.exp(_dot_nt(q4, kb) * SWA_SCALE - lse4), 0.0)
            ds = (p * (_dot_nt(do4b, vb) - delta) * SWA_SCALE).astype(BF16)
            dq_ref[:, lo:lo + BLOCK, :] = jnp.dot(ds, kb, preferred_element_type=F32).reshape(SWA_GROUP, BLOCK, d)
            band = pl.ds(pl.multiple_of(i * tq, BLOCK) + lo, 2 * BLOCK)
            dk_ref[band, :] += _dot_tn(ds, q4)
            dv_ref[band, :] += _dot_tn(p.astype(BF16), do4b)
            dsink = dsink - jnp.exp(sink_col - lse4) * delta
        per_head = [jnp.broadcast_to(jnp.sum(dsink[g * BLOCK:(g + 1) * BLOCK], axis=0, keepdims=True), (1, 128))
                    for g in range(SWA_GROUP)]
        dsink_ref[...] += jnp.concatenate(per_head + [jnp.zeros((8 - SWA_GROUP, 128), F32)], axis=0)

    acc = pl.BlockSpec((None, s + BLOCK, d), lambda j, i: (j, 0, 0))
    return _pallas_call(
        body, name=name, grid=(SWA_KV_HEADS, s // tq),
        in_specs=[grp(d), main, tail, main, tail, sink, grp(d), grp(1), grp(d)],
        out_specs=[grp(d), acc, acc, pl.BlockSpec((None, 8, 128), lambda j, i: (j, 0, 0))],
        out_shape=[jax.ShapeDtypeStruct((SWA_HEADS, s, d), F32),
                   jax.ShapeDtypeStruct((SWA_KV_HEADS, s + BLOCK, d), F32),
                   jax.ShapeDtypeStruct((SWA_KV_HEADS, s + BLOCK, d), F32),
                   jax.ShapeDtypeStruct((SWA_KV_HEADS, 8, 128), F32)],
        compiler_params=_params(dimension_semantics=("arbitrary", "arbitrary")),
    )(q, kpad, kpad, vpad, vpad, sinks, o, lse, do)


def _loss_head(y, target, name):
    s = y.shape[0]
    tm = _row_tile(s, 512)

    def body(y_ref, t_ref, dy_ref, loss_ref):
        i = pl.program_id(0)
        err = y_ref[...] - t_ref[...]
        dy_ref[...] = err * (1.0 / D_MODEL)
        part = jnp.broadcast_to(0.5 * jnp.sum(jnp.mean(err * err, axis=-1, keepdims=True), axis=0, keepdims=True), (1, 128))

        @pl.when(i == 0)
        def _():
            loss_ref[...] = part

        @pl.when(i > 0)
        def _():
            loss_ref[...] += part

    row = pl.BlockSpec((tm, D_MODEL), lambda i: (i, 0))
    return _pallas_call(
        body, name=name, grid=(s // tm,), in_specs=[row, row], out_specs=[row, _full((1, 128))],
        out_shape=[jax.ShapeDtypeStruct((s, D_MODEL), F32), jax.ShapeDtypeStruct((1, 128), F32)],
        compiler_params=_params(dimension_semantics=("arbitrary",)),
    )(y, target)


def _adamw(w, g, m, v, name):
    rows, cols = w.shape
    tr = rows
    for cand in (512, 256, 128, 64, 32, 16, 8):
        if rows % cand == 0 and rows > cand:
            tr = cand
            break

    def body(w_ref, g_ref, m_ref, v_ref, d_ref, nm_ref, nv_ref):
        gv = g_ref[...]
        nm = ADAM_B1 * m_ref[...] + (1.0 - ADAM_B1) * gv
        nv = ADAM_B2 * v_ref[...] + (1.0 - ADAM_B2) * (gv * gv)
        m_hat = nm / (1.0 - ADAM_B1 ** ADAM_STEP)
        v_hat = nv / (1.0 - ADAM_B2 ** ADAM_STEP)
        d_ref[...] = -ADAM_LR * (m_hat / (jnp.sqrt(v_hat) + ADAM_EPS) + ADAM_WD * w_ref[...])
        nm_ref[...] = nm
        nv_ref[...] = nv

    blk = pl.BlockSpec((tr, cols), lambda i: (i, 0))
    return _pallas_call(
        body, name=name, grid=(rows // tr,), in_specs=[blk] * 4, out_specs=[blk] * 3,
        out_shape=[jax.ShapeDtypeStruct((rows, cols), F32)] * 3,
        compiler_params=_params(dimension_semantics=("arbitrary",)),
    )(w, g, m, v)


def _position():
    return lax.axis_index("x"), lax.axis_index("y"), lax.axis_index("c")


def _remote(src, dst, send_sems, recv_sems, k, to):
    return pltpu.make_async_remote_copy(src_ref=src, dst_ref=dst, send_sem=send_sems.at[k], recv_sem=recv_sems.at[k],
                                        device_id=to, device_id_type=MESH)


_HBM = pl.BlockSpec(memory_space=pltpu.HBM)


def _gather_stages(ins, outs, send_sems, recv_sems):
    na = len(ins)
    x, y, c = _position()
    me, sibling = (x, y, c), (x, y, 1 - c)
    xn, yn, dg = (1 - x, y), (x, 1 - y), (1 - x, 1 - y)

    def slot(a, chip, pc, half=None):
        ref = outs[a].at[4 * chip[0] + 2 * chip[1] + pc]
        if half is None:
            return ref
        rows = ref.shape[0] // 2
        return ref.at[pl.ds(half * rows, rows)]

    def cp(a, k, chip, pc, half, to, src=None):
        dst = slot(a, chip, pc, half)
        return _remote(dst if src is None else src, dst, send_sems, recv_sems, 8 * a + k, to)

    first_hop = [(0, xn), (1, yn)]
    second_hop = [(0, xn, 2, 0, yn), (1, yn, 3, 1, xn)]

    def sends():
        out = []
        for a in range(na):
            out += [cp(a, k, (x, y), c, None, (*to, c), src=ins[a].at[c]) for k, to in first_hop]
            out += [cp(a, fwd_k, frm, c, half, (*to, c)) for _, frm, fwd_k, half, to in second_hop]
            out += [cp(a, 4 + k, frm, c, None, sibling) for k, frm in first_hop]
            out += [cp(a, 6 + half, dg, c, half, sibling) for half in (0, 1)]
        return out

    def stage0():
        for a in range(na):
            for k, to in first_hop:
                cp(a, k, (x, y), c, None, (*to, c), src=ins[a].at[c]).start()

    def stage1():
        for k, frm, fwd_k, half, to in second_hop:
            for a in range(na):
                cp(a, k, frm, c, None, me).wait_recv()
                cp(a, fwd_k, frm, c, half, (*to, c)).start()
                cp(a, 4 + k, frm, c, None, sibling).start()

    def stage2():
        for half in (0, 1):
            for a in range(na):
                cp(a, 2 + half, dg, c, half, me).wait_recv()
                cp(a, 6 + half, dg, c, half, sibling).start()

    def stage3():
        for a in range(na):
            for k, chip, half in ((4, xn, None), (5, yn, None), (6, dg, 0), (7, dg, 1)):
                cp(a, k, chip, 1 - c, half, me).wait_recv()
        for sent in sends():
            sent.wait_send()

    return [stage0, stage1, stage2, stage3]


def _gather_out_shapes(mine):
    return [jax.ShapeDtypeStruct((N_DEV,) + m.shape[1:], m.dtype) for m in mine]


def _exchange_scratch(per_array, na):
    return [pltpu.SemaphoreType.DMA((per_array * na,)), pltpu.SemaphoreType.DMA((per_array * na,))]


def _all_gather_halves(mine, name):
    na = len(mine)

    def body(*refs):
        for stage in _gather_stages(refs[:na], refs[na:2 * na], *refs[2 * na:]):
            stage()

    return _pallas_call(body, name=name, in_specs=[_HBM] * na, out_specs=[_HBM] * na, out_shape=_gather_out_shapes(mine),
                        scratch_shapes=_exchange_scratch(8, na))(*mine)


def _sibling_exchange(parts, name, other_half):
    na = len(parts)

    def body(*refs):
        ins, outs, (send_sems, recv_sems) = refs[:na], refs[na:2 * na], refs[2 * na:]
        x, y, c = _position()
        copies = [_remote(ins[a].at[:, 1 - c] if other_half else ins[a], outs[a], send_sems, recv_sems, a, (x, y, 1 - c))
                  for a in range(na)]
        for cp in copies:
            cp.start()
        for cp in copies:
            cp.wait()

    return _pallas_call(
        body, name=name, in_specs=[_HBM] * na, out_specs=[_HBM] * na,
        out_shape=[jax.ShapeDtypeStruct(p.shape[:1] + p.shape[2:] if other_half else p.shape, p.dtype) for p in parts],
        scratch_shapes=_exchange_scratch(1, na),
    )(*parts)


def _scatter_stages(ins, outs, send_sems, recv_sems):
    na = len(ins)
    x, y, c = _position()
    chips = [(1 - x, y), (x, 1 - y), (1 - x, 1 - y)]

    def copies():
        return [_remote(ins[a].at[2 * px + py], outs[a].at[j], send_sems, recv_sems, 3 * a + j, (px, py, c))
                for a in range(na) for j, (px, py) in enumerate(chips)]

    def start():
        for cp in copies():
            cp.start()

    def wait():
        for cp in copies():
            cp.wait()

    return [start, wait]


def _scatter_out_shapes(parts):
    return [jax.ShapeDtypeStruct((3,) + p.shape[1:], p.dtype) for p in parts]


def _scatter_to_chips(parts, name):
    na = len(parts)

    def body(*refs):
        for stage in _scatter_stages(refs[:na], refs[na:2 * na], *refs[2 * na:]):
            stage()

    return _pallas_call(body, name=name, in_specs=[_HBM] * na, out_specs=[_HBM] * na, out_shape=_scatter_out_shapes(parts),
                        scratch_shapes=_exchange_scratch(3, na))(*parts)


def _assemble(g4, mine, name, side_by_side, order=None):
    _, nl, r, w = g4.shape
    tr = _divisor_tile(r, 640)

    def body(in_ref, mine_ref, out_ref):
        chip = 2 * lax.axis_index("x") + lax.axis_index("y")
        blocks = [jnp.where(chip == sh, mine_ref[...], in_ref[sh]) for sh in range(N_SHARD)]
        if side_by_side:
            full = jnp.concatenate(blocks, axis=-1)
            out_ref[...] = full if order is None else _take_cols(full, order)
        else:
            for sh in range(N_SHARD):
                out_ref[sh] = blocks[sh]

    if side_by_side:
        out_spec = pl.BlockSpec((None, tr, N_SHARD * w), lambda l, i: (l, i, 0))
        out_shape = jax.ShapeDtypeStruct((nl, r, N_SHARD * w), g4.dtype)
    else:
        out_spec = pl.BlockSpec((None, N_SHARD, tr, w), lambda l, i: (l, 0, i, 0))
        out_shape = jax.ShapeDtypeStruct((nl, N_SHARD, r, w), g4.dtype)
    return _pallas_call(
        body, name=name, grid=(nl, r // tr),
        in_specs=[pl.BlockSpec((N_SHARD, None, tr, w), lambda l, i: (0, l, i, 0)),
                  pl.BlockSpec((None, tr, w), lambda l, i: (l, i, 0))],
        out_specs=out_spec, out_shape=out_shape,
        compiler_params=_params(dimension_semantics=("arbitrary", "arbitrary")),
    )(g4, mine)


def _all_reduce_small(vec, name):
    r, l = vec.shape

    def body(v_ref, out_ref, gath_ref, send_sems, recv_sems):
        x, y, c = _position()
        me = 4 * x + 2 * y + c
        gath_ref[me] = v_ref[...]
        copies = []
        for k in range(1, N_DEV):
            to = (x ^ (k >> 2), y ^ ((k >> 1) & 1), c ^ (k & 1))
            copies.append(_remote(gath_ref.at[me], gath_ref.at[me], send_sems, recv_sems, k - 1, to))
        for cp in copies:
            cp.start()
        for k in range(1, N_DEV):
            frm = 4 * (x ^ (k >> 2)) + 2 * (y ^ ((k >> 1) & 1)) + (c ^ (k & 1))
            _remote(gath_ref.at[frm], gath_ref.at[frm], send_sems, recv_sems, k - 1, (x, y, c)).wait_recv()
        for cp in copies:
            cp.wait_send()
        total = gath_ref[0]
        for d in range(1, N_DEV):
            total = total + gath_ref[d]
        out_ref[...] = total

    vm = pl.BlockSpec(memory_space=pltpu.VMEM)
    return _pallas_call(
        body, name=name, in_specs=[vm], out_specs=vm, out_shape=jax.ShapeDtypeStruct((r, l), F32),
        scratch_shapes=[pltpu.VMEM((N_DEV, r, l), F32), pltpu.SemaphoreType.DMA((N_DEV - 1,)),
                        pltpu.SemaphoreType.DMA((N_DEV - 1,))],
    )(vec)


def _sum_blocks(blocks, out_dtype, name):
    m, w = blocks[0].shape
    tr = _divisor_tile(m, 1024)

    def body(*refs):
        total = refs[0][...].astype(F32)
        for ref in refs[1:-1]:
            total = total + ref[...].astype(F32)
        refs[-1][...] = total.astype(out_dtype)

    blk = pl.BlockSpec((tr, w), lambda i: (i, 0))
    return _pallas_call(
        body, name=name, grid=(m // tr,), in_specs=[blk] * len(blocks), out_specs=blk,
        out_shape=jax.ShapeDtypeStruct((m, w), out_dtype),
        compiler_params=_params(dimension_semantics=("arbitrary",)),
    )(*blocks)


FIRST_GROUPS = (('ffn1_w_gate', 'ffn1_w_up', 'ffn1_w_down'),)
REST_GROUPS = (('ffn2_w_gate', 'ffn2_w_up', 'ffn2_w_down'), ('w_o',), ('w_in',), ('mla_w_q_b',), ('mla_w_kv_b',))
N_FIRST = len(FIRST_GROUPS)
GRAD_SETS = {'first_gu': (('ffn1_w_gate', 'ffn1_w_up'),), 'first_d': (('ffn1_w_down',),), 'rest': REST_GROUPS}
GRAD_SET_ORDER = ('first_gu', 'first_d', 'rest')


def _shard_rows(name):
    shape, axis = BIG[name]
    return shape[0] // N_SHARD if axis == 0 else shape[0]


def _group_row_offsets(group):
    return [int(v) for v in np.cumsum([0] + [_shard_rows(n) for n in group])]


def _rope_tables(s):
    pos = jnp.arange(s, dtype=F32)
    inv = 1.0 / (ROPE_THETA ** (jnp.arange(0, MLA_ROPE, 2, dtype=F32) / MLA_ROPE))
    ang = pos[:, None] * inv[None, :]
    return jnp.tile(jnp.cos(ang), (1, SWA_HEADS)), jnp.tile(jnp.sin(ang), (1, SWA_HEADS))


_MIXER_GAINS = ('mix_norm', 'mla_q_a_norm', 'mla_kv_a_norm', 'mla_q_norm', 'mla_k_norm', 'swa_q_norm', 'swa_k_norm')


def _local_step(x, target, small, ex):
    s = x.shape[0]
    cos, sin = _rope_tables(s)
    row = lambda name, l: small[name][l][None, :]
    saved, bigs = [], []
    for l in range(DEPTH):
        big = ex.first_weights(l)
        sv = {'x0': x}
        x, sv['g1'], sv['u1'], got = _ffn_fwd(x, row('ffn1_norm', l), big['ffn1_w_gate'], big['ffn1_w_up'], big['ffn1_w_down'],
                                             f"ffn1_fwd_{l}", ex.gather_behind_ffn1(l))
        big.update(ex.rest_weights(l, got))
        bigs.append(big)
        sv['x1'] = x
        gains = [row(n, l) for n in _MIXER_GAINS]
        mixer_w = (big['w_in'], big['mla_w_q_b'], big['mla_w_kv_b'])
        q_a, k_a, v_a, q_b, k_b, v_b = _pre_fwd(x, gains, *mixer_w, cos, sin, f"pre_fwd_{l}")
        o_a, lse, got = _mla_fwd(q_a, k_a, v_a, f"mla_fwd_{l}", ex.gather_behind_mla(l))
        ex.gathered_behind_mla(l, got)
        kpad = jnp.pad(k_b, ((0, 0), (BLOCK, 0), (0, 0)))
        vpad = jnp.pad(v_b, ((0, 0), (BLOCK, 0), (0, 0)))
        sinks = jnp.broadcast_to(small['swa_sinks'][l].reshape(SWA_KV_HEADS, SWA_GROUP, 1), (SWA_KV_HEADS, SWA_GROUP, 128))
        o_b, lse_b = _swa_fwd(q_b, kpad, vpad, sinks, f"swa_fwd_{l}")
        sv.update(gains=gains, mixer_w=mixer_w, q_a=q_a, k_a=k_a, v_a=v_a, q_b=q_b, kpad=kpad, vpad=vpad, sinks=sinks,
                  o_a=o_a, lse=lse, o_b=o_b, lse_b=lse_b)
        x = _post_fwd(x, o_a, o_b, row('mla_out_norm', l), row('swa_out_norm', l), big['w_o'], f"post_fwd_{l}")
        sv['x2'] = x
        x, sv['g2'], sv['u2'], _ = _ffn_fwd(x, row('ffn2_norm', l), big['ffn2_w_gate'], big['ffn2_w_up'], big['ffn2_w_down'],
                                           f"ffn2_fwd_{l}")
        saved.append(sv)

    dx, loss = _loss_head(x, target, "loss_head")

    gs = {n: [None] * DEPTH for n in SMALL_NAMES}
    t = _attn_tile(s)
    for l in reversed(range(DEPTH)):
        sv, big = saved[l], bigs[l]

        def ffn_back(tag, xin, dy, gate, up, scatter=None):
            (dxi, dgain, nb, act, dgate, dup), got = _ffn_bwd(xin, dy, gate, up, row(tag + '_norm', l), big[tag + '_w_gate'],
                                                             big[tag + '_w_up'], big[tag + '_w_down'], f"{tag}_bwd_{l}", scatter)
            gs[tag + '_norm'][l] = dgain[0]
            names = (tag + '_w_gate', tag + '_w_up', tag + '_w_down')
            if tag == 'ffn2':
                group, _ = _matmul_tn(dgate, nb, 1.0, f"{tag}_dwg_{l}", None, 0, 3)
                group, _ = _matmul_tn(dup, nb, 1.0, f"{tag}_dwu_{l}", group, 1, 3)
                group, _ = _matmul_tn(act, dy, 0.5, f"{tag}_dwd_{l}", group, 2, 3)
                return dxi, {names: group}, got
            group, _ = _matmul_tn(dgate, nb, 1.0, f"{tag}_dwg_{l}", None, 0, 2)
            group, _ = _matmul_tn(dup, nb, 1.0, f"{tag}_dwu_{l}", group, 1, 2)
            ex.grads_ready(l, 'first_gu', {names[:2]: group})
            down, arrived = _matmul_tn(act, dy, 0.5, f"{tag}_dwd_{l}", None, 0, 1, ex.scatter_behind_dwd(l))
            ex.scattered_behind_dwd(l, arrived)
            ex.grads_ready(l, 'first_d', {names[2:]: down})
            return dxi, None, got

        dx, rest_grads, _ = ffn_back('ffn2', sv['x2'], dx, sv['g2'], sv['u2'])
        do_a, do_b, dga, dgb, dwo, delta = _post_bwd(dx, sv['o_a'], sv['o_b'], row('mla_out_norm', l), row('swa_out_norm', l),
                                              big['w_o'], f"post_bwd_{l}")
        gs['mla_out_norm'][l], gs['swa_out_norm'][l] = dga[0], dgb[0]
        rest_grads['w_o'] = dwo.reshape(N_SHARD, MIX_WIDTH // N_SHARD, D_MODEL)
        dq_a, dk_a, dv_a, got = _mla_bwd(sv['q_a'], sv['k_a'], sv['v_a'], do_a, sv['lse'], delta, f"mla_bwd_{l}",
                                         ex.scatter_behind_mla(l))
        ex.scattered_behind_mla(l, got)
        dq_b, dkpad, dvpad, dsink = _swa_bwd(sv['q_b'], sv['kpad'], sv['vpad'], sv['sinks'], sv['o_b'], sv['lse_b'], do_b,
                                             f"swa_bwd_{l}")
        gs['swa_sinks'][l] = dsink[:, :SWA_GROUP, 0].reshape(SWA_HEADS)
        cts = [dq_a, dk_a, dv_a, dq_b, dkpad[:, BLOCK:], dvpad[:, BLOCK:]]
        outs = _pre_bwd(sv['x1'], dx, cts, sv['gains'], *sv['mixer_w'], cos, sin, f"pre_bwd_{l}")
        dx = outs[0]
        for n, val in zip(_MIXER_GAINS, outs[1:8]):
            gs[n][l] = val[0]
        rest_grads['w_in'], rest_grads['mla_w_q_b'], rest_grads['mla_w_kv_b'] = outs[8:11]
        ex.grads_ready(l, 'rest', rest_grads)
        scatter = ex.scatter_behind_ffn1(l)
        dx, _, got = ffn_back('ffn1', sv['x0'], dx, sv['g1'], sv['u1'], scatter)
        ex.scattered_behind_ffn1(l, got)
    return loss, dx, gs


class _Exchange:
    def __init__(self, weights, c, chip):
        halves_of = lambda a: a.reshape(a.shape[:-2] + (2, a.shape[-2] // 2, a.shape[-1]))
        self.halves_of, self.c, self.chip = halves_of, c, chip
        self.mine = [[halves_of(jnp.concatenate([weights[n][l].astype(BF16) for n in group], axis=0))
                      for group in FIRST_GROUPS + REST_GROUPS] for l in range(DEPTH)]
        self.ahead, self.waiting, self.begun, self.received = {}, {}, {}, {}

    def _assembled(self, l, which, gathered):
        groups, base = (FIRST_GROUPS, 0) if which == 'first' else (REST_GROUPS, N_FIRST)
        big = {}
        for gi, group in enumerate(groups):
            offs = _group_row_offsets(group)
            _, rh, w = self.mine[l][base + gi].shape
            col_sharded = BIG[group[0]][1] == 1
            full = _assemble(gathered[gi].reshape(N_SHARD, 1, 2 * rh, w), self.mine[l][base + gi].reshape(1, 2 * rh, w),
                             f"assemble_{which}{gi}_{l}", col_sharded, MIXER_ORDERS.get(group[0]))
            for i, n in enumerate(group):
                rows = offs[i + 1] - offs[i]
                if len(group) > 1 and offs[i] % rows == 0:
                    big[n] = ((full, (None, rows, N_SHARD * w), (0, offs[i] // rows, 0)) if col_sharded else
                              (full, (None, N_SHARD, rows, w), (0, 0, offs[i] // rows, 0)))
                elif col_sharded:
                    big[n] = full[0, offs[i]:offs[i + 1]]
                else:
                    big[n] = full[0, :, offs[i]:offs[i + 1]].reshape(BIG[n][0])
        return big

    def first_weights(self, l):
        got = self.ahead[l][:N_FIRST] if l in self.ahead else _all_gather_halves(self.mine[l][:N_FIRST], f"gather_first_{l}")
        return self._assembled(l, 'first', got)

    def gather_behind_ffn1(self, l):
        return None if l in self.ahead else self.mine[l][N_FIRST:]

    def rest_weights(self, l, got):
        return self._assembled(l, 'rest', self.ahead[l][N_FIRST:] if l in self.ahead else got)

    def gather_behind_mla(self, l):
        return self.mine[l + 1] if l + 1 < DEPTH else None

    def gathered_behind_mla(self, l, got):
        if got:
            self.ahead[l + 1] = got

    def grads_ready(self, l, which, grads):
        groups = GRAD_SETS[which]
        parts = [self.halves_of(grads[group] if group in grads else jnp.concatenate([grads[n] for n in group], axis=1))
                 for group in groups]
        if l > 0:
            self.waiting[(l, which)] = parts
        else:
            self._begin(l, [(which, parts)])

    def _begin(self, l, sets):
        tag = sets[0][0] if len(sets) == 1 else "all"
        from_sibling = _sibling_exchange(sum((parts for _, parts in sets), []), f"swap_{tag}_{l}", True)
        at = 0
        for which, parts in sets:
            chip_sums = []
            for gi, p in enumerate(parts):
                got = from_sibling[at]
                at += 1
                kept = lax.dynamic_index_in_dim(p, self.c, axis=1, keepdims=False)
                rows = N_SHARD * p.shape[2]
                pair = _sum_blocks([kept.reshape(rows, -1), got.reshape(rows, -1)], BF16, f"sum_pair_{which}{gi}_{l}")
                chip_sums.append(pair.reshape(got.shape))
            self.begun[(l, which)] = chip_sums

    def scatter_behind_mla(self, l):
        if l + 1 >= DEPTH:
            return None
        self._begin(l + 1, [(which, self.waiting.pop((l + 1, which))) for which in GRAD_SET_ORDER])
        return sum((self.begun[(l + 1, which)] for which in GRAD_SET_ORDER), [])

    def scattered_behind_mla(self, l, got):
        at = 0
        for which in GRAD_SET_ORDER if got else ():
            self.received[(l + 1, which)] = got[at:at + len(GRAD_SETS[which])]
            at += len(GRAD_SETS[which])

    def scatter_behind_dwd(self, l):
        return self.begun[(l, 'first_gu')] if l == 0 else None

    def scattered_behind_dwd(self, l, got):
        if got:
            self.received[(l, 'first_gu')] = got

    def scatter_behind_ffn1(self, l):
        return self.begun[(l, 'rest')] if l == 0 else None

    def scattered_behind_ffn1(self, l, got):
        if got:
            self.received[(l, 'rest')] = got

    def reduced(self):
        keys = sorted(self.begun)
        for key in keys:
            if key not in self.received:
                self.received[key] = _scatter_to_chips(self.begun[key], f"scatter_{key[1]}_{key[0]}")
        halves = []
        for l, which in keys:
            for gi, (cs, got) in enumerate(zip(self.begun[(l, which)], self.received[(l, which)])):
                own = lax.dynamic_index_in_dim(cs, self.chip, axis=0, keepdims=False)
                halves.append(_sum_blocks([own, got[0], got[1], got[2]], F32, f"sum_chips_{which}{gi}_{l}"))
        others = _sibling_exchange(halves, "share_halves", False)
        per_layer, at = {}, 0
        for l, which in keys:
            for group in GRAD_SETS[which]:
                mine_h, other_h = halves[at], others[at]
                at += 1
                full = jnp.where(self.c == 0, jnp.concatenate([mine_h, other_h]), jnp.concatenate([other_h, mine_h]))
                offs = _group_row_offsets(group)
                for i, n in enumerate(group):
                    per_layer[(n, l)] = full[offs[i]:offs[i + 1]]
        return {n: jnp.stack([per_layer[(n, l)] for l in range(DEPTH)]) for n in BIG_NAMES}


def kernel(x, ffn1_norm, ffn1_w_gate, ffn1_w_up, ffn1_w_down, mix_norm, w_in, mla_q_a_norm, mla_w_q_b, mla_kv_a_norm, mla_w_kv_b, mla_q_norm, mla_k_norm, swa_q_norm, swa_k_norm, swa_sinks, mla_out_norm, swa_out_norm, w_o, ffn2_norm, ffn2_w_gate, ffn2_w_up, ffn2_w_down, loss_target, m_ffn1_norm, m_ffn1_w_gate, m_ffn1_w_up, m_ffn1_w_down, m_mix_norm, m_w_in, m_mla_q_a_norm, m_mla_w_q_b, m_mla_kv_a_norm, m_mla_w_kv_b, m_mla_q_norm, m_mla_k_norm, m_swa_q_norm, m_swa_k_norm, m_swa_sinks, m_mla_out_norm, m_swa_out_norm, m_w_o, m_ffn2_norm, m_ffn2_w_gate, m_ffn2_w_up, m_ffn2_w_down, v_ffn1_norm, v_ffn1_w_gate, v_ffn1_w_up, v_ffn1_w_down, v_mix_norm, v_w_in, v_mla_q_a_norm, v_mla_w_q_b, v_mla_kv_a_norm, v_mla_w_kv_b, v_mla_q_norm, v_mla_k_norm, v_swa_q_norm, v_swa_k_norm, v_swa_sinks, v_mla_out_norm, v_swa_out_norm, v_w_o, v_ffn2_norm, v_ffn2_w_gate, v_ffn2_w_up, v_ffn2_w_down):
    args = dict(locals())
    transposed = lambda a: jnp.swapaxes(a, 1, 2)
    as_kernels_see = lambda n, a: transposed(a) if n in TRANSPOSED else a
    weights = {n: as_kernels_see(n, args[n]) for n in WEIGHT_NAMES}
    mom_m = {n: as_kernels_see(n, args["m_" + n]) for n in WEIGHT_NAMES}
    mom_v = {n: as_kernels_see(n, args["v_" + n]) for n in WEIGHT_NAMES}
    ex = _Exchange(weights, lax.axis_index("c"), 2 * lax.axis_index("x") + lax.axis_index("y"))
    loss, dx, gs = _local_step(x[0], loss_target[0], {n: weights[n] for n in SMALL_NAMES}, ex)

    small_flat = jnp.concatenate([jnp.stack(gs[n]).reshape(-1) for n in SMALL_NAMES] + [loss[0, :1]])
    n_small = small_flat.shape[0]
    lanes = -(-n_small // (8 * 128)) * 128
    small_sum = _all_reduce_small(jnp.pad(small_flat, (0, 8 * lanes - n_small)).reshape(8, lanes), "reduce_small").reshape(-1)
    loss_out = small_sum[n_small - 1]
    grads = ex.reduced()

    packed = lambda d: jnp.pad(jnp.concatenate([d[n].reshape(-1) for n in SMALL_NAMES]), (0, 8 * lanes - n_small + 1)).reshape(8, lanes)
    small_out = _adamw(packed(weights), small_sum.reshape(8, lanes), packed(mom_m), packed(mom_v), "adamw_small")
    deltas, new_m, new_v = {}, {}, {}
    off = 0
    for n in SMALL_NAMES:
        cnt = int(np.prod(weights[n].shape))
        grads[n], deltas[n], new_m[n], new_v[n] = (a.reshape(-1)[off:off + cnt].reshape(weights[n].shape)
                                                    for a in (small_sum, *small_out))
        off += cnt
    for n in BIG_NAMES:
        shp = weights[n].shape
        two_d = (shp[0] * shp[1], shp[2])
        d, nm, nv = _adamw(weights[n].reshape(two_d), grads[n].reshape(two_d), mom_m[n].reshape(two_d),
                           mom_v[n].reshape(two_d), f"adamw_{n}")
        deltas[n], new_m[n], new_v[n] = d.reshape(shp), nm.reshape(shp), nv.reshape(shp)
    for n in TRANSPOSED:
        grads[n], deltas[n], new_m[n], new_v[n] = (transposed(a) for a in (grads[n], deltas[n], new_m[n], new_v[n]))

    return (loss_out, dx[None], *[grads[n] for n in WEIGHT_NAMES], *[deltas[n] for n in WEIGHT_NAMES],
            *[new_m[n] for n in WEIGHT_NAMES], *[new_v[n] for n in WEIGHT_NAMES])
```

```python
import functools

import numpy as np
import jax
import jax.numpy as jnp
from jax import lax
from jax.experimental import pallas as pl
from jax.experimental.pallas import tpu as pltpu

F32 = jnp.float32
BF16 = jnp.bfloat16

D_MODEL = 1024
DEPTH = 2
EPS = 1e-6
ROPE_THETA = 10000.0
BLOCK = 128
MLA_HEADS = 4
MLA_Q_RANK = 256
MLA_KV_RANK = 128
MLA_NOPE = 128
MLA_ROPE = 64
MLA_V = 128
MLA_QK = MLA_NOPE + MLA_ROPE
MLA_WIDTH = MLA_HEADS * MLA_V
SWA_HEADS = 8
SWA_KV_HEADS = 2
SWA_GROUP = SWA_HEADS // SWA_KV_HEADS
SWA_HEAD_DIM = 64
SWA_WIDTH = SWA_HEADS * SWA_HEAD_DIM
MIX_WIDTH = MLA_WIDTH + SWA_WIDTH
IN_SPLITS = (MLA_Q_RANK, MLA_KV_RANK, MLA_ROPE, SWA_WIDTH, SWA_KV_HEADS * SWA_HEAD_DIM, SWA_KV_HEADS * SWA_HEAD_DIM)
IN_COLS = sum(IN_SPLITS)
IN_OFFS = tuple(int(v) for v in np.cumsum((0,) + IN_SPLITS))
D_FF = 2816
MLA_SCALE = MLA_QK ** -0.5
LOG2E = 1.4426950408889634
LN2 = 0.6931471805599453
MLA_QSCALE = MLA_SCALE * LOG2E
SWA_SCALE = SWA_HEAD_DIM ** -0.5
NEG = -1e30

ADAM_LR = 0.001
ADAM_B1 = 0.9
ADAM_B2 = 0.999
ADAM_EPS = 1e-08
ADAM_WD = 0.01
ADAM_STEP = 10

N_SHARD = 4
N_DEV = 8
VMEM_LIMIT = 56 * 1024 * 1024
MESH = pl.DeviceIdType.MESH

WEIGHT_NAMES = ['ffn1_norm', 'ffn1_w_gate', 'ffn1_w_up', 'ffn1_w_down', 'mix_norm', 'w_in', 'mla_q_a_norm', 'mla_w_q_b',
                'mla_kv_a_norm', 'mla_w_kv_b', 'mla_q_norm', 'mla_k_norm', 'swa_q_norm', 'swa_k_norm', 'swa_sinks',
                'mla_out_norm', 'swa_out_norm', 'w_o', 'ffn2_norm', 'ffn2_w_gate', 'ffn2_w_up', 'ffn2_w_down']
TRANSPOSED = ('ffn1_w_gate', 'ffn1_w_up', 'ffn2_w_gate', 'ffn2_w_up')
BIG = {'ffn1_w_gate': ((D_FF, D_MODEL), 0), 'ffn1_w_up': ((D_FF, D_MODEL), 0), 'ffn1_w_down': ((D_FF, D_MODEL), 0),
       'w_in': ((D_MODEL, IN_COLS), 1), 'mla_w_q_b': ((MLA_Q_RANK, MLA_HEADS * MLA_QK), 1),
       'mla_w_kv_b': ((MLA_KV_RANK, MLA_HEADS * (MLA_NOPE + MLA_V)), 1), 'w_o': ((MIX_WIDTH, D_MODEL), 0),
       'ffn2_w_gate': ((D_FF, D_MODEL), 0), 'ffn2_w_up': ((D_FF, D_MODEL), 0), 'ffn2_w_down': ((D_FF, D_MODEL), 0)}
BIG_NAMES = [n for n in WEIGHT_NAMES if n in BIG]
SMALL_NAMES = [n for n in WEIGHT_NAMES if n not in BIG]

_pallas_call = pl.pallas_call


def _params(**kw):
    return pltpu.CompilerParams(vmem_limit_bytes=VMEM_LIMIT, **kw)


def _full(shape):
    n = len(shape)
    return pl.BlockSpec(shape, lambda *_: (0,) * n)


def _resident(shape):
    n = len(shape)
    return pl.BlockSpec(shape, lambda *_: (0,) * n, pipeline_mode=pl.Buffered(1))


@jax.custom_vjp
def _mm(a, w):
    return jnp.dot(a.astype(BF16), w, preferred_element_type=F32)


def _mm_fwd(a, w):
    return _mm(a, w), w


def _mm_bwd(w, dy):
    return lax.dot_general(dy.astype(BF16), w, (((1,), (1,)), ((), ())), preferred_element_type=F32), None


_mm.defvjp(_mm_fwd, _mm_bwd)


def _dot_nt(a, b):
    return lax.dot_general(a, b, (((1,), (1,)), ((), ())), preferred_element_type=F32)


def _dot_tn(a, b):
    return lax.dot_general(a, b, (((0,), (0,)), ((), ())), preferred_element_type=F32)


def _rms(t, g):
    return t * lax.rsqrt(jnp.mean(t * t, axis=-1, keepdims=True) + EPS) * g


def _sigmoid(z):
    return 1.0 / (1.0 + jnp.exp(-z))


def _row_tile(s, want):
    return min(want, s)


def _divisor_tile(rows, cap):
    return max(d for d in range(16, min(rows, cap) + 1, 16) if rows % d == 0)


FF_CHUNK = D_FF


def _weight_operand(w):
    if isinstance(w, tuple):
        arr, block, index = w
        return arr, pl.BlockSpec(block, lambda *_: index, pipeline_mode=pl.Buffered(1))
    return w, _resident(w.shape)


def _weight_rows(ref, start, n):
    if len(ref.shape) == 2:
        return ref[start:start + n, :]
    per = ref.shape[1]
    return ref[start // per:(start + n) // per].reshape(n, ref.shape[2])


def _ffn_fwd(x, g, wg, wu, wd, name, gather=None):
    s = x.shape[0]
    tm = _row_tile(s, 256)
    steps = s // tm
    ng = len(gather) if gather else 0
    (wg, wg_spec), (wu, wu_spec), (wd, wd_spec) = _weight_operand(wg), _weight_operand(wu), _weight_operand(wd)

    def body(x_ref, g_ref, wg_ref, wu_ref, wd_ref, *rest):
        g_ins, (y_ref, gate_ref, up_ref), g_outs, sems = rest[:ng], rest[ng:ng + 3], rest[ng + 3:2 * ng + 3], rest[2 * ng + 3:]
        if ng:
            _run_stages_at(_gather_stages(g_ins, g_outs, *sems), [(0,), (steps * 3 // 8,), (steps * 11 // 16,), (steps - 1,)])
        xv = x_ref[...]
        nb = _rms(xv, g_ref[...]).astype(BF16)
        acc = xv
        for c in range(0, D_FF, FF_CHUNK):
            gate = _dot_nt(nb, _weight_rows(wg_ref, c, FF_CHUNK))
            up = _dot_nt(nb, _weight_rows(wu_ref, c, FF_CHUNK))
            gate_ref[:, c:c + FF_CHUNK] = gate.astype(BF16)
            up_ref[:, c:c + FF_CHUNK] = up.astype(BF16)
            act = (gate * _sigmoid(gate) * up).astype(BF16)
            acc = acc + 0.5 * jnp.dot(act, _weight_rows(wd_ref, c, FF_CHUNK), preferred_element_type=F32)
        y_ref[...] = acc

    outs = _pallas_call(
        body, name=name, grid=(steps,),
        in_specs=[pl.BlockSpec((tm, D_MODEL), lambda i: (i, 0)), _full((1, D_MODEL)), wg_spec, wu_spec, wd_spec] + [_HBM] * ng,
        out_specs=[pl.BlockSpec((tm, D_MODEL), lambda i: (i, 0)), pl.BlockSpec((tm, D_FF), lambda i: (i, 0)),
                   pl.BlockSpec((tm, D_FF), lambda i: (i, 0))] + [_HBM] * ng,
        out_shape=[jax.ShapeDtypeStruct((s, D_MODEL), F32), jax.ShapeDtypeStruct((s, D_FF), BF16),
                   jax.ShapeDtypeStruct((s, D_FF), BF16)] + (_gather_out_shapes(gather) if ng else []),
        scratch_shapes=_exchange_scratch(8, ng) if ng else [],
        compiler_params=_params(dimension_semantics=("arbitrary",)),
    )(x, g, wg, wu, wd, *(gather or []))
    return outs[0], outs[1], outs[2], outs[3:]


def _ffn_bwd(x, dy, gate, up, g, wg, wu, wd, name, scatter=None):
    s = x.shape[0]
    tm = _row_tile(s, 256)
    steps = s // tm
    ng = len(scatter) if scatter else 0
    (wg, wg_spec), (wu, wu_spec), (wd, wd_spec) = _weight_operand(wg), _weight_operand(wu), _weight_operand(wd)

    def body(x_ref, dy_ref, gate_ref, up_ref, g_ref, wg_ref, wu_ref, wd_ref, *rest):
        c_ins, (dx_ref, dgain_ref, n_ref, act_ref, dgate_ref, dup_ref) = rest[:ng], rest[ng:ng + 6]
        c_outs, sems = rest[ng + 6:2 * ng + 6], rest[2 * ng + 6:]
        if ng:
            _run_stages_at(_scatter_stages(c_ins, c_outs, *sems), [(0,), (steps - 1,)])
        i = pl.program_id(0)
        xv = x_ref[...]
        dyv = dy_ref[...]
        gv = g_ref[...]
        r = lax.rsqrt(jnp.mean(xv * xv, axis=-1, keepdims=True) + EPS)
        xh = xv * r
        n_ref[...] = (xh * gv).astype(BF16)
        dyh = (0.5 * dyv).astype(BF16)
        dn = jnp.zeros_like(xv)
        for c in range(0, D_FF, FF_CHUNK):
            dact = _dot_nt(dyh, _weight_rows(wd_ref, c, FF_CHUNK))
            gt = gate_ref[:, c:c + FF_CHUNK].astype(F32)
            u = up_ref[:, c:c + FF_CHUNK].astype(F32)
            sg = _sigmoid(gt)
            sl = gt * sg
            act_ref[:, c:c + FF_CHUNK] = (sl * u).astype(BF16)
            dup = (dact * sl).astype(BF16)
            dgate = (dact * u * (sg * (1.0 + gt * (1.0 - sg)))).astype(BF16)
            dup_ref[:, c:c + FF_CHUNK] = dup
            dgate_ref[:, c:c + FF_CHUNK] = dgate
            dn = (dn + jnp.dot(dgate, _weight_rows(wg_ref, c, FF_CHUNK), preferred_element_type=F32)
                  + jnp.dot(dup, _weight_rows(wu_ref, c, FF_CHUNK), preferred_element_type=F32))
        part = jnp.sum(dn * xh, axis=0, keepdims=True)

        @pl.when(i == 0)
        def _():
            dgain_ref[...] = part

        @pl.when(i > 0)
        def _():
            dgain_ref[...] += part

        dxh = dn * gv
        dx_ref[...] = dyv + r * (dxh - xh * jnp.mean(dxh * xh, axis=-1, keepdims=True))

    row = lambda w: pl.BlockSpec((tm, w), lambda i: (i, 0))
    outs = _pallas_call(
        body, name=name, grid=(steps,),
        in_specs=[row(D_MODEL), row(D_MODEL), row(D_FF), row(D_FF), _full((1, D_MODEL)), wg_spec, wu_spec, wd_spec]
        + [_HBM] * ng,
        out_specs=[row(D_MODEL), _full((1, D_MODEL)), row(D_MODEL), row(D_FF), row(D_FF), row(D_FF)] + [_HBM] * ng,
        out_shape=[jax.ShapeDtypeStruct((s, D_MODEL), F32), jax.ShapeDtypeStruct((1, D_MODEL), F32),
                   jax.ShapeDtypeStruct((s, D_MODEL), BF16), jax.ShapeDtypeStruct((s, D_FF), BF16),
                   jax.ShapeDtypeStruct((s, D_FF), BF16), jax.ShapeDtypeStruct((s, D_FF), BF16)]
        + (_scatter_out_shapes(scatter) if ng else []),
        scratch_shapes=_exchange_scratch(3, ng) if ng else [],
        compiler_params=_params(dimension_semantics=("arbitrary",)),
    )(x, dy, gate, up, g, wg, wu, wd, *(scatter or []))
    return outs[:6], outs[6:]


def _store_col_shards(o_ref, acc, first_shard, n_here, width):
    for q in range(n_here):
        o_ref[q] = acc[:, (first_shard + q) * width:(first_shard + q + 1) * width].astype(BF16)


def _matmul_tn(a, b, scale, name, group=None, slot=0, slots=1, scatter=None):
    t, m = a.shape
    n = b.shape[1]
    tk = _row_tile(t, 2048)
    tn = n // 2
    nk = t // tk
    per = m // N_SHARD
    earlier = [] if group is None else [group]
    ne, ng = len(earlier), len(scatter) if scatter else 0

    def body(a_ref, b_ref, *rest):
        c_ins, o_ref, c_outs = rest[ne:ne + ng], rest[ne + ng], rest[ne + ng + 1:ne + 2 * ng + 1]
        acc_ref, sems = rest[ne + 2 * ng + 1], rest[ne + 2 * ng + 2:]
        if ng:
            _run_stages_at(_scatter_stages(c_ins, c_outs, *sems), [(0, 0), (n // tn - 1, nk - 1)])
        k = pl.program_id(1)
        bv = b_ref[...]
        if scale != 1.0:
            bv = bv.astype(F32) * scale
        part = _dot_tn(a_ref[...].astype(BF16), bv.astype(BF16))

        @pl.when(k == 0)
        def _():
            acc_ref[...] = part

        @pl.when(k > 0)
        def _():
            acc_ref[...] += part

        @pl.when(k == nk - 1)
        def _():
            for sh in range(N_SHARD):
                o_ref[sh] = acc_ref[sh * per:(sh + 1) * per, :].astype(BF16)

    outs = _pallas_call(
        body, name=name, grid=(n // tn, nk),
        in_specs=[pl.BlockSpec((tk, m), lambda j, k: (k, 0)), pl.BlockSpec((tk, tn), lambda j, k: (k, j))]
        + [pl.BlockSpec(memory_space=pl.ANY)] * ne + [_HBM] * ng,
        out_specs=[pl.BlockSpec((N_SHARD, per, tn), lambda j, k: (0, slot, j))] + [_HBM] * ng,
        out_shape=[jax.ShapeDtypeStruct((N_SHARD, slots * per, n), BF16)] + (_scatter_out_shapes(scatter) if ng else []),
        scratch_shapes=[pltpu.VMEM((m, tn), F32)] + (_exchange_scratch(3, ng) if ng else []),
        input_output_aliases={2: 0} if earlier else {},
        compiler_params=_params(dimension_semantics=("arbitrary", "arbitrary")),
    )(a, b, *earlier, *(scatter or []))
    return outs[0], outs[1:]


_HALF = SWA_HEAD_DIM // 2
_IN_ORDER = (list(range(0, IN_OFFS[2]))
             + [IN_OFFS[3] + SWA_HEAD_DIM * h + i for h in range(SWA_HEADS) for i in range(_HALF)]
             + [IN_OFFS[3] + SWA_HEAD_DIM * h + _HALF + i for h in range(SWA_HEADS) for i in range(_HALF)]
             + list(range(IN_OFFS[5], IN_OFFS[6]))
             + [IN_OFFS[4] + SWA_HEAD_DIM * j + i for j in range(SWA_KV_HEADS) for i in range(_HALF)]
             + [IN_OFFS[4] + SWA_HEAD_DIM * j + _HALF + i for j in range(SWA_KV_HEADS) for i in range(_HALF)]
             + list(range(IN_OFFS[2], IN_OFFS[3])))
_QB_ORDER = ([MLA_QK * h + i for h in range(MLA_HEADS) for i in range(MLA_NOPE)]
             + [MLA_QK * h + MLA_NOPE + i for h in range(MLA_HEADS) for i in range(_HALF)]
             + [MLA_QK * h + MLA_NOPE + _HALF + i for h in range(MLA_HEADS) for i in range(_HALF)])
_KVB_ORDER = ([(MLA_NOPE + MLA_V) * h + i for h in range(MLA_HEADS) for i in range(MLA_NOPE)]
              + [(MLA_NOPE + MLA_V) * h + MLA_NOPE + i for h in range(MLA_HEADS) for i in range(MLA_V)])
MIXER_ORDERS = {'w_in': _IN_ORDER, 'mla_w_q_b': _QB_ORDER, 'mla_w_kv_b': _KVB_ORDER}
_P_CQ, _P_CKV, _P_QA, _P_QB, _P_VS, _P_KA, _P_KB, _P_PE = (int(v) for v in np.cumsum(
    (0, MLA_Q_RANK, MLA_KV_RANK, SWA_WIDTH // 2, SWA_WIDTH // 2, IN_SPLITS[5], IN_SPLITS[4] // 2, IN_SPLITS[4] // 2)))


def _runs(order):
    out, start = [], 0
    for i in range(1, len(order) + 1):
        if i == len(order) or order[i] != order[i - 1] + 1:
            out.append((order[start], i - start))
            start = i
    return out


def _inverse(order):
    inv = [0] * len(order)
    for new, old in enumerate(order):
        inv[old] = new
    return inv


def _take_cols(a, order):
    return jnp.concatenate([a[..., st:st + w] for st, w in _runs(order)], axis=-1)


def _segment_matrix(n, seg):
    return (lax.broadcasted_iota(jnp.int32, (n, n), 0) // seg == lax.broadcasted_iota(jnp.int32, (n, n), 1) // seg).astype(BF16)


def _cmm2(t, b):
    hi = t.astype(BF16)
    lo = (t - hi.astype(F32)).astype(BF16)
    return jnp.dot(hi, b, preferred_element_type=F32) + jnp.dot(lo, b, preferred_element_type=F32)


@jax.custom_vjp
def _cmm(t, b, bt):
    return jnp.dot(t.astype(BF16), b, preferred_element_type=F32)


def _cmm_fwd(t, b, bt):
    return _cmm(t, b, bt), (b, bt)


def _cmm_bwd(res, dy):
    b, bt = res
    return _cmm(dy, bt, b), None, None


_cmm.defvjp(_cmm_fwd, _cmm_bwd)


def _segsum(t, b):
    return _cmm(t, b, b)


def _rowsum(t, exact=False):
    n = t.shape[-1]
    if exact:
        return _cmm2(t, jnp.ones((n, 128), BF16))
    return _cmm(t, jnp.ones((n, 128), BF16), jnp.ones((128, n), BF16))


def _by_head(vals, width):
    lane = lax.broadcasted_iota(jnp.int32, (vals[0].shape[0], len(vals) * width), 1)
    out = vals[-1]
    for hd in range(len(vals) - 2, -1, -1):
        out = jnp.where(lane < (hd + 1) * width, vals[hd], out)
    return out


def _rope2(a, b, cos, sin):
    return a * cos - b * sin, b * cos + a * sin


def _pre_math(x, gm, gqa, gkva, gq, gk, gsq, gsk, taps, win, wqb, wkvb, cos, sin):
    h = _rms(x, gm)
    proj = _mm(h, win)
    if taps is not None:
        proj = proj + taps[0]
    cqn = _rms(proj[:, _P_CQ:_P_CKV], gqa)
    qa_all = _mm(cqn, wqb)
    ckvn = _rms(proj[:, _P_CKV:_P_QA], gkva)
    kv_all = _mm(ckvn, wkvb)
    if taps is not None:
        qa_all = qa_all + taps[1]
        kv_all = kv_all + taps[2]
    nh, hw = MLA_HEADS, MLA_HEADS * _HALF
    seg_mla = _segment_matrix(hw, _HALF)
    tile = lambda g, n: jnp.concatenate([g] * n, axis=-1)
    c4, s4 = cos[:, :hw], sin[:, :hw]

    def mla_heads(nope, r1, r2, gain):
        rr = r1 * r1 + r2 * r2
        lane_head = lax.broadcasted_iota(jnp.int32, (hw, nh * MLA_NOPE), 0) // _HALF
        spread = (lane_head == lax.broadcasted_iota(jnp.int32, (hw, nh * MLA_NOPE), 1) // MLA_NOPE).astype(BF16)
        rope_on_nope = _cmm(rr, spread, spread.T)
        ss_nope = [_rowsum(jnp.square(nope[:, hd * MLA_NOPE:(hd + 1) * MLA_NOPE])) for hd in range(nh)]
        rinv = [lax.rsqrt((ss_nope[hd] + rope_on_nope[:, hd * MLA_NOPE:(hd + 1) * MLA_NOPE]) * (1.0 / MLA_QK) + EPS)
                for hd in range(nh)]
        rl = lax.rsqrt((_segsum(rr, seg_mla) + _by_head(ss_nope, _HALF)) * (1.0 / MLA_QK) + EPS)
        o1, o2 = _rope2(r1 * rl * tile(gain[:, MLA_NOPE:MLA_NOPE + _HALF], nh), r2 * rl * tile(gain[:, MLA_NOPE + _HALF:], nh), c4, s4)
        return [jnp.concatenate([nope[:, hd * MLA_NOPE:(hd + 1) * MLA_NOPE] * rinv[hd] * gain[:, :MLA_NOPE],
                                 o1[:, hd * _HALF:(hd + 1) * _HALF], o2[:, hd * _HALF:(hd + 1) * _HALF]], axis=-1)
                for hd in range(nh)]

    q_a = mla_heads(qa_all[:, :nh * MLA_NOPE], qa_all[:, nh * MLA_NOPE:nh * MLA_NOPE + hw], qa_all[:, nh * MLA_NOPE + hw:], gq)
    pe1, pe2 = proj[:, _P_PE:_P_PE + _HALF], proj[:, _P_PE + _HALF:_P_PE + 2 * _HALF]
    k_a = mla_heads(kv_all[:, :nh * MLA_NOPE], tile(pe1, nh), tile(pe2, nh), gk)
    v_a = [kv_all[:, nh * MLA_NOPE + hd * MLA_V:nh * MLA_NOPE + (hd + 1) * MLA_V] for hd in range(nh)]

    def swa_heads(a, b, gain, n):
        w = n * _HALF
        r = lax.rsqrt(_segsum(a * a + b * b, _segment_matrix(w, _HALF)) * (1.0 / SWA_HEAD_DIM) + EPS)
        o1, o2 = _rope2(a * r * tile(gain[:, :_HALF], n), b * r * tile(gain[:, _HALF:], n), cos[:, :w], sin[:, :w])
        return [jnp.concatenate([o1[:, hd * _HALF:(hd + 1) * _HALF], o2[:, hd * _HALF:(hd + 1) * _HALF]], axis=-1) for hd in range(n)]

    q_b = swa_heads(proj[:, _P_QA:_P_QB], proj[:, _P_QB:_P_VS], gsq, SWA_HEADS)
    k_b = swa_heads(proj[:, _P_KA:_P_KB], proj[:, _P_KB:_P_PE], gsk, SWA_KV_HEADS)
    v_b = [proj[:, _P_VS + j * SWA_HEAD_DIM:_P_VS + (j + 1) * SWA_HEAD_DIM] for j in range(SWA_KV_HEADS)]
    return (q_a, k_a, v_a, q_b, k_b, v_b), (h, cqn, ckvn)


_PRE_GAIN_WIDTHS = (D_MODEL, MLA_Q_RANK, MLA_KV_RANK, MLA_QK, MLA_QK, SWA_HEAD_DIM, SWA_HEAD_DIM)
_PRE_HEADS = ((MLA_HEADS, MLA_QK), (MLA_HEADS, MLA_QK), (MLA_HEADS, MLA_V),
              (SWA_HEADS, SWA_HEAD_DIM), (SWA_KV_HEADS, SWA_HEAD_DIM), (SWA_KV_HEADS, SWA_HEAD_DIM))


def _pre_fwd(x, gains, win, wqb, wkvb, cos, sin, name):
    s = x.shape[0]
    tm = _row_tile(s, 512)

    def body(x_ref, *refs):
        g_refs, (win_ref, wqb_ref, wkvb_ref, cos_ref, sin_ref), out_refs = refs[:7], refs[7:12], refs[12:]
        outs, _ = _pre_math(x_ref[...], *[g[...] for g in g_refs], None, win_ref[...], wqb_ref[...], wkvb_ref[...],
                            cos_ref[...], sin_ref[...])
        for idx, (ref, heads) in enumerate(zip(out_refs, outs)):
            for hd, val in enumerate(heads):
                ref[hd] = (val * MLA_QSCALE if idx == 0 else val).astype(BF16)

    heads_spec = lambda nh, w: pl.BlockSpec((nh, tm, w), lambda i: (0, i, 0))
    return _pallas_call(
        body, name=name, grid=(s // tm,),
        in_specs=[pl.BlockSpec((tm, D_MODEL), lambda i: (i, 0))] + [_full((1, w)) for w in _PRE_GAIN_WIDTHS]
        + [_resident(win.shape), _resident(wqb.shape), _resident(wkvb.shape),
           pl.BlockSpec((tm, SWA_HEADS * _HALF), lambda i: (i, 0)), pl.BlockSpec((tm, SWA_HEADS * _HALF), lambda i: (i, 0))],
        out_specs=[heads_spec(nh, w) for nh, w in _PRE_HEADS],
        out_shape=[jax.ShapeDtypeStruct((nh, s, w), BF16) for nh, w in _PRE_HEADS],
        compiler_params=_params(dimension_semantics=("arbitrary",)),
    )(x, *gains, win, wqb, wkvb, cos, sin)


def _pre_bwd(x, dx_res, cts, gains, win, wqb, wkvb, cos, sin, name):
    s = x.shape[0]
    tm = _row_tile(s, 256)
    tap_widths = (IN_COLS, MLA_HEADS * MLA_QK, MLA_HEADS * (MLA_NOPE + MLA_V))

    def body(x_ref, dxr_ref, *refs):
        ct_refs, g_refs = refs[:6], refs[6:13]
        win_ref, wqb_ref, wkvb_ref, cos_ref, sin_ref = refs[13:18]
        dx_ref, dg_refs, dw_refs, acc_refs = refs[18], refs[19:26], refs[26:29], refs[29:32]
        i = pl.program_id(0)
        win_v, wqb_v, wkvb_v, cos_v, sin_v = win_ref[...], wqb_ref[...], wkvb_ref[...], cos_ref[...], sin_ref[...]

        def f(xv, gm, gqa, gkva, gq, gk, gsq, gsk, t0, t1, t2):
            return _pre_math(xv, gm, gqa, gkva, gq, gk, gsq, gsk, (t0, t1, t2), win_v, wqb_v, wkvb_v, cos_v, sin_v)

        taps = [jnp.zeros((tm, w), F32) for w in tap_widths]
        _, vjp, acts = jax.vjp(f, x_ref[...], *[g[...] for g in g_refs], *taps, has_aux=True)
        ct = tuple([ref[hd] for hd in range(nh)] for ref, (nh, _) in zip(ct_refs, _PRE_HEADS))
        grads = vjp(ct)
        dx_ref[...] = grads[0] + dxr_ref[...]
        dws = [_dot_tn(a.astype(BF16), t.astype(BF16)) for a, t in zip(acts, grads[8:11])]

        @pl.when(i == 0)
        def _():
            for ref, val in zip(dg_refs, grads[1:8]):
                ref[...] = val
            for ref, val in zip(acc_refs, dws):
                ref[...] = val

        @pl.when(i > 0)
        def _():
            for ref, val in zip(dg_refs, grads[1:8]):
                ref[...] += val
            for ref, val in zip(acc_refs, dws):
                ref[...] += val

        @pl.when(i == s // tm - 1)
        def _():
            for ref, acc, order in zip(dw_refs, acc_refs, (_IN_ORDER, _QB_ORDER, _KVB_ORDER)):
                _store_col_shards(ref, _take_cols(acc[...], _inverse(order)), 0, N_SHARD, acc.shape[1] // N_SHARD)

    heads_spec = lambda nh, w: pl.BlockSpec((nh, tm, w), lambda i: (0, i, 0))
    row = pl.BlockSpec((tm, D_MODEL), lambda i: (i, 0))
    half = pl.BlockSpec((tm, SWA_HEADS * _HALF), lambda i: (i, 0))
    shard_shapes = [(N_SHARD, w.shape[0], w.shape[1] // N_SHARD) for w in (win, wqb, wkvb)]
    return _pallas_call(
        body, name=name, grid=(s // tm,),
        in_specs=[row, row] + [heads_spec(nh, w) for nh, w in _PRE_HEADS] + [_full((1, w)) for w in _PRE_GAIN_WIDTHS]
        + [_resident(win.shape), _resident(wqb.shape), _resident(wkvb.shape), half, half],
        out_specs=[row] + [_full((1, w)) for w in _PRE_GAIN_WIDTHS] + [_full(shp) for shp in shard_shapes],
        out_shape=[jax.ShapeDtypeStruct((s, D_MODEL), F32)] + [jax.ShapeDtypeStruct((1, w), F32) for w in _PRE_GAIN_WIDTHS]
        + [jax.ShapeDtypeStruct(shp, BF16) for shp in shard_shapes],
        scratch_shapes=[pltpu.VMEM(w.shape, F32) for w in (win, wqb, wkvb)],
        compiler_params=_params(dimension_semantics=("arbitrary",)),
    )(x, dx_res, *cts, *gains, win, wqb, wkvb, cos, sin)


def _post_math(oa, ob, ga, gb, wo):
    mixed = jnp.concatenate([_rms(jnp.concatenate(oa, axis=-1), ga), _rms(jnp.concatenate(ob, axis=-1), gb)], axis=-1)
    return _mm(mixed, wo), mixed


def _post_fwd(x, oa, ob, ga, gb, wo, name):
    s = x.shape[0]
    tm = _row_tile(s, 512)

    def body(x_ref, oa_ref, ob_ref, ga_ref, gb_ref, wo_ref, y_ref):
        y, _ = _post_math([oa_ref[hd] for hd in range(MLA_HEADS)], [ob_ref[hd] for hd in range(SWA_HEADS)],
                          ga_ref[...], gb_ref[...], wo_ref[...])
        y_ref[...] = x_ref[...] + y

    row = pl.BlockSpec((tm, D_MODEL), lambda i: (i, 0))
    return _pallas_call(
        body, name=name, grid=(s // tm,),
        in_specs=[row, pl.BlockSpec((MLA_HEADS, tm, MLA_V), lambda i: (0, i, 0)),
                  pl.BlockSpec((SWA_HEADS, tm, SWA_HEAD_DIM), lambda i: (0, i, 0)),
                  _full((1, MLA_WIDTH)), _full((1, SWA_WIDTH)), _resident(wo.shape)],
        out_specs=row, out_shape=jax.ShapeDtypeStruct((s, D_MODEL), F32),
        compiler_params=_params(dimension_semantics=("arbitrary",)),
    )(x, oa, ob, ga, gb, wo)


def _post_bwd(dy, oa, ob, ga, gb, wo, name):
    s = dy.shape[0]
    tm = _row_tile(s, 512)
    t = _attn_tile(s)

    def body(dy_ref, oa_ref, ob_ref, ga_ref, gb_ref, wo_ref, doa_ref, dob_ref, dga_ref, dgb_ref, dwo_ref, delta_ref, acc_ref):
        i = pl.program_id(0)
        wo_v = wo_ref[...]
        dyv = dy_ref[...]

        def f(oa_l, ob_l, ga_v, gb_v):
            return _post_math(oa_l, ob_l, ga_v, gb_v, wo_v)

        _, vjp, mixed = jax.vjp(f, [oa_ref[hd] for hd in range(MLA_HEADS)], [ob_ref[hd] for hd in range(SWA_HEADS)],
                                ga_ref[...], gb_ref[...], has_aux=True)
        doa, dob, dga, dgb = vjp(dyv)
        for hd in range(MLA_HEADS):
            doa_ref[hd] = doa[hd]
            rows = _as_rows(_rowsum(doa[hd] * oa_ref[hd], exact=True))
            for j in range(tm // t):
                delta_ref[hd, j] = rows[:, j * t:(j + 1) * t]
        for hd in range(SWA_HEADS):
            dob_ref[hd] = dob[hd]
        dwo = _dot_tn(mixed.astype(BF16), dyv.astype(BF16))

        @pl.when(i == 0)
        def _():
            dga_ref[...] = dga
            dgb_ref[...] = dgb
            acc_ref[...] = dwo

        @pl.when(i > 0)
        def _():
            dga_ref[...] += dga
            dgb_ref[...] += dgb
            acc_ref[...] += dwo

        @pl.when(i == s // tm - 1)
        def _():
            dwo_ref[...] = acc_ref[...].astype(BF16)

    row = pl.BlockSpec((tm, D_MODEL), lambda i: (i, 0))
    oa_spec = pl.BlockSpec((MLA_HEADS, tm, MLA_V), lambda i: (0, i, 0))
    ob_spec = pl.BlockSpec((SWA_HEADS, tm, SWA_HEAD_DIM), lambda i: (0, i, 0))
    return _pallas_call(
        body, name=name, grid=(s // tm,),
        in_specs=[row, oa_spec, ob_spec, _full((1, MLA_WIDTH)), _full((1, SWA_WIDTH)), _resident(wo.shape)],
        out_specs=[oa_spec, ob_spec, _full((1, MLA_WIDTH)), _full((1, SWA_WIDTH)), _full(wo.shape),
                   pl.BlockSpec((MLA_HEADS, tm // t, 8, t), lambda i: (0, i, 0, 0))],
        out_shape=[jax.ShapeDtypeStruct((MLA_HEADS, s, MLA_V), F32), jax.ShapeDtypeStruct((SWA_HEADS, s, SWA_HEAD_DIM), F32),
                   jax.ShapeDtypeStruct((1, MLA_WIDTH), F32), jax.ShapeDtypeStruct((1, SWA_WIDTH), F32),
                   jax.ShapeDtypeStruct(wo.shape, BF16), jax.ShapeDtypeStruct((MLA_HEADS, s // t, 8, t), F32)],
        scratch_shapes=[pltpu.VMEM(wo.shape, F32)],
        compiler_params=_params(dimension_semantics=("arbitrary",)),
    )(dy, oa, ob, ga, gb, wo)


def _attn_tile(s):
    return 512 if s >= 2048 else 128


def _as_rows(cols):
    return cols.T[0:8, :]


def _causal_mask(t):
    return lax.broadcasted_iota(jnp.int32, (t, t), 1) <= lax.broadcasted_iota(jnp.int32, (t, t), 0)


def _pipelined_blocks(first, count, last_block, issue, consume, carry, prefetch_after):
    def clamped(j, slot):
        issue(jnp.minimum(j, last_block), slot)

    def pair(jj, c):
        a = first + 2 * jj
        clamped(a + 1, 1)
        c = consume(a, 0, c)
        clamped(a + 2, 0)
        return consume(a + 1, 1, c)

    clamped(first, 0)
    npairs = count // 2
    carry = lax.fori_loop(0, npairs, pair, carry)

    def odd(c):
        c = consume(first + 2 * npairs, 0, c)
        if prefetch_after:
            clamped(first + count, 0)
        return c

    return lax.cond(count - 2 * npairs == 1, odd, lambda c: c, carry)


def _run_stages_at(stages, steps):
    for stage, step in zip(stages, steps):
        here = pl.program_id(0) == step[0]
        for axis in range(1, len(step)):
            here = here & (pl.program_id(axis) == step[axis])
        pl.when(here)(stage)


def _mla_fwd(q, k, v, name, gather=None):
    nh, s, _ = q.shape
    t = _attn_tile(s)
    nq = s // t
    ng = len(gather) if gather else 0

    def body(q_ref, k_ref, v_ref, *rest):
        g_ins, (o_ref, lse_ref), g_outs = rest[:ng], rest[ng:ng + 2], rest[ng + 2:2 * ng + 2]
        (s0_ref, s1_ref), sems = rest[2 * ng + 2:2 * ng + 4], rest[2 * ng + 4:]
        if ng:
            _run_stages_at(_gather_stages(g_ins, g_outs, *sems), [(0, 0), (nh // 2, 0), (nh - 1, 0), (nh - 1, nq - 1)])
        qi = pl.program_id(1)
        qv = q_ref[...]
        s_refs = (s0_ref, s1_ref)

        def rows(j):
            return pl.ds(pl.multiple_of(j * t, t), t)

        def issue(j, slot):
            s_refs[slot][...] = _dot_nt(qv, k_ref[rows(j), :])

        def consume(j, slot, carry, masked=False):
            m, l, acc = carry
            sc = s_refs[slot][...]
            if masked:
                sc = jnp.where(_causal_mask(t), sc, NEG)
            m_new = jnp.maximum(m, jnp.max(sc, axis=-1, keepdims=True))
            alpha = jnp.exp2(m - m_new)
            p = jnp.exp2(sc - m_new)
            l = alpha * l + jnp.sum(p, axis=-1, keepdims=True)
            acc = alpha * acc + jnp.dot(p.astype(BF16), v_ref[rows(j), :], preferred_element_type=F32)
            return m_new, l, acc

        init = (jnp.full((t, 1), NEG, F32), jnp.zeros((t, 1), F32), jnp.zeros((t, MLA_V), F32))
        carry = _pipelined_blocks(0, qi, nq - 1, issue, consume, init, True)
        m, l, acc = consume(qi, 0, carry, masked=True)
        o_ref[...] = acc / l
        lse_ref[...] = _as_rows(jnp.broadcast_to(m + jnp.log2(l), (t, 128)))

    outs = _pallas_call(
        body, name=name, grid=(nh, nq),
        in_specs=[pl.BlockSpec((None, t, MLA_QK), lambda h, i: (h, i, 0)), pl.BlockSpec((None, s, MLA_QK), lambda h, i: (h, 0, 0)),
                  pl.BlockSpec((None, s, MLA_V), lambda h, i: (h, 0, 0))] + [_HBM] * ng,
        out_specs=[pl.BlockSpec((None, t, MLA_V), lambda h, i: (h, i, 0)), pl.BlockSpec((None, None, 8, t), lambda h, i: (h, i, 0, 0))]
        + [_HBM] * ng,
        out_shape=[jax.ShapeDtypeStruct((nh, s, MLA_V), F32), jax.ShapeDtypeStruct((nh, nq, 8, t), F32)]
        + (_gather_out_shapes(gather) if ng else []),
        scratch_shapes=[pltpu.VMEM((t, t), F32)] * 2 + (_exchange_scratch(8, ng) if ng else []),
        compiler_params=_params(dimension_semantics=("arbitrary", "arbitrary")),
    )(q, k, v, *(gather or []))
    return outs[0], outs[1], outs[2:]


def _mla_bwd(q, k, v, do, lse_row, delta_row, name, scatter=None):
    nh, s, _ = q.shape
    t = _attn_tile(s)
    nq = s // t
    ng = len(scatter) if scatter else 0

    def body(q_ref, k_ref, v_ref, do_ref, lse_ref, delta_ref, *rest):
        c_ins, (dq_ref, dk_ref, dv_ref), c_outs = rest[:ng], rest[ng:ng + 3], rest[ng + 3:2 * ng + 3]
        (s0_ref, s1_ref, dp0_ref, dp1_ref), sems = rest[2 * ng + 3:2 * ng + 7], rest[2 * ng + 7:]
        if ng:
            _run_stages_at(_scatter_stages(c_ins, c_outs, *sems), [(0, 0), (nh - 1, nq - 1)])
        kj = pl.program_id(1)
        kv_, vv = k_ref[...], v_ref[...]
        s_refs, dp_refs = (s0_ref, s1_ref), (dp0_ref, dp1_ref)

        @pl.when(kj == 0)
        def _():
            dq_ref[...] = jnp.zeros_like(dq_ref)

        def rows(i):
            return pl.ds(pl.multiple_of(i * t, t), t)

        def issue(i, slot):
            s_refs[slot][...] = _dot_nt(kv_, q_ref[rows(i), :])
            dp_refs[slot][...] = _dot_nt(vv, do_ref[rows(i), :].astype(BF16))

        def consume(i, slot, carry, masked=False):
            dk, dv = carry
            p = jnp.exp2(s_refs[slot][...] - lse_ref[i][0:1, :])
            if masked:
                p = jnp.where(lax.broadcasted_iota(jnp.int32, (t, t), 0) <= lax.broadcasted_iota(jnp.int32, (t, t), 1), p, 0.0)
            dv = dv + jnp.dot(p.astype(BF16), do_ref[rows(i), :].astype(BF16), preferred_element_type=F32)
            ds = (p * (dp_refs[slot][...] - delta_ref[i][0:1, :])).astype(BF16)
            dk = dk + jnp.dot(ds, q_ref[rows(i), :], preferred_element_type=F32)
            dq_ref[rows(i), :] += _dot_tn(ds, kv_) * MLA_SCALE
            return dk, dv

        issue(kj, 0)
        carry = consume(kj, 0, (jnp.zeros((t, MLA_QK), F32), jnp.zeros((t, MLA_V), F32)), masked=True)
        dk, dv = _pipelined_blocks(kj + 1, nq - 1 - kj, nq - 1, issue, consume, carry, False)
        dk_ref[...] = dk * LN2
        dv_ref[...] = dv

    tile = lambda w: pl.BlockSpec((None, t, w), lambda h, j: (h, j, 0))
    whole = lambda w: pl.BlockSpec((None, s, w), lambda h, j: (h, 0, 0))
    rows_spec = pl.BlockSpec((None, nq, 8, t), lambda h, j: (h, 0, 0, 0))
    outs = _pallas_call(
        body, name=name, grid=(nh, nq),
        in_specs=[whole(MLA_QK), tile(MLA_QK), tile(MLA_V), whole(MLA_V), rows_spec, rows_spec] + [_HBM] * ng,
        out_specs=[whole(MLA_QK), tile(MLA_QK), tile(MLA_V)] + [_HBM] * ng,
        out_shape=[jax.ShapeDtypeStruct((nh, s, MLA_QK), F32), jax.ShapeDtypeStruct((nh, s, MLA_QK), F32),
                   jax.ShapeDtypeStruct((nh, s, MLA_V), F32)] + (_scatter_out_shapes(scatter) if ng else []),
        scratch_shapes=[pltpu.VMEM((t, t), F32)] * 4 + (_exchange_scratch(3, ng) if ng else []),
        compiler_params=_params(dimension_semantics=("arbitrary", "arbitrary")),
    )(q, k, v, do, lse_row, delta_row, *(scatter or []))
    return outs[0], outs[1], outs[2], outs[3:]


def _swa_tile(s):
    return min(s, 8 * BLOCK)


def _swa_specs(tq):
    nb = tq // BLOCK
    grp = lambda w: pl.BlockSpec((SWA_GROUP, tq, w), lambda j, i: (j, i, 0))
    main = pl.BlockSpec((None, tq, SWA_HEAD_DIM), lambda j, i: (j, i, 0))
    tail = pl.BlockSpec((None, BLOCK, SWA_HEAD_DIM), lambda j, i: (j, nb * (i + 1), 0))
    sink = pl.BlockSpec((None, SWA_GROUP, 128), lambda j, i: (j, 0, 0))
    return grp, main, tail, sink


def _swa_band_mask(first):
    shape = (SWA_GROUP * BLOCK, 2 * BLOCK)
    q_rel = (lax.broadcasted_iota(jnp.int32, shape, 0) & (BLOCK - 1)) + BLOCK
    k_rel = lax.broadcasted_iota(jnp.int32, shape, 1)
    dist = q_rel - k_rel
    return (dist >= 0) & (dist < BLOCK) & ((k_rel >= BLOCK) | jnp.logical_not(first))


def _swa_sink_column(sink_ref):
    sk = sink_ref[...]
    return jnp.concatenate([jnp.broadcast_to(sk[g:g + 1, 0:1], (BLOCK, 1)) for g in range(SWA_GROUP)], axis=0)


def _swa_fwd(q, kpad, vpad, sinks, name):
    _, s, _ = q.shape
    tq = _swa_tile(s)
    grp, main, tail, sink = _swa_specs(tq)
    d = SWA_HEAD_DIM

    def body(q_ref, km_ref, kt_ref, vm_ref, vt_ref, sink_ref, o_ref, lse_ref):
        i = pl.program_id(1)
        kall = jnp.concatenate([km_ref[...], kt_ref[...]], axis=0)
        vall = jnp.concatenate([vm_ref[...], vt_ref[...]], axis=0)
        sink_col = _swa_sink_column(sink_ref)
        for b in range(tq // BLOCK):
            lo = b * BLOCK
            valid = _swa_band_mask(i == 0 if b == 0 else False)
            q4 = q_ref[:, lo:lo + BLOCK, :].reshape(SWA_GROUP * BLOCK, d)
            sc = jnp.where(valid, _dot_nt(q4, kall[lo:lo + 2 * BLOCK]) * SWA_SCALE, NEG)
            m = jnp.maximum(jnp.max(sc, axis=-1, keepdims=True), sink_col)
            e = jnp.exp(sc - m)
            den = jnp.sum(e, axis=-1, keepdims=True) + jnp.exp(sink_col - m)
            out = jnp.dot((e * (1.0 / den)).astype(BF16), vall[lo:lo + 2 * BLOCK], preferred_element_type=F32)
            o_ref[:, lo:lo + BLOCK, :] = out.reshape(SWA_GROUP, BLOCK, d)
            lse_ref[:, lo:lo + BLOCK, :] = (m + jnp.log(den)).reshape(SWA_GROUP, BLOCK, 1)

    return _pallas_call(
        body, name=name, grid=(SWA_KV_HEADS, s // tq),
        in_specs=[grp(d), main, tail, main, tail, sink], out_specs=[grp(d), grp(1)],
        out_shape=[jax.ShapeDtypeStruct((SWA_HEADS, s, d), F32), jax.ShapeDtypeStruct((SWA_HEADS, s, 1), F32)],
        compiler_params=_params(dimension_semantics=("arbitrary", "arbitrary")),
    )(q, kpad, kpad, vpad, vpad, sinks)


def _swa_bwd(q, kpad, vpad, sinks, o, lse, do, name):
    _, s, _ = q.shape
    tq = _swa_tile(s)
    grp, main, tail, sink = _swa_specs(tq)
    d = SWA_HEAD_DIM

    def body(q_ref, km_ref, kt_ref, vm_ref, vt_ref, sink_ref, o_ref, lse_ref, do_ref, dq_ref, dk_ref, dv_ref, dsink_ref):
        i = pl.program_id(1)
        kall = jnp.concatenate([km_ref[...], kt_ref[...]], axis=0)
        vall = jnp.concatenate([vm_ref[...], vt_ref[...]], axis=0)
        sink_col = _swa_sink_column(sink_ref)

        @pl.when(i == 0)
        def _():
            dk_ref[...] = jnp.zeros_like(dk_ref)
            dv_ref[...] = jnp.zeros_like(dv_ref)
            dsink_ref[...] = jnp.zeros_like(dsink_ref)

        dsink = jnp.zeros((SWA_GROUP * BLOCK, 1), F32)
        for b in range(tq // BLOCK):
            lo = b * BLOCK
            valid = _swa_band_mask(i == 0 if b == 0 else False)
            rows4 = SWA_GROUP * BLOCK
            q4 = q_ref[:, lo:lo + BLOCK, :].reshape(rows4, d)
            do4 = do_ref[:, lo:lo + BLOCK, :].reshape(rows4, d)
            lse4 = lse_ref[:, lo:lo + BLOCK, :].reshape(rows4, 1)
            delta = jnp.sum(do4 * o_ref[:, lo:lo + BLOCK, :].reshape(rows4, d), axis=-1, keepdims=True)
            kb, vb = kall[lo:lo + 2 * BLOCK], vall[lo:lo + 2 * BLOCK]
            do4b = do4.astype(BF16)
            p = jnp.where(valid, jnp.exp(_dot_nt(q4, kb) * SWA_SCALE - lse4), 0.0)
            ds = (p * (_dot_nt(do4b, vb) - delta) * SWA_SCALE).astype(BF16)
            dq_ref[:, lo:lo + BLOCK, :] = jnp.dot(ds, kb, preferred_element_type=F32).reshape(SWA_GROUP, BLOCK, d)
            band = pl.ds(pl.multiple_of(i * tq, BLOCK) + lo, 2 * BLOCK)
            dk_ref[band, :] += _dot_tn(ds, q4)
            dv_ref[band, :] += _dot_tn(p.astype(BF16), do4b)
            dsink = dsink - jnp.exp(sink_col - lse4) * delta
        per_head = [jnp.broadcast_to(jnp.sum(dsink[g * BLOCK:(g + 1) * BLOCK], axis=0, keepdims=True), (1, 128))
                    for g in range(SWA_GROUP)]
        dsink_ref[...] += jnp.concatenate(per_head + [jnp.zeros((8 - SWA_GROUP, 128), F32)], axis=0)

    acc = pl.BlockSpec((None, s + BLOCK, d), lambda j, i: (j, 0, 0))
    return _pallas_call(
        body, name=name, grid=(SWA_KV_HEADS, s // tq),
        in_specs=[grp(d), main, tail, main, tail, sink, grp(d), grp(1), grp(d)],
        out_specs=[grp(d), acc, acc, pl.BlockSpec((None, 8, 128), lambda j, i: (j, 0, 0))],
        out_shape=[jax.ShapeDtypeStruct((SWA_HEADS, s, d), F32),
                   jax.ShapeDtypeStruct((SWA_KV_HEADS, s + BLOCK, d), F32),
                   jax.ShapeDtypeStruct((SWA_KV_HEADS, s + BLOCK, d), F32),
                   jax.ShapeDtypeStruct((SWA_KV_HEADS, 8, 128), F32)],
        compiler_params=_params(dimension_semantics=("arbitrary", "arbitrary")),
    )(q, kpad, kpad, vpad, vpad, sinks, o, lse, do)


def _loss_head(y, target, name):
    s = y.shape[0]
    tm = _row_tile(s, 1024)

    def body(y_ref, t_ref, dy_ref, loss_ref):
        i = pl.program_id(0)
        err = y_ref[...] - t_ref[...]
        dy_ref[...] = err * (1.0 / D_MODEL)
        part = jnp.broadcast_to(0.5 * jnp.sum(jnp.mean(err * err, axis=-1, keepdims=True), axis=0, keepdims=True), (1, 128))

        @pl.when(i == 0)
        def _():
            loss_ref[...] = part

        @pl.when(i > 0)
        def _():
            loss_ref[...] += part

    row = pl.BlockSpec((tm, D_MODEL), lambda i: (i, 0))
    return _pallas_call(
        body, name=name, grid=(s // tm,), in_specs=[row, row], out_specs=[row, _full((1, 128))],
        out_shape=[jax.ShapeDtypeStruct((s, D_MODEL), F32), jax.ShapeDtypeStruct((1, 128), F32)],
        compiler_params=_params(dimension_semantics=("arbitrary",)),
    )(y, target)


def _adamw(w, g, m, v, name):
    rows, cols = w.shape
    tr = _divisor_tile(rows, 512) if rows % 16 == 0 else rows

    def body(w_ref, g_ref, m_ref, v_ref, d_ref, nm_ref, nv_ref):
        gv = g_ref[...]
        nm = ADAM_B1 * m_ref[...] + (1.0 - ADAM_B1) * gv
        nv = ADAM_B2 * v_ref[...] + (1.0 - ADAM_B2) * (gv * gv)
        m_hat = nm / (1.0 - ADAM_B1 ** ADAM_STEP)
        v_hat = nv / (1.0 - ADAM_B2 ** ADAM_STEP)
        d_ref[...] = -ADAM_LR * (m_hat / (jnp.sqrt(v_hat) + ADAM_EPS) + ADAM_WD * w_ref[...])
        nm_ref[...] = nm
        nv_ref[...] = nv

    blk = pl.BlockSpec((tr, cols), lambda i: (i, 0))
    return _pallas_call(
        body, name=name, grid=(rows // tr,), in_specs=[blk] * 4, out_specs=[blk] * 3,
        out_shape=[jax.ShapeDtypeStruct((rows, cols), F32)] * 3,
        compiler_params=_params(dimension_semantics=("arbitrary",)),
    )(w, g, m, v)


def _position():
    return lax.axis_index("x"), lax.axis_index("y"), lax.axis_index("c")


def _remote(src, dst, send_sems, recv_sems, k, to):
    return pltpu.make_async_remote_copy(src_ref=src, dst_ref=dst, send_sem=send_sems.at[k], recv_sem=recv_sems.at[k],
                                        device_id=to, device_id_type=MESH)


_HBM = pl.BlockSpec(memory_space=pltpu.HBM)


def _gather_stages(ins, outs, send_sems, recv_sems):
    na = len(ins)
    x, y, c = _position()
    me, sibling = (x, y, c), (x, y, 1 - c)
    xn, yn, dg = (1 - x, y), (x, 1 - y), (1 - x, 1 - y)

    def slot(a, chip, pc, half=None):
        ref = outs[a].at[4 * chip[0] + 2 * chip[1] + pc]
        if half is None:
            return ref
        rows = ref.shape[0] // 2
        return ref.at[pl.ds(half * rows, rows)]

    def cp(a, k, chip, pc, half, to, src=None):
        dst = slot(a, chip, pc, half)
        return _remote(dst if src is None else src, dst, send_sems, recv_sems, 8 * a + k, to)

    first_hop = [(0, xn), (1, yn)]
    second_hop = [(0, xn, 2, 0, yn), (1, yn, 3, 1, xn)]

    def sends():
        out = []
        for a in range(na):
            out += [cp(a, k, (x, y), c, None, (*to, c), src=ins[a].at[c]) for k, to in first_hop]
            out += [cp(a, fwd_k, frm, c, half, (*to, c)) for _, frm, fwd_k, half, to in second_hop]
            out += [cp(a, 4 + k, frm, c, None, sibling) for k, frm in first_hop]
            out += [cp(a, 6 + half, dg, c, half, sibling) for half in (0, 1)]
        return out

    def stage0():
        for a in range(na):
            for k, to in first_hop:
                cp(a, k, (x, y), c, None, (*to, c), src=ins[a].at[c]).start()

    def stage1():
        for k, frm, fwd_k, half, to in second_hop:
            for a in range(na):
                cp(a, k, frm, c, None, me).wait_recv()
                cp(a, fwd_k, frm, c, half, (*to, c)).start()
                cp(a, 4 + k, frm, c, None, sibling).start()

    def stage2():
        for half in (0, 1):
            for a in range(na):
                cp(a, 2 + half, dg, c, half, me).wait_recv()
                cp(a, 6 + half, dg, c, half, sibling).start()

    def stage3():
        for a in range(na):
            for k, chip, half in ((4, xn, None), (5, yn, None), (6, dg, 0), (7, dg, 1)):
                cp(a, k, chip, 1 - c, half, me).wait_recv()
        for sent in sends():
            sent.wait_send()

    return [stage0, stage1, stage2, stage3]


def _gather_out_shapes(mine):
    return [jax.ShapeDtypeStruct((N_DEV,) + m.shape[1:], m.dtype) for m in mine]


def _exchange_scratch(per_array, na):
    return [pltpu.SemaphoreType.DMA((per_array * na,)), pltpu.SemaphoreType.DMA((per_array * na,))]


def _all_gather_halves(mine, name):
    na = len(mine)

    def body(*refs):
        for stage in _gather_stages(refs[:na], refs[na:2 * na], *refs[2 * na:]):
            stage()

    return _pallas_call(body, name=name, in_specs=[_HBM] * na, out_specs=[_HBM] * na, out_shape=_gather_out_shapes(mine),
                        scratch_shapes=_exchange_scratch(8, na))(*mine)


def _sibling_exchange(parts, name, other_half):
    na = len(parts)

    def body(*refs):
        ins, outs, (send_sems, recv_sems) = refs[:na], refs[na:2 * na], refs[2 * na:]
        x, y, c = _position()
        copies = [_remote(ins[a].at[:, 1 - c] if other_half else ins[a], outs[a], send_sems, recv_sems, a, (x, y, 1 - c))
                  for a in range(na)]
        for cp in copies:
            cp.start()
        for cp in copies:
            cp.wait()

    return _pallas_call(
        body, name=name, in_specs=[_HBM] * na, out_specs=[_HBM] * na,
        out_shape=[jax.ShapeDtypeStruct(p.shape[:1] + p.shape[2:] if other_half else p.shape, p.dtype) for p in parts],
        scratch_shapes=_exchange_scratch(1, na),
    )(*parts)


def _scatter_stages(ins, outs, send_sems, recv_sems):
    na = len(ins)
    x, y, c = _position()
    chips = [(1 - x, y), (x, 1 - y), (1 - x, 1 - y)]

    def copies():
        return [_remote(ins[a].at[2 * px + py], outs[a].at[j], send_sems, recv_sems, 3 * a + j, (px, py, c))
                for a in range(na) for j, (px, py) in enumerate(chips)]

    def start():
        for cp in copies():
            cp.start()

    def wait():
        for cp in copies():
            cp.wait()

    return [start, wait]


def _scatter_out_shapes(parts):
    return [jax.ShapeDtypeStruct((3,) + p.shape[1:], p.dtype) for p in parts]


def _scatter_to_chips(parts, name):
    na = len(parts)

    def body(*refs):
        for stage in _scatter_stages(refs[:na], refs[na:2 * na], *refs[2 * na:]):
            stage()

    return _pallas_call(body, name=name, in_specs=[_HBM] * na, out_specs=[_HBM] * na, out_shape=_scatter_out_shapes(parts),
                        scratch_shapes=_exchange_scratch(3, na))(*parts)


def _assemble(g4, mine, name, side_by_side, order=None):
    _, nl, r, w = g4.shape
    tr = _divisor_tile(r, 640)

    def body(in_ref, mine_ref, out_ref):
        chip = 2 * lax.axis_index("x") + lax.axis_index("y")
        blocks = [jnp.where(chip == sh, mine_ref[...], in_ref[sh]) for sh in range(N_SHARD)]
        if side_by_side:
            full = jnp.concatenate(blocks, axis=-1)
            out_ref[...] = full if order is None else _take_cols(full, order)
        else:
            for sh in range(N_SHARD):
                out_ref[sh] = blocks[sh]

    if side_by_side:
        out_spec = pl.BlockSpec((None, tr, N_SHARD * w), lambda l, i: (l, i, 0))
        out_shape = jax.ShapeDtypeStruct((nl, r, N_SHARD * w), g4.dtype)
    else:
        out_spec = pl.BlockSpec((None, N_SHARD, tr, w), lambda l, i: (l, 0, i, 0))
        out_shape = jax.ShapeDtypeStruct((nl, N_SHARD, r, w), g4.dtype)
    return _pallas_call(
        body, name=name, grid=(nl, r // tr),
        in_specs=[pl.BlockSpec((N_SHARD, None, tr, w), lambda l, i: (0, l, i, 0)),
                  pl.BlockSpec((None, tr, w), lambda l, i: (l, i, 0))],
        out_specs=out_spec, out_shape=out_shape,
        compiler_params=_params(dimension_semantics=("arbitrary", "arbitrary")),
    )(g4, mine)


def _all_reduce_small(vec, name):
    r, l = vec.shape

    def body(v_ref, out_ref, gath_ref, send_sems, recv_sems):
        x, y, c = _position()
        me = 4 * x + 2 * y + c
        gath_ref[me] = v_ref[...]
        copies = []
        for k in range(1, N_DEV):
            to = (x ^ (k >> 2), y ^ ((k >> 1) & 1), c ^ (k & 1))
            copies.append(_remote(gath_ref.at[me], gath_ref.at[me], send_sems, recv_sems, k - 1, to))
        for cp in copies:
            cp.start()
        for k in range(1, N_DEV):
            frm = 4 * (x ^ (k >> 2)) + 2 * (y ^ ((k >> 1) & 1)) + (c ^ (k & 1))
            _remote(gath_ref.at[frm], gath_ref.at[frm], send_sems, recv_sems, k - 1, (x, y, c)).wait_recv()
        for cp in copies:
            cp.wait_send()
        total = gath_ref[0]
        for d in range(1, N_DEV):
            total = total + gath_ref[d]
        out_ref[...] = total

    vm = pl.BlockSpec(memory_space=pltpu.VMEM)
    return _pallas_call(
        body, name=name, in_specs=[vm], out_specs=vm, out_shape=jax.ShapeDtypeStruct((r, l), F32),
        scratch_shapes=[pltpu.VMEM((N_DEV, r, l), F32), pltpu.SemaphoreType.DMA((N_DEV - 1,)),
                        pltpu.SemaphoreType.DMA((N_DEV - 1,))],
    )(vec)


def _sum_blocks(blocks, out_dtype, name):
    m, w = blocks[0].shape
    tr = _divisor_tile(m, 1024)

    def body(*refs):
        total = refs[0][...].astype(F32)
        for ref in refs[1:-1]:
            total = total + ref[...].astype(F32)
        refs[-1][...] = total.astype(out_dtype)

    blk = pl.BlockSpec((tr, w), lambda i: (i, 0))
    return _pallas_call(
        body, name=name, grid=(m // tr,), in_specs=[blk] * len(blocks), out_specs=blk,
        out_shape=jax.ShapeDtypeStruct((m, w), out_dtype),
        compiler_params=_params(dimension_semantics=("arbitrary",)),
    )(*blocks)


FIRST_GROUPS = (('ffn1_w_gate', 'ffn1_w_up', 'ffn1_w_down'),)
REST_GROUPS = (('ffn2_w_gate', 'ffn2_w_up', 'ffn2_w_down'), ('w_o',), ('w_in',), ('mla_w_q_b',), ('mla_w_kv_b',))
N_FIRST = len(FIRST_GROUPS)
GRAD_SETS = {'first_gu': (('ffn1_w_gate', 'ffn1_w_up'),), 'first_d': (('ffn1_w_down',),), 'rest': REST_GROUPS}
GRAD_SET_ORDER = ('first_gu', 'first_d', 'rest')


def _shard_rows(name):
    shape, axis = BIG[name]
    return shape[0] // N_SHARD if axis == 0 else shape[0]


def _group_row_offsets(group):
    return [int(v) for v in np.cumsum([0] + [_shard_rows(n) for n in group])]


def _rope_tables(s):
    pos = jnp.arange(s, dtype=F32)
    inv = 1.0 / (ROPE_THETA ** (jnp.arange(0, MLA_ROPE, 2, dtype=F32) / MLA_ROPE))
    ang = pos[:, None] * inv[None, :]
    return jnp.tile(jnp.cos(ang), (1, SWA_HEADS)), jnp.tile(jnp.sin(ang), (1, SWA_HEADS))


_MIXER_GAINS = ('mix_norm', 'mla_q_a_norm', 'mla_kv_a_norm', 'mla_q_norm', 'mla_k_norm', 'swa_q_norm', 'swa_k_norm')


def _local_step(x, target, small, ex):
    s = x.shape[0]
    cos, sin = _rope_tables(s)
    row = lambda name, l: small[name][l][None, :]
    saved, bigs = [], []
    for l in range(DEPTH):
        big = ex.first_weights(l)
        sv = {'x0': x}
        x, sv['g1'], sv['u1'], got = _ffn_fwd(x, row('ffn1_norm', l), big['ffn1_w_gate'], big['ffn1_w_up'], big['ffn1_w_down'],
                                             f"ffn1_fwd_{l}", ex.gather_behind_ffn1(l))
        big.update(ex.rest_weights(l, got))
        bigs.append(big)
        sv['x1'] = x
        gains = [row(n, l) for n in _MIXER_GAINS]
        mixer_w = (big['w_in'], big['mla_w_q_b'], big['mla_w_kv_b'])
        q_a, k_a, v_a, q_b, k_b, v_b = _pre_fwd(x, gains, *mixer_w, cos, sin, f"pre_fwd_{l}")
        o_a, lse, got = _mla_fwd(q_a, k_a, v_a, f"mla_fwd_{l}", ex.gather_behind_mla(l))
        ex.gathered_behind_mla(l, got)
        kpad = jnp.pad(k_b, ((0, 0), (BLOCK, 0), (0, 0)))
        vpad = jnp.pad(v_b, ((0, 0), (BLOCK, 0), (0, 0)))
        sinks = jnp.broadcast_to(small['swa_sinks'][l].reshape(SWA_KV_HEADS, SWA_GROUP, 1), (SWA_KV_HEADS, SWA_GROUP, 128))
        o_b, lse_b = _swa_fwd(q_b, kpad, vpad, sinks, f"swa_fwd_{l}")
        sv.update(gains=gains, mixer_w=mixer_w, q_a=q_a, k_a=k_a, v_a=v_a, q_b=q_b, kpad=kpad, vpad=vpad, sinks=sinks,
                  o_a=o_a, lse=lse, o_b=o_b, lse_b=lse_b)
        x = _post_fwd(x, o_a, o_b, row('mla_out_norm', l), row('swa_out_norm', l), big['w_o'], f"post_fwd_{l}")
        sv['x2'] = x
        x, sv['g2'], sv['u2'], _ = _ffn_fwd(x, row('ffn2_norm', l), big['ffn2_w_gate'], big['ffn2_w_up'], big['ffn2_w_down'],
                                           f"ffn2_fwd_{l}")
        saved.append(sv)

    dx, loss = _loss_head(x, target, "loss_head")

    gs = {n: [None] * DEPTH for n in SMALL_NAMES}
    t = _attn_tile(s)
    for l in reversed(range(DEPTH)):
        sv, big = saved[l], bigs[l]

        def ffn_back(tag, xin, dy, gate, up, scatter=None):
            (dxi, dgain, nb, act, dgate, dup), got = _ffn_bwd(xin, dy, gate, up, row(tag + '_norm', l), big[tag + '_w_gate'],
                                                             big[tag + '_w_up'], big[tag + '_w_down'], f"{tag}_bwd_{l}", scatter)
            gs[tag + '_norm'][l] = dgain[0]
            names = (tag + '_w_gate', tag + '_w_up', tag + '_w_down')
            if tag == 'ffn2':
                group, _ = _matmul_tn(dgate, nb, 1.0, f"{tag}_dwg_{l}", None, 0, 3)
                group, _ = _matmul_tn(dup, nb, 1.0, f"{tag}_dwu_{l}", group, 1, 3)
                group, _ = _matmul_tn(act, dy, 0.5, f"{tag}_dwd_{l}", group, 2, 3)
                return dxi, {names: group}, got
            group, _ = _matmul_tn(dgate, nb, 1.0, f"{tag}_dwg_{l}", None, 0, 2)
            group, _ = _matmul_tn(dup, nb, 1.0, f"{tag}_dwu_{l}", group, 1, 2)
            ex.grads_ready(l, 'first_gu', {names[:2]: group})
            down, arrived = _matmul_tn(act, dy, 0.5, f"{tag}_dwd_{l}", None, 0, 1, ex.scatter_behind_dwd(l))
            ex.scattered_behind_dwd(l, arrived)
            ex.grads_ready(l, 'first_d', {names[2:]: down})
            return dxi, None, got

        dx, rest_grads, _ = ffn_back('ffn2', sv['x2'], dx, sv['g2'], sv['u2'])
        do_a, do_b, dga, dgb, dwo, delta = _post_bwd(dx, sv['o_a'], sv['o_b'], row('mla_out_norm', l), row('swa_out_norm', l),
                                              big['w_o'], f"post_bwd_{l}")
        gs['mla_out_norm'][l], gs['swa_out_norm'][l] = dga[0], dgb[0]
        rest_grads['w_o'] = dwo.reshape(N_SHARD, MIX_WIDTH // N_SHARD, D_MODEL)
        dq_a, dk_a, dv_a, got = _mla_bwd(sv['q_a'], sv['k_a'], sv['v_a'], do_a, sv['lse'], delta, f"mla_bwd_{l}",
                                         ex.scatter_behind_mla(l))
        ex.scattered_behind_mla(l, got)
        dq_b, dkpad, dvpad, dsink = _swa_bwd(sv['q_b'], sv['kpad'], sv['vpad'], sv['sinks'], sv['o_b'], sv['lse_b'], do_b,
                                             f"swa_bwd_{l}")
        gs['swa_sinks'][l] = dsink[:, :SWA_GROUP, 0].reshape(SWA_HEADS)
        cts = [dq_a, dk_a, dv_a, dq_b, dkpad[:, BLOCK:], dvpad[:, BLOCK:]]
        outs = _pre_bwd(sv['x1'], dx, cts, sv['gains'], *sv['mixer_w'], cos, sin, f"pre_bwd_{l}")
        dx = outs[0]
        for n, val in zip(_MIXER_GAINS, outs[1:8]):
            gs[n][l] = val[0]
        rest_grads['w_in'], rest_grads['mla_w_q_b'], rest_grads['mla_w_kv_b'] = outs[8:11]
        ex.grads_ready(l, 'rest', rest_grads)
        scatter = ex.scatter_behind_ffn1(l)
        dx, _, got = ffn_back('ffn1', sv['x0'], dx, sv['g1'], sv['u1'], scatter)
        ex.scattered_behind_ffn1(l, got)
    return loss, dx, gs


class _Exchange:
    def __init__(self, weights, c, chip):
        halves_of = lambda a: a.reshape(a.shape[:-2] + (2, a.shape[-2] // 2, a.shape[-1]))
        self.halves_of, self.c, self.chip = halves_of, c, chip
        self.mine = [[halves_of(jnp.concatenate([weights[n][l].astype(BF16) for n in group], axis=0))
                      for group in FIRST_GROUPS + REST_GROUPS] for l in range(DEPTH)]
        self.ahead, self.waiting, self.begun, self.received = {}, {}, {}, {}

    def _assembled(self, l, which, gathered):
        groups, base = (FIRST_GROUPS, 0) if which == 'first' else (REST_GROUPS, N_FIRST)
        big = {}
        for gi, group in enumerate(groups):
            offs = _group_row_offsets(group)
            _, rh, w = self.mine[l][base + gi].shape
            col_sharded = BIG[group[0]][1] == 1
            full = _assemble(gathered[gi].reshape(N_SHARD, 1, 2 * rh, w), self.mine[l][base + gi].reshape(1, 2 * rh, w),
                             f"assemble_{which}{gi}_{l}", col_sharded, MIXER_ORDERS.get(group[0]))
            for i, n in enumerate(group):
                rows = offs[i + 1] - offs[i]
                if len(group) > 1 and offs[i] % rows == 0:
                    big[n] = ((full, (None, rows, N_SHARD * w), (0, offs[i] // rows, 0)) if col_sharded else
                              (full, (None, N_SHARD, rows, w), (0, 0, offs[i] // rows, 0)))
                elif col_sharded:
                    big[n] = full[0, offs[i]:offs[i + 1]]
                else:
                    big[n] = full[0, :, offs[i]:offs[i + 1]].reshape(BIG[n][0])
        return big

    def first_weights(self, l):
        got = self.ahead[l][:N_FIRST] if l in self.ahead else _all_gather_halves(self.mine[l][:N_FIRST], f"gather_first_{l}")
        return self._assembled(l, 'first', got)

    def gather_behind_ffn1(self, l):
        return None if l in self.ahead else self.mine[l][N_FIRST:]

    def rest_weights(self, l, got):
        return self._assembled(l, 'rest', self.ahead[l][N_FIRST:] if l in self.ahead else got)

    def gather_behind_mla(self, l):
        return self.mine[l + 1] if l + 1 < DEPTH else None

    def gathered_behind_mla(self, l, got):
        if got:
            self.ahead[l + 1] = got

    def grads_ready(self, l, which, grads):
        groups = GRAD_SETS[which]
        parts = [self.halves_of(grads[group] if group in grads else jnp.concatenate([grads[n] for n in group], axis=1))
                 for group in groups]
        if l > 0:
            self.waiting[(l, which)] = parts
        else:
            self._begin(l, [(which, parts)])

    def _begin(self, l, sets):
        tag = sets[0][0] if len(sets) == 1 else "all"
        from_sibling = _sibling_exchange(sum((parts for _, parts in sets), []), f"swap_{tag}_{l}", True)
        at = 0
        for which, parts in sets:
            chip_sums = []
            for gi, p in enumerate(parts):
                got = from_sibling[at]
                at += 1
                kept = lax.dynamic_index_in_dim(p, self.c, axis=1, keepdims=False)
                rows = N_SHARD * p.shape[2]
                pair = _sum_blocks([kept.reshape(rows, -1), got.reshape(rows, -1)], BF16, f"sum_pair_{which}{gi}_{l}")
                chip_sums.append(pair.reshape(got.shape))
            self.begun[(l, which)] = chip_sums

    def scatter_behind_mla(self, l):
        if l + 1 >= DEPTH:
            return None
        self._begin(l + 1, [(which, self.waiting.pop((l + 1, which))) for which in GRAD_SET_ORDER])
        return sum((self.begun[(l + 1, which)] for which in GRAD_SET_ORDER), [])

    def scattered_behind_mla(self, l, got):
        at = 0
        for which in GRAD_SET_ORDER if got else ():
            self.received[(l + 1, which)] = got[at:at + len(GRAD_SETS[which])]
            at += len(GRAD_SETS[which])

    def scatter_behind_dwd(self, l):
        return self.begun[(l, 'first_gu')] if l == 0 else None

    def scattered_behind_dwd(self, l, got):
        if got:
            self.received[(l, 'first_gu')] = got

    def scatter_behind_ffn1(self, l):
        return self.begun[(l, 'rest')] if l == 0 else None

    def scattered_behind_ffn1(self, l, got):
        if got:
            self.received[(l, 'rest')] = got

    def reduced(self):
        keys = sorted(self.begun)
        for key in keys:
            if key not in self.received:
                self.received[key] = _scatter_to_chips(self.begun[key], f"scatter_{key[1]}_{key[0]}")
        halves = []
        for l, which in keys:
            for gi, (cs, got) in enumerate(zip(self.begun[(l, which)], self.received[(l, which)])):
                own = lax.dynamic_index_in_dim(cs, self.chip, axis=0, keepdims=False)
                halves.append(_sum_blocks([own, got[0], got[1], got[2]], F32, f"sum_chips_{which}{gi}_{l}"))
        others = _sibling_exchange(halves, "share_halves", False)
        per_layer, at = {}, 0
        for l, which in keys:
            for group in GRAD_SETS[which]:
                mine_h, other_h = halves[at], others[at]
                at += 1
                full = jnp.where(self.c == 0, jnp.concatenate([mine_h, other_h]), jnp.concatenate([other_h, mine_h]))
                offs = _group_row_offsets(group)
                for i, n in enumerate(group):
                    per_layer[(n, l)] = full[offs[i]:offs[i + 1]]
        return {n: jnp.stack([per_layer[(n, l)] for l in range(DEPTH)]) for n in BIG_NAMES}


def kernel(x, ffn1_norm, ffn1_w_gate, ffn1_w_up, ffn1_w_down, mix_norm, w_in, mla_q_a_norm, mla_w_q_b, mla_kv_a_norm, mla_w_kv_b, mla_q_norm, mla_k_norm, swa_q_norm, swa_k_norm, swa_sinks, mla_out_norm, swa_out_norm, w_o, ffn2_norm, ffn2_w_gate, ffn2_w_up, ffn2_w_down, loss_target, m_ffn1_norm, m_ffn1_w_gate, m_ffn1_w_up, m_ffn1_w_down, m_mix_norm, m_w_in, m_mla_q_a_norm, m_mla_w_q_b, m_mla_kv_a_norm, m_mla_w_kv_b, m_mla_q_norm, m_mla_k_norm, m_swa_q_norm, m_swa_k_norm, m_swa_sinks, m_mla_out_norm, m_swa_out_norm, m_w_o, m_ffn2_norm, m_ffn2_w_gate, m_ffn2_w_up, m_ffn2_w_down, v_ffn1_norm, v_ffn1_w_gate, v_ffn1_w_up, v_ffn1_w_down, v_mix_norm, v_w_in, v_mla_q_a_norm, v_mla_w_q_b, v_mla_kv_a_norm, v_mla_w_kv_b, v_mla_q_norm, v_mla_k_norm, v_swa_q_norm, v_swa_k_norm, v_swa_sinks, v_mla_out_norm, v_swa_out_norm, v_w_o, v_ffn2_norm, v_ffn2_w_gate, v_ffn2_w_up, v_ffn2_w_down):
    args = dict(locals())
    transposed = lambda a: jnp.swapaxes(a, 1, 2)
    as_kernels_see = lambda n, a: transposed(a) if n in TRANSPOSED else a
    weights = {n: as_kernels_see(n, args[n]) for n in WEIGHT_NAMES}
    mom_m = {n: as_kernels_see(n, args["m_" + n]) for n in WEIGHT_NAMES}
    mom_v = {n: as_kernels_see(n, args["v_" + n]) for n in WEIGHT_NAMES}
    ex = _Exchange(weights, lax.axis_index("c"), 2 * lax.axis_index("x") + lax.axis_index("y"))
    loss, dx, gs = _local_step(x[0], loss_target[0], {n: weights[n] for n in SMALL_NAMES}, ex)

    small_flat = jnp.concatenate([jnp.stack(gs[n]).reshape(-1) for n in SMALL_NAMES] + [loss[0, :1]])
    n_small = small_flat.shape[0]
    lanes = -(-n_small // (8 * 128)) * 128
    small_sum = _all_reduce_small(jnp.pad(small_flat, (0, 8 * lanes - n_small)).reshape(8, lanes), "reduce_small").reshape(-1)
    loss_out = small_sum[n_small - 1]
    grads = ex.reduced()

    packed = lambda d: jnp.pad(jnp.concatenate([d[n].reshape(-1) for n in SMALL_NAMES]), (0, 8 * lanes - n_small + 1)).reshape(8, lanes)
    small_out = _adamw(packed(weights), small_sum.reshape(8, lanes), packed(mom_m), packed(mom_v), "adamw_small")
    deltas, new_m, new_v = {}, {}, {}
    off = 0
    for n in SMALL_NAMES:
        cnt = int(np.prod(weights[n].shape))
        grads[n], deltas[n], new_m[n], new_v[n] = (a.reshape(-1)[off:off + cnt].reshape(weights[n].shape)
                                                    for a in (small_sum, *small_out))
        off += cnt
    for n in BIG_NAMES:
        shp = weights[n].shape
        two_d = (shp[0] * shp[1], shp[2])
        d, nm, nv = _adamw(weights[n].reshape(two_d), grads[n].reshape(two_d), mom_m[n].reshape(two_d),
                           mom_v[n].reshape(two_d), f"adamw_{n}")
        deltas[n], new_m[n], new_v[n] = d.reshape(shp), nm.reshape(shp), nv.reshape(shp)
    for n in TRANSPOSED:
        grads[n], deltas[n], new_m[n], new_v[n] = (transposed(a) for a in (grads[n], deltas[n], new_m[n], new_v[n]))

    return (loss_out, dx[None], *[grads[n] for n in WEIGHT_NAMES], *[deltas[n] for n in WEIGHT_NAMES],
            *[new_m[n] for n in WEIGHT_NAMES], *[new_v[n] for n in WEIGHT_NAMES])
```

```python
import numpy as np
import jax
import jax.numpy as jnp
from jax import lax
from jax.experimental import pallas as pl
from jax.experimental.pallas import tpu as pltpu

F32 = jnp.float32
BF16 = jnp.bfloat16

D_MODEL = 1024
DEPTH = 2
EPS = 1e-6
ROPE_THETA = 10000.0
BLOCK = 128
MLA_HEADS = 4
MLA_Q_RANK = 256
MLA_KV_RANK = 128
MLA_NOPE = 128
MLA_ROPE = 64
MLA_V = 128
MLA_QK = MLA_NOPE + MLA_ROPE
MLA_WIDTH = MLA_HEADS * MLA_V
SWA_HEADS = 8
SWA_KV_HEADS = 2
SWA_GROUP = SWA_HEADS // SWA_KV_HEADS
SWA_HEAD_DIM = 64
SWA_WIDTH = SWA_HEADS * SWA_HEAD_DIM
MIX_WIDTH = MLA_WIDTH + SWA_WIDTH
IN_SPLITS = (MLA_Q_RANK, MLA_KV_RANK, MLA_ROPE, SWA_WIDTH, SWA_KV_HEADS * SWA_HEAD_DIM, SWA_KV_HEADS * SWA_HEAD_DIM)
IN_COLS = sum(IN_SPLITS)
IN_OFFS = tuple(int(v) for v in np.cumsum((0,) + IN_SPLITS))
D_FF = 2816
MLA_SCALE = MLA_QK ** -0.5
LOG2E = 1.4426950408889634
LN2 = 0.6931471805599453
MLA_QSCALE = MLA_SCALE * LOG2E
SWA_SCALE = SWA_HEAD_DIM ** -0.5
NEG = -1e30

ADAM_LR = 0.001
ADAM_B1 = 0.9
ADAM_B2 = 0.999
ADAM_EPS = 1e-08
ADAM_WD = 0.01
ADAM_STEP = 10

N_SHARD = 4
N_DEV = 8
VMEM_LIMIT = 56 * 1024 * 1024
MESH = pl.DeviceIdType.MESH

WEIGHT_NAMES = ['ffn1_norm', 'ffn1_w_gate', 'ffn1_w_up', 'ffn1_w_down', 'mix_norm', 'w_in', 'mla_q_a_norm', 'mla_w_q_b',
                'mla_kv_a_norm', 'mla_w_kv_b', 'mla_q_norm', 'mla_k_norm', 'swa_q_norm', 'swa_k_norm', 'swa_sinks',
                'mla_out_norm', 'swa_out_norm', 'w_o', 'ffn2_norm', 'ffn2_w_gate', 'ffn2_w_up', 'ffn2_w_down']
TRANSPOSED = ('ffn1_w_gate', 'ffn1_w_up', 'ffn2_w_gate', 'ffn2_w_up')
BIG = {'ffn1_w_gate': ((D_FF, D_MODEL), 0), 'ffn1_w_up': ((D_FF, D_MODEL), 0), 'ffn1_w_down': ((D_FF, D_MODEL), 0),
       'w_in': ((D_MODEL, IN_COLS), 1), 'mla_w_q_b': ((MLA_Q_RANK, MLA_HEADS * MLA_QK), 1),
       'mla_w_kv_b': ((MLA_KV_RANK, MLA_HEADS * (MLA_NOPE + MLA_V)), 1), 'w_o': ((MIX_WIDTH, D_MODEL), 0),
       'ffn2_w_gate': ((D_FF, D_MODEL), 0), 'ffn2_w_up': ((D_FF, D_MODEL), 0), 'ffn2_w_down': ((D_FF, D_MODEL), 0)}
BIG_NAMES = [n for n in WEIGHT_NAMES if n in BIG]
SMALL_NAMES = [n for n in WEIGHT_NAMES if n not in BIG]

_pallas_call = pl.pallas_call


def _params(**kw):
    return pltpu.CompilerParams(vmem_limit_bytes=VMEM_LIMIT, **kw)


def _full(shape):
    n = len(shape)
    return pl.BlockSpec(shape, lambda *_: (0,) * n)


def _resident(shape):
    n = len(shape)
    return pl.BlockSpec(shape, lambda *_: (0,) * n, pipeline_mode=pl.Buffered(1))


@jax.custom_vjp
def _mm(a, w):
    return jnp.dot(a.astype(BF16), w, preferred_element_type=F32)


def _mm_fwd(a, w):
    return _mm(a, w), w


def _mm_bwd(w, dy):
    return lax.dot_general(dy.astype(BF16), w, (((1,), (1,)), ((), ())), preferred_element_type=F32), None


_mm.defvjp(_mm_fwd, _mm_bwd)


def _dot_nt(a, b):
    return lax.dot_general(a, b, (((1,), (1,)), ((), ())), preferred_element_type=F32)


def _dot_tn(a, b):
    return lax.dot_general(a, b, (((0,), (0,)), ((), ())), preferred_element_type=F32)


def _rms(t, g):
    return t * lax.rsqrt(jnp.mean(t * t, axis=-1, keepdims=True) + EPS) * g


def _sigmoid(z):
    return 1.0 / (1.0 + jnp.exp(-z))


def _row_tile(s, want):
    return min(want, s)


def _divisor_tile(rows, cap):
    return max(d for d in range(16, min(rows, cap) + 1, 16) if rows % d == 0)


FF_CHUNK = D_FF


def _weight_operand(w):
    if isinstance(w, tuple):
        arr, block, index = w
        return arr, pl.BlockSpec(block, lambda *_: index, pipeline_mode=pl.Buffered(1))
    return w, _resident(w.shape)


def _weight_rows(ref, start, n):
    if len(ref.shape) == 2:
        return ref[start:start + n, :]
    per = ref.shape[1]
    return ref[start // per:(start + n) // per].reshape(n, ref.shape[2])


def _ffn_fwd(x, g, wg, wu, wd, name, gather=None):
    s = x.shape[0]
    tm = _row_tile(s, 256)
    steps = s // tm
    ng = len(gather) if gather else 0
    (wg, wg_spec), (wu, wu_spec), (wd, wd_spec) = _weight_operand(wg), _weight_operand(wu), _weight_operand(wd)

    def body(x_ref, g_ref, wg_ref, wu_ref, wd_ref, *rest):
        g_ins, (y_ref, gate_ref, up_ref), g_outs, sems = rest[:ng], rest[ng:ng + 3], rest[ng + 3:2 * ng + 3], rest[2 * ng + 3:]
        if ng:
            _run_stages_at(_gather_stages(g_ins, g_outs, *sems), [(0,), (steps * 3 // 8,), (steps * 11 // 16,), (steps - 1,)])
        xv = x_ref[...]
        nb = _rms(xv, g_ref[...]).astype(BF16)
        acc = xv
        for c in range(0, D_FF, FF_CHUNK):
            gate = _dot_nt(nb, _weight_rows(wg_ref, c, FF_CHUNK))
            up = _dot_nt(nb, _weight_rows(wu_ref, c, FF_CHUNK))
            gate_ref[:, c:c + FF_CHUNK] = gate.astype(BF16)
            up_ref[:, c:c + FF_CHUNK] = up.astype(BF16)
            act = (gate * _sigmoid(gate) * up).astype(BF16)
            acc = acc + 0.5 * jnp.dot(act, _weight_rows(wd_ref, c, FF_CHUNK), preferred_element_type=F32)
        y_ref[...] = acc

    outs = _pallas_call(
        body, name=name, grid=(steps,),
        in_specs=[pl.BlockSpec((tm, D_MODEL), lambda i: (i, 0)), _full((1, D_MODEL)), wg_spec, wu_spec, wd_spec] + [_HBM] * ng,
        out_specs=[pl.BlockSpec((tm, D_MODEL), lambda i: (i, 0)), pl.BlockSpec((tm, D_FF), lambda i: (i, 0)),
                   pl.BlockSpec((tm, D_FF), lambda i: (i, 0))] + [_HBM] * ng,
        out_shape=[jax.ShapeDtypeStruct((s, D_MODEL), F32), jax.ShapeDtypeStruct((s, D_FF), BF16),
                   jax.ShapeDtypeStruct((s, D_FF), BF16)] + (_gather_out_shapes(gather) if ng else []),
        scratch_shapes=_exchange_scratch(8, ng) if ng else [],
        compiler_params=_params(dimension_semantics=("arbitrary",)),
    )(x, g, wg, wu, wd, *(gather or []))
    return outs[0], outs[1], outs[2], outs[3:]


def _ffn_bwd(x, dy, gate, up, g, wg, wu, wd, name, scatter=None):
    s = x.shape[0]
    tm = _row_tile(s, 256)
    steps = s // tm
    ng = len(scatter) if scatter else 0
    (wg, wg_spec), (wu, wu_spec), (wd, wd_spec) = _weight_operand(wg), _weight_operand(wu), _weight_operand(wd)

    def body(x_ref, dy_ref, gate_ref, up_ref, g_ref, wg_ref, wu_ref, wd_ref, *rest):
        c_ins, (dx_ref, dgain_ref, n_ref, act_ref, dgate_ref, dup_ref) = rest[:ng], rest[ng:ng + 6]
        c_outs, sems = rest[ng + 6:2 * ng + 6], rest[2 * ng + 6:]
        if ng:
            _run_stages_at(_scatter_stages(c_ins, c_outs, *sems), [(0,), (steps - 1,)])
        i = pl.program_id(0)
        xv = x_ref[...]
        dyv = dy_ref[...]
        gv = g_ref[...]
        r = lax.rsqrt(jnp.mean(xv * xv, axis=-1, keepdims=True) + EPS)
        xh = xv * r
        n_ref[...] = (xh * gv).astype(BF16)
        dyh = (0.5 * dyv).astype(BF16)
        dn = jnp.zeros_like(xv)
        for c in range(0, D_FF, FF_CHUNK):
            dact = _dot_nt(dyh, _weight_rows(wd_ref, c, FF_CHUNK))
            gt = gate_ref[:, c:c + FF_CHUNK].astype(F32)
            u = up_ref[:, c:c + FF_CHUNK].astype(F32)
            sg = _sigmoid(gt)
            sl = gt * sg
            act_ref[:, c:c + FF_CHUNK] = (sl * u).astype(BF16)
            dup = (dact * sl).astype(BF16)
            dgate = (dact * u * (sg * (1.0 + gt * (1.0 - sg)))).astype(BF16)
            dup_ref[:, c:c + FF_CHUNK] = dup
            dgate_ref[:, c:c + FF_CHUNK] = dgate
            dn = (dn + jnp.dot(dgate, _weight_rows(wg_ref, c, FF_CHUNK), preferred_element_type=F32)
                  + jnp.dot(dup, _weight_rows(wu_ref, c, FF_CHUNK), preferred_element_type=F32))
        part = jnp.sum(dn * xh, axis=0, keepdims=True)

        @pl.when(i == 0)
        def _():
            dgain_ref[...] = part

        @pl.when(i > 0)
        def _():
            dgain_ref[...] += part

        dxh = dn * gv
        dx_ref[...] = dyv + r * (dxh - xh * jnp.mean(dxh * xh, axis=-1, keepdims=True))

    row = lambda w: pl.BlockSpec((tm, w), lambda i: (i, 0))
    outs = _pallas_call(
        body, name=name, grid=(steps,),
        in_specs=[row(D_MODEL), row(D_MODEL), row(D_FF), row(D_FF), _full((1, D_MODEL)), wg_spec, wu_spec, wd_spec]
        + [_HBM] * ng,
        out_specs=[row(D_MODEL), _full((1, D_MODEL)), row(D_MODEL), row(D_FF), row(D_FF), row(D_FF)] + [_HBM] * ng,
        out_shape=[jax.ShapeDtypeStruct((s, D_MODEL), F32), jax.ShapeDtypeStruct((1, D_MODEL), F32),
                   jax.ShapeDtypeStruct((s, D_MODEL), BF16), jax.ShapeDtypeStruct((s, D_FF), BF16),
                   jax.ShapeDtypeStruct((s, D_FF), BF16), jax.ShapeDtypeStruct((s, D_FF), BF16)]
        + (_scatter_out_shapes(scatter) if ng else []),
        scratch_shapes=_exchange_scratch(3, ng) if ng else [],
        compiler_params=_params(dimension_semantics=("arbitrary",)),
    )(x, dy, gate, up, g, wg, wu, wd, *(scatter or []))
    return outs[:6], outs[6:]


def _store_col_shards(o_ref, acc, first_shard, n_here, width):
    for q in range(n_here):
        o_ref[q] = acc[:, (first_shard + q) * width:(first_shard + q + 1) * width].astype(BF16)


def _matmul_tn(a, b, scale, name, group=None, slot=0, slots=1, scatter=None):
    t, m = a.shape
    n = b.shape[1]
    tk = _row_tile(t, 2048)
    tn = n // 2
    nk = t // tk
    per = m // N_SHARD
    earlier = [] if group is None else [group]
    ne, ng = len(earlier), len(scatter) if scatter else 0

    def body(a_ref, b_ref, *rest):
        c_ins, o_ref, c_outs = rest[ne:ne + ng], rest[ne + ng], rest[ne + ng + 1:ne + 2 * ng + 1]
        acc_ref, sems = rest[ne + 2 * ng + 1], rest[ne + 2 * ng + 2:]
        if ng:
            _run_stages_at(_scatter_stages(c_ins, c_outs, *sems), [(0, 0), (n // tn - 1, nk - 1)])
        k = pl.program_id(1)
        bv = b_ref[...]
        if scale != 1.0:
            bv = bv.astype(F32) * scale
        part = _dot_tn(a_ref[...].astype(BF16), bv.astype(BF16))

        @pl.when(k == 0)
        def _():
            acc_ref[...] = part

        @pl.when(k > 0)
        def _():
            acc_ref[...] += part

        @pl.when(k == nk - 1)
        def _():
            for sh in range(N_SHARD):
                o_ref[sh] = acc_ref[sh * per:(sh + 1) * per, :].astype(BF16)

    outs = _pallas_call(
        body, name=name, grid=(n // tn, nk),
        in_specs=[pl.BlockSpec((tk, m), lambda j, k: (k, 0)), pl.BlockSpec((tk, tn), lambda j, k: (k, j))]
        + [pl.BlockSpec(memory_space=pl.ANY)] * ne + [_HBM] * ng,
        out_specs=[pl.BlockSpec((N_SHARD, per, tn), lambda j, k: (0, slot, j))] + [_HBM] * ng,
        out_shape=[jax.ShapeDtypeStruct((N_SHARD, slots * per, n), BF16)] + (_scatter_out_shapes(scatter) if ng else []),
        scratch_shapes=[pltpu.VMEM((m, tn), F32)] + (_exchange_scratch(3, ng) if ng else []),
        input_output_aliases={2: 0} if earlier else {},
        compiler_params=_params(dimension_semantics=("arbitrary", "arbitrary")),
    )(a, b, *earlier, *(scatter or []))
    return outs[0], outs[1:]


_HALF = SWA_HEAD_DIM // 2
_IN_ORDER = (list(range(0, IN_OFFS[2]))
             + [IN_OFFS[3] + SWA_HEAD_DIM * h + i for h in range(SWA_HEADS) for i in range(_HALF)]
             + [IN_OFFS[3] + SWA_HEAD_DIM * h + _HALF + i for h in range(SWA_HEADS) for i in range(_HALF)]
             + list(range(IN_OFFS[5], IN_OFFS[6]))
             + [IN_OFFS[4] + SWA_HEAD_DIM * j + i for j in range(SWA_KV_HEADS) for i in range(_HALF)]
             + [IN_OFFS[4] + SWA_HEAD_DIM * j + _HALF + i for j in range(SWA_KV_HEADS) for i in range(_HALF)]
             + list(range(IN_OFFS[2], IN_OFFS[3])))
_QB_ORDER = ([MLA_QK * h + i for h in range(MLA_HEADS) for i in range(MLA_NOPE)]
             + [MLA_QK * h + MLA_NOPE + i for h in range(MLA_HEADS) for i in range(_HALF)]
             + [MLA_QK * h + MLA_NOPE + _HALF + i for h in range(MLA_HEADS) for i in range(_HALF)])
_KVB_ORDER = ([(MLA_NOPE + MLA_V) * h + i for h in range(MLA_HEADS) for i in range(MLA_NOPE)]
              + [(MLA_NOPE + MLA_V) * h + MLA_NOPE + i for h in range(MLA_HEADS) for i in range(MLA_V)])
MIXER_ORDERS = {'w_in': _IN_ORDER, 'mla_w_q_b': _QB_ORDER, 'mla_w_kv_b': _KVB_ORDER}
_P_CQ, _P_CKV, _P_QA, _P_QB, _P_VS, _P_KA, _P_KB, _P_PE = (int(v) for v in np.cumsum(
    (0, MLA_Q_RANK, MLA_KV_RANK, SWA_WIDTH // 2, SWA_WIDTH // 2, IN_SPLITS[5], IN_SPLITS[4] // 2, IN_SPLITS[4] // 2)))


def _runs(order):
    out, start = [], 0
    for i in range(1, len(order) + 1):
        if i == len(order) or order[i] != order[i - 1] + 1:
            out.append((order[start], i - start))
            start = i
    return out


def _inverse(order):
    inv = [0] * len(order)
    for new, old in enumerate(order):
        inv[old] = new
    return inv


def _take_cols(a, order):
    return jnp.concatenate([a[..., st:st + w] for st, w in _runs(order)], axis=-1)


def _segment_matrix(n, seg):
    return (lax.broadcasted_iota(jnp.int32, (n, n), 0) // seg == lax.broadcasted_iota(jnp.int32, (n, n), 1) // seg).astype(BF16)


def _cmm2(t, b):
    hi = t.astype(BF16)
    lo = (t - hi.astype(F32)).astype(BF16)
    return jnp.dot(hi, b, preferred_element_type=F32) + jnp.dot(lo, b, preferred_element_type=F32)


@jax.custom_vjp
def _cmm(t, b, bt):
    return jnp.dot(t.astype(BF16), b, preferred_element_type=F32)


def _cmm_fwd(t, b, bt):
    return _cmm(t, b, bt), (b, bt)


def _cmm_bwd(res, dy):
    b, bt = res
    return _cmm(dy, bt, b), None, None


_cmm.defvjp(_cmm_fwd, _cmm_bwd)


def _segsum(t, b):
    return _cmm(t, b, b)


def _rowsum(t, exact=False):
    n = t.shape[-1]
    if exact:
        return _cmm2(t, jnp.ones((n, 128), BF16))
    return _cmm(t, jnp.ones((n, 128), BF16), jnp.ones((128, n), BF16))


def _by_head(vals, width):
    lane = lax.broadcasted_iota(jnp.int32, (vals[0].shape[0], len(vals) * width), 1)
    out = vals[-1]
    for hd in range(len(vals) - 2, -1, -1):
        out = jnp.where(lane < (hd + 1) * width, vals[hd], out)
    return out


def _rope2(a, b, cos, sin):
    return a * cos - b * sin, b * cos + a * sin


def _pre_math(x, gm, gqa, gkva, gq, gk, gsq, gsk, taps, win, wqb, wkvb, cos, sin):
    h = _rms(x, gm)
    proj = _mm(h, win)
    if taps is not None:
        proj = proj + taps[0]
    cqn = _rms(proj[:, _P_CQ:_P_CKV], gqa)
    qa_all = _mm(cqn, wqb)
    ckvn = _rms(proj[:, _P_CKV:_P_QA], gkva)
    kv_all = _mm(ckvn, wkvb)
    if taps is not None:
        qa_all = qa_all + taps[1]
        kv_all = kv_all + taps[2]
    nh, hw = MLA_HEADS, MLA_HEADS * _HALF
    seg_mla = _segment_matrix(hw, _HALF)
    tile = lambda g, n: jnp.concatenate([g] * n, axis=-1)
    c4, s4 = cos[:, :hw], sin[:, :hw]

    def mla_heads(nope, r1, r2, gain):
        rr = r1 * r1 + r2 * r2
        lane_head = lax.broadcasted_iota(jnp.int32, (hw, nh * MLA_NOPE), 0) // _HALF
        spread = (lane_head == lax.broadcasted_iota(jnp.int32, (hw, nh * MLA_NOPE), 1) // MLA_NOPE).astype(BF16)
        rope_on_nope = _cmm(rr, spread, spread.T)
        ss_nope = [_rowsum(jnp.square(nope[:, hd * MLA_NOPE:(hd + 1) * MLA_NOPE])) for hd in range(nh)]
        rinv = [lax.rsqrt((ss_nope[hd] + rope_on_nope[:, hd * MLA_NOPE:(hd + 1) * MLA_NOPE]) * (1.0 / MLA_QK) + EPS)
                for hd in range(nh)]
        rl = lax.rsqrt((_segsum(rr, seg_mla) + _by_head(ss_nope, _HALF)) * (1.0 / MLA_QK) + EPS)
        o1, o2 = _rope2(r1 * rl * tile(gain[:, MLA_NOPE:MLA_NOPE + _HALF], nh), r2 * rl * tile(gain[:, MLA_NOPE + _HALF:], nh), c4, s4)
        return [jnp.concatenate([nope[:, hd * MLA_NOPE:(hd + 1) * MLA_NOPE] * rinv[hd] * gain[:, :MLA_NOPE],
                                 o1[:, hd * _HALF:(hd + 1) * _HALF], o2[:, hd * _HALF:(hd + 1) * _HALF]], axis=-1)
                for hd in range(nh)]

    q_a = mla_heads(qa_all[:, :nh * MLA_NOPE], qa_all[:, nh * MLA_NOPE:nh * MLA_NOPE + hw], qa_all[:, nh * MLA_NOPE + hw:], gq)
    pe1, pe2 = proj[:, _P_PE:_P_PE + _HALF], proj[:, _P_PE + _HALF:_P_PE + 2 * _HALF]
    k_a = mla_heads(kv_all[:, :nh * MLA_NOPE], tile(pe1, nh), tile(pe2, nh), gk)
    v_a = [kv_all[:, nh * MLA_NOPE + hd * MLA_V:nh * MLA_NOPE + (hd + 1) * MLA_V] for hd in range(nh)]

    def swa_heads(a, b, gain, n):
        w = n * _HALF
        r = lax.rsqrt(_segsum(a * a + b * b, _segment_matrix(w, _HALF)) * (1.0 / SWA_HEAD_DIM) + EPS)
        o1, o2 = _rope2(a * r * tile(gain[:, :_HALF], n), b * r * tile(gain[:, _HALF:], n), cos[:, :w], sin[:, :w])
        return [jnp.concatenate([o1[:, hd * _HALF:(hd + 1) * _HALF], o2[:, hd * _HALF:(hd + 1) * _HALF]], axis=-1) for hd in range(n)]

    q_b = swa_heads(proj[:, _P_QA:_P_QB], proj[:, _P_QB:_P_VS], gsq, SWA_HEADS)
    k_b = swa_heads(proj[:, _P_KA:_P_KB], proj[:, _P_KB:_P_PE], gsk, SWA_KV_HEADS)
    v_b = [proj[:, _P_VS + j * SWA_HEAD_DIM:_P_VS + (j + 1) * SWA_HEAD_DIM] for j in range(SWA_KV_HEADS)]
    return (q_a, k_a, v_a, q_b, k_b, v_b), (h, cqn, ckvn)


_PRE_GAIN_WIDTHS = (D_MODEL, MLA_Q_RANK, MLA_KV_RANK, MLA_QK, MLA_QK, SWA_HEAD_DIM, SWA_HEAD_DIM)
_PRE_HEADS = ((MLA_HEADS, MLA_QK), (MLA_HEADS, MLA_QK), (MLA_HEADS, MLA_V),
              (SWA_HEADS, SWA_HEAD_DIM), (SWA_KV_HEADS, SWA_HEAD_DIM), (SWA_KV_HEADS, SWA_HEAD_DIM))


def _pre_fwd(x, gains, win, wqb, wkvb, cos, sin, name):
    s = x.shape[0]
    tm = _row_tile(s, 512)

    def body(x_ref, *refs):
        g_refs, (win_ref, wqb_ref, wkvb_ref, cos_ref, sin_ref), out_refs = refs[:7], refs[7:12], refs[12:]
        outs, _ = _pre_math(x_ref[...], *[g[...] for g in g_refs], None, win_ref[...], wqb_ref[...], wkvb_ref[...],
                            cos_ref[...], sin_ref[...])
        for idx, (ref, heads) in enumerate(zip(out_refs, outs)):
            for hd, val in enumerate(heads):
                ref[hd] = (val * MLA_QSCALE if idx == 0 else val).astype(BF16)

    heads_spec = lambda nh, w: pl.BlockSpec((nh, tm, w), lambda i: (0, i, 0))
    return _pallas_call(
        body, name=name, grid=(s // tm,),
        in_specs=[pl.BlockSpec((tm, D_MODEL), lambda i: (i, 0))] + [_full((1, w)) for w in _PRE_GAIN_WIDTHS]
        + [_resident(win.shape), _resident(wqb.shape), _resident(wkvb.shape),
           pl.BlockSpec((tm, SWA_HEADS * _HALF), lambda i: (i, 0)), pl.BlockSpec((tm, SWA_HEADS * _HALF), lambda i: (i, 0))],
        out_specs=[heads_spec(nh, w) for nh, w in _PRE_HEADS],
        out_shape=[jax.ShapeDtypeStruct((nh, s, w), BF16) for nh, w in _PRE_HEADS],
        compiler_params=_params(dimension_semantics=("arbitrary",)),
    )(x, *gains, win, wqb, wkvb, cos, sin)


def _pre_bwd(x, dx_res, cts, gains, win, wqb, wkvb, cos, sin, name):
    s = x.shape[0]
    tm = _row_tile(s, 256)
    tap_widths = (IN_COLS, MLA_HEADS * MLA_QK, MLA_HEADS * (MLA_NOPE + MLA_V))

    def body(x_ref, dxr_ref, *refs):
        ct_refs, g_refs = refs[:6], refs[6:13]
        win_ref, wqb_ref, wkvb_ref, cos_ref, sin_ref = refs[13:18]
        dx_ref, dg_refs, dw_refs, acc_refs = refs[18], refs[19:26], refs[26:29], refs[29:32]
        i = pl.program_id(0)
        win_v, wqb_v, wkvb_v, cos_v, sin_v = win_ref[...], wqb_ref[...], wkvb_ref[...], cos_ref[...], sin_ref[...]

        def f(xv, gm, gqa, gkva, gq, gk, gsq, gsk, t0, t1, t2):
            return _pre_math(xv, gm, gqa, gkva, gq, gk, gsq, gsk, (t0, t1, t2), win_v, wqb_v, wkvb_v, cos_v, sin_v)

        taps = [jnp.zeros((tm, w), F32) for w in tap_widths]
        _, vjp, acts = jax.vjp(f, x_ref[...], *[g[...] for g in g_refs], *taps, has_aux=True)
        ct = tuple([ref[hd] for hd in range(nh)] for ref, (nh, _) in zip(ct_refs, _PRE_HEADS))
        grads = vjp(ct)
        dx_ref[...] = grads[0] + dxr_ref[...]
        dws = [_dot_tn(a.astype(BF16), t.astype(BF16)) for a, t in zip(acts, grads[8:11])]

        @pl.when(i == 0)
        def _():
            for ref, val in zip(dg_refs, grads[1:8]):
                ref[...] = val
            for ref, val in zip(acc_refs, dws):
                ref[...] = val

        @pl.when(i > 0)
        def _():
            for ref, val in zip(dg_refs, grads[1:8]):
                ref[...] += val
            for ref, val in zip(acc_refs, dws):
                ref[...] += val

        @pl.when(i == s // tm - 1)
        def _():
            for ref, acc, order in zip(dw_refs, acc_refs, (_IN_ORDER, _QB_ORDER, _KVB_ORDER)):
                _store_col_shards(ref, _take_cols(acc[...], _inverse(order)), 0, N_SHARD, acc.shape[1] // N_SHARD)

    heads_spec = lambda nh, w: pl.BlockSpec((nh, tm, w), lambda i: (0, i, 0))
    row = pl.BlockSpec((tm, D_MODEL), lambda i: (i, 0))
    half = pl.BlockSpec((tm, SWA_HEADS * _HALF), lambda i: (i, 0))
    shard_shapes = [(N_SHARD, w.shape[0], w.shape[1] // N_SHARD) for w in (win, wqb, wkvb)]
    return _pallas_call(
        body, name=name, grid=(s // tm,),
        in_specs=[row, row] + [heads_spec(nh, w) for nh, w in _PRE_HEADS] + [_full((1, w)) for w in _PRE_GAIN_WIDTHS]
        + [_resident(win.shape), _resident(wqb.shape), _resident(wkvb.shape), half, half],
        out_specs=[row] + [_full((1, w)) for w in _PRE_GAIN_WIDTHS] + [_full(shp) for shp in shard_shapes],
        out_shape=[jax.ShapeDtypeStruct((s, D_MODEL), F32)] + [jax.ShapeDtypeStruct((1, w), F32) for w in _PRE_GAIN_WIDTHS]
        + [jax.ShapeDtypeStruct(shp, BF16) for shp in shard_shapes],
        scratch_shapes=[pltpu.VMEM(w.shape, F32) for w in (win, wqb, wkvb)],
        compiler_params=_params(dimension_semantics=("arbitrary",)),
    )(x, dx_res, *cts, *gains, win, wqb, wkvb, cos, sin)


def _post_math(oa, ob, ga, gb, wo):
    mixed = jnp.concatenate([_rms(jnp.concatenate(oa, axis=-1), ga), _rms(jnp.concatenate(ob, axis=-1), gb)], axis=-1)
    return _mm(mixed, wo), mixed


def _post_fwd(x, oa, ob, ga, gb, wo, name):
    s = x.shape[0]
    tm = _row_tile(s, 512)

    def body(x_ref, oa_ref, ob_ref, ga_ref, gb_ref, wo_ref, y_ref):
        y, _ = _post_math([oa_ref[hd] for hd in range(MLA_HEADS)], [ob_ref[hd] for hd in range(SWA_HEADS)],
                          ga_ref[...], gb_ref[...], wo_ref[...])
        y_ref[...] = x_ref[...] + y

    row = pl.BlockSpec((tm, D_MODEL), lambda i: (i, 0))
    return _pallas_call(
        body, name=name, grid=(s // tm,),
        in_specs=[row, pl.BlockSpec((MLA_HEADS, tm, MLA_V), lambda i: (0, i, 0)),
                  pl.BlockSpec((SWA_HEADS, tm, SWA_HEAD_DIM), lambda i: (0, i, 0)),
                  _full((1, MLA_WIDTH)), _full((1, SWA_WIDTH)), _resident(wo.shape)],
        out_specs=row, out_shape=jax.ShapeDtypeStruct((s, D_MODEL), F32),
        compiler_params=_params(dimension_semantics=("arbitrary",)),
    )(x, oa, ob, ga, gb, wo)


def _post_bwd(dy, oa, ob, ga, gb, wo, name):
    s = dy.shape[0]
    tm = _row_tile(s, 512)
    t = _attn_tile(s)

    def body(dy_ref, oa_ref, ob_ref, ga_ref, gb_ref, wo_ref, doa_ref, dob_ref, dga_ref, dgb_ref, dwo_ref, delta_ref, acc_ref):
        i = pl.program_id(0)
        wo_v = wo_ref[...]
        dyv = dy_ref[...]

        def f(oa_l, ob_l, ga_v, gb_v):
            return _post_math(oa_l, ob_l, ga_v, gb_v, wo_v)

        _, vjp, mixed = jax.vjp(f, [oa_ref[hd] for hd in range(MLA_HEADS)], [ob_ref[hd] for hd in range(SWA_HEADS)],
                                ga_ref[...], gb_ref[...], has_aux=True)
        doa, dob, dga, dgb = vjp(dyv)
        for hd in range(MLA_HEADS):
            doa_ref[hd] = doa[hd]
            rows = _as_rows(_rowsum(doa[hd] * oa_ref[hd], exact=True))
            for j in range(tm // t):
                delta_ref[hd, j] = rows[:, j * t:(j + 1) * t]
        for hd in range(SWA_HEADS):
            dob_ref[hd] = dob[hd]
        dwo = _dot_tn(mixed.astype(BF16), dyv.astype(BF16))

        @pl.when(i == 0)
        def _():
            dga_ref[...] = dga
            dgb_ref[...] = dgb
            acc_ref[...] = dwo

        @pl.when(i > 0)
        def _():
            dga_ref[...] += dga
            dgb_ref[...] += dgb
            acc_ref[...] += dwo

        @pl.when(i == s // tm - 1)
        def _():
            dwo_ref[...] = acc_ref[...].astype(BF16)

    row = pl.BlockSpec((tm, D_MODEL), lambda i: (i, 0))
    oa_spec = pl.BlockSpec((MLA_HEADS, tm, MLA_V), lambda i: (0, i, 0))
    ob_spec = pl.BlockSpec((SWA_HEADS, tm, SWA_HEAD_DIM), lambda i: (0, i, 0))
    return _pallas_call(
        body, name=name, grid=(s // tm,),
        in_specs=[row, oa_spec, ob_spec, _full((1, MLA_WIDTH)), _full((1, SWA_WIDTH)), _resident(wo.shape)],
        out_specs=[oa_spec, ob_spec, _full((1, MLA_WIDTH)), _full((1, SWA_WIDTH)), _full(wo.shape),
                   pl.BlockSpec((MLA_HEADS, tm // t, 8, t), lambda i: (0, i, 0, 0))],
        out_shape=[jax.ShapeDtypeStruct((MLA_HEADS, s, MLA_V), F32), jax.ShapeDtypeStruct((SWA_HEADS, s, SWA_HEAD_DIM), F32),
                   jax.ShapeDtypeStruct((1, MLA_WIDTH), F32), jax.ShapeDtypeStruct((1, SWA_WIDTH), F32),
                   jax.ShapeDtypeStruct(wo.shape, BF16), jax.ShapeDtypeStruct((MLA_HEADS, s // t, 8, t), F32)],
        scratch_shapes=[pltpu.VMEM(wo.shape, F32)],
        compiler_params=_params(dimension_semantics=("arbitrary",)),
    )(dy, oa, ob, ga, gb, wo)


def _attn_tile(s):
    return 512 if s >= 2048 else 128


def _as_rows(cols):
    return cols.T[0:8, :]


def _causal_mask(t):
    return lax.broadcasted_iota(jnp.int32, (t, t), 1) <= lax.broadcasted_iota(jnp.int32, (t, t), 0)


def _pipelined_blocks(first, count, last_block, issue, consume, carry, prefetch_after):
    def clamped(j, slot):
        issue(jnp.minimum(j, last_block), slot)

    def pair(jj, c):
        a = first + 2 * jj
        clamped(a + 1, 1)
        c = consume(a, 0, c)
        clamped(a + 2, 0)
        return consume(a + 1, 1, c)

    clamped(first, 0)
    npairs = count // 2
    carry = lax.fori_loop(0, npairs, pair, carry)

    def odd(c):
        c = consume(first + 2 * npairs, 0, c)
        if prefetch_after:
            clamped(first + count, 0)
        return c

    return lax.cond(count - 2 * npairs == 1, odd, lambda c: c, carry)


def _run_stages_at(stages, steps):
    for stage, step in zip(stages, steps):
        here = pl.program_id(0) == step[0]
        for axis in range(1, len(step)):
            here = here & (pl.program_id(axis) == step[axis])
        pl.when(here)(stage)


def _mla_fwd(q, k, v, name, gather=None):
    nh, s, _ = q.shape
    t = _attn_tile(s)
    nq = s // t
    ng = len(gather) if gather else 0

    def body(q_ref, k_ref, v_ref, *rest):
        g_ins, (o_ref, lse_ref), g_outs = rest[:ng], rest[ng:ng + 2], rest[ng + 2:2 * ng + 2]
        (s0_ref, s1_ref), sems = rest[2 * ng + 2:2 * ng + 4], rest[2 * ng + 4:]
        if ng:
            _run_stages_at(_gather_stages(g_ins, g_outs, *sems), [(0, 0), (nh // 2, 0), (nh - 1, 0), (nh - 1, nq - 1)])
        qi = pl.program_id(1)
        qv = q_ref[...]
        s_refs = (s0_ref, s1_ref)

        def rows(j):
            return pl.ds(pl.multiple_of(j * t, t), t)

        def issue(j, slot):
            s_refs[slot][...] = _dot_nt(qv, k_ref[rows(j), :])

        ones = jnp.ones((t, MLA_V), BF16)

        def consume(j, slot, carry, masked=False):
            m, acc = carry
            sc = s_refs[slot][...]
            if masked:
                sc = jnp.where(_causal_mask(t), sc, NEG)
            m_new = jnp.maximum(m, jnp.max(sc, axis=-1, keepdims=True))
            p = jnp.exp2(sc - m_new)
            v_ones = jnp.concatenate([v_ref[rows(j), :], ones], axis=-1)
            acc = jnp.exp2(m - m_new) * acc + jnp.dot(p.astype(BF16), v_ones, preferred_element_type=F32)
            return m_new, acc

        init = (jnp.full((t, 1), NEG, F32), jnp.zeros((t, 2 * MLA_V), F32))
        carry = _pipelined_blocks(0, qi, nq - 1, issue, consume, init, True)
        m, acc = consume(qi, 0, carry, masked=True)
        l = acc[:, MLA_V:]
        o_ref[...] = acc[:, :MLA_V] / l
        lse_ref[...] = _as_rows(m + jnp.log2(l))

    outs = _pallas_call(
        body, name=name, grid=(nh, nq),
        in_specs=[pl.BlockSpec((None, t, MLA_QK), lambda h, i: (h, i, 0)), pl.BlockSpec((None, s, MLA_QK), lambda h, i: (h, 0, 0)),
                  pl.BlockSpec((None, s, MLA_V), lambda h, i: (h, 0, 0))] + [_HBM] * ng,
        out_specs=[pl.BlockSpec((None, t, MLA_V), lambda h, i: (h, i, 0)), pl.BlockSpec((None, None, 8, t), lambda h, i: (h, i, 0, 0))]
        + [_HBM] * ng,
        out_shape=[jax.ShapeDtypeStruct((nh, s, MLA_V), F32), jax.ShapeDtypeStruct((nh, nq, 8, t), F32)]
        + (_gather_out_shapes(gather) if ng else []),
        scratch_shapes=[pltpu.VMEM((t, t), F32)] * 2 + (_exchange_scratch(8, ng) if ng else []),
        compiler_params=_params(dimension_semantics=("arbitrary", "arbitrary")),
    )(q, k, v, *(gather or []))
    return outs[0], outs[1], outs[2:]


def _mla_bwd(q, k, v, do, lse_row, delta_row, name, scatter=None):
    nh, s, _ = q.shape
    t = _attn_tile(s)
    nq = s // t
    ng = len(scatter) if scatter else 0

    def body(q_ref, k_ref, v_ref, do_ref, lse_ref, delta_ref, *rest):
        c_ins, (dq_ref, dk_ref, dv_ref), c_outs = rest[:ng], rest[ng:ng + 3], rest[ng + 3:2 * ng + 3]
        (s0_ref, s1_ref, dp0_ref, dp1_ref), sems = rest[2 * ng + 3:2 * ng + 7], rest[2 * ng + 7:]
        if ng:
            _run_stages_at(_scatter_stages(c_ins, c_outs, *sems), [(0, 0), (nh - 1, nq - 1)])
        kj = pl.program_id(1)
        kv_, vv = k_ref[...], v_ref[...]
        s_refs, dp_refs = (s0_ref, s1_ref), (dp0_ref, dp1_ref)

        @pl.when(kj == 0)
        def _():
            dq_ref[...] = jnp.zeros_like(dq_ref)

        def rows(i):
            return pl.ds(pl.multiple_of(i * t, t), t)

        def issue(i, slot):
            s_refs[slot][...] = _dot_nt(kv_, q_ref[rows(i), :])
            dp_refs[slot][...] = _dot_nt(vv, do_ref[rows(i), :].astype(BF16))

        def consume(i, slot, carry, masked=False):
            dk, dv = carry
            p = jnp.exp2(s_refs[slot][...] - lse_ref[i][0:1, :])
            if masked:
                p = jnp.where(lax.broadcasted_iota(jnp.int32, (t, t), 0) <= lax.broadcasted_iota(jnp.int32, (t, t), 1), p, 0.0)
            dv = dv + jnp.dot(p.astype(BF16), do_ref[rows(i), :].astype(BF16), preferred_element_type=F32)
            ds = (p * (dp_refs[slot][...] - delta_ref[i][0:1, :])).astype(BF16)
            dk = dk + jnp.dot(ds, q_ref[rows(i), :], preferred_element_type=F32)
            dq_ref[rows(i), :] += _dot_tn(ds, kv_) * MLA_SCALE
            return dk, dv

        issue(kj, 0)
        carry = consume(kj, 0, (jnp.zeros((t, MLA_QK), F32), jnp.zeros((t, MLA_V), F32)), masked=True)
        dk, dv = _pipelined_blocks(kj + 1, nq - 1 - kj, nq - 1, issue, consume, carry, False)
        dk_ref[...] = dk * LN2
        dv_ref[...] = dv

    tile = lambda w: pl.BlockSpec((None, t, w), lambda h, j: (h, j, 0))
    whole = lambda w: pl.BlockSpec((None, s, w), lambda h, j: (h, 0, 0))
    rows_spec = pl.BlockSpec((None, nq, 8, t), lambda h, j: (h, 0, 0, 0))
    outs = _pallas_call(
        body, name=name, grid=(nh, nq),
        in_specs=[whole(MLA_QK), tile(MLA_QK), tile(MLA_V), whole(MLA_V), rows_spec, rows_spec] + [_HBM] * ng,
        out_specs=[whole(MLA_QK), tile(MLA_QK), tile(MLA_V)] + [_HBM] * ng,
        out_shape=[jax.ShapeDtypeStruct((nh, s, MLA_QK), F32), jax.ShapeDtypeStruct((nh, s, MLA_QK), F32),
                   jax.ShapeDtypeStruct((nh, s, MLA_V), F32)] + (_scatter_out_shapes(scatter) if ng else []),
        scratch_shapes=[pltpu.VMEM((t, t), F32)] * 4 + (_exchange_scratch(3, ng) if ng else []),
        compiler_params=_params(dimension_semantics=("arbitrary", "arbitrary")),
    )(q, k, v, do, lse_row, delta_row, *(scatter or []))
    return outs[0], outs[1], outs[2], outs[3:]


def _swa_tile(s):
    return min(s, 8 * BLOCK)


def _swa_specs(tq):
    nb = tq // BLOCK
    grp = lambda w: pl.BlockSpec((SWA_GROUP, tq, w), lambda j, i: (j, i, 0))
    main = pl.BlockSpec((None, tq, SWA_HEAD_DIM), lambda j, i: (j, i, 0))
    tail = pl.BlockSpec((None, BLOCK, SWA_HEAD_DIM), lambda j, i: (j, nb * (i + 1), 0))
    sink = pl.BlockSpec((None, SWA_GROUP, 128), lambda j, i: (j, 0, 0))
    return grp, main, tail, sink


def _swa_band_mask(first):
    shape = (SWA_GROUP * BLOCK, 2 * BLOCK)
    q_rel = (lax.broadcasted_iota(jnp.int32, shape, 0) & (BLOCK - 1)) + BLOCK
    k_rel = lax.broadcasted_iota(jnp.int32, shape, 1)
    dist = q_rel - k_rel
    return (dist >= 0) & (dist < BLOCK) & ((k_rel >= BLOCK) | jnp.logical_not(first))


def _swa_sink_column(sink_ref):
    sk = sink_ref[...]
    return jnp.concatenate([jnp.broadcast_to(sk[g:g + 1, 0:1], (BLOCK, 1)) for g in range(SWA_GROUP)], axis=0)


def _swa_fwd(q, kpad, vpad, sinks, name):
    _, s, _ = q.shape
    tq = _swa_tile(s)
    grp, main, tail, sink = _swa_specs(tq)
    d = SWA_HEAD_DIM

    def body(q_ref, km_ref, kt_ref, vm_ref, vt_ref, sink_ref, o_ref, lse_ref):
        i = pl.program_id(1)
        kall = jnp.concatenate([km_ref[...], kt_ref[...]], axis=0)
        vall = jnp.concatenate([vm_ref[...], vt_ref[...]], axis=0)
        sink_col = _swa_sink_column(sink_ref)
        for b in range(tq // BLOCK):
            lo = b * BLOCK
            valid = _swa_band_mask(i == 0 if b == 0 else False)
            q4 = q_ref[:, lo:lo + BLOCK, :].reshape(SWA_GROUP * BLOCK, d)
            sc = jnp.where(valid, _dot_nt(q4, kall[lo:lo + 2 * BLOCK]) * SWA_SCALE, NEG)
            m = jnp.maximum(jnp.max(sc, axis=-1, keepdims=True), sink_col)
            e = jnp.exp(sc - m)
            den = jnp.sum(e, axis=-1, keepdims=True) + jnp.exp(sink_col - m)
            out = jnp.dot((e * (1.0 / den)).astype(BF16), vall[lo:lo + 2 * BLOCK], preferred_element_type=F32)
            o_ref[:, lo:lo + BLOCK, :] = out.reshape(SWA_GROUP, BLOCK, d)
            lse_ref[:, lo:lo + BLOCK, :] = (m + jnp.log(den)).reshape(SWA_GROUP, BLOCK, 1)

    return _pallas_call(
        body, name=name, grid=(SWA_KV_HEADS, s // tq),
        in_specs=[grp(d), main, tail, main, tail, sink], out_specs=[grp(d), grp(1)],
        out_shape=[jax.ShapeDtypeStruct((SWA_HEADS, s, d), F32), jax.ShapeDtypeStruct((SWA_HEADS, s, 1), F32)],
        compiler_params=_params(dimension_semantics=("arbitrary", "arbitrary")),
    )(q, kpad, kpad, vpad, vpad, sinks)


def _swa_bwd(q, kpad, vpad, sinks, o, lse, do, name):
    _, s, _ = q.shape
    tq = _swa_tile(s)
    grp, main, tail, sink = _swa_specs(tq)
    d = SWA_HEAD_DIM

    def body(q_ref, km_ref, kt_ref, vm_ref, vt_ref, sink_ref, o_ref, lse_ref, do_ref, dq_ref, dk_ref, dv_ref, dsink_ref):
        i = pl.program_id(1)
        kall = jnp.concatenate([km_ref[...], kt_ref[...]], axis=0)
        vall = jnp.concatenate([vm_ref[...], vt_ref[...]], axis=0)
        sink_col = _swa_sink_column(sink_ref)

        @pl.when(i == 0)
        def _():
            dk_ref[...] = jnp.zeros_like(dk_ref)
            dv_ref[...] = jnp.zeros_like(dv_ref)
            dsink_ref[...] = jnp.zeros_like(dsink_ref)

        dsink = jnp.zeros((SWA_GROUP * BLOCK, 1), F32)
        for b in range(tq // BLOCK):
            lo = b * BLOCK
            valid = _swa_band_mask(i == 0 if b == 0 else False)
            rows4 = SWA_GROUP * BLOCK
            q4 = q_ref[:, lo:lo + BLOCK, :].reshape(rows4, d)
            do4 = do_ref[:, lo:lo + BLOCK, :].reshape(rows4, d)
            lse4 = lse_ref[:, lo:lo + BLOCK, :].reshape(rows4, 1)
            delta = jnp.sum(do4 * o_ref[:, lo:lo + BLOCK, :].reshape(rows4, d), axis=-1, keepdims=True)
            kb, vb = kall[lo:lo + 2 * BLOCK], vall[lo:lo + 2 * BLOCK]
            do4b = do4.astype(BF16)
            p = jnp.where(valid, jnp.exp(_dot_nt(q4, kb) * SWA_SCALE - lse4), 0.0)
            ds = (p * (_dot_nt(do4b, vb) - delta) * SWA_SCALE).astype(BF16)
            dq_ref[:, lo:lo + BLOCK, :] = jnp.dot(ds, kb, preferred_element_type=F32).reshape(SWA_GROUP, BLOCK, d)
            band = pl.ds(pl.multiple_of(i * tq, BLOCK) + lo, 2 * BLOCK)
            dk_ref[band, :] += _dot_tn(ds, q4)
            dv_ref[band, :] += _dot_tn(p.astype(BF16), do4b)
            dsink = dsink - jnp.exp(sink_col - lse4) * delta
        per_head = [jnp.broadcast_to(jnp.sum(dsink[g * BLOCK:(g + 1) * BLOCK], axis=0, keepdims=True), (1, 128))
                    for g in range(SWA_GROUP)]
        dsink_ref[...] += jnp.concatenate(per_head + [jnp.zeros((8 - SWA_GROUP, 128), F32)], axis=0)

    acc = pl.BlockSpec((None, s + BLOCK, d), lambda j, i: (j, 0, 0))
    return _pallas_call(
        body, name=name, grid=(SWA_KV_HEADS, s // tq),
        in_specs=[grp(d), main, tail, main, tail, sink, grp(d), grp(1), grp(d)],
        out_specs=[grp(d), acc, acc, pl.BlockSpec((None, 8, 128), lambda j, i: (j, 0, 0))],
        out_shape=[jax.ShapeDtypeStruct((SWA_HEADS, s, d), F32),
                   jax.ShapeDtypeStruct((SWA_KV_HEADS, s + BLOCK, d), F32),
                   jax.ShapeDtypeStruct((SWA_KV_HEADS, s + BLOCK, d), F32),
                   jax.ShapeDtypeStruct((SWA_KV_HEADS, 8, 128), F32)],
        compiler_params=_params(dimension_semantics=("arbitrary", "arbitrary")),
    )(q, kpad, kpad, vpad, vpad, sinks, o, lse, do)


def _loss_head(y, target, name):
    s = y.shape[0]
    tm = _row_tile(s, 1024)

    def body(y_ref, t_ref, dy_ref, loss_ref):
        i = pl.program_id(0)
        err = y_ref[...] - t_ref[...]
        dy_ref[...] = err * (1.0 / D_MODEL)
        part = jnp.broadcast_to(0.5 * jnp.sum(jnp.mean(err * err, axis=-1, keepdims=True), axis=0, keepdims=True), (1, 128))

        @pl.when(i == 0)
        def _():
            loss_ref[...] = part

        @pl.when(i > 0)
        def _():
            loss_ref[...] += part

    row = pl.BlockSpec((tm, D_MODEL), lambda i: (i, 0))
    return _pallas_call(
        body, name=name, grid=(s // tm,), in_specs=[row, row], out_specs=[row, _full((1, 128))],
        out_shape=[jax.ShapeDtypeStruct((s, D_MODEL), F32), jax.ShapeDtypeStruct((1, 128), F32)],
        compiler_params=_params(dimension_semantics=("arbitrary",)),
    )(y, target)


def _adamw(w, g, m, v, name):
    rows, cols = w.shape
    tr = _divisor_tile(rows, 512) if rows % 16 == 0 else rows

    def body(w_ref, g_ref, m_ref, v_ref, d_ref, nm_ref, nv_ref):
        gv = g_ref[...]
        nm = ADAM_B1 * m_ref[...] + (1.0 - ADAM_B1) * gv
        nv = ADAM_B2 * v_ref[...] + (1.0 - ADAM_B2) * (gv * gv)
        m_hat = nm / (1.0 - ADAM_B1 ** ADAM_STEP)
        v_hat = nv / (1.0 - ADAM_B2 ** ADAM_STEP)
        d_ref[...] = -ADAM_LR * (m_hat / (jnp.sqrt(v_hat) + ADAM_EPS) + ADAM_WD * w_ref[...])
        nm_ref[...] = nm
        nv_ref[...] = nv

    blk = pl.BlockSpec((tr, cols), lambda i: (i, 0))
    return _pallas_call(
        body, name=name, grid=(rows // tr,), in_specs=[blk] * 4, out_specs=[blk] * 3,
        out_shape=[jax.ShapeDtypeStruct((rows, cols), F32)] * 3,
        compiler_params=_params(dimension_semantics=("arbitrary",)),
    )(w, g, m, v)


def _position():
    return lax.axis_index("x"), lax.axis_index("y"), lax.axis_index("c")


def _remote(src, dst, send_sems, recv_sems, k, to):
    return pltpu.make_async_remote_copy(src_ref=src, dst_ref=dst, send_sem=send_sems.at[k], recv_sem=recv_sems.at[k],
                                        device_id=to, device_id_type=MESH)


_HBM = pl.BlockSpec(memory_space=pltpu.HBM)


def _gather_stages(ins, outs, send_sems, recv_sems):
    na = len(ins)
    x, y, c = _position()
    me, sibling = (x, y, c), (x, y, 1 - c)
    xn, yn, dg = (1 - x, y), (x, 1 - y), (1 - x, 1 - y)

    def slot(a, chip, pc, half=None):
        ref = outs[a].at[4 * chip[0] + 2 * chip[1] + pc]
        if half is None:
            return ref
        rows = ref.shape[0] // 2
        return ref.at[pl.ds(half * rows, rows)]

    def cp(a, k, chip, pc, half, to, src=None):
        dst = slot(a, chip, pc, half)
        return _remote(dst if src is None else src, dst, send_sems, recv_sems, 8 * a + k, to)

    first_hop = [(0, xn), (1, yn)]
    second_hop = [(0, xn, 2, 0, yn), (1, yn, 3, 1, xn)]

    def sends():
        out = []
        for a in range(na):
            out += [cp(a, k, (x, y), c, None, (*to, c), src=ins[a].at[c]) for k, to in first_hop]
            out += [cp(a, fwd_k, frm, c, half, (*to, c)) for _, frm, fwd_k, half, to in second_hop]
            out += [cp(a, 4 + k, frm, c, None, sibling) for k, frm in first_hop]
            out += [cp(a, 6 + half, dg, c, half, sibling) for half in (0, 1)]
        return out

    def stage0():
        for a in range(na):
            for k, to in first_hop:
                cp(a, k, (x, y), c, None, (*to, c), src=ins[a].at[c]).start()

    def stage1():
        for k, frm, fwd_k, half, to in second_hop:
            for a in range(na):
                cp(a, k, frm, c, None, me).wait_recv()
                cp(a, fwd_k, frm, c, half, (*to, c)).start()
                cp(a, 4 + k, frm, c, None, sibling).start()

    def stage2():
        for half in (0, 1):
            for a in range(na):
                cp(a, 2 + half, dg, c, half, me).wait_recv()
                cp(a, 6 + half, dg, c, half, sibling).start()

    def stage3():
        for a in range(na):
            for k, chip, half in ((4, xn, None), (5, yn, None), (6, dg, 0), (7, dg, 1)):
                cp(a, k, chip, 1 - c, half, me).wait_recv()
        for sent in sends():
            sent.wait_send()

    return [stage0, stage1, stage2, stage3]


def _gather_out_shapes(mine):
    return [jax.ShapeDtypeStruct((N_DEV,) + m.shape[1:], m.dtype) for m in mine]


def _exchange_scratch(per_array, na):
    return [pltpu.SemaphoreType.DMA((per_array * na,)), pltpu.SemaphoreType.DMA((per_array * na,))]


def _all_gather_halves(mine, name):
    na = len(mine)

    def body(*refs):
        for stage in _gather_stages(refs[:na], refs[na:2 * na], *refs[2 * na:]):
            stage()

    return _pallas_call(body, name=name, in_specs=[_HBM] * na, out_specs=[_HBM] * na, out_shape=_gather_out_shapes(mine),
                        scratch_shapes=_exchange_scratch(8, na))(*mine)


def _sibling_exchange(parts, name, other_half):
    na = len(parts)

    def body(*refs):
        ins, outs, (send_sems, recv_sems) = refs[:na], refs[na:2 * na], refs[2 * na:]
        x, y, c = _position()
        copies = [_remote(ins[a].at[:, 1 - c] if other_half else ins[a], outs[a], send_sems, recv_sems, a, (x, y, 1 - c))
                  for a in range(na)]
        for cp in copies:
            cp.start()
        for cp in copies:
            cp.wait()

    return _pallas_call(
        body, name=name, in_specs=[_HBM] * na, out_specs=[_HBM] * na,
        out_shape=[jax.ShapeDtypeStruct(p.shape[:1] + p.shape[2:] if other_half else p.shape, p.dtype) for p in parts],
        scratch_shapes=_exchange_scratch(1, na),
    )(*parts)


def _scatter_stages(ins, outs, send_sems, recv_sems):
    na = len(ins)
    x, y, c = _position()
    chips = [(1 - x, y), (x, 1 - y), (1 - x, 1 - y)]

    def copies():
        return [_remote(ins[a].at[2 * px + py], outs[a].at[j], send_sems, recv_sems, 3 * a + j, (px, py, c))
                for a in range(na) for j, (px, py) in enumerate(chips)]

    def start():
        for cp in copies():
            cp.start()

    def wait():
        for cp in copies():
            cp.wait()

    return [start, wait]


def _scatter_out_shapes(parts):
    return [jax.ShapeDtypeStruct((3,) + p.shape[1:], p.dtype) for p in parts]


def _scatter_to_chips(parts, name):
    na = len(parts)

    def body(*refs):
        for stage in _scatter_stages(refs[:na], refs[na:2 * na], *refs[2 * na:]):
            stage()

    return _pallas_call(body, name=name, in_specs=[_HBM] * na, out_specs=[_HBM] * na, out_shape=_scatter_out_shapes(parts),
                        scratch_shapes=_exchange_scratch(3, na))(*parts)


def _assemble(g4, mine, name, side_by_side, order=None):
    _, nl, r, w = g4.shape
    tr = _divisor_tile(r, 640)

    def body(in_ref, mine_ref, out_ref):
        chip = 2 * lax.axis_index("x") + lax.axis_index("y")
        blocks = [jnp.where(chip == sh, mine_ref[...], in_ref[sh]) for sh in range(N_SHARD)]
        if side_by_side:
            full = jnp.concatenate(blocks, axis=-1)
            out_ref[...] = full if order is None else _take_cols(full, order)
        else:
            for sh in range(N_SHARD):
                out_ref[sh] = blocks[sh]

    if side_by_side:
        out_spec = pl.BlockSpec((None, tr, N_SHARD * w), lambda l, i: (l, i, 0))
        out_shape = jax.ShapeDtypeStruct((nl, r, N_SHARD * w), g4.dtype)
    else:
        out_spec = pl.BlockSpec((None, N_SHARD, tr, w), lambda l, i: (l, 0, i, 0))
        out_shape = jax.ShapeDtypeStruct((nl, N_SHARD, r, w), g4.dtype)
    return _pallas_call(
        body, name=name, grid=(nl, r // tr),
        in_specs=[pl.BlockSpec((N_SHARD, None, tr, w), lambda l, i: (0, l, i, 0)),
                  pl.BlockSpec((None, tr, w), lambda l, i: (l, i, 0))],
        out_specs=out_spec, out_shape=out_shape,
        compiler_params=_params(dimension_semantics=("arbitrary", "arbitrary")),
    )(g4, mine)


def _all_reduce_small(vec, name):
    r, l = vec.shape

    def body(v_ref, out_ref, gath_ref, send_sems, recv_sems):
        x, y, c = _position()
        me = 4 * x + 2 * y + c
        gath_ref[me] = v_ref[...]
        copies = []
        for k in range(1, N_DEV):
            to = (x ^ (k >> 2), y ^ ((k >> 1) & 1), c ^ (k & 1))
            copies.append(_remote(gath_ref.at[me], gath_ref.at[me], send_sems, recv_sems, k - 1, to))
        for cp in copies:
            cp.start()
        for k in range(1, N_DEV):
            frm = 4 * (x ^ (k >> 2)) + 2 * (y ^ ((k >> 1) & 1)) + (c ^ (k & 1))
            _remote(gath_ref.at[frm], gath_ref.at[frm], send_sems, recv_sems, k - 1, (x, y, c)).wait_recv()
        for cp in copies:
            cp.wait_send()
        total = gath_ref[0]
        for d in range(1, N_DEV):
            total = total + gath_ref[d]
        out_ref[...] = total

    vm = pl.BlockSpec(memory_space=pltpu.VMEM)
    return _pallas_call(
        body, name=name, in_specs=[vm], out_specs=vm, out_shape=jax.ShapeDtypeStruct((r, l), F32),
        scratch_shapes=[pltpu.VMEM((N_DEV, r, l), F32), pltpu.SemaphoreType.DMA((N_DEV - 1,)),
                        pltpu.SemaphoreType.DMA((N_DEV - 1,))],
    )(vec)


def _sum_blocks(blocks, out_dtype, name):
    m, w = blocks[0].shape
    tr = _divisor_tile(m, 1024)

    def body(*refs):
        total = refs[0][...].astype(F32)
        for ref in refs[1:-1]:
            total = total + ref[...].astype(F32)
        refs[-1][...] = total.astype(out_dtype)

    blk = pl.BlockSpec((tr, w), lambda i: (i, 0))
    return _pallas_call(
        body, name=name, grid=(m // tr,), in_specs=[blk] * len(blocks), out_specs=blk,
        out_shape=jax.ShapeDtypeStruct((m, w), out_dtype),
        compiler_params=_params(dimension_semantics=("arbitrary",)),
    )(*blocks)


FIRST_GROUPS = (('ffn1_w_gate', 'ffn1_w_up', 'ffn1_w_down'),)
REST_GROUPS = (('ffn2_w_gate', 'ffn2_w_up', 'ffn2_w_down'), ('w_o',), ('w_in',), ('mla_w_q_b',), ('mla_w_kv_b',))
N_FIRST = len(FIRST_GROUPS)
GRAD_SETS = {'first_gu': (('ffn1_w_gate', 'ffn1_w_up'),), 'first_d': (('ffn1_w_down',),), 'rest': REST_GROUPS}
GRAD_SET_ORDER = ('first_gu', 'first_d', 'rest')


def _shard_rows(name):
    shape, axis = BIG[name]
    return shape[0] // N_SHARD if axis == 0 else shape[0]


def _group_row_offsets(group):
    return [int(v) for v in np.cumsum([0] + [_shard_rows(n) for n in group])]


def _rope_tables(s):
    pos = jnp.arange(s, dtype=F32)
    inv = 1.0 / (ROPE_THETA ** (jnp.arange(0, MLA_ROPE, 2, dtype=F32) / MLA_ROPE))
    ang = pos[:, None] * inv[None, :]
    return jnp.tile(jnp.cos(ang), (1, SWA_HEADS)), jnp.tile(jnp.sin(ang), (1, SWA_HEADS))


_MIXER_GAINS = ('mix_norm', 'mla_q_a_norm', 'mla_kv_a_norm', 'mla_q_norm', 'mla_k_norm', 'swa_q_norm', 'swa_k_norm')


def _local_step(x, target, small, ex):
    s = x.shape[0]
    cos, sin = _rope_tables(s)
    row = lambda name, l: small[name][l][None, :]
    saved, bigs = [], []
    for l in range(DEPTH):
        big = ex.first_weights(l)
        sv = {'x0': x}
        x, sv['g1'], sv['u1'], got = _ffn_fwd(x, row('ffn1_norm', l), big['ffn1_w_gate'], big['ffn1_w_up'], big['ffn1_w_down'],
                                             f"ffn1_fwd_{l}", ex.gather_behind_ffn1(l))
        big.update(ex.rest_weights(l, got))
        bigs.append(big)
        sv['x1'] = x
        gains = [row(n, l) for n in _MIXER_GAINS]
        mixer_w = (big['w_in'], big['mla_w_q_b'], big['mla_w_kv_b'])
        q_a, k_a, v_a, q_b, k_b, v_b = _pre_fwd(x, gains, *mixer_w, cos, sin, f"pre_fwd_{l}")
        o_a, lse, got = _mla_fwd(q_a, k_a, v_a, f"mla_fwd_{l}", ex.gather_behind_mla(l))
        ex.gathered_behind_mla(l, got)
        kpad = jnp.pad(k_b, ((0, 0), (BLOCK, 0), (0, 0)))
        vpad = jnp.pad(v_b, ((0, 0), (BLOCK, 0), (0, 0)))
        sinks = jnp.broadcast_to(small['swa_sinks'][l].reshape(SWA_KV_HEADS, SWA_GROUP, 1), (SWA_KV_HEADS, SWA_GROUP, 128))
        o_b, lse_b = _swa_fwd(q_b, kpad, vpad, sinks, f"swa_fwd_{l}")
        sv.update(gains=gains, mixer_w=mixer_w, q_a=q_a, k_a=k_a, v_a=v_a, q_b=q_b, kpad=kpad, vpad=vpad, sinks=sinks,
                  o_a=o_a, lse=lse, o_b=o_b, lse_b=lse_b)
        x = _post_fwd(x, o_a, o_b, row('mla_out_norm', l), row('swa_out_norm', l), big['w_o'], f"post_fwd_{l}")
        sv['x2'] = x
        x, sv['g2'], sv['u2'], _ = _ffn_fwd(x, row('ffn2_norm', l), big['ffn2_w_gate'], big['ffn2_w_up'], big['ffn2_w_down'],
                                           f"ffn2_fwd_{l}")
        saved.append(sv)

    dx, loss = _loss_head(x, target, "loss_head")

    gs = {n: [None] * DEPTH for n in SMALL_NAMES}
    t = _attn_tile(s)
    for l in reversed(range(DEPTH)):
        sv, big = saved[l], bigs[l]

        def ffn_back(tag, xin, dy, gate, up, scatter=None):
            (dxi, dgain, nb, act, dgate, dup), got = _ffn_bwd(xin, dy, gate, up, row(tag + '_norm', l), big[tag + '_w_gate'],
                                                             big[tag + '_w_up'], big[tag + '_w_down'], f"{tag}_bwd_{l}", scatter)
            gs[tag + '_norm'][l] = dgain[0]
            names = (tag + '_w_gate', tag + '_w_up', tag + '_w_down')
            if tag == 'ffn2':
                group, _ = _matmul_tn(dgate, nb, 1.0, f"{tag}_dwg_{l}", None, 0, 3)
                group, _ = _matmul_tn(dup, nb, 1.0, f"{tag}_dwu_{l}", group, 1, 3)
                group, _ = _matmul_tn(act, dy, 0.5, f"{tag}_dwd_{l}", group, 2, 3)
                return dxi, {names: group}, got
            group, _ = _matmul_tn(dgate, nb, 1.0, f"{tag}_dwg_{l}", None, 0, 2)
            group, _ = _matmul_tn(dup, nb, 1.0, f"{tag}_dwu_{l}", group, 1, 2)
            ex.grads_ready(l, 'first_gu', {names[:2]: group})
            down, arrived = _matmul_tn(act, dy, 0.5, f"{tag}_dwd_{l}", None, 0, 1, ex.scatter_behind_dwd(l))
            ex.scattered_behind_dwd(l, arrived)
            ex.grads_ready(l, 'first_d', {names[2:]: down})
            return dxi, None, got

        dx, rest_grads, _ = ffn_back('ffn2', sv['x2'], dx, sv['g2'], sv['u2'])
        do_a, do_b, dga, dgb, dwo, delta = _post_bwd(dx, sv['o_a'], sv['o_b'], row('mla_out_norm', l), row('swa_out_norm', l),
                                              big['w_o'], f"post_bwd_{l}")
        gs['mla_out_norm'][l], gs['swa_out_norm'][l] = dga[0], dgb[0]
        rest_grads['w_o'] = dwo.reshape(N_SHARD, MIX_WIDTH // N_SHARD, D_MODEL)
        dq_a, dk_a, dv_a, got = _mla_bwd(sv['q_a'], sv['k_a'], sv['v_a'], do_a, sv['lse'], delta, f"mla_bwd_{l}",
                                         ex.scatter_behind_mla(l))
        ex.scattered_behind_mla(l, got)
        dq_b, dkpad, dvpad, dsink = _swa_bwd(sv['q_b'], sv['kpad'], sv['vpad'], sv['sinks'], sv['o_b'], sv['lse_b'], do_b,
                                             f"swa_bwd_{l}")
        gs['swa_sinks'][l] = dsink[:, :SWA_GROUP, 0].reshape(SWA_HEADS)
        cts = [dq_a, dk_a, dv_a, dq_b, dkpad[:, BLOCK:], dvpad[:, BLOCK:]]
        outs = _pre_bwd(sv['x1'], dx, cts, sv['gains'], *sv['mixer_w'], cos, sin, f"pre_bwd_{l}")
        dx = outs[0]
        for n, val in zip(_MIXER_GAINS, outs[1:8]):
            gs[n][l] = val[0]
        rest_grads['w_in'], rest_grads['mla_w_q_b'], rest_grads['mla_w_kv_b'] = outs[8:11]
        ex.grads_ready(l, 'rest', rest_grads)
        scatter = ex.scatter_behind_ffn1(l)
        dx, _, got = ffn_back('ffn1', sv['x0'], dx, sv['g1'], sv['u1'], scatter)
        ex.scattered_behind_ffn1(l, got)
    return loss, dx, gs


class _Exchange:
    def __init__(self, weights, c, chip):
        halves_of = lambda a: a.reshape(a.shape[:-2] + (2, a.shape[-2] // 2, a.shape[-1]))
        self.halves_of, self.c, self.chip = halves_of, c, chip
        self.mine = [[halves_of(jnp.concatenate([weights[n][l].astype(BF16) for n in group], axis=0))
                      for group in FIRST_GROUPS + REST_GROUPS] for l in range(DEPTH)]
        self.ahead, self.waiting, self.begun, self.received = {}, {}, {}, {}

    def _assembled(self, l, which, gathered):
        groups, base = (FIRST_GROUPS, 0) if which == 'first' else (REST_GROUPS, N_FIRST)
        big = {}
        for gi, group in enumerate(groups):
            offs = _group_row_offsets(group)
            _, rh, w = self.mine[l][base + gi].shape
            col_sharded = BIG[group[0]][1] == 1
            full = _assemble(gathered[gi].reshape(N_SHARD, 1, 2 * rh, w), self.mine[l][base + gi].reshape(1, 2 * rh, w),
                             f"assemble_{which}{gi}_{l}", col_sharded, MIXER_ORDERS.get(group[0]))
            for i, n in enumerate(group):
                rows = offs[i + 1] - offs[i]
                if len(group) > 1 and offs[i] % rows == 0:
                    big[n] = ((full, (None, rows, N_SHARD * w), (0, offs[i] // rows, 0)) if col_sharded else
                              (full, (None, N_SHARD, rows, w), (0, 0, offs[i] // rows, 0)))
                elif col_sharded:
                    big[n] = full[0, offs[i]:offs[i + 1]]
                else:
                    big[n] = full[0, :, offs[i]:offs[i + 1]].reshape(BIG[n][0])
        return big

    def first_weights(self, l):
        got = self.ahead[l][:N_FIRST] if l in self.ahead else _all_gather_halves(self.mine[l][:N_FIRST], f"gather_first_{l}")
        return self._assembled(l, 'first', got)

    def gather_behind_ffn1(self, l):
        return None if l in self.ahead else self.mine[l][N_FIRST:]

    def rest_weights(self, l, got):
        return self._assembled(l, 'rest', self.ahead[l][N_FIRST:] if l in self.ahead else got)

    def gather_behind_mla(self, l):
        return self.mine[l + 1] if l + 1 < DEPTH else None

    def gathered_behind_mla(self, l, got):
        if got:
            self.ahead[l + 1] = got

    def grads_ready(self, l, which, grads):
        groups = GRAD_SETS[which]
        parts = [self.halves_of(grads[group] if group in grads else jnp.concatenate([grads[n] for n in group], axis=1))
                 for group in groups]
        if l > 0:
            self.waiting[(l, which)] = parts
        else:
            self._begin(l, [(which, parts)])

    def _begin(self, l, sets):
        tag = sets[0][0] if len(sets) == 1 else "all"
        from_sibling = _sibling_exchange(sum((parts for _, parts in sets), []), f"swap_{tag}_{l}", True)
        at = 0
        for which, parts in sets:
            chip_sums = []
            for gi, p in enumerate(parts):
                got = from_sibling[at]
                at += 1
                kept = lax.dynamic_index_in_dim(p, self.c, axis=1, keepdims=False)
                rows = N_SHARD * p.shape[2]
                pair = _sum_blocks([kept.reshape(rows, -1), got.reshape(rows, -1)], BF16, f"sum_pair_{which}{gi}_{l}")
                chip_sums.append(pair.reshape(got.shape))
            self.begun[(l, which)] = chip_sums

    def scatter_behind_mla(self, l):
        if l + 1 >= DEPTH:
            return None
        self._begin(l + 1, [(which, self.waiting.pop((l + 1, which))) for which in GRAD_SET_ORDER])
        return sum((self.begun[(l + 1, which)] for which in GRAD_SET_ORDER), [])

    def scattered_behind_mla(self, l, got):
        at = 0
        for which in GRAD_SET_ORDER if got else ():
            self.received[(l + 1, which)] = got[at:at + len(GRAD_SETS[which])]
            at += len(GRAD_SETS[which])

    def scatter_behind_dwd(self, l):
        return self.begun[(l, 'first_gu')] if l == 0 else None

    def scattered_behind_dwd(self, l, got):
        if got:
            self.received[(l, 'first_gu')] = got

    def scatter_behind_ffn1(self, l):
        return self.begun[(l, 'rest')] if l == 0 else None

    def scattered_behind_ffn1(self, l, got):
        if got:
            self.received[(l, 'rest')] = got

    def reduced(self):
        keys = sorted(self.begun)
        for key in keys:
            if key not in self.received:
                self.received[key] = _scatter_to_chips(self.begun[key], f"scatter_{key[1]}_{key[0]}")
        halves = []
        for l, which in keys:
            for gi, (cs, got) in enumerate(zip(self.begun[(l, which)], self.received[(l, which)])):
                own = lax.dynamic_index_in_dim(cs, self.chip, axis=0, keepdims=False)
                halves.append(_sum_blocks([own, got[0], got[1], got[2]], F32, f"sum_chips_{which}{gi}_{l}"))
        others = _sibling_exchange(halves, "share_halves", False)
        per_layer, at = {}, 0
        for l, which in keys:
            for group in GRAD_SETS[which]:
                mine_h, other_h = halves[at], others[at]
                at += 1
                full = jnp.where(self.c == 0, jnp.concatenate([mine_h, other_h]), jnp.concatenate([other_h, mine_h]))
                offs = _group_row_offsets(group)
                for i, n in enumerate(group):
                    per_layer[(n, l)] = full[offs[i]:offs[i + 1]]
        return {n: jnp.stack([per_layer[(n, l)] for l in range(DEPTH)]) for n in BIG_NAMES}


def kernel(x, ffn1_norm, ffn1_w_gate, ffn1_w_up, ffn1_w_down, mix_norm, w_in, mla_q_a_norm, mla_w_q_b, mla_kv_a_norm, mla_w_kv_b, mla_q_norm, mla_k_norm, swa_q_norm, swa_k_norm, swa_sinks, mla_out_norm, swa_out_norm, w_o, ffn2_norm, ffn2_w_gate, ffn2_w_up, ffn2_w_down, loss_target, m_ffn1_norm, m_ffn1_w_gate, m_ffn1_w_up, m_ffn1_w_down, m_mix_norm, m_w_in, m_mla_q_a_norm, m_mla_w_q_b, m_mla_kv_a_norm, m_mla_w_kv_b, m_mla_q_norm, m_mla_k_norm, m_swa_q_norm, m_swa_k_norm, m_swa_sinks, m_mla_out_norm, m_swa_out_norm, m_w_o, m_ffn2_norm, m_ffn2_w_gate, m_ffn2_w_up, m_ffn2_w_down, v_ffn1_norm, v_ffn1_w_gate, v_ffn1_w_up, v_ffn1_w_down, v_mix_norm, v_w_in, v_mla_q_a_norm, v_mla_w_q_b, v_mla_kv_a_norm, v_mla_w_kv_b, v_mla_q_norm, v_mla_k_norm, v_swa_q_norm, v_swa_k_norm, v_swa_sinks, v_mla_out_norm, v_swa_out_norm, v_w_o, v_ffn2_norm, v_ffn2_w_gate, v_ffn2_w_up, v_ffn2_w_down):
    args = dict(locals())
    transposed = lambda a: jnp.swapaxes(a, 1, 2)
    as_kernels_see = lambda n, a: transposed(a) if n in TRANSPOSED else a
    weights = {n: as_kernels_see(n, args[n]) for n in WEIGHT_NAMES}
    mom_m = {n: as_kernels_see(n, args["m_" + n]) for n in WEIGHT_NAMES}
    mom_v = {n: as_kernels_see(n, args["v_" + n]) for n in WEIGHT_NAMES}
    ex = _Exchange(weights, lax.axis_index("c"), 2 * lax.axis_index("x") + lax.axis_index("y"))
    loss, dx, gs = _local_step(x[0], loss_target[0], {n: weights[n] for n in SMALL_NAMES}, ex)

    small_flat = jnp.concatenate([jnp.stack(gs[n]).reshape(-1) for n in SMALL_NAMES] + [loss[0, :1]])
    n_small = small_flat.shape[0]
    lanes = -(-n_small // (8 * 128)) * 128
    small_sum = _all_reduce_small(jnp.pad(small_flat, (0, 8 * lanes - n_small)).reshape(8, lanes), "reduce_small").reshape(-1)
    loss_out = small_sum[n_small - 1]
    grads = ex.reduced()

    packed = lambda d: jnp.pad(jnp.concatenate([d[n].reshape(-1) for n in SMALL_NAMES]), (0, 8 * lanes - n_small + 1)).reshape(8, lanes)
    small_out = _adamw(packed(weights), small_sum.reshape(8, lanes), packed(mom_m), packed(mom_v), "adamw_small")
    deltas, new_m, new_v = {}, {}, {}
    off = 0
    for n in SMALL_NAMES:
        cnt = int(np.prod(weights[n].shape))
        grads[n], deltas[n], new_m[n], new_v[n] = (a.reshape(-1)[off:off + cnt].reshape(weights[n].shape)
                                                    for a in (small_sum, *small_out))
        off += cnt
    for n in BIG_NAMES:
        shp = weights[n].shape
        two_d = (shp[0] * shp[1], shp[2])
        d, nm, nv = _adamw(weights[n].reshape(two_d), grads[n].reshape(two_d), mom_m[n].reshape(two_d),
                           mom_v[n].reshape(two_d), f"adamw_{n}")
        deltas[n], new_m[n], new_v[n] = d.reshape(shp), nm.reshape(shp), nv.reshape(shp)
    for n in TRANSPOSED:
        grads[n], deltas[n], new_m[n], new_v[n] = (transposed(a) for a in (grads[n], deltas[n], new_m[n], new_v[n]))

    return (loss_out, dx[None], *[grads[n] for n in WEIGHT_NAMES], *[deltas[n] for n in WEIGHT_NAMES],
            *[new_m[n] for n in WEIGHT_NAMES], *[new_v[n] for n in WEIGHT_NAMES])
```

```python
import numpy as np
import jax
import jax.numpy as jnp
from jax import lax
from jax.experimental import pallas as pl
from jax.experimental.pallas import tpu as pltpu

F32 = jnp.float32
BF16 = jnp.bfloat16

D_MODEL = 1024
DEPTH = 2
EPS = 1e-6
ROPE_THETA = 10000.0
BLOCK = 128
MLA_HEADS = 4
MLA_Q_RANK = 256
MLA_KV_RANK = 128
MLA_NOPE = 128
MLA_ROPE = 64
MLA_V = 128
MLA_QK = MLA_NOPE + MLA_ROPE
MLA_WIDTH = MLA_HEADS * MLA_V
SWA_HEADS = 8
SWA_KV_HEADS = 2
SWA_GROUP = SWA_HEADS // SWA_KV_HEADS
SWA_HEAD_DIM = 64
SWA_WIDTH = SWA_HEADS * SWA_HEAD_DIM
MIX_WIDTH = MLA_WIDTH + SWA_WIDTH
IN_SPLITS = (MLA_Q_RANK, MLA_KV_RANK, MLA_ROPE, SWA_WIDTH, SWA_KV_HEADS * SWA_HEAD_DIM, SWA_KV_HEADS * SWA_HEAD_DIM)
IN_COLS = sum(IN_SPLITS)
IN_OFFS = tuple(int(v) for v in np.cumsum((0,) + IN_SPLITS))
D_FF = 2816
MLA_SCALE = MLA_QK ** -0.5
LOG2E = 1.4426950408889634
LN2 = 0.6931471805599453
MLA_QSCALE = MLA_SCALE * LOG2E
SWA_SCALE = SWA_HEAD_DIM ** -0.5
NEG = -1e30

ADAM_LR = 0.001
ADAM_B1 = 0.9
ADAM_B2 = 0.999
ADAM_EPS = 1e-08
ADAM_WD = 0.01
ADAM_STEP = 10

N_SHARD = 4
N_DEV = 8
VMEM_LIMIT = 56 * 1024 * 1024
MESH = pl.DeviceIdType.MESH

WEIGHT_NAMES = ['ffn1_norm', 'ffn1_w_gate', 'ffn1_w_up', 'ffn1_w_down', 'mix_norm', 'w_in', 'mla_q_a_norm', 'mla_w_q_b',
                'mla_kv_a_norm', 'mla_w_kv_b', 'mla_q_norm', 'mla_k_norm', 'swa_q_norm', 'swa_k_norm', 'swa_sinks',
                'mla_out_norm', 'swa_out_norm', 'w_o', 'ffn2_norm', 'ffn2_w_gate', 'ffn2_w_up', 'ffn2_w_down']
TRANSPOSED = ('ffn1_w_gate', 'ffn1_w_up', 'ffn2_w_gate', 'ffn2_w_up')
BIG = {'ffn1_w_gate': ((D_FF, D_MODEL), 0), 'ffn1_w_up': ((D_FF, D_MODEL), 0), 'ffn1_w_down': ((D_FF, D_MODEL), 0),
       'w_in': ((D_MODEL, IN_COLS), 1), 'mla_w_q_b': ((MLA_Q_RANK, MLA_HEADS * MLA_QK), 1),
       'mla_w_kv_b': ((MLA_KV_RANK, MLA_HEADS * (MLA_NOPE + MLA_V)), 1), 'w_o': ((MIX_WIDTH, D_MODEL), 0),
       'ffn2_w_gate': ((D_FF, D_MODEL), 0), 'ffn2_w_up': ((D_FF, D_MODEL), 0), 'ffn2_w_down': ((D_FF, D_MODEL), 0)}
BIG_NAMES = [n for n in WEIGHT_NAMES if n in BIG]
SMALL_NAMES = [n for n in WEIGHT_NAMES if n not in BIG]

_pallas_call = pl.pallas_call


def _params(**kw):
    return pltpu.CompilerParams(vmem_limit_bytes=VMEM_LIMIT, **kw)


def _full(shape):
    n = len(shape)
    return pl.BlockSpec(shape, lambda *_: (0,) * n)


def _resident(shape):
    n = len(shape)
    return pl.BlockSpec(shape, lambda *_: (0,) * n, pipeline_mode=pl.Buffered(1))


@jax.custom_vjp
def _mm(a, w):
    return jnp.dot(a.astype(BF16), w, preferred_element_type=F32)


def _mm_fwd(a, w):
    return _mm(a, w), w


def _mm_bwd(w, dy):
    return lax.dot_general(dy.astype(BF16), w, (((1,), (1,)), ((), ())), preferred_element_type=F32), None


_mm.defvjp(_mm_fwd, _mm_bwd)


def _dot_nt(a, b):
    return lax.dot_general(a, b, (((1,), (1,)), ((), ())), preferred_element_type=F32)


def _dot_tn(a, b):
    return lax.dot_general(a, b, (((0,), (0,)), ((), ())), preferred_element_type=F32)


def _rms(t, g):
    return t * lax.rsqrt(jnp.mean(t * t, axis=-1, keepdims=True) + EPS) * g


def _sigmoid(z):
    return 1.0 / (1.0 + jnp.exp(-z))


def _row_tile(s, want):
    return min(want, s)


def _divisor_tile(rows, cap):
    return max(d for d in range(16, min(rows, cap) + 1, 16) if rows % d == 0)


FF_CHUNK = D_FF


def _weight_operand(w):
    if isinstance(w, tuple):
        arr, block, index = w
        return arr, pl.BlockSpec(block, lambda *_: index, pipeline_mode=pl.Buffered(1))
    return w, _resident(w.shape)


def _weight_rows(ref, start, n):
    if len(ref.shape) == 2:
        return ref[start:start + n, :]
    per = ref.shape[1]
    return ref[start // per:(start + n) // per].reshape(n, ref.shape[2])


def _ffn_fwd(x, g, wg, wu, wd, name, gather=None):
    s = x.shape[0]
    tm = _row_tile(s, 256)
    steps = s // tm
    ng = len(gather) if gather else 0
    (wg, wg_spec), (wu, wu_spec), (wd, wd_spec) = _weight_operand(wg), _weight_operand(wu), _weight_operand(wd)

    def body(x_ref, g_ref, wg_ref, wu_ref, wd_ref, *rest):
        g_ins, (y_ref, gate_ref, up_ref), g_outs, sems = rest[:ng], rest[ng:ng + 3], rest[ng + 3:2 * ng + 3], rest[2 * ng + 3:]
        if ng:
            _run_stages_at(_gather_stages(g_ins, g_outs, *sems), [(0,), (steps * 3 // 8,), (steps * 11 // 16,), (steps - 1,)])
        xv = x_ref[...]
        nb = _rms(xv, g_ref[...]).astype(BF16)
        acc = xv
        for c in range(0, D_FF, FF_CHUNK):
            gate = _dot_nt(nb, _weight_rows(wg_ref, c, FF_CHUNK))
            up = _dot_nt(nb, _weight_rows(wu_ref, c, FF_CHUNK))
            gate_ref[:, c:c + FF_CHUNK] = gate.astype(BF16)
            up_ref[:, c:c + FF_CHUNK] = up.astype(BF16)
            act = (gate * _sigmoid(gate) * up).astype(BF16)
            acc = acc + 0.5 * jnp.dot(act, _weight_rows(wd_ref, c, FF_CHUNK), preferred_element_type=F32)
        y_ref[...] = acc

    outs = _pallas_call(
        body, name=name, grid=(steps,),
        in_specs=[pl.BlockSpec((tm, D_MODEL), lambda i: (i, 0)), _full((1, D_MODEL)), wg_spec, wu_spec, wd_spec] + [_HBM] * ng,
        out_specs=[pl.BlockSpec((tm, D_MODEL), lambda i: (i, 0)), pl.BlockSpec((tm, D_FF), lambda i: (i, 0)),
                   pl.BlockSpec((tm, D_FF), lambda i: (i, 0))] + [_HBM] * ng,
        out_shape=[jax.ShapeDtypeStruct((s, D_MODEL), F32), jax.ShapeDtypeStruct((s, D_FF), BF16),
                   jax.ShapeDtypeStruct((s, D_FF), BF16)] + (_gather_out_shapes(gather) if ng else []),
        scratch_shapes=_exchange_scratch(8, ng) if ng else [],
        compiler_params=_params(dimension_semantics=("arbitrary",)),
    )(x, g, wg, wu, wd, *(gather or []))
    return outs[0], outs[1], outs[2], outs[3:]


def _ffn_bwd(x, dy, gate, up, g, wg, wu, wd, name, scatter=None):
    s = x.shape[0]
    tm = _row_tile(s, 256)
    steps = s // tm
    ng = len(scatter) if scatter else 0
    (wg, wg_spec), (wu, wu_spec), (wd, wd_spec) = _weight_operand(wg), _weight_operand(wu), _weight_operand(wd)

    def body(x_ref, dy_ref, gate_ref, up_ref, g_ref, wg_ref, wu_ref, wd_ref, *rest):
        c_ins, (dx_ref, dgain_ref, n_ref, act_ref, dgate_ref, dup_ref) = rest[:ng], rest[ng:ng + 6]
        c_outs, sems = rest[ng + 6:2 * ng + 6], rest[2 * ng + 6:]
        if ng:
            _run_stages_at(_scatter_stages(c_ins, c_outs, *sems), [(0,), (steps - 1,)])
        i = pl.program_id(0)
        xv = x_ref[...]
        dyv = dy_ref[...]
        gv = g_ref[...]
        r = lax.rsqrt(jnp.mean(xv * xv, axis=-1, keepdims=True) + EPS)
        xh = xv * r
        n_ref[...] = (xh * gv).astype(BF16)
        dyh = (0.5 * dyv).astype(BF16)
        dn = jnp.zeros_like(xv)
        for c in range(0, D_FF, FF_CHUNK):
            dact = _dot_nt(dyh, _weight_rows(wd_ref, c, FF_CHUNK))
            gt = gate_ref[:, c:c + FF_CHUNK].astype(F32)
            u = up_ref[:, c:c + FF_CHUNK].astype(F32)
            sg = _sigmoid(gt)
            sl = gt * sg
            act_ref[:, c:c + FF_CHUNK] = (sl * u).astype(BF16)
            dup = (dact * sl).astype(BF16)
            dgate = (dact * u * (sg * (1.0 + gt * (1.0 - sg)))).astype(BF16)
            dup_ref[:, c:c + FF_CHUNK] = dup
            dgate_ref[:, c:c + FF_CHUNK] = dgate
            dn = (dn + jnp.dot(dgate, _weight_rows(wg_ref, c, FF_CHUNK), preferred_element_type=F32)
                  + jnp.dot(dup, _weight_rows(wu_ref, c, FF_CHUNK), preferred_element_type=F32))
        part = jnp.sum(dn * xh, axis=0, keepdims=True)

        @pl.when(i == 0)
        def _():
            dgain_ref[...] = part

        @pl.when(i > 0)
        def _():
            dgain_ref[...] += part

        dxh = dn * gv
        dx_ref[...] = dyv + r * (dxh - xh * jnp.mean(dxh * xh, axis=-1, keepdims=True))

    row = lambda w: pl.BlockSpec((tm, w), lambda i: (i, 0))
    outs = _pallas_call(
        body, name=name, grid=(steps,),
        in_specs=[row(D_MODEL), row(D_MODEL), row(D_FF), row(D_FF), _full((1, D_MODEL)), wg_spec, wu_spec, wd_spec]
        + [_HBM] * ng,
        out_specs=[row(D_MODEL), _full((1, D_MODEL)), row(D_MODEL), row(D_FF), row(D_FF), row(D_FF)] + [_HBM] * ng,
        out_shape=[jax.ShapeDtypeStruct((s, D_MODEL), F32), jax.ShapeDtypeStruct((1, D_MODEL), F32),
                   jax.ShapeDtypeStruct((s, D_MODEL), BF16), jax.ShapeDtypeStruct((s, D_FF), BF16),
                   jax.ShapeDtypeStruct((s, D_FF), BF16), jax.ShapeDtypeStruct((s, D_FF), BF16)]
        + (_scatter_out_shapes(scatter) if ng else []),
        scratch_shapes=_exchange_scratch(3, ng) if ng else [],
        compiler_params=_params(dimension_semantics=("arbitrary",)),
    )(x, dy, gate, up, g, wg, wu, wd, *(scatter or []))
    return outs[:6], outs[6:]


def _store_col_shards(o_ref, acc, first_shard, n_here, width):
    for q in range(n_here):
        o_ref[q] = acc[:, (first_shard + q) * width:(first_shard + q + 1) * width].astype(BF16)


def _matmul_tn(a, b, scale, name, group=None, slot=0, slots=1, scatter=None):
    t, m = a.shape
    n = b.shape[1]
    tk = _row_tile(t, 2048)
    tn = n // 2
    nk = t // tk
    per = m // N_SHARD
    earlier = [] if group is None else [group]
    ne, ng = len(earlier), len(scatter) if scatter else 0

    def body(a_ref, b_ref, *rest):
        c_ins, o_ref, c_outs = rest[ne:ne + ng], rest[ne + ng], rest[ne + ng + 1:ne + 2 * ng + 1]
        acc_ref, sems = rest[ne + 2 * ng + 1], rest[ne + 2 * ng + 2:]
        if ng:
            _run_stages_at(_scatter_stages(c_ins, c_outs, *sems), [(0, 0), (n // tn - 1, nk - 1)])
        k = pl.program_id(1)
        bv = b_ref[...]
        if scale != 1.0:
            bv = bv.astype(F32) * scale
        part = _dot_tn(a_ref[...].astype(BF16), bv.astype(BF16))

        @pl.when(k == 0)
        def _():
            acc_ref[...] = part

        @pl.when(k > 0)
        def _():
            acc_ref[...] += part

        @pl.when(k == nk - 1)
        def _():
            for sh in range(N_SHARD):
                o_ref[sh] = acc_ref[sh * per:(sh + 1) * per, :].astype(BF16)

    outs = _pallas_call(
        body, name=name, grid=(n // tn, nk),
        in_specs=[pl.BlockSpec((tk, m), lambda j, k: (k, 0)), pl.BlockSpec((tk, tn), lambda j, k: (k, j))]
        + [pl.BlockSpec(memory_space=pl.ANY)] * ne + [_HBM] * ng,
        out_specs=[pl.BlockSpec((N_SHARD, per, tn), lambda j, k: (0, slot, j))] + [_HBM] * ng,
        out_shape=[jax.ShapeDtypeStruct((N_SHARD, slots * per, n), BF16)] + (_scatter_out_shapes(scatter) if ng else []),
        scratch_shapes=[pltpu.VMEM((m, tn), F32)] + (_exchange_scratch(3, ng) if ng else []),
        input_output_aliases={2: 0} if earlier else {},
        compiler_params=_params(dimension_semantics=("arbitrary", "arbitrary")),
    )(a, b, *earlier, *(scatter or []))
    return outs[0], outs[1:]


_HALF = SWA_HEAD_DIM // 2
_IN_ORDER = (list(range(0, IN_OFFS[2]))
             + [IN_OFFS[3] + SWA_HEAD_DIM * h + i for h in range(SWA_HEADS) for i in range(_HALF)]
             + [IN_OFFS[3] + SWA_HEAD_DIM * h + _HALF + i for h in range(SWA_HEADS) for i in range(_HALF)]
             + list(range(IN_OFFS[5], IN_OFFS[6]))
             + [IN_OFFS[4] + SWA_HEAD_DIM * j + i for j in range(SWA_KV_HEADS) for i in range(_HALF)]
             + [IN_OFFS[4] + SWA_HEAD_DIM * j + _HALF + i for j in range(SWA_KV_HEADS) for i in range(_HALF)]
             + list(range(IN_OFFS[2], IN_OFFS[3])))
_QB_ORDER = ([MLA_QK * h + i for h in range(MLA_HEADS) for i in range(MLA_NOPE)]
             + [MLA_QK * h + MLA_NOPE + i for h in range(MLA_HEADS) for i in range(_HALF)]
             + [MLA_QK * h + MLA_NOPE + _HALF + i for h in range(MLA_HEADS) for i in range(_HALF)])
_KVB_ORDER = ([(MLA_NOPE + MLA_V) * h + i for h in range(MLA_HEADS) for i in range(MLA_NOPE)]
              + [(MLA_NOPE + MLA_V) * h + MLA_NOPE + i for h in range(MLA_HEADS) for i in range(MLA_V)])
MIXER_ORDERS = {'w_in': _IN_ORDER, 'mla_w_q_b': _QB_ORDER, 'mla_w_kv_b': _KVB_ORDER}
_P_CQ, _P_CKV, _P_QA, _P_QB, _P_VS, _P_KA, _P_KB, _P_PE = (int(v) for v in np.cumsum(
    (0, MLA_Q_RANK, MLA_KV_RANK, SWA_WIDTH // 2, SWA_WIDTH // 2, IN_SPLITS[5], IN_SPLITS[4] // 2, IN_SPLITS[4] // 2)))


def _runs(order):
    out, start = [], 0
    for i in range(1, len(order) + 1):
        if i == len(order) or order[i] != order[i - 1] + 1:
            out.append((order[start], i - start))
            start = i
    return out


def _inverse(order):
    inv = [0] * len(order)
    for new, old in enumerate(order):
        inv[old] = new
    return inv


def _take_cols(a, order):
    return jnp.concatenate([a[..., st:st + w] for st, w in _runs(order)], axis=-1)


def _segment_matrix(n, seg):
    return (lax.broadcasted_iota(jnp.int32, (n, n), 0) // seg == lax.broadcasted_iota(jnp.int32, (n, n), 1) // seg).astype(BF16)


def _cmm2(t, b):
    hi = t.astype(BF16)
    lo = (t - hi.astype(F32)).astype(BF16)
    return jnp.dot(hi, b, preferred_element_type=F32) + jnp.dot(lo, b, preferred_element_type=F32)


@jax.custom_vjp
def _cmm(t, b, bt):
    return jnp.dot(t.astype(BF16), b, preferred_element_type=F32)


def _cmm_fwd(t, b, bt):
    return _cmm(t, b, bt), (b, bt)


def _cmm_bwd(res, dy):
    b, bt = res
    return _cmm(dy, bt, b), None, None


_cmm.defvjp(_cmm_fwd, _cmm_bwd)


def _segsum(t, b):
    return _cmm(t, b, b)


def _rowsum(t, exact=False):
    n = t.shape[-1]
    if exact:
        return _cmm2(t, jnp.ones((n, 128), BF16))
    return _cmm(t, jnp.ones((n, 128), BF16), jnp.ones((128, n), BF16))


def _by_head(vals, width):
    lane = lax.broadcasted_iota(jnp.int32, (vals[0].shape[0], len(vals) * width), 1)
    out = vals[-1]
    for hd in range(len(vals) - 2, -1, -1):
        out = jnp.where(lane < (hd + 1) * width, vals[hd], out)
    return out


def _rope2(a, b, cos, sin):
    return a * cos - b * sin, b * cos + a * sin


def _pre_math(x, gm, gqa, gkva, gq, gk, gsq, gsk, taps, win, wqb, wkvb, cos, sin):
    h = _rms(x, gm)
    proj = _mm(h, win)
    if taps is not None:
        proj = proj + taps[0]
    cqn = _rms(proj[:, _P_CQ:_P_CKV], gqa)
    qa_all = _mm(cqn, wqb)
    ckvn = _rms(proj[:, _P_CKV:_P_QA], gkva)
    kv_all = _mm(ckvn, wkvb)
    if taps is not None:
        qa_all = qa_all + taps[1]
        kv_all = kv_all + taps[2]
    nh, hw = MLA_HEADS, MLA_HEADS * _HALF
    seg_mla = _segment_matrix(hw, _HALF)
    tile = lambda g, n: jnp.concatenate([g] * n, axis=-1)
    c4, s4 = cos[:, :hw], sin[:, :hw]

    def mla_heads(nope, r1, r2, gain):
        rr = r1 * r1 + r2 * r2
        lane_head = lax.broadcasted_iota(jnp.int32, (hw, nh * MLA_NOPE), 0) // _HALF
        spread = (lane_head == lax.broadcasted_iota(jnp.int32, (hw, nh * MLA_NOPE), 1) // MLA_NOPE).astype(BF16)
        rope_on_nope = _cmm(rr, spread, spread.T)
        ss_nope = [_rowsum(jnp.square(nope[:, hd * MLA_NOPE:(hd + 1) * MLA_NOPE])) for hd in range(nh)]
        rinv = [lax.rsqrt((ss_nope[hd] + rope_on_nope[:, hd * MLA_NOPE:(hd + 1) * MLA_NOPE]) * (1.0 / MLA_QK) + EPS)
                for hd in range(nh)]
        rl = lax.rsqrt((_segsum(rr, seg_mla) + _by_head(ss_nope, _HALF)) * (1.0 / MLA_QK) + EPS)
        o1, o2 = _rope2(r1 * rl * tile(gain[:, MLA_NOPE:MLA_NOPE + _HALF], nh), r2 * rl * tile(gain[:, MLA_NOPE + _HALF:], nh), c4, s4)
        return [jnp.concatenate([nope[:, hd * MLA_NOPE:(hd + 1) * MLA_NOPE] * rinv[hd] * gain[:, :MLA_NOPE],
                                 o1[:, hd * _HALF:(hd + 1) * _HALF], o2[:, hd * _HALF:(hd + 1) * _HALF]], axis=-1)
                for hd in range(nh)]

    q_a = mla_heads(qa_all[:, :nh * MLA_NOPE], qa_all[:, nh * MLA_NOPE:nh * MLA_NOPE + hw], qa_all[:, nh * MLA_NOPE + hw:], gq)
    pe1, pe2 = proj[:, _P_PE:_P_PE + _HALF], proj[:, _P_PE + _HALF:_P_PE + 2 * _HALF]
    k_a = mla_heads(kv_all[:, :nh * MLA_NOPE], tile(pe1, nh), tile(pe2, nh), gk)
    v_a = [kv_all[:, nh * MLA_NOPE + hd * MLA_V:nh * MLA_NOPE + (hd + 1) * MLA_V] for hd in range(nh)]

    def swa_heads(a, b, gain, n):
        w = n * _HALF
        r = lax.rsqrt(_segsum(a * a + b * b, _segment_matrix(w, _HALF)) * (1.0 / SWA_HEAD_DIM) + EPS)
        o1, o2 = _rope2(a * r * tile(gain[:, :_HALF], n), b * r * tile(gain[:, _HALF:], n), cos[:, :w], sin[:, :w])
        return [jnp.concatenate([o1[:, hd * _HALF:(hd + 1) * _HALF], o2[:, hd * _HALF:(hd + 1) * _HALF]], axis=-1) for hd in range(n)]

    q_b = swa_heads(proj[:, _P_QA:_P_QB], proj[:, _P_QB:_P_VS], gsq, SWA_HEADS)
    k_b = swa_heads(proj[:, _P_KA:_P_KB], proj[:, _P_KB:_P_PE], gsk, SWA_KV_HEADS)
    v_b = [proj[:, _P_VS + j * SWA_HEAD_DIM:_P_VS + (j + 1) * SWA_HEAD_DIM] for j in range(SWA_KV_HEADS)]
    return (q_a, k_a, v_a, q_b, k_b, v_b), (h, cqn, ckvn)


_PRE_GAIN_WIDTHS = (D_MODEL, MLA_Q_RANK, MLA_KV_RANK, MLA_QK, MLA_QK, SWA_HEAD_DIM, SWA_HEAD_DIM)
_PRE_HEADS = ((MLA_HEADS, MLA_QK), (MLA_HEADS, MLA_QK), (MLA_HEADS, MLA_V),
              (SWA_HEADS, SWA_HEAD_DIM), (SWA_KV_HEADS, SWA_HEAD_DIM), (SWA_KV_HEADS, SWA_HEAD_DIM))


def _pre_fwd(x, gains, win, wqb, wkvb, cos, sin, name):
    s = x.shape[0]
    tm = _row_tile(s, 512)

    def body(x_ref, *refs):
        g_refs, (win_ref, wqb_ref, wkvb_ref, cos_ref, sin_ref), out_refs = refs[:7], refs[7:12], refs[12:]
        outs, _ = _pre_math(x_ref[...], *[g[...] for g in g_refs], None, win_ref[...], wqb_ref[...], wkvb_ref[...],
                            cos_ref[...], sin_ref[...])
        for idx, (ref, heads) in enumerate(zip(out_refs, outs)):
            for hd, val in enumerate(heads):
                ref[hd] = (val * MLA_QSCALE if idx == 0 else val).astype(BF16)

    heads_spec = lambda nh, w: pl.BlockSpec((nh, tm, w), lambda i: (0, i, 0))
    return _pallas_call(
        body, name=name, grid=(s // tm,),
        in_specs=[pl.BlockSpec((tm, D_MODEL), lambda i: (i, 0))] + [_full((1, w)) for w in _PRE_GAIN_WIDTHS]
        + [_resident(win.shape), _resident(wqb.shape), _resident(wkvb.shape),
           pl.BlockSpec((tm, SWA_HEADS * _HALF), lambda i: (i, 0)), pl.BlockSpec((tm, SWA_HEADS * _HALF), lambda i: (i, 0))],
        out_specs=[heads_spec(nh, w) for nh, w in _PRE_HEADS],
        out_shape=[jax.ShapeDtypeStruct((nh, s, w), BF16) for nh, w in _PRE_HEADS],
        compiler_params=_params(dimension_semantics=("arbitrary",)),
    )(x, *gains, win, wqb, wkvb, cos, sin)


def _pre_bwd(x, dx_res, cts, gains, win, wqb, wkvb, cos, sin, name):
    s = x.shape[0]
    tm = _row_tile(s, 256)
    tap_widths = (IN_COLS, MLA_HEADS * MLA_QK, MLA_HEADS * (MLA_NOPE + MLA_V))

    def body(x_ref, dxr_ref, *refs):
        ct_refs, g_refs = refs[:6], refs[6:13]
        win_ref, wqb_ref, wkvb_ref, cos_ref, sin_ref = refs[13:18]
        dx_ref, dg_refs, dw_refs, acc_refs = refs[18], refs[19:26], refs[26:29], refs[29:32]
        i = pl.program_id(0)
        win_v, wqb_v, wkvb_v, cos_v, sin_v = win_ref[...], wqb_ref[...], wkvb_ref[...], cos_ref[...], sin_ref[...]

        def f(xv, gm, gqa, gkva, gq, gk, gsq, gsk, t0, t1, t2):
            return _pre_math(xv, gm, gqa, gkva, gq, gk, gsq, gsk, (t0, t1, t2), win_v, wqb_v, wkvb_v, cos_v, sin_v)

        taps = [jnp.zeros((tm, w), F32) for w in tap_widths]
        _, vjp, acts = jax.vjp(f, x_ref[...], *[g[...] for g in g_refs], *taps, has_aux=True)
        ct = tuple([ref[hd] for hd in range(nh)] for ref, (nh, _) in zip(ct_refs, _PRE_HEADS))
        grads = vjp(ct)
        dx_ref[...] = grads[0] + dxr_ref[...]
        dws = [_dot_tn(a.astype(BF16), t.astype(BF16)) for a, t in zip(acts, grads[8:11])]

        @pl.when(i == 0)
        def _():
            for ref, val in zip(dg_refs, grads[1:8]):
                ref[...] = val
            for ref, val in zip(acc_refs, dws):
                ref[...] = val

        @pl.when(i > 0)
        def _():
            for ref, val in zip(dg_refs, grads[1:8]):
                ref[...] += val
            for ref, val in zip(acc_refs, dws):
                ref[...] += val

        @pl.when(i == s // tm - 1)
        def _():
            for ref, acc, order in zip(dw_refs, acc_refs, (_IN_ORDER, _QB_ORDER, _KVB_ORDER)):
                _store_col_shards(ref, _take_cols(acc[...], _inverse(order)), 0, N_SHARD, acc.shape[1] // N_SHARD)

    heads_spec = lambda nh, w: pl.BlockSpec((nh, tm, w), lambda i: (0, i, 0))
    row = pl.BlockSpec((tm, D_MODEL), lambda i: (i, 0))
    half = pl.BlockSpec((tm, SWA_HEADS * _HALF), lambda i: (i, 0))
    shard_shapes = [(N_SHARD, w.shape[0], w.shape[1] // N_SHARD) for w in (win, wqb, wkvb)]
    return _pallas_call(
        body, name=name, grid=(s // tm,),
        in_specs=[row, row] + [heads_spec(nh, w) for nh, w in _PRE_HEADS] + [_full((1, w)) for w in _PRE_GAIN_WIDTHS]
        + [_resident(win.shape), _resident(wqb.shape), _resident(wkvb.shape), half, half],
        out_specs=[row] + [_full((1, w)) for w in _PRE_GAIN_WIDTHS] + [_full(shp) for shp in shard_shapes],
        out_shape=[jax.ShapeDtypeStruct((s, D_MODEL), F32)] + [jax.ShapeDtypeStruct((1, w), F32) for w in _PRE_GAIN_WIDTHS]
        + [jax.ShapeDtypeStruct(shp, BF16) for shp in shard_shapes],
        scratch_shapes=[pltpu.VMEM(w.shape, F32) for w in (win, wqb, wkvb)],
        compiler_params=_params(dimension_semantics=("arbitrary",)),
    )(x, dx_res, *cts, *gains, win, wqb, wkvb, cos, sin)


def _post_math(oa, ob, ga, gb, wo):
    mixed = jnp.concatenate([_rms(jnp.concatenate(oa, axis=-1), ga), _rms(jnp.concatenate(ob, axis=-1), gb)], axis=-1)
    return _mm(mixed, wo), mixed


def _post_fwd(x, oa, ob, ga, gb, wo, name):
    s = x.shape[0]
    tm = _row_tile(s, 512)

    def body(x_ref, oa_ref, ob_ref, ga_ref, gb_ref, wo_ref, y_ref):
        y, _ = _post_math([oa_ref[hd] for hd in range(MLA_HEADS)], [ob_ref[hd] for hd in range(SWA_HEADS)],
                          ga_ref[...], gb_ref[...], wo_ref[...])
        y_ref[...] = x_ref[...] + y

    row = pl.BlockSpec((tm, D_MODEL), lambda i: (i, 0))
    return _pallas_call(
        body, name=name, grid=(s // tm,),
        in_specs=[row, pl.BlockSpec((MLA_HEADS, tm, MLA_V), lambda i: (0, i, 0)),
                  pl.BlockSpec((SWA_HEADS, tm, SWA_HEAD_DIM), lambda i: (0, i, 0)),
                  _full((1, MLA_WIDTH)), _full((1, SWA_WIDTH)), _resident(wo.shape)],
        out_specs=row, out_shape=jax.ShapeDtypeStruct((s, D_MODEL), F32),
        compiler_params=_params(dimension_semantics=("arbitrary",)),
    )(x, oa, ob, ga, gb, wo)


def _post_bwd(dy, oa, ob, ga, gb, wo, name):
    s = dy.shape[0]
    tm = _row_tile(s, 512)
    t = _attn_tile(s)

    def body(dy_ref, oa_ref, ob_ref, ga_ref, gb_ref, wo_ref, doa_ref, dob_ref, dga_ref, dgb_ref, dwo_ref, delta_ref, acc_ref):
        i = pl.program_id(0)
        wo_v = wo_ref[...]
        dyv = dy_ref[...]

        def f(oa_l, ob_l, ga_v, gb_v):
            return _post_math(oa_l, ob_l, ga_v, gb_v, wo_v)

        _, vjp, mixed = jax.vjp(f, [oa_ref[hd] for hd in range(MLA_HEADS)], [ob_ref[hd] for hd in range(SWA_HEADS)],
                                ga_ref[...], gb_ref[...], has_aux=True)
        doa, dob, dga, dgb = vjp(dyv)
        for hd in range(MLA_HEADS):
            doa_ref[hd] = doa[hd]
            rows = _as_rows(_rowsum(doa[hd] * oa_ref[hd], exact=True))
            for j in range(tm // t):
                delta_ref[hd, j] = rows[:, j * t:(j + 1) * t]
        for hd in range(SWA_HEADS):
            dob_ref[hd] = dob[hd]
        dwo = _dot_tn(mixed.astype(BF16), dyv.astype(BF16))

        @pl.when(i == 0)
        def _():
            dga_ref[...] = dga
            dgb_ref[...] = dgb
            acc_ref[...] = dwo

        @pl.when(i > 0)
        def _():
            dga_ref[...] += dga
            dgb_ref[...] += dgb
            acc_ref[...] += dwo

        @pl.when(i == s // tm - 1)
        def _():
            dwo_ref[...] = acc_ref[...].astype(BF16)

    row = pl.BlockSpec((tm, D_MODEL), lambda i: (i, 0))
    oa_spec = pl.BlockSpec((MLA_HEADS, tm, MLA_V), lambda i: (0, i, 0))
    ob_spec = pl.BlockSpec((SWA_HEADS, tm, SWA_HEAD_DIM), lambda i: (0, i, 0))
    return _pallas_call(
        body, name=name, grid=(s // tm,),
        in_specs=[row, oa_spec, ob_spec, _full((1, MLA_WIDTH)), _full((1, SWA_WIDTH)), _resident(wo.shape)],
        out_specs=[oa_spec, ob_spec, _full((1, MLA_WIDTH)), _full((1, SWA_WIDTH)), _full(wo.shape),
                   pl.BlockSpec((MLA_HEADS, tm // t, 8, t), lambda i: (0, i, 0, 0))],
        out_shape=[jax.ShapeDtypeStruct((MLA_HEADS, s, MLA_V), F32), jax.ShapeDtypeStruct((SWA_HEADS, s, SWA_HEAD_DIM), F32),
                   jax.ShapeDtypeStruct((1, MLA_WIDTH), F32), jax.ShapeDtypeStruct((1, SWA_WIDTH), F32),
                   jax.ShapeDtypeStruct(wo.shape, BF16), jax.ShapeDtypeStruct((MLA_HEADS, s // t, 8, t), F32)],
        scratch_shapes=[pltpu.VMEM(wo.shape, F32)],
        compiler_params=_params(dimension_semantics=("arbitrary",)),
    )(dy, oa, ob, ga, gb, wo)


def _attn_tile(s):
    return 512 if s >= 2048 else 128


def _as_rows(cols):
    return cols.T[0:8, :]


def _causal_mask(t):
    return lax.broadcasted_iota(jnp.int32, (t, t), 1) <= lax.broadcasted_iota(jnp.int32, (t, t), 0)


def _pipelined_blocks(first, count, last_block, issue, consume, carry, prefetch_after):
    def clamped(j, slot):
        issue(jnp.minimum(j, last_block), slot)

    def pair(jj, c):
        a = first + 2 * jj
        clamped(a + 1, 1)
        c = consume(a, 0, c)
        clamped(a + 2, 0)
        return consume(a + 1, 1, c)

    clamped(first, 0)
    npairs = count // 2
    carry = lax.fori_loop(0, npairs, pair, carry)

    def odd(c):
        c = consume(first + 2 * npairs, 0, c)
        if prefetch_after:
            clamped(first + count, 0)
        return c

    return lax.cond(count - 2 * npairs == 1, odd, lambda c: c, carry)


def _run_stages_at(stages, steps):
    for stage, step in zip(stages, steps):
        here = pl.program_id(0) == step[0]
        for axis in range(1, len(step)):
            here = here & (pl.program_id(axis) == step[axis])
        pl.when(here)(stage)


def _mla_fwd(q, k, v, name, gather=None):
    nh, s, _ = q.shape
    t = _attn_tile(s)
    nq = s // t
    ng = len(gather) if gather else 0

    def body(q_ref, k_ref, v_ref, *rest):
        g_ins, (o_ref, lse_ref), g_outs = rest[:ng], rest[ng:ng + 2], rest[ng + 2:2 * ng + 2]
        (s0_ref, s1_ref), sems = rest[2 * ng + 2:2 * ng + 4], rest[2 * ng + 4:]
        if ng:
            _run_stages_at(_gather_stages(g_ins, g_outs, *sems), [(0, 0), (nh // 2, 0), (nh - 1, 0), (nh - 1, nq - 1)])
        qi = pl.program_id(1)
        qv = q_ref[...]
        s_refs = (s0_ref, s1_ref)

        def rows(j):
            return pl.ds(pl.multiple_of(j * t, t), t)

        def issue(j, slot):
            s_refs[slot][...] = _dot_nt(qv, k_ref[rows(j), :])

        ones = jnp.ones((t, MLA_V), BF16)

        def consume(j, slot, carry, masked=False):
            m, acc = carry
            sc = s_refs[slot][...]
            if masked:
                sc = jnp.where(_causal_mask(t), sc, NEG)
            m_new = jnp.maximum(m, jnp.max(sc, axis=-1, keepdims=True))
            p = jnp.exp2(sc - m_new)
            v_ones = jnp.concatenate([v_ref[rows(j), :], ones], axis=-1)
            acc = jnp.exp2(m - m_new) * acc + jnp.dot(p.astype(BF16), v_ones, preferred_element_type=F32)
            return m_new, acc

        init = (jnp.full((t, 1), NEG, F32), jnp.zeros((t, 2 * MLA_V), F32))
        carry = _pipelined_blocks(0, qi, nq - 1, issue, consume, init, True)
        m, acc = consume(qi, 0, carry, masked=True)
        l = acc[:, MLA_V:]
        o_ref[...] = acc[:, :MLA_V] / l
        lse_ref[...] = _as_rows(m + jnp.log2(l))

    outs = _pallas_call(
        body, name=name, grid=(nh, nq),
        in_specs=[pl.BlockSpec((None, t, MLA_QK), lambda h, i: (h, i, 0)), pl.BlockSpec((None, s, MLA_QK), lambda h, i: (h, 0, 0)),
                  pl.BlockSpec((None, s, MLA_V), lambda h, i: (h, 0, 0))] + [_HBM] * ng,
        out_specs=[pl.BlockSpec((None, t, MLA_V), lambda h, i: (h, i, 0)), pl.BlockSpec((None, None, 8, t), lambda h, i: (h, i, 0, 0))]
        + [_HBM] * ng,
        out_shape=[jax.ShapeDtypeStruct((nh, s, MLA_V), F32), jax.ShapeDtypeStruct((nh, nq, 8, t), F32)]
        + (_gather_out_shapes(gather) if ng else []),
        scratch_shapes=[pltpu.VMEM((t, t), F32)] * 2 + (_exchange_scratch(8, ng) if ng else []),
        compiler_params=_params(dimension_semantics=("arbitrary", "arbitrary")),
    )(q, k, v, *(gather or []))
    return outs[0], outs[1], outs[2:]


def _mla_bwd(q, k, v, do, lse_row, delta_row, name, scatter=None):
    nh, s, _ = q.shape
    t = _attn_tile(s)
    nq = s // t
    ng = len(scatter) if scatter else 0

    def body(q_ref, k_ref, v_ref, do_ref, lse_ref, delta_ref, *rest):
        c_ins, (dq_ref, dk_ref, dv_ref), c_outs = rest[:ng], rest[ng:ng + 3], rest[ng + 3:2 * ng + 3]
        (s0_ref, s1_ref, dp0_ref, dp1_ref), sems = rest[2 * ng + 3:2 * ng + 7], rest[2 * ng + 7:]
        if ng:
            _run_stages_at(_scatter_stages(c_ins, c_outs, *sems), [(0, 0), (nh - 1, nq - 1)])
        kj = pl.program_id(1)
        kv_, vv = k_ref[...], v_ref[...]
        s_refs, dp_refs = (s0_ref, s1_ref), (dp0_ref, dp1_ref)

        @pl.when(kj == 0)
        def _():
            dq_ref[...] = jnp.zeros_like(dq_ref)

        def rows(i):
            return pl.ds(pl.multiple_of(i * t, t), t)

        def issue(i, slot):
            s_refs[slot][...] = _dot_nt(kv_, q_ref[rows(i), :])
            dp_refs[slot][...] = _dot_nt(vv, do_ref[rows(i), :].astype(BF16))

        def consume(i, slot, carry, masked=False):
            dk, dv = carry
            p = jnp.exp2(s_refs[slot][...] - lse_ref[i][0:1, :])
            if masked:
                p = jnp.where(lax.broadcasted_iota(jnp.int32, (t, t), 0) <= lax.broadcasted_iota(jnp.int32, (t, t), 1), p, 0.0)
            dv = dv + jnp.dot(p.astype(BF16), do_ref[rows(i), :].astype(BF16), preferred_element_type=F32)
            ds = (p * (dp_refs[slot][...] - delta_ref[i][0:1, :])).astype(BF16)
            dk = dk + jnp.dot(ds, q_ref[rows(i), :], preferred_element_type=F32)
            dq_ref[rows(i), :] += _dot_tn(ds, kv_) * MLA_SCALE
            return dk, dv

        issue(kj, 0)
        carry = consume(kj, 0, (jnp.zeros((t, MLA_QK), F32), jnp.zeros((t, MLA_V), F32)), masked=True)
        dk, dv = _pipelined_blocks(kj + 1, nq - 1 - kj, nq - 1, issue, consume, carry, False)
        dk_ref[...] = dk * LN2
        dv_ref[...] = dv

    tile = lambda w: pl.BlockSpec((None, t, w), lambda h, j: (h, j, 0))
    whole = lambda w: pl.BlockSpec((None, s, w), lambda h, j: (h, 0, 0))
    rows_spec = pl.BlockSpec((None, nq, 8, t), lambda h, j: (h, 0, 0, 0))
    outs = _pallas_call(
        body, name=name, grid=(nh, nq),
        in_specs=[whole(MLA_QK), tile(MLA_QK), tile(MLA_V), whole(MLA_V), rows_spec, rows_spec] + [_HBM] * ng,
        out_specs=[whole(MLA_QK), tile(MLA_QK), tile(MLA_V)] + [_HBM] * ng,
        out_shape=[jax.ShapeDtypeStruct((nh, s, MLA_QK), F32), jax.ShapeDtypeStruct((nh, s, MLA_QK), F32),
                   jax.ShapeDtypeStruct((nh, s, MLA_V), F32)] + (_scatter_out_shapes(scatter) if ng else []),
        scratch_shapes=[pltpu.VMEM((t, t), F32)] * 4 + (_exchange_scratch(3, ng) if ng else []),
        compiler_params=_params(dimension_semantics=("arbitrary", "arbitrary")),
    )(q, k, v, do, lse_row, delta_row, *(scatter or []))
    return outs[0], outs[1], outs[2], outs[3:]


def _swa_tile(s):
    return min(s, 8 * BLOCK)


def _swa_specs(tq):
    nb = tq // BLOCK
    grp = lambda w: pl.BlockSpec((SWA_GROUP, tq, w), lambda j, i: (j, i, 0))
    main = pl.BlockSpec((None, tq, SWA_HEAD_DIM), lambda j, i: (j, i, 0))
    tail = pl.BlockSpec((None, BLOCK, SWA_HEAD_DIM), lambda j, i: (j, nb * (i + 1), 0))
    sink = pl.BlockSpec((None, SWA_GROUP, 128), lambda j, i: (j, 0, 0))
    return grp, main, tail, sink


def _swa_band_mask(first):
    shape = (SWA_GROUP * BLOCK, 2 * BLOCK)
    q_rel = (lax.broadcasted_iota(jnp.int32, shape, 0) & (BLOCK - 1)) + BLOCK
    k_rel = lax.broadcasted_iota(jnp.int32, shape, 1)
    dist = q_rel - k_rel
    return (dist >= 0) & (dist < BLOCK) & ((k_rel >= BLOCK) | jnp.logical_not(first))


def _swa_sink_column(sink_ref):
    sk = sink_ref[...]
    return jnp.concatenate([jnp.broadcast_to(sk[g:g + 1, 0:1], (BLOCK, 1)) for g in range(SWA_GROUP)], axis=0)


def _swa_fwd(q, kpad, vpad, sinks, name):
    _, s, _ = q.shape
    tq = _swa_tile(s)
    grp, main, tail, sink = _swa_specs(tq)
    d = SWA_HEAD_DIM

    def body(q_ref, km_ref, kt_ref, vm_ref, vt_ref, sink_ref, o_ref, lse_ref):
        i = pl.program_id(1)
        kall = jnp.concatenate([km_ref[...], kt_ref[...]], axis=0)
        vall = jnp.concatenate([vm_ref[...], vt_ref[...]], axis=0)
        v_ones = jnp.concatenate([vall, jnp.ones_like(vall)], axis=-1)
        sink_col = _swa_sink_column(sink_ref)
        for b in range(tq // BLOCK):
            lo = b * BLOCK
            valid = _swa_band_mask(i == 0 if b == 0 else False)
            q4 = q_ref[:, lo:lo + BLOCK, :].reshape(SWA_GROUP * BLOCK, d)
            sc = jnp.where(valid, _dot_nt(q4, kall[lo:lo + 2 * BLOCK]) * SWA_SCALE, NEG)
            m = jnp.maximum(jnp.max(sc, axis=-1, keepdims=True), sink_col)
            both = jnp.dot(jnp.exp(sc - m).astype(BF16), v_ones[lo:lo + 2 * BLOCK], preferred_element_type=F32)
            den = both[:, d:d + 1] + jnp.exp(sink_col - m)
            out = both[:, :d] * (1.0 / den)
            o_ref[:, lo:lo + BLOCK, :] = out.reshape(SWA_GROUP, BLOCK, d)
            lse_ref[:, lo:lo + BLOCK, :] = (m + jnp.log(den)).reshape(SWA_GROUP, BLOCK, 1)

    return _pallas_call(
        body, name=name, grid=(SWA_KV_HEADS, s // tq),
        in_specs=[grp(d), main, tail, main, tail, sink], out_specs=[grp(d), grp(1)],
        out_shape=[jax.ShapeDtypeStruct((SWA_HEADS, s, d), F32), jax.ShapeDtypeStruct((SWA_HEADS, s, 1), F32)],
        compiler_params=_params(dimension_semantics=("arbitrary", "arbitrary")),
    )(q, kpad, kpad, vpad, vpad, sinks)


def _swa_bwd(q, kpad, vpad, sinks, o, lse, do, name):
    _, s, _ = q.shape
    tq = _swa_tile(s)
    grp, main, tail, sink = _swa_specs(tq)
    d = SWA_HEAD_DIM

    def body(q_ref, km_ref, kt_ref, vm_ref, vt_ref, sink_ref, o_ref, lse_ref, do_ref, dq_ref, dk_ref, dv_ref, dsink_ref):
        i = pl.program_id(1)
        kall = jnp.concatenate([km_ref[...], kt_ref[...]], axis=0)
        vall = jnp.concatenate([vm_ref[...], vt_ref[...]], axis=0)
        sink_col = _swa_sink_column(sink_ref)

        @pl.when(i == 0)
        def _():
            dk_ref[...] = jnp.zeros_like(dk_ref)
            dv_ref[...] = jnp.zeros_like(dv_ref)
            dsink_ref[...] = jnp.zeros_like(dsink_ref)

        dsink = jnp.zeros((SWA_GROUP * BLOCK, 1), F32)
        for b in range(tq // BLOCK):
            lo = b * BLOCK
            valid = _swa_band_mask(i == 0 if b == 0 else False)
            rows4 = SWA_GROUP * BLOCK
            q4 = q_ref[:, lo:lo + BLOCK, :].reshape(rows4, d)
            do4 = do_ref[:, lo:lo + BLOCK, :].reshape(rows4, d)
            lse4 = lse_ref[:, lo:lo + BLOCK, :].reshape(rows4, 1)
            delta = jnp.sum(do4 * o_ref[:, lo:lo + BLOCK, :].reshape(rows4, d), axis=-1, keepdims=True)
            kb, vb = kall[lo:lo + 2 * BLOCK], vall[lo:lo + 2 * BLOCK]
            do4b = do4.astype(BF16)
            p = jnp.where(valid, jnp.exp(_dot_nt(q4, kb) * SWA_SCALE - lse4), 0.0)
            ds = (p * (_dot_nt(do4b, vb) - delta) * SWA_SCALE).astype(BF16)
            dq_ref[:, lo:lo + BLOCK, :] = jnp.dot(ds, kb, preferred_element_type=F32).reshape(SWA_GROUP, BLOCK, d)
            band = pl.ds(pl.multiple_of(i * tq, BLOCK) + lo, 2 * BLOCK)
            dk_ref[band, :] += _dot_tn(ds, q4)
            dv_ref[band, :] += _dot_tn(p.astype(BF16), do4b)
            dsink = dsink - jnp.exp(sink_col - lse4) * delta
        per_head = [jnp.broadcast_to(jnp.sum(dsink[g * BLOCK:(g + 1) * BLOCK], axis=0, keepdims=True), (1, 128))
                    for g in range(SWA_GROUP)]
        dsink_ref[...] += jnp.concatenate(per_head + [jnp.zeros((8 - SWA_GROUP, 128), F32)], axis=0)

    acc = pl.BlockSpec((None, s + BLOCK, d), lambda j, i: (j, 0, 0))
    return _pallas_call(
        body, name=name, grid=(SWA_KV_HEADS, s // tq),
        in_specs=[grp(d), main, tail, main, tail, sink, grp(d), grp(1), grp(d)],
        out_specs=[grp(d), acc, acc, pl.BlockSpec((None, 8, 128), lambda j, i: (j, 0, 0))],
        out_shape=[jax.ShapeDtypeStruct((SWA_HEADS, s, d), F32),
                   jax.ShapeDtypeStruct((SWA_KV_HEADS, s + BLOCK, d), F32),
                   jax.ShapeDtypeStruct((SWA_KV_HEADS, s + BLOCK, d), F32),
                   jax.ShapeDtypeStruct((SWA_KV_HEADS, 8, 128), F32)],
        compiler_params=_params(dimension_semantics=("arbitrary", "arbitrary")),
    )(q, kpad, kpad, vpad, vpad, sinks, o, lse, do)


def _loss_head(y, target, name):
    s = y.shape[0]
    tm = _row_tile(s, 1024)

    def body(y_ref, t_ref, dy_ref, loss_ref):
        i = pl.program_id(0)
        err = y_ref[...] - t_ref[...]
        dy_ref[...] = err * (1.0 / D_MODEL)
        part = jnp.broadcast_to(0.5 * jnp.sum(jnp.mean(err * err, axis=-1, keepdims=True), axis=0, keepdims=True), (1, 128))

        @pl.when(i == 0)
        def _():
            loss_ref[...] = part

        @pl.when(i > 0)
        def _():
            loss_ref[...] += part

    row = pl.BlockSpec((tm, D_MODEL), lambda i: (i, 0))
    return _pallas_call(
        body, name=name, grid=(s // tm,), in_specs=[row, row], out_specs=[row, _full((1, 128))],
        out_shape=[jax.ShapeDtypeStruct((s, D_MODEL), F32), jax.ShapeDtypeStruct((1, 128), F32)],
        compiler_params=_params(dimension_semantics=("arbitrary",)),
    )(y, target)


def _adamw(w, g, m, v, name):
    rows, cols = w.shape
    tr = _divisor_tile(rows, 512) if rows % 16 == 0 else rows

    def body(w_ref, g_ref, m_ref, v_ref, d_ref, nm_ref, nv_ref):
        gv = g_ref[...]
        nm = ADAM_B1 * m_ref[...] + (1.0 - ADAM_B1) * gv
        nv = ADAM_B2 * v_ref[...] + (1.0 - ADAM_B2) * (gv * gv)
        m_hat = nm / (1.0 - ADAM_B1 ** ADAM_STEP)
        v_hat = nv / (1.0 - ADAM_B2 ** ADAM_STEP)
        d_ref[...] = -ADAM_LR * (m_hat / (jnp.sqrt(v_hat) + ADAM_EPS) + ADAM_WD * w_ref[...])
        nm_ref[...] = nm
        nv_ref[...] = nv

    blk = pl.BlockSpec((tr, cols), lambda i: (i, 0))
    return _pallas_call(
        body, name=name, grid=(rows // tr,), in_specs=[blk] * 4, out_specs=[blk] * 3,
        out_shape=[jax.ShapeDtypeStruct((rows, cols), F32)] * 3,
        compiler_params=_params(dimension_semantics=("arbitrary",)),
    )(w, g, m, v)


def _position():
    return lax.axis_index("x"), lax.axis_index("y"), lax.axis_index("c")


def _remote(src, dst, send_sems, recv_sems, k, to):
    return pltpu.make_async_remote_copy(src_ref=src, dst_ref=dst, send_sem=send_sems.at[k], recv_sem=recv_sems.at[k],
                                        device_id=to, device_id_type=MESH)


_HBM = pl.BlockSpec(memory_space=pltpu.HBM)


def _gather_stages(ins, outs, send_sems, recv_sems):
    na = len(ins)
    x, y, c = _position()
    me, sibling = (x, y, c), (x, y, 1 - c)
    xn, yn, dg = (1 - x, y), (x, 1 - y), (1 - x, 1 - y)

    def slot(a, chip, pc, half=None):
        ref = outs[a].at[4 * chip[0] + 2 * chip[1] + pc]
        if half is None:
            return ref
        rows = ref.shape[0] // 2
        return ref.at[pl.ds(half * rows, rows)]

    def cp(a, k, chip, pc, half, to, src=None):
        dst = slot(a, chip, pc, half)
        return _remote(dst if src is None else src, dst, send_sems, recv_sems, 8 * a + k, to)

    first_hop = [(0, xn), (1, yn)]
    second_hop = [(0, xn, 2, 0, yn), (1, yn, 3, 1, xn)]

    def sends():
        out = []
        for a in range(na):
            out += [cp(a, k, (x, y), c, None, (*to, c), src=ins[a].at[c]) for k, to in first_hop]
            out += [cp(a, fwd_k, frm, c, half, (*to, c)) for _, frm, fwd_k, half, to in second_hop]
            out += [cp(a, 4 + k, frm, c, None, sibling) for k, frm in first_hop]
            out += [cp(a, 6 + half, dg, c, half, sibling) for half in (0, 1)]
        return out

    def stage0():
        for a in range(na):
            for k, to in first_hop:
                cp(a, k, (x, y), c, None, (*to, c), src=ins[a].at[c]).start()

    def stage1():
        for k, frm, fwd_k, half, to in second_hop:
            for a in range(na):
                cp(a, k, frm, c, None, me).wait_recv()
                cp(a, fwd_k, frm, c, half, (*to, c)).start()
                cp(a, 4 + k, frm, c, None, sibling).start()

    def stage2():
        for half in (0, 1):
            for a in range(na):
                cp(a, 2 + half, dg, c, half, me).wait_recv()
                cp(a, 6 + half, dg, c, half, sibling).start()

    def stage3():
        for a in range(na):
            for k, chip, half in ((4, xn, None), (5, yn, None), (6, dg, 0), (7, dg, 1)):
                cp(a, k, chip, 1 - c, half, me).wait_recv()
        for sent in sends():
            sent.wait_send()

    return [stage0, stage1, stage2, stage3]


def _gather_out_shapes(mine):
    return [jax.ShapeDtypeStruct((N_DEV,) + m.shape[1:], m.dtype) for m in mine]


def _exchange_scratch(per_array, na):
    return [pltpu.SemaphoreType.DMA((per_array * na,)), pltpu.SemaphoreType.DMA((per_array * na,))]


def _all_gather_halves(mine, name):
    na = len(mine)

    def body(*refs):
        for stage in _gather_stages(refs[:na], refs[na:2 * na], *refs[2 * na:]):
            stage()

    return _pallas_call(body, name=name, in_specs=[_HBM] * na, out_specs=[_HBM] * na, out_shape=_gather_out_shapes(mine),
                        scratch_shapes=_exchange_scratch(8, na))(*mine)


def _sibling_exchange(parts, name, other_half):
    na = len(parts)

    def body(*refs):
        ins, outs, (send_sems, recv_sems) = refs[:na], refs[na:2 * na], refs[2 * na:]
        x, y, c = _position()
        copies = [_remote(ins[a].at[:, 1 - c] if other_half else ins[a], outs[a], send_sems, recv_sems, a, (x, y, 1 - c))
                  for a in range(na)]
        for cp in copies:
            cp.start()
        for cp in copies:
            cp.wait()

    return _pallas_call(
        body, name=name, in_specs=[_HBM] * na, out_specs=[_HBM] * na,
        out_shape=[jax.ShapeDtypeStruct(p.shape[:1] + p.shape[2:] if other_half else p.shape, p.dtype) for p in parts],
        scratch_shapes=_exchange_scratch(1, na),
    )(*parts)


def _scatter_stages(ins, outs, send_sems, recv_sems):
    na = len(ins)
    x, y, c = _position()
    chips = [(1 - x, y), (x, 1 - y), (1 - x, 1 - y)]

    def copies():
        return [_remote(ins[a].at[2 * px + py], outs[a].at[j], send_sems, recv_sems, 3 * a + j, (px, py, c))
                for a in range(na) for j, (px, py) in enumerate(chips)]

    def start():
        for cp in copies():
            cp.start()

    def wait():
        for cp in copies():
            cp.wait()

    return [start, wait]


def _scatter_out_shapes(parts):
    return [jax.ShapeDtypeStruct((3,) + p.shape[1:], p.dtype) for p in parts]


def _scatter_to_chips(parts, name):
    na = len(parts)

    def body(*refs):
        for stage in _scatter_stages(refs[:na], refs[na:2 * na], *refs[2 * na:]):
            stage()

    return _pallas_call(body, name=name, in_specs=[_HBM] * na, out_specs=[_HBM] * na, out_shape=_scatter_out_shapes(parts),
                        scratch_shapes=_exchange_scratch(3, na))(*parts)


def _assemble(g4, mine, name, side_by_side, order=None):
    _, nl, r, w = g4.shape
    tr = _divisor_tile(r, 640)

    def body(in_ref, mine_ref, out_ref):
        chip = 2 * lax.axis_index("x") + lax.axis_index("y")
        blocks = [jnp.where(chip == sh, mine_ref[...], in_ref[sh]) for sh in range(N_SHARD)]
        if side_by_side:
            full = jnp.concatenate(blocks, axis=-1)
            out_ref[...] = full if order is None else _take_cols(full, order)
        else:
            for sh in range(N_SHARD):
                out_ref[sh] = blocks[sh]

    if side_by_side:
        out_spec = pl.BlockSpec((None, tr, N_SHARD * w), lambda l, i: (l, i, 0))
        out_shape = jax.ShapeDtypeStruct((nl, r, N_SHARD * w), g4.dtype)
    else:
        out_spec = pl.BlockSpec((None, N_SHARD, tr, w), lambda l, i: (l, 0, i, 0))
        out_shape = jax.ShapeDtypeStruct((nl, N_SHARD, r, w), g4.dtype)
    return _pallas_call(
        body, name=name, grid=(nl, r // tr),
        in_specs=[pl.BlockSpec((N_SHARD, None, tr, w), lambda l, i: (0, l, i, 0)),
                  pl.BlockSpec((None, tr, w), lambda l, i: (l, i, 0))],
        out_specs=out_spec, out_shape=out_shape,
        compiler_params=_params(dimension_semantics=("arbitrary", "arbitrary")),
    )(g4, mine)


def _all_reduce_small(vec, name):
    r, l = vec.shape

    def body(v_ref, out_ref, gath_ref, send_sems, recv_sems):
        x, y, c = _position()
        me = 4 * x + 2 * y + c
        gath_ref[me] = v_ref[...]
        copies = []
        for k in range(1, N_DEV):
            to = (x ^ (k >> 2), y ^ ((k >> 1) & 1), c ^ (k & 1))
            copies.append(_remote(gath_ref.at[me], gath_ref.at[me], send_sems, recv_sems, k - 1, to))
        for cp in copies:
            cp.start()
        for k in range(1, N_DEV):
            frm = 4 * (x ^ (k >> 2)) + 2 * (y ^ ((k >> 1) & 1)) + (c ^ (k & 1))
            _remote(gath_ref.at[frm], gath_ref.at[frm], send_sems, recv_sems, k - 1, (x, y, c)).wait_recv()
        for cp in copies:
            cp.wait_send()
        total = gath_ref[0]
        for d in range(1, N_DEV):
            total = total + gath_ref[d]
        out_ref[...] = total

    vm = pl.BlockSpec(memory_space=pltpu.VMEM)
    return _pallas_call(
        body, name=name, in_specs=[vm], out_specs=vm, out_shape=jax.ShapeDtypeStruct((r, l), F32),
        scratch_shapes=[pltpu.VMEM((N_DEV, r, l), F32), pltpu.SemaphoreType.DMA((N_DEV - 1,)),
                        pltpu.SemaphoreType.DMA((N_DEV - 1,))],
    )(vec)


def _sum_blocks(blocks, out_dtype, name):
    m, w = blocks[0].shape
    tr = _divisor_tile(m, 1024)

    def body(*refs):
        total = refs[0][...].astype(F32)
        for ref in refs[1:-1]:
            total = total + ref[...].astype(F32)
        refs[-1][...] = total.astype(out_dtype)

    blk = pl.BlockSpec((tr, w), lambda i: (i, 0))
    return _pallas_call(
        body, name=name, grid=(m // tr,), in_specs=[blk] * len(blocks), out_specs=blk,
        out_shape=jax.ShapeDtypeStruct((m, w), out_dtype),
        compiler_params=_params(dimension_semantics=("arbitrary",)),
    )(*blocks)


FIRST_GROUPS = (('ffn1_w_gate', 'ffn1_w_up', 'ffn1_w_down'),)
REST_GROUPS = (('ffn2_w_gate', 'ffn2_w_up', 'ffn2_w_down'), ('w_o',), ('w_in',), ('mla_w_q_b',), ('mla_w_kv_b',))
N_FIRST = len(FIRST_GROUPS)
GRAD_SETS = {'first_gu': (('ffn1_w_gate', 'ffn1_w_up'),), 'first_d': (('ffn1_w_down',),), 'rest': REST_GROUPS}
GRAD_SET_ORDER = ('first_gu', 'first_d', 'rest')


def _shard_rows(name):
    shape, axis = BIG[name]
    return shape[0] // N_SHARD if axis == 0 else shape[0]


def _group_row_offsets(group):
    return [int(v) for v in np.cumsum([0] + [_shard_rows(n) for n in group])]


def _rope_tables(s):
    pos = jnp.arange(s, dtype=F32)
    inv = 1.0 / (ROPE_THETA ** (jnp.arange(0, MLA_ROPE, 2, dtype=F32) / MLA_ROPE))
    ang = pos[:, None] * inv[None, :]
    return jnp.tile(jnp.cos(ang), (1, SWA_HEADS)), jnp.tile(jnp.sin(ang), (1, SWA_HEADS))


_MIXER_GAINS = ('mix_norm', 'mla_q_a_norm', 'mla_kv_a_norm', 'mla_q_norm', 'mla_k_norm', 'swa_q_norm', 'swa_k_norm')


def _local_step(x, target, small, ex):
    s = x.shape[0]
    cos, sin = _rope_tables(s)
    row = lambda name, l: small[name][l][None, :]
    saved, bigs = [], []
    for l in range(DEPTH):
        big = ex.first_weights(l)
        sv = {'x0': x}
        x, sv['g1'], sv['u1'], got = _ffn_fwd(x, row('ffn1_norm', l), big['ffn1_w_gate'], big['ffn1_w_up'], big['ffn1_w_down'],
                                             f"ffn1_fwd_{l}", ex.gather_behind_ffn1(l))
        big.update(ex.rest_weights(l, got))
        bigs.append(big)
        sv['x1'] = x
        gains = [row(n, l) for n in _MIXER_GAINS]
        mixer_w = (big['w_in'], big['mla_w_q_b'], big['mla_w_kv_b'])
        q_a, k_a, v_a, q_b, k_b, v_b = _pre_fwd(x, gains, *mixer_w, cos, sin, f"pre_fwd_{l}")
        o_a, lse, got = _mla_fwd(q_a, k_a, v_a, f"mla_fwd_{l}", ex.gather_behind_mla(l))
        ex.gathered_behind_mla(l, got)
        kpad = jnp.pad(k_b, ((0, 0), (BLOCK, 0), (0, 0)))
        vpad = jnp.pad(v_b, ((0, 0), (BLOCK, 0), (0, 0)))
        sinks = jnp.broadcast_to(small['swa_sinks'][l].reshape(SWA_KV_HEADS, SWA_GROUP, 1), (SWA_KV_HEADS, SWA_GROUP, 128))
        o_b, lse_b = _swa_fwd(q_b, kpad, vpad, sinks, f"swa_fwd_{l}")
        sv.update(gains=gains, mixer_w=mixer_w, q_a=q_a, k_a=k_a, v_a=v_a, q_b=q_b, kpad=kpad, vpad=vpad, sinks=sinks,
                  o_a=o_a, lse=lse, o_b=o_b, lse_b=lse_b)
        x = _post_fwd(x, o_a, o_b, row('mla_out_norm', l), row('swa_out_norm', l), big['w_o'], f"post_fwd_{l}")
        sv['x2'] = x
        x, sv['g2'], sv['u2'], _ = _ffn_fwd(x, row('ffn2_norm', l), big['ffn2_w_gate'], big['ffn2_w_up'], big['ffn2_w_down'],
                                           f"ffn2_fwd_{l}")
        saved.append(sv)

    dx, loss = _loss_head(x, target, "loss_head")

    gs = {n: [None] * DEPTH for n in SMALL_NAMES}
    t = _attn_tile(s)
    for l in reversed(range(DEPTH)):
        sv, big = saved[l], bigs[l]

        def ffn_back(tag, xin, dy, gate, up, scatter=None):
            (dxi, dgain, nb, act, dgate, dup), got = _ffn_bwd(xin, dy, gate, up, row(tag + '_norm', l), big[tag + '_w_gate'],
                                                             big[tag + '_w_up'], big[tag + '_w_down'], f"{tag}_bwd_{l}", scatter)
            gs[tag + '_norm'][l] = dgain[0]
            names = (tag + '_w_gate', tag + '_w_up', tag + '_w_down')
            if tag == 'ffn2':
                group, _ = _matmul_tn(dgate, nb, 1.0, f"{tag}_dwg_{l}", None, 0, 3)
                group, _ = _matmul_tn(dup, nb, 1.0, f"{tag}_dwu_{l}", group, 1, 3)
                group, _ = _matmul_tn(act, dy, 0.5, f"{tag}_dwd_{l}", group, 2, 3)
                return dxi, {names: group}, got
            group, _ = _matmul_tn(dgate, nb, 1.0, f"{tag}_dwg_{l}", None, 0, 2)
            group, _ = _matmul_tn(dup, nb, 1.0, f"{tag}_dwu_{l}", group, 1, 2)
            ex.grads_ready(l, 'first_gu', {names[:2]: group})
            down, arrived = _matmul_tn(act, dy, 0.5, f"{tag}_dwd_{l}", None, 0, 1, ex.scatter_behind_dwd(l))
            ex.scattered_behind_dwd(l, arrived)
            ex.grads_ready(l, 'first_d', {names[2:]: down})
            return dxi, None, got

        dx, rest_grads, _ = ffn_back('ffn2', sv['x2'], dx, sv['g2'], sv['u2'])
        do_a, do_b, dga, dgb, dwo, delta = _post_bwd(dx, sv['o_a'], sv['o_b'], row('mla_out_norm', l), row('swa_out_norm', l),
                                              big['w_o'], f"post_bwd_{l}")
        gs['mla_out_norm'][l], gs['swa_out_norm'][l] = dga[0], dgb[0]
        rest_grads['w_o'] = dwo.reshape(N_SHARD, MIX_WIDTH // N_SHARD, D_MODEL)
        dq_a, dk_a, dv_a, got = _mla_bwd(sv['q_a'], sv['k_a'], sv['v_a'], do_a, sv['lse'], delta, f"mla_bwd_{l}",
                                         ex.scatter_behind_mla(l))
        ex.scattered_behind_mla(l, got)
        dq_b, dkpad, dvpad, dsink = _swa_bwd(sv['q_b'], sv['kpad'], sv['vpad'], sv['sinks'], sv['o_b'], sv['lse_b'], do_b,
                                             f"swa_bwd_{l}")
        gs['swa_sinks'][l] = dsink[:, :SWA_GROUP, 0].reshape(SWA_HEADS)
        cts = [dq_a, dk_a, dv_a, dq_b, dkpad[:, BLOCK:], dvpad[:, BLOCK:]]
        outs = _pre_bwd(sv['x1'], dx, cts, sv['gains'], *sv['mixer_w'], cos, sin, f"pre_bwd_{l}")
        dx = outs[0]
        for n, val in zip(_MIXER_GAINS, outs[1:8]):
            gs[n][l] = val[0]
        rest_grads['w_in'], rest_grads['mla_w_q_b'], rest_grads['mla_w_kv_b'] = outs[8:11]
        ex.grads_ready(l, 'rest', rest_grads)
        scatter = ex.scatter_behind_ffn1(l)
        dx, _, got = ffn_back('ffn1', sv['x0'], dx, sv['g1'], sv['u1'], scatter)
        ex.scattered_behind_ffn1(l, got)
    return loss, dx, gs


class _Exchange:
    def __init__(self, weights, c, chip):
        halves_of = lambda a: a.reshape(a.shape[:-2] + (2, a.shape[-2] // 2, a.shape[-1]))
        self.halves_of, self.c, self.chip = halves_of, c, chip
        self.mine = [[halves_of(jnp.concatenate([weights[n][l].astype(BF16) for n in group], axis=0))
                      for group in FIRST_GROUPS + REST_GROUPS] for l in range(DEPTH)]
        self.ahead, self.waiting, self.begun, self.received = {}, {}, {}, {}

    def _assembled(self, l, which, gathered):
        groups, base = (FIRST_GROUPS, 0) if which == 'first' else (REST_GROUPS, N_FIRST)
        big = {}
        for gi, group in enumerate(groups):
            offs = _group_row_offsets(group)
            _, rh, w = self.mine[l][base + gi].shape
            col_sharded = BIG[group[0]][1] == 1
            full = _assemble(gathered[gi].reshape(N_SHARD, 1, 2 * rh, w), self.mine[l][base + gi].reshape(1, 2 * rh, w),
                             f"assemble_{which}{gi}_{l}", col_sharded, MIXER_ORDERS.get(group[0]))
            for i, n in enumerate(group):
                rows = offs[i + 1] - offs[i]
                if len(group) > 1 and offs[i] % rows == 0:
                    big[n] = ((full, (None, rows, N_SHARD * w), (0, offs[i] // rows, 0)) if col_sharded else
                              (full, (None, N_SHARD, rows, w), (0, 0, offs[i] // rows, 0)))
                elif col_sharded:
                    big[n] = full[0, offs[i]:offs[i + 1]]
                else:
                    big[n] = full[0, :, offs[i]:offs[i + 1]].reshape(BIG[n][0])
        return big

    def first_weights(self, l):
        got = self.ahead[l][:N_FIRST] if l in self.ahead else _all_gather_halves(self.mine[l][:N_FIRST], f"gather_first_{l}")
        return self._assembled(l, 'first', got)

    def gather_behind_ffn1(self, l):
        return None if l in self.ahead else self.mine[l][N_FIRST:]

    def rest_weights(self, l, got):
        return self._assembled(l, 'rest', self.ahead[l][N_FIRST:] if l in self.ahead else got)

    def gather_behind_mla(self, l):
        return self.mine[l + 1] if l + 1 < DEPTH else None

    def gathered_behind_mla(self, l, got):
        if got:
            self.ahead[l + 1] = got

    def grads_ready(self, l, which, grads):
        groups = GRAD_SETS[which]
        parts = [self.halves_of(grads[group] if group in grads else jnp.concatenate([grads[n] for n in group], axis=1))
                 for group in groups]
        if l > 0:
            self.waiting[(l, which)] = parts
        else:
            self._begin(l, [(which, parts)])

    def _begin(self, l, sets):
        tag = sets[0][0] if len(sets) == 1 else "all"
        from_sibling = _sibling_exchange(sum((parts for _, parts in sets), []), f"swap_{tag}_{l}", True)
        at = 0
        for which, parts in sets:
            chip_sums = []
            for gi, p in enumerate(parts):
                got = from_sibling[at]
                at += 1
                kept = lax.dynamic_index_in_dim(p, self.c, axis=1, keepdims=False)
                rows = N_SHARD * p.shape[2]
                pair = _sum_blocks([kept.reshape(rows, -1), got.reshape(rows, -1)], BF16, f"sum_pair_{which}{gi}_{l}")
                chip_sums.append(pair.reshape(got.shape))
            self.begun[(l, which)] = chip_sums

    def scatter_behind_mla(self, l):
        if l + 1 >= DEPTH:
            return None
        self._begin(l + 1, [(which, self.waiting.pop((l + 1, which))) for which in GRAD_SET_ORDER])
        return sum((self.begun[(l + 1, which)] for which in GRAD_SET_ORDER), [])

    def scattered_behind_mla(self, l, got):
        at = 0
        for which in GRAD_SET_ORDER if got else ():
            self.received[(l + 1, which)] = got[at:at + len(GRAD_SETS[which])]
            at += len(GRAD_SETS[which])

    def scatter_behind_dwd(self, l):
        return self.begun[(l, 'first_gu')] if l == 0 else None

    def scattered_behind_dwd(self, l, got):
        if got:
            self.received[(l, 'first_gu')] = got

    def scatter_behind_ffn1(self, l):
        return self.begun[(l, 'rest')] if l == 0 else None

    def scattered_behind_ffn1(self, l, got):
        if got:
            self.received[(l, 'rest')] = got

    def reduced(self):
        keys = sorted(self.begun)
        for key in keys:
            if key not in self.received:
                self.received[key] = _scatter_to_chips(self.begun[key], f"scatter_{key[1]}_{key[0]}")
        halves = []
        for l, which in keys:
            for gi, (cs, got) in enumerate(zip(self.begun[(l, which)], self.received[(l, which)])):
                own = lax.dynamic_index_in_dim(cs, self.chip, axis=0, keepdims=False)
                halves.append(_sum_blocks([own, got[0], got[1], got[2]], F32, f"sum_chips_{which}{gi}_{l}"))
        others = _sibling_exchange(halves, "share_halves", False)
        per_layer, at = {}, 0
        for l, which in keys:
            for group in GRAD_SETS[which]:
                mine_h, other_h = halves[at], others[at]
                at += 1
                full = jnp.where(self.c == 0, jnp.concatenate([mine_h, other_h]), jnp.concatenate([other_h, mine_h]))
                offs = _group_row_offsets(group)
                for i, n in enumerate(group):
                    per_layer[(n, l)] = full[offs[i]:offs[i + 1]]
        return {n: jnp.stack([per_layer[(n, l)] for l in range(DEPTH)]) for n in BIG_NAMES}


def kernel(x, ffn1_norm, ffn1_w_gate, ffn1_w_up, ffn1_w_down, mix_norm, w_in, mla_q_a_norm, mla_w_q_b, mla_kv_a_norm, mla_w_kv_b, mla_q_norm, mla_k_norm, swa_q_norm, swa_k_norm, swa_sinks, mla_out_norm, swa_out_norm, w_o, ffn2_norm, ffn2_w_gate, ffn2_w_up, ffn2_w_down, loss_target, m_ffn1_norm, m_ffn1_w_gate, m_ffn1_w_up, m_ffn1_w_down, m_mix_norm, m_w_in, m_mla_q_a_norm, m_mla_w_q_b, m_mla_kv_a_norm, m_mla_w_kv_b, m_mla_q_norm, m_mla_k_norm, m_swa_q_norm, m_swa_k_norm, m_swa_sinks, m_mla_out_norm, m_swa_out_norm, m_w_o, m_ffn2_norm, m_ffn2_w_gate, m_ffn2_w_up, m_ffn2_w_down, v_ffn1_norm, v_ffn1_w_gate, v_ffn1_w_up, v_ffn1_w_down, v_mix_norm, v_w_in, v_mla_q_a_norm, v_mla_w_q_b, v_mla_kv_a_norm, v_mla_w_kv_b, v_mla_q_norm, v_mla_k_norm, v_swa_q_norm, v_swa_k_norm, v_swa_sinks, v_mla_out_norm, v_swa_out_norm, v_w_o, v_ffn2_norm, v_ffn2_w_gate, v_ffn2_w_up, v_ffn2_w_down):
    args = dict(locals())
    transposed = lambda a: jnp.swapaxes(a, 1, 2)
    as_kernels_see = lambda n, a: transposed(a) if n in TRANSPOSED else a
    weights = {n: as_kernels_see(n, args[n]) for n in WEIGHT_NAMES}
    mom_m = {n: as_kernels_see(n, args["m_" + n]) for n in WEIGHT_NAMES}
    mom_v = {n: as_kernels_see(n, args["v_" + n]) for n in WEIGHT_NAMES}
    ex = _Exchange(weights, lax.axis_index("c"), 2 * lax.axis_index("x") + lax.axis_index("y"))
    loss, dx, gs = _local_step(x[0], loss_target[0], {n: weights[n] for n in SMALL_NAMES}, ex)

    small_flat = jnp.concatenate([jnp.stack(gs[n]).reshape(-1) for n in SMALL_NAMES] + [loss[0, :1]])
    n_small = small_flat.shape[0]
    lanes = -(-n_small // (8 * 128)) * 128
    small_sum = _all_reduce_small(jnp.pad(small_flat, (0, 8 * lanes - n_small)).reshape(8, lanes), "reduce_small").reshape(-1)
    loss_out = small_sum[n_small - 1]
    grads = ex.reduced()

    packed = lambda d: jnp.pad(jnp.concatenate([d[n].reshape(-1) for n in SMALL_NAMES]), (0, 8 * lanes - n_small + 1)).reshape(8, lanes)
    small_out = _adamw(packed(weights), small_sum.reshape(8, lanes), packed(mom_m), packed(mom_v), "adamw_small")
    deltas, new_m, new_v = {}, {}, {}
    off = 0
    for n in SMALL_NAMES:
        cnt = int(np.prod(weights[n].shape))
        grads[n], deltas[n], new_m[n], new_v[n] = (a.reshape(-1)[off:off + cnt].reshape(weights[n].shape)
                                                    for a in (small_sum, *small_out))
        off += cnt
    for n in BIG_NAMES:
        shp = weights[n].shape
        two_d = (shp[0] * shp[1], shp[2])
        d, nm, nv = _adamw(weights[n].reshape(two_d), grads[n].reshape(two_d), mom_m[n].reshape(two_d),
                           mom_v[n].reshape(two_d), f"adamw_{n}")
        deltas[n], new_m[n], new_v[n] = d.reshape(shp), nm.reshape(shp), nv.reshape(shp)
    for n in TRANSPOSED:
        grads[n], deltas[n], new_m[n], new_v[n] = (transposed(a) for a in (grads[n], deltas[n], new_m[n], new_v[n]))

    return (loss_out, dx[None], *[grads[n] for n in WEIGHT_NAMES], *[deltas[n] for n in WEIGHT_NAMES],
            *[new_m[n] for n in WEIGHT_NAMES], *[new_v[n] for n in WEIGHT_NAMES])
```

```python
import numpy as np
import jax
import jax.numpy as jnp
from jax import lax
from jax.experimental import pallas as pl
from jax.experimental.pallas import tpu as pltpu

F32 = jnp.float32
BF16 = jnp.bfloat16

D_MODEL = 1024
DEPTH = 2
EPS = 1e-6
ROPE_THETA = 10000.0
BLOCK = 128
MLA_HEADS = 4
MLA_Q_RANK = 256
MLA_KV_RANK = 128
MLA_NOPE = 128
MLA_ROPE = 64
MLA_V = 128
MLA_QK = MLA_NOPE + MLA_ROPE
MLA_WIDTH = MLA_HEADS * MLA_V
SWA_HEADS = 8
SWA_KV_HEADS = 2
SWA_GROUP = SWA_HEADS // SWA_KV_HEADS
SWA_HEAD_DIM = 64
SWA_WIDTH = SWA_HEADS * SWA_HEAD_DIM
MIX_WIDTH = MLA_WIDTH + SWA_WIDTH
IN_SPLITS = (MLA_Q_RANK, MLA_KV_RANK, MLA_ROPE, SWA_WIDTH, SWA_KV_HEADS * SWA_HEAD_DIM, SWA_KV_HEADS * SWA_HEAD_DIM)
IN_COLS = sum(IN_SPLITS)
IN_OFFS = tuple(int(v) for v in np.cumsum((0,) + IN_SPLITS))
D_FF = 2816
MLA_SCALE = MLA_QK ** -0.5
LOG2E = 1.4426950408889634
LN2 = 0.6931471805599453
MLA_QSCALE = MLA_SCALE * LOG2E
SWA_SCALE = SWA_HEAD_DIM ** -0.5
NEG = -1e30

ADAM_LR = 0.001
ADAM_B1 = 0.9
ADAM_B2 = 0.999
ADAM_EPS = 1e-08
ADAM_WD = 0.01
ADAM_STEP = 10

N_SHARD = 4
N_DEV = 8
VMEM_LIMIT = 56 * 1024 * 1024
MESH = pl.DeviceIdType.MESH

WEIGHT_NAMES = ['ffn1_norm', 'ffn1_w_gate', 'ffn1_w_up', 'ffn1_w_down', 'mix_norm', 'w_in', 'mla_q_a_norm', 'mla_w_q_b',
                'mla_kv_a_norm', 'mla_w_kv_b', 'mla_q_norm', 'mla_k_norm', 'swa_q_norm', 'swa_k_norm', 'swa_sinks',
                'mla_out_norm', 'swa_out_norm', 'w_o', 'ffn2_norm', 'ffn2_w_gate', 'ffn2_w_up', 'ffn2_w_down']
TRANSPOSED = ('ffn1_w_gate', 'ffn1_w_up', 'ffn2_w_gate', 'ffn2_w_up')
BIG = {'ffn1_w_gate': ((D_FF, D_MODEL), 0), 'ffn1_w_up': ((D_FF, D_MODEL), 0), 'ffn1_w_down': ((D_FF, D_MODEL), 0),
       'w_in': ((D_MODEL, IN_COLS), 1), 'mla_w_q_b': ((MLA_Q_RANK, MLA_HEADS * MLA_QK), 1),
       'mla_w_kv_b': ((MLA_KV_RANK, MLA_HEADS * (MLA_NOPE + MLA_V)), 1), 'w_o': ((MIX_WIDTH, D_MODEL), 0),
       'ffn2_w_gate': ((D_FF, D_MODEL), 0), 'ffn2_w_up': ((D_FF, D_MODEL), 0), 'ffn2_w_down': ((D_FF, D_MODEL), 0)}
BIG_NAMES = [n for n in WEIGHT_NAMES if n in BIG]
SMALL_NAMES = [n for n in WEIGHT_NAMES if n not in BIG]

_pallas_call = pl.pallas_call


def _params(**kw):
    return pltpu.CompilerParams(vmem_limit_bytes=VMEM_LIMIT, **kw)


def _full(shape):
    n = len(shape)
    return pl.BlockSpec(shape, lambda *_: (0,) * n)


def _resident(shape):
    n = len(shape)
    return pl.BlockSpec(shape, lambda *_: (0,) * n, pipeline_mode=pl.Buffered(1))


@jax.custom_vjp
def _mm(a, w):
    return jnp.dot(a.astype(BF16), w, preferred_element_type=F32)


def _mm_fwd(a, w):
    return _mm(a, w), w


def _mm_bwd(w, dy):
    return lax.dot_general(dy.astype(BF16), w, (((1,), (1,)), ((), ())), preferred_element_type=F32), None


_mm.defvjp(_mm_fwd, _mm_bwd)


def _dot_nt(a, b):
    return lax.dot_general(a, b, (((1,), (1,)), ((), ())), preferred_element_type=F32)


def _dot_tn(a, b):
    return lax.dot_general(a, b, (((0,), (0,)), ((), ())), preferred_element_type=F32)


def _rms(t, g):
    return t * lax.rsqrt(jnp.mean(t * t, axis=-1, keepdims=True) + EPS) * g


def _sigmoid(z):
    return 1.0 / (1.0 + jnp.exp(-z))


def _row_tile(s, want):
    return min(want, s)


def _divisor_tile(rows, cap):
    return max(d for d in range(16, min(rows, cap) + 1, 16) if rows % d == 0)


FF_CHUNK = D_FF


def _weight_operand(w):
    if isinstance(w, tuple):
        arr, block, index = w
        return arr, pl.BlockSpec(block, lambda *_: index, pipeline_mode=pl.Buffered(1))
    return w, _resident(w.shape)


def _weight_rows(ref, start, n):
    if len(ref.shape) == 2:
        return ref[start:start + n, :]
    per = ref.shape[1]
    return ref[start // per:(start + n) // per].reshape(n, ref.shape[2])


def _ffn_fwd(x, g, wg, wu, wd, name, gather=None):
    s = x.shape[0]
    tm = _row_tile(s, 256)
    steps = s // tm
    ng = len(gather) if gather else 0
    (wg, wg_spec), (wu, wu_spec), (wd, wd_spec) = _weight_operand(wg), _weight_operand(wu), _weight_operand(wd)

    def body(x_ref, g_ref, wg_ref, wu_ref, wd_ref, *rest):
        g_ins, (y_ref, gate_ref, up_ref), g_outs, sems = rest[:ng], rest[ng:ng + 3], rest[ng + 3:2 * ng + 3], rest[2 * ng + 3:]
        if ng:
            _run_stages_at(_gather_stages(g_ins, g_outs, *sems), [(0,), (steps * 3 // 8,), (steps * 11 // 16,), (steps - 1,)])
        xv = x_ref[...]
        nb = _rms(xv, g_ref[...]).astype(BF16)
        acc = xv
        for c in range(0, D_FF, FF_CHUNK):
            gate = _dot_nt(nb, _weight_rows(wg_ref, c, FF_CHUNK))
            up = _dot_nt(nb, _weight_rows(wu_ref, c, FF_CHUNK))
            gate_ref[:, c:c + FF_CHUNK] = gate.astype(BF16)
            up_ref[:, c:c + FF_CHUNK] = up.astype(BF16)
            act = (gate * _sigmoid(gate) * up).astype(BF16)
            acc = acc + 0.5 * jnp.dot(act, _weight_rows(wd_ref, c, FF_CHUNK), preferred_element_type=F32)
        y_ref[...] = acc

    outs = _pallas_call(
        body, name=name, grid=(steps,),
        in_specs=[pl.BlockSpec((tm, D_MODEL), lambda i: (i, 0)), _full((1, D_MODEL)), wg_spec, wu_spec, wd_spec] + [_HBM] * ng,
        out_specs=[pl.BlockSpec((tm, D_MODEL), lambda i: (i, 0)), pl.BlockSpec((tm, D_FF), lambda i: (i, 0)),
                   pl.BlockSpec((tm, D_FF), lambda i: (i, 0))] + [_HBM] * ng,
        out_shape=[jax.ShapeDtypeStruct((s, D_MODEL), F32), jax.ShapeDtypeStruct((s, D_FF), BF16),
                   jax.ShapeDtypeStruct((s, D_FF), BF16)] + (_gather_out_shapes(gather) if ng else []),
        scratch_shapes=_exchange_scratch(8, ng) if ng else [],
        compiler_params=_params(dimension_semantics=("arbitrary",)),
    )(x, g, wg, wu, wd, *(gather or []))
    return outs[0], outs[1], outs[2], outs[3:]


def _ffn_bwd(x, dy, gate, up, g, wg, wu, wd, name, scatter=None):
    s = x.shape[0]
    tm = _row_tile(s, 256)
    steps = s // tm
    ng = len(scatter) if scatter else 0
    (wg, wg_spec), (wu, wu_spec), (wd, wd_spec) = _weight_operand(wg), _weight_operand(wu), _weight_operand(wd)

    def body(x_ref, dy_ref, gate_ref, up_ref, g_ref, wg_ref, wu_ref, wd_ref, *rest):
        c_ins, (dx_ref, dgain_ref, n_ref, act_ref, dgate_ref, dup_ref) = rest[:ng], rest[ng:ng + 6]
        c_outs, sems = rest[ng + 6:2 * ng + 6], rest[2 * ng + 6:]
        if ng:
            _run_stages_at(_scatter_stages(c_ins, c_outs, *sems), [(0,), (steps - 1,)])
        i = pl.program_id(0)
        xv = x_ref[...]
        dyv = dy_ref[...]
        gv = g_ref[...]
        r = lax.rsqrt(jnp.mean(xv * xv, axis=-1, keepdims=True) + EPS)
        xh = xv * r
        n_ref[...] = (xh * gv).astype(BF16)
        dyh = (0.5 * dyv).astype(BF16)
        dn = jnp.zeros_like(xv)
        for c in range(0, D_FF, FF_CHUNK):
            dact = _dot_nt(dyh, _weight_rows(wd_ref, c, FF_CHUNK))
            gt = gate_ref[:, c:c + FF_CHUNK].astype(F32)
            u = up_ref[:, c:c + FF_CHUNK].astype(F32)
            sg = _sigmoid(gt)
            sl = gt * sg
            act_ref[:, c:c + FF_CHUNK] = (sl * u).astype(BF16)
            dup = (dact * sl).astype(BF16)
            dgate = (dact * u * (sg * (1.0 + gt * (1.0 - sg)))).astype(BF16)
            dup_ref[:, c:c + FF_CHUNK] = dup
            dgate_ref[:, c:c + FF_CHUNK] = dgate
            dn = (dn + jnp.dot(dgate, _weight_rows(wg_ref, c, FF_CHUNK), preferred_element_type=F32)
                  + jnp.dot(dup, _weight_rows(wu_ref, c, FF_CHUNK), preferred_element_type=F32))
        part = jnp.sum(dn * xh, axis=0, keepdims=True)

        @pl.when(i == 0)
        def _():
            dgain_ref[...] = part

        @pl.when(i > 0)
        def _():
            dgain_ref[...] += part

        dxh = dn * gv
        dx_ref[...] = dyv + r * (dxh - xh * jnp.mean(dxh * xh, axis=-1, keepdims=True))

    row = lambda w: pl.BlockSpec((tm, w), lambda i: (i, 0))
    outs = _pallas_call(
        body, name=name, grid=(steps,),
        in_specs=[row(D_MODEL), row(D_MODEL), row(D_FF), row(D_FF), _full((1, D_MODEL)), wg_spec, wu_spec, wd_spec]
        + [_HBM] * ng,
        out_specs=[row(D_MODEL), _full((1, D_MODEL)), row(D_MODEL), row(D_FF), row(D_FF), row(D_FF)] + [_HBM] * ng,
        out_shape=[jax.ShapeDtypeStruct((s, D_MODEL), F32), jax.ShapeDtypeStruct((1, D_MODEL), F32),
                   jax.ShapeDtypeStruct((s, D_MODEL), BF16), jax.ShapeDtypeStruct((s, D_FF), BF16),
                   jax.ShapeDtypeStruct((s, D_FF), BF16), jax.ShapeDtypeStruct((s, D_FF), BF16)]
        + (_scatter_out_shapes(scatter) if ng else []),
        scratch_shapes=_exchange_scratch(3, ng) if ng else [],
        compiler_params=_params(dimension_semantics=("arbitrary",)),
    )(x, dy, gate, up, g, wg, wu, wd, *(scatter or []))
    return outs[:6], outs[6:]


def _store_col_shards(o_ref, acc, first_shard, n_here, width):
    for q in range(n_here):
        o_ref[q] = acc[:, (first_shard + q) * width:(first_shard + q + 1) * width].astype(BF16)


def _matmul_tn(a, b, scale, name, group=None, slot=0, slots=1, scatter=None):
    t, m = a.shape
    n = b.shape[1]
    tk = _row_tile(t, 2048)
    tn = n // 2
    nk = t // tk
    per = m // N_SHARD
    earlier = [] if group is None else [group]
    ne, ng = len(earlier), len(scatter) if scatter else 0

    def body(a_ref, b_ref, *rest):
        c_ins, o_ref, c_outs = rest[ne:ne + ng], rest[ne + ng], rest[ne + ng + 1:ne + 2 * ng + 1]
        acc_ref, sems = rest[ne + 2 * ng + 1], rest[ne + 2 * ng + 2:]
        if ng:
            _run_stages_at(_scatter_stages(c_ins, c_outs, *sems), [(0, 0), (n // tn - 1, nk - 1)])
        k = pl.program_id(1)
        bv = b_ref[...]
        if scale != 1.0:
            bv = bv.astype(F32) * scale
        part = _dot_tn(a_ref[...].astype(BF16), bv.astype(BF16))

        @pl.when(k == 0)
        def _():
            acc_ref[...] = part

        @pl.when(k > 0)
        def _():
            acc_ref[...] += part

        @pl.when(k == nk - 1)
        def _():
            for sh in range(N_SHARD):
                o_ref[sh] = acc_ref[sh * per:(sh + 1) * per, :].astype(BF16)

    outs = _pallas_call(
        body, name=name, grid=(n // tn, nk),
        in_specs=[pl.BlockSpec((tk, m), lambda j, k: (k, 0)), pl.BlockSpec((tk, tn), lambda j, k: (k, j))]
        + [pl.BlockSpec(memory_space=pl.ANY)] * ne + [_HBM] * ng,
        out_specs=[pl.BlockSpec((N_SHARD, per, tn), lambda j, k: (0, slot, j))] + [_HBM] * ng,
        out_shape=[jax.ShapeDtypeStruct((N_SHARD, slots * per, n), BF16)] + (_scatter_out_shapes(scatter) if ng else []),
        scratch_shapes=[pltpu.VMEM((m, tn), F32)] + (_exchange_scratch(3, ng) if ng else []),
        input_output_aliases={2: 0} if earlier else {},
        compiler_params=_params(dimension_semantics=("arbitrary", "arbitrary")),
    )(a, b, *earlier, *(scatter or []))
    return outs[0], outs[1:]


_HALF = SWA_HEAD_DIM // 2
_IN_ORDER = (list(range(0, IN_OFFS[2]))
             + [IN_OFFS[3] + SWA_HEAD_DIM * h + i for h in range(SWA_HEADS) for i in range(_HALF)]
             + [IN_OFFS[3] + SWA_HEAD_DIM * h + _HALF + i for h in range(SWA_HEADS) for i in range(_HALF)]
             + list(range(IN_OFFS[5], IN_OFFS[6]))
             + [IN_OFFS[4] + SWA_HEAD_DIM * j + i for j in range(SWA_KV_HEADS) for i in range(_HALF)]
             + [IN_OFFS[4] + SWA_HEAD_DIM * j + _HALF + i for j in range(SWA_KV_HEADS) for i in range(_HALF)]
             + list(range(IN_OFFS[2], IN_OFFS[3])))
_QB_ORDER = ([MLA_QK * h + i for h in range(MLA_HEADS) for i in range(MLA_NOPE)]
             + [MLA_QK * h + MLA_NOPE + i for h in range(MLA_HEADS) for i in range(_HALF)]
             + [MLA_QK * h + MLA_NOPE + _HALF + i for h in range(MLA_HEADS) for i in range(_HALF)])
_KVB_ORDER = ([(MLA_NOPE + MLA_V) * h + i for h in range(MLA_HEADS) for i in range(MLA_NOPE)]
              + [(MLA_NOPE + MLA_V) * h + MLA_NOPE + i for h in range(MLA_HEADS) for i in range(MLA_V)])
MIXER_ORDERS = {'w_in': _IN_ORDER, 'mla_w_q_b': _QB_ORDER, 'mla_w_kv_b': _KVB_ORDER}
_P_CQ, _P_CKV, _P_QA, _P_QB, _P_VS, _P_KA, _P_KB, _P_PE = (int(v) for v in np.cumsum(
    (0, MLA_Q_RANK, MLA_KV_RANK, SWA_WIDTH // 2, SWA_WIDTH // 2, IN_SPLITS[5], IN_SPLITS[4] // 2, IN_SPLITS[4] // 2)))


def _runs(order):
    out, start = [], 0
    for i in range(1, len(order) + 1):
        if i == len(order) or order[i] != order[i - 1] + 1:
            out.append((order[start], i - start))
            start = i
    return out


def _inverse(order):
    inv = [0] * len(order)
    for new, old in enumerate(order):
        inv[old] = new
    return inv


def _take_cols(a, order):
    return jnp.concatenate([a[..., st:st + w] for st, w in _runs(order)], axis=-1)


def _segment_matrix(n, seg):
    return (lax.broadcasted_iota(jnp.int32, (n, n), 0) // seg == lax.broadcasted_iota(jnp.int32, (n, n), 1) // seg).astype(BF16)


def _cmm2(t, b):
    hi = t.astype(BF16)
    lo = (t - hi.astype(F32)).astype(BF16)
    return jnp.dot(hi, b, preferred_element_type=F32) + jnp.dot(lo, b, preferred_element_type=F32)


@jax.custom_vjp
def _cmm(t, b, bt):
    return jnp.dot(t.astype(BF16), b, preferred_element_type=F32)


def _cmm_fwd(t, b, bt):
    return _cmm(t, b, bt), (b, bt)


def _cmm_bwd(res, dy):
    b, bt = res
    return _cmm(dy, bt, b), None, None


_cmm.defvjp(_cmm_fwd, _cmm_bwd)


def _segsum(t, b):
    return _cmm(t, b, b)


def _rowsum(t, exact=False):
    n = t.shape[-1]
    if exact:
        return _cmm2(t, jnp.ones((n, 128), BF16))
    return _cmm(t, jnp.ones((n, 128), BF16), jnp.ones((128, n), BF16))


def _by_head(vals, width):
    lane = lax.broadcasted_iota(jnp.int32, (vals[0].shape[0], len(vals) * width), 1)
    out = vals[-1]
    for hd in range(len(vals) - 2, -1, -1):
        out = jnp.where(lane < (hd + 1) * width, vals[hd], out)
    return out


def _rope2(a, b, cos, sin):
    return a * cos - b * sin, b * cos + a * sin


def _pre_math(x, gm, gqa, gkva, gq, gk, gsq, gsk, taps, win, wqb, wkvb, cos, sin):
    h = _rms(x, gm)
    proj = _mm(h, win)
    if taps is not None:
        proj = proj + taps[0]
    cqn = _rms(proj[:, _P_CQ:_P_CKV], gqa)
    qa_all = _mm(cqn, wqb)
    ckvn = _rms(proj[:, _P_CKV:_P_QA], gkva)
    kv_all = _mm(ckvn, wkvb)
    if taps is not None:
        qa_all = qa_all + taps[1]
        kv_all = kv_all + taps[2]
    nh, hw = MLA_HEADS, MLA_HEADS * _HALF
    seg_mla = _segment_matrix(hw, _HALF)
    tile = lambda g, n: jnp.concatenate([g] * n, axis=-1)
    c4, s4 = cos[:, :hw], sin[:, :hw]

    def mla_heads(nope, r1, r2, gain):
        rr = r1 * r1 + r2 * r2
        lane_head = lax.broadcasted_iota(jnp.int32, (hw, nh * MLA_NOPE), 0) // _HALF
        spread = (lane_head == lax.broadcasted_iota(jnp.int32, (hw, nh * MLA_NOPE), 1) // MLA_NOPE).astype(BF16)
        rope_on_nope = _cmm(rr, spread, spread.T)
        ss_nope = [_rowsum(jnp.square(nope[:, hd * MLA_NOPE:(hd + 1) * MLA_NOPE])) for hd in range(nh)]
        rinv = [lax.rsqrt((ss_nope[hd] + rope_on_nope[:, hd * MLA_NOPE:(hd + 1) * MLA_NOPE]) * (1.0 / MLA_QK) + EPS)
                for hd in range(nh)]
        rl = lax.rsqrt((_segsum(rr, seg_mla) + _by_head(ss_nope, _HALF)) * (1.0 / MLA_QK) + EPS)
        o1, o2 = _rope2(r1 * rl * tile(gain[:, MLA_NOPE:MLA_NOPE + _HALF], nh), r2 * rl * tile(gain[:, MLA_NOPE + _HALF:], nh), c4, s4)
        return [jnp.concatenate([nope[:, hd * MLA_NOPE:(hd + 1) * MLA_NOPE] * rinv[hd] * gain[:, :MLA_NOPE],
                                 o1[:, hd * _HALF:(hd + 1) * _HALF], o2[:, hd * _HALF:(hd + 1) * _HALF]], axis=-1)
                for hd in range(nh)]

    q_a = mla_heads(qa_all[:, :nh * MLA_NOPE], qa_all[:, nh * MLA_NOPE:nh * MLA_NOPE + hw], qa_all[:, nh * MLA_NOPE + hw:], gq)
    pe1, pe2 = proj[:, _P_PE:_P_PE + _HALF], proj[:, _P_PE + _HALF:_P_PE + 2 * _HALF]
    k_a = mla_heads(kv_all[:, :nh * MLA_NOPE], tile(pe1, nh), tile(pe2, nh), gk)
    v_a = [kv_all[:, nh * MLA_NOPE + hd * MLA_V:nh * MLA_NOPE + (hd + 1) * MLA_V] for hd in range(nh)]

    def swa_heads(a, b, gain, n):
        w = n * _HALF
        r = lax.rsqrt(_segsum(a * a + b * b, _segment_matrix(w, _HALF)) * (1.0 / SWA_HEAD_DIM) + EPS)
        o1, o2 = _rope2(a * r * tile(gain[:, :_HALF], n), b * r * tile(gain[:, _HALF:], n), cos[:, :w], sin[:, :w])
        return [jnp.concatenate([o1[:, hd * _HALF:(hd + 1) * _HALF], o2[:, hd * _HALF:(hd + 1) * _HALF]], axis=-1) for hd in range(n)]

    q_b = swa_heads(proj[:, _P_QA:_P_QB], proj[:, _P_QB:_P_VS], gsq, SWA_HEADS)
    k_b = swa_heads(proj[:, _P_KA:_P_KB], proj[:, _P_KB:_P_PE], gsk, SWA_KV_HEADS)
    v_b = [proj[:, _P_VS + j * SWA_HEAD_DIM:_P_VS + (j + 1) * SWA_HEAD_DIM] for j in range(SWA_KV_HEADS)]
    return (q_a, k_a, v_a, q_b, k_b, v_b), (h, cqn, ckvn)


_PRE_GAIN_WIDTHS = (D_MODEL, MLA_Q_RANK, MLA_KV_RANK, MLA_QK, MLA_QK, SWA_HEAD_DIM, SWA_HEAD_DIM)
_PRE_HEADS = ((MLA_HEADS, MLA_QK), (MLA_HEADS, MLA_QK), (MLA_HEADS, MLA_V),
              (SWA_HEADS, SWA_HEAD_DIM), (SWA_KV_HEADS, SWA_HEAD_DIM), (SWA_KV_HEADS, SWA_HEAD_DIM))


def _pre_fwd(x, gains, win, wqb, wkvb, cos, sin, name):
    s = x.shape[0]
    tm = _row_tile(s, 512)

    def body(x_ref, *refs):
        g_refs, (win_ref, wqb_ref, wkvb_ref, cos_ref, sin_ref), out_refs = refs[:7], refs[7:12], refs[12:]
        outs, _ = _pre_math(x_ref[...], *[g[...] for g in g_refs], None, win_ref[...], wqb_ref[...], wkvb_ref[...],
                            cos_ref[...], sin_ref[...])
        for idx, (ref, heads) in enumerate(zip(out_refs, outs)):
            for hd, val in enumerate(heads):
                ref[hd] = (val * MLA_QSCALE if idx == 0 else val).astype(BF16)

    heads_spec = lambda nh, w: pl.BlockSpec((nh, tm, w), lambda i: (0, i, 0))
    return _pallas_call(
        body, name=name, grid=(s // tm,),
        in_specs=[pl.BlockSpec((tm, D_MODEL), lambda i: (i, 0))] + [_full((1, w)) for w in _PRE_GAIN_WIDTHS]
        + [_resident(win.shape), _resident(wqb.shape), _resident(wkvb.shape),
           pl.BlockSpec((tm, SWA_HEADS * _HALF), lambda i: (i, 0)), pl.BlockSpec((tm, SWA_HEADS * _HALF), lambda i: (i, 0))],
        out_specs=[heads_spec(nh, w) for nh, w in _PRE_HEADS],
        out_shape=[jax.ShapeDtypeStruct((nh, s, w), BF16) for nh, w in _PRE_HEADS],
        compiler_params=_params(dimension_semantics=("arbitrary",)),
    )(x, *gains, win, wqb, wkvb, cos, sin)


def _pre_bwd(x, dx_res, cts, gains, win, wqb, wkvb, cos, sin, name):
    s = x.shape[0]
    tm = _row_tile(s, 256)
    tap_widths = (IN_COLS, MLA_HEADS * MLA_QK, MLA_HEADS * (MLA_NOPE + MLA_V))

    def body(x_ref, dxr_ref, *refs):
        ct_refs, g_refs = refs[:6], refs[6:13]
        win_ref, wqb_ref, wkvb_ref, cos_ref, sin_ref = refs[13:18]
        dx_ref, dg_refs, dw_refs, acc_refs = refs[18], refs[19:26], refs[26:29], refs[29:32]
        i = pl.program_id(0)
        win_v, wqb_v, wkvb_v, cos_v, sin_v = win_ref[...], wqb_ref[...], wkvb_ref[...], cos_ref[...], sin_ref[...]

        def f(xv, gm, gqa, gkva, gq, gk, gsq, gsk, t0, t1, t2):
            return _pre_math(xv, gm, gqa, gkva, gq, gk, gsq, gsk, (t0, t1, t2), win_v, wqb_v, wkvb_v, cos_v, sin_v)

        taps = [jnp.zeros((tm, w), F32) for w in tap_widths]
        _, vjp, acts = jax.vjp(f, x_ref[...], *[g[...] for g in g_refs], *taps, has_aux=True)
        ct = tuple([ref[hd] for hd in range(nh)] for ref, (nh, _) in zip(ct_refs, _PRE_HEADS))
        grads = vjp(ct)
        dx_ref[...] = grads[0] + dxr_ref[...]
        dws = [_dot_tn(a.astype(BF16), t.astype(BF16)) for a, t in zip(acts, grads[8:11])]

        @pl.when(i == 0)
        def _():
            for ref, val in zip(dg_refs, grads[1:8]):
                ref[...] = val
            for ref, val in zip(acc_refs, dws):
                ref[...] = val

        @pl.when(i > 0)
        def _():
            for ref, val in zip(dg_refs, grads[1:8]):
                ref[...] += val
            for ref, val in zip(acc_refs, dws):
                ref[...] += val

        @pl.when(i == s // tm - 1)
        def _():
            for ref, acc, order in zip(dw_refs, acc_refs, (_IN_ORDER, _QB_ORDER, _KVB_ORDER)):
                _store_col_shards(ref, _take_cols(acc[...], _inverse(order)), 0, N_SHARD, acc.shape[1] // N_SHARD)

    heads_spec = lambda nh, w: pl.BlockSpec((nh, tm, w), lambda i: (0, i, 0))
    row = pl.BlockSpec((tm, D_MODEL), lambda i: (i, 0))
    half = pl.BlockSpec((tm, SWA_HEADS * _HALF), lambda i: (i, 0))
    shard_shapes = [(N_SHARD, w.shape[0], w.shape[1] // N_SHARD) for w in (win, wqb, wkvb)]
    return _pallas_call(
        body, name=name, grid=(s // tm,),
        in_specs=[row, row] + [heads_spec(nh, w) for nh, w in _PRE_HEADS] + [_full((1, w)) for w in _PRE_GAIN_WIDTHS]
        + [_resident(win.shape), _resident(wqb.shape), _resident(wkvb.shape), half, half],
        out_specs=[row] + [_full((1, w)) for w in _PRE_GAIN_WIDTHS] + [_full(shp) for shp in shard_shapes],
        out_shape=[jax.ShapeDtypeStruct((s, D_MODEL), F32)] + [jax.ShapeDtypeStruct((1, w), F32) for w in _PRE_GAIN_WIDTHS]
        + [jax.ShapeDtypeStruct(shp, BF16) for shp in shard_shapes],
        scratch_shapes=[pltpu.VMEM(w.shape, F32) for w in (win, wqb, wkvb)],
        compiler_params=_params(dimension_semantics=("arbitrary",)),
    )(x, dx_res, *cts, *gains, win, wqb, wkvb, cos, sin)


def _post_math(oa, ob, ga, gb, wo):
    mixed = jnp.concatenate([_rms(jnp.concatenate(oa, axis=-1), ga), _rms(jnp.concatenate(ob, axis=-1), gb)], axis=-1)
    return _mm(mixed, wo), mixed


def _post_fwd(x, oa, ob, ga, gb, wo, name):
    s = x.shape[0]
    tm = _row_tile(s, 512)

    def body(x_ref, oa_ref, ob_ref, ga_ref, gb_ref, wo_ref, y_ref):
        y, _ = _post_math([oa_ref[hd] for hd in range(MLA_HEADS)], [ob_ref[hd] for hd in range(SWA_HEADS)],
                          ga_ref[...], gb_ref[...], wo_ref[...])
        y_ref[...] = x_ref[...] + y

    row = pl.BlockSpec((tm, D_MODEL), lambda i: (i, 0))
    return _pallas_call(
        body, name=name, grid=(s // tm,),
        in_specs=[row, pl.BlockSpec((MLA_HEADS, tm, MLA_V), lambda i: (0, i, 0)),
                  pl.BlockSpec((SWA_HEADS, tm, SWA_HEAD_DIM), lambda i: (0, i, 0)),
                  _full((1, MLA_WIDTH)), _full((1, SWA_WIDTH)), _resident(wo.shape)],
        out_specs=row, out_shape=jax.ShapeDtypeStruct((s, D_MODEL), F32),
        compiler_params=_params(dimension_semantics=("arbitrary",)),
    )(x, oa, ob, ga, gb, wo)


def _post_bwd(dy, oa, ob, ga, gb, wo, name):
    s = dy.shape[0]
    tm = _row_tile(s, 512)
    t = _attn_tile(s)

    def body(dy_ref, oa_ref, ob_ref, ga_ref, gb_ref, wo_ref, doa_ref, dob_ref, dga_ref, dgb_ref, dwo_ref, delta_ref, acc_ref):
        i = pl.program_id(0)
        wo_v = wo_ref[...]
        dyv = dy_ref[...]

        def f(oa_l, ob_l, ga_v, gb_v):
            return _post_math(oa_l, ob_l, ga_v, gb_v, wo_v)

        _, vjp, mixed = jax.vjp(f, [oa_ref[hd] for hd in range(MLA_HEADS)], [ob_ref[hd] for hd in range(SWA_HEADS)],
                                ga_ref[...], gb_ref[...], has_aux=True)
        doa, dob, dga, dgb = vjp(dyv)
        for hd in range(MLA_HEADS):
            doa_ref[hd] = doa[hd]
            rows = _as_rows(_rowsum(doa[hd] * oa_ref[hd], exact=True))
            for j in range(tm // t):
                delta_ref[hd, j] = rows[:, j * t:(j + 1) * t]
        for hd in range(SWA_HEADS):
            dob_ref[hd] = dob[hd]
        dwo = _dot_tn(mixed.astype(BF16), dyv.astype(BF16))

        @pl.when(i == 0)
        def _():
            dga_ref[...] = dga
            dgb_ref[...] = dgb
            acc_ref[...] = dwo

        @pl.when(i > 0)
        def _():
            dga_ref[...] += dga
            dgb_ref[...] += dgb
            acc_ref[...] += dwo

        @pl.when(i == s // tm - 1)
        def _():
            dwo_ref[...] = acc_ref[...].astype(BF16)

    row = pl.BlockSpec((tm, D_MODEL), lambda i: (i, 0))
    oa_spec = pl.BlockSpec((MLA_HEADS, tm, MLA_V), lambda i: (0, i, 0))
    ob_spec = pl.BlockSpec((SWA_HEADS, tm, SWA_HEAD_DIM), lambda i: (0, i, 0))
    return _pallas_call(
        body, name=name, grid=(s // tm,),
        in_specs=[row, oa_spec, ob_spec, _full((1, MLA_WIDTH)), _full((1, SWA_WIDTH)), _resident(wo.shape)],
        out_specs=[oa_spec, ob_spec, _full((1, MLA_WIDTH)), _full((1, SWA_WIDTH)), _full(wo.shape),
                   pl.BlockSpec((MLA_HEADS, tm // t, 8, t), lambda i: (0, i, 0, 0))],
        out_shape=[jax.ShapeDtypeStruct((MLA_HEADS, s, MLA_V), F32), jax.ShapeDtypeStruct((SWA_HEADS, s, SWA_HEAD_DIM), F32),
                   jax.ShapeDtypeStruct((1, MLA_WIDTH), F32), jax.ShapeDtypeStruct((1, SWA_WIDTH), F32),
                   jax.ShapeDtypeStruct(wo.shape, BF16), jax.ShapeDtypeStruct((MLA_HEADS, s // t, 8, t), F32)],
        scratch_shapes=[pltpu.VMEM(wo.shape, F32)],
        compiler_params=_params(dimension_semantics=("arbitrary",)),
    )(dy, oa, ob, ga, gb, wo)


def _attn_tile(s):
    return 512 if s >= 2048 else 128


def _as_rows(cols):
    return cols.T[0:8, :]


def _causal_mask(t):
    return lax.broadcasted_iota(jnp.int32, (t, t), 1) <= lax.broadcasted_iota(jnp.int32, (t, t), 0)


def _pipelined_blocks(first, count, last_block, issue, consume, carry, prefetch_after):
    def clamped(j, slot):
        issue(jnp.minimum(j, last_block), slot)

    def pair(jj, c):
        a = first + 2 * jj
        clamped(a + 1, 1)
        c = consume(a, 0, c)
        clamped(a + 2, 0)
        return consume(a + 1, 1, c)

    clamped(first, 0)
    npairs = count // 2
    carry = lax.fori_loop(0, npairs, pair, carry)

    def odd(c):
        c = consume(first + 2 * npairs, 0, c)
        if prefetch_after:
            clamped(first + count, 0)
        return c

    return lax.cond(count - 2 * npairs == 1, odd, lambda c: c, carry)


def _run_stages_at(stages, steps):
    for stage, step in zip(stages, steps):
        here = pl.program_id(0) == step[0]
        for axis in range(1, len(step)):
            here = here & (pl.program_id(axis) == step[axis])
        pl.when(here)(stage)


def _mla_fwd(q, k, v, name, gather=None):
    nh, s, _ = q.shape
    t = _attn_tile(s)
    nq = s // t
    ng = len(gather) if gather else 0

    def body(q_ref, k_ref, v_ref, *rest):
        g_ins, (o_ref, lse_ref), g_outs = rest[:ng], rest[ng:ng + 2], rest[ng + 2:2 * ng + 2]
        (s0_ref, s1_ref), sems = rest[2 * ng + 2:2 * ng + 4], rest[2 * ng + 4:]
        if ng:
            _run_stages_at(_gather_stages(g_ins, g_outs, *sems), [(0, 0), (nh // 2, 0), (nh - 1, 0), (nh - 1, nq - 1)])
        qi = pl.program_id(1)
        qv = q_ref[...]
        s_refs = (s0_ref, s1_ref)

        def rows(j):
            return pl.ds(pl.multiple_of(j * t, t), t)

        def issue(j, slot):
            s_refs[slot][...] = _dot_nt(qv, k_ref[rows(j), :])

        ones = jnp.ones((t, MLA_V), BF16)

        def consume(j, slot, carry, masked=False):
            m, acc = carry
            sc = s_refs[slot][...]
            if masked:
                sc = jnp.where(_causal_mask(t), sc, NEG)
            m_new = jnp.maximum(m, jnp.max(sc, axis=-1, keepdims=True))
            p = jnp.exp2(sc - m_new)
            v_ones = jnp.concatenate([v_ref[rows(j), :], ones], axis=-1)
            acc = jnp.exp2(m - m_new) * acc + jnp.dot(p.astype(BF16), v_ones, preferred_element_type=F32)
            return m_new, acc

        init = (jnp.full((t, 1), NEG, F32), jnp.zeros((t, 2 * MLA_V), F32))
        carry = _pipelined_blocks(0, qi, nq - 1, issue, consume, init, True)
        m, acc = consume(qi, 0, carry, masked=True)
        l = acc[:, MLA_V:]
        o_ref[...] = acc[:, :MLA_V] / l
        lse_ref[...] = _as_rows(m + jnp.log2(l))

    outs = _pallas_call(
        body, name=name, grid=(nh, nq),
        in_specs=[pl.BlockSpec((None, t, MLA_QK), lambda h, i: (h, i, 0)), pl.BlockSpec((None, s, MLA_QK), lambda h, i: (h, 0, 0)),
                  pl.BlockSpec((None, s, MLA_V), lambda h, i: (h, 0, 0))] + [_HBM] * ng,
        out_specs=[pl.BlockSpec((None, t, MLA_V), lambda h, i: (h, i, 0)), pl.BlockSpec((None, None, 8, t), lambda h, i: (h, i, 0, 0))]
        + [_HBM] * ng,
        out_shape=[jax.ShapeDtypeStruct((nh, s, MLA_V), F32), jax.ShapeDtypeStruct((nh, nq, 8, t), F32)]
        + (_gather_out_shapes(gather) if ng else []),
        scratch_shapes=[pltpu.VMEM((t, t), F32)] * 2 + (_exchange_scratch(8, ng) if ng else []),
        compiler_params=_params(dimension_semantics=("arbitrary", "arbitrary")),
    )(q, k, v, *(gather or []))
    return outs[0], outs[1], outs[2:]


def _mla_bwd(q, k, v, do, lse_row, delta_row, name, scatter=None):
    nh, s, _ = q.shape
    t = _attn_tile(s)
    nq = s // t
    ng = len(scatter) if scatter else 0

    def body(q_ref, k_ref, v_ref, do_ref, lse_ref, delta_ref, *rest):
        c_ins, (dq_ref, dk_ref, dv_ref), c_outs = rest[:ng], rest[ng:ng + 3], rest[ng + 3:2 * ng + 3]
        (s0_ref, s1_ref, dp0_ref, dp1_ref), sems = rest[2 * ng + 3:2 * ng + 7], rest[2 * ng + 7:]
        if ng:
            _run_stages_at(_scatter_stages(c_ins, c_outs, *sems), [(0, 0), (nh - 1, nq - 1)])
        kj = pl.program_id(1)
        s_refs, dp_refs = (s0_ref, s1_ref), (dp0_ref, dp1_ref)

        @pl.when(kj == 0)
        def _():
            dq_ref[...] = jnp.zeros_like(dq_ref)

        def rows(i):
            return pl.ds(pl.multiple_of(i * t, t), t)

        def issue(i, slot):
            s_refs[slot][...] = _dot_nt(k_ref[...], q_ref[rows(i), :])
            dp_refs[slot][...] = _dot_nt(v_ref[...], do_ref[rows(i), :].astype(BF16))

        def consume(i, slot, carry, masked=False):
            dk, dv = carry
            p = jnp.exp2(s_refs[slot][...] - lse_ref[i][0:1, :])
            if masked:
                p = jnp.where(lax.broadcasted_iota(jnp.int32, (t, t), 0) <= lax.broadcasted_iota(jnp.int32, (t, t), 1), p, 0.0)
            dv = dv + jnp.dot(p.astype(BF16), do_ref[rows(i), :].astype(BF16), preferred_element_type=F32)
            ds = (p * (dp_refs[slot][...] - delta_ref[i][0:1, :])).astype(BF16)
            dk = dk + jnp.dot(ds, q_ref[rows(i), :], preferred_element_type=F32)
            dq_ref[rows(i), :] += _dot_tn(ds, k_ref[...]) * MLA_SCALE
            return dk, dv

        issue(kj, 0)
        carry = consume(kj, 0, (jnp.zeros((t, MLA_QK), F32), jnp.zeros((t, MLA_V), F32)), masked=True)
        dk, dv = _pipelined_blocks(kj + 1, nq - 1 - kj, nq - 1, issue, consume, carry, False)
        dk_ref[...] = dk * LN2
        dv_ref[...] = dv

    tile = lambda w: pl.BlockSpec((None, t, w), lambda h, j: (h, j, 0))
    whole = lambda w: pl.BlockSpec((None, s, w), lambda h, j: (h, 0, 0))
    rows_spec = pl.BlockSpec((None, nq, 8, t), lambda h, j: (h, 0, 0, 0))
    outs = _pallas_call(
        body, name=name, grid=(nh, nq),
        in_specs=[whole(MLA_QK), tile(MLA_QK), tile(MLA_V), whole(MLA_V), rows_spec, rows_spec] + [_HBM] * ng,
        out_specs=[whole(MLA_QK), tile(MLA_QK), tile(MLA_V)] + [_HBM] * ng,
        out_shape=[jax.ShapeDtypeStruct((nh, s, MLA_QK), F32), jax.ShapeDtypeStruct((nh, s, MLA_QK), F32),
                   jax.ShapeDtypeStruct((nh, s, MLA_V), F32)] + (_scatter_out_shapes(scatter) if ng else []),
        scratch_shapes=[pltpu.VMEM((t, t), F32)] * 4 + (_exchange_scratch(3, ng) if ng else []),
        compiler_params=_params(dimension_semantics=("arbitrary", "arbitrary")),
    )(q, k, v, do, lse_row, delta_row, *(scatter or []))
    return outs[0], outs[1], outs[2], outs[3:]


def _swa_tile(s):
    return min(s, 8 * BLOCK)


def _swa_specs(tq):
    nb = tq // BLOCK
    grp = lambda w: pl.BlockSpec((SWA_GROUP, tq, w), lambda j, i: (j, i, 0))
    main = pl.BlockSpec((None, tq, SWA_HEAD_DIM), lambda j, i: (j, i, 0))
    tail = pl.BlockSpec((None, BLOCK, SWA_HEAD_DIM), lambda j, i: (j, nb * (i + 1), 0))
    sink = pl.BlockSpec((None, SWA_GROUP, 128), lambda j, i: (j, 0, 0))
    return grp, main, tail, sink


def _swa_band_mask(first):
    shape = (SWA_GROUP * BLOCK, 2 * BLOCK)
    q_rel = (lax.broadcasted_iota(jnp.int32, shape, 0) & (BLOCK - 1)) + BLOCK
    k_rel = lax.broadcasted_iota(jnp.int32, shape, 1)
    dist = q_rel - k_rel
    return (dist >= 0) & (dist < BLOCK) & ((k_rel >= BLOCK) | jnp.logical_not(first))


def _swa_sink_column(sink_ref):
    sk = sink_ref[...]
    return jnp.concatenate([jnp.broadcast_to(sk[g:g + 1, 0:1], (BLOCK, 1)) for g in range(SWA_GROUP)], axis=0)


def _swa_fwd(q, kpad, vpad, sinks, name):
    _, s, _ = q.shape
    tq = _swa_tile(s)
    grp, main, tail, sink = _swa_specs(tq)
    d = SWA_HEAD_DIM

    def body(q_ref, km_ref, kt_ref, vm_ref, vt_ref, sink_ref, o_ref, lse_ref):
        i = pl.program_id(1)
        kall = jnp.concatenate([km_ref[...], kt_ref[...]], axis=0)
        vall = jnp.concatenate([vm_ref[...], vt_ref[...]], axis=0)
        sink_col = _swa_sink_column(sink_ref)
        for b in range(tq // BLOCK):
            lo = b * BLOCK
            valid = _swa_band_mask(i == 0 if b == 0 else False)
            q4 = q_ref[:, lo:lo + BLOCK, :].reshape(SWA_GROUP * BLOCK, d)
            sc = jnp.where(valid, _dot_nt(q4, kall[lo:lo + 2 * BLOCK]) * SWA_SCALE, NEG)
            m = jnp.maximum(jnp.max(sc, axis=-1, keepdims=True), sink_col)
            e = jnp.exp(sc - m)
            den = jnp.sum(e, axis=-1, keepdims=True) + jnp.exp(sink_col - m)
            out = jnp.dot((e * (1.0 / den)).astype(BF16), vall[lo:lo + 2 * BLOCK], preferred_element_type=F32)
            o_ref[:, lo:lo + BLOCK, :] = out.reshape(SWA_GROUP, BLOCK, d)
            lse_ref[:, lo:lo + BLOCK, :] = (m + jnp.log(den)).reshape(SWA_GROUP, BLOCK, 1)

    return _pallas_call(
        body, name=name, grid=(SWA_KV_HEADS, s // tq),
        in_specs=[grp(d), main, tail, main, tail, sink], out_specs=[grp(d), grp(1)],
        out_shape=[jax.ShapeDtypeStruct((SWA_HEADS, s, d), F32), jax.ShapeDtypeStruct((SWA_HEADS, s, 1), F32)],
        compiler_params=_params(dimension_semantics=("arbitrary", "arbitrary")),
    )(q, kpad, kpad, vpad, vpad, sinks)


def _swa_bwd(q, kpad, vpad, sinks, o, lse, do, name):
    _, s, _ = q.shape
    tq = _swa_tile(s)
    grp, main, tail, sink = _swa_specs(tq)
    d = SWA_HEAD_DIM

    def body(q_ref, km_ref, kt_ref, vm_ref, vt_ref, sink_ref, o_ref, lse_ref, do_ref, dq_ref, dk_ref, dv_ref, dsink_ref):
        i = pl.program_id(1)
        kall = jnp.concatenate([km_ref[...], kt_ref[...]], axis=0)
        vall = jnp.concatenate([vm_ref[...], vt_ref[...]], axis=0)
        sink_col = _swa_sink_column(sink_ref)

        @pl.when(i == 0)
        def _():
            dk_ref[...] = jnp.zeros_like(dk_ref)
            dv_ref[...] = jnp.zeros_like(dv_ref)
            dsink_ref[...] = jnp.zeros_like(dsink_ref)

        dsink = jnp.zeros((SWA_GROUP * BLOCK, 1), F32)
        for b in range(tq // BLOCK):
            lo = b * BLOCK
            valid = _swa_band_mask(i == 0 if b == 0 else False)
            rows4 = SWA_GROUP * BLOCK
            q4 = q_ref[:, lo:lo + BLOCK, :].reshape(rows4, d)
            do4 = do_ref[:, lo:lo + BLOCK, :].reshape(rows4, d)
            lse4 = lse_ref[:, lo:lo + BLOCK, :].reshape(rows4, 1)
            delta = jnp.sum(do4 * o_ref[:, lo:lo + BLOCK, :].reshape(rows4, d), axis=-1, keepdims=True)
            kb, vb = kall[lo:lo + 2 * BLOCK], vall[lo:lo + 2 * BLOCK]
            do4b = do4.astype(BF16)
            p = jnp.where(valid, jnp.exp(_dot_nt(q4, kb) * SWA_SCALE - lse4), 0.0)
            ds = (p * (_dot_nt(do4b, vb) - delta) * SWA_SCALE).astype(BF16)
            dq_ref[:, lo:lo + BLOCK, :] = jnp.dot(ds, kb, preferred_element_type=F32).reshape(SWA_GROUP, BLOCK, d)
            band = pl.ds(pl.multiple_of(i * tq, BLOCK) + lo, 2 * BLOCK)
            dk_ref[band, :] += _dot_tn(ds, q4)
            dv_ref[band, :] += _dot_tn(p.astype(BF16), do4b)
            dsink = dsink - jnp.exp(sink_col - lse4) * delta
        per_head = [jnp.broadcast_to(jnp.sum(dsink[g * BLOCK:(g + 1) * BLOCK], axis=0, keepdims=True), (1, 128))
                    for g in range(SWA_GROUP)]
        dsink_ref[...] += jnp.concatenate(per_head + [jnp.zeros((8 - SWA_GROUP, 128), F32)], axis=0)

    acc = pl.BlockSpec((None, s + BLOCK, d), lambda j, i: (j, 0, 0))
    return _pallas_call(
        body, name=name, grid=(SWA_KV_HEADS, s // tq),
        in_specs=[grp(d), main, tail, main, tail, sink, grp(d), grp(1), grp(d)],
        out_specs=[grp(d), acc, acc, pl.BlockSpec((None, 8, 128), lambda j, i: (j, 0, 0))],
        out_shape=[jax.ShapeDtypeStruct((SWA_HEADS, s, d), F32),
                   jax.ShapeDtypeStruct((SWA_KV_HEADS, s + BLOCK, d), F32),
                   jax.ShapeDtypeStruct((SWA_KV_HEADS, s + BLOCK, d), F32),
                   jax.ShapeDtypeStruct((SWA_KV_HEADS, 8, 128), F32)],
        compiler_params=_params(dimension_semantics=("arbitrary", "arbitrary")),
    )(q, kpad, kpad, vpad, vpad, sinks, o, lse, do)


def _loss_head(y, target, name):
    s = y.shape[0]
    tm = _row_tile(s, 1024)

    def body(y_ref, t_ref, dy_ref, loss_ref):
        i = pl.program_id(0)
        err = y_ref[...] - t_ref[...]
        dy_ref[...] = err * (1.0 / D_MODEL)
        part = jnp.broadcast_to(0.5 * jnp.sum(jnp.mean(err * err, axis=-1, keepdims=True), axis=0, keepdims=True), (1, 128))

        @pl.when(i == 0)
        def _():
            loss_ref[...] = part

        @pl.when(i > 0)
        def _():
            loss_ref[...] += part

    row = pl.BlockSpec((tm, D_MODEL), lambda i: (i, 0))
    return _pallas_call(
        body, name=name, grid=(s // tm,), in_specs=[row, row], out_specs=[row, _full((1, 128))],
        out_shape=[jax.ShapeDtypeStruct((s, D_MODEL), F32), jax.ShapeDtypeStruct((1, 128), F32)],
        compiler_params=_params(dimension_semantics=("arbitrary",)),
    )(y, target)


def _adamw(w, g, m, v, name):
    rows, cols = w.shape
    tr = _divisor_tile(rows, 512) if rows % 16 == 0 else rows

    def body(w_ref, g_ref, m_ref, v_ref, d_ref, nm_ref, nv_ref):
        gv = g_ref[...]
        nm = ADAM_B1 * m_ref[...] + (1.0 - ADAM_B1) * gv
        nv = ADAM_B2 * v_ref[...] + (1.0 - ADAM_B2) * (gv * gv)
        m_hat = nm / (1.0 - ADAM_B1 ** ADAM_STEP)
        v_hat = nv / (1.0 - ADAM_B2 ** ADAM_STEP)
        d_ref[...] = -ADAM_LR * (m_hat / (jnp.sqrt(v_hat) + ADAM_EPS) + ADAM_WD * w_ref[...])
        nm_ref[...] = nm
        nv_ref[...] = nv

    blk = pl.BlockSpec((tr, cols), lambda i: (i, 0))
    return _pallas_call(
        body, name=name, grid=(rows // tr,), in_specs=[blk] * 4, out_specs=[blk] * 3,
        out_shape=[jax.ShapeDtypeStruct((rows, cols), F32)] * 3,
        compiler_params=_params(dimension_semantics=("arbitrary",)),
    )(w, g, m, v)


def _position():
    return lax.axis_index("x"), lax.axis_index("y"), lax.axis_index("c")


def _remote(src, dst, send_sems, recv_sems, k, to):
    return pltpu.make_async_remote_copy(src_ref=src, dst_ref=dst, send_sem=send_sems.at[k], recv_sem=recv_sems.at[k],
                                        device_id=to, device_id_type=MESH)


_HBM = pl.BlockSpec(memory_space=pltpu.HBM)


def _gather_stages(ins, outs, send_sems, recv_sems):
    na = len(ins)
    x, y, c = _position()
    me, sibling = (x, y, c), (x, y, 1 - c)
    xn, yn, dg = (1 - x, y), (x, 1 - y), (1 - x, 1 - y)

    def slot(a, chip, pc, half=None):
        ref = outs[a].at[4 * chip[0] + 2 * chip[1] + pc]
        if half is None:
            return ref
        rows = ref.shape[0] // 2
        return ref.at[pl.ds(half * rows, rows)]

    def cp(a, k, chip, pc, half, to, src=None):
        dst = slot(a, chip, pc, half)
        return _remote(dst if src is None else src, dst, send_sems, recv_sems, 8 * a + k, to)

    first_hop = [(0, xn), (1, yn)]
    second_hop = [(0, xn, 2, 0, yn), (1, yn, 3, 1, xn)]

    def sends():
        out = []
        for a in range(na):
            out += [cp(a, k, (x, y), c, None, (*to, c), src=ins[a].at[c]) for k, to in first_hop]
            out += [cp(a, fwd_k, frm, c, half, (*to, c)) for _, frm, fwd_k, half, to in second_hop]
            out += [cp(a, 4 + k, frm, c, None, sibling) for k, frm in first_hop]
            out += [cp(a, 6 + half, dg, c, half, sibling) for half in (0, 1)]
        return out

    def stage0():
        for a in range(na):
            for k, to in first_hop:
                cp(a, k, (x, y), c, None, (*to, c), src=ins[a].at[c]).start()

    def stage1():
        for k, frm, fwd_k, half, to in second_hop:
            for a in range(na):
                cp(a, k, frm, c, None, me).wait_recv()
                cp(a, fwd_k, frm, c, half, (*to, c)).start()
                cp(a, 4 + k, frm, c, None, sibling).start()

    def stage2():
        for half in (0, 1):
            for a in range(na):
                cp(a, 2 + half, dg, c, half, me).wait_recv()
                cp(a, 6 + half, dg, c, half, sibling).start()

    def stage3():
        for a in range(na):
            for k, chip, half in ((4, xn, None), (5, yn, None), (6, dg, 0), (7, dg, 1)):
                cp(a, k, chip, 1 - c, half, me).wait_recv()
        for sent in sends():
            sent.wait_send()

    return [stage0, stage1, stage2, stage3]


def _gather_out_shapes(mine):
    return [jax.ShapeDtypeStruct((N_DEV,) + m.shape[1:], m.dtype) for m in mine]


def _exchange_scratch(per_array, na):
    return [pltpu.SemaphoreType.DMA((per_array * na,)), pltpu.SemaphoreType.DMA((per_array * na,))]


def _all_gather_halves(mine, name):
    na = len(mine)

    def body(*refs):
        for stage in _gather_stages(refs[:na], refs[na:2 * na], *refs[2 * na:]):
            stage()

    return _pallas_call(body, name=name, in_specs=[_HBM] * na, out_specs=[_HBM] * na, out_shape=_gather_out_shapes(mine),
                        scratch_shapes=_exchange_scratch(8, na))(*mine)


def _sibling_exchange(parts, name, other_half):
    na = len(parts)

    def body(*refs):
        ins, outs, (send_sems, recv_sems) = refs[:na], refs[na:2 * na], refs[2 * na:]
        x, y, c = _position()
        copies = [_remote(ins[a].at[:, 1 - c] if other_half else ins[a], outs[a], send_sems, recv_sems, a, (x, y, 1 - c))
                  for a in range(na)]
        for cp in copies:
            cp.start()
        for cp in copies:
            cp.wait()

    return _pallas_call(
        body, name=name, in_specs=[_HBM] * na, out_specs=[_HBM] * na,
        out_shape=[jax.ShapeDtypeStruct(p.shape[:1] + p.shape[2:] if other_half else p.shape, p.dtype) for p in parts],
        scratch_shapes=_exchange_scratch(1, na),
    )(*parts)


def _scatter_stages(ins, outs, send_sems, recv_sems):
    na = len(ins)
    x, y, c = _position()
    chips = [(1 - x, y), (x, 1 - y), (1 - x, 1 - y)]

    def copies():
        return [_remote(ins[a].at[2 * px + py], outs[a].at[j], send_sems, recv_sems, 3 * a + j, (px, py, c))
                for a in range(na) for j, (px, py) in enumerate(chips)]

    def start():
        for cp in copies():
            cp.start()

    def wait():
        for cp in copies():
            cp.wait()

    return [start, wait]


def _scatter_out_shapes(parts):
    return [jax.ShapeDtypeStruct((3,) + p.shape[1:], p.dtype) for p in parts]


def _scatter_to_chips(parts, name):
    na = len(parts)

    def body(*refs):
        for stage in _scatter_stages(refs[:na], refs[na:2 * na], *refs[2 * na:]):
            stage()

    return _pallas_call(body, name=name, in_specs=[_HBM] * na, out_specs=[_HBM] * na, out_shape=_scatter_out_shapes(parts),
                        scratch_shapes=_exchange_scratch(3, na))(*parts)


def _assemble(g4, mine, name, side_by_side, order=None):
    _, nl, r, w = g4.shape
    tr = _divisor_tile(r, 640)

    def body(in_ref, mine_ref, out_ref):
        chip = 2 * lax.axis_index("x") + lax.axis_index("y")
        blocks = [jnp.where(chip == sh, mine_ref[...], in_ref[sh]) for sh in range(N_SHARD)]
        if side_by_side:
            full = jnp.concatenate(blocks, axis=-1)
            out_ref[...] = full if order is None else _take_cols(full, order)
        else:
            for sh in range(N_SHARD):
                out_ref[sh] = blocks[sh]

    if side_by_side:
        out_spec = pl.BlockSpec((None, tr, N_SHARD * w), lambda l, i: (l, i, 0))
        out_shape = jax.ShapeDtypeStruct((nl, r, N_SHARD * w), g4.dtype)
    else:
        out_spec = pl.BlockSpec((None, N_SHARD, tr, w), lambda l, i: (l, 0, i, 0))
        out_shape = jax.ShapeDtypeStruct((nl, N_SHARD, r, w), g4.dtype)
    return _pallas_call(
        body, name=name, grid=(nl, r // tr),
        in_specs=[pl.BlockSpec((N_SHARD, None, tr, w), lambda l, i: (0, l, i, 0)),
                  pl.BlockSpec((None, tr, w), lambda l, i: (l, i, 0))],
        out_specs=out_spec, out_shape=out_shape,
        compiler_params=_params(dimension_semantics=("arbitrary", "arbitrary")),
    )(g4, mine)


def _all_reduce_small(vec, name):
    r, l = vec.shape

    def body(v_ref, out_ref, gath_ref, send_sems, recv_sems):
        x, y, c = _position()
        me = 4 * x + 2 * y + c
        gath_ref[me] = v_ref[...]
        copies = []
        for k in range(1, N_DEV):
            to = (x ^ (k >> 2), y ^ ((k >> 1) & 1), c ^ (k & 1))
            copies.append(_remote(gath_ref.at[me], gath_ref.at[me], send_sems, recv_sems, k - 1, to))
        for cp in copies:
            cp.start()
        for k in range(1, N_DEV):
            frm = 4 * (x ^ (k >> 2)) + 2 * (y ^ ((k >> 1) & 1)) + (c ^ (k & 1))
            _remote(gath_ref.at[frm], gath_ref.at[frm], send_sems, recv_sems, k - 1, (x, y, c)).wait_recv()
        for cp in copies:
            cp.wait_send()
        total = gath_ref[0]
        for d in range(1, N_DEV):
            total = total + gath_ref[d]
        out_ref[...] = total

    vm = pl.BlockSpec(memory_space=pltpu.VMEM)
    return _pallas_call(
        body, name=name, in_specs=[vm], out_specs=vm, out_shape=jax.ShapeDtypeStruct((r, l), F32),
        scratch_shapes=[pltpu.VMEM((N_DEV, r, l), F32), pltpu.SemaphoreType.DMA((N_DEV - 1,)),
                        pltpu.SemaphoreType.DMA((N_DEV - 1,))],
    )(vec)


def _sum_blocks(blocks, out_dtype, name):
    m, w = blocks[0].shape
    tr = _divisor_tile(m, 1024)

    def body(*refs):
        total = refs[0][...].astype(F32)
        for ref in refs[1:-1]:
            total = total + ref[...].astype(F32)
        refs[-1][...] = total.astype(out_dtype)

    blk = pl.BlockSpec((tr, w), lambda i: (i, 0))
    return _pallas_call(
        body, name=name, grid=(m // tr,), in_specs=[blk] * len(blocks), out_specs=blk,
        out_shape=jax.ShapeDtypeStruct((m, w), out_dtype),
        compiler_params=_params(dimension_semantics=("arbitrary",)),
    )(*blocks)


FIRST_GROUPS = (('ffn1_w_gate', 'ffn1_w_up', 'ffn1_w_down'),)
REST_GROUPS = (('ffn2_w_gate', 'ffn2_w_up', 'ffn2_w_down'), ('w_o',), ('w_in',), ('mla_w_q_b',), ('mla_w_kv_b',))
N_FIRST = len(FIRST_GROUPS)
GRAD_SETS = {'first_gu': (('ffn1_w_gate', 'ffn1_w_up'),), 'first_d': (('ffn1_w_down',),), 'rest': REST_GROUPS}
GRAD_SET_ORDER = ('first_gu', 'first_d', 'rest')


def _shard_rows(name):
    shape, axis = BIG[name]
    return shape[0] // N_SHARD if axis == 0 else shape[0]


def _group_row_offsets(group):
    return [int(v) for v in np.cumsum([0] + [_shard_rows(n) for n in group])]


def _rope_tables(s):
    pos = jnp.arange(s, dtype=F32)
    inv = 1.0 / (ROPE_THETA ** (jnp.arange(0, MLA_ROPE, 2, dtype=F32) / MLA_ROPE))
    ang = pos[:, None] * inv[None, :]
    return jnp.tile(jnp.cos(ang), (1, SWA_HEADS)), jnp.tile(jnp.sin(ang), (1, SWA_HEADS))


_MIXER_GAINS = ('mix_norm', 'mla_q_a_norm', 'mla_kv_a_norm', 'mla_q_norm', 'mla_k_norm', 'swa_q_norm', 'swa_k_norm')


def _local_step(x, target, small, ex):
    s = x.shape[0]
    cos, sin = _rope_tables(s)
    row = lambda name, l: small[name][l][None, :]
    saved, bigs = [], []
    for l in range(DEPTH):
        big = ex.first_weights(l)
        sv = {'x0': x}
        x, sv['g1'], sv['u1'], got = _ffn_fwd(x, row('ffn1_norm', l), big['ffn1_w_gate'], big['ffn1_w_up'], big['ffn1_w_down'],
                                             f"ffn1_fwd_{l}", ex.gather_behind_ffn1(l))
        big.update(ex.rest_weights(l, got))
        bigs.append(big)
        sv['x1'] = x
        gains = [row(n, l) for n in _MIXER_GAINS]
        mixer_w = (big['w_in'], big['mla_w_q_b'], big['mla_w_kv_b'])
        q_a, k_a, v_a, q_b, k_b, v_b = _pre_fwd(x, gains, *mixer_w, cos, sin, f"pre_fwd_{l}")
        o_a, lse, got = _mla_fwd(q_a, k_a, v_a, f"mla_fwd_{l}", ex.gather_behind_mla(l))
        ex.gathered_behind_mla(l, got)
        kpad = jnp.pad(k_b, ((0, 0), (BLOCK, 0), (0, 0)))
        vpad = jnp.pad(v_b, ((0, 0), (BLOCK, 0), (0, 0)))
        sinks = jnp.broadcast_to(small['swa_sinks'][l].reshape(SWA_KV_HEADS, SWA_GROUP, 1), (SWA_KV_HEADS, SWA_GROUP, 128))
        o_b, lse_b = _swa_fwd(q_b, kpad, vpad, sinks, f"swa_fwd_{l}")
        sv.update(gains=gains, mixer_w=mixer_w, q_a=q_a, k_a=k_a, v_a=v_a, q_b=q_b, kpad=kpad, vpad=vpad, sinks=sinks,
                  o_a=o_a, lse=lse, o_b=o_b, lse_b=lse_b)
        x = _post_fwd(x, o_a, o_b, row('mla_out_norm', l), row('swa_out_norm', l), big['w_o'], f"post_fwd_{l}")
        sv['x2'] = x
        x, sv['g2'], sv['u2'], _ = _ffn_fwd(x, row('ffn2_norm', l), big['ffn2_w_gate'], big['ffn2_w_up'], big['ffn2_w_down'],
                                           f"ffn2_fwd_{l}")
        saved.append(sv)

    dx, loss = _loss_head(x, target, "loss_head")

    gs = {n: [None] * DEPTH for n in SMALL_NAMES}
    t = _attn_tile(s)
    for l in reversed(range(DEPTH)):
        sv, big = saved[l], bigs[l]

        def ffn_back(tag, xin, dy, gate, up, scatter=None):
            (dxi, dgain, nb, act, dgate, dup), got = _ffn_bwd(xin, dy, gate, up, row(tag + '_norm', l), big[tag + '_w_gate'],
                                                             big[tag + '_w_up'], big[tag + '_w_down'], f"{tag}_bwd_{l}", scatter)
            gs[tag + '_norm'][l] = dgain[0]
            names = (tag + '_w_gate', tag + '_w_up', tag + '_w_down')
            if tag == 'ffn2':
                group, _ = _matmul_tn(dgate, nb, 1.0, f"{tag}_dwg_{l}", None, 0, 3)
                group, _ = _matmul_tn(dup, nb, 1.0, f"{tag}_dwu_{l}", group, 1, 3)
                group, _ = _matmul_tn(act, dy, 0.5, f"{tag}_dwd_{l}", group, 2, 3)
                return dxi, {names: group}, got
            group, _ = _matmul_tn(dgate, nb, 1.0, f"{tag}_dwg_{l}", None, 0, 2)
            group, _ = _matmul_tn(dup, nb, 1.0, f"{tag}_dwu_{l}", group, 1, 2)
            ex.grads_ready(l, 'first_gu', {names[:2]: group})
            down, arrived = _matmul_tn(act, dy, 0.5, f"{tag}_dwd_{l}", None, 0, 1, ex.scatter_behind_dwd(l))
            ex.scattered_behind_dwd(l, arrived)
            ex.grads_ready(l, 'first_d', {names[2:]: down})
            return dxi, None, got

        dx, rest_grads, _ = ffn_back('ffn2', sv['x2'], dx, sv['g2'], sv['u2'])
        do_a, do_b, dga, dgb, dwo, delta = _post_bwd(dx, sv['o_a'], sv['o_b'], row('mla_out_norm', l), row('swa_out_norm', l),
                                              big['w_o'], f"post_bwd_{l}")
        gs['mla_out_norm'][l], gs['swa_out_norm'][l] = dga[0], dgb[0]
        rest_grads['w_o'] = dwo.reshape(N_SHARD, MIX_WIDTH // N_SHARD, D_MODEL)
        dq_a, dk_a, dv_a, got = _mla_bwd(sv['q_a'], sv['k_a'], sv['v_a'], do_a, sv['lse'], delta, f"mla_bwd_{l}",
                                         ex.scatter_behind_mla(l))
        ex.scattered_behind_mla(l, got)
        dq_b, dkpad, dvpad, dsink = _swa_bwd(sv['q_b'], sv['kpad'], sv['vpad'], sv['sinks'], sv['o_b'], sv['lse_b'], do_b,
                                             f"swa_bwd_{l}")
        gs['swa_sinks'][l] = dsink[:, :SWA_GROUP, 0].reshape(SWA_HEADS)
        cts = [dq_a, dk_a, dv_a, dq_b, dkpad[:, BLOCK:], dvpad[:, BLOCK:]]
        outs = _pre_bwd(sv['x1'], dx, cts, sv['gains'], *sv['mixer_w'], cos, sin, f"pre_bwd_{l}")
        dx = outs[0]
        for n, val in zip(_MIXER_GAINS, outs[1:8]):
            gs[n][l] = val[0]
        rest_grads['w_in'], rest_grads['mla_w_q_b'], rest_grads['mla_w_kv_b'] = outs[8:11]
        ex.grads_ready(l, 'rest', rest_grads)
        scatter = ex.scatter_behind_ffn1(l)
        dx, _, got = ffn_back('ffn1', sv['x0'], dx, sv['g1'], sv['u1'], scatter)
        ex.scattered_behind_ffn1(l, got)
    return loss, dx, gs


class _Exchange:
    def __init__(self, weights, c, chip):
        halves_of = lambda a: a.reshape(a.shape[:-2] + (2, a.shape[-2] // 2, a.shape[-1]))
        self.halves_of, self.c, self.chip = halves_of, c, chip
        self.mine = [[halves_of(jnp.concatenate([weights[n][l].astype(BF16) for n in group], axis=0))
                      for group in FIRST_GROUPS + REST_GROUPS] for l in range(DEPTH)]
        self.ahead, self.waiting, self.begun, self.received = {}, {}, {}, {}

    def _assembled(self, l, which, gathered):
        groups, base = (FIRST_GROUPS, 0) if which == 'first' else (REST_GROUPS, N_FIRST)
        big = {}
        for gi, group in enumerate(groups):
            offs = _group_row_offsets(group)
            _, rh, w = self.mine[l][base + gi].shape
            col_sharded = BIG[group[0]][1] == 1
            full = _assemble(gathered[gi].reshape(N_SHARD, 1, 2 * rh, w), self.mine[l][base + gi].reshape(1, 2 * rh, w),
                             f"assemble_{which}{gi}_{l}", col_sharded, MIXER_ORDERS.get(group[0]))
            for i, n in enumerate(group):
                rows = offs[i + 1] - offs[i]
                if len(group) > 1 and offs[i] % rows == 0:
                    big[n] = ((full, (None, rows, N_SHARD * w), (0, offs[i] // rows, 0)) if col_sharded else
                              (full, (None, N_SHARD, rows, w), (0, 0, offs[i] // rows, 0)))
                elif col_sharded:
                    big[n] = full[0, offs[i]:offs[i + 1]]
                else:
                    big[n] = full[0, :, offs[i]:offs[i + 1]].reshape(BIG[n][0])
        return big

    def first_weights(self, l):
        got = self.ahead[l][:N_FIRST] if l in self.ahead else _all_gather_halves(self.mine[l][:N_FIRST], f"gather_first_{l}")
        return self._assembled(l, 'first', got)

    def gather_behind_ffn1(self, l):
        return None if l in self.ahead else self.mine[l][N_FIRST:]

    def rest_weights(self, l, got):
        return self._assembled(l, 'rest', self.ahead[l][N_FIRST:] if l in self.ahead else got)

    def gather_behind_mla(self, l):
        return self.mine[l + 1] if l + 1 < DEPTH else None

    def gathered_behind_mla(self, l, got):
        if got:
            self.ahead[l + 1] = got

    def grads_ready(self, l, which, grads):
        groups = GRAD_SETS[which]
        parts = [self.halves_of(grads[group] if group in grads else jnp.concatenate([grads[n] for n in group], axis=1))
                 for group in groups]
        if l > 0:
            self.waiting[(l, which)] = parts
        else:
            self._begin(l, [(which, parts)])

    def _begin(self, l, sets):
        tag = sets[0][0] if len(sets) == 1 else "all"
        from_sibling = _sibling_exchange(sum((parts for _, parts in sets), []), f"swap_{tag}_{l}", True)
        at = 0
        for which, parts in sets:
            chip_sums = []
            for gi, p in enumerate(parts):
                got = from_sibling[at]
                at += 1
                kept = lax.dynamic_index_in_dim(p, self.c, axis=1, keepdims=False)
                rows = N_SHARD * p.shape[2]
                pair = _sum_blocks([kept.reshape(rows, -1), got.reshape(rows, -1)], BF16, f"sum_pair_{which}{gi}_{l}")
                chip_sums.append(pair.reshape(got.shape))
            self.begun[(l, which)] = chip_sums

    def scatter_behind_mla(self, l):
        if l + 1 >= DEPTH:
            return None
        self._begin(l + 1, [(which, self.waiting.pop((l + 1, which))) for which in GRAD_SET_ORDER])
        return sum((self.begun[(l + 1, which)] for which in GRAD_SET_ORDER), [])

    def scattered_behind_mla(self, l, got):
        at = 0
        for which in GRAD_SET_ORDER if got else ():
            self.received[(l + 1, which)] = got[at:at + len(GRAD_SETS[which])]
            at += len(GRAD_SETS[which])

    def scatter_behind_dwd(self, l):
        return self.begun[(l, 'first_gu')] if l == 0 else None

    def scattered_behind_dwd(self, l, got):
        if got:
            self.received[(l, 'first_gu')] = got

    def scatter_behind_ffn1(self, l):
        return self.begun[(l, 'rest')] if l == 0 else None

    def scattered_behind_ffn1(self, l, got):
        if got:
            self.received[(l, 'rest')] = got

    def reduced(self):
        keys = sorted(self.begun)
        for key in keys:
            if key not in self.received:
                self.received[key] = _scatter_to_chips(self.begun[key], f"scatter_{key[1]}_{key[0]}")
        halves = []
        for l, which in keys:
            for gi, (cs, got) in enumerate(zip(self.begun[(l, which)], self.received[(l, which)])):
                own = lax.dynamic_index_in_dim(cs, self.chip, axis=0, keepdims=False)
                halves.append(_sum_blocks([own, got[0], got[1], got[2]], F32, f"sum_chips_{which}{gi}_{l}"))
        others = _sibling_exchange(halves, "share_halves", False)
        per_layer, at = {}, 0
        for l, which in keys:
            for group in GRAD_SETS[which]:
                mine_h, other_h = halves[at], others[at]
                at += 1
                full = jnp.where(self.c == 0, jnp.concatenate([mine_h, other_h]), jnp.concatenate([other_h, mine_h]))
                offs = _group_row_offsets(group)
                for i, n in enumerate(group):
                    per_layer[(n, l)] = full[offs[i]:offs[i + 1]]
        return {n: jnp.stack([per_layer[(n, l)] for l in range(DEPTH)]) for n in BIG_NAMES}


def kernel(x, ffn1_norm, ffn1_w_gate, ffn1_w_up, ffn1_w_down, mix_norm, w_in, mla_q_a_norm, mla_w_q_b, mla_kv_a_norm, mla_w_kv_b, mla_q_norm, mla_k_norm, swa_q_norm, swa_k_norm, swa_sinks, mla_out_norm, swa_out_norm, w_o, ffn2_norm, ffn2_w_gate, ffn2_w_up, ffn2_w_down, loss_target, m_ffn1_norm, m_ffn1_w_gate, m_ffn1_w_up, m_ffn1_w_down, m_mix_norm, m_w_in, m_mla_q_a_norm, m_mla_w_q_b, m_mla_kv_a_norm, m_mla_w_kv_b, m_mla_q_norm, m_mla_k_norm, m_swa_q_norm, m_swa_k_norm, m_swa_sinks, m_mla_out_norm, m_swa_out_norm, m_w_o, m_ffn2_norm, m_ffn2_w_gate, m_ffn2_w_up, m_ffn2_w_down, v_ffn1_norm, v_ffn1_w_gate, v_ffn1_w_up, v_ffn1_w_down, v_mix_norm, v_w_in, v_mla_q_a_norm, v_mla_w_q_b, v_mla_kv_a_norm, v_mla_w_kv_b, v_mla_q_norm, v_mla_k_norm, v_swa_q_norm, v_swa_k_norm, v_swa_sinks, v_mla_out_norm, v_swa_out_norm, v_w_o, v_ffn2_norm, v_ffn2_w_gate, v_ffn2_w_up, v_ffn2_w_down):
    args = dict(locals())
    transposed = lambda a: jnp.swapaxes(a, 1, 2)
    as_kernels_see = lambda n, a: transposed(a) if n in TRANSPOSED else a
    weights = {n: as_kernels_see(n, args[n]) for n in WEIGHT_NAMES}
    mom_m = {n: as_kernels_see(n, args["m_" + n]) for n in WEIGHT_NAMES}
    mom_v = {n: as_kernels_see(n, args["v_" + n]) for n in WEIGHT_NAMES}
    ex = _Exchange(weights, lax.axis_index("c"), 2 * lax.axis_index("x") + lax.axis_index("y"))
    loss, dx, gs = _local_step(x[0], loss_target[0], {n: weights[n] for n in SMALL_NAMES}, ex)

    small_flat = jnp.concatenate([jnp.stack(gs[n]).reshape(-1) for n in SMALL_NAMES] + [loss[0, :1]])
    n_small = small_flat.shape[0]
    lanes = -(-n_small // (8 * 128)) * 128
    small_sum = _all_reduce_small(jnp.pad(small_flat, (0, 8 * lanes - n_small)).reshape(8, lanes), "reduce_small").reshape(-1)
    loss_out = small_sum[n_small - 1]
    grads = ex.reduced()

    packed = lambda d: jnp.pad(jnp.concatenate([d[n].reshape(-1) for n in SMALL_NAMES]), (0, 8 * lanes - n_small + 1)).reshape(8, lanes)
    small_out = _adamw(packed(weights), small_sum.reshape(8, lanes), packed(mom_m), packed(mom_v), "adamw_small")
    deltas, new_m, new_v = {}, {}, {}
    off = 0
    for n in SMALL_NAMES:
        cnt = int(np.prod(weights[n].shape))
        grads[n], deltas[n], new_m[n], new_v[n] = (a.reshape(-1)[off:off + cnt].reshape(weights[n].shape)
                                                    for a in (small_sum, *small_out))
        off += cnt
    for n in BIG_NAMES:
        shp = weights[n].shape
        two_d = (shp[0] * shp[1], shp[2])
        d, nm, nv = _adamw(weights[n].reshape(two_d), grads[n].reshape(two_d), mom_m[n].reshape(two_d),
                           mom_v[n].reshape(two_d), f"adamw_{n}")
        deltas[n], new_m[n], new_v[n] = d.reshape(shp), nm.reshape(shp), nv.reshape(shp)
    for n in TRANSPOSED:
        grads[n], deltas[n], new_m[n], new_v[n] = (transposed(a) for a in (grads[n], deltas[n], new_m[n], new_v[n]))

    return (loss_out, dx[None], *[grads[n] for n in WEIGHT_NAMES], *[deltas[n] for n in WEIGHT_NAMES],
            *[new_m[n] for n in WEIGHT_NAMES], *[new_v[n] for n in WEIGHT_NAMES])
```
